```python
import jax, jax.numpy as jnp
from jax import lax
import numpy as np

D_MODEL = 1024
BATCH = 32
SEQ = 2048
DEPTH = 1

CTX_LEN = 256
GRID_W = 64

A_HEADS = 4
A_DK = 128
A_DV = 128
A_KW = A_HEADS * A_DK
A_VW = A_HEADS * A_DV
A_CHUNK = 64

B_GROUPS = 4
B_GW = 128
B_W = B_GROUPS * B_GW
B_CHUNK = 2 * GRID_W

D_FF = -(-8 * D_MODEL // (3 * 256)) * 256

N_MOD = 6
EPS = 1e-6

CTX_STATE_COLS = 2 * A_KW + A_VW
SPLIT_IDX = (A_KW, 2 * A_KW, 2 * A_KW + A_VW, 3 * A_KW + A_VW, 3 * A_KW + 2 * A_VW,
             3 * A_KW + 2 * A_VW + B_W, 3 * A_KW + 2 * A_VW + 2 * B_W,
             3 * A_KW + 2 * A_VW + 2 * B_W + D_MODEL)
IN_COLS = 3 * A_KW + 2 * A_VW + 2 * B_W + 2 * D_MODEL

kernel_name = 'hybrid_hgrn2_chunkmlp_dit_block'


def rmsnorm(x, g):
    xf = x.astype(jnp.float32)
    y = xf * lax.rsqrt(jnp.mean(xf * xf, axis=-1, keepdims=True) + EPS)
    return (y * g.astype(jnp.float32)).astype(x.dtype)


def layernorm(x, g, b):
    xf = x.astype(jnp.float32)
    mu = jnp.mean(xf, axis=-1, keepdims=True)
    var = jnp.mean(jnp.square(xf - mu), axis=-1, keepdims=True)
    y = (xf - mu) * lax.rsqrt(var + EPS) * g.astype(jnp.float32) + b.astype(jnp.float32)
    return y.astype(x.dtype)


def modulate(h, shift, scale):
    return h * (1 + scale) + shift


def split_heads(t):
    return t.reshape(t.shape[0], t.shape[1], A_HEADS, -1).astype(jnp.float32)


def flip_seq(t):
    return jnp.flip(t, axis=1)


def forget_gate(f_logit, lb):
    f = lb + (1 - lb) * jax.nn.sigmoid(f_logit.astype(jnp.float32))
    return jnp.log(f), 1 - f


def hgrn2_chunked(q, k, logf, v, s0):
    b_, L = q.shape[:2]
    n = L // A_CHUNK
    rs = lambda t: t.reshape(b_, n, A_CHUNK, *t.shape[2:])
    q, k, logf, v = rs(q), rs(k), rs(logf), rs(v)
    bcum = jnp.cumsum(logf, axis=2)
    blast = bcum[:, :, -1:]
    mid = 0.5 * blast
    q_in = q * jnp.exp(bcum - mid)
    k_in = k * jnp.exp(mid - bcum)
    scores = jnp.einsum('bnthd,bnshd->bnhts', q_in, k_in)
    mask = jnp.tril(jnp.ones((A_CHUNK, A_CHUNK), dtype=bool))
    scores = jnp.where(mask, scores, 0.0)
    o_intra = jnp.einsum('bnhts,bnshe->bnthe', scores, v)
    d_state = jnp.einsum('bnshd,bnshe->bnhde', k * jnp.exp(blast - bcum), v)
    decay = jnp.exp(blast[:, :, 0])

    def step(s, inp):
        a, d = inp
        return a[..., None] * s + d, s

    s_final, s_in = lax.scan(step, s0, (jnp.moveaxis(decay, 1, 0), jnp.moveaxis(d_state, 1, 0)))
    s_in = jnp.moveaxis(s_in, 0, 1)
    o_inter = jnp.einsum('bnthd,bnhde->bnthe', q * jnp.exp(bcum), s_in)
    o = (o_intra + o_inter).reshape(b_, L, A_HEADS, A_DV)
    return o, s_final


def hgrn2_final_state(k, logf, v):
    bcum = jnp.cumsum(logf, axis=1)
    w = k * jnp.exp(bcum[:, -1:] - bcum)
    return jnp.einsum('blhd,blhe->bhde', w, v)


def chunk_sgu(u, v, ln_g, ln_b, w_s, b_s, n_chunks):
    b_, L = u.shape[:2]
    v = layernorm(v, ln_g, ln_b)
    vc = v.reshape(b_, n_chunks, B_CHUNK, B_GROUPS, B_GW)
    mixed = jnp.einsum('gts,bnsgc->bntgc', w_s, vc) + jnp.transpose(b_s)[:, :, None]
    return u * mixed.reshape(b_, L, B_W)


def token_mixers(p, lb, g_norm_a, ln_v_g, ln_v_b, w_s, b_s, w_pa, w_pb, w_o, s0_f, s0_b, n_chunks_b):
    f_f, f_b, i_in, q, og, u, v, ga, gb = jnp.split(p, SPLIT_IDX, axis=-1)
    b_, L = p.shape[:2]
    logf_f, k_f = forget_gate(f_f, lb[0])
    logf_b, k_b = forget_gate(f_b, lb[1])
    qh, vh = split_heads(q), split_heads(i_in)
    o_f, s_f = hgrn2_chunked(qh, split_heads(k_f), split_heads(logf_f), vh, s0_f)
    o_b, s_b = hgrn2_chunked(flip_seq(qh), flip_seq(split_heads(k_b)), flip_seq(split_heads(logf_b)),
                             flip_seq(vh), s0_b)
    o_a = rmsnorm(o_f + flip_seq(o_b), g_norm_a).reshape(b_, L, A_VW).astype(p.dtype)
    o_a = o_a * jax.nn.silu(og)
    o_b_mlp = chunk_sgu(jax.nn.gelu(u), jax.nn.gelu(v), ln_v_g, ln_v_b, w_s, b_s, n_chunks_b)
    merged = jax.nn.sigmoid(ga) * (o_a @ w_pa) + jax.nn.sigmoid(gb) * (o_b_mlp @ w_pb)
    return merged @ w_o, s_f, s_b


def swiglu(h, w_up, w_down):
    a, b = jnp.split(h @ w_up, 2, axis=-1)
    return (jax.nn.silu(a) * b) @ w_down


def _fwd_setup_inputs(seed: int = 0) -> dict:
    key = jax.random.key(seed)
    ks = jax.random.split(key, 21)
    nrm = lambda k, shape, s: jax.random.normal(k, shape, jnp.float32) * s
    return {
        'x': nrm(ks[0], (BATCH, SEQ, D_MODEL), 1.0),
        'c': nrm(ks[1], (BATCH, D_MODEL), 1.0),
        'ctx': nrm(ks[2], (BATCH, CTX_LEN, D_MODEL), 1.0),
        'c_ctx': nrm(ks[3], (D_MODEL,), 1.0),
        'w_mod': nrm(ks[4], (DEPTH, D_MODEL, N_MOD * D_MODEL), 0.5 * D_MODEL ** -0.5),
        'b_mod': nrm(ks[5], (DEPTH, N_MOD * D_MODEL), 0.01),
        'g_mix': 1.0 + nrm(ks[6], (DEPTH, D_MODEL), 0.1),
        'g_ffn': 1.0 + nrm(ks[7], (DEPTH, D_MODEL), 0.1),
        'w_in': nrm(ks[8], (DEPTH, D_MODEL, IN_COLS), D_MODEL ** -0.5),
        'lb_gamma': nrm(ks[9], (DEPTH + 1, 2, A_KW), 0.5),
        'g_norm_a': 1.0 + nrm(ks[10], (DEPTH, A_DV), 0.1),
        'ln_v_g': 1.0 + nrm(ks[11], (DEPTH, B_W), 0.1),
        'ln_v_b': nrm(ks[12], (DEPTH, B_W), 0.02),
        'w_s': nrm(ks[13], (DEPTH, B_GROUPS, B_CHUNK, B_CHUNK), 0.5 * B_CHUNK ** -0.5),
        'b_s': 1.0 + nrm(ks[14], (DEPTH, B_GROUPS, B_CHUNK), 0.1),
        'w_pa': nrm(ks[15], (DEPTH, A_VW, D_MODEL), A_VW ** -0.5),
        'w_pb': nrm(ks[16], (DEPTH, B_W, D_MODEL), B_W ** -0.5),
        'w_o': nrm(ks[17], (DEPTH, D_MODEL, D_MODEL), D_MODEL ** -0.5),
        'w_up': nrm(ks[18], (DEPTH, D_MODEL, 2 * D_FF), D_MODEL ** -0.5),
        'w_down': nrm(ks[19], (DEPTH, D_FF, D_MODEL), D_FF ** -0.5),
        'g_final': 1.0 + nrm(ks[20], (D_MODEL,), 0.1),
    }


def _fwd_reference(x, c, ctx, c_ctx, w_mod, b_mod, g_mix, g_ffn, w_in, lb_gamma, g_norm_a,
              ln_v_g, ln_v_b, w_s, b_s, w_pa, w_pb, w_o, w_up, w_down, g_final):
    bsz, L = x.shape[0], x.shape[1]
    rows = L // GRID_W
    n_chunks_lat = rows // 2
    n_chunks_ctx = ctx.shape[1] // B_CHUNK
    lb_all = jnp.cumsum(jax.nn.softmax(lb_gamma.astype(jnp.float32), axis=0), axis=0)
    for l in range(DEPTH):
        last = l == DEPTH - 1
        lb = lb_all[l]
        mod = (jax.nn.silu(c) @ w_mod[l] + b_mod[l]).reshape(bsz, N_MOD, D_MODEL)
        mc = (jax.nn.silu(c_ctx) @ w_mod[l] + b_mod[l]).reshape(N_MOD, D_MODEL)
        hc = modulate(rmsnorm(ctx, g_mix[l]), mc[0], mc[1])
        if last:
            pc = hc @ w_in[l][:, :CTX_STATE_COLS]
            f_f, f_b, i_c = jnp.split(pc, (A_KW, 2 * A_KW), axis=-1)
            logf_f, k_f = forget_gate(f_f, lb[0])
            logf_b, k_b = forget_gate(f_b, lb[1])
            vh = split_heads(i_c)
            s_ctx_f = hgrn2_final_state(split_heads(k_f), split_heads(logf_f), vh)
            s_ctx_b = hgrn2_final_state(flip_seq(split_heads(k_b)), flip_seq(split_heads(logf_b)), flip_seq(vh))
        else:
            zeros = jnp.zeros((bsz, A_HEADS, A_DK, A_DV), jnp.float32)
            mix_c, s_ctx_f, s_ctx_b = token_mixers(hc @ w_in[l], lb, g_norm_a[l], ln_v_g[l], ln_v_b[l],
                                                   w_s[l], b_s[l], w_pa[l], w_pb[l], w_o[l],
                                                   zeros, zeros, n_chunks_ctx)
            ctx = ctx + mc[2] * mix_c
            hc2 = modulate(rmsnorm(ctx, g_ffn[l]), mc[3], mc[4])
            ctx = ctx + mc[5] * swiglu(hc2, w_up[l], w_down[l])
        h = modulate(rmsnorm(x, g_mix[l]), mod[:, 0, None], mod[:, 1, None])
        mix_x, _, _ = token_mixers(h @ w_in[l], lb, g_norm_a[l], ln_v_g[l], ln_v_b[l],
                                   w_s[l], b_s[l], w_pa[l], w_pb[l], w_o[l],
                                   s_ctx_f, s_ctx_b, n_chunks_lat)
        x = x + mod[:, 2, None] * mix_x
        h2 = modulate(rmsnorm(x, g_ffn[l]), mod[:, 3, None], mod[:, 4, None])
        x = x + mod[:, 5, None] * swiglu(h2, w_up[l], w_down[l])
    return rmsnorm(x, g_final)


import jax as _jax
import jax.numpy as _jnp

TWIN_FORMAT = 'train_step'
FWD_PARAMS = ['x', 'c', 'ctx', 'c_ctx', 'w_mod', 'b_mod', 'g_mix', 'g_ffn', 'w_in', 'lb_gamma', 'g_norm_a', 'ln_v_g', 'ln_v_b', 'w_s', 'b_s', 'w_pa', 'w_pb', 'w_o', 'w_up', 'w_down', 'g_final']
TWIN_WEIGHTS = ['c_ctx', 'w_mod', 'b_mod', 'g_mix', 'g_ffn', 'w_in', 'lb_gamma', 'g_norm_a', 'ln_v_g', 'ln_v_b', 'w_s', 'b_s', 'w_pa', 'w_pb', 'w_o', 'w_up', 'w_down', 'g_final']
TWIN_DIFF_INPUT = 'x'
TWIN_INPUTS = ['x', 'c', 'ctx', 'c_ctx', 'w_mod', 'b_mod', 'g_mix', 'g_ffn', 'w_in', 'lb_gamma', 'g_norm_a', 'ln_v_g', 'ln_v_b', 'w_s', 'b_s', 'w_pa', 'w_pb', 'w_o', 'w_up', 'w_down', 'g_final', 'loss_target', 'm_c_ctx', 'm_w_mod', 'm_b_mod', 'm_g_mix', 'm_g_ffn', 'm_w_in', 'm_lb_gamma', 'm_g_norm_a', 'm_ln_v_g', 'm_ln_v_b', 'm_w_s', 'm_b_s', 'm_w_pa', 'm_w_pb', 'm_w_o', 'm_w_up', 'm_w_down', 'm_g_final', 'v_c_ctx', 'v_w_mod', 'v_b_mod', 'v_g_mix', 'v_g_ffn', 'v_w_in', 'v_lb_gamma', 'v_g_norm_a', 'v_ln_v_g', 'v_ln_v_b', 'v_w_s', 'v_b_s', 'v_w_pa', 'v_w_pb', 'v_w_o', 'v_w_up', 'v_w_down', 'v_g_final']
TWIN_OUTPUTS = ['loss', 'grad_x', 'grad_c_ctx', 'grad_w_mod', 'grad_b_mod', 'grad_g_mix', 'grad_g_ffn', 'grad_w_in', 'grad_lb_gamma', 'grad_g_norm_a', 'grad_ln_v_g', 'grad_ln_v_b', 'grad_w_s', 'grad_b_s', 'grad_w_pa', 'grad_w_pb', 'grad_w_o', 'grad_w_up', 'grad_w_down', 'grad_g_final', 'delta_c_ctx', 'delta_w_mod', 'delta_b_mod', 'delta_g_mix', 'delta_g_ffn', 'delta_w_in', 'delta_lb_gamma', 'delta_g_norm_a', 'delta_ln_v_g', 'delta_ln_v_b', 'delta_w_s', 'delta_b_s', 'delta_w_pa', 'delta_w_pb', 'delta_w_o', 'delta_w_up', 'delta_w_down', 'delta_g_final', 'new_m_c_ctx', 'new_m_w_mod', 'new_m_b_mod', 'new_m_g_mix', 'new_m_g_ffn', 'new_m_w_in', 'new_m_lb_gamma', 'new_m_g_norm_a', 'new_m_ln_v_g', 'new_m_ln_v_b', 'new_m_w_s', 'new_m_b_s', 'new_m_w_pa', 'new_m_w_pb', 'new_m_w_o', 'new_m_w_up', 'new_m_w_down', 'new_m_g_final', 'new_v_c_ctx', 'new_v_w_mod', 'new_v_b_mod', 'new_v_g_mix', 'new_v_g_ffn', 'new_v_w_in', 'new_v_lb_gamma', 'new_v_g_norm_a', 'new_v_ln_v_g', 'new_v_ln_v_b', 'new_v_w_s', 'new_v_b_s', 'new_v_w_pa', 'new_v_w_pb', 'new_v_w_o', 'new_v_w_up', 'new_v_w_down', 'new_v_g_final']
TWIN_LEAF_KINDS = {'loss': 'loss', 'grad_x': 'grad_x', 'grad_c_ctx': 'grad_w', 'grad_w_mod': 'grad_w', 'grad_b_mod': 'grad_w', 'grad_g_mix': 'grad_w', 'grad_g_ffn': 'grad_w', 'grad_w_in': 'grad_w', 'grad_lb_gamma': 'grad_w', 'grad_g_norm_a': 'grad_w', 'grad_ln_v_g': 'grad_w', 'grad_ln_v_b': 'grad_w', 'grad_w_s': 'grad_w', 'grad_b_s': 'grad_w', 'grad_w_pa': 'grad_w', 'grad_w_pb': 'grad_w', 'grad_w_o': 'grad_w', 'grad_w_up': 'grad_w', 'grad_w_down': 'grad_w', 'grad_g_final': 'grad_w', 'delta_c_ctx': 'delta_w', 'delta_w_mod': 'delta_w', 'delta_b_mod': 'delta_w', 'delta_g_mix': 'delta_w', 'delta_g_ffn': 'delta_w', 'delta_w_in': 'delta_w', 'delta_lb_gamma': 'delta_w', 'delta_g_norm_a': 'delta_w', 'delta_ln_v_g': 'delta_w', 'delta_ln_v_b': 'delta_w', 'delta_w_s': 'delta_w', 'delta_b_s': 'delta_w', 'delta_w_pa': 'delta_w', 'delta_w_pb': 'delta_w', 'delta_w_o': 'delta_w', 'delta_w_up': 'delta_w', 'delta_w_down': 'delta_w', 'delta_g_final': 'delta_w', 'new_m_c_ctx': 'new_m', 'new_m_w_mod': 'new_m', 'new_m_b_mod': 'new_m', 'new_m_g_mix': 'new_m', 'new_m_g_ffn': 'new_m', 'new_m_w_in': 'new_m', 'new_m_lb_gamma': 'new_m', 'new_m_g_norm_a': 'new_m', 'new_m_ln_v_g': 'new_m', 'new_m_ln_v_b': 'new_m', 'new_m_w_s': 'new_m', 'new_m_b_s': 'new_m', 'new_m_w_pa': 'new_m', 'new_m_w_pb': 'new_m', 'new_m_w_o': 'new_m', 'new_m_w_up': 'new_m', 'new_m_w_down': 'new_m', 'new_m_g_final': 'new_m', 'new_v_c_ctx': 'new_v', 'new_v_w_mod': 'new_v', 'new_v_b_mod': 'new_v', 'new_v_g_mix': 'new_v', 'new_v_g_ffn': 'new_v', 'new_v_w_in': 'new_v', 'new_v_lb_gamma': 'new_v', 'new_v_g_norm_a': 'new_v', 'new_v_ln_v_g': 'new_v', 'new_v_ln_v_b': 'new_v', 'new_v_w_s': 'new_v', 'new_v_b_s': 'new_v', 'new_v_w_pa': 'new_v', 'new_v_w_pb': 'new_v', 'new_v_w_o': 'new_v', 'new_v_w_up': 'new_v', 'new_v_w_down': 'new_v', 'new_v_g_final': 'new_v'}


def _forward(args):
    return _fwd_reference(*[args[k] for k in FWD_PARAMS])


def _output_shape():
    out = _jax.eval_shape(lambda: _forward(_fwd_setup_inputs(0)))
    return out.shape, out.dtype

N_MICROBATCH = 1
ADAM_LR = 0.001
ADAM_B1 = 0.9
ADAM_B2 = 0.999
ADAM_EPS = 1e-08
ADAM_WD = 0.01
ADAM_STEP = 10
PER_EXAMPLE_BATCH_AXIS = {'x': 0, 'c': 0, 'ctx': 0, 'loss_target': 0}
SHARED_INPUTS = []
_WEIGHT_DTYPES = {'c_ctx': _jnp.float32, 'w_mod': _jnp.float32, 'b_mod': _jnp.float32, 'g_mix': _jnp.float32, 'g_ffn': _jnp.float32, 'w_in': _jnp.float32, 'lb_gamma': _jnp.float32, 'g_norm_a': _jnp.float32, 'ln_v_g': _jnp.float32, 'ln_v_b': _jnp.float32, 'w_s': _jnp.float32, 'b_s': _jnp.float32, 'w_pa': _jnp.float32, 'w_pb': _jnp.float32, 'w_o': _jnp.float32, 'w_up': _jnp.float32, 'w_down': _jnp.float32, 'g_final': _jnp.float32}
MOMENT_SCALE = {'c_ctx': 1.576776e-03, 'w_mod': 1.575424e-01, 'b_mod': 2.956907e-01, 'g_mix': 8.024449e-02, 'g_ffn': 8.076714e-02, 'w_in': 3.521791e-02, 'lb_gamma': 1.315982e-02, 'g_norm_a': 8.078717e-02, 'ln_v_g': 2.223757e-02, 'ln_v_b': 2.186464e-02, 'w_s': 4.508406e-02, 'b_s': 4.530497e-02, 'w_pa': 2.854292e-02, 'w_pb': 3.773067e-02, 'w_o': 4.807577e-02, 'w_up': 3.482433e-02, 'w_down': 5.757442e-02, 'g_final': 6.447827e+01}


def _to_microbatches(a, axis):
    t = _jnp.moveaxis(a, axis, 0)
    t = t.reshape((N_MICROBATCH, t.shape[0] // N_MICROBATCH) + t.shape[1:])
    return _jnp.moveaxis(t, 1, axis + 1)


def setup_inputs(seed: int = 0) -> dict:
    inp = _fwd_setup_inputs(seed)
    key = _jax.random.fold_in(_jax.random.key(seed), 7919)
    shape, _ = _output_shape()
    out = dict(inp)
    out["loss_target"] = _jax.random.normal(_jax.random.fold_in(key, 0), shape, _jnp.float32)
    for i, name in enumerate(TWIN_WEIGHTS):
        w = inp[name].astype(_jnp.float32)
        if MOMENT_SCALE is None:
            s = _jnp.sqrt(_jnp.mean(_jnp.square(w)) + 1e-30)
        else:
            s = MOMENT_SCALE[name]
        km, kv = _jax.random.split(_jax.random.fold_in(key, i + 1))
        out[name] = w
        out["m_" + name] = s * _jax.random.normal(km, w.shape, _jnp.float32)
        out["v_" + name] = (s * s) * _jax.random.uniform(kv, w.shape, _jnp.float32, 0.5, 1.5)
    if N_MICROBATCH > 1:
        for name, axis in PER_EXAMPLE_BATCH_AXIS.items():
            out[name] = _to_microbatches(out[name], axis)
    return {'x': out['x'], 'c': out['c'], 'ctx': out['ctx'], 'c_ctx': out['c_ctx'], 'w_mod': out['w_mod'], 'b_mod': out['b_mod'], 'g_mix': out['g_mix'], 'g_ffn': out['g_ffn'], 'w_in': out['w_in'], 'lb_gamma': out['lb_gamma'], 'g_norm_a': out['g_norm_a'], 'ln_v_g': out['ln_v_g'], 'ln_v_b': out['ln_v_b'], 'w_s': out['w_s'], 'b_s': out['b_s'], 'w_pa': out['w_pa'], 'w_pb': out['w_pb'], 'w_o': out['w_o'], 'w_up': out['w_up'], 'w_down': out['w_down'], 'g_final': out['g_final'], 'loss_target': out['loss_target'], 'm_c_ctx': out['m_c_ctx'], 'm_w_mod': out['m_w_mod'], 'm_b_mod': out['m_b_mod'], 'm_g_mix': out['m_g_mix'], 'm_g_ffn': out['m_g_ffn'], 'm_w_in': out['m_w_in'], 'm_lb_gamma': out['m_lb_gamma'], 'm_g_norm_a': out['m_g_norm_a'], 'm_ln_v_g': out['m_ln_v_g'], 'm_ln_v_b': out['m_ln_v_b'], 'm_w_s': out['m_w_s'], 'm_b_s': out['m_b_s'], 'm_w_pa': out['m_w_pa'], 'm_w_pb': out['m_w_pb'], 'm_w_o': out['m_w_o'], 'm_w_up': out['m_w_up'], 'm_w_down': out['m_w_down'], 'm_g_final': out['m_g_final'], 'v_c_ctx': out['v_c_ctx'], 'v_w_mod': out['v_w_mod'], 'v_b_mod': out['v_b_mod'], 'v_g_mix': out['v_g_mix'], 'v_g_ffn': out['v_g_ffn'], 'v_w_in': out['v_w_in'], 'v_lb_gamma': out['v_lb_gamma'], 'v_g_norm_a': out['v_g_norm_a'], 'v_ln_v_g': out['v_ln_v_g'], 'v_ln_v_b': out['v_ln_v_b'], 'v_w_s': out['v_w_s'], 'v_b_s': out['v_b_s'], 'v_w_pa': out['v_w_pa'], 'v_w_pb': out['v_w_pb'], 'v_w_o': out['v_w_o'], 'v_w_up': out['v_w_up'], 'v_w_down': out['v_w_down'], 'v_g_final': out['v_g_final']}


def _loss(weights, diff, rest, loss_target):
    with _jax.named_scope("forward"):
        args = {**rest, TWIN_DIFF_INPUT: diff, **{k: w.astype(_WEIGHT_DTYPES[k]) for k, w in weights.items()}}
        y = _forward(args)
    with _jax.named_scope("loss_head"):
        err = _jnp.square(y.astype(_jnp.float32) - loss_target)
        return 0.5 * _jnp.sum(_jnp.mean(err, axis=-1)) if err.ndim else 0.5 * err


def _adamw(w, g, m, v):
    m = ADAM_B1 * m + (1.0 - ADAM_B1) * g
    v = ADAM_B2 * v + (1.0 - ADAM_B2) * _jnp.square(g)
    m_hat = m / (1.0 - ADAM_B1 ** ADAM_STEP)
    v_hat = v / (1.0 - ADAM_B2 ** ADAM_STEP)
    delta = -ADAM_LR * (m_hat / (_jnp.sqrt(v_hat) + ADAM_EPS) + ADAM_WD * w)
    return delta, m, v


def reference(x, c, ctx, c_ctx, w_mod, b_mod, g_mix, g_ffn, w_in, lb_gamma, g_norm_a, ln_v_g, ln_v_b, w_s, b_s, w_pa, w_pb, w_o, w_up, w_down, g_final, loss_target, m_c_ctx, m_w_mod, m_b_mod, m_g_mix, m_g_ffn, m_w_in, m_lb_gamma, m_g_norm_a, m_ln_v_g, m_ln_v_b, m_w_s, m_b_s, m_w_pa, m_w_pb, m_w_o, m_w_up, m_w_down, m_g_final, v_c_ctx, v_w_mod, v_b_mod, v_g_mix, v_g_ffn, v_w_in, v_lb_gamma, v_g_norm_a, v_ln_v_g, v_ln_v_b, v_w_s, v_b_s, v_w_pa, v_w_pb, v_w_o, v_w_up, v_w_down, v_g_final):
    given = dict(x=x, c=c, ctx=ctx, c_ctx=c_ctx, w_mod=w_mod, b_mod=b_mod, g_mix=g_mix, g_ffn=g_ffn, w_in=w_in, lb_gamma=lb_gamma, g_norm_a=g_norm_a, ln_v_g=ln_v_g, ln_v_b=ln_v_b, w_s=w_s, b_s=b_s, w_pa=w_pa, w_pb=w_pb, w_o=w_o, w_up=w_up, w_down=w_down, g_final=g_final, loss_target=loss_target, m_c_ctx=m_c_ctx, m_w_mod=m_w_mod, m_b_mod=m_b_mod, m_g_mix=m_g_mix, m_g_ffn=m_g_ffn, m_w_in=m_w_in, m_lb_gamma=m_lb_gamma, m_g_norm_a=m_g_norm_a, m_ln_v_g=m_ln_v_g, m_ln_v_b=m_ln_v_b, m_w_s=m_w_s, m_b_s=m_b_s, m_w_pa=m_w_pa, m_w_pb=m_w_pb, m_w_o=m_w_o, m_w_up=m_w_up, m_w_down=m_w_down, m_g_final=m_g_final, v_c_ctx=v_c_ctx, v_w_mod=v_w_mod, v_b_mod=v_b_mod, v_g_mix=v_g_mix, v_g_ffn=v_g_ffn, v_w_in=v_w_in, v_lb_gamma=v_lb_gamma, v_g_norm_a=v_g_norm_a, v_ln_v_g=v_ln_v_g, v_ln_v_b=v_ln_v_b, v_w_s=v_w_s, v_b_s=v_b_s, v_w_pa=v_w_pa, v_w_pb=v_w_pb, v_w_o=v_w_o, v_w_up=v_w_up, v_w_down=v_w_down, v_g_final=v_g_final)
    weights = {n: given[n] for n in TWIN_WEIGHTS}
    shared = {n: given[n] for n in SHARED_INPUTS}
    per_example = {n: given[n] for n in ['x', 'c', 'ctx']}
    grad_fn = _jax.value_and_grad(_loss, argnums=(0, 1))

    def one_microbatch(ex, loss_target):
        ex = dict(ex)
        diff = ex.pop(TWIN_DIFF_INPUT)
        return grad_fn(weights, diff, {**shared, **ex}, loss_target)

    if N_MICROBATCH == 1:
        loss, (grad_w, grad_x) = one_microbatch(per_example, given["loss_target"])
    else:
        def body(carry, xs):
            loss_sum, grad_sum = carry
            l_k, (gw_k, gx_k) = one_microbatch(xs[0], xs[1])
            with _jax.named_scope("update"):
                return (loss_sum + l_k, _jax.tree.map(_jnp.add, grad_sum, gw_k)), gx_k

        init = (_jnp.zeros((), _jnp.float32), _jax.tree.map(_jnp.zeros_like, weights))
        (loss, grad_w), grad_x = _jax.lax.scan(body, init, (per_example, given["loss_target"]))
    with _jax.named_scope("update"):
        delta_w, new_m, new_v = {}, {}, {}
        for n in TWIN_WEIGHTS:
            delta_w[n], new_m[n], new_v[n] = _adamw(weights[n], grad_w[n], given["m_" + n], given["v_" + n])
    return (loss, grad_x, *[grad_w[n] for n in TWIN_WEIGHTS], *[delta_w[n] for n in TWIN_WEIGHTS],
            *[new_m[n] for n in TWIN_WEIGHTS], *[new_v[n] for n in TWIN_WEIGHTS])
```

```python
import functools

import jax
import jax.numpy as jnp
from jax import lax
from jax.experimental import pallas as pl
from jax.experimental.pallas import tpu as pltpu

F32 = jnp.float32
MXU_DTYPE = jnp.bfloat16
PAYLOAD_DTYPE = jnp.bfloat16

N_DEV = 8
D = 1024
HEADS = 4
DK = 128
KW = HEADS * DK
CHUNK = 64
SGU_BLOCK = 128
GROUPS = 4
D_FF = 2816
FF_CHUNK = 256
N_MOD = 6
IN_COLS = 5632
CTX_COLS = 1536
TAIL_COLS = IN_COLS - 4 * KW
EPS = 1e-6
ADAM_LR, ADAM_B1, ADAM_B2, ADAM_EPS, ADAM_WD, ADAM_STEP = 0.001, 0.9, 0.999, 1e-08, 0.01, 10

VMEM_LIMIT = 56 * 1024 * 1024
TOKEN_TILE = 256
SMALL_ROWS = 104


def _params(sem):
    return pltpu.CompilerParams(dimension_semantics=sem, vmem_limit_bytes=VMEM_LIMIT)


_DN = {"nn": (((1,), (0,)), ((), ())), "nt": (((1,), (1,)), ((), ())), "tn": (((0,), (0,)), ((), ()))}


def _dot(a, b, form="nn"):
    return lax.dot_general(a.astype(MXU_DTYPE), b.astype(MXU_DTYPE), _DN[form], preferred_element_type=F32)


def _dotx(a, b, form="nn"):
    return lax.dot_general(a.astype(F32), b.astype(F32), _DN[form], preferred_element_type=F32,
                           precision=lax.Precision.HIGHEST)


def _full(shape, single=False):
    n = len(shape)
    if single:
        return pl.BlockSpec(shape, lambda *_: (0,) * n, pipeline_mode=pl.Buffered(1))
    return pl.BlockSpec(shape, lambda *_: (0,) * n)


def _sigmoid(z):
    return 1.0 / (1.0 + jnp.exp(-z))


def _gelu(x):
    c = 0.7978845608028654
    t = jnp.tanh(c * (x + 0.044715 * x * x * x))
    return 0.5 * x * (1.0 + t), t


def _gelu_grad(x, t):
    c = 0.7978845608028654
    return 0.5 * (1.0 + t) + 0.5 * x * (1.0 - t * t) * c * (1.0 + 3 * 0.044715 * x * x)


def _exchange(items, name):
    n = len(items)
    out_shape = []
    for a, mode in items:
        blk = a.shape if mode == "gather" else a.shape[1:]
        out_shape.append(jax.ShapeDtypeStruct((N_DEV,) + tuple(blk), a.dtype))

    def body(*refs):
        srcs, dsts = refs[:n], refs[n:2 * n]
        send_sems, recv_sems, local_sems = refs[2 * n:]
        x, y, c = lax.axis_index("x"), lax.axis_index("y"), lax.axis_index("c")
        me = 4 * x + 2 * y + c

        def src_for(i, dev):
            return srcs[i] if items[i][1] == "gather" else srcs[i].at[dev]

        local = [pltpu.make_async_copy(src_for(i, me), dsts[i].at[me], local_sems.at[i]) for i in range(n)]
        for cp in local:
            cp.start()
        remote = []
        for k in range(1, N_DEV):
            px = jnp.bitwise_xor(x, (k >> 2) & 1)
            py = jnp.bitwise_xor(y, (k >> 1) & 1)
            pc = jnp.bitwise_xor(c, k & 1)
            peer = 4 * px + 2 * py + pc
            for i in range(n):
                cp = pltpu.make_async_remote_copy(
                    src_ref=src_for(i, peer), dst_ref=dsts[i].at[me],
                    send_sem=send_sems.at[i * (N_DEV - 1) + k - 1], recv_sem=recv_sems.at[i * (N_DEV - 1) + k - 1],
                    device_id=(px, py, pc), device_id_type=pl.DeviceIdType.MESH)
                cp.start()
                remote.append(cp)
        for cp in remote:
            cp.wait()
        for cp in local:
            cp.wait()

    any_spec = pl.BlockSpec(memory_space=pl.ANY)
    return pl.pallas_call(
        body, name=name, out_shape=out_shape,
        in_specs=[any_spec] * n, out_specs=[any_spec] * n,
        scratch_shapes=[pltpu.SemaphoreType.DMA((n * (N_DEV - 1),)), pltpu.SemaphoreType.DMA((n * (N_DEV - 1),)),
                        pltpu.SemaphoreType.DMA((n,))],
    )(*[a for a, _ in items])


def _mod_fwd(cvec, w_mod_l, b_mod_l):
    rows, cols = cvec.shape[0], w_mod_l.shape[1]

    def body(c_ref, w_ref, b_ref, o_ref, s_ref):
        cv = c_ref[...]
        s = cv * _sigmoid(cv)
        s_ref[...] = s
        o_ref[...] = _dot(s, w_ref[...]) + b_ref[...]

    return pl.pallas_call(
        body, name="mod_fwd",
        out_shape=(jax.ShapeDtypeStruct((rows, cols), F32), jax.ShapeDtypeStruct((rows, D), F32)),
        in_specs=[_full((rows, D)), _full((D, cols)), _full((1, cols))],
        out_specs=(_full((rows, cols)), _full((rows, D))), grid=(1,),
        compiler_params=_params(("arbitrary",)),
    )(cvec, w_mod_l, b_mod_l)


def _mod_bwd(svec, cvec, dmod_l, w_mod_l):
    rows, cols = dmod_l.shape

    def body(s_ref, c_ref, d_ref, w_ref, gw_ref, gc_ref):
        gw_ref[...] = _dot(s_ref[...], d_ref[...], "tn")
        cv = c_ref[...]
        sg = _sigmoid(cv)
        gc_ref[...] = _dot(d_ref[...], w_ref[...], "nt") * (sg * (1.0 + cv * (1.0 - sg)))

    return pl.pallas_call(
        body, name="mod_bwd",
        out_shape=(jax.ShapeDtypeStruct((D, cols), F32), jax.ShapeDtypeStruct((rows, D), F32)),
        in_specs=[_full((rows, D)), _full((rows, D)), _full((rows, cols)), _full((D, cols))],
        out_specs=(_full((D, cols)), _full((rows, D))), grid=(1,),
        compiler_params=_params(("arbitrary",)),
    )(svec, cvec, dmod_l, w_mod_l)


def _inproj(xt, modv, g, w_inT, n_cols, rows_per_example, name):
    rows = xt.shape[0]
    tm = min(TOKEN_TILE, rows_per_example)
    per_b = rows_per_example // tm
    shared_mod = modv.shape[0] == 1

    def body(x_ref, mod_ref, g_ref, w_ref, p_ref, h_ref):
        x = x_ref[...]
        r = lax.rsqrt(jnp.mean(x * x, axis=-1, keepdims=True) + EPS)
        h = (x * r * g_ref[...]) * (1.0 + mod_ref[0, 1:2, :]) + mod_ref[0, 0:1, :]
        hb = h.astype(MXU_DTYPE)
        h_ref[...] = hb
        for j in range(n_cols // KW):
            p_ref[:, j * KW:(j + 1) * KW] = _dot(hb, w_ref[j * KW:(j + 1) * KW, :], "nt").astype(p_ref.dtype)

    mod_idx = (lambda i: (0, 0, 0)) if shared_mod else (lambda i: (i // per_b, 0, 0))
    return pl.pallas_call(
        body, name=name,
        out_shape=(jax.ShapeDtypeStruct((rows, n_cols), MXU_DTYPE), jax.ShapeDtypeStruct((rows, D), MXU_DTYPE)),
        grid=(rows // tm,),
        in_specs=[pl.BlockSpec((tm, D), lambda i: (i, 0)), pl.BlockSpec((1, N_MOD, D), mod_idx), _full((1, D)),
                  pl.BlockSpec((n_cols, D), lambda i: (0, 0), pipeline_mode=pl.Buffered(1))],
        out_specs=(pl.BlockSpec((tm, n_cols), lambda i: (i, 0)), pl.BlockSpec((tm, D), lambda i: (i, 0))),
        compiler_params=_params(("arbitrary",)),
    )(xt, modv, g, w_inT)


def _tri(reverse):
    row = lax.broadcasted_iota(jnp.int32, (CHUNK, CHUNK), 0)
    col = lax.broadcasted_iota(jnp.int32, (CHUNK, CHUNK), 1)
    return (col >= row) if reverse else (col <= row)


def _lower_bound(gam_ref, direction):
    return _sigmoid(gam_ref[direction:direction + 1, :] - gam_ref[2 + direction:3 + direction, :])


def _gate_prep(z, lb, tri_f):
    sg = _sigmoid(z)
    f = lb + (1.0 - lb) * sg
    g = jnp.log(f)
    b = _dotx(tri_f, g)
    bl = jnp.sum(g, axis=0, keepdims=True)
    return sg, f, 1.0 - f, b, bl


def _hgrn_fwd(p, gam, s0, rows_per_example, with_out, name):
    rows = p.shape[0]
    nb_ex = rows // rows_per_example
    rb = min(TOKEN_TILE, rows_per_example)
    cpb = rb // CHUNK
    nb = rows_per_example // rb
    n_chunks = rows // CHUNK
    has_s0 = s0 is not None

    def body(*refs):
        it = iter(refs)
        gam_ref = next(it)
        zf_ref, vf_ref = next(it), next(it)
        qf_ref = next(it) if with_out else None
        zb_ref, vb_ref = next(it), next(it)
        qb_ref = next(it) if with_out else None
        s0_ref = next(it) if has_s0 else None
        if with_out:
            of_ref, ob_ref = next(it), next(it)
        stash_f, stash_b, fin_ref = next(it), next(it), next(it)
        st_ref = next(it)
        i = pl.program_id(1)

        @pl.when(i == 0)
        def _():
            if has_s0:
                st_ref[...] = s0_ref[:, 0]
            else:
                st_ref[...] = jnp.zeros_like(st_ref)

        for direction, (z_ref, v_ref, q_ref, stash) in enumerate(
                ((zf_ref, vf_ref, qf_ref, stash_f), (zb_ref, vb_ref, qb_ref, stash_b))):
            reverse = direction == 1
            tri = _tri(reverse)
            tri_f = tri.astype(F32)
            lb = _lower_bound(gam_ref, direction)
            order = range(cpb - 1, -1, -1) if reverse else range(cpb)
            for j in order:
                rs = slice(j * CHUNK, (j + 1) * CHUNK)
                z = z_ref[rs, :].astype(F32)
                v = v_ref[rs, :].astype(F32)
                _, _, k, b, bl = _gate_prep(z, lb, tri_f)
                mid = 0.5 * bl
                kd = k * jnp.exp(bl - b)
                a = jnp.exp(bl)
                if with_out:
                    q = q_ref[rs, :].astype(F32)
                    qi = q * jnp.exp(b - mid)
                    ki = k * jnp.exp(mid - b)
                    qe = q * jnp.exp(b)
                for h in range(HEADS):
                    hs = slice(h * DK, (h + 1) * DK)
                    st = st_ref[direction, h]
                    stash[j, h] = st.astype(stash.dtype)
                    if with_out:
                        sc = jnp.where(tri, _dot(qi[:, hs], ki[:, hs], "nt"), 0.0)
                        o = _dot(sc, v[:, hs]) + _dot(qe[:, hs], st, "nt")
                        (ob_ref if reverse else of_ref)[rs, hs] = o
                    st_ref[direction, h] = st * a[:, hs] + _dot(v[:, hs], kd[:, hs], "tn")

        @pl.when(i == nb - 1)
        def _():
            fin_ref[:, 0] = st_ref[...]

    up = lambda b, i: b * nb + i
    down = lambda b, i: b * nb + nb - 1 - i
    col = lambda rowf, c: pl.BlockSpec((rb, KW), lambda b, i: (rowf(b, i), c))
    in_specs = [_full((4, KW)), col(up, 0), col(up, 2)] + ([col(up, 3)] if with_out else [])
    in_specs += [col(down, 1), col(down, 2)] + ([col(down, 3)] if with_out else [])
    args = [gam, p, p] + ([p] if with_out else []) + [p, p] + ([p] if with_out else [])
    if has_s0:
        in_specs.append(pl.BlockSpec((2, 1, HEADS, DK, DK), lambda b, i: (0, b, 0, 0, 0)))
        args.append(s0)
    out_shape, out_specs = [], []
    if with_out:
        out_shape += [jax.ShapeDtypeStruct((rows, KW), F32)] * 2
        out_specs += [pl.BlockSpec((rb, KW), lambda b, i: (up(b, i), 0)),
                      pl.BlockSpec((rb, KW), lambda b, i: (down(b, i), 0))]
    out_shape += [jax.ShapeDtypeStruct((n_chunks, HEADS, DK, DK), MXU_DTYPE)] * 2
    out_specs += [pl.BlockSpec((cpb, HEADS, DK, DK), lambda b, i: (up(b, i), 0, 0, 0)),
                  pl.BlockSpec((cpb, HEADS, DK, DK), lambda b, i: (down(b, i), 0, 0, 0))]
    out_shape.append(jax.ShapeDtypeStruct((2, nb_ex, HEADS, DK, DK), F32))
    out_specs.append(pl.BlockSpec((2, 1, HEADS, DK, DK), lambda b, i: (0, b, 0, 0, 0)))
    return pl.pallas_call(
        body, name=name, out_shape=out_shape, grid=(nb_ex, nb), in_specs=in_specs, out_specs=out_specs,
        scratch_shapes=[pltpu.VMEM((2, HEADS, DK, DK), F32)],
        compiler_params=_params(("arbitrary", "arbitrary")),
    )(*args)


def _hgrn_bwd(p, gam, do, stash_f, stash_b, ds_end, rows_per_example, with_out, name):
    rows = p.shape[0]
    nb_ex = rows // rows_per_example
    rb = min(TOKEN_TILE, rows_per_example)
    cpb = rb // CHUNK
    nb = rows_per_example // rb
    has_end = ds_end is not None

    def body(*refs):
        it = iter(refs)
        gam_ref = next(it)
        ins = []
        for _ in range(2):
            z_ref, v_ref = next(it), next(it)
            q_ref = next(it) if with_out else None
            do_ref = next(it) if with_out else None
            ins.append((z_ref, v_ref, q_ref, do_ref, next(it)))
        end_ref = next(it) if has_end else None
        outs = []
        for _ in range(2):
            dz_ref, dv_ref = next(it), next(it)
            dq_ref = next(it) if with_out else None
            outs.append((dz_ref, dv_ref, dq_ref))
        dlb_ref, ds0_ref = next(it), next(it)
        dst_ref = next(it)
        b_id, i = pl.program_id(0), pl.program_id(1)

        @pl.when(i == 0)
        def _():
            if has_end:
                dst_ref[...] = end_ref[:, 0]
            else:
                dst_ref[...] = jnp.zeros_like(dst_ref)

        @pl.when((i == 0) & (b_id == 0))
        def _():
            dlb_ref[...] = jnp.zeros_like(dlb_ref)

        for direction in range(2):
            z_ref, v_ref, q_ref, do_ref, stash = ins[direction]
            dz_ref, dv_ref, dq_ref = outs[direction]
            reverse = direction == 1
            tri = _tri(reverse)
            tri_f = tri.astype(F32)
            lb = _lower_bound(gam_ref, direction)
            order = range(cpb) if reverse else range(cpb - 1, -1, -1)
            dlb_acc = jnp.zeros((1, KW), F32)
            for j in order:
                rs = slice(j * CHUNK, (j + 1) * CHUNK)
                z = z_ref[rs, :].astype(F32)
                v = v_ref[rs, :].astype(F32)
                sg, f, k, b, bl = _gate_prep(z, lb, tri_f)
                mid = 0.5 * bl
                e3 = jnp.exp(bl - b)
                kd = k * e3
                a = jnp.exp(bl)
                if with_out:
                    q = q_ref[rs, :].astype(F32)
                    dout = do_ref[rs, :].astype(F32)
                    e1, e2, e4 = jnp.exp(b - mid), jnp.exp(mid - b), jnp.exp(b)
                    qi, ki, qe = q * e1, k * e2, q * e4
                dkd_p, dv_p, da_p, dqi_p, dki_p, dqe_p = [], [], [], [], [], []
                for h in range(HEADS):
                    hs = slice(h * DK, (h + 1) * DK)
                    st_in = stash[j, h]
                    dst = dst_ref[direction, h]
                    dkd_p.append(_dot(v[:, hs], dst))
                    dvh = _dot(kd[:, hs], dst, "nt")
                    da_p.append(jnp.sum(dst * st_in.astype(F32), axis=0, keepdims=True))
                    new_dst = dst * a[:, hs]
                    if with_out:
                        sc = jnp.where(tri, _dot(qi[:, hs], ki[:, hs], "nt"), 0.0)
                        dsc = jnp.where(tri, _dot(dout[:, hs], v[:, hs], "nt"), 0.0)
                        dqi_p.append(_dot(dsc, ki[:, hs]))
                        dki_p.append(_dot(dsc, qi[:, hs], "tn"))
                        dqe_p.append(_dot(dout[:, hs], st_in))
                        dvh = dvh + _dot(sc, dout[:, hs], "tn")
                        new_dst = new_dst + _dot(dout[:, hs], qe[:, hs], "tn")
                    dv_p.append(dvh)
                    dst_ref[direction, h] = new_dst
                cat = lambda parts: jnp.concatenate(parts, axis=1)
                dkd, da = cat(dkd_p), cat(da_p)
                dv_ref[rs, :] = cat(dv_p)
                t_kd = dkd * kd
                dk = dkd * e3
                db = -t_kd
                dbl = jnp.sum(t_kd, axis=0, keepdims=True) + da * a
                if with_out:
                    dqi, dki, dqe = cat(dqi_p), cat(dki_p), cat(dqe_p)
                    dq_ref[rs, :] = dqi * e1 + dqe * e4
                    dk = dk + dki * e2
                    t_qi, t_ki, t_qe = dqi * qi, dki * ki, dqe * qe
                    db = db + t_qi - t_ki + t_qe
                    dbl = dbl + 0.5 * jnp.sum(t_ki - t_qi, axis=0, keepdims=True)
                dg = _dotx(tri_f, db, "tn") + dbl
                df = dg / f - dk
                dz_ref[rs, :] = df * (1.0 - lb) * sg * (1.0 - sg)
                dlb_acc = dlb_acc + jnp.sum(df * (1.0 - sg), axis=0, keepdims=True)
            dlb_ref[direction:direction + 1, :] += dlb_acc

        @pl.when(i == nb - 1)
        def _():
            ds0_ref[:, 0] = dst_ref[...]

    rows_of = (lambda b, i: b * nb + nb - 1 - i, lambda b, i: b * nb + i)
    in_specs, args = [_full((4, KW))], [gam]
    for direction in range(2):
        rf = rows_of[direction]
        col = lambda c, rf=rf: pl.BlockSpec((rb, KW), lambda b, i: (rf(b, i), c))
        in_specs += [col(direction), col(2)]
        args += [p, p]
        if with_out:
            in_specs += [col(3), col(0)]
            args += [p, do]
        in_specs.append(pl.BlockSpec((cpb, HEADS, DK, DK), lambda b, i, rf=rf: (rf(b, i), 0, 0, 0)))
        args.append((stash_f, stash_b)[direction])
    if has_end:
        in_specs.append(pl.BlockSpec((2, 1, HEADS, DK, DK), lambda b, i: (0, b, 0, 0, 0)))
        args.append(ds_end)
    out_shape, out_specs = [], []
    for direction in range(2):
        rf = rows_of[direction]
        n_out = 3 if with_out else 2
        out_shape += [jax.ShapeDtypeStruct((rows, KW), F32)] * n_out
        out_specs += [pl.BlockSpec((rb, KW), lambda b, i, rf=rf: (rf(b, i), 0))] * n_out
    out_shape += [jax.ShapeDtypeStruct((2, KW), F32), jax.ShapeDtypeStruct((2, nb_ex, HEADS, DK, DK), F32)]
    out_specs += [_full((2, KW)), pl.BlockSpec((2, 1, HEADS, DK, DK), lambda b, i: (0, b, 0, 0, 0))]
    return pl.pallas_call(
        body, name=name, out_shape=out_shape, grid=(nb_ex, nb), in_specs=in_specs, out_specs=out_specs,
        scratch_shapes=[pltpu.VMEM((2, HEADS, DK, DK), F32)],
        compiler_params=_params(("arbitrary", "arbitrary")),
    )(*args)


def _tail_forward(osum, og, u, v, ga, gb, gna, ln_g, ln_b, ws_ref, bs_ref, wpaT_ref, wpbT_ref):
    tm = osum.shape[0]
    gna4 = jnp.concatenate([gna] * HEADS, axis=1)
    r_parts = []
    for h in range(HEADS):
        oh = osum[:, h * DK:(h + 1) * DK]
        r_parts.append(jnp.broadcast_to(lax.rsqrt(jnp.mean(oh * oh, axis=-1, keepdims=True) + EPS), (tm, DK)))
    r = jnp.concatenate(r_parts, axis=1)
    on = osum * r
    sg_og = _sigmoid(og)
    silu_og = og * sg_og
    oan = on * gna4
    oa = oan * silu_og
    ug, tu = _gelu(u)
    vg, tv = _gelu(v)
    mu = jnp.mean(vg, axis=-1, keepdims=True)
    vc = vg - mu
    rstd = lax.rsqrt(jnp.mean(vc * vc, axis=-1, keepdims=True) + EPS)
    vhat = vc * rstd
    vln = vhat * ln_g + ln_b
    blocks = []
    for n in range(tm // SGU_BLOCK):
        rs = slice(n * SGU_BLOCK, (n + 1) * SGU_BLOCK)
        blocks.append(jnp.concatenate(
            [_dot(ws_ref[g], vln[rs, g * DK:(g + 1) * DK]) + bs_ref[g] for g in range(GROUPS)], axis=1))
    mixed = jnp.concatenate(blocks, axis=0) if len(blocks) > 1 else blocks[0]
    obm = ug * mixed
    pa = _dot(oa, wpaT_ref[...], "nt")
    pb = _dot(obm, wpbT_ref[...], "nt")
    sga, sgb = _sigmoid(ga), _sigmoid(gb)
    merged = sga * pa + sgb * pb
    return dict(r=r, on=on, sg_og=sg_og, silu_og=silu_og, oan=oan, oa=oa, ug=ug, tu=tu, tv=tv, rstd=rstd, vhat=vhat,
                vln=vln, mixed=mixed, obm=obm, pa=pa, pb=pb, sga=sga, sgb=sgb, merged=merged, gna4=gna4)


def _tail_in_specs(tm):
    tile = lambda c: pl.BlockSpec((tm, KW), lambda i: (i, c))
    return [tile(c) for c in range(4, 11)]


def _tail_weight_specs():
    return [_full((1, DK)), _full((1, KW)), _full((1, KW)), _full((GROUPS, SGU_BLOCK, SGU_BLOCK)),
            _full((GROUPS, SGU_BLOCK, 1)), _full((D, KW), single=True), _full((D, KW), single=True),
            _full((D, D), single=True)]


def _read_tail_inputs(of_ref, ob_ref, pcols):
    osum = of_ref[...] + ob_ref[...]
    og, u, v = (pcols[j][...].astype(F32) for j in range(3))
    ga = jnp.concatenate([pcols[3][...], pcols[4][...]], axis=1).astype(F32)
    gb = jnp.concatenate([pcols[5][...], pcols[6][...]], axis=1).astype(F32)
    return osum, og, u, v, ga, gb


def _tail_fwd(p, o_up, o_down, xt, modv, gna, ln_g, ln_b, w_s, b_s, w_paT, w_pbT, w_o, rows_per_example):
    rows = xt.shape[0]
    tm = min(TOKEN_TILE, rows_per_example)
    per_b = rows_per_example // tm

    def body(of_ref, ob_ref, *rest):
        pcols = rest[:7]
        (x_ref, mod_ref, gna_ref, lng_ref, lnb_ref, ws_ref, bs_ref, wpaT_ref, wpbT_ref, wo_ref,
         x1_ref, mix_ref, merged_ref, oa_ref, obm_ref) = rest[7:]
        t = _tail_forward(*_read_tail_inputs(of_ref, ob_ref, pcols), gna_ref[...], lng_ref[...], lnb_ref[...],
                          ws_ref, bs_ref, wpaT_ref, wpbT_ref)
        mix = _dot(t["merged"], wo_ref[...])
        x1_ref[...] = x_ref[...] + mod_ref[0, 2:3, :] * mix
        mix_ref[...] = mix.astype(mix_ref.dtype)
        merged_ref[...] = t["merged"].astype(merged_ref.dtype)
        oa_ref[...] = t["oa"].astype(oa_ref.dtype)
        obm_ref[...] = t["obm"].astype(obm_ref.dtype)

    row = lambda w: pl.BlockSpec((tm, w), lambda i: (i, 0))
    in_specs = [row(KW), row(KW)] + _tail_in_specs(tm) + [row(D), pl.BlockSpec((1, N_MOD, D), lambda i: (i // per_b, 0, 0))]
    in_specs += _tail_weight_specs()
    return pl.pallas_call(
        body, name="tail_fwd", grid=(rows // tm,),
        out_shape=(jax.ShapeDtypeStruct((rows, D), F32), jax.ShapeDtypeStruct((rows, D), MXU_DTYPE),
                   jax.ShapeDtypeStruct((rows, D), MXU_DTYPE), jax.ShapeDtypeStruct((rows, KW), MXU_DTYPE),
                   jax.ShapeDtypeStruct((rows, KW), MXU_DTYPE)),
        in_specs=in_specs, out_specs=(row(D), row(D), row(D), row(KW), row(KW)),
        compiler_params=_params(("arbitrary",)),
    )(o_up, o_down, *([p] * 7), xt, modv, gna, ln_g, ln_b, w_s, b_s, w_paT, w_pbT, w_o)


def _tail_bwd(p, o_up, o_down, dx1, mix, modv, gna, ln_g, ln_b, w_s, b_s, w_paT, w_pbT, w_o, rows_per_example):
    rows = dx1.shape[0]
    nb_ex = rows // rows_per_example
    tm = min(TOKEN_TILE, rows_per_example)
    per_b = rows_per_example // tm

    def body(of_ref, ob_ref, *rest):
        pcols = rest[:7]
        (dx1_ref, mix_ref, mod_ref, gna_ref, lng_ref, lnb_ref, ws_ref, bs_ref, wpaT_ref, wpbT_ref, wo_ref,
         dpt_ref, do_ref, dmix_ref, dpa_ref, dpb_ref, dmod_ref, small_ref, dws_ref, dbs_ref) = rest[7:]
        i = pl.program_id(0)

        @pl.when(i == 0)
        def _():
            small_ref[...] = jnp.zeros_like(small_ref)
            dws_ref[...] = jnp.zeros_like(dws_ref)
            dbs_ref[...] = jnp.zeros_like(dbs_ref)

        @pl.when(i % per_b == 0)
        def _():
            dmod_ref[...] = jnp.zeros_like(dmod_ref)

        osum, og, u, v, ga, gb = _read_tail_inputs(of_ref, ob_ref, pcols)
        ln_g = lng_ref[...]
        t = _tail_forward(osum, og, u, v, ga, gb, gna_ref[...], ln_g, lnb_ref[...], ws_ref, bs_ref, wpaT_ref, wpbT_ref)
        dx1v = dx1_ref[...]
        dmod_ref[0, 2:3, :] += jnp.sum(dx1v * mix_ref[...].astype(F32), axis=0, keepdims=True)
        dmix = dx1v * mod_ref[0, 2:3, :]
        dmix_ref[...] = dmix.astype(dmix_ref.dtype)
        dmerged = _dot(dmix, wo_ref[...], "nt")
        sga, sgb = t["sga"], t["sgb"]
        dpa = dmerged * sga
        dpb = dmerged * sgb
        dpa_ref[...] = dpa.astype(dpa_ref.dtype)
        dpb_ref[...] = dpb.astype(dpb_ref.dtype)
        dga = dmerged * t["pa"] * sga * (1.0 - sga)
        dgb = dmerged * t["pb"] * sgb * (1.0 - sgb)
        doa = _dot(dpa, wpaT_ref[...])
        dobm = _dot(dpb, wpbT_ref[...])
        dug = dobm * t["mixed"]
        dmixed = dobm * t["ug"]
        du = dug * _gelu_grad(u, t["tu"])
        dvln_blocks = []
        for n in range(tm // SGU_BLOCK):
            rs = slice(n * SGU_BLOCK, (n + 1) * SGU_BLOCK)
            parts = []
            for g in range(GROUPS):
                gs = slice(g * DK, (g + 1) * DK)
                dm = dmixed[rs, gs]
                parts.append(_dot(ws_ref[g], dm, "tn"))
                dws_ref[g] += _dot(dm, t["vln"][rs, gs], "nt")
                dbs_ref[g] += jnp.sum(dm, axis=1, keepdims=True)
            dvln_blocks.append(jnp.concatenate(parts, axis=1))
        dvln = jnp.concatenate(dvln_blocks, axis=0) if len(dvln_blocks) > 1 else dvln_blocks[0]
        vhat = t["vhat"]
        small_ref[1:2, 0:KW] += jnp.sum(dvln * vhat, axis=0, keepdims=True)
        small_ref[2:3, 0:KW] += jnp.sum(dvln, axis=0, keepdims=True)
        dvhat = dvln * ln_g
        dvg = t["rstd"] * (dvhat - jnp.mean(dvhat, axis=-1, keepdims=True)
                           - vhat * jnp.mean(dvhat * vhat, axis=-1, keepdims=True))
        dv = dvg * _gelu_grad(v, t["tv"])
        sg_og = t["sg_og"]
        doan = doa * t["silu_og"]
        dog = doa * t["oan"] * (sg_og * (1.0 + og * (1.0 - sg_og)))
        prod = doan * t["on"]
        dgna = jnp.zeros((1, DK), F32)
        for h in range(HEADS):
            dgna = dgna + jnp.sum(prod[:, h * DK:(h + 1) * DK], axis=0, keepdims=True)
        small_ref[0:1, 0:DK] += dgna
        don = doan * t["gna4"]
        dot_parts = []
        for h in range(HEADS):
            hs = slice(h * DK, (h + 1) * DK)
            m = jnp.mean(don[:, hs] * t["on"][:, hs], axis=-1, keepdims=True)
            dot_parts.append(t["r"][:, hs] * (don[:, hs] - t["on"][:, hs] * m))
        do_ref[...] = jnp.concatenate(dot_parts, axis=1).astype(do_ref.dtype)
        for j, val in enumerate((dog, du, dv)):
            dpt_ref[:, j * KW:(j + 1) * KW] = val.astype(dpt_ref.dtype)
        dpt_ref[:, 3 * KW:3 * KW + D] = dga.astype(dpt_ref.dtype)
        dpt_ref[:, 3 * KW + D:] = dgb.astype(dpt_ref.dtype)

    row = lambda w: pl.BlockSpec((tm, w), lambda i: (i, 0))
    in_specs = [row(KW), row(KW)] + _tail_in_specs(tm) + [row(D), row(D), pl.BlockSpec((1, N_MOD, D), lambda i: (i // per_b, 0, 0))]
    in_specs += _tail_weight_specs()
    cd = MXU_DTYPE
    return pl.pallas_call(
        body, name="tail_bwd", grid=(rows // tm,),
        out_shape=(jax.ShapeDtypeStruct((rows, TAIL_COLS), cd), jax.ShapeDtypeStruct((rows, KW), cd),
                   jax.ShapeDtypeStruct((rows, D), cd), jax.ShapeDtypeStruct((rows, D), cd),
                   jax.ShapeDtypeStruct((rows, D), cd), jax.ShapeDtypeStruct((nb_ex, 8, D), F32),
                   jax.ShapeDtypeStruct((8, D), F32), jax.ShapeDtypeStruct((GROUPS, SGU_BLOCK, SGU_BLOCK), F32),
                   jax.ShapeDtypeStruct((GROUPS, SGU_BLOCK, 1), F32)),
        in_specs=in_specs,
        out_specs=(row(TAIL_COLS), row(KW), row(D), row(D), row(D),
                   pl.BlockSpec((1, 8, D), lambda i: (i // per_b, 0, 0)), _full((8, D)),
                   _full((GROUPS, SGU_BLOCK, SGU_BLOCK)), _full((GROUPS, SGU_BLOCK, 1))),
        compiler_params=_params(("arbitrary",)),
    )(o_up, o_down, *([p] * 7), dx1, mix, modv, gna, ln_g, ln_b, w_s, b_s, w_paT, w_pbT, w_o)


def _ffn(x1, target, modv, g_ffn, g_final, w_upT, w_down, rows_per_example):
    rows = x1.shape[0]
    nb_ex = rows // rows_per_example
    tm = min(TOKEN_TILE, rows_per_example)
    per_b = rows_per_example // tm
    n_ff = D_FF // FF_CHUNK

    def body(x1_ref, tgt_ref, mod_ref, gffn_ref, gfin_ref, wup_ref, wdn_ref,
             dx1_ref, h2_ref, dffn_ref, act_ref, dup_ref, dmod_ref, small_ref, a_scr, b_scr):
        i = pl.program_id(0)

        @pl.when(i == 0)
        def _():
            small_ref[...] = jnp.zeros_like(small_ref)

        @pl.when(i % per_b == 0)
        def _():
            dmod_ref[...] = jnp.zeros_like(dmod_ref)

        x1v = x1_ref[...]
        g2 = gffn_ref[...]
        m3, m4, m5 = mod_ref[0, 3:4, :], mod_ref[0, 4:5, :], mod_ref[0, 5:6, :]
        r2 = lax.rsqrt(jnp.mean(x1v * x1v, axis=-1, keepdims=True) + EPS)
        xn2 = x1v * r2
        h2 = (xn2 * g2) * (1.0 + m4) + m3
        h2b = h2.astype(MXU_DTYPE)
        h2_ref[...] = h2b
        ffn = jnp.zeros((tm, D), F32)
        for j in range(n_ff):
            cs = slice(j * FF_CHUNK, (j + 1) * FF_CHUNK)
            a = _dot(h2b, wup_ref[j * FF_CHUNK:(j + 1) * FF_CHUNK, :], "nt")
            bgate = _dot(h2b, wup_ref[D_FF + j * FF_CHUNK:D_FF + (j + 1) * FF_CHUNK, :], "nt")
            a_scr[:, cs] = a
            b_scr[:, cs] = bgate
            act = (a * _sigmoid(a) * bgate).astype(MXU_DTYPE)
            act_ref[:, cs] = act
            ffn = ffn + _dot(act, wdn_ref[cs, :])
        x2 = x1v + m5 * ffn
        r3 = lax.rsqrt(jnp.mean(x2 * x2, axis=-1, keepdims=True) + EPS)
        xn3 = x2 * r3
        gf = gfin_ref[...]
        err = xn3 * gf - tgt_ref[...]
        loss = 0.5 * jnp.sum(jnp.mean(err * err, axis=-1, keepdims=True), axis=0, keepdims=True)
        small_ref[2:3, :] += jnp.broadcast_to(loss, (1, D))
        dy = err * (1.0 / D)
        small_ref[1:2, :] += jnp.sum(dy * xn3, axis=0, keepdims=True)
        dxn3 = dy * gf
        dx2 = r3 * (dxn3 - xn3 * jnp.mean(dxn3 * xn3, axis=-1, keepdims=True))
        dmod_ref[0, 5:6, :] += jnp.sum(dx2 * ffn, axis=0, keepdims=True)
        dffn = (dx2 * m5).astype(MXU_DTYPE)
        dffn_ref[...] = dffn
        dh2 = jnp.zeros((tm, D), F32)
        for j in range(n_ff):
            cs = slice(j * FF_CHUNK, (j + 1) * FF_CHUNK)
            dact = _dot(dffn, wdn_ref[cs, :], "nt")
            a, bgate = a_scr[:, cs], b_scr[:, cs]
            s = _sigmoid(a)
            da = (dact * bgate * (s * (1.0 + a * (1.0 - s)))).astype(MXU_DTYPE)
            dbg = (dact * a * s).astype(MXU_DTYPE)
            dup_ref[:, cs] = da
            dup_ref[:, D_FF + j * FF_CHUNK:D_FF + (j + 1) * FF_CHUNK] = dbg
            dh2 = dh2 + _dot(da, wup_ref[j * FF_CHUNK:(j + 1) * FF_CHUNK, :])
            dh2 = dh2 + _dot(dbg, wup_ref[D_FF + j * FF_CHUNK:D_FF + (j + 1) * FF_CHUNK, :])
        dmod_ref[0, 3:4, :] += jnp.sum(dh2, axis=0, keepdims=True)
        dmod_ref[0, 4:5, :] += jnp.sum(dh2 * xn2 * g2, axis=0, keepdims=True)
        small_ref[0:1, :] += jnp.sum(dh2 * (1.0 + m4) * xn2, axis=0, keepdims=True)
        dxn2 = dh2 * g2 * (1.0 + m4)
        dx1_ref[...] = dx2 + r2 * (dxn2 - xn2 * jnp.mean(dxn2 * xn2, axis=-1, keepdims=True))

    row = lambda w: pl.BlockSpec((tm, w), lambda i: (i, 0))
    cd = MXU_DTYPE
    return pl.pallas_call(
        body, name="ffn_fwd_bwd", grid=(rows // tm,),
        out_shape=(jax.ShapeDtypeStruct((rows, D), F32), jax.ShapeDtypeStruct((rows, D), cd),
                   jax.ShapeDtypeStruct((rows, D), cd), jax.ShapeDtypeStruct((rows, D_FF), cd),
                   jax.ShapeDtypeStruct((rows, 2 * D_FF), cd), jax.ShapeDtypeStruct((nb_ex, 8, D), F32),
                   jax.ShapeDtypeStruct((8, D), F32)),
        in_specs=[row(D), row(D), pl.BlockSpec((1, N_MOD, D), lambda i: (i // per_b, 0, 0)), _full((1, D)), _full((1, D)),
                  _full((2 * D_FF, D), single=True), _full((D_FF, D), single=True)],
        out_specs=(row(D), row(D), row(D), row(D_FF), row(2 * D_FF),
                   pl.BlockSpec((1, 8, D), lambda i: (i // per_b, 0, 0)), _full((8, D))),
        scratch_shapes=[pltpu.VMEM((tm, D_FF), F32), pltpu.VMEM((tm, D_FF), F32)],
        compiler_params=_params(("arbitrary",)),
    )(x1, target, modv, g_ffn, g_final, w_upT, w_down)


def _inproj_bwd(pieces, dpt, xt, dx1, modv, g, w_inT, rows_per_example, name):
    rows = xt.shape[0]
    latent = dx1 is not None
    n_cols = IN_COLS if latent else CTX_COLS
    tm = min(TOKEN_TILE, rows_per_example)
    per_b = rows_per_example // tm
    n_mod_blocks = rows // rows_per_example if latent else 1
    n_pieces = len(pieces)

    def body(*refs):
        it = iter(refs)
        pc = [next(it) for _ in range(n_pieces)]
        dpt_ref = next(it) if latent else None
        x_ref = next(it)
        dx1_ref = next(it) if latent else None
        mod_ref, g_ref, w_ref = next(it), next(it), next(it)
        gx_ref = next(it) if latent else None
        dp_ref, dmod_ref, small_ref = next(it), next(it), next(it)
        i = pl.program_id(0)

        @pl.when(i == 0)
        def _():
            small_ref[...] = jnp.zeros_like(small_ref)

        @pl.when((i % per_b == 0) if latent else (i == 0))
        def _():
            dmod_ref[...] = jnp.zeros_like(dmod_ref)

        cols = [pc[0][...], pc[1][...], pc[2][...] + pc[3][...]]
        if latent:
            cols.append(pc[4][...] + pc[5][...])
        dh = jnp.zeros((tm, D), F32)
        for j, val in enumerate(cols):
            vb = val.astype(MXU_DTYPE)
            dp_ref[:, j * KW:(j + 1) * KW] = vb
            dh = dh + _dot(vb, w_ref[j * KW:(j + 1) * KW, :])
        if latent:
            for j in range(4, IN_COLS // KW):
                vb = dpt_ref[:, (j - 4) * KW:(j - 3) * KW]
                dp_ref[:, j * KW:(j + 1) * KW] = vb
                dh = dh + _dot(vb, w_ref[j * KW:(j + 1) * KW, :])
        x = x_ref[...]
        gv = g_ref[...]
        m1 = mod_ref[0, 1:2, :]
        r = lax.rsqrt(jnp.mean(x * x, axis=-1, keepdims=True) + EPS)
        xn = x * r
        dmod_ref[0, 0:1, :] += jnp.sum(dh, axis=0, keepdims=True)
        dmod_ref[0, 1:2, :] += jnp.sum(dh * xn * gv, axis=0, keepdims=True)
        small_ref[0:1, :] += jnp.sum(dh * (1.0 + m1) * xn, axis=0, keepdims=True)
        if latent:
            dxn = dh * gv * (1.0 + m1)
            gx_ref[...] = dx1_ref[...] + r * (dxn - xn * jnp.mean(dxn * xn, axis=-1, keepdims=True))

    row = lambda w: pl.BlockSpec((tm, w), lambda i: (i, 0))
    mod_idx = (lambda i: (i // per_b, 0, 0)) if latent else (lambda i: (0, 0, 0))
    in_specs = [row(KW)] * n_pieces + ([row(TAIL_COLS)] if latent else []) + [row(D)] + ([row(D)] if latent else [])
    in_specs += [pl.BlockSpec((1, N_MOD, D), mod_idx), _full((1, D)),
                 pl.BlockSpec((n_cols, D), lambda i: (0, 0), pipeline_mode=pl.Buffered(1))]
    args = list(pieces) + ([dpt] if latent else []) + [xt] + ([dx1] if latent else []) + [modv, g, w_inT]
    out_shape = ([jax.ShapeDtypeStruct((rows, D), F32)] if latent else []) + [
        jax.ShapeDtypeStruct((rows, n_cols), MXU_DTYPE), jax.ShapeDtypeStruct((n_mod_blocks, 8, D), F32),
        jax.ShapeDtypeStruct((8, D), F32)]
    out_specs = ([row(D)] if latent else []) + [row(n_cols), pl.BlockSpec((1, 8, D), mod_idx), _full((8, D))]
    return pl.pallas_call(
        body, name=name, grid=(rows // tm,), out_shape=out_shape, in_specs=in_specs, out_specs=out_specs,
        compiler_params=_params(("arbitrary",)),
    )(*args)


def _grad_matmul(a, b, name, init=None, tn=512, tt=1024):
    rows, n = a.shape
    k = b.shape[1]
    tn = min(tn, n)
    tt = min(tt, rows)
    steps = rows // tt
    has_init = init is not None

    def body(*refs):
        if has_init:
            a_ref, b_ref, init_ref, o_ref, acc = refs
        else:
            a_ref, b_ref, o_ref, acc = refs
        t = pl.program_id(1)

        @pl.when(t == 0)
        def _():
            acc[...] = init_ref[...].astype(F32) if has_init else jnp.zeros_like(acc)

        acc[...] += _dot(a_ref[...], b_ref[...], "tn")

        @pl.when(t == steps - 1)
        def _():
            o_ref[...] = acc[...].astype(o_ref.dtype)

    in_specs = [pl.BlockSpec((tt, tn), lambda i, t: (t, i)), pl.BlockSpec((tt, k), lambda i, t: (t, 0))]
    args = [a, b]
    if has_init:
        in_specs.append(pl.BlockSpec((tn, k), lambda i, t: (i, 0)))
        args.append(init)
    return pl.pallas_call(
        body, name=name, grid=(n // tn, steps), out_shape=jax.ShapeDtypeStruct((n, k), PAYLOAD_DTYPE),
        in_specs=in_specs, out_specs=pl.BlockSpec((tn, k), lambda i, t: (i, 0)),
        scratch_shapes=[pltpu.VMEM((tn, k), F32)],
        compiler_params=_params(("arbitrary", "arbitrary")),
    )(*args)


def _row_tile(rows, limit=256):
    if rows <= limit:
        return rows
    for t in range(limit, 7, -8):
        if rows % t == 0:
            return t
    return rows


def _sum8(stack, name):
    _, rows, cols = stack.shape
    tr = _row_tile(rows)

    def body(s_ref, o_ref):
        acc = s_ref[0].astype(F32)
        for j in range(1, N_DEV):
            acc = acc + s_ref[j].astype(F32)
        o_ref[...] = acc

    return pl.pallas_call(
        body, name=name, grid=(rows // tr,), out_shape=jax.ShapeDtypeStruct((rows, cols), F32),
        in_specs=[pl.BlockSpec((N_DEV, tr, cols), lambda i: (0, i, 0))],
        out_specs=pl.BlockSpec((tr, cols), lambda i: (i, 0)),
        compiler_params=_params(("arbitrary",)),
    )(stack)


def _small_reduce(stack, nb_ex):
    def body(s_ref, o_ref, bm_ref):
        acc = s_ref[0]
        for j in range(1, N_DEV):
            acc = acc + s_ref[j]
        o_ref[...] = acc
        bm = acc[8:8 + N_MOD, :]
        for e in range(nb_ex):
            bm = bm + acc[16 + e * N_MOD:16 + (e + 1) * N_MOD, :]
        bm_ref[...] = jnp.concatenate([bm, jnp.zeros((8 - N_MOD, D), F32)], axis=0)

    return pl.pallas_call(
        body, name="small_reduce", grid=(1,),
        out_shape=(jax.ShapeDtypeStruct((SMALL_ROWS, D), F32), jax.ShapeDtypeStruct((8, D), F32)),
        in_specs=[_full((N_DEV, SMALL_ROWS, D))], out_specs=(_full((SMALL_ROWS, D)), _full((8, D))),
        compiler_params=_params(("arbitrary",)),
    )(stack)


def _adamw(w, g, m, v, name):
    shape = w.shape
    cols = shape[-1]
    rows = 1
    for s in shape[:-1]:
        rows *= s
    tr = _row_tile(rows)

    def body(w_ref, g_ref, m_ref, v_ref, d_ref, nm_ref, nv_ref):
        gv = g_ref[...]
        nm = ADAM_B1 * m_ref[...] + (1.0 - ADAM_B1) * gv
        nv = ADAM_B2 * v_ref[...] + (1.0 - ADAM_B2) * (gv * gv)
        m_hat = nm / (1.0 - ADAM_B1 ** ADAM_STEP)
        v_hat = nv / (1.0 - ADAM_B2 ** ADAM_STEP)
        d_ref[...] = -ADAM_LR * (m_hat / (jnp.sqrt(v_hat) + ADAM_EPS) + ADAM_WD * w_ref[...])
        nm_ref[...] = nm
        nv_ref[...] = nv

    blk = pl.BlockSpec((tr, cols), lambda i: (i, 0))
    sd = jax.ShapeDtypeStruct((rows, cols), F32)
    d, nm, nv = pl.pallas_call(
        body, name=name, grid=(rows // tr,), out_shape=(sd, sd, sd), in_specs=[blk] * 4, out_specs=(blk, blk, blk),
        compiler_params=_params(("arbitrary",)),
    )(w.reshape(rows, cols), g.reshape(rows, cols), m.reshape(rows, cols), v.reshape(rows, cols))
    return d.reshape(shape), nm.reshape(shape), nv.reshape(shape)


def _local_step(x, ctx, target, modv, mcv, gam, g_mix, g_ffn, gna, ln_g, ln_b, w_s, b_s, g_final,
                w_inT, w_upT, w_down, w_o, w_paT, w_pbT):
    nb_ex, seq, _ = x.shape
    ctx_len = ctx.shape[1]
    xt = x.reshape(nb_ex * seq, D)
    ct = ctx.reshape(nb_ex * ctx_len, D)
    tgt = target.reshape(nb_ex * seq, D)
    bs3 = b_s.reshape(GROUPS, SGU_BLOCK, 1)

    pc, hc = _inproj(ct, mcv, g_mix, w_inT, CTX_COLS, ctx_len, "inproj_ctx")
    p, h = _inproj(xt, modv, g_mix, w_inT, IN_COLS, seq, "inproj_lat")
    cst_f, cst_b, s_ctx = _hgrn_fwd(pc, gam, None, ctx_len, False, "hgrn_fwd_ctx")
    o_up, o_down, st_f, st_b, _ = _hgrn_fwd(p, gam, s_ctx, seq, True, "hgrn_fwd_lat")
    x1, mix, merged, oa, obm = _tail_fwd(p, o_up, o_down, xt, modv, gna, ln_g, ln_b, w_s, bs3, w_paT, w_pbT, w_o, seq)
    dx1, h2, dffn, act, dup, dmod_ffn, small_ffn = _ffn(x1, tgt, modv, g_ffn, g_final, w_upT, w_down, seq)
    dpt, do, dmix, dpa, dpb, dmod_tail, small_tail, dws, dbs = _tail_bwd(
        p, o_up, o_down, dx1, mix, modv, gna, ln_g, ln_b, w_s, bs3, w_paT, w_pbT, w_o, seq)
    dzf, dvf, dqf, dzb, dvb, dqb, dlb, ds0 = _hgrn_bwd(p, gam, do, st_f, st_b, None, seq, True, "hgrn_bwd_lat")
    czf, cvf, czb, cvb, dlb_c, _ = _hgrn_bwd(pc, gam, None, cst_f, cst_b, ds0, ctx_len, False, "hgrn_bwd_ctx")
    grad_x, dp, dmod_in, small_in = _inproj_bwd([dzf, dzb, dvf, dvb, dqf, dqb], dpt, xt, dx1, modv, g_mix, w_inT,
                                                 seq, "inproj_bwd_lat")
    dpc, dmc, small_c = _inproj_bwd([czf, czb, cvf, cvb], None, ct, None, mcv, g_mix, w_inT, ctx_len, "inproj_bwd_ctx")
    g_ctx = _grad_matmul(dpc, hc, "gw_in_ctx")
    g_ctx = jnp.pad(g_ctx, ((0, IN_COLS - CTX_COLS), (0, 0)))
    gw_inT = _grad_matmul(dp, h, "gw_in", init=g_ctx)
    gw_upT = _grad_matmul(dup, h2, "gw_up")
    gw_down = _grad_matmul(act, dffn, "gw_down", tn=256)
    gw_o = _grad_matmul(merged, dmix, "gw_o")
    gw_paT = _grad_matmul(dpa, oa, "gw_pa")
    gw_pbT = _grad_matmul(dpb, obm, "gw_pb")

    z = lambda r: jnp.zeros((r, D), F32)
    pad = lambda a: jnp.pad(a, ((0, 0), (0, D - a.shape[1])))
    dlb_row = (dlb + dlb_c).reshape(1, 2 * KW)
    dmod = dmod_in + dmod_tail + dmod_ffn
    small = jnp.concatenate([
        small_in[0:1] + small_c[0:1],
        small_ffn[0:1],
        small_ffn[1:2],
        small_tail[0:1],
        small_tail[1:2],
        small_tail[2:3],
        pad(dbs.reshape(1, GROUPS * SGU_BLOCK)),
        dlb_row,
        dmc[0, 0:N_MOD],
        z(2),
        dmod[:, 0:N_MOD].reshape(nb_ex * N_MOD, D),
        z(24 - nb_ex * N_MOD),
        dws.reshape(GROUPS * SGU_BLOCK * SGU_BLOCK // D, D),
    ], axis=0)
    loss = small_ffn[2, 0]
    return loss, grad_x.reshape(x.shape), (gw_inT, gw_upT, gw_down, gw_o, gw_paT, gw_pbT), small


def kernel(x, c, ctx, c_ctx, w_mod, b_mod, g_mix, g_ffn, w_in, lb_gamma, g_norm_a, ln_v_g, ln_v_b, w_s, b_s, w_pa, w_pb, w_o, w_up, w_down, g_final, loss_target, m_c_ctx, m_w_mod, m_b_mod, m_g_mix, m_g_ffn, m_w_in, m_lb_gamma, m_g_norm_a, m_ln_v_g, m_ln_v_b, m_w_s, m_b_s, m_w_pa, m_w_pb, m_w_o, m_w_up, m_w_down, m_g_final, v_c_ctx, v_w_mod, v_b_mod, v_g_mix, v_g_ffn, v_w_in, v_lb_gamma, v_g_norm_a, v_ln_v_g, v_ln_v_b, v_w_s, v_b_s, v_w_pa, v_w_pb, v_w_o, v_w_up, v_w_down, v_g_final):
    nb_ex = x.shape[0]
    me = 4 * lax.axis_index("x") + 2 * lax.axis_index("y") + lax.axis_index("c")
    cd = MXU_DTYPE
    mod_cols = w_mod.shape[2]
    lb_cols = lb_gamma.shape[2]

    w_inT_l = w_in[0].T.astype(cd)
    w_upT_l = w_up[0].T.astype(cd)
    w_paT_l = w_pa[0].T.astype(cd)
    w_pbT_l = w_pb[0].T.astype(cd)
    cl = jnp.concatenate([c, jnp.pad(lb_gamma.reshape(1, 4 * lb_cols), ((0, 0), (0, D - 4 * lb_cols))),
                          jnp.zeros((8 - nb_ex - 1, D), F32)], axis=0)
    g_in, g_up, g_down, g_o, g_pa, g_pb, g_cl = _exchange(
        [(w_inT_l, "gather"), (w_upT_l, "gather"), (w_down[0].astype(cd), "gather"), (w_o[0].astype(cd), "gather"),
         (w_paT_l, "gather"), (w_pbT_l, "gather"), (cl, "gather")], "gather_weights")
    w_inT = g_in.reshape(IN_COLS, D)
    w_upT = g_up.reshape(2 * D_FF, D)
    w_down_f = g_down.reshape(D_FF, D)
    w_o_f = g_o.reshape(D, D)
    w_paT = g_pa.reshape(D, KW)
    w_pbT = g_pb.reshape(D, KW)
    c_all = g_cl[:, 0:nb_ex].reshape(N_DEV * nb_ex, D)
    gam = jnp.transpose(g_cl[:, nb_ex, 0:4 * lb_cols].reshape(N_DEV, 4, lb_cols), (1, 0, 2)).reshape(4, KW)

    n_c = N_DEV * nb_ex
    cvec = jnp.concatenate([c_all, c_ctx.reshape(1, D), jnp.zeros((7, D), F32)], axis=0)
    b_mod_l = lax.dynamic_slice(b_mod, (0, me * mod_cols), (1, mod_cols))
    mod_l, svec = _mod_fwd(cvec, w_mod[0], b_mod_l)
    (g_mod,) = _exchange([(mod_l, "gather")], "gather_mod")
    mod_all = jnp.transpose(g_mod, (1, 0, 2)).reshape(n_c + 8, N_MOD * D)
    modv = lax.dynamic_slice(mod_all, (me * nb_ex, 0), (nb_ex, N_MOD * D)).reshape(nb_ex, N_MOD, D)
    mcv = mod_all[n_c].reshape(1, N_MOD, D)

    loss_l, grad_x, partial, small = _local_step(
        x, ctx, loss_target, modv, mcv, gam, g_mix, g_ffn, g_norm_a, ln_v_g, ln_v_b, w_s[0], b_s[0],
        g_final.reshape(1, D), w_inT, w_upT, w_down_f, w_o_f, w_paT, w_pbT)
    loss = lax.psum(loss_l, ("x", "y", "c"))

    gw_inT, gw_upT, gw_down, gw_o, gw_paT, gw_pbT = partial
    blocks = lambda a: a.reshape(N_DEV, a.shape[0] // N_DEV, a.shape[1])
    r_in, r_up, r_down, r_o, r_pa, r_pb, r_small = _exchange(
        [(blocks(gw_inT), "scatter"), (blocks(gw_upT), "scatter"), (blocks(gw_down), "scatter"),
         (blocks(gw_o), "scatter"), (blocks(gw_paT), "scatter"), (blocks(gw_pbT), "scatter"), (small, "gather")],
        "scatter_grads")
    grad_w_in = _sum8(r_in, "sum_w_in").T[None]
    grad_w_up = _sum8(r_up, "sum_w_up").T[None]
    grad_w_down = _sum8(r_down, "sum_w_down")[None]
    grad_w_o = _sum8(r_o, "sum_w_o")[None]
    grad_w_pa = _sum8(r_pa, "sum_w_pa").T[None]
    grad_w_pb = _sum8(r_pb, "sum_w_pb").T[None]
    tot, bm = _small_reduce(r_small, nb_ex)
    grad_g_mix, grad_g_ffn, grad_g_final = tot[0:1], tot[1:2], tot[2]
    grad_g_norm_a = tot[3:4, 0:DK]
    grad_ln_v_g, grad_ln_v_b = tot[4:5, 0:KW], tot[5:6, 0:KW]
    grad_b_s = tot[6, 0:GROUPS * SGU_BLOCK].reshape(1, GROUPS, SGU_BLOCK)
    grad_w_s = tot[40:104].reshape(1, GROUPS, SGU_BLOCK, SGU_BLOCK)
    grad_b_mod = bm[0:N_MOD].reshape(1, N_MOD * D)
    lb_l = jax.nn.sigmoid(lb_gamma[0] - lb_gamma[1])
    dlb_l = lax.dynamic_slice(tot[7].reshape(2, KW), (0, me * lb_cols), (2, lb_cols))
    dgam0 = dlb_l * lb_l * (1.0 - lb_l)
    grad_lb_gamma = jnp.stack([dgam0, -dgam0], axis=0)

    dmod_all = r_small[:, 16:16 + nb_ex * N_MOD].reshape(n_c, N_MOD * D)
    dmod_l = jnp.concatenate([lax.dynamic_slice(dmod_all, (0, me * mod_cols), (n_c, mod_cols)),
                              lax.dynamic_slice(tot[8:8 + N_MOD].reshape(1, N_MOD * D), (0, me * mod_cols), (1, mod_cols)),
                              jnp.zeros((7, mod_cols), F32)], axis=0)
    gw_mod, gc = _mod_bwd(svec, cvec, dmod_l, w_mod[0])
    grad_w_mod = gw_mod[None]
    (r_gc,) = _exchange([(gc[n_c:n_c + 8], "gather")], "gather_c_ctx")
    grad_c_ctx = _sum8(r_gc, "sum_c_ctx")[0]

    names = ["c_ctx", "w_mod", "b_mod", "g_mix", "g_ffn", "w_in", "lb_gamma", "g_norm_a", "ln_v_g", "ln_v_b", "w_s",
             "b_s", "w_pa", "w_pb", "w_o", "w_up", "w_down", "g_final"]
    weights = [c_ctx, w_mod, b_mod, g_mix, g_ffn, w_in, lb_gamma, g_norm_a, ln_v_g, ln_v_b, w_s, b_s, w_pa, w_pb, w_o,
               w_up, w_down, g_final]
    grads = [grad_c_ctx, grad_w_mod, grad_b_mod, grad_g_mix, grad_g_ffn, grad_w_in, grad_lb_gamma, grad_g_norm_a,
             grad_ln_v_g, grad_ln_v_b, grad_w_s, grad_b_s, grad_w_pa, grad_w_pb, grad_w_o, grad_w_up, grad_w_down,
             grad_g_final]
    ms = [m_c_ctx, m_w_mod, m_b_mod, m_g_mix, m_g_ffn, m_w_in, m_lb_gamma, m_g_norm_a, m_ln_v_g, m_ln_v_b, m_w_s, m_b_s,
          m_w_pa, m_w_pb, m_w_o, m_w_up, m_w_down, m_g_final]
    vs = [v_c_ctx, v_w_mod, v_b_mod, v_g_mix, v_g_ffn, v_w_in, v_lb_gamma, v_g_norm_a, v_ln_v_g, v_ln_v_b, v_w_s, v_b_s,
          v_w_pa, v_w_pb, v_w_o, v_w_up, v_w_down, v_g_final]
    deltas, new_ms, new_vs = [], [], []
    for nm, w, g, m, v in zip(names, weights, grads, ms, vs):
        d, nm_, nv_ = _adamw(w, g.reshape(w.shape), m, v, "adamw_" + nm)
        deltas.append(d)
        new_ms.append(nm_)
        new_vs.append(nv_)
    grads = [g.reshape(w.shape) for g, w in zip(grads, weights)]
    return (loss, grad_x, *grads, *deltas, *new_ms, *new_vs)
```

```python
import functools

import jax
import jax.numpy as jnp
from jax import lax
from jax.experimental import pallas as pl
from jax.experimental.pallas import tpu as pltpu

F32 = jnp.float32
MXU_DTYPE = jnp.bfloat16
PAYLOAD_DTYPE = jnp.bfloat16

N_DEV = 8
D = 1024
HEADS = 4
DK = 128
KW = HEADS * DK
CHUNK = 64
SGU_BLOCK = 128
GROUPS = 4
D_FF = 2816
FF_CHUNK = 256
N_MOD = 6
IN_COLS = 5632
CTX_COLS = 1536
TAIL_COLS = IN_COLS - 4 * KW
EPS = 1e-6
ADAM_LR, ADAM_B1, ADAM_B2, ADAM_EPS, ADAM_WD, ADAM_STEP = 0.001, 0.9, 0.999, 1e-08, 0.01, 10

VMEM_LIMIT = 56 * 1024 * 1024
TOKEN_TILE = 256
SMALL_ROWS = 104


def _params(sem):
    return pltpu.CompilerParams(dimension_semantics=sem, vmem_limit_bytes=VMEM_LIMIT)


_DN = {"nn": (((1,), (0,)), ((), ())), "nt": (((1,), (1,)), ((), ())), "tn": (((0,), (0,)), ((), ()))}


def _dot(a, b, form="nn"):
    return lax.dot_general(a.astype(MXU_DTYPE), b.astype(MXU_DTYPE), _DN[form], preferred_element_type=F32)


def _dotx(a, b, form="nn"):
    return lax.dot_general(a.astype(F32), b.astype(F32), _DN[form], preferred_element_type=F32,
                           precision=lax.Precision.HIGHEST)


def _full(shape, single=False):
    n = len(shape)
    if single:
        return pl.BlockSpec(shape, lambda *_: (0,) * n, pipeline_mode=pl.Buffered(1))
    return pl.BlockSpec(shape, lambda *_: (0,) * n)


def _sigmoid(z):
    return 1.0 / (1.0 + jnp.exp(-z))


def _gelu(x):
    c = 0.7978845608028654
    t = jnp.tanh(c * (x + 0.044715 * x * x * x))
    return 0.5 * x * (1.0 + t), t


def _gelu_grad(x, t):
    c = 0.7978845608028654
    return 0.5 * (1.0 + t) + 0.5 * x * (1.0 - t * t) * c * (1.0 + 3 * 0.044715 * x * x)


def _exchange(items, name):
    n = len(items)
    out_shape = []
    for a, mode in items:
        blk = a.shape if mode == "gather" else a.shape[1:]
        out_shape.append(jax.ShapeDtypeStruct((N_DEV,) + tuple(blk), a.dtype))

    def body(*refs):
        srcs, dsts = refs[:n], refs[n:2 * n]
        send_sems, recv_sems, local_sems = refs[2 * n:]
        x, y, c = lax.axis_index("x"), lax.axis_index("y"), lax.axis_index("c")
        me = 4 * x + 2 * y + c

        def src_for(i, dev):
            return srcs[i] if items[i][1] == "gather" else srcs[i].at[dev]

        local = [pltpu.make_async_copy(src_for(i, me), dsts[i].at[me], local_sems.at[i]) for i in range(n)]
        for cp in local:
            cp.start()
        remote = []
        for k in range(1, N_DEV):
            px = jnp.bitwise_xor(x, (k >> 2) & 1)
            py = jnp.bitwise_xor(y, (k >> 1) & 1)
            pc = jnp.bitwise_xor(c, k & 1)
            peer = 4 * px + 2 * py + pc
            for i in range(n):
                cp = pltpu.make_async_remote_copy(
                    src_ref=src_for(i, peer), dst_ref=dsts[i].at[me],
                    send_sem=send_sems.at[i * (N_DEV - 1) + k - 1], recv_sem=recv_sems.at[i * (N_DEV - 1) + k - 1],
                    device_id=(px, py, pc), device_id_type=pl.DeviceIdType.MESH)
                cp.start()
                remote.append(cp)
        for cp in remote:
            cp.wait()
        for cp in local:
            cp.wait()

    any_spec = pl.BlockSpec(memory_space=pl.ANY)
    return pl.pallas_call(
        body, name=name, out_shape=out_shape,
        in_specs=[any_spec] * n, out_specs=[any_spec] * n,
        scratch_shapes=[pltpu.SemaphoreType.DMA((n * (N_DEV - 1),)), pltpu.SemaphoreType.DMA((n * (N_DEV - 1),)),
                        pltpu.SemaphoreType.DMA((n,))],
    )(*[a for a, _ in items])


_HBM = pl.BlockSpec(memory_space=pltpu.HBM)
_SEM = pl.BlockSpec(memory_space=pltpu.SEMAPHORE)
_EFFECT = pltpu.SideEffectType.DATAFLOW_SIDE_EFFECTING


def _split_copies(items, srcs, lands, send_sems, recv_sems):
    x, y, c = lax.axis_index("x"), lax.axis_index("y"), lax.axis_index("c")
    me = 4 * x + 2 * y + c
    copies = []
    for k in range(1, N_DEV):
        px = jnp.bitwise_xor(x, (k >> 2) & 1)
        py = jnp.bitwise_xor(y, (k >> 1) & 1)
        pc = jnp.bitwise_xor(c, k & 1)
        peer = 4 * px + 2 * py + pc
        for i in range(len(items)):
            src = srcs[i] if items[i][1] == "gather" else srcs[i].at[peer]
            copies.append(pltpu.make_async_remote_copy(
                src_ref=src, dst_ref=lands[i].at[me],
                send_sem=send_sems.at[i * (N_DEV - 1) + k - 1], recv_sem=recv_sems.at[i * (N_DEV - 1) + k - 1],
                device_id=(px, py, pc), device_id_type=pl.DeviceIdType.MESH))
    return me, copies


def _exchange_start(items, name):
    n = len(items)
    n_sem = n * (N_DEV - 1)
    srcs, lands = [], []
    for a, mode in items:
        blk = a.shape if mode == "gather" else a.shape[1:]
        srcs.append(pltpu.with_memory_space_constraint(a, pltpu.HBM))
        lands.append(pltpu.with_memory_space_constraint(lax.empty((N_DEV,) + tuple(blk), a.dtype), pltpu.HBM))

    def body(*refs):
        src_refs, land_refs = refs[:n], refs[n:2 * n]
        send_sems, recv_sems = refs[2 * n], refs[2 * n + 1]
        token = refs[4 * n + 2]
        local_sems = refs[4 * n + 3]
        me, copies = _split_copies(items, src_refs, land_refs, send_sems, recv_sems)
        for cp in copies:
            cp.start()
        for i in range(n):
            own = src_refs[i] if items[i][1] == "gather" else src_refs[i].at[me]
            cp = pltpu.make_async_copy(own, land_refs[i].at[me], local_sems.at[i])
            cp.start()
            cp.wait()
        token[...] = jnp.zeros_like(token)

    out_shape = [pltpu.SemaphoreType.DMA((n_sem,)), pltpu.SemaphoreType.DMA((n_sem,))]
    out_shape += [pltpu.HBM(a.shape, a.dtype) for a in srcs] + [pltpu.HBM(a.shape, a.dtype) for a in lands]
    out_shape.append(jax.ShapeDtypeStruct((8, 128), F32))
    outs = pl.pallas_call(
        body, name=name, out_shape=out_shape,
        in_specs=[_HBM] * (2 * n), out_specs=[_SEM, _SEM] + [_HBM] * (2 * n) + [pl.BlockSpec(memory_space=pltpu.VMEM)],
        input_output_aliases={i: 2 + i for i in range(2 * n)},
        scratch_shapes=[pltpu.SemaphoreType.DMA((n,))],
        compiler_params=pltpu.CompilerParams(has_side_effects=_EFFECT),
    )(*srcs, *lands)
    handle = (items, name, outs[0], outs[1], outs[2:2 + n], outs[2 + n:2 + 2 * n])
    return handle, outs[2 + 2 * n]


def _exchange_wait(handle, after):
    items, name, send_sems, recv_sems, srcs, lands = handle
    n = len(items)

    def body(*refs):
        src_refs, land_refs = refs[:n], refs[n:2 * n]
        send_ref, recv_ref = refs[2 * n], refs[2 * n + 1]
        _, copies = _split_copies(items, src_refs, land_refs, send_ref, recv_ref)
        for cp in copies:
            cp.wait_send()
            cp.wait_recv()

    outs = pl.pallas_call(
        body, name=name + "_wait",
        out_shape=[pltpu.HBM(a.shape, a.dtype) for a in srcs] + [pltpu.HBM(a.shape, a.dtype) for a in lands],
        in_specs=[_HBM] * (2 * n) + [_SEM, _SEM, pl.BlockSpec(memory_space=pl.ANY)], out_specs=[_HBM] * (2 * n),
        input_output_aliases={i: i for i in range(2 * n)},
        compiler_params=pltpu.CompilerParams(has_side_effects=_EFFECT),
    )(*srcs, *lands, send_sems, recv_sems, after)
    return outs[n:]


def _mod_fwd(cvec, w_mod_l, b_mod_l):
    rows, cols = cvec.shape[0], w_mod_l.shape[1]

    def body(c_ref, w_ref, b_ref, o_ref, s_ref):
        cv = c_ref[...]
        s = cv * _sigmoid(cv)
        s_ref[...] = s
        o_ref[...] = _dot(s, w_ref[...]) + b_ref[...]

    return pl.pallas_call(
        body, name="mod_fwd",
        out_shape=(jax.ShapeDtypeStruct((rows, cols), F32), jax.ShapeDtypeStruct((rows, D), F32)),
        in_specs=[_full((rows, D)), _full((D, cols)), _full((1, cols))],
        out_specs=(_full((rows, cols)), _full((rows, D))), grid=(1,),
        compiler_params=_params(("arbitrary",)),
    )(cvec, w_mod_l, b_mod_l)


def _mod_bwd(svec, cvec, dmod_l, w_mod_l):
    rows, cols = dmod_l.shape

    def body(s_ref, c_ref, d_ref, w_ref, gw_ref, gc_ref):
        gw_ref[...] = _dot(s_ref[...], d_ref[...], "tn")
        cv = c_ref[...]
        sg = _sigmoid(cv)
        gc_ref[...] = _dot(d_ref[...], w_ref[...], "nt") * (sg * (1.0 + cv * (1.0 - sg)))

    return pl.pallas_call(
        body, name="mod_bwd",
        out_shape=(jax.ShapeDtypeStruct((D, cols), F32), jax.ShapeDtypeStruct((rows, D), F32)),
        in_specs=[_full((rows, D)), _full((rows, D)), _full((rows, cols)), _full((D, cols))],
        out_specs=(_full((D, cols)), _full((rows, D))), grid=(1,),
        compiler_params=_params(("arbitrary",)),
    )(svec, cvec, dmod_l, w_mod_l)


def _inproj(xt, modv, g, w_inT, n_cols, rows_per_example, name):
    rows = xt.shape[0]
    tm = min(TOKEN_TILE, rows_per_example)
    per_b = rows_per_example // tm
    shared_mod = modv.shape[0] == 1

    def body(x_ref, mod_ref, g_ref, w_ref, p_ref, h_ref):
        x = x_ref[...]
        r = lax.rsqrt(jnp.mean(x * x, axis=-1, keepdims=True) + EPS)
        h = (x * r * g_ref[...]) * (1.0 + mod_ref[0, 1:2, :]) + mod_ref[0, 0:1, :]
        hb = h.astype(MXU_DTYPE)
        h_ref[...] = hb
        for j in range(n_cols // KW):
            p_ref[:, j * KW:(j + 1) * KW] = _dot(hb, w_ref[j * KW:(j + 1) * KW, :], "nt").astype(p_ref.dtype)

    mod_idx = (lambda i: (0, 0, 0)) if shared_mod else (lambda i: (i // per_b, 0, 0))
    return pl.pallas_call(
        body, name=name,
        out_shape=(jax.ShapeDtypeStruct((rows, n_cols), MXU_DTYPE), jax.ShapeDtypeStruct((rows, D), MXU_DTYPE)),
        grid=(rows // tm,),
        in_specs=[pl.BlockSpec((tm, D), lambda i: (i, 0)), pl.BlockSpec((1, N_MOD, D), mod_idx), _full((1, D)),
                  pl.BlockSpec((n_cols, D), lambda i: (0, 0), pipeline_mode=pl.Buffered(1))],
        out_specs=(pl.BlockSpec((tm, n_cols), lambda i: (i, 0)), pl.BlockSpec((tm, D), lambda i: (i, 0))),
        compiler_params=_params(("arbitrary",)),
    )(xt, modv, g, w_inT)


def _tri(reverse):
    row = lax.broadcasted_iota(jnp.int32, (CHUNK, CHUNK), 0)
    col = lax.broadcasted_iota(jnp.int32, (CHUNK, CHUNK), 1)
    return (col >= row) if reverse else (col <= row)


def _lower_bound(gam_ref, direction):
    return _sigmoid(gam_ref[direction:direction + 1, :] - gam_ref[2 + direction:3 + direction, :])


def _gate_prep(z, lb, tri_f):
    sg = _sigmoid(z)
    f = lb + (1.0 - lb) * sg
    g = jnp.log(f)
    b = _dotx(tri_f, g)
    bl = jnp.sum(g, axis=0, keepdims=True)
    return sg, f, 1.0 - f, b, bl


def _hgrn_fwd(p, gam, s0, rows_per_example, with_out, name):
    rows = p.shape[0]
    nb_ex = rows // rows_per_example
    rb = min(TOKEN_TILE, rows_per_example)
    cpb = rb // CHUNK
    nb = rows_per_example // rb
    n_chunks = rows // CHUNK
    has_s0 = s0 is not None

    def body(*refs):
        it = iter(refs)
        gam_ref = next(it)
        zf_ref, vf_ref = next(it), next(it)
        qf_ref = next(it) if with_out else None
        zb_ref, vb_ref = next(it), next(it)
        qb_ref = next(it) if with_out else None
        s0_ref = next(it) if has_s0 else None
        if with_out:
            of_ref, ob_ref = next(it), next(it)
        stash_f, stash_b, fin_ref = next(it), next(it), next(it)
        st_ref = next(it)
        i = pl.program_id(1)

        @pl.when(i == 0)
        def _():
            if has_s0:
                st_ref[...] = s0_ref[:, 0]
            else:
                st_ref[...] = jnp.zeros_like(st_ref)

        for direction, (z_ref, v_ref, q_ref, stash) in enumerate(
                ((zf_ref, vf_ref, qf_ref, stash_f), (zb_ref, vb_ref, qb_ref, stash_b))):
            reverse = direction == 1
            tri = _tri(reverse)
            tri_f = tri.astype(F32)
            lb = _lower_bound(gam_ref, direction)
            order = range(cpb - 1, -1, -1) if reverse else range(cpb)
            for j in order:
                rs = slice(j * CHUNK, (j + 1) * CHUNK)
                z = z_ref[rs, :].astype(F32)
                v = v_ref[rs, :].astype(F32)
                _, _, k, b, bl = _gate_prep(z, lb, tri_f)
                mid = 0.5 * bl
                kd = k * jnp.exp(bl - b)
                a = jnp.exp(bl)
                if with_out:
                    q = q_ref[rs, :].astype(F32)
                    qi = q * jnp.exp(b - mid)
                    ki = k * jnp.exp(mid - b)
                    qe = q * jnp.exp(b)
                for h in range(HEADS):
                    hs = slice(h * DK, (h + 1) * DK)
                    st = st_ref[direction, h]
                    stash[j, h] = st.astype(stash.dtype)
                    if with_out:
                        sc = jnp.where(tri, _dot(qi[:, hs], ki[:, hs], "nt"), 0.0)
                        o = _dot(sc, v[:, hs]) + _dot(qe[:, hs], st, "nt")
                        (ob_ref if reverse else of_ref)[rs, hs] = o
                    st_ref[direction, h] = st * a[:, hs] + _dot(v[:, hs], kd[:, hs], "tn")

        @pl.when(i == nb - 1)
        def _():
            fin_ref[:, 0] = st_ref[...]

    up = lambda b, i: b * nb + i
    down = lambda b, i: b * nb + nb - 1 - i
    col = lambda rowf, c: pl.BlockSpec((rb, KW), lambda b, i: (rowf(b, i), c))
    in_specs = [_full((4, KW)), col(up, 0), col(up, 2)] + ([col(up, 3)] if with_out else [])
    in_specs += [col(down, 1), col(down, 2)] + ([col(down, 3)] if with_out else [])
    args = [gam, p, p] + ([p] if with_out else []) + [p, p] + ([p] if with_out else [])
    if has_s0:
        in_specs.append(pl.BlockSpec((2, 1, HEADS, DK, DK), lambda b, i: (0, b, 0, 0, 0)))
        args.append(s0)
    out_shape, out_specs = [], []
    if with_out:
        out_shape += [jax.ShapeDtypeStruct((rows, KW), F32)] * 2
        out_specs += [pl.BlockSpec((rb, KW), lambda b, i: (up(b, i), 0)),
                      pl.BlockSpec((rb, KW), lambda b, i: (down(b, i), 0))]
    out_shape += [jax.ShapeDtypeStruct((n_chunks, HEADS, DK, DK), MXU_DTYPE)] * 2
    out_specs += [pl.BlockSpec((cpb, HEADS, DK, DK), lambda b, i: (up(b, i), 0, 0, 0)),
                  pl.BlockSpec((cpb, HEADS, DK, DK), lambda b, i: (down(b, i), 0, 0, 0))]
    out_shape.append(jax.ShapeDtypeStruct((2, nb_ex, HEADS, DK, DK), F32))
    out_specs.append(pl.BlockSpec((2, 1, HEADS, DK, DK), lambda b, i: (0, b, 0, 0, 0)))
    return pl.pallas_call(
        body, name=name, out_shape=out_shape, grid=(nb_ex, nb), in_specs=in_specs, out_specs=out_specs,
        scratch_shapes=[pltpu.VMEM((2, HEADS, DK, DK), F32)],
        compiler_params=_params(("arbitrary", "arbitrary")),
    )(*args)


def _hgrn_bwd(p, gam, do, stash_f, stash_b, ds_end, rows_per_example, with_out, name):
    rows = p.shape[0]
    nb_ex = rows // rows_per_example
    rb = min(TOKEN_TILE, rows_per_example)
    cpb = rb // CHUNK
    nb = rows_per_example // rb
    has_end = ds_end is not None

    def body(*refs):
        it = iter(refs)
        gam_ref = next(it)
        ins = []
        for _ in range(2):
            z_ref, v_ref = next(it), next(it)
            q_ref = next(it) if with_out else None
            do_ref = next(it) if with_out else None
            ins.append((z_ref, v_ref, q_ref, do_ref, next(it)))
        end_ref = next(it) if has_end else None
        outs = []
        for _ in range(2):
            dz_ref, dv_ref = next(it), next(it)
            dq_ref = next(it) if with_out else None
            outs.append((dz_ref, dv_ref, dq_ref))
        dlb_ref, ds0_ref = next(it), next(it)
        dst_ref = next(it)
        b_id, i = pl.program_id(0), pl.program_id(1)

        @pl.when(i == 0)
        def _():
            if has_end:
                dst_ref[...] = end_ref[:, 0]
            else:
                dst_ref[...] = jnp.zeros_like(dst_ref)

        @pl.when((i == 0) & (b_id == 0))
        def _():
            dlb_ref[...] = jnp.zeros_like(dlb_ref)

        for direction in range(2):
            z_ref, v_ref, q_ref, do_ref, stash = ins[direction]
            dz_ref, dv_ref, dq_ref = outs[direction]
            reverse = direction == 1
            tri = _tri(reverse)
            tri_f = tri.astype(F32)
            lb = _lower_bound(gam_ref, direction)
            order = range(cpb) if reverse else range(cpb - 1, -1, -1)
            dlb_acc = jnp.zeros((1, KW), F32)
            for j in order:
                rs = slice(j * CHUNK, (j + 1) * CHUNK)
                z = z_ref[rs, :].astype(F32)
                v = v_ref[rs, :].astype(F32)
                sg, f, k, b, bl = _gate_prep(z, lb, tri_f)
                mid = 0.5 * bl
                e3 = jnp.exp(bl - b)
                kd = k * e3
                a = jnp.exp(bl)
                if with_out:
                    q = q_ref[rs, :].astype(F32)
                    dout = do_ref[rs, :].astype(F32)
                    e1, e2, e4 = jnp.exp(b - mid), jnp.exp(mid - b), jnp.exp(b)
                    qi, ki, qe = q * e1, k * e2, q * e4
                dkd_p, dv_p, da_p, dqi_p, dki_p, dqe_p = [], [], [], [], [], []
                for h in range(HEADS):
                    hs = slice(h * DK, (h + 1) * DK)
                    st_in = stash[j, h]
                    dst = dst_ref[direction, h]
                    dkd_p.append(_dot(v[:, hs], dst))
                    dvh = _dot(kd[:, hs], dst, "nt")
                    da_p.append(jnp.sum(dst * st_in.astype(F32), axis=0, keepdims=True))
                    new_dst = dst * a[:, hs]
                    if with_out:
                        sc = jnp.where(tri, _dot(qi[:, hs], ki[:, hs], "nt"), 0.0)
                        dsc = jnp.where(tri, _dot(dout[:, hs], v[:, hs], "nt"), 0.0)
                        dqi_p.append(_dot(dsc, ki[:, hs]))
                        dki_p.append(_dot(dsc, qi[:, hs], "tn"))
                        dqe_p.append(_dot(dout[:, hs], st_in))
                        dvh = dvh + _dot(sc, dout[:, hs], "tn")
                        new_dst = new_dst + _dot(dout[:, hs], qe[:, hs], "tn")
                    dv_p.append(dvh)
                    dst_ref[direction, h] = new_dst
                cat = lambda parts: jnp.concatenate(parts, axis=1)
                dkd, da = cat(dkd_p), cat(da_p)
                dv_ref[rs, :] = cat(dv_p)
                t_kd = dkd * kd
                dk = dkd * e3
                db = -t_kd
                dbl = jnp.sum(t_kd, axis=0, keepdims=True) + da * a
                if with_out:
                    dqi, dki, dqe = cat(dqi_p), cat(dki_p), cat(dqe_p)
                    dq_ref[rs, :] = dqi * e1 + dqe * e4
                    dk = dk + dki * e2
                    t_qi, t_ki, t_qe = dqi * qi, dki * ki, dqe * qe
                    db = db + t_qi - t_ki + t_qe
                    dbl = dbl + 0.5 * jnp.sum(t_ki - t_qi, axis=0, keepdims=True)
                dg = _dotx(tri_f, db, "tn") + dbl
                df = dg / f - dk
                dz_ref[rs, :] = df * (1.0 - lb) * sg * (1.0 - sg)
                dlb_acc = dlb_acc + jnp.sum(df * (1.0 - sg), axis=0, keepdims=True)
            dlb_ref[direction:direction + 1, :] += dlb_acc

        @pl.when(i == nb - 1)
        def _():
            ds0_ref[:, 0] = dst_ref[...]

    rows_of = (lambda b, i: b * nb + nb - 1 - i, lambda b, i: b * nb + i)
    in_specs, args = [_full((4, KW))], [gam]
    for direction in range(2):
        rf = rows_of[direction]
        col = lambda c, rf=rf: pl.BlockSpec((rb, KW), lambda b, i: (rf(b, i), c))
        in_specs += [col(direction), col(2)]
        args += [p, p]
        if with_out:
            in_specs += [col(3), col(0)]
            args += [p, do]
        in_specs.append(pl.BlockSpec((cpb, HEADS, DK, DK), lambda b, i, rf=rf: (rf(b, i), 0, 0, 0)))
        args.append((stash_f, stash_b)[direction])
    if has_end:
        in_specs.append(pl.BlockSpec((2, 1, HEADS, DK, DK), lambda b, i: (0, b, 0, 0, 0)))
        args.append(ds_end)
    out_shape, out_specs = [], []
    for direction in range(2):
        rf = rows_of[direction]
        n_out = 3 if with_out else 2
        out_shape += [jax.ShapeDtypeStruct((rows, KW), F32)] * n_out
        out_specs += [pl.BlockSpec((rb, KW), lambda b, i, rf=rf: (rf(b, i), 0))] * n_out
    out_shape += [jax.ShapeDtypeStruct((2, KW), F32), jax.ShapeDtypeStruct((2, nb_ex, HEADS, DK, DK), F32)]
    out_specs += [_full((2, KW)), pl.BlockSpec((2, 1, HEADS, DK, DK), lambda b, i: (0, b, 0, 0, 0))]
    return pl.pallas_call(
        body, name=name, out_shape=out_shape, grid=(nb_ex, nb), in_specs=in_specs, out_specs=out_specs,
        scratch_shapes=[pltpu.VMEM((2, HEADS, DK, DK), F32)],
        compiler_params=_params(("arbitrary", "arbitrary")),
    )(*args)


def _tail_forward(osum, og, u, v, ga, gb, gna, ln_g, ln_b, ws_ref, bs_ref, wpaT_ref, wpbT_ref):
    tm = osum.shape[0]
    gna4 = jnp.concatenate([gna] * HEADS, axis=1)
    r_parts = []
    for h in range(HEADS):
        oh = osum[:, h * DK:(h + 1) * DK]
        r_parts.append(jnp.broadcast_to(lax.rsqrt(jnp.mean(oh * oh, axis=-1, keepdims=True) + EPS), (tm, DK)))
    r = jnp.concatenate(r_parts, axis=1)
    on = osum * r
    sg_og = _sigmoid(og)
    silu_og = og * sg_og
    oan = on * gna4
    oa = oan * silu_og
    ug, tu = _gelu(u)
    vg, tv = _gelu(v)
    mu = jnp.mean(vg, axis=-1, keepdims=True)
    vc = vg - mu
    rstd = lax.rsqrt(jnp.mean(vc * vc, axis=-1, keepdims=True) + EPS)
    vhat = vc * rstd
    vln = vhat * ln_g + ln_b
    blocks = []
    for n in range(tm // SGU_BLOCK):
        rs = slice(n * SGU_BLOCK, (n + 1) * SGU_BLOCK)
        blocks.append(jnp.concatenate(
            [_dot(ws_ref[g], vln[rs, g * DK:(g + 1) * DK]) + bs_ref[g] for g in range(GROUPS)], axis=1))
    mixed = jnp.concatenate(blocks, axis=0) if len(blocks) > 1 else blocks[0]
    obm = ug * mixed
    pa = _dot(oa, wpaT_ref[...], "nt")
    pb = _dot(obm, wpbT_ref[...], "nt")
    sga, sgb = _sigmoid(ga), _sigmoid(gb)
    merged = sga * pa + sgb * pb
    return dict(r=r, on=on, sg_og=sg_og, silu_og=silu_og, oan=oan, oa=oa, ug=ug, tu=tu, tv=tv, rstd=rstd, vhat=vhat,
                vln=vln, mixed=mixed, obm=obm, pa=pa, pb=pb, sga=sga, sgb=sgb, merged=merged, gna4=gna4)


def _tail_in_specs(tm):
    tile = lambda c: pl.BlockSpec((tm, KW), lambda i: (i, c))
    return [tile(c) for c in range(4, 11)]


def _tail_weight_specs():
    return [_full((1, DK)), _full((1, KW)), _full((1, KW)), _full((GROUPS, SGU_BLOCK, SGU_BLOCK)),
            _full((GROUPS, SGU_BLOCK, 1)), _full((D, KW), single=True), _full((D, KW), single=True),
            _full((D, D), single=True)]


def _read_tail_inputs(of_ref, ob_ref, pcols):
    osum = of_ref[...] + ob_ref[...]
    og, u, v = (pcols[j][...].astype(F32) for j in range(3))
    ga = jnp.concatenate([pcols[3][...], pcols[4][...]], axis=1).astype(F32)
    gb = jnp.concatenate([pcols[5][...], pcols[6][...]], axis=1).astype(F32)
    return osum, og, u, v, ga, gb


def _tail_fwd(p, o_up, o_down, xt, modv, gna, ln_g, ln_b, w_s, b_s, w_paT, w_pbT, w_o, rows_per_example):
    rows = xt.shape[0]
    tm = min(TOKEN_TILE, rows_per_example)
    per_b = rows_per_example // tm

    def body(of_ref, ob_ref, *rest):
        pcols = rest[:7]
        (x_ref, mod_ref, gna_ref, lng_ref, lnb_ref, ws_ref, bs_ref, wpaT_ref, wpbT_ref, wo_ref,
         x1_ref, mix_ref, merged_ref, oa_ref, obm_ref) = rest[7:]
        t = _tail_forward(*_read_tail_inputs(of_ref, ob_ref, pcols), gna_ref[...], lng_ref[...], lnb_ref[...],
                          ws_ref, bs_ref, wpaT_ref, wpbT_ref)
        mix = _dot(t["merged"], wo_ref[...])
        x1_ref[...] = x_ref[...] + mod_ref[0, 2:3, :] * mix
        mix_ref[...] = mix.astype(mix_ref.dtype)
        merged_ref[...] = t["merged"].astype(merged_ref.dtype)
        oa_ref[...] = t["oa"].astype(oa_ref.dtype)
        obm_ref[...] = t["obm"].astype(obm_ref.dtype)

    row = lambda w: pl.BlockSpec((tm, w), lambda i: (i, 0))
    in_specs = [row(KW), row(KW)] + _tail_in_specs(tm) + [row(D), pl.BlockSpec((1, N_MOD, D), lambda i: (i // per_b, 0, 0))]
    in_specs += _tail_weight_specs()
    return pl.pallas_call(
        body, name="tail_fwd", grid=(rows // tm,),
        out_shape=(jax.ShapeDtypeStruct((rows, D), F32), jax.ShapeDtypeStruct((rows, D), MXU_DTYPE),
                   jax.ShapeDtypeStruct((rows, D), MXU_DTYPE), jax.ShapeDtypeStruct((rows, KW), MXU_DTYPE),
                   jax.ShapeDtypeStruct((rows, KW), MXU_DTYPE)),
        in_specs=in_specs, out_specs=(row(D), row(D), row(D), row(KW), row(KW)),
        compiler_params=_params(("arbitrary",)),
    )(o_up, o_down, *([p] * 7), xt, modv, gna, ln_g, ln_b, w_s, b_s, w_paT, w_pbT, w_o)


def _tail_bwd(p, o_up, o_down, dx1, mix, modv, gna, ln_g, ln_b, w_s, b_s, w_paT, w_pbT, w_o, rows_per_example):
    rows = dx1.shape[0]
    nb_ex = rows // rows_per_example
    tm = min(TOKEN_TILE, rows_per_example)
    per_b = rows_per_example // tm

    def body(of_ref, ob_ref, *rest):
        pcols = rest[:7]
        (dx1_ref, mix_ref, mod_ref, gna_ref, lng_ref, lnb_ref, ws_ref, bs_ref, wpaT_ref, wpbT_ref, wo_ref,
         dpt_ref, do_ref, dmix_ref, dpa_ref, dpb_ref, dmod_ref, small_ref, dws_ref, dbs_ref) = rest[7:]
        i = pl.program_id(0)

        @pl.when(i == 0)
        def _():
            small_ref[...] = jnp.zeros_like(small_ref)
            dws_ref[...] = jnp.zeros_like(dws_ref)
            dbs_ref[...] = jnp.zeros_like(dbs_ref)

        @pl.when(i % per_b == 0)
        def _():
            dmod_ref[...] = jnp.zeros_like(dmod_ref)

        osum, og, u, v, ga, gb = _read_tail_inputs(of_ref, ob_ref, pcols)
        ln_g = lng_ref[...]
        t = _tail_forward(osum, og, u, v, ga, gb, gna_ref[...], ln_g, lnb_ref[...], ws_ref, bs_ref, wpaT_ref, wpbT_ref)
        dx1v = dx1_ref[...]
        dmod_ref[0, 2:3, :] += jnp.sum(dx1v * mix_ref[...].astype(F32), axis=0, keepdims=True)
        dmix = dx1v * mod_ref[0, 2:3, :]
        dmix_ref[...] = dmix.astype(dmix_ref.dtype)
        dmerged = _dot(dmix, wo_ref[...], "nt")
        sga, sgb = t["sga"], t["sgb"]
        dpa = dmerged * sga
        dpb = dmerged * sgb
        dpa_ref[...] = dpa.astype(dpa_ref.dtype)
        dpb_ref[...] = dpb.astype(dpb_ref.dtype)
        dga = dmerged * t["pa"] * sga * (1.0 - sga)
        dgb = dmerged * t["pb"] * sgb * (1.0 - sgb)
        doa = _dot(dpa, wpaT_ref[...])
        dobm = _dot(dpb, wpbT_ref[...])
        dug = dobm * t["mixed"]
        dmixed = dobm * t["ug"]
        du = dug * _gelu_grad(u, t["tu"])
        dvln_blocks = []
        for n in range(tm // SGU_BLOCK):
            rs = slice(n * SGU_BLOCK, (n + 1) * SGU_BLOCK)
            parts = []
            for g in range(GROUPS):
                gs = slice(g * DK, (g + 1) * DK)
                dm = dmixed[rs, gs]
                parts.append(_dot(ws_ref[g], dm, "tn"))
                dws_ref[g] += _dot(dm, t["vln"][rs, gs], "nt")
                dbs_ref[g] += jnp.sum(dm, axis=1, keepdims=True)
            dvln_blocks.append(jnp.concatenate(parts, axis=1))
        dvln = jnp.concatenate(dvln_blocks, axis=0) if len(dvln_blocks) > 1 else dvln_blocks[0]
        vhat = t["vhat"]
        small_ref[1:2, 0:KW] += jnp.sum(dvln * vhat, axis=0, keepdims=True)
        small_ref[2:3, 0:KW] += jnp.sum(dvln, axis=0, keepdims=True)
        dvhat = dvln * ln_g
        dvg = t["rstd"] * (dvhat - jnp.mean(dvhat, axis=-1, keepdims=True)
                           - vhat * jnp.mean(dvhat * vhat, axis=-1, keepdims=True))
        dv = dvg * _gelu_grad(v, t["tv"])
        sg_og = t["sg_og"]
        doan = doa * t["silu_og"]
        dog = doa * t["oan"] * (sg_og * (1.0 + og * (1.0 - sg_og)))
        prod = doan * t["on"]
        dgna = jnp.zeros((1, DK), F32)
        for h in range(HEADS):
            dgna = dgna + jnp.sum(prod[:, h * DK:(h + 1) * DK], axis=0, keepdims=True)
        small_ref[0:1, 0:DK] += dgna
        don = doan * t["gna4"]
        dot_parts = []
        for h in range(HEADS):
            hs = slice(h * DK, (h + 1) * DK)
            m = jnp.mean(don[:, hs] * t["on"][:, hs], axis=-1, keepdims=True)
            dot_parts.append(t["r"][:, hs] * (don[:, hs] - t["on"][:, hs] * m))
        do_ref[...] = jnp.concatenate(dot_parts, axis=1).astype(do_ref.dtype)
        for j, val in enumerate((dog, du, dv)):
            dpt_ref[:, j * KW:(j + 1) * KW] = val.astype(dpt_ref.dtype)
        dpt_ref[:, 3 * KW:3 * KW + D] = dga.astype(dpt_ref.dtype)
        dpt_ref[:, 3 * KW + D:] = dgb.astype(dpt_ref.dtype)

    row = lambda w: pl.BlockSpec((tm, w), lambda i: (i, 0))
    in_specs = [row(KW), row(KW)] + _tail_in_specs(tm) + [row(D), row(D), pl.BlockSpec((1, N_MOD, D), lambda i: (i // per_b, 0, 0))]
    in_specs += _tail_weight_specs()
    cd = MXU_DTYPE
    return pl.pallas_call(
        body, name="tail_bwd", grid=(rows // tm,),
        out_shape=(jax.ShapeDtypeStruct((rows, TAIL_COLS), cd), jax.ShapeDtypeStruct((rows, KW), cd),
                   jax.ShapeDtypeStruct((rows, D), cd), jax.ShapeDtypeStruct((rows, D), cd),
                   jax.ShapeDtypeStruct((rows, D), cd), jax.ShapeDtypeStruct((nb_ex, 8, D), F32),
                   jax.ShapeDtypeStruct((8, D), F32), jax.ShapeDtypeStruct((GROUPS, SGU_BLOCK, SGU_BLOCK), F32),
                   jax.ShapeDtypeStruct((GROUPS, SGU_BLOCK, 1), F32)),
        in_specs=in_specs,
        out_specs=(row(TAIL_COLS), row(KW), row(D), row(D), row(D),
                   pl.BlockSpec((1, 8, D), lambda i: (i // per_b, 0, 0)), _full((8, D)),
                   _full((GROUPS, SGU_BLOCK, SGU_BLOCK)), _full((GROUPS, SGU_BLOCK, 1))),
        compiler_params=_params(("arbitrary",)),
    )(o_up, o_down, *([p] * 7), dx1, mix, modv, gna, ln_g, ln_b, w_s, b_s, w_paT, w_pbT, w_o)


def _ffn(x1, target, modv, g_ffn, g_final, w_upT, w_down, rows_per_example):
    rows = x1.shape[0]
    nb_ex = rows // rows_per_example
    tm = min(TOKEN_TILE, rows_per_example)
    per_b = rows_per_example // tm
    n_ff = D_FF // FF_CHUNK

    def body(x1_ref, tgt_ref, mod_ref, gffn_ref, gfin_ref, wup_ref, wdn_ref,
             dx1_ref, h2_ref, dffn_ref, act_ref, dup_ref, dmod_ref, small_ref, a_scr, b_scr):
        i = pl.program_id(0)

        @pl.when(i == 0)
        def _():
            small_ref[...] = jnp.zeros_like(small_ref)

        @pl.when(i % per_b == 0)
        def _():
            dmod_ref[...] = jnp.zeros_like(dmod_ref)

        x1v = x1_ref[...]
        g2 = gffn_ref[...]
        m3, m4, m5 = mod_ref[0, 3:4, :], mod_ref[0, 4:5, :], mod_ref[0, 5:6, :]
        r2 = lax.rsqrt(jnp.mean(x1v * x1v, axis=-1, keepdims=True) + EPS)
        xn2 = x1v * r2
        h2 = (xn2 * g2) * (1.0 + m4) + m3
        h2b = h2.astype(MXU_DTYPE)
        h2_ref[...] = h2b
        ffn = jnp.zeros((tm, D), F32)
        for j in range(n_ff):
            cs = slice(j * FF_CHUNK, (j + 1) * FF_CHUNK)
            a = _dot(h2b, wup_ref[j * FF_CHUNK:(j + 1) * FF_CHUNK, :], "nt")
            bgate = _dot(h2b, wup_ref[D_FF + j * FF_CHUNK:D_FF + (j + 1) * FF_CHUNK, :], "nt")
            a_scr[:, cs] = a
            b_scr[:, cs] = bgate
            act = (a * _sigmoid(a) * bgate).astype(MXU_DTYPE)
            act_ref[:, cs] = act
            ffn = ffn + _dot(act, wdn_ref[cs, :])
        x2 = x1v + m5 * ffn
        r3 = lax.rsqrt(jnp.mean(x2 * x2, axis=-1, keepdims=True) + EPS)
        xn3 = x2 * r3
        gf = gfin_ref[...]
        err = xn3 * gf - tgt_ref[...]
        loss = 0.5 * jnp.sum(jnp.mean(err * err, axis=-1, keepdims=True), axis=0, keepdims=True)
        small_ref[2:3, :] += jnp.broadcast_to(loss, (1, D))
        dy = err * (1.0 / D)
        small_ref[1:2, :] += jnp.sum(dy * xn3, axis=0, keepdims=True)
        dxn3 = dy * gf
        dx2 = r3 * (dxn3 - xn3 * jnp.mean(dxn3 * xn3, axis=-1, keepdims=True))
        dmod_ref[0, 5:6, :] += jnp.sum(dx2 * ffn, axis=0, keepdims=True)
        dffn = (dx2 * m5).astype(MXU_DTYPE)
        dffn_ref[...] = dffn
        dh2 = jnp.zeros((tm, D), F32)
        for j in range(n_ff):
            cs = slice(j * FF_CHUNK, (j + 1) * FF_CHUNK)
            dact = _dot(dffn, wdn_ref[cs, :], "nt")
            a, bgate = a_scr[:, cs], b_scr[:, cs]
            s = _sigmoid(a)
            da = (dact * bgate * (s * (1.0 + a * (1.0 - s)))).astype(MXU_DTYPE)
            dbg = (dact * a * s).astype(MXU_DTYPE)
            dup_ref[:, cs] = da
            dup_ref[:, D_FF + j * FF_CHUNK:D_FF + (j + 1) * FF_CHUNK] = dbg
            dh2 = dh2 + _dot(da, wup_ref[j * FF_CHUNK:(j + 1) * FF_CHUNK, :])
            dh2 = dh2 + _dot(dbg, wup_ref[D_FF + j * FF_CHUNK:D_FF + (j + 1) * FF_CHUNK, :])
        dmod_ref[0, 3:4, :] += jnp.sum(dh2, axis=0, keepdims=True)
        dmod_ref[0, 4:5, :] += jnp.sum(dh2 * xn2 * g2, axis=0, keepdims=True)
        small_ref[0:1, :] += jnp.sum(dh2 * (1.0 + m4) * xn2, axis=0, keepdims=True)
        dxn2 = dh2 * g2 * (1.0 + m4)
        dx1_ref[...] = dx2 + r2 * (dxn2 - xn2 * jnp.mean(dxn2 * xn2, axis=-1, keepdims=True))

    row = lambda w: pl.BlockSpec((tm, w), lambda i: (i, 0))
    cd = MXU_DTYPE
    return pl.pallas_call(
        body, name="ffn_fwd_bwd", grid=(rows // tm,),
        out_shape=(jax.ShapeDtypeStruct((rows, D), F32), jax.ShapeDtypeStruct((rows, D), cd),
                   jax.ShapeDtypeStruct((rows, D), cd), jax.ShapeDtypeStruct((rows, D_FF), cd),
                   jax.ShapeDtypeStruct((rows, 2 * D_FF), cd), jax.ShapeDtypeStruct((nb_ex, 8, D), F32),
                   jax.ShapeDtypeStruct((8, D), F32)),
        in_specs=[row(D), row(D), pl.BlockSpec((1, N_MOD, D), lambda i: (i // per_b, 0, 0)), _full((1, D)), _full((1, D)),
                  _full((2 * D_FF, D), single=True), _full((D_FF, D), single=True)],
        out_specs=(row(D), row(D), row(D), row(D_FF), row(2 * D_FF),
                   pl.BlockSpec((1, 8, D), lambda i: (i // per_b, 0, 0)), _full((8, D))),
        scratch_shapes=[pltpu.VMEM((tm, D_FF), F32), pltpu.VMEM((tm, D_FF), F32)],
        compiler_params=_params(("arbitrary",)),
    )(x1, target, modv, g_ffn, g_final, w_upT, w_down)


def _inproj_bwd(pieces, dpt, xt, dx1, modv, g, w_inT, rows_per_example, name):
    rows = xt.shape[0]
    latent = dx1 is not None
    n_cols = IN_COLS if latent else CTX_COLS
    tm = min(TOKEN_TILE, rows_per_example)
    per_b = rows_per_example // tm
    n_mod_blocks = rows // rows_per_example if latent else 1
    n_pieces = len(pieces)

    def body(*refs):
        it = iter(refs)
        pc = [next(it) for _ in range(n_pieces)]
        dpt_ref = next(it) if latent else None
        x_ref = next(it)
        dx1_ref = next(it) if latent else None
        mod_ref, g_ref, w_ref = next(it), next(it), next(it)
        gx_ref = next(it) if latent else None
        dp_ref, dmod_ref, small_ref = next(it), next(it), next(it)
        i = pl.program_id(0)

        @pl.when(i == 0)
        def _():
            small_ref[...] = jnp.zeros_like(small_ref)

        @pl.when((i % per_b == 0) if latent else (i == 0))
        def _():
            dmod_ref[...] = jnp.zeros_like(dmod_ref)

        cols = [pc[0][...], pc[1][...], pc[2][...] + pc[3][...]]
        if latent:
            cols.append(pc[4][...] + pc[5][...])
        dh = jnp.zeros((tm, D), F32)
        for j, val in enumerate(cols):
            vb = val.astype(MXU_DTYPE)
            dp_ref[:, j * KW:(j + 1) * KW] = vb
            dh = dh + _dot(vb, w_ref[j * KW:(j + 1) * KW, :])
        if latent:
            for j in range(4, IN_COLS // KW):
                vb = dpt_ref[:, (j - 4) * KW:(j - 3) * KW]
                dp_ref[:, j * KW:(j + 1) * KW] = vb
                dh = dh + _dot(vb, w_ref[j * KW:(j + 1) * KW, :])
        x = x_ref[...]
        gv = g_ref[...]
        m1 = mod_ref[0, 1:2, :]
        r = lax.rsqrt(jnp.mean(x * x, axis=-1, keepdims=True) + EPS)
        xn = x * r
        dmod_ref[0, 0:1, :] += jnp.sum(dh, axis=0, keepdims=True)
        dmod_ref[0, 1:2, :] += jnp.sum(dh * xn * gv, axis=0, keepdims=True)
        small_ref[0:1, :] += jnp.sum(dh * (1.0 + m1) * xn, axis=0, keepdims=True)
        if latent:
            dxn = dh * gv * (1.0 + m1)
            gx_ref[...] = dx1_ref[...] + r * (dxn - xn * jnp.mean(dxn * xn, axis=-1, keepdims=True))

    row = lambda w: pl.BlockSpec((tm, w), lambda i: (i, 0))
    mod_idx = (lambda i: (i // per_b, 0, 0)) if latent else (lambda i: (0, 0, 0))
    in_specs = [row(KW)] * n_pieces + ([row(TAIL_COLS)] if latent else []) + [row(D)] + ([row(D)] if latent else [])
    in_specs += [pl.BlockSpec((1, N_MOD, D), mod_idx), _full((1, D)),
                 pl.BlockSpec((n_cols, D), lambda i: (0, 0), pipeline_mode=pl.Buffered(1))]
    args = list(pieces) + ([dpt] if latent else []) + [xt] + ([dx1] if latent else []) + [modv, g, w_inT]
    out_shape = ([jax.ShapeDtypeStruct((rows, D), F32)] if latent else []) + [
        jax.ShapeDtypeStruct((rows, n_cols), MXU_DTYPE), jax.ShapeDtypeStruct((n_mod_blocks, 8, D), F32),
        jax.ShapeDtypeStruct((8, D), F32)]
    out_specs = ([row(D)] if latent else []) + [row(n_cols), pl.BlockSpec((1, 8, D), mod_idx), _full((8, D))]
    return pl.pallas_call(
        body, name=name, grid=(rows // tm,), out_shape=out_shape, in_specs=in_specs, out_specs=out_specs,
        compiler_params=_params(("arbitrary",)),
    )(*args)


def _grad_matmul(a, b, name, init=None, tn=512, tt=1024):
    rows, n = a.shape
    k = b.shape[1]
    tn = min(tn, n)
    tt = min(tt, rows)
    steps = rows // tt
    has_init = init is not None

    def body(*refs):
        if has_init:
            a_ref, b_ref, init_ref, o_ref, acc = refs
        else:
            a_ref, b_ref, o_ref, acc = refs
        t = pl.program_id(1)

        @pl.when(t == 0)
        def _():
            acc[...] = init_ref[...].astype(F32) if has_init else jnp.zeros_like(acc)

        acc[...] += _dot(a_ref[...], b_ref[...], "tn")

        @pl.when(t == steps - 1)
        def _():
            o_ref[...] = acc[...].astype(o_ref.dtype)

    in_specs = [pl.BlockSpec((tt, tn), lambda i, t: (t, i)), pl.BlockSpec((tt, k), lambda i, t: (t, 0))]
    args = [a, b]
    if has_init:
        in_specs.append(pl.BlockSpec((tn, k), lambda i, t: (i, 0)))
        args.append(init)
    return pl.pallas_call(
        body, name=name, grid=(n // tn, steps), out_shape=jax.ShapeDtypeStruct((n, k), PAYLOAD_DTYPE),
        in_specs=in_specs, out_specs=pl.BlockSpec((tn, k), lambda i, t: (i, 0)),
        scratch_shapes=[pltpu.VMEM((tn, k), F32)],
        compiler_params=_params(("arbitrary", "arbitrary")),
    )(*args)


def _row_tile(rows, limit=256):
    if rows <= limit:
        return rows
    for t in range(limit, 7, -8):
        if rows % t == 0:
            return t
    return rows


def _sum8(stack, name):
    _, rows, cols = stack.shape
    tr = _row_tile(rows)

    def body(s_ref, o_ref):
        acc = s_ref[0].astype(F32)
        for j in range(1, N_DEV):
            acc = acc + s_ref[j].astype(F32)
        o_ref[...] = acc

    return pl.pallas_call(
        body, name=name, grid=(rows // tr,), out_shape=jax.ShapeDtypeStruct((rows, cols), F32),
        in_specs=[pl.BlockSpec((N_DEV, tr, cols), lambda i: (0, i, 0))],
        out_specs=pl.BlockSpec((tr, cols), lambda i: (i, 0)),
        compiler_params=_params(("arbitrary",)),
    )(stack)


def _small_reduce(stack, gam, nb_ex):
    def body(s_ref, gam_ref, o_ref, bm_ref):
        acc = s_ref[0]
        for j in range(1, N_DEV):
            acc = acc + s_ref[j]
        o_ref[...] = acc
        bm = acc[8:8 + N_MOD, :]
        for e in range(nb_ex):
            bm = bm + acc[16 + e * N_MOD:16 + (e + 1) * N_MOD, :]
        lb = jnp.concatenate([_lower_bound(gam_ref, 0), _lower_bound(gam_ref, 1)], axis=1)
        dgam = acc[7:8, :] * lb * (1.0 - lb)
        bm_ref[...] = jnp.concatenate([bm, dgam, -dgam], axis=0)

    return pl.pallas_call(
        body, name="small_reduce", grid=(1,),
        out_shape=(jax.ShapeDtypeStruct((SMALL_ROWS, D), F32), jax.ShapeDtypeStruct((8, D), F32)),
        in_specs=[_full((N_DEV, SMALL_ROWS, D)), _full((4, KW))], out_specs=(_full((SMALL_ROWS, D)), _full((8, D))),
        compiler_params=_params(("arbitrary",)),
    )(stack, gam)


def _adamw(w, g, m, v, name):
    shape = w.shape
    cols = shape[-1]
    rows = 1
    for s in shape[:-1]:
        rows *= s
    tr = _row_tile(rows)

    def body(w_ref, g_ref, m_ref, v_ref, d_ref, nm_ref, nv_ref):
        gv = g_ref[...]
        nm = ADAM_B1 * m_ref[...] + (1.0 - ADAM_B1) * gv
        nv = ADAM_B2 * v_ref[...] + (1.0 - ADAM_B2) * (gv * gv)
        m_hat = nm / (1.0 - ADAM_B1 ** ADAM_STEP)
        v_hat = nv / (1.0 - ADAM_B2 ** ADAM_STEP)
        d_ref[...] = -ADAM_LR * (m_hat / (jnp.sqrt(v_hat) + ADAM_EPS) + ADAM_WD * w_ref[...])
        nm_ref[...] = nm
        nv_ref[...] = nv

    blk = pl.BlockSpec((tr, cols), lambda i: (i, 0))
    sd = jax.ShapeDtypeStruct((rows, cols), F32)
    d, nm, nv = pl.pallas_call(
        body, name=name, grid=(rows // tr,), out_shape=(sd, sd, sd), in_specs=[blk] * 4, out_specs=(blk, blk, blk),
        compiler_params=_params(("arbitrary",)),
    )(w.reshape(rows, cols), g.reshape(rows, cols), m.reshape(rows, cols), v.reshape(rows, cols))
    return d.reshape(shape), nm.reshape(shape), nv.reshape(shape)


def _local_step(x, ctx, target, modv, mcv, gam, g_mix, g_ffn, gna, ln_g, ln_b, w_s, b_s, g_final,
                w_inT, late_weights, emit):
    nb_ex, seq, _ = x.shape
    ctx_len = ctx.shape[1]
    xt = x.reshape(nb_ex * seq, D)
    ct = ctx.reshape(nb_ex * ctx_len, D)
    tgt = target.reshape(nb_ex * seq, D)
    bs3 = b_s.reshape(GROUPS, SGU_BLOCK, 1)

    def behind(a, token):
        return a if token is None else a + token[0:1, 0:1]

    pc, hc = _inproj(ct, mcv, g_mix, w_inT, CTX_COLS, ctx_len, "inproj_ctx")
    p, h = _inproj(xt, modv, g_mix, w_inT, IN_COLS, seq, "inproj_lat")
    cst_f, cst_b, s_ctx = _hgrn_fwd(pc, gam, None, ctx_len, False, "hgrn_fwd_ctx")
    o_up, o_down, st_f, st_b, _ = _hgrn_fwd(p, gam, s_ctx, seq, True, "hgrn_fwd_lat")
    w_upT, w_down, w_o, w_paT, w_pbT = late_weights(o_up)
    x1, mix, merged, oa, obm = _tail_fwd(p, o_up, o_down, xt, modv, gna, ln_g, ln_b, w_s, bs3, w_paT, w_pbT, w_o, seq)
    dx1, h2, dffn, act, dup, dmod_ffn, small_ffn = _ffn(x1, tgt, modv, g_ffn, g_final, w_upT, w_down, seq)
    gw_upT = _grad_matmul(dup, h2, "gw_up")
    gw_down = _grad_matmul(act, dffn, "gw_down", tn=256)
    token = emit("ffn", (gw_upT, gw_down))
    dpt, do, dmix, dpa, dpb, dmod_tail, small_tail, dws, dbs = _tail_bwd(
        p, o_up, o_down, dx1, mix, modv, behind(gna, token), ln_g, ln_b, w_s, bs3, w_paT, w_pbT, w_o, seq)
    gw_o = _grad_matmul(merged, dmix, "gw_o")
    gw_paT = _grad_matmul(dpa, oa, "gw_pa")
    gw_pbT = _grad_matmul(dpb, obm, "gw_pb")
    token = emit("tail", (gw_o, gw_paT, gw_pbT))
    gam_b = behind(gam, token)
    dzf, dvf, dqf, dzb, dvb, dqb, dlb, ds0 = _hgrn_bwd(p, gam_b, do, st_f, st_b, None, seq, True, "hgrn_bwd_lat")
    czf, cvf, czb, cvb, dlb_c, _ = _hgrn_bwd(pc, gam_b, None, cst_f, cst_b, ds0, ctx_len, False, "hgrn_bwd_ctx")
    grad_x, dp, dmod_in, small_in = _inproj_bwd([dzf, dzb, dvf, dvb, dqf, dqb], dpt, xt, dx1, modv, g_mix, w_inT,
                                                 seq, "inproj_bwd_lat")
    dpc, dmc, small_c = _inproj_bwd([czf, czb, cvf, cvb], None, ct, None, mcv, g_mix, w_inT, ctx_len, "inproj_bwd_ctx")
    g_ctx = _grad_matmul(dpc, hc, "gw_in_ctx")
    g_ctx = jnp.pad(g_ctx, ((0, IN_COLS - CTX_COLS), (0, 0)))
    gw_inT = _grad_matmul(dp, h, "gw_in", init=g_ctx)

    z = lambda r: jnp.zeros((r, D), F32)
    pad = lambda a: jnp.pad(a, ((0, 0), (0, D - a.shape[1])))
    dlb_row = (dlb + dlb_c).reshape(1, 2 * KW)
    dmod = dmod_in + dmod_tail + dmod_ffn
    small = jnp.concatenate([
        small_in[0:1] + small_c[0:1],
        small_ffn[0:1],
        small_ffn[1:2],
        small_tail[0:1],
        small_tail[1:2],
        small_tail[2:3],
        pad(dbs.reshape(1, GROUPS * SGU_BLOCK)),
        dlb_row,
        dmc[0, 0:N_MOD],
        z(2),
        dmod[:, 0:N_MOD].reshape(nb_ex * N_MOD, D),
        z(24 - nb_ex * N_MOD),
        dws.reshape(GROUPS * SGU_BLOCK * SGU_BLOCK // D, D),
    ], axis=0)
    loss = small_ffn[2, 0]
    return loss, grad_x.reshape(x.shape), gw_inT, small


def kernel(x, c, ctx, c_ctx, w_mod, b_mod, g_mix, g_ffn, w_in, lb_gamma, g_norm_a, ln_v_g, ln_v_b, w_s, b_s, w_pa, w_pb, w_o, w_up, w_down, g_final, loss_target, m_c_ctx, m_w_mod, m_b_mod, m_g_mix, m_g_ffn, m_w_in, m_lb_gamma, m_g_norm_a, m_ln_v_g, m_ln_v_b, m_w_s, m_b_s, m_w_pa, m_w_pb, m_w_o, m_w_up, m_w_down, m_g_final, v_c_ctx, v_w_mod, v_b_mod, v_g_mix, v_g_ffn, v_w_in, v_lb_gamma, v_g_norm_a, v_ln_v_g, v_ln_v_b, v_w_s, v_b_s, v_w_pa, v_w_pb, v_w_o, v_w_up, v_w_down, v_g_final):
    nb_ex = x.shape[0]
    me = 4 * lax.axis_index("x") + 2 * lax.axis_index("y") + lax.axis_index("c")
    cd = MXU_DTYPE
    mod_cols = w_mod.shape[2]
    lb_cols = lb_gamma.shape[2]

    w_inT_l = w_in[0].T.astype(cd)
    w_upT_l = w_up[0].T.astype(cd)
    w_paT_l = w_pa[0].T.astype(cd)
    w_pbT_l = w_pb[0].T.astype(cd)
    cl = jnp.concatenate([c, jnp.pad(lb_gamma.reshape(1, 4 * lb_cols), ((0, 0), (0, D - 4 * lb_cols))),
                          jnp.zeros((8 - nb_ex - 1, D), F32)], axis=0)
    g_in, g_cl = _exchange([(w_inT_l, "gather"), (cl, "gather")], "gather_w_in")
    w_inT = g_in.reshape(IN_COLS, D)
    c_all = g_cl[:, 0:nb_ex].reshape(N_DEV * nb_ex, D)
    gam = jnp.transpose(g_cl[:, nb_ex, 0:4 * lb_cols].reshape(N_DEV, 4, lb_cols), (1, 0, 2)).reshape(4, KW)
    late, token = _exchange_start(
        [(w_upT_l, "gather"), (w_down[0].astype(cd), "gather"), (w_o[0].astype(cd), "gather"), (w_paT_l, "gather"),
         (w_pbT_l, "gather")], "gather_late")

    def late_weights(after):
        g_up, g_down, g_o, g_pa, g_pb = _exchange_wait(late, after)
        return (g_up.reshape(2 * D_FF, D), g_down.reshape(D_FF, D), g_o.reshape(D, D), g_pa.reshape(D, KW),
                g_pb.reshape(D, KW))

    n_c = N_DEV * nb_ex
    cvec = jnp.concatenate([c_all, c_ctx.reshape(1, D), jnp.zeros((7, D), F32)], axis=0) + token[0:1, 0:1]
    b_mod_l = lax.dynamic_slice(b_mod, (0, me * mod_cols), (1, mod_cols))
    mod_l, svec = _mod_fwd(cvec, w_mod[0], b_mod_l)
    (g_mod,) = _exchange([(mod_l, "gather")], "gather_mod")
    mod_all = jnp.transpose(g_mod, (1, 0, 2)).reshape(n_c + 8, N_MOD * D)
    modv = lax.dynamic_slice(mod_all, (me * nb_ex, 0), (nb_ex, N_MOD * D)).reshape(nb_ex, N_MOD, D)
    mcv = mod_all[n_c].reshape(1, N_MOD, D)

    blocks = lambda a: a.reshape(N_DEV, a.shape[0] // N_DEV, a.shape[1])
    in_flight = {}

    def emit(stage, grads):
        in_flight[stage], tok = _exchange_start([(blocks(g), "scatter") for g in grads], "scatter_" + stage)
        return tok

    loss_l, grad_x, gw_inT, small = _local_step(
        x, ctx, loss_target, modv, mcv, gam, g_mix, g_ffn, g_norm_a, ln_v_g, ln_v_b, w_s[0], b_s[0],
        g_final.reshape(1, D), w_inT, late_weights, emit)
    loss = lax.psum(loss_l, ("x", "y", "c"))
    last, token = _exchange_start([(blocks(gw_inT), "scatter"), (small, "gather")], "scatter_in")

    r_up, r_down = _exchange_wait(in_flight["ffn"], token)
    r_o, r_pa, r_pb = _exchange_wait(in_flight["tail"], token)
    grad_w_up = _sum8(r_up, "sum_w_up").T[None]
    grad_w_down = _sum8(r_down, "sum_w_down")[None]
    grad_w_o = _sum8(r_o, "sum_w_o")[None]
    grad_w_pa = _sum8(r_pa, "sum_w_pa").T[None]
    grad_w_pb = _sum8(r_pb, "sum_w_pb").T[None]
    early = {"w_up": (w_up, grad_w_up, m_w_up, v_w_up), "w_down": (w_down, grad_w_down, m_w_down, v_w_down),
             "w_o": (w_o, grad_w_o, m_w_o, v_w_o), "w_pa": (w_pa, grad_w_pa, m_w_pa, v_w_pa),
             "w_pb": (w_pb, grad_w_pb, m_w_pb, v_w_pb)}
    early_out = {nm: _adamw(*args, "adamw_" + nm) for nm, args in early.items()}

    r_in, r_small = _exchange_wait(last, early_out["w_up"][0])
    grad_w_in = _sum8(r_in, "sum_w_in").T[None]
    tot, bm = _small_reduce(r_small, gam, nb_ex)
    grad_g_mix, grad_g_ffn, grad_g_final = tot[0:1], tot[1:2], tot[2]
    grad_g_norm_a = tot[3:4, 0:DK]
    grad_ln_v_g, grad_ln_v_b = tot[4:5, 0:KW], tot[5:6, 0:KW]
    grad_b_s = tot[6, 0:GROUPS * SGU_BLOCK].reshape(1, GROUPS, SGU_BLOCK)
    grad_w_s = tot[40:104].reshape(1, GROUPS, SGU_BLOCK, SGU_BLOCK)
    grad_b_mod = bm[0:N_MOD].reshape(1, N_MOD * D)
    grad_lb_gamma = lax.dynamic_slice(bm[6:8].reshape(2, 2, KW), (0, 0, me * lb_cols), (2, 2, lb_cols))

    dmod_all = r_small[:, 16:16 + nb_ex * N_MOD].reshape(n_c, N_MOD * D)
    dmod_l = jnp.concatenate([lax.dynamic_slice(dmod_all, (0, me * mod_cols), (n_c, mod_cols)),
                              lax.dynamic_slice(tot[8:8 + N_MOD].reshape(1, N_MOD * D), (0, me * mod_cols), (1, mod_cols)),
                              jnp.zeros((7, mod_cols), F32)], axis=0)
    gw_mod, gc = _mod_bwd(svec, cvec, dmod_l, w_mod[0])
    grad_w_mod = gw_mod[None]
    (r_gc,) = _exchange([(gc[n_c:n_c + 8], "gather")], "gather_c_ctx")
    grad_c_ctx = _sum8(r_gc, "sum_c_ctx")[0]

    names = ["c_ctx", "w_mod", "b_mod", "g_mix", "g_ffn", "w_in", "lb_gamma", "g_norm_a", "ln_v_g", "ln_v_b", "w_s",
             "b_s", "w_pa", "w_pb", "w_o", "w_up", "w_down", "g_final"]
    weights = [c_ctx, w_mod, b_mod, g_mix, g_ffn, w_in, lb_gamma, g_norm_a, ln_v_g, ln_v_b, w_s, b_s, w_pa, w_pb, w_o,
               w_up, w_down, g_final]
    grads = [grad_c_ctx, grad_w_mod, grad_b_mod, grad_g_mix, grad_g_ffn, grad_w_in, grad_lb_gamma, grad_g_norm_a,
             grad_ln_v_g, grad_ln_v_b, grad_w_s, grad_b_s, grad_w_pa, grad_w_pb, grad_w_o, grad_w_up, grad_w_down,
             grad_g_final]
    ms = [m_c_ctx, m_w_mod, m_b_mod, m_g_mix, m_g_ffn, m_w_in, m_lb_gamma, m_g_norm_a, m_ln_v_g, m_ln_v_b, m_w_s, m_b_s,
          m_w_pa, m_w_pb, m_w_o, m_w_up, m_w_down, m_g_final]
    vs = [v_c_ctx, v_w_mod, v_b_mod, v_g_mix, v_g_ffn, v_w_in, v_lb_gamma, v_g_norm_a, v_ln_v_g, v_ln_v_b, v_w_s, v_b_s,
          v_w_pa, v_w_pb, v_w_o, v_w_up, v_w_down, v_g_final]
    deltas, new_ms, new_vs = [], [], []
    for nm, w, g, m, v in zip(names, weights, grads, ms, vs):
        d, nm_, nv_ = early_out[nm] if nm in early_out else _adamw(w, g.reshape(w.shape), m, v, "adamw_" + nm)
        deltas.append(d)
        new_ms.append(nm_)
        new_vs.append(nv_)
    grads = [g.reshape(w.shape) for g, w in zip(grads, weights)]
    return (loss, grad_x, *grads, *deltas, *new_ms, *new_vs)
```

```python
import functools

import jax
import jax.numpy as jnp
from jax import lax
from jax.experimental import pallas as pl
from jax.experimental.pallas import tpu as pltpu

F32 = jnp.float32
MXU_DTYPE = jnp.bfloat16
PAYLOAD_DTYPE = jnp.bfloat16

N_DEV = 8
D = 1024
HEADS = 4
DK = 128
KW = HEADS * DK
CHUNK = 64
SGU_BLOCK = 128
GROUPS = 4
D_FF = 2816
FF_CHUNK = 256
N_MOD = 6
IN_COLS = 5632
CTX_COLS = 1536
TAIL_COLS = IN_COLS - 4 * KW
EPS = 1e-6
ADAM_LR, ADAM_B1, ADAM_B2, ADAM_EPS, ADAM_WD, ADAM_STEP = 0.001, 0.9, 0.999, 1e-08, 0.01, 10

VMEM_LIMIT = 56 * 1024 * 1024
TOKEN_TILE = 256
SMALL_ROWS = 104


def _params(sem):
    return pltpu.CompilerParams(dimension_semantics=sem, vmem_limit_bytes=VMEM_LIMIT)


_DN = {"nn": (((1,), (0,)), ((), ())), "nt": (((1,), (1,)), ((), ())), "tn": (((0,), (0,)), ((), ()))}


def _dot(a, b, form="nn"):
    return lax.dot_general(a.astype(MXU_DTYPE), b.astype(MXU_DTYPE), _DN[form], preferred_element_type=F32)


def _dotx(a, b, form="nn"):
    return lax.dot_general(a.astype(F32), b.astype(F32), _DN[form], preferred_element_type=F32,
                           precision=lax.Precision.HIGHEST)


def _full(shape, single=False):
    n = len(shape)
    if single:
        return pl.BlockSpec(shape, lambda *_: (0,) * n, pipeline_mode=pl.Buffered(1))
    return pl.BlockSpec(shape, lambda *_: (0,) * n)


def _sigmoid(z):
    return 1.0 / (1.0 + jnp.exp(-z))


def _gelu(x):
    c = 0.7978845608028654
    t = jnp.tanh(c * (x + 0.044715 * x * x * x))
    return 0.5 * x * (1.0 + t), t


def _gelu_grad(x, t):
    c = 0.7978845608028654
    return 0.5 * (1.0 + t) + 0.5 * x * (1.0 - t * t) * c * (1.0 + 3 * 0.044715 * x * x)


def _exchange(items, name):
    n = len(items)
    out_shape = []
    for a, mode in items:
        blk = a.shape if mode == "gather" else a.shape[1:]
        out_shape.append(jax.ShapeDtypeStruct((N_DEV,) + tuple(blk), a.dtype))

    def body(*refs):
        srcs, dsts = refs[:n], refs[n:2 * n]
        send_sems, recv_sems, local_sems = refs[2 * n:]
        x, y, c = lax.axis_index("x"), lax.axis_index("y"), lax.axis_index("c")
        me = 4 * x + 2 * y + c

        def src_for(i, dev):
            return srcs[i] if items[i][1] == "gather" else srcs[i].at[dev]

        local = [pltpu.make_async_copy(src_for(i, me), dsts[i].at[me], local_sems.at[i]) for i in range(n)]
        for cp in local:
            cp.start()
        remote = []
        for k in range(1, N_DEV):
            px = jnp.bitwise_xor(x, (k >> 2) & 1)
            py = jnp.bitwise_xor(y, (k >> 1) & 1)
            pc = jnp.bitwise_xor(c, k & 1)
            peer = 4 * px + 2 * py + pc
            for i in range(n):
                cp = pltpu.make_async_remote_copy(
                    src_ref=src_for(i, peer), dst_ref=dsts[i].at[me],
                    send_sem=send_sems.at[i * (N_DEV - 1) + k - 1], recv_sem=recv_sems.at[i * (N_DEV - 1) + k - 1],
                    device_id=(px, py, pc), device_id_type=pl.DeviceIdType.MESH)
                cp.start()
                remote.append(cp)
        for cp in remote:
            cp.wait()
        for cp in local:
            cp.wait()

    any_spec = pl.BlockSpec(memory_space=pl.ANY)
    return pl.pallas_call(
        body, name=name, out_shape=out_shape,
        in_specs=[any_spec] * n, out_specs=[any_spec] * n,
        scratch_shapes=[pltpu.SemaphoreType.DMA((n * (N_DEV - 1),)), pltpu.SemaphoreType.DMA((n * (N_DEV - 1),)),
                        pltpu.SemaphoreType.DMA((n,))],
    )(*[a for a, _ in items])


_HBM = pl.BlockSpec(memory_space=pltpu.HBM)
_SEM = pl.BlockSpec(memory_space=pltpu.SEMAPHORE)
_EFFECT = pltpu.SideEffectType.DATAFLOW_SIDE_EFFECTING


def _split_copies(items, srcs, lands, send_sems, recv_sems):
    x, y, c = lax.axis_index("x"), lax.axis_index("y"), lax.axis_index("c")
    me = 4 * x + 2 * y + c
    copies = []
    for k in range(1, N_DEV):
        px = jnp.bitwise_xor(x, (k >> 2) & 1)
        py = jnp.bitwise_xor(y, (k >> 1) & 1)
        pc = jnp.bitwise_xor(c, k & 1)
        peer = 4 * px + 2 * py + pc
        for i in range(len(items)):
            src = srcs[i] if items[i][1] == "gather" else srcs[i].at[peer]
            copies.append(pltpu.make_async_remote_copy(
                src_ref=src, dst_ref=lands[i].at[me],
                send_sem=send_sems.at[i * (N_DEV - 1) + k - 1], recv_sem=recv_sems.at[i * (N_DEV - 1) + k - 1],
                device_id=(px, py, pc), device_id_type=pl.DeviceIdType.MESH))
    return me, copies


def _exchange_start(items, name, after):
    n = len(items)
    n_sem = n * (N_DEV - 1)
    srcs, lands = [], []
    for a, mode in items:
        blk = a.shape if mode == "gather" else a.shape[1:]
        srcs.append(pltpu.with_memory_space_constraint(a, pltpu.HBM))
        lands.append(pltpu.with_memory_space_constraint(lax.empty((N_DEV,) + tuple(blk), a.dtype), pltpu.HBM))

    def body(*refs):
        src_refs, land_refs = refs[:n], refs[n:2 * n]
        send_sems, recv_sems = refs[2 * n + 1], refs[2 * n + 2]
        token = refs[4 * n + 3]
        local_sems = refs[4 * n + 4]
        me, copies = _split_copies(items, src_refs, land_refs, send_sems, recv_sems)
        for i in range(n):
            own = src_refs[i] if items[i][1] == "gather" else src_refs[i].at[me]
            cp = pltpu.make_async_copy(own, land_refs[i].at[me], local_sems.at[i])
            cp.start()
            cp.wait()
        for cp in copies:
            cp.start()
        token[...] = jnp.zeros_like(token)

    out_shape = [pltpu.SemaphoreType.DMA((n_sem,)), pltpu.SemaphoreType.DMA((n_sem,))]
    out_shape += [pltpu.HBM(a.shape, a.dtype) for a in srcs] + [pltpu.HBM(a.shape, a.dtype) for a in lands]
    out_shape.append(jax.ShapeDtypeStruct((8, 128), F32))
    outs = pl.pallas_call(
        body, name=name, out_shape=out_shape,
        in_specs=[_HBM] * (2 * n) + [pl.BlockSpec(memory_space=pl.ANY)],
        out_specs=[_SEM, _SEM] + [_HBM] * (2 * n) + [pl.BlockSpec(memory_space=pltpu.VMEM)],
        input_output_aliases={i: 2 + i for i in range(2 * n)},
        scratch_shapes=[pltpu.SemaphoreType.DMA((n,))],
        compiler_params=pltpu.CompilerParams(has_side_effects=_EFFECT),
    )(*srcs, *lands, after)
    handle = (items, name, outs[0], outs[1], outs[2:2 + n], outs[2 + n:2 + 2 * n])
    return handle, outs[2 + 2 * n]


def _exchange_wait(handle, after):
    items, name, send_sems, recv_sems, srcs, lands = handle
    n = len(items)

    def body(*refs):
        src_refs, land_refs = refs[:n], refs[n:2 * n]
        send_ref, recv_ref = refs[2 * n], refs[2 * n + 1]
        _, copies = _split_copies(items, src_refs, land_refs, send_ref, recv_ref)
        for cp in copies:
            cp.wait_send()
            cp.wait_recv()

    outs = pl.pallas_call(
        body, name=name + "_wait",
        out_shape=[pltpu.HBM(a.shape, a.dtype) for a in srcs] + [pltpu.HBM(a.shape, a.dtype) for a in lands],
        in_specs=[_HBM] * (2 * n) + [_SEM, _SEM, pl.BlockSpec(memory_space=pl.ANY)], out_specs=[_HBM] * (2 * n),
        input_output_aliases={i: i for i in range(2 * n)},
        compiler_params=pltpu.CompilerParams(has_side_effects=_EFFECT),
    )(*srcs, *lands, send_sems, recv_sems, after)
    return outs[n:]


def _mod_fwd(cvec, w_mod_l, b_mod_l):
    rows, cols = cvec.shape[0], w_mod_l.shape[1]

    def body(c_ref, w_ref, b_ref, o_ref, s_ref):
        cv = c_ref[...]
        s = cv * _sigmoid(cv)
        s_ref[...] = s
        o_ref[...] = _dot(s, w_ref[...]) + b_ref[...]

    return pl.pallas_call(
        body, name="mod_fwd",
        out_shape=(jax.ShapeDtypeStruct((rows, cols), F32), jax.ShapeDtypeStruct((rows, D), F32)),
        in_specs=[_full((rows, D)), _full((D, cols)), _full((1, cols))],
        out_specs=(_full((rows, cols)), _full((rows, D))), grid=(1,),
        compiler_params=_params(("arbitrary",)),
    )(cvec, w_mod_l, b_mod_l)


def _mod_bwd(svec, cvec, dmod_l, w_mod_l):
    rows, cols = dmod_l.shape

    def body(s_ref, c_ref, d_ref, w_ref, gw_ref, gc_ref):
        gw_ref[...] = _dot(s_ref[...], d_ref[...], "tn")
        cv = c_ref[...]
        sg = _sigmoid(cv)
        gc_ref[...] = _dot(d_ref[...], w_ref[...], "nt") * (sg * (1.0 + cv * (1.0 - sg)))

    return pl.pallas_call(
        body, name="mod_bwd",
        out_shape=(jax.ShapeDtypeStruct((D, cols), F32), jax.ShapeDtypeStruct((rows, D), F32)),
        in_specs=[_full((rows, D)), _full((rows, D)), _full((rows, cols)), _full((D, cols))],
        out_specs=(_full((D, cols)), _full((rows, D))), grid=(1,),
        compiler_params=_params(("arbitrary",)),
    )(svec, cvec, dmod_l, w_mod_l)


def _inproj(xt, modv, g, w_inT, n_cols, rows_per_example, name):
    rows = xt.shape[0]
    tm = min(TOKEN_TILE, rows_per_example)
    per_b = rows_per_example // tm
    shared_mod = modv.shape[0] == 1

    def body(x_ref, mod_ref, g_ref, w_ref, p_ref, h_ref):
        x = x_ref[...]
        r = lax.rsqrt(jnp.mean(x * x, axis=-1, keepdims=True) + EPS)
        h = (x * r * g_ref[...]) * (1.0 + mod_ref[0, 1:2, :]) + mod_ref[0, 0:1, :]
        hb = h.astype(MXU_DTYPE)
        h_ref[...] = hb
        for j in range(n_cols // KW):
            p_ref[:, j * KW:(j + 1) * KW] = _dot(hb, w_ref[j * KW:(j + 1) * KW, :], "nt").astype(p_ref.dtype)

    mod_idx = (lambda i: (0, 0, 0)) if shared_mod else (lambda i: (i // per_b, 0, 0))
    return pl.pallas_call(
        body, name=name,
        out_shape=(jax.ShapeDtypeStruct((rows, n_cols), MXU_DTYPE), jax.ShapeDtypeStruct((rows, D), MXU_DTYPE)),
        grid=(rows // tm,),
        in_specs=[pl.BlockSpec((tm, D), lambda i: (i, 0)), pl.BlockSpec((1, N_MOD, D), mod_idx), _full((1, D)),
                  pl.BlockSpec((n_cols, D), lambda i: (0, 0), pipeline_mode=pl.Buffered(1))],
        out_specs=(pl.BlockSpec((tm, n_cols), lambda i: (i, 0)), pl.BlockSpec((tm, D), lambda i: (i, 0))),
        compiler_params=_params(("arbitrary",)),
    )(xt, modv, g, w_inT)


def _tri(reverse):
    row = lax.broadcasted_iota(jnp.int32, (CHUNK, CHUNK), 0)
    col = lax.broadcasted_iota(jnp.int32, (CHUNK, CHUNK), 1)
    return (col >= row) if reverse else (col <= row)


def _lower_bound(gam_ref, direction):
    return _sigmoid(gam_ref[direction:direction + 1, :] - gam_ref[2 + direction:3 + direction, :])


def _gate_prep(z, lb, tri_f):
    sg = _sigmoid(z)
    f = lb + (1.0 - lb) * sg
    g = jnp.log(f)
    b = _dotx(tri_f, g)
    bl = jnp.sum(g, axis=0, keepdims=True)
    return sg, f, 1.0 - f, b, bl


def _hgrn_fwd(p, gam, s0, rows_per_example, with_out, name):
    rows = p.shape[0]
    nb_ex = rows // rows_per_example
    rb = min(TOKEN_TILE, rows_per_example)
    cpb = rb // CHUNK
    nb = rows_per_example // rb
    n_chunks = rows // CHUNK
    has_s0 = s0 is not None

    def body(*refs):
        it = iter(refs)
        gam_ref = next(it)
        zf_ref, vf_ref = next(it), next(it)
        qf_ref = next(it) if with_out else None
        zb_ref, vb_ref = next(it), next(it)
        qb_ref = next(it) if with_out else None
        s0_ref = next(it) if has_s0 else None
        if with_out:
            of_ref, ob_ref = next(it), next(it)
        stash_f, stash_b, fin_ref = next(it), next(it), next(it)
        st_ref = next(it)
        i = pl.program_id(1)

        @pl.when(i == 0)
        def _():
            if has_s0:
                st_ref[...] = s0_ref[:, 0]
            else:
                st_ref[...] = jnp.zeros_like(st_ref)

        for direction, (z_ref, v_ref, q_ref, stash) in enumerate(
                ((zf_ref, vf_ref, qf_ref, stash_f), (zb_ref, vb_ref, qb_ref, stash_b))):
            reverse = direction == 1
            tri = _tri(reverse)
            tri_f = tri.astype(F32)
            lb = _lower_bound(gam_ref, direction)
            order = range(cpb - 1, -1, -1) if reverse else range(cpb)
            for j in order:
                rs = slice(j * CHUNK, (j + 1) * CHUNK)
                z = z_ref[rs, :].astype(F32)
                v = v_ref[rs, :].astype(F32)
                _, _, k, b, bl = _gate_prep(z, lb, tri_f)
                mid = 0.5 * bl
                kd = k * jnp.exp(bl - b)
                a = jnp.exp(bl)
                if with_out:
                    q = q_ref[rs, :].astype(F32)
                    qi = q * jnp.exp(b - mid)
                    ki = k * jnp.exp(mid - b)
                    qe = q * jnp.exp(b)
                for h in range(HEADS):
                    hs = slice(h * DK, (h + 1) * DK)
                    st = st_ref[direction, h]
                    stash[j, h] = st.astype(stash.dtype)
                    if with_out:
                        sc = jnp.where(tri, _dot(qi[:, hs], ki[:, hs], "nt"), 0.0)
                        o = _dot(sc, v[:, hs]) + _dot(qe[:, hs], st, "nt")
                        (ob_ref if reverse else of_ref)[rs, hs] = o
                    st_ref[direction, h] = st * a[:, hs] + _dot(v[:, hs], kd[:, hs], "tn")

        @pl.when(i == nb - 1)
        def _():
            fin_ref[:, 0] = st_ref[...]

    up = lambda b, i: b * nb + i
    down = lambda b, i: b * nb + nb - 1 - i
    col = lambda rowf, c: pl.BlockSpec((rb, KW), lambda b, i: (rowf(b, i), c))
    in_specs = [_full((4, KW)), col(up, 0), col(up, 2)] + ([col(up, 3)] if with_out else [])
    in_specs += [col(down, 1), col(down, 2)] + ([col(down, 3)] if with_out else [])
    args = [gam, p, p] + ([p] if with_out else []) + [p, p] + ([p] if with_out else [])
    if has_s0:
        in_specs.append(pl.BlockSpec((2, 1, HEADS, DK, DK), lambda b, i: (0, b, 0, 0, 0)))
        args.append(s0)
    out_shape, out_specs = [], []
    if with_out:
        out_shape += [jax.ShapeDtypeStruct((rows, KW), F32)] * 2
        out_specs += [pl.BlockSpec((rb, KW), lambda b, i: (up(b, i), 0)),
                      pl.BlockSpec((rb, KW), lambda b, i: (down(b, i), 0))]
    out_shape += [jax.ShapeDtypeStruct((n_chunks, HEADS, DK, DK), MXU_DTYPE)] * 2
    out_specs += [pl.BlockSpec((cpb, HEADS, DK, DK), lambda b, i: (up(b, i), 0, 0, 0)),
                  pl.BlockSpec((cpb, HEADS, DK, DK), lambda b, i: (down(b, i), 0, 0, 0))]
    out_shape.append(jax.ShapeDtypeStruct((2, nb_ex, HEADS, DK, DK), F32))
    out_specs.append(pl.BlockSpec((2, 1, HEADS, DK, DK), lambda b, i: (0, b, 0, 0, 0)))
    return pl.pallas_call(
        body, name=name, out_shape=out_shape, grid=(nb_ex, nb), in_specs=in_specs, out_specs=out_specs,
        scratch_shapes=[pltpu.VMEM((2, HEADS, DK, DK), F32)],
        compiler_params=_params(("arbitrary", "arbitrary")),
    )(*args)


def _hgrn_bwd(p, gam, do, stash_f, stash_b, ds_end, rows_per_example, with_out, name):
    rows = p.shape[0]
    nb_ex = rows // rows_per_example
    rb = min(TOKEN_TILE, rows_per_example)
    cpb = rb // CHUNK
    nb = rows_per_example // rb
    has_end = ds_end is not None

    def body(*refs):
        it = iter(refs)
        gam_ref = next(it)
        ins = []
        for _ in range(2):
            z_ref, v_ref = next(it), next(it)
            q_ref = next(it) if with_out else None
            do_ref = next(it) if with_out else None
            ins.append((z_ref, v_ref, q_ref, do_ref, next(it)))
        end_ref = next(it) if has_end else None
        outs = []
        for _ in range(2):
            dz_ref, dv_ref = next(it), next(it)
            dq_ref = next(it) if with_out else None
            outs.append((dz_ref, dv_ref, dq_ref))
        dlb_ref, ds0_ref = next(it), next(it)
        dst_ref = next(it)
        b_id, i = pl.program_id(0), pl.program_id(1)

        @pl.when(i == 0)
        def _():
            if has_end:
                dst_ref[...] = end_ref[:, 0]
            else:
                dst_ref[...] = jnp.zeros_like(dst_ref)

        @pl.when((i == 0) & (b_id == 0))
        def _():
            dlb_ref[...] = jnp.zeros_like(dlb_ref)

        for direction in range(2):
            z_ref, v_ref, q_ref, do_ref, stash = ins[direction]
            dz_ref, dv_ref, dq_ref = outs[direction]
            reverse = direction == 1
            tri = _tri(reverse)
            tri_f = tri.astype(F32)
            lb = _lower_bound(gam_ref, direction)
            order = range(cpb) if reverse else range(cpb - 1, -1, -1)
            dlb_acc = jnp.zeros((1, KW), F32)
            for j in order:
                rs = slice(j * CHUNK, (j + 1) * CHUNK)
                z = z_ref[rs, :].astype(F32)
                v = v_ref[rs, :].astype(F32)
                sg, f, k, b, bl = _gate_prep(z, lb, tri_f)
                mid = 0.5 * bl
                e3 = jnp.exp(bl - b)
                kd = k * e3
                a = jnp.exp(bl)
                if with_out:
                    q = q_ref[rs, :].astype(F32)
                    dout = do_ref[rs, :].astype(F32)
                    e1, e2, e4 = jnp.exp(b - mid), jnp.exp(mid - b), jnp.exp(b)
                    qi, ki, qe = q * e1, k * e2, q * e4
                dkd_p, dv_p, da_p, dqi_p, dki_p, dqe_p = [], [], [], [], [], []
                for h in range(HEADS):
                    hs = slice(h * DK, (h + 1) * DK)
                    st_in = stash[j, h]
                    dst = dst_ref[direction, h]
                    dkd_p.append(_dot(v[:, hs], dst))
                    dvh = _dot(kd[:, hs], dst, "nt")
                    da_p.append(jnp.sum(dst * st_in.astype(F32), axis=0, keepdims=True))
                    new_dst = dst * a[:, hs]
                    if with_out:
                        sc = jnp.where(tri, _dot(qi[:, hs], ki[:, hs], "nt"), 0.0)
                        dsc = jnp.where(tri, _dot(dout[:, hs], v[:, hs], "nt"), 0.0)
                        dqi_p.append(_dot(dsc, ki[:, hs]))
                        dki_p.append(_dot(dsc, qi[:, hs], "tn"))
                        dqe_p.append(_dot(dout[:, hs], st_in))
                        dvh = dvh + _dot(sc, dout[:, hs], "tn")
                        new_dst = new_dst + _dot(dout[:, hs], qe[:, hs], "tn")
                    dv_p.append(dvh)
                    dst_ref[direction, h] = new_dst
                cat = lambda parts: jnp.concatenate(parts, axis=1)
                dkd, da = cat(dkd_p), cat(da_p)
                dv_ref[rs, :] = cat(dv_p)
                t_kd = dkd * kd
                dk = dkd * e3
                db = -t_kd
                dbl = jnp.sum(t_kd, axis=0, keepdims=True) + da * a
                if with_out:
                    dqi, dki, dqe = cat(dqi_p), cat(dki_p), cat(dqe_p)
                    dq_ref[rs, :] = dqi * e1 + dqe * e4
                    dk = dk + dki * e2
                    t_qi, t_ki, t_qe = dqi * qi, dki * ki, dqe * qe
                    db = db + t_qi - t_ki + t_qe
                    dbl = dbl + 0.5 * jnp.sum(t_ki - t_qi, axis=0, keepdims=True)
                dg = _dotx(tri_f, db, "tn") + dbl
                df = dg / f - dk
                dz_ref[rs, :] = df * (1.0 - lb) * sg * (1.0 - sg)
                dlb_acc = dlb_acc + jnp.sum(df * (1.0 - sg), axis=0, keepdims=True)
            dlb_ref[direction:direction + 1, :] += dlb_acc

        @pl.when(i == nb - 1)
        def _():
            ds0_ref[:, 0] = dst_ref[...]

    rows_of = (lambda b, i: b * nb + nb - 1 - i, lambda b, i: b * nb + i)
    in_specs, args = [_full((4, KW))], [gam]
    for direction in range(2):
        rf = rows_of[direction]
        col = lambda c, rf=rf: pl.BlockSpec((rb, KW), lambda b, i: (rf(b, i), c))
        in_specs += [col(direction), col(2)]
        args += [p, p]
        if with_out:
            in_specs += [col(3), col(0)]
            args += [p, do]
        in_specs.append(pl.BlockSpec((cpb, HEADS, DK, DK), lambda b, i, rf=rf: (rf(b, i), 0, 0, 0)))
        args.append((stash_f, stash_b)[direction])
    if has_end:
        in_specs.append(pl.BlockSpec((2, 1, HEADS, DK, DK), lambda b, i: (0, b, 0, 0, 0)))
        args.append(ds_end)
    out_shape, out_specs = [], []
    for direction in range(2):
        rf = rows_of[direction]
        n_out = 3 if with_out else 2
        out_shape += [jax.ShapeDtypeStruct((rows, KW), F32)] * n_out
        out_specs += [pl.BlockSpec((rb, KW), lambda b, i, rf=rf: (rf(b, i), 0))] * n_out
    out_shape += [jax.ShapeDtypeStruct((2, KW), F32), jax.ShapeDtypeStruct((2, nb_ex, HEADS, DK, DK), F32)]
    out_specs += [_full((2, KW)), pl.BlockSpec((2, 1, HEADS, DK, DK), lambda b, i: (0, b, 0, 0, 0))]
    return pl.pallas_call(
        body, name=name, out_shape=out_shape, grid=(nb_ex, nb), in_specs=in_specs, out_specs=out_specs,
        scratch_shapes=[pltpu.VMEM((2, HEADS, DK, DK), F32)],
        compiler_params=_params(("arbitrary", "arbitrary")),
    )(*args)


def _tail_forward(osum, og, u, v, ga, gb, gna, ln_g, ln_b, ws_ref, bs_ref, wpaT_ref, wpbT_ref):
    tm = osum.shape[0]
    gna4 = jnp.concatenate([gna] * HEADS, axis=1)
    r_parts = []
    for h in range(HEADS):
        oh = osum[:, h * DK:(h + 1) * DK]
        r_parts.append(jnp.broadcast_to(lax.rsqrt(jnp.mean(oh * oh, axis=-1, keepdims=True) + EPS), (tm, DK)))
    r = jnp.concatenate(r_parts, axis=1)
    on = osum * r
    sg_og = _sigmoid(og)
    silu_og = og * sg_og
    oan = on * gna4
    oa = oan * silu_og
    ug, tu = _gelu(u)
    vg, tv = _gelu(v)
    mu = jnp.mean(vg, axis=-1, keepdims=True)
    vc = vg - mu
    rstd = lax.rsqrt(jnp.mean(vc * vc, axis=-1, keepdims=True) + EPS)
    vhat = vc * rstd
    vln = vhat * ln_g + ln_b
    blocks = []
    for n in range(tm // SGU_BLOCK):
        rs = slice(n * SGU_BLOCK, (n + 1) * SGU_BLOCK)
        blocks.append(jnp.concatenate(
            [_dot(ws_ref[g], vln[rs, g * DK:(g + 1) * DK]) + bs_ref[g] for g in range(GROUPS)], axis=1))
    mixed = jnp.concatenate(blocks, axis=0) if len(blocks) > 1 else blocks[0]
    obm = ug * mixed
    pa = _dot(oa, wpaT_ref[...], "nt")
    pb = _dot(obm, wpbT_ref[...], "nt")
    sga, sgb = _sigmoid(ga), _sigmoid(gb)
    merged = sga * pa + sgb * pb
    return dict(r=r, on=on, sg_og=sg_og, silu_og=silu_og, oan=oan, oa=oa, ug=ug, tu=tu, tv=tv, rstd=rstd, vhat=vhat,
                vln=vln, mixed=mixed, obm=obm, pa=pa, pb=pb, sga=sga, sgb=sgb, merged=merged, gna4=gna4)


def _tail_in_specs(tm):
    tile = lambda c: pl.BlockSpec((tm, KW), lambda i: (i, c))
    return [tile(c) for c in range(4, 11)]


def _tail_weight_specs():
    return [_full((1, DK)), _full((1, KW)), _full((1, KW)), _full((GROUPS, SGU_BLOCK, SGU_BLOCK)),
            _full((GROUPS, SGU_BLOCK, 1)), _full((D, KW), single=True), _full((D, KW), single=True),
            _full((D, D), single=True)]


def _read_tail_inputs(of_ref, ob_ref, pcols):
    osum = of_ref[...] + ob_ref[...]
    og, u, v = (pcols[j][...].astype(F32) for j in range(3))
    ga = jnp.concatenate([pcols[3][...], pcols[4][...]], axis=1).astype(F32)
    gb = jnp.concatenate([pcols[5][...], pcols[6][...]], axis=1).astype(F32)
    return osum, og, u, v, ga, gb


def _tail_fwd(p, o_up, o_down, xt, modv, gna, ln_g, ln_b, w_s, b_s, w_paT, w_pbT, w_o, rows_per_example):
    rows = xt.shape[0]
    tm = min(TOKEN_TILE, rows_per_example)
    per_b = rows_per_example // tm

    def body(of_ref, ob_ref, *rest):
        pcols = rest[:7]
        (x_ref, mod_ref, gna_ref, lng_ref, lnb_ref, ws_ref, bs_ref, wpaT_ref, wpbT_ref, wo_ref,
         x1_ref, mix_ref, merged_ref, oa_ref, obm_ref) = rest[7:]
        t = _tail_forward(*_read_tail_inputs(of_ref, ob_ref, pcols), gna_ref[...], lng_ref[...], lnb_ref[...],
                          ws_ref, bs_ref, wpaT_ref, wpbT_ref)
        mix = _dot(t["merged"], wo_ref[...])
        x1_ref[...] = x_ref[...] + mod_ref[0, 2:3, :] * mix
        mix_ref[...] = mix.astype(mix_ref.dtype)
        merged_ref[...] = t["merged"].astype(merged_ref.dtype)
        oa_ref[...] = t["oa"].astype(oa_ref.dtype)
        obm_ref[...] = t["obm"].astype(obm_ref.dtype)

    row = lambda w: pl.BlockSpec((tm, w), lambda i: (i, 0))
    in_specs = [row(KW), row(KW)] + _tail_in_specs(tm) + [row(D), pl.BlockSpec((1, N_MOD, D), lambda i: (i // per_b, 0, 0))]
    in_specs += _tail_weight_specs()
    return pl.pallas_call(
        body, name="tail_fwd", grid=(rows // tm,),
        out_shape=(jax.ShapeDtypeStruct((rows, D), F32), jax.ShapeDtypeStruct((rows, D), MXU_DTYPE),
                   jax.ShapeDtypeStruct((rows, D), MXU_DTYPE), jax.ShapeDtypeStruct((rows, KW), MXU_DTYPE),
                   jax.ShapeDtypeStruct((rows, KW), MXU_DTYPE)),
        in_specs=in_specs, out_specs=(row(D), row(D), row(D), row(KW), row(KW)),
        compiler_params=_params(("arbitrary",)),
    )(o_up, o_down, *([p] * 7), xt, modv, gna, ln_g, ln_b, w_s, b_s, w_paT, w_pbT, w_o)


def _tail_bwd(p, o_up, o_down, dx1, mix, modv, gna, ln_g, ln_b, w_s, b_s, w_paT, w_pbT, w_o, rows_per_example):
    rows = dx1.shape[0]
    nb_ex = rows // rows_per_example
    tm = min(TOKEN_TILE, rows_per_example)
    per_b = rows_per_example // tm

    def body(of_ref, ob_ref, *rest):
        pcols = rest[:7]
        (dx1_ref, mix_ref, mod_ref, gna_ref, lng_ref, lnb_ref, ws_ref, bs_ref, wpaT_ref, wpbT_ref, wo_ref,
         dpt_ref, do_ref, dmix_ref, dpa_ref, dpb_ref, dmod_ref, small_ref, dws_ref, dbs_ref) = rest[7:]
        i = pl.program_id(0)

        @pl.when(i == 0)
        def _():
            small_ref[...] = jnp.zeros_like(small_ref)
            dws_ref[...] = jnp.zeros_like(dws_ref)
            dbs_ref[...] = jnp.zeros_like(dbs_ref)

        @pl.when(i % per_b == 0)
        def _():
            dmod_ref[...] = jnp.zeros_like(dmod_ref)

        osum, og, u, v, ga, gb = _read_tail_inputs(of_ref, ob_ref, pcols)
        ln_g = lng_ref[...]
        t = _tail_forward(osum, og, u, v, ga, gb, gna_ref[...], ln_g, lnb_ref[...], ws_ref, bs_ref, wpaT_ref, wpbT_ref)
        dx1v = dx1_ref[...]
        dmod_ref[0, 2:3, :] += jnp.sum(dx1v * mix_ref[...].astype(F32), axis=0, keepdims=True)
        dmix = dx1v * mod_ref[0, 2:3, :]
        dmix_ref[...] = dmix.astype(dmix_ref.dtype)
        dmerged = _dot(dmix, wo_ref[...], "nt")
        sga, sgb = t["sga"], t["sgb"]
        dpa = dmerged * sga
        dpb = dmerged * sgb
        dpa_ref[...] = dpa.astype(dpa_ref.dtype)
        dpb_ref[...] = dpb.astype(dpb_ref.dtype)
        dga = dmerged * t["pa"] * sga * (1.0 - sga)
        dgb = dmerged * t["pb"] * sgb * (1.0 - sgb)
        doa = _dot(dpa, wpaT_ref[...])
        dobm = _dot(dpb, wpbT_ref[...])
        dug = dobm * t["mixed"]
        dmixed = dobm * t["ug"]
        du = dug * _gelu_grad(u, t["tu"])
        dvln_blocks = []
        for n in range(tm // SGU_BLOCK):
            rs = slice(n * SGU_BLOCK, (n + 1) * SGU_BLOCK)
            parts = []
            for g in range(GROUPS):
                gs = slice(g * DK, (g + 1) * DK)
                dm = dmixed[rs, gs]
                parts.append(_dot(ws_ref[g], dm, "tn"))
                dws_ref[g] += _dot(dm, t["vln"][rs, gs], "nt")
                dbs_ref[g] += jnp.sum(dm, axis=1, keepdims=True)
            dvln_blocks.append(jnp.concatenate(parts, axis=1))
        dvln = jnp.concatenate(dvln_blocks, axis=0) if len(dvln_blocks) > 1 else dvln_blocks[0]
        vhat = t["vhat"]
        small_ref[1:2, 0:KW] += jnp.sum(dvln * vhat, axis=0, keepdims=True)
        small_ref[2:3, 0:KW] += jnp.sum(dvln, axis=0, keepdims=True)
        dvhat = dvln * ln_g
        dvg = t["rstd"] * (dvhat - jnp.mean(dvhat, axis=-1, keepdims=True)
                           - vhat * jnp.mean(dvhat * vhat, axis=-1, keepdims=True))
        dv = dvg * _gelu_grad(v, t["tv"])
        sg_og = t["sg_og"]
        doan = doa * t["silu_og"]
        dog = doa * t["oan"] * (sg_og * (1.0 + og * (1.0 - sg_og)))
        prod = doan * t["on"]
        dgna = jnp.zeros((1, DK), F32)
        for h in range(HEADS):
            dgna = dgna + jnp.sum(prod[:, h * DK:(h + 1) * DK], axis=0, keepdims=True)
        small_ref[0:1, 0:DK] += dgna
        don = doan * t["gna4"]
        dot_parts = []
        for h in range(HEADS):
            hs = slice(h * DK, (h + 1) * DK)
            m = jnp.mean(don[:, hs] * t["on"][:, hs], axis=-1, keepdims=True)
            dot_parts.append(t["r"][:, hs] * (don[:, hs] - t["on"][:, hs] * m))
        do_ref[...] = jnp.concatenate(dot_parts, axis=1).astype(do_ref.dtype)
        for j, val in enumerate((dog, du, dv)):
            dpt_ref[:, j * KW:(j + 1) * KW] = val.astype(dpt_ref.dtype)
        dpt_ref[:, 3 * KW:3 * KW + D] = dga.astype(dpt_ref.dtype)
        dpt_ref[:, 3 * KW + D:] = dgb.astype(dpt_ref.dtype)

    row = lambda w: pl.BlockSpec((tm, w), lambda i: (i, 0))
    in_specs = [row(KW), row(KW)] + _tail_in_specs(tm) + [row(D), row(D), pl.BlockSpec((1, N_MOD, D), lambda i: (i // per_b, 0, 0))]
    in_specs += _tail_weight_specs()
    cd = MXU_DTYPE
    return pl.pallas_call(
        body, name="tail_bwd", grid=(rows // tm,),
        out_shape=(jax.ShapeDtypeStruct((rows, TAIL_COLS), cd), jax.ShapeDtypeStruct((rows, KW), cd),
                   jax.ShapeDtypeStruct((rows, D), cd), jax.ShapeDtypeStruct((rows, D), cd),
                   jax.ShapeDtypeStruct((rows, D), cd), jax.ShapeDtypeStruct((nb_ex, 8, D), F32),
                   jax.ShapeDtypeStruct((8, D), F32), jax.ShapeDtypeStruct((GROUPS, SGU_BLOCK, SGU_BLOCK), F32),
                   jax.ShapeDtypeStruct((GROUPS, SGU_BLOCK, 1), F32)),
        in_specs=in_specs,
        out_specs=(row(TAIL_COLS), row(KW), row(D), row(D), row(D),
                   pl.BlockSpec((1, 8, D), lambda i: (i // per_b, 0, 0)), _full((8, D)),
                   _full((GROUPS, SGU_BLOCK, SGU_BLOCK)), _full((GROUPS, SGU_BLOCK, 1))),
        compiler_params=_params(("arbitrary",)),
    )(o_up, o_down, *([p] * 7), dx1, mix, modv, gna, ln_g, ln_b, w_s, b_s, w_paT, w_pbT, w_o)


def _ffn(x1, target, modv, g_ffn, g_final, w_upT, w_down, rows_per_example):
    rows = x1.shape[0]
    nb_ex = rows // rows_per_example
    tm = min(TOKEN_TILE, rows_per_example)
    per_b = rows_per_example // tm
    n_ff = D_FF // FF_CHUNK

    def body(x1_ref, tgt_ref, mod_ref, gffn_ref, gfin_ref, wup_ref, wdn_ref,
             dx1_ref, h2_ref, dffn_ref, act_ref, dup_ref, dmod_ref, small_ref, a_scr, b_scr):
        i = pl.program_id(0)

        @pl.when(i == 0)
        def _():
            small_ref[...] = jnp.zeros_like(small_ref)

        @pl.when(i % per_b == 0)
        def _():
            dmod_ref[...] = jnp.zeros_like(dmod_ref)

        x1v = x1_ref[...]
        g2 = gffn_ref[...]
        m3, m4, m5 = mod_ref[0, 3:4, :], mod_ref[0, 4:5, :], mod_ref[0, 5:6, :]
        r2 = lax.rsqrt(jnp.mean(x1v * x1v, axis=-1, keepdims=True) + EPS)
        xn2 = x1v * r2
        h2 = (xn2 * g2) * (1.0 + m4) + m3
        h2b = h2.astype(MXU_DTYPE)
        h2_ref[...] = h2b
        ffn = jnp.zeros((tm, D), F32)
        for j in range(n_ff):
            cs = slice(j * FF_CHUNK, (j + 1) * FF_CHUNK)
            a = _dot(h2b, wup_ref[j * FF_CHUNK:(j + 1) * FF_CHUNK, :], "nt")
            bgate = _dot(h2b, wup_ref[D_FF + j * FF_CHUNK:D_FF + (j + 1) * FF_CHUNK, :], "nt")
            a_scr[:, cs] = a
            b_scr[:, cs] = bgate
            act = (a * _sigmoid(a) * bgate).astype(MXU_DTYPE)
            act_ref[:, cs] = act
            ffn = ffn + _dot(act, wdn_ref[cs, :])
        x2 = x1v + m5 * ffn
        r3 = lax.rsqrt(jnp.mean(x2 * x2, axis=-1, keepdims=True) + EPS)
        xn3 = x2 * r3
        gf = gfin_ref[...]
        err = xn3 * gf - tgt_ref[...]
        loss = 0.5 * jnp.sum(jnp.mean(err * err, axis=-1, keepdims=True), axis=0, keepdims=True)
        small_ref[2:3, :] += jnp.broadcast_to(loss, (1, D))
        dy = err * (1.0 / D)
        small_ref[1:2, :] += jnp.sum(dy * xn3, axis=0, keepdims=True)
        dxn3 = dy * gf
        dx2 = r3 * (dxn3 - xn3 * jnp.mean(dxn3 * xn3, axis=-1, keepdims=True))
        dmod_ref[0, 5:6, :] += jnp.sum(dx2 * ffn, axis=0, keepdims=True)
        dffn = (dx2 * m5).astype(MXU_DTYPE)
        dffn_ref[...] = dffn
        dh2 = jnp.zeros((tm, D), F32)
        for j in range(n_ff):
            cs = slice(j * FF_CHUNK, (j + 1) * FF_CHUNK)
            dact = _dot(dffn, wdn_ref[cs, :], "nt")
            a, bgate = a_scr[:, cs], b_scr[:, cs]
            s = _sigmoid(a)
            da = (dact * bgate * (s * (1.0 + a * (1.0 - s)))).astype(MXU_DTYPE)
            dbg = (dact * a * s).astype(MXU_DTYPE)
            dup_ref[:, cs] = da
            dup_ref[:, D_FF + j * FF_CHUNK:D_FF + (j + 1) * FF_CHUNK] = dbg
            dh2 = dh2 + _dot(da, wup_ref[j * FF_CHUNK:(j + 1) * FF_CHUNK, :])
            dh2 = dh2 + _dot(dbg, wup_ref[D_FF + j * FF_CHUNK:D_FF + (j + 1) * FF_CHUNK, :])
        dmod_ref[0, 3:4, :] += jnp.sum(dh2, axis=0, keepdims=True)
        dmod_ref[0, 4:5, :] += jnp.sum(dh2 * xn2 * g2, axis=0, keepdims=True)
        small_ref[0:1, :] += jnp.sum(dh2 * (1.0 + m4) * xn2, axis=0, keepdims=True)
        dxn2 = dh2 * g2 * (1.0 + m4)
        dx1_ref[...] = dx2 + r2 * (dxn2 - xn2 * jnp.mean(dxn2 * xn2, axis=-1, keepdims=True))

    row = lambda w: pl.BlockSpec((tm, w), lambda i: (i, 0))
    cd = MXU_DTYPE
    return pl.pallas_call(
        body, name="ffn_fwd_bwd", grid=(rows // tm,),
        out_shape=(jax.ShapeDtypeStruct((rows, D), F32), jax.ShapeDtypeStruct((rows, D), cd),
                   jax.ShapeDtypeStruct((rows, D), cd), jax.ShapeDtypeStruct((rows, D_FF), cd),
                   jax.ShapeDtypeStruct((rows, 2 * D_FF), cd), jax.ShapeDtypeStruct((nb_ex, 8, D), F32),
                   jax.ShapeDtypeStruct((8, D), F32)),
        in_specs=[row(D), row(D), pl.BlockSpec((1, N_MOD, D), lambda i: (i // per_b, 0, 0)), _full((1, D)), _full((1, D)),
                  _full((2 * D_FF, D), single=True), _full((D_FF, D), single=True)],
        out_specs=(row(D), row(D), row(D), row(D_FF), row(2 * D_FF),
                   pl.BlockSpec((1, 8, D), lambda i: (i // per_b, 0, 0)), _full((8, D))),
        scratch_shapes=[pltpu.VMEM((tm, D_FF), F32), pltpu.VMEM((tm, D_FF), F32)],
        compiler_params=_params(("arbitrary",)),
    )(x1, target, modv, g_ffn, g_final, w_upT, w_down)


def _inproj_bwd(pieces, dpt, xt, dx1, modv, g, w_inT, rows_per_example, name):
    rows = xt.shape[0]
    latent = dx1 is not None
    n_cols = IN_COLS if latent else CTX_COLS
    tm = min(TOKEN_TILE, rows_per_example)
    per_b = rows_per_example // tm
    n_mod_blocks = rows // rows_per_example if latent else 1
    n_pieces = len(pieces)

    def body(*refs):
        it = iter(refs)
        pc = [next(it) for _ in range(n_pieces)]
        dpt_ref = next(it) if latent else None
        x_ref = next(it)
        dx1_ref = next(it) if latent else None
        mod_ref, g_ref, w_ref = next(it), next(it), next(it)
        gx_ref = next(it) if latent else None
        dp_ref, dmod_ref, small_ref = next(it), next(it), next(it)
        i = pl.program_id(0)

        @pl.when(i == 0)
        def _():
            small_ref[...] = jnp.zeros_like(small_ref)

        @pl.when((i % per_b == 0) if latent else (i == 0))
        def _():
            dmod_ref[...] = jnp.zeros_like(dmod_ref)

        cols = [pc[0][...], pc[1][...], pc[2][...] + pc[3][...]]
        if latent:
            cols.append(pc[4][...] + pc[5][...])
        dh = jnp.zeros((tm, D), F32)
        for j, val in enumerate(cols):
            vb = val.astype(MXU_DTYPE)
            dp_ref[:, j * KW:(j + 1) * KW] = vb
            dh = dh + _dot(vb, w_ref[j * KW:(j + 1) * KW, :])
        if latent:
            for j in range(4, IN_COLS // KW):
                vb = dpt_ref[:, (j - 4) * KW:(j - 3) * KW]
                dp_ref[:, j * KW:(j + 1) * KW] = vb
                dh = dh + _dot(vb, w_ref[j * KW:(j + 1) * KW, :])
        x = x_ref[...]
        gv = g_ref[...]
        m1 = mod_ref[0, 1:2, :]
        r = lax.rsqrt(jnp.mean(x * x, axis=-1, keepdims=True) + EPS)
        xn = x * r
        dmod_ref[0, 0:1, :] += jnp.sum(dh, axis=0, keepdims=True)
        dmod_ref[0, 1:2, :] += jnp.sum(dh * xn * gv, axis=0, keepdims=True)
        small_ref[0:1, :] += jnp.sum(dh * (1.0 + m1) * xn, axis=0, keepdims=True)
        if latent:
            dxn = dh * gv * (1.0 + m1)
            gx_ref[...] = dx1_ref[...] + r * (dxn - xn * jnp.mean(dxn * xn, axis=-1, keepdims=True))

    row = lambda w: pl.BlockSpec((tm, w), lambda i: (i, 0))
    mod_idx = (lambda i: (i // per_b, 0, 0)) if latent else (lambda i: (0, 0, 0))
    in_specs = [row(KW)] * n_pieces + ([row(TAIL_COLS)] if latent else []) + [row(D)] + ([row(D)] if latent else [])
    in_specs += [pl.BlockSpec((1, N_MOD, D), mod_idx), _full((1, D)),
                 pl.BlockSpec((n_cols, D), lambda i: (0, 0), pipeline_mode=pl.Buffered(1))]
    args = list(pieces) + ([dpt] if latent else []) + [xt] + ([dx1] if latent else []) + [modv, g, w_inT]
    out_shape = ([jax.ShapeDtypeStruct((rows, D), F32)] if latent else []) + [
        jax.ShapeDtypeStruct((rows, n_cols), MXU_DTYPE), jax.ShapeDtypeStruct((n_mod_blocks, 8, D), F32),
        jax.ShapeDtypeStruct((8, D), F32)]
    out_specs = ([row(D)] if latent else []) + [row(n_cols), pl.BlockSpec((1, 8, D), mod_idx), _full((8, D))]
    return pl.pallas_call(
        body, name=name, grid=(rows // tm,), out_shape=out_shape, in_specs=in_specs, out_specs=out_specs,
        compiler_params=_params(("arbitrary",)),
    )(*args)


def _grad_matmul(a, b, name, init=None, tn=512, tt=1024):
    rows, n = a.shape
    k = b.shape[1]
    tn = min(tn, n)
    tt = min(tt, rows)
    steps = rows // tt
    has_init = init is not None

    def body(*refs):
        if has_init:
            a_ref, b_ref, init_ref, o_ref, acc = refs
        else:
            a_ref, b_ref, o_ref, acc = refs
        t = pl.program_id(1)

        @pl.when(t == 0)
        def _():
            acc[...] = init_ref[...].astype(F32) if has_init else jnp.zeros_like(acc)

        acc[...] += _dot(a_ref[...], b_ref[...], "tn")

        @pl.when(t == steps - 1)
        def _():
            o_ref[...] = acc[...].astype(o_ref.dtype)

    in_specs = [pl.BlockSpec((tt, tn), lambda i, t: (t, i)), pl.BlockSpec((tt, k), lambda i, t: (t, 0))]
    args = [a, b]
    if has_init:
        in_specs.append(pl.BlockSpec((tn, k), lambda i, t: (i, 0)))
        args.append(init)
    return pl.pallas_call(
        body, name=name, grid=(n // tn, steps), out_shape=jax.ShapeDtypeStruct((n, k), PAYLOAD_DTYPE),
        in_specs=in_specs, out_specs=pl.BlockSpec((tn, k), lambda i, t: (i, 0)),
        scratch_shapes=[pltpu.VMEM((tn, k), F32)],
        compiler_params=_params(("arbitrary", "arbitrary")),
    )(*args)


def _row_tile(rows, limit=256):
    if rows <= limit:
        return rows
    for t in range(limit, 7, -8):
        if rows % t == 0:
            return t
    return rows


def _sum8(stack, name):
    _, rows, cols = stack.shape
    tr = _row_tile(rows)

    def body(s_ref, o_ref):
        acc = s_ref[0].astype(F32)
        for j in range(1, N_DEV):
            acc = acc + s_ref[j].astype(F32)
        o_ref[...] = acc

    return pl.pallas_call(
        body, name=name, grid=(rows // tr,), out_shape=jax.ShapeDtypeStruct((rows, cols), F32),
        in_specs=[pl.BlockSpec((N_DEV, tr, cols), lambda i: (0, i, 0))],
        out_specs=pl.BlockSpec((tr, cols), lambda i: (i, 0)),
        compiler_params=_params(("arbitrary",)),
    )(stack)


def _small_reduce(stack, gam, nb_ex):
    def body(s_ref, gam_ref, o_ref, bm_ref):
        acc = s_ref[0]
        for j in range(1, N_DEV):
            acc = acc + s_ref[j]
        o_ref[...] = acc
        bm = acc[8:8 + N_MOD, :]
        for e in range(nb_ex):
            bm = bm + acc[16 + e * N_MOD:16 + (e + 1) * N_MOD, :]
        lb = jnp.concatenate([_lower_bound(gam_ref, 0), _lower_bound(gam_ref, 1)], axis=1)
        dgam = acc[7:8, :] * lb * (1.0 - lb)
        bm_ref[...] = jnp.concatenate([bm, dgam, -dgam], axis=0)

    return pl.pallas_call(
        body, name="small_reduce", grid=(1,),
        out_shape=(jax.ShapeDtypeStruct((SMALL_ROWS, D), F32), jax.ShapeDtypeStruct((8, D), F32)),
        in_specs=[_full((N_DEV, SMALL_ROWS, D)), _full((4, KW))], out_specs=(_full((SMALL_ROWS, D)), _full((8, D))),
        compiler_params=_params(("arbitrary",)),
    )(stack, gam)


def _adamw(w, g, m, v, name):
    shape = w.shape
    cols = shape[-1]
    rows = 1
    for s in shape[:-1]:
        rows *= s
    tr = _row_tile(rows)

    def body(w_ref, g_ref, m_ref, v_ref, d_ref, nm_ref, nv_ref):
        gv = g_ref[...]
        nm = ADAM_B1 * m_ref[...] + (1.0 - ADAM_B1) * gv
        nv = ADAM_B2 * v_ref[...] + (1.0 - ADAM_B2) * (gv * gv)
        m_hat = nm / (1.0 - ADAM_B1 ** ADAM_STEP)
        v_hat = nv / (1.0 - ADAM_B2 ** ADAM_STEP)
        d_ref[...] = -ADAM_LR * (m_hat / (jnp.sqrt(v_hat) + ADAM_EPS) + ADAM_WD * w_ref[...])
        nm_ref[...] = nm
        nv_ref[...] = nv

    blk = pl.BlockSpec((tr, cols), lambda i: (i, 0))
    sd = jax.ShapeDtypeStruct((rows, cols), F32)
    d, nm, nv = pl.pallas_call(
        body, name=name, grid=(rows // tr,), out_shape=(sd, sd, sd), in_specs=[blk] * 4, out_specs=(blk, blk, blk),
        compiler_params=_params(("arbitrary",)),
    )(w.reshape(rows, cols), g.reshape(rows, cols), m.reshape(rows, cols), v.reshape(rows, cols))
    return d.reshape(shape), nm.reshape(shape), nv.reshape(shape)


def _local_step(x, ctx, target, modv, mcv, gam, g_mix, g_ffn, gna, ln_g, ln_b, w_s, b_s, g_final,
                w_inT, late_weights, emit):
    nb_ex, seq, _ = x.shape
    ctx_len = ctx.shape[1]
    xt = x.reshape(nb_ex * seq, D)
    ct = ctx.reshape(nb_ex * ctx_len, D)
    tgt = target.reshape(nb_ex * seq, D)
    bs3 = b_s.reshape(GROUPS, SGU_BLOCK, 1)

    def behind(a, token):
        return a if token is None else a + token[0:1, 0:1]

    pc, hc = _inproj(ct, mcv, g_mix, w_inT, CTX_COLS, ctx_len, "inproj_ctx")
    p, h = _inproj(xt, modv, g_mix, w_inT, IN_COLS, seq, "inproj_lat")
    cst_f, cst_b, s_ctx = _hgrn_fwd(pc, gam, None, ctx_len, False, "hgrn_fwd_ctx")
    o_up, o_down, st_f, st_b, _ = _hgrn_fwd(p, gam, s_ctx, seq, True, "hgrn_fwd_lat")
    w_upT, w_down, w_o, w_paT, w_pbT = late_weights(o_up)
    x1, mix, merged, oa, obm = _tail_fwd(p, o_up, o_down, xt, modv, gna, ln_g, ln_b, w_s, bs3, w_paT, w_pbT, w_o, seq)
    dx1, h2, dffn, act, dup, dmod_ffn, small_ffn = _ffn(x1, tgt, modv, g_ffn, g_final, w_upT, w_down, seq)
    gw_upT = _grad_matmul(dup, h2, "gw_up")
    gw_down = _grad_matmul(act, dffn, "gw_down", tn=256)
    token = emit("ffn", (gw_upT, gw_down))
    dpt, do, dmix, dpa, dpb, dmod_tail, small_tail, dws, dbs = _tail_bwd(
        p, o_up, o_down, dx1, mix, modv, behind(gna, token), ln_g, ln_b, w_s, bs3, w_paT, w_pbT, w_o, seq)
    gw_o = _grad_matmul(merged, dmix, "gw_o")
    gw_paT = _grad_matmul(dpa, oa, "gw_pa")
    gw_pbT = _grad_matmul(dpb, obm, "gw_pb")
    token = emit("tail", (gw_o, gw_paT, gw_pbT))
    gam_b = behind(gam, token)
    dzf, dvf, dqf, dzb, dvb, dqb, dlb, ds0 = _hgrn_bwd(p, gam_b, do, st_f, st_b, None, seq, True, "hgrn_bwd_lat")
    czf, cvf, czb, cvb, dlb_c, _ = _hgrn_bwd(pc, gam_b, None, cst_f, cst_b, ds0, ctx_len, False, "hgrn_bwd_ctx")
    grad_x, dp, dmod_in, small_in = _inproj_bwd([dzf, dzb, dvf, dvb, dqf, dqb], dpt, xt, dx1, modv, g_mix, w_inT,
                                                 seq, "inproj_bwd_lat")
    dpc, dmc, small_c = _inproj_bwd([czf, czb, cvf, cvb], None, ct, None, mcv, g_mix, w_inT, ctx_len, "inproj_bwd_ctx")
    g_ctx = _grad_matmul(dpc, hc, "gw_in_ctx")
    g_ctx = jnp.pad(g_ctx, ((0, IN_COLS - CTX_COLS), (0, 0)))
    gw_inT = _grad_matmul(dp, h, "gw_in", init=g_ctx)

    z = lambda r: jnp.zeros((r, D), F32)
    pad = lambda a: jnp.pad(a, ((0, 0), (0, D - a.shape[1])))
    dlb_row = (dlb + dlb_c).reshape(1, 2 * KW)
    dmod = dmod_in + dmod_tail + dmod_ffn
    small = jnp.concatenate([
        small_in[0:1] + small_c[0:1],
        small_ffn[0:1],
        small_ffn[1:2],
        small_tail[0:1],
        small_tail[1:2],
        small_tail[2:3],
        pad(dbs.reshape(1, GROUPS * SGU_BLOCK)),
        dlb_row,
        dmc[0, 0:N_MOD],
        z(2),
        dmod[:, 0:N_MOD].reshape(nb_ex * N_MOD, D),
        z(24 - nb_ex * N_MOD),
        dws.reshape(GROUPS * SGU_BLOCK * SGU_BLOCK // D, D),
    ], axis=0)
    loss = small_ffn[2, 0]
    return loss, grad_x.reshape(x.shape), gw_inT, small


def kernel(x, c, ctx, c_ctx, w_mod, b_mod, g_mix, g_ffn, w_in, lb_gamma, g_norm_a, ln_v_g, ln_v_b, w_s, b_s, w_pa, w_pb, w_o, w_up, w_down, g_final, loss_target, m_c_ctx, m_w_mod, m_b_mod, m_g_mix, m_g_ffn, m_w_in, m_lb_gamma, m_g_norm_a, m_ln_v_g, m_ln_v_b, m_w_s, m_b_s, m_w_pa, m_w_pb, m_w_o, m_w_up, m_w_down, m_g_final, v_c_ctx, v_w_mod, v_b_mod, v_g_mix, v_g_ffn, v_w_in, v_lb_gamma, v_g_norm_a, v_ln_v_g, v_ln_v_b, v_w_s, v_b_s, v_w_pa, v_w_pb, v_w_o, v_w_up, v_w_down, v_g_final):
    nb_ex = x.shape[0]
    me = 4 * lax.axis_index("x") + 2 * lax.axis_index("y") + lax.axis_index("c")
    cd = MXU_DTYPE
    mod_cols = w_mod.shape[2]
    lb_cols = lb_gamma.shape[2]

    w_inT_l = w_in[0].T.astype(cd)
    w_upT_l = w_up[0].T.astype(cd)
    w_paT_l = w_pa[0].T.astype(cd)
    w_pbT_l = w_pb[0].T.astype(cd)
    cl = jnp.concatenate([c, jnp.pad(lb_gamma.reshape(1, 4 * lb_cols), ((0, 0), (0, D - 4 * lb_cols))),
                          jnp.zeros((8 - nb_ex - 1, D), F32)], axis=0)
    g_in, g_cl = _exchange([(w_inT_l, "gather"), (cl, "gather")], "gather_w_in")
    w_inT = g_in.reshape(IN_COLS, D)
    c_all = g_cl[:, 0:nb_ex].reshape(N_DEV * nb_ex, D)
    gam = jnp.transpose(g_cl[:, nb_ex, 0:4 * lb_cols].reshape(N_DEV, 4, lb_cols), (1, 0, 2)).reshape(4, KW)
    late, token = _exchange_start(
        [(w_upT_l, "gather"), (w_down[0].astype(cd), "gather"), (w_o[0].astype(cd), "gather"), (w_paT_l, "gather"),
         (w_pbT_l, "gather")], "gather_late", after=g_cl)

    def late_weights(after):
        g_up, g_down, g_o, g_pa, g_pb = _exchange_wait(late, after)
        return (g_up.reshape(2 * D_FF, D), g_down.reshape(D_FF, D), g_o.reshape(D, D), g_pa.reshape(D, KW),
                g_pb.reshape(D, KW))

    n_c = N_DEV * nb_ex
    cvec = jnp.concatenate([c_all, c_ctx.reshape(1, D), jnp.zeros((7, D), F32)], axis=0) + token[0:1, 0:1]
    b_mod_l = lax.dynamic_slice(b_mod, (0, me * mod_cols), (1, mod_cols))
    mod_l, svec = _mod_fwd(cvec, w_mod[0], b_mod_l)
    (g_mod,) = _exchange([(mod_l, "gather")], "gather_mod")
    mod_all = jnp.transpose(g_mod, (1, 0, 2)).reshape(n_c + 8, N_MOD * D)
    modv = lax.dynamic_slice(mod_all, (me * nb_ex, 0), (nb_ex, N_MOD * D)).reshape(nb_ex, N_MOD, D)
    mcv = mod_all[n_c].reshape(1, N_MOD, D)

    blocks = lambda a: a.reshape(N_DEV, a.shape[0] // N_DEV, a.shape[1])
    in_flight = {}

    def emit(stage, grads):
        in_flight[stage], tok = _exchange_start([(blocks(g), "scatter") for g in grads], "scatter_" + stage,
                                                after=grads[-1])
        return tok

    loss_l, grad_x, gw_inT, small = _local_step(
        x, ctx, loss_target, modv, mcv, gam, g_mix, g_ffn, g_norm_a, ln_v_g, ln_v_b, w_s[0], b_s[0],
        g_final.reshape(1, D), w_inT, late_weights, emit)
    loss = lax.psum(loss_l, ("x", "y", "c"))
    last, token = _exchange_start([(blocks(gw_inT), "scatter"), (small, "gather")], "scatter_in", after=small)

    r_up, r_down = _exchange_wait(in_flight["ffn"], token)
    r_o, r_pa, r_pb = _exchange_wait(in_flight["tail"], token)
    grad_w_up = _sum8(r_up, "sum_w_up").T[None]
    grad_w_down = _sum8(r_down, "sum_w_down")[None]
    grad_w_o = _sum8(r_o, "sum_w_o")[None]
    grad_w_pa = _sum8(r_pa, "sum_w_pa").T[None]
    grad_w_pb = _sum8(r_pb, "sum_w_pb").T[None]
    early = {"w_up": (w_up, grad_w_up, m_w_up, v_w_up), "w_down": (w_down, grad_w_down, m_w_down, v_w_down),
             "w_o": (w_o, grad_w_o, m_w_o, v_w_o), "w_pa": (w_pa, grad_w_pa, m_w_pa, v_w_pa),
             "w_pb": (w_pb, grad_w_pb, m_w_pb, v_w_pb)}
    early_out = {nm: _adamw(*args, "adamw_" + nm) for nm, args in early.items()}

    r_in, r_small = _exchange_wait(last, early_out["w_up"][0])
    grad_w_in = _sum8(r_in, "sum_w_in").T[None]
    tot, bm = _small_reduce(r_small, gam, nb_ex)
    grad_g_mix, grad_g_ffn, grad_g_final = tot[0:1], tot[1:2], tot[2]
    grad_g_norm_a = tot[3:4, 0:DK]
    grad_ln_v_g, grad_ln_v_b = tot[4:5, 0:KW], tot[5:6, 0:KW]
    grad_b_s = tot[6, 0:GROUPS * SGU_BLOCK].reshape(1, GROUPS, SGU_BLOCK)
    grad_w_s = tot[40:104].reshape(1, GROUPS, SGU_BLOCK, SGU_BLOCK)
    grad_b_mod = bm[0:N_MOD].reshape(1, N_MOD * D)
    grad_lb_gamma = lax.dynamic_slice(bm[6:8].reshape(2, 2, KW), (0, 0, me * lb_cols), (2, 2, lb_cols))

    dmod_all = r_small[:, 16:16 + nb_ex * N_MOD].reshape(n_c, N_MOD * D)
    dmod_l = jnp.concatenate([lax.dynamic_slice(dmod_all, (0, me * mod_cols), (n_c, mod_cols)),
                              lax.dynamic_slice(tot[8:8 + N_MOD].reshape(1, N_MOD * D), (0, me * mod_cols), (1, mod_cols)),
                              jnp.zeros((7, mod_cols), F32)], axis=0)
    gw_mod, gc = _mod_bwd(svec, cvec, dmod_l, w_mod[0])
    grad_w_mod = gw_mod[None]
    (r_gc,) = _exchange([(gc[n_c:n_c + 8], "gather")], "gather_c_ctx")
    grad_c_ctx = _sum8(r_gc, "sum_c_ctx")[0]

    names = ["c_ctx", "w_mod", "b_mod", "g_mix", "g_ffn", "w_in", "lb_gamma", "g_norm_a", "ln_v_g", "ln_v_b", "w_s",
             "b_s", "w_pa", "w_pb", "w_o", "w_up", "w_down", "g_final"]
    weights = [c_ctx, w_mod, b_mod, g_mix, g_ffn, w_in, lb_gamma, g_norm_a, ln_v_g, ln_v_b, w_s, b_s, w_pa, w_pb, w_o,
               w_up, w_down, g_final]
    grads = [grad_c_ctx, grad_w_mod, grad_b_mod, grad_g_mix, grad_g_ffn, grad_w_in, grad_lb_gamma, grad_g_norm_a,
             grad_ln_v_g, grad_ln_v_b, grad_w_s, grad_b_s, grad_w_pa, grad_w_pb, grad_w_o, grad_w_up, grad_w_down,
             grad_g_final]
    ms = [m_c_ctx, m_w_mod, m_b_mod, m_g_mix, m_g_ffn, m_w_in, m_lb_gamma, m_g_norm_a, m_ln_v_g, m_ln_v_b, m_w_s, m_b_s,
          m_w_pa, m_w_pb, m_w_o, m_w_up, m_w_down, m_g_final]
    vs = [v_c_ctx, v_w_mod, v_b_mod, v_g_mix, v_g_ffn, v_w_in, v_lb_gamma, v_g_norm_a, v_ln_v_g, v_ln_v_b, v_w_s, v_b_s,
          v_w_pa, v_w_pb, v_w_o, v_w_up, v_w_down, v_g_final]
    deltas, new_ms, new_vs = [], [], []
    for nm, w, g, m, v in zip(names, weights, grads, ms, vs):
        d, nm_, nv_ = early_out[nm] if nm in early_out else _adamw(w, g.reshape(w.shape), m, v, "adamw_" + nm)
        deltas.append(d)
        new_ms.append(nm_)
        new_vs.append(nv_)
    grads = [g.reshape(w.shape) for g, w in zip(grads, weights)]
    return (loss, grad_x, *grads, *deltas, *new_ms, *new_vs)
```

```python
import functools

import jax
import jax.numpy as jnp
from jax import lax
from jax.experimental import pallas as pl
from jax.experimental.pallas import tpu as pltpu

F32 = jnp.float32
MXU_DTYPE = jnp.bfloat16
PAYLOAD_DTYPE = jnp.bfloat16

N_DEV = 8
D = 1024
HEADS = 4
DK = 128
KW = HEADS * DK
CHUNK = 64
SGU_BLOCK = 128
GROUPS = 4
D_FF = 2816
FF_CHUNK = 256
N_MOD = 6
IN_COLS = 5632
CTX_COLS = 1536
TAIL_COLS = IN_COLS - 4 * KW
EPS = 1e-6
ADAM_LR, ADAM_B1, ADAM_B2, ADAM_EPS, ADAM_WD, ADAM_STEP = 0.001, 0.9, 0.999, 1e-08, 0.01, 10

VMEM_LIMIT = 56 * 1024 * 1024
TOKEN_TILE = 256
SMALL_ROWS = 104


def _params(sem):
    return pltpu.CompilerParams(dimension_semantics=sem, vmem_limit_bytes=VMEM_LIMIT)


_DN = {"nn": (((1,), (0,)), ((), ())), "nt": (((1,), (1,)), ((), ())), "tn": (((0,), (0,)), ((), ()))}


def _dot(a, b, form="nn"):
    return lax.dot_general(a.astype(MXU_DTYPE), b.astype(MXU_DTYPE), _DN[form], preferred_element_type=F32)


def _dotx(a, b, form="nn"):
    return lax.dot_general(a.astype(F32), b.astype(F32), _DN[form], preferred_element_type=F32,
                           precision=lax.Precision.HIGHEST)


def _full(shape, single=False):
    n = len(shape)
    if single:
        return pl.BlockSpec(shape, lambda *_: (0,) * n, pipeline_mode=pl.Buffered(1))
    return pl.BlockSpec(shape, lambda *_: (0,) * n)


def _ordered_behind(body, in_specs, args, after):
    if after is None:
        return body
    at = len(in_specs)
    in_specs.append(pl.BlockSpec(memory_space=pl.ANY))
    args.append(after)
    return lambda *refs: body(*refs[:at], *refs[at + 1:])


def _sigmoid(z):
    return 1.0 / (1.0 + jnp.exp(-z))


def _gelu(x):
    c = 0.7978845608028654
    t = jnp.tanh(c * (x + 0.044715 * x * x * x))
    return 0.5 * x * (1.0 + t), t


def _gelu_grad(x, t):
    c = 0.7978845608028654
    return 0.5 * (1.0 + t) + 0.5 * x * (1.0 - t * t) * c * (1.0 + 3 * 0.044715 * x * x)


def _exchange(items, name):
    n = len(items)
    out_shape = []
    for a, mode in items:
        blk = a.shape if mode == "gather" else a.shape[1:]
        out_shape.append(jax.ShapeDtypeStruct((N_DEV,) + tuple(blk), a.dtype))

    def body(*refs):
        srcs, dsts = refs[:n], refs[n:2 * n]
        send_sems, recv_sems, local_sems = refs[2 * n:]
        x, y, c = lax.axis_index("x"), lax.axis_index("y"), lax.axis_index("c")
        me = 4 * x + 2 * y + c

        def src_for(i, dev):
            return srcs[i] if items[i][1] == "gather" else srcs[i].at[dev]

        local = [pltpu.make_async_copy(src_for(i, me), dsts[i].at[me], local_sems.at[i]) for i in range(n)]
        for cp in local:
            cp.start()
        remote = []
        for k in range(1, N_DEV):
            px = jnp.bitwise_xor(x, (k >> 2) & 1)
            py = jnp.bitwise_xor(y, (k >> 1) & 1)
            pc = jnp.bitwise_xor(c, k & 1)
            peer = 4 * px + 2 * py + pc
            for i in range(n):
                cp = pltpu.make_async_remote_copy(
                    src_ref=src_for(i, peer), dst_ref=dsts[i].at[me],
                    send_sem=send_sems.at[i * (N_DEV - 1) + k - 1], recv_sem=recv_sems.at[i * (N_DEV - 1) + k - 1],
                    device_id=(px, py, pc), device_id_type=pl.DeviceIdType.MESH)
                cp.start()
                remote.append(cp)
        for cp in remote:
            cp.wait()
        for cp in local:
            cp.wait()

    any_spec = pl.BlockSpec(memory_space=pl.ANY)
    return pl.pallas_call(
        body, name=name, out_shape=out_shape,
        in_specs=[any_spec] * n, out_specs=[any_spec] * n,
        scratch_shapes=[pltpu.SemaphoreType.DMA((n * (N_DEV - 1),)), pltpu.SemaphoreType.DMA((n * (N_DEV - 1),)),
                        pltpu.SemaphoreType.DMA((n,))],
    )(*[a for a, _ in items])


_HBM = pl.BlockSpec(memory_space=pltpu.HBM)
_SEM = pl.BlockSpec(memory_space=pltpu.SEMAPHORE)
_EFFECT = pltpu.SideEffectType.DATAFLOW_SIDE_EFFECTING


def _split_copies(items, srcs, lands, send_sems, recv_sems):
    x, y, c = lax.axis_index("x"), lax.axis_index("y"), lax.axis_index("c")
    me = 4 * x + 2 * y + c
    copies = []
    for k in range(1, N_DEV):
        px = jnp.bitwise_xor(x, (k >> 2) & 1)
        py = jnp.bitwise_xor(y, (k >> 1) & 1)
        pc = jnp.bitwise_xor(c, k & 1)
        peer = 4 * px + 2 * py + pc
        for i in range(len(items)):
            src = srcs[i] if items[i][1] == "gather" else srcs[i].at[peer]
            copies.append(pltpu.make_async_remote_copy(
                src_ref=src, dst_ref=lands[i].at[me],
                send_sem=send_sems.at[i * (N_DEV - 1) + k - 1], recv_sem=recv_sems.at[i * (N_DEV - 1) + k - 1],
                device_id=(px, py, pc), device_id_type=pl.DeviceIdType.MESH))
    return me, copies


def _exchange_start(items, name, after):
    n = len(items)
    n_sem = n * (N_DEV - 1)
    srcs, lands = [], []
    for a, mode in items:
        blk = a.shape if mode == "gather" else a.shape[1:]
        srcs.append(pltpu.with_memory_space_constraint(a, pltpu.HBM))
        lands.append(pltpu.with_memory_space_constraint(lax.empty((N_DEV,) + tuple(blk), a.dtype), pltpu.HBM))

    def body(*refs):
        src_refs, land_refs = refs[:n], refs[n:2 * n]
        send_sems, recv_sems = refs[2 * n + 1], refs[2 * n + 2]
        local_sems = refs[4 * n + 3]
        me, copies = _split_copies(items, src_refs, land_refs, send_sems, recv_sems)
        for i in range(n):
            own = src_refs[i] if items[i][1] == "gather" else src_refs[i].at[me]
            cp = pltpu.make_async_copy(own, land_refs[i].at[me], local_sems.at[i])
            cp.start()
            cp.wait()
        for cp in copies:
            cp.start()

    out_shape = [pltpu.SemaphoreType.DMA((n_sem,)), pltpu.SemaphoreType.DMA((n_sem,))]
    out_shape += [pltpu.HBM(a.shape, a.dtype) for a in srcs] + [pltpu.HBM(a.shape, a.dtype) for a in lands]
    outs = pl.pallas_call(
        body, name=name, out_shape=out_shape,
        in_specs=[_HBM] * (2 * n) + [pl.BlockSpec(memory_space=pl.ANY)],
        out_specs=[_SEM, _SEM] + [_HBM] * (2 * n),
        input_output_aliases={i: 2 + i for i in range(2 * n)},
        scratch_shapes=[pltpu.SemaphoreType.DMA((n,))],
        compiler_params=pltpu.CompilerParams(has_side_effects=_EFFECT),
    )(*srcs, *lands, after)
    handle = (items, name, outs[0], outs[1], outs[2:2 + n], outs[2 + n:2 + 2 * n])
    return handle, outs[2]


def _exchange_wait(handle, after):
    items, name, send_sems, recv_sems, srcs, lands = handle
    n = len(items)

    def body(*refs):
        src_refs, land_refs = refs[:n], refs[n:2 * n]
        send_ref, recv_ref = refs[2 * n], refs[2 * n + 1]
        _, copies = _split_copies(items, src_refs, land_refs, send_ref, recv_ref)
        for cp in copies:
            cp.wait_send()
            cp.wait_recv()

    outs = pl.pallas_call(
        body, name=name + "_wait",
        out_shape=[pltpu.HBM(a.shape, a.dtype) for a in srcs] + [pltpu.HBM(a.shape, a.dtype) for a in lands],
        in_specs=[_HBM] * (2 * n) + [_SEM, _SEM, pl.BlockSpec(memory_space=pl.ANY)], out_specs=[_HBM] * (2 * n),
        input_output_aliases={i: i for i in range(2 * n)},
        compiler_params=pltpu.CompilerParams(has_side_effects=_EFFECT),
    )(*srcs, *lands, send_sems, recv_sems, after)
    return outs[n:]


def _mod_fwd(cvec, w_mod_l, b_mod_l):
    rows, cols = cvec.shape[0], w_mod_l.shape[1]

    def body(c_ref, w_ref, b_ref, o_ref, s_ref):
        cv = c_ref[...]
        s = cv * _sigmoid(cv)
        s_ref[...] = s
        o_ref[...] = _dot(s, w_ref[...]) + b_ref[...]

    return pl.pallas_call(
        body, name="mod_fwd",
        out_shape=(jax.ShapeDtypeStruct((rows, cols), F32), jax.ShapeDtypeStruct((rows, D), F32)),
        in_specs=[_full((rows, D)), _full((D, cols)), _full((1, cols))],
        out_specs=(_full((rows, cols)), _full((rows, D))), grid=(1,),
        compiler_params=_params(("arbitrary",)),
    )(cvec, w_mod_l, b_mod_l)


def _mod_bwd(svec, cvec, dmod_l, w_mod_l):
    rows, cols = dmod_l.shape

    def body(s_ref, c_ref, d_ref, w_ref, gw_ref, gc_ref):
        gw_ref[...] = _dot(s_ref[...], d_ref[...], "tn")
        cv = c_ref[...]
        sg = _sigmoid(cv)
        gc_ref[...] = _dot(d_ref[...], w_ref[...], "nt") * (sg * (1.0 + cv * (1.0 - sg)))

    return pl.pallas_call(
        body, name="mod_bwd",
        out_shape=(jax.ShapeDtypeStruct((D, cols), F32), jax.ShapeDtypeStruct((rows, D), F32)),
        in_specs=[_full((rows, D)), _full((rows, D)), _full((rows, cols)), _full((D, cols))],
        out_specs=(_full((D, cols)), _full((rows, D))), grid=(1,),
        compiler_params=_params(("arbitrary",)),
    )(svec, cvec, dmod_l, w_mod_l)


def _inproj(xt, modv, g, w_inT, n_cols, rows_per_example, name, after=None):
    rows = xt.shape[0]
    tm = min(TOKEN_TILE, rows_per_example)
    per_b = rows_per_example // tm
    shared_mod = modv.shape[0] == 1

    def body(x_ref, mod_ref, g_ref, w_ref, p_ref, h_ref):
        x = x_ref[...]
        r = lax.rsqrt(jnp.mean(x * x, axis=-1, keepdims=True) + EPS)
        h = (x * r * g_ref[...]) * (1.0 + mod_ref[0, 1:2, :]) + mod_ref[0, 0:1, :]
        hb = h.astype(MXU_DTYPE)
        h_ref[...] = hb
        for j in range(n_cols // KW):
            p_ref[:, j * KW:(j + 1) * KW] = _dot(hb, w_ref[j * KW:(j + 1) * KW, :], "nt").astype(p_ref.dtype)

    mod_idx = (lambda i: (0, 0, 0)) if shared_mod else (lambda i: (i // per_b, 0, 0))
    in_specs = [pl.BlockSpec((tm, D), lambda i: (i, 0)), pl.BlockSpec((1, N_MOD, D), mod_idx), _full((1, D)),
                pl.BlockSpec((n_cols, D), lambda i: (0, 0), pipeline_mode=pl.Buffered(1))]
    args = [xt, modv, g, w_inT]
    body = _ordered_behind(body, in_specs, args, after)
    return pl.pallas_call(
        body, name=name,
        out_shape=(jax.ShapeDtypeStruct((rows, n_cols), MXU_DTYPE), jax.ShapeDtypeStruct((rows, D), MXU_DTYPE)),
        grid=(rows // tm,), in_specs=in_specs,
        out_specs=(pl.BlockSpec((tm, n_cols), lambda i: (i, 0)), pl.BlockSpec((tm, D), lambda i: (i, 0))),
        compiler_params=_params(("arbitrary",)),
    )(*args)


def _tri(reverse):
    row = lax.broadcasted_iota(jnp.int32, (CHUNK, CHUNK), 0)
    col = lax.broadcasted_iota(jnp.int32, (CHUNK, CHUNK), 1)
    return (col >= row) if reverse else (col <= row)


def _lower_bound(gam_ref, direction):
    return _sigmoid(gam_ref[direction:direction + 1, :] - gam_ref[2 + direction:3 + direction, :])


def _gate_prep(z, lb, tri_f):
    sg = _sigmoid(z)
    f = lb + (1.0 - lb) * sg
    g = jnp.log(f)
    b = _dotx(tri_f, g)
    bl = jnp.sum(g, axis=0, keepdims=True)
    return sg, f, 1.0 - f, b, bl


def _hgrn_fwd(p, gam, s0, rows_per_example, with_out, name):
    rows = p.shape[0]
    nb_ex = rows // rows_per_example
    rb = min(TOKEN_TILE, rows_per_example)
    cpb = rb // CHUNK
    nb = rows_per_example // rb
    n_chunks = rows // CHUNK
    has_s0 = s0 is not None

    def body(*refs):
        it = iter(refs)
        gam_ref = next(it)
        zf_ref, vf_ref = next(it), next(it)
        qf_ref = next(it) if with_out else None
        zb_ref, vb_ref = next(it), next(it)
        qb_ref = next(it) if with_out else None
        s0_ref = next(it) if has_s0 else None
        if with_out:
            of_ref, ob_ref = next(it), next(it)
        stash_f, stash_b, fin_ref = next(it), next(it), next(it)
        st_ref = next(it)
        i = pl.program_id(1)

        @pl.when(i == 0)
        def _():
            if has_s0:
                st_ref[...] = s0_ref[:, 0]
            else:
                st_ref[...] = jnp.zeros_like(st_ref)

        for direction, (z_ref, v_ref, q_ref, stash) in enumerate(
                ((zf_ref, vf_ref, qf_ref, stash_f), (zb_ref, vb_ref, qb_ref, stash_b))):
            reverse = direction == 1
            tri = _tri(reverse)
            tri_f = tri.astype(F32)
            lb = _lower_bound(gam_ref, direction)
            order = range(cpb - 1, -1, -1) if reverse else range(cpb)
            for j in order:
                rs = slice(j * CHUNK, (j + 1) * CHUNK)
                z = z_ref[rs, :].astype(F32)
                v = v_ref[rs, :].astype(F32)
                _, _, k, b, bl = _gate_prep(z, lb, tri_f)
                mid = 0.5 * bl
                kd = k * jnp.exp(bl - b)
                a = jnp.exp(bl)
                if with_out:
                    q = q_ref[rs, :].astype(F32)
                    qi = q * jnp.exp(b - mid)
                    ki = k * jnp.exp(mid - b)
                    qe = q * jnp.exp(b)
                for h in range(HEADS):
                    hs = slice(h * DK, (h + 1) * DK)
                    st = st_ref[direction, h]
                    stash[j, h] = st.astype(stash.dtype)
                    if with_out:
                        sc = jnp.where(tri, _dot(qi[:, hs], ki[:, hs], "nt"), 0.0)
                        o = _dot(sc, v[:, hs]) + _dot(qe[:, hs], st, "nt")
                        (ob_ref if reverse else of_ref)[rs, hs] = o
                    st_ref[direction, h] = st * a[:, hs] + _dot(v[:, hs], kd[:, hs], "tn")

        @pl.when(i == nb - 1)
        def _():
            fin_ref[:, 0] = st_ref[...]

    up = lambda b, i: b * nb + i
    down = lambda b, i: b * nb + nb - 1 - i
    col = lambda rowf, c: pl.BlockSpec((rb, KW), lambda b, i: (rowf(b, i), c))
    in_specs = [_full((4, KW)), col(up, 0), col(up, 2)] + ([col(up, 3)] if with_out else [])
    in_specs += [col(down, 1), col(down, 2)] + ([col(down, 3)] if with_out else [])
    args = [gam, p, p] + ([p] if with_out else []) + [p, p] + ([p] if with_out else [])
    if has_s0:
        in_specs.append(pl.BlockSpec((2, 1, HEADS, DK, DK), lambda b, i: (0, b, 0, 0, 0)))
        args.append(s0)
    out_shape, out_specs = [], []
    if with_out:
        out_shape += [jax.ShapeDtypeStruct((rows, KW), F32)] * 2
        out_specs += [pl.BlockSpec((rb, KW), lambda b, i: (up(b, i), 0)),
                      pl.BlockSpec((rb, KW), lambda b, i: (down(b, i), 0))]
    out_shape += [jax.ShapeDtypeStruct((n_chunks, HEADS, DK, DK), MXU_DTYPE)] * 2
    out_specs += [pl.BlockSpec((cpb, HEADS, DK, DK), lambda b, i: (up(b, i), 0, 0, 0)),
                  pl.BlockSpec((cpb, HEADS, DK, DK), lambda b, i: (down(b, i), 0, 0, 0))]
    out_shape.append(jax.ShapeDtypeStruct((2, nb_ex, HEADS, DK, DK), F32))
    out_specs.append(pl.BlockSpec((2, 1, HEADS, DK, DK), lambda b, i: (0, b, 0, 0, 0)))
    return pl.pallas_call(
        body, name=name, out_shape=out_shape, grid=(nb_ex, nb), in_specs=in_specs, out_specs=out_specs,
        scratch_shapes=[pltpu.VMEM((2, HEADS, DK, DK), F32)],
        compiler_params=_params(("arbitrary", "arbitrary")),
    )(*args)


def _hgrn_bwd(p, gam, do, stash_f, stash_b, ds_end, rows_per_example, with_out, name, after=None):
    rows = p.shape[0]
    nb_ex = rows // rows_per_example
    rb = min(TOKEN_TILE, rows_per_example)
    cpb = rb // CHUNK
    nb = rows_per_example // rb
    has_end = ds_end is not None

    def body(*refs):
        it = iter(refs)
        gam_ref = next(it)
        ins = []
        for _ in range(2):
            z_ref, v_ref = next(it), next(it)
            q_ref = next(it) if with_out else None
            do_ref = next(it) if with_out else None
            ins.append((z_ref, v_ref, q_ref, do_ref, next(it)))
        end_ref = next(it) if has_end else None
        outs = []
        for _ in range(2):
            dz_ref, dv_ref = next(it), next(it)
            dq_ref = next(it) if with_out else None
            outs.append((dz_ref, dv_ref, dq_ref))
        dlb_ref, ds0_ref = next(it), next(it)
        dst_ref = next(it)
        b_id, i = pl.program_id(0), pl.program_id(1)

        @pl.when(i == 0)
        def _():
            if has_end:
                dst_ref[...] = end_ref[:, 0]
            else:
                dst_ref[...] = jnp.zeros_like(dst_ref)

        @pl.when((i == 0) & (b_id == 0))
        def _():
            dlb_ref[...] = jnp.zeros_like(dlb_ref)

        for direction in range(2):
            z_ref, v_ref, q_ref, do_ref, stash = ins[direction]
            dz_ref, dv_ref, dq_ref = outs[direction]
            reverse = direction == 1
            tri = _tri(reverse)
            tri_f = tri.astype(F32)
            lb = _lower_bound(gam_ref, direction)
            order = range(cpb) if reverse else range(cpb - 1, -1, -1)
            dlb_acc = jnp.zeros((1, KW), F32)
            for j in order:
                rs = slice(j * CHUNK, (j + 1) * CHUNK)
                z = z_ref[rs, :].astype(F32)
                v = v_ref[rs, :].astype(F32)
                sg, f, k, b, bl = _gate_prep(z, lb, tri_f)
                mid = 0.5 * bl
                e3 = jnp.exp(bl - b)
                kd = k * e3
                a = jnp.exp(bl)
                if with_out:
                    q = q_ref[rs, :].astype(F32)
                    dout = do_ref[rs, :].astype(F32)
                    e1, e2, e4 = jnp.exp(b - mid), jnp.exp(mid - b), jnp.exp(b)
                    qi, ki, qe = q * e1, k * e2, q * e4
                dkd_p, dv_p, da_p, dqi_p, dki_p, dqe_p = [], [], [], [], [], []
                for h in range(HEADS):
                    hs = slice(h * DK, (h + 1) * DK)
                    st_in = stash[j, h]
                    dst = dst_ref[direction, h]
                    dkd_p.append(_dot(v[:, hs], dst))
                    dvh = _dot(kd[:, hs], dst, "nt")
                    da_p.append(jnp.sum(dst * st_in.astype(F32), axis=0, keepdims=True))
                    new_dst = dst * a[:, hs]
                    if with_out:
                        sc = jnp.where(tri, _dot(qi[:, hs], ki[:, hs], "nt"), 0.0)
                        dsc = jnp.where(tri, _dot(dout[:, hs], v[:, hs], "nt"), 0.0)
                        dqi_p.append(_dot(dsc, ki[:, hs]))
                        dki_p.append(_dot(dsc, qi[:, hs], "tn"))
                        dqe_p.append(_dot(dout[:, hs], st_in))
                        dvh = dvh + _dot(sc, dout[:, hs], "tn")
                        new_dst = new_dst + _dot(dout[:, hs], qe[:, hs], "tn")
                    dv_p.append(dvh)
                    dst_ref[direction, h] = new_dst
                cat = lambda parts: jnp.concatenate(parts, axis=1)
                dkd, da = cat(dkd_p), cat(da_p)
                dv_ref[rs, :] = cat(dv_p)
                t_kd = dkd * kd
                dk = dkd * e3
                db = -t_kd
                dbl = jnp.sum(t_kd, axis=0, keepdims=True) + da * a
                if with_out:
                    dqi, dki, dqe = cat(dqi_p), cat(dki_p), cat(dqe_p)
                    dq_ref[rs, :] = dqi * e1 + dqe * e4
                    dk = dk + dki * e2
                    t_qi, t_ki, t_qe = dqi * qi, dki * ki, dqe * qe
                    db = db + t_qi - t_ki + t_qe
                    dbl = dbl + 0.5 * jnp.sum(t_ki - t_qi, axis=0, keepdims=True)
                dg = _dotx(tri_f, db, "tn") + dbl
                df = dg / f - dk
                dz_ref[rs, :] = df * (1.0 - lb) * sg * (1.0 - sg)
                dlb_acc = dlb_acc + jnp.sum(df * (1.0 - sg), axis=0, keepdims=True)
            dlb_ref[direction:direction + 1, :] += dlb_acc

        @pl.when(i == nb - 1)
        def _():
            ds0_ref[:, 0] = dst_ref[...]

    rows_of = (lambda b, i: b * nb + nb - 1 - i, lambda b, i: b * nb + i)
    in_specs, args = [_full((4, KW))], [gam]
    for direction in range(2):
        rf = rows_of[direction]
        col = lambda c, rf=rf: pl.BlockSpec((rb, KW), lambda b, i: (rf(b, i), c))
        in_specs += [col(direction), col(2)]
        args += [p, p]
        if with_out:
            in_specs += [col(3), col(0)]
            args += [p, do]
        in_specs.append(pl.BlockSpec((cpb, HEADS, DK, DK), lambda b, i, rf=rf: (rf(b, i), 0, 0, 0)))
        args.append((stash_f, stash_b)[direction])
    if has_end:
        in_specs.append(pl.BlockSpec((2, 1, HEADS, DK, DK), lambda b, i: (0, b, 0, 0, 0)))
        args.append(ds_end)
    out_shape, out_specs = [], []
    for direction in range(2):
        rf = rows_of[direction]
        n_out = 3 if with_out else 2
        out_shape += [jax.ShapeDtypeStruct((rows, KW), F32)] * n_out
        out_specs += [pl.BlockSpec((rb, KW), lambda b, i, rf=rf: (rf(b, i), 0))] * n_out
    out_shape += [jax.ShapeDtypeStruct((2, KW), F32), jax.ShapeDtypeStruct((2, nb_ex, HEADS, DK, DK), F32)]
    out_specs += [_full((2, KW)), pl.BlockSpec((2, 1, HEADS, DK, DK), lambda b, i: (0, b, 0, 0, 0))]
    body = _ordered_behind(body, in_specs, args, after)
    return pl.pallas_call(
        body, name=name, out_shape=out_shape, grid=(nb_ex, nb), in_specs=in_specs, out_specs=out_specs,
        scratch_shapes=[pltpu.VMEM((2, HEADS, DK, DK), F32)],
        compiler_params=_params(("arbitrary", "arbitrary")),
    )(*args)


def _tail_forward(osum, og, u, v, ga, gb, gna, ln_g, ln_b, ws_ref, bs_ref, wpaT_ref, wpbT_ref):
    tm = osum.shape[0]
    gna4 = jnp.concatenate([gna] * HEADS, axis=1)
    r_parts = []
    for h in range(HEADS):
        oh = osum[:, h * DK:(h + 1) * DK]
        r_parts.append(jnp.broadcast_to(lax.rsqrt(jnp.mean(oh * oh, axis=-1, keepdims=True) + EPS), (tm, DK)))
    r = jnp.concatenate(r_parts, axis=1)
    on = osum * r
    sg_og = _sigmoid(og)
    silu_og = og * sg_og
    oan = on * gna4
    oa = oan * silu_og
    ug, tu = _gelu(u)
    vg, tv = _gelu(v)
    mu = jnp.mean(vg, axis=-1, keepdims=True)
    vc = vg - mu
    rstd = lax.rsqrt(jnp.mean(vc * vc, axis=-1, keepdims=True) + EPS)
    vhat = vc * rstd
    vln = vhat * ln_g + ln_b
    blocks = []
    for n in range(tm // SGU_BLOCK):
        rs = slice(n * SGU_BLOCK, (n + 1) * SGU_BLOCK)
        blocks.append(jnp.concatenate(
            [_dot(ws_ref[g], vln[rs, g * DK:(g + 1) * DK]) + bs_ref[g] for g in range(GROUPS)], axis=1))
    mixed = jnp.concatenate(blocks, axis=0) if len(blocks) > 1 else blocks[0]
    obm = ug * mixed
    pa = _dot(oa, wpaT_ref[...], "nt")
    pb = _dot(obm, wpbT_ref[...], "nt")
    sga, sgb = _sigmoid(ga), _sigmoid(gb)
    merged = sga * pa + sgb * pb
    return dict(r=r, on=on, sg_og=sg_og, silu_og=silu_og, oan=oan, oa=oa, ug=ug, tu=tu, tv=tv, rstd=rstd, vhat=vhat,
                vln=vln, mixed=mixed, obm=obm, pa=pa, pb=pb, sga=sga, sgb=sgb, merged=merged, gna4=gna4)


def _tail_in_specs(tm):
    tile = lambda c: pl.BlockSpec((tm, KW), lambda i: (i, c))
    return [tile(c) for c in range(4, 11)]


def _tail_weight_specs():
    return [_full((1, DK)), _full((1, KW)), _full((1, KW)), _full((GROUPS, SGU_BLOCK, SGU_BLOCK)),
            _full((GROUPS, SGU_BLOCK, 1)), _full((D, KW), single=True), _full((D, KW), single=True),
            _full((D, D), single=True)]


def _read_tail_inputs(of_ref, ob_ref, pcols):
    osum = of_ref[...] + ob_ref[...]
    og, u, v = (pcols[j][...].astype(F32) for j in range(3))
    ga = jnp.concatenate([pcols[3][...], pcols[4][...]], axis=1).astype(F32)
    gb = jnp.concatenate([pcols[5][...], pcols[6][...]], axis=1).astype(F32)
    return osum, og, u, v, ga, gb


def _tail_fwd(p, o_up, o_down, xt, modv, gna, ln_g, ln_b, w_s, b_s, w_paT, w_pbT, w_o, rows_per_example):
    rows = xt.shape[0]
    tm = min(TOKEN_TILE, rows_per_example)
    per_b = rows_per_example // tm

    def body(of_ref, ob_ref, *rest):
        pcols = rest[:7]
        (x_ref, mod_ref, gna_ref, lng_ref, lnb_ref, ws_ref, bs_ref, wpaT_ref, wpbT_ref, wo_ref,
         x1_ref, mix_ref, merged_ref, oa_ref, obm_ref) = rest[7:]
        t = _tail_forward(*_read_tail_inputs(of_ref, ob_ref, pcols), gna_ref[...], lng_ref[...], lnb_ref[...],
                          ws_ref, bs_ref, wpaT_ref, wpbT_ref)
        mix = _dot(t["merged"], wo_ref[...])
        x1_ref[...] = x_ref[...] + mod_ref[0, 2:3, :] * mix
        mix_ref[...] = mix.astype(mix_ref.dtype)
        merged_ref[...] = t["merged"].astype(merged_ref.dtype)
        oa_ref[...] = t["oa"].astype(oa_ref.dtype)
        obm_ref[...] = t["obm"].astype(obm_ref.dtype)

    row = lambda w: pl.BlockSpec((tm, w), lambda i: (i, 0))
    in_specs = [row(KW), row(KW)] + _tail_in_specs(tm) + [row(D), pl.BlockSpec((1, N_MOD, D), lambda i: (i // per_b, 0, 0))]
    in_specs += _tail_weight_specs()
    return pl.pallas_call(
        body, name="tail_fwd", grid=(rows // tm,),
        out_shape=(jax.ShapeDtypeStruct((rows, D), F32), jax.ShapeDtypeStruct((rows, D), MXU_DTYPE),
                   jax.ShapeDtypeStruct((rows, D), MXU_DTYPE), jax.ShapeDtypeStruct((rows, KW), MXU_DTYPE),
                   jax.ShapeDtypeStruct((rows, KW), MXU_DTYPE)),
        in_specs=in_specs, out_specs=(row(D), row(D), row(D), row(KW), row(KW)),
        compiler_params=_params(("arbitrary",)),
    )(o_up, o_down, *([p] * 7), xt, modv, gna, ln_g, ln_b, w_s, b_s, w_paT, w_pbT, w_o)


def _tail_bwd(p, o_up, o_down, dx1, mix, modv, gna, ln_g, ln_b, w_s, b_s, w_paT, w_pbT, w_o, rows_per_example,
              after=None):
    rows = dx1.shape[0]
    nb_ex = rows // rows_per_example
    tm = min(TOKEN_TILE, rows_per_example)
    per_b = rows_per_example // tm

    def body(of_ref, ob_ref, *rest):
        pcols = rest[:7]
        (dx1_ref, mix_ref, mod_ref, gna_ref, lng_ref, lnb_ref, ws_ref, bs_ref, wpaT_ref, wpbT_ref, wo_ref,
         dpt_ref, do_ref, dmix_ref, dpa_ref, dpb_ref, dmod_ref, small_ref, dws_ref, dbs_ref) = rest[7:]
        i = pl.program_id(0)

        @pl.when(i == 0)
        def _():
            small_ref[...] = jnp.zeros_like(small_ref)
            dws_ref[...] = jnp.zeros_like(dws_ref)
            dbs_ref[...] = jnp.zeros_like(dbs_ref)

        @pl.when(i % per_b == 0)
        def _():
            dmod_ref[...] = jnp.zeros_like(dmod_ref)

        osum, og, u, v, ga, gb = _read_tail_inputs(of_ref, ob_ref, pcols)
        ln_g = lng_ref[...]
        t = _tail_forward(osum, og, u, v, ga, gb, gna_ref[...], ln_g, lnb_ref[...], ws_ref, bs_ref, wpaT_ref, wpbT_ref)
        dx1v = dx1_ref[...]
        dmod_ref[0, 2:3, :] += jnp.sum(dx1v * mix_ref[...].astype(F32), axis=0, keepdims=True)
        dmix = dx1v * mod_ref[0, 2:3, :]
        dmix_ref[...] = dmix.astype(dmix_ref.dtype)
        dmerged = _dot(dmix, wo_ref[...], "nt")
        sga, sgb = t["sga"], t["sgb"]
        dpa = dmerged * sga
        dpb = dmerged * sgb
        dpa_ref[...] = dpa.astype(dpa_ref.dtype)
        dpb_ref[...] = dpb.astype(dpb_ref.dtype)
        dga = dmerged * t["pa"] * sga * (1.0 - sga)
        dgb = dmerged * t["pb"] * sgb * (1.0 - sgb)
        doa = _dot(dpa, wpaT_ref[...])
        dobm = _dot(dpb, wpbT_ref[...])
        dug = dobm * t["mixed"]
        dmixed = dobm * t["ug"]
        du = dug * _gelu_grad(u, t["tu"])
        dvln_blocks = []
        for n in range(tm // SGU_BLOCK):
            rs = slice(n * SGU_BLOCK, (n + 1) * SGU_BLOCK)
            parts = []
            for g in range(GROUPS):
                gs = slice(g * DK, (g + 1) * DK)
                dm = dmixed[rs, gs]
                parts.append(_dot(ws_ref[g], dm, "tn"))
                dws_ref[g] += _dot(dm, t["vln"][rs, gs], "nt")
                dbs_ref[g] += jnp.sum(dm, axis=1, keepdims=True)
            dvln_blocks.append(jnp.concatenate(parts, axis=1))
        dvln = jnp.concatenate(dvln_blocks, axis=0) if len(dvln_blocks) > 1 else dvln_blocks[0]
        vhat = t["vhat"]
        small_ref[1:2, 0:KW] += jnp.sum(dvln * vhat, axis=0, keepdims=True)
        small_ref[2:3, 0:KW] += jnp.sum(dvln, axis=0, keepdims=True)
        dvhat = dvln * ln_g
        dvg = t["rstd"] * (dvhat - jnp.mean(dvhat, axis=-1, keepdims=True)
                           - vhat * jnp.mean(dvhat * vhat, axis=-1, keepdims=True))
        dv = dvg * _gelu_grad(v, t["tv"])
        sg_og = t["sg_og"]
        doan = doa * t["silu_og"]
        dog = doa * t["oan"] * (sg_og * (1.0 + og * (1.0 - sg_og)))
        prod = doan * t["on"]
        dgna = jnp.zeros((1, DK), F32)
        for h in range(HEADS):
            dgna = dgna + jnp.sum(prod[:, h * DK:(h + 1) * DK], axis=0, keepdims=True)
        small_ref[0:1, 0:DK] += dgna
        don = doan * t["gna4"]
        dot_parts = []
        for h in range(HEADS):
            hs = slice(h * DK, (h + 1) * DK)
            m = jnp.mean(don[:, hs] * t["on"][:, hs], axis=-1, keepdims=True)
            dot_parts.append(t["r"][:, hs] * (don[:, hs] - t["on"][:, hs] * m))
        do_ref[...] = jnp.concatenate(dot_parts, axis=1).astype(do_ref.dtype)
        for j, val in enumerate((dog, du, dv)):
            dpt_ref[:, j * KW:(j + 1) * KW] = val.astype(dpt_ref.dtype)
        dpt_ref[:, 3 * KW:3 * KW + D] = dga.astype(dpt_ref.dtype)
        dpt_ref[:, 3 * KW + D:] = dgb.astype(dpt_ref.dtype)

    row = lambda w: pl.BlockSpec((tm, w), lambda i: (i, 0))
    in_specs = [row(KW), row(KW)] + _tail_in_specs(tm) + [row(D), row(D), pl.BlockSpec((1, N_MOD, D), lambda i: (i // per_b, 0, 0))]
    in_specs += _tail_weight_specs()
    args = [o_up, o_down, *([p] * 7), dx1, mix, modv, gna, ln_g, ln_b, w_s, b_s, w_paT, w_pbT, w_o]
    body = _ordered_behind(body, in_specs, args, after)
    cd = MXU_DTYPE
    return pl.pallas_call(
        body, name="tail_bwd", grid=(rows // tm,),
        out_shape=(jax.ShapeDtypeStruct((rows, TAIL_COLS), cd), jax.ShapeDtypeStruct((rows, KW), cd),
                   jax.ShapeDtypeStruct((rows, D), cd), jax.ShapeDtypeStruct((rows, D), cd),
                   jax.ShapeDtypeStruct((rows, D), cd), jax.ShapeDtypeStruct((nb_ex, 8, D), F32),
                   jax.ShapeDtypeStruct((8, D), F32), jax.ShapeDtypeStruct((GROUPS, SGU_BLOCK, SGU_BLOCK), F32),
                   jax.ShapeDtypeStruct((GROUPS, SGU_BLOCK, 1), F32)),
        in_specs=in_specs,
        out_specs=(row(TAIL_COLS), row(KW), row(D), row(D), row(D),
                   pl.BlockSpec((1, 8, D), lambda i: (i // per_b, 0, 0)), _full((8, D)),
                   _full((GROUPS, SGU_BLOCK, SGU_BLOCK)), _full((GROUPS, SGU_BLOCK, 1))),
        compiler_params=_params(("arbitrary",)),
    )(*args)


def _ffn(x1, target, modv, g_ffn, g_final, w_upT, w_down, rows_per_example):
    rows = x1.shape[0]
    nb_ex = rows // rows_per_example
    tm = min(TOKEN_TILE, rows_per_example)
    per_b = rows_per_example // tm
    n_ff = D_FF // FF_CHUNK

    def body(x1_ref, tgt_ref, mod_ref, gffn_ref, gfin_ref, wup_ref, wdn_ref,
             dx1_ref, h2_ref, dffn_ref, act_ref, dup_ref, dmod_ref, small_ref, a_scr, b_scr):
        i = pl.program_id(0)

        @pl.when(i == 0)
        def _():
            small_ref[...] = jnp.zeros_like(small_ref)

        @pl.when(i % per_b == 0)
        def _():
            dmod_ref[...] = jnp.zeros_like(dmod_ref)

        x1v = x1_ref[...]
        g2 = gffn_ref[...]
        m3, m4, m5 = mod_ref[0, 3:4, :], mod_ref[0, 4:5, :], mod_ref[0, 5:6, :]
        r2 = lax.rsqrt(jnp.mean(x1v * x1v, axis=-1, keepdims=True) + EPS)
        xn2 = x1v * r2
        h2 = (xn2 * g2) * (1.0 + m4) + m3
        h2b = h2.astype(MXU_DTYPE)
        h2_ref[...] = h2b
        ffn = jnp.zeros((tm, D), F32)
        for j in range(n_ff):
            cs = slice(j * FF_CHUNK, (j + 1) * FF_CHUNK)
            a = _dot(h2b, wup_ref[j * FF_CHUNK:(j + 1) * FF_CHUNK, :], "nt")
            bgate = _dot(h2b, wup_ref[D_FF + j * FF_CHUNK:D_FF + (j + 1) * FF_CHUNK, :], "nt")
            a_scr[:, cs] = a
            b_scr[:, cs] = bgate
            act = (a * _sigmoid(a) * bgate).astype(MXU_DTYPE)
            act_ref[:, cs] = act
            ffn = ffn + _dot(act, wdn_ref[cs, :])
        x2 = x1v + m5 * ffn
        r3 = lax.rsqrt(jnp.mean(x2 * x2, axis=-1, keepdims=True) + EPS)
        xn3 = x2 * r3
        gf = gfin_ref[...]
        err = xn3 * gf - tgt_ref[...]
        loss = 0.5 * jnp.sum(jnp.mean(err * err, axis=-1, keepdims=True), axis=0, keepdims=True)
        small_ref[2:3, :] += jnp.broadcast_to(loss, (1, D))
        dy = err * (1.0 / D)
        small_ref[1:2, :] += jnp.sum(dy * xn3, axis=0, keepdims=True)
        dxn3 = dy * gf
        dx2 = r3 * (dxn3 - xn3 * jnp.mean(dxn3 * xn3, axis=-1, keepdims=True))
        dmod_ref[0, 5:6, :] += jnp.sum(dx2 * ffn, axis=0, keepdims=True)
        dffn = (dx2 * m5).astype(MXU_DTYPE)
        dffn_ref[...] = dffn
        dh2 = jnp.zeros((tm, D), F32)
        for j in range(n_ff):
            cs = slice(j * FF_CHUNK, (j + 1) * FF_CHUNK)
            dact = _dot(dffn, wdn_ref[cs, :], "nt")
            a, bgate = a_scr[:, cs], b_scr[:, cs]
            s = _sigmoid(a)
            da = (dact * bgate * (s * (1.0 + a * (1.0 - s)))).astype(MXU_DTYPE)
            dbg = (dact * a * s).astype(MXU_DTYPE)
            dup_ref[:, cs] = da
            dup_ref[:, D_FF + j * FF_CHUNK:D_FF + (j + 1) * FF_CHUNK] = dbg
            dh2 = dh2 + _dot(da, wup_ref[j * FF_CHUNK:(j + 1) * FF_CHUNK, :])
            dh2 = dh2 + _dot(dbg, wup_ref[D_FF + j * FF_CHUNK:D_FF + (j + 1) * FF_CHUNK, :])
        dmod_ref[0, 3:4, :] += jnp.sum(dh2, axis=0, keepdims=True)
        dmod_ref[0, 4:5, :] += jnp.sum(dh2 * xn2 * g2, axis=0, keepdims=True)
        small_ref[0:1, :] += jnp.sum(dh2 * (1.0 + m4) * xn2, axis=0, keepdims=True)
        dxn2 = dh2 * g2 * (1.0 + m4)
        dx1_ref[...] = dx2 + r2 * (dxn2 - xn2 * jnp.mean(dxn2 * xn2, axis=-1, keepdims=True))

    row = lambda w: pl.BlockSpec((tm, w), lambda i: (i, 0))
    cd = MXU_DTYPE
    return pl.pallas_call(
        body, name="ffn_fwd_bwd", grid=(rows // tm,),
        out_shape=(jax.ShapeDtypeStruct((rows, D), F32), jax.ShapeDtypeStruct((rows, D), cd),
                   jax.ShapeDtypeStruct((rows, D), cd), jax.ShapeDtypeStruct((rows, D_FF), cd),
                   jax.ShapeDtypeStruct((rows, 2 * D_FF), cd), jax.ShapeDtypeStruct((nb_ex, 8, D), F32),
                   jax.ShapeDtypeStruct((8, D), F32)),
        in_specs=[row(D), row(D), pl.BlockSpec((1, N_MOD, D), lambda i: (i // per_b, 0, 0)), _full((1, D)), _full((1, D)),
                  _full((2 * D_FF, D), single=True), _full((D_FF, D), single=True)],
        out_specs=(row(D), row(D), row(D), row(D_FF), row(2 * D_FF),
                   pl.BlockSpec((1, 8, D), lambda i: (i // per_b, 0, 0)), _full((8, D))),
        scratch_shapes=[pltpu.VMEM((tm, D_FF), F32), pltpu.VMEM((tm, D_FF), F32)],
        compiler_params=_params(("arbitrary",)),
    )(x1, target, modv, g_ffn, g_final, w_upT, w_down)


def _inproj_bwd(pieces, dpt, xt, dx1, modv, g, w_inT, rows_per_example, name):
    rows = xt.shape[0]
    latent = dx1 is not None
    n_cols = IN_COLS if latent else CTX_COLS
    tm = min(TOKEN_TILE, rows_per_example)
    per_b = rows_per_example // tm
    n_mod_blocks = rows // rows_per_example if latent else 1
    n_pieces = len(pieces)

    def body(*refs):
        it = iter(refs)
        pc = [next(it) for _ in range(n_pieces)]
        dpt_ref = next(it) if latent else None
        x_ref = next(it)
        dx1_ref = next(it) if latent else None
        mod_ref, g_ref, w_ref = next(it), next(it), next(it)
        gx_ref = next(it) if latent else None
        dp_ref, dmod_ref, small_ref = next(it), next(it), next(it)
        i = pl.program_id(0)

        @pl.when(i == 0)
        def _():
            small_ref[...] = jnp.zeros_like(small_ref)

        @pl.when((i % per_b == 0) if latent else (i == 0))
        def _():
            dmod_ref[...] = jnp.zeros_like(dmod_ref)

        cols = [pc[0][...], pc[1][...], pc[2][...] + pc[3][...]]
        if latent:
            cols.append(pc[4][...] + pc[5][...])
        dh = jnp.zeros((tm, D), F32)
        for j, val in enumerate(cols):
            vb = val.astype(MXU_DTYPE)
            dp_ref[:, j * KW:(j + 1) * KW] = vb
            dh = dh + _dot(vb, w_ref[j * KW:(j + 1) * KW, :])
        if latent:
            for j in range(4, IN_COLS // KW):
                vb = dpt_ref[:, (j - 4) * KW:(j - 3) * KW]
                dp_ref[:, j * KW:(j + 1) * KW] = vb
                dh = dh + _dot(vb, w_ref[j * KW:(j + 1) * KW, :])
        x = x_ref[...]
        gv = g_ref[...]
        m1 = mod_ref[0, 1:2, :]
        r = lax.rsqrt(jnp.mean(x * x, axis=-1, keepdims=True) + EPS)
        xn = x * r
        dmod_ref[0, 0:1, :] += jnp.sum(dh, axis=0, keepdims=True)
        dmod_ref[0, 1:2, :] += jnp.sum(dh * xn * gv, axis=0, keepdims=True)
        small_ref[0:1, :] += jnp.sum(dh * (1.0 + m1) * xn, axis=0, keepdims=True)
        if latent:
            dxn = dh * gv * (1.0 + m1)
            gx_ref[...] = dx1_ref[...] + r * (dxn - xn * jnp.mean(dxn * xn, axis=-1, keepdims=True))

    row = lambda w: pl.BlockSpec((tm, w), lambda i: (i, 0))
    mod_idx = (lambda i: (i // per_b, 0, 0)) if latent else (lambda i: (0, 0, 0))
    in_specs = [row(KW)] * n_pieces + ([row(TAIL_COLS)] if latent else []) + [row(D)] + ([row(D)] if latent else [])
    in_specs += [pl.BlockSpec((1, N_MOD, D), mod_idx), _full((1, D)),
                 pl.BlockSpec((n_cols, D), lambda i: (0, 0), pipeline_mode=pl.Buffered(1))]
    args = list(pieces) + ([dpt] if latent else []) + [xt] + ([dx1] if latent else []) + [modv, g, w_inT]
    out_shape = ([jax.ShapeDtypeStruct((rows, D), F32)] if latent else []) + [
        jax.ShapeDtypeStruct((rows, n_cols), MXU_DTYPE), jax.ShapeDtypeStruct((n_mod_blocks, 8, D), F32),
        jax.ShapeDtypeStruct((8, D), F32)]
    out_specs = ([row(D)] if latent else []) + [row(n_cols), pl.BlockSpec((1, 8, D), mod_idx), _full((8, D))]
    return pl.pallas_call(
        body, name=name, grid=(rows // tm,), out_shape=out_shape, in_specs=in_specs, out_specs=out_specs,
        compiler_params=_params(("arbitrary",)),
    )(*args)


def _grad_matmul(a, b, name, init=None, tn=512, tt=1024):
    rows, n = a.shape
    k = b.shape[1]
    tn = min(tn, n)
    tt = min(tt, rows)
    steps = rows // tt
    has_init = init is not None

    def body(*refs):
        if has_init:
            a_ref, b_ref, init_ref, o_ref, acc = refs
        else:
            a_ref, b_ref, o_ref, acc = refs
        t = pl.program_id(1)

        @pl.when(t == 0)
        def _():
            acc[...] = init_ref[...].astype(F32) if has_init else jnp.zeros_like(acc)

        acc[...] += _dot(a_ref[...], b_ref[...], "tn")

        @pl.when(t == steps - 1)
        def _():
            o_ref[...] = acc[...].astype(o_ref.dtype)

    in_specs = [pl.BlockSpec((tt, tn), lambda i, t: (t, i)), pl.BlockSpec((tt, k), lambda i, t: (t, 0))]
    args = [a, b]
    if has_init:
        in_specs.append(pl.BlockSpec((tn, k), lambda i, t: (i, 0)))
        args.append(init)
    return pl.pallas_call(
        body, name=name, grid=(n // tn, steps), out_shape=jax.ShapeDtypeStruct((n, k), PAYLOAD_DTYPE),
        in_specs=in_specs, out_specs=pl.BlockSpec((tn, k), lambda i, t: (i, 0)),
        scratch_shapes=[pltpu.VMEM((tn, k), F32)],
        compiler_params=_params(("arbitrary", "arbitrary")),
    )(*args)


def _row_tile(rows, limit=256):
    if rows <= limit:
        return rows
    for t in range(limit, 7, -8):
        if rows % t == 0:
            return t
    return rows


def _sum8(stack, name):
    _, rows, cols = stack.shape
    tr = _row_tile(rows)

    def body(s_ref, o_ref):
        acc = s_ref[0].astype(F32)
        for j in range(1, N_DEV):
            acc = acc + s_ref[j].astype(F32)
        o_ref[...] = acc

    return pl.pallas_call(
        body, name=name, grid=(rows // tr,), out_shape=jax.ShapeDtypeStruct((rows, cols), F32),
        in_specs=[pl.BlockSpec((N_DEV, tr, cols), lambda i: (0, i, 0))],
        out_specs=pl.BlockSpec((tr, cols), lambda i: (i, 0)),
        compiler_params=_params(("arbitrary",)),
    )(stack)


def _small_reduce(stack, gam, nb_ex):
    def body(s_ref, gam_ref, o_ref, bm_ref):
        acc = s_ref[0]
        for j in range(1, N_DEV):
            acc = acc + s_ref[j]
        o_ref[...] = acc
        bm = acc[8:8 + N_MOD, :]
        for e in range(nb_ex):
            bm = bm + acc[16 + e * N_MOD:16 + (e + 1) * N_MOD, :]
        lb = jnp.concatenate([_lower_bound(gam_ref, 0), _lower_bound(gam_ref, 1)], axis=1)
        dgam = acc[7:8, :] * lb * (1.0 - lb)
        bm_ref[...] = jnp.concatenate([bm, dgam, -dgam], axis=0)

    return pl.pallas_call(
        body, name="small_reduce", grid=(1,),
        out_shape=(jax.ShapeDtypeStruct((SMALL_ROWS, D), F32), jax.ShapeDtypeStruct((8, D), F32)),
        in_specs=[_full((N_DEV, SMALL_ROWS, D)), _full((4, KW))], out_specs=(_full((SMALL_ROWS, D)), _full((8, D))),
        compiler_params=_params(("arbitrary",)),
    )(stack, gam)


def _adamw(w, g, m, v, name):
    shape = w.shape
    cols = shape[-1]
    rows = 1
    for s in shape[:-1]:
        rows *= s
    tr = _row_tile(rows)

    def body(w_ref, g_ref, m_ref, v_ref, d_ref, nm_ref, nv_ref):
        gv = g_ref[...]
        nm = ADAM_B1 * m_ref[...] + (1.0 - ADAM_B1) * gv
        nv = ADAM_B2 * v_ref[...] + (1.0 - ADAM_B2) * (gv * gv)
        m_hat = nm / (1.0 - ADAM_B1 ** ADAM_STEP)
        v_hat = nv / (1.0 - ADAM_B2 ** ADAM_STEP)
        d_ref[...] = -ADAM_LR * (m_hat / (jnp.sqrt(v_hat) + ADAM_EPS) + ADAM_WD * w_ref[...])
        nm_ref[...] = nm
        nv_ref[...] = nv

    blk = pl.BlockSpec((tr, cols), lambda i: (i, 0))
    sd = jax.ShapeDtypeStruct((rows, cols), F32)
    d, nm, nv = pl.pallas_call(
        body, name=name, grid=(rows // tr,), out_shape=(sd, sd, sd), in_specs=[blk] * 4, out_specs=(blk, blk, blk),
        compiler_params=_params(("arbitrary",)),
    )(w.reshape(rows, cols), g.reshape(rows, cols), m.reshape(rows, cols), v.reshape(rows, cols))
    return d.reshape(shape), nm.reshape(shape), nv.reshape(shape)


def _local_step(x, ctx, target, modv, mcv, gam, g_mix, g_ffn, gna, ln_g, ln_b, w_s, b_s, g_final,
                w_inT, late_weights, emit, after=None):
    nb_ex, seq, _ = x.shape
    ctx_len = ctx.shape[1]
    xt = x.reshape(nb_ex * seq, D)
    ct = ctx.reshape(nb_ex * ctx_len, D)
    tgt = target.reshape(nb_ex * seq, D)
    bs3 = b_s.reshape(GROUPS, SGU_BLOCK, 1)

    pc, hc = _inproj(ct, mcv, g_mix, w_inT, CTX_COLS, ctx_len, "inproj_ctx", after=after)
    p, h = _inproj(xt, modv, g_mix, w_inT, IN_COLS, seq, "inproj_lat", after=after)
    cst_f, cst_b, s_ctx = _hgrn_fwd(pc, gam, None, ctx_len, False, "hgrn_fwd_ctx")
    o_up, o_down, st_f, st_b, _ = _hgrn_fwd(p, gam, s_ctx, seq, True, "hgrn_fwd_lat")
    w_upT, w_down, w_o, w_paT, w_pbT = late_weights(o_up)
    x1, mix, merged, oa, obm = _tail_fwd(p, o_up, o_down, xt, modv, gna, ln_g, ln_b, w_s, bs3, w_paT, w_pbT, w_o, seq)
    dx1, h2, dffn, act, dup, dmod_ffn, small_ffn = _ffn(x1, tgt, modv, g_ffn, g_final, w_upT, w_down, seq)
    gw_upT = _grad_matmul(dup, h2, "gw_up")
    gw_down = _grad_matmul(act, dffn, "gw_down", tn=256)
    started = emit("ffn", (gw_upT, gw_down))
    dpt, do, dmix, dpa, dpb, dmod_tail, small_tail, dws, dbs = _tail_bwd(
        p, o_up, o_down, dx1, mix, modv, gna, ln_g, ln_b, w_s, bs3, w_paT, w_pbT, w_o, seq, after=started)
    gw_o = _grad_matmul(merged, dmix, "gw_o")
    gw_paT = _grad_matmul(dpa, oa, "gw_pa")
    gw_pbT = _grad_matmul(dpb, obm, "gw_pb")
    started = emit("tail", (gw_o, gw_paT, gw_pbT))
    dzf, dvf, dqf, dzb, dvb, dqb, dlb, ds0 = _hgrn_bwd(p, gam, do, st_f, st_b, None, seq, True, "hgrn_bwd_lat",
                                                       after=started)
    czf, cvf, czb, cvb, dlb_c, _ = _hgrn_bwd(pc, gam, None, cst_f, cst_b, ds0, ctx_len, False, "hgrn_bwd_ctx")
    grad_x, dp, dmod_in, small_in = _inproj_bwd([dzf, dzb, dvf, dvb, dqf, dqb], dpt, xt, dx1, modv, g_mix, w_inT,
                                                 seq, "inproj_bwd_lat")
    dpc, dmc, small_c = _inproj_bwd([czf, czb, cvf, cvb], None, ct, None, mcv, g_mix, w_inT, ctx_len, "inproj_bwd_ctx")
    g_ctx = _grad_matmul(dpc, hc, "gw_in_ctx")
    g_ctx = jnp.pad(g_ctx, ((0, IN_COLS - CTX_COLS), (0, 0)))
    gw_inT = _grad_matmul(dp, h, "gw_in", init=g_ctx)

    z = lambda r: jnp.zeros((r, D), F32)
    pad = lambda a: jnp.pad(a, ((0, 0), (0, D - a.shape[1])))
    dlb_row = (dlb + dlb_c).reshape(1, 2 * KW)
    dmod = dmod_in + dmod_tail + dmod_ffn
    small = jnp.concatenate([
        small_in[0:1] + small_c[0:1],
        small_ffn[0:1],
        small_ffn[1:2],
        small_tail[0:1],
        small_tail[1:2],
        small_tail[2:3],
        pad(dbs.reshape(1, GROUPS * SGU_BLOCK)),
        dlb_row,
        dmc[0, 0:N_MOD],
        z(2),
        dmod[:, 0:N_MOD].reshape(nb_ex * N_MOD, D),
        z(24 - nb_ex * N_MOD),
        dws.reshape(GROUPS * SGU_BLOCK * SGU_BLOCK // D, D),
    ], axis=0)
    loss = small_ffn[2, 0]
    return loss, grad_x.reshape(x.shape), gw_inT, small


def kernel(x, c, ctx, c_ctx, w_mod, b_mod, g_mix, g_ffn, w_in, lb_gamma, g_norm_a, ln_v_g, ln_v_b, w_s, b_s, w_pa, w_pb, w_o, w_up, w_down, g_final, loss_target, m_c_ctx, m_w_mod, m_b_mod, m_g_mix, m_g_ffn, m_w_in, m_lb_gamma, m_g_norm_a, m_ln_v_g, m_ln_v_b, m_w_s, m_b_s, m_w_pa, m_w_pb, m_w_o, m_w_up, m_w_down, m_g_final, v_c_ctx, v_w_mod, v_b_mod, v_g_mix, v_g_ffn, v_w_in, v_lb_gamma, v_g_norm_a, v_ln_v_g, v_ln_v_b, v_w_s, v_b_s, v_w_pa, v_w_pb, v_w_o, v_w_up, v_w_down, v_g_final):
    nb_ex = x.shape[0]
    me = 4 * lax.axis_index("x") + 2 * lax.axis_index("y") + lax.axis_index("c")
    cd = MXU_DTYPE
    mod_cols = w_mod.shape[2]
    lb_cols = lb_gamma.shape[2]

    w_inT_l = w_in[0].T.astype(cd)
    w_upT_l = w_up[0].T.astype(cd)
    w_paT_l = w_pa[0].T.astype(cd)
    w_pbT_l = w_pb[0].T.astype(cd)
    cl = jnp.concatenate([c, jnp.pad(lb_gamma.reshape(1, 4 * lb_cols), ((0, 0), (0, D - 4 * lb_cols))),
                          jnp.zeros((8 - nb_ex - 1, D), F32)], axis=0)
    g_in, g_cl = _exchange([(w_inT_l, "gather"), (cl, "gather")], "gather_w_in")
    w_inT = g_in.reshape(IN_COLS, D)
    c_all = g_cl[:, 0:nb_ex].reshape(N_DEV * nb_ex, D)
    gam = jnp.transpose(g_cl[:, nb_ex, 0:4 * lb_cols].reshape(N_DEV, 4, lb_cols), (1, 0, 2)).reshape(4, KW)

    n_c = N_DEV * nb_ex
    cvec = jnp.concatenate([c_all, c_ctx.reshape(1, D), jnp.zeros((7, D), F32)], axis=0)
    b_mod_l = lax.dynamic_slice(b_mod, (0, me * mod_cols), (1, mod_cols))
    mod_l, svec = _mod_fwd(cvec, w_mod[0], b_mod_l)
    (g_mod,) = _exchange([(mod_l, "gather")], "gather_mod")
    mod_all = jnp.transpose(g_mod, (1, 0, 2)).reshape(n_c + 8, N_MOD * D)
    modv = lax.dynamic_slice(mod_all, (me * nb_ex, 0), (nb_ex, N_MOD * D)).reshape(nb_ex, N_MOD, D)
    mcv = mod_all[n_c].reshape(1, N_MOD, D)

    late, late_started = _exchange_start(
        [(w_upT_l, "gather"), (w_down[0].astype(cd), "gather"), (w_o[0].astype(cd), "gather"), (w_paT_l, "gather"),
         (w_pbT_l, "gather")], "gather_late", after=g_mod)

    def late_weights(after):
        g_up, g_down, g_o, g_pa, g_pb = _exchange_wait(late, after)
        return (g_up.reshape(2 * D_FF, D), g_down.reshape(D_FF, D), g_o.reshape(D, D), g_pa.reshape(D, KW),
                g_pb.reshape(D, KW))

    blocks = lambda a: a.reshape(N_DEV, a.shape[0] // N_DEV, a.shape[1])
    in_flight = {}

    def emit(stage, grads):
        in_flight[stage], started = _exchange_start([(blocks(g), "scatter") for g in grads], "scatter_" + stage,
                                                    after=grads[-1])
        return started

    loss_l, grad_x, gw_inT, small = _local_step(
        x, ctx, loss_target, modv, mcv, gam, g_mix, g_ffn, g_norm_a, ln_v_g, ln_v_b, w_s[0], b_s[0],
        g_final.reshape(1, D), w_inT, late_weights, emit, after=late_started)
    loss = lax.psum(loss_l, ("x", "y", "c"))
    last, last_started = _exchange_start([(blocks(gw_inT), "scatter"), (small, "gather")], "scatter_in", after=small)

    r_up, r_down = _exchange_wait(in_flight["ffn"], last_started)
    r_o, r_pa, r_pb = _exchange_wait(in_flight["tail"], last_started)
    grad_w_up = _sum8(r_up, "sum_w_up").T[None]
    grad_w_down = _sum8(r_down, "sum_w_down")[None]
    grad_w_o = _sum8(r_o, "sum_w_o")[None]
    grad_w_pa = _sum8(r_pa, "sum_w_pa").T[None]
    grad_w_pb = _sum8(r_pb, "sum_w_pb").T[None]
    early = {"w_up": (w_up, grad_w_up, m_w_up, v_w_up), "w_down": (w_down, grad_w_down, m_w_down, v_w_down),
             "w_o": (w_o, grad_w_o, m_w_o, v_w_o), "w_pa": (w_pa, grad_w_pa, m_w_pa, v_w_pa),
             "w_pb": (w_pb, grad_w_pb, m_w_pb, v_w_pb)}
    early_out = {nm: _adamw(*args, "adamw_" + nm) for nm, args in early.items()}

    r_in, r_small = _exchange_wait(last, early_out["w_up"][0])
    grad_w_in = _sum8(r_in, "sum_w_in").T[None]
    tot, bm = _small_reduce(r_small, gam, nb_ex)
    grad_g_mix, grad_g_ffn, grad_g_final = tot[0:1], tot[1:2], tot[2]
    grad_g_norm_a = tot[3:4, 0:DK]
    grad_ln_v_g, grad_ln_v_b = tot[4:5, 0:KW], tot[5:6, 0:KW]
    grad_b_s = tot[6, 0:GROUPS * SGU_BLOCK].reshape(1, GROUPS, SGU_BLOCK)
    grad_w_s = tot[40:104].reshape(1, GROUPS, SGU_BLOCK, SGU_BLOCK)
    grad_b_mod = bm[0:N_MOD].reshape(1, N_MOD * D)
    grad_lb_gamma = lax.dynamic_slice(bm[6:8].reshape(2, 2, KW), (0, 0, me * lb_cols), (2, 2, lb_cols))

    dmod_all = r_small[:, 16:16 + nb_ex * N_MOD].reshape(n_c, N_MOD * D)
    dmod_l = jnp.concatenate([lax.dynamic_slice(dmod_all, (0, me * mod_cols), (n_c, mod_cols)),
                              lax.dynamic_slice(tot[8:8 + N_MOD].reshape(1, N_MOD * D), (0, me * mod_cols), (1, mod_cols)),
                              jnp.zeros((7, mod_cols), F32)], axis=0)
    gw_mod, gc = _mod_bwd(svec, cvec, dmod_l, w_mod[0])
    grad_w_mod = gw_mod[None]
    (r_gc,) = _exchange([(gc[n_c:n_c + 8], "gather")], "gather_c_ctx")
    grad_c_ctx = _sum8(r_gc, "sum_c_ctx")[0]

    names = ["c_ctx", "w_mod", "b_mod", "g_mix", "g_ffn", "w_in", "lb_gamma", "g_norm_a", "ln_v_g", "ln_v_b", "w_s",
             "b_s", "w_pa", "w_pb", "w_o", "w_up", "w_down", "g_final"]
    weights = [c_ctx, w_mod, b_mod, g_mix, g_ffn, w_in, lb_gamma, g_norm_a, ln_v_g, ln_v_b, w_s, b_s, w_pa, w_pb, w_o,
               w_up, w_down, g_final]
    grads = [grad_c_ctx, grad_w_mod, grad_b_mod, grad_g_mix, grad_g_ffn, grad_w_in, grad_lb_gamma, grad_g_norm_a,
             grad_ln_v_g, grad_ln_v_b, grad_w_s, grad_b_s, grad_w_pa, grad_w_pb, grad_w_o, grad_w_up, grad_w_down,
             grad_g_final]
    ms = [m_c_ctx, m_w_mod, m_b_mod, m_g_mix, m_g_ffn, m_w_in, m_lb_gamma, m_g_norm_a, m_ln_v_g, m_ln_v_b, m_w_s, m_b_s,
          m_w_pa, m_w_pb, m_w_o, m_w_up, m_w_down, m_g_final]
    vs = [v_c_ctx, v_w_mod, v_b_mod, v_g_mix, v_g_ffn, v_w_in, v_lb_gamma, v_g_norm_a, v_ln_v_g, v_ln_v_b, v_w_s, v_b_s,
          v_w_pa, v_w_pb, v_w_o, v_w_up, v_w_down, v_g_final]
    deltas, new_ms, new_vs = [], [], []
    for nm, w, g, m, v in zip(names, weights, grads, ms, vs):
        d, nm_, nv_ = early_out[nm] if nm in early_out else _adamw(w, g.reshape(w.shape), m, v, "adamw_" + nm)
        deltas.append(d)
        new_ms.append(nm_)
        new_vs.append(nv_)
    grads = [g.reshape(w.shape) for g, w in zip(grads, weights)]
    return (loss, grad_x, *grads, *deltas, *new_ms, *new_vs)
```

```python
import functools

import jax
import jax.numpy as jnp
from jax import lax
from jax.experimental import pallas as pl
from jax.experimental.pallas import tpu as pltpu

F32 = jnp.float32
MXU_DTYPE = jnp.bfloat16
PAYLOAD_DTYPE = jnp.bfloat16

N_DEV = 8
D = 1024
HEADS = 4
DK = 128
KW = HEADS * DK
CHUNK = 64
SGU_BLOCK = 128
GROUPS = 4
D_FF = 2816
FF_CHUNK = 256
N_MOD = 6
IN_COLS = 5632
CTX_COLS = 1536
TAIL_COLS = IN_COLS - 4 * KW
EPS = 1e-6
ADAM_LR, ADAM_B1, ADAM_B2, ADAM_EPS, ADAM_WD, ADAM_STEP = 0.001, 0.9, 0.999, 1e-08, 0.01, 10

VMEM_LIMIT = 56 * 1024 * 1024
TOKEN_TILE = 256
SMALL_ROWS = 104


def _params(sem):
    return pltpu.CompilerParams(dimension_semantics=sem, vmem_limit_bytes=VMEM_LIMIT)


_DN = {"nn": (((1,), (0,)), ((), ())), "nt": (((1,), (1,)), ((), ())), "tn": (((0,), (0,)), ((), ()))}


def _dot(a, b, form="nn"):
    return lax.dot_general(a.astype(MXU_DTYPE), b.astype(MXU_DTYPE), _DN[form], preferred_element_type=F32)


def _dotx(a, b, form="nn"):
    return lax.dot_general(a.astype(F32), b.astype(F32), _DN[form], preferred_element_type=F32,
                           precision=lax.Precision.HIGHEST)


def _full(shape, single=False):
    n = len(shape)
    if single:
        return pl.BlockSpec(shape, lambda *_: (0,) * n, pipeline_mode=pl.Buffered(1))
    return pl.BlockSpec(shape, lambda *_: (0,) * n)


def _ordered_behind(body, in_specs, args, after):
    if after is None:
        return body
    at = len(in_specs)
    in_specs.append(pl.BlockSpec(memory_space=pl.ANY))
    args.append(after)
    return lambda *refs: body(*refs[:at], *refs[at + 1:])


def _sigmoid(z):
    return 1.0 / (1.0 + jnp.exp(-z))


def _gelu(x):
    c = 0.7978845608028654
    t = jnp.tanh(c * (x + 0.044715 * x * x * x))
    return 0.5 * x * (1.0 + t), t


def _gelu_grad(x, t):
    c = 0.7978845608028654
    return 0.5 * (1.0 + t) + 0.5 * x * (1.0 - t * t) * c * (1.0 + 3 * 0.044715 * x * x)


def _exchange(items, name):
    n = len(items)
    out_shape = []
    for a, mode in items:
        blk = a.shape if mode == "gather" else a.shape[1:]
        out_shape.append(jax.ShapeDtypeStruct((N_DEV,) + tuple(blk), a.dtype))

    def body(*refs):
        srcs, dsts = refs[:n], refs[n:2 * n]
        send_sems, recv_sems, local_sems = refs[2 * n:]
        x, y, c = lax.axis_index("x"), lax.axis_index("y"), lax.axis_index("c")
        me = 4 * x + 2 * y + c

        def src_for(i, dev):
            return srcs[i] if items[i][1] == "gather" else srcs[i].at[dev]

        local = [pltpu.make_async_copy(src_for(i, me), dsts[i].at[me], local_sems.at[i]) for i in range(n)]
        for cp in local:
            cp.start()
        remote = []
        for k in range(1, N_DEV):
            px = jnp.bitwise_xor(x, (k >> 2) & 1)
            py = jnp.bitwise_xor(y, (k >> 1) & 1)
            pc = jnp.bitwise_xor(c, k & 1)
            peer = 4 * px + 2 * py + pc
            for i in range(n):
                cp = pltpu.make_async_remote_copy(
                    src_ref=src_for(i, peer), dst_ref=dsts[i].at[me],
                    send_sem=send_sems.at[i * (N_DEV - 1) + k - 1], recv_sem=recv_sems.at[i * (N_DEV - 1) + k - 1],
                    device_id=(px, py, pc), device_id_type=pl.DeviceIdType.MESH)
                cp.start()
                remote.append(cp)
        for cp in remote:
            cp.wait()
        for cp in local:
            cp.wait()

    any_spec = pl.BlockSpec(memory_space=pl.ANY)
    return pl.pallas_call(
        body, name=name, out_shape=out_shape,
        in_specs=[any_spec] * n, out_specs=[any_spec] * n,
        scratch_shapes=[pltpu.SemaphoreType.DMA((n * (N_DEV - 1),)), pltpu.SemaphoreType.DMA((n * (N_DEV - 1),)),
                        pltpu.SemaphoreType.DMA((n,))],
    )(*[a for a, _ in items])


_HBM = pl.BlockSpec(memory_space=pltpu.HBM)
_SEM = pl.BlockSpec(memory_space=pltpu.SEMAPHORE)
_EFFECT = pltpu.SideEffectType.DATAFLOW_SIDE_EFFECTING


def _split_copies(items, srcs, lands, send_sems, recv_sems):
    x, y, c = lax.axis_index("x"), lax.axis_index("y"), lax.axis_index("c")
    me = 4 * x + 2 * y + c
    copies = []
    for k in range(1, N_DEV):
        px = jnp.bitwise_xor(x, (k >> 2) & 1)
        py = jnp.bitwise_xor(y, (k >> 1) & 1)
        pc = jnp.bitwise_xor(c, k & 1)
        peer = 4 * px + 2 * py + pc
        for i in range(len(items)):
            src = srcs[i] if items[i][1] == "gather" else srcs[i].at[peer]
            copies.append(pltpu.make_async_remote_copy(
                src_ref=src, dst_ref=lands[i].at[me],
                send_sem=send_sems.at[i * (N_DEV - 1) + k - 1], recv_sem=recv_sems.at[i * (N_DEV - 1) + k - 1],
                device_id=(px, py, pc), device_id_type=pl.DeviceIdType.MESH))
    return me, copies


def _exchange_start(items, name, after):
    n = len(items)
    n_sem = n * (N_DEV - 1)
    srcs, lands = [], []
    for a, mode in items:
        blk = a.shape if mode == "gather" else a.shape[1:]
        srcs.append(pltpu.with_memory_space_constraint(a, pltpu.HBM))
        lands.append(pltpu.with_memory_space_constraint(lax.empty((N_DEV,) + tuple(blk), a.dtype), pltpu.HBM))

    def body(*refs):
        src_refs, land_refs = refs[:n], refs[n:2 * n]
        send_sems, recv_sems = refs[2 * n + 1], refs[2 * n + 2]
        local_sems = refs[4 * n + 3]
        me, copies = _split_copies(items, src_refs, land_refs, send_sems, recv_sems)
        for i in range(n):
            own = src_refs[i] if items[i][1] == "gather" else src_refs[i].at[me]
            cp = pltpu.make_async_copy(own, land_refs[i].at[me], local_sems.at[i])
            cp.start()
            cp.wait()
        for cp in copies:
            cp.start()

    out_shape = [pltpu.SemaphoreType.DMA((n_sem,)), pltpu.SemaphoreType.DMA((n_sem,))]
    out_shape += [pltpu.HBM(a.shape, a.dtype) for a in srcs] + [pltpu.HBM(a.shape, a.dtype) for a in lands]
    outs = pl.pallas_call(
        body, name=name, out_shape=out_shape,
        in_specs=[_HBM] * (2 * n) + [pl.BlockSpec(memory_space=pl.ANY)],
        out_specs=[_SEM, _SEM] + [_HBM] * (2 * n),
        input_output_aliases={i: 2 + i for i in range(2 * n)},
        scratch_shapes=[pltpu.SemaphoreType.DMA((n,))],
        compiler_params=pltpu.CompilerParams(has_side_effects=_EFFECT),
    )(*srcs, *lands, after)
    handle = (items, name, outs[0], outs[1], outs[2:2 + n], outs[2 + n:2 + 2 * n])
    return handle, outs[2]


class _Sender:
    def __init__(self, items, chunks):
        self.items, self.chunks, self.n = items, chunks, len(items)
        self.srcs, self.lands = [], []
        for a, mode in items:
            blk = a.shape if mode == "gather" else a.shape[1:]
            self.srcs.append(pltpu.with_memory_space_constraint(a, pltpu.HBM))
            self.lands.append(pltpu.with_memory_space_constraint(lax.empty((N_DEV,) + tuple(blk), a.dtype), pltpu.HBM))

    def issue(self, src_refs, land_refs, send_sems, recv_sems, local_sems, step, n_steps):
        x, y, c = lax.axis_index("x"), lax.axis_index("y"), lax.axis_index("c")
        me = 4 * x + 2 * y + c
        copies = []
        for ch in range(max(self.chunks)):
            for k in range(1, N_DEV):
                px = jnp.bitwise_xor(x, (k >> 2) & 1)
                py = jnp.bitwise_xor(y, (k >> 1) & 1)
                pc = jnp.bitwise_xor(c, k & 1)
                peer = 4 * px + 2 * py + pc
                for i, (_, mode) in enumerate(self.items):
                    if ch >= self.chunks[i]:
                        continue
                    n_rows = land_refs[i].shape[1] // self.chunks[i]
                    rows = pl.ds(ch * n_rows, n_rows)
                    src = src_refs[i].at[rows] if mode == "gather" else src_refs[i].at[peer].at[rows]
                    copies.append(pltpu.make_async_remote_copy(
                        src_ref=src, dst_ref=land_refs[i].at[me].at[rows],
                        send_sem=send_sems.at[i * (N_DEV - 1) + k - 1], recv_sem=recv_sems.at[i * (N_DEV - 1) + k - 1],
                        device_id=(px, py, pc), device_id_type=pl.DeviceIdType.MESH))
        own = [pltpu.make_async_copy(src_refs[i] if mode == "gather" else src_refs[i].at[me], land_refs[i].at[me],
                                     local_sems.at[i]) for i, (_, mode) in enumerate(self.items)]

        @pl.when(step == 0)
        def _():
            for cp in own:
                cp.start()

        for s in range(n_steps):
            group = [cp for j, cp in enumerate(copies) if (j * n_steps) // len(copies) == s]
            if group:
                @pl.when(step == s)
                def _(group=group):
                    for cp in group:
                        cp.start()

        @pl.when(step == n_steps - 1)
        def _():
            for cp in own:
                cp.wait()


def _host_call(body, name, grid, in_specs, args, out_shape, out_specs, scratch_shapes, after=None, sender=None):
    in_specs, args, out_shape, out_specs = list(in_specs), list(args), list(out_shape), list(out_specs)
    scratch_shapes = list(scratch_shapes)
    semantics = ("arbitrary",) * len(grid)
    body = _ordered_behind(body, in_specs, args, after)
    if sender is None:
        res = pl.pallas_call(body, name=name, grid=grid, in_specs=in_specs, out_specs=out_specs, out_shape=out_shape,
                             scratch_shapes=scratch_shapes, compiler_params=_params(semantics))(*args)
        return res, None
    n, n_in, n_out, n_scr = sender.n, len(in_specs), len(out_shape), len(scratch_shapes)
    n_sem = n * (N_DEV - 1)
    n_steps = 1
    for g in grid:
        n_steps *= g
    compute = body

    def body(*refs):
        ins, s_in = refs[:n_in], refs[n_in:n_in + 2 * n]
        o0 = n_in + 2 * n
        outs, s_out = refs[o0:o0 + n_out], refs[o0 + n_out:o0 + n_out + 2 + 2 * n]
        scr = refs[o0 + n_out + 2 + 2 * n:]
        compute(*ins, *outs, *scr[:n_scr])
        step = pl.program_id(0)
        for d in range(1, len(grid)):
            step = step * grid[d] + pl.program_id(d)
        sender.issue(s_in[:n], s_in[n:], s_out[0], s_out[1], scr[n_scr], step, n_steps)

    res = pl.pallas_call(
        body, name=name, grid=grid,
        in_specs=in_specs + [_HBM] * (2 * n), out_specs=out_specs + [_SEM, _SEM] + [_HBM] * (2 * n),
        out_shape=out_shape + [pltpu.SemaphoreType.DMA((n_sem,)), pltpu.SemaphoreType.DMA((n_sem,))]
        + [pltpu.HBM(a.shape, a.dtype) for a in sender.srcs] + [pltpu.HBM(a.shape, a.dtype) for a in sender.lands],
        input_output_aliases={n_in + j: n_out + 2 + j for j in range(2 * n)},
        scratch_shapes=scratch_shapes + [pltpu.SemaphoreType.DMA((n,))],
        compiler_params=pltpu.CompilerParams(dimension_semantics=semantics, vmem_limit_bytes=VMEM_LIMIT,
                                             has_side_effects=_EFFECT),
    )(*args, *sender.srcs, *sender.lands)
    handle = (sender.items, name, res[n_out], res[n_out + 1], res[n_out + 2:n_out + 2 + n],
              res[n_out + 2 + n:n_out + 2 + 2 * n])
    return res[:n_out], handle


def _exchange_wait(handle, after):
    items, name, send_sems, recv_sems, srcs, lands = handle
    n = len(items)

    def body(*refs):
        src_refs, land_refs = refs[:n], refs[n:2 * n]
        send_ref, recv_ref = refs[2 * n], refs[2 * n + 1]
        _, copies = _split_copies(items, src_refs, land_refs, send_ref, recv_ref)
        for cp in copies:
            cp.wait_send()
            cp.wait_recv()

    outs = pl.pallas_call(
        body, name=name + "_wait",
        out_shape=[pltpu.HBM(a.shape, a.dtype) for a in srcs] + [pltpu.HBM(a.shape, a.dtype) for a in lands],
        in_specs=[_HBM] * (2 * n) + [_SEM, _SEM, pl.BlockSpec(memory_space=pl.ANY)], out_specs=[_HBM] * (2 * n),
        input_output_aliases={i: i for i in range(2 * n)},
        compiler_params=pltpu.CompilerParams(has_side_effects=_EFFECT),
    )(*srcs, *lands, send_sems, recv_sems, after)
    return outs[n:]


def _mod_fwd(cvec, w_mod_l, b_mod_l):
    rows, cols = cvec.shape[0], w_mod_l.shape[1]

    def body(c_ref, w_ref, b_ref, o_ref, s_ref):
        cv = c_ref[...]
        s = cv * _sigmoid(cv)
        s_ref[...] = s
        o_ref[...] = _dot(s, w_ref[...]) + b_ref[...]

    return pl.pallas_call(
        body, name="mod_fwd",
        out_shape=(jax.ShapeDtypeStruct((rows, cols), F32), jax.ShapeDtypeStruct((rows, D), F32)),
        in_specs=[_full((rows, D)), _full((D, cols)), _full((1, cols))],
        out_specs=(_full((rows, cols)), _full((rows, D))), grid=(1,),
        compiler_params=_params(("arbitrary",)),
    )(cvec, w_mod_l, b_mod_l)


def _mod_bwd(svec, cvec, dmod_l, w_mod_l):
    rows, cols = dmod_l.shape

    def body(s_ref, c_ref, d_ref, w_ref, gw_ref, gc_ref):
        gw_ref[...] = _dot(s_ref[...], d_ref[...], "tn")
        cv = c_ref[...]
        sg = _sigmoid(cv)
        gc_ref[...] = _dot(d_ref[...], w_ref[...], "nt") * (sg * (1.0 + cv * (1.0 - sg)))

    return pl.pallas_call(
        body, name="mod_bwd",
        out_shape=(jax.ShapeDtypeStruct((D, cols), F32), jax.ShapeDtypeStruct((rows, D), F32)),
        in_specs=[_full((rows, D)), _full((rows, D)), _full((rows, cols)), _full((D, cols))],
        out_specs=(_full((D, cols)), _full((rows, D))), grid=(1,),
        compiler_params=_params(("arbitrary",)),
    )(svec, cvec, dmod_l, w_mod_l)


def _inproj(xt, modv, g, w_inT, n_cols, rows_per_example, name, after=None, sender=None):
    rows = xt.shape[0]
    tm = min(TOKEN_TILE, rows_per_example)
    per_b = rows_per_example // tm
    shared_mod = modv.shape[0] == 1

    def body(x_ref, mod_ref, g_ref, w_ref, p_ref, h_ref):
        x = x_ref[...]
        r = lax.rsqrt(jnp.mean(x * x, axis=-1, keepdims=True) + EPS)
        h = (x * r * g_ref[...]) * (1.0 + mod_ref[0, 1:2, :]) + mod_ref[0, 0:1, :]
        hb = h.astype(MXU_DTYPE)
        h_ref[...] = hb
        for j in range(n_cols // KW):
            p_ref[:, j * KW:(j + 1) * KW] = _dot(hb, w_ref[j * KW:(j + 1) * KW, :], "nt").astype(p_ref.dtype)

    mod_idx = (lambda i: (0, 0, 0)) if shared_mod else (lambda i: (i // per_b, 0, 0))
    in_specs = [pl.BlockSpec((tm, D), lambda i: (i, 0)), pl.BlockSpec((1, N_MOD, D), mod_idx), _full((1, D)),
                pl.BlockSpec((n_cols, D), lambda i: (0, 0), pipeline_mode=pl.Buffered(1))]
    (p, h), handle = _host_call(
        body, name, (rows // tm,), in_specs, [xt, modv, g, w_inT],
        [jax.ShapeDtypeStruct((rows, n_cols), MXU_DTYPE), jax.ShapeDtypeStruct((rows, D), MXU_DTYPE)],
        [pl.BlockSpec((tm, n_cols), lambda i: (i, 0)), pl.BlockSpec((tm, D), lambda i: (i, 0))], [],
        after=after, sender=sender)
    return p, h, handle


def _tri(reverse):
    row = lax.broadcasted_iota(jnp.int32, (CHUNK, CHUNK), 0)
    col = lax.broadcasted_iota(jnp.int32, (CHUNK, CHUNK), 1)
    return (col >= row) if reverse else (col <= row)


def _lower_bound(gam_ref, direction):
    return _sigmoid(gam_ref[direction:direction + 1, :] - gam_ref[2 + direction:3 + direction, :])


def _gate_prep(z, lb, tri_f):
    sg = _sigmoid(z)
    f = lb + (1.0 - lb) * sg
    g = jnp.log(f)
    b = _dotx(tri_f, g)
    bl = jnp.sum(g, axis=0, keepdims=True)
    return sg, f, 1.0 - f, b, bl


def _hgrn_fwd(p, gam, s0, rows_per_example, with_out, name, sender=None):
    rows = p.shape[0]
    nb_ex = rows // rows_per_example
    rb = min(TOKEN_TILE, rows_per_example)
    cpb = rb // CHUNK
    nb = rows_per_example // rb
    n_chunks = rows // CHUNK
    has_s0 = s0 is not None

    def body(*refs):
        it = iter(refs)
        gam_ref = next(it)
        zf_ref, vf_ref = next(it), next(it)
        qf_ref = next(it) if with_out else None
        zb_ref, vb_ref = next(it), next(it)
        qb_ref = next(it) if with_out else None
        s0_ref = next(it) if has_s0 else None
        if with_out:
            of_ref, ob_ref = next(it), next(it)
        stash_f, stash_b, fin_ref = next(it), next(it), next(it)
        st_ref = next(it)
        i = pl.program_id(1)

        @pl.when(i == 0)
        def _():
            if has_s0:
                st_ref[...] = s0_ref[:, 0]
            else:
                st_ref[...] = jnp.zeros_like(st_ref)

        for direction, (z_ref, v_ref, q_ref, stash) in enumerate(
                ((zf_ref, vf_ref, qf_ref, stash_f), (zb_ref, vb_ref, qb_ref, stash_b))):
            reverse = direction == 1
            tri = _tri(reverse)
            tri_f = tri.astype(F32)
            lb = _lower_bound(gam_ref, direction)
            order = range(cpb - 1, -1, -1) if reverse else range(cpb)
            for j in order:
                rs = slice(j * CHUNK, (j + 1) * CHUNK)
                z = z_ref[rs, :].astype(F32)
                v = v_ref[rs, :].astype(F32)
                _, _, k, b, bl = _gate_prep(z, lb, tri_f)
                mid = 0.5 * bl
                kd = k * jnp.exp(bl - b)
                a = jnp.exp(bl)
                if with_out:
                    q = q_ref[rs, :].astype(F32)
                    qi = q * jnp.exp(b - mid)
                    ki = k * jnp.exp(mid - b)
                    qe = q * jnp.exp(b)
                for h in range(HEADS):
                    hs = slice(h * DK, (h + 1) * DK)
                    st = st_ref[direction, h]
                    stash[j, h] = st.astype(stash.dtype)
                    if with_out:
                        sc = jnp.where(tri, _dot(qi[:, hs], ki[:, hs], "nt"), 0.0)
                        o = _dot(sc, v[:, hs]) + _dot(qe[:, hs], st, "nt")
                        (ob_ref if reverse else of_ref)[rs, hs] = o
                    st_ref[direction, h] = st * a[:, hs] + _dot(v[:, hs], kd[:, hs], "tn")

        @pl.when(i == nb - 1)
        def _():
            fin_ref[:, 0] = st_ref[...]

    up = lambda b, i: b * nb + i
    down = lambda b, i: b * nb + nb - 1 - i
    col = lambda rowf, c: pl.BlockSpec((rb, KW), lambda b, i: (rowf(b, i), c))
    in_specs = [_full((4, KW)), col(up, 0), col(up, 2)] + ([col(up, 3)] if with_out else [])
    in_specs += [col(down, 1), col(down, 2)] + ([col(down, 3)] if with_out else [])
    args = [gam, p, p] + ([p] if with_out else []) + [p, p] + ([p] if with_out else [])
    if has_s0:
        in_specs.append(pl.BlockSpec((2, 1, HEADS, DK, DK), lambda b, i: (0, b, 0, 0, 0)))
        args.append(s0)
    out_shape, out_specs = [], []
    if with_out:
        out_shape += [jax.ShapeDtypeStruct((rows, KW), F32)] * 2
        out_specs += [pl.BlockSpec((rb, KW), lambda b, i: (up(b, i), 0)),
                      pl.BlockSpec((rb, KW), lambda b, i: (down(b, i), 0))]
    out_shape += [jax.ShapeDtypeStruct((n_chunks, HEADS, DK, DK), MXU_DTYPE)] * 2
    out_specs += [pl.BlockSpec((cpb, HEADS, DK, DK), lambda b, i: (up(b, i), 0, 0, 0)),
                  pl.BlockSpec((cpb, HEADS, DK, DK), lambda b, i: (down(b, i), 0, 0, 0))]
    out_shape.append(jax.ShapeDtypeStruct((2, nb_ex, HEADS, DK, DK), F32))
    out_specs.append(pl.BlockSpec((2, 1, HEADS, DK, DK), lambda b, i: (0, b, 0, 0, 0)))
    res, handle = _host_call(body, name, (nb_ex, nb), in_specs, args, out_shape, out_specs,
                             [pltpu.VMEM((2, HEADS, DK, DK), F32)], sender=sender)
    return (*res, handle)


def _hgrn_bwd(p, gam, do, stash_f, stash_b, ds_end, rows_per_example, with_out, name, after=None, sender=None):
    rows = p.shape[0]
    nb_ex = rows // rows_per_example
    rb = min(TOKEN_TILE, rows_per_example)
    cpb = rb // CHUNK
    nb = rows_per_example // rb
    has_end = ds_end is not None

    def body(*refs):
        it = iter(refs)
        gam_ref = next(it)
        ins = []
        for _ in range(2):
            z_ref, v_ref = next(it), next(it)
            q_ref = next(it) if with_out else None
            do_ref = next(it) if with_out else None
            ins.append((z_ref, v_ref, q_ref, do_ref, next(it)))
        end_ref = next(it) if has_end else None
        outs = []
        for _ in range(2):
            dz_ref, dv_ref = next(it), next(it)
            dq_ref = next(it) if with_out else None
            outs.append((dz_ref, dv_ref, dq_ref))
        dlb_ref, ds0_ref = next(it), next(it)
        dst_ref = next(it)
        b_id, i = pl.program_id(0), pl.program_id(1)

        @pl.when(i == 0)
        def _():
            if has_end:
                dst_ref[...] = end_ref[:, 0]
            else:
                dst_ref[...] = jnp.zeros_like(dst_ref)

        @pl.when((i == 0) & (b_id == 0))
        def _():
            dlb_ref[...] = jnp.zeros_like(dlb_ref)

        for direction in range(2):
            z_ref, v_ref, q_ref, do_ref, stash = ins[direction]
            dz_ref, dv_ref, dq_ref = outs[direction]
            reverse = direction == 1
            tri = _tri(reverse)
            tri_f = tri.astype(F32)
            lb = _lower_bound(gam_ref, direction)
            order = range(cpb) if reverse else range(cpb - 1, -1, -1)
            dlb_acc = jnp.zeros((1, KW), F32)
            for j in order:
                rs = slice(j * CHUNK, (j + 1) * CHUNK)
                z = z_ref[rs, :].astype(F32)
                v = v_ref[rs, :].astype(F32)
                sg, f, k, b, bl = _gate_prep(z, lb, tri_f)
                mid = 0.5 * bl
                e3 = jnp.exp(bl - b)
                kd = k * e3
                a = jnp.exp(bl)
                if with_out:
                    q = q_ref[rs, :].astype(F32)
                    dout = do_ref[rs, :].astype(F32)
                    e1, e2, e4 = jnp.exp(b - mid), jnp.exp(mid - b), jnp.exp(b)
                    qi, ki, qe = q * e1, k * e2, q * e4
                dkd_p, dv_p, da_p, dqi_p, dki_p, dqe_p = [], [], [], [], [], []
                for h in range(HEADS):
                    hs = slice(h * DK, (h + 1) * DK)
                    st_in = stash[j, h]
                    dst = dst_ref[direction, h]
                    dkd_p.append(_dot(v[:, hs], dst))
                    dvh = _dot(kd[:, hs], dst, "nt")
                    da_p.append(jnp.sum(dst * st_in.astype(F32), axis=0, keepdims=True))
                    new_dst = dst * a[:, hs]
                    if with_out:
                        sc = jnp.where(tri, _dot(qi[:, hs], ki[:, hs], "nt"), 0.0)
                        dsc = jnp.where(tri, _dot(dout[:, hs], v[:, hs], "nt"), 0.0)
                        dqi_p.append(_dot(dsc, ki[:, hs]))
                        dki_p.append(_dot(dsc, qi[:, hs], "tn"))
                        dqe_p.append(_dot(dout[:, hs], st_in))
                        dvh = dvh + _dot(sc, dout[:, hs], "tn")
                        new_dst = new_dst + _dot(dout[:, hs], qe[:, hs], "tn")
                    dv_p.append(dvh)
                    dst_ref[direction, h] = new_dst
                cat = lambda parts: jnp.concatenate(parts, axis=1)
                dkd, da = cat(dkd_p), cat(da_p)
                dv_ref[rs, :] = cat(dv_p)
                t_kd = dkd * kd
                dk = dkd * e3
                db = -t_kd
                dbl = jnp.sum(t_kd, axis=0, keepdims=True) + da * a
                if with_out:
                    dqi, dki, dqe = cat(dqi_p), cat(dki_p), cat(dqe_p)
                    dq_ref[rs, :] = dqi * e1 + dqe * e4
                    dk = dk + dki * e2
                    t_qi, t_ki, t_qe = dqi * qi, dki * ki, dqe * qe
                    db = db + t_qi - t_ki + t_qe
                    dbl = dbl + 0.5 * jnp.sum(t_ki - t_qi, axis=0, keepdims=True)
                dg = _dotx(tri_f, db, "tn") + dbl
                df = dg / f - dk
                dz_ref[rs, :] = df * (1.0 - lb) * sg * (1.0 - sg)
                dlb_acc = dlb_acc + jnp.sum(df * (1.0 - sg), axis=0, keepdims=True)
            dlb_ref[direction:direction + 1, :] += dlb_acc

        @pl.when(i == nb - 1)
        def _():
            ds0_ref[:, 0] = dst_ref[...]

    rows_of = (lambda b, i: b * nb + nb - 1 - i, lambda b, i: b * nb + i)
    in_specs, args = [_full((4, KW))], [gam]
    for direction in range(2):
        rf = rows_of[direction]
        col = lambda c, rf=rf: pl.BlockSpec((rb, KW), lambda b, i: (rf(b, i), c))
        in_specs += [col(direction), col(2)]
        args += [p, p]
        if with_out:
            in_specs += [col(3), col(0)]
            args += [p, do]
        in_specs.append(pl.BlockSpec((cpb, HEADS, DK, DK), lambda b, i, rf=rf: (rf(b, i), 0, 0, 0)))
        args.append((stash_f, stash_b)[direction])
    if has_end:
        in_specs.append(pl.BlockSpec((2, 1, HEADS, DK, DK), lambda b, i: (0, b, 0, 0, 0)))
        args.append(ds_end)
    out_shape, out_specs = [], []
    for direction in range(2):
        rf = rows_of[direction]
        n_out = 3 if with_out else 2
        out_shape += [jax.ShapeDtypeStruct((rows, KW), F32)] * n_out
        out_specs += [pl.BlockSpec((rb, KW), lambda b, i, rf=rf: (rf(b, i), 0))] * n_out
    out_shape += [jax.ShapeDtypeStruct((2, KW), F32), jax.ShapeDtypeStruct((2, nb_ex, HEADS, DK, DK), F32)]
    out_specs += [_full((2, KW)), pl.BlockSpec((2, 1, HEADS, DK, DK), lambda b, i: (0, b, 0, 0, 0))]
    res, handle = _host_call(body, name, (nb_ex, nb), in_specs, args, out_shape, out_specs,
                             [pltpu.VMEM((2, HEADS, DK, DK), F32)], after=after, sender=sender)
    return (*res, handle)


def _tail_forward(osum, og, u, v, ga, gb, gna, ln_g, ln_b, ws_ref, bs_ref, wpaT_ref, wpbT_ref):
    tm = osum.shape[0]
    gna4 = jnp.concatenate([gna] * HEADS, axis=1)
    r_parts = []
    for h in range(HEADS):
        oh = osum[:, h * DK:(h + 1) * DK]
        r_parts.append(jnp.broadcast_to(lax.rsqrt(jnp.mean(oh * oh, axis=-1, keepdims=True) + EPS), (tm, DK)))
    r = jnp.concatenate(r_parts, axis=1)
    on = osum * r
    sg_og = _sigmoid(og)
    silu_og = og * sg_og
    oan = on * gna4
    oa = oan * silu_og
    ug, tu = _gelu(u)
    vg, tv = _gelu(v)
    mu = jnp.mean(vg, axis=-1, keepdims=True)
    vc = vg - mu
    rstd = lax.rsqrt(jnp.mean(vc * vc, axis=-1, keepdims=True) + EPS)
    vhat = vc * rstd
    vln = vhat * ln_g + ln_b
    blocks = []
    for n in range(tm // SGU_BLOCK):
        rs = slice(n * SGU_BLOCK, (n + 1) * SGU_BLOCK)
        blocks.append(jnp.concatenate(
            [_dot(ws_ref[g], vln[rs, g * DK:(g + 1) * DK]) + bs_ref[g] for g in range(GROUPS)], axis=1))
    mixed = jnp.concatenate(blocks, axis=0) if len(blocks) > 1 else blocks[0]
    obm = ug * mixed
    pa = _dot(oa, wpaT_ref[...], "nt")
    pb = _dot(obm, wpbT_ref[...], "nt")
    sga, sgb = _sigmoid(ga), _sigmoid(gb)
    merged = sga * pa + sgb * pb
    return dict(r=r, on=on, sg_og=sg_og, silu_og=silu_og, oan=oan, oa=oa, ug=ug, tu=tu, tv=tv, rstd=rstd, vhat=vhat,
                vln=vln, mixed=mixed, obm=obm, pa=pa, pb=pb, sga=sga, sgb=sgb, merged=merged, gna4=gna4)


def _tail_in_specs(tm):
    tile = lambda c: pl.BlockSpec((tm, KW), lambda i: (i, c))
    return [tile(c) for c in range(4, 11)]


def _tail_weight_specs():
    return [_full((1, DK)), _full((1, KW)), _full((1, KW)), _full((GROUPS, SGU_BLOCK, SGU_BLOCK)),
            _full((GROUPS, SGU_BLOCK, 1)), _full((D, KW), single=True), _full((D, KW), single=True),
            _full((D, D), single=True)]


def _read_tail_inputs(of_ref, ob_ref, pcols):
    osum = of_ref[...] + ob_ref[...]
    og, u, v = (pcols[j][...].astype(F32) for j in range(3))
    ga = jnp.concatenate([pcols[3][...], pcols[4][...]], axis=1).astype(F32)
    gb = jnp.concatenate([pcols[5][...], pcols[6][...]], axis=1).astype(F32)
    return osum, og, u, v, ga, gb


def _tail_fwd(p, o_up, o_down, xt, modv, gna, ln_g, ln_b, w_s, b_s, w_paT, w_pbT, w_o, rows_per_example):
    rows = xt.shape[0]
    tm = min(TOKEN_TILE, rows_per_example)
    per_b = rows_per_example // tm

    def body(of_ref, ob_ref, *rest):
        pcols = rest[:7]
        (x_ref, mod_ref, gna_ref, lng_ref, lnb_ref, ws_ref, bs_ref, wpaT_ref, wpbT_ref, wo_ref,
         x1_ref, mix_ref, merged_ref, oa_ref, obm_ref) = rest[7:]
        t = _tail_forward(*_read_tail_inputs(of_ref, ob_ref, pcols), gna_ref[...], lng_ref[...], lnb_ref[...],
                          ws_ref, bs_ref, wpaT_ref, wpbT_ref)
        mix = _dot(t["merged"], wo_ref[...])
        x1_ref[...] = x_ref[...] + mod_ref[0, 2:3, :] * mix
        mix_ref[...] = mix.astype(mix_ref.dtype)
        merged_ref[...] = t["merged"].astype(merged_ref.dtype)
        oa_ref[...] = t["oa"].astype(oa_ref.dtype)
        obm_ref[...] = t["obm"].astype(obm_ref.dtype)

    row = lambda w: pl.BlockSpec((tm, w), lambda i: (i, 0))
    in_specs = [row(KW), row(KW)] + _tail_in_specs(tm) + [row(D), pl.BlockSpec((1, N_MOD, D), lambda i: (i // per_b, 0, 0))]
    in_specs += _tail_weight_specs()
    return pl.pallas_call(
        body, name="tail_fwd", grid=(rows // tm,),
        out_shape=(jax.ShapeDtypeStruct((rows, D), F32), jax.ShapeDtypeStruct((rows, D), MXU_DTYPE),
                   jax.ShapeDtypeStruct((rows, D), MXU_DTYPE), jax.ShapeDtypeStruct((rows, KW), MXU_DTYPE),
                   jax.ShapeDtypeStruct((rows, KW), MXU_DTYPE)),
        in_specs=in_specs, out_specs=(row(D), row(D), row(D), row(KW), row(KW)),
        compiler_params=_params(("arbitrary",)),
    )(o_up, o_down, *([p] * 7), xt, modv, gna, ln_g, ln_b, w_s, b_s, w_paT, w_pbT, w_o)


def _tail_bwd(p, o_up, o_down, dx1, mix, modv, gna, ln_g, ln_b, w_s, b_s, w_paT, w_pbT, w_o, rows_per_example,
              after=None, sender=None):
    rows = dx1.shape[0]
    nb_ex = rows // rows_per_example
    tm = min(TOKEN_TILE, rows_per_example)
    per_b = rows_per_example // tm

    def body(of_ref, ob_ref, *rest):
        pcols = rest[:7]
        (dx1_ref, mix_ref, mod_ref, gna_ref, lng_ref, lnb_ref, ws_ref, bs_ref, wpaT_ref, wpbT_ref, wo_ref,
         dpt_ref, do_ref, dmix_ref, dpa_ref, dpb_ref, dmod_ref, small_ref, dws_ref, dbs_ref) = rest[7:]
        i = pl.program_id(0)

        @pl.when(i == 0)
        def _():
            small_ref[...] = jnp.zeros_like(small_ref)
            dws_ref[...] = jnp.zeros_like(dws_ref)
            dbs_ref[...] = jnp.zeros_like(dbs_ref)

        @pl.when(i % per_b == 0)
        def _():
            dmod_ref[...] = jnp.zeros_like(dmod_ref)

        osum, og, u, v, ga, gb = _read_tail_inputs(of_ref, ob_ref, pcols)
        ln_g = lng_ref[...]
        t = _tail_forward(osum, og, u, v, ga, gb, gna_ref[...], ln_g, lnb_ref[...], ws_ref, bs_ref, wpaT_ref, wpbT_ref)
        dx1v = dx1_ref[...]
        dmod_ref[0, 2:3, :] += jnp.sum(dx1v * mix_ref[...].astype(F32), axis=0, keepdims=True)
        dmix = dx1v * mod_ref[0, 2:3, :]
        dmix_ref[...] = dmix.astype(dmix_ref.dtype)
        dmerged = _dot(dmix, wo_ref[...], "nt")
        sga, sgb = t["sga"], t["sgb"]
        dpa = dmerged * sga
        dpb = dmerged * sgb
        dpa_ref[...] = dpa.astype(dpa_ref.dtype)
        dpb_ref[...] = dpb.astype(dpb_ref.dtype)
        dga = dmerged * t["pa"] * sga * (1.0 - sga)
        dgb = dmerged * t["pb"] * sgb * (1.0 - sgb)
        doa = _dot(dpa, wpaT_ref[...])
        dobm = _dot(dpb, wpbT_ref[...])
        dug = dobm * t["mixed"]
        dmixed = dobm * t["ug"]
        du = dug * _gelu_grad(u, t["tu"])
        dvln_blocks = []
        for n in range(tm // SGU_BLOCK):
            rs = slice(n * SGU_BLOCK, (n + 1) * SGU_BLOCK)
            parts = []
            for g in range(GROUPS):
                gs = slice(g * DK, (g + 1) * DK)
                dm = dmixed[rs, gs]
                parts.append(_dot(ws_ref[g], dm, "tn"))
                dws_ref[g] += _dot(dm, t["vln"][rs, gs], "nt")
                dbs_ref[g] += jnp.sum(dm, axis=1, keepdims=True)
            dvln_blocks.append(jnp.concatenate(parts, axis=1))
        dvln = jnp.concatenate(dvln_blocks, axis=0) if len(dvln_blocks) > 1 else dvln_blocks[0]
        vhat = t["vhat"]
        small_ref[1:2, 0:KW] += jnp.sum(dvln * vhat, axis=0, keepdims=True)
        small_ref[2:3, 0:KW] += jnp.sum(dvln, axis=0, keepdims=True)
        dvhat = dvln * ln_g
        dvg = t["rstd"] * (dvhat - jnp.mean(dvhat, axis=-1, keepdims=True)
                           - vhat * jnp.mean(dvhat * vhat, axis=-1, keepdims=True))
        dv = dvg * _gelu_grad(v, t["tv"])
        sg_og = t["sg_og"]
        doan = doa * t["silu_og"]
        dog = doa * t["oan"] * (sg_og * (1.0 + og * (1.0 - sg_og)))
        prod = doan * t["on"]
        dgna = jnp.zeros((1, DK), F32)
        for h in range(HEADS):
            dgna = dgna + jnp.sum(prod[:, h * DK:(h + 1) * DK], axis=0, keepdims=True)
        small_ref[0:1, 0:DK] += dgna
        don = doan * t["gna4"]
        dot_parts = []
        for h in range(HEADS):
            hs = slice(h * DK, (h + 1) * DK)
            m = jnp.mean(don[:, hs] * t["on"][:, hs], axis=-1, keepdims=True)
            dot_parts.append(t["r"][:, hs] * (don[:, hs] - t["on"][:, hs] * m))
        do_ref[...] = jnp.concatenate(dot_parts, axis=1).astype(do_ref.dtype)
        for j, val in enumerate((dog, du, dv)):
            dpt_ref[:, j * KW:(j + 1) * KW] = val.astype(dpt_ref.dtype)
        dpt_ref[:, 3 * KW:3 * KW + D] = dga.astype(dpt_ref.dtype)
        dpt_ref[:, 3 * KW + D:] = dgb.astype(dpt_ref.dtype)

    row = lambda w: pl.BlockSpec((tm, w), lambda i: (i, 0))
    in_specs = [row(KW), row(KW)] + _tail_in_specs(tm) + [row(D), row(D), pl.BlockSpec((1, N_MOD, D), lambda i: (i // per_b, 0, 0))]
    in_specs += _tail_weight_specs()
    args = [o_up, o_down, *([p] * 7), dx1, mix, modv, gna, ln_g, ln_b, w_s, b_s, w_paT, w_pbT, w_o]
    cd = MXU_DTYPE
    res, handle = _host_call(
        body, "tail_bwd", (rows // tm,), in_specs, args,
        [jax.ShapeDtypeStruct((rows, TAIL_COLS), cd), jax.ShapeDtypeStruct((rows, KW), cd),
         jax.ShapeDtypeStruct((rows, D), cd), jax.ShapeDtypeStruct((rows, D), cd),
         jax.ShapeDtypeStruct((rows, D), cd), jax.ShapeDtypeStruct((nb_ex, 8, D), F32),
         jax.ShapeDtypeStruct((8, D), F32), jax.ShapeDtypeStruct((GROUPS, SGU_BLOCK, SGU_BLOCK), F32),
         jax.ShapeDtypeStruct((GROUPS, SGU_BLOCK, 1), F32)],
        [row(TAIL_COLS), row(KW), row(D), row(D), row(D),
         pl.BlockSpec((1, 8, D), lambda i: (i // per_b, 0, 0)), _full((8, D)),
         _full((GROUPS, SGU_BLOCK, SGU_BLOCK)), _full((GROUPS, SGU_BLOCK, 1))], [],
        after=after, sender=sender)
    return (*res, handle)


def _ffn(x1, target, modv, g_ffn, g_final, w_upT, w_down, rows_per_example):
    rows = x1.shape[0]
    nb_ex = rows // rows_per_example
    tm = min(TOKEN_TILE, rows_per_example)
    per_b = rows_per_example // tm
    n_ff = D_FF // FF_CHUNK

    def body(x1_ref, tgt_ref, mod_ref, gffn_ref, gfin_ref, wup_ref, wdn_ref,
             dx1_ref, h2_ref, dffn_ref, act_ref, dup_ref, dmod_ref, small_ref, a_scr, b_scr):
        i = pl.program_id(0)

        @pl.when(i == 0)
        def _():
            small_ref[...] = jnp.zeros_like(small_ref)

        @pl.when(i % per_b == 0)
        def _():
            dmod_ref[...] = jnp.zeros_like(dmod_ref)

        x1v = x1_ref[...]
        g2 = gffn_ref[...]
        m3, m4, m5 = mod_ref[0, 3:4, :], mod_ref[0, 4:5, :], mod_ref[0, 5:6, :]
        r2 = lax.rsqrt(jnp.mean(x1v * x1v, axis=-1, keepdims=True) + EPS)
        xn2 = x1v * r2
        h2 = (xn2 * g2) * (1.0 + m4) + m3
        h2b = h2.astype(MXU_DTYPE)
        h2_ref[...] = h2b
        ffn = jnp.zeros((tm, D), F32)
        for j in range(n_ff):
            cs = slice(j * FF_CHUNK, (j + 1) * FF_CHUNK)
            a = _dot(h2b, wup_ref[j * FF_CHUNK:(j + 1) * FF_CHUNK, :], "nt")
            bgate = _dot(h2b, wup_ref[D_FF + j * FF_CHUNK:D_FF + (j + 1) * FF_CHUNK, :], "nt")
            a_scr[:, cs] = a
            b_scr[:, cs] = bgate
            act = (a * _sigmoid(a) * bgate).astype(MXU_DTYPE)
            act_ref[:, cs] = act
            ffn = ffn + _dot(act, wdn_ref[cs, :])
        x2 = x1v + m5 * ffn
        r3 = lax.rsqrt(jnp.mean(x2 * x2, axis=-1, keepdims=True) + EPS)
        xn3 = x2 * r3
        gf = gfin_ref[...]
        err = xn3 * gf - tgt_ref[...]
        loss = 0.5 * jnp.sum(jnp.mean(err * err, axis=-1, keepdims=True), axis=0, keepdims=True)
        small_ref[2:3, :] += jnp.broadcast_to(loss, (1, D))
        dy = err * (1.0 / D)
        small_ref[1:2, :] += jnp.sum(dy * xn3, axis=0, keepdims=True)
        dxn3 = dy * gf
        dx2 = r3 * (dxn3 - xn3 * jnp.mean(dxn3 * xn3, axis=-1, keepdims=True))
        dmod_ref[0, 5:6, :] += jnp.sum(dx2 * ffn, axis=0, keepdims=True)
        dffn = (dx2 * m5).astype(MXU_DTYPE)
        dffn_ref[...] = dffn
        dh2 = jnp.zeros((tm, D), F32)
        for j in range(n_ff):
            cs = slice(j * FF_CHUNK, (j + 1) * FF_CHUNK)
            dact = _dot(dffn, wdn_ref[cs, :], "nt")
            a, bgate = a_scr[:, cs], b_scr[:, cs]
            s = _sigmoid(a)
            da = (dact * bgate * (s * (1.0 + a * (1.0 - s)))).astype(MXU_DTYPE)
            dbg = (dact * a * s).astype(MXU_DTYPE)
            dup_ref[:, cs] = da
            dup_ref[:, D_FF + j * FF_CHUNK:D_FF + (j + 1) * FF_CHUNK] = dbg
            dh2 = dh2 + _dot(da, wup_ref[j * FF_CHUNK:(j + 1) * FF_CHUNK, :])
            dh2 = dh2 + _dot(dbg, wup_ref[D_FF + j * FF_CHUNK:D_FF + (j + 1) * FF_CHUNK, :])
        dmod_ref[0, 3:4, :] += jnp.sum(dh2, axis=0, keepdims=True)
        dmod_ref[0, 4:5, :] += jnp.sum(dh2 * xn2 * g2, axis=0, keepdims=True)
        small_ref[0:1, :] += jnp.sum(dh2 * (1.0 + m4) * xn2, axis=0, keepdims=True)
        dxn2 = dh2 * g2 * (1.0 + m4)
        dx1_ref[...] = dx2 + r2 * (dxn2 - xn2 * jnp.mean(dxn2 * xn2, axis=-1, keepdims=True))

    row = lambda w: pl.BlockSpec((tm, w), lambda i: (i, 0))
    cd = MXU_DTYPE
    return pl.pallas_call(
        body, name="ffn_fwd_bwd", grid=(rows // tm,),
        out_shape=(jax.ShapeDtypeStruct((rows, D), F32), jax.ShapeDtypeStruct((rows, D), cd),
                   jax.ShapeDtypeStruct((rows, D), cd), jax.ShapeDtypeStruct((rows, D_FF), cd),
                   jax.ShapeDtypeStruct((rows, 2 * D_FF), cd), jax.ShapeDtypeStruct((nb_ex, 8, D), F32),
                   jax.ShapeDtypeStruct((8, D), F32)),
        in_specs=[row(D), row(D), pl.BlockSpec((1, N_MOD, D), lambda i: (i // per_b, 0, 0)), _full((1, D)), _full((1, D)),
                  _full((2 * D_FF, D), single=True), _full((D_FF, D), single=True)],
        out_specs=(row(D), row(D), row(D), row(D_FF), row(2 * D_FF),
                   pl.BlockSpec((1, 8, D), lambda i: (i // per_b, 0, 0)), _full((8, D))),
        scratch_shapes=[pltpu.VMEM((tm, D_FF), F32), pltpu.VMEM((tm, D_FF), F32)],
        compiler_params=_params(("arbitrary",)),
    )(x1, target, modv, g_ffn, g_final, w_upT, w_down)


def _inproj_bwd(pieces, dpt, xt, dx1, modv, g, w_inT, rows_per_example, name):
    rows = xt.shape[0]
    latent = dx1 is not None
    n_cols = IN_COLS if latent else CTX_COLS
    tm = min(TOKEN_TILE, rows_per_example)
    per_b = rows_per_example // tm
    n_mod_blocks = rows // rows_per_example if latent else 1
    n_pieces = len(pieces)

    def body(*refs):
        it = iter(refs)
        pc = [next(it) for _ in range(n_pieces)]
        dpt_ref = next(it) if latent else None
        x_ref = next(it)
        dx1_ref = next(it) if latent else None
        mod_ref, g_ref, w_ref = next(it), next(it), next(it)
        gx_ref = next(it) if latent else None
        dp_ref, dmod_ref, small_ref = next(it), next(it), next(it)
        i = pl.program_id(0)

        @pl.when(i == 0)
        def _():
            small_ref[...] = jnp.zeros_like(small_ref)

        @pl.when((i % per_b == 0) if latent else (i == 0))
        def _():
            dmod_ref[...] = jnp.zeros_like(dmod_ref)

        cols = [pc[0][...], pc[1][...], pc[2][...] + pc[3][...]]
        if latent:
            cols.append(pc[4][...] + pc[5][...])
        dh = jnp.zeros((tm, D), F32)
        for j, val in enumerate(cols):
            vb = val.astype(MXU_DTYPE)
            dp_ref[:, j * KW:(j + 1) * KW] = vb
            dh = dh + _dot(vb, w_ref[j * KW:(j + 1) * KW, :])
        if latent:
            for j in range(4, IN_COLS // KW):
                vb = dpt_ref[:, (j - 4) * KW:(j - 3) * KW]
                dp_ref[:, j * KW:(j + 1) * KW] = vb
                dh = dh + _dot(vb, w_ref[j * KW:(j + 1) * KW, :])
        x = x_ref[...]
        gv = g_ref[...]
        m1 = mod_ref[0, 1:2, :]
        r = lax.rsqrt(jnp.mean(x * x, axis=-1, keepdims=True) + EPS)
        xn = x * r
        dmod_ref[0, 0:1, :] += jnp.sum(dh, axis=0, keepdims=True)
        dmod_ref[0, 1:2, :] += jnp.sum(dh * xn * gv, axis=0, keepdims=True)
        small_ref[0:1, :] += jnp.sum(dh * (1.0 + m1) * xn, axis=0, keepdims=True)
        if latent:
            dxn = dh * gv * (1.0 + m1)
            gx_ref[...] = dx1_ref[...] + r * (dxn - xn * jnp.mean(dxn * xn, axis=-1, keepdims=True))

    row = lambda w: pl.BlockSpec((tm, w), lambda i: (i, 0))
    mod_idx = (lambda i: (i // per_b, 0, 0)) if latent else (lambda i: (0, 0, 0))
    in_specs = [row(KW)] * n_pieces + ([row(TAIL_COLS)] if latent else []) + [row(D)] + ([row(D)] if latent else [])
    in_specs += [pl.BlockSpec((1, N_MOD, D), mod_idx), _full((1, D)),
                 pl.BlockSpec((n_cols, D), lambda i: (0, 0), pipeline_mode=pl.Buffered(1))]
    args = list(pieces) + ([dpt] if latent else []) + [xt] + ([dx1] if latent else []) + [modv, g, w_inT]
    out_shape = ([jax.ShapeDtypeStruct((rows, D), F32)] if latent else []) + [
        jax.ShapeDtypeStruct((rows, n_cols), MXU_DTYPE), jax.ShapeDtypeStruct((n_mod_blocks, 8, D), F32),
        jax.ShapeDtypeStruct((8, D), F32)]
    out_specs = ([row(D)] if latent else []) + [row(n_cols), pl.BlockSpec((1, 8, D), mod_idx), _full((8, D))]
    return pl.pallas_call(
        body, name=name, grid=(rows // tm,), out_shape=out_shape, in_specs=in_specs, out_specs=out_specs,
        compiler_params=_params(("arbitrary",)),
    )(*args)


def _grad_matmul(a, b, name, init=None, tn=512, tt=1024):
    rows, n = a.shape
    k = b.shape[1]
    tn = min(tn, n)
    tt = min(tt, rows)
    steps = rows // tt
    has_init = init is not None
    init_blocks = init.shape[0] // tn if has_init else 0

    def body(*refs):
        if has_init:
            a_ref, b_ref, init_ref, o_ref, acc = refs
        else:
            a_ref, b_ref, o_ref, acc = refs
        i, t = pl.program_id(0), pl.program_id(1)

        @pl.when(t == 0)
        def _():
            acc[...] = jnp.zeros_like(acc)

        if has_init:
            @pl.when((t == 0) & (i < init_blocks))
            def _():
                acc[...] = init_ref[...].astype(F32)

        acc[...] += _dot(a_ref[...], b_ref[...], "tn")

        @pl.when(t == steps - 1)
        def _():
            o_ref[...] = acc[...].astype(o_ref.dtype)

    in_specs = [pl.BlockSpec((tt, tn), lambda i, t: (t, i)), pl.BlockSpec((tt, k), lambda i, t: (t, 0))]
    args = [a, b]
    if has_init:
        in_specs.append(pl.BlockSpec((tn, k), lambda i, t: (jnp.minimum(i, init_blocks - 1), 0)))
        args.append(init)
    return pl.pallas_call(
        body, name=name, grid=(n // tn, steps), out_shape=jax.ShapeDtypeStruct((n, k), PAYLOAD_DTYPE),
        in_specs=in_specs, out_specs=pl.BlockSpec((tn, k), lambda i, t: (i, 0)),
        scratch_shapes=[pltpu.VMEM((tn, k), F32)],
        compiler_params=_params(("arbitrary", "arbitrary")),
    )(*args)


def _row_tile(rows, limit=256):
    if rows <= limit:
        return rows
    for t in range(limit, 7, -8):
        if rows % t == 0:
            return t
    return rows


def _sum8(stack, name):
    _, rows, cols = stack.shape
    tr = _row_tile(rows)

    def body(s_ref, o_ref):
        acc = s_ref[0].astype(F32)
        for j in range(1, N_DEV):
            acc = acc + s_ref[j].astype(F32)
        o_ref[...] = acc

    return pl.pallas_call(
        body, name=name, grid=(rows // tr,), out_shape=jax.ShapeDtypeStruct((rows, cols), F32),
        in_specs=[pl.BlockSpec((N_DEV, tr, cols), lambda i: (0, i, 0))],
        out_specs=pl.BlockSpec((tr, cols), lambda i: (i, 0)),
        compiler_params=_params(("arbitrary",)),
    )(stack)


def _small_reduce(stack, gam, nb_ex):
    def body(s_ref, gam_ref, o_ref, bm_ref):
        acc = s_ref[0]
        for j in range(1, N_DEV):
            acc = acc + s_ref[j]
        o_ref[...] = acc
        bm = acc[8:8 + N_MOD, :]
        for e in range(nb_ex):
            bm = bm + acc[16 + e * N_MOD:16 + (e + 1) * N_MOD, :]
        lb = jnp.concatenate([_lower_bound(gam_ref, 0), _lower_bound(gam_ref, 1)], axis=1)
        dgam = acc[7:8, :] * lb * (1.0 - lb)
        bm_ref[...] = jnp.concatenate([bm, dgam, -dgam], axis=0)

    return pl.pallas_call(
        body, name="small_reduce", grid=(1,),
        out_shape=(jax.ShapeDtypeStruct((SMALL_ROWS, D), F32), jax.ShapeDtypeStruct((8, D), F32)),
        in_specs=[_full((N_DEV, SMALL_ROWS, D)), _full((4, KW))], out_specs=(_full((SMALL_ROWS, D)), _full((8, D))),
        compiler_params=_params(("arbitrary",)),
    )(stack, gam)


def _adamw_update(w, gv, m, v):
    nm = ADAM_B1 * m + (1.0 - ADAM_B1) * gv
    nv = ADAM_B2 * v + (1.0 - ADAM_B2) * (gv * gv)
    m_hat = nm / (1.0 - ADAM_B1 ** ADAM_STEP)
    v_hat = nv / (1.0 - ADAM_B2 ** ADAM_STEP)
    return -ADAM_LR * (m_hat / (jnp.sqrt(v_hat) + ADAM_EPS) + ADAM_WD * w), nm, nv


def _adamw_sum8(stack, w, m, v, name):
    _, rows, cols = stack.shape
    tr = _row_tile(rows)

    def body(s_ref, w_ref, m_ref, v_ref, g_ref, d_ref, nm_ref, nv_ref):
        gv = s_ref[0].astype(F32)
        for j in range(1, N_DEV):
            gv = gv + s_ref[j].astype(F32)
        g_ref[...] = gv
        d_ref[...], nm_ref[...], nv_ref[...] = _adamw_update(w_ref[...], gv, m_ref[...], v_ref[...])

    blk = pl.BlockSpec((tr, cols), lambda i: (i, 0))
    sd = jax.ShapeDtypeStruct((rows, cols), F32)
    return pl.pallas_call(
        body, name=name, grid=(rows // tr,), out_shape=(sd, sd, sd, sd),
        in_specs=[pl.BlockSpec((N_DEV, tr, cols), lambda i: (0, i, 0)), blk, blk, blk], out_specs=(blk, blk, blk, blk),
        compiler_params=_params(("arbitrary",)),
    )(stack, w, m, v)


def _adamw(w, g, m, v, name):
    shape = w.shape
    cols = shape[-1]
    rows = 1
    for s in shape[:-1]:
        rows *= s
    tr = _row_tile(rows)

    def body(w_ref, g_ref, m_ref, v_ref, d_ref, nm_ref, nv_ref):
        gv = g_ref[...]
        nm = ADAM_B1 * m_ref[...] + (1.0 - ADAM_B1) * gv
        nv = ADAM_B2 * v_ref[...] + (1.0 - ADAM_B2) * (gv * gv)
        m_hat = nm / (1.0 - ADAM_B1 ** ADAM_STEP)
        v_hat = nv / (1.0 - ADAM_B2 ** ADAM_STEP)
        d_ref[...] = -ADAM_LR * (m_hat / (jnp.sqrt(v_hat) + ADAM_EPS) + ADAM_WD * w_ref[...])
        nm_ref[...] = nm
        nv_ref[...] = nv

    blk = pl.BlockSpec((tr, cols), lambda i: (i, 0))
    sd = jax.ShapeDtypeStruct((rows, cols), F32)
    d, nm, nv = pl.pallas_call(
        body, name=name, grid=(rows // tr,), out_shape=(sd, sd, sd), in_specs=[blk] * 4, out_specs=(blk, blk, blk),
        compiler_params=_params(("arbitrary",)),
    )(w.reshape(rows, cols), g.reshape(rows, cols), m.reshape(rows, cols), v.reshape(rows, cols))
    return d.reshape(shape), nm.reshape(shape), nv.reshape(shape)


class _LocalWeights:
    def __init__(self, w_upT, w_down, w_o, w_paT, w_pbT):
        self.weights = (w_upT, w_down, w_o, w_paT, w_pbT)
        self.grads = {}

    def sender(self, stage, grads=None):
        if grads is not None:
            self.grads[stage] = grads
        return None

    def sent(self, stage, handle):
        pass

    def mixer_weights(self, after):
        return self.weights[1:]

    def ffn_weights(self, after):
        return self.weights[0]


def _local_step(x, ctx, target, modv, mcv, gam, g_mix, g_ffn, gna, ln_g, ln_b, w_s, b_s, g_final, w_inT, comm):
    nb_ex, seq, _ = x.shape
    ctx_len = ctx.shape[1]
    xt = x.reshape(nb_ex * seq, D)
    ct = ctx.reshape(nb_ex * ctx_len, D)
    tgt = target.reshape(nb_ex * seq, D)
    bs3 = b_s.reshape(GROUPS, SGU_BLOCK, 1)

    pc, hc, _ = _inproj(ct, mcv, g_mix, w_inT, CTX_COLS, ctx_len, "inproj_ctx")
    p, h, handle = _inproj(xt, modv, g_mix, w_inT, IN_COLS, seq, "inproj_lat", sender=comm.sender("inproj"))
    comm.sent("inproj", handle)
    cst_f, cst_b, s_ctx, _ = _hgrn_fwd(pc, gam, None, ctx_len, False, "hgrn_fwd_ctx")
    o_up, o_down, st_f, st_b, _, handle = _hgrn_fwd(p, gam, s_ctx, seq, True, "hgrn_fwd_lat",
                                                    sender=comm.sender("scan"))
    comm.sent("scan", handle)
    w_down, w_o, w_paT, w_pbT = comm.mixer_weights(o_up)
    x1, mix, merged, oa, obm = _tail_fwd(p, o_up, o_down, xt, modv, gna, ln_g, ln_b, w_s, bs3, w_paT, w_pbT, w_o, seq)
    w_upT = comm.ffn_weights(x1)
    dx1, h2, dffn, act, dup, dmod_ffn, small_ffn = _ffn(x1, tgt, modv, g_ffn, g_final, w_upT, w_down, seq)
    gw_upT = _grad_matmul(dup, h2, "gw_up")
    gw_down = _grad_matmul(act, dffn, "gw_down", tn=256)
    dpt, do, dmix, dpa, dpb, dmod_tail, small_tail, dws, dbs, handle = _tail_bwd(
        p, o_up, o_down, dx1, mix, modv, gna, ln_g, ln_b, w_s, bs3, w_paT, w_pbT, w_o, seq,
        sender=comm.sender("tail_bwd", (gw_upT,)))
    comm.sent("tail_bwd", handle)
    gw_o = _grad_matmul(merged, dmix, "gw_o")
    gw_paT = _grad_matmul(dpa, oa, "gw_pa")
    gw_pbT = _grad_matmul(dpb, obm, "gw_pb")
    dzf, dvf, dqf, dzb, dvb, dqb, dlb, ds0, handle = _hgrn_bwd(
        p, gam, do, st_f, st_b, None, seq, True, "hgrn_bwd_lat",
        sender=comm.sender("scan_bwd", (gw_down, gw_o, gw_paT, gw_pbT)))
    comm.sent("scan_bwd", handle)
    czf, cvf, czb, cvb, dlb_c, _, _ = _hgrn_bwd(pc, gam, None, cst_f, cst_b, ds0, ctx_len, False, "hgrn_bwd_ctx")
    grad_x, dp, dmod_in, small_in = _inproj_bwd([dzf, dzb, dvf, dvb, dqf, dqb], dpt, xt, dx1, modv, g_mix, w_inT,
                                                 seq, "inproj_bwd_lat")
    dpc, dmc, small_c = _inproj_bwd([czf, czb, cvf, cvb], None, ct, None, mcv, g_mix, w_inT, ctx_len, "inproj_bwd_ctx")
    g_ctx = _grad_matmul(dpc, hc, "gw_in_ctx")
    gw_inT = _grad_matmul(dp, h, "gw_in", init=g_ctx)

    z = lambda r: jnp.zeros((r, D), F32)
    pad = lambda a: jnp.pad(a, ((0, 0), (0, D - a.shape[1])))
    dlb_row = (dlb + dlb_c).reshape(1, 2 * KW)
    dmod = dmod_in + dmod_tail + dmod_ffn
    small = jnp.concatenate([
        small_in[0:1] + small_c[0:1],
        small_ffn[0:1],
        small_ffn[1:2],
        small_tail[0:1],
        small_tail[1:2],
        small_tail[2:3],
        pad(dbs.reshape(1, GROUPS * SGU_BLOCK)),
        dlb_row,
        dmc[0, 0:N_MOD],
        z(2),
        dmod[:, 0:N_MOD].reshape(nb_ex * N_MOD, D),
        z(24 - nb_ex * N_MOD),
        dws.reshape(GROUPS * SGU_BLOCK * SGU_BLOCK // D, D),
    ], axis=0)
    loss = small_ffn[2, 0]
    return loss, grad_x.reshape(x.shape), gw_inT, small


def kernel(x, c, ctx, c_ctx, w_mod, b_mod, g_mix, g_ffn, w_in, lb_gamma, g_norm_a, ln_v_g, ln_v_b, w_s, b_s, w_pa, w_pb, w_o, w_up, w_down, g_final, loss_target, m_c_ctx, m_w_mod, m_b_mod, m_g_mix, m_g_ffn, m_w_in, m_lb_gamma, m_g_norm_a, m_ln_v_g, m_ln_v_b, m_w_s, m_b_s, m_w_pa, m_w_pb, m_w_o, m_w_up, m_w_down, m_g_final, v_c_ctx, v_w_mod, v_b_mod, v_g_mix, v_g_ffn, v_w_in, v_lb_gamma, v_g_norm_a, v_ln_v_g, v_ln_v_b, v_w_s, v_b_s, v_w_pa, v_w_pb, v_w_o, v_w_up, v_w_down, v_g_final):
    nb_ex = x.shape[0]
    me = 4 * lax.axis_index("x") + 2 * lax.axis_index("y") + lax.axis_index("c")
    cd = MXU_DTYPE
    mod_cols = w_mod.shape[2]
    lb_cols = lb_gamma.shape[2]

    w_inT_l = w_in[0].T.astype(cd)
    w_upT_l = w_up[0].T.astype(cd)
    w_paT_l = w_pa[0].T.astype(cd)
    w_pbT_l = w_pb[0].T.astype(cd)
    cl = jnp.concatenate([c, jnp.pad(lb_gamma.reshape(1, 4 * lb_cols), ((0, 0), (0, D - 4 * lb_cols))),
                          jnp.zeros((8 - nb_ex - 1, D), F32)], axis=0)
    g_in, g_cl = _exchange([(w_inT_l, "gather"), (cl, "gather")], "gather_w_in")
    w_inT = g_in.reshape(IN_COLS, D)
    c_all = g_cl[:, 0:nb_ex].reshape(N_DEV * nb_ex, D)
    gam = jnp.transpose(g_cl[:, nb_ex, 0:4 * lb_cols].reshape(N_DEV, 4, lb_cols), (1, 0, 2)).reshape(4, KW)

    n_c = N_DEV * nb_ex
    cvec = jnp.concatenate([c_all, c_ctx.reshape(1, D), jnp.zeros((7, D), F32)], axis=0)
    b_mod_l = lax.dynamic_slice(b_mod, (0, me * mod_cols), (1, mod_cols))
    mod_l, svec = _mod_fwd(cvec, w_mod[0], b_mod_l)
    (g_mod,) = _exchange([(mod_l, "gather")], "gather_mod")
    mod_all = jnp.transpose(g_mod, (1, 0, 2)).reshape(n_c + 8, N_MOD * D)
    modv = lax.dynamic_slice(mod_all, (me * nb_ex, 0), (nb_ex, N_MOD * D)).reshape(nb_ex, N_MOD, D)
    mcv = mod_all[n_c].reshape(1, N_MOD, D)

    blocks = lambda a: a.reshape(N_DEV, a.shape[0] // N_DEV, a.shape[1])
    handles = {}

    class Comm:
        def sender(self, stage, grads=None):
            if stage == "inproj":
                return _Sender([(w_down[0].astype(cd), "gather"), (w_o[0].astype(cd), "gather"), (w_paT_l, "gather"),
                                (w_pbT_l, "gather")], [1, 1, 1, 1])
            if stage == "scan":
                return _Sender([(w_upT_l, "gather")], [2])
            if stage == "tail_bwd":
                return _Sender([(blocks(grads[0]), "scatter")], [2])
            return _Sender([(blocks(g), "scatter") for g in grads], [1] * len(grads))

        def sent(self, stage, handle):
            handles[stage] = handle

        def mixer_weights(self, after):
            g_down, g_o, g_pa, g_pb = _exchange_wait(handles["inproj"], after)
            return g_down.reshape(D_FF, D), g_o.reshape(D, D), g_pa.reshape(D, KW), g_pb.reshape(D, KW)

        def ffn_weights(self, after):
            (g_up,) = _exchange_wait(handles["scan"], after)
            return g_up.reshape(2 * D_FF, D)

    loss_l, grad_x, gw_inT, small = _local_step(
        x, ctx, loss_target, modv, mcv, gam, g_mix, g_ffn, g_norm_a, ln_v_g, ln_v_b, w_s[0], b_s[0],
        g_final.reshape(1, D), w_inT, Comm())
    loss = lax.psum(loss_l, ("x", "y", "c"))
    last, last_started = _exchange_start([(blocks(gw_inT), "scatter"), (small, "gather")], "scatter_in", after=small)

    (r_up,) = _exchange_wait(handles["tail_bwd"], last_started)
    r_down, r_o, r_pa, r_pb = _exchange_wait(handles["scan_bwd"], last_started)
    done = {
        "w_up": [a.T[None] for a in _adamw_sum8(r_up, w_up[0].T, m_w_up[0].T, v_w_up[0].T, "adamw_w_up")],
        "w_down": [a[None] for a in _adamw_sum8(r_down, w_down[0], m_w_down[0], v_w_down[0], "adamw_w_down")],
        "w_o": [a[None] for a in _adamw_sum8(r_o, w_o[0], m_w_o[0], v_w_o[0], "adamw_w_o")],
    }
    grad_w_up, grad_w_down, grad_w_o = done["w_up"][0], done["w_down"][0], done["w_o"][0]
    grad_w_pa = _sum8(r_pa, "sum_w_pa").T[None]
    grad_w_pb = _sum8(r_pb, "sum_w_pb").T[None]

    r_in, r_small = _exchange_wait(last, done["w_up"][1])
    done["w_in"] = [a.T[None] for a in _adamw_sum8(r_in, w_in[0].T, m_w_in[0].T, v_w_in[0].T, "adamw_w_in")]
    grad_w_in = done["w_in"][0]
    tot, bm = _small_reduce(r_small, gam, nb_ex)
    grad_g_mix, grad_g_ffn, grad_g_final = tot[0:1], tot[1:2], tot[2]
    grad_g_norm_a = tot[3:4, 0:DK]
    grad_ln_v_g, grad_ln_v_b = tot[4:5, 0:KW], tot[5:6, 0:KW]
    grad_b_s = tot[6, 0:GROUPS * SGU_BLOCK].reshape(1, GROUPS, SGU_BLOCK)
    grad_w_s = tot[40:104].reshape(1, GROUPS, SGU_BLOCK, SGU_BLOCK)
    grad_b_mod = bm[0:N_MOD].reshape(1, N_MOD * D)
    grad_lb_gamma = lax.dynamic_slice(bm[6:8].reshape(2, 2, KW), (0, 0, me * lb_cols), (2, 2, lb_cols))

    dmod_all = r_small[:, 16:16 + nb_ex * N_MOD].reshape(n_c, N_MOD * D)
    dmod_l = jnp.concatenate([lax.dynamic_slice(dmod_all, (0, me * mod_cols), (n_c, mod_cols)),
                              lax.dynamic_slice(tot[8:8 + N_MOD].reshape(1, N_MOD * D), (0, me * mod_cols), (1, mod_cols)),
                              jnp.zeros((7, mod_cols), F32)], axis=0)
    gw_mod, gc = _mod_bwd(svec, cvec, dmod_l, w_mod[0])
    grad_w_mod = gw_mod[None]
    (r_gc,) = _exchange([(gc[n_c:n_c + 8], "gather")], "gather_c_ctx")
    grad_c_ctx = _sum8(r_gc, "sum_c_ctx")[0]

    names = ["c_ctx", "w_mod", "b_mod", "g_mix", "g_ffn", "w_in", "lb_gamma", "g_norm_a", "ln_v_g", "ln_v_b", "w_s",
             "b_s", "w_pa", "w_pb", "w_o", "w_up", "w_down", "g_final"]
    weights = [c_ctx, w_mod, b_mod, g_mix, g_ffn, w_in, lb_gamma, g_norm_a, ln_v_g, ln_v_b, w_s, b_s, w_pa, w_pb, w_o,
               w_up, w_down, g_final]
    grads = [grad_c_ctx, grad_w_mod, grad_b_mod, grad_g_mix, grad_g_ffn, grad_w_in, grad_lb_gamma, grad_g_norm_a,
             grad_ln_v_g, grad_ln_v_b, grad_w_s, grad_b_s, grad_w_pa, grad_w_pb, grad_w_o, grad_w_up, grad_w_down,
             grad_g_final]
    ms = [m_c_ctx, m_w_mod, m_b_mod, m_g_mix, m_g_ffn, m_w_in, m_lb_gamma, m_g_norm_a, m_ln_v_g, m_ln_v_b, m_w_s, m_b_s,
          m_w_pa, m_w_pb, m_w_o, m_w_up, m_w_down, m_g_final]
    vs = [v_c_ctx, v_w_mod, v_b_mod, v_g_mix, v_g_ffn, v_w_in, v_lb_gamma, v_g_norm_a, v_ln_v_g, v_ln_v_b, v_w_s, v_b_s,
          v_w_pa, v_w_pb, v_w_o, v_w_up, v_w_down, v_g_final]
    deltas, new_ms, new_vs = [], [], []
    for nm, w, g, m, v in zip(names, weights, grads, ms, vs):
        d, nm_, nv_ = done[nm][1:] if nm in done else _adamw(w, g.reshape(w.shape), m, v, "adamw_" + nm)
        deltas.append(d)
        new_ms.append(nm_)
        new_vs.append(nv_)
    grads = [g.reshape(w.shape) for g, w in zip(grads, weights)]
    return (loss, grad_x, *grads, *deltas, *new_ms, *new_vs)
```

```python
import functools

import jax
import jax.numpy as jnp
from jax import lax
from jax.experimental import pallas as pl
from jax.experimental.pallas import tpu as pltpu

F32 = jnp.float32
MXU_DTYPE = jnp.bfloat16
PAYLOAD_DTYPE = jnp.bfloat16

N_DEV = 8
D = 1024
HEADS = 4
DK = 128
KW = HEADS * DK
CHUNK = 64
SGU_BLOCK = 128
GROUPS = 4
D_FF = 2816
FF_CHUNK = 256
N_MOD = 6
IN_COLS = 5632
CTX_COLS = 1536
TAIL_COLS = IN_COLS - 4 * KW
EPS = 1e-6
ADAM_LR, ADAM_B1, ADAM_B2, ADAM_EPS, ADAM_WD, ADAM_STEP = 0.001, 0.9, 0.999, 1e-08, 0.01, 10

VMEM_LIMIT = 56 * 1024 * 1024
TOKEN_TILE = 256
SMALL_ROWS = 104


def _params(sem):
    return pltpu.CompilerParams(dimension_semantics=sem, vmem_limit_bytes=VMEM_LIMIT)


_DN = {"nn": (((1,), (0,)), ((), ())), "nt": (((1,), (1,)), ((), ())), "tn": (((0,), (0,)), ((), ()))}


def _dot(a, b, form="nn"):
    return lax.dot_general(a.astype(MXU_DTYPE), b.astype(MXU_DTYPE), _DN[form], preferred_element_type=F32)


def _dotx(a, b, form="nn"):
    return lax.dot_general(a.astype(F32), b.astype(F32), _DN[form], preferred_element_type=F32,
                           precision=lax.Precision.HIGHEST)


def _full(shape, single=False):
    n = len(shape)
    if single:
        return pl.BlockSpec(shape, lambda *_: (0,) * n, pipeline_mode=pl.Buffered(1))
    return pl.BlockSpec(shape, lambda *_: (0,) * n)


def _ordered_behind(body, in_specs, args, after):
    if after is None:
        return body
    at = len(in_specs)
    in_specs.append(pl.BlockSpec(memory_space=pl.ANY))
    args.append(after)
    return lambda *refs: body(*refs[:at], *refs[at + 1:])


def _sigmoid(z):
    return 1.0 / (1.0 + jnp.exp(-z))


def _gelu(x):
    c = 0.7978845608028654
    t = jnp.tanh(c * (x + 0.044715 * x * x * x))
    return 0.5 * x * (1.0 + t), t


def _gelu_grad(x, t):
    c = 0.7978845608028654
    return 0.5 * (1.0 + t) + 0.5 * x * (1.0 - t * t) * c * (1.0 + 3 * 0.044715 * x * x)


def _exchange(items, name, after=None):
    n = len(items)
    out_shape = []
    for a, mode in items:
        blk = a.shape if mode == "gather" else a.shape[1:]
        out_shape.append(jax.ShapeDtypeStruct((N_DEV,) + tuple(blk), a.dtype))

    def body(*refs):
        srcs, dsts = refs[:n], refs[n:2 * n]
        send_sems, recv_sems, local_sems = refs[2 * n:]
        x, y, c = lax.axis_index("x"), lax.axis_index("y"), lax.axis_index("c")
        me = 4 * x + 2 * y + c

        def src_for(i, dev):
            return srcs[i] if items[i][1] == "gather" else srcs[i].at[dev]

        local = [pltpu.make_async_copy(src_for(i, me), dsts[i].at[me], local_sems.at[i]) for i in range(n)]
        for cp in local:
            cp.start()
        remote = []
        for k in range(1, N_DEV):
            px = jnp.bitwise_xor(x, (k >> 2) & 1)
            py = jnp.bitwise_xor(y, (k >> 1) & 1)
            pc = jnp.bitwise_xor(c, k & 1)
            peer = 4 * px + 2 * py + pc
            for i in range(n):
                cp = pltpu.make_async_remote_copy(
                    src_ref=src_for(i, peer), dst_ref=dsts[i].at[me],
                    send_sem=send_sems.at[i * (N_DEV - 1) + k - 1], recv_sem=recv_sems.at[i * (N_DEV - 1) + k - 1],
                    device_id=(px, py, pc), device_id_type=pl.DeviceIdType.MESH)
                cp.start()
                remote.append(cp)
        for cp in remote:
            cp.wait()
        for cp in local:
            cp.wait()

    any_spec = pl.BlockSpec(memory_space=pl.ANY)
    in_specs, args = [any_spec] * n, [a for a, _ in items]
    if after is not None:
        in_specs.append(any_spec)
        args.append(after)
        exchange = body
        body = lambda *refs: exchange(*refs[:n], *refs[n + 1:])
    return pl.pallas_call(
        body, name=name, out_shape=out_shape, in_specs=in_specs, out_specs=[any_spec] * n,
        scratch_shapes=[pltpu.SemaphoreType.DMA((n * (N_DEV - 1),)), pltpu.SemaphoreType.DMA((n * (N_DEV - 1),)),
                        pltpu.SemaphoreType.DMA((n,))],
    )(*args)


_HBM = pl.BlockSpec(memory_space=pltpu.HBM)
_SEM = pl.BlockSpec(memory_space=pltpu.SEMAPHORE)
_EFFECT = pltpu.SideEffectType.DATAFLOW_SIDE_EFFECTING


def _split_copies(items, srcs, lands, send_sems, recv_sems):
    x, y, c = lax.axis_index("x"), lax.axis_index("y"), lax.axis_index("c")
    me = 4 * x + 2 * y + c
    copies = []
    for k in range(1, N_DEV):
        px = jnp.bitwise_xor(x, (k >> 2) & 1)
        py = jnp.bitwise_xor(y, (k >> 1) & 1)
        pc = jnp.bitwise_xor(c, k & 1)
        peer = 4 * px + 2 * py + pc
        for i in range(len(items)):
            src = srcs[i] if items[i][1] == "gather" else srcs[i].at[peer]
            copies.append(pltpu.make_async_remote_copy(
                src_ref=src, dst_ref=lands[i].at[me],
                send_sem=send_sems.at[i * (N_DEV - 1) + k - 1], recv_sem=recv_sems.at[i * (N_DEV - 1) + k - 1],
                device_id=(px, py, pc), device_id_type=pl.DeviceIdType.MESH))
    return me, copies


def _exchange_start(items, name, after):
    n = len(items)
    n_sem = n * (N_DEV - 1)
    srcs, lands = [], []
    for a, mode in items:
        blk = a.shape if mode == "gather" else a.shape[1:]
        srcs.append(pltpu.with_memory_space_constraint(a, pltpu.HBM))
        lands.append(pltpu.with_memory_space_constraint(lax.empty((N_DEV,) + tuple(blk), a.dtype), pltpu.HBM))

    def body(*refs):
        src_refs, land_refs = refs[:n], refs[n:2 * n]
        send_sems, recv_sems = refs[2 * n + 1], refs[2 * n + 2]
        local_sems = refs[4 * n + 3]
        me, copies = _split_copies(items, src_refs, land_refs, send_sems, recv_sems)
        for i in range(n):
            own = src_refs[i] if items[i][1] == "gather" else src_refs[i].at[me]
            cp = pltpu.make_async_copy(own, land_refs[i].at[me], local_sems.at[i])
            cp.start()
            cp.wait()
        for cp in copies:
            cp.start()

    out_shape = [pltpu.SemaphoreType.DMA((n_sem,)), pltpu.SemaphoreType.DMA((n_sem,))]
    out_shape += [pltpu.HBM(a.shape, a.dtype) for a in srcs] + [pltpu.HBM(a.shape, a.dtype) for a in lands]
    outs = pl.pallas_call(
        body, name=name, out_shape=out_shape,
        in_specs=[_HBM] * (2 * n) + [pl.BlockSpec(memory_space=pl.ANY)],
        out_specs=[_SEM, _SEM] + [_HBM] * (2 * n),
        input_output_aliases={i: 2 + i for i in range(2 * n)},
        scratch_shapes=[pltpu.SemaphoreType.DMA((n,))],
        compiler_params=pltpu.CompilerParams(has_side_effects=_EFFECT),
    )(*srcs, *lands, after)
    handle = (items, name, outs[0], outs[1], outs[2:2 + n], outs[2 + n:2 + 2 * n])
    return handle, outs[2]


class _Sender:
    PIECE_ROWS = 352

    def __init__(self, items, chunks=None):
        self.items, self.n = items, len(items)
        self.chunks = chunks
        if chunks is None:
            block_rows = [a.shape[0] if mode == "gather" else a.shape[1] for a, mode in items]
            self.chunks = [r // self.PIECE_ROWS if r % self.PIECE_ROWS == 0 else 1 for r in block_rows]
        self.srcs, self.lands = [], []
        for a, mode in items:
            blk = a.shape if mode == "gather" else a.shape[1:]
            self.srcs.append(pltpu.with_memory_space_constraint(a, pltpu.HBM))
            self.lands.append(pltpu.with_memory_space_constraint(lax.empty((N_DEV,) + tuple(blk), a.dtype), pltpu.HBM))

    def issue(self, src_refs, land_refs, send_sems, recv_sems, local_sems, step, n_steps):
        x, y, c = lax.axis_index("x"), lax.axis_index("y"), lax.axis_index("c")
        me = 4 * x + 2 * y + c
        copies = []
        for ch in range(max(self.chunks)):
            for k in range(1, N_DEV):
                px = jnp.bitwise_xor(x, (k >> 2) & 1)
                py = jnp.bitwise_xor(y, (k >> 1) & 1)
                pc = jnp.bitwise_xor(c, k & 1)
                peer = 4 * px + 2 * py + pc
                for i, (_, mode) in enumerate(self.items):
                    if ch >= self.chunks[i]:
                        continue
                    n_rows = land_refs[i].shape[1] // self.chunks[i]
                    rows = pl.ds(ch * n_rows, n_rows)
                    src = src_refs[i].at[rows] if mode == "gather" else src_refs[i].at[peer].at[rows]
                    copies.append(pltpu.make_async_remote_copy(
                        src_ref=src, dst_ref=land_refs[i].at[me].at[rows],
                        send_sem=send_sems.at[i * (N_DEV - 1) + k - 1], recv_sem=recv_sems.at[i * (N_DEV - 1) + k - 1],
                        device_id=(px, py, pc), device_id_type=pl.DeviceIdType.MESH))
        own = [pltpu.make_async_copy(src_refs[i] if mode == "gather" else src_refs[i].at[me], land_refs[i].at[me],
                                     local_sems.at[i]) for i, (_, mode) in enumerate(self.items)]

        @pl.when(step == 0)
        def _():
            for cp in own:
                cp.start()

        for s in range(n_steps):
            group = [cp for j, cp in enumerate(copies) if (j * n_steps) // len(copies) == s]
            if group:
                @pl.when(step == s)
                def _(group=group):
                    for cp in group:
                        cp.start()

        @pl.when(step == n_steps - 1)
        def _():
            for cp in own:
                cp.wait()


def _host_call(body, name, grid, in_specs, args, out_shape, out_specs, scratch_shapes, after=None, sender=None):
    in_specs, args, out_shape, out_specs = list(in_specs), list(args), list(out_shape), list(out_specs)
    scratch_shapes = list(scratch_shapes)
    semantics = ("arbitrary",) * len(grid)
    body = _ordered_behind(body, in_specs, args, after)
    if sender is None:
        res = pl.pallas_call(body, name=name, grid=grid, in_specs=in_specs, out_specs=out_specs, out_shape=out_shape,
                             scratch_shapes=scratch_shapes, compiler_params=_params(semantics))(*args)
        return res, None
    n, n_in, n_out, n_scr = sender.n, len(in_specs), len(out_shape), len(scratch_shapes)
    n_sem = n * (N_DEV - 1)
    n_steps = 1
    for g in grid:
        n_steps *= g
    compute = body

    def body(*refs):
        ins, s_in = refs[:n_in], refs[n_in:n_in + 2 * n]
        o0 = n_in + 2 * n
        outs, s_out = refs[o0:o0 + n_out], refs[o0 + n_out:o0 + n_out + 2 + 2 * n]
        scr = refs[o0 + n_out + 2 + 2 * n:]
        compute(*ins, *outs, *scr[:n_scr])
        step = pl.program_id(0)
        for d in range(1, len(grid)):
            step = step * grid[d] + pl.program_id(d)
        sender.issue(s_in[:n], s_in[n:], s_out[0], s_out[1], scr[n_scr], step, n_steps)

    res = pl.pallas_call(
        body, name=name, grid=grid,
        in_specs=in_specs + [_HBM] * (2 * n), out_specs=out_specs + [_SEM, _SEM] + [_HBM] * (2 * n),
        out_shape=out_shape + [pltpu.SemaphoreType.DMA((n_sem,)), pltpu.SemaphoreType.DMA((n_sem,))]
        + [pltpu.HBM(a.shape, a.dtype) for a in sender.srcs] + [pltpu.HBM(a.shape, a.dtype) for a in sender.lands],
        input_output_aliases={n_in + j: n_out + 2 + j for j in range(2 * n)},
        scratch_shapes=scratch_shapes + [pltpu.SemaphoreType.DMA((n,))],
        compiler_params=pltpu.CompilerParams(dimension_semantics=semantics, vmem_limit_bytes=VMEM_LIMIT,
                                             has_side_effects=_EFFECT),
    )(*args, *sender.srcs, *sender.lands)
    handle = (sender.items, name, res[n_out], res[n_out + 1], res[n_out + 2:n_out + 2 + n],
              res[n_out + 2 + n:n_out + 2 + 2 * n])
    return res[:n_out], handle


def _exchange_wait(handle, after):
    items, name, send_sems, recv_sems, srcs, lands = handle
    n = len(items)

    def body(*refs):
        src_refs, land_refs = refs[:n], refs[n:2 * n]
        send_ref, recv_ref = refs[2 * n], refs[2 * n + 1]
        _, copies = _split_copies(items, src_refs, land_refs, send_ref, recv_ref)
        for cp in copies:
            cp.wait_send()
            cp.wait_recv()

    outs = pl.pallas_call(
        body, name=name + "_wait",
        out_shape=[pltpu.HBM(a.shape, a.dtype) for a in srcs] + [pltpu.HBM(a.shape, a.dtype) for a in lands],
        in_specs=[_HBM] * (2 * n) + [_SEM, _SEM, pl.BlockSpec(memory_space=pl.ANY)], out_specs=[_HBM] * (2 * n),
        input_output_aliases={i: i for i in range(2 * n)},
        compiler_params=pltpu.CompilerParams(has_side_effects=_EFFECT),
    )(*srcs, *lands, send_sems, recv_sems, after)
    return outs[n:]


def _mod_fwd(cvec, w_mod_l, b_mod_l):
    rows, cols = cvec.shape[0], w_mod_l.shape[1]

    def body(c_ref, w_ref, b_ref, o_ref, s_ref):
        cv = c_ref[...]
        s = cv * _sigmoid(cv)
        s_ref[...] = s
        o_ref[...] = _dot(s, w_ref[...]) + b_ref[...]

    return pl.pallas_call(
        body, name="mod_fwd",
        out_shape=(jax.ShapeDtypeStruct((rows, cols), F32), jax.ShapeDtypeStruct((rows, D), F32)),
        in_specs=[_full((rows, D)), _full((D, cols)), _full((1, cols))],
        out_specs=(_full((rows, cols)), _full((rows, D))), grid=(1,),
        compiler_params=_params(("arbitrary",)),
    )(cvec, w_mod_l, b_mod_l)


def _mod_bwd(svec, cvec, dmod_l, w_mod_l):
    rows, cols = dmod_l.shape

    def body(s_ref, c_ref, d_ref, w_ref, gw_ref, gc_ref):
        gw_ref[...] = _dot(s_ref[...], d_ref[...], "tn")
        cv = c_ref[...]
        sg = _sigmoid(cv)
        gc_ref[...] = _dot(d_ref[...], w_ref[...], "nt") * (sg * (1.0 + cv * (1.0 - sg)))

    return pl.pallas_call(
        body, name="mod_bwd",
        out_shape=(jax.ShapeDtypeStruct((D, cols), F32), jax.ShapeDtypeStruct((rows, D), F32)),
        in_specs=[_full((rows, D)), _full((rows, D)), _full((rows, cols)), _full((D, cols))],
        out_specs=(_full((D, cols)), _full((rows, D))), grid=(1,),
        compiler_params=_params(("arbitrary",)),
    )(svec, cvec, dmod_l, w_mod_l)


def _inproj(xt, modv, g, w_inT, n_cols, rows_per_example, name, after=None, sender=None):
    rows = xt.shape[0]
    tm = min(TOKEN_TILE, rows_per_example)
    per_b = rows_per_example // tm
    shared_mod = modv.shape[0] == 1

    def body(x_ref, mod_ref, g_ref, w_ref, p_ref, h_ref):
        x = x_ref[...]
        r = lax.rsqrt(jnp.mean(x * x, axis=-1, keepdims=True) + EPS)
        h = (x * r * g_ref[...]) * (1.0 + mod_ref[0, 1:2, :]) + mod_ref[0, 0:1, :]
        hb = h.astype(MXU_DTYPE)
        h_ref[...] = hb
        for j in range(n_cols // KW):
            p_ref[:, j * KW:(j + 1) * KW] = _dot(hb, w_ref[j * KW:(j + 1) * KW, :], "nt").astype(p_ref.dtype)

    mod_idx = (lambda i: (0, 0, 0)) if shared_mod else (lambda i: (i // per_b, 0, 0))
    in_specs = [pl.BlockSpec((tm, D), lambda i: (i, 0)), pl.BlockSpec((1, N_MOD, D), mod_idx), _full((1, D)),
                pl.BlockSpec((n_cols, D), lambda i: (0, 0), pipeline_mode=pl.Buffered(1))]
    (p, h), handle = _host_call(
        body, name, (rows // tm,), in_specs, [xt, modv, g, w_inT],
        [jax.ShapeDtypeStruct((rows, n_cols), MXU_DTYPE), jax.ShapeDtypeStruct((rows, D), MXU_DTYPE)],
        [pl.BlockSpec((tm, n_cols), lambda i: (i, 0)), pl.BlockSpec((tm, D), lambda i: (i, 0))], [],
        after=after, sender=sender)
    return p, h, handle


def _tri(reverse):
    row = lax.broadcasted_iota(jnp.int32, (CHUNK, CHUNK), 0)
    col = lax.broadcasted_iota(jnp.int32, (CHUNK, CHUNK), 1)
    return (col >= row) if reverse else (col <= row)


def _lower_bound(gam_ref, direction):
    return _sigmoid(gam_ref[direction:direction + 1, :] - gam_ref[2 + direction:3 + direction, :])


def _gate_prep(z, lb, tri_f):
    sg = _sigmoid(z)
    f = lb + (1.0 - lb) * sg
    g = jnp.log(f)
    b = _dotx(tri_f, g)
    bl = jnp.sum(g, axis=0, keepdims=True)
    return sg, f, 1.0 - f, b, bl


def _hgrn_fwd(p, gam, s0, rows_per_example, with_out, name, sender=None):
    rows = p.shape[0]
    nb_ex = rows // rows_per_example
    rb = min(TOKEN_TILE, rows_per_example)
    cpb = rb // CHUNK
    nb = rows_per_example // rb
    n_chunks = rows // CHUNK
    has_s0 = s0 is not None

    def body(*refs):
        it = iter(refs)
        gam_ref = next(it)
        zf_ref, vf_ref = next(it), next(it)
        qf_ref = next(it) if with_out else None
        zb_ref, vb_ref = next(it), next(it)
        qb_ref = next(it) if with_out else None
        s0_ref = next(it) if has_s0 else None
        if with_out:
            of_ref, ob_ref = next(it), next(it)
        stash_f, stash_b, fin_ref = next(it), next(it), next(it)
        st_ref = next(it)
        i = pl.program_id(1)

        @pl.when(i == 0)
        def _():
            if has_s0:
                st_ref[...] = s0_ref[:, 0]
            else:
                st_ref[...] = jnp.zeros_like(st_ref)

        for direction, (z_ref, v_ref, q_ref, stash) in enumerate(
                ((zf_ref, vf_ref, qf_ref, stash_f), (zb_ref, vb_ref, qb_ref, stash_b))):
            reverse = direction == 1
            tri = _tri(reverse)
            tri_f = tri.astype(F32)
            lb = _lower_bound(gam_ref, direction)
            order = range(cpb - 1, -1, -1) if reverse else range(cpb)
            for j in order:
                rs = slice(j * CHUNK, (j + 1) * CHUNK)
                z = z_ref[rs, :].astype(F32)
                v = v_ref[rs, :].astype(F32)
                _, _, k, b, bl = _gate_prep(z, lb, tri_f)
                mid = 0.5 * bl
                kd = k * jnp.exp(bl - b)
                a = jnp.exp(bl)
                if with_out:
                    q = q_ref[rs, :].astype(F32)
                    qi = q * jnp.exp(b - mid)
                    ki = k * jnp.exp(mid - b)
                    qe = q * jnp.exp(b)
                for h in range(HEADS):
                    hs = slice(h * DK, (h + 1) * DK)
                    st = st_ref[direction, h]
                    stash[j, h] = st.astype(stash.dtype)
                    if with_out:
                        sc = jnp.where(tri, _dot(qi[:, hs], ki[:, hs], "nt"), 0.0)
                        o = _dot(sc, v[:, hs]) + _dot(qe[:, hs], st, "nt")
                        (ob_ref if reverse else of_ref)[rs, hs] = o
                    st_ref[direction, h] = st * a[:, hs] + _dot(v[:, hs], kd[:, hs], "tn")

        @pl.when(i == nb - 1)
        def _():
            fin_ref[:, 0] = st_ref[...]

    up = lambda b, i: b * nb + i
    down = lambda b, i: b * nb + nb - 1 - i
    col = lambda rowf, c: pl.BlockSpec((rb, KW), lambda b, i: (rowf(b, i), c))
    in_specs = [_full((4, KW)), col(up, 0), col(up, 2)] + ([col(up, 3)] if with_out else [])
    in_specs += [col(down, 1), col(down, 2)] + ([col(down, 3)] if with_out else [])
    args = [gam, p, p] + ([p] if with_out else []) + [p, p] + ([p] if with_out else [])
    if has_s0:
        in_specs.append(pl.BlockSpec((2, 1, HEADS, DK, DK), lambda b, i: (0, b, 0, 0, 0)))
        args.append(s0)
    out_shape, out_specs = [], []
    if with_out:
        out_shape += [jax.ShapeDtypeStruct((rows, KW), F32)] * 2
        out_specs += [pl.BlockSpec((rb, KW), lambda b, i: (up(b, i), 0)),
                      pl.BlockSpec((rb, KW), lambda b, i: (down(b, i), 0))]
    out_shape += [jax.ShapeDtypeStruct((n_chunks, HEADS, DK, DK), MXU_DTYPE)] * 2
    out_specs += [pl.BlockSpec((cpb, HEADS, DK, DK), lambda b, i: (up(b, i), 0, 0, 0)),
                  pl.BlockSpec((cpb, HEADS, DK, DK), lambda b, i: (down(b, i), 0, 0, 0))]
    out_shape.append(jax.ShapeDtypeStruct((2, nb_ex, HEADS, DK, DK), F32))
    out_specs.append(pl.BlockSpec((2, 1, HEADS, DK, DK), lambda b, i: (0, b, 0, 0, 0)))
    res, handle = _host_call(body, name, (nb_ex, nb), in_specs, args, out_shape, out_specs,
                             [pltpu.VMEM((2, HEADS, DK, DK), F32)], sender=sender)
    return (*res, handle)


def _hgrn_bwd(p, gam, do, stash_f, stash_b, ds_end, rows_per_example, with_out, name, after=None, sender=None):
    rows = p.shape[0]
    nb_ex = rows // rows_per_example
    rb = min(TOKEN_TILE, rows_per_example)
    cpb = rb // CHUNK
    nb = rows_per_example // rb
    has_end = ds_end is not None

    def body(*refs):
        it = iter(refs)
        gam_ref = next(it)
        ins = []
        for _ in range(2):
            z_ref, v_ref = next(it), next(it)
            q_ref = next(it) if with_out else None
            do_ref = next(it) if with_out else None
            ins.append((z_ref, v_ref, q_ref, do_ref, next(it)))
        end_ref = next(it) if has_end else None
        outs = []
        for _ in range(2):
            dz_ref, dv_ref = next(it), next(it)
            dq_ref = next(it) if with_out else None
            outs.append((dz_ref, dv_ref, dq_ref))
        dlb_ref, ds0_ref = next(it), next(it)
        dst_ref = next(it)
        b_id, i = pl.program_id(0), pl.program_id(1)

        @pl.when(i == 0)
        def _():
            if has_end:
                dst_ref[...] = end_ref[:, 0]
            else:
                dst_ref[...] = jnp.zeros_like(dst_ref)

        @pl.when((i == 0) & (b_id == 0))
        def _():
            dlb_ref[...] = jnp.zeros_like(dlb_ref)

        for direction in range(2):
            z_ref, v_ref, q_ref, do_ref, stash = ins[direction]
            dz_ref, dv_ref, dq_ref = outs[direction]
            reverse = direction == 1
            tri = _tri(reverse)
            tri_f = tri.astype(F32)
            lb = _lower_bound(gam_ref, direction)
            order = range(cpb) if reverse else range(cpb - 1, -1, -1)
            dlb_acc = jnp.zeros((1, KW), F32)
            for j in order:
                rs = slice(j * CHUNK, (j + 1) * CHUNK)
                z = z_ref[rs, :].astype(F32)
                v = v_ref[rs, :].astype(F32)
                sg, f, k, b, bl = _gate_prep(z, lb, tri_f)
                mid = 0.5 * bl
                e3 = jnp.exp(bl - b)
                kd = k * e3
                a = jnp.exp(bl)
                if with_out:
                    q = q_ref[rs, :].astype(F32)
                    dout = do_ref[rs, :].astype(F32)
                    e1, e2, e4 = jnp.exp(b - mid), jnp.exp(mid - b), jnp.exp(b)
                    qi, ki, qe = q * e1, k * e2, q * e4
                dkd_p, dv_p, da_p, dqi_p, dki_p, dqe_p = [], [], [], [], [], []
                for h in range(HEADS):
                    hs = slice(h * DK, (h + 1) * DK)
                    st_in = stash[j, h]
                    dst = dst_ref[direction, h]
                    dkd_p.append(_dot(v[:, hs], dst))
                    dvh = _dot(kd[:, hs], dst, "nt")
                    da_p.append(jnp.sum(dst * st_in.astype(F32), axis=0, keepdims=True))
                    new_dst = dst * a[:, hs]
                    if with_out:
                        sc = jnp.where(tri, _dot(qi[:, hs], ki[:, hs], "nt"), 0.0)
                        dsc = jnp.where(tri, _dot(dout[:, hs], v[:, hs], "nt"), 0.0)
                        dqi_p.append(_dot(dsc, ki[:, hs]))
                        dki_p.append(_dot(dsc, qi[:, hs], "tn"))
                        dqe_p.append(_dot(dout[:, hs], st_in))
                        dvh = dvh + _dot(sc, dout[:, hs], "tn")
                        new_dst = new_dst + _dot(dout[:, hs], qe[:, hs], "tn")
                    dv_p.append(dvh)
                    dst_ref[direction, h] = new_dst
                cat = lambda parts: jnp.concatenate(parts, axis=1)
                dkd, da = cat(dkd_p), cat(da_p)
                dv_ref[rs, :] = cat(dv_p)
                t_kd = dkd * kd
                dk = dkd * e3
                db = -t_kd
                dbl = jnp.sum(t_kd, axis=0, keepdims=True) + da * a
                if with_out:
                    dqi, dki, dqe = cat(dqi_p), cat(dki_p), cat(dqe_p)
                    dq_ref[rs, :] = dqi * e1 + dqe * e4
                    dk = dk + dki * e2
                    t_qi, t_ki, t_qe = dqi * qi, dki * ki, dqe * qe
                    db = db + t_qi - t_ki + t_qe
                    dbl = dbl + 0.5 * jnp.sum(t_ki - t_qi, axis=0, keepdims=True)
                dg = _dotx(tri_f, db, "tn") + dbl
                df = dg / f - dk
                dz_ref[rs, :] = df * (1.0 - lb) * sg * (1.0 - sg)
                dlb_acc = dlb_acc + jnp.sum(df * (1.0 - sg), axis=0, keepdims=True)
            dlb_ref[direction:direction + 1, :] += dlb_acc

        @pl.when(i == nb - 1)
        def _():
            ds0_ref[:, 0] = dst_ref[...]

    rows_of = (lambda b, i: b * nb + nb - 1 - i, lambda b, i: b * nb + i)
    in_specs, args = [_full((4, KW))], [gam]
    for direction in range(2):
        rf = rows_of[direction]
        col = lambda c, rf=rf: pl.BlockSpec((rb, KW), lambda b, i: (rf(b, i), c))
        in_specs += [col(direction), col(2)]
        args += [p, p]
        if with_out:
            in_specs += [col(3), col(0)]
            args += [p, do]
        in_specs.append(pl.BlockSpec((cpb, HEADS, DK, DK), lambda b, i, rf=rf: (rf(b, i), 0, 0, 0)))
        args.append((stash_f, stash_b)[direction])
    if has_end:
        in_specs.append(pl.BlockSpec((2, 1, HEADS, DK, DK), lambda b, i: (0, b, 0, 0, 0)))
        args.append(ds_end)
    out_shape, out_specs = [], []
    for direction in range(2):
        rf = rows_of[direction]
        n_out = 3 if with_out else 2
        out_shape += [jax.ShapeDtypeStruct((rows, KW), F32)] * n_out
        out_specs += [pl.BlockSpec((rb, KW), lambda b, i, rf=rf: (rf(b, i), 0))] * n_out
    out_shape += [jax.ShapeDtypeStruct((2, KW), F32), jax.ShapeDtypeStruct((2, nb_ex, HEADS, DK, DK), F32)]
    out_specs += [_full((2, KW)), pl.BlockSpec((2, 1, HEADS, DK, DK), lambda b, i: (0, b, 0, 0, 0))]
    res, handle = _host_call(body, name, (nb_ex, nb), in_specs, args, out_shape, out_specs,
                             [pltpu.VMEM((2, HEADS, DK, DK), F32)], after=after, sender=sender)
    return (*res, handle)


def _tail_forward(osum, og, u, v, ga, gb, gna, ln_g, ln_b, ws_ref, bs_ref, wpaT_ref, wpbT_ref):
    tm = osum.shape[0]
    gna4 = jnp.concatenate([gna] * HEADS, axis=1)
    r_parts = []
    for h in range(HEADS):
        oh = osum[:, h * DK:(h + 1) * DK]
        r_parts.append(jnp.broadcast_to(lax.rsqrt(jnp.mean(oh * oh, axis=-1, keepdims=True) + EPS), (tm, DK)))
    r = jnp.concatenate(r_parts, axis=1)
    on = osum * r
    sg_og = _sigmoid(og)
    silu_og = og * sg_og
    oan = on * gna4
    oa = oan * silu_og
    ug, tu = _gelu(u)
    vg, tv = _gelu(v)
    mu = jnp.mean(vg, axis=-1, keepdims=True)
    vc = vg - mu
    rstd = lax.rsqrt(jnp.mean(vc * vc, axis=-1, keepdims=True) + EPS)
    vhat = vc * rstd
    vln = vhat * ln_g + ln_b
    blocks = []
    for n in range(tm // SGU_BLOCK):
        rs = slice(n * SGU_BLOCK, (n + 1) * SGU_BLOCK)
        blocks.append(jnp.concatenate(
            [_dot(ws_ref[g], vln[rs, g * DK:(g + 1) * DK]) + bs_ref[g] for g in range(GROUPS)], axis=1))
    mixed = jnp.concatenate(blocks, axis=0) if len(blocks) > 1 else blocks[0]
    obm = ug * mixed
    pa = _dot(oa, wpaT_ref[...], "nt")
    pb = _dot(obm, wpbT_ref[...], "nt")
    sga, sgb = _sigmoid(ga), _sigmoid(gb)
    merged = sga * pa + sgb * pb
    return dict(r=r, on=on, sg_og=sg_og, silu_og=silu_og, oan=oan, oa=oa, ug=ug, tu=tu, tv=tv, rstd=rstd, vhat=vhat,
                vln=vln, mixed=mixed, obm=obm, pa=pa, pb=pb, sga=sga, sgb=sgb, merged=merged, gna4=gna4)


def _tail_in_specs(tm):
    tile = lambda c: pl.BlockSpec((tm, KW), lambda i: (i, c))
    return [tile(c) for c in range(4, 11)]


def _tail_weight_specs():
    return [_full((1, DK)), _full((1, KW)), _full((1, KW)), _full((GROUPS, SGU_BLOCK, SGU_BLOCK)),
            _full((GROUPS, SGU_BLOCK, 1)), _full((D, KW), single=True), _full((D, KW), single=True),
            _full((D, D), single=True)]


def _read_tail_inputs(of_ref, ob_ref, pcols):
    osum = of_ref[...] + ob_ref[...]
    og, u, v = (pcols[j][...].astype(F32) for j in range(3))
    ga = jnp.concatenate([pcols[3][...], pcols[4][...]], axis=1).astype(F32)
    gb = jnp.concatenate([pcols[5][...], pcols[6][...]], axis=1).astype(F32)
    return osum, og, u, v, ga, gb


def _tail_fwd(p, o_up, o_down, xt, modv, gna, ln_g, ln_b, w_s, b_s, w_paT, w_pbT, w_o, rows_per_example):
    rows = xt.shape[0]
    tm = min(TOKEN_TILE, rows_per_example)
    per_b = rows_per_example // tm

    def body(of_ref, ob_ref, *rest):
        pcols = rest[:7]
        (x_ref, mod_ref, gna_ref, lng_ref, lnb_ref, ws_ref, bs_ref, wpaT_ref, wpbT_ref, wo_ref,
         x1_ref, mix_ref, merged_ref, oa_ref, obm_ref) = rest[7:]
        t = _tail_forward(*_read_tail_inputs(of_ref, ob_ref, pcols), gna_ref[...], lng_ref[...], lnb_ref[...],
                          ws_ref, bs_ref, wpaT_ref, wpbT_ref)
        mix = _dot(t["merged"], wo_ref[...])
        x1_ref[...] = x_ref[...] + mod_ref[0, 2:3, :] * mix
        mix_ref[...] = mix.astype(mix_ref.dtype)
        merged_ref[...] = t["merged"].astype(merged_ref.dtype)
        oa_ref[...] = t["oa"].astype(oa_ref.dtype)
        obm_ref[...] = t["obm"].astype(obm_ref.dtype)

    row = lambda w: pl.BlockSpec((tm, w), lambda i: (i, 0))
    in_specs = [row(KW), row(KW)] + _tail_in_specs(tm) + [row(D), pl.BlockSpec((1, N_MOD, D), lambda i: (i // per_b, 0, 0))]
    in_specs += _tail_weight_specs()
    return pl.pallas_call(
        body, name="tail_fwd", grid=(rows // tm,),
        out_shape=(jax.ShapeDtypeStruct((rows, D), F32), jax.ShapeDtypeStruct((rows, D), MXU_DTYPE),
                   jax.ShapeDtypeStruct((rows, D), MXU_DTYPE), jax.ShapeDtypeStruct((rows, KW), MXU_DTYPE),
                   jax.ShapeDtypeStruct((rows, KW), MXU_DTYPE)),
        in_specs=in_specs, out_specs=(row(D), row(D), row(D), row(KW), row(KW)),
        compiler_params=_params(("arbitrary",)),
    )(o_up, o_down, *([p] * 7), xt, modv, gna, ln_g, ln_b, w_s, b_s, w_paT, w_pbT, w_o)


def _tail_bwd(p, o_up, o_down, dx1, mix, modv, gna, ln_g, ln_b, w_s, b_s, w_paT, w_pbT, w_o, rows_per_example,
              after=None, sender=None):
    rows = dx1.shape[0]
    nb_ex = rows // rows_per_example
    tm = min(TOKEN_TILE, rows_per_example)
    per_b = rows_per_example // tm

    def body(of_ref, ob_ref, *rest):
        pcols = rest[:7]
        (dx1_ref, mix_ref, mod_ref, gna_ref, lng_ref, lnb_ref, ws_ref, bs_ref, wpaT_ref, wpbT_ref, wo_ref,
         dpt_ref, do_ref, dmix_ref, dpa_ref, dpb_ref, dmod_ref, small_ref, dws_ref, dbs_ref) = rest[7:]
        i = pl.program_id(0)

        @pl.when(i == 0)
        def _():
            small_ref[...] = jnp.zeros_like(small_ref)
            dws_ref[...] = jnp.zeros_like(dws_ref)
            dbs_ref[...] = jnp.zeros_like(dbs_ref)

        @pl.when(i % per_b == 0)
        def _():
            dmod_ref[...] = jnp.zeros_like(dmod_ref)

        osum, og, u, v, ga, gb = _read_tail_inputs(of_ref, ob_ref, pcols)
        ln_g = lng_ref[...]
        t = _tail_forward(osum, og, u, v, ga, gb, gna_ref[...], ln_g, lnb_ref[...], ws_ref, bs_ref, wpaT_ref, wpbT_ref)
        dx1v = dx1_ref[...]
        dmod_ref[0, 2:3, :] += jnp.sum(dx1v * mix_ref[...].astype(F32), axis=0, keepdims=True)
        dmix = dx1v * mod_ref[0, 2:3, :]
        dmix_ref[...] = dmix.astype(dmix_ref.dtype)
        dmerged = _dot(dmix, wo_ref[...], "nt")
        sga, sgb = t["sga"], t["sgb"]
        dpa = dmerged * sga
        dpb = dmerged * sgb
        dpa_ref[...] = dpa.astype(dpa_ref.dtype)
        dpb_ref[...] = dpb.astype(dpb_ref.dtype)
        dga = dmerged * t["pa"] * sga * (1.0 - sga)
        dgb = dmerged * t["pb"] * sgb * (1.0 - sgb)
        doa = _dot(dpa, wpaT_ref[...])
        dobm = _dot(dpb, wpbT_ref[...])
        dug = dobm * t["mixed"]
        dmixed = dobm * t["ug"]
        du = dug * _gelu_grad(u, t["tu"])
        dvln_blocks = []
        for n in range(tm // SGU_BLOCK):
            rs = slice(n * SGU_BLOCK, (n + 1) * SGU_BLOCK)
            parts = []
            for g in range(GROUPS):
                gs = slice(g * DK, (g + 1) * DK)
                dm = dmixed[rs, gs]
                parts.append(_dot(ws_ref[g], dm, "tn"))
                dws_ref[g] += _dot(dm, t["vln"][rs, gs], "nt")
                dbs_ref[g] += jnp.sum(dm, axis=1, keepdims=True)
            dvln_blocks.append(jnp.concatenate(parts, axis=1))
        dvln = jnp.concatenate(dvln_blocks, axis=0) if len(dvln_blocks) > 1 else dvln_blocks[0]
        vhat = t["vhat"]
        small_ref[1:2, 0:KW] += jnp.sum(dvln * vhat, axis=0, keepdims=True)
        small_ref[2:3, 0:KW] += jnp.sum(dvln, axis=0, keepdims=True)
        dvhat = dvln * ln_g
        dvg = t["rstd"] * (dvhat - jnp.mean(dvhat, axis=-1, keepdims=True)
                           - vhat * jnp.mean(dvhat * vhat, axis=-1, keepdims=True))
        dv = dvg * _gelu_grad(v, t["tv"])
        sg_og = t["sg_og"]
        doan = doa * t["silu_og"]
        dog = doa * t["oan"] * (sg_og * (1.0 + og * (1.0 - sg_og)))
        prod = doan * t["on"]
        dgna = jnp.zeros((1, DK), F32)
        for h in range(HEADS):
            dgna = dgna + jnp.sum(prod[:, h * DK:(h + 1) * DK], axis=0, keepdims=True)
        small_ref[0:1, 0:DK] += dgna
        don = doan * t["gna4"]
        dot_parts = []
        for h in range(HEADS):
            hs = slice(h * DK, (h + 1) * DK)
            m = jnp.mean(don[:, hs] * t["on"][:, hs], axis=-1, keepdims=True)
            dot_parts.append(t["r"][:, hs] * (don[:, hs] - t["on"][:, hs] * m))
        do_ref[...] = jnp.concatenate(dot_parts, axis=1).astype(do_ref.dtype)
        for j, val in enumerate((dog, du, dv)):
            dpt_ref[:, j * KW:(j + 1) * KW] = val.astype(dpt_ref.dtype)
        dpt_ref[:, 3 * KW:3 * KW + D] = dga.astype(dpt_ref.dtype)
        dpt_ref[:, 3 * KW + D:] = dgb.astype(dpt_ref.dtype)

    row = lambda w: pl.BlockSpec((tm, w), lambda i: (i, 0))
    in_specs = [row(KW), row(KW)] + _tail_in_specs(tm) + [row(D), row(D), pl.BlockSpec((1, N_MOD, D), lambda i: (i // per_b, 0, 0))]
    in_specs += _tail_weight_specs()
    args = [o_up, o_down, *([p] * 7), dx1, mix, modv, gna, ln_g, ln_b, w_s, b_s, w_paT, w_pbT, w_o]
    cd = MXU_DTYPE
    res, handle = _host_call(
        body, "tail_bwd", (rows // tm,), in_specs, args,
        [jax.ShapeDtypeStruct((rows, TAIL_COLS), cd), jax.ShapeDtypeStruct((rows, KW), cd),
         jax.ShapeDtypeStruct((rows, D), cd), jax.ShapeDtypeStruct((rows, D), cd),
         jax.ShapeDtypeStruct((rows, D), cd), jax.ShapeDtypeStruct((nb_ex, 8, D), F32),
         jax.ShapeDtypeStruct((8, D), F32), jax.ShapeDtypeStruct((GROUPS, SGU_BLOCK, SGU_BLOCK), F32),
         jax.ShapeDtypeStruct((GROUPS, SGU_BLOCK, 1), F32)],
        [row(TAIL_COLS), row(KW), row(D), row(D), row(D),
         pl.BlockSpec((1, 8, D), lambda i: (i // per_b, 0, 0)), _full((8, D)),
         _full((GROUPS, SGU_BLOCK, SGU_BLOCK)), _full((GROUPS, SGU_BLOCK, 1))], [],
        after=after, sender=sender)
    return (*res, handle)


def _ffn(x1, target, modv, g_ffn, g_final, w_upT, w_down, rows_per_example):
    rows = x1.shape[0]
    nb_ex = rows // rows_per_example
    tm = min(TOKEN_TILE, rows_per_example)
    per_b = rows_per_example // tm
    n_ff = D_FF // FF_CHUNK

    def body(x1_ref, tgt_ref, mod_ref, gffn_ref, gfin_ref, wup_ref, wdn_ref,
             dx1_ref, h2_ref, dffn_ref, act_ref, dup_ref, dmod_ref, small_ref, a_scr, b_scr):
        i = pl.program_id(0)

        @pl.when(i == 0)
        def _():
            small_ref[...] = jnp.zeros_like(small_ref)

        @pl.when(i % per_b == 0)
        def _():
            dmod_ref[...] = jnp.zeros_like(dmod_ref)

        x1v = x1_ref[...]
        g2 = gffn_ref[...]
        m3, m4, m5 = mod_ref[0, 3:4, :], mod_ref[0, 4:5, :], mod_ref[0, 5:6, :]
        r2 = lax.rsqrt(jnp.mean(x1v * x1v, axis=-1, keepdims=True) + EPS)
        xn2 = x1v * r2
        h2 = (xn2 * g2) * (1.0 + m4) + m3
        h2b = h2.astype(MXU_DTYPE)
        h2_ref[...] = h2b
        ffn = jnp.zeros((tm, D), F32)
        for j in range(n_ff):
            cs = slice(j * FF_CHUNK, (j + 1) * FF_CHUNK)
            a = _dot(h2b, wup_ref[j * FF_CHUNK:(j + 1) * FF_CHUNK, :], "nt")
            bgate = _dot(h2b, wup_ref[D_FF + j * FF_CHUNK:D_FF + (j + 1) * FF_CHUNK, :], "nt")
            a_scr[:, cs] = a
            b_scr[:, cs] = bgate
            act = (a * _sigmoid(a) * bgate).astype(MXU_DTYPE)
            act_ref[:, cs] = act
            ffn = ffn + _dot(act, wdn_ref[cs, :])
        x2 = x1v + m5 * ffn
        r3 = lax.rsqrt(jnp.mean(x2 * x2, axis=-1, keepdims=True) + EPS)
        xn3 = x2 * r3
        gf = gfin_ref[...]
        err = xn3 * gf - tgt_ref[...]
        loss = 0.5 * jnp.sum(jnp.mean(err * err, axis=-1, keepdims=True), axis=0, keepdims=True)
        small_ref[2:3, :] += jnp.broadcast_to(loss, (1, D))
        dy = err * (1.0 / D)
        small_ref[1:2, :] += jnp.sum(dy * xn3, axis=0, keepdims=True)
        dxn3 = dy * gf
        dx2 = r3 * (dxn3 - xn3 * jnp.mean(dxn3 * xn3, axis=-1, keepdims=True))
        dmod_ref[0, 5:6, :] += jnp.sum(dx2 * ffn, axis=0, keepdims=True)
        dffn = (dx2 * m5).astype(MXU_DTYPE)
        dffn_ref[...] = dffn
        dh2 = jnp.zeros((tm, D), F32)
        for j in range(n_ff):
            cs = slice(j * FF_CHUNK, (j + 1) * FF_CHUNK)
            dact = _dot(dffn, wdn_ref[cs, :], "nt")
            a, bgate = a_scr[:, cs], b_scr[:, cs]
            s = _sigmoid(a)
            da = (dact * bgate * (s * (1.0 + a * (1.0 - s)))).astype(MXU_DTYPE)
            dbg = (dact * a * s).astype(MXU_DTYPE)
            dup_ref[:, cs] = da
            dup_ref[:, D_FF + j * FF_CHUNK:D_FF + (j + 1) * FF_CHUNK] = dbg
            dh2 = dh2 + _dot(da, wup_ref[j * FF_CHUNK:(j + 1) * FF_CHUNK, :])
            dh2 = dh2 + _dot(dbg, wup_ref[D_FF + j * FF_CHUNK:D_FF + (j + 1) * FF_CHUNK, :])
        dmod_ref[0, 3:4, :] += jnp.sum(dh2, axis=0, keepdims=True)
        dmod_ref[0, 4:5, :] += jnp.sum(dh2 * xn2 * g2, axis=0, keepdims=True)
        small_ref[0:1, :] += jnp.sum(dh2 * (1.0 + m4) * xn2, axis=0, keepdims=True)
        dxn2 = dh2 * g2 * (1.0 + m4)
        dx1_ref[...] = dx2 + r2 * (dxn2 - xn2 * jnp.mean(dxn2 * xn2, axis=-1, keepdims=True))

    row = lambda w: pl.BlockSpec((tm, w), lambda i: (i, 0))
    cd = MXU_DTYPE
    return pl.pallas_call(
        body, name="ffn_fwd_bwd", grid=(rows // tm,),
        out_shape=(jax.ShapeDtypeStruct((rows, D), F32), jax.ShapeDtypeStruct((rows, D), cd),
                   jax.ShapeDtypeStruct((rows, D), cd), jax.ShapeDtypeStruct((rows, D_FF), cd),
                   jax.ShapeDtypeStruct((rows, 2 * D_FF), cd), jax.ShapeDtypeStruct((nb_ex, 8, D), F32),
                   jax.ShapeDtypeStruct((8, D), F32)),
        in_specs=[row(D), row(D), pl.BlockSpec((1, N_MOD, D), lambda i: (i // per_b, 0, 0)), _full((1, D)), _full((1, D)),
                  _full((2 * D_FF, D), single=True), _full((D_FF, D), single=True)],
        out_specs=(row(D), row(D), row(D), row(D_FF), row(2 * D_FF),
                   pl.BlockSpec((1, 8, D), lambda i: (i // per_b, 0, 0)), _full((8, D))),
        scratch_shapes=[pltpu.VMEM((tm, D_FF), F32), pltpu.VMEM((tm, D_FF), F32)],
        compiler_params=_params(("arbitrary",)),
    )(x1, target, modv, g_ffn, g_final, w_upT, w_down)


def _inproj_bwd(pieces, dpt, xt, dx1, modv, g, w_inT, rows_per_example, name):
    rows = xt.shape[0]
    latent = dx1 is not None
    n_cols = IN_COLS if latent else CTX_COLS
    tm = min(TOKEN_TILE, rows_per_example)
    per_b = rows_per_example // tm
    n_mod_blocks = rows // rows_per_example if latent else 1
    n_pieces = len(pieces)

    def body(*refs):
        it = iter(refs)
        pc = [next(it) for _ in range(n_pieces)]
        dpt_ref = next(it) if latent else None
        x_ref = next(it)
        dx1_ref = next(it) if latent else None
        mod_ref, g_ref, w_ref = next(it), next(it), next(it)
        gx_ref = next(it) if latent else None
        dp_ref, dmod_ref, small_ref = next(it), next(it), next(it)
        i = pl.program_id(0)

        @pl.when(i == 0)
        def _():
            small_ref[...] = jnp.zeros_like(small_ref)

        @pl.when((i % per_b == 0) if latent else (i == 0))
        def _():
            dmod_ref[...] = jnp.zeros_like(dmod_ref)

        cols = [pc[0][...], pc[1][...], pc[2][...] + pc[3][...]]
        if latent:
            cols.append(pc[4][...] + pc[5][...])
        dh = jnp.zeros((tm, D), F32)
        for j, val in enumerate(cols):
            vb = val.astype(MXU_DTYPE)
            dp_ref[:, j * KW:(j + 1) * KW] = vb
            dh = dh + _dot(vb, w_ref[j * KW:(j + 1) * KW, :])
        if latent:
            for j in range(4, IN_COLS // KW):
                vb = dpt_ref[:, (j - 4) * KW:(j - 3) * KW]
                dp_ref[:, j * KW:(j + 1) * KW] = vb
                dh = dh + _dot(vb, w_ref[j * KW:(j + 1) * KW, :])
        x = x_ref[...]
        gv = g_ref[...]
        m1 = mod_ref[0, 1:2, :]
        r = lax.rsqrt(jnp.mean(x * x, axis=-1, keepdims=True) + EPS)
        xn = x * r
        dmod_ref[0, 0:1, :] += jnp.sum(dh, axis=0, keepdims=True)
        dmod_ref[0, 1:2, :] += jnp.sum(dh * xn * gv, axis=0, keepdims=True)
        small_ref[0:1, :] += jnp.sum(dh * (1.0 + m1) * xn, axis=0, keepdims=True)
        if latent:
            dxn = dh * gv * (1.0 + m1)
            gx_ref[...] = dx1_ref[...] + r * (dxn - xn * jnp.mean(dxn * xn, axis=-1, keepdims=True))

    row = lambda w: pl.BlockSpec((tm, w), lambda i: (i, 0))
    mod_idx = (lambda i: (i // per_b, 0, 0)) if latent else (lambda i: (0, 0, 0))
    in_specs = [row(KW)] * n_pieces + ([row(TAIL_COLS)] if latent else []) + [row(D)] + ([row(D)] if latent else [])
    in_specs += [pl.BlockSpec((1, N_MOD, D), mod_idx), _full((1, D)),
                 pl.BlockSpec((n_cols, D), lambda i: (0, 0), pipeline_mode=pl.Buffered(1))]
    args = list(pieces) + ([dpt] if latent else []) + [xt] + ([dx1] if latent else []) + [modv, g, w_inT]
    out_shape = ([jax.ShapeDtypeStruct((rows, D), F32)] if latent else []) + [
        jax.ShapeDtypeStruct((rows, n_cols), MXU_DTYPE), jax.ShapeDtypeStruct((n_mod_blocks, 8, D), F32),
        jax.ShapeDtypeStruct((8, D), F32)]
    out_specs = ([row(D)] if latent else []) + [row(n_cols), pl.BlockSpec((1, 8, D), mod_idx), _full((8, D))]
    return pl.pallas_call(
        body, name=name, grid=(rows // tm,), out_shape=out_shape, in_specs=in_specs, out_specs=out_specs,
        compiler_params=_params(("arbitrary",)),
    )(*args)


def _grad_matmul(a, b, name, init=None, tn=512, tt=1024, sender=None):
    rows, n = a.shape
    k = b.shape[1]
    tn = min(tn, n)
    tt = min(tt, rows)
    steps = rows // tt
    has_init = init is not None
    init_blocks = init.shape[0] // tn if has_init else 0

    def body(*refs):
        if has_init:
            a_ref, b_ref, init_ref, o_ref, acc = refs
        else:
            a_ref, b_ref, o_ref, acc = refs
        i, t = pl.program_id(0), pl.program_id(1)

        @pl.when(t == 0)
        def _():
            acc[...] = jnp.zeros_like(acc)

        if has_init:
            @pl.when((t == 0) & (i < init_blocks))
            def _():
                acc[...] = init_ref[...].astype(F32)

        acc[...] += _dot(a_ref[...], b_ref[...], "tn")

        @pl.when(t == steps - 1)
        def _():
            o_ref[...] = acc[...].astype(o_ref.dtype)

    in_specs = [pl.BlockSpec((tt, tn), lambda i, t: (t, i)), pl.BlockSpec((tt, k), lambda i, t: (t, 0))]
    args = [a, b]
    if has_init:
        in_specs.append(pl.BlockSpec((tn, k), lambda i, t: (jnp.minimum(i, init_blocks - 1), 0)))
        args.append(init)
    (out,), handle = _host_call(
        body, name, (n // tn, steps), in_specs, args, [jax.ShapeDtypeStruct((n, k), PAYLOAD_DTYPE)],
        [pl.BlockSpec((tn, k), lambda i, t: (i, 0))], [pltpu.VMEM((tn, k), F32)], sender=sender)
    return out, handle


def _row_tile(rows, limit=256):
    if rows <= limit:
        return rows
    for t in range(limit, 7, -8):
        if rows % t == 0:
            return t
    return rows


def _sum8(stack, name):
    _, rows, cols = stack.shape
    tr = _row_tile(rows)

    def body(s_ref, o_ref):
        acc = s_ref[0].astype(F32)
        for j in range(1, N_DEV):
            acc = acc + s_ref[j].astype(F32)
        o_ref[...] = acc

    return pl.pallas_call(
        body, name=name, grid=(rows // tr,), out_shape=jax.ShapeDtypeStruct((rows, cols), F32),
        in_specs=[pl.BlockSpec((N_DEV, tr, cols), lambda i: (0, i, 0))],
        out_specs=pl.BlockSpec((tr, cols), lambda i: (i, 0)),
        compiler_params=_params(("arbitrary",)),
    )(stack)


def _small_reduce(stack, gam, nb_ex):
    def body(s_ref, gam_ref, o_ref, bm_ref):
        acc = s_ref[0]
        for j in range(1, N_DEV):
            acc = acc + s_ref[j]
        o_ref[...] = acc
        bm = acc[8:8 + N_MOD, :]
        for e in range(nb_ex):
            bm = bm + acc[16 + e * N_MOD:16 + (e + 1) * N_MOD, :]
        lb = jnp.concatenate([_lower_bound(gam_ref, 0), _lower_bound(gam_ref, 1)], axis=1)
        dgam = acc[7:8, :] * lb * (1.0 - lb)
        bm_ref[...] = jnp.concatenate([bm, dgam, -dgam], axis=0)

    return pl.pallas_call(
        body, name="small_reduce", grid=(1,),
        out_shape=(jax.ShapeDtypeStruct((SMALL_ROWS, D), F32), jax.ShapeDtypeStruct((8, D), F32)),
        in_specs=[_full((N_DEV, SMALL_ROWS, D)), _full((4, KW))], out_specs=(_full((SMALL_ROWS, D)), _full((8, D))),
        compiler_params=_params(("arbitrary",)),
    )(stack, gam)


def _adamw_update(w, gv, m, v):
    nm = ADAM_B1 * m + (1.0 - ADAM_B1) * gv
    nv = ADAM_B2 * v + (1.0 - ADAM_B2) * (gv * gv)
    m_hat = nm / (1.0 - ADAM_B1 ** ADAM_STEP)
    v_hat = nv / (1.0 - ADAM_B2 ** ADAM_STEP)
    return -ADAM_LR * (m_hat / (jnp.sqrt(v_hat) + ADAM_EPS) + ADAM_WD * w), nm, nv


def _adamw_sum8(stack, w, m, v, name):
    _, rows, cols = stack.shape
    tr = _row_tile(rows)

    def body(s_ref, w_ref, m_ref, v_ref, g_ref, d_ref, nm_ref, nv_ref):
        gv = s_ref[0].astype(F32)
        for j in range(1, N_DEV):
            gv = gv + s_ref[j].astype(F32)
        g_ref[...] = gv
        d_ref[...], nm_ref[...], nv_ref[...] = _adamw_update(w_ref[...], gv, m_ref[...], v_ref[...])

    blk = pl.BlockSpec((tr, cols), lambda i: (i, 0))
    sd = jax.ShapeDtypeStruct((rows, cols), F32)
    return pl.pallas_call(
        body, name=name, grid=(rows // tr,), out_shape=(sd, sd, sd, sd),
        in_specs=[pl.BlockSpec((N_DEV, tr, cols), lambda i: (0, i, 0)), blk, blk, blk], out_specs=(blk, blk, blk, blk),
        compiler_params=_params(("arbitrary",)),
    )(stack, w, m, v)


def _adamw(w, g, m, v, name):
    shape = w.shape
    cols = shape[-1]
    rows = 1
    for s in shape[:-1]:
        rows *= s
    tr = _row_tile(rows)

    def body(w_ref, g_ref, m_ref, v_ref, d_ref, nm_ref, nv_ref):
        gv = g_ref[...]
        nm = ADAM_B1 * m_ref[...] + (1.0 - ADAM_B1) * gv
        nv = ADAM_B2 * v_ref[...] + (1.0 - ADAM_B2) * (gv * gv)
        m_hat = nm / (1.0 - ADAM_B1 ** ADAM_STEP)
        v_hat = nv / (1.0 - ADAM_B2 ** ADAM_STEP)
        d_ref[...] = -ADAM_LR * (m_hat / (jnp.sqrt(v_hat) + ADAM_EPS) + ADAM_WD * w_ref[...])
        nm_ref[...] = nm
        nv_ref[...] = nv

    blk = pl.BlockSpec((tr, cols), lambda i: (i, 0))
    sd = jax.ShapeDtypeStruct((rows, cols), F32)
    d, nm, nv = pl.pallas_call(
        body, name=name, grid=(rows // tr,), out_shape=(sd, sd, sd), in_specs=[blk] * 4, out_specs=(blk, blk, blk),
        compiler_params=_params(("arbitrary",)),
    )(w.reshape(rows, cols), g.reshape(rows, cols), m.reshape(rows, cols), v.reshape(rows, cols))
    return d.reshape(shape), nm.reshape(shape), nv.reshape(shape)


def _owner_blocks(a):
    return a.reshape(N_DEV, a.shape[0] // N_DEV, a.shape[1])


class _LocalWeights:
    def __init__(self, w_upT, w_down, w_o, w_paT, w_pbT):
        self.weights = (w_upT, w_down, w_o, w_paT, w_pbT)
        self.items = {}

    def sender(self, stage, items=None):
        self.items[stage] = items
        return None

    def sent(self, stage, handle):
        pass

    def mixer_weights(self, after):
        return self.weights[1:]

    def ffn_weights(self, after):
        return self.weights[0]


def _local_step(x, ctx, target, modv, mcv, gam, g_mix, g_ffn, gna, ln_g, ln_b, w_s, b_s, g_final, w_inT, comm):
    nb_ex, seq, _ = x.shape
    ctx_len = ctx.shape[1]
    xt = x.reshape(nb_ex * seq, D)
    ct = ctx.reshape(nb_ex * ctx_len, D)
    tgt = target.reshape(nb_ex * seq, D)
    bs3 = b_s.reshape(GROUPS, SGU_BLOCK, 1)

    pc, hc, _ = _inproj(ct, mcv, g_mix, w_inT, CTX_COLS, ctx_len, "inproj_ctx")
    p, h, handle = _inproj(xt, modv, g_mix, w_inT, IN_COLS, seq, "inproj_lat", sender=comm.sender("inproj"))
    comm.sent("inproj", handle)
    cst_f, cst_b, s_ctx, _ = _hgrn_fwd(pc, gam, None, ctx_len, False, "hgrn_fwd_ctx")
    o_up, o_down, st_f, st_b, _, handle = _hgrn_fwd(p, gam, s_ctx, seq, True, "hgrn_fwd_lat",
                                                    sender=comm.sender("scan"))
    comm.sent("scan", handle)
    w_down, w_o, w_paT, w_pbT = comm.mixer_weights(o_up)
    x1, mix, merged, oa, obm = _tail_fwd(p, o_up, o_down, xt, modv, gna, ln_g, ln_b, w_s, bs3, w_paT, w_pbT, w_o, seq)
    w_upT = comm.ffn_weights(x1)
    dx1, h2, dffn, act, dup, dmod_ffn, small_ffn = _ffn(x1, tgt, modv, g_ffn, g_final, w_upT, w_down, seq)
    dpt, do, dmix, dpa, dpb, dmod_tail, small_tail, dws, dbs, _ = _tail_bwd(
        p, o_up, o_down, dx1, mix, modv, gna, ln_g, ln_b, w_s, bs3, w_paT, w_pbT, w_o, seq)
    dzf, dvf, dqf, dzb, dvb, dqb, dlb, ds0, _ = _hgrn_bwd(p, gam, do, st_f, st_b, None, seq, True, "hgrn_bwd_lat")
    czf, cvf, czb, cvb, dlb_c, _, _ = _hgrn_bwd(pc, gam, None, cst_f, cst_b, ds0, ctx_len, False, "hgrn_bwd_ctx")
    grad_x, dp, dmod_in, small_in = _inproj_bwd([dzf, dzb, dvf, dvb, dqf, dqb], dpt, xt, dx1, modv, g_mix, w_inT,
                                                 seq, "inproj_bwd_lat")
    dpc, dmc, small_c = _inproj_bwd([czf, czb, cvf, cvb], None, ct, None, mcv, g_mix, w_inT, ctx_len, "inproj_bwd_ctx")

    z = lambda r: jnp.zeros((r, D), F32)
    pad = lambda a: jnp.pad(a, ((0, 0), (0, D - a.shape[1])))
    dlb_row = (dlb + dlb_c).reshape(1, 2 * KW)
    dmod = dmod_in + dmod_tail + dmod_ffn
    small = jnp.concatenate([
        small_in[0:1] + small_c[0:1],
        small_ffn[0:1],
        small_ffn[1:2],
        small_tail[0:1],
        small_tail[1:2],
        small_tail[2:3],
        pad(dbs.reshape(1, GROUPS * SGU_BLOCK)),
        dlb_row,
        dmc[0, 0:N_MOD],
        z(2),
        dmod[:, 0:N_MOD].reshape(nb_ex * N_MOD, D),
        z(24 - nb_ex * N_MOD),
        dws.reshape(GROUPS * SGU_BLOCK * SGU_BLOCK // D, D),
    ], axis=0)

    chain = [("gw_in", dp, h, dict(init=_grad_matmul(dpc, hc, "gw_in_ctx")[0])), ("gw_up", dup, h2, {}),
             ("gw_down", act, dffn, dict(tn=256)), ("gw_o", merged, dmix, {}), ("gw_pb", dpb, obm, {}),
             ("gw_pa", dpa, oa, {})]
    items = [(small, "gather")]
    for name, a, b, kw in chain:
        g, handle = _grad_matmul(a, b, name, sender=comm.sender(name, items), **kw)
        comm.sent(name, handle)
        items = [(_owner_blocks(g), "scatter")]
    comm.sender("last", items)
    return small_ffn[2, 0], grad_x.reshape(x.shape)


def kernel(x, c, ctx, c_ctx, w_mod, b_mod, g_mix, g_ffn, w_in, lb_gamma, g_norm_a, ln_v_g, ln_v_b, w_s, b_s, w_pa, w_pb, w_o, w_up, w_down, g_final, loss_target, m_c_ctx, m_w_mod, m_b_mod, m_g_mix, m_g_ffn, m_w_in, m_lb_gamma, m_g_norm_a, m_ln_v_g, m_ln_v_b, m_w_s, m_b_s, m_w_pa, m_w_pb, m_w_o, m_w_up, m_w_down, m_g_final, v_c_ctx, v_w_mod, v_b_mod, v_g_mix, v_g_ffn, v_w_in, v_lb_gamma, v_g_norm_a, v_ln_v_g, v_ln_v_b, v_w_s, v_b_s, v_w_pa, v_w_pb, v_w_o, v_w_up, v_w_down, v_g_final):
    nb_ex = x.shape[0]
    me = 4 * lax.axis_index("x") + 2 * lax.axis_index("y") + lax.axis_index("c")
    cd = MXU_DTYPE
    mod_cols = w_mod.shape[2]
    lb_cols = lb_gamma.shape[2]

    w_inT_l = w_in[0].T.astype(cd)
    w_upT_l = w_up[0].T.astype(cd)
    w_paT_l = w_pa[0].T.astype(cd)
    w_pbT_l = w_pb[0].T.astype(cd)
    cl = jnp.concatenate([c, jnp.pad(lb_gamma.reshape(1, 4 * lb_cols), ((0, 0), (0, D - 4 * lb_cols))),
                          jnp.zeros((8 - nb_ex - 1, D), F32)], axis=0)
    g_in, g_cl = _exchange([(w_inT_l, "gather"), (cl, "gather")], "gather_w_in")
    w_inT = g_in.reshape(IN_COLS, D)
    c_all = g_cl[:, 0:nb_ex].reshape(N_DEV * nb_ex, D)
    gam = jnp.transpose(g_cl[:, nb_ex, 0:4 * lb_cols].reshape(N_DEV, 4, lb_cols), (1, 0, 2)).reshape(4, KW)

    n_c = N_DEV * nb_ex
    cvec = jnp.concatenate([c_all, c_ctx.reshape(1, D), jnp.zeros((7, D), F32)], axis=0)
    b_mod_l = lax.dynamic_slice(b_mod, (0, me * mod_cols), (1, mod_cols))
    mod_l, svec = _mod_fwd(cvec, w_mod[0], b_mod_l)
    (g_mod,) = _exchange([(mod_l, "gather")], "gather_mod")
    mod_all = jnp.transpose(g_mod, (1, 0, 2)).reshape(n_c + 8, N_MOD * D)
    modv = lax.dynamic_slice(mod_all, (me * nb_ex, 0), (nb_ex, N_MOD * D)).reshape(nb_ex, N_MOD, D)
    mcv = mod_all[n_c].reshape(1, N_MOD, D)

    handles, leftover = {}, {}

    class Comm:
        def sender(self, stage, items=None):
            if stage == "inproj":
                return _Sender([(w_down[0].astype(cd), "gather"), (w_o[0].astype(cd), "gather"), (w_paT_l, "gather"),
                                (w_pbT_l, "gather")])
            if stage == "scan":
                return _Sender([(w_upT_l, "gather")])
            if stage == "last":
                leftover["items"] = items
                return None
            return _Sender(items)

        def sent(self, stage, handle):
            handles[stage] = handle

        def mixer_weights(self, after):
            g_down, g_o, g_pa, g_pb = _exchange_wait(handles["inproj"], after)
            return g_down.reshape(D_FF, D), g_o.reshape(D, D), g_pa.reshape(D, KW), g_pb.reshape(D, KW)

        def ffn_weights(self, after):
            (g_up,) = _exchange_wait(handles["scan"], after)
            return g_up.reshape(2 * D_FF, D)

    loss_l, grad_x = _local_step(
        x, ctx, loss_target, modv, mcv, gam, g_mix, g_ffn, g_norm_a, ln_v_g, ln_v_b, w_s[0], b_s[0],
        g_final.reshape(1, D), w_inT, Comm())
    loss = lax.psum(loss_l, ("x", "y", "c"))
    last, last_started = _exchange_start(leftover["items"], "scatter_last", after=leftover["items"][0][0])

    (r_small,) = _exchange_wait(handles["gw_in"], last_started)
    (r_in,) = _exchange_wait(handles["gw_up"], r_small)
    done = {"w_in": [a.T[None] for a in _adamw_sum8(r_in, w_in[0].T, m_w_in[0].T, v_w_in[0].T, "adamw_w_in")]}
    (r_up,) = _exchange_wait(handles["gw_down"], done["w_in"][1])
    done["w_up"] = [a.T[None] for a in _adamw_sum8(r_up, w_up[0].T, m_w_up[0].T, v_w_up[0].T, "adamw_w_up")]
    (r_down,) = _exchange_wait(handles["gw_o"], done["w_up"][1])
    done["w_down"] = [a[None] for a in _adamw_sum8(r_down, w_down[0], m_w_down[0], v_w_down[0], "adamw_w_down")]
    (r_o,) = _exchange_wait(handles["gw_pb"], done["w_down"][1])
    done["w_o"] = [a[None] for a in _adamw_sum8(r_o, w_o[0], m_w_o[0], v_w_o[0], "adamw_w_o")]
    (r_pb,) = _exchange_wait(handles["gw_pa"], done["w_o"][1])
    (r_pa,) = _exchange_wait(last, r_pb)
    grad_w_in, grad_w_up, grad_w_down, grad_w_o = (done[k][0] for k in ("w_in", "w_up", "w_down", "w_o"))
    grad_w_pa = _sum8(r_pa, "sum_w_pa").T[None]
    grad_w_pb = _sum8(r_pb, "sum_w_pb").T[None]
    tot, bm = _small_reduce(r_small, gam, nb_ex)
    grad_g_mix, grad_g_ffn, grad_g_final = tot[0:1], tot[1:2], tot[2]
    grad_g_norm_a = tot[3:4, 0:DK]
    grad_ln_v_g, grad_ln_v_b = tot[4:5, 0:KW], tot[5:6, 0:KW]
    grad_b_s = tot[6, 0:GROUPS * SGU_BLOCK].reshape(1, GROUPS, SGU_BLOCK)
    grad_w_s = tot[40:104].reshape(1, GROUPS, SGU_BLOCK, SGU_BLOCK)
    grad_b_mod = bm[0:N_MOD].reshape(1, N_MOD * D)
    grad_lb_gamma = lax.dynamic_slice(bm[6:8].reshape(2, 2, KW), (0, 0, me * lb_cols), (2, 2, lb_cols))

    dmod_all = r_small[:, 16:16 + nb_ex * N_MOD].reshape(n_c, N_MOD * D)
    dmod_l = jnp.concatenate([lax.dynamic_slice(dmod_all, (0, me * mod_cols), (n_c, mod_cols)),
                              lax.dynamic_slice(tot[8:8 + N_MOD].reshape(1, N_MOD * D), (0, me * mod_cols), (1, mod_cols)),
                              jnp.zeros((7, mod_cols), F32)], axis=0)
    gw_mod, gc = _mod_bwd(svec, cvec, dmod_l, w_mod[0])
    grad_w_mod = gw_mod[None]
    (r_gc,) = _exchange([(gc[n_c:n_c + 8], "gather")], "gather_c_ctx", after=r_pa)
    grad_c_ctx = _sum8(r_gc, "sum_c_ctx")[0]

    names = ["c_ctx", "w_mod", "b_mod", "g_mix", "g_ffn", "w_in", "lb_gamma", "g_norm_a", "ln_v_g", "ln_v_b", "w_s",
             "b_s", "w_pa", "w_pb", "w_o", "w_up", "w_down", "g_final"]
    weights = [c_ctx, w_mod, b_mod, g_mix, g_ffn, w_in, lb_gamma, g_norm_a, ln_v_g, ln_v_b, w_s, b_s, w_pa, w_pb, w_o,
               w_up, w_down, g_final]
    grads = [grad_c_ctx, grad_w_mod, grad_b_mod, grad_g_mix, grad_g_ffn, grad_w_in, grad_lb_gamma, grad_g_norm_a,
             grad_ln_v_g, grad_ln_v_b, grad_w_s, grad_b_s, grad_w_pa, grad_w_pb, grad_w_o, grad_w_up, grad_w_down,
             grad_g_final]
    ms = [m_c_ctx, m_w_mod, m_b_mod, m_g_mix, m_g_ffn, m_w_in, m_lb_gamma, m_g_norm_a, m_ln_v_g, m_ln_v_b, m_w_s, m_b_s,
          m_w_pa, m_w_pb, m_w_o, m_w_up, m_w_down, m_g_final]
    vs = [v_c_ctx, v_w_mod, v_b_mod, v_g_mix, v_g_ffn, v_w_in, v_lb_gamma, v_g_norm_a, v_ln_v_g, v_ln_v_b, v_w_s, v_b_s,
          v_w_pa, v_w_pb, v_w_o, v_w_up, v_w_down, v_g_final]
    deltas, new_ms, new_vs = [], [], []
    for nm, w, g, m, v in zip(names, weights, grads, ms, vs):
        d, nm_, nv_ = done[nm][1:] if nm in done else _adamw(w, g.reshape(w.shape), m, v, "adamw_" + nm)
        deltas.append(d)
        new_ms.append(nm_)
        new_vs.append(nv_)
    grads = [g.reshape(w.shape) for g, w in zip(grads, weights)]
    return (loss, grad_x, *grads, *deltas, *new_ms, *new_vs)
```

```python
import functools

import jax
import jax.numpy as jnp
from jax import lax
from jax.experimental import pallas as pl
from jax.experimental.pallas import tpu as pltpu

F32 = jnp.float32
MXU_DTYPE = jnp.bfloat16
PAYLOAD_DTYPE = jnp.bfloat16

N_DEV = 8
D = 1024
HEADS = 4
DK = 128
KW = HEADS * DK
CHUNK = 64
SGU_BLOCK = 128
GROUPS = 4
D_FF = 2816
FF_CHUNK = 256
N_MOD = 6
IN_COLS = 5632
CTX_COLS = 1536
TAIL_COLS = IN_COLS - 4 * KW
EPS = 1e-6
ADAM_LR, ADAM_B1, ADAM_B2, ADAM_EPS, ADAM_WD, ADAM_STEP = 0.001, 0.9, 0.999, 1e-08, 0.01, 10

VMEM_LIMIT = 56 * 1024 * 1024
TOKEN_TILE = 256
SMALL_ROWS = 104


def _params(sem):
    return pltpu.CompilerParams(dimension_semantics=sem, vmem_limit_bytes=VMEM_LIMIT)


_DN = {"nn": (((1,), (0,)), ((), ())), "nt": (((1,), (1,)), ((), ())), "tn": (((0,), (0,)), ((), ()))}


def _dot(a, b, form="nn"):
    return lax.dot_general(a.astype(MXU_DTYPE), b.astype(MXU_DTYPE), _DN[form], preferred_element_type=F32)


def _dotx(a, b, form="nn"):
    return lax.dot_general(a.astype(F32), b.astype(F32), _DN[form], preferred_element_type=F32,
                           precision=lax.Precision.HIGHEST)


def _full(shape, single=False):
    n = len(shape)
    if single:
        return pl.BlockSpec(shape, lambda *_: (0,) * n, pipeline_mode=pl.Buffered(1))
    return pl.BlockSpec(shape, lambda *_: (0,) * n)


def _ordered_behind(body, in_specs, args, after):
    if after is None:
        return body
    at = len(in_specs)
    in_specs.append(pl.BlockSpec(memory_space=pl.ANY))
    args.append(after)
    return lambda *refs: body(*refs[:at], *refs[at + 1:])


def _sigmoid(z):
    return 1.0 / (1.0 + jnp.exp(-z))


def _gelu(x):
    c = 0.7978845608028654
    t = jnp.tanh(c * (x + 0.044715 * x * x * x))
    return 0.5 * x * (1.0 + t), t


def _gelu_grad(x, t):
    c = 0.7978845608028654
    return 0.5 * (1.0 + t) + 0.5 * x * (1.0 - t * t) * c * (1.0 + 3 * 0.044715 * x * x)


def _exchange(items, name, after=None):
    n = len(items)
    out_shape = []
    for a, mode in items:
        blk = a.shape if mode == "gather" else a.shape[1:]
        out_shape.append(jax.ShapeDtypeStruct((N_DEV,) + tuple(blk), a.dtype))

    def body(*refs):
        srcs, dsts = refs[:n], refs[n:2 * n]
        send_sems, recv_sems, local_sems = refs[2 * n:]
        x, y, c = lax.axis_index("x"), lax.axis_index("y"), lax.axis_index("c")
        me = 4 * x + 2 * y + c

        def src_for(i, dev):
            return srcs[i] if items[i][1] == "gather" else srcs[i].at[dev]

        local = [pltpu.make_async_copy(src_for(i, me), dsts[i].at[me], local_sems.at[i]) for i in range(n)]
        for cp in local:
            cp.start()
        remote = []
        for k in range(1, N_DEV):
            px = jnp.bitwise_xor(x, (k >> 2) & 1)
            py = jnp.bitwise_xor(y, (k >> 1) & 1)
            pc = jnp.bitwise_xor(c, k & 1)
            peer = 4 * px + 2 * py + pc
            for i in range(n):
                cp = pltpu.make_async_remote_copy(
                    src_ref=src_for(i, peer), dst_ref=dsts[i].at[me],
                    send_sem=send_sems.at[i * (N_DEV - 1) + k - 1], recv_sem=recv_sems.at[i * (N_DEV - 1) + k - 1],
                    device_id=(px, py, pc), device_id_type=pl.DeviceIdType.MESH)
                cp.start()
                remote.append(cp)
        for cp in remote:
            cp.wait()
        for cp in local:
            cp.wait()

    any_spec = pl.BlockSpec(memory_space=pl.ANY)
    in_specs, args = [any_spec] * n, [a for a, _ in items]
    if after is not None:
        in_specs.append(any_spec)
        args.append(after)
        exchange = body
        body = lambda *refs: exchange(*refs[:n], *refs[n + 1:])
    return pl.pallas_call(
        body, name=name, out_shape=out_shape, in_specs=in_specs, out_specs=[any_spec] * n,
        scratch_shapes=[pltpu.SemaphoreType.DMA((n * (N_DEV - 1),)), pltpu.SemaphoreType.DMA((n * (N_DEV - 1),)),
                        pltpu.SemaphoreType.DMA((n,))],
    )(*args)


_HBM = pl.BlockSpec(memory_space=pltpu.HBM)
_SEM = pl.BlockSpec(memory_space=pltpu.SEMAPHORE)
_EFFECT = pltpu.SideEffectType.DATAFLOW_SIDE_EFFECTING


def _split_copies(items, srcs, lands, send_sems, recv_sems):
    x, y, c = lax.axis_index("x"), lax.axis_index("y"), lax.axis_index("c")
    me = 4 * x + 2 * y + c
    copies = []
    for k in range(1, N_DEV):
        px = jnp.bitwise_xor(x, (k >> 2) & 1)
        py = jnp.bitwise_xor(y, (k >> 1) & 1)
        pc = jnp.bitwise_xor(c, k & 1)
        peer = 4 * px + 2 * py + pc
        for i in range(len(items)):
            src = srcs[i] if items[i][1] == "gather" else srcs[i].at[peer]
            copies.append(pltpu.make_async_remote_copy(
                src_ref=src, dst_ref=lands[i].at[me],
                send_sem=send_sems.at[i * (N_DEV - 1) + k - 1], recv_sem=recv_sems.at[i * (N_DEV - 1) + k - 1],
                device_id=(px, py, pc), device_id_type=pl.DeviceIdType.MESH))
    return me, copies


def _exchange_start(items, name, after):
    n = len(items)
    n_sem = n * (N_DEV - 1)
    srcs, lands = [], []
    for a, mode in items:
        blk = a.shape if mode == "gather" else a.shape[1:]
        srcs.append(pltpu.with_memory_space_constraint(a, pltpu.HBM))
        lands.append(pltpu.with_memory_space_constraint(lax.empty((N_DEV,) + tuple(blk), a.dtype), pltpu.HBM))

    def body(*refs):
        src_refs, land_refs = refs[:n], refs[n:2 * n]
        send_sems, recv_sems = refs[2 * n + 1], refs[2 * n + 2]
        local_sems = refs[4 * n + 3]
        me, copies = _split_copies(items, src_refs, land_refs, send_sems, recv_sems)
        for i in range(n):
            own = src_refs[i] if items[i][1] == "gather" else src_refs[i].at[me]
            cp = pltpu.make_async_copy(own, land_refs[i].at[me], local_sems.at[i])
            cp.start()
            cp.wait()
        for cp in copies:
            cp.start()

    out_shape = [pltpu.SemaphoreType.DMA((n_sem,)), pltpu.SemaphoreType.DMA((n_sem,))]
    out_shape += [pltpu.HBM(a.shape, a.dtype) for a in srcs] + [pltpu.HBM(a.shape, a.dtype) for a in lands]
    outs = pl.pallas_call(
        body, name=name, out_shape=out_shape,
        in_specs=[_HBM] * (2 * n) + [pl.BlockSpec(memory_space=pl.ANY)],
        out_specs=[_SEM, _SEM] + [_HBM] * (2 * n),
        input_output_aliases={i: 2 + i for i in range(2 * n)},
        scratch_shapes=[pltpu.SemaphoreType.DMA((n,))],
        compiler_params=pltpu.CompilerParams(has_side_effects=_EFFECT),
    )(*srcs, *lands, after)
    handle = (items, name, outs[0], outs[1], outs[2:2 + n], outs[2 + n:2 + 2 * n])
    return handle, outs[2]


class _Sender:
    PIECE_ROWS = 352

    def __init__(self, items, chunks=None):
        self.items, self.n = items, len(items)
        self.chunks = chunks
        if chunks is None:
            block_rows = [a.shape[0] if mode == "gather" else a.shape[1] for a, mode in items]
            self.chunks = [r // self.PIECE_ROWS if r % self.PIECE_ROWS == 0 else 1 for r in block_rows]
        self.srcs, self.lands = [], []
        for a, mode in items:
            blk = a.shape if mode == "gather" else a.shape[1:]
            self.srcs.append(pltpu.with_memory_space_constraint(a, pltpu.HBM))
            self.lands.append(pltpu.with_memory_space_constraint(lax.empty((N_DEV,) + tuple(blk), a.dtype), pltpu.HBM))

    def issue(self, src_refs, land_refs, send_sems, recv_sems, local_sems, step, n_steps):
        x, y, c = lax.axis_index("x"), lax.axis_index("y"), lax.axis_index("c")
        me = 4 * x + 2 * y + c
        copies = []
        for ch in range(max(self.chunks)):
            for k in range(1, N_DEV):
                px = jnp.bitwise_xor(x, (k >> 2) & 1)
                py = jnp.bitwise_xor(y, (k >> 1) & 1)
                pc = jnp.bitwise_xor(c, k & 1)
                peer = 4 * px + 2 * py + pc
                for i, (_, mode) in enumerate(self.items):
                    if ch >= self.chunks[i]:
                        continue
                    n_rows = land_refs[i].shape[1] // self.chunks[i]
                    rows = pl.ds(ch * n_rows, n_rows)
                    src = src_refs[i].at[rows] if mode == "gather" else src_refs[i].at[peer].at[rows]
                    copies.append(pltpu.make_async_remote_copy(
                        src_ref=src, dst_ref=land_refs[i].at[me].at[rows],
                        send_sem=send_sems.at[i * (N_DEV - 1) + k - 1], recv_sem=recv_sems.at[i * (N_DEV - 1) + k - 1],
                        device_id=(px, py, pc), device_id_type=pl.DeviceIdType.MESH))
        own = [pltpu.make_async_copy(src_refs[i] if mode == "gather" else src_refs[i].at[me], land_refs[i].at[me],
                                     local_sems.at[i]) for i, (_, mode) in enumerate(self.items)]

        @pl.when(step == 0)
        def _():
            for cp in own:
                cp.start()

        for s in range(n_steps):
            group = [cp for j, cp in enumerate(copies) if (j * n_steps) // len(copies) == s]
            if group:
                @pl.when(step == s)
                def _(group=group):
                    for cp in group:
                        cp.start()

        @pl.when(step == n_steps - 1)
        def _():
            for cp in own:
                cp.wait()


def _host_call(body, name, grid, in_specs, args, out_shape, out_specs, scratch_shapes, after=None, sender=None):
    in_specs, args, out_shape, out_specs = list(in_specs), list(args), list(out_shape), list(out_specs)
    scratch_shapes = list(scratch_shapes)
    semantics = ("arbitrary",) * len(grid)
    body = _ordered_behind(body, in_specs, args, after)
    if sender is None:
        res = pl.pallas_call(body, name=name, grid=grid, in_specs=in_specs, out_specs=out_specs, out_shape=out_shape,
                             scratch_shapes=scratch_shapes, compiler_params=_params(semantics))(*args)
        return res, None
    n, n_in, n_out, n_scr = sender.n, len(in_specs), len(out_shape), len(scratch_shapes)
    n_sem = n * (N_DEV - 1)
    n_steps = 1
    for g in grid:
        n_steps *= g
    compute = body

    def body(*refs):
        ins, s_in = refs[:n_in], refs[n_in:n_in + 2 * n]
        o0 = n_in + 2 * n
        outs, s_out = refs[o0:o0 + n_out], refs[o0 + n_out:o0 + n_out + 2 + 2 * n]
        scr = refs[o0 + n_out + 2 + 2 * n:]
        compute(*ins, *outs, *scr[:n_scr])
        step = pl.program_id(0)
        for d in range(1, len(grid)):
            step = step * grid[d] + pl.program_id(d)
        sender.issue(s_in[:n], s_in[n:], s_out[0], s_out[1], scr[n_scr], step, n_steps)

    res = pl.pallas_call(
        body, name=name, grid=grid,
        in_specs=in_specs + [_HBM] * (2 * n), out_specs=out_specs + [_SEM, _SEM] + [_HBM] * (2 * n),
        out_shape=out_shape + [pltpu.SemaphoreType.DMA((n_sem,)), pltpu.SemaphoreType.DMA((n_sem,))]
        + [pltpu.HBM(a.shape, a.dtype) for a in sender.srcs] + [pltpu.HBM(a.shape, a.dtype) for a in sender.lands],
        input_output_aliases={n_in + j: n_out + 2 + j for j in range(2 * n)},
        scratch_shapes=scratch_shapes + [pltpu.SemaphoreType.DMA((n,))],
        compiler_params=pltpu.CompilerParams(dimension_semantics=semantics, vmem_limit_bytes=VMEM_LIMIT,
                                             has_side_effects=_EFFECT),
    )(*args, *sender.srcs, *sender.lands)
    handle = (sender.items, name, res[n_out], res[n_out + 1], res[n_out + 2:n_out + 2 + n],
              res[n_out + 2 + n:n_out + 2 + 2 * n])
    return res[:n_out], handle


def _exchange_wait(handle, after):
    items, name, send_sems, recv_sems, srcs, lands = handle
    n = len(items)

    def body(*refs):
        src_refs, land_refs = refs[:n], refs[n:2 * n]
        send_ref, recv_ref = refs[2 * n], refs[2 * n + 1]
        _, copies = _split_copies(items, src_refs, land_refs, send_ref, recv_ref)
        for cp in copies:
            cp.wait_send()
            cp.wait_recv()

    outs = pl.pallas_call(
        body, name=name + "_wait",
        out_shape=[pltpu.HBM(a.shape, a.dtype) for a in srcs] + [pltpu.HBM(a.shape, a.dtype) for a in lands],
        in_specs=[_HBM] * (2 * n) + [_SEM, _SEM, pl.BlockSpec(memory_space=pl.ANY)], out_specs=[_HBM] * (2 * n),
        input_output_aliases={i: i for i in range(2 * n)},
        compiler_params=pltpu.CompilerParams(has_side_effects=_EFFECT),
    )(*srcs, *lands, send_sems, recv_sems, after)
    return outs[n:]


def _mod_fwd(cvec, w_mod_l, b_mod_l):
    rows, cols = cvec.shape[0], w_mod_l.shape[1]

    def body(c_ref, w_ref, b_ref, o_ref, s_ref):
        cv = c_ref[...]
        s = cv * _sigmoid(cv)
        s_ref[...] = s
        o_ref[...] = _dot(s, w_ref[...]) + b_ref[...]

    return pl.pallas_call(
        body, name="mod_fwd",
        out_shape=(jax.ShapeDtypeStruct((rows, cols), F32), jax.ShapeDtypeStruct((rows, D), F32)),
        in_specs=[_full((rows, D)), _full((D, cols)), _full((1, cols))],
        out_specs=(_full((rows, cols)), _full((rows, D))), grid=(1,),
        compiler_params=_params(("arbitrary",)),
    )(cvec, w_mod_l, b_mod_l)


def _mod_bwd(svec, cvec, dmod_l, w_mod_l):
    rows, cols = dmod_l.shape

    def body(s_ref, c_ref, d_ref, w_ref, gw_ref, gc_ref):
        gw_ref[...] = _dot(s_ref[...], d_ref[...], "tn")
        cv = c_ref[...]
        sg = _sigmoid(cv)
        gc_ref[...] = _dot(d_ref[...], w_ref[...], "nt") * (sg * (1.0 + cv * (1.0 - sg)))

    return pl.pallas_call(
        body, name="mod_bwd",
        out_shape=(jax.ShapeDtypeStruct((D, cols), F32), jax.ShapeDtypeStruct((rows, D), F32)),
        in_specs=[_full((rows, D)), _full((rows, D)), _full((rows, cols)), _full((D, cols))],
        out_specs=(_full((D, cols)), _full((rows, D))), grid=(1,),
        compiler_params=_params(("arbitrary",)),
    )(svec, cvec, dmod_l, w_mod_l)


def _inproj(xt, modv, g, w_inT, n_cols, rows_per_example, name, after=None, sender=None):
    rows = xt.shape[0]
    tm = min(TOKEN_TILE, rows_per_example)
    per_b = rows_per_example // tm
    shared_mod = modv.shape[0] == 1

    def body(x_ref, mod_ref, g_ref, w_ref, p_ref, h_ref):
        x = x_ref[...]
        r = lax.rsqrt(jnp.mean(x * x, axis=-1, keepdims=True) + EPS)
        h = (x * r * g_ref[...]) * (1.0 + mod_ref[0, 1:2, :]) + mod_ref[0, 0:1, :]
        hb = h.astype(MXU_DTYPE)
        h_ref[...] = hb
        for j in range(n_cols // KW):
            p_ref[:, j * KW:(j + 1) * KW] = _dot(hb, w_ref[j * KW:(j + 1) * KW, :], "nt").astype(p_ref.dtype)

    mod_idx = (lambda i: (0, 0, 0)) if shared_mod else (lambda i: (i // per_b, 0, 0))
    in_specs = [pl.BlockSpec((tm, D), lambda i: (i, 0)), pl.BlockSpec((1, N_MOD, D), mod_idx), _full((1, D)),
                pl.BlockSpec((n_cols, D), lambda i: (0, 0), pipeline_mode=pl.Buffered(1))]
    (p, h), handle = _host_call(
        body, name, (rows // tm,), in_specs, [xt, modv, g, w_inT],
        [jax.ShapeDtypeStruct((rows, n_cols), MXU_DTYPE), jax.ShapeDtypeStruct((rows, D), MXU_DTYPE)],
        [pl.BlockSpec((tm, n_cols), lambda i: (i, 0)), pl.BlockSpec((tm, D), lambda i: (i, 0))], [],
        after=after, sender=sender)
    return p, h, handle


def _tri(reverse):
    row = lax.broadcasted_iota(jnp.int32, (CHUNK, CHUNK), 0)
    col = lax.broadcasted_iota(jnp.int32, (CHUNK, CHUNK), 1)
    return (col >= row) if reverse else (col <= row)


def _lower_bound(gam_ref, direction):
    return _sigmoid(gam_ref[direction:direction + 1, :] - gam_ref[2 + direction:3 + direction, :])


def _gate_prep(z, lb, tri_f):
    sg = _sigmoid(z)
    f = lb + (1.0 - lb) * sg
    g = jnp.log(f)
    b = _dotx(tri_f, g)
    bl = jnp.sum(g, axis=0, keepdims=True)
    return sg, f, 1.0 - f, b, bl


def _hgrn_fwd(p, gam, s0, rows_per_example, with_out, name, sender=None):
    rows = p.shape[0]
    nb_ex = rows // rows_per_example
    rb = min(TOKEN_TILE, rows_per_example)
    cpb = rb // CHUNK
    nb = rows_per_example // rb
    n_chunks = rows // CHUNK
    has_s0 = s0 is not None

    def body(*refs):
        it = iter(refs)
        gam_ref = next(it)
        zf_ref, vf_ref = next(it), next(it)
        qf_ref = next(it) if with_out else None
        zb_ref, vb_ref = next(it), next(it)
        qb_ref = next(it) if with_out else None
        s0_ref = next(it) if has_s0 else None
        if with_out:
            of_ref, ob_ref = next(it), next(it)
        stash_f, stash_b, fin_ref = next(it), next(it), next(it)
        st_ref = next(it)
        i = pl.program_id(1)

        @pl.when(i == 0)
        def _():
            if has_s0:
                st_ref[...] = s0_ref[:, 0]
            else:
                st_ref[...] = jnp.zeros_like(st_ref)

        for direction, (z_ref, v_ref, q_ref, stash) in enumerate(
                ((zf_ref, vf_ref, qf_ref, stash_f), (zb_ref, vb_ref, qb_ref, stash_b))):
            reverse = direction == 1
            tri = _tri(reverse)
            tri_f = tri.astype(F32)
            lb = _lower_bound(gam_ref, direction)
            order = range(cpb - 1, -1, -1) if reverse else range(cpb)
            for j in order:
                rs = slice(j * CHUNK, (j + 1) * CHUNK)
                z = z_ref[rs, :].astype(F32)
                v = v_ref[rs, :].astype(F32)
                _, _, k, b, bl = _gate_prep(z, lb, tri_f)
                mid = 0.5 * bl
                kd = k * jnp.exp(bl - b)
                a = jnp.exp(bl)
                if with_out:
                    q = q_ref[rs, :].astype(F32)
                    qi = q * jnp.exp(b - mid)
                    ki = k * jnp.exp(mid - b)
                    qe = q * jnp.exp(b)
                for h in range(HEADS):
                    hs = slice(h * DK, (h + 1) * DK)
                    st = st_ref[direction, h]
                    stash[j, h] = st.astype(stash.dtype)
                    if with_out:
                        sc = jnp.where(tri, _dot(qi[:, hs], ki[:, hs], "nt"), 0.0)
                        o = _dot(sc, v[:, hs]) + _dot(qe[:, hs], st, "nt")
                        (ob_ref if reverse else of_ref)[rs, hs] = o
                    st_ref[direction, h] = st * a[:, hs] + _dot(v[:, hs], kd[:, hs], "tn")

        @pl.when(i == nb - 1)
        def _():
            fin_ref[:, 0] = st_ref[...]

    up = lambda b, i: b * nb + i
    down = lambda b, i: b * nb + nb - 1 - i
    col = lambda rowf, c: pl.BlockSpec((rb, KW), lambda b, i: (rowf(b, i), c))
    in_specs = [_full((4, KW)), col(up, 0), col(up, 2)] + ([col(up, 3)] if with_out else [])
    in_specs += [col(down, 1), col(down, 2)] + ([col(down, 3)] if with_out else [])
    args = [gam, p, p] + ([p] if with_out else []) + [p, p] + ([p] if with_out else [])
    if has_s0:
        in_specs.append(pl.BlockSpec((2, 1, HEADS, DK, DK), lambda b, i: (0, b, 0, 0, 0)))
        args.append(s0)
    out_shape, out_specs = [], []
    if with_out:
        out_shape += [jax.ShapeDtypeStruct((rows, KW), F32)] * 2
        out_specs += [pl.BlockSpec((rb, KW), lambda b, i: (up(b, i), 0)),
                      pl.BlockSpec((rb, KW), lambda b, i: (down(b, i), 0))]
    out_shape += [jax.ShapeDtypeStruct((n_chunks, HEADS, DK, DK), MXU_DTYPE)] * 2
    out_specs += [pl.BlockSpec((cpb, HEADS, DK, DK), lambda b, i: (up(b, i), 0, 0, 0)),
                  pl.BlockSpec((cpb, HEADS, DK, DK), lambda b, i: (down(b, i), 0, 0, 0))]
    out_shape.append(jax.ShapeDtypeStruct((2, nb_ex, HEADS, DK, DK), F32))
    out_specs.append(pl.BlockSpec((2, 1, HEADS, DK, DK), lambda b, i: (0, b, 0, 0, 0)))
    res, handle = _host_call(body, name, (nb_ex, nb), in_specs, args, out_shape, out_specs,
                             [pltpu.VMEM((2, HEADS, DK, DK), F32)], sender=sender)
    return (*res, handle)


def _hgrn_bwd(p, gam, do, stash_f, stash_b, ds_end, rows_per_example, with_out, name, after=None, sender=None):
    rows = p.shape[0]
    nb_ex = rows // rows_per_example
    rb = min(TOKEN_TILE, rows_per_example)
    cpb = rb // CHUNK
    nb = rows_per_example // rb
    has_end = ds_end is not None

    def body(*refs):
        it = iter(refs)
        gam_ref = next(it)
        ins = []
        for _ in range(2):
            z_ref, v_ref = next(it), next(it)
            q_ref = next(it) if with_out else None
            do_ref = next(it) if with_out else None
            ins.append((z_ref, v_ref, q_ref, do_ref, next(it)))
        end_ref = next(it) if has_end else None
        outs = []
        for _ in range(2):
            dz_ref, dv_ref = next(it), next(it)
            dq_ref = next(it) if with_out else None
            outs.append((dz_ref, dv_ref, dq_ref))
        dlb_ref, ds0_ref = next(it), next(it)
        dst_ref = next(it)
        b_id, i = pl.program_id(0), pl.program_id(1)

        @pl.when(i == 0)
        def _():
            if has_end:
                dst_ref[...] = end_ref[:, 0]
            else:
                dst_ref[...] = jnp.zeros_like(dst_ref)

        @pl.when((i == 0) & (b_id == 0))
        def _():
            dlb_ref[...] = jnp.zeros_like(dlb_ref)

        for direction in range(2):
            z_ref, v_ref, q_ref, do_ref, stash = ins[direction]
            dz_ref, dv_ref, dq_ref = outs[direction]
            reverse = direction == 1
            tri = _tri(reverse)
            tri_f = tri.astype(F32)
            lb = _lower_bound(gam_ref, direction)
            order = range(cpb) if reverse else range(cpb - 1, -1, -1)
            dlb_acc = jnp.zeros((1, KW), F32)
            for j in order:
                rs = slice(j * CHUNK, (j + 1) * CHUNK)
                z = z_ref[rs, :].astype(F32)
                v = v_ref[rs, :].astype(F32)
                sg, f, k, b, bl = _gate_prep(z, lb, tri_f)
                mid = 0.5 * bl
                e3 = jnp.exp(bl - b)
                kd = k * e3
                a = jnp.exp(bl)
                if with_out:
                    q = q_ref[rs, :].astype(F32)
                    dout = do_ref[rs, :].astype(F32)
                    e1, e2, e4 = jnp.exp(b - mid), jnp.exp(mid - b), jnp.exp(b)
                    qi, ki, qe = q * e1, k * e2, q * e4
                dkd_p, dv_p, da_p, dqi_p, dki_p, dqe_p = [], [], [], [], [], []
                for h in range(HEADS):
                    hs = slice(h * DK, (h + 1) * DK)
                    st_in = stash[j, h]
                    dst = dst_ref[direction, h]
                    dkd_p.append(_dot(v[:, hs], dst))
                    dvh = _dot(kd[:, hs], dst, "nt")
                    da_p.append(jnp.sum(dst * st_in.astype(F32), axis=0, keepdims=True))
                    new_dst = dst * a[:, hs]
                    if with_out:
                        sc = jnp.where(tri, _dot(qi[:, hs], ki[:, hs], "nt"), 0.0)
                        dsc = jnp.where(tri, _dot(dout[:, hs], v[:, hs], "nt"), 0.0)
                        dqi_p.append(_dot(dsc, ki[:, hs]))
                        dki_p.append(_dot(dsc, qi[:, hs], "tn"))
                        dqe_p.append(_dot(dout[:, hs], st_in))
                        dvh = dvh + _dot(sc, dout[:, hs], "tn")
                        new_dst = new_dst + _dot(dout[:, hs], qe[:, hs], "tn")
                    dv_p.append(dvh)
                    dst_ref[direction, h] = new_dst
                cat = lambda parts: jnp.concatenate(parts, axis=1)
                dkd, da = cat(dkd_p), cat(da_p)
                dv_ref[rs, :] = cat(dv_p)
                t_kd = dkd * kd
                dk = dkd * e3
                db = -t_kd
                dbl = jnp.sum(t_kd, axis=0, keepdims=True) + da * a
                if with_out:
                    dqi, dki, dqe = cat(dqi_p), cat(dki_p), cat(dqe_p)
                    dq_ref[rs, :] = dqi * e1 + dqe * e4
                    dk = dk + dki * e2
                    t_qi, t_ki, t_qe = dqi * qi, dki * ki, dqe * qe
                    db = db + t_qi - t_ki + t_qe
                    dbl = dbl + 0.5 * jnp.sum(t_ki - t_qi, axis=0, keepdims=True)
                dg = _dotx(tri_f, db, "tn") + dbl
                df = dg / f - dk
                dz_ref[rs, :] = df * (1.0 - lb) * sg * (1.0 - sg)
                dlb_acc = dlb_acc + jnp.sum(df * (1.0 - sg), axis=0, keepdims=True)
            dlb_ref[direction:direction + 1, :] += dlb_acc

        @pl.when(i == nb - 1)
        def _():
            ds0_ref[:, 0] = dst_ref[...]

    rows_of = (lambda b, i: b * nb + nb - 1 - i, lambda b, i: b * nb + i)
    in_specs, args = [_full((4, KW))], [gam]
    for direction in range(2):
        rf = rows_of[direction]
        col = lambda c, rf=rf: pl.BlockSpec((rb, KW), lambda b, i: (rf(b, i), c))
        in_specs += [col(direction), col(2)]
        args += [p, p]
        if with_out:
            in_specs += [col(3), col(0)]
            args += [p, do]
        in_specs.append(pl.BlockSpec((cpb, HEADS, DK, DK), lambda b, i, rf=rf: (rf(b, i), 0, 0, 0)))
        args.append((stash_f, stash_b)[direction])
    if has_end:
        in_specs.append(pl.BlockSpec((2, 1, HEADS, DK, DK), lambda b, i: (0, b, 0, 0, 0)))
        args.append(ds_end)
    out_shape, out_specs = [], []
    for direction in range(2):
        rf = rows_of[direction]
        n_out = 3 if with_out else 2
        out_shape += [jax.ShapeDtypeStruct((rows, KW), F32)] * n_out
        out_specs += [pl.BlockSpec((rb, KW), lambda b, i, rf=rf: (rf(b, i), 0))] * n_out
    out_shape += [jax.ShapeDtypeStruct((2, KW), F32), jax.ShapeDtypeStruct((2, nb_ex, HEADS, DK, DK), F32)]
    out_specs += [_full((2, KW)), pl.BlockSpec((2, 1, HEADS, DK, DK), lambda b, i: (0, b, 0, 0, 0))]
    res, handle = _host_call(body, name, (nb_ex, nb), in_specs, args, out_shape, out_specs,
                             [pltpu.VMEM((2, HEADS, DK, DK), F32)], after=after, sender=sender)
    return (*res, handle)


def _tail_forward(osum, og, u, v, ga, gb, gna, ln_g, ln_b, ws_ref, bs_ref, wpaT_ref, wpbT_ref):
    tm = osum.shape[0]
    gna4 = jnp.concatenate([gna] * HEADS, axis=1)
    r_parts = []
    for h in range(HEADS):
        oh = osum[:, h * DK:(h + 1) * DK]
        r_parts.append(jnp.broadcast_to(lax.rsqrt(jnp.mean(oh * oh, axis=-1, keepdims=True) + EPS), (tm, DK)))
    r = jnp.concatenate(r_parts, axis=1)
    on = osum * r
    sg_og = _sigmoid(og)
    silu_og = og * sg_og
    oan = on * gna4
    oa = oan * silu_og
    ug, tu = _gelu(u)
    vg, tv = _gelu(v)
    mu = jnp.mean(vg, axis=-1, keepdims=True)
    vc = vg - mu
    rstd = lax.rsqrt(jnp.mean(vc * vc, axis=-1, keepdims=True) + EPS)
    vhat = vc * rstd
    vln = vhat * ln_g + ln_b
    blocks = []
    for n in range(tm // SGU_BLOCK):
        rs = slice(n * SGU_BLOCK, (n + 1) * SGU_BLOCK)
        blocks.append(jnp.concatenate(
            [_dot(ws_ref[g], vln[rs, g * DK:(g + 1) * DK]) + bs_ref[g] for g in range(GROUPS)], axis=1))
    mixed = jnp.concatenate(blocks, axis=0) if len(blocks) > 1 else blocks[0]
    obm = ug * mixed
    pa = _dot(oa, wpaT_ref[...], "nt")
    pb = _dot(obm, wpbT_ref[...], "nt")
    sga, sgb = _sigmoid(ga), _sigmoid(gb)
    merged = sga * pa + sgb * pb
    return dict(r=r, on=on, sg_og=sg_og, silu_og=silu_og, oan=oan, oa=oa, ug=ug, tu=tu, tv=tv, rstd=rstd, vhat=vhat,
                vln=vln, mixed=mixed, obm=obm, pa=pa, pb=pb, sga=sga, sgb=sgb, merged=merged, gna4=gna4)


def _tail_in_specs(tm):
    tile = lambda c: pl.BlockSpec((tm, KW), lambda i: (i, c))
    return [tile(c) for c in range(4, 11)]


def _tail_weight_specs():
    return [_full((1, DK)), _full((1, KW)), _full((1, KW)), _full((GROUPS, SGU_BLOCK, SGU_BLOCK)),
            _full((GROUPS, SGU_BLOCK, 1)), _full((D, KW), single=True), _full((D, KW), single=True),
            _full((D, D), single=True)]


def _read_tail_inputs(of_ref, ob_ref, pcols):
    osum = of_ref[...] + ob_ref[...]
    og, u, v = (pcols[j][...].astype(F32) for j in range(3))
    ga = jnp.concatenate([pcols[3][...], pcols[4][...]], axis=1).astype(F32)
    gb = jnp.concatenate([pcols[5][...], pcols[6][...]], axis=1).astype(F32)
    return osum, og, u, v, ga, gb


def _tail_fwd(p, o_up, o_down, xt, modv, gna, ln_g, ln_b, w_s, b_s, w_paT, w_pbT, w_o, rows_per_example):
    rows = xt.shape[0]
    tm = min(TOKEN_TILE, rows_per_example)
    per_b = rows_per_example // tm

    def body(of_ref, ob_ref, *rest):
        pcols = rest[:7]
        (x_ref, mod_ref, gna_ref, lng_ref, lnb_ref, ws_ref, bs_ref, wpaT_ref, wpbT_ref, wo_ref,
         x1_ref, mix_ref, merged_ref, oa_ref, obm_ref) = rest[7:]
        t = _tail_forward(*_read_tail_inputs(of_ref, ob_ref, pcols), gna_ref[...], lng_ref[...], lnb_ref[...],
                          ws_ref, bs_ref, wpaT_ref, wpbT_ref)
        mix = _dot(t["merged"], wo_ref[...])
        x1_ref[...] = x_ref[...] + mod_ref[0, 2:3, :] * mix
        mix_ref[...] = mix.astype(mix_ref.dtype)
        merged_ref[...] = t["merged"].astype(merged_ref.dtype)
        oa_ref[...] = t["oa"].astype(oa_ref.dtype)
        obm_ref[...] = t["obm"].astype(obm_ref.dtype)

    row = lambda w: pl.BlockSpec((tm, w), lambda i: (i, 0))
    in_specs = [row(KW), row(KW)] + _tail_in_specs(tm) + [row(D), pl.BlockSpec((1, N_MOD, D), lambda i: (i // per_b, 0, 0))]
    in_specs += _tail_weight_specs()
    return pl.pallas_call(
        body, name="tail_fwd", grid=(rows // tm,),
        out_shape=(jax.ShapeDtypeStruct((rows, D), F32), jax.ShapeDtypeStruct((rows, D), MXU_DTYPE),
                   jax.ShapeDtypeStruct((rows, D), MXU_DTYPE), jax.ShapeDtypeStruct((rows, KW), MXU_DTYPE),
                   jax.ShapeDtypeStruct((rows, KW), MXU_DTYPE)),
        in_specs=in_specs, out_specs=(row(D), row(D), row(D), row(KW), row(KW)),
        compiler_params=_params(("arbitrary",)),
    )(o_up, o_down, *([p] * 7), xt, modv, gna, ln_g, ln_b, w_s, b_s, w_paT, w_pbT, w_o)


def _tail_bwd(p, o_up, o_down, dx1, mix, modv, gna, ln_g, ln_b, w_s, b_s, w_paT, w_pbT, w_o, rows_per_example,
              after=None, sender=None):
    rows = dx1.shape[0]
    nb_ex = rows // rows_per_example
    tm = min(TOKEN_TILE, rows_per_example)
    per_b = rows_per_example // tm

    def body(of_ref, ob_ref, *rest):
        pcols = rest[:7]
        (dx1_ref, mix_ref, mod_ref, gna_ref, lng_ref, lnb_ref, ws_ref, bs_ref, wpaT_ref, wpbT_ref, wo_ref,
         dpt_ref, do_ref, dmix_ref, dpa_ref, dpb_ref, dmod_ref, small_ref, dws_ref, dbs_ref) = rest[7:]
        i = pl.program_id(0)

        @pl.when(i == 0)
        def _():
            small_ref[...] = jnp.zeros_like(small_ref)
            dws_ref[...] = jnp.zeros_like(dws_ref)
            dbs_ref[...] = jnp.zeros_like(dbs_ref)

        @pl.when(i % per_b == 0)
        def _():
            dmod_ref[...] = jnp.zeros_like(dmod_ref)

        osum, og, u, v, ga, gb = _read_tail_inputs(of_ref, ob_ref, pcols)
        ln_g = lng_ref[...]
        t = _tail_forward(osum, og, u, v, ga, gb, gna_ref[...], ln_g, lnb_ref[...], ws_ref, bs_ref, wpaT_ref, wpbT_ref)
        dx1v = dx1_ref[...]
        dmod_ref[0, 2:3, :] += jnp.sum(dx1v * mix_ref[...].astype(F32), axis=0, keepdims=True)
        dmix = dx1v * mod_ref[0, 2:3, :]
        dmix_ref[...] = dmix.astype(dmix_ref.dtype)
        dmerged = _dot(dmix, wo_ref[...], "nt")
        sga, sgb = t["sga"], t["sgb"]
        dpa = dmerged * sga
        dpb = dmerged * sgb
        dpa_ref[...] = dpa.astype(dpa_ref.dtype)
        dpb_ref[...] = dpb.astype(dpb_ref.dtype)
        dga = dmerged * t["pa"] * sga * (1.0 - sga)
        dgb = dmerged * t["pb"] * sgb * (1.0 - sgb)
        doa = _dot(dpa, wpaT_ref[...])
        dobm = _dot(dpb, wpbT_ref[...])
        dug = dobm * t["mixed"]
        dmixed = dobm * t["ug"]
        du = dug * _gelu_grad(u, t["tu"])
        dvln_blocks = []
        for n in range(tm // SGU_BLOCK):
            rs = slice(n * SGU_BLOCK, (n + 1) * SGU_BLOCK)
            parts = []
            for g in range(GROUPS):
                gs = slice(g * DK, (g + 1) * DK)
                dm = dmixed[rs, gs]
                parts.append(_dot(ws_ref[g], dm, "tn"))
                dws_ref[g] += _dot(dm, t["vln"][rs, gs], "nt")
                dbs_ref[g] += jnp.sum(dm, axis=1, keepdims=True)
            dvln_blocks.append(jnp.concatenate(parts, axis=1))
        dvln = jnp.concatenate(dvln_blocks, axis=0) if len(dvln_blocks) > 1 else dvln_blocks[0]
        vhat = t["vhat"]
        small_ref[1:2, 0:KW] += jnp.sum(dvln * vhat, axis=0, keepdims=True)
        small_ref[2:3, 0:KW] += jnp.sum(dvln, axis=0, keepdims=True)
        dvhat = dvln * ln_g
        dvg = t["rstd"] * (dvhat - jnp.mean(dvhat, axis=-1, keepdims=True)
                           - vhat * jnp.mean(dvhat * vhat, axis=-1, keepdims=True))
        dv = dvg * _gelu_grad(v, t["tv"])
        sg_og = t["sg_og"]
        doan = doa * t["silu_og"]
        dog = doa * t["oan"] * (sg_og * (1.0 + og * (1.0 - sg_og)))
        prod = doan * t["on"]
        dgna = jnp.zeros((1, DK), F32)
        for h in range(HEADS):
            dgna = dgna + jnp.sum(prod[:, h * DK:(h + 1) * DK], axis=0, keepdims=True)
        small_ref[0:1, 0:DK] += dgna
        don = doan * t["gna4"]
        dot_parts = []
        for h in range(HEADS):
            hs = slice(h * DK, (h + 1) * DK)
            m = jnp.mean(don[:, hs] * t["on"][:, hs], axis=-1, keepdims=True)
            dot_parts.append(t["r"][:, hs] * (don[:, hs] - t["on"][:, hs] * m))
        do_ref[...] = jnp.concatenate(dot_parts, axis=1).astype(do_ref.dtype)
        for j, val in enumerate((dog, du, dv)):
            dpt_ref[:, j * KW:(j + 1) * KW] = val.astype(dpt_ref.dtype)
        dpt_ref[:, 3 * KW:3 * KW + D] = dga.astype(dpt_ref.dtype)
        dpt_ref[:, 3 * KW + D:] = dgb.astype(dpt_ref.dtype)

    row = lambda w: pl.BlockSpec((tm, w), lambda i: (i, 0))
    in_specs = [row(KW), row(KW)] + _tail_in_specs(tm) + [row(D), row(D), pl.BlockSpec((1, N_MOD, D), lambda i: (i // per_b, 0, 0))]
    in_specs += _tail_weight_specs()
    args = [o_up, o_down, *([p] * 7), dx1, mix, modv, gna, ln_g, ln_b, w_s, b_s, w_paT, w_pbT, w_o]
    cd = MXU_DTYPE
    res, handle = _host_call(
        body, "tail_bwd", (rows // tm,), in_specs, args,
        [jax.ShapeDtypeStruct((rows, TAIL_COLS), cd), jax.ShapeDtypeStruct((rows, KW), cd),
         jax.ShapeDtypeStruct((rows, D), cd), jax.ShapeDtypeStruct((rows, D), cd),
         jax.ShapeDtypeStruct((rows, D), cd), jax.ShapeDtypeStruct((nb_ex, 8, D), F32),
         jax.ShapeDtypeStruct((8, D), F32), jax.ShapeDtypeStruct((GROUPS, SGU_BLOCK, SGU_BLOCK), F32),
         jax.ShapeDtypeStruct((GROUPS, SGU_BLOCK, 1), F32)],
        [row(TAIL_COLS), row(KW), row(D), row(D), row(D),
         pl.BlockSpec((1, 8, D), lambda i: (i // per_b, 0, 0)), _full((8, D)),
         _full((GROUPS, SGU_BLOCK, SGU_BLOCK)), _full((GROUPS, SGU_BLOCK, 1))], [],
        after=after, sender=sender)
    return (*res, handle)


def _ffn(x1, target, modv, g_ffn, g_final, w_upT, w_down, rows_per_example):
    rows = x1.shape[0]
    nb_ex = rows // rows_per_example
    tm = min(TOKEN_TILE, rows_per_example)
    per_b = rows_per_example // tm
    n_ff = D_FF // FF_CHUNK

    def body(x1_ref, tgt_ref, mod_ref, gffn_ref, gfin_ref, wup_ref, wdn_ref,
             dx1_ref, h2_ref, dffn_ref, act_ref, dup_ref, dmod_ref, small_ref, a_scr, b_scr):
        i = pl.program_id(0)

        @pl.when(i == 0)
        def _():
            small_ref[...] = jnp.zeros_like(small_ref)

        @pl.when(i % per_b == 0)
        def _():
            dmod_ref[...] = jnp.zeros_like(dmod_ref)

        x1v = x1_ref[...]
        g2 = gffn_ref[...]
        m3, m4, m5 = mod_ref[0, 3:4, :], mod_ref[0, 4:5, :], mod_ref[0, 5:6, :]
        r2 = lax.rsqrt(jnp.mean(x1v * x1v, axis=-1, keepdims=True) + EPS)
        xn2 = x1v * r2
        h2 = (xn2 * g2) * (1.0 + m4) + m3
        h2b = h2.astype(MXU_DTYPE)
        h2_ref[...] = h2b
        for j in range(n_ff):
            cs = slice(j * FF_CHUNK, (j + 1) * FF_CHUNK)
            a = _dot(h2b, wup_ref[j * FF_CHUNK:(j + 1) * FF_CHUNK, :], "nt")
            bgate = _dot(h2b, wup_ref[D_FF + j * FF_CHUNK:D_FF + (j + 1) * FF_CHUNK, :], "nt")
            a_scr[:, cs] = a
            b_scr[:, cs] = bgate
            act_ref[:, cs] = (a * _sigmoid(a) * bgate).astype(MXU_DTYPE)
        ffn = _dot(act_ref[...], wdn_ref[...])
        x2 = x1v + m5 * ffn
        r3 = lax.rsqrt(jnp.mean(x2 * x2, axis=-1, keepdims=True) + EPS)
        xn3 = x2 * r3
        gf = gfin_ref[...]
        err = xn3 * gf - tgt_ref[...]
        loss = 0.5 * jnp.sum(jnp.mean(err * err, axis=-1, keepdims=True), axis=0, keepdims=True)
        small_ref[2:3, :] += jnp.broadcast_to(loss, (1, D))
        dy = err * (1.0 / D)
        small_ref[1:2, :] += jnp.sum(dy * xn3, axis=0, keepdims=True)
        dxn3 = dy * gf
        dx2 = r3 * (dxn3 - xn3 * jnp.mean(dxn3 * xn3, axis=-1, keepdims=True))
        dmod_ref[0, 5:6, :] += jnp.sum(dx2 * ffn, axis=0, keepdims=True)
        dffn = (dx2 * m5).astype(MXU_DTYPE)
        dffn_ref[...] = dffn
        for j in range(n_ff):
            cs = slice(j * FF_CHUNK, (j + 1) * FF_CHUNK)
            dact = _dot(dffn, wdn_ref[cs, :], "nt")
            a, bgate = a_scr[:, cs], b_scr[:, cs]
            s = _sigmoid(a)
            dup_ref[:, cs] = (dact * bgate * (s * (1.0 + a * (1.0 - s)))).astype(MXU_DTYPE)
            dup_ref[:, D_FF + j * FF_CHUNK:D_FF + (j + 1) * FF_CHUNK] = (dact * a * s).astype(MXU_DTYPE)
        dh2 = _dot(dup_ref[...], wup_ref[...])
        dmod_ref[0, 3:4, :] += jnp.sum(dh2, axis=0, keepdims=True)
        dmod_ref[0, 4:5, :] += jnp.sum(dh2 * xn2 * g2, axis=0, keepdims=True)
        small_ref[0:1, :] += jnp.sum(dh2 * (1.0 + m4) * xn2, axis=0, keepdims=True)
        dxn2 = dh2 * g2 * (1.0 + m4)
        dx1_ref[...] = dx2 + r2 * (dxn2 - xn2 * jnp.mean(dxn2 * xn2, axis=-1, keepdims=True))

    row = lambda w: pl.BlockSpec((tm, w), lambda i: (i, 0))
    cd = MXU_DTYPE
    return pl.pallas_call(
        body, name="ffn_fwd_bwd", grid=(rows // tm,),
        out_shape=(jax.ShapeDtypeStruct((rows, D), F32), jax.ShapeDtypeStruct((rows, D), cd),
                   jax.ShapeDtypeStruct((rows, D), cd), jax.ShapeDtypeStruct((rows, D_FF), cd),
                   jax.ShapeDtypeStruct((rows, 2 * D_FF), cd), jax.ShapeDtypeStruct((nb_ex, 8, D), F32),
                   jax.ShapeDtypeStruct((8, D), F32)),
        in_specs=[row(D), row(D), pl.BlockSpec((1, N_MOD, D), lambda i: (i // per_b, 0, 0)), _full((1, D)), _full((1, D)),
                  _full((2 * D_FF, D), single=True), _full((D_FF, D), single=True)],
        out_specs=(row(D), row(D), row(D), row(D_FF), row(2 * D_FF),
                   pl.BlockSpec((1, 8, D), lambda i: (i // per_b, 0, 0)), _full((8, D))),
        scratch_shapes=[pltpu.VMEM((tm, D_FF), F32), pltpu.VMEM((tm, D_FF), F32)],
        compiler_params=_params(("arbitrary",)),
    )(x1, target, modv, g_ffn, g_final, w_upT, w_down)


def _inproj_bwd(pieces, dpt, xt, dx1, modv, g, w_inT, rows_per_example, name):
    rows = xt.shape[0]
    latent = dx1 is not None
    n_cols = IN_COLS if latent else CTX_COLS
    tm = min(TOKEN_TILE, rows_per_example)
    per_b = rows_per_example // tm
    n_mod_blocks = rows // rows_per_example if latent else 1
    n_pieces = len(pieces)

    def body(*refs):
        it = iter(refs)
        pc = [next(it) for _ in range(n_pieces)]
        dpt_ref = next(it) if latent else None
        x_ref = next(it)
        dx1_ref = next(it) if latent else None
        mod_ref, g_ref, w_ref = next(it), next(it), next(it)
        gx_ref = next(it) if latent else None
        dp_ref, dmod_ref, small_ref = next(it), next(it), next(it)
        i = pl.program_id(0)

        @pl.when(i == 0)
        def _():
            small_ref[...] = jnp.zeros_like(small_ref)

        @pl.when((i % per_b == 0) if latent else (i == 0))
        def _():
            dmod_ref[...] = jnp.zeros_like(dmod_ref)

        cols = [pc[0][...], pc[1][...], pc[2][...] + pc[3][...]]
        if latent:
            cols.append(pc[4][...] + pc[5][...])
        for j, val in enumerate(cols):
            dp_ref[:, j * KW:(j + 1) * KW] = val.astype(MXU_DTYPE)
        if latent:
            dp_ref[:, 4 * KW:] = dpt_ref[...]
        dh = _dot(dp_ref[...], w_ref[...])
        x = x_ref[...]
        gv = g_ref[...]
        m1 = mod_ref[0, 1:2, :]
        r = lax.rsqrt(jnp.mean(x * x, axis=-1, keepdims=True) + EPS)
        xn = x * r
        dmod_ref[0, 0:1, :] += jnp.sum(dh, axis=0, keepdims=True)
        dmod_ref[0, 1:2, :] += jnp.sum(dh * xn * gv, axis=0, keepdims=True)
        small_ref[0:1, :] += jnp.sum(dh * (1.0 + m1) * xn, axis=0, keepdims=True)
        if latent:
            dxn = dh * gv * (1.0 + m1)
            gx_ref[...] = dx1_ref[...] + r * (dxn - xn * jnp.mean(dxn * xn, axis=-1, keepdims=True))

    row = lambda w: pl.BlockSpec((tm, w), lambda i: (i, 0))
    mod_idx = (lambda i: (i // per_b, 0, 0)) if latent else (lambda i: (0, 0, 0))
    in_specs = [row(KW)] * n_pieces + ([row(TAIL_COLS)] if latent else []) + [row(D)] + ([row(D)] if latent else [])
    in_specs += [pl.BlockSpec((1, N_MOD, D), mod_idx), _full((1, D)),
                 pl.BlockSpec((n_cols, D), lambda i: (0, 0), pipeline_mode=pl.Buffered(1))]
    args = list(pieces) + ([dpt] if latent else []) + [xt] + ([dx1] if latent else []) + [modv, g, w_inT]
    out_shape = ([jax.ShapeDtypeStruct((rows, D), F32)] if latent else []) + [
        jax.ShapeDtypeStruct((rows, n_cols), MXU_DTYPE), jax.ShapeDtypeStruct((n_mod_blocks, 8, D), F32),
        jax.ShapeDtypeStruct((8, D), F32)]
    out_specs = ([row(D)] if latent else []) + [row(n_cols), pl.BlockSpec((1, 8, D), mod_idx), _full((8, D))]
    return pl.pallas_call(
        body, name=name, grid=(rows // tm,), out_shape=out_shape, in_specs=in_specs, out_specs=out_specs,
        compiler_params=_params(("arbitrary",)),
    )(*args)


def _grad_matmul(a, b, name, init=None, tn=512, sender=None):
    rows, n = a.shape
    k = b.shape[1]
    tn = min(tn, n)
    has_init = init is not None
    init_blocks = init.shape[0] // tn if has_init else 0

    def body(*refs):
        if has_init:
            a_ref, b_ref, init_ref, o_ref = refs
        else:
            a_ref, b_ref, o_ref = refs
        g = _dot(a_ref[...], b_ref[...], "tn")
        if has_init:
            g = g + jnp.where(pl.program_id(0) < init_blocks, init_ref[...].astype(F32), 0.0)
        o_ref[...] = g.astype(o_ref.dtype)

    in_specs = [pl.BlockSpec((rows, tn), lambda i: (0, i)), _full((rows, k), single=True)]
    args = [a, b]
    if has_init:
        in_specs.append(pl.BlockSpec((tn, k), lambda i: (jnp.minimum(i, init_blocks - 1), 0)))
        args.append(init)
    (out,), handle = _host_call(
        body, name, (n // tn,), in_specs, args, [jax.ShapeDtypeStruct((n, k), PAYLOAD_DTYPE)],
        [pl.BlockSpec((tn, k), lambda i: (i, 0))], [], sender=sender)
    return out, handle


def _row_tile(rows, limit=256):
    if rows <= limit:
        return rows
    for t in range(limit, 7, -8):
        if rows % t == 0:
            return t
    return rows


def _sum8(stack, name):
    _, rows, cols = stack.shape
    tr = _row_tile(rows)

    def body(s_ref, o_ref):
        acc = s_ref[0].astype(F32)
        for j in range(1, N_DEV):
            acc = acc + s_ref[j].astype(F32)
        o_ref[...] = acc

    return pl.pallas_call(
        body, name=name, grid=(rows // tr,), out_shape=jax.ShapeDtypeStruct((rows, cols), F32),
        in_specs=[pl.BlockSpec((N_DEV, tr, cols), lambda i: (0, i, 0))],
        out_specs=pl.BlockSpec((tr, cols), lambda i: (i, 0)),
        compiler_params=_params(("arbitrary",)),
    )(stack)


def _small_reduce(stack, gam, nb_ex):
    def body(s_ref, gam_ref, o_ref, bm_ref):
        acc = s_ref[0]
        for j in range(1, N_DEV):
            acc = acc + s_ref[j]
        o_ref[...] = acc
        bm = acc[8:8 + N_MOD, :]
        for e in range(nb_ex):
            bm = bm + acc[16 + e * N_MOD:16 + (e + 1) * N_MOD, :]
        lb = jnp.concatenate([_lower_bound(gam_ref, 0), _lower_bound(gam_ref, 1)], axis=1)
        dgam = acc[7:8, :] * lb * (1.0 - lb)
        bm_ref[...] = jnp.concatenate([bm, dgam, -dgam], axis=0)

    return pl.pallas_call(
        body, name="small_reduce", grid=(1,),
        out_shape=(jax.ShapeDtypeStruct((SMALL_ROWS, D), F32), jax.ShapeDtypeStruct((8, D), F32)),
        in_specs=[_full((N_DEV, SMALL_ROWS, D)), _full((4, KW))], out_specs=(_full((SMALL_ROWS, D)), _full((8, D))),
        compiler_params=_params(("arbitrary",)),
    )(stack, gam)


def _adamw_update(w, gv, m, v):
    nm = ADAM_B1 * m + (1.0 - ADAM_B1) * gv
    nv = ADAM_B2 * v + (1.0 - ADAM_B2) * (gv * gv)
    m_hat = nm / (1.0 - ADAM_B1 ** ADAM_STEP)
    v_hat = nv / (1.0 - ADAM_B2 ** ADAM_STEP)
    return -ADAM_LR * (m_hat / (jnp.sqrt(v_hat) + ADAM_EPS) + ADAM_WD * w), nm, nv


def _adamw_sum8(stack, w, m, v, name):
    _, rows, cols = stack.shape
    tr = _row_tile(rows)

    def body(s_ref, w_ref, m_ref, v_ref, g_ref, d_ref, nm_ref, nv_ref):
        gv = s_ref[0].astype(F32)
        for j in range(1, N_DEV):
            gv = gv + s_ref[j].astype(F32)
        g_ref[...] = gv
        d_ref[...], nm_ref[...], nv_ref[...] = _adamw_update(w_ref[...], gv, m_ref[...], v_ref[...])

    blk = pl.BlockSpec((tr, cols), lambda i: (i, 0))
    sd = jax.ShapeDtypeStruct((rows, cols), F32)
    return pl.pallas_call(
        body, name=name, grid=(rows // tr,), out_shape=(sd, sd, sd, sd),
        in_specs=[pl.BlockSpec((N_DEV, tr, cols), lambda i: (0, i, 0)), blk, blk, blk], out_specs=(blk, blk, blk, blk),
        compiler_params=_params(("arbitrary",)),
    )(stack, w, m, v)


def _adamw(w, g, m, v, name):
    shape = w.shape
    cols = shape[-1]
    rows = 1
    for s in shape[:-1]:
        rows *= s
    tr = _row_tile(rows)

    def body(w_ref, g_ref, m_ref, v_ref, d_ref, nm_ref, nv_ref):
        gv = g_ref[...]
        nm = ADAM_B1 * m_ref[...] + (1.0 - ADAM_B1) * gv
        nv = ADAM_B2 * v_ref[...] + (1.0 - ADAM_B2) * (gv * gv)
        m_hat = nm / (1.0 - ADAM_B1 ** ADAM_STEP)
        v_hat = nv / (1.0 - ADAM_B2 ** ADAM_STEP)
        d_ref[...] = -ADAM_LR * (m_hat / (jnp.sqrt(v_hat) + ADAM_EPS) + ADAM_WD * w_ref[...])
        nm_ref[...] = nm
        nv_ref[...] = nv

    blk = pl.BlockSpec((tr, cols), lambda i: (i, 0))
    sd = jax.ShapeDtypeStruct((rows, cols), F32)
    d, nm, nv = pl.pallas_call(
        body, name=name, grid=(rows // tr,), out_shape=(sd, sd, sd), in_specs=[blk] * 4, out_specs=(blk, blk, blk),
        compiler_params=_params(("arbitrary",)),
    )(w.reshape(rows, cols), g.reshape(rows, cols), m.reshape(rows, cols), v.reshape(rows, cols))
    return d.reshape(shape), nm.reshape(shape), nv.reshape(shape)


def _owner_blocks(a):
    return a.reshape(N_DEV, a.shape[0] // N_DEV, a.shape[1])


class _LocalWeights:
    def __init__(self, w_upT, w_down, w_o, w_paT, w_pbT):
        self.weights = (w_upT, w_down, w_o, w_paT, w_pbT)
        self.items = {}

    def sender(self, stage, items=None):
        self.items[stage] = items
        return None

    def sent(self, stage, handle):
        pass

    def mixer_weights(self, after):
        return self.weights[1:]

    def ffn_weights(self, after):
        return self.weights[0]


def _local_step(x, ctx, target, modv, mcv, gam, g_mix, g_ffn, gna, ln_g, ln_b, w_s, b_s, g_final, w_inT, comm):
    nb_ex, seq, _ = x.shape
    ctx_len = ctx.shape[1]
    xt = x.reshape(nb_ex * seq, D)
    ct = ctx.reshape(nb_ex * ctx_len, D)
    tgt = target.reshape(nb_ex * seq, D)
    bs3 = b_s.reshape(GROUPS, SGU_BLOCK, 1)

    pc, hc, _ = _inproj(ct, mcv, g_mix, w_inT, CTX_COLS, ctx_len, "inproj_ctx")
    p, h, handle = _inproj(xt, modv, g_mix, w_inT, IN_COLS, seq, "inproj_lat", sender=comm.sender("inproj"))
    comm.sent("inproj", handle)
    cst_f, cst_b, s_ctx, _ = _hgrn_fwd(pc, gam, None, ctx_len, False, "hgrn_fwd_ctx")
    o_up, o_down, st_f, st_b, _, handle = _hgrn_fwd(p, gam, s_ctx, seq, True, "hgrn_fwd_lat",
                                                    sender=comm.sender("scan"))
    comm.sent("scan", handle)
    w_down, w_o, w_paT, w_pbT = comm.mixer_weights(o_up)
    x1, mix, merged, oa, obm = _tail_fwd(p, o_up, o_down, xt, modv, gna, ln_g, ln_b, w_s, bs3, w_paT, w_pbT, w_o, seq)
    w_upT = comm.ffn_weights(x1)
    dx1, h2, dffn, act, dup, dmod_ffn, small_ffn = _ffn(x1, tgt, modv, g_ffn, g_final, w_upT, w_down, seq)
    dpt, do, dmix, dpa, dpb, dmod_tail, small_tail, dws, dbs, _ = _tail_bwd(
        p, o_up, o_down, dx1, mix, modv, gna, ln_g, ln_b, w_s, bs3, w_paT, w_pbT, w_o, seq)
    dzf, dvf, dqf, dzb, dvb, dqb, dlb, ds0, _ = _hgrn_bwd(p, gam, do, st_f, st_b, None, seq, True, "hgrn_bwd_lat")
    czf, cvf, czb, cvb, dlb_c, _, _ = _hgrn_bwd(pc, gam, None, cst_f, cst_b, ds0, ctx_len, False, "hgrn_bwd_ctx")
    grad_x, dp, dmod_in, small_in = _inproj_bwd([dzf, dzb, dvf, dvb, dqf, dqb], dpt, xt, dx1, modv, g_mix, w_inT,
                                                 seq, "inproj_bwd_lat")
    dpc, dmc, small_c = _inproj_bwd([czf, czb, cvf, cvb], None, ct, None, mcv, g_mix, w_inT, ctx_len, "inproj_bwd_ctx")

    z = lambda r: jnp.zeros((r, D), F32)
    pad = lambda a: jnp.pad(a, ((0, 0), (0, D - a.shape[1])))
    dlb_row = (dlb + dlb_c).reshape(1, 2 * KW)
    dmod = dmod_in + dmod_tail + dmod_ffn
    small = jnp.concatenate([
        small_in[0:1] + small_c[0:1],
        small_ffn[0:1],
        small_ffn[1:2],
        small_tail[0:1],
        small_tail[1:2],
        small_tail[2:3],
        pad(dbs.reshape(1, GROUPS * SGU_BLOCK)),
        dlb_row,
        dmc[0, 0:N_MOD],
        z(2),
        dmod[:, 0:N_MOD].reshape(nb_ex * N_MOD, D),
        z(24 - nb_ex * N_MOD),
        dws.reshape(GROUPS * SGU_BLOCK * SGU_BLOCK // D, D),
    ], axis=0)

    chain = [("gw_in", dp, h, dict(init=_grad_matmul(dpc, hc, "gw_in_ctx")[0])), ("gw_up", dup, h2, {}),
             ("gw_down", act, dffn, dict(tn=256)), ("gw_o", merged, dmix, {}), ("gw_pb", dpb, obm, {}),
             ("gw_pa", dpa, oa, {})]
    items = [(small, "gather")]
    for name, a, b, kw in chain:
        g, handle = _grad_matmul(a, b, name, sender=comm.sender(name, items), **kw)
        comm.sent(name, handle)
        items = [(_owner_blocks(g), "scatter")]
    comm.sender("last", items)
    return small_ffn[2, 0], grad_x.reshape(x.shape)


def kernel(x, c, ctx, c_ctx, w_mod, b_mod, g_mix, g_ffn, w_in, lb_gamma, g_norm_a, ln_v_g, ln_v_b, w_s, b_s, w_pa, w_pb, w_o, w_up, w_down, g_final, loss_target, m_c_ctx, m_w_mod, m_b_mod, m_g_mix, m_g_ffn, m_w_in, m_lb_gamma, m_g_norm_a, m_ln_v_g, m_ln_v_b, m_w_s, m_b_s, m_w_pa, m_w_pb, m_w_o, m_w_up, m_w_down, m_g_final, v_c_ctx, v_w_mod, v_b_mod, v_g_mix, v_g_ffn, v_w_in, v_lb_gamma, v_g_norm_a, v_ln_v_g, v_ln_v_b, v_w_s, v_b_s, v_w_pa, v_w_pb, v_w_o, v_w_up, v_w_down, v_g_final):
    nb_ex = x.shape[0]
    me = 4 * lax.axis_index("x") + 2 * lax.axis_index("y") + lax.axis_index("c")
    cd = MXU_DTYPE
    mod_cols = w_mod.shape[2]
    lb_cols = lb_gamma.shape[2]

    w_inT_l = w_in[0].T.astype(cd)
    w_upT_l = w_up[0].T.astype(cd)
    w_paT_l = w_pa[0].T.astype(cd)
    w_pbT_l = w_pb[0].T.astype(cd)
    cl = jnp.concatenate([c, jnp.pad(lb_gamma.reshape(1, 4 * lb_cols), ((0, 0), (0, D - 4 * lb_cols))),
                          jnp.zeros((8 - nb_ex - 1, D), F32)], axis=0)
    g_in, g_cl = _exchange([(w_inT_l, "gather"), (cl, "gather")], "gather_w_in")
    w_inT = g_in.reshape(IN_COLS, D)
    c_all = g_cl[:, 0:nb_ex].reshape(N_DEV * nb_ex, D)
    gam = jnp.transpose(g_cl[:, nb_ex, 0:4 * lb_cols].reshape(N_DEV, 4, lb_cols), (1, 0, 2)).reshape(4, KW)

    n_c = N_DEV * nb_ex
    cvec = jnp.concatenate([c_all, c_ctx.reshape(1, D), jnp.zeros((7, D), F32)], axis=0)
    b_mod_l = lax.dynamic_slice(b_mod, (0, me * mod_cols), (1, mod_cols))
    mod_l, svec = _mod_fwd(cvec, w_mod[0], b_mod_l)
    (g_mod,) = _exchange([(mod_l, "gather")], "gather_mod")
    mod_all = jnp.transpose(g_mod, (1, 0, 2)).reshape(n_c + 8, N_MOD * D)
    modv = lax.dynamic_slice(mod_all, (me * nb_ex, 0), (nb_ex, N_MOD * D)).reshape(nb_ex, N_MOD, D)
    mcv = mod_all[n_c].reshape(1, N_MOD, D)

    handles, leftover = {}, {}

    class Comm:
        def sender(self, stage, items=None):
            if stage == "inproj":
                return _Sender([(w_down[0].astype(cd), "gather"), (w_o[0].astype(cd), "gather"), (w_paT_l, "gather"),
                                (w_pbT_l, "gather")])
            if stage == "scan":
                return _Sender([(w_upT_l, "gather")])
            if stage == "last":
                leftover["items"] = items
                return None
            return _Sender(items)

        def sent(self, stage, handle):
            handles[stage] = handle

        def mixer_weights(self, after):
            g_down, g_o, g_pa, g_pb = _exchange_wait(handles["inproj"], after)
            return g_down.reshape(D_FF, D), g_o.reshape(D, D), g_pa.reshape(D, KW), g_pb.reshape(D, KW)

        def ffn_weights(self, after):
            (g_up,) = _exchange_wait(handles["scan"], after)
            return g_up.reshape(2 * D_FF, D)

    loss_l, grad_x = _local_step(
        x, ctx, loss_target, modv, mcv, gam, g_mix, g_ffn, g_norm_a, ln_v_g, ln_v_b, w_s[0], b_s[0],
        g_final.reshape(1, D), w_inT, Comm())
    loss = lax.psum(loss_l, ("x", "y", "c"))
    last, last_started = _exchange_start(leftover["items"], "scatter_last", after=leftover["items"][0][0])

    (r_small,) = _exchange_wait(handles["gw_in"], last_started)
    (r_in,) = _exchange_wait(handles["gw_up"], r_small)
    done = {"w_in": [a.T[None] for a in _adamw_sum8(r_in, w_in[0].T, m_w_in[0].T, v_w_in[0].T, "adamw_w_in")]}
    (r_up,) = _exchange_wait(handles["gw_down"], done["w_in"][1])
    done["w_up"] = [a.T[None] for a in _adamw_sum8(r_up, w_up[0].T, m_w_up[0].T, v_w_up[0].T, "adamw_w_up")]
    (r_down,) = _exchange_wait(handles["gw_o"], done["w_up"][1])
    done["w_down"] = [a[None] for a in _adamw_sum8(r_down, w_down[0], m_w_down[0], v_w_down[0], "adamw_w_down")]
    (r_o,) = _exchange_wait(handles["gw_pb"], done["w_down"][1])
    done["w_o"] = [a[None] for a in _adamw_sum8(r_o, w_o[0], m_w_o[0], v_w_o[0], "adamw_w_o")]
    (r_pb,) = _exchange_wait(handles["gw_pa"], done["w_o"][1])
    (r_pa,) = _exchange_wait(last, r_pb)
    grad_w_in, grad_w_up, grad_w_down, grad_w_o = (done[k][0] for k in ("w_in", "w_up", "w_down", "w_o"))
    grad_w_pa = _sum8(r_pa, "sum_w_pa").T[None]
    grad_w_pb = _sum8(r_pb, "sum_w_pb").T[None]
    tot, bm = _small_reduce(r_small, gam, nb_ex)
    grad_g_mix, grad_g_ffn, grad_g_final = tot[0:1], tot[1:2], tot[2]
    grad_g_norm_a = tot[3:4, 0:DK]
    grad_ln_v_g, grad_ln_v_b = tot[4:5, 0:KW], tot[5:6, 0:KW]
    grad_b_s = tot[6, 0:GROUPS * SGU_BLOCK].reshape(1, GROUPS, SGU_BLOCK)
    grad_w_s = tot[40:104].reshape(1, GROUPS, SGU_BLOCK, SGU_BLOCK)
    grad_b_mod = bm[0:N_MOD].reshape(1, N_MOD * D)
    grad_lb_gamma = lax.dynamic_slice(bm[6:8].reshape(2, 2, KW), (0, 0, me * lb_cols), (2, 2, lb_cols))

    dmod_all = r_small[:, 16:16 + nb_ex * N_MOD].reshape(n_c, N_MOD * D)
    dmod_l = jnp.concatenate([lax.dynamic_slice(dmod_all, (0, me * mod_cols), (n_c, mod_cols)),
                              lax.dynamic_slice(tot[8:8 + N_MOD].reshape(1, N_MOD * D), (0, me * mod_cols), (1, mod_cols)),
                              jnp.zeros((7, mod_cols), F32)], axis=0)
    gw_mod, gc = _mod_bwd(svec, cvec, dmod_l, w_mod[0])
    grad_w_mod = gw_mod[None]
    (r_gc,) = _exchange([(gc[n_c:n_c + 8], "gather")], "gather_c_ctx", after=r_pa)
    grad_c_ctx = _sum8(r_gc, "sum_c_ctx")[0]

    names = ["c_ctx", "w_mod", "b_mod", "g_mix", "g_ffn", "w_in", "lb_gamma", "g_norm_a", "ln_v_g", "ln_v_b", "w_s",
             "b_s", "w_pa", "w_pb", "w_o", "w_up", "w_down", "g_final"]
    weights = [c_ctx, w_mod, b_mod, g_mix, g_ffn, w_in, lb_gamma, g_norm_a, ln_v_g, ln_v_b, w_s, b_s, w_pa, w_pb, w_o,
               w_up, w_down, g_final]
    grads = [grad_c_ctx, grad_w_mod, grad_b_mod, grad_g_mix, grad_g_ffn, grad_w_in, grad_lb_gamma, grad_g_norm_a,
             grad_ln_v_g, grad_ln_v_b, grad_w_s, grad_b_s, grad_w_pa, grad_w_pb, grad_w_o, grad_w_up, grad_w_down,
             grad_g_final]
    ms = [m_c_ctx, m_w_mod, m_b_mod, m_g_mix, m_g_ffn, m_w_in, m_lb_gamma, m_g_norm_a, m_ln_v_g, m_ln_v_b, m_w_s, m_b_s,
          m_w_pa, m_w_pb, m_w_o, m_w_up, m_w_down, m_g_final]
    vs = [v_c_ctx, v_w_mod, v_b_mod, v_g_mix, v_g_ffn, v_w_in, v_lb_gamma, v_g_norm_a, v_ln_v_g, v_ln_v_b, v_w_s, v_b_s,
          v_w_pa, v_w_pb, v_w_o, v_w_up, v_w_down, v_g_final]
    deltas, new_ms, new_vs = [], [], []
    for nm, w, g, m, v in zip(names, weights, grads, ms, vs):
        d, nm_, nv_ = done[nm][1:] if nm in done else _adamw(w, g.reshape(w.shape), m, v, "adamw_" + nm)
        deltas.append(d)
        new_ms.append(nm_)
        new_vs.append(nv_)
    grads = [g.reshape(w.shape) for g, w in zip(grads, weights)]
    return (loss, grad_x, *grads, *deltas, *new_ms, *new_vs)
```

```python
import functools

import jax
import jax.numpy as jnp
from jax import lax
from jax.experimental import pallas as pl
from jax.experimental.pallas import tpu as pltpu

F32 = jnp.float32
MXU_DTYPE = jnp.bfloat16
PAYLOAD_DTYPE = jnp.bfloat16

N_DEV = 8
D = 1024
HEADS = 4
DK = 128
KW = HEADS * DK
CHUNK = 64
SGU_BLOCK = 128
GROUPS = 4
D_FF = 2816
FF_CHUNK = 256
N_MOD = 6
IN_COLS = 5632
CTX_COLS = 1536
TAIL_COLS = IN_COLS - 4 * KW
EPS = 1e-6
ADAM_LR, ADAM_B1, ADAM_B2, ADAM_EPS, ADAM_WD, ADAM_STEP = 0.001, 0.9, 0.999, 1e-08, 0.01, 10

VMEM_LIMIT = 56 * 1024 * 1024
TOKEN_TILE = 256
SMALL_ROWS = 104


def _params(sem):
    return pltpu.CompilerParams(dimension_semantics=sem, vmem_limit_bytes=VMEM_LIMIT)


_DN = {"nn": (((1,), (0,)), ((), ())), "nt": (((1,), (1,)), ((), ())), "tn": (((0,), (0,)), ((), ()))}


def _dot(a, b, form="nn"):
    return lax.dot_general(a.astype(MXU_DTYPE), b.astype(MXU_DTYPE), _DN[form], preferred_element_type=F32)


def _dotx(a, b, form="nn"):
    return lax.dot_general(a.astype(F32), b.astype(F32), _DN[form], preferred_element_type=F32,
                           precision=lax.Precision.HIGHEST)


def _full(shape, single=False):
    n = len(shape)
    if single:
        return pl.BlockSpec(shape, lambda *_: (0,) * n, pipeline_mode=pl.Buffered(1))
    return pl.BlockSpec(shape, lambda *_: (0,) * n)


def _ordered_behind(body, in_specs, args, after):
    if after is None:
        return body
    at = len(in_specs)
    in_specs.append(pl.BlockSpec(memory_space=pl.ANY))
    args.append(after)
    return lambda *refs: body(*refs[:at], *refs[at + 1:])


def _sigmoid(z):
    return 1.0 / (1.0 + jnp.exp(-z))


def _gelu(x):
    c = 0.7978845608028654
    t = jnp.tanh(c * (x + 0.044715 * x * x * x))
    return 0.5 * x * (1.0 + t), t


def _gelu_grad(x, t):
    c = 0.7978845608028654
    return 0.5 * (1.0 + t) + 0.5 * x * (1.0 - t * t) * c * (1.0 + 3 * 0.044715 * x * x)


def _exchange(items, name, after=None):
    n = len(items)
    out_shape = []
    for a, mode in items:
        blk = a.shape if mode == "gather" else a.shape[1:]
        out_shape.append(jax.ShapeDtypeStruct((N_DEV,) + tuple(blk), a.dtype))

    def body(*refs):
        srcs, dsts = refs[:n], refs[n:2 * n]
        send_sems, recv_sems, local_sems = refs[2 * n:]
        x, y, c = lax.axis_index("x"), lax.axis_index("y"), lax.axis_index("c")
        me = 4 * x + 2 * y + c

        def src_for(i, dev):
            return srcs[i] if items[i][1] == "gather" else srcs[i].at[dev]

        local = [pltpu.make_async_copy(src_for(i, me), dsts[i].at[me], local_sems.at[i]) for i in range(n)]
        for cp in local:
            cp.start()
        remote = []
        for k in range(1, N_DEV):
            px = jnp.bitwise_xor(x, (k >> 2) & 1)
            py = jnp.bitwise_xor(y, (k >> 1) & 1)
            pc = jnp.bitwise_xor(c, k & 1)
            peer = 4 * px + 2 * py + pc
            for i in range(n):
                cp = pltpu.make_async_remote_copy(
                    src_ref=src_for(i, peer), dst_ref=dsts[i].at[me],
                    send_sem=send_sems.at[i * (N_DEV - 1) + k - 1], recv_sem=recv_sems.at[i * (N_DEV - 1) + k - 1],
                    device_id=(px, py, pc), device_id_type=pl.DeviceIdType.MESH)
                cp.start()
                remote.append(cp)
        for cp in remote:
            cp.wait()
        for cp in local:
            cp.wait()

    any_spec = pl.BlockSpec(memory_space=pl.ANY)
    in_specs, args = [any_spec] * n, [a for a, _ in items]
    if after is not None:
        in_specs.append(any_spec)
        args.append(after)
        exchange = body
        body = lambda *refs: exchange(*refs[:n], *refs[n + 1:])
    return pl.pallas_call(
        body, name=name, out_shape=out_shape, in_specs=in_specs, out_specs=[any_spec] * n,
        scratch_shapes=[pltpu.SemaphoreType.DMA((n * (N_DEV - 1),)), pltpu.SemaphoreType.DMA((n * (N_DEV - 1),)),
                        pltpu.SemaphoreType.DMA((n,))],
    )(*args)


def _gather_two_level(arrays, name):
    n = len(arrays)

    def body(*refs):
        srcs, dsts = refs[:n], refs[n:2 * n]
        send_sems, recv_sems, local_sems = refs[2 * n:]
        x, y, c = lax.axis_index("x"), lax.axis_index("y"), lax.axis_index("c")
        sibling = (x, y, 1 - c)
        chips = [(1 - x, y), (x, 1 - y), (1 - x, 1 - y)]

        def slot(px, py, pc):
            return 4 * px + 2 * py + pc

        def copy(i, k, block, to, src=None):
            return pltpu.make_async_remote_copy(
                src_ref=dsts[i].at[slot(*block)] if src is None else src, dst_ref=dsts[i].at[slot(*block)],
                send_sem=send_sems.at[i * 7 + k], recv_sem=recv_sems.at[i * 7 + k],
                device_id=to, device_id_type=pl.DeviceIdType.MESH)

        me = (x, y, c)
        mine = [pltpu.make_async_copy(srcs[i], dsts[i].at[slot(*me)], local_sems.at[i]) for i in range(n)]
        for cp in mine:
            cp.start()
        first = []
        for j, chip in enumerate(chips):
            first += [copy(i, 1 + j, me, (*chip, c), src=srcs[i]) for i in range(n)]
        first += [copy(i, 0, me, sibling, src=srcs[i]) for i in range(n)]
        for cp in first:
            cp.start()
        passed = []
        for j, chip in enumerate(chips):
            for i in range(n):
                copy(i, 1 + j, (*chip, c), me).wait_recv()
                cp = copy(i, 4 + j, (*chip, c), sibling)
                cp.start()
                passed.append(cp)
        for i in range(n):
            copy(i, 0, sibling, me).wait_recv()
            for j, chip in enumerate(chips):
                copy(i, 4 + j, (*chip, 1 - c), me).wait_recv()
        for cp in first + passed:
            cp.wait_send()
        for cp in mine:
            cp.wait()

    any_spec = pl.BlockSpec(memory_space=pl.ANY)
    return pl.pallas_call(
        body, name=name, out_shape=[jax.ShapeDtypeStruct((N_DEV,) + a.shape, a.dtype) for a in arrays],
        in_specs=[any_spec] * n, out_specs=[any_spec] * n,
        scratch_shapes=[pltpu.SemaphoreType.DMA((n * 7,)), pltpu.SemaphoreType.DMA((n * 7,)),
                        pltpu.SemaphoreType.DMA((n,))],
    )(*arrays)


_HBM = pl.BlockSpec(memory_space=pltpu.HBM)
_SEM = pl.BlockSpec(memory_space=pltpu.SEMAPHORE)
_EFFECT = pltpu.SideEffectType.DATAFLOW_SIDE_EFFECTING


def _split_copies(items, srcs, lands, send_sems, recv_sems):
    x, y, c = lax.axis_index("x"), lax.axis_index("y"), lax.axis_index("c")
    me = 4 * x + 2 * y + c
    copies = []
    for k in range(1, N_DEV):
        px = jnp.bitwise_xor(x, (k >> 2) & 1)
        py = jnp.bitwise_xor(y, (k >> 1) & 1)
        pc = jnp.bitwise_xor(c, k & 1)
        peer = 4 * px + 2 * py + pc
        for i in range(len(items)):
            src = srcs[i] if items[i][1] == "gather" else srcs[i].at[peer]
            copies.append(pltpu.make_async_remote_copy(
                src_ref=src, dst_ref=lands[i].at[me],
                send_sem=send_sems.at[i * (N_DEV - 1) + k - 1], recv_sem=recv_sems.at[i * (N_DEV - 1) + k - 1],
                device_id=(px, py, pc), device_id_type=pl.DeviceIdType.MESH))
    return me, copies


def _exchange_start(items, name, after):
    n = len(items)
    n_sem = n * (N_DEV - 1)
    srcs, lands = [], []
    for a, mode in items:
        blk = a.shape if mode == "gather" else a.shape[1:]
        srcs.append(pltpu.with_memory_space_constraint(a, pltpu.HBM))
        lands.append(pltpu.with_memory_space_constraint(lax.empty((N_DEV,) + tuple(blk), a.dtype), pltpu.HBM))

    def body(*refs):
        src_refs, land_refs = refs[:n], refs[n:2 * n]
        send_sems, recv_sems = refs[2 * n + 1], refs[2 * n + 2]
        local_sems = refs[4 * n + 3]
        me, copies = _split_copies(items, src_refs, land_refs, send_sems, recv_sems)
        for i in range(n):
            own = src_refs[i] if items[i][1] == "gather" else src_refs[i].at[me]
            cp = pltpu.make_async_copy(own, land_refs[i].at[me], local_sems.at[i])
            cp.start()
            cp.wait()
        for cp in copies:
            cp.start()

    out_shape = [pltpu.SemaphoreType.DMA((n_sem,)), pltpu.SemaphoreType.DMA((n_sem,))]
    out_shape += [pltpu.HBM(a.shape, a.dtype) for a in srcs] + [pltpu.HBM(a.shape, a.dtype) for a in lands]
    outs = pl.pallas_call(
        body, name=name, out_shape=out_shape,
        in_specs=[_HBM] * (2 * n) + [pl.BlockSpec(memory_space=pl.ANY)],
        out_specs=[_SEM, _SEM] + [_HBM] * (2 * n),
        input_output_aliases={i: 2 + i for i in range(2 * n)},
        scratch_shapes=[pltpu.SemaphoreType.DMA((n,))],
        compiler_params=pltpu.CompilerParams(has_side_effects=_EFFECT),
    )(*srcs, *lands, after)
    handle = (items, name, outs[0], outs[1], outs[2:2 + n], outs[2 + n:2 + 2 * n])
    return handle, outs[2]


class _Sender:
    PIECE_ROWS = 352

    def __init__(self, items, chunks=None):
        self.items, self.n = items, len(items)
        self.chunks = chunks
        if chunks is None:
            block_rows = [a.shape[0] if mode == "gather" else a.shape[1] for a, mode in items]
            self.chunks = [r // self.PIECE_ROWS if r % self.PIECE_ROWS == 0 else 1 for r in block_rows]
        self.srcs, self.lands = [], []
        for a, mode in items:
            blk = a.shape if mode == "gather" else a.shape[1:]
            self.srcs.append(pltpu.with_memory_space_constraint(a, pltpu.HBM))
            self.lands.append(pltpu.with_memory_space_constraint(lax.empty((N_DEV,) + tuple(blk), a.dtype), pltpu.HBM))

    def issue(self, src_refs, land_refs, send_sems, recv_sems, local_sems, step, n_steps):
        x, y, c = lax.axis_index("x"), lax.axis_index("y"), lax.axis_index("c")
        me = 4 * x + 2 * y + c
        copies = []
        for ch in range(max(self.chunks)):
            for k in range(1, N_DEV):
                px = jnp.bitwise_xor(x, (k >> 2) & 1)
                py = jnp.bitwise_xor(y, (k >> 1) & 1)
                pc = jnp.bitwise_xor(c, k & 1)
                peer = 4 * px + 2 * py + pc
                for i, (_, mode) in enumerate(self.items):
                    if ch >= self.chunks[i]:
                        continue
                    n_rows = land_refs[i].shape[1] // self.chunks[i]
                    rows = pl.ds(ch * n_rows, n_rows)
                    src = src_refs[i].at[rows] if mode == "gather" else src_refs[i].at[peer].at[rows]
                    copies.append(pltpu.make_async_remote_copy(
                        src_ref=src, dst_ref=land_refs[i].at[me].at[rows],
                        send_sem=send_sems.at[i * (N_DEV - 1) + k - 1], recv_sem=recv_sems.at[i * (N_DEV - 1) + k - 1],
                        device_id=(px, py, pc), device_id_type=pl.DeviceIdType.MESH))
        own = [pltpu.make_async_copy(src_refs[i] if mode == "gather" else src_refs[i].at[me], land_refs[i].at[me],
                                     local_sems.at[i]) for i, (_, mode) in enumerate(self.items)]

        @pl.when(step == 0)
        def _():
            for cp in own:
                cp.start()

        for s in range(n_steps):
            group = [cp for j, cp in enumerate(copies) if (j * n_steps) // len(copies) == s]
            if group:
                @pl.when(step == s)
                def _(group=group):
                    for cp in group:
                        cp.start()

        @pl.when(step == n_steps - 1)
        def _():
            for cp in own:
                cp.wait()


def _host_call(body, name, grid, in_specs, args, out_shape, out_specs, scratch_shapes, after=None, sender=None):
    in_specs, args, out_shape, out_specs = list(in_specs), list(args), list(out_shape), list(out_specs)
    scratch_shapes = list(scratch_shapes)
    semantics = ("arbitrary",) * len(grid)
    body = _ordered_behind(body, in_specs, args, after)
    if sender is None:
        res = pl.pallas_call(body, name=name, grid=grid, in_specs=in_specs, out_specs=out_specs, out_shape=out_shape,
                             scratch_shapes=scratch_shapes, compiler_params=_params(semantics))(*args)
        return res, None
    n, n_in, n_out, n_scr = sender.n, len(in_specs), len(out_shape), len(scratch_shapes)
    n_sem = n * (N_DEV - 1)
    n_steps = 1
    for g in grid:
        n_steps *= g
    compute = body

    def body(*refs):
        ins, s_in = refs[:n_in], refs[n_in:n_in + 2 * n]
        o0 = n_in + 2 * n
        outs, s_out = refs[o0:o0 + n_out], refs[o0 + n_out:o0 + n_out + 2 + 2 * n]
        scr = refs[o0 + n_out + 2 + 2 * n:]
        compute(*ins, *outs, *scr[:n_scr])
        step = pl.program_id(0)
        for d in range(1, len(grid)):
            step = step * grid[d] + pl.program_id(d)
        sender.issue(s_in[:n], s_in[n:], s_out[0], s_out[1], scr[n_scr], step, n_steps)

    res = pl.pallas_call(
        body, name=name, grid=grid,
        in_specs=in_specs + [_HBM] * (2 * n), out_specs=out_specs + [_SEM, _SEM] + [_HBM] * (2 * n),
        out_shape=out_shape + [pltpu.SemaphoreType.DMA((n_sem,)), pltpu.SemaphoreType.DMA((n_sem,))]
        + [pltpu.HBM(a.shape, a.dtype) for a in sender.srcs] + [pltpu.HBM(a.shape, a.dtype) for a in sender.lands],
        input_output_aliases={n_in + j: n_out + 2 + j for j in range(2 * n)},
        scratch_shapes=scratch_shapes + [pltpu.SemaphoreType.DMA((n,))],
        compiler_params=pltpu.CompilerParams(dimension_semantics=semantics, vmem_limit_bytes=VMEM_LIMIT,
                                             has_side_effects=_EFFECT),
    )(*args, *sender.srcs, *sender.lands)
    handle = (sender.items, name, res[n_out], res[n_out + 1], res[n_out + 2:n_out + 2 + n],
              res[n_out + 2 + n:n_out + 2 + 2 * n])
    return res[:n_out], handle


def _exchange_wait(handle, after):
    items, name, send_sems, recv_sems, srcs, lands = handle
    n = len(items)

    def body(*refs):
        src_refs, land_refs = refs[:n], refs[n:2 * n]
        send_ref, recv_ref = refs[2 * n], refs[2 * n + 1]
        _, copies = _split_copies(items, src_refs, land_refs, send_ref, recv_ref)
        for cp in copies:
            cp.wait_send()
            cp.wait_recv()

    outs = pl.pallas_call(
        body, name=name + "_wait",
        out_shape=[pltpu.HBM(a.shape, a.dtype) for a in srcs] + [pltpu.HBM(a.shape, a.dtype) for a in lands],
        in_specs=[_HBM] * (2 * n) + [_SEM, _SEM, pl.BlockSpec(memory_space=pl.ANY)], out_specs=[_HBM] * (2 * n),
        input_output_aliases={i: i for i in range(2 * n)},
        compiler_params=pltpu.CompilerParams(has_side_effects=_EFFECT),
    )(*srcs, *lands, send_sems, recv_sems, after)
    return outs[n:]


def _mod_fwd(cvec, w_mod_l, b_mod_l):
    rows, cols = cvec.shape[0], w_mod_l.shape[1]

    def body(c_ref, w_ref, b_ref, o_ref, s_ref):
        cv = c_ref[...]
        s = cv * _sigmoid(cv)
        s_ref[...] = s
        o_ref[...] = _dot(s, w_ref[...]) + b_ref[...]

    return pl.pallas_call(
        body, name="mod_fwd",
        out_shape=(jax.ShapeDtypeStruct((rows, cols), F32), jax.ShapeDtypeStruct((rows, D), F32)),
        in_specs=[_full((rows, D)), _full((D, cols)), _full((1, cols))],
        out_specs=(_full((rows, cols)), _full((rows, D))), grid=(1,),
        compiler_params=_params(("arbitrary",)),
    )(cvec, w_mod_l, b_mod_l)


def _mod_bwd(svec, cvec, dmod_l, w_mod_l):
    rows, cols = dmod_l.shape

    def body(s_ref, c_ref, d_ref, w_ref, gw_ref, gc_ref):
        gw_ref[...] = _dot(s_ref[...], d_ref[...], "tn")
        cv = c_ref[...]
        sg = _sigmoid(cv)
        gc_ref[...] = _dot(d_ref[...], w_ref[...], "nt") * (sg * (1.0 + cv * (1.0 - sg)))

    return pl.pallas_call(
        body, name="mod_bwd",
        out_shape=(jax.ShapeDtypeStruct((D, cols), F32), jax.ShapeDtypeStruct((rows, D), F32)),
        in_specs=[_full((rows, D)), _full((rows, D)), _full((rows, cols)), _full((D, cols))],
        out_specs=(_full((D, cols)), _full((rows, D))), grid=(1,),
        compiler_params=_params(("arbitrary",)),
    )(svec, cvec, dmod_l, w_mod_l)


def _inproj(xt, modv, g, w_inT, n_cols, rows_per_example, name, after=None, sender=None):
    rows = xt.shape[0]
    tm = min(TOKEN_TILE, rows_per_example)
    per_b = rows_per_example // tm
    shared_mod = modv.shape[0] == 1

    def body(x_ref, mod_ref, g_ref, w_ref, p_ref, h_ref):
        x = x_ref[...]
        r = lax.rsqrt(jnp.mean(x * x, axis=-1, keepdims=True) + EPS)
        h = (x * r * g_ref[...]) * (1.0 + mod_ref[0, 1:2, :]) + mod_ref[0, 0:1, :]
        hb = h.astype(MXU_DTYPE)
        h_ref[...] = hb
        for j in range(n_cols // KW):
            p_ref[:, j * KW:(j + 1) * KW] = _dot(hb, w_ref[j * KW:(j + 1) * KW, :], "nt").astype(p_ref.dtype)

    mod_idx = (lambda i: (0, 0, 0)) if shared_mod else (lambda i: (i // per_b, 0, 0))
    in_specs = [pl.BlockSpec((tm, D), lambda i: (i, 0)), pl.BlockSpec((1, N_MOD, D), mod_idx), _full((1, D)),
                pl.BlockSpec((n_cols, D), lambda i: (0, 0), pipeline_mode=pl.Buffered(1))]
    (p, h), handle = _host_call(
        body, name, (rows // tm,), in_specs, [xt, modv, g, w_inT],
        [jax.ShapeDtypeStruct((rows, n_cols), MXU_DTYPE), jax.ShapeDtypeStruct((rows, D), MXU_DTYPE)],
        [pl.BlockSpec((tm, n_cols), lambda i: (i, 0)), pl.BlockSpec((tm, D), lambda i: (i, 0))], [],
        after=after, sender=sender)
    return p, h, handle


def _tri(reverse):
    row = lax.broadcasted_iota(jnp.int32, (CHUNK, CHUNK), 0)
    col = lax.broadcasted_iota(jnp.int32, (CHUNK, CHUNK), 1)
    return (col >= row) if reverse else (col <= row)


def _lower_bound(gam_ref, direction):
    return _sigmoid(gam_ref[direction:direction + 1, :] - gam_ref[2 + direction:3 + direction, :])


def _gate_prep(z, lb, tri_f):
    sg = _sigmoid(z)
    f = lb + (1.0 - lb) * sg
    g = jnp.log(f)
    b = _dotx(tri_f, g)
    bl = jnp.sum(g, axis=0, keepdims=True)
    return sg, f, 1.0 - f, b, bl


def _hgrn_fwd(p, gam, s0, rows_per_example, with_out, name, sender=None):
    rows = p.shape[0]
    nb_ex = rows // rows_per_example
    rb = min(TOKEN_TILE, rows_per_example)
    cpb = rb // CHUNK
    nb = rows_per_example // rb
    n_chunks = rows // CHUNK
    has_s0 = s0 is not None

    def body(*refs):
        it = iter(refs)
        gam_ref = next(it)
        zf_ref, vf_ref = next(it), next(it)
        qf_ref = next(it) if with_out else None
        zb_ref, vb_ref = next(it), next(it)
        qb_ref = next(it) if with_out else None
        s0_ref = next(it) if has_s0 else None
        if with_out:
            of_ref, ob_ref = next(it), next(it)
        stash_f, stash_b, fin_ref = next(it), next(it), next(it)
        st_ref = next(it)
        i = pl.program_id(1)

        @pl.when(i == 0)
        def _():
            if has_s0:
                st_ref[...] = s0_ref[:, 0]
            else:
                st_ref[...] = jnp.zeros_like(st_ref)

        for direction, (z_ref, v_ref, q_ref, stash) in enumerate(
                ((zf_ref, vf_ref, qf_ref, stash_f), (zb_ref, vb_ref, qb_ref, stash_b))):
            reverse = direction == 1
            tri = _tri(reverse)
            tri_f = tri.astype(F32)
            lb = _lower_bound(gam_ref, direction)
            order = range(cpb - 1, -1, -1) if reverse else range(cpb)
            for j in order:
                rs = slice(j * CHUNK, (j + 1) * CHUNK)
                z = z_ref[rs, :].astype(F32)
                v = v_ref[rs, :].astype(F32)
                _, _, k, b, bl = _gate_prep(z, lb, tri_f)
                mid = 0.5 * bl
                kd = k * jnp.exp(bl - b)
                a = jnp.exp(bl)
                if with_out:
                    q = q_ref[rs, :].astype(F32)
                    qi = q * jnp.exp(b - mid)
                    ki = k * jnp.exp(mid - b)
                    qe = q * jnp.exp(b)
                for h in range(HEADS):
                    hs = slice(h * DK, (h + 1) * DK)
                    st = st_ref[direction, h]
                    stash[j, h] = st.astype(stash.dtype)
                    if with_out:
                        sc = jnp.where(tri, _dot(qi[:, hs], ki[:, hs], "nt"), 0.0)
                        o = _dot(sc, v[:, hs]) + _dot(qe[:, hs], st, "nt")
                        (ob_ref if reverse else of_ref)[rs, hs] = o
                    st_ref[direction, h] = st * a[:, hs] + _dot(v[:, hs], kd[:, hs], "tn")

        @pl.when(i == nb - 1)
        def _():
            fin_ref[:, 0] = st_ref[...]

    up = lambda b, i: b * nb + i
    down = lambda b, i: b * nb + nb - 1 - i
    col = lambda rowf, c: pl.BlockSpec((rb, KW), lambda b, i: (rowf(b, i), c))
    in_specs = [_full((4, KW)), col(up, 0), col(up, 2)] + ([col(up, 3)] if with_out else [])
    in_specs += [col(down, 1), col(down, 2)] + ([col(down, 3)] if with_out else [])
    args = [gam, p, p] + ([p] if with_out else []) + [p, p] + ([p] if with_out else [])
    if has_s0:
        in_specs.append(pl.BlockSpec((2, 1, HEADS, DK, DK), lambda b, i: (0, b, 0, 0, 0)))
        args.append(s0)
    out_shape, out_specs = [], []
    if with_out:
        out_shape += [jax.ShapeDtypeStruct((rows, KW), F32)] * 2
        out_specs += [pl.BlockSpec((rb, KW), lambda b, i: (up(b, i), 0)),
                      pl.BlockSpec((rb, KW), lambda b, i: (down(b, i), 0))]
    out_shape += [jax.ShapeDtypeStruct((n_chunks, HEADS, DK, DK), MXU_DTYPE)] * 2
    out_specs += [pl.BlockSpec((cpb, HEADS, DK, DK), lambda b, i: (up(b, i), 0, 0, 0)),
                  pl.BlockSpec((cpb, HEADS, DK, DK), lambda b, i: (down(b, i), 0, 0, 0))]
    out_shape.append(jax.ShapeDtypeStruct((2, nb_ex, HEADS, DK, DK), F32))
    out_specs.append(pl.BlockSpec((2, 1, HEADS, DK, DK), lambda b, i: (0, b, 0, 0, 0)))
    res, handle = _host_call(body, name, (nb_ex, nb), in_specs, args, out_shape, out_specs,
                             [pltpu.VMEM((2, HEADS, DK, DK), F32)], sender=sender)
    return (*res, handle)


def _hgrn_bwd(p, gam, do, stash_f, stash_b, ds_end, rows_per_example, with_out, name, after=None, sender=None):
    rows = p.shape[0]
    nb_ex = rows // rows_per_example
    rb = min(TOKEN_TILE, rows_per_example)
    cpb = rb // CHUNK
    nb = rows_per_example // rb
    has_end = ds_end is not None

    def body(*refs):
        it = iter(refs)
        gam_ref = next(it)
        ins = []
        for _ in range(2):
            z_ref, v_ref = next(it), next(it)
            q_ref = next(it) if with_out else None
            do_ref = next(it) if with_out else None
            ins.append((z_ref, v_ref, q_ref, do_ref, next(it)))
        end_ref = next(it) if has_end else None
        outs = []
        for _ in range(2):
            dz_ref, dv_ref = next(it), next(it)
            dq_ref = next(it) if with_out else None
            outs.append((dz_ref, dv_ref, dq_ref))
        dlb_ref, ds0_ref = next(it), next(it)
        dst_ref = next(it)
        b_id, i = pl.program_id(0), pl.program_id(1)

        @pl.when(i == 0)
        def _():
            if has_end:
                dst_ref[...] = end_ref[:, 0]
            else:
                dst_ref[...] = jnp.zeros_like(dst_ref)

        @pl.when((i == 0) & (b_id == 0))
        def _():
            dlb_ref[...] = jnp.zeros_like(dlb_ref)

        for direction in range(2):
            z_ref, v_ref, q_ref, do_ref, stash = ins[direction]
            dz_ref, dv_ref, dq_ref = outs[direction]
            reverse = direction == 1
            tri = _tri(reverse)
            tri_f = tri.astype(F32)
            lb = _lower_bound(gam_ref, direction)
            order = range(cpb) if reverse else range(cpb - 1, -1, -1)
            dlb_acc = jnp.zeros((1, KW), F32)
            for j in order:
                rs = slice(j * CHUNK, (j + 1) * CHUNK)
                z = z_ref[rs, :].astype(F32)
                v = v_ref[rs, :].astype(F32)
                sg, f, k, b, bl = _gate_prep(z, lb, tri_f)
                mid = 0.5 * bl
                e3 = jnp.exp(bl - b)
                kd = k * e3
                a = jnp.exp(bl)
                if with_out:
                    q = q_ref[rs, :].astype(F32)
                    dout = do_ref[rs, :].astype(F32)
                    e1, e2, e4 = jnp.exp(b - mid), jnp.exp(mid - b), jnp.exp(b)
                    qi, ki, qe = q * e1, k * e2, q * e4
                dkd_p, dv_p, da_p, dqi_p, dki_p, dqe_p = [], [], [], [], [], []
                for h in range(HEADS):
                    hs = slice(h * DK, (h + 1) * DK)
                    st_in = stash[j, h]
                    dst = dst_ref[direction, h]
                    dkd_p.append(_dot(v[:, hs], dst))
                    dvh = _dot(kd[:, hs], dst, "nt")
                    da_p.append(jnp.sum(dst * st_in.astype(F32), axis=0, keepdims=True))
                    new_dst = dst * a[:, hs]
                    if with_out:
                        sc = jnp.where(tri, _dot(qi[:, hs], ki[:, hs], "nt"), 0.0)
                        dsc = jnp.where(tri, _dot(dout[:, hs], v[:, hs], "nt"), 0.0)
                        dqi_p.append(_dot(dsc, ki[:, hs]))
                        dki_p.append(_dot(dsc, qi[:, hs], "tn"))
                        dqe_p.append(_dot(dout[:, hs], st_in))
                        dvh = dvh + _dot(sc, dout[:, hs], "tn")
                        new_dst = new_dst + _dot(dout[:, hs], qe[:, hs], "tn")
                    dv_p.append(dvh)
                    dst_ref[direction, h] = new_dst
                cat = lambda parts: jnp.concatenate(parts, axis=1)
                dkd, da = cat(dkd_p), cat(da_p)
                dv_ref[rs, :] = cat(dv_p)
                t_kd = dkd * kd
                dk = dkd * e3
                db = -t_kd
                dbl = jnp.sum(t_kd, axis=0, keepdims=True) + da * a
                if with_out:
                    dqi, dki, dqe = cat(dqi_p), cat(dki_p), cat(dqe_p)
                    dq_ref[rs, :] = dqi * e1 + dqe * e4
                    dk = dk + dki * e2
                    t_qi, t_ki, t_qe = dqi * qi, dki * ki, dqe * qe
                    db = db + t_qi - t_ki + t_qe
                    dbl = dbl + 0.5 * jnp.sum(t_ki - t_qi, axis=0, keepdims=True)
                dg = _dotx(tri_f, db, "tn") + dbl
                df = dg / f - dk
                dz_ref[rs, :] = df * (1.0 - lb) * sg * (1.0 - sg)
                dlb_acc = dlb_acc + jnp.sum(df * (1.0 - sg), axis=0, keepdims=True)
            dlb_ref[direction:direction + 1, :] += dlb_acc

        @pl.when(i == nb - 1)
        def _():
            ds0_ref[:, 0] = dst_ref[...]

    rows_of = (lambda b, i: b * nb + nb - 1 - i, lambda b, i: b * nb + i)
    in_specs, args = [_full((4, KW))], [gam]
    for direction in range(2):
        rf = rows_of[direction]
        col = lambda c, rf=rf: pl.BlockSpec((rb, KW), lambda b, i: (rf(b, i), c))
        in_specs += [col(direction), col(2)]
        args += [p, p]
        if with_out:
            in_specs += [col(3), col(0)]
            args += [p, do]
        in_specs.append(pl.BlockSpec((cpb, HEADS, DK, DK), lambda b, i, rf=rf: (rf(b, i), 0, 0, 0)))
        args.append((stash_f, stash_b)[direction])
    if has_end:
        in_specs.append(pl.BlockSpec((2, 1, HEADS, DK, DK), lambda b, i: (0, b, 0, 0, 0)))
        args.append(ds_end)
    out_shape, out_specs = [], []
    for direction in range(2):
        rf = rows_of[direction]
        n_out = 3 if with_out else 2
        out_shape += [jax.ShapeDtypeStruct((rows, KW), F32)] * n_out
        out_specs += [pl.BlockSpec((rb, KW), lambda b, i, rf=rf: (rf(b, i), 0))] * n_out
    out_shape += [jax.ShapeDtypeStruct((2, KW), F32), jax.ShapeDtypeStruct((2, nb_ex, HEADS, DK, DK), F32)]
    out_specs += [_full((2, KW)), pl.BlockSpec((2, 1, HEADS, DK, DK), lambda b, i: (0, b, 0, 0, 0))]
    res, handle = _host_call(body, name, (nb_ex, nb), in_specs, args, out_shape, out_specs,
                             [pltpu.VMEM((2, HEADS, DK, DK), F32)], after=after, sender=sender)
    return (*res, handle)


def _tail_forward(osum, og, u, v, ga, gb, gna, ln_g, ln_b, ws_ref, bs_ref, wpaT_ref, wpbT_ref):
    tm = osum.shape[0]
    gna4 = jnp.concatenate([gna] * HEADS, axis=1)
    r_parts = []
    for h in range(HEADS):
        oh = osum[:, h * DK:(h + 1) * DK]
        r_parts.append(jnp.broadcast_to(lax.rsqrt(jnp.mean(oh * oh, axis=-1, keepdims=True) + EPS), (tm, DK)))
    r = jnp.concatenate(r_parts, axis=1)
    on = osum * r
    sg_og = _sigmoid(og)
    silu_og = og * sg_og
    oan = on * gna4
    oa = oan * silu_og
    ug, tu = _gelu(u)
    vg, tv = _gelu(v)
    mu = jnp.mean(vg, axis=-1, keepdims=True)
    vc = vg - mu
    rstd = lax.rsqrt(jnp.mean(vc * vc, axis=-1, keepdims=True) + EPS)
    vhat = vc * rstd
    vln = vhat * ln_g + ln_b
    blocks = []
    for n in range(tm // SGU_BLOCK):
        rs = slice(n * SGU_BLOCK, (n + 1) * SGU_BLOCK)
        blocks.append(jnp.concatenate(
            [_dot(ws_ref[g], vln[rs, g * DK:(g + 1) * DK]) + bs_ref[g] for g in range(GROUPS)], axis=1))
    mixed = jnp.concatenate(blocks, axis=0) if len(blocks) > 1 else blocks[0]
    obm = ug * mixed
    pa = _dot(oa, wpaT_ref[...], "nt")
    pb = _dot(obm, wpbT_ref[...], "nt")
    sga, sgb = _sigmoid(ga), _sigmoid(gb)
    merged = sga * pa + sgb * pb
    return dict(r=r, on=on, sg_og=sg_og, silu_og=silu_og, oan=oan, oa=oa, ug=ug, tu=tu, tv=tv, rstd=rstd, vhat=vhat,
                vln=vln, mixed=mixed, obm=obm, pa=pa, pb=pb, sga=sga, sgb=sgb, merged=merged, gna4=gna4)


def _tail_in_specs(tm):
    tile = lambda c: pl.BlockSpec((tm, KW), lambda i: (i, c))
    return [tile(c) for c in range(4, 11)]


def _tail_weight_specs():
    return [_full((1, DK)), _full((1, KW)), _full((1, KW)), _full((GROUPS, SGU_BLOCK, SGU_BLOCK)),
            _full((GROUPS, SGU_BLOCK, 1)), _full((D, KW), single=True), _full((D, KW), single=True),
            _full((D, D), single=True)]


def _read_tail_inputs(of_ref, ob_ref, pcols):
    osum = of_ref[...] + ob_ref[...]
    og, u, v = (pcols[j][...].astype(F32) for j in range(3))
    ga = jnp.concatenate([pcols[3][...], pcols[4][...]], axis=1).astype(F32)
    gb = jnp.concatenate([pcols[5][...], pcols[6][...]], axis=1).astype(F32)
    return osum, og, u, v, ga, gb


def _tail_fwd(p, o_up, o_down, xt, modv, gna, ln_g, ln_b, w_s, b_s, w_paT, w_pbT, w_o, rows_per_example):
    rows = xt.shape[0]
    tm = min(TOKEN_TILE, rows_per_example)
    per_b = rows_per_example // tm

    def body(of_ref, ob_ref, *rest):
        pcols = rest[:7]
        (x_ref, mod_ref, gna_ref, lng_ref, lnb_ref, ws_ref, bs_ref, wpaT_ref, wpbT_ref, wo_ref,
         x1_ref, mix_ref, merged_ref, oa_ref, obm_ref) = rest[7:]
        t = _tail_forward(*_read_tail_inputs(of_ref, ob_ref, pcols), gna_ref[...], lng_ref[...], lnb_ref[...],
                          ws_ref, bs_ref, wpaT_ref, wpbT_ref)
        mix = _dot(t["merged"], wo_ref[...])
        x1_ref[...] = x_ref[...] + mod_ref[0, 2:3, :] * mix
        mix_ref[...] = mix.astype(mix_ref.dtype)
        merged_ref[...] = t["merged"].astype(merged_ref.dtype)
        oa_ref[...] = t["oa"].astype(oa_ref.dtype)
        obm_ref[...] = t["obm"].astype(obm_ref.dtype)

    row = lambda w: pl.BlockSpec((tm, w), lambda i: (i, 0))
    in_specs = [row(KW), row(KW)] + _tail_in_specs(tm) + [row(D), pl.BlockSpec((1, N_MOD, D), lambda i: (i // per_b, 0, 0))]
    in_specs += _tail_weight_specs()
    return pl.pallas_call(
        body, name="tail_fwd", grid=(rows // tm,),
        out_shape=(jax.ShapeDtypeStruct((rows, D), F32), jax.ShapeDtypeStruct((rows, D), MXU_DTYPE),
                   jax.ShapeDtypeStruct((rows, D), MXU_DTYPE), jax.ShapeDtypeStruct((rows, KW), MXU_DTYPE),
                   jax.ShapeDtypeStruct((rows, KW), MXU_DTYPE)),
        in_specs=in_specs, out_specs=(row(D), row(D), row(D), row(KW), row(KW)),
        compiler_params=_params(("arbitrary",)),
    )(o_up, o_down, *([p] * 7), xt, modv, gna, ln_g, ln_b, w_s, b_s, w_paT, w_pbT, w_o)


def _tail_bwd(p, o_up, o_down, dx1, mix, modv, gna, ln_g, ln_b, w_s, b_s, w_paT, w_pbT, w_o, rows_per_example,
              after=None, sender=None):
    rows = dx1.shape[0]
    nb_ex = rows // rows_per_example
    tm = min(TOKEN_TILE, rows_per_example)
    per_b = rows_per_example // tm

    def body(of_ref, ob_ref, *rest):
        pcols = rest[:7]
        (dx1_ref, mix_ref, mod_ref, gna_ref, lng_ref, lnb_ref, ws_ref, bs_ref, wpaT_ref, wpbT_ref, wo_ref,
         dpt_ref, do_ref, dmix_ref, dpa_ref, dpb_ref, dmod_ref, small_ref, dws_ref, dbs_ref) = rest[7:]
        i = pl.program_id(0)

        @pl.when(i == 0)
        def _():
            small_ref[...] = jnp.zeros_like(small_ref)
            dws_ref[...] = jnp.zeros_like(dws_ref)
            dbs_ref[...] = jnp.zeros_like(dbs_ref)

        @pl.when(i % per_b == 0)
        def _():
            dmod_ref[...] = jnp.zeros_like(dmod_ref)

        osum, og, u, v, ga, gb = _read_tail_inputs(of_ref, ob_ref, pcols)
        ln_g = lng_ref[...]
        t = _tail_forward(osum, og, u, v, ga, gb, gna_ref[...], ln_g, lnb_ref[...], ws_ref, bs_ref, wpaT_ref, wpbT_ref)
        dx1v = dx1_ref[...]
        dmod_ref[0, 2:3, :] += jnp.sum(dx1v * mix_ref[...].astype(F32), axis=0, keepdims=True)
        dmix = dx1v * mod_ref[0, 2:3, :]
        dmix_ref[...] = dmix.astype(dmix_ref.dtype)
        dmerged = _dot(dmix, wo_ref[...], "nt")
        sga, sgb = t["sga"], t["sgb"]
        dpa = dmerged * sga
        dpb = dmerged * sgb
        dpa_ref[...] = dpa.astype(dpa_ref.dtype)
        dpb_ref[...] = dpb.astype(dpb_ref.dtype)
        dga = dmerged * t["pa"] * sga * (1.0 - sga)
        dgb = dmerged * t["pb"] * sgb * (1.0 - sgb)
        doa = _dot(dpa, wpaT_ref[...])
        dobm = _dot(dpb, wpbT_ref[...])
        dug = dobm * t["mixed"]
        dmixed = dobm * t["ug"]
        du = dug * _gelu_grad(u, t["tu"])
        dvln_blocks = []
        for n in range(tm // SGU_BLOCK):
            rs = slice(n * SGU_BLOCK, (n + 1) * SGU_BLOCK)
            parts = []
            for g in range(GROUPS):
                gs = slice(g * DK, (g + 1) * DK)
                dm = dmixed[rs, gs]
                parts.append(_dot(ws_ref[g], dm, "tn"))
                dws_ref[g] += _dot(dm, t["vln"][rs, gs], "nt")
                dbs_ref[g] += jnp.sum(dm, axis=1, keepdims=True)
            dvln_blocks.append(jnp.concatenate(parts, axis=1))
        dvln = jnp.concatenate(dvln_blocks, axis=0) if len(dvln_blocks) > 1 else dvln_blocks[0]
        vhat = t["vhat"]
        small_ref[1:2, 0:KW] += jnp.sum(dvln * vhat, axis=0, keepdims=True)
        small_ref[2:3, 0:KW] += jnp.sum(dvln, axis=0, keepdims=True)
        dvhat = dvln * ln_g
        dvg = t["rstd"] * (dvhat - jnp.mean(dvhat, axis=-1, keepdims=True)
                           - vhat * jnp.mean(dvhat * vhat, axis=-1, keepdims=True))
        dv = dvg * _gelu_grad(v, t["tv"])
        sg_og = t["sg_og"]
        doan = doa * t["silu_og"]
        dog = doa * t["oan"] * (sg_og * (1.0 + og * (1.0 - sg_og)))
        prod = doan * t["on"]
        dgna = jnp.zeros((1, DK), F32)
        for h in range(HEADS):
            dgna = dgna + jnp.sum(prod[:, h * DK:(h + 1) * DK], axis=0, keepdims=True)
        small_ref[0:1, 0:DK] += dgna
        don = doan * t["gna4"]
        dot_parts = []
        for h in range(HEADS):
            hs = slice(h * DK, (h + 1) * DK)
            m = jnp.mean(don[:, hs] * t["on"][:, hs], axis=-1, keepdims=True)
            dot_parts.append(t["r"][:, hs] * (don[:, hs] - t["on"][:, hs] * m))
        do_ref[...] = jnp.concatenate(dot_parts, axis=1).astype(do_ref.dtype)
        for j, val in enumerate((dog, du, dv)):
            dpt_ref[:, j * KW:(j + 1) * KW] = val.astype(dpt_ref.dtype)
        dpt_ref[:, 3 * KW:3 * KW + D] = dga.astype(dpt_ref.dtype)
        dpt_ref[:, 3 * KW + D:] = dgb.astype(dpt_ref.dtype)

    row = lambda w: pl.BlockSpec((tm, w), lambda i: (i, 0))
    in_specs = [row(KW), row(KW)] + _tail_in_specs(tm) + [row(D), row(D), pl.BlockSpec((1, N_MOD, D), lambda i: (i // per_b, 0, 0))]
    in_specs += _tail_weight_specs()
    args = [o_up, o_down, *([p] * 7), dx1, mix, modv, gna, ln_g, ln_b, w_s, b_s, w_paT, w_pbT, w_o]
    cd = MXU_DTYPE
    res, handle = _host_call(
        body, "tail_bwd", (rows // tm,), in_specs, args,
        [jax.ShapeDtypeStruct((rows, TAIL_COLS), cd), jax.ShapeDtypeStruct((rows, KW), cd),
         jax.ShapeDtypeStruct((rows, D), cd), jax.ShapeDtypeStruct((rows, D), cd),
         jax.ShapeDtypeStruct((rows, D), cd), jax.ShapeDtypeStruct((nb_ex, 8, D), F32),
         jax.ShapeDtypeStruct((8, D), F32), jax.ShapeDtypeStruct((GROUPS, SGU_BLOCK, SGU_BLOCK), F32),
         jax.ShapeDtypeStruct((GROUPS, SGU_BLOCK, 1), F32)],
        [row(TAIL_COLS), row(KW), row(D), row(D), row(D),
         pl.BlockSpec((1, 8, D), lambda i: (i // per_b, 0, 0)), _full((8, D)),
         _full((GROUPS, SGU_BLOCK, SGU_BLOCK)), _full((GROUPS, SGU_BLOCK, 1))], [],
        after=after, sender=sender)
    return (*res, handle)


def _ffn(x1, target, modv, g_ffn, g_final, w_upT, w_down, rows_per_example):
    rows = x1.shape[0]
    nb_ex = rows // rows_per_example
    tm = min(TOKEN_TILE, rows_per_example)
    per_b = rows_per_example // tm
    n_ff = D_FF // FF_CHUNK

    def body(x1_ref, tgt_ref, mod_ref, gffn_ref, gfin_ref, wup_ref, wdn_ref,
             dx1_ref, h2_ref, dffn_ref, act_ref, dup_ref, dmod_ref, small_ref, a_scr, b_scr):
        i = pl.program_id(0)

        @pl.when(i == 0)
        def _():
            small_ref[...] = jnp.zeros_like(small_ref)

        @pl.when(i % per_b == 0)
        def _():
            dmod_ref[...] = jnp.zeros_like(dmod_ref)

        x1v = x1_ref[...]
        g2 = gffn_ref[...]
        m3, m4, m5 = mod_ref[0, 3:4, :], mod_ref[0, 4:5, :], mod_ref[0, 5:6, :]
        r2 = lax.rsqrt(jnp.mean(x1v * x1v, axis=-1, keepdims=True) + EPS)
        xn2 = x1v * r2
        h2 = (xn2 * g2) * (1.0 + m4) + m3
        h2b = h2.astype(MXU_DTYPE)
        h2_ref[...] = h2b
        for j in range(n_ff):
            cs = slice(j * FF_CHUNK, (j + 1) * FF_CHUNK)
            a = _dot(h2b, wup_ref[j * FF_CHUNK:(j + 1) * FF_CHUNK, :], "nt")
            bgate = _dot(h2b, wup_ref[D_FF + j * FF_CHUNK:D_FF + (j + 1) * FF_CHUNK, :], "nt")
            a_scr[:, cs] = a
            b_scr[:, cs] = bgate
            act_ref[:, cs] = (a * _sigmoid(a) * bgate).astype(MXU_DTYPE)
        ffn = _dot(act_ref[...], wdn_ref[...])
        x2 = x1v + m5 * ffn
        r3 = lax.rsqrt(jnp.mean(x2 * x2, axis=-1, keepdims=True) + EPS)
        xn3 = x2 * r3
        gf = gfin_ref[...]
        err = xn3 * gf - tgt_ref[...]
        loss = 0.5 * jnp.sum(jnp.mean(err * err, axis=-1, keepdims=True), axis=0, keepdims=True)
        small_ref[2:3, :] += jnp.broadcast_to(loss, (1, D))
        dy = err * (1.0 / D)
        small_ref[1:2, :] += jnp.sum(dy * xn3, axis=0, keepdims=True)
        dxn3 = dy * gf
        dx2 = r3 * (dxn3 - xn3 * jnp.mean(dxn3 * xn3, axis=-1, keepdims=True))
        dmod_ref[0, 5:6, :] += jnp.sum(dx2 * ffn, axis=0, keepdims=True)
        dffn = (dx2 * m5).astype(MXU_DTYPE)
        dffn_ref[...] = dffn
        for j in range(n_ff):
            cs = slice(j * FF_CHUNK, (j + 1) * FF_CHUNK)
            dact = _dot(dffn, wdn_ref[cs, :], "nt")
            a, bgate = a_scr[:, cs], b_scr[:, cs]
            s = _sigmoid(a)
            dup_ref[:, cs] = (dact * bgate * (s * (1.0 + a * (1.0 - s)))).astype(MXU_DTYPE)
            dup_ref[:, D_FF + j * FF_CHUNK:D_FF + (j + 1) * FF_CHUNK] = (dact * a * s).astype(MXU_DTYPE)
        dh2 = _dot(dup_ref[...], wup_ref[...])
        dmod_ref[0, 3:4, :] += jnp.sum(dh2, axis=0, keepdims=True)
        dmod_ref[0, 4:5, :] += jnp.sum(dh2 * xn2 * g2, axis=0, keepdims=True)
        small_ref[0:1, :] += jnp.sum(dh2 * (1.0 + m4) * xn2, axis=0, keepdims=True)
        dxn2 = dh2 * g2 * (1.0 + m4)
        dx1_ref[...] = dx2 + r2 * (dxn2 - xn2 * jnp.mean(dxn2 * xn2, axis=-1, keepdims=True))

    row = lambda w: pl.BlockSpec((tm, w), lambda i: (i, 0))
    cd = MXU_DTYPE
    return pl.pallas_call(
        body, name="ffn_fwd_bwd", grid=(rows // tm,),
        out_shape=(jax.ShapeDtypeStruct((rows, D), F32), jax.ShapeDtypeStruct((rows, D), cd),
                   jax.ShapeDtypeStruct((rows, D), cd), jax.ShapeDtypeStruct((rows, D_FF), cd),
                   jax.ShapeDtypeStruct((rows, 2 * D_FF), cd), jax.ShapeDtypeStruct((nb_ex, 8, D), F32),
                   jax.ShapeDtypeStruct((8, D), F32)),
        in_specs=[row(D), row(D), pl.BlockSpec((1, N_MOD, D), lambda i: (i // per_b, 0, 0)), _full((1, D)), _full((1, D)),
                  _full((2 * D_FF, D), single=True), _full((D_FF, D), single=True)],
        out_specs=(row(D), row(D), row(D), row(D_FF), row(2 * D_FF),
                   pl.BlockSpec((1, 8, D), lambda i: (i // per_b, 0, 0)), _full((8, D))),
        scratch_shapes=[pltpu.VMEM((tm, D_FF), F32), pltpu.VMEM((tm, D_FF), F32)],
        compiler_params=_params(("arbitrary",)),
    )(x1, target, modv, g_ffn, g_final, w_upT, w_down)


def _inproj_bwd(pieces, dpt, xt, dx1, modv, g, w_inT, rows_per_example, name):
    rows = xt.shape[0]
    latent = dx1 is not None
    n_cols = IN_COLS if latent else CTX_COLS
    tm = min(TOKEN_TILE, rows_per_example)
    per_b = rows_per_example // tm
    n_mod_blocks = rows // rows_per_example if latent else 1
    n_pieces = len(pieces)

    def body(*refs):
        it = iter(refs)
        pc = [next(it) for _ in range(n_pieces)]
        dpt_ref = next(it) if latent else None
        x_ref = next(it)
        dx1_ref = next(it) if latent else None
        mod_ref, g_ref, w_ref = next(it), next(it), next(it)
        gx_ref = next(it) if latent else None
        dp_ref, dmod_ref, small_ref = next(it), next(it), next(it)
        i = pl.program_id(0)

        @pl.when(i == 0)
        def _():
            small_ref[...] = jnp.zeros_like(small_ref)

        @pl.when((i % per_b == 0) if latent else (i == 0))
        def _():
            dmod_ref[...] = jnp.zeros_like(dmod_ref)

        cols = [pc[0][...], pc[1][...], pc[2][...] + pc[3][...]]
        if latent:
            cols.append(pc[4][...] + pc[5][...])
        for j, val in enumerate(cols):
            dp_ref[:, j * KW:(j + 1) * KW] = val.astype(MXU_DTYPE)
        if latent:
            dp_ref[:, 4 * KW:] = dpt_ref[...]
        dh = _dot(dp_ref[...], w_ref[...])
        x = x_ref[...]
        gv = g_ref[...]
        m1 = mod_ref[0, 1:2, :]
        r = lax.rsqrt(jnp.mean(x * x, axis=-1, keepdims=True) + EPS)
        xn = x * r
        dmod_ref[0, 0:1, :] += jnp.sum(dh, axis=0, keepdims=True)
        dmod_ref[0, 1:2, :] += jnp.sum(dh * xn * gv, axis=0, keepdims=True)
        small_ref[0:1, :] += jnp.sum(dh * (1.0 + m1) * xn, axis=0, keepdims=True)
        if latent:
            dxn = dh * gv * (1.0 + m1)
            gx_ref[...] = dx1_ref[...] + r * (dxn - xn * jnp.mean(dxn * xn, axis=-1, keepdims=True))

    row = lambda w: pl.BlockSpec((tm, w), lambda i: (i, 0))
    mod_idx = (lambda i: (i // per_b, 0, 0)) if latent else (lambda i: (0, 0, 0))
    in_specs = [row(KW)] * n_pieces + ([row(TAIL_COLS)] if latent else []) + [row(D)] + ([row(D)] if latent else [])
    in_specs += [pl.BlockSpec((1, N_MOD, D), mod_idx), _full((1, D)),
                 pl.BlockSpec((n_cols, D), lambda i: (0, 0), pipeline_mode=pl.Buffered(1))]
    args = list(pieces) + ([dpt] if latent else []) + [xt] + ([dx1] if latent else []) + [modv, g, w_inT]
    out_shape = ([jax.ShapeDtypeStruct((rows, D), F32)] if latent else []) + [
        jax.ShapeDtypeStruct((rows, n_cols), MXU_DTYPE), jax.ShapeDtypeStruct((n_mod_blocks, 8, D), F32),
        jax.ShapeDtypeStruct((8, D), F32)]
    out_specs = ([row(D)] if latent else []) + [row(n_cols), pl.BlockSpec((1, 8, D), mod_idx), _full((8, D))]
    return pl.pallas_call(
        body, name=name, grid=(rows // tm,), out_shape=out_shape, in_specs=in_specs, out_specs=out_specs,
        compiler_params=_params(("arbitrary",)),
    )(*args)


def _grad_matmul(a, b, name, init=None, tn=512, sender=None):
    rows, n = a.shape
    k = b.shape[1]
    tn = min(tn, n)
    has_init = init is not None
    init_blocks = init.shape[0] // tn if has_init else 0

    def body(*refs):
        if has_init:
            a_ref, b_ref, init_ref, o_ref = refs
        else:
            a_ref, b_ref, o_ref = refs
        g = _dot(a_ref[...], b_ref[...], "tn")
        if has_init:
            g = g + jnp.where(pl.program_id(0) < init_blocks, init_ref[...].astype(F32), 0.0)
        o_ref[...] = g.astype(o_ref.dtype)

    in_specs = [pl.BlockSpec((rows, tn), lambda i: (0, i)), _full((rows, k), single=True)]
    args = [a, b]
    if has_init:
        in_specs.append(pl.BlockSpec((tn, k), lambda i: (jnp.minimum(i, init_blocks - 1), 0)))
        args.append(init)
    (out,), handle = _host_call(
        body, name, (n // tn,), in_specs, args, [jax.ShapeDtypeStruct((n, k), PAYLOAD_DTYPE)],
        [pl.BlockSpec((tn, k), lambda i: (i, 0))], [], sender=sender)
    return out, handle


def _row_tile(rows, limit=256):
    if rows <= limit:
        return rows
    for t in range(limit, 7, -8):
        if rows % t == 0:
            return t
    return rows


def _sum8(stack, name):
    _, rows, cols = stack.shape
    tr = _row_tile(rows)

    def body(s_ref, o_ref):
        acc = s_ref[0].astype(F32)
        for j in range(1, N_DEV):
            acc = acc + s_ref[j].astype(F32)
        o_ref[...] = acc

    return pl.pallas_call(
        body, name=name, grid=(rows // tr,), out_shape=jax.ShapeDtypeStruct((rows, cols), F32),
        in_specs=[pl.BlockSpec((N_DEV, tr, cols), lambda i: (0, i, 0))],
        out_specs=pl.BlockSpec((tr, cols), lambda i: (i, 0)),
        compiler_params=_params(("arbitrary",)),
    )(stack)


def _small_reduce(stack, gam, nb_ex):
    def body(s_ref, gam_ref, o_ref, bm_ref):
        acc = s_ref[0]
        for j in range(1, N_DEV):
            acc = acc + s_ref[j]
        o_ref[...] = acc
        bm = acc[8:8 + N_MOD, :]
        for e in range(nb_ex):
            bm = bm + acc[16 + e * N_MOD:16 + (e + 1) * N_MOD, :]
        lb = jnp.concatenate([_lower_bound(gam_ref, 0), _lower_bound(gam_ref, 1)], axis=1)
        dgam = acc[7:8, :] * lb * (1.0 - lb)
        bm_ref[...] = jnp.concatenate([bm, dgam, -dgam], axis=0)

    return pl.pallas_call(
        body, name="small_reduce", grid=(1,),
        out_shape=(jax.ShapeDtypeStruct((SMALL_ROWS, D), F32), jax.ShapeDtypeStruct((8, D), F32)),
        in_specs=[_full((N_DEV, SMALL_ROWS, D)), _full((4, KW))], out_specs=(_full((SMALL_ROWS, D)), _full((8, D))),
        compiler_params=_params(("arbitrary",)),
    )(stack, gam)


def _adamw_update(w, gv, m, v):
    nm = ADAM_B1 * m + (1.0 - ADAM_B1) * gv
    nv = ADAM_B2 * v + (1.0 - ADAM_B2) * (gv * gv)
    m_hat = nm / (1.0 - ADAM_B1 ** ADAM_STEP)
    v_hat = nv / (1.0 - ADAM_B2 ** ADAM_STEP)
    return -ADAM_LR * (m_hat / (jnp.sqrt(v_hat) + ADAM_EPS) + ADAM_WD * w), nm, nv


def _adamw_sum8(stack, w, m, v, name):
    _, rows, cols = stack.shape
    tr = _row_tile(rows)

    def body(s_ref, w_ref, m_ref, v_ref, g_ref, d_ref, nm_ref, nv_ref):
        gv = s_ref[0].astype(F32)
        for j in range(1, N_DEV):
            gv = gv + s_ref[j].astype(F32)
        g_ref[...] = gv
        d_ref[...], nm_ref[...], nv_ref[...] = _adamw_update(w_ref[...], gv, m_ref[...], v_ref[...])

    blk = pl.BlockSpec((tr, cols), lambda i: (i, 0))
    sd = jax.ShapeDtypeStruct((rows, cols), F32)
    return pl.pallas_call(
        body, name=name, grid=(rows // tr,), out_shape=(sd, sd, sd, sd),
        in_specs=[pl.BlockSpec((N_DEV, tr, cols), lambda i: (0, i, 0)), blk, blk, blk], out_specs=(blk, blk, blk, blk),
        compiler_params=_params(("arbitrary",)),
    )(stack, w, m, v)


def _adamw(w, g, m, v, name):
    shape = w.shape
    cols = shape[-1]
    rows = 1
    for s in shape[:-1]:
        rows *= s
    tr = _row_tile(rows)

    def body(w_ref, g_ref, m_ref, v_ref, d_ref, nm_ref, nv_ref):
        gv = g_ref[...]
        nm = ADAM_B1 * m_ref[...] + (1.0 - ADAM_B1) * gv
        nv = ADAM_B2 * v_ref[...] + (1.0 - ADAM_B2) * (gv * gv)
        m_hat = nm / (1.0 - ADAM_B1 ** ADAM_STEP)
        v_hat = nv / (1.0 - ADAM_B2 ** ADAM_STEP)
        d_ref[...] = -ADAM_LR * (m_hat / (jnp.sqrt(v_hat) + ADAM_EPS) + ADAM_WD * w_ref[...])
        nm_ref[...] = nm
        nv_ref[...] = nv

    blk = pl.BlockSpec((tr, cols), lambda i: (i, 0))
    sd = jax.ShapeDtypeStruct((rows, cols), F32)
    d, nm, nv = pl.pallas_call(
        body, name=name, grid=(rows // tr,), out_shape=(sd, sd, sd), in_specs=[blk] * 4, out_specs=(blk, blk, blk),
        compiler_params=_params(("arbitrary",)),
    )(w.reshape(rows, cols), g.reshape(rows, cols), m.reshape(rows, cols), v.reshape(rows, cols))
    return d.reshape(shape), nm.reshape(shape), nv.reshape(shape)


def _owner_blocks(a):
    return a.reshape(N_DEV, a.shape[0] // N_DEV, a.shape[1])


class _LocalWeights:
    def __init__(self, w_upT, w_down, w_o, w_paT, w_pbT):
        self.weights = (w_upT, w_down, w_o, w_paT, w_pbT)
        self.items = {}

    def sender(self, stage, items=None):
        self.items[stage] = items
        return None

    def sent(self, stage, handle):
        pass

    def mixer_weights(self, after):
        return self.weights[1:]

    def ffn_weights(self, after):
        return self.weights[0]


def _local_step(x, ctx, target, modv, mcv, gam, g_mix, g_ffn, gna, ln_g, ln_b, w_s, b_s, g_final, w_inT, comm):
    nb_ex, seq, _ = x.shape
    ctx_len = ctx.shape[1]
    xt = x.reshape(nb_ex * seq, D)
    ct = ctx.reshape(nb_ex * ctx_len, D)
    tgt = target.reshape(nb_ex * seq, D)
    bs3 = b_s.reshape(GROUPS, SGU_BLOCK, 1)

    pc, hc, _ = _inproj(ct, mcv, g_mix, w_inT, CTX_COLS, ctx_len, "inproj_ctx")
    p, h, handle = _inproj(xt, modv, g_mix, w_inT, IN_COLS, seq, "inproj_lat", sender=comm.sender("inproj"))
    comm.sent("inproj", handle)
    cst_f, cst_b, s_ctx, _ = _hgrn_fwd(pc, gam, None, ctx_len, False, "hgrn_fwd_ctx")
    o_up, o_down, st_f, st_b, _, handle = _hgrn_fwd(p, gam, s_ctx, seq, True, "hgrn_fwd_lat",
                                                    sender=comm.sender("scan"))
    comm.sent("scan", handle)
    w_down, w_o, w_paT, w_pbT = comm.mixer_weights(o_up)
    x1, mix, merged, oa, obm = _tail_fwd(p, o_up, o_down, xt, modv, gna, ln_g, ln_b, w_s, bs3, w_paT, w_pbT, w_o, seq)
    w_upT = comm.ffn_weights(x1)
    dx1, h2, dffn, act, dup, dmod_ffn, small_ffn = _ffn(x1, tgt, modv, g_ffn, g_final, w_upT, w_down, seq)
    dpt, do, dmix, dpa, dpb, dmod_tail, small_tail, dws, dbs, _ = _tail_bwd(
        p, o_up, o_down, dx1, mix, modv, gna, ln_g, ln_b, w_s, bs3, w_paT, w_pbT, w_o, seq)
    dzf, dvf, dqf, dzb, dvb, dqb, dlb, ds0, _ = _hgrn_bwd(p, gam, do, st_f, st_b, None, seq, True, "hgrn_bwd_lat")
    czf, cvf, czb, cvb, dlb_c, _, _ = _hgrn_bwd(pc, gam, None, cst_f, cst_b, ds0, ctx_len, False, "hgrn_bwd_ctx")
    grad_x, dp, dmod_in, small_in = _inproj_bwd([dzf, dzb, dvf, dvb, dqf, dqb], dpt, xt, dx1, modv, g_mix, w_inT,
                                                 seq, "inproj_bwd_lat")
    dpc, dmc, small_c = _inproj_bwd([czf, czb, cvf, cvb], None, ct, None, mcv, g_mix, w_inT, ctx_len, "inproj_bwd_ctx")

    z = lambda r: jnp.zeros((r, D), F32)
    pad = lambda a: jnp.pad(a, ((0, 0), (0, D - a.shape[1])))
    dlb_row = (dlb + dlb_c).reshape(1, 2 * KW)
    dmod = dmod_in + dmod_tail + dmod_ffn
    small = jnp.concatenate([
        small_in[0:1] + small_c[0:1],
        small_ffn[0:1],
        small_ffn[1:2],
        small_tail[0:1],
        small_tail[1:2],
        small_tail[2:3],
        pad(dbs.reshape(1, GROUPS * SGU_BLOCK)),
        dlb_row,
        dmc[0, 0:N_MOD],
        small_ffn[2:3],
        z(1),
        dmod[:, 0:N_MOD].reshape(nb_ex * N_MOD, D),
        z(24 - nb_ex * N_MOD),
        dws.reshape(GROUPS * SGU_BLOCK * SGU_BLOCK // D, D),
    ], axis=0)

    chain = [("gw_in", dp, h, dict(init=_grad_matmul(dpc, hc, "gw_in_ctx")[0])), ("gw_up", dup, h2, {}),
             ("gw_down", act, dffn, dict(tn=256)), ("gw_o", merged, dmix, {}), ("gw_pb", dpb, obm, {}),
             ("gw_pa", dpa, oa, {})]
    items = [(small, "gather")]
    for name, a, b, kw in chain:
        g, handle = _grad_matmul(a, b, name, sender=comm.sender(name, items), **kw)
        comm.sent(name, handle)
        items = [(_owner_blocks(g), "scatter")]
    comm.sender("last", items)
    return grad_x.reshape(x.shape)


def kernel(x, c, ctx, c_ctx, w_mod, b_mod, g_mix, g_ffn, w_in, lb_gamma, g_norm_a, ln_v_g, ln_v_b, w_s, b_s, w_pa, w_pb, w_o, w_up, w_down, g_final, loss_target, m_c_ctx, m_w_mod, m_b_mod, m_g_mix, m_g_ffn, m_w_in, m_lb_gamma, m_g_norm_a, m_ln_v_g, m_ln_v_b, m_w_s, m_b_s, m_w_pa, m_w_pb, m_w_o, m_w_up, m_w_down, m_g_final, v_c_ctx, v_w_mod, v_b_mod, v_g_mix, v_g_ffn, v_w_in, v_lb_gamma, v_g_norm_a, v_ln_v_g, v_ln_v_b, v_w_s, v_b_s, v_w_pa, v_w_pb, v_w_o, v_w_up, v_w_down, v_g_final):
    nb_ex = x.shape[0]
    me = 4 * lax.axis_index("x") + 2 * lax.axis_index("y") + lax.axis_index("c")
    cd = MXU_DTYPE
    mod_cols = w_mod.shape[2]
    lb_cols = lb_gamma.shape[2]

    w_inT_l = w_in[0].T.astype(cd)
    w_upT_l = w_up[0].T.astype(cd)
    w_paT_l = w_pa[0].T.astype(cd)
    w_pbT_l = w_pb[0].T.astype(cd)
    cl = jnp.concatenate([c, jnp.pad(lb_gamma.reshape(1, 4 * lb_cols), ((0, 0), (0, D - 4 * lb_cols))),
                          jnp.zeros((8 - nb_ex - 1, D), F32)], axis=0)
    g_in, g_cl = _gather_two_level([w_inT_l, cl], "gather_w_in")
    w_inT = g_in.reshape(IN_COLS, D)
    c_all = g_cl[:, 0:nb_ex].reshape(N_DEV * nb_ex, D)
    gam = jnp.transpose(g_cl[:, nb_ex, 0:4 * lb_cols].reshape(N_DEV, 4, lb_cols), (1, 0, 2)).reshape(4, KW)

    n_c = N_DEV * nb_ex
    cvec = jnp.concatenate([c_all, c_ctx.reshape(1, D), jnp.zeros((7, D), F32)], axis=0)
    b_mod_l = lax.dynamic_slice(b_mod, (0, me * mod_cols), (1, mod_cols))
    mod_l, svec = _mod_fwd(cvec, w_mod[0], b_mod_l)
    (g_mod,) = _exchange([(mod_l, "gather")], "gather_mod")
    mod_all = jnp.transpose(g_mod, (1, 0, 2)).reshape(n_c + 8, N_MOD * D)
    modv = lax.dynamic_slice(mod_all, (me * nb_ex, 0), (nb_ex, N_MOD * D)).reshape(nb_ex, N_MOD, D)
    mcv = mod_all[n_c].reshape(1, N_MOD, D)

    handles, leftover = {}, {}

    class Comm:
        def sender(self, stage, items=None):
            if stage == "inproj":
                return _Sender([(w_down[0].astype(cd), "gather"), (w_o[0].astype(cd), "gather"), (w_paT_l, "gather"),
                                (w_pbT_l, "gather")])
            if stage == "scan":
                return _Sender([(w_upT_l, "gather")])
            if stage == "last":
                leftover["items"] = items
                return None
            return _Sender(items)

        def sent(self, stage, handle):
            handles[stage] = handle

        def mixer_weights(self, after):
            g_down, g_o, g_pa, g_pb = _exchange_wait(handles["inproj"], after)
            return g_down.reshape(D_FF, D), g_o.reshape(D, D), g_pa.reshape(D, KW), g_pb.reshape(D, KW)

        def ffn_weights(self, after):
            (g_up,) = _exchange_wait(handles["scan"], after)
            return g_up.reshape(2 * D_FF, D)

    grad_x = _local_step(
        x, ctx, loss_target, modv, mcv, gam, g_mix, g_ffn, g_norm_a, ln_v_g, ln_v_b, w_s[0], b_s[0],
        g_final.reshape(1, D), w_inT, Comm())
    last, last_started = _exchange_start(leftover["items"], "scatter_last", after=leftover["items"][0][0])

    (r_small,) = _exchange_wait(handles["gw_in"], last_started)
    (r_in,) = _exchange_wait(handles["gw_up"], r_small)
    done = {"w_in": [a.T[None] for a in _adamw_sum8(r_in, w_in[0].T, m_w_in[0].T, v_w_in[0].T, "adamw_w_in")]}
    (r_up,) = _exchange_wait(handles["gw_down"], done["w_in"][1])
    done["w_up"] = [a.T[None] for a in _adamw_sum8(r_up, w_up[0].T, m_w_up[0].T, v_w_up[0].T, "adamw_w_up")]
    (r_down,) = _exchange_wait(handles["gw_o"], done["w_up"][1])
    done["w_down"] = [a[None] for a in _adamw_sum8(r_down, w_down[0], m_w_down[0], v_w_down[0], "adamw_w_down")]
    (r_o,) = _exchange_wait(handles["gw_pb"], done["w_down"][1])
    done["w_o"] = [a[None] for a in _adamw_sum8(r_o, w_o[0], m_w_o[0], v_w_o[0], "adamw_w_o")]
    (r_pb,) = _exchange_wait(handles["gw_pa"], done["w_o"][1])
    (r_pa,) = _exchange_wait(last, r_pb)
    grad_w_in, grad_w_up, grad_w_down, grad_w_o = (done[k][0] for k in ("w_in", "w_up", "w_down", "w_o"))
    grad_w_pa = _sum8(r_pa, "sum_w_pa").T[None]
    grad_w_pb = _sum8(r_pb, "sum_w_pb").T[None]
    tot, bm = _small_reduce(r_small, gam, nb_ex)
    loss = tot[14, 0]
    grad_g_mix, grad_g_ffn, grad_g_final = tot[0:1], tot[1:2], tot[2]
    grad_g_norm_a = tot[3:4, 0:DK]
    grad_ln_v_g, grad_ln_v_b = tot[4:5, 0:KW], tot[5:6, 0:KW]
    grad_b_s = tot[6, 0:GROUPS * SGU_BLOCK].reshape(1, GROUPS, SGU_BLOCK)
    grad_w_s = tot[40:104].reshape(1, GROUPS, SGU_BLOCK, SGU_BLOCK)
    grad_b_mod = bm[0:N_MOD].reshape(1, N_MOD * D)
    grad_lb_gamma = lax.dynamic_slice(bm[6:8].reshape(2, 2, KW), (0, 0, me * lb_cols), (2, 2, lb_cols))

    dmod_all = r_small[:, 16:16 + nb_ex * N_MOD].reshape(n_c, N_MOD * D)
    dmod_l = jnp.concatenate([lax.dynamic_slice(dmod_all, (0, me * mod_cols), (n_c, mod_cols)),
                              lax.dynamic_slice(tot[8:8 + N_MOD].reshape(1, N_MOD * D), (0, me * mod_cols), (1, mod_cols)),
                              jnp.zeros((7, mod_cols), F32)], axis=0)
    gw_mod, gc = _mod_bwd(svec, cvec, dmod_l, w_mod[0])
    grad_w_mod = gw_mod[None]
    (r_gc,) = _exchange([(gc[n_c:n_c + 8], "gather")], "gather_c_ctx", after=r_pa)
    grad_c_ctx = _sum8(r_gc, "sum_c_ctx")[0]

    names = ["c_ctx", "w_mod", "b_mod", "g_mix", "g_ffn", "w_in", "lb_gamma", "g_norm_a", "ln_v_g", "ln_v_b", "w_s",
             "b_s", "w_pa", "w_pb", "w_o", "w_up", "w_down", "g_final"]
    weights = [c_ctx, w_mod, b_mod, g_mix, g_ffn, w_in, lb_gamma, g_norm_a, ln_v_g, ln_v_b, w_s, b_s, w_pa, w_pb, w_o,
               w_up, w_down, g_final]
    grads = [grad_c_ctx, grad_w_mod, grad_b_mod, grad_g_mix, grad_g_ffn, grad_w_in, grad_lb_gamma, grad_g_norm_a,
             grad_ln_v_g, grad_ln_v_b, grad_w_s, grad_b_s, grad_w_pa, grad_w_pb, grad_w_o, grad_w_up, grad_w_down,
             grad_g_final]
    ms = [m_c_ctx, m_w_mod, m_b_mod, m_g_mix, m_g_ffn, m_w_in, m_lb_gamma, m_g_norm_a, m_ln_v_g, m_ln_v_b, m_w_s, m_b_s,
          m_w_pa, m_w_pb, m_w_o, m_w_up, m_w_down, m_g_final]
    vs = [v_c_ctx, v_w_mod, v_b_mod, v_g_mix, v_g_ffn, v_w_in, v_lb_gamma, v_g_norm_a, v_ln_v_g, v_ln_v_b, v_w_s, v_b_s,
          v_w_pa, v_w_pb, v_w_o, v_w_up, v_w_down, v_g_final]
    deltas, new_ms, new_vs = [], [], []
    for nm, w, g, m, v in zip(names, weights, grads, ms, vs):
        d, nm_, nv_ = done[nm][1:] if nm in done else _adamw(w, g.reshape(w.shape), m, v, "adamw_" + nm)
        deltas.append(d)
        new_ms.append(nm_)
        new_vs.append(nv_)
    grads = [g.reshape(w.shape) for g, w in zip(grads, weights)]
    return (loss, grad_x, *grads, *deltas, *new_ms, *new_vs)
```

```python
import functools

import jax
import jax.numpy as jnp
from jax import lax
from jax.experimental import pallas as pl
from jax.experimental.pallas import tpu as pltpu

F32 = jnp.float32
MXU_DTYPE = jnp.bfloat16
PAYLOAD_DTYPE = jnp.bfloat16

N_DEV = 8
D = 1024
HEADS = 4
DK = 128
KW = HEADS * DK
CHUNK = 64
SGU_BLOCK = 128
GROUPS = 4
D_FF = 2816
FF_CHUNK = 256
N_MOD = 6
IN_COLS = 5632
CTX_COLS = 1536
TAIL_COLS = IN_COLS - 4 * KW
EPS = 1e-6
ADAM_LR, ADAM_B1, ADAM_B2, ADAM_EPS, ADAM_WD, ADAM_STEP = 0.001, 0.9, 0.999, 1e-08, 0.01, 10

VMEM_LIMIT = 56 * 1024 * 1024
TOKEN_TILE = 256
SMALL_ROWS = 104


def _params(sem):
    return pltpu.CompilerParams(dimension_semantics=sem, vmem_limit_bytes=VMEM_LIMIT)


_DN = {"nn": (((1,), (0,)), ((), ())), "nt": (((1,), (1,)), ((), ())), "tn": (((0,), (0,)), ((), ()))}


def _dot(a, b, form="nn"):
    return lax.dot_general(a.astype(MXU_DTYPE), b.astype(MXU_DTYPE), _DN[form], preferred_element_type=F32)


def _mask_dot(mask, v):
    bf = jnp.bfloat16
    hi = v.astype(bf)
    r1 = v - hi.astype(F32)
    mid = r1.astype(bf)
    lo = (r1 - mid.astype(F32)).astype(bf)
    w = v.shape[1]
    s = lax.dot_general(mask.astype(bf), jnp.concatenate([hi, mid, lo], axis=1), _DN["nn"], preferred_element_type=F32)
    return (s[:, 2 * w:] + s[:, w:2 * w]) + s[:, :w]


def _full(shape, single=False):
    n = len(shape)
    if single:
        return pl.BlockSpec(shape, lambda *_: (0,) * n, pipeline_mode=pl.Buffered(1))
    return pl.BlockSpec(shape, lambda *_: (0,) * n)


def _ordered_behind(body, in_specs, args, after):
    if after is None:
        return body
    at = len(in_specs)
    in_specs.append(pl.BlockSpec(memory_space=pl.ANY))
    args.append(after)
    return lambda *refs: body(*refs[:at], *refs[at + 1:])


def _sigmoid(z):
    return 0.5 * jnp.tanh(0.5 * z) + 0.5


def _gelu(x):
    c = 0.7978845608028654
    t = jnp.tanh(c * (x + 0.044715 * x * x * x))
    return 0.5 * x * (1.0 + t), t


def _gelu_grad(x, t):
    c = 0.7978845608028654
    return 0.5 * (1.0 + t) + 0.5 * x * (1.0 - t * t) * c * (1.0 + 3 * 0.044715 * x * x)


def _exchange(items, name, after=None):
    n = len(items)
    out_shape = []
    for a, mode in items:
        blk = a.shape if mode == "gather" else a.shape[1:]
        out_shape.append(jax.ShapeDtypeStruct((N_DEV,) + tuple(blk), a.dtype))

    def body(*refs):
        srcs, dsts = refs[:n], refs[n:2 * n]
        send_sems, recv_sems, local_sems = refs[2 * n:]
        x, y, c = lax.axis_index("x"), lax.axis_index("y"), lax.axis_index("c")
        me = 4 * x + 2 * y + c

        def src_for(i, dev):
            return srcs[i] if items[i][1] == "gather" else srcs[i].at[dev]

        local = [pltpu.make_async_copy(src_for(i, me), dsts[i].at[me], local_sems.at[i]) for i in range(n)]
        for cp in local:
            cp.start()
        remote = []
        for k in range(1, N_DEV):
            px = jnp.bitwise_xor(x, (k >> 2) & 1)
            py = jnp.bitwise_xor(y, (k >> 1) & 1)
            pc = jnp.bitwise_xor(c, k & 1)
            peer = 4 * px + 2 * py + pc
            for i in range(n):
                cp = pltpu.make_async_remote_copy(
                    src_ref=src_for(i, peer), dst_ref=dsts[i].at[me],
                    send_sem=send_sems.at[i * (N_DEV - 1) + k - 1], recv_sem=recv_sems.at[i * (N_DEV - 1) + k - 1],
                    device_id=(px, py, pc), device_id_type=pl.DeviceIdType.MESH)
                cp.start()
                remote.append(cp)
        for cp in remote:
            cp.wait()
        for cp in local:
            cp.wait()

    any_spec = pl.BlockSpec(memory_space=pl.ANY)
    in_specs, args = [any_spec] * n, [a for a, _ in items]
    if after is not None:
        in_specs.append(any_spec)
        args.append(after)
        exchange = body
        body = lambda *refs: exchange(*refs[:n], *refs[n + 1:])
    return pl.pallas_call(
        body, name=name, out_shape=out_shape, in_specs=in_specs, out_specs=[any_spec] * n,
        scratch_shapes=[pltpu.SemaphoreType.DMA((n * (N_DEV - 1),)), pltpu.SemaphoreType.DMA((n * (N_DEV - 1),)),
                        pltpu.SemaphoreType.DMA((n,))],
    )(*args)


def _gather_two_level(arrays, name):
    n = len(arrays)

    def body(*refs):
        srcs, dsts = refs[:n], refs[n:2 * n]
        send_sems, recv_sems, local_sems = refs[2 * n:]
        x, y, c = lax.axis_index("x"), lax.axis_index("y"), lax.axis_index("c")
        sibling = (x, y, 1 - c)
        chips = [(1 - x, y), (x, 1 - y), (1 - x, 1 - y)]

        def slot(px, py, pc):
            return 4 * px + 2 * py + pc

        def copy(i, k, block, to, src=None):
            return pltpu.make_async_remote_copy(
                src_ref=dsts[i].at[slot(*block)] if src is None else src, dst_ref=dsts[i].at[slot(*block)],
                send_sem=send_sems.at[i * 7 + k], recv_sem=recv_sems.at[i * 7 + k],
                device_id=to, device_id_type=pl.DeviceIdType.MESH)

        me = (x, y, c)
        mine = [pltpu.make_async_copy(srcs[i], dsts[i].at[slot(*me)], local_sems.at[i]) for i in range(n)]
        for cp in mine:
            cp.start()
        first = []
        for j, chip in enumerate(chips):
            first += [copy(i, 1 + j, me, (*chip, c), src=srcs[i]) for i in range(n)]
        first += [copy(i, 0, me, sibling, src=srcs[i]) for i in range(n)]
        for cp in first:
            cp.start()
        passed = []
        for j, chip in enumerate(chips):
            for i in range(n):
                copy(i, 1 + j, (*chip, c), me).wait_recv()
                cp = copy(i, 4 + j, (*chip, c), sibling)
                cp.start()
                passed.append(cp)
        for i in range(n):
            copy(i, 0, sibling, me).wait_recv()
            for j, chip in enumerate(chips):
                copy(i, 4 + j, (*chip, 1 - c), me).wait_recv()
        for cp in first + passed:
            cp.wait_send()
        for cp in mine:
            cp.wait()

    any_spec = pl.BlockSpec(memory_space=pl.ANY)
    return pl.pallas_call(
        body, name=name, out_shape=[jax.ShapeDtypeStruct((N_DEV,) + a.shape, a.dtype) for a in arrays],
        in_specs=[any_spec] * n, out_specs=[any_spec] * n,
        scratch_shapes=[pltpu.SemaphoreType.DMA((n * 7,)), pltpu.SemaphoreType.DMA((n * 7,)),
                        pltpu.SemaphoreType.DMA((n,))],
    )(*arrays)


_HBM = pl.BlockSpec(memory_space=pltpu.HBM)
_SEM = pl.BlockSpec(memory_space=pltpu.SEMAPHORE)
_EFFECT = pltpu.SideEffectType.DATAFLOW_SIDE_EFFECTING


def _split_copies(items, srcs, lands, send_sems, recv_sems):
    x, y, c = lax.axis_index("x"), lax.axis_index("y"), lax.axis_index("c")
    me = 4 * x + 2 * y + c
    copies = []
    for k in range(1, N_DEV):
        px = jnp.bitwise_xor(x, (k >> 2) & 1)
        py = jnp.bitwise_xor(y, (k >> 1) & 1)
        pc = jnp.bitwise_xor(c, k & 1)
        peer = 4 * px + 2 * py + pc
        for i in range(len(items)):
            src = srcs[i] if items[i][1] == "gather" else srcs[i].at[peer]
            copies.append(pltpu.make_async_remote_copy(
                src_ref=src, dst_ref=lands[i].at[me],
                send_sem=send_sems.at[i * (N_DEV - 1) + k - 1], recv_sem=recv_sems.at[i * (N_DEV - 1) + k - 1],
                device_id=(px, py, pc), device_id_type=pl.DeviceIdType.MESH))
    return me, copies


def _exchange_start(items, name, after):
    n = len(items)
    n_sem = n * (N_DEV - 1)
    srcs, lands = [], []
    for a, mode in items:
        blk = a.shape if mode == "gather" else a.shape[1:]
        srcs.append(pltpu.with_memory_space_constraint(a, pltpu.HBM))
        lands.append(pltpu.with_memory_space_constraint(lax.empty((N_DEV,) + tuple(blk), a.dtype), pltpu.HBM))

    def body(*refs):
        src_refs, land_refs = refs[:n], refs[n:2 * n]
        send_sems, recv_sems = refs[2 * n + 1], refs[2 * n + 2]
        local_sems = refs[4 * n + 3]
        me, copies = _split_copies(items, src_refs, land_refs, send_sems, recv_sems)
        for i in range(n):
            own = src_refs[i] if items[i][1] == "gather" else src_refs[i].at[me]
            cp = pltpu.make_async_copy(own, land_refs[i].at[me], local_sems.at[i])
            cp.start()
            cp.wait()
        for cp in copies:
            cp.start()

    out_shape = [pltpu.SemaphoreType.DMA((n_sem,)), pltpu.SemaphoreType.DMA((n_sem,))]
    out_shape += [pltpu.HBM(a.shape, a.dtype) for a in srcs] + [pltpu.HBM(a.shape, a.dtype) for a in lands]
    outs = pl.pallas_call(
        body, name=name, out_shape=out_shape,
        in_specs=[_HBM] * (2 * n) + [pl.BlockSpec(memory_space=pl.ANY)],
        out_specs=[_SEM, _SEM] + [_HBM] * (2 * n),
        input_output_aliases={i: 2 + i for i in range(2 * n)},
        scratch_shapes=[pltpu.SemaphoreType.DMA((n,))],
        compiler_params=pltpu.CompilerParams(has_side_effects=_EFFECT),
    )(*srcs, *lands, after)
    handle = (items, name, outs[0], outs[1], outs[2:2 + n], outs[2 + n:2 + 2 * n])
    return handle, outs[2]


class _Sender:
    PIECE_ROWS = 352

    def __init__(self, items, chunks=None):
        self.items, self.n = items, len(items)
        self.chunks = chunks
        if chunks is None:
            block_rows = [a.shape[0] if mode == "gather" else a.shape[1] for a, mode in items]
            self.chunks = [r // self.PIECE_ROWS if r % self.PIECE_ROWS == 0 else 1 for r in block_rows]
        self.srcs, self.lands = [], []
        for a, mode in items:
            blk = a.shape if mode == "gather" else a.shape[1:]
            self.srcs.append(pltpu.with_memory_space_constraint(a, pltpu.HBM))
            self.lands.append(pltpu.with_memory_space_constraint(lax.empty((N_DEV,) + tuple(blk), a.dtype), pltpu.HBM))

    def issue(self, src_refs, land_refs, send_sems, recv_sems, local_sems, step, n_steps):
        x, y, c = lax.axis_index("x"), lax.axis_index("y"), lax.axis_index("c")
        me = 4 * x + 2 * y + c
        copies = []
        for ch in range(max(self.chunks)):
            for k in range(1, N_DEV):
                px = jnp.bitwise_xor(x, (k >> 2) & 1)
                py = jnp.bitwise_xor(y, (k >> 1) & 1)
                pc = jnp.bitwise_xor(c, k & 1)
                peer = 4 * px + 2 * py + pc
                for i, (_, mode) in enumerate(self.items):
                    if ch >= self.chunks[i]:
                        continue
                    n_rows = land_refs[i].shape[1] // self.chunks[i]
                    rows = pl.ds(ch * n_rows, n_rows)
                    src = src_refs[i].at[rows] if mode == "gather" else src_refs[i].at[peer].at[rows]
                    copies.append(pltpu.make_async_remote_copy(
                        src_ref=src, dst_ref=land_refs[i].at[me].at[rows],
                        send_sem=send_sems.at[i * (N_DEV - 1) + k - 1], recv_sem=recv_sems.at[i * (N_DEV - 1) + k - 1],
                        device_id=(px, py, pc), device_id_type=pl.DeviceIdType.MESH))
        own = [pltpu.make_async_copy(src_refs[i] if mode == "gather" else src_refs[i].at[me], land_refs[i].at[me],
                                     local_sems.at[i]) for i, (_, mode) in enumerate(self.items)]

        @pl.when(step == 0)
        def _():
            for cp in own:
                cp.start()

        for s in range(n_steps):
            group = [cp for j, cp in enumerate(copies) if (j * n_steps) // len(copies) == s]
            if group:
                @pl.when(step == s)
                def _(group=group):
                    for cp in group:
                        cp.start()

        @pl.when(step == n_steps - 1)
        def _():
            for cp in own:
                cp.wait()


def _host_call(body, name, grid, in_specs, args, out_shape, out_specs, scratch_shapes, after=None, sender=None):
    in_specs, args, out_shape, out_specs = list(in_specs), list(args), list(out_shape), list(out_specs)
    scratch_shapes = list(scratch_shapes)
    semantics = ("arbitrary",) * len(grid)
    body = _ordered_behind(body, in_specs, args, after)
    if sender is None:
        res = pl.pallas_call(body, name=name, grid=grid, in_specs=in_specs, out_specs=out_specs, out_shape=out_shape,
                             scratch_shapes=scratch_shapes, compiler_params=_params(semantics))(*args)
        return res, None
    n, n_in, n_out, n_scr = sender.n, len(in_specs), len(out_shape), len(scratch_shapes)
    n_sem = n * (N_DEV - 1)
    n_steps = 1
    for g in grid:
        n_steps *= g
    compute = body

    def body(*refs):
        ins, s_in = refs[:n_in], refs[n_in:n_in + 2 * n]
        o0 = n_in + 2 * n
        outs, s_out = refs[o0:o0 + n_out], refs[o0 + n_out:o0 + n_out + 2 + 2 * n]
        scr = refs[o0 + n_out + 2 + 2 * n:]
        compute(*ins, *outs, *scr[:n_scr])
        step = pl.program_id(0)
        for d in range(1, len(grid)):
            step = step * grid[d] + pl.program_id(d)
        sender.issue(s_in[:n], s_in[n:], s_out[0], s_out[1], scr[n_scr], step, n_steps)

    res = pl.pallas_call(
        body, name=name, grid=grid,
        in_specs=in_specs + [_HBM] * (2 * n), out_specs=out_specs + [_SEM, _SEM] + [_HBM] * (2 * n),
        out_shape=out_shape + [pltpu.SemaphoreType.DMA((n_sem,)), pltpu.SemaphoreType.DMA((n_sem,))]
        + [pltpu.HBM(a.shape, a.dtype) for a in sender.srcs] + [pltpu.HBM(a.shape, a.dtype) for a in sender.lands],
        input_output_aliases={n_in + j: n_out + 2 + j for j in range(2 * n)},
        scratch_shapes=scratch_shapes + [pltpu.SemaphoreType.DMA((n,))],
        compiler_params=pltpu.CompilerParams(dimension_semantics=semantics, vmem_limit_bytes=VMEM_LIMIT,
                                             has_side_effects=_EFFECT),
    )(*args, *sender.srcs, *sender.lands)
    handle = (sender.items, name, res[n_out], res[n_out + 1], res[n_out + 2:n_out + 2 + n],
              res[n_out + 2 + n:n_out + 2 + 2 * n])
    return res[:n_out], handle


def _exchange_wait(handle, after):
    items, name, send_sems, recv_sems, srcs, lands = handle
    n = len(items)

    def body(*refs):
        src_refs, land_refs = refs[:n], refs[n:2 * n]
        send_ref, recv_ref = refs[2 * n], refs[2 * n + 1]
        _, copies = _split_copies(items, src_refs, land_refs, send_ref, recv_ref)
        for cp in copies:
            cp.wait_send()
            cp.wait_recv()

    outs = pl.pallas_call(
        body, name=name + "_wait",
        out_shape=[pltpu.HBM(a.shape, a.dtype) for a in srcs] + [pltpu.HBM(a.shape, a.dtype) for a in lands],
        in_specs=[_HBM] * (2 * n) + [_SEM, _SEM, pl.BlockSpec(memory_space=pl.ANY)], out_specs=[_HBM] * (2 * n),
        input_output_aliases={i: i for i in range(2 * n)},
        compiler_params=pltpu.CompilerParams(has_side_effects=_EFFECT),
    )(*srcs, *lands, send_sems, recv_sems, after)
    return outs[n:]


def _mod_fwd(cvec, w_mod_l, b_mod_l):
    rows, cols = cvec.shape[0], w_mod_l.shape[1]

    def body(c_ref, w_ref, b_ref, o_ref, s_ref):
        cv = c_ref[...]
        s = cv * _sigmoid(cv)
        s_ref[...] = s
        o_ref[...] = _dot(s, w_ref[...]) + b_ref[...]

    return pl.pallas_call(
        body, name="mod_fwd",
        out_shape=(jax.ShapeDtypeStruct((rows, cols), F32), jax.ShapeDtypeStruct((rows, D), F32)),
        in_specs=[_full((rows, D)), _full((D, cols)), _full((1, cols))],
        out_specs=(_full((rows, cols)), _full((rows, D))), grid=(1,),
        compiler_params=_params(("arbitrary",)),
    )(cvec, w_mod_l, b_mod_l)


def _mod_bwd(svec, cvec, dmod_l, w_mod_l):
    rows, cols = dmod_l.shape

    def body(s_ref, c_ref, d_ref, w_ref, gw_ref, gc_ref):
        gw_ref[...] = _dot(s_ref[...], d_ref[...], "tn")
        cv = c_ref[...]
        sg = _sigmoid(cv)
        gc_ref[...] = _dot(d_ref[...], w_ref[...], "nt") * (sg * (1.0 + cv * (1.0 - sg)))

    return pl.pallas_call(
        body, name="mod_bwd",
        out_shape=(jax.ShapeDtypeStruct((D, cols), F32), jax.ShapeDtypeStruct((rows, D), F32)),
        in_specs=[_full((rows, D)), _full((rows, D)), _full((rows, cols)), _full((D, cols))],
        out_specs=(_full((D, cols)), _full((rows, D))), grid=(1,),
        compiler_params=_params(("arbitrary",)),
    )(svec, cvec, dmod_l, w_mod_l)


def _inproj(xt, modv, g, w_inT, n_cols, rows_per_example, name, after=None, sender=None):
    rows = xt.shape[0]
    tm = min(TOKEN_TILE, rows_per_example)
    per_b = rows_per_example // tm
    shared_mod = modv.shape[0] == 1

    def body(x_ref, mod_ref, g_ref, w_ref, p_ref, h_ref):
        x = x_ref[...]
        r = lax.rsqrt(jnp.mean(x * x, axis=-1, keepdims=True) + EPS)
        h = (x * r * g_ref[...]) * (1.0 + mod_ref[0, 1:2, :]) + mod_ref[0, 0:1, :]
        hb = h.astype(MXU_DTYPE)
        h_ref[...] = hb
        for j in range(n_cols // KW):
            p_ref[:, j * KW:(j + 1) * KW] = _dot(hb, w_ref[j * KW:(j + 1) * KW, :], "nt").astype(p_ref.dtype)

    mod_idx = (lambda i: (0, 0, 0)) if shared_mod else (lambda i: (i // per_b, 0, 0))
    in_specs = [pl.BlockSpec((tm, D), lambda i: (i, 0)), pl.BlockSpec((1, N_MOD, D), mod_idx), _full((1, D)),
                pl.BlockSpec((n_cols, D), lambda i: (0, 0), pipeline_mode=pl.Buffered(1))]
    (p, h), handle = _host_call(
        body, name, (rows // tm,), in_specs, [xt, modv, g, w_inT],
        [jax.ShapeDtypeStruct((rows, n_cols), MXU_DTYPE), jax.ShapeDtypeStruct((rows, D), MXU_DTYPE)],
        [pl.BlockSpec((tm, n_cols), lambda i: (i, 0)), pl.BlockSpec((tm, D), lambda i: (i, 0))], [],
        after=after, sender=sender)
    return p, h, handle


def _tri(reverse):
    row = lax.broadcasted_iota(jnp.int32, (CHUNK, CHUNK), 0)
    col = lax.broadcasted_iota(jnp.int32, (CHUNK, CHUNK), 1)
    return (col >= row) if reverse else (col <= row)


def _lower_bound(gam_ref, direction):
    return _sigmoid(gam_ref[direction:direction + 1, :] - gam_ref[2 + direction:3 + direction, :])


def _gate_prep(z, lb, tri):
    sg = _sigmoid(z)
    f = lb + (1.0 - lb) * sg
    g = jnp.log(f)
    b = _mask_dot(tri, g)
    bl = jnp.sum(g, axis=0, keepdims=True)
    return sg, g, 1.0 - f, b, bl


def _hgrn_fwd(p, gam, s0, rows_per_example, with_out, name, sender=None):
    rows = p.shape[0]
    nb_ex = rows // rows_per_example
    rb = min(TOKEN_TILE, rows_per_example)
    cpb = rb // CHUNK
    nb = rows_per_example // rb
    n_chunks = rows // CHUNK
    has_s0 = s0 is not None

    def body(*refs):
        it = iter(refs)
        gam_ref = next(it)
        zf_ref, vf_ref = next(it), next(it)
        qf_ref = next(it) if with_out else None
        zb_ref, vb_ref = next(it), next(it)
        qb_ref = next(it) if with_out else None
        s0_ref = next(it) if has_s0 else None
        if with_out:
            of_ref, ob_ref = next(it), next(it)
        stash_f, stash_b, fin_ref = next(it), next(it), next(it)
        st_ref = next(it)
        i = pl.program_id(1)

        @pl.when(i == 0)
        def _():
            if has_s0:
                st_ref[...] = s0_ref[:, 0]
            else:
                st_ref[...] = jnp.zeros_like(st_ref)

        for direction, (z_ref, v_ref, q_ref, stash) in enumerate(
                ((zf_ref, vf_ref, qf_ref, stash_f), (zb_ref, vb_ref, qb_ref, stash_b))):
            reverse = direction == 1
            tri = _tri(reverse)
            lb = _lower_bound(gam_ref, direction)
            order = range(cpb - 1, -1, -1) if reverse else range(cpb)
            for j in order:
                rs = slice(j * CHUNK, (j + 1) * CHUNK)
                z = z_ref[rs, :].astype(F32)
                v = v_ref[rs, :].astype(F32)
                _, _, k, b, bl = _gate_prep(z, lb, tri)
                mid = 0.5 * bl
                em = jnp.exp(mid)
                e2 = jnp.exp(mid - b)
                kd = k * (e2 * em)
                a = em * em
                if with_out:
                    q = q_ref[rs, :].astype(F32)
                    qi = q * jnp.exp(b - mid)
                    ki = k * e2
                    qe = qi * em
                for h in range(HEADS):
                    hs = slice(h * DK, (h + 1) * DK)
                    st = st_ref[direction, h]
                    stash[j, h] = st.astype(stash.dtype)
                    if with_out:
                        sc = jnp.where(tri, _dot(qi[:, hs], ki[:, hs], "nt"), 0.0)
                        o = _dot(sc, v[:, hs]) + _dot(qe[:, hs], st, "nt")
                        (ob_ref if reverse else of_ref)[rs, hs] = o
                    st_ref[direction, h] = st * a[:, hs] + _dot(v[:, hs], kd[:, hs], "tn")

        @pl.when(i == nb - 1)
        def _():
            fin_ref[:, 0] = st_ref[...]

    up = lambda b, i: b * nb + i
    down = lambda b, i: b * nb + nb - 1 - i
    col = lambda rowf, c: pl.BlockSpec((rb, KW), lambda b, i: (rowf(b, i), c))
    in_specs = [_full((4, KW)), col(up, 0), col(up, 2)] + ([col(up, 3)] if with_out else [])
    in_specs += [col(down, 1), col(down, 2)] + ([col(down, 3)] if with_out else [])
    args = [gam, p, p] + ([p] if with_out else []) + [p, p] + ([p] if with_out else [])
    if has_s0:
        in_specs.append(pl.BlockSpec((2, 1, HEADS, DK, DK), lambda b, i: (0, b, 0, 0, 0)))
        args.append(s0)
    out_shape, out_specs = [], []
    if with_out:
        out_shape += [jax.ShapeDtypeStruct((rows, KW), F32)] * 2
        out_specs += [pl.BlockSpec((rb, KW), lambda b, i: (up(b, i), 0)),
                      pl.BlockSpec((rb, KW), lambda b, i: (down(b, i), 0))]
    out_shape += [jax.ShapeDtypeStruct((n_chunks, HEADS, DK, DK), MXU_DTYPE)] * 2
    out_specs += [pl.BlockSpec((cpb, HEADS, DK, DK), lambda b, i: (up(b, i), 0, 0, 0)),
                  pl.BlockSpec((cpb, HEADS, DK, DK), lambda b, i: (down(b, i), 0, 0, 0))]
    out_shape.append(jax.ShapeDtypeStruct((2, nb_ex, HEADS, DK, DK), F32))
    out_specs.append(pl.BlockSpec((2, 1, HEADS, DK, DK), lambda b, i: (0, b, 0, 0, 0)))
    res, handle = _host_call(body, name, (nb_ex, nb), in_specs, args, out_shape, out_specs,
                             [pltpu.VMEM((2, HEADS, DK, DK), F32)], sender=sender)
    return (*res, handle)


def _hgrn_bwd(p, gam, do, stash_f, stash_b, ds_end, rows_per_example, with_out, name, after=None, sender=None):
    rows = p.shape[0]
    nb_ex = rows // rows_per_example
    rb = min(TOKEN_TILE, rows_per_example)
    cpb = rb // CHUNK
    nb = rows_per_example // rb
    has_end = ds_end is not None

    def body(*refs):
        it = iter(refs)
        gam_ref = next(it)
        ins = []
        for _ in range(2):
            z_ref, v_ref = next(it), next(it)
            q_ref = next(it) if with_out else None
            do_ref = next(it) if with_out else None
            ins.append((z_ref, v_ref, q_ref, do_ref, next(it)))
        end_ref = next(it) if has_end else None
        outs = []
        for _ in range(2):
            dz_ref, dv_ref = next(it), next(it)
            dq_ref = next(it) if with_out else None
            outs.append((dz_ref, dv_ref, dq_ref))
        dlb_ref, ds0_ref = next(it), next(it)
        dst_ref = next(it)
        b_id, i = pl.program_id(0), pl.program_id(1)

        @pl.when(i == 0)
        def _():
            if has_end:
                dst_ref[...] = end_ref[:, 0]
            else:
                dst_ref[...] = jnp.zeros_like(dst_ref)

        @pl.when((i == 0) & (b_id == 0))
        def _():
            dlb_ref[...] = jnp.zeros_like(dlb_ref)

        for direction in range(2):
            z_ref, v_ref, q_ref, do_ref, stash = ins[direction]
            dz_ref, dv_ref, dq_ref = outs[direction]
            reverse = direction == 1
            tri = _tri(reverse)
            tri_t = _tri(not reverse)
            lb = _lower_bound(gam_ref, direction)
            order = range(cpb) if reverse else range(cpb - 1, -1, -1)
            dlb_acc = jnp.zeros((1, KW), F32)
            for j in order:
                rs = slice(j * CHUNK, (j + 1) * CHUNK)
                z = z_ref[rs, :].astype(F32)
                v = v_ref[rs, :].astype(F32)
                sg, g, k, b, bl = _gate_prep(z, lb, tri)
                mid = 0.5 * bl
                em = jnp.exp(mid)
                e2 = jnp.exp(mid - b)
                e3 = e2 * em
                kd = k * e3
                a = em * em
                if with_out:
                    q = q_ref[rs, :].astype(F32)
                    dout = do_ref[rs, :].astype(F32)
                    e1 = jnp.exp(b - mid)
                    e4 = e1 * em
                    qi, ki, qe = q * e1, k * e2, q * e4
                dkd_p, dv_p, da_p, dqi_p, dki_p, dqe_p = [], [], [], [], [], []
                for h in range(HEADS):
                    hs = slice(h * DK, (h + 1) * DK)
                    st_in = stash[j, h]
                    dst = dst_ref[direction, h]
                    dkd_p.append(_dot(v[:, hs], dst))
                    dvh = _dot(kd[:, hs], dst, "nt")
                    da_p.append(jnp.sum(dst * st_in.astype(F32), axis=0, keepdims=True))
                    new_dst = dst * a[:, hs]
                    if with_out:
                        sc = jnp.where(tri, _dot(qi[:, hs], ki[:, hs], "nt"), 0.0)
                        dsc = jnp.where(tri, _dot(dout[:, hs], v[:, hs], "nt"), 0.0)
                        dqi_p.append(_dot(dsc, ki[:, hs]))
                        dki_p.append(_dot(dsc, qi[:, hs], "tn"))
                        dqe_p.append(_dot(dout[:, hs], st_in))
                        dvh = dvh + _dot(sc, dout[:, hs], "tn")
                        new_dst = new_dst + _dot(dout[:, hs], qe[:, hs], "tn")
                    dv_p.append(dvh)
                    dst_ref[direction, h] = new_dst
                cat = lambda parts: jnp.concatenate(parts, axis=1)
                dkd, da = cat(dkd_p), cat(da_p)
                dv_ref[rs, :] = cat(dv_p)
                t_kd = dkd * kd
                dk = dkd * e3
                db = -t_kd
                dbl = jnp.sum(t_kd, axis=0, keepdims=True) + da * a
                if with_out:
                    dqi, dki, dqe = cat(dqi_p), cat(dki_p), cat(dqe_p)
                    dq_ref[rs, :] = dqi * e1 + dqe * e4
                    dk = dk + dki * e2
                    t_qi, t_ki, t_qe = dqi * qi, dki * ki, dqe * qe
                    db = db + t_qi - t_ki + t_qe
                    dbl = dbl + 0.5 * jnp.sum(t_ki - t_qi, axis=0, keepdims=True)
                dg = _mask_dot(tri_t, db) + dbl
                df = dg * jnp.exp(-g) - dk
                dz_ref[rs, :] = df * (1.0 - lb) * sg * (1.0 - sg)
                dlb_acc = dlb_acc + jnp.sum(df * (1.0 - sg), axis=0, keepdims=True)
            dlb_ref[direction:direction + 1, :] += dlb_acc

        @pl.when(i == nb - 1)
        def _():
            ds0_ref[:, 0] = dst_ref[...]

    rows_of = (lambda b, i: b * nb + nb - 1 - i, lambda b, i: b * nb + i)
    in_specs, args = [_full((4, KW))], [gam]
    for direction in range(2):
        rf = rows_of[direction]
        col = lambda c, rf=rf: pl.BlockSpec((rb, KW), lambda b, i: (rf(b, i), c))
        in_specs += [col(direction), col(2)]
        args += [p, p]
        if with_out:
            in_specs += [col(3), col(0)]
            args += [p, do]
        in_specs.append(pl.BlockSpec((cpb, HEADS, DK, DK), lambda b, i, rf=rf: (rf(b, i), 0, 0, 0)))
        args.append((stash_f, stash_b)[direction])
    if has_end:
        in_specs.append(pl.BlockSpec((2, 1, HEADS, DK, DK), lambda b, i: (0, b, 0, 0, 0)))
        args.append(ds_end)
    out_shape, out_specs = [], []
    for direction in range(2):
        rf = rows_of[direction]
        n_out = 3 if with_out else 2
        out_shape += [jax.ShapeDtypeStruct((rows, KW), F32)] * n_out
        out_specs += [pl.BlockSpec((rb, KW), lambda b, i, rf=rf: (rf(b, i), 0))] * n_out
    out_shape += [jax.ShapeDtypeStruct((2, KW), F32), jax.ShapeDtypeStruct((2, nb_ex, HEADS, DK, DK), F32)]
    out_specs += [_full((2, KW)), pl.BlockSpec((2, 1, HEADS, DK, DK), lambda b, i: (0, b, 0, 0, 0))]
    res, handle = _host_call(body, name, (nb_ex, nb), in_specs, args, out_shape, out_specs,
                             [pltpu.VMEM((2, HEADS, DK, DK), F32)], after=after, sender=sender)
    return (*res, handle)


def _tail_forward(osum, og, u, v, ga, gb, gna, ln_g, ln_b, ws_ref, bs_ref, wpaT_ref, wpbT_ref):
    tm = osum.shape[0]
    gna4 = jnp.concatenate([gna] * HEADS, axis=1)
    r_parts = []
    for h in range(HEADS):
        oh = osum[:, h * DK:(h + 1) * DK]
        r_parts.append(jnp.broadcast_to(lax.rsqrt(jnp.mean(oh * oh, axis=-1, keepdims=True) + EPS), (tm, DK)))
    r = jnp.concatenate(r_parts, axis=1)
    on = osum * r
    sg_og = _sigmoid(og)
    silu_og = og * sg_og
    oan = on * gna4
    oa = oan * silu_og
    ug, tu = _gelu(u)
    vg, tv = _gelu(v)
    mu = jnp.mean(vg, axis=-1, keepdims=True)
    vc = vg - mu
    rstd = lax.rsqrt(jnp.mean(vc * vc, axis=-1, keepdims=True) + EPS)
    vhat = vc * rstd
    vln = vhat * ln_g + ln_b
    blocks = []
    for n in range(tm // SGU_BLOCK):
        rs = slice(n * SGU_BLOCK, (n + 1) * SGU_BLOCK)
        blocks.append(jnp.concatenate(
            [_dot(ws_ref[g], vln[rs, g * DK:(g + 1) * DK]) + bs_ref[g] for g in range(GROUPS)], axis=1))
    mixed = jnp.concatenate(blocks, axis=0) if len(blocks) > 1 else blocks[0]
    obm = ug * mixed
    pa = _dot(oa, wpaT_ref[...], "nt")
    pb = _dot(obm, wpbT_ref[...], "nt")
    sga, sgb = _sigmoid(ga), _sigmoid(gb)
    merged = sga * pa + sgb * pb
    return dict(r=r, on=on, sg_og=sg_og, silu_og=silu_og, oan=oan, oa=oa, ug=ug, tu=tu, tv=tv, rstd=rstd, vhat=vhat,
                vln=vln, mixed=mixed, obm=obm, pa=pa, pb=pb, sga=sga, sgb=sgb, merged=merged, gna4=gna4)


def _tail_in_specs(tm):
    tile = lambda c: pl.BlockSpec((tm, KW), lambda i: (i, c))
    return [tile(c) for c in range(4, 11)]


def _tail_weight_specs():
    return [_full((1, DK)), _full((1, KW)), _full((1, KW)), _full((GROUPS, SGU_BLOCK, SGU_BLOCK)),
            _full((GROUPS, SGU_BLOCK, 1)), _full((D, KW), single=True), _full((D, KW), single=True),
            _full((D, D), single=True)]


def _read_tail_inputs(of_ref, ob_ref, pcols):
    osum = of_ref[...] + ob_ref[...]
    og, u, v = (pcols[j][...].astype(F32) for j in range(3))
    ga = jnp.concatenate([pcols[3][...], pcols[4][...]], axis=1).astype(F32)
    gb = jnp.concatenate([pcols[5][...], pcols[6][...]], axis=1).astype(F32)
    return osum, og, u, v, ga, gb


def _tail_fwd(p, o_up, o_down, xt, modv, gna, ln_g, ln_b, w_s, b_s, w_paT, w_pbT, w_o, rows_per_example):
    rows = xt.shape[0]
    tm = min(TOKEN_TILE, rows_per_example)
    per_b = rows_per_example // tm

    def body(of_ref, ob_ref, *rest):
        pcols = rest[:7]
        (x_ref, mod_ref, gna_ref, lng_ref, lnb_ref, ws_ref, bs_ref, wpaT_ref, wpbT_ref, wo_ref,
         x1_ref, mix_ref, merged_ref, oa_ref, obm_ref) = rest[7:]
        t = _tail_forward(*_read_tail_inputs(of_ref, ob_ref, pcols), gna_ref[...], lng_ref[...], lnb_ref[...],
                          ws_ref, bs_ref, wpaT_ref, wpbT_ref)
        mix = _dot(t["merged"], wo_ref[...])
        x1_ref[...] = x_ref[...] + mod_ref[0, 2:3, :] * mix
        mix_ref[...] = mix.astype(mix_ref.dtype)
        merged_ref[...] = t["merged"].astype(merged_ref.dtype)
        oa_ref[...] = t["oa"].astype(oa_ref.dtype)
        obm_ref[...] = t["obm"].astype(obm_ref.dtype)

    row = lambda w: pl.BlockSpec((tm, w), lambda i: (i, 0))
    in_specs = [row(KW), row(KW)] + _tail_in_specs(tm) + [row(D), pl.BlockSpec((1, N_MOD, D), lambda i: (i // per_b, 0, 0))]
    in_specs += _tail_weight_specs()
    return pl.pallas_call(
        body, name="tail_fwd", grid=(rows // tm,),
        out_shape=(jax.ShapeDtypeStruct((rows, D), F32), jax.ShapeDtypeStruct((rows, D), MXU_DTYPE),
                   jax.ShapeDtypeStruct((rows, D), MXU_DTYPE), jax.ShapeDtypeStruct((rows, KW), MXU_DTYPE),
                   jax.ShapeDtypeStruct((rows, KW), MXU_DTYPE)),
        in_specs=in_specs, out_specs=(row(D), row(D), row(D), row(KW), row(KW)),
        compiler_params=_params(("arbitrary",)),
    )(o_up, o_down, *([p] * 7), xt, modv, gna, ln_g, ln_b, w_s, b_s, w_paT, w_pbT, w_o)


def _tail_bwd(p, o_up, o_down, dx1, mix, modv, gna, ln_g, ln_b, w_s, b_s, w_paT, w_pbT, w_o, rows_per_example,
              after=None, sender=None):
    rows = dx1.shape[0]
    nb_ex = rows // rows_per_example
    tm = min(TOKEN_TILE, rows_per_example)
    per_b = rows_per_example // tm

    def body(of_ref, ob_ref, *rest):
        pcols = rest[:7]
        (dx1_ref, mix_ref, mod_ref, gna_ref, lng_ref, lnb_ref, ws_ref, bs_ref, wpaT_ref, wpbT_ref, wo_ref,
         dpt_ref, do_ref, dmix_ref, dpa_ref, dpb_ref, dmod_ref, small_ref, dws_ref, dbs_ref) = rest[7:]
        i = pl.program_id(0)

        @pl.when(i == 0)
        def _():
            small_ref[...] = jnp.zeros_like(small_ref)
            dws_ref[...] = jnp.zeros_like(dws_ref)
            dbs_ref[...] = jnp.zeros_like(dbs_ref)

        @pl.when(i % per_b == 0)
        def _():
            dmod_ref[...] = jnp.zeros_like(dmod_ref)

        osum, og, u, v, ga, gb = _read_tail_inputs(of_ref, ob_ref, pcols)
        ln_g = lng_ref[...]
        t = _tail_forward(osum, og, u, v, ga, gb, gna_ref[...], ln_g, lnb_ref[...], ws_ref, bs_ref, wpaT_ref, wpbT_ref)
        dx1v = dx1_ref[...]
        dmod_ref[0, 2:3, :] += jnp.sum(dx1v * mix_ref[...].astype(F32), axis=0, keepdims=True)
        dmix = dx1v * mod_ref[0, 2:3, :]
        dmix_ref[...] = dmix.astype(dmix_ref.dtype)
        dmerged = _dot(dmix, wo_ref[...], "nt")
        sga, sgb = t["sga"], t["sgb"]
        dpa = dmerged * sga
        dpb = dmerged * sgb
        dpa_ref[...] = dpa.astype(dpa_ref.dtype)
        dpb_ref[...] = dpb.astype(dpb_ref.dtype)
        dga = dmerged * t["pa"] * sga * (1.0 - sga)
        dgb = dmerged * t["pb"] * sgb * (1.0 - sgb)
        doa = _dot(dpa, wpaT_ref[...])
        dobm = _dot(dpb, wpbT_ref[...])
        dug = dobm * t["mixed"]
        dmixed = dobm * t["ug"]
        du = dug * _gelu_grad(u, t["tu"])
        dvln_blocks = []
        for n in range(tm // SGU_BLOCK):
            rs = slice(n * SGU_BLOCK, (n + 1) * SGU_BLOCK)
            parts = []
            for g in range(GROUPS):
                gs = slice(g * DK, (g + 1) * DK)
                dm = dmixed[rs, gs]
                parts.append(_dot(ws_ref[g], dm, "tn"))
                dws_ref[g] += _dot(dm, t["vln"][rs, gs], "nt")
                dbs_ref[g] += jnp.sum(dm, axis=1, keepdims=True)
            dvln_blocks.append(jnp.concatenate(parts, axis=1))
        dvln = jnp.concatenate(dvln_blocks, axis=0) if len(dvln_blocks) > 1 else dvln_blocks[0]
        vhat = t["vhat"]
        small_ref[1:2, 0:KW] += jnp.sum(dvln * vhat, axis=0, keepdims=True)
        small_ref[2:3, 0:KW] += jnp.sum(dvln, axis=0, keepdims=True)
        dvhat = dvln * ln_g
        dvg = t["rstd"] * (dvhat - jnp.mean(dvhat, axis=-1, keepdims=True)
                           - vhat * jnp.mean(dvhat * vhat, axis=-1, keepdims=True))
        dv = dvg * _gelu_grad(v, t["tv"])
        sg_og = t["sg_og"]
        doan = doa * t["silu_og"]
        dog = doa * t["oan"] * (sg_og * (1.0 + og * (1.0 - sg_og)))
        prod = doan * t["on"]
        dgna = jnp.zeros((1, DK), F32)
        for h in range(HEADS):
            dgna = dgna + jnp.sum(prod[:, h * DK:(h + 1) * DK], axis=0, keepdims=True)
        small_ref[0:1, 0:DK] += dgna
        don = doan * t["gna4"]
        dot_parts = []
        for h in range(HEADS):
            hs = slice(h * DK, (h + 1) * DK)
            m = jnp.mean(don[:, hs] * t["on"][:, hs], axis=-1, keepdims=True)
            dot_parts.append(t["r"][:, hs] * (don[:, hs] - t["on"][:, hs] * m))
        do_ref[...] = jnp.concatenate(dot_parts, axis=1).astype(do_ref.dtype)
        for j, val in enumerate((dog, du, dv)):
            dpt_ref[:, j * KW:(j + 1) * KW] = val.astype(dpt_ref.dtype)
        dpt_ref[:, 3 * KW:3 * KW + D] = dga.astype(dpt_ref.dtype)
        dpt_ref[:, 3 * KW + D:] = dgb.astype(dpt_ref.dtype)

    row = lambda w: pl.BlockSpec((tm, w), lambda i: (i, 0))
    in_specs = [row(KW), row(KW)] + _tail_in_specs(tm) + [row(D), row(D), pl.BlockSpec((1, N_MOD, D), lambda i: (i // per_b, 0, 0))]
    in_specs += _tail_weight_specs()
    args = [o_up, o_down, *([p] * 7), dx1, mix, modv, gna, ln_g, ln_b, w_s, b_s, w_paT, w_pbT, w_o]
    cd = MXU_DTYPE
    res, handle = _host_call(
        body, "tail_bwd", (rows // tm,), in_specs, args,
        [jax.ShapeDtypeStruct((rows, TAIL_COLS), cd), jax.ShapeDtypeStruct((rows, KW), cd),
         jax.ShapeDtypeStruct((rows, D), cd), jax.ShapeDtypeStruct((rows, D), cd),
         jax.ShapeDtypeStruct((rows, D), cd), jax.ShapeDtypeStruct((nb_ex, 8, D), F32),
         jax.ShapeDtypeStruct((8, D), F32), jax.ShapeDtypeStruct((GROUPS, SGU_BLOCK, SGU_BLOCK), F32),
         jax.ShapeDtypeStruct((GROUPS, SGU_BLOCK, 1), F32)],
        [row(TAIL_COLS), row(KW), row(D), row(D), row(D),
         pl.BlockSpec((1, 8, D), lambda i: (i // per_b, 0, 0)), _full((8, D)),
         _full((GROUPS, SGU_BLOCK, SGU_BLOCK)), _full((GROUPS, SGU_BLOCK, 1))], [],
        after=after, sender=sender)
    return (*res, handle)


def _ffn(x1, target, modv, g_ffn, g_final, w_upT, w_down, rows_per_example):
    rows = x1.shape[0]
    nb_ex = rows // rows_per_example
    tm = min(TOKEN_TILE, rows_per_example)
    per_b = rows_per_example // tm
    n_ff = D_FF // FF_CHUNK

    def body(x1_ref, tgt_ref, mod_ref, gffn_ref, gfin_ref, wup_ref, wdn_ref,
             dx1_ref, h2_ref, dffn_ref, act_ref, dup_ref, dmod_ref, small_ref, a_scr, b_scr):
        i = pl.program_id(0)

        @pl.when(i == 0)
        def _():
            small_ref[...] = jnp.zeros_like(small_ref)

        @pl.when(i % per_b == 0)
        def _():
            dmod_ref[...] = jnp.zeros_like(dmod_ref)

        x1v = x1_ref[...]
        g2 = gffn_ref[...]
        m3, m4, m5 = mod_ref[0, 3:4, :], mod_ref[0, 4:5, :], mod_ref[0, 5:6, :]
        r2 = lax.rsqrt(jnp.mean(x1v * x1v, axis=-1, keepdims=True) + EPS)
        xn2 = x1v * r2
        h2 = (xn2 * g2) * (1.0 + m4) + m3
        h2b = h2.astype(MXU_DTYPE)
        h2_ref[...] = h2b
        for j in range(n_ff):
            cs = slice(j * FF_CHUNK, (j + 1) * FF_CHUNK)
            a = _dot(h2b, wup_ref[j * FF_CHUNK:(j + 1) * FF_CHUNK, :], "nt")
            bgate = _dot(h2b, wup_ref[D_FF + j * FF_CHUNK:D_FF + (j + 1) * FF_CHUNK, :], "nt")
            a_scr[:, cs] = a
            b_scr[:, cs] = bgate
            act_ref[:, cs] = (a * _sigmoid(a) * bgate).astype(MXU_DTYPE)
        ffn = _dot(act_ref[...], wdn_ref[...])
        x2 = x1v + m5 * ffn
        r3 = lax.rsqrt(jnp.mean(x2 * x2, axis=-1, keepdims=True) + EPS)
        xn3 = x2 * r3
        gf = gfin_ref[...]
        err = xn3 * gf - tgt_ref[...]
        loss = 0.5 * jnp.sum(jnp.mean(err * err, axis=-1, keepdims=True), axis=0, keepdims=True)
        small_ref[2:3, :] += jnp.broadcast_to(loss, (1, D))
        dy = err * (1.0 / D)
        small_ref[1:2, :] += jnp.sum(dy * xn3, axis=0, keepdims=True)
        dxn3 = dy * gf
        dx2 = r3 * (dxn3 - xn3 * jnp.mean(dxn3 * xn3, axis=-1, keepdims=True))
        dmod_ref[0, 5:6, :] += jnp.sum(dx2 * ffn, axis=0, keepdims=True)
        dffn = (dx2 * m5).astype(MXU_DTYPE)
        dffn_ref[...] = dffn
        for j in range(n_ff):
            cs = slice(j * FF_CHUNK, (j + 1) * FF_CHUNK)
            dact = _dot(dffn, wdn_ref[cs, :], "nt")
            a, bgate = a_scr[:, cs], b_scr[:, cs]
            s = _sigmoid(a)
            dup_ref[:, cs] = (dact * bgate * (s * (1.0 + a * (1.0 - s)))).astype(MXU_DTYPE)
            dup_ref[:, D_FF + j * FF_CHUNK:D_FF + (j + 1) * FF_CHUNK] = (dact * a * s).astype(MXU_DTYPE)
        dh2 = _dot(dup_ref[...], wup_ref[...])
        dmod_ref[0, 3:4, :] += jnp.sum(dh2, axis=0, keepdims=True)
        dmod_ref[0, 4:5, :] += jnp.sum(dh2 * xn2 * g2, axis=0, keepdims=True)
        small_ref[0:1, :] += jnp.sum(dh2 * (1.0 + m4) * xn2, axis=0, keepdims=True)
        dxn2 = dh2 * g2 * (1.0 + m4)
        dx1_ref[...] = dx2 + r2 * (dxn2 - xn2 * jnp.mean(dxn2 * xn2, axis=-1, keepdims=True))

    row = lambda w: pl.BlockSpec((tm, w), lambda i: (i, 0))
    cd = MXU_DTYPE
    return pl.pallas_call(
        body, name="ffn_fwd_bwd", grid=(rows // tm,),
        out_shape=(jax.ShapeDtypeStruct((rows, D), F32), jax.ShapeDtypeStruct((rows, D), cd),
                   jax.ShapeDtypeStruct((rows, D), cd), jax.ShapeDtypeStruct((rows, D_FF), cd),
                   jax.ShapeDtypeStruct((rows, 2 * D_FF), cd), jax.ShapeDtypeStruct((nb_ex, 8, D), F32),
                   jax.ShapeDtypeStruct((8, D), F32)),
        in_specs=[row(D), row(D), pl.BlockSpec((1, N_MOD, D), lambda i: (i // per_b, 0, 0)), _full((1, D)), _full((1, D)),
                  _full((2 * D_FF, D), single=True), _full((D_FF, D), single=True)],
        out_specs=(row(D), row(D), row(D), row(D_FF), row(2 * D_FF),
                   pl.BlockSpec((1, 8, D), lambda i: (i // per_b, 0, 0)), _full((8, D))),
        scratch_shapes=[pltpu.VMEM((tm, D_FF), F32), pltpu.VMEM((tm, D_FF), F32)],
        compiler_params=_params(("arbitrary",)),
    )(x1, target, modv, g_ffn, g_final, w_upT, w_down)


def _inproj_bwd(pieces, dpt, xt, dx1, modv, g, w_inT, rows_per_example, name):
    rows = xt.shape[0]
    latent = dx1 is not None
    n_cols = IN_COLS if latent else CTX_COLS
    tm = min(TOKEN_TILE, rows_per_example)
    per_b = rows_per_example // tm
    n_mod_blocks = rows // rows_per_example if latent else 1
    n_pieces = len(pieces)

    def body(*refs):
        it = iter(refs)
        pc = [next(it) for _ in range(n_pieces)]
        dpt_ref = next(it) if latent else None
        x_ref = next(it)
        dx1_ref = next(it) if latent else None
        mod_ref, g_ref, w_ref = next(it), next(it), next(it)
        gx_ref = next(it) if latent else None
        dp_ref, dmod_ref, small_ref = next(it), next(it), next(it)
        i = pl.program_id(0)

        @pl.when(i == 0)
        def _():
            small_ref[...] = jnp.zeros_like(small_ref)

        @pl.when((i % per_b == 0) if latent else (i == 0))
        def _():
            dmod_ref[...] = jnp.zeros_like(dmod_ref)

        cols = [pc[0][...], pc[1][...], pc[2][...] + pc[3][...]]
        if latent:
            cols.append(pc[4][...] + pc[5][...])
        for j, val in enumerate(cols):
            dp_ref[:, j * KW:(j + 1) * KW] = val.astype(MXU_DTYPE)
        if latent:
            dp_ref[:, 4 * KW:] = dpt_ref[...]
        dh = _dot(dp_ref[...], w_ref[...])
        x = x_ref[...]
        gv = g_ref[...]
        m1 = mod_ref[0, 1:2, :]
        r = lax.rsqrt(jnp.mean(x * x, axis=-1, keepdims=True) + EPS)
        xn = x * r
        dmod_ref[0, 0:1, :] += jnp.sum(dh, axis=0, keepdims=True)
        dmod_ref[0, 1:2, :] += jnp.sum(dh * xn * gv, axis=0, keepdims=True)
        small_ref[0:1, :] += jnp.sum(dh * (1.0 + m1) * xn, axis=0, keepdims=True)
        if latent:
            dxn = dh * gv * (1.0 + m1)
            gx_ref[...] = dx1_ref[...] + r * (dxn - xn * jnp.mean(dxn * xn, axis=-1, keepdims=True))

    row = lambda w: pl.BlockSpec((tm, w), lambda i: (i, 0))
    mod_idx = (lambda i: (i // per_b, 0, 0)) if latent else (lambda i: (0, 0, 0))
    in_specs = [row(KW)] * n_pieces + ([row(TAIL_COLS)] if latent else []) + [row(D)] + ([row(D)] if latent else [])
    in_specs += [pl.BlockSpec((1, N_MOD, D), mod_idx), _full((1, D)),
                 pl.BlockSpec((n_cols, D), lambda i: (0, 0), pipeline_mode=pl.Buffered(1))]
    args = list(pieces) + ([dpt] if latent else []) + [xt] + ([dx1] if latent else []) + [modv, g, w_inT]
    out_shape = ([jax.ShapeDtypeStruct((rows, D), F32)] if latent else []) + [
        jax.ShapeDtypeStruct((rows, n_cols), MXU_DTYPE), jax.ShapeDtypeStruct((n_mod_blocks, 8, D), F32),
        jax.ShapeDtypeStruct((8, D), F32)]
    out_specs = ([row(D)] if latent else []) + [row(n_cols), pl.BlockSpec((1, 8, D), mod_idx), _full((8, D))]
    return pl.pallas_call(
        body, name=name, grid=(rows // tm,), out_shape=out_shape, in_specs=in_specs, out_specs=out_specs,
        compiler_params=_params(("arbitrary",)),
    )(*args)


def _grad_matmul(a, b, name, init=None, tn=512, sender=None):
    rows, n = a.shape
    k = b.shape[1]
    tn = min(tn, n)
    has_init = init is not None
    init_blocks = init.shape[0] // tn if has_init else 0

    def body(*refs):
        if has_init:
            a_ref, b_ref, init_ref, o_ref = refs
        else:
            a_ref, b_ref, o_ref = refs
        g = _dot(a_ref[...], b_ref[...], "tn")
        if has_init:
            g = g + jnp.where(pl.program_id(0) < init_blocks, init_ref[...].astype(F32), 0.0)
        o_ref[...] = g.astype(o_ref.dtype)

    in_specs = [pl.BlockSpec((rows, tn), lambda i: (0, i)), _full((rows, k), single=True)]
    args = [a, b]
    if has_init:
        in_specs.append(pl.BlockSpec((tn, k), lambda i: (jnp.minimum(i, init_blocks - 1), 0)))
        args.append(init)
    (out,), handle = _host_call(
        body, name, (n // tn,), in_specs, args, [jax.ShapeDtypeStruct((n, k), PAYLOAD_DTYPE)],
        [pl.BlockSpec((tn, k), lambda i: (i, 0))], [], sender=sender)
    return out, handle


def _row_tile(rows, limit=256):
    if rows <= limit:
        return rows
    for t in range(limit, 7, -8):
        if rows % t == 0:
            return t
    return rows


def _sum8(stack, name):
    _, rows, cols = stack.shape
    tr = _row_tile(rows)

    def body(s_ref, o_ref):
        acc = s_ref[0].astype(F32)
        for j in range(1, N_DEV):
            acc = acc + s_ref[j].astype(F32)
        o_ref[...] = acc

    return pl.pallas_call(
        body, name=name, grid=(rows // tr,), out_shape=jax.ShapeDtypeStruct((rows, cols), F32),
        in_specs=[pl.BlockSpec((N_DEV, tr, cols), lambda i: (0, i, 0))],
        out_specs=pl.BlockSpec((tr, cols), lambda i: (i, 0)),
        compiler_params=_params(("arbitrary",)),
    )(stack)


def _small_reduce(stack, gam, nb_ex):
    def body(s_ref, gam_ref, o_ref, bm_ref):
        acc = s_ref[0]
        for j in range(1, N_DEV):
            acc = acc + s_ref[j]
        o_ref[...] = acc
        bm = acc[8:8 + N_MOD, :]
        for e in range(nb_ex):
            bm = bm + acc[16 + e * N_MOD:16 + (e + 1) * N_MOD, :]
        lb = jnp.concatenate([_lower_bound(gam_ref, 0), _lower_bound(gam_ref, 1)], axis=1)
        dgam = acc[7:8, :] * lb * (1.0 - lb)
        bm_ref[...] = jnp.concatenate([bm, dgam, -dgam], axis=0)

    return pl.pallas_call(
        body, name="small_reduce", grid=(1,),
        out_shape=(jax.ShapeDtypeStruct((SMALL_ROWS, D), F32), jax.ShapeDtypeStruct((8, D), F32)),
        in_specs=[_full((N_DEV, SMALL_ROWS, D)), _full((4, KW))], out_specs=(_full((SMALL_ROWS, D)), _full((8, D))),
        compiler_params=_params(("arbitrary",)),
    )(stack, gam)


def _adamw_update(w, gv, m, v):
    nm = ADAM_B1 * m + (1.0 - ADAM_B1) * gv
    nv = ADAM_B2 * v + (1.0 - ADAM_B2) * (gv * gv)
    m_hat = nm / (1.0 - ADAM_B1 ** ADAM_STEP)
    v_hat = nv / (1.0 - ADAM_B2 ** ADAM_STEP)
    return -ADAM_LR * (m_hat / (jnp.sqrt(v_hat) + ADAM_EPS) + ADAM_WD * w), nm, nv


def _adamw_sum8(stack, w, m, v, name):
    _, rows, cols = stack.shape
    tr = _row_tile(rows)

    def body(s_ref, w_ref, m_ref, v_ref, g_ref, d_ref, nm_ref, nv_ref):
        gv = s_ref[0].astype(F32)
        for j in range(1, N_DEV):
            gv = gv + s_ref[j].astype(F32)
        g_ref[...] = gv
        d_ref[...], nm_ref[...], nv_ref[...] = _adamw_update(w_ref[...], gv, m_ref[...], v_ref[...])

    blk = pl.BlockSpec((tr, cols), lambda i: (i, 0))
    sd = jax.ShapeDtypeStruct((rows, cols), F32)
    return pl.pallas_call(
        body, name=name, grid=(rows // tr,), out_shape=(sd, sd, sd, sd),
        in_specs=[pl.BlockSpec((N_DEV, tr, cols), lambda i: (0, i, 0)), blk, blk, blk], out_specs=(blk, blk, blk, blk),
        compiler_params=_params(("arbitrary",)),
    )(stack, w, m, v)


def _adamw(w, g, m, v, name):
    shape = w.shape
    cols = shape[-1]
    rows = 1
    for s in shape[:-1]:
        rows *= s
    tr = _row_tile(rows)

    def body(w_ref, g_ref, m_ref, v_ref, d_ref, nm_ref, nv_ref):
        gv = g_ref[...]
        nm = ADAM_B1 * m_ref[...] + (1.0 - ADAM_B1) * gv
        nv = ADAM_B2 * v_ref[...] + (1.0 - ADAM_B2) * (gv * gv)
        m_hat = nm / (1.0 - ADAM_B1 ** ADAM_STEP)
        v_hat = nv / (1.0 - ADAM_B2 ** ADAM_STEP)
        d_ref[...] = -ADAM_LR * (m_hat / (jnp.sqrt(v_hat) + ADAM_EPS) + ADAM_WD * w_ref[...])
        nm_ref[...] = nm
        nv_ref[...] = nv

    blk = pl.BlockSpec((tr, cols), lambda i: (i, 0))
    sd = jax.ShapeDtypeStruct((rows, cols), F32)
    d, nm, nv = pl.pallas_call(
        body, name=name, grid=(rows // tr,), out_shape=(sd, sd, sd), in_specs=[blk] * 4, out_specs=(blk, blk, blk),
        compiler_params=_params(("arbitrary",)),
    )(w.reshape(rows, cols), g.reshape(rows, cols), m.reshape(rows, cols), v.reshape(rows, cols))
    return d.reshape(shape), nm.reshape(shape), nv.reshape(shape)


def _owner_blocks(a):
    return a.reshape(N_DEV, a.shape[0] // N_DEV, a.shape[1])


class _LocalWeights:
    def __init__(self, w_upT, w_down, w_o, w_paT, w_pbT):
        self.weights = (w_upT, w_down, w_o, w_paT, w_pbT)
        self.items = {}

    def sender(self, stage, items=None):
        self.items[stage] = items
        return None

    def sent(self, stage, handle):
        pass

    def mixer_weights(self, after):
        return self.weights[1:]

    def ffn_weights(self, after):
        return self.weights[0]


def _local_step(x, ctx, target, modv, mcv, gam, g_mix, g_ffn, gna, ln_g, ln_b, w_s, b_s, g_final, w_inT, comm):
    nb_ex, seq, _ = x.shape
    ctx_len = ctx.shape[1]
    xt = x.reshape(nb_ex * seq, D)
    ct = ctx.reshape(nb_ex * ctx_len, D)
    tgt = target.reshape(nb_ex * seq, D)
    bs3 = b_s.reshape(GROUPS, SGU_BLOCK, 1)

    pc, hc, _ = _inproj(ct, mcv, g_mix, w_inT, CTX_COLS, ctx_len, "inproj_ctx")
    p, h, handle = _inproj(xt, modv, g_mix, w_inT, IN_COLS, seq, "inproj_lat", sender=comm.sender("inproj"))
    comm.sent("inproj", handle)
    cst_f, cst_b, s_ctx, _ = _hgrn_fwd(pc, gam, None, ctx_len, False, "hgrn_fwd_ctx")
    o_up, o_down, st_f, st_b, _, handle = _hgrn_fwd(p, gam, s_ctx, seq, True, "hgrn_fwd_lat",
                                                    sender=comm.sender("scan"))
    comm.sent("scan", handle)
    w_down, w_o, w_paT, w_pbT = comm.mixer_weights(o_up)
    x1, mix, merged, oa, obm = _tail_fwd(p, o_up, o_down, xt, modv, gna, ln_g, ln_b, w_s, bs3, w_paT, w_pbT, w_o, seq)
    w_upT = comm.ffn_weights(x1)
    dx1, h2, dffn, act, dup, dmod_ffn, small_ffn = _ffn(x1, tgt, modv, g_ffn, g_final, w_upT, w_down, seq)
    dpt, do, dmix, dpa, dpb, dmod_tail, small_tail, dws, dbs, _ = _tail_bwd(
        p, o_up, o_down, dx1, mix, modv, gna, ln_g, ln_b, w_s, bs3, w_paT, w_pbT, w_o, seq)
    dzf, dvf, dqf, dzb, dvb, dqb, dlb, ds0, _ = _hgrn_bwd(p, gam, do, st_f, st_b, None, seq, True, "hgrn_bwd_lat")
    czf, cvf, czb, cvb, dlb_c, _, _ = _hgrn_bwd(pc, gam, None, cst_f, cst_b, ds0, ctx_len, False, "hgrn_bwd_ctx")
    grad_x, dp, dmod_in, small_in = _inproj_bwd([dzf, dzb, dvf, dvb, dqf, dqb], dpt, xt, dx1, modv, g_mix, w_inT,
                                                 seq, "inproj_bwd_lat")
    dpc, dmc, small_c = _inproj_bwd([czf, czb, cvf, cvb], None, ct, None, mcv, g_mix, w_inT, ctx_len, "inproj_bwd_ctx")

    z = lambda r: jnp.zeros((r, D), F32)
    pad = lambda a: jnp.pad(a, ((0, 0), (0, D - a.shape[1])))
    dlb_row = (dlb + dlb_c).reshape(1, 2 * KW)
    dmod = dmod_in + dmod_tail + dmod_ffn
    small = jnp.concatenate([
        small_in[0:1] + small_c[0:1],
        small_ffn[0:1],
        small_ffn[1:2],
        small_tail[0:1],
        small_tail[1:2],
        small_tail[2:3],
        pad(dbs.reshape(1, GROUPS * SGU_BLOCK)),
        dlb_row,
        dmc[0, 0:N_MOD],
        small_ffn[2:3],
        z(1),
        dmod[:, 0:N_MOD].reshape(nb_ex * N_MOD, D),
        z(24 - nb_ex * N_MOD),
        dws.reshape(GROUPS * SGU_BLOCK * SGU_BLOCK // D, D),
    ], axis=0)

    chain = [("gw_in", dp, h, dict(init=_grad_matmul(dpc, hc, "gw_in_ctx")[0])), ("gw_up", dup, h2, {}),
             ("gw_down", act, dffn, dict(tn=256)), ("gw_o", merged, dmix, {}), ("gw_pb", dpb, obm, {}),
             ("gw_pa", dpa, oa, {})]
    items = [(small, "gather")]
    for name, a, b, kw in chain:
        g, handle = _grad_matmul(a, b, name, sender=comm.sender(name, items), **kw)
        comm.sent(name, handle)
        items = [(_owner_blocks(g), "scatter")]
    comm.sender("last", items)
    return grad_x.reshape(x.shape)


def kernel(x, c, ctx, c_ctx, w_mod, b_mod, g_mix, g_ffn, w_in, lb_gamma, g_norm_a, ln_v_g, ln_v_b, w_s, b_s, w_pa, w_pb, w_o, w_up, w_down, g_final, loss_target, m_c_ctx, m_w_mod, m_b_mod, m_g_mix, m_g_ffn, m_w_in, m_lb_gamma, m_g_norm_a, m_ln_v_g, m_ln_v_b, m_w_s, m_b_s, m_w_pa, m_w_pb, m_w_o, m_w_up, m_w_down, m_g_final, v_c_ctx, v_w_mod, v_b_mod, v_g_mix, v_g_ffn, v_w_in, v_lb_gamma, v_g_norm_a, v_ln_v_g, v_ln_v_b, v_w_s, v_b_s, v_w_pa, v_w_pb, v_w_o, v_w_up, v_w_down, v_g_final):
    nb_ex = x.shape[0]
    me = 4 * lax.axis_index("x") + 2 * lax.axis_index("y") + lax.axis_index("c")
    cd = MXU_DTYPE
    mod_cols = w_mod.shape[2]
    lb_cols = lb_gamma.shape[2]

    w_inT_l = w_in[0].T.astype(cd)
    w_upT_l = w_up[0].T.astype(cd)
    w_paT_l = w_pa[0].T.astype(cd)
    w_pbT_l = w_pb[0].T.astype(cd)
    cl = jnp.concatenate([c, jnp.pad(lb_gamma.reshape(1, 4 * lb_cols), ((0, 0), (0, D - 4 * lb_cols))),
                          jnp.zeros((8 - nb_ex - 1, D), F32)], axis=0)
    g_in, g_cl = _gather_two_level([w_inT_l, cl], "gather_w_in")
    w_inT = g_in.reshape(IN_COLS, D)
    c_all = g_cl[:, 0:nb_ex].reshape(N_DEV * nb_ex, D)
    gam = jnp.transpose(g_cl[:, nb_ex, 0:4 * lb_cols].reshape(N_DEV, 4, lb_cols), (1, 0, 2)).reshape(4, KW)

    n_c = N_DEV * nb_ex
    cvec = jnp.concatenate([c_all, c_ctx.reshape(1, D), jnp.zeros((7, D), F32)], axis=0)
    b_mod_l = lax.dynamic_slice(b_mod, (0, me * mod_cols), (1, mod_cols))
    mod_l, svec = _mod_fwd(cvec, w_mod[0], b_mod_l)
    (g_mod,) = _exchange([(mod_l, "gather")], "gather_mod")
    mod_all = jnp.transpose(g_mod, (1, 0, 2)).reshape(n_c + 8, N_MOD * D)
    modv = lax.dynamic_slice(mod_all, (me * nb_ex, 0), (nb_ex, N_MOD * D)).reshape(nb_ex, N_MOD, D)
    mcv = mod_all[n_c].reshape(1, N_MOD, D)

    handles, leftover = {}, {}

    class Comm:
        def sender(self, stage, items=None):
            if stage == "inproj":
                return _Sender([(w_down[0].astype(cd), "gather"), (w_o[0].astype(cd), "gather"), (w_paT_l, "gather"),
                                (w_pbT_l, "gather")])
            if stage == "scan":
                return _Sender([(w_upT_l, "gather")])
            if stage == "last":
                leftover["items"] = items
                return None
            return _Sender(items)

        def sent(self, stage, handle):
            handles[stage] = handle

        def mixer_weights(self, after):
            g_down, g_o, g_pa, g_pb = _exchange_wait(handles["inproj"], after)
            return g_down.reshape(D_FF, D), g_o.reshape(D, D), g_pa.reshape(D, KW), g_pb.reshape(D, KW)

        def ffn_weights(self, after):
            (g_up,) = _exchange_wait(handles["scan"], after)
            return g_up.reshape(2 * D_FF, D)

    grad_x = _local_step(
        x, ctx, loss_target, modv, mcv, gam, g_mix, g_ffn, g_norm_a, ln_v_g, ln_v_b, w_s[0], b_s[0],
        g_final.reshape(1, D), w_inT, Comm())
    last, last_started = _exchange_start(leftover["items"], "scatter_last", after=leftover["items"][0][0])

    (r_small,) = _exchange_wait(handles["gw_in"], last_started)
    (r_in,) = _exchange_wait(handles["gw_up"], r_small)
    done = {"w_in": [a.T[None] for a in _adamw_sum8(r_in, w_in[0].T, m_w_in[0].T, v_w_in[0].T, "adamw_w_in")]}
    (r_up,) = _exchange_wait(handles["gw_down"], done["w_in"][1])
    done["w_up"] = [a.T[None] for a in _adamw_sum8(r_up, w_up[0].T, m_w_up[0].T, v_w_up[0].T, "adamw_w_up")]
    (r_down,) = _exchange_wait(handles["gw_o"], done["w_up"][1])
    done["w_down"] = [a[None] for a in _adamw_sum8(r_down, w_down[0], m_w_down[0], v_w_down[0], "adamw_w_down")]
    (r_o,) = _exchange_wait(handles["gw_pb"], done["w_down"][1])
    done["w_o"] = [a[None] for a in _adamw_sum8(r_o, w_o[0], m_w_o[0], v_w_o[0], "adamw_w_o")]
    (r_pb,) = _exchange_wait(handles["gw_pa"], done["w_o"][1])
    (r_pa,) = _exchange_wait(last, r_pb)
    grad_w_in, grad_w_up, grad_w_down, grad_w_o = (done[k][0] for k in ("w_in", "w_up", "w_down", "w_o"))
    grad_w_pa = _sum8(r_pa, "sum_w_pa").T[None]
    grad_w_pb = _sum8(r_pb, "sum_w_pb").T[None]
    tot, bm = _small_reduce(r_small, gam, nb_ex)
    loss = tot[14, 0]
    grad_g_mix, grad_g_ffn, grad_g_final = tot[0:1], tot[1:2], tot[2]
    grad_g_norm_a = tot[3:4, 0:DK]
    grad_ln_v_g, grad_ln_v_b = tot[4:5, 0:KW], tot[5:6, 0:KW]
    grad_b_s = tot[6, 0:GROUPS * SGU_BLOCK].reshape(1, GROUPS, SGU_BLOCK)
    grad_w_s = tot[40:104].reshape(1, GROUPS, SGU_BLOCK, SGU_BLOCK)
    grad_b_mod = bm[0:N_MOD].reshape(1, N_MOD * D)
    grad_lb_gamma = lax.dynamic_slice(bm[6:8].reshape(2, 2, KW), (0, 0, me * lb_cols), (2, 2, lb_cols))

    dmod_all = r_small[:, 16:16 + nb_ex * N_MOD].reshape(n_c, N_MOD * D)
    dmod_l = jnp.concatenate([lax.dynamic_slice(dmod_all, (0, me * mod_cols), (n_c, mod_cols)),
                              lax.dynamic_slice(tot[8:8 + N_MOD].reshape(1, N_MOD * D), (0, me * mod_cols), (1, mod_cols)),
                              jnp.zeros((7, mod_cols), F32)], axis=0)
    gw_mod, gc = _mod_bwd(svec, cvec, dmod_l, w_mod[0])
    grad_w_mod = gw_mod[None]
    (r_gc,) = _exchange([(gc[n_c:n_c + 8], "gather")], "gather_c_ctx", after=r_pa)
    grad_c_ctx = _sum8(r_gc, "sum_c_ctx")[0]

    names = ["c_ctx", "w_mod", "b_mod", "g_mix", "g_ffn", "w_in", "lb_gamma", "g_norm_a", "ln_v_g", "ln_v_b", "w_s",
             "b_s", "w_pa", "w_pb", "w_o", "w_up", "w_down", "g_final"]
    weights = [c_ctx, w_mod, b_mod, g_mix, g_ffn, w_in, lb_gamma, g_norm_a, ln_v_g, ln_v_b, w_s, b_s, w_pa, w_pb, w_o,
               w_up, w_down, g_final]
    grads = [grad_c_ctx, grad_w_mod, grad_b_mod, grad_g_mix, grad_g_ffn, grad_w_in, grad_lb_gamma, grad_g_norm_a,
             grad_ln_v_g, grad_ln_v_b, grad_w_s, grad_b_s, grad_w_pa, grad_w_pb, grad_w_o, grad_w_up, grad_w_down,
             grad_g_final]
    ms = [m_c_ctx, m_w_mod, m_b_mod, m_g_mix, m_g_ffn, m_w_in, m_lb_gamma, m_g_norm_a, m_ln_v_g, m_ln_v_b, m_w_s, m_b_s,
          m_w_pa, m_w_pb, m_w_o, m_w_up, m_w_down, m_g_final]
    vs = [v_c_ctx, v_w_mod, v_b_mod, v_g_mix, v_g_ffn, v_w_in, v_lb_gamma, v_g_norm_a, v_ln_v_g, v_ln_v_b, v_w_s, v_b_s,
          v_w_pa, v_w_pb, v_w_o, v_w_up, v_w_down, v_g_final]
    deltas, new_ms, new_vs = [], [], []
    for nm, w, g, m, v in zip(names, weights, grads, ms, vs):
        d, nm_, nv_ = done[nm][1:] if nm in done else _adamw(w, g.reshape(w.shape), m, v, "adamw_" + nm)
        deltas.append(d)
        new_ms.append(nm_)
        new_vs.append(nv_)
    grads = [g.reshape(w.shape) for g, w in zip(grads, weights)]
    return (loss, grad_x, *grads, *deltas, *new_ms, *new_vs)
```

```python
import functools

import jax
import jax.numpy as jnp
from jax import lax
from jax.experimental import pallas as pl
from jax.experimental.pallas import tpu as pltpu

F32 = jnp.float32
MXU_DTYPE = jnp.bfloat16
PAYLOAD_DTYPE = jnp.bfloat16

N_DEV = 8
D = 1024
HEADS = 4
DK = 128
KW = HEADS * DK
CHUNK = 64
SGU_BLOCK = 128
GROUPS = 4
D_FF = 2816
FF_CHUNK = 256
N_MOD = 6
IN_COLS = 5632
CTX_COLS = 1536
TAIL_COLS = IN_COLS - 4 * KW
EPS = 1e-6
ADAM_LR, ADAM_B1, ADAM_B2, ADAM_EPS, ADAM_WD, ADAM_STEP = 0.001, 0.9, 0.999, 1e-08, 0.01, 10

VMEM_LIMIT = 56 * 1024 * 1024
TOKEN_TILE = 256
SMALL_ROWS = 40


def _params(sem):
    return pltpu.CompilerParams(dimension_semantics=sem, vmem_limit_bytes=VMEM_LIMIT)


_DN = {"nn": (((1,), (0,)), ((), ())), "nt": (((1,), (1,)), ((), ())), "tn": (((0,), (0,)), ((), ()))}


def _dot(a, b, form="nn"):
    return lax.dot_general(a.astype(MXU_DTYPE), b.astype(MXU_DTYPE), _DN[form], preferred_element_type=F32)


def _mask_dot(mask, v):
    bf = jnp.bfloat16
    hi = v.astype(bf)
    r1 = v - hi.astype(F32)
    mid = r1.astype(bf)
    lo = (r1 - mid.astype(F32)).astype(bf)
    w = v.shape[1]
    s = lax.dot_general(mask.astype(bf), jnp.concatenate([hi, mid, lo], axis=1), _DN["nn"], preferred_element_type=F32)
    return (s[:, 2 * w:] + s[:, w:2 * w]) + s[:, :w]


def _full(shape, single=False):
    n = len(shape)
    if single:
        return pl.BlockSpec(shape, lambda *_: (0,) * n, pipeline_mode=pl.Buffered(1))
    return pl.BlockSpec(shape, lambda *_: (0,) * n)


def _ordered_behind(body, in_specs, args, after):
    if after is None:
        return body
    at = len(in_specs)
    in_specs.append(pl.BlockSpec(memory_space=pl.ANY))
    args.append(after)
    return lambda *refs: body(*refs[:at], *refs[at + 1:])


def _sigmoid(z):
    return 0.5 * jnp.tanh(0.5 * z) + 0.5


def _gelu(x):
    c = 0.7978845608028654
    t = jnp.tanh(c * (x + 0.044715 * x * x * x))
    return 0.5 * x * (1.0 + t), t


def _gelu_grad(x, t):
    c = 0.7978845608028654
    return 0.5 * (1.0 + t) + 0.5 * x * (1.0 - t * t) * c * (1.0 + 3 * 0.044715 * x * x)


def _exchange(items, name, after=None):
    n = len(items)
    out_shape = []
    for a, mode in items:
        blk = a.shape if mode == "gather" else a.shape[1:]
        out_shape.append(jax.ShapeDtypeStruct((N_DEV,) + tuple(blk), a.dtype))

    def body(*refs):
        srcs, dsts = refs[:n], refs[n:2 * n]
        send_sems, recv_sems, local_sems = refs[2 * n:]
        x, y, c = lax.axis_index("x"), lax.axis_index("y"), lax.axis_index("c")
        me = 4 * x + 2 * y + c

        def src_for(i, dev):
            return srcs[i] if items[i][1] == "gather" else srcs[i].at[dev]

        local = [pltpu.make_async_copy(src_for(i, me), dsts[i].at[me], local_sems.at[i]) for i in range(n)]
        for cp in local:
            cp.start()
        remote = []
        for k in range(1, N_DEV):
            px = jnp.bitwise_xor(x, (k >> 2) & 1)
            py = jnp.bitwise_xor(y, (k >> 1) & 1)
            pc = jnp.bitwise_xor(c, k & 1)
            peer = 4 * px + 2 * py + pc
            for i in range(n):
                cp = pltpu.make_async_remote_copy(
                    src_ref=src_for(i, peer), dst_ref=dsts[i].at[me],
                    send_sem=send_sems.at[i * (N_DEV - 1) + k - 1], recv_sem=recv_sems.at[i * (N_DEV - 1) + k - 1],
                    device_id=(px, py, pc), device_id_type=pl.DeviceIdType.MESH)
                cp.start()
                remote.append(cp)
        for cp in remote:
            cp.wait()
        for cp in local:
            cp.wait()

    any_spec = pl.BlockSpec(memory_space=pl.ANY)
    in_specs, args = [any_spec] * n, [a for a, _ in items]
    if after is not None:
        in_specs.append(any_spec)
        args.append(after)
        exchange = body
        body = lambda *refs: exchange(*refs[:n], *refs[n + 1:])
    return pl.pallas_call(
        body, name=name, out_shape=out_shape, in_specs=in_specs, out_specs=[any_spec] * n,
        scratch_shapes=[pltpu.SemaphoreType.DMA((n * (N_DEV - 1),)), pltpu.SemaphoreType.DMA((n * (N_DEV - 1),)),
                        pltpu.SemaphoreType.DMA((n,))],
    )(*args)


def _gather_two_level(arrays, name):
    n = len(arrays)

    def body(*refs):
        srcs, dsts = refs[:n], refs[n:2 * n]
        send_sems, recv_sems, local_sems = refs[2 * n:]
        x, y, c = lax.axis_index("x"), lax.axis_index("y"), lax.axis_index("c")
        sibling = (x, y, 1 - c)
        chips = [(1 - x, y), (x, 1 - y), (1 - x, 1 - y)]

        def slot(px, py, pc):
            return 4 * px + 2 * py + pc

        def copy(i, k, block, to, src=None):
            return pltpu.make_async_remote_copy(
                src_ref=dsts[i].at[slot(*block)] if src is None else src, dst_ref=dsts[i].at[slot(*block)],
                send_sem=send_sems.at[i * 7 + k], recv_sem=recv_sems.at[i * 7 + k],
                device_id=to, device_id_type=pl.DeviceIdType.MESH)

        me = (x, y, c)
        mine = [pltpu.make_async_copy(srcs[i], dsts[i].at[slot(*me)], local_sems.at[i]) for i in range(n)]
        for cp in mine:
            cp.start()
        first = []
        for j, chip in enumerate(chips):
            first += [copy(i, 1 + j, me, (*chip, c), src=srcs[i]) for i in range(n)]
        first += [copy(i, 0, me, sibling, src=srcs[i]) for i in range(n)]
        for cp in first:
            cp.start()
        passed = []
        for j, chip in enumerate(chips):
            for i in range(n):
                copy(i, 1 + j, (*chip, c), me).wait_recv()
                cp = copy(i, 4 + j, (*chip, c), sibling)
                cp.start()
                passed.append(cp)
        for i in range(n):
            copy(i, 0, sibling, me).wait_recv()
            for j, chip in enumerate(chips):
                copy(i, 4 + j, (*chip, 1 - c), me).wait_recv()
        for cp in first + passed:
            cp.wait_send()
        for cp in mine:
            cp.wait()

    any_spec = pl.BlockSpec(memory_space=pl.ANY)
    return pl.pallas_call(
        body, name=name, out_shape=[jax.ShapeDtypeStruct((N_DEV,) + a.shape, a.dtype) for a in arrays],
        in_specs=[any_spec] * n, out_specs=[any_spec] * n,
        scratch_shapes=[pltpu.SemaphoreType.DMA((n * 7,)), pltpu.SemaphoreType.DMA((n * 7,)),
                        pltpu.SemaphoreType.DMA((n,))],
    )(*arrays)


_HBM = pl.BlockSpec(memory_space=pltpu.HBM)
_SEM = pl.BlockSpec(memory_space=pltpu.SEMAPHORE)
_EFFECT = pltpu.SideEffectType.DATAFLOW_SIDE_EFFECTING


def _split_copies(items, srcs, lands, send_sems, recv_sems):
    x, y, c = lax.axis_index("x"), lax.axis_index("y"), lax.axis_index("c")
    me = 4 * x + 2 * y + c
    copies = []
    for k in range(1, N_DEV):
        px = jnp.bitwise_xor(x, (k >> 2) & 1)
        py = jnp.bitwise_xor(y, (k >> 1) & 1)
        pc = jnp.bitwise_xor(c, k & 1)
        peer = 4 * px + 2 * py + pc
        for i in range(len(items)):
            src = srcs[i] if items[i][1] == "gather" else srcs[i].at[peer]
            copies.append(pltpu.make_async_remote_copy(
                src_ref=src, dst_ref=lands[i].at[me],
                send_sem=send_sems.at[i * (N_DEV - 1) + k - 1], recv_sem=recv_sems.at[i * (N_DEV - 1) + k - 1],
                device_id=(px, py, pc), device_id_type=pl.DeviceIdType.MESH))
    return me, copies


def _exchange_start(items, name, after):
    n = len(items)
    n_sem = n * (N_DEV - 1)
    srcs, lands = [], []
    for a, mode in items:
        blk = a.shape if mode == "gather" else a.shape[1:]
        srcs.append(pltpu.with_memory_space_constraint(a, pltpu.HBM))
        lands.append(pltpu.with_memory_space_constraint(lax.empty((N_DEV,) + tuple(blk), a.dtype), pltpu.HBM))

    def body(*refs):
        src_refs, land_refs = refs[:n], refs[n:2 * n]
        send_sems, recv_sems = refs[2 * n + 1], refs[2 * n + 2]
        local_sems = refs[4 * n + 3]
        me, copies = _split_copies(items, src_refs, land_refs, send_sems, recv_sems)
        for i in range(n):
            own = src_refs[i] if items[i][1] == "gather" else src_refs[i].at[me]
            cp = pltpu.make_async_copy(own, land_refs[i].at[me], local_sems.at[i])
            cp.start()
            cp.wait()
        for cp in copies:
            cp.start()

    out_shape = [pltpu.SemaphoreType.DMA((n_sem,)), pltpu.SemaphoreType.DMA((n_sem,))]
    out_shape += [pltpu.HBM(a.shape, a.dtype) for a in srcs] + [pltpu.HBM(a.shape, a.dtype) for a in lands]
    outs = pl.pallas_call(
        body, name=name, out_shape=out_shape,
        in_specs=[_HBM] * (2 * n) + [pl.BlockSpec(memory_space=pl.ANY)],
        out_specs=[_SEM, _SEM] + [_HBM] * (2 * n),
        input_output_aliases={i: 2 + i for i in range(2 * n)},
        scratch_shapes=[pltpu.SemaphoreType.DMA((n,))],
        compiler_params=pltpu.CompilerParams(has_side_effects=_EFFECT),
    )(*srcs, *lands, after)
    handle = (items, name, outs[0], outs[1], outs[2:2 + n], outs[2 + n:2 + 2 * n])
    return handle, outs[2]


class _Sender:
    PIECE_ROWS = 352

    def __init__(self, items, chunks=None):
        self.items, self.n = items, len(items)
        self.chunks = chunks
        if chunks is None:
            block_rows = [a.shape[0] if mode == "gather" else a.shape[1] for a, mode in items]
            self.chunks = [r // self.PIECE_ROWS if r % self.PIECE_ROWS == 0 else 1 for r in block_rows]
        self.srcs, self.lands = [], []
        for a, mode in items:
            blk = a.shape if mode == "gather" else a.shape[1:]
            self.srcs.append(pltpu.with_memory_space_constraint(a, pltpu.HBM))
            self.lands.append(pltpu.with_memory_space_constraint(lax.empty((N_DEV,) + tuple(blk), a.dtype), pltpu.HBM))

    def issue(self, src_refs, land_refs, send_sems, recv_sems, local_sems, step, n_steps):
        x, y, c = lax.axis_index("x"), lax.axis_index("y"), lax.axis_index("c")
        me = 4 * x + 2 * y + c
        copies = []
        for ch in range(max(self.chunks)):
            for k in range(1, N_DEV):
                px = jnp.bitwise_xor(x, (k >> 2) & 1)
                py = jnp.bitwise_xor(y, (k >> 1) & 1)
                pc = jnp.bitwise_xor(c, k & 1)
                peer = 4 * px + 2 * py + pc
                for i, (_, mode) in enumerate(self.items):
                    if ch >= self.chunks[i]:
                        continue
                    n_rows = land_refs[i].shape[1] // self.chunks[i]
                    rows = pl.ds(ch * n_rows, n_rows)
                    src = src_refs[i].at[rows] if mode == "gather" else src_refs[i].at[peer].at[rows]
                    copies.append(pltpu.make_async_remote_copy(
                        src_ref=src, dst_ref=land_refs[i].at[me].at[rows],
                        send_sem=send_sems.at[i * (N_DEV - 1) + k - 1], recv_sem=recv_sems.at[i * (N_DEV - 1) + k - 1],
                        device_id=(px, py, pc), device_id_type=pl.DeviceIdType.MESH))
        own = [pltpu.make_async_copy(src_refs[i] if mode == "gather" else src_refs[i].at[me], land_refs[i].at[me],
                                     local_sems.at[i]) for i, (_, mode) in enumerate(self.items)]

        @pl.when(step == 0)
        def _():
            for cp in own:
                cp.start()

        for s in range(n_steps):
            group = [cp for j, cp in enumerate(copies) if (j * n_steps) // len(copies) == s]
            if group:
                @pl.when(step == s)
                def _(group=group):
                    for cp in group:
                        cp.start()

        @pl.when(step == n_steps - 1)
        def _():
            for cp in own:
                cp.wait()


def _host_call(body, name, grid, in_specs, args, out_shape, out_specs, scratch_shapes, after=None, sender=None):
    in_specs, args, out_shape, out_specs = list(in_specs), list(args), list(out_shape), list(out_specs)
    scratch_shapes = list(scratch_shapes)
    semantics = ("arbitrary",) * len(grid)
    body = _ordered_behind(body, in_specs, args, after)
    if sender is None:
        res = pl.pallas_call(body, name=name, grid=grid, in_specs=in_specs, out_specs=out_specs, out_shape=out_shape,
                             scratch_shapes=scratch_shapes, compiler_params=_params(semantics))(*args)
        return res, None
    n, n_in, n_out, n_scr = sender.n, len(in_specs), len(out_shape), len(scratch_shapes)
    n_sem = n * (N_DEV - 1)
    n_steps = 1
    for g in grid:
        n_steps *= g
    compute = body

    def body(*refs):
        ins, s_in = refs[:n_in], refs[n_in:n_in + 2 * n]
        o0 = n_in + 2 * n
        outs, s_out = refs[o0:o0 + n_out], refs[o0 + n_out:o0 + n_out + 2 + 2 * n]
        scr = refs[o0 + n_out + 2 + 2 * n:]
        compute(*ins, *outs, *scr[:n_scr])
        step = pl.program_id(0)
        for d in range(1, len(grid)):
            step = step * grid[d] + pl.program_id(d)
        sender.issue(s_in[:n], s_in[n:], s_out[0], s_out[1], scr[n_scr], step, n_steps)

    res = pl.pallas_call(
        body, name=name, grid=grid,
        in_specs=in_specs + [_HBM] * (2 * n), out_specs=out_specs + [_SEM, _SEM] + [_HBM] * (2 * n),
        out_shape=out_shape + [pltpu.SemaphoreType.DMA((n_sem,)), pltpu.SemaphoreType.DMA((n_sem,))]
        + [pltpu.HBM(a.shape, a.dtype) for a in sender.srcs] + [pltpu.HBM(a.shape, a.dtype) for a in sender.lands],
        input_output_aliases={n_in + j: n_out + 2 + j for j in range(2 * n)},
        scratch_shapes=scratch_shapes + [pltpu.SemaphoreType.DMA((n,))],
        compiler_params=pltpu.CompilerParams(dimension_semantics=semantics, vmem_limit_bytes=VMEM_LIMIT,
                                             has_side_effects=_EFFECT),
    )(*args, *sender.srcs, *sender.lands)
    handle = (sender.items, name, res[n_out], res[n_out + 1], res[n_out + 2:n_out + 2 + n],
              res[n_out + 2 + n:n_out + 2 + 2 * n])
    return res[:n_out], handle


def _exchange_wait(handle, after):
    items, name, send_sems, recv_sems, srcs, lands = handle
    n = len(items)

    def body(*refs):
        src_refs, land_refs = refs[:n], refs[n:2 * n]
        send_ref, recv_ref = refs[2 * n], refs[2 * n + 1]
        _, copies = _split_copies(items, src_refs, land_refs, send_ref, recv_ref)
        for cp in copies:
            cp.wait_send()
            cp.wait_recv()

    outs = pl.pallas_call(
        body, name=name + "_wait",
        out_shape=[pltpu.HBM(a.shape, a.dtype) for a in srcs] + [pltpu.HBM(a.shape, a.dtype) for a in lands],
        in_specs=[_HBM] * (2 * n) + [_SEM, _SEM, pl.BlockSpec(memory_space=pl.ANY)], out_specs=[_HBM] * (2 * n),
        input_output_aliases={i: i for i in range(2 * n)},
        compiler_params=pltpu.CompilerParams(has_side_effects=_EFFECT),
    )(*srcs, *lands, send_sems, recv_sems, after)
    return outs[n:]


def _mod_fwd(cvec, w_mod_l, b_mod_l):
    rows, cols = cvec.shape[0], w_mod_l.shape[1]

    def body(c_ref, w_ref, b_ref, o_ref, s_ref):
        cv = c_ref[...]
        s = cv * _sigmoid(cv)
        s_ref[...] = s
        o_ref[...] = _dot(s, w_ref[...]) + b_ref[...]

    return pl.pallas_call(
        body, name="mod_fwd",
        out_shape=(jax.ShapeDtypeStruct((rows, cols), F32), jax.ShapeDtypeStruct((rows, D), F32)),
        in_specs=[_full((rows, D)), _full((D, cols)), _full((1, cols))],
        out_specs=(_full((rows, cols)), _full((rows, D))), grid=(1,),
        compiler_params=_params(("arbitrary",)),
    )(cvec, w_mod_l, b_mod_l)


def _mod_bwd(svec, cvec, dmod_l, w_mod_l):
    rows, cols = dmod_l.shape

    def body(s_ref, c_ref, d_ref, w_ref, gw_ref, gc_ref):
        gw_ref[...] = _dot(s_ref[...], d_ref[...], "tn")
        cv = c_ref[...]
        sg = _sigmoid(cv)
        gc_ref[...] = _dot(d_ref[...], w_ref[...], "nt") * (sg * (1.0 + cv * (1.0 - sg)))

    return pl.pallas_call(
        body, name="mod_bwd",
        out_shape=(jax.ShapeDtypeStruct((D, cols), F32), jax.ShapeDtypeStruct((rows, D), F32)),
        in_specs=[_full((rows, D)), _full((rows, D)), _full((rows, cols)), _full((D, cols))],
        out_specs=(_full((D, cols)), _full((rows, D))), grid=(1,),
        compiler_params=_params(("arbitrary",)),
    )(svec, cvec, dmod_l, w_mod_l)


def _inproj(xt, modv, g, w_inT, n_cols, rows_per_example, name, after=None, sender=None):
    rows = xt.shape[0]
    tm = min(TOKEN_TILE, rows_per_example)
    per_b = rows_per_example // tm
    shared_mod = modv.shape[0] == 1

    def body(x_ref, mod_ref, g_ref, w_ref, p_ref, h_ref):
        x = x_ref[...]
        r = lax.rsqrt(jnp.mean(x * x, axis=-1, keepdims=True) + EPS)
        h = (x * r * g_ref[...]) * (1.0 + mod_ref[0, 1:2, :]) + mod_ref[0, 0:1, :]
        hb = h.astype(MXU_DTYPE)
        h_ref[...] = hb
        for j in range(n_cols // KW):
            p_ref[:, j * KW:(j + 1) * KW] = _dot(hb, w_ref[j * KW:(j + 1) * KW, :], "nt").astype(p_ref.dtype)

    mod_idx = (lambda i: (0, 0, 0)) if shared_mod else (lambda i: (i // per_b, 0, 0))
    in_specs = [pl.BlockSpec((tm, D), lambda i: (i, 0)), pl.BlockSpec((1, N_MOD, D), mod_idx), _full((1, D)),
                pl.BlockSpec((n_cols, D), lambda i: (0, 0), pipeline_mode=pl.Buffered(1))]
    (p, h), handle = _host_call(
        body, name, (rows // tm,), in_specs, [xt, modv, g, w_inT],
        [jax.ShapeDtypeStruct((rows, n_cols), MXU_DTYPE), jax.ShapeDtypeStruct((rows, D), MXU_DTYPE)],
        [pl.BlockSpec((tm, n_cols), lambda i: (i, 0)), pl.BlockSpec((tm, D), lambda i: (i, 0))], [],
        after=after, sender=sender)
    return p, h, handle


def _tri(reverse):
    row = lax.broadcasted_iota(jnp.int32, (CHUNK, CHUNK), 0)
    col = lax.broadcasted_iota(jnp.int32, (CHUNK, CHUNK), 1)
    return (col >= row) if reverse else (col <= row)


def _lower_bound(gam_ref, direction):
    return _sigmoid(gam_ref[direction:direction + 1, :] - gam_ref[2 + direction:3 + direction, :])


def _gate_prep(z, lb, tri):
    sg = _sigmoid(z)
    f = lb + (1.0 - lb) * sg
    g = jnp.log(f)
    b = _mask_dot(tri, g)
    bl = jnp.sum(g, axis=0, keepdims=True)
    return sg, g, 1.0 - f, b, bl


def _hgrn_fwd(p, gam, s0, rows_per_example, with_out, name, sender=None):
    rows = p.shape[0]
    nb_ex = rows // rows_per_example
    rb = min(TOKEN_TILE, rows_per_example)
    cpb = rb // CHUNK
    nb = rows_per_example // rb
    n_chunks = rows // CHUNK
    has_s0 = s0 is not None

    def body(*refs):
        it = iter(refs)
        gam_ref = next(it)
        zf_ref, vf_ref = next(it), next(it)
        qf_ref = next(it) if with_out else None
        zb_ref, vb_ref = next(it), next(it)
        qb_ref = next(it) if with_out else None
        s0_ref = next(it) if has_s0 else None
        if with_out:
            of_ref, ob_ref = next(it), next(it)
        stash_f, stash_b, fin_ref = next(it), next(it), next(it)
        st_ref = next(it)
        i = pl.program_id(1)

        @pl.when(i == 0)
        def _():
            if has_s0:
                st_ref[...] = s0_ref[:, 0]
            else:
                st_ref[...] = jnp.zeros_like(st_ref)

        for direction, (z_ref, v_ref, q_ref, stash) in enumerate(
                ((zf_ref, vf_ref, qf_ref, stash_f), (zb_ref, vb_ref, qb_ref, stash_b))):
            reverse = direction == 1
            tri = _tri(reverse)
            lb = _lower_bound(gam_ref, direction)
            order = range(cpb - 1, -1, -1) if reverse else range(cpb)
            for j in order:
                rs = slice(j * CHUNK, (j + 1) * CHUNK)
                z = z_ref[rs, :].astype(F32)
                v = v_ref[rs, :].astype(F32)
                _, _, k, b, bl = _gate_prep(z, lb, tri)
                mid = 0.5 * bl
                em = jnp.exp(mid)
                e2 = jnp.exp(mid - b)
                kd = k * (e2 * em)
                a = em * em
                if with_out:
                    q = q_ref[rs, :].astype(F32)
                    qi = q * jnp.exp(b - mid)
                    ki = k * e2
                    qe = qi * em
                for h in range(HEADS):
                    hs = slice(h * DK, (h + 1) * DK)
                    st = st_ref[direction, h]
                    stash[j, h] = st.astype(stash.dtype)
                    if with_out:
                        sc = jnp.where(tri, _dot(qi[:, hs], ki[:, hs], "nt"), 0.0)
                        o = _dot(sc, v[:, hs]) + _dot(qe[:, hs], st, "nt")
                        (ob_ref if reverse else of_ref)[rs, hs] = o
                    st_ref[direction, h] = st * a[:, hs] + _dot(v[:, hs], kd[:, hs], "tn")

        @pl.when(i == nb - 1)
        def _():
            fin_ref[:, 0] = st_ref[...]

    up = lambda b, i: b * nb + i
    down = lambda b, i: b * nb + nb - 1 - i
    col = lambda rowf, c: pl.BlockSpec((rb, KW), lambda b, i: (rowf(b, i), c))
    in_specs = [_full((4, KW)), col(up, 0), col(up, 2)] + ([col(up, 3)] if with_out else [])
    in_specs += [col(down, 1), col(down, 2)] + ([col(down, 3)] if with_out else [])
    args = [gam, p, p] + ([p] if with_out else []) + [p, p] + ([p] if with_out else [])
    if has_s0:
        in_specs.append(pl.BlockSpec((2, 1, HEADS, DK, DK), lambda b, i: (0, b, 0, 0, 0)))
        args.append(s0)
    out_shape, out_specs = [], []
    if with_out:
        out_shape += [jax.ShapeDtypeStruct((rows, KW), F32)] * 2
        out_specs += [pl.BlockSpec((rb, KW), lambda b, i: (up(b, i), 0)),
                      pl.BlockSpec((rb, KW), lambda b, i: (down(b, i), 0))]
    out_shape += [jax.ShapeDtypeStruct((n_chunks, HEADS, DK, DK), MXU_DTYPE)] * 2
    out_specs += [pl.BlockSpec((cpb, HEADS, DK, DK), lambda b, i: (up(b, i), 0, 0, 0)),
                  pl.BlockSpec((cpb, HEADS, DK, DK), lambda b, i: (down(b, i), 0, 0, 0))]
    out_shape.append(jax.ShapeDtypeStruct((2, nb_ex, HEADS, DK, DK), F32))
    out_specs.append(pl.BlockSpec((2, 1, HEADS, DK, DK), lambda b, i: (0, b, 0, 0, 0)))
    res, handle = _host_call(body, name, (nb_ex, nb), in_specs, args, out_shape, out_specs,
                             [pltpu.VMEM((2, HEADS, DK, DK), F32)], sender=sender)
    return (*res, handle)


def _hgrn_bwd(p, gam, do, stash_f, stash_b, ds_end, rows_per_example, with_out, name, after=None, sender=None):
    rows = p.shape[0]
    nb_ex = rows // rows_per_example
    rb = min(TOKEN_TILE, rows_per_example)
    cpb = rb // CHUNK
    nb = rows_per_example // rb
    has_end = ds_end is not None

    def body(*refs):
        it = iter(refs)
        gam_ref = next(it)
        ins = []
        for _ in range(2):
            z_ref, v_ref = next(it), next(it)
            q_ref = next(it) if with_out else None
            do_ref = next(it) if with_out else None
            ins.append((z_ref, v_ref, q_ref, do_ref, next(it)))
        end_ref = next(it) if has_end else None
        outs = [next(it), next(it)]
        dlb_ref, ds0_ref = next(it), next(it)
        dst_ref = next(it)
        b_id, i = pl.program_id(0), pl.program_id(1)

        @pl.when(i == 0)
        def _():
            if has_end:
                dst_ref[...] = end_ref[:, 0]
            else:
                dst_ref[...] = jnp.zeros_like(dst_ref)

        @pl.when((i == 0) & (b_id == 0))
        def _():
            dlb_ref[...] = jnp.zeros_like(dlb_ref)

        for direction in range(2):
            z_ref, v_ref, q_ref, do_ref, stash = ins[direction]
            dgrp_ref = outs[direction]
            reverse = direction == 1
            tri = _tri(reverse)
            tri_t = _tri(not reverse)
            lb = _lower_bound(gam_ref, direction)
            order = range(cpb) if reverse else range(cpb - 1, -1, -1)
            dlb_acc = jnp.zeros((1, KW), F32)
            for j in order:
                rs = slice(j * CHUNK, (j + 1) * CHUNK)
                z = z_ref[rs, :].astype(F32)
                v = v_ref[rs, :].astype(F32)
                sg, g, k, b, bl = _gate_prep(z, lb, tri)
                mid = 0.5 * bl
                em = jnp.exp(mid)
                e2 = jnp.exp(mid - b)
                e3 = e2 * em
                kd = k * e3
                a = em * em
                if with_out:
                    q = q_ref[rs, :].astype(F32)
                    dout = do_ref[rs, :].astype(F32)
                    e1 = jnp.exp(b - mid)
                    e4 = e1 * em
                    qi, ki, qe = q * e1, k * e2, q * e4
                dkd_p, dv_p, da_p, dqi_p, dki_p, dqe_p = [], [], [], [], [], []
                for h in range(HEADS):
                    hs = slice(h * DK, (h + 1) * DK)
                    st_in = stash[j, h]
                    dst = dst_ref[direction, h]
                    dkd_p.append(_dot(v[:, hs], dst))
                    dvh = _dot(kd[:, hs], dst, "nt")
                    da_p.append(jnp.sum(dst * st_in.astype(F32), axis=0, keepdims=True))
                    new_dst = dst * a[:, hs]
                    if with_out:
                        sc = jnp.where(tri, _dot(qi[:, hs], ki[:, hs], "nt"), 0.0)
                        dsc = jnp.where(tri, _dot(dout[:, hs], v[:, hs], "nt"), 0.0)
                        dqi_p.append(_dot(dsc, ki[:, hs]))
                        dki_p.append(_dot(dsc, qi[:, hs], "tn"))
                        dqe_p.append(_dot(dout[:, hs], st_in))
                        dvh = dvh + _dot(sc, dout[:, hs], "tn")
                        new_dst = new_dst + _dot(dout[:, hs], qe[:, hs], "tn")
                    dv_p.append(dvh)
                    dst_ref[direction, h] = new_dst
                cat = lambda parts: jnp.concatenate(parts, axis=1)
                dkd, da = cat(dkd_p), cat(da_p)
                dgrp_ref[rs, KW:2 * KW] = cat(dv_p).astype(dgrp_ref.dtype)
                t_kd = dkd * kd
                dk = dkd * e3
                db = -t_kd
                dbl = jnp.sum(t_kd, axis=0, keepdims=True) + da * a
                if with_out:
                    dqi, dki, dqe = cat(dqi_p), cat(dki_p), cat(dqe_p)
                    dgrp_ref[rs, 2 * KW:] = (dqi * e1 + dqe * e4).astype(dgrp_ref.dtype)
                    dk = dk + dki * e2
                    t_qi, t_ki, t_qe = dqi * qi, dki * ki, dqe * qe
                    db = db + t_qi - t_ki + t_qe
                    dbl = dbl + 0.5 * jnp.sum(t_ki - t_qi, axis=0, keepdims=True)
                dg = _mask_dot(tri_t, db) + dbl
                df = dg * jnp.exp(-g) - dk
                dgrp_ref[rs, 0:KW] = (df * (1.0 - lb) * sg * (1.0 - sg)).astype(dgrp_ref.dtype)
                dlb_acc = dlb_acc + jnp.sum(df * (1.0 - sg), axis=0, keepdims=True)
            dlb_ref[direction:direction + 1, :] += dlb_acc

        @pl.when(i == nb - 1)
        def _():
            ds0_ref[:, 0] = dst_ref[...]

    rows_of = (lambda b, i: b * nb + nb - 1 - i, lambda b, i: b * nb + i)
    in_specs, args = [_full((4, KW))], [gam]
    for direction in range(2):
        rf = rows_of[direction]
        col = lambda c, rf=rf: pl.BlockSpec((rb, KW), lambda b, i: (rf(b, i), c))
        in_specs += [col(direction), col(2)]
        args += [p, p]
        if with_out:
            in_specs += [col(3), col(0)]
            args += [p, do]
        in_specs.append(pl.BlockSpec((cpb, HEADS, DK, DK), lambda b, i, rf=rf: (rf(b, i), 0, 0, 0)))
        args.append((stash_f, stash_b)[direction])
    if has_end:
        in_specs.append(pl.BlockSpec((2, 1, HEADS, DK, DK), lambda b, i: (0, b, 0, 0, 0)))
        args.append(ds_end)
    out_shape, out_specs = [], []
    for direction in range(2):
        rf = rows_of[direction]
        width = (3 if with_out else 2) * KW
        out_shape.append(jax.ShapeDtypeStruct((rows, width), MXU_DTYPE))
        out_specs.append(pl.BlockSpec((rb, width), lambda b, i, rf=rf: (rf(b, i), 0)))
    out_shape += [jax.ShapeDtypeStruct((2, KW), F32), jax.ShapeDtypeStruct((2, nb_ex, HEADS, DK, DK), F32)]
    out_specs += [_full((2, KW)), pl.BlockSpec((2, 1, HEADS, DK, DK), lambda b, i: (0, b, 0, 0, 0))]
    res, handle = _host_call(body, name, (nb_ex, nb), in_specs, args, out_shape, out_specs,
                             [pltpu.VMEM((2, HEADS, DK, DK), F32)], after=after, sender=sender)
    return (*res, handle)


def _tail_forward(osum, og, u, v, ga, gb, gna, ln_g, ln_b, ws_ref, bs_ref, wpaT_ref, wpbT_ref):
    tm = osum.shape[0]
    gna4 = jnp.concatenate([gna] * HEADS, axis=1)
    r_parts = []
    for h in range(HEADS):
        oh = osum[:, h * DK:(h + 1) * DK]
        r_parts.append(jnp.broadcast_to(lax.rsqrt(jnp.mean(oh * oh, axis=-1, keepdims=True) + EPS), (tm, DK)))
    r = jnp.concatenate(r_parts, axis=1)
    on = osum * r
    sg_og = _sigmoid(og)
    silu_og = og * sg_og
    oan = on * gna4
    oa = oan * silu_og
    ug, tu = _gelu(u)
    vg, tv = _gelu(v)
    mu = jnp.mean(vg, axis=-1, keepdims=True)
    vc = vg - mu
    rstd = lax.rsqrt(jnp.mean(vc * vc, axis=-1, keepdims=True) + EPS)
    vhat = vc * rstd
    vln = vhat * ln_g + ln_b
    blocks = []
    for n in range(tm // SGU_BLOCK):
        rs = slice(n * SGU_BLOCK, (n + 1) * SGU_BLOCK)
        blocks.append(jnp.concatenate(
            [_dot(ws_ref[g], vln[rs, g * DK:(g + 1) * DK]) + bs_ref[g] for g in range(GROUPS)], axis=1))
    mixed = jnp.concatenate(blocks, axis=0) if len(blocks) > 1 else blocks[0]
    obm = ug * mixed
    pa = _dot(oa, wpaT_ref[...], "nt")
    pb = _dot(obm, wpbT_ref[...], "nt")
    sga, sgb = _sigmoid(ga), _sigmoid(gb)
    merged = sga * pa + sgb * pb
    return dict(r=r, on=on, sg_og=sg_og, silu_og=silu_og, oan=oan, oa=oa, ug=ug, tu=tu, tv=tv, rstd=rstd, vhat=vhat,
                vln=vln, mixed=mixed, obm=obm, pa=pa, pb=pb, sga=sga, sgb=sgb, merged=merged, gna4=gna4)


def _tail_in_specs(tm):
    tile = lambda c: pl.BlockSpec((tm, KW), lambda i: (i, c))
    return [tile(c) for c in range(4, 11)]


def _tail_weight_specs():
    return [_full((1, DK)), _full((1, KW)), _full((1, KW)), _full((GROUPS, SGU_BLOCK, SGU_BLOCK)),
            _full((GROUPS, SGU_BLOCK, 1)), _full((D, KW), single=True), _full((D, KW), single=True),
            _full((D, D), single=True)]


def _read_tail_inputs(of_ref, ob_ref, pcols):
    osum = of_ref[...] + ob_ref[...]
    og, u, v = (pcols[j][...].astype(F32) for j in range(3))
    ga = jnp.concatenate([pcols[3][...], pcols[4][...]], axis=1).astype(F32)
    gb = jnp.concatenate([pcols[5][...], pcols[6][...]], axis=1).astype(F32)
    return osum, og, u, v, ga, gb


def _tail_fwd(p, o_up, o_down, xt, modv, gna, ln_g, ln_b, w_s, b_s, w_paT, w_pbT, w_o, rows_per_example):
    rows = xt.shape[0]
    tm = min(TOKEN_TILE, rows_per_example)
    per_b = rows_per_example // tm

    def body(of_ref, ob_ref, *rest):
        pcols = rest[:7]
        (x_ref, mod_ref, gna_ref, lng_ref, lnb_ref, ws_ref, bs_ref, wpaT_ref, wpbT_ref, wo_ref,
         x1_ref, mix_ref, merged_ref, oa_ref, obm_ref) = rest[7:]
        t = _tail_forward(*_read_tail_inputs(of_ref, ob_ref, pcols), gna_ref[...], lng_ref[...], lnb_ref[...],
                          ws_ref, bs_ref, wpaT_ref, wpbT_ref)
        mix = _dot(t["merged"], wo_ref[...])
        x1_ref[...] = x_ref[...] + mod_ref[0, 2:3, :] * mix
        mix_ref[...] = mix.astype(mix_ref.dtype)
        merged_ref[...] = t["merged"].astype(merged_ref.dtype)
        oa_ref[...] = t["oa"].astype(oa_ref.dtype)
        obm_ref[...] = t["obm"].astype(obm_ref.dtype)

    row = lambda w: pl.BlockSpec((tm, w), lambda i: (i, 0))
    in_specs = [row(KW), row(KW)] + _tail_in_specs(tm) + [row(D), pl.BlockSpec((1, N_MOD, D), lambda i: (i // per_b, 0, 0))]
    in_specs += _tail_weight_specs()
    return pl.pallas_call(
        body, name="tail_fwd", grid=(rows // tm,),
        out_shape=(jax.ShapeDtypeStruct((rows, D), F32), jax.ShapeDtypeStruct((rows, D), MXU_DTYPE),
                   jax.ShapeDtypeStruct((rows, D), MXU_DTYPE), jax.ShapeDtypeStruct((rows, KW), MXU_DTYPE),
                   jax.ShapeDtypeStruct((rows, KW), MXU_DTYPE)),
        in_specs=in_specs, out_specs=(row(D), row(D), row(D), row(KW), row(KW)),
        compiler_params=_params(("arbitrary",)),
    )(o_up, o_down, *([p] * 7), xt, modv, gna, ln_g, ln_b, w_s, b_s, w_paT, w_pbT, w_o)


def _tail_bwd(p, o_up, o_down, dx1, mix, modv, gna, ln_g, ln_b, w_s, b_s, w_paT, w_pbT, w_o, rows_per_example,
              after=None, sender=None):
    rows = dx1.shape[0]
    nb_ex = rows // rows_per_example
    tm = min(TOKEN_TILE, rows_per_example)
    per_b = rows_per_example // tm

    def body(of_ref, ob_ref, *rest):
        pcols = rest[:7]
        (dx1_ref, mix_ref, mod_ref, gna_ref, lng_ref, lnb_ref, ws_ref, bs_ref, wpaT_ref, wpbT_ref, wo_ref,
         dpt_ref, do_ref, dmix_ref, dpa_ref, dpb_ref, dmod_ref, small_ref, dws_ref, dbs_ref) = rest[7:]
        i = pl.program_id(0)

        @pl.when(i == 0)
        def _():
            small_ref[...] = jnp.zeros_like(small_ref)
            dws_ref[...] = jnp.zeros_like(dws_ref)
            dbs_ref[...] = jnp.zeros_like(dbs_ref)

        @pl.when(i % per_b == 0)
        def _():
            dmod_ref[...] = jnp.zeros_like(dmod_ref)

        osum, og, u, v, ga, gb = _read_tail_inputs(of_ref, ob_ref, pcols)
        ln_g = lng_ref[...]
        t = _tail_forward(osum, og, u, v, ga, gb, gna_ref[...], ln_g, lnb_ref[...], ws_ref, bs_ref, wpaT_ref, wpbT_ref)
        dx1v = dx1_ref[...]
        dmod_ref[0, 2:3, :] += jnp.sum(dx1v * mix_ref[...].astype(F32), axis=0, keepdims=True)
        dmix = dx1v * mod_ref[0, 2:3, :]
        dmix_ref[...] = dmix.astype(dmix_ref.dtype)
        dmerged = _dot(dmix, wo_ref[...], "nt")
        sga, sgb = t["sga"], t["sgb"]
        dpa = dmerged * sga
        dpb = dmerged * sgb
        dpa_ref[...] = dpa.astype(dpa_ref.dtype)
        dpb_ref[...] = dpb.astype(dpb_ref.dtype)
        dga = dmerged * t["pa"] * sga * (1.0 - sga)
        dgb = dmerged * t["pb"] * sgb * (1.0 - sgb)
        doa = _dot(dpa, wpaT_ref[...])
        dobm = _dot(dpb, wpbT_ref[...])
        dug = dobm * t["mixed"]
        dmixed = dobm * t["ug"]
        du = dug * _gelu_grad(u, t["tu"])
        dvln_blocks = []
        for n in range(tm // SGU_BLOCK):
            rs = slice(n * SGU_BLOCK, (n + 1) * SGU_BLOCK)
            parts = []
            for g in range(GROUPS):
                gs = slice(g * DK, (g + 1) * DK)
                dm = dmixed[rs, gs]
                parts.append(_dot(ws_ref[g], dm, "tn"))
                dws_ref[g] += _dot(dm, t["vln"][rs, gs], "nt")
                dbs_ref[g] += jnp.sum(dm, axis=1, keepdims=True)
            dvln_blocks.append(jnp.concatenate(parts, axis=1))
        dvln = jnp.concatenate(dvln_blocks, axis=0) if len(dvln_blocks) > 1 else dvln_blocks[0]
        vhat = t["vhat"]
        small_ref[1:2, 0:KW] += jnp.sum(dvln * vhat, axis=0, keepdims=True)
        small_ref[2:3, 0:KW] += jnp.sum(dvln, axis=0, keepdims=True)
        dvhat = dvln * ln_g
        dvg = t["rstd"] * (dvhat - jnp.mean(dvhat, axis=-1, keepdims=True)
                           - vhat * jnp.mean(dvhat * vhat, axis=-1, keepdims=True))
        dv = dvg * _gelu_grad(v, t["tv"])
        sg_og = t["sg_og"]
        doan = doa * t["silu_og"]
        dog = doa * t["oan"] * (sg_og * (1.0 + og * (1.0 - sg_og)))
        prod = doan * t["on"]
        dgna = jnp.zeros((1, DK), F32)
        for h in range(HEADS):
            dgna = dgna + jnp.sum(prod[:, h * DK:(h + 1) * DK], axis=0, keepdims=True)
        small_ref[0:1, 0:DK] += dgna
        don = doan * t["gna4"]
        dot_parts = []
        for h in range(HEADS):
            hs = slice(h * DK, (h + 1) * DK)
            m = jnp.mean(don[:, hs] * t["on"][:, hs], axis=-1, keepdims=True)
            dot_parts.append(t["r"][:, hs] * (don[:, hs] - t["on"][:, hs] * m))
        do_ref[...] = jnp.concatenate(dot_parts, axis=1).astype(do_ref.dtype)
        for j, val in enumerate((dog, du, dv)):
            dpt_ref[:, j * KW:(j + 1) * KW] = val.astype(dpt_ref.dtype)
        dpt_ref[:, 3 * KW:3 * KW + D] = dga.astype(dpt_ref.dtype)
        dpt_ref[:, 3 * KW + D:] = dgb.astype(dpt_ref.dtype)

    row = lambda w: pl.BlockSpec((tm, w), lambda i: (i, 0))
    in_specs = [row(KW), row(KW)] + _tail_in_specs(tm) + [row(D), row(D), pl.BlockSpec((1, N_MOD, D), lambda i: (i // per_b, 0, 0))]
    in_specs += _tail_weight_specs()
    args = [o_up, o_down, *([p] * 7), dx1, mix, modv, gna, ln_g, ln_b, w_s, b_s, w_paT, w_pbT, w_o]
    cd = MXU_DTYPE
    res, handle = _host_call(
        body, "tail_bwd", (rows // tm,), in_specs, args,
        [jax.ShapeDtypeStruct((rows, TAIL_COLS), cd), jax.ShapeDtypeStruct((rows, KW), cd),
         jax.ShapeDtypeStruct((rows, D), cd), jax.ShapeDtypeStruct((rows, D), cd),
         jax.ShapeDtypeStruct((rows, D), cd), jax.ShapeDtypeStruct((nb_ex, 8, D), F32),
         jax.ShapeDtypeStruct((8, D), F32), jax.ShapeDtypeStruct((GROUPS, SGU_BLOCK, SGU_BLOCK), F32),
         jax.ShapeDtypeStruct((GROUPS, SGU_BLOCK, 1), F32)],
        [row(TAIL_COLS), row(KW), row(D), row(D), row(D),
         pl.BlockSpec((1, 8, D), lambda i: (i // per_b, 0, 0)), _full((8, D)),
         _full((GROUPS, SGU_BLOCK, SGU_BLOCK)), _full((GROUPS, SGU_BLOCK, 1))], [],
        after=after, sender=sender)
    return (*res, handle)


def _ffn(x1, target, modv, g_ffn, g_final, w_upT, w_down, rows_per_example):
    rows = x1.shape[0]
    nb_ex = rows // rows_per_example
    tm = min(TOKEN_TILE, rows_per_example)
    per_b = rows_per_example // tm
    n_ff = D_FF // FF_CHUNK

    def body(x1_ref, tgt_ref, mod_ref, gffn_ref, gfin_ref, wup_ref, wdn_ref,
             dx1_ref, h2_ref, dffn_ref, act_ref, dup_ref, dmod_ref, small_ref, a_scr, b_scr):
        i = pl.program_id(0)

        @pl.when(i == 0)
        def _():
            small_ref[...] = jnp.zeros_like(small_ref)

        @pl.when(i % per_b == 0)
        def _():
            dmod_ref[...] = jnp.zeros_like(dmod_ref)

        x1v = x1_ref[...]
        g2 = gffn_ref[...]
        m3, m4, m5 = mod_ref[0, 3:4, :], mod_ref[0, 4:5, :], mod_ref[0, 5:6, :]
        r2 = lax.rsqrt(jnp.mean(x1v * x1v, axis=-1, keepdims=True) + EPS)
        xn2 = x1v * r2
        h2 = (xn2 * g2) * (1.0 + m4) + m3
        h2b = h2.astype(MXU_DTYPE)
        h2_ref[...] = h2b
        for j in range(n_ff):
            cs = slice(j * FF_CHUNK, (j + 1) * FF_CHUNK)
            a = _dot(h2b, wup_ref[j * FF_CHUNK:(j + 1) * FF_CHUNK, :], "nt")
            bgate = _dot(h2b, wup_ref[D_FF + j * FF_CHUNK:D_FF + (j + 1) * FF_CHUNK, :], "nt")
            a_scr[:, cs] = a
            b_scr[:, cs] = bgate
            act_ref[:, cs] = (a * _sigmoid(a) * bgate).astype(MXU_DTYPE)
        ffn = _dot(act_ref[...], wdn_ref[...])
        x2 = x1v + m5 * ffn
        r3 = lax.rsqrt(jnp.mean(x2 * x2, axis=-1, keepdims=True) + EPS)
        xn3 = x2 * r3
        gf = gfin_ref[...]
        err = xn3 * gf - tgt_ref[...]
        loss = 0.5 * jnp.sum(jnp.mean(err * err, axis=-1, keepdims=True), axis=0, keepdims=True)
        small_ref[2:3, :] += jnp.broadcast_to(loss, (1, D))
        dy = err * (1.0 / D)
        small_ref[1:2, :] += jnp.sum(dy * xn3, axis=0, keepdims=True)
        dxn3 = dy * gf
        dx2 = r3 * (dxn3 - xn3 * jnp.mean(dxn3 * xn3, axis=-1, keepdims=True))
        dmod_ref[0, 5:6, :] += jnp.sum(dx2 * ffn, axis=0, keepdims=True)
        dffn = (dx2 * m5).astype(MXU_DTYPE)
        dffn_ref[...] = dffn
        for j in range(n_ff):
            cs = slice(j * FF_CHUNK, (j + 1) * FF_CHUNK)
            dact = _dot(dffn, wdn_ref[cs, :], "nt")
            a, bgate = a_scr[:, cs], b_scr[:, cs]
            s = _sigmoid(a)
            dup_ref[:, cs] = (dact * bgate * (s * (1.0 + a * (1.0 - s)))).astype(MXU_DTYPE)
            dup_ref[:, D_FF + j * FF_CHUNK:D_FF + (j + 1) * FF_CHUNK] = (dact * a * s).astype(MXU_DTYPE)
        dh2 = _dot(dup_ref[...], wup_ref[...])
        dmod_ref[0, 3:4, :] += jnp.sum(dh2, axis=0, keepdims=True)
        dmod_ref[0, 4:5, :] += jnp.sum(dh2 * xn2 * g2, axis=0, keepdims=True)
        small_ref[0:1, :] += jnp.sum(dh2 * (1.0 + m4) * xn2, axis=0, keepdims=True)
        dxn2 = dh2 * g2 * (1.0 + m4)
        dx1_ref[...] = dx2 + r2 * (dxn2 - xn2 * jnp.mean(dxn2 * xn2, axis=-1, keepdims=True))

    row = lambda w: pl.BlockSpec((tm, w), lambda i: (i, 0))
    cd = MXU_DTYPE
    return pl.pallas_call(
        body, name="ffn_fwd_bwd", grid=(rows // tm,),
        out_shape=(jax.ShapeDtypeStruct((rows, D), F32), jax.ShapeDtypeStruct((rows, D), cd),
                   jax.ShapeDtypeStruct((rows, D), cd), jax.ShapeDtypeStruct((rows, D_FF), cd),
                   jax.ShapeDtypeStruct((rows, 2 * D_FF), cd), jax.ShapeDtypeStruct((nb_ex, 8, D), F32),
                   jax.ShapeDtypeStruct((8, D), F32)),
        in_specs=[row(D), row(D), pl.BlockSpec((1, N_MOD, D), lambda i: (i // per_b, 0, 0)), _full((1, D)), _full((1, D)),
                  _full((2 * D_FF, D), single=True), _full((D_FF, D), single=True)],
        out_specs=(row(D), row(D), row(D), row(D_FF), row(2 * D_FF),
                   pl.BlockSpec((1, 8, D), lambda i: (i // per_b, 0, 0)), _full((8, D))),
        scratch_shapes=[pltpu.VMEM((tm, D_FF), F32), pltpu.VMEM((tm, D_FF), F32)],
        compiler_params=_params(("arbitrary",)),
    )(x1, target, modv, g_ffn, g_final, w_upT, w_down)


def _scan_columns(up, down, n_groups):
    cols = [up[:, 0:KW].astype(F32), down[:, 0:KW].astype(F32)]
    for j in range(1, n_groups):
        cols.append(up[:, j * KW:(j + 1) * KW].astype(F32) + down[:, j * KW:(j + 1) * KW].astype(F32))
    return cols


def _inproj_bwd(d_up, d_down, dpt, xt, dx1, modv, g, w_inT, rows_per_example, name, sender=None):
    rows = xt.shape[0]
    latent = dx1 is not None
    n_cols = IN_COLS if latent else CTX_COLS
    n_groups = d_up.shape[1] // KW
    tm = min(TOKEN_TILE, rows_per_example)
    per_b = rows_per_example // tm
    n_mod_blocks = rows // rows_per_example if latent else 1

    def body(*refs):
        it = iter(refs)
        up_ref, down_ref = next(it), next(it)
        dpt_ref = next(it) if latent else None
        x_ref = next(it)
        dx1_ref = next(it) if latent else None
        mod_ref, g_ref, w_ref = next(it), next(it), next(it)
        gx_ref = next(it) if latent else None
        dp_out = None if latent else next(it)
        dmod_ref, small_ref = next(it), next(it)
        dp_ref = next(it) if latent else dp_out
        i = pl.program_id(0)

        @pl.when(i == 0)
        def _():
            small_ref[...] = jnp.zeros_like(small_ref)

        @pl.when((i % per_b == 0) if latent else (i == 0))
        def _():
            dmod_ref[...] = jnp.zeros_like(dmod_ref)

        for j, val in enumerate(_scan_columns(up_ref[...], down_ref[...], n_groups)):
            dp_ref[:, j * KW:(j + 1) * KW] = val.astype(MXU_DTYPE)
        if latent:
            dp_ref[:, 4 * KW:] = dpt_ref[...]
        dh = _dot(dp_ref[...], w_ref[...])
        x = x_ref[...]
        gv = g_ref[...]
        m1 = mod_ref[0, 1:2, :]
        r = lax.rsqrt(jnp.mean(x * x, axis=-1, keepdims=True) + EPS)
        xn = x * r
        dmod_ref[0, 0:1, :] += jnp.sum(dh, axis=0, keepdims=True)
        dmod_ref[0, 1:2, :] += jnp.sum(dh * xn * gv, axis=0, keepdims=True)
        small_ref[0:1, :] += jnp.sum(dh * (1.0 + m1) * xn, axis=0, keepdims=True)
        if latent:
            dxn = dh * gv * (1.0 + m1)
            gx_ref[...] = dx1_ref[...] + r * (dxn - xn * jnp.mean(dxn * xn, axis=-1, keepdims=True))

    row = lambda w: pl.BlockSpec((tm, w), lambda i: (i, 0))
    mod_idx = (lambda i: (i // per_b, 0, 0)) if latent else (lambda i: (0, 0, 0))
    in_specs = [row(n_groups * KW)] * 2 + ([row(TAIL_COLS)] if latent else []) + [row(D)] + ([row(D)] if latent else [])
    in_specs += [pl.BlockSpec((1, N_MOD, D), mod_idx), _full((1, D)),
                 pl.BlockSpec((n_cols, D), lambda i: (0, 0), pipeline_mode=pl.Buffered(1))]
    args = [d_up, d_down] + ([dpt] if latent else []) + [xt] + ([dx1] if latent else []) + [modv, g, w_inT]
    first = jax.ShapeDtypeStruct((rows, D), F32) if latent else jax.ShapeDtypeStruct((rows, n_cols), MXU_DTYPE)
    out_shape = [first, jax.ShapeDtypeStruct((n_mod_blocks, 8, D), F32), jax.ShapeDtypeStruct((8, D), F32)]
    out_specs = [row(D) if latent else row(n_cols), pl.BlockSpec((1, 8, D), mod_idx), _full((8, D))]
    scratch = [pltpu.VMEM((tm, n_cols), MXU_DTYPE)] if latent else []
    res, handle = _host_call(body, name, (rows // tm,), in_specs, args, out_shape, out_specs, scratch, sender=sender)
    return (*res, handle)


def _grad_matmul(a, b, name, init=None, tn=512, sender=None):
    rows, n = a.shape
    k = b.shape[1]
    tn = min(tn, n)
    has_init = init is not None
    init_blocks = init.shape[0] // tn if has_init else 0

    def body(*refs):
        if has_init:
            a_ref, b_ref, init_ref, o_ref = refs
        else:
            a_ref, b_ref, o_ref = refs
        g = _dot(a_ref[...], b_ref[...], "tn")
        if has_init:
            g = g + jnp.where(pl.program_id(0) < init_blocks, init_ref[...].astype(F32), 0.0)
        o_ref[...] = g.astype(o_ref.dtype)

    in_specs = [pl.BlockSpec((rows, tn), lambda i: (0, i)), _full((rows, k), single=True)]
    args = [a, b]
    if has_init:
        in_specs.append(pl.BlockSpec((tn, k), lambda i: (jnp.minimum(i, init_blocks - 1), 0)))
        args.append(init)
    (out,), handle = _host_call(
        body, name, (n // tn,), in_specs, args, [jax.ShapeDtypeStruct((n, k), PAYLOAD_DTYPE)],
        [pl.BlockSpec((tn, k), lambda i: (i, 0))], [], sender=sender)
    return out, handle


def _grad_in(d_up, d_down, dpt, h, init, sender=None):
    rows = h.shape[0]
    tn = 256
    per_group = KW // tn
    n_scan = 4 * per_group
    init_blocks = init.shape[0] // tn

    def body(up_ref, down_ref, dpt_ref, h_ref, init_ref, o_ref):
        i = pl.program_id(0)
        both = (up_ref[...].astype(F32) + down_ref[...].astype(F32)).astype(MXU_DTYPE)
        a = jnp.where(i < per_group, up_ref[...],
                      jnp.where(i < 2 * per_group, down_ref[...], jnp.where(i < n_scan, both, dpt_ref[...])))
        g = _dot(a, h_ref[...], "tn") + jnp.where(i < init_blocks, init_ref[...].astype(F32), 0.0)
        o_ref[...] = g.astype(o_ref.dtype)

    last = 3 * per_group - 1
    col = lambda f: pl.BlockSpec((rows, tn), lambda i: (0, f(i)))
    in_specs = [col(lambda i: jnp.clip(jnp.where(i < per_group, i, i - per_group), 0, last)),
                col(lambda i: jnp.clip(i - per_group, 0, last)),
                col(lambda i: jnp.clip(i - n_scan, 0, TAIL_COLS // tn - 1)),
                _full((rows, D), single=True),
                pl.BlockSpec((tn, D), lambda i: (jnp.minimum(i, init_blocks - 1), 0))]
    (out,), handle = _host_call(
        body, "gw_in", (IN_COLS // tn,), in_specs, [d_up, d_down, dpt, h, init],
        [jax.ShapeDtypeStruct((IN_COLS, D), PAYLOAD_DTYPE)], [pl.BlockSpec((tn, D), lambda i: (i, 0))], [],
        sender=sender)
    return out, handle


def _row_tile(rows, limit=256):
    if rows <= limit:
        return rows
    for t in range(limit, 7, -8):
        if rows % t == 0:
            return t
    return rows


def _sum8(stack, name):
    _, rows, cols = stack.shape
    tr = _row_tile(rows)

    def body(s_ref, o_ref):
        acc = s_ref[0].astype(F32)
        for j in range(1, N_DEV):
            acc = acc + s_ref[j].astype(F32)
        o_ref[...] = acc

    return pl.pallas_call(
        body, name=name, grid=(rows // tr,), out_shape=jax.ShapeDtypeStruct((rows, cols), F32),
        in_specs=[pl.BlockSpec((N_DEV, tr, cols), lambda i: (0, i, 0))],
        out_specs=pl.BlockSpec((tr, cols), lambda i: (i, 0)),
        compiler_params=_params(("arbitrary",)),
    )(stack)


def _small_reduce(early, late, gam, nb_ex):
    def body(s_ref, l_ref, gam_ref, o_ref, bm_ref):
        acc = s_ref[0] + l_ref[0]
        for j in range(1, N_DEV):
            acc = acc + (s_ref[j] + l_ref[j])
        o_ref[...] = acc
        bm = acc[8:8 + N_MOD, :]
        for e in range(nb_ex):
            bm = bm + acc[16 + e * N_MOD:16 + (e + 1) * N_MOD, :]
        lb = jnp.concatenate([_lower_bound(gam_ref, 0), _lower_bound(gam_ref, 1)], axis=1)
        dgam = acc[7:8, :] * lb * (1.0 - lb)
        bm_ref[...] = jnp.concatenate([bm, dgam, -dgam], axis=0)

    return pl.pallas_call(
        body, name="small_reduce", grid=(1,),
        out_shape=(jax.ShapeDtypeStruct((SMALL_ROWS, D), F32), jax.ShapeDtypeStruct((8, D), F32)),
        in_specs=[_full((N_DEV, SMALL_ROWS, D)), _full((N_DEV, SMALL_ROWS, D)), _full((4, KW))],
        out_specs=(_full((SMALL_ROWS, D)), _full((8, D))),
        compiler_params=_params(("arbitrary",)),
    )(early, late, gam)


def _adamw_update(w, gv, m, v):
    nm = ADAM_B1 * m + (1.0 - ADAM_B1) * gv
    nv = ADAM_B2 * v + (1.0 - ADAM_B2) * (gv * gv)
    m_hat = nm / (1.0 - ADAM_B1 ** ADAM_STEP)
    v_hat = nv / (1.0 - ADAM_B2 ** ADAM_STEP)
    return -ADAM_LR * (m_hat / (jnp.sqrt(v_hat) + ADAM_EPS) + ADAM_WD * w), nm, nv


def _adamw_sum8(stack, w, m, v, name):
    _, rows, cols = stack.shape
    tr = _row_tile(rows)

    def body(s_ref, w_ref, m_ref, v_ref, g_ref, d_ref, nm_ref, nv_ref):
        gv = s_ref[0].astype(F32)
        for j in range(1, N_DEV):
            gv = gv + s_ref[j].astype(F32)
        g_ref[...] = gv
        d_ref[...], nm_ref[...], nv_ref[...] = _adamw_update(w_ref[...], gv, m_ref[...], v_ref[...])

    blk = pl.BlockSpec((tr, cols), lambda i: (i, 0))
    sd = jax.ShapeDtypeStruct((rows, cols), F32)
    return pl.pallas_call(
        body, name=name, grid=(rows // tr,), out_shape=(sd, sd, sd, sd),
        in_specs=[pl.BlockSpec((N_DEV, tr, cols), lambda i: (0, i, 0)), blk, blk, blk], out_specs=(blk, blk, blk, blk),
        compiler_params=_params(("arbitrary",)),
    )(stack, w, m, v)


def _adamw(w, g, m, v, name):
    shape = w.shape
    cols = shape[-1]
    rows = 1
    for s in shape[:-1]:
        rows *= s
    tr = _row_tile(rows)

    def body(w_ref, g_ref, m_ref, v_ref, d_ref, nm_ref, nv_ref):
        gv = g_ref[...]
        nm = ADAM_B1 * m_ref[...] + (1.0 - ADAM_B1) * gv
        nv = ADAM_B2 * v_ref[...] + (1.0 - ADAM_B2) * (gv * gv)
        m_hat = nm / (1.0 - ADAM_B1 ** ADAM_STEP)
        v_hat = nv / (1.0 - ADAM_B2 ** ADAM_STEP)
        d_ref[...] = -ADAM_LR * (m_hat / (jnp.sqrt(v_hat) + ADAM_EPS) + ADAM_WD * w_ref[...])
        nm_ref[...] = nm
        nv_ref[...] = nv

    blk = pl.BlockSpec((tr, cols), lambda i: (i, 0))
    sd = jax.ShapeDtypeStruct((rows, cols), F32)
    d, nm, nv = pl.pallas_call(
        body, name=name, grid=(rows // tr,), out_shape=(sd, sd, sd), in_specs=[blk] * 4, out_specs=(blk, blk, blk),
        compiler_params=_params(("arbitrary",)),
    )(w.reshape(rows, cols), g.reshape(rows, cols), m.reshape(rows, cols), v.reshape(rows, cols))
    return d.reshape(shape), nm.reshape(shape), nv.reshape(shape)


def _owner_blocks(a):
    return a.reshape(N_DEV, a.shape[0] // N_DEV, a.shape[1])


class _LocalWeights:
    def __init__(self, w_upT, w_down, w_o, w_paT, w_pbT):
        self.weights = (w_upT, w_down, w_o, w_paT, w_pbT)
        self.items = {}

    def sender(self, stage, items=None):
        self.items[stage] = items
        return None

    def sent(self, stage, handle):
        pass

    def mixer_weights(self, after):
        return self.weights[1:]

    def ffn_weights(self, after):
        return self.weights[0]


def _local_step(x, ctx, target, modv, mcv, gam, g_mix, g_ffn, gna, ln_g, ln_b, w_s, b_s, g_final, w_inT, comm):
    nb_ex, seq, _ = x.shape
    ctx_len = ctx.shape[1]
    xt = x.reshape(nb_ex * seq, D)
    ct = ctx.reshape(nb_ex * ctx_len, D)
    tgt = target.reshape(nb_ex * seq, D)
    bs3 = b_s.reshape(GROUPS, SGU_BLOCK, 1)

    pc, hc, _ = _inproj(ct, mcv, g_mix, w_inT, CTX_COLS, ctx_len, "inproj_ctx")
    p, h, handle = _inproj(xt, modv, g_mix, w_inT, IN_COLS, seq, "inproj_lat", sender=comm.sender("inproj"))
    comm.sent("inproj", handle)
    cst_f, cst_b, s_ctx, _ = _hgrn_fwd(pc, gam, None, ctx_len, False, "hgrn_fwd_ctx")
    o_up, o_down, st_f, st_b, _, handle = _hgrn_fwd(p, gam, s_ctx, seq, True, "hgrn_fwd_lat",
                                                    sender=comm.sender("scan"))
    comm.sent("scan", handle)
    w_down, w_o, w_paT, w_pbT = comm.mixer_weights(o_up)
    x1, mix, merged, oa, obm = _tail_fwd(p, o_up, o_down, xt, modv, gna, ln_g, ln_b, w_s, bs3, w_paT, w_pbT, w_o, seq)
    w_upT = comm.ffn_weights(x1)
    dx1, h2, dffn, act, dup, dmod_ffn, small_ffn = _ffn(x1, tgt, modv, g_ffn, g_final, w_upT, w_down, seq)
    gw_upT, _ = _grad_matmul(dup, h2, "gw_up")
    gw_down, _ = _grad_matmul(act, dffn, "gw_down", tn=256)
    scatter = lambda *grads: [(_owner_blocks(g), "scatter") for g in grads]
    dpt, do, dmix, dpa, dpb, dmod_tail, small_tail, dws, dbs, handle = _tail_bwd(
        p, o_up, o_down, dx1, mix, modv, gna, ln_g, ln_b, w_s, bs3, w_paT, w_pbT, w_o, seq,
        sender=comm.sender("tail_bwd", scatter(gw_upT)))
    comm.sent("tail_bwd", handle)
    gw_o, _ = _grad_matmul(merged, dmix, "gw_o")
    gw_paT, _ = _grad_matmul(dpa, oa, "gw_pa")
    gw_pbT, _ = _grad_matmul(dpb, obm, "gw_pb")
    def at_row(row, a):
        return jnp.pad(a, ((row, SMALL_ROWS - row - a.shape[0]), (0, D - a.shape[1])))

    small_early = (at_row(1, small_ffn[0:2])
                   + at_row(3, small_tail[0:3])
                   + at_row(6, dbs.reshape(1, GROUPS * SGU_BLOCK))
                   + at_row(14, small_ffn[2:3]))
    dws_rows = dws.reshape(GROUPS * SGU_BLOCK, SGU_BLOCK)
    d_up, d_down, dlb, ds0, handle = _hgrn_bwd(
        p, gam, do, st_f, st_b, None, seq, True, "hgrn_bwd_lat",
        sender=comm.sender("scan_bwd", scatter(gw_down, gw_o, gw_paT, gw_pbT)
                           + [(small_early, "gather"), (dws_rows, "gather")]))
    comm.sent("scan_bwd", handle)
    c_up, c_down, dlb_c, _, _ = _hgrn_bwd(pc, gam, None, cst_f, cst_b, ds0, ctx_len, False, "hgrn_bwd_ctx")
    dpc, dmc, small_c, _ = _inproj_bwd(c_up, c_down, None, ct, None, mcv, g_mix, w_inT, ctx_len, "inproj_bwd_ctx")
    gw_inT, _ = _grad_in(d_up, d_down, dpt, h, _grad_matmul(dpc, hc, "gw_in_ctx")[0])
    grad_x, dmod_in, small_in, handle = _inproj_bwd(d_up, d_down, dpt, xt, dx1, modv, g_mix, w_inT, seq,
                                                   "inproj_bwd_lat", sender=comm.sender("inproj_bwd", scatter(gw_inT)))
    comm.sent("inproj_bwd", handle)
    dmod = dmod_in + dmod_tail + dmod_ffn
    small_late = (at_row(0, small_in[0:1] + small_c[0:1])
                  + at_row(7, (dlb + dlb_c).reshape(1, 2 * KW))
                  + at_row(8, dmc[0, 0:N_MOD])
                  + at_row(16, dmod[:, 0:N_MOD].reshape(nb_ex * N_MOD, D)))
    comm.sender("last", [(small_late, "gather")])
    return grad_x.reshape(x.shape)


def kernel(x, c, ctx, c_ctx, w_mod, b_mod, g_mix, g_ffn, w_in, lb_gamma, g_norm_a, ln_v_g, ln_v_b, w_s, b_s, w_pa, w_pb, w_o, w_up, w_down, g_final, loss_target, m_c_ctx, m_w_mod, m_b_mod, m_g_mix, m_g_ffn, m_w_in, m_lb_gamma, m_g_norm_a, m_ln_v_g, m_ln_v_b, m_w_s, m_b_s, m_w_pa, m_w_pb, m_w_o, m_w_up, m_w_down, m_g_final, v_c_ctx, v_w_mod, v_b_mod, v_g_mix, v_g_ffn, v_w_in, v_lb_gamma, v_g_norm_a, v_ln_v_g, v_ln_v_b, v_w_s, v_b_s, v_w_pa, v_w_pb, v_w_o, v_w_up, v_w_down, v_g_final):
    nb_ex = x.shape[0]
    me = 4 * lax.axis_index("x") + 2 * lax.axis_index("y") + lax.axis_index("c")
    cd = MXU_DTYPE
    mod_cols = w_mod.shape[2]
    lb_cols = lb_gamma.shape[2]

    w_inT_l = w_in[0].T.astype(cd)
    w_upT_l = w_up[0].T.astype(cd)
    w_paT_l = w_pa[0].T.astype(cd)
    w_pbT_l = w_pb[0].T.astype(cd)
    cl = jnp.concatenate([c, jnp.pad(lb_gamma.reshape(1, 4 * lb_cols), ((0, 0), (0, D - 4 * lb_cols))),
                          jnp.zeros((8 - nb_ex - 1, D), F32)], axis=0)
    g_in, g_cl = _gather_two_level([w_inT_l, cl], "gather_w_in")
    w_inT = g_in.reshape(IN_COLS, D)
    c_all = g_cl[:, 0:nb_ex].reshape(N_DEV * nb_ex, D)
    gam = jnp.transpose(g_cl[:, nb_ex, 0:4 * lb_cols].reshape(N_DEV, 4, lb_cols), (1, 0, 2)).reshape(4, KW)

    n_c = N_DEV * nb_ex
    cvec = jnp.concatenate([c_all, c_ctx.reshape(1, D), jnp.zeros((7, D), F32)], axis=0)
    b_mod_l = lax.dynamic_slice(b_mod, (0, me * mod_cols), (1, mod_cols))
    mod_l, svec = _mod_fwd(cvec, w_mod[0], b_mod_l)
    (g_mod,) = _exchange([(mod_l, "gather")], "gather_mod")
    mod_all = jnp.transpose(g_mod, (1, 0, 2)).reshape(n_c + 8, N_MOD * D)
    modv = lax.dynamic_slice(mod_all, (me * nb_ex, 0), (nb_ex, N_MOD * D)).reshape(nb_ex, N_MOD, D)
    mcv = mod_all[n_c].reshape(1, N_MOD, D)

    handles, leftover = {}, {}

    class Comm:
        def sender(self, stage, items=None):
            if stage == "inproj":
                return _Sender([(w_down[0].astype(cd), "gather"), (w_o[0].astype(cd), "gather"), (w_paT_l, "gather"),
                                (w_pbT_l, "gather")])
            if stage == "scan":
                return _Sender([(w_upT_l, "gather")])
            if stage == "last":
                leftover["items"] = items
                return None
            return _Sender(items)

        def sent(self, stage, handle):
            handles[stage] = handle

        def mixer_weights(self, after):
            g_down, g_o, g_pa, g_pb = _exchange_wait(handles["inproj"], after)
            return g_down.reshape(D_FF, D), g_o.reshape(D, D), g_pa.reshape(D, KW), g_pb.reshape(D, KW)

        def ffn_weights(self, after):
            (g_up,) = _exchange_wait(handles["scan"], after)
            return g_up.reshape(2 * D_FF, D)

    grad_x = _local_step(
        x, ctx, loss_target, modv, mcv, gam, g_mix, g_ffn, g_norm_a, ln_v_g, ln_v_b, w_s[0], b_s[0],
        g_final.reshape(1, D), w_inT, Comm())
    last, last_started = _exchange_start(leftover["items"], "gather_small_late", after=leftover["items"][0][0])

    (r_up,) = _exchange_wait(handles["tail_bwd"], last_started)
    r_down, r_o, r_pa, r_pb, r_small, r_dws = _exchange_wait(handles["scan_bwd"], r_up)
    raw_up = _adamw_sum8(r_up, w_up[0].T, m_w_up[0].T, v_w_up[0].T, "adamw_w_up")
    raw_down = _adamw_sum8(r_down, w_down[0], m_w_down[0], v_w_down[0], "adamw_w_down")
    raw_o = _adamw_sum8(r_o, w_o[0], m_w_o[0], v_w_o[0], "adamw_w_o")
    (r_in,) = _exchange_wait(handles["inproj_bwd"], raw_o[1])
    raw_in = _adamw_sum8(r_in, w_in[0].T, m_w_in[0].T, v_w_in[0].T, "adamw_w_in")
    (r_late,) = _exchange_wait(last, raw_in[1])
    done = {"w_in": [a.T[None] for a in raw_in], "w_up": [a.T[None] for a in raw_up],
            "w_down": [a[None] for a in raw_down], "w_o": [a[None] for a in raw_o]}
    grad_w_in, grad_w_up, grad_w_down, grad_w_o = (done[k][0] for k in ("w_in", "w_up", "w_down", "w_o"))
    grad_w_pa = _sum8(r_pa, "sum_w_pa").T[None]
    grad_w_pb = _sum8(r_pb, "sum_w_pb").T[None]
    tot, bm = _small_reduce(r_small, r_late, gam, nb_ex)
    loss = tot[14, 0]
    grad_g_mix, grad_g_ffn, grad_g_final = tot[0:1], tot[1:2], tot[2]
    grad_g_norm_a = tot[3:4, 0:DK]
    grad_ln_v_g, grad_ln_v_b = tot[4:5, 0:KW], tot[5:6, 0:KW]
    grad_b_s = tot[6, 0:GROUPS * SGU_BLOCK].reshape(1, GROUPS, SGU_BLOCK)
    grad_w_s = _sum8(r_dws, "sum_w_s").reshape(1, GROUPS, SGU_BLOCK, SGU_BLOCK)
    grad_b_mod = bm[0:N_MOD].reshape(1, N_MOD * D)
    grad_lb_gamma = lax.dynamic_slice(bm[6:8].reshape(2, 2, KW), (0, 0, me * lb_cols), (2, 2, lb_cols))

    dmod_all = r_late[:, 16:16 + nb_ex * N_MOD].reshape(n_c, N_MOD * D)
    dmod_l = jnp.concatenate([lax.dynamic_slice(dmod_all, (0, me * mod_cols), (n_c, mod_cols)),
                              lax.dynamic_slice(tot[8:8 + N_MOD].reshape(1, N_MOD * D), (0, me * mod_cols), (1, mod_cols)),
                              jnp.zeros((7, mod_cols), F32)], axis=0)
    gw_mod, gc = _mod_bwd(svec, cvec, dmod_l, w_mod[0])
    grad_w_mod = gw_mod[None]
    (r_gc,) = _exchange([(gc[n_c:n_c + 8], "gather")], "gather_c_ctx", after=r_late)
    grad_c_ctx = _sum8(r_gc, "sum_c_ctx")[0]

    names = ["c_ctx", "w_mod", "b_mod", "g_mix", "g_ffn", "w_in", "lb_gamma", "g_norm_a", "ln_v_g", "ln_v_b", "w_s",
             "b_s", "w_pa", "w_pb", "w_o", "w_up", "w_down", "g_final"]
    weights = [c_ctx, w_mod, b_mod, g_mix, g_ffn, w_in, lb_gamma, g_norm_a, ln_v_g, ln_v_b, w_s, b_s, w_pa, w_pb, w_o,
               w_up, w_down, g_final]
    grads = [grad_c_ctx, grad_w_mod, grad_b_mod, grad_g_mix, grad_g_ffn, grad_w_in, grad_lb_gamma, grad_g_norm_a,
             grad_ln_v_g, grad_ln_v_b, grad_w_s, grad_b_s, grad_w_pa, grad_w_pb, grad_w_o, grad_w_up, grad_w_down,
             grad_g_final]
    ms = [m_c_ctx, m_w_mod, m_b_mod, m_g_mix, m_g_ffn, m_w_in, m_lb_gamma, m_g_norm_a, m_ln_v_g, m_ln_v_b, m_w_s, m_b_s,
          m_w_pa, m_w_pb, m_w_o, m_w_up, m_w_down, m_g_final]
    vs = [v_c_ctx, v_w_mod, v_b_mod, v_g_mix, v_g_ffn, v_w_in, v_lb_gamma, v_g_norm_a, v_ln_v_g, v_ln_v_b, v_w_s, v_b_s,
          v_w_pa, v_w_pb, v_w_o, v_w_up, v_w_down, v_g_final]
    deltas, new_ms, new_vs = [], [], []
    for nm, w, g, m, v in zip(names, weights, grads, ms, vs):
        d, nm_, nv_ = done[nm][1:] if nm in done else _adamw(w, g.reshape(w.shape), m, v, "adamw_" + nm)
        deltas.append(d)
        new_ms.append(nm_)
        new_vs.append(nv_)
    grads = [g.reshape(w.shape) for g, w in zip(grads, weights)]
    return (loss, grad_x, *grads, *deltas, *new_ms, *new_vs)
```

```python
import functools

import jax
import jax.numpy as jnp
from jax import lax
from jax.experimental import pallas as pl
from jax.experimental.pallas import tpu as pltpu

F32 = jnp.float32
MXU_DTYPE = jnp.bfloat16
PAYLOAD_DTYPE = jnp.bfloat16

N_DEV = 8
D = 1024
HEADS = 4
DK = 128
KW = HEADS * DK
CHUNK = 64
SGU_BLOCK = 128
GROUPS = 4
D_FF = 2816
FF_CHUNK = 256
N_MOD = 6
IN_COLS = 5632
CTX_COLS = 1536
TAIL_COLS = IN_COLS - 4 * KW
EPS = 1e-6
ADAM_LR, ADAM_B1, ADAM_B2, ADAM_EPS, ADAM_WD, ADAM_STEP = 0.001, 0.9, 0.999, 1e-08, 0.01, 10

VMEM_LIMIT = 56 * 1024 * 1024
TOKEN_TILE = 256
SMALL_ROWS = 40


def _params(sem):
    return pltpu.CompilerParams(dimension_semantics=sem, vmem_limit_bytes=VMEM_LIMIT)


_DN = {"nn": (((1,), (0,)), ((), ())), "nt": (((1,), (1,)), ((), ())), "tn": (((0,), (0,)), ((), ()))}


def _dot(a, b, form="nn"):
    return lax.dot_general(a.astype(MXU_DTYPE), b.astype(MXU_DTYPE), _DN[form], preferred_element_type=F32)


def _mask_dot(mask, v):
    bf = jnp.bfloat16
    hi = v.astype(bf)
    r1 = v - hi.astype(F32)
    mid = r1.astype(bf)
    lo = (r1 - mid.astype(F32)).astype(bf)
    w = v.shape[1]
    s = lax.dot_general(mask.astype(bf), jnp.concatenate([hi, mid, lo], axis=1), _DN["nn"], preferred_element_type=F32)
    return (s[:, 2 * w:] + s[:, w:2 * w]) + s[:, :w]


def _full(shape, single=False):
    n = len(shape)
    if single:
        return pl.BlockSpec(shape, lambda *_: (0,) * n, pipeline_mode=pl.Buffered(1))
    return pl.BlockSpec(shape, lambda *_: (0,) * n)


def _ordered_behind(body, in_specs, args, after):
    if after is None:
        return body
    at = len(in_specs)
    in_specs.append(pl.BlockSpec(memory_space=pl.ANY))
    args.append(after)
    return lambda *refs: body(*refs[:at], *refs[at + 1:])


def _sigmoid(z):
    return 0.5 * jnp.tanh(0.5 * z) + 0.5


def _gelu(x):
    c = 0.7978845608028654
    t = jnp.tanh(c * (x + 0.044715 * x * x * x))
    return 0.5 * x * (1.0 + t), t


def _gelu_grad(x, t):
    c = 0.7978845608028654
    return 0.5 * (1.0 + t) + 0.5 * x * (1.0 - t * t) * c * (1.0 + 3 * 0.044715 * x * x)


def _exchange(items, name, after=None):
    n = len(items)
    out_shape = []
    for a, mode in items:
        blk = a.shape if mode == "gather" else a.shape[1:]
        out_shape.append(jax.ShapeDtypeStruct((N_DEV,) + tuple(blk), a.dtype))

    def body(*refs):
        srcs, dsts = refs[:n], refs[n:2 * n]
        send_sems, recv_sems, local_sems = refs[2 * n:]
        x, y, c = lax.axis_index("x"), lax.axis_index("y"), lax.axis_index("c")
        me = 4 * x + 2 * y + c

        def src_for(i, dev):
            return srcs[i] if items[i][1] == "gather" else srcs[i].at[dev]

        local = [pltpu.make_async_copy(src_for(i, me), dsts[i].at[me], local_sems.at[i]) for i in range(n)]
        for cp in local:
            cp.start()
        remote = []
        for k in range(1, N_DEV):
            px = jnp.bitwise_xor(x, (k >> 2) & 1)
            py = jnp.bitwise_xor(y, (k >> 1) & 1)
            pc = jnp.bitwise_xor(c, k & 1)
            peer = 4 * px + 2 * py + pc
            for i in range(n):
                cp = pltpu.make_async_remote_copy(
                    src_ref=src_for(i, peer), dst_ref=dsts[i].at[me],
                    send_sem=send_sems.at[i * (N_DEV - 1) + k - 1], recv_sem=recv_sems.at[i * (N_DEV - 1) + k - 1],
                    device_id=(px, py, pc), device_id_type=pl.DeviceIdType.MESH)
                cp.start()
                remote.append(cp)
        for cp in remote:
            cp.wait()
        for cp in local:
            cp.wait()

    any_spec = pl.BlockSpec(memory_space=pl.ANY)
    in_specs, args = [any_spec] * n, [a for a, _ in items]
    if after is not None:
        in_specs.append(any_spec)
        args.append(after)
        exchange = body
        body = lambda *refs: exchange(*refs[:n], *refs[n + 1:])
    return pl.pallas_call(
        body, name=name, out_shape=out_shape, in_specs=in_specs, out_specs=[any_spec] * n,
        scratch_shapes=[pltpu.SemaphoreType.DMA((n * (N_DEV - 1),)), pltpu.SemaphoreType.DMA((n * (N_DEV - 1),)),
                        pltpu.SemaphoreType.DMA((n,))],
    )(*args)


def _gather_two_level(arrays, name):
    n = len(arrays)

    def body(*refs):
        srcs, dsts = refs[:n], refs[n:2 * n]
        send_sems, recv_sems, local_sems = refs[2 * n:]
        x, y, c = lax.axis_index("x"), lax.axis_index("y"), lax.axis_index("c")
        sibling = (x, y, 1 - c)
        chips = [(1 - x, y), (x, 1 - y), (1 - x, 1 - y)]

        def slot(px, py, pc):
            return 4 * px + 2 * py + pc

        def copy(i, k, block, to, src=None):
            return pltpu.make_async_remote_copy(
                src_ref=dsts[i].at[slot(*block)] if src is None else src, dst_ref=dsts[i].at[slot(*block)],
                send_sem=send_sems.at[i * 7 + k], recv_sem=recv_sems.at[i * 7 + k],
                device_id=to, device_id_type=pl.DeviceIdType.MESH)

        me = (x, y, c)
        mine = [pltpu.make_async_copy(srcs[i], dsts[i].at[slot(*me)], local_sems.at[i]) for i in range(n)]
        for cp in mine:
            cp.start()
        first = []
        for j, chip in enumerate(chips):
            first += [copy(i, 1 + j, me, (*chip, c), src=srcs[i]) for i in range(n)]
        first += [copy(i, 0, me, sibling, src=srcs[i]) for i in range(n)]
        for cp in first:
            cp.start()
        passed = []
        for j, chip in enumerate(chips):
            for i in range(n):
                copy(i, 1 + j, (*chip, c), me).wait_recv()
                cp = copy(i, 4 + j, (*chip, c), sibling)
                cp.start()
                passed.append(cp)
        for i in range(n):
            copy(i, 0, sibling, me).wait_recv()
            for j, chip in enumerate(chips):
                copy(i, 4 + j, (*chip, 1 - c), me).wait_recv()
        for cp in first + passed:
            cp.wait_send()
        for cp in mine:
            cp.wait()

    any_spec = pl.BlockSpec(memory_space=pl.ANY)
    return pl.pallas_call(
        body, name=name, out_shape=[jax.ShapeDtypeStruct((N_DEV,) + a.shape, a.dtype) for a in arrays],
        in_specs=[any_spec] * n, out_specs=[any_spec] * n,
        scratch_shapes=[pltpu.SemaphoreType.DMA((n * 7,)), pltpu.SemaphoreType.DMA((n * 7,)),
                        pltpu.SemaphoreType.DMA((n,))],
    )(*arrays)


_HBM = pl.BlockSpec(memory_space=pltpu.HBM)
_SEM = pl.BlockSpec(memory_space=pltpu.SEMAPHORE)
_EFFECT = pltpu.SideEffectType.DATAFLOW_SIDE_EFFECTING


def _split_copies(items, srcs, lands, send_sems, recv_sems):
    x, y, c = lax.axis_index("x"), lax.axis_index("y"), lax.axis_index("c")
    me = 4 * x + 2 * y + c
    copies = []
    for k in range(1, N_DEV):
        px = jnp.bitwise_xor(x, (k >> 2) & 1)
        py = jnp.bitwise_xor(y, (k >> 1) & 1)
        pc = jnp.bitwise_xor(c, k & 1)
        peer = 4 * px + 2 * py + pc
        for i in range(len(items)):
            src = srcs[i] if items[i][1] == "gather" else srcs[i].at[peer]
            copies.append(pltpu.make_async_remote_copy(
                src_ref=src, dst_ref=lands[i].at[me],
                send_sem=send_sems.at[i * (N_DEV - 1) + k - 1], recv_sem=recv_sems.at[i * (N_DEV - 1) + k - 1],
                device_id=(px, py, pc), device_id_type=pl.DeviceIdType.MESH))
    return me, copies


def _exchange_start(items, name, after):
    n = len(items)
    n_sem = n * (N_DEV - 1)
    srcs, lands = [], []
    for a, mode in items:
        blk = a.shape if mode == "gather" else a.shape[1:]
        srcs.append(pltpu.with_memory_space_constraint(a, pltpu.HBM))
        lands.append(pltpu.with_memory_space_constraint(lax.empty((N_DEV,) + tuple(blk), a.dtype), pltpu.HBM))

    def body(*refs):
        src_refs, land_refs = refs[:n], refs[n:2 * n]
        send_sems, recv_sems = refs[2 * n + 1], refs[2 * n + 2]
        local_sems = refs[4 * n + 3]
        me, copies = _split_copies(items, src_refs, land_refs, send_sems, recv_sems)
        for i in range(n):
            own = src_refs[i] if items[i][1] == "gather" else src_refs[i].at[me]
            cp = pltpu.make_async_copy(own, land_refs[i].at[me], local_sems.at[i])
            cp.start()
            cp.wait()
        for cp in copies:
            cp.start()

    out_shape = [pltpu.SemaphoreType.DMA((n_sem,)), pltpu.SemaphoreType.DMA((n_sem,))]
    out_shape += [pltpu.HBM(a.shape, a.dtype) for a in srcs] + [pltpu.HBM(a.shape, a.dtype) for a in lands]
    outs = pl.pallas_call(
        body, name=name, out_shape=out_shape,
        in_specs=[_HBM] * (2 * n) + [pl.BlockSpec(memory_space=pl.ANY)],
        out_specs=[_SEM, _SEM] + [_HBM] * (2 * n),
        input_output_aliases={i: 2 + i for i in range(2 * n)},
        scratch_shapes=[pltpu.SemaphoreType.DMA((n,))],
        compiler_params=pltpu.CompilerParams(has_side_effects=_EFFECT),
    )(*srcs, *lands, after)
    handle = (items, name, outs[0], outs[1], outs[2:2 + n], outs[2 + n:2 + 2 * n])
    return handle, outs[2]


class _Sender:
    PIECE_ROWS = 352

    def __init__(self, items, chunks=None):
        self.items, self.n = items, len(items)
        self.chunks = chunks
        if chunks is None:
            block_rows = [a.shape[0] if mode == "gather" else a.shape[1] for a, mode in items]
            self.chunks = [r // self.PIECE_ROWS if r % self.PIECE_ROWS == 0 else 1 for r in block_rows]
        self.srcs, self.lands = [], []
        for a, mode in items:
            blk = a.shape if mode == "gather" else a.shape[1:]
            self.srcs.append(pltpu.with_memory_space_constraint(a, pltpu.HBM))
            self.lands.append(pltpu.with_memory_space_constraint(lax.empty((N_DEV,) + tuple(blk), a.dtype), pltpu.HBM))

    def issue(self, src_refs, land_refs, send_sems, recv_sems, local_sems, step, n_steps):
        x, y, c = lax.axis_index("x"), lax.axis_index("y"), lax.axis_index("c")
        me = 4 * x + 2 * y + c
        copies = []
        for ch in range(max(self.chunks)):
            for k in range(1, N_DEV):
                px = jnp.bitwise_xor(x, (k >> 2) & 1)
                py = jnp.bitwise_xor(y, (k >> 1) & 1)
                pc = jnp.bitwise_xor(c, k & 1)
                peer = 4 * px + 2 * py + pc
                for i, (_, mode) in enumerate(self.items):
                    if ch >= self.chunks[i]:
                        continue
                    n_rows = land_refs[i].shape[1] // self.chunks[i]
                    rows = pl.ds(ch * n_rows, n_rows)
                    src = src_refs[i].at[rows] if mode == "gather" else src_refs[i].at[peer].at[rows]
                    copies.append(pltpu.make_async_remote_copy(
                        src_ref=src, dst_ref=land_refs[i].at[me].at[rows],
                        send_sem=send_sems.at[i * (N_DEV - 1) + k - 1], recv_sem=recv_sems.at[i * (N_DEV - 1) + k - 1],
                        device_id=(px, py, pc), device_id_type=pl.DeviceIdType.MESH))
        own = [pltpu.make_async_copy(src_refs[i] if mode == "gather" else src_refs[i].at[me], land_refs[i].at[me],
                                     local_sems.at[i]) for i, (_, mode) in enumerate(self.items)]

        @pl.when(step == 0)
        def _():
            for cp in own:
                cp.start()

        for s in range(n_steps):
            group = [cp for j, cp in enumerate(copies) if (j * n_steps) // len(copies) == s]
            if group:
                @pl.when(step == s)
                def _(group=group):
                    for cp in group:
                        cp.start()

        @pl.when(step == n_steps - 1)
        def _():
            for cp in own:
                cp.wait()


def _host_call(body, name, grid, in_specs, args, out_shape, out_specs, scratch_shapes, after=None, sender=None):
    in_specs, args, out_shape, out_specs = list(in_specs), list(args), list(out_shape), list(out_specs)
    scratch_shapes = list(scratch_shapes)
    semantics = ("arbitrary",) * len(grid)
    body = _ordered_behind(body, in_specs, args, after)
    if sender is None:
        res = pl.pallas_call(body, name=name, grid=grid, in_specs=in_specs, out_specs=out_specs, out_shape=out_shape,
                             scratch_shapes=scratch_shapes, compiler_params=_params(semantics))(*args)
        return res, None
    n, n_in, n_out, n_scr = sender.n, len(in_specs), len(out_shape), len(scratch_shapes)
    n_sem = n * (N_DEV - 1)
    n_steps = 1
    for g in grid:
        n_steps *= g
    compute = body

    def body(*refs):
        ins, s_in = refs[:n_in], refs[n_in:n_in + 2 * n]
        o0 = n_in + 2 * n
        outs, s_out = refs[o0:o0 + n_out], refs[o0 + n_out:o0 + n_out + 2 + 2 * n]
        scr = refs[o0 + n_out + 2 + 2 * n:]
        compute(*ins, *outs, *scr[:n_scr])
        step = pl.program_id(0)
        for d in range(1, len(grid)):
            step = step * grid[d] + pl.program_id(d)
        sender.issue(s_in[:n], s_in[n:], s_out[0], s_out[1], scr[n_scr], step, n_steps)

    res = pl.pallas_call(
        body, name=name, grid=grid,
        in_specs=in_specs + [_HBM] * (2 * n), out_specs=out_specs + [_SEM, _SEM] + [_HBM] * (2 * n),
        out_shape=out_shape + [pltpu.SemaphoreType.DMA((n_sem,)), pltpu.SemaphoreType.DMA((n_sem,))]
        + [pltpu.HBM(a.shape, a.dtype) for a in sender.srcs] + [pltpu.HBM(a.shape, a.dtype) for a in sender.lands],
        input_output_aliases={n_in + j: n_out + 2 + j for j in range(2 * n)},
        scratch_shapes=scratch_shapes + [pltpu.SemaphoreType.DMA((n,))],
        compiler_params=pltpu.CompilerParams(dimension_semantics=semantics, vmem_limit_bytes=VMEM_LIMIT,
                                             has_side_effects=_EFFECT),
    )(*args, *sender.srcs, *sender.lands)
    handle = (sender.items, name, res[n_out], res[n_out + 1], res[n_out + 2:n_out + 2 + n],
              res[n_out + 2 + n:n_out + 2 + 2 * n])
    return res[:n_out], handle


def _exchange_wait(handle, after):
    items, name, send_sems, recv_sems, srcs, lands = handle
    n = len(items)

    def body(*refs):
        src_refs, land_refs = refs[:n], refs[n:2 * n]
        send_ref, recv_ref = refs[2 * n], refs[2 * n + 1]
        _, copies = _split_copies(items, src_refs, land_refs, send_ref, recv_ref)
        for cp in copies:
            cp.wait_send()
            cp.wait_recv()

    outs = pl.pallas_call(
        body, name=name + "_wait",
        out_shape=[pltpu.HBM(a.shape, a.dtype) for a in srcs] + [pltpu.HBM(a.shape, a.dtype) for a in lands],
        in_specs=[_HBM] * (2 * n) + [_SEM, _SEM, pl.BlockSpec(memory_space=pl.ANY)], out_specs=[_HBM] * (2 * n),
        input_output_aliases={i: i for i in range(2 * n)},
        compiler_params=pltpu.CompilerParams(has_side_effects=_EFFECT),
    )(*srcs, *lands, send_sems, recv_sems, after)
    return outs[n:]


def _mod_fwd(cvec, w_mod_l, b_mod_l):
    rows, cols = cvec.shape[0], w_mod_l.shape[1]

    def body(c_ref, w_ref, b_ref, o_ref, s_ref):
        cv = c_ref[...]
        s = cv * _sigmoid(cv)
        s_ref[...] = s
        o_ref[...] = _dot(s, w_ref[...]) + b_ref[...]

    return pl.pallas_call(
        body, name="mod_fwd",
        out_shape=(jax.ShapeDtypeStruct((rows, cols), F32), jax.ShapeDtypeStruct((rows, D), F32)),
        in_specs=[_full((rows, D)), _full((D, cols)), _full((1, cols))],
        out_specs=(_full((rows, cols)), _full((rows, D))), grid=(1,),
        compiler_params=_params(("arbitrary",)),
    )(cvec, w_mod_l, b_mod_l)


def _mod_bwd(svec, cvec, dmod_l, w_mod_l):
    rows, cols = dmod_l.shape

    def body(s_ref, c_ref, d_ref, w_ref, gw_ref, gc_ref):
        gw_ref[...] = _dot(s_ref[...], d_ref[...], "tn")
        cv = c_ref[...]
        sg = _sigmoid(cv)
        gc_ref[...] = _dot(d_ref[...], w_ref[...], "nt") * (sg * (1.0 + cv * (1.0 - sg)))

    return pl.pallas_call(
        body, name="mod_bwd",
        out_shape=(jax.ShapeDtypeStruct((D, cols), F32), jax.ShapeDtypeStruct((rows, D), F32)),
        in_specs=[_full((rows, D)), _full((rows, D)), _full((rows, cols)), _full((D, cols))],
        out_specs=(_full((D, cols)), _full((rows, D))), grid=(1,),
        compiler_params=_params(("arbitrary",)),
    )(svec, cvec, dmod_l, w_mod_l)


def _inproj(xt, modv, g, w_inT, n_cols, rows_per_example, name, after=None, sender=None):
    rows = xt.shape[0]
    tm = min(TOKEN_TILE, rows_per_example)
    per_b = rows_per_example // tm
    shared_mod = modv.shape[0] == 1

    def body(x_ref, mod_ref, g_ref, w_ref, p_ref, h_ref):
        x = x_ref[...]
        r = lax.rsqrt(jnp.mean(x * x, axis=-1, keepdims=True) + EPS)
        h = (x * r * g_ref[...]) * (1.0 + mod_ref[0, 1:2, :]) + mod_ref[0, 0:1, :]
        hb = h.astype(MXU_DTYPE)
        h_ref[...] = hb
        for j in range(n_cols // KW):
            p_ref[:, j * KW:(j + 1) * KW] = _dot(hb, w_ref[j * KW:(j + 1) * KW, :], "nt").astype(p_ref.dtype)

    mod_idx = (lambda i: (0, 0, 0)) if shared_mod else (lambda i: (i // per_b, 0, 0))
    in_specs = [pl.BlockSpec((tm, D), lambda i: (i, 0)), pl.BlockSpec((1, N_MOD, D), mod_idx), _full((1, D)),
                pl.BlockSpec((n_cols, D), lambda i: (0, 0), pipeline_mode=pl.Buffered(1))]
    (p, h), handle = _host_call(
        body, name, (rows // tm,), in_specs, [xt, modv, g, w_inT],
        [jax.ShapeDtypeStruct((rows, n_cols), MXU_DTYPE), jax.ShapeDtypeStruct((rows, D), MXU_DTYPE)],
        [pl.BlockSpec((tm, n_cols), lambda i: (i, 0)), pl.BlockSpec((tm, D), lambda i: (i, 0))], [],
        after=after, sender=sender)
    return p, h, handle


def _tri(reverse, n):
    row = lax.broadcasted_iota(jnp.int32, (n, n), 0)
    col = lax.broadcasted_iota(jnp.int32, (n, n), 1)
    same = (row // CHUNK) == (col // CHUNK)
    return same & ((col >= row) if reverse else (col <= row))


def _per_chunk_rows(x, reverse):
    n = x.shape[0]
    rows = [x[j * CHUNK:j * CHUNK + 1] if reverse else x[(j + 1) * CHUNK - 1:(j + 1) * CHUNK] for j in range(n // CHUNK)]
    return jnp.concatenate([jnp.broadcast_to(r, (CHUNK, x.shape[1])) for r in rows], axis=0), rows


def _lower_bound(gam_ref, direction):
    return _sigmoid(gam_ref[direction:direction + 1, :] - gam_ref[2 + direction:3 + direction, :])


def _gate_prep(z, lb, tri, reverse):
    sg = _sigmoid(z)
    f = lb + (1.0 - lb) * sg
    g = jnp.log(f)
    b = _mask_dot(tri, g)
    bl, bl_rows = _per_chunk_rows(b, reverse)
    mid = 0.5 * bl
    return sg, g, 1.0 - f, b, jnp.exp(mid), [jnp.exp(0.5 * r) for r in bl_rows], jnp.exp(mid - b), mid


def _hgrn_fwd(p, gam, s0, rows_per_example, with_out, name, sender=None):
    rows = p.shape[0]
    nb_ex = rows // rows_per_example
    rb = min(TOKEN_TILE, rows_per_example)
    cpb = rb // CHUNK
    nb = rows_per_example // rb
    n_chunks = rows // CHUNK
    has_s0 = s0 is not None

    def body(*refs):
        it = iter(refs)
        gam_ref = next(it)
        zf_ref, vf_ref = next(it), next(it)
        qf_ref = next(it) if with_out else None
        zb_ref, vb_ref = next(it), next(it)
        qb_ref = next(it) if with_out else None
        s0_ref = next(it) if has_s0 else None
        if with_out:
            of_ref, ob_ref = next(it), next(it)
        stash_f, stash_b, fin_ref = next(it), next(it), next(it)
        st_ref = next(it)
        i = pl.program_id(1)

        @pl.when(i == 0)
        def _():
            if has_s0:
                st_ref[...] = s0_ref[:, 0]
            else:
                st_ref[...] = jnp.zeros_like(st_ref)

        for direction, (z_ref, v_ref, q_ref, stash) in enumerate(
                ((zf_ref, vf_ref, qf_ref, stash_f), (zb_ref, vb_ref, qb_ref, stash_b))):
            reverse = direction == 1
            tri = _tri(reverse, rb)
            lb = _lower_bound(gam_ref, direction)
            z = z_ref[...].astype(F32)
            v = v_ref[...].astype(F32)
            _, _, k, b, em, em_rows, e2, mid = _gate_prep(z, lb, tri, reverse)
            kd = (k * (e2 * em)).astype(MXU_DTYPE)
            vb = v.astype(MXU_DTYPE)
            if with_out:
                q = q_ref[...].astype(F32)
                qi = q * jnp.exp(b - mid)
                qe = (qi * em).astype(MXU_DTYPE)
                qi = qi.astype(MXU_DTYPE)
                ki = (k * e2).astype(MXU_DTYPE)
                intra = []
                for h in range(HEADS):
                    hs = slice(h * DK, (h + 1) * DK)
                    sc = jnp.where(tri, _dot(qi[:, hs], ki[:, hs], "nt"), 0.0)
                    intra.append(_dot(sc, vb[:, hs]))
            for j in (range(cpb - 1, -1, -1) if reverse else range(cpb)):
                rs = slice(j * CHUNK, (j + 1) * CHUNK)
                a = em_rows[j] * em_rows[j]
                for h in range(HEADS):
                    hs = slice(h * DK, (h + 1) * DK)
                    st = st_ref[direction, h]
                    stash[j, h] = st.astype(stash.dtype)
                    if with_out:
                        (ob_ref if reverse else of_ref)[rs, hs] = intra[h][rs] + _dot(qe[rs, hs], st, "nt")
                    st_ref[direction, h] = st * a[:, hs] + _dot(vb[rs, hs], kd[rs, hs], "tn")

        @pl.when(i == nb - 1)
        def _():
            fin_ref[:, 0] = st_ref[...]

    up = lambda b, i: b * nb + i
    down = lambda b, i: b * nb + nb - 1 - i
    col = lambda rowf, c: pl.BlockSpec((rb, KW), lambda b, i: (rowf(b, i), c))
    in_specs = [_full((4, KW)), col(up, 0), col(up, 2)] + ([col(up, 3)] if with_out else [])
    in_specs += [col(down, 1), col(down, 2)] + ([col(down, 3)] if with_out else [])
    args = [gam, p, p] + ([p] if with_out else []) + [p, p] + ([p] if with_out else [])
    if has_s0:
        in_specs.append(pl.BlockSpec((2, 1, HEADS, DK, DK), lambda b, i: (0, b, 0, 0, 0)))
        args.append(s0)
    out_shape, out_specs = [], []
    if with_out:
        out_shape += [jax.ShapeDtypeStruct((rows, KW), F32)] * 2
        out_specs += [pl.BlockSpec((rb, KW), lambda b, i: (up(b, i), 0)),
                      pl.BlockSpec((rb, KW), lambda b, i: (down(b, i), 0))]
    out_shape += [jax.ShapeDtypeStruct((n_chunks, HEADS, DK, DK), MXU_DTYPE)] * 2
    out_specs += [pl.BlockSpec((cpb, HEADS, DK, DK), lambda b, i: (up(b, i), 0, 0, 0)),
                  pl.BlockSpec((cpb, HEADS, DK, DK), lambda b, i: (down(b, i), 0, 0, 0))]
    out_shape.append(jax.ShapeDtypeStruct((2, nb_ex, HEADS, DK, DK), F32))
    out_specs.append(pl.BlockSpec((2, 1, HEADS, DK, DK), lambda b, i: (0, b, 0, 0, 0)))
    res, handle = _host_call(body, name, (nb_ex, nb), in_specs, args, out_shape, out_specs,
                             [pltpu.VMEM((2, HEADS, DK, DK), F32)], sender=sender)
    return (*res, handle)


def _hgrn_bwd(p, gam, do, stash_f, stash_b, ds_end, rows_per_example, with_out, name, after=None, sender=None):
    rows = p.shape[0]
    nb_ex = rows // rows_per_example
    rb = min(TOKEN_TILE, rows_per_example)
    cpb = rb // CHUNK
    nb = rows_per_example // rb
    has_end = ds_end is not None

    def body(*refs):
        it = iter(refs)
        gam_ref = next(it)
        ins = []
        for _ in range(2):
            z_ref, v_ref = next(it), next(it)
            q_ref = next(it) if with_out else None
            do_ref = next(it) if with_out else None
            ins.append((z_ref, v_ref, q_ref, do_ref, next(it)))
        end_ref = next(it) if has_end else None
        outs = [next(it), next(it)]
        dlb_ref, ds0_ref = next(it), next(it)
        dst_ref = next(it)
        b_id, i = pl.program_id(0), pl.program_id(1)

        @pl.when(i == 0)
        def _():
            if has_end:
                dst_ref[...] = end_ref[:, 0]
            else:
                dst_ref[...] = jnp.zeros_like(dst_ref)

        @pl.when((i == 0) & (b_id == 0))
        def _():
            dlb_ref[...] = jnp.zeros_like(dlb_ref)

        for direction in range(2):
            z_ref, v_ref, q_ref, do_ref, stash = ins[direction]
            dgrp_ref = outs[direction]
            reverse = direction == 1
            tri = _tri(reverse, rb)
            tri_t = _tri(not reverse, rb)
            lb = _lower_bound(gam_ref, direction)
            heads = [slice(h * DK, (h + 1) * DK) for h in range(HEADS)]
            chunks = [slice(j * CHUNK, (j + 1) * CHUNK) for j in range(cpb)]
            grid_cat = lambda parts: jnp.concatenate([jnp.concatenate(row, axis=1) for row in parts], axis=0)
            cat = lambda parts: jnp.concatenate(parts, axis=1)
            z = z_ref[...].astype(F32)
            sg, g, k, b, em, em_rows, e2, mid = _gate_prep(z, lb, tri, reverse)
            e3 = e2 * em
            kd = k * e3
            kd_b = kd.astype(MXU_DTYPE)
            vb = v_ref[...].astype(MXU_DTYPE)
            if with_out:
                q = q_ref[...].astype(F32)
                dout = do_ref[...].astype(MXU_DTYPE)
                e1 = jnp.exp(b - mid)
                e4 = e1 * em
                qi, ki, qe = q * e1, k * e2, q * e4
                qi_b, ki_b, qe_b = qi.astype(MXU_DTYPE), ki.astype(MXU_DTYPE), qe.astype(MXU_DTYPE)
                dqi_p, dki_p, dv_p = [], [], []
                for hs in heads:
                    sc = jnp.where(tri, _dot(qi_b[:, hs], ki_b[:, hs], "nt"), 0.0)
                    dsc = jnp.where(tri, _dot(dout[:, hs], vb[:, hs], "nt"), 0.0)
                    dqi_p.append(_dot(dsc, ki_b[:, hs]))
                    dki_p.append(_dot(dsc, qi_b[:, hs], "tn"))
                    dv_p.append(_dot(sc, dout[:, hs], "tn"))
                dqi, dki, dv = cat(dqi_p), cat(dki_p), cat(dv_p)
                dqe = grid_cat([[_dot(dout[rs, hs], stash[j, h]) for h, hs in enumerate(heads)]
                                for j, rs in enumerate(chunks)])
                grow = [[_dot(dout[rs, hs], qe_b[rs, hs], "tn") for hs in heads] for rs in chunks]
            dkd_p = [[None] * HEADS for _ in range(cpb)]
            dvs_p = [[None] * HEADS for _ in range(cpb)]
            da_p = [[None] * HEADS for _ in range(cpb)]
            for j in (range(cpb) if reverse else range(cpb - 1, -1, -1)):
                rs = chunks[j]
                a = em_rows[j] * em_rows[j]
                for h, hs in enumerate(heads):
                    dst = dst_ref[direction, h]
                    dkd_p[j][h] = _dot(vb[rs, hs], dst)
                    dvs_p[j][h] = _dot(kd_b[rs, hs], dst, "nt")
                    da_p[j][h] = jnp.broadcast_to(
                        jnp.sum(dst * stash[j, h].astype(F32), axis=0, keepdims=True), (CHUNK, DK))
                    new_dst = dst * a[:, hs]
                    dst_ref[direction, h] = new_dst + grow[j][h] if with_out else new_dst
            dkd, dvs, da = grid_cat(dkd_p), grid_cat(dvs_p), grid_cat(da_p)
            t_kd = dkd * kd
            dk = dkd * e3
            db = -t_kd
            tot = t_kd
            if with_out:
                dgrp_ref[:, KW:2 * KW] = (dvs + dv).astype(dgrp_ref.dtype)
                dgrp_ref[:, 2 * KW:] = (dqi * e1 + dqe * e4).astype(dgrp_ref.dtype)
                dk = dk + dki * e2
                t_qi, t_ki, t_qe = dqi * qi, dki * ki, dqe * qe
                db = db + t_qi - t_ki + t_qe
                tot = tot + 0.5 * (t_ki - t_qi)
            else:
                dgrp_ref[:, KW:2 * KW] = dvs.astype(dgrp_ref.dtype)
            dbl = jnp.concatenate([jnp.broadcast_to(jnp.sum(tot[rs], axis=0, keepdims=True), (CHUNK, KW))
                                   for rs in chunks], axis=0) + da * (em * em)
            dg = _mask_dot(tri_t, db) + dbl
            df = dg * jnp.exp(-g) - dk
            dgrp_ref[:, 0:KW] = (df * (1.0 - lb) * sg * (1.0 - sg)).astype(dgrp_ref.dtype)
            dlb_ref[direction:direction + 1, :] += jnp.sum(df * (1.0 - sg), axis=0, keepdims=True)

        @pl.when(i == nb - 1)
        def _():
            ds0_ref[:, 0] = dst_ref[...]

    rows_of = (lambda b, i: b * nb + nb - 1 - i, lambda b, i: b * nb + i)
    in_specs, args = [_full((4, KW))], [gam]
    for direction in range(2):
        rf = rows_of[direction]
        col = lambda c, rf=rf: pl.BlockSpec((rb, KW), lambda b, i: (rf(b, i), c))
        in_specs += [col(direction), col(2)]
        args += [p, p]
        if with_out:
            in_specs += [col(3), col(0)]
            args += [p, do]
        in_specs.append(pl.BlockSpec((cpb, HEADS, DK, DK), lambda b, i, rf=rf: (rf(b, i), 0, 0, 0)))
        args.append((stash_f, stash_b)[direction])
    if has_end:
        in_specs.append(pl.BlockSpec((2, 1, HEADS, DK, DK), lambda b, i: (0, b, 0, 0, 0)))
        args.append(ds_end)
    out_shape, out_specs = [], []
    for direction in range(2):
        rf = rows_of[direction]
        width = (3 if with_out else 2) * KW
        out_shape.append(jax.ShapeDtypeStruct((rows, width), MXU_DTYPE))
        out_specs.append(pl.BlockSpec((rb, width), lambda b, i, rf=rf: (rf(b, i), 0)))
    out_shape += [jax.ShapeDtypeStruct((2, KW), F32), jax.ShapeDtypeStruct((2, nb_ex, HEADS, DK, DK), F32)]
    out_specs += [_full((2, KW)), pl.BlockSpec((2, 1, HEADS, DK, DK), lambda b, i: (0, b, 0, 0, 0))]
    res, handle = _host_call(body, name, (nb_ex, nb), in_specs, args, out_shape, out_specs,
                             [pltpu.VMEM((2, HEADS, DK, DK), F32)], after=after, sender=sender)
    return (*res, handle)


def _tail_forward(osum, og, u, v, ga, gb, gna, ln_g, ln_b, ws_ref, bs_ref, wpaT_ref, wpbT_ref):
    tm = osum.shape[0]
    gna4 = jnp.concatenate([gna] * HEADS, axis=1)
    r_parts = []
    for h in range(HEADS):
        oh = osum[:, h * DK:(h + 1) * DK]
        r_parts.append(jnp.broadcast_to(lax.rsqrt(jnp.mean(oh * oh, axis=-1, keepdims=True) + EPS), (tm, DK)))
    r = jnp.concatenate(r_parts, axis=1)
    on = osum * r
    sg_og = _sigmoid(og)
    silu_og = og * sg_og
    oan = on * gna4
    oa = oan * silu_og
    ug, tu = _gelu(u)
    vg, tv = _gelu(v)
    mu = jnp.mean(vg, axis=-1, keepdims=True)
    vc = vg - mu
    rstd = lax.rsqrt(jnp.mean(vc * vc, axis=-1, keepdims=True) + EPS)
    vhat = vc * rstd
    vln = vhat * ln_g + ln_b
    blocks = []
    for n in range(tm // SGU_BLOCK):
        rs = slice(n * SGU_BLOCK, (n + 1) * SGU_BLOCK)
        blocks.append(jnp.concatenate(
            [_dot(ws_ref[g], vln[rs, g * DK:(g + 1) * DK]) + bs_ref[g] for g in range(GROUPS)], axis=1))
    mixed = jnp.concatenate(blocks, axis=0) if len(blocks) > 1 else blocks[0]
    obm = ug * mixed
    pa = _dot(oa, wpaT_ref[...], "nt")
    pb = _dot(obm, wpbT_ref[...], "nt")
    sga, sgb = _sigmoid(ga), _sigmoid(gb)
    merged = sga * pa + sgb * pb
    return dict(r=r, on=on, sg_og=sg_og, silu_og=silu_og, oan=oan, oa=oa, ug=ug, tu=tu, tv=tv, rstd=rstd, vhat=vhat,
                vln=vln, mixed=mixed, obm=obm, pa=pa, pb=pb, sga=sga, sgb=sgb, merged=merged, gna4=gna4)


def _tail_in_specs(tm):
    tile = lambda c: pl.BlockSpec((tm, KW), lambda i: (i, c))
    return [tile(c) for c in range(4, 11)]


def _tail_weight_specs():
    return [_full((1, DK)), _full((1, KW)), _full((1, KW)), _full((GROUPS, SGU_BLOCK, SGU_BLOCK)),
            _full((GROUPS, SGU_BLOCK, 1)), _full((D, KW), single=True), _full((D, KW), single=True),
            _full((D, D), single=True)]


def _read_tail_inputs(of_ref, ob_ref, pcols):
    osum = of_ref[...] + ob_ref[...]
    og, u, v = (pcols[j][...].astype(F32) for j in range(3))
    ga = jnp.concatenate([pcols[3][...], pcols[4][...]], axis=1).astype(F32)
    gb = jnp.concatenate([pcols[5][...], pcols[6][...]], axis=1).astype(F32)
    return osum, og, u, v, ga, gb


def _tail_fwd(p, o_up, o_down, xt, modv, gna, ln_g, ln_b, w_s, b_s, w_paT, w_pbT, w_o, rows_per_example):
    rows = xt.shape[0]
    tm = min(TOKEN_TILE, rows_per_example)
    per_b = rows_per_example // tm

    def body(of_ref, ob_ref, *rest):
        pcols = rest[:7]
        (x_ref, mod_ref, gna_ref, lng_ref, lnb_ref, ws_ref, bs_ref, wpaT_ref, wpbT_ref, wo_ref,
         x1_ref, mix_ref, merged_ref, oa_ref, obm_ref) = rest[7:]
        t = _tail_forward(*_read_tail_inputs(of_ref, ob_ref, pcols), gna_ref[...], lng_ref[...], lnb_ref[...],
                          ws_ref, bs_ref, wpaT_ref, wpbT_ref)
        mix = _dot(t["merged"], wo_ref[...])
        x1_ref[...] = x_ref[...] + mod_ref[0, 2:3, :] * mix
        mix_ref[...] = mix.astype(mix_ref.dtype)
        merged_ref[...] = t["merged"].astype(merged_ref.dtype)
        oa_ref[...] = t["oa"].astype(oa_ref.dtype)
        obm_ref[...] = t["obm"].astype(obm_ref.dtype)

    row = lambda w: pl.BlockSpec((tm, w), lambda i: (i, 0))
    in_specs = [row(KW), row(KW)] + _tail_in_specs(tm) + [row(D), pl.BlockSpec((1, N_MOD, D), lambda i: (i // per_b, 0, 0))]
    in_specs += _tail_weight_specs()
    return pl.pallas_call(
        body, name="tail_fwd", grid=(rows // tm,),
        out_shape=(jax.ShapeDtypeStruct((rows, D), F32), jax.ShapeDtypeStruct((rows, D), MXU_DTYPE),
                   jax.ShapeDtypeStruct((rows, D), MXU_DTYPE), jax.ShapeDtypeStruct((rows, KW), MXU_DTYPE),
                   jax.ShapeDtypeStruct((rows, KW), MXU_DTYPE)),
        in_specs=in_specs, out_specs=(row(D), row(D), row(D), row(KW), row(KW)),
        compiler_params=_params(("arbitrary",)),
    )(o_up, o_down, *([p] * 7), xt, modv, gna, ln_g, ln_b, w_s, b_s, w_paT, w_pbT, w_o)


def _tail_bwd(p, o_up, o_down, dx1, mix, modv, gna, ln_g, ln_b, w_s, b_s, w_paT, w_pbT, w_o, rows_per_example,
              after=None, sender=None):
    rows = dx1.shape[0]
    nb_ex = rows // rows_per_example
    tm = min(TOKEN_TILE, rows_per_example)
    per_b = rows_per_example // tm

    def body(of_ref, ob_ref, *rest):
        pcols = rest[:7]
        (dx1_ref, mix_ref, mod_ref, gna_ref, lng_ref, lnb_ref, ws_ref, bs_ref, wpaT_ref, wpbT_ref, wo_ref,
         dpt_ref, do_ref, dmix_ref, dpa_ref, dpb_ref, dmod_ref, small_ref, dws_ref, dbs_ref) = rest[7:]
        i = pl.program_id(0)

        @pl.when(i == 0)
        def _():
            small_ref[...] = jnp.zeros_like(small_ref)
            dws_ref[...] = jnp.zeros_like(dws_ref)
            dbs_ref[...] = jnp.zeros_like(dbs_ref)

        @pl.when(i % per_b == 0)
        def _():
            dmod_ref[...] = jnp.zeros_like(dmod_ref)

        osum, og, u, v, ga, gb = _read_tail_inputs(of_ref, ob_ref, pcols)
        ln_g = lng_ref[...]
        t = _tail_forward(osum, og, u, v, ga, gb, gna_ref[...], ln_g, lnb_ref[...], ws_ref, bs_ref, wpaT_ref, wpbT_ref)
        dx1v = dx1_ref[...]
        dmod_ref[0, 2:3, :] += jnp.sum(dx1v * mix_ref[...].astype(F32), axis=0, keepdims=True)
        dmix = dx1v * mod_ref[0, 2:3, :]
        dmix_ref[...] = dmix.astype(dmix_ref.dtype)
        dmerged = _dot(dmix, wo_ref[...], "nt")
        sga, sgb = t["sga"], t["sgb"]
        dpa = dmerged * sga
        dpb = dmerged * sgb
        dpa_ref[...] = dpa.astype(dpa_ref.dtype)
        dpb_ref[...] = dpb.astype(dpb_ref.dtype)
        dga = dmerged * t["pa"] * sga * (1.0 - sga)
        dgb = dmerged * t["pb"] * sgb * (1.0 - sgb)
        doa = _dot(dpa, wpaT_ref[...])
        dobm = _dot(dpb, wpbT_ref[...])
        dug = dobm * t["mixed"]
        dmixed = dobm * t["ug"]
        du = dug * _gelu_grad(u, t["tu"])
        dvln_blocks = []
        for n in range(tm // SGU_BLOCK):
            rs = slice(n * SGU_BLOCK, (n + 1) * SGU_BLOCK)
            parts = []
            for g in range(GROUPS):
                gs = slice(g * DK, (g + 1) * DK)
                dm = dmixed[rs, gs]
                parts.append(_dot(ws_ref[g], dm, "tn"))
                dws_ref[g] += _dot(dm, t["vln"][rs, gs], "nt")
                dbs_ref[g] += jnp.sum(dm, axis=1, keepdims=True)
            dvln_blocks.append(jnp.concatenate(parts, axis=1))
        dvln = jnp.concatenate(dvln_blocks, axis=0) if len(dvln_blocks) > 1 else dvln_blocks[0]
        vhat = t["vhat"]
        small_ref[1:2, 0:KW] += jnp.sum(dvln * vhat, axis=0, keepdims=True)
        small_ref[2:3, 0:KW] += jnp.sum(dvln, axis=0, keepdims=True)
        dvhat = dvln * ln_g
        dvg = t["rstd"] * (dvhat - jnp.mean(dvhat, axis=-1, keepdims=True)
                           - vhat * jnp.mean(dvhat * vhat, axis=-1, keepdims=True))
        dv = dvg * _gelu_grad(v, t["tv"])
        sg_og = t["sg_og"]
        doan = doa * t["silu_og"]
        dog = doa * t["oan"] * (sg_og * (1.0 + og * (1.0 - sg_og)))
        prod = doan * t["on"]
        dgna = jnp.zeros((1, DK), F32)
        for h in range(HEADS):
            dgna = dgna + jnp.sum(prod[:, h * DK:(h + 1) * DK], axis=0, keepdims=True)
        small_ref[0:1, 0:DK] += dgna
        don = doan * t["gna4"]
        dot_parts = []
        for h in range(HEADS):
            hs = slice(h * DK, (h + 1) * DK)
            m = jnp.mean(don[:, hs] * t["on"][:, hs], axis=-1, keepdims=True)
            dot_parts.append(t["r"][:, hs] * (don[:, hs] - t["on"][:, hs] * m))
        do_ref[...] = jnp.concatenate(dot_parts, axis=1).astype(do_ref.dtype)
        for j, val in enumerate((dog, du, dv)):
            dpt_ref[:, j * KW:(j + 1) * KW] = val.astype(dpt_ref.dtype)
        dpt_ref[:, 3 * KW:3 * KW + D] = dga.astype(dpt_ref.dtype)
        dpt_ref[:, 3 * KW + D:] = dgb.astype(dpt_ref.dtype)

    row = lambda w: pl.BlockSpec((tm, w), lambda i: (i, 0))
    in_specs = [row(KW), row(KW)] + _tail_in_specs(tm) + [row(D), row(D), pl.BlockSpec((1, N_MOD, D), lambda i: (i // per_b, 0, 0))]
    in_specs += _tail_weight_specs()
    args = [o_up, o_down, *([p] * 7), dx1, mix, modv, gna, ln_g, ln_b, w_s, b_s, w_paT, w_pbT, w_o]
    cd = MXU_DTYPE
    res, handle = _host_call(
        body, "tail_bwd", (rows // tm,), in_specs, args,
        [jax.ShapeDtypeStruct((rows, TAIL_COLS), cd), jax.ShapeDtypeStruct((rows, KW), cd),
         jax.ShapeDtypeStruct((rows, D), cd), jax.ShapeDtypeStruct((rows, D), cd),
         jax.ShapeDtypeStruct((rows, D), cd), jax.ShapeDtypeStruct((nb_ex, 8, D), F32),
         jax.ShapeDtypeStruct((8, D), F32), jax.ShapeDtypeStruct((GROUPS, SGU_BLOCK, SGU_BLOCK), F32),
         jax.ShapeDtypeStruct((GROUPS, SGU_BLOCK, 1), F32)],
        [row(TAIL_COLS), row(KW), row(D), row(D), row(D),
         pl.BlockSpec((1, 8, D), lambda i: (i // per_b, 0, 0)), _full((8, D)),
         _full((GROUPS, SGU_BLOCK, SGU_BLOCK)), _full((GROUPS, SGU_BLOCK, 1))], [],
        after=after, sender=sender)
    return (*res, handle)


def _ffn(x1, target, modv, g_ffn, g_final, w_upT, w_down, rows_per_example):
    rows = x1.shape[0]
    nb_ex = rows // rows_per_example
    tm = min(TOKEN_TILE, rows_per_example)
    per_b = rows_per_example // tm
    n_ff = D_FF // FF_CHUNK

    def body(x1_ref, tgt_ref, mod_ref, gffn_ref, gfin_ref, wup_ref, wdn_ref,
             dx1_ref, h2_ref, dffn_ref, act_ref, dup_ref, dmod_ref, small_ref, a_scr, b_scr):
        i = pl.program_id(0)

        @pl.when(i == 0)
        def _():
            small_ref[...] = jnp.zeros_like(small_ref)

        @pl.when(i % per_b == 0)
        def _():
            dmod_ref[...] = jnp.zeros_like(dmod_ref)

        x1v = x1_ref[...]
        g2 = gffn_ref[...]
        m3, m4, m5 = mod_ref[0, 3:4, :], mod_ref[0, 4:5, :], mod_ref[0, 5:6, :]
        r2 = lax.rsqrt(jnp.mean(x1v * x1v, axis=-1, keepdims=True) + EPS)
        xn2 = x1v * r2
        h2 = (xn2 * g2) * (1.0 + m4) + m3
        h2b = h2.astype(MXU_DTYPE)
        h2_ref[...] = h2b
        for j in range(n_ff):
            cs = slice(j * FF_CHUNK, (j + 1) * FF_CHUNK)
            a = _dot(h2b, wup_ref[j * FF_CHUNK:(j + 1) * FF_CHUNK, :], "nt")
            bgate = _dot(h2b, wup_ref[D_FF + j * FF_CHUNK:D_FF + (j + 1) * FF_CHUNK, :], "nt")
            a_scr[:, cs] = a
            b_scr[:, cs] = bgate
            act_ref[:, cs] = (a * _sigmoid(a) * bgate).astype(MXU_DTYPE)
        ffn = _dot(act_ref[...], wdn_ref[...])
        x2 = x1v + m5 * ffn
        r3 = lax.rsqrt(jnp.mean(x2 * x2, axis=-1, keepdims=True) + EPS)
        xn3 = x2 * r3
        gf = gfin_ref[...]
        err = xn3 * gf - tgt_ref[...]
        loss = 0.5 * jnp.sum(jnp.mean(err * err, axis=-1, keepdims=True), axis=0, keepdims=True)
        small_ref[2:3, :] += jnp.broadcast_to(loss, (1, D))
        dy = err * (1.0 / D)
        small_ref[1:2, :] += jnp.sum(dy * xn3, axis=0, keepdims=True)
        dxn3 = dy * gf
        dx2 = r3 * (dxn3 - xn3 * jnp.mean(dxn3 * xn3, axis=-1, keepdims=True))
        dmod_ref[0, 5:6, :] += jnp.sum(dx2 * ffn, axis=0, keepdims=True)
        dffn = (dx2 * m5).astype(MXU_DTYPE)
        dffn_ref[...] = dffn
        for j in range(n_ff):
            cs = slice(j * FF_CHUNK, (j + 1) * FF_CHUNK)
            dact = _dot(dffn, wdn_ref[cs, :], "nt")
            a, bgate = a_scr[:, cs], b_scr[:, cs]
            s = _sigmoid(a)
            dup_ref[:, cs] = (dact * bgate * (s * (1.0 + a * (1.0 - s)))).astype(MXU_DTYPE)
            dup_ref[:, D_FF + j * FF_CHUNK:D_FF + (j + 1) * FF_CHUNK] = (dact * a * s).astype(MXU_DTYPE)
        dh2 = _dot(dup_ref[...], wup_ref[...])
        dmod_ref[0, 3:4, :] += jnp.sum(dh2, axis=0, keepdims=True)
        dmod_ref[0, 4:5, :] += jnp.sum(dh2 * xn2 * g2, axis=0, keepdims=True)
        small_ref[0:1, :] += jnp.sum(dh2 * (1.0 + m4) * xn2, axis=0, keepdims=True)
        dxn2 = dh2 * g2 * (1.0 + m4)
        dx1_ref[...] = dx2 + r2 * (dxn2 - xn2 * jnp.mean(dxn2 * xn2, axis=-1, keepdims=True))

    row = lambda w: pl.BlockSpec((tm, w), lambda i: (i, 0))
    cd = MXU_DTYPE
    return pl.pallas_call(
        body, name="ffn_fwd_bwd", grid=(rows // tm,),
        out_shape=(jax.ShapeDtypeStruct((rows, D), F32), jax.ShapeDtypeStruct((rows, D), cd),
                   jax.ShapeDtypeStruct((rows, D), cd), jax.ShapeDtypeStruct((rows, D_FF), cd),
                   jax.ShapeDtypeStruct((rows, 2 * D_FF), cd), jax.ShapeDtypeStruct((nb_ex, 8, D), F32),
                   jax.ShapeDtypeStruct((8, D), F32)),
        in_specs=[row(D), row(D), pl.BlockSpec((1, N_MOD, D), lambda i: (i // per_b, 0, 0)), _full((1, D)), _full((1, D)),
                  _full((2 * D_FF, D), single=True), _full((D_FF, D), single=True)],
        out_specs=(row(D), row(D), row(D), row(D_FF), row(2 * D_FF),
                   pl.BlockSpec((1, 8, D), lambda i: (i // per_b, 0, 0)), _full((8, D))),
        scratch_shapes=[pltpu.VMEM((tm, D_FF), F32), pltpu.VMEM((tm, D_FF), F32)],
        compiler_params=_params(("arbitrary",)),
    )(x1, target, modv, g_ffn, g_final, w_upT, w_down)


def _scan_columns(up, down, n_groups):
    cols = [up[:, 0:KW].astype(F32), down[:, 0:KW].astype(F32)]
    for j in range(1, n_groups):
        cols.append(up[:, j * KW:(j + 1) * KW].astype(F32) + down[:, j * KW:(j + 1) * KW].astype(F32))
    return cols


def _inproj_bwd(d_up, d_down, dpt, xt, dx1, modv, g, w_inT, rows_per_example, name, sender=None):
    rows = xt.shape[0]
    latent = dx1 is not None
    n_cols = IN_COLS if latent else CTX_COLS
    n_groups = d_up.shape[1] // KW
    tm = min(TOKEN_TILE, rows_per_example)
    per_b = rows_per_example // tm
    n_mod_blocks = rows // rows_per_example if latent else 1

    def body(*refs):
        it = iter(refs)
        up_ref, down_ref = next(it), next(it)
        dpt_ref = next(it) if latent else None
        x_ref = next(it)
        dx1_ref = next(it) if latent else None
        mod_ref, g_ref, w_ref = next(it), next(it), next(it)
        gx_ref = next(it) if latent else None
        dp_out = None if latent else next(it)
        dmod_ref, small_ref = next(it), next(it)
        dp_ref = next(it) if latent else dp_out
        i = pl.program_id(0)

        @pl.when(i == 0)
        def _():
            small_ref[...] = jnp.zeros_like(small_ref)

        @pl.when((i % per_b == 0) if latent else (i == 0))
        def _():
            dmod_ref[...] = jnp.zeros_like(dmod_ref)

        for j, val in enumerate(_scan_columns(up_ref[...], down_ref[...], n_groups)):
            dp_ref[:, j * KW:(j + 1) * KW] = val.astype(MXU_DTYPE)
        if latent:
            dp_ref[:, 4 * KW:] = dpt_ref[...]
        dh = _dot(dp_ref[...], w_ref[...])
        x = x_ref[...]
        gv = g_ref[...]
        m1 = mod_ref[0, 1:2, :]
        r = lax.rsqrt(jnp.mean(x * x, axis=-1, keepdims=True) + EPS)
        xn = x * r
        dmod_ref[0, 0:1, :] += jnp.sum(dh, axis=0, keepdims=True)
        dmod_ref[0, 1:2, :] += jnp.sum(dh * xn * gv, axis=0, keepdims=True)
        small_ref[0:1, :] += jnp.sum(dh * (1.0 + m1) * xn, axis=0, keepdims=True)
        if latent:
            dxn = dh * gv * (1.0 + m1)
            gx_ref[...] = dx1_ref[...] + r * (dxn - xn * jnp.mean(dxn * xn, axis=-1, keepdims=True))

    row = lambda w: pl.BlockSpec((tm, w), lambda i: (i, 0))
    mod_idx = (lambda i: (i // per_b, 0, 0)) if latent else (lambda i: (0, 0, 0))
    in_specs = [row(n_groups * KW)] * 2 + ([row(TAIL_COLS)] if latent else []) + [row(D)] + ([row(D)] if latent else [])
    in_specs += [pl.BlockSpec((1, N_MOD, D), mod_idx), _full((1, D)),
                 pl.BlockSpec((n_cols, D), lambda i: (0, 0), pipeline_mode=pl.Buffered(1))]
    args = [d_up, d_down] + ([dpt] if latent else []) + [xt] + ([dx1] if latent else []) + [modv, g, w_inT]
    first = jax.ShapeDtypeStruct((rows, D), F32) if latent else jax.ShapeDtypeStruct((rows, n_cols), MXU_DTYPE)
    out_shape = [first, jax.ShapeDtypeStruct((n_mod_blocks, 8, D), F32), jax.ShapeDtypeStruct((8, D), F32)]
    out_specs = [row(D) if latent else row(n_cols), pl.BlockSpec((1, 8, D), mod_idx), _full((8, D))]
    scratch = [pltpu.VMEM((tm, n_cols), MXU_DTYPE)] if latent else []
    res, handle = _host_call(body, name, (rows // tm,), in_specs, args, out_shape, out_specs, scratch, sender=sender)
    return (*res, handle)


def _grad_matmul(a, b, name, init=None, tn=512, sender=None):
    rows, n = a.shape
    k = b.shape[1]
    tn = min(tn, n)
    has_init = init is not None
    init_blocks = init.shape[0] // tn if has_init else 0

    def body(*refs):
        if has_init:
            a_ref, b_ref, init_ref, o_ref = refs
        else:
            a_ref, b_ref, o_ref = refs
        g = _dot(a_ref[...], b_ref[...], "tn")
        if has_init:
            g = g + jnp.where(pl.program_id(0) < init_blocks, init_ref[...].astype(F32), 0.0)
        o_ref[...] = g.astype(o_ref.dtype)

    in_specs = [pl.BlockSpec((rows, tn), lambda i: (0, i)), _full((rows, k), single=True)]
    args = [a, b]
    if has_init:
        in_specs.append(pl.BlockSpec((tn, k), lambda i: (jnp.minimum(i, init_blocks - 1), 0)))
        args.append(init)
    (out,), handle = _host_call(
        body, name, (n // tn,), in_specs, args, [jax.ShapeDtypeStruct((n, k), PAYLOAD_DTYPE)],
        [pl.BlockSpec((tn, k), lambda i: (i, 0))], [], sender=sender)
    return out, handle


def _grad_in(d_up, d_down, dpt, h, init, sender=None):
    rows = h.shape[0]
    tn = 256
    per_group = KW // tn
    n_scan = 4 * per_group
    init_blocks = init.shape[0] // tn

    def body(up_ref, down_ref, dpt_ref, h_ref, init_ref, o_ref):
        i = pl.program_id(0)
        both = (up_ref[...].astype(F32) + down_ref[...].astype(F32)).astype(MXU_DTYPE)
        a = jnp.where(i < per_group, up_ref[...],
                      jnp.where(i < 2 * per_group, down_ref[...], jnp.where(i < n_scan, both, dpt_ref[...])))
        g = _dot(a, h_ref[...], "tn") + jnp.where(i < init_blocks, init_ref[...].astype(F32), 0.0)
        o_ref[...] = g.astype(o_ref.dtype)

    last = 3 * per_group - 1
    col = lambda f: pl.BlockSpec((rows, tn), lambda i: (0, f(i)))
    in_specs = [col(lambda i: jnp.clip(jnp.where(i < per_group, i, i - per_group), 0, last)),
                col(lambda i: jnp.clip(i - per_group, 0, last)),
                col(lambda i: jnp.clip(i - n_scan, 0, TAIL_COLS // tn - 1)),
                _full((rows, D), single=True),
                pl.BlockSpec((tn, D), lambda i: (jnp.minimum(i, init_blocks - 1), 0))]
    (out,), handle = _host_call(
        body, "gw_in", (IN_COLS // tn,), in_specs, [d_up, d_down, dpt, h, init],
        [jax.ShapeDtypeStruct((IN_COLS, D), PAYLOAD_DTYPE)], [pl.BlockSpec((tn, D), lambda i: (i, 0))], [],
        sender=sender)
    return out, handle


def _row_tile(rows, limit=256):
    if rows <= limit:
        return rows
    for t in range(limit, 7, -8):
        if rows % t == 0:
            return t
    return rows


def _sum8(stack, name):
    _, rows, cols = stack.shape
    tr = _row_tile(rows)

    def body(s_ref, o_ref):
        acc = s_ref[0].astype(F32)
        for j in range(1, N_DEV):
            acc = acc + s_ref[j].astype(F32)
        o_ref[...] = acc

    return pl.pallas_call(
        body, name=name, grid=(rows // tr,), out_shape=jax.ShapeDtypeStruct((rows, cols), F32),
        in_specs=[pl.BlockSpec((N_DEV, tr, cols), lambda i: (0, i, 0))],
        out_specs=pl.BlockSpec((tr, cols), lambda i: (i, 0)),
        compiler_params=_params(("arbitrary",)),
    )(stack)


def _small_reduce(early, late, gam, nb_ex):
    def body(s_ref, l_ref, gam_ref, o_ref, bm_ref):
        acc = s_ref[0] + l_ref[0]
        for j in range(1, N_DEV):
            acc = acc + (s_ref[j] + l_ref[j])
        o_ref[...] = acc
        bm = acc[8:8 + N_MOD, :]
        for e in range(nb_ex):
            bm = bm + acc[16 + e * N_MOD:16 + (e + 1) * N_MOD, :]
        lb = jnp.concatenate([_lower_bound(gam_ref, 0), _lower_bound(gam_ref, 1)], axis=1)
        dgam = acc[7:8, :] * lb * (1.0 - lb)
        bm_ref[...] = jnp.concatenate([bm, dgam, -dgam], axis=0)

    return pl.pallas_call(
        body, name="small_reduce", grid=(1,),
        out_shape=(jax.ShapeDtypeStruct((SMALL_ROWS, D), F32), jax.ShapeDtypeStruct((8, D), F32)),
        in_specs=[_full((N_DEV, SMALL_ROWS, D)), _full((N_DEV, SMALL_ROWS, D)), _full((4, KW))],
        out_specs=(_full((SMALL_ROWS, D)), _full((8, D))),
        compiler_params=_params(("arbitrary",)),
    )(early, late, gam)


def _adamw_update(w, gv, m, v):
    nm = ADAM_B1 * m + (1.0 - ADAM_B1) * gv
    nv = ADAM_B2 * v + (1.0 - ADAM_B2) * (gv * gv)
    m_hat = nm / (1.0 - ADAM_B1 ** ADAM_STEP)
    v_hat = nv / (1.0 - ADAM_B2 ** ADAM_STEP)
    return -ADAM_LR * (m_hat / (jnp.sqrt(v_hat) + ADAM_EPS) + ADAM_WD * w), nm, nv


def _adamw_sum8(stack, w, m, v, name):
    _, rows, cols = stack.shape
    tr = _row_tile(rows)

    def body(s_ref, w_ref, m_ref, v_ref, g_ref, d_ref, nm_ref, nv_ref):
        gv = s_ref[0].astype(F32)
        for j in range(1, N_DEV):
            gv = gv + s_ref[j].astype(F32)
        g_ref[...] = gv
        d_ref[...], nm_ref[...], nv_ref[...] = _adamw_update(w_ref[...], gv, m_ref[...], v_ref[...])

    blk = pl.BlockSpec((tr, cols), lambda i: (i, 0))
    sd = jax.ShapeDtypeStruct((rows, cols), F32)
    return pl.pallas_call(
        body, name=name, grid=(rows // tr,), out_shape=(sd, sd, sd, sd),
        in_specs=[pl.BlockSpec((N_DEV, tr, cols), lambda i: (0, i, 0)), blk, blk, blk], out_specs=(blk, blk, blk, blk),
        compiler_params=_params(("arbitrary",)),
    )(stack, w, m, v)


def _adamw(w, g, m, v, name):
    shape = w.shape
    cols = shape[-1]
    rows = 1
    for s in shape[:-1]:
        rows *= s
    tr = _row_tile(rows)

    def body(w_ref, g_ref, m_ref, v_ref, d_ref, nm_ref, nv_ref):
        gv = g_ref[...]
        nm = ADAM_B1 * m_ref[...] + (1.0 - ADAM_B1) * gv
        nv = ADAM_B2 * v_ref[...] + (1.0 - ADAM_B2) * (gv * gv)
        m_hat = nm / (1.0 - ADAM_B1 ** ADAM_STEP)
        v_hat = nv / (1.0 - ADAM_B2 ** ADAM_STEP)
        d_ref[...] = -ADAM_LR * (m_hat / (jnp.sqrt(v_hat) + ADAM_EPS) + ADAM_WD * w_ref[...])
        nm_ref[...] = nm
        nv_ref[...] = nv

    blk = pl.BlockSpec((tr, cols), lambda i: (i, 0))
    sd = jax.ShapeDtypeStruct((rows, cols), F32)
    d, nm, nv = pl.pallas_call(
        body, name=name, grid=(rows // tr,), out_shape=(sd, sd, sd), in_specs=[blk] * 4, out_specs=(blk, blk, blk),
        compiler_params=_params(("arbitrary",)),
    )(w.reshape(rows, cols), g.reshape(rows, cols), m.reshape(rows, cols), v.reshape(rows, cols))
    return d.reshape(shape), nm.reshape(shape), nv.reshape(shape)


def _owner_blocks(a):
    return a.reshape(N_DEV, a.shape[0] // N_DEV, a.shape[1])


class _LocalWeights:
    def __init__(self, w_upT, w_down, w_o, w_paT, w_pbT):
        self.weights = (w_upT, w_down, w_o, w_paT, w_pbT)
        self.items = {}

    def sender(self, stage, items=None):
        self.items[stage] = items
        return None

    def sent(self, stage, handle):
        pass

    def mixer_weights(self, after):
        return self.weights[1:]

    def ffn_weights(self, after):
        return self.weights[0]


def _local_step(x, ctx, target, modv, mcv, gam, g_mix, g_ffn, gna, ln_g, ln_b, w_s, b_s, g_final, w_inT, comm):
    nb_ex, seq, _ = x.shape
    ctx_len = ctx.shape[1]
    xt = x.reshape(nb_ex * seq, D)
    ct = ctx.reshape(nb_ex * ctx_len, D)
    tgt = target.reshape(nb_ex * seq, D)
    bs3 = b_s.reshape(GROUPS, SGU_BLOCK, 1)

    pc, hc, _ = _inproj(ct, mcv, g_mix, w_inT, CTX_COLS, ctx_len, "inproj_ctx")
    p, h, handle = _inproj(xt, modv, g_mix, w_inT, IN_COLS, seq, "inproj_lat", sender=comm.sender("inproj"))
    comm.sent("inproj", handle)
    cst_f, cst_b, s_ctx, _ = _hgrn_fwd(pc, gam, None, ctx_len, False, "hgrn_fwd_ctx")
    o_up, o_down, st_f, st_b, _, handle = _hgrn_fwd(p, gam, s_ctx, seq, True, "hgrn_fwd_lat",
                                                    sender=comm.sender("scan"))
    comm.sent("scan", handle)
    w_down, w_o, w_paT, w_pbT = comm.mixer_weights(o_up)
    x1, mix, merged, oa, obm = _tail_fwd(p, o_up, o_down, xt, modv, gna, ln_g, ln_b, w_s, bs3, w_paT, w_pbT, w_o, seq)
    w_upT = comm.ffn_weights(x1)
    dx1, h2, dffn, act, dup, dmod_ffn, small_ffn = _ffn(x1, tgt, modv, g_ffn, g_final, w_upT, w_down, seq)
    gw_upT, _ = _grad_matmul(dup, h2, "gw_up")
    gw_down, _ = _grad_matmul(act, dffn, "gw_down", tn=256)
    scatter = lambda *grads: [(_owner_blocks(g), "scatter") for g in grads]
    dpt, do, dmix, dpa, dpb, dmod_tail, small_tail, dws, dbs, handle = _tail_bwd(
        p, o_up, o_down, dx1, mix, modv, gna, ln_g, ln_b, w_s, bs3, w_paT, w_pbT, w_o, seq,
        sender=comm.sender("tail_bwd", scatter(gw_upT)))
    comm.sent("tail_bwd", handle)
    gw_o, _ = _grad_matmul(merged, dmix, "gw_o")
    gw_paT, _ = _grad_matmul(dpa, oa, "gw_pa")
    gw_pbT, _ = _grad_matmul(dpb, obm, "gw_pb")
    def at_row(row, a):
        return jnp.pad(a, ((row, SMALL_ROWS - row - a.shape[0]), (0, D - a.shape[1])))

    small_early = (at_row(1, small_ffn[0:2])
                   + at_row(3, small_tail[0:3])
                   + at_row(6, dbs.reshape(1, GROUPS * SGU_BLOCK))
                   + at_row(14, small_ffn[2:3]))
    dws_rows = dws.reshape(GROUPS * SGU_BLOCK, SGU_BLOCK)
    d_up, d_down, dlb, ds0, handle = _hgrn_bwd(
        p, gam, do, st_f, st_b, None, seq, True, "hgrn_bwd_lat",
        sender=comm.sender("scan_bwd", scatter(gw_down, gw_o, gw_paT, gw_pbT)
                           + [(small_early, "gather"), (dws_rows, "gather")]))
    comm.sent("scan_bwd", handle)
    c_up, c_down, dlb_c, _, _ = _hgrn_bwd(pc, gam, None, cst_f, cst_b, ds0, ctx_len, False, "hgrn_bwd_ctx")
    dpc, dmc, small_c, _ = _inproj_bwd(c_up, c_down, None, ct, None, mcv, g_mix, w_inT, ctx_len, "inproj_bwd_ctx")
    gw_inT, _ = _grad_in(d_up, d_down, dpt, h, _grad_matmul(dpc, hc, "gw_in_ctx")[0])
    grad_x, dmod_in, small_in, handle = _inproj_bwd(d_up, d_down, dpt, xt, dx1, modv, g_mix, w_inT, seq,
                                                   "inproj_bwd_lat", sender=comm.sender("inproj_bwd", scatter(gw_inT)))
    comm.sent("inproj_bwd", handle)
    dmod = dmod_in + dmod_tail + dmod_ffn
    small_late = (at_row(0, small_in[0:1] + small_c[0:1])
                  + at_row(7, (dlb + dlb_c).reshape(1, 2 * KW))
                  + at_row(8, dmc[0, 0:N_MOD])
                  + at_row(16, dmod[:, 0:N_MOD].reshape(nb_ex * N_MOD, D)))
    comm.sender("last", [(small_late, "gather")])
    return grad_x.reshape(x.shape)


def kernel(x, c, ctx, c_ctx, w_mod, b_mod, g_mix, g_ffn, w_in, lb_gamma, g_norm_a, ln_v_g, ln_v_b, w_s, b_s, w_pa, w_pb, w_o, w_up, w_down, g_final, loss_target, m_c_ctx, m_w_mod, m_b_mod, m_g_mix, m_g_ffn, m_w_in, m_lb_gamma, m_g_norm_a, m_ln_v_g, m_ln_v_b, m_w_s, m_b_s, m_w_pa, m_w_pb, m_w_o, m_w_up, m_w_down, m_g_final, v_c_ctx, v_w_mod, v_b_mod, v_g_mix, v_g_ffn, v_w_in, v_lb_gamma, v_g_norm_a, v_ln_v_g, v_ln_v_b, v_w_s, v_b_s, v_w_pa, v_w_pb, v_w_o, v_w_up, v_w_down, v_g_final):
    nb_ex = x.shape[0]
    me = 4 * lax.axis_index("x") + 2 * lax.axis_index("y") + lax.axis_index("c")
    cd = MXU_DTYPE
    mod_cols = w_mod.shape[2]
    lb_cols = lb_gamma.shape[2]

    w_inT_l = w_in[0].T.astype(cd)
    w_upT_l = w_up[0].T.astype(cd)
    w_paT_l = w_pa[0].T.astype(cd)
    w_pbT_l = w_pb[0].T.astype(cd)
    cl = jnp.concatenate([c, jnp.pad(lb_gamma.reshape(1, 4 * lb_cols), ((0, 0), (0, D - 4 * lb_cols))),
                          jnp.zeros((8 - nb_ex - 1, D), F32)], axis=0)
    g_in, g_cl = _gather_two_level([w_inT_l, cl], "gather_w_in")
    w_inT = g_in.reshape(IN_COLS, D)
    c_all = g_cl[:, 0:nb_ex].reshape(N_DEV * nb_ex, D)
    gam = jnp.transpose(g_cl[:, nb_ex, 0:4 * lb_cols].reshape(N_DEV, 4, lb_cols), (1, 0, 2)).reshape(4, KW)

    n_c = N_DEV * nb_ex
    cvec = jnp.concatenate([c_all, c_ctx.reshape(1, D), jnp.zeros((7, D), F32)], axis=0)
    b_mod_l = lax.dynamic_slice(b_mod, (0, me * mod_cols), (1, mod_cols))
    mod_l, svec = _mod_fwd(cvec, w_mod[0], b_mod_l)
    (g_mod,) = _exchange([(mod_l, "gather")], "gather_mod")
    mod_all = jnp.transpose(g_mod, (1, 0, 2)).reshape(n_c + 8, N_MOD * D)
    modv = lax.dynamic_slice(mod_all, (me * nb_ex, 0), (nb_ex, N_MOD * D)).reshape(nb_ex, N_MOD, D)
    mcv = mod_all[n_c].reshape(1, N_MOD, D)

    handles, leftover = {}, {}

    class Comm:
        def sender(self, stage, items=None):
            if stage == "inproj":
                return _Sender([(w_down[0].astype(cd), "gather"), (w_o[0].astype(cd), "gather"), (w_paT_l, "gather"),
                                (w_pbT_l, "gather")])
            if stage == "scan":
                return _Sender([(w_upT_l, "gather")])
            if stage == "last":
                leftover["items"] = items
                return None
            return _Sender(items)

        def sent(self, stage, handle):
            handles[stage] = handle

        def mixer_weights(self, after):
            g_down, g_o, g_pa, g_pb = _exchange_wait(handles["inproj"], after)
            return g_down.reshape(D_FF, D), g_o.reshape(D, D), g_pa.reshape(D, KW), g_pb.reshape(D, KW)

        def ffn_weights(self, after):
            (g_up,) = _exchange_wait(handles["scan"], after)
            return g_up.reshape(2 * D_FF, D)

    grad_x = _local_step(
        x, ctx, loss_target, modv, mcv, gam, g_mix, g_ffn, g_norm_a, ln_v_g, ln_v_b, w_s[0], b_s[0],
        g_final.reshape(1, D), w_inT, Comm())
    last, last_started = _exchange_start(leftover["items"], "gather_small_late", after=leftover["items"][0][0])

    (r_up,) = _exchange_wait(handles["tail_bwd"], last_started)
    r_down, r_o, r_pa, r_pb, r_small, r_dws = _exchange_wait(handles["scan_bwd"], r_up)
    raw_up = _adamw_sum8(r_up, w_up[0].T, m_w_up[0].T, v_w_up[0].T, "adamw_w_up")
    raw_down = _adamw_sum8(r_down, w_down[0], m_w_down[0], v_w_down[0], "adamw_w_down")
    raw_o = _adamw_sum8(r_o, w_o[0], m_w_o[0], v_w_o[0], "adamw_w_o")
    (r_in,) = _exchange_wait(handles["inproj_bwd"], raw_o[1])
    raw_in = _adamw_sum8(r_in, w_in[0].T, m_w_in[0].T, v_w_in[0].T, "adamw_w_in")
    (r_late,) = _exchange_wait(last, raw_in[1])
    done = {"w_in": [a.T[None] for a in raw_in], "w_up": [a.T[None] for a in raw_up],
            "w_down": [a[None] for a in raw_down], "w_o": [a[None] for a in raw_o]}
    grad_w_in, grad_w_up, grad_w_down, grad_w_o = (done[k][0] for k in ("w_in", "w_up", "w_down", "w_o"))
    grad_w_pa = _sum8(r_pa, "sum_w_pa").T[None]
    grad_w_pb = _sum8(r_pb, "sum_w_pb").T[None]
    tot, bm = _small_reduce(r_small, r_late, gam, nb_ex)
    loss = tot[14, 0]
    grad_g_mix, grad_g_ffn, grad_g_final = tot[0:1], tot[1:2], tot[2]
    grad_g_norm_a = tot[3:4, 0:DK]
    grad_ln_v_g, grad_ln_v_b = tot[4:5, 0:KW], tot[5:6, 0:KW]
    grad_b_s = tot[6, 0:GROUPS * SGU_BLOCK].reshape(1, GROUPS, SGU_BLOCK)
    grad_w_s = _sum8(r_dws, "sum_w_s").reshape(1, GROUPS, SGU_BLOCK, SGU_BLOCK)
    grad_b_mod = bm[0:N_MOD].reshape(1, N_MOD * D)
    grad_lb_gamma = lax.dynamic_slice(bm[6:8].reshape(2, 2, KW), (0, 0, me * lb_cols), (2, 2, lb_cols))

    dmod_all = r_late[:, 16:16 + nb_ex * N_MOD].reshape(n_c, N_MOD * D)
    dmod_l = jnp.concatenate([lax.dynamic_slice(dmod_all, (0, me * mod_cols), (n_c, mod_cols)),
                              lax.dynamic_slice(tot[8:8 + N_MOD].reshape(1, N_MOD * D), (0, me * mod_cols), (1, mod_cols)),
                              jnp.zeros((7, mod_cols), F32)], axis=0)
    gw_mod, gc = _mod_bwd(svec, cvec, dmod_l, w_mod[0])
    grad_w_mod = gw_mod[None]
    (r_gc,) = _exchange([(gc[n_c:n_c + 8], "gather")], "gather_c_ctx", after=r_late)
    grad_c_ctx = _sum8(r_gc, "sum_c_ctx")[0]

    names = ["c_ctx", "w_mod", "b_mod", "g_mix", "g_ffn", "w_in", "lb_gamma", "g_norm_a", "ln_v_g", "ln_v_b", "w_s",
             "b_s", "w_pa", "w_pb", "w_o", "w_up", "w_down", "g_final"]
    weights = [c_ctx, w_mod, b_mod, g_mix, g_ffn, w_in, lb_gamma, g_norm_a, ln_v_g, ln_v_b, w_s, b_s, w_pa, w_pb, w_o,
               w_up, w_down, g_final]
    grads = [grad_c_ctx, grad_w_mod, grad_b_mod, grad_g_mix, grad_g_ffn, grad_w_in, grad_lb_gamma, grad_g_norm_a,
             grad_ln_v_g, grad_ln_v_b, grad_w_s, grad_b_s, grad_w_pa, grad_w_pb, grad_w_o, grad_w_up, grad_w_down,
             grad_g_final]
    ms = [m_c_ctx, m_w_mod, m_b_mod, m_g_mix, m_g_ffn, m_w_in, m_lb_gamma, m_g_norm_a, m_ln_v_g, m_ln_v_b, m_w_s, m_b_s,
          m_w_pa, m_w_pb, m_w_o, m_w_up, m_w_down, m_g_final]
    vs = [v_c_ctx, v_w_mod, v_b_mod, v_g_mix, v_g_ffn, v_w_in, v_lb_gamma, v_g_norm_a, v_ln_v_g, v_ln_v_b, v_w_s, v_b_s,
          v_w_pa, v_w_pb, v_w_o, v_w_up, v_w_down, v_g_final]
    deltas, new_ms, new_vs = [], [], []
    for nm, w, g, m, v in zip(names, weights, grads, ms, vs):
        d, nm_, nv_ = done[nm][1:] if nm in done else _adamw(w, g.reshape(w.shape), m, v, "adamw_" + nm)
        deltas.append(d)
        new_ms.append(nm_)
        new_vs.append(nv_)
    grads = [g.reshape(w.shape) for g, w in zip(grads, weights)]
    return (loss, grad_x, *grads, *deltas, *new_ms, *new_vs)
```

```python
import functools

import jax
import jax.numpy as jnp
from jax import lax
from jax.experimental import pallas as pl
from jax.experimental.pallas import tpu as pltpu

F32 = jnp.float32
MXU_DTYPE = jnp.bfloat16
PAYLOAD_DTYPE = jnp.bfloat16

N_DEV = 8
D = 1024
HEADS = 4
DK = 128
KW = HEADS * DK
CHUNK = 64
SGU_BLOCK = 128
GROUPS = 4
D_FF = 2816
FF_CHUNK = 256
N_MOD = 6
IN_COLS = 5632
CTX_COLS = 1536
TAIL_COLS = IN_COLS - 4 * KW
EPS = 1e-6
ADAM_LR, ADAM_B1, ADAM_B2, ADAM_EPS, ADAM_WD, ADAM_STEP = 0.001, 0.9, 0.999, 1e-08, 0.01, 10

VMEM_LIMIT = 56 * 1024 * 1024
TOKEN_TILE = 256
SMALL_ROWS = 40


def _params(sem):
    return pltpu.CompilerParams(dimension_semantics=sem, vmem_limit_bytes=VMEM_LIMIT)


_DN = {"nn": (((1,), (0,)), ((), ())), "nt": (((1,), (1,)), ((), ())), "tn": (((0,), (0,)), ((), ()))}


def _dot(a, b, form="nn"):
    return lax.dot_general(a.astype(MXU_DTYPE), b.astype(MXU_DTYPE), _DN[form], preferred_element_type=F32)


def _mask_dot(mask, v):
    bf = jnp.bfloat16
    hi = v.astype(bf)
    r1 = v - hi.astype(F32)
    mid = r1.astype(bf)
    lo = (r1 - mid.astype(F32)).astype(bf)
    w = v.shape[1]
    s = lax.dot_general(mask.astype(bf), jnp.concatenate([hi, mid, lo], axis=1), _DN["nn"], preferred_element_type=F32)
    return (s[:, 2 * w:] + s[:, w:2 * w]) + s[:, :w]


def _full(shape, single=False):
    n = len(shape)
    if single:
        return pl.BlockSpec(shape, lambda *_: (0,) * n, pipeline_mode=pl.Buffered(1))
    return pl.BlockSpec(shape, lambda *_: (0,) * n)


def _ordered_behind(body, in_specs, args, after):
    if after is None:
        return body
    at = len(in_specs)
    in_specs.append(pl.BlockSpec(memory_space=pl.ANY))
    args.append(after)
    return lambda *refs: body(*refs[:at], *refs[at + 1:])


def _sigmoid(z):
    return 0.5 * jnp.tanh(0.5 * z) + 0.5


def _gelu(x):
    c = 0.7978845608028654
    t = jnp.tanh(c * (x + 0.044715 * x * x * x))
    return 0.5 * x * (1.0 + t), t


def _gelu_grad(x, t):
    c = 0.7978845608028654
    return 0.5 * (1.0 + t) + 0.5 * x * (1.0 - t * t) * c * (1.0 + 3 * 0.044715 * x * x)


def _exchange(items, name, after=None):
    n = len(items)
    out_shape = []
    for a, mode in items:
        blk = a.shape if mode == "gather" else a.shape[1:]
        out_shape.append(jax.ShapeDtypeStruct((N_DEV,) + tuple(blk), a.dtype))

    def body(*refs):
        srcs, dsts = refs[:n], refs[n:2 * n]
        send_sems, recv_sems, local_sems = refs[2 * n:]
        x, y, c = lax.axis_index("x"), lax.axis_index("y"), lax.axis_index("c")
        me = 4 * x + 2 * y + c

        def src_for(i, dev):
            return srcs[i] if items[i][1] == "gather" else srcs[i].at[dev]

        local = [pltpu.make_async_copy(src_for(i, me), dsts[i].at[me], local_sems.at[i]) for i in range(n)]
        for cp in local:
            cp.start()
        remote = []
        for k in range(1, N_DEV):
            px = jnp.bitwise_xor(x, (k >> 2) & 1)
            py = jnp.bitwise_xor(y, (k >> 1) & 1)
            pc = jnp.bitwise_xor(c, k & 1)
            peer = 4 * px + 2 * py + pc
            for i in range(n):
                cp = pltpu.make_async_remote_copy(
                    src_ref=src_for(i, peer), dst_ref=dsts[i].at[me],
                    send_sem=send_sems.at[i * (N_DEV - 1) + k - 1], recv_sem=recv_sems.at[i * (N_DEV - 1) + k - 1],
                    device_id=(px, py, pc), device_id_type=pl.DeviceIdType.MESH)
                cp.start()
                remote.append(cp)
        for cp in remote:
            cp.wait()
        for cp in local:
            cp.wait()

    any_spec = pl.BlockSpec(memory_space=pl.ANY)
    in_specs, args = [any_spec] * n, [a for a, _ in items]
    if after is not None:
        in_specs.append(any_spec)
        args.append(after)
        exchange = body
        body = lambda *refs: exchange(*refs[:n], *refs[n + 1:])
    return pl.pallas_call(
        body, name=name, out_shape=out_shape, in_specs=in_specs, out_specs=[any_spec] * n,
        scratch_shapes=[pltpu.SemaphoreType.DMA((n * (N_DEV - 1),)), pltpu.SemaphoreType.DMA((n * (N_DEV - 1),)),
                        pltpu.SemaphoreType.DMA((n,))],
    )(*args)


def _gather_two_level(arrays, name):
    n = len(arrays)

    def body(*refs):
        srcs, dsts = refs[:n], refs[n:2 * n]
        send_sems, recv_sems, local_sems = refs[2 * n:]
        x, y, c = lax.axis_index("x"), lax.axis_index("y"), lax.axis_index("c")
        sibling = (x, y, 1 - c)
        chips = [(1 - x, y), (x, 1 - y), (1 - x, 1 - y)]

        def slot(px, py, pc):
            return 4 * px + 2 * py + pc

        def copy(i, k, block, to, src=None):
            return pltpu.make_async_remote_copy(
                src_ref=dsts[i].at[slot(*block)] if src is None else src, dst_ref=dsts[i].at[slot(*block)],
                send_sem=send_sems.at[i * 7 + k], recv_sem=recv_sems.at[i * 7 + k],
                device_id=to, device_id_type=pl.DeviceIdType.MESH)

        me = (x, y, c)
        mine = [pltpu.make_async_copy(srcs[i], dsts[i].at[slot(*me)], local_sems.at[i]) for i in range(n)]
        for cp in mine:
            cp.start()
        first = []
        for j, chip in enumerate(chips):
            first += [copy(i, 1 + j, me, (*chip, c), src=srcs[i]) for i in range(n)]
        first += [copy(i, 0, me, sibling, src=srcs[i]) for i in range(n)]
        for cp in first:
            cp.start()
        passed = []
        for j, chip in enumerate(chips):
            for i in range(n):
                copy(i, 1 + j, (*chip, c), me).wait_recv()
                cp = copy(i, 4 + j, (*chip, c), sibling)
                cp.start()
                passed.append(cp)
        for i in range(n):
            copy(i, 0, sibling, me).wait_recv()
            for j, chip in enumerate(chips):
                copy(i, 4 + j, (*chip, 1 - c), me).wait_recv()
        for cp in first + passed:
            cp.wait_send()
        for cp in mine:
            cp.wait()

    any_spec = pl.BlockSpec(memory_space=pl.ANY)
    return pl.pallas_call(
        body, name=name, out_shape=[jax.ShapeDtypeStruct((N_DEV,) + a.shape, a.dtype) for a in arrays],
        in_specs=[any_spec] * n, out_specs=[any_spec] * n,
        scratch_shapes=[pltpu.SemaphoreType.DMA((n * 7,)), pltpu.SemaphoreType.DMA((n * 7,)),
                        pltpu.SemaphoreType.DMA((n,))],
    )(*arrays)


_HBM = pl.BlockSpec(memory_space=pltpu.HBM)
_SEM = pl.BlockSpec(memory_space=pltpu.SEMAPHORE)
_EFFECT = pltpu.SideEffectType.DATAFLOW_SIDE_EFFECTING


def _split_copies(items, srcs, lands, send_sems, recv_sems):
    x, y, c = lax.axis_index("x"), lax.axis_index("y"), lax.axis_index("c")
    me = 4 * x + 2 * y + c
    copies = []
    for k in range(1, N_DEV):
        px = jnp.bitwise_xor(x, (k >> 2) & 1)
        py = jnp.bitwise_xor(y, (k >> 1) & 1)
        pc = jnp.bitwise_xor(c, k & 1)
        peer = 4 * px + 2 * py + pc
        for i in range(len(items)):
            src = srcs[i] if items[i][1] == "gather" else srcs[i].at[peer]
            copies.append(pltpu.make_async_remote_copy(
                src_ref=src, dst_ref=lands[i].at[me],
                send_sem=send_sems.at[i * (N_DEV - 1) + k - 1], recv_sem=recv_sems.at[i * (N_DEV - 1) + k - 1],
                device_id=(px, py, pc), device_id_type=pl.DeviceIdType.MESH))
    return me, copies


def _exchange_start(items, name, after):
    n = len(items)
    n_sem = n * (N_DEV - 1)
    srcs, lands = [], []
    for a, mode in items:
        blk = a.shape if mode == "gather" else a.shape[1:]
        srcs.append(pltpu.with_memory_space_constraint(a, pltpu.HBM))
        lands.append(pltpu.with_memory_space_constraint(lax.empty((N_DEV,) + tuple(blk), a.dtype), pltpu.HBM))

    def body(*refs):
        src_refs, land_refs = refs[:n], refs[n:2 * n]
        send_sems, recv_sems = refs[2 * n + 1], refs[2 * n + 2]
        local_sems = refs[4 * n + 3]
        me, copies = _split_copies(items, src_refs, land_refs, send_sems, recv_sems)
        for i in range(n):
            own = src_refs[i] if items[i][1] == "gather" else src_refs[i].at[me]
            cp = pltpu.make_async_copy(own, land_refs[i].at[me], local_sems.at[i])
            cp.start()
            cp.wait()
        for cp in copies:
            cp.start()

    out_shape = [pltpu.SemaphoreType.DMA((n_sem,)), pltpu.SemaphoreType.DMA((n_sem,))]
    out_shape += [pltpu.HBM(a.shape, a.dtype) for a in srcs] + [pltpu.HBM(a.shape, a.dtype) for a in lands]
    outs = pl.pallas_call(
        body, name=name, out_shape=out_shape,
        in_specs=[_HBM] * (2 * n) + [pl.BlockSpec(memory_space=pl.ANY)],
        out_specs=[_SEM, _SEM] + [_HBM] * (2 * n),
        input_output_aliases={i: 2 + i for i in range(2 * n)},
        scratch_shapes=[pltpu.SemaphoreType.DMA((n,))],
        compiler_params=pltpu.CompilerParams(has_side_effects=_EFFECT),
    )(*srcs, *lands, after)
    handle = (items, name, outs[0], outs[1], outs[2:2 + n], outs[2 + n:2 + 2 * n])
    return handle, outs[2]


class _Sender:
    PIECE_ROWS = 352

    def __init__(self, items, chunks=None):
        self.items, self.n = items, len(items)
        self.chunks = chunks
        if chunks is None:
            block_rows = [a.shape[0] if mode == "gather" else a.shape[1] for a, mode in items]
            self.chunks = [r // self.PIECE_ROWS if r % self.PIECE_ROWS == 0 else 1 for r in block_rows]
        self.srcs, self.lands = [], []
        for a, mode in items:
            blk = a.shape if mode == "gather" else a.shape[1:]
            self.srcs.append(pltpu.with_memory_space_constraint(a, pltpu.HBM))
            self.lands.append(pltpu.with_memory_space_constraint(lax.empty((N_DEV,) + tuple(blk), a.dtype), pltpu.HBM))

    def issue(self, src_refs, land_refs, send_sems, recv_sems, local_sems, step, n_steps):
        x, y, c = lax.axis_index("x"), lax.axis_index("y"), lax.axis_index("c")
        me = 4 * x + 2 * y + c
        copies = []
        for ch in range(max(self.chunks)):
            for k in range(1, N_DEV):
                px = jnp.bitwise_xor(x, (k >> 2) & 1)
                py = jnp.bitwise_xor(y, (k >> 1) & 1)
                pc = jnp.bitwise_xor(c, k & 1)
                peer = 4 * px + 2 * py + pc
                for i, (_, mode) in enumerate(self.items):
                    if ch >= self.chunks[i]:
                        continue
                    n_rows = land_refs[i].shape[1] // self.chunks[i]
                    rows = pl.ds(ch * n_rows, n_rows)
                    src = src_refs[i].at[rows] if mode == "gather" else src_refs[i].at[peer].at[rows]
                    copies.append(pltpu.make_async_remote_copy(
                        src_ref=src, dst_ref=land_refs[i].at[me].at[rows],
                        send_sem=send_sems.at[i * (N_DEV - 1) + k - 1], recv_sem=recv_sems.at[i * (N_DEV - 1) + k - 1],
                        device_id=(px, py, pc), device_id_type=pl.DeviceIdType.MESH))
        own = [pltpu.make_async_copy(src_refs[i] if mode == "gather" else src_refs[i].at[me], land_refs[i].at[me],
                                     local_sems.at[i]) for i, (_, mode) in enumerate(self.items)]

        @pl.when(step == 0)
        def _():
            for cp in own:
                cp.start()

        for s in range(n_steps):
            group = [cp for j, cp in enumerate(copies) if (j * n_steps) // len(copies) == s]
            if group:
                @pl.when(step == s)
                def _(group=group):
                    for cp in group:
                        cp.start()

        @pl.when(step == n_steps - 1)
        def _():
            for cp in own:
                cp.wait()


def _host_call(body, name, grid, in_specs, args, out_shape, out_specs, scratch_shapes, after=None, sender=None):
    in_specs, args, out_shape, out_specs = list(in_specs), list(args), list(out_shape), list(out_specs)
    scratch_shapes = list(scratch_shapes)
    semantics = ("arbitrary",) * len(grid)
    body = _ordered_behind(body, in_specs, args, after)
    if sender is None:
        res = pl.pallas_call(body, name=name, grid=grid, in_specs=in_specs, out_specs=out_specs, out_shape=out_shape,
                             scratch_shapes=scratch_shapes, compiler_params=_params(semantics))(*args)
        return res, None
    n, n_in, n_out, n_scr = sender.n, len(in_specs), len(out_shape), len(scratch_shapes)
    n_sem = n * (N_DEV - 1)
    n_steps = 1
    for g in grid:
        n_steps *= g
    compute = body

    def body(*refs):
        ins, s_in = refs[:n_in], refs[n_in:n_in + 2 * n]
        o0 = n_in + 2 * n
        outs, s_out = refs[o0:o0 + n_out], refs[o0 + n_out:o0 + n_out + 2 + 2 * n]
        scr = refs[o0 + n_out + 2 + 2 * n:]
        compute(*ins, *outs, *scr[:n_scr])
        step = pl.program_id(0)
        for d in range(1, len(grid)):
            step = step * grid[d] + pl.program_id(d)
        sender.issue(s_in[:n], s_in[n:], s_out[0], s_out[1], scr[n_scr], step, n_steps)

    res = pl.pallas_call(
        body, name=name, grid=grid,
        in_specs=in_specs + [_HBM] * (2 * n), out_specs=out_specs + [_SEM, _SEM] + [_HBM] * (2 * n),
        out_shape=out_shape + [pltpu.SemaphoreType.DMA((n_sem,)), pltpu.SemaphoreType.DMA((n_sem,))]
        + [pltpu.HBM(a.shape, a.dtype) for a in sender.srcs] + [pltpu.HBM(a.shape, a.dtype) for a in sender.lands],
        input_output_aliases={n_in + j: n_out + 2 + j for j in range(2 * n)},
        scratch_shapes=scratch_shapes + [pltpu.SemaphoreType.DMA((n,))],
        compiler_params=pltpu.CompilerParams(dimension_semantics=semantics, vmem_limit_bytes=VMEM_LIMIT,
                                             has_side_effects=_EFFECT),
    )(*args, *sender.srcs, *sender.lands)
    handle = (sender.items, name, res[n_out], res[n_out + 1], res[n_out + 2:n_out + 2 + n],
              res[n_out + 2 + n:n_out + 2 + 2 * n])
    return res[:n_out], handle


def _exchange_wait(handle, after):
    items, name, send_sems, recv_sems, srcs, lands = handle
    n = len(items)

    def body(*refs):
        src_refs, land_refs = refs[:n], refs[n:2 * n]
        send_ref, recv_ref = refs[2 * n], refs[2 * n + 1]
        _, copies = _split_copies(items, src_refs, land_refs, send_ref, recv_ref)
        for cp in copies:
            cp.wait_send()
            cp.wait_recv()

    outs = pl.pallas_call(
        body, name=name + "_wait",
        out_shape=[pltpu.HBM(a.shape, a.dtype) for a in srcs] + [pltpu.HBM(a.shape, a.dtype) for a in lands],
        in_specs=[_HBM] * (2 * n) + [_SEM, _SEM, pl.BlockSpec(memory_space=pl.ANY)], out_specs=[_HBM] * (2 * n),
        input_output_aliases={i: i for i in range(2 * n)},
        compiler_params=pltpu.CompilerParams(has_side_effects=_EFFECT),
    )(*srcs, *lands, send_sems, recv_sems, after)
    return outs[n:]


def _mod_fwd(cvec, w_mod_l, b_mod_l):
    rows, cols = cvec.shape[0], w_mod_l.shape[1]

    def body(c_ref, w_ref, b_ref, o_ref, s_ref):
        cv = c_ref[...]
        s = cv * _sigmoid(cv)
        s_ref[...] = s
        o_ref[...] = _dot(s, w_ref[...]) + b_ref[...]

    return pl.pallas_call(
        body, name="mod_fwd",
        out_shape=(jax.ShapeDtypeStruct((rows, cols), F32), jax.ShapeDtypeStruct((rows, D), F32)),
        in_specs=[_full((rows, D)), _full((D, cols)), _full((1, cols))],
        out_specs=(_full((rows, cols)), _full((rows, D))), grid=(1,),
        compiler_params=_params(("arbitrary",)),
    )(cvec, w_mod_l, b_mod_l)


def _mod_bwd(svec, cvec, dmod_l, w_mod_l):
    rows, cols = dmod_l.shape

    def body(s_ref, c_ref, d_ref, w_ref, gw_ref, gc_ref):
        gw_ref[...] = _dot(s_ref[...], d_ref[...], "tn")
        cv = c_ref[...]
        sg = _sigmoid(cv)
        gc_ref[...] = _dot(d_ref[...], w_ref[...], "nt") * (sg * (1.0 + cv * (1.0 - sg)))

    return pl.pallas_call(
        body, name="mod_bwd",
        out_shape=(jax.ShapeDtypeStruct((D, cols), F32), jax.ShapeDtypeStruct((rows, D), F32)),
        in_specs=[_full((rows, D)), _full((rows, D)), _full((rows, cols)), _full((D, cols))],
        out_specs=(_full((D, cols)), _full((rows, D))), grid=(1,),
        compiler_params=_params(("arbitrary",)),
    )(svec, cvec, dmod_l, w_mod_l)


def _inproj(xt, modv, g, w_inT, n_cols, rows_per_example, name, after=None, sender=None):
    rows = xt.shape[0]
    tm = min(TOKEN_TILE, rows_per_example)
    per_b = rows_per_example // tm
    shared_mod = modv.shape[0] == 1

    def body(x_ref, mod_ref, g_ref, w_ref, p_ref, h_ref):
        x = x_ref[...]
        r = lax.rsqrt(jnp.mean(x * x, axis=-1, keepdims=True) + EPS)
        h = (x * r * g_ref[...]) * (1.0 + mod_ref[0, 1:2, :]) + mod_ref[0, 0:1, :]
        hb = h.astype(MXU_DTYPE)
        h_ref[...] = hb
        for j in range(n_cols // KW):
            p_ref[:, j * KW:(j + 1) * KW] = _dot(hb, w_ref[j * KW:(j + 1) * KW, :], "nt").astype(p_ref.dtype)

    mod_idx = (lambda i: (0, 0, 0)) if shared_mod else (lambda i: (i // per_b, 0, 0))
    in_specs = [pl.BlockSpec((tm, D), lambda i: (i, 0)), pl.BlockSpec((1, N_MOD, D), mod_idx), _full((1, D)),
                pl.BlockSpec((n_cols, D), lambda i: (0, 0), pipeline_mode=pl.Buffered(1))]
    (p, h), handle = _host_call(
        body, name, (rows // tm,), in_specs, [xt, modv, g, w_inT],
        [jax.ShapeDtypeStruct((rows, n_cols), MXU_DTYPE), jax.ShapeDtypeStruct((rows, D), MXU_DTYPE)],
        [pl.BlockSpec((tm, n_cols), lambda i: (i, 0)), pl.BlockSpec((tm, D), lambda i: (i, 0))], [],
        after=after, sender=sender)
    return p, h, handle


def _tri(reverse, n):
    row = lax.broadcasted_iota(jnp.int32, (n, n), 0)
    col = lax.broadcasted_iota(jnp.int32, (n, n), 1)
    same = (row // CHUNK) == (col // CHUNK)
    return same & ((col >= row) if reverse else (col <= row))


def _per_chunk_rows(x, reverse):
    n = x.shape[0]
    rows = [x[j * CHUNK:j * CHUNK + 1] if reverse else x[(j + 1) * CHUNK - 1:(j + 1) * CHUNK] for j in range(n // CHUNK)]
    return jnp.concatenate([jnp.broadcast_to(r, (CHUNK, x.shape[1])) for r in rows], axis=0), rows


def _lower_bound(gam_ref, direction):
    return _sigmoid(gam_ref[direction:direction + 1, :] - gam_ref[2 + direction:3 + direction, :])


def _gate_prep(z, lb, tri, reverse):
    sg = _sigmoid(z)
    f = lb + (1.0 - lb) * sg
    g = jnp.log(f)
    b = _mask_dot(tri, g)
    bl, bl_rows = _per_chunk_rows(b, reverse)
    mid = 0.5 * bl
    return sg, g, 1.0 - f, b, jnp.exp(mid), [jnp.exp(0.5 * r) for r in bl_rows], jnp.exp(mid - b), mid


def _hgrn_fwd(p, gam, s0, rows_per_example, with_out, name, sender=None):
    rows = p.shape[0]
    nb_ex = rows // rows_per_example
    rb = min(TOKEN_TILE, rows_per_example)
    cpb = rb // CHUNK
    nb = rows_per_example // rb
    n_chunks = rows // CHUNK
    has_s0 = s0 is not None

    def body(*refs):
        it = iter(refs)
        gam_ref = next(it)
        zf_ref, vf_ref = next(it), next(it)
        qf_ref = next(it) if with_out else None
        zb_ref, vb_ref = next(it), next(it)
        qb_ref = next(it) if with_out else None
        s0_ref = next(it) if has_s0 else None
        if with_out:
            of_ref, ob_ref = next(it), next(it)
        stash_f, stash_b, fin_ref = next(it), next(it), next(it)
        st_ref = next(it)
        i = pl.program_id(1)

        @pl.when(i == 0)
        def _():
            if has_s0:
                st_ref[...] = s0_ref[:, 0]
            else:
                st_ref[...] = jnp.zeros_like(st_ref)

        for direction, (z_ref, v_ref, q_ref, stash) in enumerate(
                ((zf_ref, vf_ref, qf_ref, stash_f), (zb_ref, vb_ref, qb_ref, stash_b))):
            reverse = direction == 1
            tri = _tri(reverse, rb)
            lb = _lower_bound(gam_ref, direction)
            z = z_ref[...].astype(F32)
            v = v_ref[...].astype(F32)
            _, _, k, b, em, em_rows, e2, mid = _gate_prep(z, lb, tri, reverse)
            kd = (k * (e2 * em)).astype(MXU_DTYPE)
            vb = v.astype(MXU_DTYPE)
            if with_out:
                q = q_ref[...].astype(F32)
                qi = q * jnp.exp(b - mid)
                qe = (qi * em).astype(MXU_DTYPE)
                qi = qi.astype(MXU_DTYPE)
                ki = (k * e2).astype(MXU_DTYPE)
                intra = []
                for h in range(HEADS):
                    hs = slice(h * DK, (h + 1) * DK)
                    sc = jnp.where(tri, _dot(qi[:, hs], ki[:, hs], "nt"), 0.0)
                    intra.append(_dot(sc, vb[:, hs]))
            for j in (range(cpb - 1, -1, -1) if reverse else range(cpb)):
                rs = slice(j * CHUNK, (j + 1) * CHUNK)
                a = em_rows[j] * em_rows[j]
                for h in range(HEADS):
                    hs = slice(h * DK, (h + 1) * DK)
                    st = st_ref[direction, h]
                    stash[j, h] = st.astype(stash.dtype)
                    if with_out:
                        (ob_ref if reverse else of_ref)[rs, hs] = intra[h][rs] + _dot(qe[rs, hs], st, "nt")
                    st_ref[direction, h] = st * a[:, hs] + _dot(vb[rs, hs], kd[rs, hs], "tn")

        @pl.when(i == nb - 1)
        def _():
            fin_ref[:, 0] = st_ref[...]

    up = lambda b, i: b * nb + i
    down = lambda b, i: b * nb + nb - 1 - i
    col = lambda rowf, c: pl.BlockSpec((rb, KW), lambda b, i: (rowf(b, i), c))
    in_specs = [_full((4, KW)), col(up, 0), col(up, 2)] + ([col(up, 3)] if with_out else [])
    in_specs += [col(down, 1), col(down, 2)] + ([col(down, 3)] if with_out else [])
    args = [gam, p, p] + ([p] if with_out else []) + [p, p] + ([p] if with_out else [])
    if has_s0:
        in_specs.append(pl.BlockSpec((2, 1, HEADS, DK, DK), lambda b, i: (0, b, 0, 0, 0)))
        args.append(s0)
    out_shape, out_specs = [], []
    if with_out:
        out_shape += [jax.ShapeDtypeStruct((rows, KW), F32)] * 2
        out_specs += [pl.BlockSpec((rb, KW), lambda b, i: (up(b, i), 0)),
                      pl.BlockSpec((rb, KW), lambda b, i: (down(b, i), 0))]
    out_shape += [jax.ShapeDtypeStruct((n_chunks, HEADS, DK, DK), MXU_DTYPE)] * 2
    out_specs += [pl.BlockSpec((cpb, HEADS, DK, DK), lambda b, i: (up(b, i), 0, 0, 0)),
                  pl.BlockSpec((cpb, HEADS, DK, DK), lambda b, i: (down(b, i), 0, 0, 0))]
    out_shape.append(jax.ShapeDtypeStruct((2, nb_ex, HEADS, DK, DK), F32))
    out_specs.append(pl.BlockSpec((2, 1, HEADS, DK, DK), lambda b, i: (0, b, 0, 0, 0)))
    res, handle = _host_call(body, name, (nb_ex, nb), in_specs, args, out_shape, out_specs,
                             [pltpu.VMEM((2, HEADS, DK, DK), F32)], sender=sender)
    return (*res, handle)


def _hgrn_bwd(p, gam, do, stash_f, stash_b, ds_end, rows_per_example, with_out, name, after=None, sender=None):
    rows = p.shape[0]
    nb_ex = rows // rows_per_example
    rb = min(TOKEN_TILE, rows_per_example)
    cpb = rb // CHUNK
    nb = rows_per_example // rb
    has_end = ds_end is not None

    def body(*refs):
        it = iter(refs)
        gam_ref = next(it)
        ins = []
        for _ in range(2):
            z_ref, v_ref = next(it), next(it)
            q_ref = next(it) if with_out else None
            do_ref = next(it) if with_out else None
            ins.append((z_ref, v_ref, q_ref, do_ref, next(it)))
        end_ref = next(it) if has_end else None
        outs = [next(it), next(it)]
        dlb_ref, ds0_ref = next(it), next(it)
        dst_ref = next(it)
        b_id, i = pl.program_id(0), pl.program_id(1)

        @pl.when(i == 0)
        def _():
            if has_end:
                dst_ref[...] = end_ref[:, 0]
            else:
                dst_ref[...] = jnp.zeros_like(dst_ref)

        @pl.when((i == 0) & (b_id == 0))
        def _():
            dlb_ref[...] = jnp.zeros_like(dlb_ref)

        for direction in range(2):
            z_ref, v_ref, q_ref, do_ref, stash = ins[direction]
            dgrp_ref = outs[direction]
            reverse = direction == 1
            tri = _tri(reverse, rb)
            tri_t = _tri(not reverse, rb)
            lb = _lower_bound(gam_ref, direction)
            heads = [slice(h * DK, (h + 1) * DK) for h in range(HEADS)]
            chunks = [slice(j * CHUNK, (j + 1) * CHUNK) for j in range(cpb)]
            grid_cat = lambda parts: jnp.concatenate([jnp.concatenate(row, axis=1) for row in parts], axis=0)
            cat = lambda parts: jnp.concatenate(parts, axis=1)
            z = z_ref[...].astype(F32)
            sg, g, k, b, em, em_rows, e2, mid = _gate_prep(z, lb, tri, reverse)
            e3 = e2 * em
            kd = k * e3
            kd_b = kd.astype(MXU_DTYPE)
            vb = v_ref[...].astype(MXU_DTYPE)
            if with_out:
                q = q_ref[...].astype(F32)
                dout = do_ref[...].astype(MXU_DTYPE)
                e1 = jnp.exp(b - mid)
                e4 = e1 * em
                qi, ki, qe = q * e1, k * e2, q * e4
                qi_b, ki_b, qe_b = qi.astype(MXU_DTYPE), ki.astype(MXU_DTYPE), qe.astype(MXU_DTYPE)
                dqi_p, dki_p, dv_p = [], [], []
                for hs in heads:
                    sc = jnp.where(tri, _dot(qi_b[:, hs], ki_b[:, hs], "nt"), 0.0)
                    dsc = jnp.where(tri, _dot(dout[:, hs], vb[:, hs], "nt"), 0.0)
                    dqi_p.append(_dot(dsc, ki_b[:, hs]))
                    dki_p.append(_dot(dsc, qi_b[:, hs], "tn"))
                    dv_p.append(_dot(sc, dout[:, hs], "tn"))
                dqi, dki, dv = cat(dqi_p), cat(dki_p), cat(dv_p)
                dqe = grid_cat([[_dot(dout[rs, hs], stash[j, h]) for h, hs in enumerate(heads)]
                                for j, rs in enumerate(chunks)])
                grow = [[_dot(dout[rs, hs], qe_b[rs, hs], "tn") for hs in heads] for rs in chunks]
            dkd_p = [[None] * HEADS for _ in range(cpb)]
            dvs_p = [[None] * HEADS for _ in range(cpb)]
            da_p = [[None] * HEADS for _ in range(cpb)]
            for j in (range(cpb) if reverse else range(cpb - 1, -1, -1)):
                rs = chunks[j]
                a = em_rows[j] * em_rows[j]
                for h, hs in enumerate(heads):
                    dst = dst_ref[direction, h]
                    dkd_p[j][h] = _dot(vb[rs, hs], dst)
                    dvs_p[j][h] = _dot(kd_b[rs, hs], dst, "nt")
                    da_p[j][h] = jnp.broadcast_to(
                        jnp.sum(dst * stash[j, h].astype(F32), axis=0, keepdims=True), (CHUNK, DK))
                    new_dst = dst * a[:, hs]
                    dst_ref[direction, h] = new_dst + grow[j][h] if with_out else new_dst
            dkd, dvs, da = grid_cat(dkd_p), grid_cat(dvs_p), grid_cat(da_p)
            t_kd = dkd * kd
            dk = dkd * e3
            db = -t_kd
            tot = t_kd
            if with_out:
                dgrp_ref[:, KW:2 * KW] = (dvs + dv).astype(dgrp_ref.dtype)
                dgrp_ref[:, 2 * KW:] = (dqi * e1 + dqe * e4).astype(dgrp_ref.dtype)
                dk = dk + dki * e2
                t_qi, t_ki, t_qe = dqi * qi, dki * ki, dqe * qe
                db = db + t_qi - t_ki + t_qe
                tot = tot + 0.5 * (t_ki - t_qi)
            else:
                dgrp_ref[:, KW:2 * KW] = dvs.astype(dgrp_ref.dtype)
            dbl = jnp.concatenate([jnp.broadcast_to(jnp.sum(tot[rs], axis=0, keepdims=True), (CHUNK, KW))
                                   for rs in chunks], axis=0) + da * (em * em)
            dg = _mask_dot(tri_t, db) + dbl
            df = dg * jnp.exp(-g) - dk
            dgrp_ref[:, 0:KW] = (df * (1.0 - lb) * sg * (1.0 - sg)).astype(dgrp_ref.dtype)
            dlb_ref[direction:direction + 1, :] += jnp.sum(df * (1.0 - sg), axis=0, keepdims=True)

        @pl.when(i == nb - 1)
        def _():
            ds0_ref[:, 0] = dst_ref[...]

    rows_of = (lambda b, i: b * nb + nb - 1 - i, lambda b, i: b * nb + i)
    in_specs, args = [_full((4, KW))], [gam]
    for direction in range(2):
        rf = rows_of[direction]
        col = lambda c, rf=rf: pl.BlockSpec((rb, KW), lambda b, i: (rf(b, i), c))
        in_specs += [col(direction), col(2)]
        args += [p, p]
        if with_out:
            in_specs += [col(3), col(0)]
            args += [p, do]
        in_specs.append(pl.BlockSpec((cpb, HEADS, DK, DK), lambda b, i, rf=rf: (rf(b, i), 0, 0, 0)))
        args.append((stash_f, stash_b)[direction])
    if has_end:
        in_specs.append(pl.BlockSpec((2, 1, HEADS, DK, DK), lambda b, i: (0, b, 0, 0, 0)))
        args.append(ds_end)
    out_shape, out_specs = [], []
    for direction in range(2):
        rf = rows_of[direction]
        width = (3 if with_out else 2) * KW
        out_shape.append(jax.ShapeDtypeStruct((rows, width), MXU_DTYPE))
        out_specs.append(pl.BlockSpec((rb, width), lambda b, i, rf=rf: (rf(b, i), 0)))
    out_shape += [jax.ShapeDtypeStruct((2, KW), F32), jax.ShapeDtypeStruct((2, nb_ex, HEADS, DK, DK), F32)]
    out_specs += [_full((2, KW)), pl.BlockSpec((2, 1, HEADS, DK, DK), lambda b, i: (0, b, 0, 0, 0))]
    res, handle = _host_call(body, name, (nb_ex, nb), in_specs, args, out_shape, out_specs,
                             [pltpu.VMEM((2, HEADS, DK, DK), F32)], after=after, sender=sender)
    return (*res, handle)


def _tail_forward(osum, og, u, v, ga, gb, gna, ln_g, ln_b, ws_ref, bs_ref, wpaT_ref, wpbT_ref):
    tm = osum.shape[0]
    gna4 = jnp.concatenate([gna] * HEADS, axis=1)
    r_parts = []
    for h in range(HEADS):
        oh = osum[:, h * DK:(h + 1) * DK]
        r_parts.append(jnp.broadcast_to(lax.rsqrt(jnp.mean(oh * oh, axis=-1, keepdims=True) + EPS), (tm, DK)))
    r = jnp.concatenate(r_parts, axis=1)
    on = osum * r
    sg_og = _sigmoid(og)
    silu_og = og * sg_og
    oan = on * gna4
    oa = oan * silu_og
    ug, tu = _gelu(u)
    vg, tv = _gelu(v)
    mu = jnp.mean(vg, axis=-1, keepdims=True)
    vc = vg - mu
    rstd = lax.rsqrt(jnp.mean(vc * vc, axis=-1, keepdims=True) + EPS)
    vhat = vc * rstd
    vln = vhat * ln_g + ln_b
    blocks = []
    for n in range(tm // SGU_BLOCK):
        rs = slice(n * SGU_BLOCK, (n + 1) * SGU_BLOCK)
        blocks.append(jnp.concatenate(
            [_dot(ws_ref[g], vln[rs, g * DK:(g + 1) * DK]) + bs_ref[g] for g in range(GROUPS)], axis=1))
    mixed = jnp.concatenate(blocks, axis=0) if len(blocks) > 1 else blocks[0]
    obm = ug * mixed
    pa = _dot(oa, wpaT_ref[...], "nt")
    pb = _dot(obm, wpbT_ref[...], "nt")
    sga, sgb = _sigmoid(ga), _sigmoid(gb)
    merged = sga * pa + sgb * pb
    return dict(r=r, on=on, sg_og=sg_og, silu_og=silu_og, oan=oan, oa=oa, ug=ug, tu=tu, tv=tv, rstd=rstd, vhat=vhat,
                vln=vln, mixed=mixed, obm=obm, pa=pa, pb=pb, sga=sga, sgb=sgb, merged=merged, gna4=gna4)


def _tail_in_specs(tm):
    tile = lambda c: pl.BlockSpec((tm, KW), lambda i: (i, c))
    return [tile(c) for c in range(4, 11)]


def _tail_weight_specs():
    return [_full((1, DK)), _full((1, KW)), _full((1, KW)), _full((GROUPS, SGU_BLOCK, SGU_BLOCK)),
            _full((GROUPS, SGU_BLOCK, 1)), _full((D, KW), single=True), _full((D, KW), single=True),
            _full((D, D), single=True)]


def _read_tail_inputs(of_ref, ob_ref, pcols):
    osum = of_ref[...] + ob_ref[...]
    og, u, v = (pcols[j][...].astype(F32) for j in range(3))
    ga = jnp.concatenate([pcols[3][...], pcols[4][...]], axis=1).astype(F32)
    gb = jnp.concatenate([pcols[5][...], pcols[6][...]], axis=1).astype(F32)
    return osum, og, u, v, ga, gb


def _tail_fwd(p, o_up, o_down, xt, modv, gna, ln_g, ln_b, w_s, b_s, w_paT, w_pbT, w_o, rows_per_example):
    rows = xt.shape[0]
    tm = min(TOKEN_TILE, rows_per_example)
    per_b = rows_per_example // tm

    def body(of_ref, ob_ref, *rest):
        pcols = rest[:7]
        (x_ref, mod_ref, gna_ref, lng_ref, lnb_ref, ws_ref, bs_ref, wpaT_ref, wpbT_ref, wo_ref,
         x1_ref, mix_ref, merged_ref, oa_ref, obm_ref) = rest[7:]
        t = _tail_forward(*_read_tail_inputs(of_ref, ob_ref, pcols), gna_ref[...], lng_ref[...], lnb_ref[...],
                          ws_ref, bs_ref, wpaT_ref, wpbT_ref)
        mix = _dot(t["merged"], wo_ref[...])
        x1_ref[...] = x_ref[...] + mod_ref[0, 2:3, :] * mix
        mix_ref[...] = mix.astype(mix_ref.dtype)
        merged_ref[...] = t["merged"].astype(merged_ref.dtype)
        oa_ref[...] = t["oa"].astype(oa_ref.dtype)
        obm_ref[...] = t["obm"].astype(obm_ref.dtype)

    row = lambda w: pl.BlockSpec((tm, w), lambda i: (i, 0))
    in_specs = [row(KW), row(KW)] + _tail_in_specs(tm) + [row(D), pl.BlockSpec((1, N_MOD, D), lambda i: (i // per_b, 0, 0))]
    in_specs += _tail_weight_specs()
    return pl.pallas_call(
        body, name="tail_fwd", grid=(rows // tm,),
        out_shape=(jax.ShapeDtypeStruct((rows, D), F32), jax.ShapeDtypeStruct((rows, D), MXU_DTYPE),
                   jax.ShapeDtypeStruct((rows, D), MXU_DTYPE), jax.ShapeDtypeStruct((rows, KW), MXU_DTYPE),
                   jax.ShapeDtypeStruct((rows, KW), MXU_DTYPE)),
        in_specs=in_specs, out_specs=(row(D), row(D), row(D), row(KW), row(KW)),
        compiler_params=_params(("arbitrary",)),
    )(o_up, o_down, *([p] * 7), xt, modv, gna, ln_g, ln_b, w_s, b_s, w_paT, w_pbT, w_o)


def _tail_bwd(p, o_up, o_down, dx1, mix, modv, gna, ln_g, ln_b, w_s, b_s, w_paT, w_pbT, w_o, rows_per_example,
              after=None, sender=None):
    rows = dx1.shape[0]
    nb_ex = rows // rows_per_example
    tm = min(TOKEN_TILE, rows_per_example)
    per_b = rows_per_example // tm

    def body(of_ref, ob_ref, *rest):
        pcols = rest[:7]
        (dx1_ref, mix_ref, mod_ref, gna_ref, lng_ref, lnb_ref, ws_ref, bs_ref, wpaT_ref, wpbT_ref, wo_ref,
         dpt_ref, do_ref, dmix_ref, dpa_ref, dpb_ref, dmod_ref, small_ref, dws_ref, dbs_ref) = rest[7:]
        i = pl.program_id(0)

        @pl.when(i == 0)
        def _():
            small_ref[...] = jnp.zeros_like(small_ref)
            dws_ref[...] = jnp.zeros_like(dws_ref)
            dbs_ref[...] = jnp.zeros_like(dbs_ref)

        @pl.when(i % per_b == 0)
        def _():
            dmod_ref[...] = jnp.zeros_like(dmod_ref)

        osum, og, u, v, ga, gb = _read_tail_inputs(of_ref, ob_ref, pcols)
        ln_g = lng_ref[...]
        t = _tail_forward(osum, og, u, v, ga, gb, gna_ref[...], ln_g, lnb_ref[...], ws_ref, bs_ref, wpaT_ref, wpbT_ref)
        dx1v = dx1_ref[...]
        dmod_ref[0, 2:3, :] += jnp.sum(dx1v * mix_ref[...].astype(F32), axis=0, keepdims=True)
        dmix = dx1v * mod_ref[0, 2:3, :]
        dmix_ref[...] = dmix.astype(dmix_ref.dtype)
        dmerged = _dot(dmix, wo_ref[...], "nt")
        sga, sgb = t["sga"], t["sgb"]
        dpa = dmerged * sga
        dpb = dmerged * sgb
        dpa_ref[...] = dpa.astype(dpa_ref.dtype)
        dpb_ref[...] = dpb.astype(dpb_ref.dtype)
        dga = dmerged * t["pa"] * sga * (1.0 - sga)
        dgb = dmerged * t["pb"] * sgb * (1.0 - sgb)
        doa = _dot(dpa, wpaT_ref[...])
        dobm = _dot(dpb, wpbT_ref[...])
        dug = dobm * t["mixed"]
        dmixed = dobm * t["ug"]
        du = dug * _gelu_grad(u, t["tu"])
        dvln_blocks = []
        for n in range(tm // SGU_BLOCK):
            rs = slice(n * SGU_BLOCK, (n + 1) * SGU_BLOCK)
            parts = []
            for g in range(GROUPS):
                gs = slice(g * DK, (g + 1) * DK)
                dm = dmixed[rs, gs]
                parts.append(_dot(ws_ref[g], dm, "tn"))
                dws_ref[g] += _dot(dm, t["vln"][rs, gs], "nt")
                dbs_ref[g] += jnp.sum(dm, axis=1, keepdims=True)
            dvln_blocks.append(jnp.concatenate(parts, axis=1))
        dvln = jnp.concatenate(dvln_blocks, axis=0) if len(dvln_blocks) > 1 else dvln_blocks[0]
        vhat = t["vhat"]
        small_ref[1:2, 0:KW] += jnp.sum(dvln * vhat, axis=0, keepdims=True)
        small_ref[2:3, 0:KW] += jnp.sum(dvln, axis=0, keepdims=True)
        dvhat = dvln * ln_g
        dvg = t["rstd"] * (dvhat - jnp.mean(dvhat, axis=-1, keepdims=True)
                           - vhat * jnp.mean(dvhat * vhat, axis=-1, keepdims=True))
        dv = dvg * _gelu_grad(v, t["tv"])
        sg_og = t["sg_og"]
        doan = doa * t["silu_og"]
        dog = doa * t["oan"] * (sg_og * (1.0 + og * (1.0 - sg_og)))
        prod = doan * t["on"]
        dgna = jnp.zeros((1, DK), F32)
        for h in range(HEADS):
            dgna = dgna + jnp.sum(prod[:, h * DK:(h + 1) * DK], axis=0, keepdims=True)
        small_ref[0:1, 0:DK] += dgna
        don = doan * t["gna4"]
        dot_parts = []
        for h in range(HEADS):
            hs = slice(h * DK, (h + 1) * DK)
            m = jnp.mean(don[:, hs] * t["on"][:, hs], axis=-1, keepdims=True)
            dot_parts.append(t["r"][:, hs] * (don[:, hs] - t["on"][:, hs] * m))
        do_ref[...] = jnp.concatenate(dot_parts, axis=1).astype(do_ref.dtype)
        for j, val in enumerate((dog, du, dv)):
            dpt_ref[:, j * KW:(j + 1) * KW] = val.astype(dpt_ref.dtype)
        dpt_ref[:, 3 * KW:3 * KW + D] = dga.astype(dpt_ref.dtype)
        dpt_ref[:, 3 * KW + D:] = dgb.astype(dpt_ref.dtype)

    row = lambda w: pl.BlockSpec((tm, w), lambda i: (i, 0))
    in_specs = [row(KW), row(KW)] + _tail_in_specs(tm) + [row(D), row(D), pl.BlockSpec((1, N_MOD, D), lambda i: (i // per_b, 0, 0))]
    in_specs += _tail_weight_specs()
    args = [o_up, o_down, *([p] * 7), dx1, mix, modv, gna, ln_g, ln_b, w_s, b_s, w_paT, w_pbT, w_o]
    cd = MXU_DTYPE
    res, handle = _host_call(
        body, "tail_bwd", (rows // tm,), in_specs, args,
        [jax.ShapeDtypeStruct((rows, TAIL_COLS), cd), jax.ShapeDtypeStruct((rows, KW), cd),
         jax.ShapeDtypeStruct((rows, D), cd), jax.ShapeDtypeStruct((rows, D), cd),
         jax.ShapeDtypeStruct((rows, D), cd), jax.ShapeDtypeStruct((nb_ex, 8, D), F32),
         jax.ShapeDtypeStruct((8, D), F32), jax.ShapeDtypeStruct((GROUPS, SGU_BLOCK, SGU_BLOCK), F32),
         jax.ShapeDtypeStruct((GROUPS, SGU_BLOCK, 1), F32)],
        [row(TAIL_COLS), row(KW), row(D), row(D), row(D),
         pl.BlockSpec((1, 8, D), lambda i: (i // per_b, 0, 0)), _full((8, D)),
         _full((GROUPS, SGU_BLOCK, SGU_BLOCK)), _full((GROUPS, SGU_BLOCK, 1))], [],
        after=after, sender=sender)
    return (*res, handle)


def _ffn(x1, target, modv, g_ffn, g_final, w_upT, w_down, rows_per_example):
    rows = x1.shape[0]
    nb_ex = rows // rows_per_example
    tm = min(TOKEN_TILE, rows_per_example)
    per_b = rows_per_example // tm
    n_ff = D_FF // FF_CHUNK

    def body(x1_ref, tgt_ref, mod_ref, gffn_ref, gfin_ref, wup_ref, wdn_ref,
             dx1_ref, h2_ref, dffn_ref, act_ref, dup_ref, dmod_ref, small_ref, up_scr, dact_scr):
        i = pl.program_id(0)

        @pl.when(i == 0)
        def _():
            small_ref[...] = jnp.zeros_like(small_ref)

        @pl.when(i % per_b == 0)
        def _():
            dmod_ref[...] = jnp.zeros_like(dmod_ref)

        x1v = x1_ref[...]
        g2 = gffn_ref[...]
        m3, m4, m5 = mod_ref[0, 3:4, :], mod_ref[0, 4:5, :], mod_ref[0, 5:6, :]
        r2 = lax.rsqrt(jnp.mean(x1v * x1v, axis=-1, keepdims=True) + EPS)
        xn2 = x1v * r2
        h2 = (xn2 * g2) * (1.0 + m4) + m3
        h2b = h2.astype(MXU_DTYPE)
        h2_ref[...] = h2b
        up_scr[...] = _dot(h2b, wup_ref[...], "nt")
        for j in range(n_ff):
            cs = slice(j * FF_CHUNK, (j + 1) * FF_CHUNK)
            a, bgate = up_scr[:, cs], up_scr[:, D_FF + j * FF_CHUNK:D_FF + (j + 1) * FF_CHUNK]
            act_ref[:, cs] = (a * _sigmoid(a) * bgate).astype(MXU_DTYPE)
        ffn = _dot(act_ref[...], wdn_ref[...])
        x2 = x1v + m5 * ffn
        r3 = lax.rsqrt(jnp.mean(x2 * x2, axis=-1, keepdims=True) + EPS)
        xn3 = x2 * r3
        gf = gfin_ref[...]
        err = xn3 * gf - tgt_ref[...]
        loss = 0.5 * jnp.sum(jnp.mean(err * err, axis=-1, keepdims=True), axis=0, keepdims=True)
        small_ref[2:3, :] += jnp.broadcast_to(loss, (1, D))
        dy = err * (1.0 / D)
        small_ref[1:2, :] += jnp.sum(dy * xn3, axis=0, keepdims=True)
        dxn3 = dy * gf
        dx2 = r3 * (dxn3 - xn3 * jnp.mean(dxn3 * xn3, axis=-1, keepdims=True))
        dmod_ref[0, 5:6, :] += jnp.sum(dx2 * ffn, axis=0, keepdims=True)
        dffn = (dx2 * m5).astype(MXU_DTYPE)
        dffn_ref[...] = dffn
        dact_scr[...] = _dot(dffn, wdn_ref[...], "nt")
        for j in range(n_ff):
            cs = slice(j * FF_CHUNK, (j + 1) * FF_CHUNK)
            dact = dact_scr[:, cs]
            a, bgate = up_scr[:, cs], up_scr[:, D_FF + j * FF_CHUNK:D_FF + (j + 1) * FF_CHUNK]
            s = _sigmoid(a)
            dup_ref[:, cs] = (dact * bgate * (s * (1.0 + a * (1.0 - s)))).astype(MXU_DTYPE)
            dup_ref[:, D_FF + j * FF_CHUNK:D_FF + (j + 1) * FF_CHUNK] = (dact * a * s).astype(MXU_DTYPE)
        dh2 = _dot(dup_ref[...], wup_ref[...])
        dmod_ref[0, 3:4, :] += jnp.sum(dh2, axis=0, keepdims=True)
        dmod_ref[0, 4:5, :] += jnp.sum(dh2 * xn2 * g2, axis=0, keepdims=True)
        small_ref[0:1, :] += jnp.sum(dh2 * (1.0 + m4) * xn2, axis=0, keepdims=True)
        dxn2 = dh2 * g2 * (1.0 + m4)
        dx1_ref[...] = dx2 + r2 * (dxn2 - xn2 * jnp.mean(dxn2 * xn2, axis=-1, keepdims=True))

    row = lambda w: pl.BlockSpec((tm, w), lambda i: (i, 0))
    cd = MXU_DTYPE
    return pl.pallas_call(
        body, name="ffn_fwd_bwd", grid=(rows // tm,),
        out_shape=(jax.ShapeDtypeStruct((rows, D), F32), jax.ShapeDtypeStruct((rows, D), cd),
                   jax.ShapeDtypeStruct((rows, D), cd), jax.ShapeDtypeStruct((rows, D_FF), cd),
                   jax.ShapeDtypeStruct((rows, 2 * D_FF), cd), jax.ShapeDtypeStruct((nb_ex, 8, D), F32),
                   jax.ShapeDtypeStruct((8, D), F32)),
        in_specs=[row(D), row(D), pl.BlockSpec((1, N_MOD, D), lambda i: (i // per_b, 0, 0)), _full((1, D)), _full((1, D)),
                  _full((2 * D_FF, D), single=True), _full((D_FF, D), single=True)],
        out_specs=(row(D), row(D), row(D), row(D_FF), row(2 * D_FF),
                   pl.BlockSpec((1, 8, D), lambda i: (i // per_b, 0, 0)), _full((8, D))),
        scratch_shapes=[pltpu.VMEM((tm, 2 * D_FF), F32), pltpu.VMEM((tm, D_FF), F32)],
        compiler_params=_params(("arbitrary",)),
    )(x1, target, modv, g_ffn, g_final, w_upT, w_down)


def _scan_columns(up, down, n_groups):
    cols = [up[:, 0:KW].astype(F32), down[:, 0:KW].astype(F32)]
    for j in range(1, n_groups):
        cols.append(up[:, j * KW:(j + 1) * KW].astype(F32) + down[:, j * KW:(j + 1) * KW].astype(F32))
    return cols


def _inproj_bwd(d_up, d_down, dpt, xt, dx1, modv, g, w_inT, rows_per_example, name, sender=None):
    rows = xt.shape[0]
    latent = dx1 is not None
    n_cols = IN_COLS if latent else CTX_COLS
    n_groups = d_up.shape[1] // KW
    tm = min(TOKEN_TILE, rows_per_example)
    per_b = rows_per_example // tm
    n_mod_blocks = rows // rows_per_example if latent else 1

    def body(*refs):
        it = iter(refs)
        up_ref, down_ref = next(it), next(it)
        dpt_ref = next(it) if latent else None
        x_ref = next(it)
        dx1_ref = next(it) if latent else None
        mod_ref, g_ref, w_ref = next(it), next(it), next(it)
        gx_ref = next(it) if latent else None
        dp_out = None if latent else next(it)
        dmod_ref, small_ref = next(it), next(it)
        dp_ref = next(it) if latent else dp_out
        i = pl.program_id(0)

        @pl.when(i == 0)
        def _():
            small_ref[...] = jnp.zeros_like(small_ref)

        @pl.when((i % per_b == 0) if latent else (i == 0))
        def _():
            dmod_ref[...] = jnp.zeros_like(dmod_ref)

        for j, val in enumerate(_scan_columns(up_ref[...], down_ref[...], n_groups)):
            dp_ref[:, j * KW:(j + 1) * KW] = val.astype(MXU_DTYPE)
        if latent:
            dh = _dot(dp_ref[...], w_ref[0:4 * KW, :]) + _dot(dpt_ref[...], w_ref[4 * KW:, :])
        else:
            dh = _dot(dp_ref[...], w_ref[...])
        x = x_ref[...]
        gv = g_ref[...]
        m1 = mod_ref[0, 1:2, :]
        r = lax.rsqrt(jnp.mean(x * x, axis=-1, keepdims=True) + EPS)
        xn = x * r
        dmod_ref[0, 0:1, :] += jnp.sum(dh, axis=0, keepdims=True)
        dmod_ref[0, 1:2, :] += jnp.sum(dh * xn * gv, axis=0, keepdims=True)
        small_ref[0:1, :] += jnp.sum(dh * (1.0 + m1) * xn, axis=0, keepdims=True)
        if latent:
            dxn = dh * gv * (1.0 + m1)
            gx_ref[...] = dx1_ref[...] + r * (dxn - xn * jnp.mean(dxn * xn, axis=-1, keepdims=True))

    row = lambda w: pl.BlockSpec((tm, w), lambda i: (i, 0))
    mod_idx = (lambda i: (i // per_b, 0, 0)) if latent else (lambda i: (0, 0, 0))
    in_specs = [row(n_groups * KW)] * 2 + ([row(TAIL_COLS)] if latent else []) + [row(D)] + ([row(D)] if latent else [])
    in_specs += [pl.BlockSpec((1, N_MOD, D), mod_idx), _full((1, D)),
                 pl.BlockSpec((n_cols, D), lambda i: (0, 0), pipeline_mode=pl.Buffered(1))]
    args = [d_up, d_down] + ([dpt] if latent else []) + [xt] + ([dx1] if latent else []) + [modv, g, w_inT]
    first = jax.ShapeDtypeStruct((rows, D), F32) if latent else jax.ShapeDtypeStruct((rows, n_cols), MXU_DTYPE)
    out_shape = [first, jax.ShapeDtypeStruct((n_mod_blocks, 8, D), F32), jax.ShapeDtypeStruct((8, D), F32)]
    out_specs = [row(D) if latent else row(n_cols), pl.BlockSpec((1, 8, D), mod_idx), _full((8, D))]
    scratch = [pltpu.VMEM((tm, 4 * KW), MXU_DTYPE)] if latent else []
    res, handle = _host_call(body, name, (rows // tm,), in_specs, args, out_shape, out_specs, scratch, sender=sender)
    return (*res, handle)


def _grad_matmul(a, b, name, init=None, tn=512, sender=None):
    rows, n = a.shape
    k = b.shape[1]
    tn = min(tn, n)
    has_init = init is not None
    init_blocks = init.shape[0] // tn if has_init else 0

    def body(*refs):
        if has_init:
            a_ref, b_ref, init_ref, o_ref = refs
        else:
            a_ref, b_ref, o_ref = refs
        g = _dot(a_ref[...], b_ref[...], "tn")
        if has_init:
            g = g + jnp.where(pl.program_id(0) < init_blocks, init_ref[...].astype(F32), 0.0)
        o_ref[...] = g.astype(o_ref.dtype)

    in_specs = [pl.BlockSpec((rows, tn), lambda i: (0, i)), _full((rows, k), single=True)]
    args = [a, b]
    if has_init:
        in_specs.append(pl.BlockSpec((tn, k), lambda i: (jnp.minimum(i, init_blocks - 1), 0)))
        args.append(init)
    (out,), handle = _host_call(
        body, name, (n // tn,), in_specs, args, [jax.ShapeDtypeStruct((n, k), PAYLOAD_DTYPE)],
        [pl.BlockSpec((tn, k), lambda i: (i, 0))], [], sender=sender)
    return out, handle


def _grad_in(d_up, d_down, dpt, h, init, sender=None):
    rows = h.shape[0]
    tn = 256
    per_group = KW // tn
    n_scan = 4 * per_group
    init_blocks = init.shape[0] // tn

    def body(up_ref, down_ref, dpt_ref, h_ref, init_ref, o_ref):
        i = pl.program_id(0)
        both = (up_ref[...].astype(F32) + down_ref[...].astype(F32)).astype(MXU_DTYPE)
        a = jnp.where(i < per_group, up_ref[...],
                      jnp.where(i < 2 * per_group, down_ref[...], jnp.where(i < n_scan, both, dpt_ref[...])))
        g = _dot(a, h_ref[...], "tn") + jnp.where(i < init_blocks, init_ref[...].astype(F32), 0.0)
        o_ref[...] = g.astype(o_ref.dtype)

    last = 3 * per_group - 1
    col = lambda f: pl.BlockSpec((rows, tn), lambda i: (0, f(i)))
    in_specs = [col(lambda i: jnp.clip(jnp.where(i < per_group, i, i - per_group), 0, last)),
                col(lambda i: jnp.clip(i - per_group, 0, last)),
                col(lambda i: jnp.clip(i - n_scan, 0, TAIL_COLS // tn - 1)),
                _full((rows, D), single=True),
                pl.BlockSpec((tn, D), lambda i: (jnp.minimum(i, init_blocks - 1), 0))]
    (out,), handle = _host_call(
        body, "gw_in", (IN_COLS // tn,), in_specs, [d_up, d_down, dpt, h, init],
        [jax.ShapeDtypeStruct((IN_COLS, D), PAYLOAD_DTYPE)], [pl.BlockSpec((tn, D), lambda i: (i, 0))], [],
        sender=sender)
    return out, handle


def _row_tile(rows, limit=256):
    if rows <= limit:
        return rows
    for t in range(limit, 7, -8):
        if rows % t == 0:
            return t
    return rows


def _sum8(stack, name):
    _, rows, cols = stack.shape
    tr = _row_tile(rows)

    def body(s_ref, o_ref):
        acc = s_ref[0].astype(F32)
        for j in range(1, N_DEV):
            acc = acc + s_ref[j].astype(F32)
        o_ref[...] = acc

    return pl.pallas_call(
        body, name=name, grid=(rows // tr,), out_shape=jax.ShapeDtypeStruct((rows, cols), F32),
        in_specs=[pl.BlockSpec((N_DEV, tr, cols), lambda i: (0, i, 0))],
        out_specs=pl.BlockSpec((tr, cols), lambda i: (i, 0)),
        compiler_params=_params(("arbitrary",)),
    )(stack)


def _small_reduce(early, late, gam, nb_ex):
    def body(s_ref, l_ref, gam_ref, o_ref, bm_ref):
        acc = s_ref[0] + l_ref[0]
        for j in range(1, N_DEV):
            acc = acc + (s_ref[j] + l_ref[j])
        o_ref[...] = acc
        bm = acc[8:8 + N_MOD, :]
        for e in range(nb_ex):
            bm = bm + acc[16 + e * N_MOD:16 + (e + 1) * N_MOD, :]
        lb = jnp.concatenate([_lower_bound(gam_ref, 0), _lower_bound(gam_ref, 1)], axis=1)
        dgam = acc[7:8, :] * lb * (1.0 - lb)
        bm_ref[...] = jnp.concatenate([bm, dgam, -dgam], axis=0)

    return pl.pallas_call(
        body, name="small_reduce", grid=(1,),
        out_shape=(jax.ShapeDtypeStruct((SMALL_ROWS, D), F32), jax.ShapeDtypeStruct((8, D), F32)),
        in_specs=[_full((N_DEV, SMALL_ROWS, D)), _full((N_DEV, SMALL_ROWS, D)), _full((4, KW))],
        out_specs=(_full((SMALL_ROWS, D)), _full((8, D))),
        compiler_params=_params(("arbitrary",)),
    )(early, late, gam)


def _adamw_update(w, gv, m, v):
    nm = ADAM_B1 * m + (1.0 - ADAM_B1) * gv
    nv = ADAM_B2 * v + (1.0 - ADAM_B2) * (gv * gv)
    m_hat = nm / (1.0 - ADAM_B1 ** ADAM_STEP)
    v_hat = nv / (1.0 - ADAM_B2 ** ADAM_STEP)
    return -ADAM_LR * (m_hat / (jnp.sqrt(v_hat) + ADAM_EPS) + ADAM_WD * w), nm, nv


def _adamw_sum8(stack, w, m, v, name):
    _, rows, cols = stack.shape
    tr = _row_tile(rows)

    def body(s_ref, w_ref, m_ref, v_ref, g_ref, d_ref, nm_ref, nv_ref):
        gv = s_ref[0].astype(F32)
        for j in range(1, N_DEV):
            gv = gv + s_ref[j].astype(F32)
        g_ref[...] = gv
        d_ref[...], nm_ref[...], nv_ref[...] = _adamw_update(w_ref[...], gv, m_ref[...], v_ref[...])

    blk = pl.BlockSpec((tr, cols), lambda i: (i, 0))
    sd = jax.ShapeDtypeStruct((rows, cols), F32)
    return pl.pallas_call(
        body, name=name, grid=(rows // tr,), out_shape=(sd, sd, sd, sd),
        in_specs=[pl.BlockSpec((N_DEV, tr, cols), lambda i: (0, i, 0)), blk, blk, blk], out_specs=(blk, blk, blk, blk),
        compiler_params=_params(("arbitrary",)),
    )(stack, w, m, v)


def _adamw(w, g, m, v, name):
    shape = w.shape
    cols = shape[-1]
    rows = 1
    for s in shape[:-1]:
        rows *= s
    tr = _row_tile(rows)

    def body(w_ref, g_ref, m_ref, v_ref, d_ref, nm_ref, nv_ref):
        gv = g_ref[...]
        nm = ADAM_B1 * m_ref[...] + (1.0 - ADAM_B1) * gv
        nv = ADAM_B2 * v_ref[...] + (1.0 - ADAM_B2) * (gv * gv)
        m_hat = nm / (1.0 - ADAM_B1 ** ADAM_STEP)
        v_hat = nv / (1.0 - ADAM_B2 ** ADAM_STEP)
        d_ref[...] = -ADAM_LR * (m_hat / (jnp.sqrt(v_hat) + ADAM_EPS) + ADAM_WD * w_ref[...])
        nm_ref[...] = nm
        nv_ref[...] = nv

    blk = pl.BlockSpec((tr, cols), lambda i: (i, 0))
    sd = jax.ShapeDtypeStruct((rows, cols), F32)
    d, nm, nv = pl.pallas_call(
        body, name=name, grid=(rows // tr,), out_shape=(sd, sd, sd), in_specs=[blk] * 4, out_specs=(blk, blk, blk),
        compiler_params=_params(("arbitrary",)),
    )(w.reshape(rows, cols), g.reshape(rows, cols), m.reshape(rows, cols), v.reshape(rows, cols))
    return d.reshape(shape), nm.reshape(shape), nv.reshape(shape)


def _owner_blocks(a):
    return a.reshape(N_DEV, a.shape[0] // N_DEV, a.shape[1])


class _LocalWeights:
    def __init__(self, w_upT, w_down, w_o, w_paT, w_pbT):
        self.weights = (w_upT, w_down, w_o, w_paT, w_pbT)
        self.items = {}

    def sender(self, stage, items=None):
        self.items[stage] = items
        return None

    def sent(self, stage, handle):
        pass

    def mixer_weights(self, after):
        return self.weights[1:]

    def ffn_weights(self, after):
        return self.weights[0]


def _local_step(x, ctx, target, modv, mcv, gam, g_mix, g_ffn, gna, ln_g, ln_b, w_s, b_s, g_final, w_inT, comm):
    nb_ex, seq, _ = x.shape
    ctx_len = ctx.shape[1]
    xt = x.reshape(nb_ex * seq, D)
    ct = ctx.reshape(nb_ex * ctx_len, D)
    tgt = target.reshape(nb_ex * seq, D)
    bs3 = b_s.reshape(GROUPS, SGU_BLOCK, 1)

    pc, hc, _ = _inproj(ct, mcv, g_mix, w_inT, CTX_COLS, ctx_len, "inproj_ctx")
    p, h, handle = _inproj(xt, modv, g_mix, w_inT, IN_COLS, seq, "inproj_lat", sender=comm.sender("inproj"))
    comm.sent("inproj", handle)
    cst_f, cst_b, s_ctx, _ = _hgrn_fwd(pc, gam, None, ctx_len, False, "hgrn_fwd_ctx")
    o_up, o_down, st_f, st_b, _, handle = _hgrn_fwd(p, gam, s_ctx, seq, True, "hgrn_fwd_lat",
                                                    sender=comm.sender("scan"))
    comm.sent("scan", handle)
    w_down, w_o, w_paT, w_pbT = comm.mixer_weights(o_up)
    x1, mix, merged, oa, obm = _tail_fwd(p, o_up, o_down, xt, modv, gna, ln_g, ln_b, w_s, bs3, w_paT, w_pbT, w_o, seq)
    w_upT = comm.ffn_weights(x1)
    dx1, h2, dffn, act, dup, dmod_ffn, small_ffn = _ffn(x1, tgt, modv, g_ffn, g_final, w_upT, w_down, seq)
    gw_upT, _ = _grad_matmul(dup, h2, "gw_up")
    gw_down, _ = _grad_matmul(act, dffn, "gw_down", tn=256)
    scatter = lambda *grads: [(_owner_blocks(g), "scatter") for g in grads]
    dpt, do, dmix, dpa, dpb, dmod_tail, small_tail, dws, dbs, handle = _tail_bwd(
        p, o_up, o_down, dx1, mix, modv, gna, ln_g, ln_b, w_s, bs3, w_paT, w_pbT, w_o, seq,
        sender=comm.sender("tail_bwd", scatter(gw_upT)))
    comm.sent("tail_bwd", handle)
    gw_o, _ = _grad_matmul(merged, dmix, "gw_o")
    gw_paT, _ = _grad_matmul(dpa, oa, "gw_pa")
    gw_pbT, _ = _grad_matmul(dpb, obm, "gw_pb")
    def at_row(row, a):
        return jnp.pad(a, ((row, SMALL_ROWS - row - a.shape[0]), (0, D - a.shape[1])))

    small_early = (at_row(1, small_ffn[0:2])
                   + at_row(3, small_tail[0:3])
                   + at_row(6, dbs.reshape(1, GROUPS * SGU_BLOCK))
                   + at_row(14, small_ffn[2:3]))
    dws_rows = dws.reshape(GROUPS * SGU_BLOCK, SGU_BLOCK)
    d_up, d_down, dlb, ds0, handle = _hgrn_bwd(
        p, gam, do, st_f, st_b, None, seq, True, "hgrn_bwd_lat",
        sender=comm.sender("scan_bwd", scatter(gw_down, gw_o, gw_paT, gw_pbT)
                           + [(small_early, "gather"), (dws_rows, "gather")]))
    comm.sent("scan_bwd", handle)
    c_up, c_down, dlb_c, _, _ = _hgrn_bwd(pc, gam, None, cst_f, cst_b, ds0, ctx_len, False, "hgrn_bwd_ctx")
    dpc, dmc, small_c, _ = _inproj_bwd(c_up, c_down, None, ct, None, mcv, g_mix, w_inT, ctx_len, "inproj_bwd_ctx")
    gw_inT, _ = _grad_in(d_up, d_down, dpt, h, _grad_matmul(dpc, hc, "gw_in_ctx")[0])
    grad_x, dmod_in, small_in, handle = _inproj_bwd(d_up, d_down, dpt, xt, dx1, modv, g_mix, w_inT, seq,
                                                   "inproj_bwd_lat", sender=comm.sender("inproj_bwd", scatter(gw_inT)))
    comm.sent("inproj_bwd", handle)
    dmod = dmod_in + dmod_tail + dmod_ffn
    small_late = (at_row(0, small_in[0:1] + small_c[0:1])
                  + at_row(7, (dlb + dlb_c).reshape(1, 2 * KW))
                  + at_row(8, dmc[0, 0:N_MOD])
                  + at_row(16, dmod[:, 0:N_MOD].reshape(nb_ex * N_MOD, D)))
    comm.sender("last", [(small_late, "gather")])
    return grad_x.reshape(x.shape)


def kernel(x, c, ctx, c_ctx, w_mod, b_mod, g_mix, g_ffn, w_in, lb_gamma, g_norm_a, ln_v_g, ln_v_b, w_s, b_s, w_pa, w_pb, w_o, w_up, w_down, g_final, loss_target, m_c_ctx, m_w_mod, m_b_mod, m_g_mix, m_g_ffn, m_w_in, m_lb_gamma, m_g_norm_a, m_ln_v_g, m_ln_v_b, m_w_s, m_b_s, m_w_pa, m_w_pb, m_w_o, m_w_up, m_w_down, m_g_final, v_c_ctx, v_w_mod, v_b_mod, v_g_mix, v_g_ffn, v_w_in, v_lb_gamma, v_g_norm_a, v_ln_v_g, v_ln_v_b, v_w_s, v_b_s, v_w_pa, v_w_pb, v_w_o, v_w_up, v_w_down, v_g_final):
    nb_ex = x.shape[0]
    me = 4 * lax.axis_index("x") + 2 * lax.axis_index("y") + lax.axis_index("c")
    cd = MXU_DTYPE
    mod_cols = w_mod.shape[2]
    lb_cols = lb_gamma.shape[2]

    w_inT_l = w_in[0].T.astype(cd)
    w_upT_l = w_up[0].T.astype(cd)
    w_paT_l = w_pa[0].T.astype(cd)
    w_pbT_l = w_pb[0].T.astype(cd)
    cl = jnp.concatenate([c, jnp.pad(lb_gamma.reshape(1, 4 * lb_cols), ((0, 0), (0, D - 4 * lb_cols))),
                          jnp.zeros((8 - nb_ex - 1, D), F32)], axis=0)
    g_in, g_cl = _gather_two_level([w_inT_l, cl], "gather_w_in")
    w_inT = g_in.reshape(IN_COLS, D)
    c_all = g_cl[:, 0:nb_ex].reshape(N_DEV * nb_ex, D)
    gam = jnp.transpose(g_cl[:, nb_ex, 0:4 * lb_cols].reshape(N_DEV, 4, lb_cols), (1, 0, 2)).reshape(4, KW)

    n_c = N_DEV * nb_ex
    cvec = jnp.concatenate([c_all, c_ctx.reshape(1, D), jnp.zeros((7, D), F32)], axis=0)
    b_mod_l = lax.dynamic_slice(b_mod, (0, me * mod_cols), (1, mod_cols))
    mod_l, svec = _mod_fwd(cvec, w_mod[0], b_mod_l)
    (g_mod,) = _exchange([(mod_l, "gather")], "gather_mod")
    mod_all = jnp.transpose(g_mod, (1, 0, 2)).reshape(n_c + 8, N_MOD * D)
    modv = lax.dynamic_slice(mod_all, (me * nb_ex, 0), (nb_ex, N_MOD * D)).reshape(nb_ex, N_MOD, D)
    mcv = mod_all[n_c].reshape(1, N_MOD, D)

    handles, leftover = {}, {}

    class Comm:
        def sender(self, stage, items=None):
            if stage == "inproj":
                return _Sender([(w_down[0].astype(cd), "gather"), (w_o[0].astype(cd), "gather"), (w_paT_l, "gather"),
                                (w_pbT_l, "gather")])
            if stage == "scan":
                return _Sender([(w_upT_l, "gather")])
            if stage == "last":
                leftover["items"] = items
                return None
            return _Sender(items)

        def sent(self, stage, handle):
            handles[stage] = handle

        def mixer_weights(self, after):
            g_down, g_o, g_pa, g_pb = _exchange_wait(handles["inproj"], after)
            return g_down.reshape(D_FF, D), g_o.reshape(D, D), g_pa.reshape(D, KW), g_pb.reshape(D, KW)

        def ffn_weights(self, after):
            (g_up,) = _exchange_wait(handles["scan"], after)
            return g_up.reshape(2 * D_FF, D)

    grad_x = _local_step(
        x, ctx, loss_target, modv, mcv, gam, g_mix, g_ffn, g_norm_a, ln_v_g, ln_v_b, w_s[0], b_s[0],
        g_final.reshape(1, D), w_inT, Comm())
    last, last_started = _exchange_start(leftover["items"], "gather_small_late", after=leftover["items"][0][0])

    (r_up,) = _exchange_wait(handles["tail_bwd"], last_started)
    r_down, r_o, r_pa, r_pb, r_small, r_dws = _exchange_wait(handles["scan_bwd"], r_up)
    raw_up = _adamw_sum8(r_up, w_up[0].T, m_w_up[0].T, v_w_up[0].T, "adamw_w_up")
    raw_down = _adamw_sum8(r_down, w_down[0], m_w_down[0], v_w_down[0], "adamw_w_down")
    raw_o = _adamw_sum8(r_o, w_o[0], m_w_o[0], v_w_o[0], "adamw_w_o")
    (r_in,) = _exchange_wait(handles["inproj_bwd"], raw_up[1])
    raw_in = _adamw_sum8(r_in, w_in[0].T, m_w_in[0].T, v_w_in[0].T, "adamw_w_in")
    (r_late,) = _exchange_wait(last, raw_in[1])
    done = {"w_in": [a.T[None] for a in raw_in], "w_up": [a.T[None] for a in raw_up],
            "w_down": [a[None] for a in raw_down], "w_o": [a[None] for a in raw_o]}
    grad_w_in, grad_w_up, grad_w_down, grad_w_o = (done[k][0] for k in ("w_in", "w_up", "w_down", "w_o"))
    grad_w_pa = _sum8(r_pa, "sum_w_pa").T[None]
    grad_w_pb = _sum8(r_pb, "sum_w_pb").T[None]
    tot, bm = _small_reduce(r_small, r_late, gam, nb_ex)
    loss = tot[14, 0]
    grad_g_mix, grad_g_ffn, grad_g_final = tot[0:1], tot[1:2], tot[2]
    grad_g_norm_a = tot[3:4, 0:DK]
    grad_ln_v_g, grad_ln_v_b = tot[4:5, 0:KW], tot[5:6, 0:KW]
    grad_b_s = tot[6, 0:GROUPS * SGU_BLOCK].reshape(1, GROUPS, SGU_BLOCK)
    grad_w_s = _sum8(r_dws, "sum_w_s").reshape(1, GROUPS, SGU_BLOCK, SGU_BLOCK)
    grad_b_mod = bm[0:N_MOD].reshape(1, N_MOD * D)
    grad_lb_gamma = lax.dynamic_slice(bm[6:8].reshape(2, 2, KW), (0, 0, me * lb_cols), (2, 2, lb_cols))

    dmod_all = r_late[:, 16:16 + nb_ex * N_MOD].reshape(n_c, N_MOD * D)
    dmod_l = jnp.concatenate([lax.dynamic_slice(dmod_all, (0, me * mod_cols), (n_c, mod_cols)),
                              lax.dynamic_slice(tot[8:8 + N_MOD].reshape(1, N_MOD * D), (0, me * mod_cols), (1, mod_cols)),
                              jnp.zeros((7, mod_cols), F32)], axis=0)
    gw_mod, gc = _mod_bwd(svec, cvec, dmod_l, w_mod[0])
    grad_w_mod = gw_mod[None]
    (r_gc,) = _exchange([(gc[n_c:n_c + 8], "gather")], "gather_c_ctx", after=r_late)
    grad_c_ctx = _sum8(r_gc, "sum_c_ctx")[0]

    names = ["c_ctx", "w_mod", "b_mod", "g_mix", "g_ffn", "w_in", "lb_gamma", "g_norm_a", "ln_v_g", "ln_v_b", "w_s",
             "b_s", "w_pa", "w_pb", "w_o", "w_up", "w_down", "g_final"]
    weights = [c_ctx, w_mod, b_mod, g_mix, g_ffn, w_in, lb_gamma, g_norm_a, ln_v_g, ln_v_b, w_s, b_s, w_pa, w_pb, w_o,
               w_up, w_down, g_final]
    grads = [grad_c_ctx, grad_w_mod, grad_b_mod, grad_g_mix, grad_g_ffn, grad_w_in, grad_lb_gamma, grad_g_norm_a,
             grad_ln_v_g, grad_ln_v_b, grad_w_s, grad_b_s, grad_w_pa, grad_w_pb, grad_w_o, grad_w_up, grad_w_down,
             grad_g_final]
    ms = [m_c_ctx, m_w_mod, m_b_mod, m_g_mix, m_g_ffn, m_w_in, m_lb_gamma, m_g_norm_a, m_ln_v_g, m_ln_v_b, m_w_s, m_b_s,
          m_w_pa, m_w_pb, m_w_o, m_w_up, m_w_down, m_g_final]
    vs = [v_c_ctx, v_w_mod, v_b_mod, v_g_mix, v_g_ffn, v_w_in, v_lb_gamma, v_g_norm_a, v_ln_v_g, v_ln_v_b, v_w_s, v_b_s,
          v_w_pa, v_w_pb, v_w_o, v_w_up, v_w_down, v_g_final]
    deltas, new_ms, new_vs = [], [], []
    for nm, w, g, m, v in zip(names, weights, grads, ms, vs):
        d, nm_, nv_ = done[nm][1:] if nm in done else _adamw(w, g.reshape(w.shape), m, v, "adamw_" + nm)
        deltas.append(d)
        new_ms.append(nm_)
        new_vs.append(nv_)
    grads = [g.reshape(w.shape) for g, w in zip(grads, weights)]
    return (loss, grad_x, *grads, *deltas, *new_ms, *new_vs)
```

```python
import functools

import jax
import jax.numpy as jnp
from jax import lax
from jax.experimental import pallas as pl
from jax.experimental.pallas import tpu as pltpu

F32 = jnp.float32
MXU_DTYPE = jnp.bfloat16
PAYLOAD_DTYPE = jnp.bfloat16

N_DEV = 8
D = 1024
HEADS = 4
DK = 128
KW = HEADS * DK
CHUNK = 64
SGU_BLOCK = 128
GROUPS = 4
D_FF = 2816
FF_CHUNK = 256
N_MOD = 6
IN_COLS = 5632
CTX_COLS = 1536
TAIL_COLS = IN_COLS - 4 * KW
EPS = 1e-6
ADAM_LR, ADAM_B1, ADAM_B2, ADAM_EPS, ADAM_WD, ADAM_STEP = 0.001, 0.9, 0.999, 1e-08, 0.01, 10

VMEM_LIMIT = 56 * 1024 * 1024
TOKEN_TILE = 256
SMALL_ROWS = 40


def _params(sem):
    return pltpu.CompilerParams(dimension_semantics=sem, vmem_limit_bytes=VMEM_LIMIT)


_DN = {"nn": (((1,), (0,)), ((), ())), "nt": (((1,), (1,)), ((), ())), "tn": (((0,), (0,)), ((), ()))}


def _dot(a, b, form="nn"):
    return lax.dot_general(a.astype(MXU_DTYPE), b.astype(MXU_DTYPE), _DN[form], preferred_element_type=F32)


def _mask_dot(mask, v):
    bf = jnp.bfloat16
    hi = v.astype(bf)
    r1 = v - hi.astype(F32)
    mid = r1.astype(bf)
    lo = (r1 - mid.astype(F32)).astype(bf)
    w = v.shape[1]
    s = lax.dot_general(mask.astype(bf), jnp.concatenate([hi, mid, lo], axis=1), _DN["nn"], preferred_element_type=F32)
    return (s[:, 2 * w:] + s[:, w:2 * w]) + s[:, :w]


def _full(shape, single=False):
    n = len(shape)
    if single:
        return pl.BlockSpec(shape, lambda *_: (0,) * n, pipeline_mode=pl.Buffered(1))
    return pl.BlockSpec(shape, lambda *_: (0,) * n)


def _ordered_behind(body, in_specs, args, after):
    if after is None:
        return body
    at = len(in_specs)
    in_specs.append(pl.BlockSpec(memory_space=pl.ANY))
    args.append(after)
    return lambda *refs: body(*refs[:at], *refs[at + 1:])


def _sigmoid(z):
    return 0.5 * jnp.tanh(0.5 * z) + 0.5


def _gelu(x):
    c = 0.7978845608028654
    t = jnp.tanh(c * (x + 0.044715 * x * x * x))
    return 0.5 * x * (1.0 + t), t


def _gelu_grad(x, t):
    c = 0.7978845608028654
    return 0.5 * (1.0 + t) + 0.5 * x * (1.0 - t * t) * c * (1.0 + 3 * 0.044715 * x * x)


def _exchange(items, name, after=None):
    n = len(items)
    out_shape = []
    for a, mode in items:
        blk = a.shape if mode == "gather" else a.shape[1:]
        out_shape.append(jax.ShapeDtypeStruct((N_DEV,) + tuple(blk), a.dtype))

    def body(*refs):
        srcs, dsts = refs[:n], refs[n:2 * n]
        send_sems, recv_sems, local_sems = refs[2 * n:]
        x, y, c = lax.axis_index("x"), lax.axis_index("y"), lax.axis_index("c")
        me = 4 * x + 2 * y + c

        def src_for(i, dev):
            return srcs[i] if items[i][1] == "gather" else srcs[i].at[dev]

        local = [pltpu.make_async_copy(src_for(i, me), dsts[i].at[me], local_sems.at[i]) for i in range(n)]
        for cp in local:
            cp.start()
        remote = []
        for k in range(1, N_DEV):
            px = jnp.bitwise_xor(x, (k >> 2) & 1)
            py = jnp.bitwise_xor(y, (k >> 1) & 1)
            pc = jnp.bitwise_xor(c, k & 1)
            peer = 4 * px + 2 * py + pc
            for i in range(n):
                cp = pltpu.make_async_remote_copy(
                    src_ref=src_for(i, peer), dst_ref=dsts[i].at[me],
                    send_sem=send_sems.at[i * (N_DEV - 1) + k - 1], recv_sem=recv_sems.at[i * (N_DEV - 1) + k - 1],
                    device_id=(px, py, pc), device_id_type=pl.DeviceIdType.MESH)
                cp.start()
                remote.append(cp)
        for cp in remote:
            cp.wait()
        for cp in local:
            cp.wait()

    any_spec = pl.BlockSpec(memory_space=pl.ANY)
    in_specs, args = [any_spec] * n, [a for a, _ in items]
    if after is not None:
        in_specs.append(any_spec)
        args.append(after)
        exchange = body
        body = lambda *refs: exchange(*refs[:n], *refs[n + 1:])
    return pl.pallas_call(
        body, name=name, out_shape=out_shape, in_specs=in_specs, out_specs=[any_spec] * n,
        scratch_shapes=[pltpu.SemaphoreType.DMA((n * (N_DEV - 1),)), pltpu.SemaphoreType.DMA((n * (N_DEV - 1),)),
                        pltpu.SemaphoreType.DMA((n,))],
    )(*args)


def _gather_two_level(arrays, name):
    n = len(arrays)

    def body(*refs):
        srcs, dsts = refs[:n], refs[n:2 * n]
        send_sems, recv_sems, local_sems = refs[2 * n:]
        x, y, c = lax.axis_index("x"), lax.axis_index("y"), lax.axis_index("c")
        sibling = (x, y, 1 - c)
        chips = [(1 - x, y), (x, 1 - y), (1 - x, 1 - y)]

        def slot(px, py, pc):
            return 4 * px + 2 * py + pc

        def copy(i, k, block, to, src=None):
            return pltpu.make_async_remote_copy(
                src_ref=dsts[i].at[slot(*block)] if src is None else src, dst_ref=dsts[i].at[slot(*block)],
                send_sem=send_sems.at[i * 7 + k], recv_sem=recv_sems.at[i * 7 + k],
                device_id=to, device_id_type=pl.DeviceIdType.MESH)

        me = (x, y, c)
        mine = [pltpu.make_async_copy(srcs[i], dsts[i].at[slot(*me)], local_sems.at[i]) for i in range(n)]
        for cp in mine:
            cp.start()
        first = []
        for j, chip in enumerate(chips):
            first += [copy(i, 1 + j, me, (*chip, c), src=srcs[i]) for i in range(n)]
        first += [copy(i, 0, me, sibling, src=srcs[i]) for i in range(n)]
        for cp in first:
            cp.start()
        passed = []
        for j, chip in enumerate(chips):
            for i in range(n):
                copy(i, 1 + j, (*chip, c), me).wait_recv()
                cp = copy(i, 4 + j, (*chip, c), sibling)
                cp.start()
                passed.append(cp)
        for i in range(n):
            copy(i, 0, sibling, me).wait_recv()
            for j, chip in enumerate(chips):
                copy(i, 4 + j, (*chip, 1 - c), me).wait_recv()
        for cp in first + passed:
            cp.wait_send()
        for cp in mine:
            cp.wait()

    any_spec = pl.BlockSpec(memory_space=pl.ANY)
    return pl.pallas_call(
        body, name=name, out_shape=[jax.ShapeDtypeStruct((N_DEV,) + a.shape, a.dtype) for a in arrays],
        in_specs=[any_spec] * n, out_specs=[any_spec] * n,
        scratch_shapes=[pltpu.SemaphoreType.DMA((n * 7,)), pltpu.SemaphoreType.DMA((n * 7,)),
                        pltpu.SemaphoreType.DMA((n,))],
    )(*arrays)


_HBM = pl.BlockSpec(memory_space=pltpu.HBM)
_SEM = pl.BlockSpec(memory_space=pltpu.SEMAPHORE)
_EFFECT = pltpu.SideEffectType.DATAFLOW_SIDE_EFFECTING


def _split_copies(items, srcs, lands, send_sems, recv_sems):
    x, y, c = lax.axis_index("x"), lax.axis_index("y"), lax.axis_index("c")
    me = 4 * x + 2 * y + c
    copies = []
    for k in range(1, N_DEV):
        px = jnp.bitwise_xor(x, (k >> 2) & 1)
        py = jnp.bitwise_xor(y, (k >> 1) & 1)
        pc = jnp.bitwise_xor(c, k & 1)
        peer = 4 * px + 2 * py + pc
        for i in range(len(items)):
            src = srcs[i] if items[i][1] == "gather" else srcs[i].at[peer]
            copies.append(pltpu.make_async_remote_copy(
                src_ref=src, dst_ref=lands[i].at[me],
                send_sem=send_sems.at[i * (N_DEV - 1) + k - 1], recv_sem=recv_sems.at[i * (N_DEV - 1) + k - 1],
                device_id=(px, py, pc), device_id_type=pl.DeviceIdType.MESH))
    return me, copies


def _exchange_start(items, name, after):
    n = len(items)
    n_sem = n * (N_DEV - 1)
    srcs, lands = [], []
    for a, mode in items:
        blk = a.shape if mode == "gather" else a.shape[1:]
        srcs.append(pltpu.with_memory_space_constraint(a, pltpu.HBM))
        lands.append(pltpu.with_memory_space_constraint(lax.empty((N_DEV,) + tuple(blk), a.dtype), pltpu.HBM))

    def body(*refs):
        src_refs, land_refs = refs[:n], refs[n:2 * n]
        send_sems, recv_sems = refs[2 * n + 1], refs[2 * n + 2]
        local_sems = refs[4 * n + 3]
        me, copies = _split_copies(items, src_refs, land_refs, send_sems, recv_sems)
        for i in range(n):
            own = src_refs[i] if items[i][1] == "gather" else src_refs[i].at[me]
            cp = pltpu.make_async_copy(own, land_refs[i].at[me], local_sems.at[i])
            cp.start()
            cp.wait()
        for cp in copies:
            cp.start()

    out_shape = [pltpu.SemaphoreType.DMA((n_sem,)), pltpu.SemaphoreType.DMA((n_sem,))]
    out_shape += [pltpu.HBM(a.shape, a.dtype) for a in srcs] + [pltpu.HBM(a.shape, a.dtype) for a in lands]
    outs = pl.pallas_call(
        body, name=name, out_shape=out_shape,
        in_specs=[_HBM] * (2 * n) + [pl.BlockSpec(memory_space=pl.ANY)],
        out_specs=[_SEM, _SEM] + [_HBM] * (2 * n),
        input_output_aliases={i: 2 + i for i in range(2 * n)},
        scratch_shapes=[pltpu.SemaphoreType.DMA((n,))],
        compiler_params=pltpu.CompilerParams(has_side_effects=_EFFECT),
    )(*srcs, *lands, after)
    handle = (items, name, outs[0], outs[1], outs[2:2 + n], outs[2 + n:2 + 2 * n])
    return handle, outs[2]


class _Sender:
    PIECE_ROWS = 352

    def __init__(self, items, chunks=None):
        self.items, self.n = items, len(items)
        self.chunks = chunks
        if chunks is None:
            block_rows = [a.shape[0] if mode == "gather" else a.shape[1] for a, mode in items]
            self.chunks = [r // self.PIECE_ROWS if r % self.PIECE_ROWS == 0 else 1 for r in block_rows]
        self.srcs, self.lands = [], []
        for a, mode in items:
            blk = a.shape if mode == "gather" else a.shape[1:]
            self.srcs.append(pltpu.with_memory_space_constraint(a, pltpu.HBM))
            self.lands.append(pltpu.with_memory_space_constraint(lax.empty((N_DEV,) + tuple(blk), a.dtype), pltpu.HBM))

    def issue(self, src_refs, land_refs, send_sems, recv_sems, local_sems, step, n_steps):
        x, y, c = lax.axis_index("x"), lax.axis_index("y"), lax.axis_index("c")
        me = 4 * x + 2 * y + c
        copies = []
        for ch in range(max(self.chunks)):
            for k in range(1, N_DEV):
                px = jnp.bitwise_xor(x, (k >> 2) & 1)
                py = jnp.bitwise_xor(y, (k >> 1) & 1)
                pc = jnp.bitwise_xor(c, k & 1)
                peer = 4 * px + 2 * py + pc
                for i, (_, mode) in enumerate(self.items):
                    if ch >= self.chunks[i]:
                        continue
                    n_rows = land_refs[i].shape[1] // self.chunks[i]
                    rows = pl.ds(ch * n_rows, n_rows)
                    src = src_refs[i].at[rows] if mode == "gather" else src_refs[i].at[peer].at[rows]
                    copies.append(pltpu.make_async_remote_copy(
                        src_ref=src, dst_ref=land_refs[i].at[me].at[rows],
                        send_sem=send_sems.at[i * (N_DEV - 1) + k - 1], recv_sem=recv_sems.at[i * (N_DEV - 1) + k - 1],
                        device_id=(px, py, pc), device_id_type=pl.DeviceIdType.MESH))
        own = [pltpu.make_async_copy(src_refs[i] if mode == "gather" else src_refs[i].at[me], land_refs[i].at[me],
                                     local_sems.at[i]) for i, (_, mode) in enumerate(self.items)]

        @pl.when(step == 0)
        def _():
            for cp in own:
                cp.start()

        for s in range(n_steps):
            group = [cp for j, cp in enumerate(copies) if (j * n_steps) // len(copies) == s]
            if group:
                @pl.when(step == s)
                def _(group=group):
                    for cp in group:
                        cp.start()

        @pl.when(step == n_steps - 1)
        def _():
            for cp in own:
                cp.wait()


def _host_call(body, name, grid, in_specs, args, out_shape, out_specs, scratch_shapes, after=None, sender=None):
    in_specs, args, out_shape, out_specs = list(in_specs), list(args), list(out_shape), list(out_specs)
    scratch_shapes = list(scratch_shapes)
    semantics = ("arbitrary",) * len(grid)
    body = _ordered_behind(body, in_specs, args, after)
    if sender is None:
        res = pl.pallas_call(body, name=name, grid=grid, in_specs=in_specs, out_specs=out_specs, out_shape=out_shape,
                             scratch_shapes=scratch_shapes, compiler_params=_params(semantics))(*args)
        return res, None
    n, n_in, n_out, n_scr = sender.n, len(in_specs), len(out_shape), len(scratch_shapes)
    n_sem = n * (N_DEV - 1)
    n_steps = 1
    for g in grid:
        n_steps *= g
    compute = body

    def body(*refs):
        ins, s_in = refs[:n_in], refs[n_in:n_in + 2 * n]
        o0 = n_in + 2 * n
        outs, s_out = refs[o0:o0 + n_out], refs[o0 + n_out:o0 + n_out + 2 + 2 * n]
        scr = refs[o0 + n_out + 2 + 2 * n:]
        compute(*ins, *outs, *scr[:n_scr])
        step = pl.program_id(0)
        for d in range(1, len(grid)):
            step = step * grid[d] + pl.program_id(d)
        sender.issue(s_in[:n], s_in[n:], s_out[0], s_out[1], scr[n_scr], step, n_steps)

    res = pl.pallas_call(
        body, name=name, grid=grid,
        in_specs=in_specs + [_HBM] * (2 * n), out_specs=out_specs + [_SEM, _SEM] + [_HBM] * (2 * n),
        out_shape=out_shape + [pltpu.SemaphoreType.DMA((n_sem,)), pltpu.SemaphoreType.DMA((n_sem,))]
        + [pltpu.HBM(a.shape, a.dtype) for a in sender.srcs] + [pltpu.HBM(a.shape, a.dtype) for a in sender.lands],
        input_output_aliases={n_in + j: n_out + 2 + j for j in range(2 * n)},
        scratch_shapes=scratch_shapes + [pltpu.SemaphoreType.DMA((n,))],
        compiler_params=pltpu.CompilerParams(dimension_semantics=semantics, vmem_limit_bytes=VMEM_LIMIT,
                                             has_side_effects=_EFFECT),
    )(*args, *sender.srcs, *sender.lands)
    handle = (sender.items, name, res[n_out], res[n_out + 1], res[n_out + 2:n_out + 2 + n],
              res[n_out + 2 + n:n_out + 2 + 2 * n])
    return res[:n_out], handle


def _exchange_wait(handle, after):
    items, name, send_sems, recv_sems, srcs, lands = handle
    n = len(items)

    def body(*refs):
        src_refs, land_refs = refs[:n], refs[n:2 * n]
        send_ref, recv_ref = refs[2 * n], refs[2 * n + 1]
        _, copies = _split_copies(items, src_refs, land_refs, send_ref, recv_ref)
        for cp in copies:
            cp.wait_send()
            cp.wait_recv()

    outs = pl.pallas_call(
        body, name=name + "_wait",
        out_shape=[pltpu.HBM(a.shape, a.dtype) for a in srcs] + [pltpu.HBM(a.shape, a.dtype) for a in lands],
        in_specs=[_HBM] * (2 * n) + [_SEM, _SEM, pl.BlockSpec(memory_space=pl.ANY)], out_specs=[_HBM] * (2 * n),
        input_output_aliases={i: i for i in range(2 * n)},
        compiler_params=pltpu.CompilerParams(has_side_effects=_EFFECT),
    )(*srcs, *lands, send_sems, recv_sems, after)
    return outs[n:]


def _mod_fwd(cvec, w_mod_l, b_mod_l):
    rows, cols = cvec.shape[0], w_mod_l.shape[1]

    def body(c_ref, w_ref, b_ref, o_ref, s_ref):
        cv = c_ref[...]
        s = cv * _sigmoid(cv)
        s_ref[...] = s
        o_ref[...] = _dot(s, w_ref[...]) + b_ref[...]

    return pl.pallas_call(
        body, name="mod_fwd",
        out_shape=(jax.ShapeDtypeStruct((rows, cols), F32), jax.ShapeDtypeStruct((rows, D), F32)),
        in_specs=[_full((rows, D)), _full((D, cols)), _full((1, cols))],
        out_specs=(_full((rows, cols)), _full((rows, D))), grid=(1,),
        compiler_params=_params(("arbitrary",)),
    )(cvec, w_mod_l, b_mod_l)


def _mod_bwd(svec, cvec, dmod_l, w_mod_l):
    rows, cols = dmod_l.shape

    def body(s_ref, c_ref, d_ref, w_ref, gw_ref, gc_ref):
        gw_ref[...] = _dot(s_ref[...], d_ref[...], "tn")
        cv = c_ref[...]
        sg = _sigmoid(cv)
        gc_ref[...] = _dot(d_ref[...], w_ref[...], "nt") * (sg * (1.0 + cv * (1.0 - sg)))

    return pl.pallas_call(
        body, name="mod_bwd",
        out_shape=(jax.ShapeDtypeStruct((D, cols), F32), jax.ShapeDtypeStruct((rows, D), F32)),
        in_specs=[_full((rows, D)), _full((rows, D)), _full((rows, cols)), _full((D, cols))],
        out_specs=(_full((D, cols)), _full((rows, D))), grid=(1,),
        compiler_params=_params(("arbitrary",)),
    )(svec, cvec, dmod_l, w_mod_l)


def _inproj(xt, modv, g, w_inT, n_cols, rows_per_example, name, after=None, sender=None):
    rows = xt.shape[0]
    tm = min(TOKEN_TILE, rows_per_example)
    per_b = rows_per_example // tm
    shared_mod = modv.shape[0] == 1

    def body(x_ref, mod_ref, g_ref, w_ref, p_ref, h_ref):
        x = x_ref[...]
        r = lax.rsqrt(jnp.mean(x * x, axis=-1, keepdims=True) + EPS)
        h = (x * r * g_ref[...]) * (1.0 + mod_ref[0, 1:2, :]) + mod_ref[0, 0:1, :]
        hb = h.astype(MXU_DTYPE)
        h_ref[...] = hb
        for j in range(n_cols // KW):
            p_ref[:, j * KW:(j + 1) * KW] = _dot(hb, w_ref[j * KW:(j + 1) * KW, :], "nt").astype(p_ref.dtype)

    mod_idx = (lambda i: (0, 0, 0)) if shared_mod else (lambda i: (i // per_b, 0, 0))
    in_specs = [pl.BlockSpec((tm, D), lambda i: (i, 0)), pl.BlockSpec((1, N_MOD, D), mod_idx), _full((1, D)),
                pl.BlockSpec((n_cols, D), lambda i: (0, 0), pipeline_mode=pl.Buffered(1))]
    (p, h), handle = _host_call(
        body, name, (rows // tm,), in_specs, [xt, modv, g, w_inT],
        [jax.ShapeDtypeStruct((rows, n_cols), MXU_DTYPE), jax.ShapeDtypeStruct((rows, D), MXU_DTYPE)],
        [pl.BlockSpec((tm, n_cols), lambda i: (i, 0)), pl.BlockSpec((tm, D), lambda i: (i, 0))], [],
        after=after, sender=sender)
    return p, h, handle


def _tri(reverse, n):
    row = lax.broadcasted_iota(jnp.int32, (n, n), 0)
    col = lax.broadcasted_iota(jnp.int32, (n, n), 1)
    same = (row // CHUNK) == (col // CHUNK)
    return same & ((col >= row) if reverse else (col <= row))


def _per_chunk_rows(x, reverse):
    n = x.shape[0]
    rows = [x[j * CHUNK:j * CHUNK + 1] if reverse else x[(j + 1) * CHUNK - 1:(j + 1) * CHUNK] for j in range(n // CHUNK)]
    return jnp.concatenate([jnp.broadcast_to(r, (CHUNK, x.shape[1])) for r in rows], axis=0), rows


def _lower_bound(gam_ref, direction):
    return _sigmoid(gam_ref[direction:direction + 1, :] - gam_ref[2 + direction:3 + direction, :])


def _gate_prep(z, lb, tri, reverse):
    sg = _sigmoid(z)
    f = lb + (1.0 - lb) * sg
    g = jnp.log(f)
    b = _mask_dot(tri, g)
    bl, bl_rows = _per_chunk_rows(b, reverse)
    mid = 0.5 * bl
    return sg, g, 1.0 - f, b, jnp.exp(mid), [jnp.exp(0.5 * r) for r in bl_rows], jnp.exp(mid - b), mid


def _hgrn_fwd(p, gam, s0, rows_per_example, with_out, name, sender=None):
    rows = p.shape[0]
    nb_ex = rows // rows_per_example
    rb = min(TOKEN_TILE, rows_per_example)
    cpb = rb // CHUNK
    nb = rows_per_example // rb
    n_chunks = rows // CHUNK
    has_s0 = s0 is not None

    def body(*refs):
        it = iter(refs)
        gam_ref = next(it)
        zf_ref, vf_ref = next(it), next(it)
        qf_ref = next(it) if with_out else None
        zb_ref, vb_ref = next(it), next(it)
        qb_ref = next(it) if with_out else None
        s0_ref = next(it) if has_s0 else None
        if with_out:
            of_ref, ob_ref = next(it), next(it)
        stash_f, stash_b, fin_ref = next(it), next(it), next(it)
        st_ref = next(it)
        i = pl.program_id(1)

        @pl.when(i == 0)
        def _():
            if has_s0:
                st_ref[...] = s0_ref[:, 0]
            else:
                st_ref[...] = jnp.zeros_like(st_ref)

        for direction, (z_ref, v_ref, q_ref, stash) in enumerate(
                ((zf_ref, vf_ref, qf_ref, stash_f), (zb_ref, vb_ref, qb_ref, stash_b))):
            reverse = direction == 1
            tri = _tri(reverse, rb)
            lb = _lower_bound(gam_ref, direction)
            z = z_ref[...].astype(F32)
            v = v_ref[...].astype(F32)
            _, _, k, b, em, em_rows, e2, mid = _gate_prep(z, lb, tri, reverse)
            kd = (k * (e2 * em)).astype(MXU_DTYPE)
            vb = v.astype(MXU_DTYPE)
            if with_out:
                q = q_ref[...].astype(F32)
                qi = q * jnp.exp(b - mid)
                qe = (qi * em).astype(MXU_DTYPE)
                qi = qi.astype(MXU_DTYPE)
                ki = (k * e2).astype(MXU_DTYPE)
                intra = []
                for h in range(HEADS):
                    hs = slice(h * DK, (h + 1) * DK)
                    sc = jnp.where(tri, _dot(qi[:, hs], ki[:, hs], "nt"), 0.0)
                    intra.append(_dot(sc, vb[:, hs]))
            for j in (range(cpb - 1, -1, -1) if reverse else range(cpb)):
                rs = slice(j * CHUNK, (j + 1) * CHUNK)
                a = em_rows[j] * em_rows[j]
                for h in range(HEADS):
                    hs = slice(h * DK, (h + 1) * DK)
                    st = st_ref[direction, h]
                    stash[j, h] = st.astype(stash.dtype)
                    if with_out:
                        (ob_ref if reverse else of_ref)[rs, hs] = intra[h][rs] + _dot(qe[rs, hs], st, "nt")
                    st_ref[direction, h] = st * a[:, hs] + _dot(vb[rs, hs], kd[rs, hs], "tn")

        @pl.when(i == nb - 1)
        def _():
            fin_ref[:, 0] = st_ref[...]

    up = lambda b, i: b * nb + i
    down = lambda b, i: b * nb + nb - 1 - i
    col = lambda rowf, c: pl.BlockSpec((rb, KW), lambda b, i: (rowf(b, i), c))
    in_specs = [_full((4, KW)), col(up, 0), col(up, 2)] + ([col(up, 3)] if with_out else [])
    in_specs += [col(down, 1), col(down, 2)] + ([col(down, 3)] if with_out else [])
    args = [gam, p, p] + ([p] if with_out else []) + [p, p] + ([p] if with_out else [])
    if has_s0:
        in_specs.append(pl.BlockSpec((2, 1, HEADS, DK, DK), lambda b, i: (0, b, 0, 0, 0)))
        args.append(s0)
    out_shape, out_specs = [], []
    if with_out:
        out_shape += [jax.ShapeDtypeStruct((rows, KW), F32)] * 2
        out_specs += [pl.BlockSpec((rb, KW), lambda b, i: (up(b, i), 0)),
                      pl.BlockSpec((rb, KW), lambda b, i: (down(b, i), 0))]
    out_shape += [jax.ShapeDtypeStruct((n_chunks, HEADS, DK, DK), MXU_DTYPE)] * 2
    out_specs += [pl.BlockSpec((cpb, HEADS, DK, DK), lambda b, i: (up(b, i), 0, 0, 0)),
                  pl.BlockSpec((cpb, HEADS, DK, DK), lambda b, i: (down(b, i), 0, 0, 0))]
    out_shape.append(jax.ShapeDtypeStruct((2, nb_ex, HEADS, DK, DK), F32))
    out_specs.append(pl.BlockSpec((2, 1, HEADS, DK, DK), lambda b, i: (0, b, 0, 0, 0)))
    res, handle = _host_call(body, name, (nb_ex, nb), in_specs, args, out_shape, out_specs,
                             [pltpu.VMEM((2, HEADS, DK, DK), F32)], sender=sender)
    return (*res, handle)


def _hgrn_bwd(p, gam, do, stash_f, stash_b, ds_end, rows_per_example, with_out, name, after=None, sender=None):
    rows = p.shape[0]
    nb_ex = rows // rows_per_example
    rb = min(TOKEN_TILE, rows_per_example)
    cpb = rb // CHUNK
    nb = rows_per_example // rb
    has_end = ds_end is not None

    def body(*refs):
        it = iter(refs)
        gam_ref = next(it)
        ins = []
        for _ in range(2):
            z_ref, v_ref = next(it), next(it)
            q_ref = next(it) if with_out else None
            do_ref = next(it) if with_out else None
            ins.append((z_ref, v_ref, q_ref, do_ref, next(it)))
        end_ref = next(it) if has_end else None
        outs = [next(it), next(it)]
        dlb_ref, ds0_ref = next(it), next(it)
        dst_ref = next(it)
        b_id, i = pl.program_id(0), pl.program_id(1)

        @pl.when(i == 0)
        def _():
            if has_end:
                dst_ref[...] = end_ref[:, 0]
            else:
                dst_ref[...] = jnp.zeros_like(dst_ref)

        @pl.when((i == 0) & (b_id == 0))
        def _():
            dlb_ref[...] = jnp.zeros_like(dlb_ref)

        for direction in range(2):
            z_ref, v_ref, q_ref, do_ref, stash = ins[direction]
            dgrp_ref = outs[direction]
            reverse = direction == 1
            tri = _tri(reverse, rb)
            tri_t = _tri(not reverse, rb)
            lb = _lower_bound(gam_ref, direction)
            heads = [slice(h * DK, (h + 1) * DK) for h in range(HEADS)]
            chunks = [slice(j * CHUNK, (j + 1) * CHUNK) for j in range(cpb)]
            grid_cat = lambda parts: jnp.concatenate([jnp.concatenate(row, axis=1) for row in parts], axis=0)
            cat = lambda parts: jnp.concatenate(parts, axis=1)
            z = z_ref[...].astype(F32)
            sg, g, k, b, em, em_rows, e2, mid = _gate_prep(z, lb, tri, reverse)
            e3 = e2 * em
            kd = k * e3
            kd_b = kd.astype(MXU_DTYPE)
            vb = v_ref[...].astype(MXU_DTYPE)
            if with_out:
                q = q_ref[...].astype(F32)
                dout = do_ref[...].astype(MXU_DTYPE)
                e1 = jnp.exp(b - mid)
                e4 = e1 * em
                qi, ki, qe = q * e1, k * e2, q * e4
                qi_b, ki_b, qe_b = qi.astype(MXU_DTYPE), ki.astype(MXU_DTYPE), qe.astype(MXU_DTYPE)
                dqi_p, dki_p, dv_p = [], [], []
                for hs in heads:
                    sc = jnp.where(tri, _dot(qi_b[:, hs], ki_b[:, hs], "nt"), 0.0)
                    dsc = jnp.where(tri, _dot(dout[:, hs], vb[:, hs], "nt"), 0.0)
                    dqi_p.append(_dot(dsc, ki_b[:, hs]))
                    dki_p.append(_dot(dsc, qi_b[:, hs], "tn"))
                    dv_p.append(_dot(sc, dout[:, hs], "tn"))
                dqi, dki, dv = cat(dqi_p), cat(dki_p), cat(dv_p)
                dqe = grid_cat([[_dot(dout[rs, hs], stash[j, h]) for h, hs in enumerate(heads)]
                                for j, rs in enumerate(chunks)])
                grow = [[_dot(dout[rs, hs], qe_b[rs, hs], "tn") for hs in heads] for rs in chunks]
            dkd_p = [[None] * HEADS for _ in range(cpb)]
            dvs_p = [[None] * HEADS for _ in range(cpb)]
            da_p = [[None] * HEADS for _ in range(cpb)]
            for j in (range(cpb) if reverse else range(cpb - 1, -1, -1)):
                rs = chunks[j]
                a = em_rows[j] * em_rows[j]
                for h, hs in enumerate(heads):
                    dst = dst_ref[direction, h]
                    dkd_p[j][h] = _dot(vb[rs, hs], dst)
                    dvs_p[j][h] = _dot(kd_b[rs, hs], dst, "nt")
                    da_p[j][h] = jnp.broadcast_to(
                        jnp.sum(dst * stash[j, h].astype(F32), axis=0, keepdims=True), (CHUNK, DK))
                    new_dst = dst * a[:, hs]
                    dst_ref[direction, h] = new_dst + grow[j][h] if with_out else new_dst
            dkd, dvs, da = grid_cat(dkd_p), grid_cat(dvs_p), grid_cat(da_p)
            t_kd = dkd * kd
            dk = dkd * e3
            db = -t_kd
            tot = t_kd
            if with_out:
                dgrp_ref[:, KW:2 * KW] = (dvs + dv).astype(dgrp_ref.dtype)
                dgrp_ref[:, 2 * KW:] = (dqi * e1 + dqe * e4).astype(dgrp_ref.dtype)
                dk = dk + dki * e2
                t_qi, t_ki, t_qe = dqi * qi, dki * ki, dqe * qe
                db = db + t_qi - t_ki + t_qe
                tot = tot + 0.5 * (t_ki - t_qi)
            else:
                dgrp_ref[:, KW:2 * KW] = dvs.astype(dgrp_ref.dtype)
            dbl = jnp.concatenate([jnp.broadcast_to(jnp.sum(tot[rs], axis=0, keepdims=True), (CHUNK, KW))
                                   for rs in chunks], axis=0) + da * (em * em)
            dg = _mask_dot(tri_t, db) + dbl
            df = dg * jnp.exp(-g) - dk
            dgrp_ref[:, 0:KW] = (df * (1.0 - lb) * sg * (1.0 - sg)).astype(dgrp_ref.dtype)
            dlb_ref[direction:direction + 1, :] += jnp.sum(df * (1.0 - sg), axis=0, keepdims=True)

        @pl.when(i == nb - 1)
        def _():
            ds0_ref[:, 0] = dst_ref[...]

    rows_of = (lambda b, i: b * nb + nb - 1 - i, lambda b, i: b * nb + i)
    in_specs, args = [_full((4, KW))], [gam]
    for direction in range(2):
        rf = rows_of[direction]
        col = lambda c, rf=rf: pl.BlockSpec((rb, KW), lambda b, i: (rf(b, i), c))
        in_specs += [col(direction), col(2)]
        args += [p, p]
        if with_out:
            in_specs += [col(3), col(0)]
            args += [p, do]
        in_specs.append(pl.BlockSpec((cpb, HEADS, DK, DK), lambda b, i, rf=rf: (rf(b, i), 0, 0, 0)))
        args.append((stash_f, stash_b)[direction])
    if has_end:
        in_specs.append(pl.BlockSpec((2, 1, HEADS, DK, DK), lambda b, i: (0, b, 0, 0, 0)))
        args.append(ds_end)
    out_shape, out_specs = [], []
    for direction in range(2):
        rf = rows_of[direction]
        width = (3 if with_out else 2) * KW
        out_shape.append(jax.ShapeDtypeStruct((rows, width), MXU_DTYPE))
        out_specs.append(pl.BlockSpec((rb, width), lambda b, i, rf=rf: (rf(b, i), 0)))
    out_shape += [jax.ShapeDtypeStruct((2, KW), F32), jax.ShapeDtypeStruct((2, nb_ex, HEADS, DK, DK), F32)]
    out_specs += [_full((2, KW)), pl.BlockSpec((2, 1, HEADS, DK, DK), lambda b, i: (0, b, 0, 0, 0))]
    res, handle = _host_call(body, name, (nb_ex, nb), in_specs, args, out_shape, out_specs,
                             [pltpu.VMEM((2, HEADS, DK, DK), F32)], after=after, sender=sender)
    return (*res, handle)


def _tail_forward(osum, og, u, v, ga, gb, gna, ln_g, ln_b, ws_ref, bs_ref, wpaT_ref, wpbT_ref):
    tm = osum.shape[0]
    gna4 = jnp.concatenate([gna] * HEADS, axis=1)
    r_parts = []
    for h in range(HEADS):
        oh = osum[:, h * DK:(h + 1) * DK]
        r_parts.append(jnp.broadcast_to(lax.rsqrt(jnp.mean(oh * oh, axis=-1, keepdims=True) + EPS), (tm, DK)))
    r = jnp.concatenate(r_parts, axis=1)
    on = osum * r
    sg_og = _sigmoid(og)
    silu_og = og * sg_og
    oan = on * gna4
    oa = oan * silu_og
    ug, tu = _gelu(u)
    vg, tv = _gelu(v)
    mu = jnp.mean(vg, axis=-1, keepdims=True)
    vc = vg - mu
    rstd = lax.rsqrt(jnp.mean(vc * vc, axis=-1, keepdims=True) + EPS)
    vhat = vc * rstd
    vln = vhat * ln_g + ln_b
    blocks = []
    for n in range(tm // SGU_BLOCK):
        rs = slice(n * SGU_BLOCK, (n + 1) * SGU_BLOCK)
        blocks.append(jnp.concatenate(
            [_dot(ws_ref[g], vln[rs, g * DK:(g + 1) * DK]) + bs_ref[g] for g in range(GROUPS)], axis=1))
    mixed = jnp.concatenate(blocks, axis=0) if len(blocks) > 1 else blocks[0]
    obm = ug * mixed
    pa = _dot(oa, wpaT_ref[...], "nt")
    pb = _dot(obm, wpbT_ref[...], "nt")
    sga, sgb = _sigmoid(ga), _sigmoid(gb)
    merged = sga * pa + sgb * pb
    return dict(r=r, on=on, sg_og=sg_og, silu_og=silu_og, oan=oan, oa=oa, ug=ug, tu=tu, tv=tv, rstd=rstd, vhat=vhat,
                vln=vln, mixed=mixed, obm=obm, pa=pa, pb=pb, sga=sga, sgb=sgb, merged=merged, gna4=gna4)


def _tail_in_specs(tm):
    tile = lambda c: pl.BlockSpec((tm, KW), lambda i: (i, c))
    return [tile(c) for c in range(4, 11)]


def _tail_weight_specs():
    return [_full((1, DK)), _full((1, KW)), _full((1, KW)), _full((GROUPS, SGU_BLOCK, SGU_BLOCK)),
            _full((GROUPS, SGU_BLOCK, 1)), _full((D, KW), single=True), _full((D, KW), single=True),
            _full((D, D), single=True)]


def _read_tail_inputs(of_ref, ob_ref, pcols):
    osum = of_ref[...] + ob_ref[...]
    og, u, v = (pcols[j][...].astype(F32) for j in range(3))
    ga = jnp.concatenate([pcols[3][...], pcols[4][...]], axis=1).astype(F32)
    gb = jnp.concatenate([pcols[5][...], pcols[6][...]], axis=1).astype(F32)
    return osum, og, u, v, ga, gb


def _tail_fwd(p, o_up, o_down, xt, modv, gna, ln_g, ln_b, w_s, b_s, w_paT, w_pbT, w_o, rows_per_example):
    rows = xt.shape[0]
    tm = min(TOKEN_TILE, rows_per_example)
    per_b = rows_per_example // tm

    def body(of_ref, ob_ref, *rest):
        pcols = rest[:7]
        (x_ref, mod_ref, gna_ref, lng_ref, lnb_ref, ws_ref, bs_ref, wpaT_ref, wpbT_ref, wo_ref,
         x1_ref, mix_ref, merged_ref, oa_ref, obm_ref) = rest[7:]
        t = _tail_forward(*_read_tail_inputs(of_ref, ob_ref, pcols), gna_ref[...], lng_ref[...], lnb_ref[...],
                          ws_ref, bs_ref, wpaT_ref, wpbT_ref)
        mix = _dot(t["merged"], wo_ref[...])
        x1_ref[...] = x_ref[...] + mod_ref[0, 2:3, :] * mix
        mix_ref[...] = mix.astype(mix_ref.dtype)
        merged_ref[...] = t["merged"].astype(merged_ref.dtype)
        oa_ref[...] = t["oa"].astype(oa_ref.dtype)
        obm_ref[...] = t["obm"].astype(obm_ref.dtype)

    row = lambda w: pl.BlockSpec((tm, w), lambda i: (i, 0))
    in_specs = [row(KW), row(KW)] + _tail_in_specs(tm) + [row(D), pl.BlockSpec((1, N_MOD, D), lambda i: (i // per_b, 0, 0))]
    in_specs += _tail_weight_specs()
    return pl.pallas_call(
        body, name="tail_fwd", grid=(rows // tm,),
        out_shape=(jax.ShapeDtypeStruct((rows, D), F32), jax.ShapeDtypeStruct((rows, D), MXU_DTYPE),
                   jax.ShapeDtypeStruct((rows, D), MXU_DTYPE), jax.ShapeDtypeStruct((rows, KW), MXU_DTYPE),
                   jax.ShapeDtypeStruct((rows, KW), MXU_DTYPE)),
        in_specs=in_specs, out_specs=(row(D), row(D), row(D), row(KW), row(KW)),
        compiler_params=_params(("arbitrary",)),
    )(o_up, o_down, *([p] * 7), xt, modv, gna, ln_g, ln_b, w_s, b_s, w_paT, w_pbT, w_o)


def _tail_bwd(p, o_up, o_down, dx1, mix, modv, gna, ln_g, ln_b, w_s, b_s, w_paT, w_pbT, w_o, rows_per_example,
              after=None, sender=None):
    rows = dx1.shape[0]
    nb_ex = rows // rows_per_example
    tm = min(TOKEN_TILE, rows_per_example)
    per_b = rows_per_example // tm

    def body(of_ref, ob_ref, *rest):
        pcols = rest[:7]
        (dx1_ref, mix_ref, mod_ref, gna_ref, lng_ref, lnb_ref, ws_ref, bs_ref, wpaT_ref, wpbT_ref, wo_ref,
         dpt_ref, do_ref, dmix_ref, dpa_ref, dpb_ref, dmod_ref, small_ref, dws_ref, dbs_ref) = rest[7:]
        i = pl.program_id(0)

        @pl.when(i == 0)
        def _():
            small_ref[...] = jnp.zeros_like(small_ref)
            dws_ref[...] = jnp.zeros_like(dws_ref)
            dbs_ref[...] = jnp.zeros_like(dbs_ref)

        @pl.when(i % per_b == 0)
        def _():
            dmod_ref[...] = jnp.zeros_like(dmod_ref)

        osum, og, u, v, ga, gb = _read_tail_inputs(of_ref, ob_ref, pcols)
        ln_g = lng_ref[...]
        t = _tail_forward(osum, og, u, v, ga, gb, gna_ref[...], ln_g, lnb_ref[...], ws_ref, bs_ref, wpaT_ref, wpbT_ref)
        dx1v = dx1_ref[...]
        dmod_ref[0, 2:3, :] += jnp.sum(dx1v * mix_ref[...].astype(F32), axis=0, keepdims=True)
        dmix = dx1v * mod_ref[0, 2:3, :]
        dmix_ref[...] = dmix.astype(dmix_ref.dtype)
        dmerged = _dot(dmix, wo_ref[...], "nt")
        sga, sgb = t["sga"], t["sgb"]
        dpa = dmerged * sga
        dpb = dmerged * sgb
        dpa_ref[...] = dpa.astype(dpa_ref.dtype)
        dpb_ref[...] = dpb.astype(dpb_ref.dtype)
        dga = dmerged * t["pa"] * sga * (1.0 - sga)
        dgb = dmerged * t["pb"] * sgb * (1.0 - sgb)
        doa = _dot(dpa, wpaT_ref[...])
        dobm = _dot(dpb, wpbT_ref[...])
        dug = dobm * t["mixed"]
        dmixed = dobm * t["ug"]
        du = dug * _gelu_grad(u, t["tu"])
        dvln_blocks = []
        for n in range(tm // SGU_BLOCK):
            rs = slice(n * SGU_BLOCK, (n + 1) * SGU_BLOCK)
            parts = []
            for g in range(GROUPS):
                gs = slice(g * DK, (g + 1) * DK)
                dm = dmixed[rs, gs]
                parts.append(_dot(ws_ref[g], dm, "tn"))
                dws_ref[g] += _dot(dm, t["vln"][rs, gs], "nt")
                dbs_ref[g] += jnp.sum(dm, axis=1, keepdims=True)
            dvln_blocks.append(jnp.concatenate(parts, axis=1))
        dvln = jnp.concatenate(dvln_blocks, axis=0) if len(dvln_blocks) > 1 else dvln_blocks[0]
        vhat = t["vhat"]
        small_ref[1:2, 0:KW] += jnp.sum(dvln * vhat, axis=0, keepdims=True)
        small_ref[2:3, 0:KW] += jnp.sum(dvln, axis=0, keepdims=True)
        dvhat = dvln * ln_g
        dvg = t["rstd"] * (dvhat - jnp.mean(dvhat, axis=-1, keepdims=True)
                           - vhat * jnp.mean(dvhat * vhat, axis=-1, keepdims=True))
        dv = dvg * _gelu_grad(v, t["tv"])
        sg_og = t["sg_og"]
        doan = doa * t["silu_og"]
        dog = doa * t["oan"] * (sg_og * (1.0 + og * (1.0 - sg_og)))
        prod = doan * t["on"]
        dgna = jnp.zeros((1, DK), F32)
        for h in range(HEADS):
            dgna = dgna + jnp.sum(prod[:, h * DK:(h + 1) * DK], axis=0, keepdims=True)
        small_ref[0:1, 0:DK] += dgna
        don = doan * t["gna4"]
        dot_parts = []
        for h in range(HEADS):
            hs = slice(h * DK, (h + 1) * DK)
            m = jnp.mean(don[:, hs] * t["on"][:, hs], axis=-1, keepdims=True)
            dot_parts.append(t["r"][:, hs] * (don[:, hs] - t["on"][:, hs] * m))
        do_ref[...] = jnp.concatenate(dot_parts, axis=1).astype(do_ref.dtype)
        for j, val in enumerate((dog, du, dv)):
            dpt_ref[:, j * KW:(j + 1) * KW] = val.astype(dpt_ref.dtype)
        dpt_ref[:, 3 * KW:3 * KW + D] = dga.astype(dpt_ref.dtype)
        dpt_ref[:, 3 * KW + D:] = dgb.astype(dpt_ref.dtype)

    row = lambda w: pl.BlockSpec((tm, w), lambda i: (i, 0))
    in_specs = [row(KW), row(KW)] + _tail_in_specs(tm) + [row(D), row(D), pl.BlockSpec((1, N_MOD, D), lambda i: (i // per_b, 0, 0))]
    in_specs += _tail_weight_specs()
    args = [o_up, o_down, *([p] * 7), dx1, mix, modv, gna, ln_g, ln_b, w_s, b_s, w_paT, w_pbT, w_o]
    cd = MXU_DTYPE
    res, handle = _host_call(
        body, "tail_bwd", (rows // tm,), in_specs, args,
        [jax.ShapeDtypeStruct((rows, TAIL_COLS), cd), jax.ShapeDtypeStruct((rows, KW), cd),
         jax.ShapeDtypeStruct((rows, D), cd), jax.ShapeDtypeStruct((rows, D), cd),
         jax.ShapeDtypeStruct((rows, D), cd), jax.ShapeDtypeStruct((nb_ex, 8, D), F32),
         jax.ShapeDtypeStruct((8, D), F32), jax.ShapeDtypeStruct((GROUPS, SGU_BLOCK, SGU_BLOCK), F32),
         jax.ShapeDtypeStruct((GROUPS, SGU_BLOCK, 1), F32)],
        [row(TAIL_COLS), row(KW), row(D), row(D), row(D),
         pl.BlockSpec((1, 8, D), lambda i: (i // per_b, 0, 0)), _full((8, D)),
         _full((GROUPS, SGU_BLOCK, SGU_BLOCK)), _full((GROUPS, SGU_BLOCK, 1))], [],
        after=after, sender=sender)
    return (*res, handle)


def _ffn(x1, target, modv, g_ffn, g_final, w_upT, w_down, rows_per_example):
    rows = x1.shape[0]
    nb_ex = rows // rows_per_example
    tm = min(TOKEN_TILE, rows_per_example)
    per_b = rows_per_example // tm
    n_ff = D_FF // FF_CHUNK

    def body(x1_ref, tgt_ref, mod_ref, gffn_ref, gfin_ref, wup_ref, wdn_ref,
             dx1_ref, h2_ref, dffn_ref, act_ref, dup_ref, dmod_ref, small_ref, up_scr):
        i = pl.program_id(0)

        @pl.when(i == 0)
        def _():
            small_ref[...] = jnp.zeros_like(small_ref)

        @pl.when(i % per_b == 0)
        def _():
            dmod_ref[...] = jnp.zeros_like(dmod_ref)

        x1v = x1_ref[...]
        g2 = gffn_ref[...]
        m3, m4, m5 = mod_ref[0, 3:4, :], mod_ref[0, 4:5, :], mod_ref[0, 5:6, :]
        r2 = lax.rsqrt(jnp.mean(x1v * x1v, axis=-1, keepdims=True) + EPS)
        xn2 = x1v * r2
        h2 = (xn2 * g2) * (1.0 + m4) + m3
        h2b = h2.astype(MXU_DTYPE)
        h2_ref[...] = h2b
        def up_pair(j):
            lo = j * FF_CHUNK
            return (_dot(h2b, wup_ref[lo:lo + FF_CHUNK, :], "nt"),
                    _dot(h2b, wup_ref[D_FF + lo:D_FF + lo + FF_CHUNK, :], "nt"))

        group_end = {min(e, n_ff): s for s, e in ((0, 4), (4, 8), (8, 12))}
        cur, ffn = up_pair(0), None
        for j in range(n_ff):
            nxt = up_pair(j + 1) if j + 1 < n_ff else None
            cs = slice(j * FF_CHUNK, (j + 1) * FF_CHUNK)
            a, bgate = cur
            up_scr[:, cs] = a
            up_scr[:, D_FF + j * FF_CHUNK:D_FF + (j + 1) * FF_CHUNK] = bgate
            act_ref[:, cs] = (a * _sigmoid(a) * bgate).astype(MXU_DTYPE)
            cur = nxt
            if j + 1 in group_end:
                gs = slice(group_end[j + 1] * FF_CHUNK, (j + 1) * FF_CHUNK)
                part = _dot(act_ref[:, gs], wdn_ref[gs, :])
                ffn = part if ffn is None else ffn + part
        x2 = x1v + m5 * ffn
        r3 = lax.rsqrt(jnp.mean(x2 * x2, axis=-1, keepdims=True) + EPS)
        xn3 = x2 * r3
        gf = gfin_ref[...]
        err = xn3 * gf - tgt_ref[...]
        loss = 0.5 * jnp.sum(jnp.mean(err * err, axis=-1, keepdims=True), axis=0, keepdims=True)
        small_ref[2:3, :] += jnp.broadcast_to(loss, (1, D))
        dy = err * (1.0 / D)
        small_ref[1:2, :] += jnp.sum(dy * xn3, axis=0, keepdims=True)
        dxn3 = dy * gf
        dx2 = r3 * (dxn3 - xn3 * jnp.mean(dxn3 * xn3, axis=-1, keepdims=True))
        dmod_ref[0, 5:6, :] += jnp.sum(dx2 * ffn, axis=0, keepdims=True)
        dffn = (dx2 * m5).astype(MXU_DTYPE)
        dffn_ref[...] = dffn
        dact_of = lambda j: _dot(dffn, wdn_ref[j * FF_CHUNK:(j + 1) * FF_CHUNK, :], "nt")
        cur, dh2 = dact_of(0), None
        for j in range(n_ff):
            nxt = dact_of(j + 1) if j + 1 < n_ff else None
            cs = slice(j * FF_CHUNK, (j + 1) * FF_CHUNK)
            a, bgate = up_scr[:, cs], up_scr[:, D_FF + j * FF_CHUNK:D_FF + (j + 1) * FF_CHUNK]
            s = _sigmoid(a)
            dup_ref[:, cs] = (cur * bgate * (s * (1.0 + a * (1.0 - s)))).astype(MXU_DTYPE)
            dup_ref[:, D_FF + j * FF_CHUNK:D_FF + (j + 1) * FF_CHUNK] = (cur * a * s).astype(MXU_DTYPE)
            cur = nxt
            if j + 1 in group_end:
                lo, hi = group_end[j + 1] * FF_CHUNK, (j + 1) * FF_CHUNK
                part = (_dot(dup_ref[:, lo:hi], wup_ref[lo:hi, :])
                        + _dot(dup_ref[:, D_FF + lo:D_FF + hi], wup_ref[D_FF + lo:D_FF + hi, :]))
                dh2 = part if dh2 is None else dh2 + part
        dmod_ref[0, 3:4, :] += jnp.sum(dh2, axis=0, keepdims=True)
        dmod_ref[0, 4:5, :] += jnp.sum(dh2 * xn2 * g2, axis=0, keepdims=True)
        small_ref[0:1, :] += jnp.sum(dh2 * (1.0 + m4) * xn2, axis=0, keepdims=True)
        dxn2 = dh2 * g2 * (1.0 + m4)
        dx1_ref[...] = dx2 + r2 * (dxn2 - xn2 * jnp.mean(dxn2 * xn2, axis=-1, keepdims=True))

    row = lambda w: pl.BlockSpec((tm, w), lambda i: (i, 0))
    cd = MXU_DTYPE
    return pl.pallas_call(
        body, name="ffn_fwd_bwd", grid=(rows // tm,),
        out_shape=(jax.ShapeDtypeStruct((rows, D), F32), jax.ShapeDtypeStruct((rows, D), cd),
                   jax.ShapeDtypeStruct((rows, D), cd), jax.ShapeDtypeStruct((rows, D_FF), cd),
                   jax.ShapeDtypeStruct((rows, 2 * D_FF), cd), jax.ShapeDtypeStruct((nb_ex, 8, D), F32),
                   jax.ShapeDtypeStruct((8, D), F32)),
        in_specs=[row(D), row(D), pl.BlockSpec((1, N_MOD, D), lambda i: (i // per_b, 0, 0)), _full((1, D)), _full((1, D)),
                  _full((2 * D_FF, D), single=True), _full((D_FF, D), single=True)],
        out_specs=(row(D), row(D), row(D), row(D_FF), row(2 * D_FF),
                   pl.BlockSpec((1, 8, D), lambda i: (i // per_b, 0, 0)), _full((8, D))),
        scratch_shapes=[pltpu.VMEM((tm, 2 * D_FF), F32)],
        compiler_params=_params(("arbitrary",)),
    )(x1, target, modv, g_ffn, g_final, w_upT, w_down)


def _scan_columns(up, down, n_groups):
    cols = [up[:, 0:KW].astype(F32), down[:, 0:KW].astype(F32)]
    for j in range(1, n_groups):
        cols.append(up[:, j * KW:(j + 1) * KW].astype(F32) + down[:, j * KW:(j + 1) * KW].astype(F32))
    return cols


def _inproj_bwd(d_up, d_down, dpt, xt, dx1, modv, g, w_inT, rows_per_example, name, sender=None):
    rows = xt.shape[0]
    latent = dx1 is not None
    n_cols = IN_COLS if latent else CTX_COLS
    n_groups = d_up.shape[1] // KW
    tm = min(TOKEN_TILE, rows_per_example)
    per_b = rows_per_example // tm
    n_mod_blocks = rows // rows_per_example if latent else 1

    def body(*refs):
        it = iter(refs)
        up_ref, down_ref = next(it), next(it)
        dpt_ref = next(it) if latent else None
        x_ref = next(it)
        dx1_ref = next(it) if latent else None
        mod_ref, g_ref, w_ref = next(it), next(it), next(it)
        gx_ref = next(it) if latent else None
        dp_out = None if latent else next(it)
        dmod_ref, small_ref = next(it), next(it)
        dp_ref = next(it) if latent else dp_out
        i = pl.program_id(0)

        @pl.when(i == 0)
        def _():
            small_ref[...] = jnp.zeros_like(small_ref)

        @pl.when((i % per_b == 0) if latent else (i == 0))
        def _():
            dmod_ref[...] = jnp.zeros_like(dmod_ref)

        for j, val in enumerate(_scan_columns(up_ref[...], down_ref[...], n_groups)):
            dp_ref[:, j * KW:(j + 1) * KW] = val.astype(MXU_DTYPE)
        if latent:
            dh = _dot(dp_ref[...], w_ref[0:4 * KW, :]) + _dot(dpt_ref[...], w_ref[4 * KW:, :])
        else:
            dh = _dot(dp_ref[...], w_ref[...])
        x = x_ref[...]
        gv = g_ref[...]
        m1 = mod_ref[0, 1:2, :]
        r = lax.rsqrt(jnp.mean(x * x, axis=-1, keepdims=True) + EPS)
        xn = x * r
        dmod_ref[0, 0:1, :] += jnp.sum(dh, axis=0, keepdims=True)
        dmod_ref[0, 1:2, :] += jnp.sum(dh * xn * gv, axis=0, keepdims=True)
        small_ref[0:1, :] += jnp.sum(dh * (1.0 + m1) * xn, axis=0, keepdims=True)
        if latent:
            dxn = dh * gv * (1.0 + m1)
            gx_ref[...] = dx1_ref[...] + r * (dxn - xn * jnp.mean(dxn * xn, axis=-1, keepdims=True))

    row = lambda w: pl.BlockSpec((tm, w), lambda i: (i, 0))
    mod_idx = (lambda i: (i // per_b, 0, 0)) if latent else (lambda i: (0, 0, 0))
    in_specs = [row(n_groups * KW)] * 2 + ([row(TAIL_COLS)] if latent else []) + [row(D)] + ([row(D)] if latent else [])
    in_specs += [pl.BlockSpec((1, N_MOD, D), mod_idx), _full((1, D)),
                 pl.BlockSpec((n_cols, D), lambda i: (0, 0), pipeline_mode=pl.Buffered(1))]
    args = [d_up, d_down] + ([dpt] if latent else []) + [xt] + ([dx1] if latent else []) + [modv, g, w_inT]
    first = jax.ShapeDtypeStruct((rows, D), F32) if latent else jax.ShapeDtypeStruct((rows, n_cols), MXU_DTYPE)
    out_shape = [first, jax.ShapeDtypeStruct((n_mod_blocks, 8, D), F32), jax.ShapeDtypeStruct((8, D), F32)]
    out_specs = [row(D) if latent else row(n_cols), pl.BlockSpec((1, 8, D), mod_idx), _full((8, D))]
    scratch = [pltpu.VMEM((tm, 4 * KW), MXU_DTYPE)] if latent else []
    res, handle = _host_call(body, name, (rows // tm,), in_specs, args, out_shape, out_specs, scratch, sender=sender)
    return (*res, handle)


def _grad_matmul(a, b, name, init=None, tn=512, sender=None):
    rows, n = a.shape
    k = b.shape[1]
    tn = min(tn, n)
    has_init = init is not None
    init_blocks = init.shape[0] // tn if has_init else 0

    def body(*refs):
        if has_init:
            a_ref, b_ref, init_ref, o_ref = refs
        else:
            a_ref, b_ref, o_ref = refs
        g = _dot(a_ref[...], b_ref[...], "tn")
        if has_init:
            g = g + jnp.where(pl.program_id(0) < init_blocks, init_ref[...].astype(F32), 0.0)
        o_ref[...] = g.astype(o_ref.dtype)

    in_specs = [pl.BlockSpec((rows, tn), lambda i: (0, i)), _full((rows, k), single=True)]
    args = [a, b]
    if has_init:
        in_specs.append(pl.BlockSpec((tn, k), lambda i: (jnp.minimum(i, init_blocks - 1), 0)))
        args.append(init)
    (out,), handle = _host_call(
        body, name, (n // tn,), in_specs, args, [jax.ShapeDtypeStruct((n, k), PAYLOAD_DTYPE)],
        [pl.BlockSpec((tn, k), lambda i: (i, 0))], [], sender=sender)
    return out, handle


def _grad_in(d_up, d_down, dpt, h, init, sender=None):
    rows = h.shape[0]
    tn = 256
    per_group = KW // tn
    n_scan = 4 * per_group
    init_blocks = init.shape[0] // tn

    def body(up_ref, down_ref, dpt_ref, h_ref, init_ref, o_ref):
        i = pl.program_id(0)
        both = (up_ref[...].astype(F32) + down_ref[...].astype(F32)).astype(MXU_DTYPE)
        a = jnp.where(i < per_group, up_ref[...],
                      jnp.where(i < 2 * per_group, down_ref[...], jnp.where(i < n_scan, both, dpt_ref[...])))
        g = _dot(a, h_ref[...], "tn") + jnp.where(i < init_blocks, init_ref[...].astype(F32), 0.0)
        o_ref[...] = g.astype(o_ref.dtype)

    last = 3 * per_group - 1
    col = lambda f: pl.BlockSpec((rows, tn), lambda i: (0, f(i)))
    in_specs = [col(lambda i: jnp.clip(jnp.where(i < per_group, i, i - per_group), 0, last)),
                col(lambda i: jnp.clip(i - per_group, 0, last)),
                col(lambda i: jnp.clip(i - n_scan, 0, TAIL_COLS // tn - 1)),
                _full((rows, D), single=True),
                pl.BlockSpec((tn, D), lambda i: (jnp.minimum(i, init_blocks - 1), 0))]
    (out,), handle = _host_call(
        body, "gw_in", (IN_COLS // tn,), in_specs, [d_up, d_down, dpt, h, init],
        [jax.ShapeDtypeStruct((IN_COLS, D), PAYLOAD_DTYPE)], [pl.BlockSpec((tn, D), lambda i: (i, 0))], [],
        sender=sender)
    return out, handle


def _row_tile(rows, limit=256):
    if rows <= limit:
        return rows
    for t in range(limit, 7, -8):
        if rows % t == 0:
            return t
    return rows


def _sum8(stack, name):
    _, rows, cols = stack.shape
    tr = _row_tile(rows)

    def body(s_ref, o_ref):
        acc = s_ref[0].astype(F32)
        for j in range(1, N_DEV):
            acc = acc + s_ref[j].astype(F32)
        o_ref[...] = acc

    return pl.pallas_call(
        body, name=name, grid=(rows // tr,), out_shape=jax.ShapeDtypeStruct((rows, cols), F32),
        in_specs=[pl.BlockSpec((N_DEV, tr, cols), lambda i: (0, i, 0))],
        out_specs=pl.BlockSpec((tr, cols), lambda i: (i, 0)),
        compiler_params=_params(("arbitrary",)),
    )(stack)


def _adamw_update(w, gv, m, v):
    nm = ADAM_B1 * m + (1.0 - ADAM_B1) * gv
    nv = ADAM_B2 * v + (1.0 - ADAM_B2) * (gv * gv)
    m_hat = nm / (1.0 - ADAM_B1 ** ADAM_STEP)
    v_hat = nv / (1.0 - ADAM_B2 ** ADAM_STEP)
    return -ADAM_LR * (m_hat / (jnp.sqrt(v_hat) + ADAM_EPS) + ADAM_WD * w), nm, nv


SMALL_PARAMS = (("g_mix", 0, D), ("g_ffn", 1, D), ("g_final", 2, D), ("g_norm_a", 3, DK), ("ln_v_g", 4, KW),
                ("ln_v_b", 5, KW), ("b_s", 6, GROUPS * SGU_BLOCK))


def _small_finish(early, late, dws, gam, nb_ex, params):
    names = [n for n, _, _ in SMALL_PARAMS] + ["b_mod", "w_s"]

    def body(*refs):
        s_ref, l_ref, dws_ref, gam_ref = refs[:4]
        p_refs = refs[4:4 + 3 * len(names)]
        tot_ref, dgam_ref = refs[4 + 3 * len(names):6 + 3 * len(names)]
        o_refs = refs[6 + 3 * len(names):]
        acc = s_ref[0] + l_ref[0]
        gws = dws_ref[0]
        for j in range(1, N_DEV):
            acc = acc + (s_ref[j] + l_ref[j])
            gws = gws + dws_ref[j]
        tot_ref[...] = acc
        bm = acc[8:8 + N_MOD, :]
        for e in range(nb_ex):
            bm = bm + acc[16 + e * N_MOD:16 + (e + 1) * N_MOD, :]
        lb = jnp.concatenate([_lower_bound(gam_ref, 0), _lower_bound(gam_ref, 1)], axis=1)
        dgam = acc[7:8, :] * lb * (1.0 - lb)
        dgam_ref[...] = jnp.concatenate([dgam, -dgam], axis=0)
        grads = [acc[row:row + 1, 0:width] for _, row, width in SMALL_PARAMS] + [bm, gws]
        for k, g in enumerate(grads):
            w_ref, m_ref, v_ref = p_refs[3 * k:3 * k + 3]
            o_refs[4 * k][...] = g
            o_refs[4 * k + 1][...], o_refs[4 * k + 2][...], o_refs[4 * k + 3][...] = _adamw_update(
                w_ref[...], g, m_ref[...], v_ref[...])

    p_args, p_specs, o_shapes, o_specs = [], [], [], []
    for n in names:
        for a in params[n]:
            p_args.append(a)
            p_specs.append(_full(a.shape))
        o_shapes += [jax.ShapeDtypeStruct(params[n][0].shape, F32)] * 4
        o_specs += [_full(params[n][0].shape)] * 4
    res = pl.pallas_call(
        body, name="small_finish", grid=(1,),
        out_shape=[jax.ShapeDtypeStruct((SMALL_ROWS, D), F32), jax.ShapeDtypeStruct((2, D), F32)] + o_shapes,
        in_specs=[_full(early.shape), _full(late.shape), _full(dws.shape), _full((4, KW))] + p_specs,
        out_specs=[_full((SMALL_ROWS, D)), _full((2, D))] + o_specs,
        compiler_params=_params(("arbitrary",)),
    )(early, late, dws, gam, *p_args)
    return res[0], res[1], {n: res[2 + 4 * k:6 + 4 * k] for k, n in enumerate(names)}


def _adamw_sum8(stack, w, m, v, name):
    _, rows, cols = stack.shape
    tr = _row_tile(rows)

    def body(s_ref, w_ref, m_ref, v_ref, g_ref, d_ref, nm_ref, nv_ref):
        gv = s_ref[0].astype(F32)
        for j in range(1, N_DEV):
            gv = gv + s_ref[j].astype(F32)
        g_ref[...] = gv
        d_ref[...], nm_ref[...], nv_ref[...] = _adamw_update(w_ref[...], gv, m_ref[...], v_ref[...])

    blk = pl.BlockSpec((tr, cols), lambda i: (i, 0))
    sd = jax.ShapeDtypeStruct((rows, cols), F32)
    return pl.pallas_call(
        body, name=name, grid=(rows // tr,), out_shape=(sd, sd, sd, sd),
        in_specs=[pl.BlockSpec((N_DEV, tr, cols), lambda i: (0, i, 0)), blk, blk, blk], out_specs=(blk, blk, blk, blk),
        compiler_params=_params(("arbitrary",)),
    )(stack, w, m, v)


def _adamw(w, g, m, v, name):
    shape = w.shape
    cols = shape[-1]
    rows = 1
    for s in shape[:-1]:
        rows *= s
    tr = _row_tile(rows)

    def body(w_ref, g_ref, m_ref, v_ref, d_ref, nm_ref, nv_ref):
        gv = g_ref[...]
        nm = ADAM_B1 * m_ref[...] + (1.0 - ADAM_B1) * gv
        nv = ADAM_B2 * v_ref[...] + (1.0 - ADAM_B2) * (gv * gv)
        m_hat = nm / (1.0 - ADAM_B1 ** ADAM_STEP)
        v_hat = nv / (1.0 - ADAM_B2 ** ADAM_STEP)
        d_ref[...] = -ADAM_LR * (m_hat / (jnp.sqrt(v_hat) + ADAM_EPS) + ADAM_WD * w_ref[...])
        nm_ref[...] = nm
        nv_ref[...] = nv

    blk = pl.BlockSpec((tr, cols), lambda i: (i, 0))
    sd = jax.ShapeDtypeStruct((rows, cols), F32)
    d, nm, nv = pl.pallas_call(
        body, name=name, grid=(rows // tr,), out_shape=(sd, sd, sd), in_specs=[blk] * 4, out_specs=(blk, blk, blk),
        compiler_params=_params(("arbitrary",)),
    )(w.reshape(rows, cols), g.reshape(rows, cols), m.reshape(rows, cols), v.reshape(rows, cols))
    return d.reshape(shape), nm.reshape(shape), nv.reshape(shape)


def _owner_blocks(a):
    return a.reshape(N_DEV, a.shape[0] // N_DEV, a.shape[1])


class _LocalWeights:
    def __init__(self, w_upT, w_down, w_o, w_paT, w_pbT):
        self.weights = (w_upT, w_down, w_o, w_paT, w_pbT)
        self.items = {}

    def sender(self, stage, items=None):
        self.items[stage] = items
        return None

    def sent(self, stage, handle):
        pass

    def mixer_weights(self, after):
        return self.weights[1:]

    def ffn_weights(self, after):
        return self.weights[0]


def _local_step(x, ctx, target, modv, mcv, gam, g_mix, g_ffn, gna, ln_g, ln_b, w_s, b_s, g_final, w_inT, comm):
    nb_ex, seq, _ = x.shape
    ctx_len = ctx.shape[1]
    xt = x.reshape(nb_ex * seq, D)
    ct = ctx.reshape(nb_ex * ctx_len, D)
    tgt = target.reshape(nb_ex * seq, D)
    bs3 = b_s.reshape(GROUPS, SGU_BLOCK, 1)

    pc, hc, _ = _inproj(ct, mcv, g_mix, w_inT, CTX_COLS, ctx_len, "inproj_ctx")
    p, h, handle = _inproj(xt, modv, g_mix, w_inT, IN_COLS, seq, "inproj_lat", sender=comm.sender("inproj"))
    comm.sent("inproj", handle)
    cst_f, cst_b, s_ctx, _ = _hgrn_fwd(pc, gam, None, ctx_len, False, "hgrn_fwd_ctx")
    o_up, o_down, st_f, st_b, _, handle = _hgrn_fwd(p, gam, s_ctx, seq, True, "hgrn_fwd_lat",
                                                    sender=comm.sender("scan"))
    comm.sent("scan", handle)
    w_down, w_o, w_paT, w_pbT = comm.mixer_weights(o_up)
    x1, mix, merged, oa, obm = _tail_fwd(p, o_up, o_down, xt, modv, gna, ln_g, ln_b, w_s, bs3, w_paT, w_pbT, w_o, seq)
    w_upT = comm.ffn_weights(x1)
    dx1, h2, dffn, act, dup, dmod_ffn, small_ffn = _ffn(x1, tgt, modv, g_ffn, g_final, w_upT, w_down, seq)
    gw_upT, _ = _grad_matmul(dup, h2, "gw_up")
    gw_down, _ = _grad_matmul(act, dffn, "gw_down", tn=256)
    scatter = lambda *grads: [(_owner_blocks(g), "scatter") for g in grads]
    dpt, do, dmix, dpa, dpb, dmod_tail, small_tail, dws, dbs, handle = _tail_bwd(
        p, o_up, o_down, dx1, mix, modv, gna, ln_g, ln_b, w_s, bs3, w_paT, w_pbT, w_o, seq,
        sender=comm.sender("tail_bwd", scatter(gw_upT)))
    comm.sent("tail_bwd", handle)
    gw_o, _ = _grad_matmul(merged, dmix, "gw_o")
    gw_paT, _ = _grad_matmul(dpa, oa, "gw_pa")
    gw_pbT, _ = _grad_matmul(dpb, obm, "gw_pb")
    def at_row(row, a):
        return jnp.pad(a, ((row, SMALL_ROWS - row - a.shape[0]), (0, D - a.shape[1])))

    small_early = (at_row(1, small_ffn[0:2])
                   + at_row(3, small_tail[0:3])
                   + at_row(6, dbs.reshape(1, GROUPS * SGU_BLOCK))
                   + at_row(14, small_ffn[2:3]))
    dws_rows = dws.reshape(GROUPS * SGU_BLOCK, SGU_BLOCK)
    d_up, d_down, dlb, ds0, handle = _hgrn_bwd(
        p, gam, do, st_f, st_b, None, seq, True, "hgrn_bwd_lat",
        sender=comm.sender("scan_bwd", scatter(gw_down, gw_o, gw_paT, gw_pbT)
                           + [(small_early, "gather"), (dws_rows, "gather")]))
    comm.sent("scan_bwd", handle)
    c_up, c_down, dlb_c, _, _ = _hgrn_bwd(pc, gam, None, cst_f, cst_b, ds0, ctx_len, False, "hgrn_bwd_ctx")
    dpc, dmc, small_c, _ = _inproj_bwd(c_up, c_down, None, ct, None, mcv, g_mix, w_inT, ctx_len, "inproj_bwd_ctx")
    gw_inT, _ = _grad_in(d_up, d_down, dpt, h, _grad_matmul(dpc, hc, "gw_in_ctx")[0])
    grad_x, dmod_in, small_in, handle = _inproj_bwd(d_up, d_down, dpt, xt, dx1, modv, g_mix, w_inT, seq,
                                                   "inproj_bwd_lat", sender=comm.sender("inproj_bwd", scatter(gw_inT)))
    comm.sent("inproj_bwd", handle)
    dmod = dmod_in + dmod_tail + dmod_ffn
    small_late = (at_row(0, small_in[0:1] + small_c[0:1])
                  + at_row(7, (dlb + dlb_c).reshape(1, 2 * KW))
                  + at_row(8, dmc[0, 0:N_MOD])
                  + at_row(16, dmod[:, 0:N_MOD].reshape(nb_ex * N_MOD, D)))
    comm.sender("last", [(small_late, "gather")])
    return grad_x.reshape(x.shape)


def kernel(x, c, ctx, c_ctx, w_mod, b_mod, g_mix, g_ffn, w_in, lb_gamma, g_norm_a, ln_v_g, ln_v_b, w_s, b_s, w_pa, w_pb, w_o, w_up, w_down, g_final, loss_target, m_c_ctx, m_w_mod, m_b_mod, m_g_mix, m_g_ffn, m_w_in, m_lb_gamma, m_g_norm_a, m_ln_v_g, m_ln_v_b, m_w_s, m_b_s, m_w_pa, m_w_pb, m_w_o, m_w_up, m_w_down, m_g_final, v_c_ctx, v_w_mod, v_b_mod, v_g_mix, v_g_ffn, v_w_in, v_lb_gamma, v_g_norm_a, v_ln_v_g, v_ln_v_b, v_w_s, v_b_s, v_w_pa, v_w_pb, v_w_o, v_w_up, v_w_down, v_g_final):
    nb_ex = x.shape[0]
    me = 4 * lax.axis_index("x") + 2 * lax.axis_index("y") + lax.axis_index("c")
    cd = MXU_DTYPE
    mod_cols = w_mod.shape[2]
    lb_cols = lb_gamma.shape[2]

    w_inT_l = w_in[0].T.astype(cd)
    w_upT_l = w_up[0].T.astype(cd)
    w_paT_l = w_pa[0].T.astype(cd)
    w_pbT_l = w_pb[0].T.astype(cd)
    cl = jnp.concatenate([c, jnp.pad(lb_gamma.reshape(1, 4 * lb_cols), ((0, 0), (0, D - 4 * lb_cols))),
                          jnp.zeros((8 - nb_ex - 1, D), F32)], axis=0)
    g_in, g_cl = _gather_two_level([w_inT_l, cl], "gather_w_in")
    w_inT = g_in.reshape(IN_COLS, D)
    c_all = g_cl[:, 0:nb_ex].reshape(N_DEV * nb_ex, D)
    gam = jnp.transpose(g_cl[:, nb_ex, 0:4 * lb_cols].reshape(N_DEV, 4, lb_cols), (1, 0, 2)).reshape(4, KW)

    n_c = N_DEV * nb_ex
    cvec = jnp.concatenate([c_all, c_ctx.reshape(1, D), jnp.zeros((7, D), F32)], axis=0)
    b_mod_l = lax.dynamic_slice(b_mod, (0, me * mod_cols), (1, mod_cols))
    mod_l, svec = _mod_fwd(cvec, w_mod[0], b_mod_l)
    (g_mod,) = _exchange([(mod_l, "gather")], "gather_mod")
    mod_all = jnp.transpose(g_mod, (1, 0, 2)).reshape(n_c + 8, N_MOD * D)
    modv = lax.dynamic_slice(mod_all, (me * nb_ex, 0), (nb_ex, N_MOD * D)).reshape(nb_ex, N_MOD, D)
    mcv = mod_all[n_c].reshape(1, N_MOD, D)

    handles, leftover = {}, {}

    class Comm:
        def sender(self, stage, items=None):
            if stage == "inproj":
                return _Sender([(w_down[0].astype(cd), "gather"), (w_o[0].astype(cd), "gather"), (w_paT_l, "gather"),
                                (w_pbT_l, "gather")])
            if stage == "scan":
                return _Sender([(w_upT_l, "gather")])
            if stage == "last":
                leftover["items"] = items
                return None
            return _Sender(items)

        def sent(self, stage, handle):
            handles[stage] = handle

        def mixer_weights(self, after):
            g_down, g_o, g_pa, g_pb = _exchange_wait(handles["inproj"], after)
            return g_down.reshape(D_FF, D), g_o.reshape(D, D), g_pa.reshape(D, KW), g_pb.reshape(D, KW)

        def ffn_weights(self, after):
            (g_up,) = _exchange_wait(handles["scan"], after)
            return g_up.reshape(2 * D_FF, D)

    grad_x = _local_step(
        x, ctx, loss_target, modv, mcv, gam, g_mix, g_ffn, g_norm_a, ln_v_g, ln_v_b, w_s[0], b_s[0],
        g_final.reshape(1, D), w_inT, Comm())
    last, last_started = _exchange_start(leftover["items"], "gather_small_late", after=leftover["items"][0][0])

    (r_up,) = _exchange_wait(handles["tail_bwd"], last_started)
    r_down, r_o, r_pa, r_pb, r_small, r_dws = _exchange_wait(handles["scan_bwd"], r_up)
    raw_up = _adamw_sum8(r_up, w_up[0].T, m_w_up[0].T, v_w_up[0].T, "adamw_w_up")
    raw_down = _adamw_sum8(r_down, w_down[0], m_w_down[0], v_w_down[0], "adamw_w_down")
    raw_o = _adamw_sum8(r_o, w_o[0], m_w_o[0], v_w_o[0], "adamw_w_o")
    (r_in,) = _exchange_wait(handles["inproj_bwd"], raw_up[1])
    raw_in = _adamw_sum8(r_in, w_in[0].T, m_w_in[0].T, v_w_in[0].T, "adamw_w_in")
    (r_late,) = _exchange_wait(last, raw_in[1])
    done = {"w_in": [a.T[None] for a in raw_in], "w_up": [a.T[None] for a in raw_up],
            "w_down": [a[None] for a in raw_down], "w_o": [a[None] for a in raw_o]}
    grad_w_in, grad_w_up, grad_w_down, grad_w_o = (done[k][0] for k in ("w_in", "w_up", "w_down", "w_o"))
    grad_w_pa = _sum8(r_pa, "sum_w_pa").T[None]
    grad_w_pb = _sum8(r_pb, "sum_w_pb").T[None]
    as_2d = {"g_final": (1, D), "b_s": (1, GROUPS * SGU_BLOCK), "b_mod": (N_MOD, D), "w_s": (GROUPS * SGU_BLOCK, SGU_BLOCK)}
    small_params = {"g_mix": (g_mix, m_g_mix, v_g_mix), "g_ffn": (g_ffn, m_g_ffn, v_g_ffn),
                    "g_final": (g_final, m_g_final, v_g_final), "g_norm_a": (g_norm_a, m_g_norm_a, v_g_norm_a),
                    "ln_v_g": (ln_v_g, m_ln_v_g, v_ln_v_g), "ln_v_b": (ln_v_b, m_ln_v_b, v_ln_v_b),
                    "b_s": (b_s, m_b_s, v_b_s), "b_mod": (b_mod, m_b_mod, v_b_mod), "w_s": (w_s, m_w_s, v_w_s)}
    tot, dgam, small_done = _small_finish(
        r_small, r_late, r_dws, gam, nb_ex,
        {n: tuple(a.reshape(as_2d.get(n, a.shape)) for a in wmv) for n, wmv in small_params.items()})
    for n, outs in small_done.items():
        done[n] = [a.reshape(small_params[n][0].shape) for a in outs]
    loss = tot[14, 0]
    grad_g_mix, grad_g_ffn, grad_g_final, grad_g_norm_a, grad_ln_v_g, grad_ln_v_b, grad_b_s, grad_b_mod, grad_w_s = (
        done[n][0] for n in ("g_mix", "g_ffn", "g_final", "g_norm_a", "ln_v_g", "ln_v_b", "b_s", "b_mod", "w_s"))
    grad_lb_gamma = lax.dynamic_slice(dgam.reshape(2, 2, KW), (0, 0, me * lb_cols), (2, 2, lb_cols))

    dmod_all = r_late[:, 16:16 + nb_ex * N_MOD].reshape(n_c, N_MOD * D)
    dmod_l = jnp.concatenate([lax.dynamic_slice(dmod_all, (0, me * mod_cols), (n_c, mod_cols)),
                              lax.dynamic_slice(tot[8:8 + N_MOD].reshape(1, N_MOD * D), (0, me * mod_cols), (1, mod_cols)),
                              jnp.zeros((7, mod_cols), F32)], axis=0)
    gw_mod, gc = _mod_bwd(svec, cvec, dmod_l, w_mod[0])
    grad_w_mod = gw_mod[None]
    (r_gc,) = _exchange([(gc[n_c:n_c + 8], "gather")], "gather_c_ctx", after=r_late)
    grad_c_ctx = _sum8(r_gc, "sum_c_ctx")[0]

    names = ["c_ctx", "w_mod", "b_mod", "g_mix", "g_ffn", "w_in", "lb_gamma", "g_norm_a", "ln_v_g", "ln_v_b", "w_s",
             "b_s", "w_pa", "w_pb", "w_o", "w_up", "w_down", "g_final"]
    weights = [c_ctx, w_mod, b_mod, g_mix, g_ffn, w_in, lb_gamma, g_norm_a, ln_v_g, ln_v_b, w_s, b_s, w_pa, w_pb, w_o,
               w_up, w_down, g_final]
    grads = [grad_c_ctx, grad_w_mod, grad_b_mod, grad_g_mix, grad_g_ffn, grad_w_in, grad_lb_gamma, grad_g_norm_a,
             grad_ln_v_g, grad_ln_v_b, grad_w_s, grad_b_s, grad_w_pa, grad_w_pb, grad_w_o, grad_w_up, grad_w_down,
             grad_g_final]
    ms = [m_c_ctx, m_w_mod, m_b_mod, m_g_mix, m_g_ffn, m_w_in, m_lb_gamma, m_g_norm_a, m_ln_v_g, m_ln_v_b, m_w_s, m_b_s,
          m_w_pa, m_w_pb, m_w_o, m_w_up, m_w_down, m_g_final]
    vs = [v_c_ctx, v_w_mod, v_b_mod, v_g_mix, v_g_ffn, v_w_in, v_lb_gamma, v_g_norm_a, v_ln_v_g, v_ln_v_b, v_w_s, v_b_s,
          v_w_pa, v_w_pb, v_w_o, v_w_up, v_w_down, v_g_final]
    deltas, new_ms, new_vs = [], [], []
    for nm, w, g, m, v in zip(names, weights, grads, ms, vs):
        d, nm_, nv_ = done[nm][1:] if nm in done else _adamw(w, g.reshape(w.shape), m, v, "adamw_" + nm)
        deltas.append(d)
        new_ms.append(nm_)
        new_vs.append(nv_)
    grads = [g.reshape(w.shape) for g, w in zip(grads, weights)]
    return (loss, grad_x, *grads, *deltas, *new_ms, *new_vs)
```

```python
import functools

import jax
import jax.numpy as jnp
from jax import lax
from jax.experimental import pallas as pl
from jax.experimental.pallas import tpu as pltpu

F32 = jnp.float32
MXU_DTYPE = jnp.bfloat16
PAYLOAD_DTYPE = jnp.bfloat16

N_DEV = 8
D = 1024
HEADS = 4
DK = 128
KW = HEADS * DK
CHUNK = 64
SGU_BLOCK = 128
GROUPS = 4
D_FF = 2816
FF_CHUNK = 256
N_MOD = 6
IN_COLS = 5632
CTX_COLS = 1536
TAIL_COLS = IN_COLS - 4 * KW
EPS = 1e-6
ADAM_LR, ADAM_B1, ADAM_B2, ADAM_EPS, ADAM_WD, ADAM_STEP = 0.001, 0.9, 0.999, 1e-08, 0.01, 10

VMEM_LIMIT = 56 * 1024 * 1024
TOKEN_TILE = 256
SMALL_ROWS = 40


def _params(sem):
    return pltpu.CompilerParams(dimension_semantics=sem, vmem_limit_bytes=VMEM_LIMIT)


_DN = {"nn": (((1,), (0,)), ((), ())), "nt": (((1,), (1,)), ((), ())), "tn": (((0,), (0,)), ((), ()))}


def _dot(a, b, form="nn"):
    return lax.dot_general(a.astype(MXU_DTYPE), b.astype(MXU_DTYPE), _DN[form], preferred_element_type=F32)


def _mask_dot(mask, v):
    bf = jnp.bfloat16
    hi = v.astype(bf)
    r1 = v - hi.astype(F32)
    mid = r1.astype(bf)
    lo = (r1 - mid.astype(F32)).astype(bf)
    w = v.shape[1]
    s = lax.dot_general(mask.astype(bf), jnp.concatenate([hi, mid, lo], axis=1), _DN["nn"], preferred_element_type=F32)
    return (s[:, 2 * w:] + s[:, w:2 * w]) + s[:, :w]


def _full(shape, single=False):
    n = len(shape)
    if single:
        return pl.BlockSpec(shape, lambda *_: (0,) * n, pipeline_mode=pl.Buffered(1))
    return pl.BlockSpec(shape, lambda *_: (0,) * n)


def _ordered_behind(body, in_specs, args, after):
    if after is None:
        return body
    at = len(in_specs)
    in_specs.append(pl.BlockSpec(memory_space=pl.ANY))
    args.append(after)
    return lambda *refs: body(*refs[:at], *refs[at + 1:])


def _sigmoid(z):
    return 0.5 * jnp.tanh(0.5 * z) + 0.5


def _gelu(x):
    c = 0.7978845608028654
    t = jnp.tanh(c * (x + 0.044715 * x * x * x))
    return 0.5 * x * (1.0 + t), t


def _gelu_grad(x, t):
    c = 0.7978845608028654
    return 0.5 * (1.0 + t) + 0.5 * x * (1.0 - t * t) * c * (1.0 + 3 * 0.044715 * x * x)


def _exchange(items, name, after=None):
    n = len(items)
    out_shape = []
    for a, mode in items:
        blk = a.shape if mode == "gather" else a.shape[1:]
        out_shape.append(jax.ShapeDtypeStruct((N_DEV,) + tuple(blk), a.dtype))

    def body(*refs):
        srcs, dsts = refs[:n], refs[n:2 * n]
        send_sems, recv_sems, local_sems = refs[2 * n:]
        x, y, c = lax.axis_index("x"), lax.axis_index("y"), lax.axis_index("c")
        me = 4 * x + 2 * y + c

        def src_for(i, dev):
            return srcs[i] if items[i][1] == "gather" else srcs[i].at[dev]

        local = [pltpu.make_async_copy(src_for(i, me), dsts[i].at[me], local_sems.at[i]) for i in range(n)]
        for cp in local:
            cp.start()
        remote = []
        for k in range(1, N_DEV):
            px = jnp.bitwise_xor(x, (k >> 2) & 1)
            py = jnp.bitwise_xor(y, (k >> 1) & 1)
            pc = jnp.bitwise_xor(c, k & 1)
            peer = 4 * px + 2 * py + pc
            for i in range(n):
                cp = pltpu.make_async_remote_copy(
                    src_ref=src_for(i, peer), dst_ref=dsts[i].at[me],
                    send_sem=send_sems.at[i * (N_DEV - 1) + k - 1], recv_sem=recv_sems.at[i * (N_DEV - 1) + k - 1],
                    device_id=(px, py, pc), device_id_type=pl.DeviceIdType.MESH)
                cp.start()
                remote.append(cp)
        for cp in remote:
            cp.wait()
        for cp in local:
            cp.wait()

    any_spec = pl.BlockSpec(memory_space=pl.ANY)
    in_specs, args = [any_spec] * n, [a for a, _ in items]
    if after is not None:
        in_specs.append(any_spec)
        args.append(after)
        exchange = body
        body = lambda *refs: exchange(*refs[:n], *refs[n + 1:])
    return pl.pallas_call(
        body, name=name, out_shape=out_shape, in_specs=in_specs, out_specs=[any_spec] * n,
        scratch_shapes=[pltpu.SemaphoreType.DMA((n * (N_DEV - 1),)), pltpu.SemaphoreType.DMA((n * (N_DEV - 1),)),
                        pltpu.SemaphoreType.DMA((n,))],
    )(*args)


def _gather_two_level(arrays, name):
    n = len(arrays)

    def body(*refs):
        srcs, dsts = refs[:n], refs[n:2 * n]
        send_sems, recv_sems, local_sems = refs[2 * n:]
        x, y, c = lax.axis_index("x"), lax.axis_index("y"), lax.axis_index("c")
        me, sibling = (x, y, c), (x, y, 1 - c)
        x_nbr, y_nbr, diag = (1 - x, y, c), (x, 1 - y, c), (1 - x, 1 - y, c)

        def slot(px, py, pc):
            return 4 * px + 2 * py + pc

        def copy(i, k, block, to, src=None):
            return pltpu.make_async_remote_copy(
                src_ref=dsts[i].at[slot(*block)] if src is None else src, dst_ref=dsts[i].at[slot(*block)],
                send_sem=send_sems.at[i * 7 + k], recv_sem=recv_sems.at[i * 7 + k],
                device_id=to, device_id_type=pl.DeviceIdType.MESH)

        mine = [pltpu.make_async_copy(srcs[i], dsts[i].at[slot(*me)], local_sems.at[i]) for i in range(n)]
        for cp in mine:
            cp.start()
        first = []
        for k, to in ((1, x_nbr), (2, y_nbr), (0, sibling)):
            first += [copy(i, k, me, to, src=srcs[i]) for i in range(n)]
        for cp in first:
            cp.start()

        def relay_then_pass(k_from, frm, to, k_other, other):
            for i in range(n):
                copy(i, k_from, frm, me).wait_recv()
                copy(i, 3, frm, to).start()
                copy(i, 3 + k_from, frm, sibling).start()
            for i in range(n):
                copy(i, k_other, other, me).wait_recv()
                copy(i, 3 + k_other, other, sibling).start()

        @pl.when(c == 1)
        def _():
            relay_then_pass(1, x_nbr, y_nbr, 2, y_nbr)

        @pl.when(c == 0)
        def _():
            relay_then_pass(2, y_nbr, x_nbr, 1, x_nbr)

        for i in range(n):
            copy(i, 3, diag, me).wait_recv()
            copy(i, 6, diag, sibling).start()
        for i in range(n):
            copy(i, 0, sibling, me).wait_recv()
            for k, chip in ((4, x_nbr), (5, y_nbr), (6, diag)):
                copy(i, k, (chip[0], chip[1], 1 - c), me).wait_recv()
        for i in range(n):
            for k in range(7):
                copy(i, k, me, me, src=srcs[i]).wait_send()
        for cp in mine:
            cp.wait()

    any_spec = pl.BlockSpec(memory_space=pl.ANY)
    return pl.pallas_call(
        body, name=name, out_shape=[jax.ShapeDtypeStruct((N_DEV,) + a.shape, a.dtype) for a in arrays],
        in_specs=[any_spec] * n, out_specs=[any_spec] * n,
        scratch_shapes=[pltpu.SemaphoreType.DMA((n * 7,)), pltpu.SemaphoreType.DMA((n * 7,)),
                        pltpu.SemaphoreType.DMA((n,))],
    )(*arrays)


_HBM = pl.BlockSpec(memory_space=pltpu.HBM)
_SEM = pl.BlockSpec(memory_space=pltpu.SEMAPHORE)
_EFFECT = pltpu.SideEffectType.DATAFLOW_SIDE_EFFECTING


def _split_copies(items, srcs, lands, send_sems, recv_sems):
    x, y, c = lax.axis_index("x"), lax.axis_index("y"), lax.axis_index("c")
    me = 4 * x + 2 * y + c
    copies = []
    for k in range(1, N_DEV):
        px = jnp.bitwise_xor(x, (k >> 2) & 1)
        py = jnp.bitwise_xor(y, (k >> 1) & 1)
        pc = jnp.bitwise_xor(c, k & 1)
        peer = 4 * px + 2 * py + pc
        for i in range(len(items)):
            src = srcs[i] if items[i][1] == "gather" else srcs[i].at[peer]
            copies.append(pltpu.make_async_remote_copy(
                src_ref=src, dst_ref=lands[i].at[me],
                send_sem=send_sems.at[i * (N_DEV - 1) + k - 1], recv_sem=recv_sems.at[i * (N_DEV - 1) + k - 1],
                device_id=(px, py, pc), device_id_type=pl.DeviceIdType.MESH))
    return me, copies


def _exchange_start(items, name, after):
    n = len(items)
    n_sem = n * (N_DEV - 1)
    srcs, lands = [], []
    for a, mode in items:
        blk = a.shape if mode == "gather" else a.shape[1:]
        srcs.append(pltpu.with_memory_space_constraint(a, pltpu.HBM))
        lands.append(pltpu.with_memory_space_constraint(lax.empty((N_DEV,) + tuple(blk), a.dtype), pltpu.HBM))

    def body(*refs):
        src_refs, land_refs = refs[:n], refs[n:2 * n]
        send_sems, recv_sems = refs[2 * n + 1], refs[2 * n + 2]
        local_sems = refs[4 * n + 3]
        me, copies = _split_copies(items, src_refs, land_refs, send_sems, recv_sems)
        for i in range(n):
            own = src_refs[i] if items[i][1] == "gather" else src_refs[i].at[me]
            cp = pltpu.make_async_copy(own, land_refs[i].at[me], local_sems.at[i])
            cp.start()
            cp.wait()
        for cp in copies:
            cp.start()

    out_shape = [pltpu.SemaphoreType.DMA((n_sem,)), pltpu.SemaphoreType.DMA((n_sem,))]
    out_shape += [pltpu.HBM(a.shape, a.dtype) for a in srcs] + [pltpu.HBM(a.shape, a.dtype) for a in lands]
    outs = pl.pallas_call(
        body, name=name, out_shape=out_shape,
        in_specs=[_HBM] * (2 * n) + [pl.BlockSpec(memory_space=pl.ANY)],
        out_specs=[_SEM, _SEM] + [_HBM] * (2 * n),
        input_output_aliases={i: 2 + i for i in range(2 * n)},
        scratch_shapes=[pltpu.SemaphoreType.DMA((n,))],
        compiler_params=pltpu.CompilerParams(has_side_effects=_EFFECT),
    )(*srcs, *lands, after)
    handle = (items, name, outs[0], outs[1], outs[2:2 + n], outs[2 + n:2 + 2 * n])
    return handle, outs[2]


class _Sender:
    PIECE_ROWS = 352

    def __init__(self, items, chunks=None):
        self.items, self.n = items, len(items)
        self.chunks = chunks
        if chunks is None:
            block_rows = [a.shape[0] if mode == "gather" else a.shape[1] for a, mode in items]
            self.chunks = [r // self.PIECE_ROWS if r % self.PIECE_ROWS == 0 else 1 for r in block_rows]
        self.srcs, self.lands = [], []
        for a, mode in items:
            blk = a.shape if mode == "gather" else a.shape[1:]
            self.srcs.append(pltpu.with_memory_space_constraint(a, pltpu.HBM))
            self.lands.append(pltpu.with_memory_space_constraint(lax.empty((N_DEV,) + tuple(blk), a.dtype), pltpu.HBM))

    def issue(self, src_refs, land_refs, send_sems, recv_sems, local_sems, step, n_steps):
        x, y, c = lax.axis_index("x"), lax.axis_index("y"), lax.axis_index("c")
        me = 4 * x + 2 * y + c
        copies = []
        for ch in range(max(self.chunks)):
            for k in range(1, N_DEV):
                px = jnp.bitwise_xor(x, (k >> 2) & 1)
                py = jnp.bitwise_xor(y, (k >> 1) & 1)
                pc = jnp.bitwise_xor(c, k & 1)
                peer = 4 * px + 2 * py + pc
                for i, (_, mode) in enumerate(self.items):
                    if ch >= self.chunks[i]:
                        continue
                    n_rows = land_refs[i].shape[1] // self.chunks[i]
                    rows = pl.ds(ch * n_rows, n_rows)
                    src = src_refs[i].at[rows] if mode == "gather" else src_refs[i].at[peer].at[rows]
                    copies.append(pltpu.make_async_remote_copy(
                        src_ref=src, dst_ref=land_refs[i].at[me].at[rows],
                        send_sem=send_sems.at[i * (N_DEV - 1) + k - 1], recv_sem=recv_sems.at[i * (N_DEV - 1) + k - 1],
                        device_id=(px, py, pc), device_id_type=pl.DeviceIdType.MESH))
        own = [pltpu.make_async_copy(src_refs[i] if mode == "gather" else src_refs[i].at[me], land_refs[i].at[me],
                                     local_sems.at[i]) for i, (_, mode) in enumerate(self.items)]

        @pl.when(step == 0)
        def _():
            for cp in own:
                cp.start()

        for s in range(n_steps):
            group = [cp for j, cp in enumerate(copies) if (j * n_steps) // len(copies) == s]
            if group:
                @pl.when(step == s)
                def _(group=group):
                    for cp in group:
                        cp.start()

        @pl.when(step == n_steps - 1)
        def _():
            for cp in own:
                cp.wait()


def _host_call(body, name, grid, in_specs, args, out_shape, out_specs, scratch_shapes, after=None, sender=None):
    in_specs, args, out_shape, out_specs = list(in_specs), list(args), list(out_shape), list(out_specs)
    scratch_shapes = list(scratch_shapes)
    semantics = ("arbitrary",) * len(grid)
    body = _ordered_behind(body, in_specs, args, after)
    if sender is None:
        res = pl.pallas_call(body, name=name, grid=grid, in_specs=in_specs, out_specs=out_specs, out_shape=out_shape,
                             scratch_shapes=scratch_shapes, compiler_params=_params(semantics))(*args)
        return res, None
    n, n_in, n_out, n_scr = sender.n, len(in_specs), len(out_shape), len(scratch_shapes)
    n_sem = n * (N_DEV - 1)
    n_steps = 1
    for g in grid:
        n_steps *= g
    compute = body

    def body(*refs):
        ins, s_in = refs[:n_in], refs[n_in:n_in + 2 * n]
        o0 = n_in + 2 * n
        outs, s_out = refs[o0:o0 + n_out], refs[o0 + n_out:o0 + n_out + 2 + 2 * n]
        scr = refs[o0 + n_out + 2 + 2 * n:]
        compute(*ins, *outs, *scr[:n_scr])
        step = pl.program_id(0)
        for d in range(1, len(grid)):
            step = step * grid[d] + pl.program_id(d)
        sender.issue(s_in[:n], s_in[n:], s_out[0], s_out[1], scr[n_scr], step, n_steps)

    res = pl.pallas_call(
        body, name=name, grid=grid,
        in_specs=in_specs + [_HBM] * (2 * n), out_specs=out_specs + [_SEM, _SEM] + [_HBM] * (2 * n),
        out_shape=out_shape + [pltpu.SemaphoreType.DMA((n_sem,)), pltpu.SemaphoreType.DMA((n_sem,))]
        + [pltpu.HBM(a.shape, a.dtype) for a in sender.srcs] + [pltpu.HBM(a.shape, a.dtype) for a in sender.lands],
        input_output_aliases={n_in + j: n_out + 2 + j for j in range(2 * n)},
        scratch_shapes=scratch_shapes + [pltpu.SemaphoreType.DMA((n,))],
        compiler_params=pltpu.CompilerParams(dimension_semantics=semantics, vmem_limit_bytes=VMEM_LIMIT,
                                             has_side_effects=_EFFECT),
    )(*args, *sender.srcs, *sender.lands)
    handle = (sender.items, name, res[n_out], res[n_out + 1], res[n_out + 2:n_out + 2 + n],
              res[n_out + 2 + n:n_out + 2 + 2 * n])
    return res[:n_out], handle


def _exchange_wait(handle, after):
    items, name, send_sems, recv_sems, srcs, lands = handle
    n = len(items)

    def body(*refs):
        src_refs, land_refs = refs[:n], refs[n:2 * n]
        send_ref, recv_ref = refs[2 * n], refs[2 * n + 1]
        _, copies = _split_copies(items, src_refs, land_refs, send_ref, recv_ref)
        for cp in copies:
            cp.wait_send()
            cp.wait_recv()

    outs = pl.pallas_call(
        body, name=name + "_wait",
        out_shape=[pltpu.HBM(a.shape, a.dtype) for a in srcs] + [pltpu.HBM(a.shape, a.dtype) for a in lands],
        in_specs=[_HBM] * (2 * n) + [_SEM, _SEM, pl.BlockSpec(memory_space=pl.ANY)], out_specs=[_HBM] * (2 * n),
        input_output_aliases={i: i for i in range(2 * n)},
        compiler_params=pltpu.CompilerParams(has_side_effects=_EFFECT),
    )(*srcs, *lands, send_sems, recv_sems, after)
    return outs[n:]


def _mod_fwd(cvec, w_mod_l, b_mod_l):
    rows, cols = cvec.shape[0], w_mod_l.shape[1]

    def body(c_ref, w_ref, b_ref, o_ref, s_ref):
        cv = c_ref[...]
        s = cv * _sigmoid(cv)
        s_ref[...] = s
        o_ref[...] = _dot(s, w_ref[...]) + b_ref[...]

    return pl.pallas_call(
        body, name="mod_fwd",
        out_shape=(jax.ShapeDtypeStruct((rows, cols), F32), jax.ShapeDtypeStruct((rows, D), F32)),
        in_specs=[_full((rows, D)), _full((D, cols)), _full((1, cols))],
        out_specs=(_full((rows, cols)), _full((rows, D))), grid=(1,),
        compiler_params=_params(("arbitrary",)),
    )(cvec, w_mod_l, b_mod_l)


def _mod_bwd(svec, cvec, dmod_l, w_mod_l):
    rows, cols = dmod_l.shape

    def body(s_ref, c_ref, d_ref, w_ref, gw_ref, gc_ref):
        gw_ref[...] = _dot(s_ref[...], d_ref[...], "tn")
        cv = c_ref[...]
        sg = _sigmoid(cv)
        gc_ref[...] = _dot(d_ref[...], w_ref[...], "nt") * (sg * (1.0 + cv * (1.0 - sg)))

    return pl.pallas_call(
        body, name="mod_bwd",
        out_shape=(jax.ShapeDtypeStruct((D, cols), F32), jax.ShapeDtypeStruct((rows, D), F32)),
        in_specs=[_full((rows, D)), _full((rows, D)), _full((rows, cols)), _full((D, cols))],
        out_specs=(_full((D, cols)), _full((rows, D))), grid=(1,),
        compiler_params=_params(("arbitrary",)),
    )(svec, cvec, dmod_l, w_mod_l)


def _inproj(xt, modv, g, w_inT, n_cols, rows_per_example, name, after=None, sender=None):
    rows = xt.shape[0]
    tm = min(TOKEN_TILE, rows_per_example)
    per_b = rows_per_example // tm
    shared_mod = modv.shape[0] == 1

    def body(x_ref, mod_ref, g_ref, w_ref, p_ref, h_ref):
        x = x_ref[...]
        r = lax.rsqrt(jnp.mean(x * x, axis=-1, keepdims=True) + EPS)
        h = (x * r * g_ref[...]) * (1.0 + mod_ref[0, 1:2, :]) + mod_ref[0, 0:1, :]
        hb = h.astype(MXU_DTYPE)
        h_ref[...] = hb
        for j in range(n_cols // KW):
            p_ref[:, j * KW:(j + 1) * KW] = _dot(hb, w_ref[j * KW:(j + 1) * KW, :], "nt").astype(p_ref.dtype)

    mod_idx = (lambda i: (0, 0, 0)) if shared_mod else (lambda i: (i // per_b, 0, 0))
    in_specs = [pl.BlockSpec((tm, D), lambda i: (i, 0)), pl.BlockSpec((1, N_MOD, D), mod_idx), _full((1, D)),
                pl.BlockSpec((n_cols, D), lambda i: (0, 0), pipeline_mode=pl.Buffered(1))]
    (p, h), handle = _host_call(
        body, name, (rows // tm,), in_specs, [xt, modv, g, w_inT],
        [jax.ShapeDtypeStruct((rows, n_cols), MXU_DTYPE), jax.ShapeDtypeStruct((rows, D), MXU_DTYPE)],
        [pl.BlockSpec((tm, n_cols), lambda i: (i, 0)), pl.BlockSpec((tm, D), lambda i: (i, 0))], [],
        after=after, sender=sender)
    return p, h, handle


def _tri(reverse, n):
    row = lax.broadcasted_iota(jnp.int32, (n, n), 0)
    col = lax.broadcasted_iota(jnp.int32, (n, n), 1)
    same = (row // CHUNK) == (col // CHUNK)
    return same & ((col >= row) if reverse else (col <= row))


def _per_chunk_rows(x, reverse):
    n = x.shape[0]
    rows = [x[j * CHUNK:j * CHUNK + 1] if reverse else x[(j + 1) * CHUNK - 1:(j + 1) * CHUNK] for j in range(n // CHUNK)]
    return jnp.concatenate([jnp.broadcast_to(r, (CHUNK, x.shape[1])) for r in rows], axis=0), rows


def _lower_bound(gam_ref, direction):
    return _sigmoid(gam_ref[direction:direction + 1, :] - gam_ref[2 + direction:3 + direction, :])


def _gate_prep(z, lb, tri, reverse):
    sg = _sigmoid(z)
    f = lb + (1.0 - lb) * sg
    g = jnp.log(f)
    b = _mask_dot(tri, g)
    bl, bl_rows = _per_chunk_rows(b, reverse)
    mid = 0.5 * bl
    return sg, g, 1.0 - f, b, jnp.exp(mid), [jnp.exp(0.5 * r) for r in bl_rows], jnp.exp(mid - b), mid


def _hgrn_fwd(p, gam, s0, rows_per_example, with_out, name, sender=None):
    rows = p.shape[0]
    nb_ex = rows // rows_per_example
    rb = min(TOKEN_TILE, rows_per_example)
    cpb = rb // CHUNK
    nb = rows_per_example // rb
    n_chunks = rows // CHUNK
    has_s0 = s0 is not None

    def body(*refs):
        it = iter(refs)
        gam_ref = next(it)
        zf_ref, vf_ref = next(it), next(it)
        qf_ref = next(it) if with_out else None
        zb_ref, vb_ref = next(it), next(it)
        qb_ref = next(it) if with_out else None
        s0_ref = next(it) if has_s0 else None
        if with_out:
            of_ref, ob_ref = next(it), next(it)
        stash_f, stash_b, fin_ref = next(it), next(it), next(it)
        st_ref = next(it)
        i = pl.program_id(1)

        @pl.when(i == 0)
        def _():
            if has_s0:
                st_ref[...] = s0_ref[:, 0]
            else:
                st_ref[...] = jnp.zeros_like(st_ref)

        for direction, (z_ref, v_ref, q_ref, stash) in enumerate(
                ((zf_ref, vf_ref, qf_ref, stash_f), (zb_ref, vb_ref, qb_ref, stash_b))):
            reverse = direction == 1
            tri = _tri(reverse, rb)
            lb = _lower_bound(gam_ref, direction)
            z = z_ref[...].astype(F32)
            v = v_ref[...].astype(F32)
            _, _, k, b, em, em_rows, e2, mid = _gate_prep(z, lb, tri, reverse)
            kd = (k * (e2 * em)).astype(MXU_DTYPE)
            vb = v.astype(MXU_DTYPE)
            if with_out:
                q = q_ref[...].astype(F32)
                qi = q * jnp.exp(b - mid)
                qe = (qi * em).astype(MXU_DTYPE)
                qi = qi.astype(MXU_DTYPE)
                ki = (k * e2).astype(MXU_DTYPE)
                intra = []
                for h in range(HEADS):
                    hs = slice(h * DK, (h + 1) * DK)
                    sc = jnp.where(tri, _dot(qi[:, hs], ki[:, hs], "nt"), 0.0)
                    intra.append(_dot(sc, vb[:, hs]))
            for j in (range(cpb - 1, -1, -1) if reverse else range(cpb)):
                rs = slice(j * CHUNK, (j + 1) * CHUNK)
                a = em_rows[j] * em_rows[j]
                for h in range(HEADS):
                    hs = slice(h * DK, (h + 1) * DK)
                    st = st_ref[direction, h]
                    stash[j, h] = st.astype(stash.dtype)
                    if with_out:
                        (ob_ref if reverse else of_ref)[rs, hs] = intra[h][rs] + _dot(qe[rs, hs], st, "nt")
                    st_ref[direction, h] = st * a[:, hs] + _dot(vb[rs, hs], kd[rs, hs], "tn")

        @pl.when(i == nb - 1)
        def _():
            fin_ref[:, 0] = st_ref[...]

    up = lambda b, i: b * nb + i
    down = lambda b, i: b * nb + nb - 1 - i
    col = lambda rowf, c: pl.BlockSpec((rb, KW), lambda b, i: (rowf(b, i), c))
    in_specs = [_full((4, KW)), col(up, 0), col(up, 2)] + ([col(up, 3)] if with_out else [])
    in_specs += [col(down, 1), col(down, 2)] + ([col(down, 3)] if with_out else [])
    args = [gam, p, p] + ([p] if with_out else []) + [p, p] + ([p] if with_out else [])
    if has_s0:
        in_specs.append(pl.BlockSpec((2, 1, HEADS, DK, DK), lambda b, i: (0, b, 0, 0, 0)))
        args.append(s0)
    out_shape, out_specs = [], []
    if with_out:
        out_shape += [jax.ShapeDtypeStruct((rows, KW), F32)] * 2
        out_specs += [pl.BlockSpec((rb, KW), lambda b, i: (up(b, i), 0)),
                      pl.BlockSpec((rb, KW), lambda b, i: (down(b, i), 0))]
    out_shape += [jax.ShapeDtypeStruct((n_chunks, HEADS, DK, DK), MXU_DTYPE)] * 2
    out_specs += [pl.BlockSpec((cpb, HEADS, DK, DK), lambda b, i: (up(b, i), 0, 0, 0)),
                  pl.BlockSpec((cpb, HEADS, DK, DK), lambda b, i: (down(b, i), 0, 0, 0))]
    out_shape.append(jax.ShapeDtypeStruct((2, nb_ex, HEADS, DK, DK), F32))
    out_specs.append(pl.BlockSpec((2, 1, HEADS, DK, DK), lambda b, i: (0, b, 0, 0, 0)))
    res, handle = _host_call(body, name, (nb_ex, nb), in_specs, args, out_shape, out_specs,
                             [pltpu.VMEM((2, HEADS, DK, DK), F32)], sender=sender)
    return (*res, handle)


def _hgrn_bwd(p, gam, do, stash_f, stash_b, ds_end, rows_per_example, with_out, name, after=None, sender=None):
    rows = p.shape[0]
    nb_ex = rows // rows_per_example
    rb = min(TOKEN_TILE, rows_per_example)
    cpb = rb // CHUNK
    nb = rows_per_example // rb
    has_end = ds_end is not None

    def body(*refs):
        it = iter(refs)
        gam_ref = next(it)
        ins = []
        for _ in range(2):
            z_ref, v_ref = next(it), next(it)
            q_ref = next(it) if with_out else None
            do_ref = next(it) if with_out else None
            ins.append((z_ref, v_ref, q_ref, do_ref, next(it)))
        end_ref = next(it) if has_end else None
        outs = [next(it), next(it)]
        dlb_ref, ds0_ref = next(it), next(it)
        dst_ref = next(it)
        b_id, i = pl.program_id(0), pl.program_id(1)

        @pl.when(i == 0)
        def _():
            if has_end:
                dst_ref[...] = end_ref[:, 0]
            else:
                dst_ref[...] = jnp.zeros_like(dst_ref)

        @pl.when((i == 0) & (b_id == 0))
        def _():
            dlb_ref[...] = jnp.zeros_like(dlb_ref)

        for direction in range(2):
            z_ref, v_ref, q_ref, do_ref, stash = ins[direction]
            dgrp_ref = outs[direction]
            reverse = direction == 1
            tri = _tri(reverse, rb)
            tri_t = _tri(not reverse, rb)
            lb = _lower_bound(gam_ref, direction)
            heads = [slice(h * DK, (h + 1) * DK) for h in range(HEADS)]
            chunks = [slice(j * CHUNK, (j + 1) * CHUNK) for j in range(cpb)]
            grid_cat = lambda parts: jnp.concatenate([jnp.concatenate(row, axis=1) for row in parts], axis=0)
            cat = lambda parts: jnp.concatenate(parts, axis=1)
            z = z_ref[...].astype(F32)
            sg, g, k, b, em, em_rows, e2, mid = _gate_prep(z, lb, tri, reverse)
            e3 = e2 * em
            kd = k * e3
            kd_b = kd.astype(MXU_DTYPE)
            vb = v_ref[...].astype(MXU_DTYPE)
            if with_out:
                q = q_ref[...].astype(F32)
                dout = do_ref[...].astype(MXU_DTYPE)
                e1 = jnp.exp(b - mid)
                e4 = e1 * em
                qi, ki, qe = q * e1, k * e2, q * e4
                qi_b, ki_b, qe_b = qi.astype(MXU_DTYPE), ki.astype(MXU_DTYPE), qe.astype(MXU_DTYPE)
                dqi_p, dki_p, dv_p = [], [], []
                for hs in heads:
                    sc = jnp.where(tri, _dot(qi_b[:, hs], ki_b[:, hs], "nt"), 0.0)
                    dsc = jnp.where(tri, _dot(dout[:, hs], vb[:, hs], "nt"), 0.0)
                    dqi_p.append(_dot(dsc, ki_b[:, hs]))
                    dki_p.append(_dot(dsc, qi_b[:, hs], "tn"))
                    dv_p.append(_dot(sc, dout[:, hs], "tn"))
                dqi, dki, dv = cat(dqi_p), cat(dki_p), cat(dv_p)
                dqe = grid_cat([[_dot(dout[rs, hs], stash[j, h]) for h, hs in enumerate(heads)]
                                for j, rs in enumerate(chunks)])
                grow = [[_dot(dout[rs, hs], qe_b[rs, hs], "tn") for hs in heads] for rs in chunks]
            dkd_p = [[None] * HEADS for _ in range(cpb)]
            dvs_p = [[None] * HEADS for _ in range(cpb)]
            da_p = [[None] * HEADS for _ in range(cpb)]
            for j in (range(cpb) if reverse else range(cpb - 1, -1, -1)):
                rs = chunks[j]
                a = em_rows[j] * em_rows[j]
                for h, hs in enumerate(heads):
                    dst = dst_ref[direction, h]
                    dkd_p[j][h] = _dot(vb[rs, hs], dst)
                    dvs_p[j][h] = _dot(kd_b[rs, hs], dst, "nt")
                    da_p[j][h] = jnp.broadcast_to(
                        jnp.sum(dst * stash[j, h].astype(F32), axis=0, keepdims=True), (CHUNK, DK))
                    new_dst = dst * a[:, hs]
                    dst_ref[direction, h] = new_dst + grow[j][h] if with_out else new_dst
            dkd, dvs, da = grid_cat(dkd_p), grid_cat(dvs_p), grid_cat(da_p)
            t_kd = dkd * kd
            dk = dkd * e3
            db = -t_kd
            tot = t_kd
            if with_out:
                dgrp_ref[:, KW:2 * KW] = (dvs + dv).astype(dgrp_ref.dtype)
                dgrp_ref[:, 2 * KW:] = (dqi * e1 + dqe * e4).astype(dgrp_ref.dtype)
                dk = dk + dki * e2
                t_qi, t_ki, t_qe = dqi * qi, dki * ki, dqe * qe
                db = db + t_qi - t_ki + t_qe
                tot = tot + 0.5 * (t_ki - t_qi)
            else:
                dgrp_ref[:, KW:2 * KW] = dvs.astype(dgrp_ref.dtype)
            dbl = jnp.concatenate([jnp.broadcast_to(jnp.sum(tot[rs], axis=0, keepdims=True), (CHUNK, KW))
                                   for rs in chunks], axis=0) + da * (em * em)
            dg = _mask_dot(tri_t, db) + dbl
            df = dg * jnp.exp(-g) - dk
            dgrp_ref[:, 0:KW] = (df * (1.0 - lb) * sg * (1.0 - sg)).astype(dgrp_ref.dtype)
            dlb_ref[direction:direction + 1, :] += jnp.sum(df * (1.0 - sg), axis=0, keepdims=True)

        @pl.when(i == nb - 1)
        def _():
            ds0_ref[:, 0] = dst_ref[...]

    rows_of = (lambda b, i: b * nb + nb - 1 - i, lambda b, i: b * nb + i)
    in_specs, args = [_full((4, KW))], [gam]
    for direction in range(2):
        rf = rows_of[direction]
        col = lambda c, rf=rf: pl.BlockSpec((rb, KW), lambda b, i: (rf(b, i), c))
        in_specs += [col(direction), col(2)]
        args += [p, p]
        if with_out:
            in_specs += [col(3), col(0)]
            args += [p, do]
        in_specs.append(pl.BlockSpec((cpb, HEADS, DK, DK), lambda b, i, rf=rf: (rf(b, i), 0, 0, 0)))
        args.append((stash_f, stash_b)[direction])
    if has_end:
        in_specs.append(pl.BlockSpec((2, 1, HEADS, DK, DK), lambda b, i: (0, b, 0, 0, 0)))
        args.append(ds_end)
    out_shape, out_specs = [], []
    for direction in range(2):
        rf = rows_of[direction]
        width = (3 if with_out else 2) * KW
        out_shape.append(jax.ShapeDtypeStruct((rows, width), MXU_DTYPE))
        out_specs.append(pl.BlockSpec((rb, width), lambda b, i, rf=rf: (rf(b, i), 0)))
    out_shape += [jax.ShapeDtypeStruct((2, KW), F32), jax.ShapeDtypeStruct((2, nb_ex, HEADS, DK, DK), F32)]
    out_specs += [_full((2, KW)), pl.BlockSpec((2, 1, HEADS, DK, DK), lambda b, i: (0, b, 0, 0, 0))]
    res, handle = _host_call(body, name, (nb_ex, nb), in_specs, args, out_shape, out_specs,
                             [pltpu.VMEM((2, HEADS, DK, DK), F32)], after=after, sender=sender)
    return (*res, handle)


def _tail_forward(osum, og, u, v, ga, gb, gna, ln_g, ln_b, ws_ref, bs_ref, wpaT_ref, wpbT_ref):
    tm = osum.shape[0]
    gna4 = jnp.concatenate([gna] * HEADS, axis=1)
    r_parts = []
    for h in range(HEADS):
        oh = osum[:, h * DK:(h + 1) * DK]
        r_parts.append(jnp.broadcast_to(lax.rsqrt(jnp.mean(oh * oh, axis=-1, keepdims=True) + EPS), (tm, DK)))
    r = jnp.concatenate(r_parts, axis=1)
    on = osum * r
    sg_og = _sigmoid(og)
    silu_og = og * sg_og
    oan = on * gna4
    oa = oan * silu_og
    ug, tu = _gelu(u)
    vg, tv = _gelu(v)
    mu = jnp.mean(vg, axis=-1, keepdims=True)
    vc = vg - mu
    rstd = lax.rsqrt(jnp.mean(vc * vc, axis=-1, keepdims=True) + EPS)
    vhat = vc * rstd
    vln = vhat * ln_g + ln_b
    blocks = []
    for n in range(tm // SGU_BLOCK):
        rs = slice(n * SGU_BLOCK, (n + 1) * SGU_BLOCK)
        blocks.append(jnp.concatenate(
            [_dot(ws_ref[g], vln[rs, g * DK:(g + 1) * DK]) + bs_ref[g] for g in range(GROUPS)], axis=1))
    mixed = jnp.concatenate(blocks, axis=0) if len(blocks) > 1 else blocks[0]
    obm = ug * mixed
    pa = _dot(oa, wpaT_ref[...], "nt")
    pb = _dot(obm, wpbT_ref[...], "nt")
    sga, sgb = _sigmoid(ga), _sigmoid(gb)
    merged = sga * pa + sgb * pb
    return dict(r=r, on=on, sg_og=sg_og, silu_og=silu_og, oan=oan, oa=oa, ug=ug, tu=tu, tv=tv, rstd=rstd, vhat=vhat,
                vln=vln, mixed=mixed, obm=obm, pa=pa, pb=pb, sga=sga, sgb=sgb, merged=merged, gna4=gna4)


def _tail_in_specs(tm):
    tile = lambda c: pl.BlockSpec((tm, KW), lambda i: (i, c))
    return [tile(c) for c in range(4, 11)]


def _tail_weight_specs():
    return [_full((1, DK)), _full((1, KW)), _full((1, KW)), _full((GROUPS, SGU_BLOCK, SGU_BLOCK)),
            _full((GROUPS, SGU_BLOCK, 1)), _full((D, KW), single=True), _full((D, KW), single=True),
            _full((D, D), single=True)]


def _read_tail_inputs(of_ref, ob_ref, pcols):
    osum = of_ref[...] + ob_ref[...]
    og, u, v = (pcols[j][...].astype(F32) for j in range(3))
    ga = jnp.concatenate([pcols[3][...], pcols[4][...]], axis=1).astype(F32)
    gb = jnp.concatenate([pcols[5][...], pcols[6][...]], axis=1).astype(F32)
    return osum, og, u, v, ga, gb


def _tail_fwd(p, o_up, o_down, xt, modv, gna, ln_g, ln_b, w_s, b_s, w_paT, w_pbT, w_o, rows_per_example):
    rows = xt.shape[0]
    tm = min(TOKEN_TILE, rows_per_example)
    per_b = rows_per_example // tm

    def body(of_ref, ob_ref, *rest):
        pcols = rest[:7]
        (x_ref, mod_ref, gna_ref, lng_ref, lnb_ref, ws_ref, bs_ref, wpaT_ref, wpbT_ref, wo_ref,
         x1_ref, mix_ref, merged_ref, oa_ref, obm_ref) = rest[7:]
        t = _tail_forward(*_read_tail_inputs(of_ref, ob_ref, pcols), gna_ref[...], lng_ref[...], lnb_ref[...],
                          ws_ref, bs_ref, wpaT_ref, wpbT_ref)
        mix = _dot(t["merged"], wo_ref[...])
        x1_ref[...] = x_ref[...] + mod_ref[0, 2:3, :] * mix
        mix_ref[...] = mix.astype(mix_ref.dtype)
        merged_ref[...] = t["merged"].astype(merged_ref.dtype)
        oa_ref[...] = t["oa"].astype(oa_ref.dtype)
        obm_ref[...] = t["obm"].astype(obm_ref.dtype)

    row = lambda w: pl.BlockSpec((tm, w), lambda i: (i, 0))
    in_specs = [row(KW), row(KW)] + _tail_in_specs(tm) + [row(D), pl.BlockSpec((1, N_MOD, D), lambda i: (i // per_b, 0, 0))]
    in_specs += _tail_weight_specs()
    return pl.pallas_call(
        body, name="tail_fwd", grid=(rows // tm,),
        out_shape=(jax.ShapeDtypeStruct((rows, D), F32), jax.ShapeDtypeStruct((rows, D), MXU_DTYPE),
                   jax.ShapeDtypeStruct((rows, D), MXU_DTYPE), jax.ShapeDtypeStruct((rows, KW), MXU_DTYPE),
                   jax.ShapeDtypeStruct((rows, KW), MXU_DTYPE)),
        in_specs=in_specs, out_specs=(row(D), row(D), row(D), row(KW), row(KW)),
        compiler_params=_params(("arbitrary",)),
    )(o_up, o_down, *([p] * 7), xt, modv, gna, ln_g, ln_b, w_s, b_s, w_paT, w_pbT, w_o)


def _tail_bwd(p, o_up, o_down, dx1, mix, modv, gna, ln_g, ln_b, w_s, b_s, w_paT, w_pbT, w_o, rows_per_example,
              after=None, sender=None):
    rows = dx1.shape[0]
    nb_ex = rows // rows_per_example
    tm = min(TOKEN_TILE, rows_per_example)
    per_b = rows_per_example // tm

    def body(of_ref, ob_ref, *rest):
        pcols = rest[:7]
        (dx1_ref, mix_ref, mod_ref, gna_ref, lng_ref, lnb_ref, ws_ref, bs_ref, wpaT_ref, wpbT_ref, wo_ref,
         dpt_ref, do_ref, dmix_ref, dpa_ref, dpb_ref, dmod_ref, small_ref, dws_ref, dbs_ref) = rest[7:]
        i = pl.program_id(0)

        @pl.when(i == 0)
        def _():
            small_ref[...] = jnp.zeros_like(small_ref)
            dws_ref[...] = jnp.zeros_like(dws_ref)
            dbs_ref[...] = jnp.zeros_like(dbs_ref)

        @pl.when(i % per_b == 0)
        def _():
            dmod_ref[...] = jnp.zeros_like(dmod_ref)

        osum, og, u, v, ga, gb = _read_tail_inputs(of_ref, ob_ref, pcols)
        ln_g = lng_ref[...]
        t = _tail_forward(osum, og, u, v, ga, gb, gna_ref[...], ln_g, lnb_ref[...], ws_ref, bs_ref, wpaT_ref, wpbT_ref)
        dx1v = dx1_ref[...]
        dmod_ref[0, 2:3, :] += jnp.sum(dx1v * mix_ref[...].astype(F32), axis=0, keepdims=True)
        dmix = dx1v * mod_ref[0, 2:3, :]
        dmix_ref[...] = dmix.astype(dmix_ref.dtype)
        dmerged = _dot(dmix, wo_ref[...], "nt")
        sga, sgb = t["sga"], t["sgb"]
        dpa = dmerged * sga
        dpb = dmerged * sgb
        dpa_ref[...] = dpa.astype(dpa_ref.dtype)
        dpb_ref[...] = dpb.astype(dpb_ref.dtype)
        dga = dmerged * t["pa"] * sga * (1.0 - sga)
        dgb = dmerged * t["pb"] * sgb * (1.0 - sgb)
        doa = _dot(dpa, wpaT_ref[...])
        dobm = _dot(dpb, wpbT_ref[...])
        dug = dobm * t["mixed"]
        dmixed = dobm * t["ug"]
        du = dug * _gelu_grad(u, t["tu"])
        dvln_blocks = []
        for n in range(tm // SGU_BLOCK):
            rs = slice(n * SGU_BLOCK, (n + 1) * SGU_BLOCK)
            parts = []
            for g in range(GROUPS):
                gs = slice(g * DK, (g + 1) * DK)
                dm = dmixed[rs, gs]
                parts.append(_dot(ws_ref[g], dm, "tn"))
                dws_ref[g] += _dot(dm, t["vln"][rs, gs], "nt")
                dbs_ref[g] += jnp.sum(dm, axis=1, keepdims=True)
            dvln_blocks.append(jnp.concatenate(parts, axis=1))
        dvln = jnp.concatenate(dvln_blocks, axis=0) if len(dvln_blocks) > 1 else dvln_blocks[0]
        vhat = t["vhat"]
        small_ref[1:2, 0:KW] += jnp.sum(dvln * vhat, axis=0, keepdims=True)
        small_ref[2:3, 0:KW] += jnp.sum(dvln, axis=0, keepdims=True)
        dvhat = dvln * ln_g
        dvg = t["rstd"] * (dvhat - jnp.mean(dvhat, axis=-1, keepdims=True)
                           - vhat * jnp.mean(dvhat * vhat, axis=-1, keepdims=True))
        dv = dvg * _gelu_grad(v, t["tv"])
        sg_og = t["sg_og"]
        doan = doa * t["silu_og"]
        dog = doa * t["oan"] * (sg_og * (1.0 + og * (1.0 - sg_og)))
        prod = doan * t["on"]
        dgna = jnp.zeros((1, DK), F32)
        for h in range(HEADS):
            dgna = dgna + jnp.sum(prod[:, h * DK:(h + 1) * DK], axis=0, keepdims=True)
        small_ref[0:1, 0:DK] += dgna
        don = doan * t["gna4"]
        dot_parts = []
        for h in range(HEADS):
            hs = slice(h * DK, (h + 1) * DK)
            m = jnp.mean(don[:, hs] * t["on"][:, hs], axis=-1, keepdims=True)
            dot_parts.append(t["r"][:, hs] * (don[:, hs] - t["on"][:, hs] * m))
        do_ref[...] = jnp.concatenate(dot_parts, axis=1).astype(do_ref.dtype)
        for j, val in enumerate((dog, du, dv)):
            dpt_ref[:, j * KW:(j + 1) * KW] = val.astype(dpt_ref.dtype)
        dpt_ref[:, 3 * KW:3 * KW + D] = dga.astype(dpt_ref.dtype)
        dpt_ref[:, 3 * KW + D:] = dgb.astype(dpt_ref.dtype)

    row = lambda w: pl.BlockSpec((tm, w), lambda i: (i, 0))
    in_specs = [row(KW), row(KW)] + _tail_in_specs(tm) + [row(D), row(D), pl.BlockSpec((1, N_MOD, D), lambda i: (i // per_b, 0, 0))]
    in_specs += _tail_weight_specs()
    args = [o_up, o_down, *([p] * 7), dx1, mix, modv, gna, ln_g, ln_b, w_s, b_s, w_paT, w_pbT, w_o]
    cd = MXU_DTYPE
    res, handle = _host_call(
        body, "tail_bwd", (rows // tm,), in_specs, args,
        [jax.ShapeDtypeStruct((rows, TAIL_COLS), cd), jax.ShapeDtypeStruct((rows, KW), cd),
         jax.ShapeDtypeStruct((rows, D), cd), jax.ShapeDtypeStruct((rows, D), cd),
         jax.ShapeDtypeStruct((rows, D), cd), jax.ShapeDtypeStruct((nb_ex, 8, D), F32),
         jax.ShapeDtypeStruct((8, D), F32), jax.ShapeDtypeStruct((GROUPS, SGU_BLOCK, SGU_BLOCK), F32),
         jax.ShapeDtypeStruct((GROUPS, SGU_BLOCK, 1), F32)],
        [row(TAIL_COLS), row(KW), row(D), row(D), row(D),
         pl.BlockSpec((1, 8, D), lambda i: (i // per_b, 0, 0)), _full((8, D)),
         _full((GROUPS, SGU_BLOCK, SGU_BLOCK)), _full((GROUPS, SGU_BLOCK, 1))], [],
        after=after, sender=sender)
    return (*res, handle)


def _ffn(x1, target, modv, g_ffn, g_final, w_upT, w_down, rows_per_example):
    rows = x1.shape[0]
    nb_ex = rows // rows_per_example
    tm = min(TOKEN_TILE, rows_per_example)
    per_b = rows_per_example // tm
    n_ff = D_FF // FF_CHUNK

    def body(x1_ref, tgt_ref, mod_ref, gffn_ref, gfin_ref, wup_ref, wdn_ref,
             dx1_ref, h2_ref, dffn_ref, act_ref, dup_ref, dmod_ref, small_ref, up_scr):
        i = pl.program_id(0)

        @pl.when(i == 0)
        def _():
            small_ref[...] = jnp.zeros_like(small_ref)

        @pl.when(i % per_b == 0)
        def _():
            dmod_ref[...] = jnp.zeros_like(dmod_ref)

        x1v = x1_ref[...]
        g2 = gffn_ref[...]
        m3, m4, m5 = mod_ref[0, 3:4, :], mod_ref[0, 4:5, :], mod_ref[0, 5:6, :]
        r2 = lax.rsqrt(jnp.mean(x1v * x1v, axis=-1, keepdims=True) + EPS)
        xn2 = x1v * r2
        h2 = (xn2 * g2) * (1.0 + m4) + m3
        h2b = h2.astype(MXU_DTYPE)
        h2_ref[...] = h2b
        def up_pair(j):
            lo = j * FF_CHUNK
            return (_dot(h2b, wup_ref[lo:lo + FF_CHUNK, :], "nt"),
                    _dot(h2b, wup_ref[D_FF + lo:D_FF + lo + FF_CHUNK, :], "nt"))

        group_end = {min(e, n_ff): s for s, e in ((0, 4), (4, 8), (8, 12))}
        cur, ffn = up_pair(0), None
        for j in range(n_ff):
            nxt = up_pair(j + 1) if j + 1 < n_ff else None
            cs = slice(j * FF_CHUNK, (j + 1) * FF_CHUNK)
            a, bgate = cur
            up_scr[:, cs] = a
            up_scr[:, D_FF + j * FF_CHUNK:D_FF + (j + 1) * FF_CHUNK] = bgate
            act_ref[:, cs] = (a * _sigmoid(a) * bgate).astype(MXU_DTYPE)
            cur = nxt
            if j + 1 in group_end:
                gs = slice(group_end[j + 1] * FF_CHUNK, (j + 1) * FF_CHUNK)
                part = _dot(act_ref[:, gs], wdn_ref[gs, :])
                ffn = part if ffn is None else ffn + part
        x2 = x1v + m5 * ffn
        r3 = lax.rsqrt(jnp.mean(x2 * x2, axis=-1, keepdims=True) + EPS)
        xn3 = x2 * r3
        gf = gfin_ref[...]
        err = xn3 * gf - tgt_ref[...]
        loss = 0.5 * jnp.sum(jnp.mean(err * err, axis=-1, keepdims=True), axis=0, keepdims=True)
        small_ref[2:3, :] += jnp.broadcast_to(loss, (1, D))
        dy = err * (1.0 / D)
        small_ref[1:2, :] += jnp.sum(dy * xn3, axis=0, keepdims=True)
        dxn3 = dy * gf
        dx2 = r3 * (dxn3 - xn3 * jnp.mean(dxn3 * xn3, axis=-1, keepdims=True))
        dmod_ref[0, 5:6, :] += jnp.sum(dx2 * ffn, axis=0, keepdims=True)
        dffn = (dx2 * m5).astype(MXU_DTYPE)
        dffn_ref[...] = dffn
        dact_of = lambda j: _dot(dffn, wdn_ref[j * FF_CHUNK:(j + 1) * FF_CHUNK, :], "nt")
        cur, dh2 = dact_of(0), None
        for j in range(n_ff):
            nxt = dact_of(j + 1) if j + 1 < n_ff else None
            cs = slice(j * FF_CHUNK, (j + 1) * FF_CHUNK)
            a, bgate = up_scr[:, cs], up_scr[:, D_FF + j * FF_CHUNK:D_FF + (j + 1) * FF_CHUNK]
            s = _sigmoid(a)
            dup_ref[:, cs] = (cur * bgate * (s * (1.0 + a * (1.0 - s)))).astype(MXU_DTYPE)
            dup_ref[:, D_FF + j * FF_CHUNK:D_FF + (j + 1) * FF_CHUNK] = (cur * a * s).astype(MXU_DTYPE)
            cur = nxt
            if j + 1 in group_end:
                lo, hi = group_end[j + 1] * FF_CHUNK, (j + 1) * FF_CHUNK
                part = (_dot(dup_ref[:, lo:hi], wup_ref[lo:hi, :])
                        + _dot(dup_ref[:, D_FF + lo:D_FF + hi], wup_ref[D_FF + lo:D_FF + hi, :]))
                dh2 = part if dh2 is None else dh2 + part
        dmod_ref[0, 3:4, :] += jnp.sum(dh2, axis=0, keepdims=True)
        dmod_ref[0, 4:5, :] += jnp.sum(dh2 * xn2 * g2, axis=0, keepdims=True)
        small_ref[0:1, :] += jnp.sum(dh2 * (1.0 + m4) * xn2, axis=0, keepdims=True)
        dxn2 = dh2 * g2 * (1.0 + m4)
        dx1_ref[...] = dx2 + r2 * (dxn2 - xn2 * jnp.mean(dxn2 * xn2, axis=-1, keepdims=True))

    row = lambda w: pl.BlockSpec((tm, w), lambda i: (i, 0))
    cd = MXU_DTYPE
    return pl.pallas_call(
        body, name="ffn_fwd_bwd", grid=(rows // tm,),
        out_shape=(jax.ShapeDtypeStruct((rows, D), F32), jax.ShapeDtypeStruct((rows, D), cd),
                   jax.ShapeDtypeStruct((rows, D), cd), jax.ShapeDtypeStruct((rows, D_FF), cd),
                   jax.ShapeDtypeStruct((rows, 2 * D_FF), cd), jax.ShapeDtypeStruct((nb_ex, 8, D), F32),
                   jax.ShapeDtypeStruct((8, D), F32)),
        in_specs=[row(D), row(D), pl.BlockSpec((1, N_MOD, D), lambda i: (i // per_b, 0, 0)), _full((1, D)), _full((1, D)),
                  _full((2 * D_FF, D), single=True), _full((D_FF, D), single=True)],
        out_specs=(row(D), row(D), row(D), row(D_FF), row(2 * D_FF),
                   pl.BlockSpec((1, 8, D), lambda i: (i // per_b, 0, 0)), _full((8, D))),
        scratch_shapes=[pltpu.VMEM((tm, 2 * D_FF), F32)],
        compiler_params=_params(("arbitrary",)),
    )(x1, target, modv, g_ffn, g_final, w_upT, w_down)


def _scan_columns(up, down, n_groups):
    cols = [up[:, 0:KW].astype(F32), down[:, 0:KW].astype(F32)]
    for j in range(1, n_groups):
        cols.append(up[:, j * KW:(j + 1) * KW].astype(F32) + down[:, j * KW:(j + 1) * KW].astype(F32))
    return cols


def _inproj_bwd(d_up, d_down, dpt, xt, dx1, modv, g, w_inT, rows_per_example, name, sender=None):
    rows = xt.shape[0]
    latent = dx1 is not None
    n_cols = IN_COLS if latent else CTX_COLS
    n_groups = d_up.shape[1] // KW
    tm = min(TOKEN_TILE, rows_per_example)
    per_b = rows_per_example // tm
    n_mod_blocks = rows // rows_per_example if latent else 1

    def body(*refs):
        it = iter(refs)
        up_ref, down_ref = next(it), next(it)
        dpt_ref = next(it) if latent else None
        x_ref = next(it)
        dx1_ref = next(it) if latent else None
        mod_ref, g_ref, w_ref = next(it), next(it), next(it)
        gx_ref = next(it) if latent else None
        dp_out = None if latent else next(it)
        dmod_ref, small_ref = next(it), next(it)
        dp_ref = next(it) if latent else dp_out
        i = pl.program_id(0)

        @pl.when(i == 0)
        def _():
            small_ref[...] = jnp.zeros_like(small_ref)

        @pl.when((i % per_b == 0) if latent else (i == 0))
        def _():
            dmod_ref[...] = jnp.zeros_like(dmod_ref)

        for j, val in enumerate(_scan_columns(up_ref[...], down_ref[...], n_groups)):
            dp_ref[:, j * KW:(j + 1) * KW] = val.astype(MXU_DTYPE)
        if latent:
            dh = _dot(dp_ref[...], w_ref[0:4 * KW, :]) + _dot(dpt_ref[...], w_ref[4 * KW:, :])
        else:
            dh = _dot(dp_ref[...], w_ref[...])
        x = x_ref[...]
        gv = g_ref[...]
        m1 = mod_ref[0, 1:2, :]
        r = lax.rsqrt(jnp.mean(x * x, axis=-1, keepdims=True) + EPS)
        xn = x * r
        dmod_ref[0, 0:1, :] += jnp.sum(dh, axis=0, keepdims=True)
        dmod_ref[0, 1:2, :] += jnp.sum(dh * xn * gv, axis=0, keepdims=True)
        small_ref[0:1, :] += jnp.sum(dh * (1.0 + m1) * xn, axis=0, keepdims=True)
        if latent:
            dxn = dh * gv * (1.0 + m1)
            gx_ref[...] = dx1_ref[...] + r * (dxn - xn * jnp.mean(dxn * xn, axis=-1, keepdims=True))

    row = lambda w: pl.BlockSpec((tm, w), lambda i: (i, 0))
    mod_idx = (lambda i: (i // per_b, 0, 0)) if latent else (lambda i: (0, 0, 0))
    in_specs = [row(n_groups * KW)] * 2 + ([row(TAIL_COLS)] if latent else []) + [row(D)] + ([row(D)] if latent else [])
    in_specs += [pl.BlockSpec((1, N_MOD, D), mod_idx), _full((1, D)),
                 pl.BlockSpec((n_cols, D), lambda i: (0, 0), pipeline_mode=pl.Buffered(1))]
    args = [d_up, d_down] + ([dpt] if latent else []) + [xt] + ([dx1] if latent else []) + [modv, g, w_inT]
    first = jax.ShapeDtypeStruct((rows, D), F32) if latent else jax.ShapeDtypeStruct((rows, n_cols), MXU_DTYPE)
    out_shape = [first, jax.ShapeDtypeStruct((n_mod_blocks, 8, D), F32), jax.ShapeDtypeStruct((8, D), F32)]
    out_specs = [row(D) if latent else row(n_cols), pl.BlockSpec((1, 8, D), mod_idx), _full((8, D))]
    scratch = [pltpu.VMEM((tm, 4 * KW), MXU_DTYPE)] if latent else []
    res, handle = _host_call(body, name, (rows // tm,), in_specs, args, out_shape, out_specs, scratch, sender=sender)
    return (*res, handle)


def _grad_matmul(a, b, name, init=None, tn=512, sender=None):
    rows, n = a.shape
    k = b.shape[1]
    tn = min(tn, n)
    has_init = init is not None
    init_blocks = init.shape[0] // tn if has_init else 0

    def body(*refs):
        if has_init:
            a_ref, b_ref, init_ref, o_ref = refs
        else:
            a_ref, b_ref, o_ref = refs
        g = _dot(a_ref[...], b_ref[...], "tn")
        if has_init:
            g = g + jnp.where(pl.program_id(0) < init_blocks, init_ref[...].astype(F32), 0.0)
        o_ref[...] = g.astype(o_ref.dtype)

    in_specs = [pl.BlockSpec((rows, tn), lambda i: (0, i)), _full((rows, k), single=True)]
    args = [a, b]
    if has_init:
        in_specs.append(pl.BlockSpec((tn, k), lambda i: (jnp.minimum(i, init_blocks - 1), 0)))
        args.append(init)
    (out,), handle = _host_call(
        body, name, (n // tn,), in_specs, args, [jax.ShapeDtypeStruct((n, k), PAYLOAD_DTYPE)],
        [pl.BlockSpec((tn, k), lambda i: (i, 0))], [], sender=sender)
    return out, handle


def _grad_in(d_up, d_down, dpt, h, init, sender=None):
    rows = h.shape[0]
    tn = 256
    per_group = KW // tn
    n_scan = 4 * per_group
    init_blocks = init.shape[0] // tn

    def body(up_ref, down_ref, dpt_ref, h_ref, init_ref, o_ref):
        i = pl.program_id(0)
        both = (up_ref[...].astype(F32) + down_ref[...].astype(F32)).astype(MXU_DTYPE)
        a = jnp.where(i < per_group, up_ref[...],
                      jnp.where(i < 2 * per_group, down_ref[...], jnp.where(i < n_scan, both, dpt_ref[...])))
        g = _dot(a, h_ref[...], "tn") + jnp.where(i < init_blocks, init_ref[...].astype(F32), 0.0)
        o_ref[...] = g.astype(o_ref.dtype)

    last = 3 * per_group - 1
    col = lambda f: pl.BlockSpec((rows, tn), lambda i: (0, f(i)))
    in_specs = [col(lambda i: jnp.clip(jnp.where(i < per_group, i, i - per_group), 0, last)),
                col(lambda i: jnp.clip(i - per_group, 0, last)),
                col(lambda i: jnp.clip(i - n_scan, 0, TAIL_COLS // tn - 1)),
                _full((rows, D), single=True),
                pl.BlockSpec((tn, D), lambda i: (jnp.minimum(i, init_blocks - 1), 0))]
    (out,), handle = _host_call(
        body, "gw_in", (IN_COLS // tn,), in_specs, [d_up, d_down, dpt, h, init],
        [jax.ShapeDtypeStruct((IN_COLS, D), PAYLOAD_DTYPE)], [pl.BlockSpec((tn, D), lambda i: (i, 0))], [],
        sender=sender)
    return out, handle


def _row_tile(rows, limit=256):
    if rows <= limit:
        return rows
    for t in range(limit, 7, -8):
        if rows % t == 0:
            return t
    return rows


def _sum8(stack, name):
    _, rows, cols = stack.shape
    tr = _row_tile(rows)

    def body(s_ref, o_ref):
        acc = s_ref[0].astype(F32)
        for j in range(1, N_DEV):
            acc = acc + s_ref[j].astype(F32)
        o_ref[...] = acc

    return pl.pallas_call(
        body, name=name, grid=(rows // tr,), out_shape=jax.ShapeDtypeStruct((rows, cols), F32),
        in_specs=[pl.BlockSpec((N_DEV, tr, cols), lambda i: (0, i, 0))],
        out_specs=pl.BlockSpec((tr, cols), lambda i: (i, 0)),
        compiler_params=_params(("arbitrary",)),
    )(stack)


def _adamw_update(w, gv, m, v):
    nm = ADAM_B1 * m + (1.0 - ADAM_B1) * gv
    nv = ADAM_B2 * v + (1.0 - ADAM_B2) * (gv * gv)
    m_hat = nm / (1.0 - ADAM_B1 ** ADAM_STEP)
    v_hat = nv / (1.0 - ADAM_B2 ** ADAM_STEP)
    return -ADAM_LR * (m_hat / (jnp.sqrt(v_hat) + ADAM_EPS) + ADAM_WD * w), nm, nv


SMALL_PARAMS = (("g_mix", 0, D), ("g_ffn", 1, D), ("g_final", 2, D), ("g_norm_a", 3, DK), ("ln_v_g", 4, KW),
                ("ln_v_b", 5, KW), ("b_s", 6, GROUPS * SGU_BLOCK))


def _small_finish(early, late, dws, gam, nb_ex, params):
    names = [n for n, _, _ in SMALL_PARAMS] + ["b_mod", "w_s"]

    def body(*refs):
        s_ref, l_ref, dws_ref, gam_ref = refs[:4]
        p_refs = refs[4:4 + 3 * len(names)]
        tot_ref, dgam_ref = refs[4 + 3 * len(names):6 + 3 * len(names)]
        o_refs = refs[6 + 3 * len(names):]
        acc = s_ref[0] + l_ref[0]
        gws = dws_ref[0]
        for j in range(1, N_DEV):
            acc = acc + (s_ref[j] + l_ref[j])
            gws = gws + dws_ref[j]
        tot_ref[...] = acc
        bm = acc[8:8 + N_MOD, :]
        for e in range(nb_ex):
            bm = bm + acc[16 + e * N_MOD:16 + (e + 1) * N_MOD, :]
        lb = jnp.concatenate([_lower_bound(gam_ref, 0), _lower_bound(gam_ref, 1)], axis=1)
        dgam = acc[7:8, :] * lb * (1.0 - lb)
        dgam_ref[...] = jnp.concatenate([dgam, -dgam], axis=0)
        grads = [acc[row:row + 1, 0:width] for _, row, width in SMALL_PARAMS] + [bm, gws]
        for k, g in enumerate(grads):
            w_ref, m_ref, v_ref = p_refs[3 * k:3 * k + 3]
            o_refs[4 * k][...] = g
            o_refs[4 * k + 1][...], o_refs[4 * k + 2][...], o_refs[4 * k + 3][...] = _adamw_update(
                w_ref[...], g, m_ref[...], v_ref[...])

    p_args, p_specs, o_shapes, o_specs = [], [], [], []
    for n in names:
        for a in params[n]:
            p_args.append(a)
            p_specs.append(_full(a.shape))
        o_shapes += [jax.ShapeDtypeStruct(params[n][0].shape, F32)] * 4
        o_specs += [_full(params[n][0].shape)] * 4
    res = pl.pallas_call(
        body, name="small_finish", grid=(1,),
        out_shape=[jax.ShapeDtypeStruct((SMALL_ROWS, D), F32), jax.ShapeDtypeStruct((2, D), F32)] + o_shapes,
        in_specs=[_full(early.shape), _full(late.shape), _full(dws.shape), _full((4, KW))] + p_specs,
        out_specs=[_full((SMALL_ROWS, D)), _full((2, D))] + o_specs,
        compiler_params=_params(("arbitrary",)),
    )(early, late, dws, gam, *p_args)
    return res[0], res[1], {n: res[2 + 4 * k:6 + 4 * k] for k, n in enumerate(names)}


def _adamw_sum8(stack, w, m, v, name):
    _, rows, cols = stack.shape
    tr = _row_tile(rows)

    def body(s_ref, w_ref, m_ref, v_ref, g_ref, d_ref, nm_ref, nv_ref):
        gv = s_ref[0].astype(F32)
        for j in range(1, N_DEV):
            gv = gv + s_ref[j].astype(F32)
        g_ref[...] = gv
        d_ref[...], nm_ref[...], nv_ref[...] = _adamw_update(w_ref[...], gv, m_ref[...], v_ref[...])

    blk = pl.BlockSpec((tr, cols), lambda i: (i, 0))
    sd = jax.ShapeDtypeStruct((rows, cols), F32)
    return pl.pallas_call(
        body, name=name, grid=(rows // tr,), out_shape=(sd, sd, sd, sd),
        in_specs=[pl.BlockSpec((N_DEV, tr, cols), lambda i: (0, i, 0)), blk, blk, blk], out_specs=(blk, blk, blk, blk),
        compiler_params=_params(("arbitrary",)),
    )(stack, w, m, v)


def _adamw(w, g, m, v, name):
    shape = w.shape
    cols = shape[-1]
    rows = 1
    for s in shape[:-1]:
        rows *= s
    tr = _row_tile(rows)

    def body(w_ref, g_ref, m_ref, v_ref, d_ref, nm_ref, nv_ref):
        gv = g_ref[...]
        nm = ADAM_B1 * m_ref[...] + (1.0 - ADAM_B1) * gv
        nv = ADAM_B2 * v_ref[...] + (1.0 - ADAM_B2) * (gv * gv)
        m_hat = nm / (1.0 - ADAM_B1 ** ADAM_STEP)
        v_hat = nv / (1.0 - ADAM_B2 ** ADAM_STEP)
        d_ref[...] = -ADAM_LR * (m_hat / (jnp.sqrt(v_hat) + ADAM_EPS) + ADAM_WD * w_ref[...])
        nm_ref[...] = nm
        nv_ref[...] = nv

    blk = pl.BlockSpec((tr, cols), lambda i: (i, 0))
    sd = jax.ShapeDtypeStruct((rows, cols), F32)
    d, nm, nv = pl.pallas_call(
        body, name=name, grid=(rows // tr,), out_shape=(sd, sd, sd), in_specs=[blk] * 4, out_specs=(blk, blk, blk),
        compiler_params=_params(("arbitrary",)),
    )(w.reshape(rows, cols), g.reshape(rows, cols), m.reshape(rows, cols), v.reshape(rows, cols))
    return d.reshape(shape), nm.reshape(shape), nv.reshape(shape)


def _owner_blocks(a):
    return a.reshape(N_DEV, a.shape[0] // N_DEV, a.shape[1])


class _LocalWeights:
    def __init__(self, w_upT, w_down, w_o, w_paT, w_pbT):
        self.weights = (w_upT, w_down, w_o, w_paT, w_pbT)
        self.items = {}

    def sender(self, stage, items=None):
        self.items[stage] = items
        return None

    def sent(self, stage, handle):
        pass

    def mixer_weights(self, after):
        return self.weights[1:]

    def ffn_weights(self, after):
        return self.weights[0]


def _local_step(x, ctx, target, modv, mcv, gam, g_mix, g_ffn, gna, ln_g, ln_b, w_s, b_s, g_final, w_inT, comm):
    nb_ex, seq, _ = x.shape
    ctx_len = ctx.shape[1]
    xt = x.reshape(nb_ex * seq, D)
    ct = ctx.reshape(nb_ex * ctx_len, D)
    tgt = target.reshape(nb_ex * seq, D)
    bs3 = b_s.reshape(GROUPS, SGU_BLOCK, 1)

    pc, hc, _ = _inproj(ct, mcv, g_mix, w_inT, CTX_COLS, ctx_len, "inproj_ctx")
    p, h, handle = _inproj(xt, modv, g_mix, w_inT, IN_COLS, seq, "inproj_lat", sender=comm.sender("inproj"))
    comm.sent("inproj", handle)
    cst_f, cst_b, s_ctx, _ = _hgrn_fwd(pc, gam, None, ctx_len, False, "hgrn_fwd_ctx")
    o_up, o_down, st_f, st_b, _, handle = _hgrn_fwd(p, gam, s_ctx, seq, True, "hgrn_fwd_lat",
                                                    sender=comm.sender("scan"))
    comm.sent("scan", handle)
    w_down, w_o, w_paT, w_pbT = comm.mixer_weights(o_up)
    x1, mix, merged, oa, obm = _tail_fwd(p, o_up, o_down, xt, modv, gna, ln_g, ln_b, w_s, bs3, w_paT, w_pbT, w_o, seq)
    w_upT = comm.ffn_weights(x1)
    dx1, h2, dffn, act, dup, dmod_ffn, small_ffn = _ffn(x1, tgt, modv, g_ffn, g_final, w_upT, w_down, seq)
    gw_upT, _ = _grad_matmul(dup, h2, "gw_up")
    gw_down, _ = _grad_matmul(act, dffn, "gw_down", tn=256)
    scatter = lambda *grads: [(_owner_blocks(g), "scatter") for g in grads]
    dpt, do, dmix, dpa, dpb, dmod_tail, small_tail, dws, dbs, handle = _tail_bwd(
        p, o_up, o_down, dx1, mix, modv, gna, ln_g, ln_b, w_s, bs3, w_paT, w_pbT, w_o, seq,
        sender=comm.sender("tail_bwd", scatter(gw_upT)))
    comm.sent("tail_bwd", handle)
    gw_o, _ = _grad_matmul(merged, dmix, "gw_o")
    gw_paT, _ = _grad_matmul(dpa, oa, "gw_pa")
    gw_pbT, _ = _grad_matmul(dpb, obm, "gw_pb")
    def at_row(row, a):
        return jnp.pad(a, ((row, SMALL_ROWS - row - a.shape[0]), (0, D - a.shape[1])))

    small_early = (at_row(1, small_ffn[0:2])
                   + at_row(3, small_tail[0:3])
                   + at_row(6, dbs.reshape(1, GROUPS * SGU_BLOCK))
                   + at_row(14, small_ffn[2:3]))
    dws_rows = dws.reshape(GROUPS * SGU_BLOCK, SGU_BLOCK)
    d_up, d_down, dlb, ds0, handle = _hgrn_bwd(
        p, gam, do, st_f, st_b, None, seq, True, "hgrn_bwd_lat",
        sender=comm.sender("scan_bwd", scatter(gw_down, gw_o, gw_paT, gw_pbT)
                           + [(small_early, "gather"), (dws_rows, "gather")]))
    comm.sent("scan_bwd", handle)
    c_up, c_down, dlb_c, _, _ = _hgrn_bwd(pc, gam, None, cst_f, cst_b, ds0, ctx_len, False, "hgrn_bwd_ctx")
    dpc, dmc, small_c, _ = _inproj_bwd(c_up, c_down, None, ct, None, mcv, g_mix, w_inT, ctx_len, "inproj_bwd_ctx")
    gw_inT, _ = _grad_in(d_up, d_down, dpt, h, _grad_matmul(dpc, hc, "gw_in_ctx")[0])
    grad_x, dmod_in, small_in, handle = _inproj_bwd(d_up, d_down, dpt, xt, dx1, modv, g_mix, w_inT, seq,
                                                   "inproj_bwd_lat", sender=comm.sender("inproj_bwd", scatter(gw_inT)))
    comm.sent("inproj_bwd", handle)
    dmod = dmod_in + dmod_tail + dmod_ffn
    small_late = (at_row(0, small_in[0:1] + small_c[0:1])
                  + at_row(7, (dlb + dlb_c).reshape(1, 2 * KW))
                  + at_row(8, dmc[0, 0:N_MOD])
                  + at_row(16, dmod[:, 0:N_MOD].reshape(nb_ex * N_MOD, D)))
    comm.sender("last", [(small_late, "gather")])
    return grad_x.reshape(x.shape)


def kernel(x, c, ctx, c_ctx, w_mod, b_mod, g_mix, g_ffn, w_in, lb_gamma, g_norm_a, ln_v_g, ln_v_b, w_s, b_s, w_pa, w_pb, w_o, w_up, w_down, g_final, loss_target, m_c_ctx, m_w_mod, m_b_mod, m_g_mix, m_g_ffn, m_w_in, m_lb_gamma, m_g_norm_a, m_ln_v_g, m_ln_v_b, m_w_s, m_b_s, m_w_pa, m_w_pb, m_w_o, m_w_up, m_w_down, m_g_final, v_c_ctx, v_w_mod, v_b_mod, v_g_mix, v_g_ffn, v_w_in, v_lb_gamma, v_g_norm_a, v_ln_v_g, v_ln_v_b, v_w_s, v_b_s, v_w_pa, v_w_pb, v_w_o, v_w_up, v_w_down, v_g_final):
    nb_ex = x.shape[0]
    me = 4 * lax.axis_index("x") + 2 * lax.axis_index("y") + lax.axis_index("c")
    cd = MXU_DTYPE
    mod_cols = w_mod.shape[2]
    lb_cols = lb_gamma.shape[2]

    w_inT_l = w_in[0].T.astype(cd)
    w_upT_l = w_up[0].T.astype(cd)
    w_paT_l = w_pa[0].T.astype(cd)
    w_pbT_l = w_pb[0].T.astype(cd)
    cl = jnp.concatenate([c, jnp.pad(lb_gamma.reshape(1, 4 * lb_cols), ((0, 0), (0, D - 4 * lb_cols))),
                          jnp.zeros((8 - nb_ex - 1, D), F32)], axis=0)
    g_in, g_cl = _gather_two_level([w_inT_l, cl], "gather_w_in")
    w_inT = g_in.reshape(IN_COLS, D)
    c_all = g_cl[:, 0:nb_ex].reshape(N_DEV * nb_ex, D)
    gam = jnp.transpose(g_cl[:, nb_ex, 0:4 * lb_cols].reshape(N_DEV, 4, lb_cols), (1, 0, 2)).reshape(4, KW)

    n_c = N_DEV * nb_ex
    cvec = jnp.concatenate([c_all, c_ctx.reshape(1, D), jnp.zeros((7, D), F32)], axis=0)
    b_mod_l = lax.dynamic_slice(b_mod, (0, me * mod_cols), (1, mod_cols))
    mod_l, svec = _mod_fwd(cvec, w_mod[0], b_mod_l)
    (g_mod,) = _exchange([(mod_l, "gather")], "gather_mod")
    mod_all = jnp.transpose(g_mod, (1, 0, 2)).reshape(n_c + 8, N_MOD * D)
    modv = lax.dynamic_slice(mod_all, (me * nb_ex, 0), (nb_ex, N_MOD * D)).reshape(nb_ex, N_MOD, D)
    mcv = mod_all[n_c].reshape(1, N_MOD, D)

    handles, leftover = {}, {}

    class Comm:
        def sender(self, stage, items=None):
            if stage == "inproj":
                return _Sender([(w_down[0].astype(cd), "gather"), (w_o[0].astype(cd), "gather"), (w_paT_l, "gather"),
                                (w_pbT_l, "gather")])
            if stage == "scan":
                return _Sender([(w_upT_l, "gather")])
            if stage == "last":
                leftover["items"] = items
                return None
            return _Sender(items)

        def sent(self, stage, handle):
            handles[stage] = handle

        def mixer_weights(self, after):
            g_down, g_o, g_pa, g_pb = _exchange_wait(handles["inproj"], after)
            return g_down.reshape(D_FF, D), g_o.reshape(D, D), g_pa.reshape(D, KW), g_pb.reshape(D, KW)

        def ffn_weights(self, after):
            (g_up,) = _exchange_wait(handles["scan"], after)
            return g_up.reshape(2 * D_FF, D)

    grad_x = _local_step(
        x, ctx, loss_target, modv, mcv, gam, g_mix, g_ffn, g_norm_a, ln_v_g, ln_v_b, w_s[0], b_s[0],
        g_final.reshape(1, D), w_inT, Comm())
    last, last_started = _exchange_start(leftover["items"], "gather_small_late", after=leftover["items"][0][0])

    (r_up,) = _exchange_wait(handles["tail_bwd"], last_started)
    r_down, r_o, r_pa, r_pb, r_small, r_dws = _exchange_wait(handles["scan_bwd"], r_up)
    raw_up = _adamw_sum8(r_up, w_up[0].T, m_w_up[0].T, v_w_up[0].T, "adamw_w_up")
    raw_down = _adamw_sum8(r_down, w_down[0], m_w_down[0], v_w_down[0], "adamw_w_down")
    raw_o = _adamw_sum8(r_o, w_o[0], m_w_o[0], v_w_o[0], "adamw_w_o")
    (r_in,) = _exchange_wait(handles["inproj_bwd"], raw_up[1])
    raw_in = _adamw_sum8(r_in, w_in[0].T, m_w_in[0].T, v_w_in[0].T, "adamw_w_in")
    (r_late,) = _exchange_wait(last, raw_in[1])
    done = {"w_in": [a.T[None] for a in raw_in], "w_up": [a.T[None] for a in raw_up],
            "w_down": [a[None] for a in raw_down], "w_o": [a[None] for a in raw_o]}
    grad_w_in, grad_w_up, grad_w_down, grad_w_o = (done[k][0] for k in ("w_in", "w_up", "w_down", "w_o"))
    grad_w_pa = _sum8(r_pa, "sum_w_pa").T[None]
    grad_w_pb = _sum8(r_pb, "sum_w_pb").T[None]
    as_2d = {"g_final": (1, D), "b_s": (1, GROUPS * SGU_BLOCK), "b_mod": (N_MOD, D), "w_s": (GROUPS * SGU_BLOCK, SGU_BLOCK)}
    small_params = {"g_mix": (g_mix, m_g_mix, v_g_mix), "g_ffn": (g_ffn, m_g_ffn, v_g_ffn),
                    "g_final": (g_final, m_g_final, v_g_final), "g_norm_a": (g_norm_a, m_g_norm_a, v_g_norm_a),
                    "ln_v_g": (ln_v_g, m_ln_v_g, v_ln_v_g), "ln_v_b": (ln_v_b, m_ln_v_b, v_ln_v_b),
                    "b_s": (b_s, m_b_s, v_b_s), "b_mod": (b_mod, m_b_mod, v_b_mod), "w_s": (w_s, m_w_s, v_w_s)}
    tot, dgam, small_done = _small_finish(
        r_small, r_late, r_dws, gam, nb_ex,
        {n: tuple(a.reshape(as_2d.get(n, a.shape)) for a in wmv) for n, wmv in small_params.items()})
    for n, outs in small_done.items():
        done[n] = [a.reshape(small_params[n][0].shape) for a in outs]
    loss = tot[14, 0]
    grad_g_mix, grad_g_ffn, grad_g_final, grad_g_norm_a, grad_ln_v_g, grad_ln_v_b, grad_b_s, grad_b_mod, grad_w_s = (
        done[n][0] for n in ("g_mix", "g_ffn", "g_final", "g_norm_a", "ln_v_g", "ln_v_b", "b_s", "b_mod", "w_s"))
    grad_lb_gamma = lax.dynamic_slice(dgam.reshape(2, 2, KW), (0, 0, me * lb_cols), (2, 2, lb_cols))

    dmod_all = r_late[:, 16:16 + nb_ex * N_MOD].reshape(n_c, N_MOD * D)
    dmod_l = jnp.concatenate([lax.dynamic_slice(dmod_all, (0, me * mod_cols), (n_c, mod_cols)),
                              lax.dynamic_slice(tot[8:8 + N_MOD].reshape(1, N_MOD * D), (0, me * mod_cols), (1, mod_cols)),
                              jnp.zeros((7, mod_cols), F32)], axis=0)
    gw_mod, gc = _mod_bwd(svec, cvec, dmod_l, w_mod[0])
    grad_w_mod = gw_mod[None]
    (r_gc,) = _exchange([(gc[n_c:n_c + 8], "gather")], "gather_c_ctx", after=r_late)
    grad_c_ctx = _sum8(r_gc, "sum_c_ctx")[0]

    names = ["c_ctx", "w_mod", "b_mod", "g_mix", "g_ffn", "w_in", "lb_gamma", "g_norm_a", "ln_v_g", "ln_v_b", "w_s",
             "b_s", "w_pa", "w_pb", "w_o", "w_up", "w_down", "g_final"]
    weights = [c_ctx, w_mod, b_mod, g_mix, g_ffn, w_in, lb_gamma, g_norm_a, ln_v_g, ln_v_b, w_s, b_s, w_pa, w_pb, w_o,
               w_up, w_down, g_final]
    grads = [grad_c_ctx, grad_w_mod, grad_b_mod, grad_g_mix, grad_g_ffn, grad_w_in, grad_lb_gamma, grad_g_norm_a,
             grad_ln_v_g, grad_ln_v_b, grad_w_s, grad_b_s, grad_w_pa, grad_w_pb, grad_w_o, grad_w_up, grad_w_down,
             grad_g_final]
    ms = [m_c_ctx, m_w_mod, m_b_mod, m_g_mix, m_g_ffn, m_w_in, m_lb_gamma, m_g_norm_a, m_ln_v_g, m_ln_v_b, m_w_s, m_b_s,
          m_w_pa, m_w_pb, m_w_o, m_w_up, m_w_down, m_g_final]
    vs = [v_c_ctx, v_w_mod, v_b_mod, v_g_mix, v_g_ffn, v_w_in, v_lb_gamma, v_g_norm_a, v_ln_v_g, v_ln_v_b, v_w_s, v_b_s,
          v_w_pa, v_w_pb, v_w_o, v_w_up, v_w_down, v_g_final]
    deltas, new_ms, new_vs = [], [], []
    for nm, w, g, m, v in zip(names, weights, grads, ms, vs):
        d, nm_, nv_ = done[nm][1:] if nm in done else _adamw(w, g.reshape(w.shape), m, v, "adamw_" + nm)
        deltas.append(d)
        new_ms.append(nm_)
        new_vs.append(nv_)
    grads = [g.reshape(w.shape) for g, w in zip(grads, weights)]
    return (loss, grad_x, *grads, *deltas, *new_ms, *new_vs)
```

```python
import functools

import jax
import jax.numpy as jnp
from jax import lax
from jax.experimental import pallas as pl
from jax.experimental.pallas import tpu as pltpu

F32 = jnp.float32
MXU_DTYPE = jnp.bfloat16
PAYLOAD_DTYPE = jnp.bfloat16

N_DEV = 8
D = 1024
HEADS = 4
DK = 128
KW = HEADS * DK
CHUNK = 64
SGU_BLOCK = 128
GROUPS = 4
D_FF = 2816
FF_CHUNK = 256
N_MOD = 6
IN_COLS = 5632
CTX_COLS = 1536
TAIL_COLS = IN_COLS - 4 * KW
EPS = 1e-6
ADAM_LR, ADAM_B1, ADAM_B2, ADAM_EPS, ADAM_WD, ADAM_STEP = 0.001, 0.9, 0.999, 1e-08, 0.01, 10

VMEM_LIMIT = 56 * 1024 * 1024
TOKEN_TILE = 256
SMALL_ROWS = 40


def _params(sem):
    return pltpu.CompilerParams(dimension_semantics=sem, vmem_limit_bytes=VMEM_LIMIT)


_DN = {"nn": (((1,), (0,)), ((), ())), "nt": (((1,), (1,)), ((), ())), "tn": (((0,), (0,)), ((), ()))}


def _dot(a, b, form="nn"):
    return lax.dot_general(a.astype(MXU_DTYPE), b.astype(MXU_DTYPE), _DN[form], preferred_element_type=F32)


def _mask_dot(mask, v):
    bf = jnp.bfloat16
    hi = v.astype(bf)
    r1 = v - hi.astype(F32)
    mid = r1.astype(bf)
    lo = (r1 - mid.astype(F32)).astype(bf)
    w = v.shape[1]
    s = lax.dot_general(mask.astype(bf), jnp.concatenate([hi, mid, lo], axis=1), _DN["nn"], preferred_element_type=F32)
    return (s[:, 2 * w:] + s[:, w:2 * w]) + s[:, :w]


def _full(shape, single=False):
    n = len(shape)
    if single:
        return pl.BlockSpec(shape, lambda *_: (0,) * n, pipeline_mode=pl.Buffered(1))
    return pl.BlockSpec(shape, lambda *_: (0,) * n)


def _ordered_behind(body, in_specs, args, after):
    if after is None:
        return body
    at = len(in_specs)
    in_specs.append(pl.BlockSpec(memory_space=pl.ANY))
    args.append(after)
    return lambda *refs: body(*refs[:at], *refs[at + 1:])


def _sigmoid(z):
    return 0.5 * jnp.tanh(0.5 * z) + 0.5


def _gelu(x):
    c = 0.7978845608028654
    t = jnp.tanh(c * (x + 0.044715 * x * x * x))
    return 0.5 * x * (1.0 + t), t


def _gelu_grad(x, t):
    c = 0.7978845608028654
    return 0.5 * (1.0 + t) + 0.5 * x * (1.0 - t * t) * c * (1.0 + 3 * 0.044715 * x * x)


def _exchange(items, name, after=None):
    n = len(items)
    out_shape = []
    for a, mode in items:
        blk = a.shape if mode == "gather" else a.shape[1:]
        out_shape.append(jax.ShapeDtypeStruct((N_DEV,) + tuple(blk), a.dtype))

    def body(*refs):
        srcs, dsts = refs[:n], refs[n:2 * n]
        send_sems, recv_sems, local_sems = refs[2 * n:]
        x, y, c = lax.axis_index("x"), lax.axis_index("y"), lax.axis_index("c")
        me = 4 * x + 2 * y + c

        def src_for(i, dev):
            return srcs[i] if items[i][1] == "gather" else srcs[i].at[dev]

        local = [pltpu.make_async_copy(src_for(i, me), dsts[i].at[me], local_sems.at[i]) for i in range(n)]
        for cp in local:
            cp.start()
        remote = []
        for k in range(1, N_DEV):
            px = jnp.bitwise_xor(x, (k >> 2) & 1)
            py = jnp.bitwise_xor(y, (k >> 1) & 1)
            pc = jnp.bitwise_xor(c, k & 1)
            peer = 4 * px + 2 * py + pc
            for i in range(n):
                cp = pltpu.make_async_remote_copy(
                    src_ref=src_for(i, peer), dst_ref=dsts[i].at[me],
                    send_sem=send_sems.at[i * (N_DEV - 1) + k - 1], recv_sem=recv_sems.at[i * (N_DEV - 1) + k - 1],
                    device_id=(px, py, pc), device_id_type=pl.DeviceIdType.MESH)
                cp.start()
                remote.append(cp)
        for cp in remote:
            cp.wait()
        for cp in local:
            cp.wait()

    any_spec = pl.BlockSpec(memory_space=pl.ANY)
    in_specs, args = [any_spec] * n, [a for a, _ in items]
    if after is not None:
        in_specs.append(any_spec)
        args.append(after)
        exchange = body
        body = lambda *refs: exchange(*refs[:n], *refs[n + 1:])
    return pl.pallas_call(
        body, name=name, out_shape=out_shape, in_specs=in_specs, out_specs=[any_spec] * n,
        scratch_shapes=[pltpu.SemaphoreType.DMA((n * (N_DEV - 1),)), pltpu.SemaphoreType.DMA((n * (N_DEV - 1),)),
                        pltpu.SemaphoreType.DMA((n,))],
    )(*args)


def _gather_two_level(arrays, name):
    n = len(arrays)
    pieces = []
    for i, a in enumerate(arrays):
        rows = _Sender.PIECE_ROWS if a.shape[0] % _Sender.PIECE_ROWS == 0 else a.shape[0]
        pieces += [(i, r0, rows) for r0 in range(0, a.shape[0], rows)]

    def body(*refs):
        srcs, dsts = refs[:n], refs[n:2 * n]
        send_sems, recv_sems, local_sems = refs[2 * n:]
        x, y, c = lax.axis_index("x"), lax.axis_index("y"), lax.axis_index("c")
        me, sibling = (x, y, c), (x, y, 1 - c)
        x_nbr, y_nbr, diag = (1 - x, y, c), (x, 1 - y, c), (1 - x, 1 - y, c)

        def slot(px, py, pc):
            return 4 * px + 2 * py + pc

        def copy(u, k, block, to, own=False):
            i, r0, rows = pieces[u]
            there = dsts[i].at[slot(*block)].at[pl.ds(r0, rows)]
            return pltpu.make_async_remote_copy(
                src_ref=srcs[i].at[pl.ds(r0, rows)] if own else there, dst_ref=there,
                send_sem=send_sems.at[u * 7 + k], recv_sem=recv_sems.at[u * 7 + k],
                device_id=to, device_id_type=pl.DeviceIdType.MESH)

        units = range(len(pieces))
        mine = [pltpu.make_async_copy(srcs[i], dsts[i].at[slot(*me)], local_sems.at[i]) for i in range(n)]
        for cp in mine:
            cp.start()
        for u in units:
            copy(u, 1, me, x_nbr, own=True).start()
            copy(u, 2, me, y_nbr, own=True).start()
        for u in units:
            copy(u, 0, me, sibling, own=True).start()

        def relay_then_pass(k_from, frm, to, k_other, other):
            for u in units:
                copy(u, k_from, frm, me).wait_recv()
                copy(u, 3, frm, to).start()
                copy(u, 3 + k_from, frm, sibling).start()
            for u in units:
                copy(u, k_other, other, me).wait_recv()
                copy(u, 3 + k_other, other, sibling).start()

        @pl.when(c == 1)
        def _():
            relay_then_pass(1, x_nbr, y_nbr, 2, y_nbr)

        @pl.when(c == 0)
        def _():
            relay_then_pass(2, y_nbr, x_nbr, 1, x_nbr)

        for u in units:
            copy(u, 3, diag, me).wait_recv()
            copy(u, 6, diag, sibling).start()
        for u in units:
            copy(u, 0, sibling, me).wait_recv()
            for k, chip in ((4, x_nbr), (5, y_nbr), (6, diag)):
                copy(u, k, (chip[0], chip[1], 1 - c), me).wait_recv()
        for u in units:
            for k in range(7):
                copy(u, k, me, me, own=True).wait_send()
        for cp in mine:
            cp.wait()

    any_spec = pl.BlockSpec(memory_space=pl.ANY)
    return pl.pallas_call(
        body, name=name, out_shape=[jax.ShapeDtypeStruct((N_DEV,) + a.shape, a.dtype) for a in arrays],
        in_specs=[any_spec] * n, out_specs=[any_spec] * n,
        scratch_shapes=[pltpu.SemaphoreType.DMA((len(pieces) * 7,)), pltpu.SemaphoreType.DMA((len(pieces) * 7,)),
                        pltpu.SemaphoreType.DMA((n,))],
    )(*arrays)


_HBM = pl.BlockSpec(memory_space=pltpu.HBM)
_SEM = pl.BlockSpec(memory_space=pltpu.SEMAPHORE)
_EFFECT = pltpu.SideEffectType.DATAFLOW_SIDE_EFFECTING


def _split_copies(items, srcs, lands, send_sems, recv_sems):
    x, y, c = lax.axis_index("x"), lax.axis_index("y"), lax.axis_index("c")
    me = 4 * x + 2 * y + c
    copies = []
    for k in range(1, N_DEV):
        px = jnp.bitwise_xor(x, (k >> 2) & 1)
        py = jnp.bitwise_xor(y, (k >> 1) & 1)
        pc = jnp.bitwise_xor(c, k & 1)
        peer = 4 * px + 2 * py + pc
        for i in range(len(items)):
            src = srcs[i] if items[i][1] == "gather" else srcs[i].at[peer]
            copies.append(pltpu.make_async_remote_copy(
                src_ref=src, dst_ref=lands[i].at[me],
                send_sem=send_sems.at[i * (N_DEV - 1) + k - 1], recv_sem=recv_sems.at[i * (N_DEV - 1) + k - 1],
                device_id=(px, py, pc), device_id_type=pl.DeviceIdType.MESH))
    return me, copies


def _exchange_start(items, name, after):
    n = len(items)
    n_sem = n * (N_DEV - 1)
    srcs, lands = [], []
    for a, mode in items:
        blk = a.shape if mode == "gather" else a.shape[1:]
        srcs.append(pltpu.with_memory_space_constraint(a, pltpu.HBM))
        lands.append(pltpu.with_memory_space_constraint(lax.empty((N_DEV,) + tuple(blk), a.dtype), pltpu.HBM))

    def body(*refs):
        src_refs, land_refs = refs[:n], refs[n:2 * n]
        send_sems, recv_sems = refs[2 * n + 1], refs[2 * n + 2]
        local_sems = refs[4 * n + 3]
        me, copies = _split_copies(items, src_refs, land_refs, send_sems, recv_sems)
        for i in range(n):
            own = src_refs[i] if items[i][1] == "gather" else src_refs[i].at[me]
            cp = pltpu.make_async_copy(own, land_refs[i].at[me], local_sems.at[i])
            cp.start()
            cp.wait()
        for cp in copies:
            cp.start()

    out_shape = [pltpu.SemaphoreType.DMA((n_sem,)), pltpu.SemaphoreType.DMA((n_sem,))]
    out_shape += [pltpu.HBM(a.shape, a.dtype) for a in srcs] + [pltpu.HBM(a.shape, a.dtype) for a in lands]
    outs = pl.pallas_call(
        body, name=name, out_shape=out_shape,
        in_specs=[_HBM] * (2 * n) + [pl.BlockSpec(memory_space=pl.ANY)],
        out_specs=[_SEM, _SEM] + [_HBM] * (2 * n),
        input_output_aliases={i: 2 + i for i in range(2 * n)},
        scratch_shapes=[pltpu.SemaphoreType.DMA((n,))],
        compiler_params=pltpu.CompilerParams(has_side_effects=_EFFECT),
    )(*srcs, *lands, after)
    handle = (items, name, outs[0], outs[1], outs[2:2 + n], outs[2 + n:2 + 2 * n])
    return handle, outs[2]


class _Sender:
    PIECE_ROWS = 352

    def __init__(self, items, chunks=None):
        self.items, self.n = items, len(items)
        self.chunks = chunks
        if chunks is None:
            block_rows = [a.shape[0] if mode == "gather" else a.shape[1] for a, mode in items]
            self.chunks = [r // self.PIECE_ROWS if r % self.PIECE_ROWS == 0 else 1 for r in block_rows]
        self.srcs, self.lands = [], []
        for a, mode in items:
            blk = a.shape if mode == "gather" else a.shape[1:]
            self.srcs.append(pltpu.with_memory_space_constraint(a, pltpu.HBM))
            self.lands.append(pltpu.with_memory_space_constraint(lax.empty((N_DEV,) + tuple(blk), a.dtype), pltpu.HBM))

    def issue(self, src_refs, land_refs, send_sems, recv_sems, local_sems, step, n_steps):
        x, y, c = lax.axis_index("x"), lax.axis_index("y"), lax.axis_index("c")
        me = 4 * x + 2 * y + c
        copies = []
        for ch in range(max(self.chunks)):
            for k in range(1, N_DEV):
                px = jnp.bitwise_xor(x, (k >> 2) & 1)
                py = jnp.bitwise_xor(y, (k >> 1) & 1)
                pc = jnp.bitwise_xor(c, k & 1)
                peer = 4 * px + 2 * py + pc
                for i, (_, mode) in enumerate(self.items):
                    if ch >= self.chunks[i]:
                        continue
                    n_rows = land_refs[i].shape[1] // self.chunks[i]
                    rows = pl.ds(ch * n_rows, n_rows)
                    src = src_refs[i].at[rows] if mode == "gather" else src_refs[i].at[peer].at[rows]
                    copies.append(pltpu.make_async_remote_copy(
                        src_ref=src, dst_ref=land_refs[i].at[me].at[rows],
                        send_sem=send_sems.at[i * (N_DEV - 1) + k - 1], recv_sem=recv_sems.at[i * (N_DEV - 1) + k - 1],
                        device_id=(px, py, pc), device_id_type=pl.DeviceIdType.MESH))
        own = [pltpu.make_async_copy(src_refs[i] if mode == "gather" else src_refs[i].at[me], land_refs[i].at[me],
                                     local_sems.at[i]) for i, (_, mode) in enumerate(self.items)]

        @pl.when(step == 0)
        def _():
            for cp in own:
                cp.start()

        for s in range(n_steps):
            group = [cp for j, cp in enumerate(copies) if (j * n_steps) // len(copies) == s]
            if group:
                @pl.when(step == s)
                def _(group=group):
                    for cp in group:
                        cp.start()

        @pl.when(step == n_steps - 1)
        def _():
            for cp in own:
                cp.wait()


def _host_call(body, name, grid, in_specs, args, out_shape, out_specs, scratch_shapes, after=None, sender=None):
    in_specs, args, out_shape, out_specs = list(in_specs), list(args), list(out_shape), list(out_specs)
    scratch_shapes = list(scratch_shapes)
    semantics = ("arbitrary",) * len(grid)
    body = _ordered_behind(body, in_specs, args, after)
    if sender is None:
        res = pl.pallas_call(body, name=name, grid=grid, in_specs=in_specs, out_specs=out_specs, out_shape=out_shape,
                             scratch_shapes=scratch_shapes, compiler_params=_params(semantics))(*args)
        return res, None
    n, n_in, n_out, n_scr = sender.n, len(in_specs), len(out_shape), len(scratch_shapes)
    n_sem = n * (N_DEV - 1)
    n_steps = 1
    for g in grid:
        n_steps *= g
    compute = body

    def body(*refs):
        ins, s_in = refs[:n_in], refs[n_in:n_in + 2 * n]
        o0 = n_in + 2 * n
        outs, s_out = refs[o0:o0 + n_out], refs[o0 + n_out:o0 + n_out + 2 + 2 * n]
        scr = refs[o0 + n_out + 2 + 2 * n:]
        compute(*ins, *outs, *scr[:n_scr])
        step = pl.program_id(0)
        for d in range(1, len(grid)):
            step = step * grid[d] + pl.program_id(d)
        sender.issue(s_in[:n], s_in[n:], s_out[0], s_out[1], scr[n_scr], step, n_steps)

    res = pl.pallas_call(
        body, name=name, grid=grid,
        in_specs=in_specs + [_HBM] * (2 * n), out_specs=out_specs + [_SEM, _SEM] + [_HBM] * (2 * n),
        out_shape=out_shape + [pltpu.SemaphoreType.DMA((n_sem,)), pltpu.SemaphoreType.DMA((n_sem,))]
        + [pltpu.HBM(a.shape, a.dtype) for a in sender.srcs] + [pltpu.HBM(a.shape, a.dtype) for a in sender.lands],
        input_output_aliases={n_in + j: n_out + 2 + j for j in range(2 * n)},
        scratch_shapes=scratch_shapes + [pltpu.SemaphoreType.DMA((n,))],
        compiler_params=pltpu.CompilerParams(dimension_semantics=semantics, vmem_limit_bytes=VMEM_LIMIT,
                                             has_side_effects=_EFFECT),
    )(*args, *sender.srcs, *sender.lands)
    handle = (sender.items, name, res[n_out], res[n_out + 1], res[n_out + 2:n_out + 2 + n],
              res[n_out + 2 + n:n_out + 2 + 2 * n])
    return res[:n_out], handle


def _exchange_wait(handle, after):
    items, name, send_sems, recv_sems, srcs, lands = handle
    n = len(items)

    def body(*refs):
        src_refs, land_refs = refs[:n], refs[n:2 * n]
        send_ref, recv_ref = refs[2 * n], refs[2 * n + 1]
        _, copies = _split_copies(items, src_refs, land_refs, send_ref, recv_ref)
        for cp in copies:
            cp.wait_send()
            cp.wait_recv()

    outs = pl.pallas_call(
        body, name=name + "_wait",
        out_shape=[pltpu.HBM(a.shape, a.dtype) for a in srcs] + [pltpu.HBM(a.shape, a.dtype) for a in lands],
        in_specs=[_HBM] * (2 * n) + [_SEM, _SEM, pl.BlockSpec(memory_space=pl.ANY)], out_specs=[_HBM] * (2 * n),
        input_output_aliases={i: i for i in range(2 * n)},
        compiler_params=pltpu.CompilerParams(has_side_effects=_EFFECT),
    )(*srcs, *lands, send_sems, recv_sems, after)
    return outs[n:]


def _mod_fwd(cvec, w_mod_l, b_mod_l):
    rows, cols = cvec.shape[0], w_mod_l.shape[1]

    def body(c_ref, w_ref, b_ref, o_ref, s_ref):
        cv = c_ref[...]
        s = cv * _sigmoid(cv)
        s_ref[...] = s
        o_ref[...] = _dot(s, w_ref[...]) + b_ref[...]

    return pl.pallas_call(
        body, name="mod_fwd",
        out_shape=(jax.ShapeDtypeStruct((rows, cols), F32), jax.ShapeDtypeStruct((rows, D), F32)),
        in_specs=[_full((rows, D)), _full((D, cols)), _full((1, cols))],
        out_specs=(_full((rows, cols)), _full((rows, D))), grid=(1,),
        compiler_params=_params(("arbitrary",)),
    )(cvec, w_mod_l, b_mod_l)


def _mod_bwd(svec, cvec, dmod_l, w_mod_l):
    rows, cols = dmod_l.shape

    def body(s_ref, c_ref, d_ref, w_ref, gw_ref, gc_ref):
        gw_ref[...] = _dot(s_ref[...], d_ref[...], "tn")
        cv = c_ref[...]
        sg = _sigmoid(cv)
        gc_ref[...] = _dot(d_ref[...], w_ref[...], "nt") * (sg * (1.0 + cv * (1.0 - sg)))

    return pl.pallas_call(
        body, name="mod_bwd",
        out_shape=(jax.ShapeDtypeStruct((D, cols), F32), jax.ShapeDtypeStruct((rows, D), F32)),
        in_specs=[_full((rows, D)), _full((rows, D)), _full((rows, cols)), _full((D, cols))],
        out_specs=(_full((D, cols)), _full((rows, D))), grid=(1,),
        compiler_params=_params(("arbitrary",)),
    )(svec, cvec, dmod_l, w_mod_l)


def _inproj(xt, modv, g, w_inT, n_cols, rows_per_example, name, after=None, sender=None):
    rows = xt.shape[0]
    tm = min(TOKEN_TILE, rows_per_example)
    per_b = rows_per_example // tm
    shared_mod = modv.shape[0] == 1

    def body(x_ref, mod_ref, g_ref, w_ref, p_ref, h_ref):
        x = x_ref[...]
        r = lax.rsqrt(jnp.mean(x * x, axis=-1, keepdims=True) + EPS)
        h = (x * r * g_ref[...]) * (1.0 + mod_ref[0, 1:2, :]) + mod_ref[0, 0:1, :]
        hb = h.astype(MXU_DTYPE)
        h_ref[...] = hb
        for j in range(n_cols // KW):
            p_ref[:, j * KW:(j + 1) * KW] = _dot(hb, w_ref[j * KW:(j + 1) * KW, :], "nt").astype(p_ref.dtype)

    mod_idx = (lambda i: (0, 0, 0)) if shared_mod else (lambda i: (i // per_b, 0, 0))
    in_specs = [pl.BlockSpec((tm, D), lambda i: (i, 0)), pl.BlockSpec((1, N_MOD, D), mod_idx), _full((1, D)),
                pl.BlockSpec((n_cols, D), lambda i: (0, 0), pipeline_mode=pl.Buffered(1))]
    (p, h), handle = _host_call(
        body, name, (rows // tm,), in_specs, [xt, modv, g, w_inT],
        [jax.ShapeDtypeStruct((rows, n_cols), MXU_DTYPE), jax.ShapeDtypeStruct((rows, D), MXU_DTYPE)],
        [pl.BlockSpec((tm, n_cols), lambda i: (i, 0)), pl.BlockSpec((tm, D), lambda i: (i, 0))], [],
        after=after, sender=sender)
    return p, h, handle


def _tri(reverse, n):
    row = lax.broadcasted_iota(jnp.int32, (n, n), 0)
    col = lax.broadcasted_iota(jnp.int32, (n, n), 1)
    same = (row // CHUNK) == (col // CHUNK)
    return same & ((col >= row) if reverse else (col <= row))


def _per_chunk_rows(x, reverse):
    n = x.shape[0]
    rows = [x[j * CHUNK:j * CHUNK + 1] if reverse else x[(j + 1) * CHUNK - 1:(j + 1) * CHUNK] for j in range(n // CHUNK)]
    return jnp.concatenate([jnp.broadcast_to(r, (CHUNK, x.shape[1])) for r in rows], axis=0), rows


def _lower_bound(gam_ref, direction):
    return _sigmoid(gam_ref[direction:direction + 1, :] - gam_ref[2 + direction:3 + direction, :])


def _gate_prep(z, lb, tri, reverse):
    sg = _sigmoid(z)
    f = lb + (1.0 - lb) * sg
    g = jnp.log(f)
    b = _mask_dot(tri, g)
    bl, bl_rows = _per_chunk_rows(b, reverse)
    mid = 0.5 * bl
    return sg, g, 1.0 - f, b, jnp.exp(mid), [jnp.exp(0.5 * r) for r in bl_rows], jnp.exp(mid - b), mid


def _hgrn_fwd(p, gam, s0, rows_per_example, with_out, name, sender=None):
    rows = p.shape[0]
    nb_ex = rows // rows_per_example
    rb = min(TOKEN_TILE, rows_per_example)
    cpb = rb // CHUNK
    nb = rows_per_example // rb
    n_chunks = rows // CHUNK
    has_s0 = s0 is not None

    def body(*refs):
        it = iter(refs)
        gam_ref = next(it)
        zf_ref, vf_ref = next(it), next(it)
        qf_ref = next(it) if with_out else None
        zb_ref, vb_ref = next(it), next(it)
        qb_ref = next(it) if with_out else None
        s0_ref = next(it) if has_s0 else None
        if with_out:
            of_ref, ob_ref = next(it), next(it)
        stash_f, stash_b, fin_ref = next(it), next(it), next(it)
        st_ref = next(it)
        i = pl.program_id(1)

        @pl.when(i == 0)
        def _():
            if has_s0:
                st_ref[...] = s0_ref[:, 0]
            else:
                st_ref[...] = jnp.zeros_like(st_ref)

        for direction, (z_ref, v_ref, q_ref, stash) in enumerate(
                ((zf_ref, vf_ref, qf_ref, stash_f), (zb_ref, vb_ref, qb_ref, stash_b))):
            reverse = direction == 1
            tri = _tri(reverse, rb)
            lb = _lower_bound(gam_ref, direction)
            z = z_ref[...].astype(F32)
            v = v_ref[...].astype(F32)
            _, _, k, b, em, em_rows, e2, mid = _gate_prep(z, lb, tri, reverse)
            kd = (k * (e2 * em)).astype(MXU_DTYPE)
            vb = v.astype(MXU_DTYPE)
            if with_out:
                q = q_ref[...].astype(F32)
                qi = q * jnp.exp(b - mid)
                qe = (qi * em).astype(MXU_DTYPE)
                qi = qi.astype(MXU_DTYPE)
                ki = (k * e2).astype(MXU_DTYPE)
                intra = []
                for h in range(HEADS):
                    hs = slice(h * DK, (h + 1) * DK)
                    sc = jnp.where(tri, _dot(qi[:, hs], ki[:, hs], "nt"), 0.0)
                    intra.append(_dot(sc, vb[:, hs]))
            for j in (range(cpb - 1, -1, -1) if reverse else range(cpb)):
                rs = slice(j * CHUNK, (j + 1) * CHUNK)
                a = em_rows[j] * em_rows[j]
                for h in range(HEADS):
                    hs = slice(h * DK, (h + 1) * DK)
                    st = st_ref[direction, h]
                    stash[j, h] = st.astype(stash.dtype)
                    if with_out:
                        (ob_ref if reverse else of_ref)[rs, hs] = intra[h][rs] + _dot(qe[rs, hs], st, "nt")
                    st_ref[direction, h] = st * a[:, hs] + _dot(vb[rs, hs], kd[rs, hs], "tn")

        @pl.when(i == nb - 1)
        def _():
            fin_ref[:, 0] = st_ref[...]

    up = lambda b, i: b * nb + i
    down = lambda b, i: b * nb + nb - 1 - i
    col = lambda rowf, c: pl.BlockSpec((rb, KW), lambda b, i: (rowf(b, i), c))
    in_specs = [_full((4, KW)), col(up, 0), col(up, 2)] + ([col(up, 3)] if with_out else [])
    in_specs += [col(down, 1), col(down, 2)] + ([col(down, 3)] if with_out else [])
    args = [gam, p, p] + ([p] if with_out else []) + [p, p] + ([p] if with_out else [])
    if has_s0:
        in_specs.append(pl.BlockSpec((2, 1, HEADS, DK, DK), lambda b, i: (0, b, 0, 0, 0)))
        args.append(s0)
    out_shape, out_specs = [], []
    if with_out:
        out_shape += [jax.ShapeDtypeStruct((rows, KW), F32)] * 2
        out_specs += [pl.BlockSpec((rb, KW), lambda b, i: (up(b, i), 0)),
                      pl.BlockSpec((rb, KW), lambda b, i: (down(b, i), 0))]
    out_shape += [jax.ShapeDtypeStruct((n_chunks, HEADS, DK, DK), MXU_DTYPE)] * 2
    out_specs += [pl.BlockSpec((cpb, HEADS, DK, DK), lambda b, i: (up(b, i), 0, 0, 0)),
                  pl.BlockSpec((cpb, HEADS, DK, DK), lambda b, i: (down(b, i), 0, 0, 0))]
    out_shape.append(jax.ShapeDtypeStruct((2, nb_ex, HEADS, DK, DK), F32))
    out_specs.append(pl.BlockSpec((2, 1, HEADS, DK, DK), lambda b, i: (0, b, 0, 0, 0)))
    res, handle = _host_call(body, name, (nb_ex, nb), in_specs, args, out_shape, out_specs,
                             [pltpu.VMEM((2, HEADS, DK, DK), F32)], sender=sender)
    return (*res, handle)


def _hgrn_bwd(p, gam, do, stash_f, stash_b, ds_end, rows_per_example, with_out, name, after=None, sender=None):
    rows = p.shape[0]
    nb_ex = rows // rows_per_example
    rb = min(TOKEN_TILE, rows_per_example)
    cpb = rb // CHUNK
    nb = rows_per_example // rb
    has_end = ds_end is not None

    def body(*refs):
        it = iter(refs)
        gam_ref = next(it)
        ins = []
        for _ in range(2):
            z_ref, v_ref = next(it), next(it)
            q_ref = next(it) if with_out else None
            do_ref = next(it) if with_out else None
            ins.append((z_ref, v_ref, q_ref, do_ref, next(it)))
        end_ref = next(it) if has_end else None
        outs = [next(it), next(it)]
        dlb_ref, ds0_ref = next(it), next(it)
        dst_ref = next(it)
        b_id, i = pl.program_id(0), pl.program_id(1)

        @pl.when(i == 0)
        def _():
            if has_end:
                dst_ref[...] = end_ref[:, 0]
            else:
                dst_ref[...] = jnp.zeros_like(dst_ref)

        @pl.when((i == 0) & (b_id == 0))
        def _():
            dlb_ref[...] = jnp.zeros_like(dlb_ref)

        for direction in range(2):
            z_ref, v_ref, q_ref, do_ref, stash = ins[direction]
            dgrp_ref = outs[direction]
            reverse = direction == 1
            tri = _tri(reverse, rb)
            tri_t = _tri(not reverse, rb)
            lb = _lower_bound(gam_ref, direction)
            heads = [slice(h * DK, (h + 1) * DK) for h in range(HEADS)]
            chunks = [slice(j * CHUNK, (j + 1) * CHUNK) for j in range(cpb)]
            grid_cat = lambda parts: jnp.concatenate([jnp.concatenate(row, axis=1) for row in parts], axis=0)
            cat = lambda parts: jnp.concatenate(parts, axis=1)
            z = z_ref[...].astype(F32)
            sg, g, k, b, em, em_rows, e2, mid = _gate_prep(z, lb, tri, reverse)
            e3 = e2 * em
            kd = k * e3
            kd_b = kd.astype(MXU_DTYPE)
            vb = v_ref[...].astype(MXU_DTYPE)
            if with_out:
                q = q_ref[...].astype(F32)
                dout = do_ref[...].astype(MXU_DTYPE)
                e1 = jnp.exp(b - mid)
                e4 = e1 * em
                qi, ki, qe = q * e1, k * e2, q * e4
                qi_b, ki_b, qe_b = qi.astype(MXU_DTYPE), ki.astype(MXU_DTYPE), qe.astype(MXU_DTYPE)
                dqi_p, dki_p, dv_p = [], [], []
                for hs in heads:
                    sc = jnp.where(tri, _dot(qi_b[:, hs], ki_b[:, hs], "nt"), 0.0)
                    dsc = jnp.where(tri, _dot(dout[:, hs], vb[:, hs], "nt"), 0.0)
                    dqi_p.append(_dot(dsc, ki_b[:, hs]))
                    dki_p.append(_dot(dsc, qi_b[:, hs], "tn"))
                    dv_p.append(_dot(sc, dout[:, hs], "tn"))
                dqi, dki, dv = cat(dqi_p), cat(dki_p), cat(dv_p)
                dqe = grid_cat([[_dot(dout[rs, hs], stash[j, h]) for h, hs in enumerate(heads)]
                                for j, rs in enumerate(chunks)])
                grow = [[_dot(dout[rs, hs], qe_b[rs, hs], "tn") for hs in heads] for rs in chunks]
            dkd_p = [[None] * HEADS for _ in range(cpb)]
            dvs_p = [[None] * HEADS for _ in range(cpb)]
            da_p = [[None] * HEADS for _ in range(cpb)]
            for j in (range(cpb) if reverse else range(cpb - 1, -1, -1)):
                rs = chunks[j]
                a = em_rows[j] * em_rows[j]
                for h, hs in enumerate(heads):
                    dst = dst_ref[direction, h]
                    dkd_p[j][h] = _dot(vb[rs, hs], dst)
                    dvs_p[j][h] = _dot(kd_b[rs, hs], dst, "nt")
                    da_p[j][h] = jnp.broadcast_to(
                        jnp.sum(dst * stash[j, h].astype(F32), axis=0, keepdims=True), (CHUNK, DK))
                    new_dst = dst * a[:, hs]
                    dst_ref[direction, h] = new_dst + grow[j][h] if with_out else new_dst
            dkd, dvs, da = grid_cat(dkd_p), grid_cat(dvs_p), grid_cat(da_p)
            t_kd = dkd * kd
            dk = dkd * e3
            db = -t_kd
            tot = t_kd
            if with_out:
                dgrp_ref[:, KW:2 * KW] = (dvs + dv).astype(dgrp_ref.dtype)
                dgrp_ref[:, 2 * KW:] = (dqi * e1 + dqe * e4).astype(dgrp_ref.dtype)
                dk = dk + dki * e2
                t_qi, t_ki, t_qe = dqi * qi, dki * ki, dqe * qe
                db = db + t_qi - t_ki + t_qe
                tot = tot + 0.5 * (t_ki - t_qi)
            else:
                dgrp_ref[:, KW:2 * KW] = dvs.astype(dgrp_ref.dtype)
            dbl = jnp.concatenate([jnp.broadcast_to(jnp.sum(tot[rs], axis=0, keepdims=True), (CHUNK, KW))
                                   for rs in chunks], axis=0) + da * (em * em)
            dg = _mask_dot(tri_t, db) + dbl
            df = dg * jnp.exp(-g) - dk
            dgrp_ref[:, 0:KW] = (df * (1.0 - lb) * sg * (1.0 - sg)).astype(dgrp_ref.dtype)
            dlb_ref[direction:direction + 1, :] += jnp.sum(df * (1.0 - sg), axis=0, keepdims=True)

        @pl.when(i == nb - 1)
        def _():
            ds0_ref[:, 0] = dst_ref[...]

    rows_of = (lambda b, i: b * nb + nb - 1 - i, lambda b, i: b * nb + i)
    in_specs, args = [_full((4, KW))], [gam]
    for direction in range(2):
        rf = rows_of[direction]
        col = lambda c, rf=rf: pl.BlockSpec((rb, KW), lambda b, i: (rf(b, i), c))
        in_specs += [col(direction), col(2)]
        args += [p, p]
        if with_out:
            in_specs += [col(3), col(0)]
            args += [p, do]
        in_specs.append(pl.BlockSpec((cpb, HEADS, DK, DK), lambda b, i, rf=rf: (rf(b, i), 0, 0, 0)))
        args.append((stash_f, stash_b)[direction])
    if has_end:
        in_specs.append(pl.BlockSpec((2, 1, HEADS, DK, DK), lambda b, i: (0, b, 0, 0, 0)))
        args.append(ds_end)
    out_shape, out_specs = [], []
    for direction in range(2):
        rf = rows_of[direction]
        width = (3 if with_out else 2) * KW
        out_shape.append(jax.ShapeDtypeStruct((rows, width), MXU_DTYPE))
        out_specs.append(pl.BlockSpec((rb, width), lambda b, i, rf=rf: (rf(b, i), 0)))
    out_shape += [jax.ShapeDtypeStruct((2, KW), F32), jax.ShapeDtypeStruct((2, nb_ex, HEADS, DK, DK), F32)]
    out_specs += [_full((2, KW)), pl.BlockSpec((2, 1, HEADS, DK, DK), lambda b, i: (0, b, 0, 0, 0))]
    res, handle = _host_call(body, name, (nb_ex, nb), in_specs, args, out_shape, out_specs,
                             [pltpu.VMEM((2, HEADS, DK, DK), F32)], after=after, sender=sender)
    return (*res, handle)


def _tail_forward(osum, og, u, v, ga, gb, gna, ln_g, ln_b, ws_ref, bs_ref, wpaT_ref, wpbT_ref):
    tm = osum.shape[0]
    gna4 = jnp.concatenate([gna] * HEADS, axis=1)
    r_parts = []
    for h in range(HEADS):
        oh = osum[:, h * DK:(h + 1) * DK]
        r_parts.append(jnp.broadcast_to(lax.rsqrt(jnp.mean(oh * oh, axis=-1, keepdims=True) + EPS), (tm, DK)))
    r = jnp.concatenate(r_parts, axis=1)
    on = osum * r
    sg_og = _sigmoid(og)
    silu_og = og * sg_og
    oan = on * gna4
    oa = oan * silu_og
    ug, tu = _gelu(u)
    vg, tv = _gelu(v)
    mu = jnp.mean(vg, axis=-1, keepdims=True)
    vc = vg - mu
    rstd = lax.rsqrt(jnp.mean(vc * vc, axis=-1, keepdims=True) + EPS)
    vhat = vc * rstd
    vln = vhat * ln_g + ln_b
    blocks = []
    for n in range(tm // SGU_BLOCK):
        rs = slice(n * SGU_BLOCK, (n + 1) * SGU_BLOCK)
        blocks.append(jnp.concatenate(
            [_dot(ws_ref[g], vln[rs, g * DK:(g + 1) * DK]) + bs_ref[g] for g in range(GROUPS)], axis=1))
    mixed = jnp.concatenate(blocks, axis=0) if len(blocks) > 1 else blocks[0]
    obm = ug * mixed
    pa = _dot(oa, wpaT_ref[...], "nt")
    pb = _dot(obm, wpbT_ref[...], "nt")
    sga, sgb = _sigmoid(ga), _sigmoid(gb)
    merged = sga * pa + sgb * pb
    return dict(r=r, on=on, sg_og=sg_og, silu_og=silu_og, oan=oan, oa=oa, ug=ug, tu=tu, tv=tv, rstd=rstd, vhat=vhat,
                vln=vln, mixed=mixed, obm=obm, pa=pa, pb=pb, sga=sga, sgb=sgb, merged=merged, gna4=gna4)


def _tail_in_specs(tm):
    tile = lambda c: pl.BlockSpec((tm, KW), lambda i: (i, c))
    return [tile(c) for c in range(4, 11)]


def _tail_weight_specs():
    return [_full((1, DK)), _full((1, KW)), _full((1, KW)), _full((GROUPS, SGU_BLOCK, SGU_BLOCK)),
            _full((GROUPS, SGU_BLOCK, 1)), _full((D, KW), single=True), _full((D, KW), single=True),
            _full((D, D), single=True)]


def _read_tail_inputs(of_ref, ob_ref, pcols):
    osum = of_ref[...] + ob_ref[...]
    og, u, v = (pcols[j][...].astype(F32) for j in range(3))
    ga = jnp.concatenate([pcols[3][...], pcols[4][...]], axis=1).astype(F32)
    gb = jnp.concatenate([pcols[5][...], pcols[6][...]], axis=1).astype(F32)
    return osum, og, u, v, ga, gb


def _tail_fwd(p, o_up, o_down, xt, modv, gna, ln_g, ln_b, w_s, b_s, w_paT, w_pbT, w_o, rows_per_example):
    rows = xt.shape[0]
    tm = min(TOKEN_TILE, rows_per_example)
    per_b = rows_per_example // tm

    def body(of_ref, ob_ref, *rest):
        pcols = rest[:7]
        (x_ref, mod_ref, gna_ref, lng_ref, lnb_ref, ws_ref, bs_ref, wpaT_ref, wpbT_ref, wo_ref,
         x1_ref, mix_ref, merged_ref, oa_ref, obm_ref) = rest[7:]
        t = _tail_forward(*_read_tail_inputs(of_ref, ob_ref, pcols), gna_ref[...], lng_ref[...], lnb_ref[...],
                          ws_ref, bs_ref, wpaT_ref, wpbT_ref)
        mix = _dot(t["merged"], wo_ref[...])
        x1_ref[...] = x_ref[...] + mod_ref[0, 2:3, :] * mix
        mix_ref[...] = mix.astype(mix_ref.dtype)
        merged_ref[...] = t["merged"].astype(merged_ref.dtype)
        oa_ref[...] = t["oa"].astype(oa_ref.dtype)
        obm_ref[...] = t["obm"].astype(obm_ref.dtype)

    row = lambda w: pl.BlockSpec((tm, w), lambda i: (i, 0))
    in_specs = [row(KW), row(KW)] + _tail_in_specs(tm) + [row(D), pl.BlockSpec((1, N_MOD, D), lambda i: (i // per_b, 0, 0))]
    in_specs += _tail_weight_specs()
    return pl.pallas_call(
        body, name="tail_fwd", grid=(rows // tm,),
        out_shape=(jax.ShapeDtypeStruct((rows, D), F32), jax.ShapeDtypeStruct((rows, D), MXU_DTYPE),
                   jax.ShapeDtypeStruct((rows, D), MXU_DTYPE), jax.ShapeDtypeStruct((rows, KW), MXU_DTYPE),
                   jax.ShapeDtypeStruct((rows, KW), MXU_DTYPE)),
        in_specs=in_specs, out_specs=(row(D), row(D), row(D), row(KW), row(KW)),
        compiler_params=_params(("arbitrary",)),
    )(o_up, o_down, *([p] * 7), xt, modv, gna, ln_g, ln_b, w_s, b_s, w_paT, w_pbT, w_o)


def _tail_bwd(p, o_up, o_down, dx1, mix, modv, gna, ln_g, ln_b, w_s, b_s, w_paT, w_pbT, w_o, rows_per_example,
              after=None, sender=None):
    rows = dx1.shape[0]
    nb_ex = rows // rows_per_example
    tm = min(TOKEN_TILE, rows_per_example)
    per_b = rows_per_example // tm

    def body(of_ref, ob_ref, *rest):
        pcols = rest[:7]
        (dx1_ref, mix_ref, mod_ref, gna_ref, lng_ref, lnb_ref, ws_ref, bs_ref, wpaT_ref, wpbT_ref, wo_ref,
         dpt_ref, do_ref, dmix_ref, dpa_ref, dpb_ref, dmod_ref, small_ref, dws_ref, dbs_ref) = rest[7:]
        i = pl.program_id(0)

        @pl.when(i == 0)
        def _():
            small_ref[...] = jnp.zeros_like(small_ref)
            dws_ref[...] = jnp.zeros_like(dws_ref)
            dbs_ref[...] = jnp.zeros_like(dbs_ref)

        @pl.when(i % per_b == 0)
        def _():
            dmod_ref[...] = jnp.zeros_like(dmod_ref)

        osum, og, u, v, ga, gb = _read_tail_inputs(of_ref, ob_ref, pcols)
        ln_g = lng_ref[...]
        t = _tail_forward(osum, og, u, v, ga, gb, gna_ref[...], ln_g, lnb_ref[...], ws_ref, bs_ref, wpaT_ref, wpbT_ref)
        dx1v = dx1_ref[...]
        dmod_ref[0, 2:3, :] += jnp.sum(dx1v * mix_ref[...].astype(F32), axis=0, keepdims=True)
        dmix = dx1v * mod_ref[0, 2:3, :]
        dmix_ref[...] = dmix.astype(dmix_ref.dtype)
        dmerged = _dot(dmix, wo_ref[...], "nt")
        sga, sgb = t["sga"], t["sgb"]
        dpa = dmerged * sga
        dpb = dmerged * sgb
        dpa_ref[...] = dpa.astype(dpa_ref.dtype)
        dpb_ref[...] = dpb.astype(dpb_ref.dtype)
        dga = dmerged * t["pa"] * sga * (1.0 - sga)
        dgb = dmerged * t["pb"] * sgb * (1.0 - sgb)
        doa = _dot(dpa, wpaT_ref[...])
        dobm = _dot(dpb, wpbT_ref[...])
        dug = dobm * t["mixed"]
        dmixed = dobm * t["ug"]
        du = dug * _gelu_grad(u, t["tu"])
        dvln_blocks = []
        for n in range(tm // SGU_BLOCK):
            rs = slice(n * SGU_BLOCK, (n + 1) * SGU_BLOCK)
            parts = []
            for g in range(GROUPS):
                gs = slice(g * DK, (g + 1) * DK)
                dm = dmixed[rs, gs]
                parts.append(_dot(ws_ref[g], dm, "tn"))
                dws_ref[g] += _dot(dm, t["vln"][rs, gs], "nt")
                dbs_ref[g] += jnp.sum(dm, axis=1, keepdims=True)
            dvln_blocks.append(jnp.concatenate(parts, axis=1))
        dvln = jnp.concatenate(dvln_blocks, axis=0) if len(dvln_blocks) > 1 else dvln_blocks[0]
        vhat = t["vhat"]
        small_ref[1:2, 0:KW] += jnp.sum(dvln * vhat, axis=0, keepdims=True)
        small_ref[2:3, 0:KW] += jnp.sum(dvln, axis=0, keepdims=True)
        dvhat = dvln * ln_g
        dvg = t["rstd"] * (dvhat - jnp.mean(dvhat, axis=-1, keepdims=True)
                           - vhat * jnp.mean(dvhat * vhat, axis=-1, keepdims=True))
        dv = dvg * _gelu_grad(v, t["tv"])
        sg_og = t["sg_og"]
        doan = doa * t["silu_og"]
        dog = doa * t["oan"] * (sg_og * (1.0 + og * (1.0 - sg_og)))
        prod = doan * t["on"]
        dgna = jnp.zeros((1, DK), F32)
        for h in range(HEADS):
            dgna = dgna + jnp.sum(prod[:, h * DK:(h + 1) * DK], axis=0, keepdims=True)
        small_ref[0:1, 0:DK] += dgna
        don = doan * t["gna4"]
        dot_parts = []
        for h in range(HEADS):
            hs = slice(h * DK, (h + 1) * DK)
            m = jnp.mean(don[:, hs] * t["on"][:, hs], axis=-1, keepdims=True)
            dot_parts.append(t["r"][:, hs] * (don[:, hs] - t["on"][:, hs] * m))
        do_ref[...] = jnp.concatenate(dot_parts, axis=1).astype(do_ref.dtype)
        for j, val in enumerate((dog, du, dv)):
            dpt_ref[:, j * KW:(j + 1) * KW] = val.astype(dpt_ref.dtype)
        dpt_ref[:, 3 * KW:3 * KW + D] = dga.astype(dpt_ref.dtype)
        dpt_ref[:, 3 * KW + D:] = dgb.astype(dpt_ref.dtype)

    row = lambda w: pl.BlockSpec((tm, w), lambda i: (i, 0))
    in_specs = [row(KW), row(KW)] + _tail_in_specs(tm) + [row(D), row(D), pl.BlockSpec((1, N_MOD, D), lambda i: (i // per_b, 0, 0))]
    in_specs += _tail_weight_specs()
    args = [o_up, o_down, *([p] * 7), dx1, mix, modv, gna, ln_g, ln_b, w_s, b_s, w_paT, w_pbT, w_o]
    cd = MXU_DTYPE
    res, handle = _host_call(
        body, "tail_bwd", (rows // tm,), in_specs, args,
        [jax.ShapeDtypeStruct((rows, TAIL_COLS), cd), jax.ShapeDtypeStruct((rows, KW), cd),
         jax.ShapeDtypeStruct((rows, D), cd), jax.ShapeDtypeStruct((rows, D), cd),
         jax.ShapeDtypeStruct((rows, D), cd), jax.ShapeDtypeStruct((nb_ex, 8, D), F32),
         jax.ShapeDtypeStruct((8, D), F32), jax.ShapeDtypeStruct((GROUPS, SGU_BLOCK, SGU_BLOCK), F32),
         jax.ShapeDtypeStruct((GROUPS, SGU_BLOCK, 1), F32)],
        [row(TAIL_COLS), row(KW), row(D), row(D), row(D),
         pl.BlockSpec((1, 8, D), lambda i: (i // per_b, 0, 0)), _full((8, D)),
         _full((GROUPS, SGU_BLOCK, SGU_BLOCK)), _full((GROUPS, SGU_BLOCK, 1))], [],
        after=after, sender=sender)
    return (*res, handle)


def _ffn(x1, target, modv, g_ffn, g_final, w_upT, w_down, rows_per_example):
    rows = x1.shape[0]
    nb_ex = rows // rows_per_example
    tm = min(TOKEN_TILE, rows_per_example)
    per_b = rows_per_example // tm
    n_ff = D_FF // FF_CHUNK

    def body(x1_ref, tgt_ref, mod_ref, gffn_ref, gfin_ref, wup_ref, wdn_ref,
             dx1_ref, h2_ref, dffn_ref, act_ref, dup_ref, dmod_ref, small_ref, up_scr):
        i = pl.program_id(0)

        @pl.when(i == 0)
        def _():
            small_ref[...] = jnp.zeros_like(small_ref)

        @pl.when(i % per_b == 0)
        def _():
            dmod_ref[...] = jnp.zeros_like(dmod_ref)

        x1v = x1_ref[...]
        g2 = gffn_ref[...]
        m3, m4, m5 = mod_ref[0, 3:4, :], mod_ref[0, 4:5, :], mod_ref[0, 5:6, :]
        r2 = lax.rsqrt(jnp.mean(x1v * x1v, axis=-1, keepdims=True) + EPS)
        xn2 = x1v * r2
        h2 = (xn2 * g2) * (1.0 + m4) + m3
        h2b = h2.astype(MXU_DTYPE)
        h2_ref[...] = h2b
        def up_pair(j):
            lo = j * FF_CHUNK
            return (_dot(h2b, wup_ref[lo:lo + FF_CHUNK, :], "nt"),
                    _dot(h2b, wup_ref[D_FF + lo:D_FF + lo + FF_CHUNK, :], "nt"))

        group_end = {min(e, n_ff): s for s, e in ((0, 4), (4, 8), (8, 12))}
        cur, ffn = up_pair(0), None
        for j in range(n_ff):
            nxt = up_pair(j + 1) if j + 1 < n_ff else None
            cs = slice(j * FF_CHUNK, (j + 1) * FF_CHUNK)
            a, bgate = cur
            up_scr[:, cs] = a
            up_scr[:, D_FF + j * FF_CHUNK:D_FF + (j + 1) * FF_CHUNK] = bgate
            act_ref[:, cs] = (a * _sigmoid(a) * bgate).astype(MXU_DTYPE)
            cur = nxt
            if j + 1 in group_end:
                gs = slice(group_end[j + 1] * FF_CHUNK, (j + 1) * FF_CHUNK)
                part = _dot(act_ref[:, gs], wdn_ref[gs, :])
                ffn = part if ffn is None else ffn + part
        x2 = x1v + m5 * ffn
        r3 = lax.rsqrt(jnp.mean(x2 * x2, axis=-1, keepdims=True) + EPS)
        xn3 = x2 * r3
        gf = gfin_ref[...]
        err = xn3 * gf - tgt_ref[...]
        loss = 0.5 * jnp.sum(jnp.mean(err * err, axis=-1, keepdims=True), axis=0, keepdims=True)
        small_ref[2:3, :] += jnp.broadcast_to(loss, (1, D))
        dy = err * (1.0 / D)
        small_ref[1:2, :] += jnp.sum(dy * xn3, axis=0, keepdims=True)
        dxn3 = dy * gf
        dx2 = r3 * (dxn3 - xn3 * jnp.mean(dxn3 * xn3, axis=-1, keepdims=True))
        dmod_ref[0, 5:6, :] += jnp.sum(dx2 * ffn, axis=0, keepdims=True)
        dffn = (dx2 * m5).astype(MXU_DTYPE)
        dffn_ref[...] = dffn
        dact_of = lambda j: _dot(dffn, wdn_ref[j * FF_CHUNK:(j + 1) * FF_CHUNK, :], "nt")
        cur, dh2 = dact_of(0), None
        for j in range(n_ff):
            nxt = dact_of(j + 1) if j + 1 < n_ff else None
            cs = slice(j * FF_CHUNK, (j + 1) * FF_CHUNK)
            a, bgate = up_scr[:, cs], up_scr[:, D_FF + j * FF_CHUNK:D_FF + (j + 1) * FF_CHUNK]
            s = _sigmoid(a)
            dup_ref[:, cs] = (cur * bgate * (s * (1.0 + a * (1.0 - s)))).astype(MXU_DTYPE)
            dup_ref[:, D_FF + j * FF_CHUNK:D_FF + (j + 1) * FF_CHUNK] = (cur * a * s).astype(MXU_DTYPE)
            cur = nxt
            if j + 1 in group_end:
                lo, hi = group_end[j + 1] * FF_CHUNK, (j + 1) * FF_CHUNK
                part = (_dot(dup_ref[:, lo:hi], wup_ref[lo:hi, :])
                        + _dot(dup_ref[:, D_FF + lo:D_FF + hi], wup_ref[D_FF + lo:D_FF + hi, :]))
                dh2 = part if dh2 is None else dh2 + part
        dmod_ref[0, 3:4, :] += jnp.sum(dh2, axis=0, keepdims=True)
        dmod_ref[0, 4:5, :] += jnp.sum(dh2 * xn2 * g2, axis=0, keepdims=True)
        small_ref[0:1, :] += jnp.sum(dh2 * (1.0 + m4) * xn2, axis=0, keepdims=True)
        dxn2 = dh2 * g2 * (1.0 + m4)
        dx1_ref[...] = dx2 + r2 * (dxn2 - xn2 * jnp.mean(dxn2 * xn2, axis=-1, keepdims=True))

    row = lambda w: pl.BlockSpec((tm, w), lambda i: (i, 0))
    cd = MXU_DTYPE
    return pl.pallas_call(
        body, name="ffn_fwd_bwd", grid=(rows // tm,),
        out_shape=(jax.ShapeDtypeStruct((rows, D), F32), jax.ShapeDtypeStruct((rows, D), cd),
                   jax.ShapeDtypeStruct((rows, D), cd), jax.ShapeDtypeStruct((rows, D_FF), cd),
                   jax.ShapeDtypeStruct((rows, 2 * D_FF), cd), jax.ShapeDtypeStruct((nb_ex, 8, D), F32),
                   jax.ShapeDtypeStruct((8, D), F32)),
        in_specs=[row(D), row(D), pl.BlockSpec((1, N_MOD, D), lambda i: (i // per_b, 0, 0)), _full((1, D)), _full((1, D)),
                  _full((2 * D_FF, D), single=True), _full((D_FF, D), single=True)],
        out_specs=(row(D), row(D), row(D), row(D_FF), row(2 * D_FF),
                   pl.BlockSpec((1, 8, D), lambda i: (i // per_b, 0, 0)), _full((8, D))),
        scratch_shapes=[pltpu.VMEM((tm, 2 * D_FF), F32)],
        compiler_params=_params(("arbitrary",)),
    )(x1, target, modv, g_ffn, g_final, w_upT, w_down)


def _scan_columns(up, down, n_groups):
    cols = [up[:, 0:KW].astype(F32), down[:, 0:KW].astype(F32)]
    for j in range(1, n_groups):
        cols.append(up[:, j * KW:(j + 1) * KW].astype(F32) + down[:, j * KW:(j + 1) * KW].astype(F32))
    return cols


def _inproj_bwd(d_up, d_down, dpt, xt, dx1, modv, g, w_inT, rows_per_example, name, sender=None):
    rows = xt.shape[0]
    latent = dx1 is not None
    n_cols = IN_COLS if latent else CTX_COLS
    n_groups = d_up.shape[1] // KW
    tm = min(TOKEN_TILE, rows_per_example)
    per_b = rows_per_example // tm
    n_mod_blocks = rows // rows_per_example if latent else 1

    def body(*refs):
        it = iter(refs)
        up_ref, down_ref = next(it), next(it)
        dpt_ref = next(it) if latent else None
        x_ref = next(it)
        dx1_ref = next(it) if latent else None
        mod_ref, g_ref, w_ref = next(it), next(it), next(it)
        gx_ref = next(it) if latent else None
        dp_out = None if latent else next(it)
        dmod_ref, small_ref = next(it), next(it)
        dp_ref = next(it) if latent else dp_out
        i = pl.program_id(0)

        @pl.when(i == 0)
        def _():
            small_ref[...] = jnp.zeros_like(small_ref)

        @pl.when((i % per_b == 0) if latent else (i == 0))
        def _():
            dmod_ref[...] = jnp.zeros_like(dmod_ref)

        for j, val in enumerate(_scan_columns(up_ref[...], down_ref[...], n_groups)):
            dp_ref[:, j * KW:(j + 1) * KW] = val.astype(MXU_DTYPE)
        if latent:
            dh = _dot(dp_ref[...], w_ref[0:4 * KW, :]) + _dot(dpt_ref[...], w_ref[4 * KW:, :])
        else:
            dh = _dot(dp_ref[...], w_ref[...])
        x = x_ref[...]
        gv = g_ref[...]
        m1 = mod_ref[0, 1:2, :]
        r = lax.rsqrt(jnp.mean(x * x, axis=-1, keepdims=True) + EPS)
        xn = x * r
        dmod_ref[0, 0:1, :] += jnp.sum(dh, axis=0, keepdims=True)
        dmod_ref[0, 1:2, :] += jnp.sum(dh * xn * gv, axis=0, keepdims=True)
        small_ref[0:1, :] += jnp.sum(dh * (1.0 + m1) * xn, axis=0, keepdims=True)
        if latent:
            dxn = dh * gv * (1.0 + m1)
            gx_ref[...] = dx1_ref[...] + r * (dxn - xn * jnp.mean(dxn * xn, axis=-1, keepdims=True))

    row = lambda w: pl.BlockSpec((tm, w), lambda i: (i, 0))
    mod_idx = (lambda i: (i // per_b, 0, 0)) if latent else (lambda i: (0, 0, 0))
    in_specs = [row(n_groups * KW)] * 2 + ([row(TAIL_COLS)] if latent else []) + [row(D)] + ([row(D)] if latent else [])
    in_specs += [pl.BlockSpec((1, N_MOD, D), mod_idx), _full((1, D)),
                 pl.BlockSpec((n_cols, D), lambda i: (0, 0), pipeline_mode=pl.Buffered(1))]
    args = [d_up, d_down] + ([dpt] if latent else []) + [xt] + ([dx1] if latent else []) + [modv, g, w_inT]
    first = jax.ShapeDtypeStruct((rows, D), F32) if latent else jax.ShapeDtypeStruct((rows, n_cols), MXU_DTYPE)
    out_shape = [first, jax.ShapeDtypeStruct((n_mod_blocks, 8, D), F32), jax.ShapeDtypeStruct((8, D), F32)]
    out_specs = [row(D) if latent else row(n_cols), pl.BlockSpec((1, 8, D), mod_idx), _full((8, D))]
    scratch = [pltpu.VMEM((tm, 4 * KW), MXU_DTYPE)] if latent else []
    res, handle = _host_call(body, name, (rows // tm,), in_specs, args, out_shape, out_specs, scratch, sender=sender)
    return (*res, handle)


def _grad_matmul(a, b, name, init=None, tn=512, sender=None):
    rows, n = a.shape
    k = b.shape[1]
    tn = min(tn, n)
    has_init = init is not None
    init_blocks = init.shape[0] // tn if has_init else 0

    def body(*refs):
        if has_init:
            a_ref, b_ref, init_ref, o_ref = refs
        else:
            a_ref, b_ref, o_ref = refs
        g = _dot(a_ref[...], b_ref[...], "tn")
        if has_init:
            g = g + jnp.where(pl.program_id(0) < init_blocks, init_ref[...].astype(F32), 0.0)
        o_ref[...] = g.astype(o_ref.dtype)

    in_specs = [pl.BlockSpec((rows, tn), lambda i: (0, i)), _full((rows, k), single=True)]
    args = [a, b]
    if has_init:
        in_specs.append(pl.BlockSpec((tn, k), lambda i: (jnp.minimum(i, init_blocks - 1), 0)))
        args.append(init)
    (out,), handle = _host_call(
        body, name, (n // tn,), in_specs, args, [jax.ShapeDtypeStruct((n, k), PAYLOAD_DTYPE)],
        [pl.BlockSpec((tn, k), lambda i: (i, 0))], [], sender=sender)
    return out, handle


def _grad_in(d_up, d_down, dpt, h, init, sender=None):
    rows = h.shape[0]
    tn = 256
    per_group = KW // tn
    n_scan = 4 * per_group
    init_blocks = init.shape[0] // tn

    def body(up_ref, down_ref, dpt_ref, h_ref, init_ref, o_ref):
        i = pl.program_id(0)
        both = (up_ref[...].astype(F32) + down_ref[...].astype(F32)).astype(MXU_DTYPE)
        a = jnp.where(i < per_group, up_ref[...],
                      jnp.where(i < 2 * per_group, down_ref[...], jnp.where(i < n_scan, both, dpt_ref[...])))
        g = _dot(a, h_ref[...], "tn") + jnp.where(i < init_blocks, init_ref[...].astype(F32), 0.0)
        o_ref[...] = g.astype(o_ref.dtype)

    last = 3 * per_group - 1
    col = lambda f: pl.BlockSpec((rows, tn), lambda i: (0, f(i)))
    in_specs = [col(lambda i: jnp.clip(jnp.where(i < per_group, i, i - per_group), 0, last)),
                col(lambda i: jnp.clip(i - per_group, 0, last)),
                col(lambda i: jnp.clip(i - n_scan, 0, TAIL_COLS // tn - 1)),
                _full((rows, D), single=True),
                pl.BlockSpec((tn, D), lambda i: (jnp.minimum(i, init_blocks - 1), 0))]
    (out,), handle = _host_call(
        body, "gw_in", (IN_COLS // tn,), in_specs, [d_up, d_down, dpt, h, init],
        [jax.ShapeDtypeStruct((IN_COLS, D), PAYLOAD_DTYPE)], [pl.BlockSpec((tn, D), lambda i: (i, 0))], [],
        sender=sender)
    return out, handle


def _row_tile(rows, limit=256):
    if rows <= limit:
        return rows
    for t in range(limit, 7, -8):
        if rows % t == 0:
            return t
    return rows


def _sum8(stack, name):
    _, rows, cols = stack.shape
    tr = _row_tile(rows)

    def body(s_ref, o_ref):
        acc = s_ref[0].astype(F32)
        for j in range(1, N_DEV):
            acc = acc + s_ref[j].astype(F32)
        o_ref[...] = acc

    return pl.pallas_call(
        body, name=name, grid=(rows // tr,), out_shape=jax.ShapeDtypeStruct((rows, cols), F32),
        in_specs=[pl.BlockSpec((N_DEV, tr, cols), lambda i: (0, i, 0))],
        out_specs=pl.BlockSpec((tr, cols), lambda i: (i, 0)),
        compiler_params=_params(("arbitrary",)),
    )(stack)


def _adamw_update(w, gv, m, v):
    nm = ADAM_B1 * m + (1.0 - ADAM_B1) * gv
    nv = ADAM_B2 * v + (1.0 - ADAM_B2) * (gv * gv)
    m_hat = nm / (1.0 - ADAM_B1 ** ADAM_STEP)
    v_hat = nv / (1.0 - ADAM_B2 ** ADAM_STEP)
    return -ADAM_LR * (m_hat / (jnp.sqrt(v_hat) + ADAM_EPS) + ADAM_WD * w), nm, nv


SMALL_PARAMS = (("g_mix", 0, D), ("g_ffn", 1, D), ("g_final", 2, D), ("g_norm_a", 3, DK), ("ln_v_g", 4, KW),
                ("ln_v_b", 5, KW), ("b_s", 6, GROUPS * SGU_BLOCK))


def _small_finish(early, late, dws, gam, nb_ex, params):
    names = [n for n, _, _ in SMALL_PARAMS] + ["b_mod", "w_s"]

    def body(*refs):
        s_ref, l_ref, dws_ref, gam_ref = refs[:4]
        p_refs = refs[4:4 + 3 * len(names)]
        tot_ref, dgam_ref = refs[4 + 3 * len(names):6 + 3 * len(names)]
        o_refs = refs[6 + 3 * len(names):]
        acc = s_ref[0] + l_ref[0]
        gws = dws_ref[0]
        for j in range(1, N_DEV):
            acc = acc + (s_ref[j] + l_ref[j])
            gws = gws + dws_ref[j]
        tot_ref[...] = acc
        bm = acc[8:8 + N_MOD, :]
        for e in range(nb_ex):
            bm = bm + acc[16 + e * N_MOD:16 + (e + 1) * N_MOD, :]
        lb = jnp.concatenate([_lower_bound(gam_ref, 0), _lower_bound(gam_ref, 1)], axis=1)
        dgam = acc[7:8, :] * lb * (1.0 - lb)
        dgam_ref[...] = jnp.concatenate([dgam, -dgam], axis=0)
        grads = [acc[row:row + 1, 0:width] for _, row, width in SMALL_PARAMS] + [bm, gws]
        for k, g in enumerate(grads):
            w_ref, m_ref, v_ref = p_refs[3 * k:3 * k + 3]
            o_refs[4 * k][...] = g
            o_refs[4 * k + 1][...], o_refs[4 * k + 2][...], o_refs[4 * k + 3][...] = _adamw_update(
                w_ref[...], g, m_ref[...], v_ref[...])

    p_args, p_specs, o_shapes, o_specs = [], [], [], []
    for n in names:
        for a in params[n]:
            p_args.append(a)
            p_specs.append(_full(a.shape))
        o_shapes += [jax.ShapeDtypeStruct(params[n][0].shape, F32)] * 4
        o_specs += [_full(params[n][0].shape)] * 4
    res = pl.pallas_call(
        body, name="small_finish", grid=(1,),
        out_shape=[jax.ShapeDtypeStruct((SMALL_ROWS, D), F32), jax.ShapeDtypeStruct((2, D), F32)] + o_shapes,
        in_specs=[_full(early.shape), _full(late.shape), _full(dws.shape), _full((4, KW))] + p_specs,
        out_specs=[_full((SMALL_ROWS, D)), _full((2, D))] + o_specs,
        compiler_params=_params(("arbitrary",)),
    )(early, late, dws, gam, *p_args)
    return res[0], res[1], {n: res[2 + 4 * k:6 + 4 * k] for k, n in enumerate(names)}


def _adamw_sum8(stack, w, m, v, name):
    _, rows, cols = stack.shape
    tr = _row_tile(rows)

    def body(s_ref, w_ref, m_ref, v_ref, g_ref, d_ref, nm_ref, nv_ref):
        gv = s_ref[0].astype(F32)
        for j in range(1, N_DEV):
            gv = gv + s_ref[j].astype(F32)
        g_ref[...] = gv
        d_ref[...], nm_ref[...], nv_ref[...] = _adamw_update(w_ref[...], gv, m_ref[...], v_ref[...])

    blk = pl.BlockSpec((tr, cols), lambda i: (i, 0))
    sd = jax.ShapeDtypeStruct((rows, cols), F32)
    return pl.pallas_call(
        body, name=name, grid=(rows // tr,), out_shape=(sd, sd, sd, sd),
        in_specs=[pl.BlockSpec((N_DEV, tr, cols), lambda i: (0, i, 0)), blk, blk, blk], out_specs=(blk, blk, blk, blk),
        compiler_params=_params(("arbitrary",)),
    )(stack, w, m, v)


def _adamw(w, g, m, v, name):
    shape = w.shape
    cols = shape[-1]
    rows = 1
    for s in shape[:-1]:
        rows *= s
    tr = _row_tile(rows)

    def body(w_ref, g_ref, m_ref, v_ref, d_ref, nm_ref, nv_ref):
        gv = g_ref[...]
        nm = ADAM_B1 * m_ref[...] + (1.0 - ADAM_B1) * gv
        nv = ADAM_B2 * v_ref[...] + (1.0 - ADAM_B2) * (gv * gv)
        m_hat = nm / (1.0 - ADAM_B1 ** ADAM_STEP)
        v_hat = nv / (1.0 - ADAM_B2 ** ADAM_STEP)
        d_ref[...] = -ADAM_LR * (m_hat / (jnp.sqrt(v_hat) + ADAM_EPS) + ADAM_WD * w_ref[...])
        nm_ref[...] = nm
        nv_ref[...] = nv

    blk = pl.BlockSpec((tr, cols), lambda i: (i, 0))
    sd = jax.ShapeDtypeStruct((rows, cols), F32)
    d, nm, nv = pl.pallas_call(
        body, name=name, grid=(rows // tr,), out_shape=(sd, sd, sd), in_specs=[blk] * 4, out_specs=(blk, blk, blk),
        compiler_params=_params(("arbitrary",)),
    )(w.reshape(rows, cols), g.reshape(rows, cols), m.reshape(rows, cols), v.reshape(rows, cols))
    return d.reshape(shape), nm.reshape(shape), nv.reshape(shape)


def _owner_blocks(a):
    return a.reshape(N_DEV, a.shape[0] // N_DEV, a.shape[1])


class _LocalWeights:
    def __init__(self, w_upT, w_down, w_o, w_paT, w_pbT):
        self.weights = (w_upT, w_down, w_o, w_paT, w_pbT)
        self.items = {}

    def sender(self, stage, items=None):
        self.items[stage] = items
        return None

    def sent(self, stage, handle):
        pass

    def mixer_weights(self, after):
        return self.weights[1:]

    def ffn_weights(self, after):
        return self.weights[0]


def _local_step(x, ctx, target, modv, mcv, gam, g_mix, g_ffn, gna, ln_g, ln_b, w_s, b_s, g_final, w_inT, comm):
    nb_ex, seq, _ = x.shape
    ctx_len = ctx.shape[1]
    xt = x.reshape(nb_ex * seq, D)
    ct = ctx.reshape(nb_ex * ctx_len, D)
    tgt = target.reshape(nb_ex * seq, D)
    bs3 = b_s.reshape(GROUPS, SGU_BLOCK, 1)

    pc, hc, _ = _inproj(ct, mcv, g_mix, w_inT, CTX_COLS, ctx_len, "inproj_ctx")
    p, h, handle = _inproj(xt, modv, g_mix, w_inT, IN_COLS, seq, "inproj_lat", sender=comm.sender("inproj"))
    comm.sent("inproj", handle)
    cst_f, cst_b, s_ctx, _ = _hgrn_fwd(pc, gam, None, ctx_len, False, "hgrn_fwd_ctx")
    o_up, o_down, st_f, st_b, _, handle = _hgrn_fwd(p, gam, s_ctx, seq, True, "hgrn_fwd_lat",
                                                    sender=comm.sender("scan"))
    comm.sent("scan", handle)
    w_down, w_o, w_paT, w_pbT = comm.mixer_weights(o_up)
    x1, mix, merged, oa, obm = _tail_fwd(p, o_up, o_down, xt, modv, gna, ln_g, ln_b, w_s, bs3, w_paT, w_pbT, w_o, seq)
    w_upT = comm.ffn_weights(x1)
    dx1, h2, dffn, act, dup, dmod_ffn, small_ffn = _ffn(x1, tgt, modv, g_ffn, g_final, w_upT, w_down, seq)
    gw_upT, _ = _grad_matmul(dup, h2, "gw_up")
    gw_down, _ = _grad_matmul(act, dffn, "gw_down", tn=256)
    scatter = lambda *grads: [(_owner_blocks(g), "scatter") for g in grads]
    dpt, do, dmix, dpa, dpb, dmod_tail, small_tail, dws, dbs, handle = _tail_bwd(
        p, o_up, o_down, dx1, mix, modv, gna, ln_g, ln_b, w_s, bs3, w_paT, w_pbT, w_o, seq,
        sender=comm.sender("tail_bwd", scatter(gw_upT)))
    comm.sent("tail_bwd", handle)
    gw_o, _ = _grad_matmul(merged, dmix, "gw_o")
    gw_paT, _ = _grad_matmul(dpa, oa, "gw_pa")
    gw_pbT, _ = _grad_matmul(dpb, obm, "gw_pb")
    def at_row(row, a):
        return jnp.pad(a, ((row, SMALL_ROWS - row - a.shape[0]), (0, D - a.shape[1])))

    small_early = (at_row(1, small_ffn[0:2])
                   + at_row(3, small_tail[0:3])
                   + at_row(6, dbs.reshape(1, GROUPS * SGU_BLOCK))
                   + at_row(14, small_ffn[2:3]))
    dws_rows = dws.reshape(GROUPS * SGU_BLOCK, SGU_BLOCK)
    d_up, d_down, dlb, ds0, handle = _hgrn_bwd(
        p, gam, do, st_f, st_b, None, seq, True, "hgrn_bwd_lat",
        sender=comm.sender("scan_bwd", scatter(gw_down, gw_o, gw_paT, gw_pbT)
                           + [(small_early, "gather"), (dws_rows, "gather")]))
    comm.sent("scan_bwd", handle)
    c_up, c_down, dlb_c, _, _ = _hgrn_bwd(pc, gam, None, cst_f, cst_b, ds0, ctx_len, False, "hgrn_bwd_ctx")
    dpc, dmc, small_c, _ = _inproj_bwd(c_up, c_down, None, ct, None, mcv, g_mix, w_inT, ctx_len, "inproj_bwd_ctx")
    gw_inT, _ = _grad_in(d_up, d_down, dpt, h, _grad_matmul(dpc, hc, "gw_in_ctx")[0])
    grad_x, dmod_in, small_in, handle = _inproj_bwd(d_up, d_down, dpt, xt, dx1, modv, g_mix, w_inT, seq,
                                                   "inproj_bwd_lat", sender=comm.sender("inproj_bwd", scatter(gw_inT)))
    comm.sent("inproj_bwd", handle)
    dmod = dmod_in + dmod_tail + dmod_ffn
    small_late = (at_row(0, small_in[0:1] + small_c[0:1])
                  + at_row(7, (dlb + dlb_c).reshape(1, 2 * KW))
                  + at_row(8, dmc[0, 0:N_MOD])
                  + at_row(16, dmod[:, 0:N_MOD].reshape(nb_ex * N_MOD, D)))
    comm.sender("last", [(small_late, "gather")])
    return grad_x.reshape(x.shape)


def kernel(x, c, ctx, c_ctx, w_mod, b_mod, g_mix, g_ffn, w_in, lb_gamma, g_norm_a, ln_v_g, ln_v_b, w_s, b_s, w_pa, w_pb, w_o, w_up, w_down, g_final, loss_target, m_c_ctx, m_w_mod, m_b_mod, m_g_mix, m_g_ffn, m_w_in, m_lb_gamma, m_g_norm_a, m_ln_v_g, m_ln_v_b, m_w_s, m_b_s, m_w_pa, m_w_pb, m_w_o, m_w_up, m_w_down, m_g_final, v_c_ctx, v_w_mod, v_b_mod, v_g_mix, v_g_ffn, v_w_in, v_lb_gamma, v_g_norm_a, v_ln_v_g, v_ln_v_b, v_w_s, v_b_s, v_w_pa, v_w_pb, v_w_o, v_w_up, v_w_down, v_g_final):
    nb_ex = x.shape[0]
    me = 4 * lax.axis_index("x") + 2 * lax.axis_index("y") + lax.axis_index("c")
    cd = MXU_DTYPE
    mod_cols = w_mod.shape[2]
    lb_cols = lb_gamma.shape[2]

    w_inT_l = w_in[0].T.astype(cd)
    w_upT_l = w_up[0].T.astype(cd)
    w_paT_l = w_pa[0].T.astype(cd)
    w_pbT_l = w_pb[0].T.astype(cd)
    cl = jnp.concatenate([c, jnp.pad(lb_gamma.reshape(1, 4 * lb_cols), ((0, 0), (0, D - 4 * lb_cols))),
                          jnp.zeros((8 - nb_ex - 1, D), F32)], axis=0)
    g_in, g_cl = _gather_two_level([w_inT_l, cl], "gather_w_in")
    w_inT = g_in.reshape(IN_COLS, D)
    c_all = g_cl[:, 0:nb_ex].reshape(N_DEV * nb_ex, D)
    gam = jnp.transpose(g_cl[:, nb_ex, 0:4 * lb_cols].reshape(N_DEV, 4, lb_cols), (1, 0, 2)).reshape(4, KW)

    n_c = N_DEV * nb_ex
    cvec = jnp.concatenate([c_all, c_ctx.reshape(1, D), jnp.zeros((7, D), F32)], axis=0)
    b_mod_l = lax.dynamic_slice(b_mod, (0, me * mod_cols), (1, mod_cols))
    mod_l, svec = _mod_fwd(cvec, w_mod[0], b_mod_l)
    (g_mod,) = _gather_two_level([mod_l], "gather_mod")
    mod_all = jnp.transpose(g_mod, (1, 0, 2)).reshape(n_c + 8, N_MOD * D)
    modv = lax.dynamic_slice(mod_all, (me * nb_ex, 0), (nb_ex, N_MOD * D)).reshape(nb_ex, N_MOD, D)
    mcv = mod_all[n_c].reshape(1, N_MOD, D)

    handles, leftover = {}, {}

    class Comm:
        def sender(self, stage, items=None):
            if stage == "inproj":
                return _Sender([(w_down[0].astype(cd), "gather"), (w_o[0].astype(cd), "gather"), (w_paT_l, "gather"),
                                (w_pbT_l, "gather")])
            if stage == "scan":
                return _Sender([(w_upT_l, "gather")])
            if stage == "last":
                leftover["items"] = items
                return None
            return _Sender(items)

        def sent(self, stage, handle):
            handles[stage] = handle

        def mixer_weights(self, after):
            g_down, g_o, g_pa, g_pb = _exchange_wait(handles["inproj"], after)
            return g_down.reshape(D_FF, D), g_o.reshape(D, D), g_pa.reshape(D, KW), g_pb.reshape(D, KW)

        def ffn_weights(self, after):
            (g_up,) = _exchange_wait(handles["scan"], after)
            return g_up.reshape(2 * D_FF, D)

    grad_x = _local_step(
        x, ctx, loss_target, modv, mcv, gam, g_mix, g_ffn, g_norm_a, ln_v_g, ln_v_b, w_s[0], b_s[0],
        g_final.reshape(1, D), w_inT, Comm())
    last, last_started = _exchange_start(leftover["items"], "gather_small_late", after=leftover["items"][0][0])

    (r_up,) = _exchange_wait(handles["tail_bwd"], last_started)
    r_down, r_o, r_pa, r_pb, r_small, r_dws = _exchange_wait(handles["scan_bwd"], r_up)
    raw_up = _adamw_sum8(r_up, w_up[0].T, m_w_up[0].T, v_w_up[0].T, "adamw_w_up")
    raw_down = _adamw_sum8(r_down, w_down[0], m_w_down[0], v_w_down[0], "adamw_w_down")
    raw_o = _adamw_sum8(r_o, w_o[0], m_w_o[0], v_w_o[0], "adamw_w_o")
    (r_in,) = _exchange_wait(handles["inproj_bwd"], raw_up[1])
    raw_in = _adamw_sum8(r_in, w_in[0].T, m_w_in[0].T, v_w_in[0].T, "adamw_w_in")
    (r_late,) = _exchange_wait(last, raw_in[1])
    done = {"w_in": [a.T[None] for a in raw_in], "w_up": [a.T[None] for a in raw_up],
            "w_down": [a[None] for a in raw_down], "w_o": [a[None] for a in raw_o]}
    grad_w_in, grad_w_up, grad_w_down, grad_w_o = (done[k][0] for k in ("w_in", "w_up", "w_down", "w_o"))
    grad_w_pa = _sum8(r_pa, "sum_w_pa").T[None]
    grad_w_pb = _sum8(r_pb, "sum_w_pb").T[None]
    as_2d = {"g_final": (1, D), "b_s": (1, GROUPS * SGU_BLOCK), "b_mod": (N_MOD, D), "w_s": (GROUPS * SGU_BLOCK, SGU_BLOCK)}
    small_params = {"g_mix": (g_mix, m_g_mix, v_g_mix), "g_ffn": (g_ffn, m_g_ffn, v_g_ffn),
                    "g_final": (g_final, m_g_final, v_g_final), "g_norm_a": (g_norm_a, m_g_norm_a, v_g_norm_a),
                    "ln_v_g": (ln_v_g, m_ln_v_g, v_ln_v_g), "ln_v_b": (ln_v_b, m_ln_v_b, v_ln_v_b),
                    "b_s": (b_s, m_b_s, v_b_s), "b_mod": (b_mod, m_b_mod, v_b_mod), "w_s": (w_s, m_w_s, v_w_s)}
    tot, dgam, small_done = _small_finish(
        r_small, r_late, r_dws, gam, nb_ex,
        {n: tuple(a.reshape(as_2d.get(n, a.shape)) for a in wmv) for n, wmv in small_params.items()})
    for n, outs in small_done.items():
        done[n] = [a.reshape(small_params[n][0].shape) for a in outs]
    loss = tot[14, 0]
    grad_g_mix, grad_g_ffn, grad_g_final, grad_g_norm_a, grad_ln_v_g, grad_ln_v_b, grad_b_s, grad_b_mod, grad_w_s = (
        done[n][0] for n in ("g_mix", "g_ffn", "g_final", "g_norm_a", "ln_v_g", "ln_v_b", "b_s", "b_mod", "w_s"))
    grad_lb_gamma = lax.dynamic_slice(dgam.reshape(2, 2, KW), (0, 0, me * lb_cols), (2, 2, lb_cols))

    dmod_all = r_late[:, 16:16 + nb_ex * N_MOD].reshape(n_c, N_MOD * D)
    dmod_l = jnp.concatenate([lax.dynamic_slice(dmod_all, (0, me * mod_cols), (n_c, mod_cols)),
                              lax.dynamic_slice(tot[8:8 + N_MOD].reshape(1, N_MOD * D), (0, me * mod_cols), (1, mod_cols)),
                              jnp.zeros((7, mod_cols), F32)], axis=0)
    gw_mod, gc = _mod_bwd(svec, cvec, dmod_l, w_mod[0])
    grad_w_mod = gw_mod[None]
    (r_gc,) = _exchange([(gc[n_c:n_c + 8], "gather")], "gather_c_ctx", after=r_late)
    grad_c_ctx = _sum8(r_gc, "sum_c_ctx")[0]

    names = ["c_ctx", "w_mod", "b_mod", "g_mix", "g_ffn", "w_in", "lb_gamma", "g_norm_a", "ln_v_g", "ln_v_b", "w_s",
             "b_s", "w_pa", "w_pb", "w_o", "w_up", "w_down", "g_final"]
    weights = [c_ctx, w_mod, b_mod, g_mix, g_ffn, w_in, lb_gamma, g_norm_a, ln_v_g, ln_v_b, w_s, b_s, w_pa, w_pb, w_o,
               w_up, w_down, g_final]
    grads = [grad_c_ctx, grad_w_mod, grad_b_mod, grad_g_mix, grad_g_ffn, grad_w_in, grad_lb_gamma, grad_g_norm_a,
             grad_ln_v_g, grad_ln_v_b, grad_w_s, grad_b_s, grad_w_pa, grad_w_pb, grad_w_o, grad_w_up, grad_w_down,
             grad_g_final]
    ms = [m_c_ctx, m_w_mod, m_b_mod, m_g_mix, m_g_ffn, m_w_in, m_lb_gamma, m_g_norm_a, m_ln_v_g, m_ln_v_b, m_w_s, m_b_s,
          m_w_pa, m_w_pb, m_w_o, m_w_up, m_w_down, m_g_final]
    vs = [v_c_ctx, v_w_mod, v_b_mod, v_g_mix, v_g_ffn, v_w_in, v_lb_gamma, v_g_norm_a, v_ln_v_g, v_ln_v_b, v_w_s, v_b_s,
          v_w_pa, v_w_pb, v_w_o, v_w_up, v_w_down, v_g_final]
    deltas, new_ms, new_vs = [], [], []
    for nm, w, g, m, v in zip(names, weights, grads, ms, vs):
        d, nm_, nv_ = done[nm][1:] if nm in done else _adamw(w, g.reshape(w.shape), m, v, "adamw_" + nm)
        deltas.append(d)
        new_ms.append(nm_)
        new_vs.append(nv_)
    grads = [g.reshape(w.shape) for g, w in zip(grads, weights)]
    return (loss, grad_x, *grads, *deltas, *new_ms, *new_vs)
```

```python
import functools

import jax
import jax.numpy as jnp
from jax import lax
from jax.experimental import pallas as pl
from jax.experimental.pallas import tpu as pltpu

F32 = jnp.float32
MXU_DTYPE = jnp.bfloat16
PAYLOAD_DTYPE = jnp.bfloat16

N_DEV = 8
D = 1024
HEADS = 4
DK = 128
KW = HEADS * DK
CHUNK = 64
SGU_BLOCK = 128
GROUPS = 4
D_FF = 2816
FF_CHUNK = 256
N_MOD = 6
IN_COLS = 5632
CTX_COLS = 1536
TAIL_COLS = IN_COLS - 4 * KW
EPS = 1e-6
ADAM_LR, ADAM_B1, ADAM_B2, ADAM_EPS, ADAM_WD, ADAM_STEP = 0.001, 0.9, 0.999, 1e-08, 0.01, 10

VMEM_LIMIT = 56 * 1024 * 1024
TOKEN_TILE = 256
TAIL_TILE = 512
SMALL_ROWS = 40


def _params(sem):
    return pltpu.CompilerParams(dimension_semantics=sem, vmem_limit_bytes=VMEM_LIMIT)


_DN = {"nn": (((1,), (0,)), ((), ())), "nt": (((1,), (1,)), ((), ())), "tn": (((0,), (0,)), ((), ()))}


def _dot(a, b, form="nn"):
    return lax.dot_general(a.astype(MXU_DTYPE), b.astype(MXU_DTYPE), _DN[form], preferred_element_type=F32)


def _mask_dot(mask, v):
    bf = jnp.bfloat16
    hi = v.astype(bf)
    r1 = v - hi.astype(F32)
    mid = r1.astype(bf)
    lo = (r1 - mid.astype(F32)).astype(bf)
    w = v.shape[1]
    s = lax.dot_general(mask.astype(bf), jnp.concatenate([hi, mid, lo], axis=1), _DN["nn"], preferred_element_type=F32)
    return (s[:, 2 * w:] + s[:, w:2 * w]) + s[:, :w]


def _full(shape, single=False):
    n = len(shape)
    if single:
        return pl.BlockSpec(shape, lambda *_: (0,) * n, pipeline_mode=pl.Buffered(1))
    return pl.BlockSpec(shape, lambda *_: (0,) * n)


def _ordered_behind(body, in_specs, args, after):
    if after is None:
        return body
    at = len(in_specs)
    in_specs.append(pl.BlockSpec(memory_space=pl.ANY))
    args.append(after)
    return lambda *refs: body(*refs[:at], *refs[at + 1:])


def _sigmoid(z):
    return 0.5 * jnp.tanh(0.5 * z) + 0.5


def _gelu(x):
    c = 0.7978845608028654
    t = jnp.tanh(c * (x + 0.044715 * x * x * x))
    return 0.5 * x * (1.0 + t), t


def _gelu_grad(x, t):
    c = 0.7978845608028654
    return 0.5 * (1.0 + t) + 0.5 * x * (1.0 - t * t) * c * (1.0 + 3 * 0.044715 * x * x)


def _exchange(items, name, after=None):
    n = len(items)
    out_shape = []
    for a, mode in items:
        blk = a.shape if mode == "gather" else a.shape[1:]
        out_shape.append(jax.ShapeDtypeStruct((N_DEV,) + tuple(blk), a.dtype))

    def body(*refs):
        srcs, dsts = refs[:n], refs[n:2 * n]
        send_sems, recv_sems, local_sems = refs[2 * n:]
        x, y, c = lax.axis_index("x"), lax.axis_index("y"), lax.axis_index("c")
        me = 4 * x + 2 * y + c

        def src_for(i, dev):
            return srcs[i] if items[i][1] == "gather" else srcs[i].at[dev]

        local = [pltpu.make_async_copy(src_for(i, me), dsts[i].at[me], local_sems.at[i]) for i in range(n)]
        for cp in local:
            cp.start()
        remote = []
        for k in range(1, N_DEV):
            px = jnp.bitwise_xor(x, (k >> 2) & 1)
            py = jnp.bitwise_xor(y, (k >> 1) & 1)
            pc = jnp.bitwise_xor(c, k & 1)
            peer = 4 * px + 2 * py + pc
            for i in range(n):
                cp = pltpu.make_async_remote_copy(
                    src_ref=src_for(i, peer), dst_ref=dsts[i].at[me],
                    send_sem=send_sems.at[i * (N_DEV - 1) + k - 1], recv_sem=recv_sems.at[i * (N_DEV - 1) + k - 1],
                    device_id=(px, py, pc), device_id_type=pl.DeviceIdType.MESH)
                cp.start()
                remote.append(cp)
        for cp in remote:
            cp.wait()
        for cp in local:
            cp.wait()

    any_spec = pl.BlockSpec(memory_space=pl.ANY)
    in_specs, args = [any_spec] * n, [a for a, _ in items]
    if after is not None:
        in_specs.append(any_spec)
        args.append(after)
        exchange = body
        body = lambda *refs: exchange(*refs[:n], *refs[n + 1:])
    return pl.pallas_call(
        body, name=name, out_shape=out_shape, in_specs=in_specs, out_specs=[any_spec] * n,
        scratch_shapes=[pltpu.SemaphoreType.DMA((n * (N_DEV - 1),)), pltpu.SemaphoreType.DMA((n * (N_DEV - 1),)),
                        pltpu.SemaphoreType.DMA((n,))],
    )(*args)


def _gather_two_level(arrays, name):
    n = len(arrays)
    pieces = []
    for i, a in enumerate(arrays):
        rows = _Sender.PIECE_ROWS if a.shape[0] % _Sender.PIECE_ROWS == 0 else a.shape[0]
        pieces += [(i, r0, rows) for r0 in range(0, a.shape[0], rows)]

    def body(*refs):
        srcs, dsts = refs[:n], refs[n:2 * n]
        send_sems, recv_sems, local_sems = refs[2 * n:]
        x, y, c = lax.axis_index("x"), lax.axis_index("y"), lax.axis_index("c")
        me, sibling = (x, y, c), (x, y, 1 - c)
        x_nbr, y_nbr, diag = (1 - x, y, c), (x, 1 - y, c), (1 - x, 1 - y, c)

        def slot(px, py, pc):
            return 4 * px + 2 * py + pc

        def copy(u, k, block, to, own=False):
            i, r0, rows = pieces[u]
            there = dsts[i].at[slot(*block)].at[pl.ds(r0, rows)]
            return pltpu.make_async_remote_copy(
                src_ref=srcs[i].at[pl.ds(r0, rows)] if own else there, dst_ref=there,
                send_sem=send_sems.at[u * 7 + k], recv_sem=recv_sems.at[u * 7 + k],
                device_id=to, device_id_type=pl.DeviceIdType.MESH)

        units = range(len(pieces))
        mine = [pltpu.make_async_copy(srcs[i], dsts[i].at[slot(*me)], local_sems.at[i]) for i in range(n)]
        for cp in mine:
            cp.start()
        for u in units:
            copy(u, 1, me, x_nbr, own=True).start()
            copy(u, 2, me, y_nbr, own=True).start()
        for u in units:
            copy(u, 0, me, sibling, own=True).start()

        def relay_then_pass(k_from, frm, to, k_other, other):
            for u in units:
                copy(u, k_from, frm, me).wait_recv()
                copy(u, 3, frm, to).start()
                copy(u, 3 + k_from, frm, sibling).start()
            for u in units:
                copy(u, k_other, other, me).wait_recv()
                copy(u, 3 + k_other, other, sibling).start()

        @pl.when(c == 1)
        def _():
            relay_then_pass(1, x_nbr, y_nbr, 2, y_nbr)

        @pl.when(c == 0)
        def _():
            relay_then_pass(2, y_nbr, x_nbr, 1, x_nbr)

        for u in units:
            copy(u, 3, diag, me).wait_recv()
            copy(u, 6, diag, sibling).start()
        for u in units:
            copy(u, 0, sibling, me).wait_recv()
            for k, chip in ((4, x_nbr), (5, y_nbr), (6, diag)):
                copy(u, k, (chip[0], chip[1], 1 - c), me).wait_recv()
        for u in units:
            for k in range(7):
                copy(u, k, me, me, own=True).wait_send()
        for cp in mine:
            cp.wait()

    any_spec = pl.BlockSpec(memory_space=pl.ANY)
    return pl.pallas_call(
        body, name=name, out_shape=[jax.ShapeDtypeStruct((N_DEV,) + a.shape, a.dtype) for a in arrays],
        in_specs=[any_spec] * n, out_specs=[any_spec] * n,
        scratch_shapes=[pltpu.SemaphoreType.DMA((len(pieces) * 7,)), pltpu.SemaphoreType.DMA((len(pieces) * 7,)),
                        pltpu.SemaphoreType.DMA((n,))],
    )(*arrays)


_HBM = pl.BlockSpec(memory_space=pltpu.HBM)
_SEM = pl.BlockSpec(memory_space=pltpu.SEMAPHORE)
_EFFECT = pltpu.SideEffectType.DATAFLOW_SIDE_EFFECTING


def _split_copies(items, srcs, lands, send_sems, recv_sems):
    x, y, c = lax.axis_index("x"), lax.axis_index("y"), lax.axis_index("c")
    me = 4 * x + 2 * y + c
    copies = []
    for k in range(1, N_DEV):
        px = jnp.bitwise_xor(x, (k >> 2) & 1)
        py = jnp.bitwise_xor(y, (k >> 1) & 1)
        pc = jnp.bitwise_xor(c, k & 1)
        peer = 4 * px + 2 * py + pc
        for i in range(len(items)):
            src = srcs[i] if items[i][1] == "gather" else srcs[i].at[peer]
            copies.append(pltpu.make_async_remote_copy(
                src_ref=src, dst_ref=lands[i].at[me],
                send_sem=send_sems.at[i * (N_DEV - 1) + k - 1], recv_sem=recv_sems.at[i * (N_DEV - 1) + k - 1],
                device_id=(px, py, pc), device_id_type=pl.DeviceIdType.MESH))
    return me, copies


def _exchange_start(items, name, after):
    n = len(items)
    n_sem = n * (N_DEV - 1)
    srcs, lands = [], []
    for a, mode in items:
        blk = a.shape if mode == "gather" else a.shape[1:]
        srcs.append(pltpu.with_memory_space_constraint(a, pltpu.HBM))
        lands.append(pltpu.with_memory_space_constraint(lax.empty((N_DEV,) + tuple(blk), a.dtype), pltpu.HBM))

    def body(*refs):
        src_refs, land_refs = refs[:n], refs[n:2 * n]
        send_sems, recv_sems = refs[2 * n + 1], refs[2 * n + 2]
        local_sems = refs[4 * n + 3]
        me, copies = _split_copies(items, src_refs, land_refs, send_sems, recv_sems)
        for i in range(n):
            own = src_refs[i] if items[i][1] == "gather" else src_refs[i].at[me]
            cp = pltpu.make_async_copy(own, land_refs[i].at[me], local_sems.at[i])
            cp.start()
            cp.wait()
        for cp in copies:
            cp.start()

    out_shape = [pltpu.SemaphoreType.DMA((n_sem,)), pltpu.SemaphoreType.DMA((n_sem,))]
    out_shape += [pltpu.HBM(a.shape, a.dtype) for a in srcs] + [pltpu.HBM(a.shape, a.dtype) for a in lands]
    outs = pl.pallas_call(
        body, name=name, out_shape=out_shape,
        in_specs=[_HBM] * (2 * n) + [pl.BlockSpec(memory_space=pl.ANY)],
        out_specs=[_SEM, _SEM] + [_HBM] * (2 * n),
        input_output_aliases={i: 2 + i for i in range(2 * n)},
        scratch_shapes=[pltpu.SemaphoreType.DMA((n,))],
        compiler_params=pltpu.CompilerParams(has_side_effects=_EFFECT),
    )(*srcs, *lands, after)
    handle = (items, name, outs[0], outs[1], outs[2:2 + n], outs[2 + n:2 + 2 * n])
    return handle, outs[2]


class _Sender:
    PIECE_ROWS = 352

    def __init__(self, items, chunks=None):
        self.items, self.n = items, len(items)
        self.chunks = chunks
        if chunks is None:
            block_rows = [a.shape[0] if mode == "gather" else a.shape[1] for a, mode in items]
            self.chunks = [r // self.PIECE_ROWS if r % self.PIECE_ROWS == 0 else 1 for r in block_rows]
        self.srcs, self.lands = [], []
        for a, mode in items:
            blk = a.shape if mode == "gather" else a.shape[1:]
            self.srcs.append(pltpu.with_memory_space_constraint(a, pltpu.HBM))
            self.lands.append(pltpu.with_memory_space_constraint(lax.empty((N_DEV,) + tuple(blk), a.dtype), pltpu.HBM))

    def issue(self, src_refs, land_refs, send_sems, recv_sems, local_sems, step, n_steps):
        x, y, c = lax.axis_index("x"), lax.axis_index("y"), lax.axis_index("c")
        me = 4 * x + 2 * y + c
        copies = []
        for ch in range(max(self.chunks)):
            for k in range(1, N_DEV):
                px = jnp.bitwise_xor(x, (k >> 2) & 1)
                py = jnp.bitwise_xor(y, (k >> 1) & 1)
                pc = jnp.bitwise_xor(c, k & 1)
                peer = 4 * px + 2 * py + pc
                for i, (_, mode) in enumerate(self.items):
                    if ch >= self.chunks[i]:
                        continue
                    n_rows = land_refs[i].shape[1] // self.chunks[i]
                    rows = pl.ds(ch * n_rows, n_rows)
                    src = src_refs[i].at[rows] if mode == "gather" else src_refs[i].at[peer].at[rows]
                    copies.append(pltpu.make_async_remote_copy(
                        src_ref=src, dst_ref=land_refs[i].at[me].at[rows],
                        send_sem=send_sems.at[i * (N_DEV - 1) + k - 1], recv_sem=recv_sems.at[i * (N_DEV - 1) + k - 1],
                        device_id=(px, py, pc), device_id_type=pl.DeviceIdType.MESH))
        own = [pltpu.make_async_copy(src_refs[i] if mode == "gather" else src_refs[i].at[me], land_refs[i].at[me],
                                     local_sems.at[i]) for i, (_, mode) in enumerate(self.items)]

        @pl.when(step == 0)
        def _():
            for cp in own:
                cp.start()

        for s in range(n_steps):
            group = [cp for j, cp in enumerate(copies) if (j * n_steps) // len(copies) == s]
            if group:
                @pl.when(step == s)
                def _(group=group):
                    for cp in group:
                        cp.start()

        @pl.when(step == n_steps - 1)
        def _():
            for cp in own:
                cp.wait()


def _host_call(body, name, grid, in_specs, args, out_shape, out_specs, scratch_shapes, after=None, sender=None):
    in_specs, args, out_shape, out_specs = list(in_specs), list(args), list(out_shape), list(out_specs)
    scratch_shapes = list(scratch_shapes)
    semantics = ("arbitrary",) * len(grid)
    body = _ordered_behind(body, in_specs, args, after)
    if sender is None:
        res = pl.pallas_call(body, name=name, grid=grid, in_specs=in_specs, out_specs=out_specs, out_shape=out_shape,
                             scratch_shapes=scratch_shapes, compiler_params=_params(semantics))(*args)
        return res, None
    n, n_in, n_out, n_scr = sender.n, len(in_specs), len(out_shape), len(scratch_shapes)
    n_sem = n * (N_DEV - 1)
    n_steps = 1
    for g in grid:
        n_steps *= g
    compute = body

    def body(*refs):
        ins, s_in = refs[:n_in], refs[n_in:n_in + 2 * n]
        o0 = n_in + 2 * n
        outs, s_out = refs[o0:o0 + n_out], refs[o0 + n_out:o0 + n_out + 2 + 2 * n]
        scr = refs[o0 + n_out + 2 + 2 * n:]
        compute(*ins, *outs, *scr[:n_scr])
        step = pl.program_id(0)
        for d in range(1, len(grid)):
            step = step * grid[d] + pl.program_id(d)
        sender.issue(s_in[:n], s_in[n:], s_out[0], s_out[1], scr[n_scr], step, n_steps)

    res = pl.pallas_call(
        body, name=name, grid=grid,
        in_specs=in_specs + [_HBM] * (2 * n), out_specs=out_specs + [_SEM, _SEM] + [_HBM] * (2 * n),
        out_shape=out_shape + [pltpu.SemaphoreType.DMA((n_sem,)), pltpu.SemaphoreType.DMA((n_sem,))]
        + [pltpu.HBM(a.shape, a.dtype) for a in sender.srcs] + [pltpu.HBM(a.shape, a.dtype) for a in sender.lands],
        input_output_aliases={n_in + j: n_out + 2 + j for j in range(2 * n)},
        scratch_shapes=scratch_shapes + [pltpu.SemaphoreType.DMA((n,))],
        compiler_params=pltpu.CompilerParams(dimension_semantics=semantics, vmem_limit_bytes=VMEM_LIMIT,
                                             has_side_effects=_EFFECT),
    )(*args, *sender.srcs, *sender.lands)
    handle = (sender.items, name, res[n_out], res[n_out + 1], res[n_out + 2:n_out + 2 + n],
              res[n_out + 2 + n:n_out + 2 + 2 * n])
    return res[:n_out], handle


def _exchange_wait(handle, after):
    items, name, send_sems, recv_sems, srcs, lands = handle
    n = len(items)

    def body(*refs):
        src_refs, land_refs = refs[:n], refs[n:2 * n]
        send_ref, recv_ref = refs[2 * n], refs[2 * n + 1]
        _, copies = _split_copies(items, src_refs, land_refs, send_ref, recv_ref)
        for cp in copies:
            cp.wait_send()
            cp.wait_recv()

    outs = pl.pallas_call(
        body, name=name + "_wait",
        out_shape=[pltpu.HBM(a.shape, a.dtype) for a in srcs] + [pltpu.HBM(a.shape, a.dtype) for a in lands],
        in_specs=[_HBM] * (2 * n) + [_SEM, _SEM, pl.BlockSpec(memory_space=pl.ANY)], out_specs=[_HBM] * (2 * n),
        input_output_aliases={i: i for i in range(2 * n)},
        compiler_params=pltpu.CompilerParams(has_side_effects=_EFFECT),
    )(*srcs, *lands, send_sems, recv_sems, after)
    return outs[n:]


def _mod_fwd(cvec, w_mod_l, b_mod_l):
    rows, cols = cvec.shape[0], w_mod_l.shape[1]

    def body(c_ref, w_ref, b_ref, o_ref, s_ref):
        cv = c_ref[...]
        s = cv * _sigmoid(cv)
        s_ref[...] = s
        o_ref[...] = _dot(s, w_ref[...]) + b_ref[...]

    return pl.pallas_call(
        body, name="mod_fwd",
        out_shape=(jax.ShapeDtypeStruct((rows, cols), F32), jax.ShapeDtypeStruct((rows, D), F32)),
        in_specs=[_full((rows, D)), _full((D, cols)), _full((1, cols))],
        out_specs=(_full((rows, cols)), _full((rows, D))), grid=(1,),
        compiler_params=_params(("arbitrary",)),
    )(cvec, w_mod_l, b_mod_l)


def _mod_bwd(svec, cvec, dmod_l, w_mod_l):
    rows, cols = dmod_l.shape

    def body(s_ref, c_ref, d_ref, w_ref, gw_ref, gc_ref):
        gw_ref[...] = _dot(s_ref[...], d_ref[...], "tn")
        cv = c_ref[...]
        sg = _sigmoid(cv)
        gc_ref[...] = _dot(d_ref[...], w_ref[...], "nt") * (sg * (1.0 + cv * (1.0 - sg)))

    return pl.pallas_call(
        body, name="mod_bwd",
        out_shape=(jax.ShapeDtypeStruct((D, cols), F32), jax.ShapeDtypeStruct((rows, D), F32)),
        in_specs=[_full((rows, D)), _full((rows, D)), _full((rows, cols)), _full((D, cols))],
        out_specs=(_full((D, cols)), _full((rows, D))), grid=(1,),
        compiler_params=_params(("arbitrary",)),
    )(svec, cvec, dmod_l, w_mod_l)


def _inproj(xt, modv, g, w_inT, n_cols, rows_per_example, name, after=None, sender=None):
    rows = xt.shape[0]
    tm = min(TOKEN_TILE, rows_per_example)
    per_b = rows_per_example // tm
    shared_mod = modv.shape[0] == 1

    def body(x_ref, mod_ref, g_ref, w_ref, p_ref, h_ref):
        x = x_ref[...]
        r = lax.rsqrt(jnp.mean(x * x, axis=-1, keepdims=True) + EPS)
        h = (x * r * g_ref[...]) * (1.0 + mod_ref[0, 1:2, :]) + mod_ref[0, 0:1, :]
        hb = h.astype(MXU_DTYPE)
        h_ref[...] = hb
        for j in range(n_cols // KW):
            p_ref[:, j * KW:(j + 1) * KW] = _dot(hb, w_ref[j * KW:(j + 1) * KW, :], "nt").astype(p_ref.dtype)

    mod_idx = (lambda i: (0, 0, 0)) if shared_mod else (lambda i: (i // per_b, 0, 0))
    in_specs = [pl.BlockSpec((tm, D), lambda i: (i, 0)), pl.BlockSpec((1, N_MOD, D), mod_idx), _full((1, D)),
                pl.BlockSpec((n_cols, D), lambda i: (0, 0), pipeline_mode=pl.Buffered(1))]
    (p, h), handle = _host_call(
        body, name, (rows // tm,), in_specs, [xt, modv, g, w_inT],
        [jax.ShapeDtypeStruct((rows, n_cols), MXU_DTYPE), jax.ShapeDtypeStruct((rows, D), MXU_DTYPE)],
        [pl.BlockSpec((tm, n_cols), lambda i: (i, 0)), pl.BlockSpec((tm, D), lambda i: (i, 0))], [],
        after=after, sender=sender)
    return p, h, handle


def _tri(reverse, n):
    row = lax.broadcasted_iota(jnp.int32, (n, n), 0)
    col = lax.broadcasted_iota(jnp.int32, (n, n), 1)
    same = (row // CHUNK) == (col // CHUNK)
    return same & ((col >= row) if reverse else (col <= row))


def _per_chunk_rows(x, reverse):
    n = x.shape[0]
    rows = [x[j * CHUNK:j * CHUNK + 1] if reverse else x[(j + 1) * CHUNK - 1:(j + 1) * CHUNK] for j in range(n // CHUNK)]
    return jnp.concatenate([jnp.broadcast_to(r, (CHUNK, x.shape[1])) for r in rows], axis=0), rows


def _lower_bound(gam_ref, direction):
    return _sigmoid(gam_ref[direction:direction + 1, :] - gam_ref[2 + direction:3 + direction, :])


def _gate_prep(z, lb, tri, reverse):
    sg = _sigmoid(z)
    f = lb + (1.0 - lb) * sg
    g = jnp.log(f)
    b = _mask_dot(tri, g)
    bl, bl_rows = _per_chunk_rows(b, reverse)
    mid = 0.5 * bl
    return sg, g, 1.0 - f, b, jnp.exp(mid), [jnp.exp(0.5 * r) for r in bl_rows], jnp.exp(mid - b), mid


def _hgrn_fwd(p, gam, s0, rows_per_example, with_out, name, sender=None):
    rows = p.shape[0]
    nb_ex = rows // rows_per_example
    rb = min(TOKEN_TILE, rows_per_example)
    cpb = rb // CHUNK
    nb = rows_per_example // rb
    n_chunks = rows // CHUNK
    has_s0 = s0 is not None

    def body(*refs):
        it = iter(refs)
        gam_ref = next(it)
        zf_ref, vf_ref = next(it), next(it)
        qf_ref = next(it) if with_out else None
        zb_ref, vb_ref = next(it), next(it)
        qb_ref = next(it) if with_out else None
        s0_ref = next(it) if has_s0 else None
        if with_out:
            of_ref, ob_ref = next(it), next(it)
        stash_f, stash_b, fin_ref = next(it), next(it), next(it)
        st_ref = next(it)
        i = pl.program_id(1)

        @pl.when(i == 0)
        def _():
            if has_s0:
                st_ref[...] = s0_ref[:, 0]
            else:
                st_ref[...] = jnp.zeros_like(st_ref)

        for direction, (z_ref, v_ref, q_ref, stash) in enumerate(
                ((zf_ref, vf_ref, qf_ref, stash_f), (zb_ref, vb_ref, qb_ref, stash_b))):
            reverse = direction == 1
            tri = _tri(reverse, rb)
            lb = _lower_bound(gam_ref, direction)
            z = z_ref[...].astype(F32)
            v = v_ref[...].astype(F32)
            _, _, k, b, em, em_rows, e2, mid = _gate_prep(z, lb, tri, reverse)
            kd = (k * (e2 * em)).astype(MXU_DTYPE)
            vb = v.astype(MXU_DTYPE)
            if with_out:
                q = q_ref[...].astype(F32)
                qi = q * jnp.exp(b - mid)
                qe = (qi * em).astype(MXU_DTYPE)
                qi = qi.astype(MXU_DTYPE)
                ki = (k * e2).astype(MXU_DTYPE)
                intra = []
                for h in range(HEADS):
                    hs = slice(h * DK, (h + 1) * DK)
                    sc = jnp.where(tri, _dot(qi[:, hs], ki[:, hs], "nt"), 0.0)
                    intra.append(_dot(sc, vb[:, hs]))
            for j in (range(cpb - 1, -1, -1) if reverse else range(cpb)):
                rs = slice(j * CHUNK, (j + 1) * CHUNK)
                a = em_rows[j] * em_rows[j]
                for h in range(HEADS):
                    hs = slice(h * DK, (h + 1) * DK)
                    st = st_ref[direction, h]
                    stash[j, h] = st.astype(stash.dtype)
                    if with_out:
                        (ob_ref if reverse else of_ref)[rs, hs] = intra[h][rs] + _dot(qe[rs, hs], st, "nt")
                    st_ref[direction, h] = st * a[:, hs] + _dot(vb[rs, hs], kd[rs, hs], "tn")

        @pl.when(i == nb - 1)
        def _():
            fin_ref[:, 0] = st_ref[...]

    up = lambda b, i: b * nb + i
    down = lambda b, i: b * nb + nb - 1 - i
    col = lambda rowf, c: pl.BlockSpec((rb, KW), lambda b, i: (rowf(b, i), c))
    in_specs = [_full((4, KW)), col(up, 0), col(up, 2)] + ([col(up, 3)] if with_out else [])
    in_specs += [col(down, 1), col(down, 2)] + ([col(down, 3)] if with_out else [])
    args = [gam, p, p] + ([p] if with_out else []) + [p, p] + ([p] if with_out else [])
    if has_s0:
        in_specs.append(pl.BlockSpec((2, 1, HEADS, DK, DK), lambda b, i: (0, b, 0, 0, 0)))
        args.append(s0)
    out_shape, out_specs = [], []
    if with_out:
        out_shape += [jax.ShapeDtypeStruct((rows, KW), F32)] * 2
        out_specs += [pl.BlockSpec((rb, KW), lambda b, i: (up(b, i), 0)),
                      pl.BlockSpec((rb, KW), lambda b, i: (down(b, i), 0))]
    out_shape += [jax.ShapeDtypeStruct((n_chunks, HEADS, DK, DK), MXU_DTYPE)] * 2
    out_specs += [pl.BlockSpec((cpb, HEADS, DK, DK), lambda b, i: (up(b, i), 0, 0, 0)),
                  pl.BlockSpec((cpb, HEADS, DK, DK), lambda b, i: (down(b, i), 0, 0, 0))]
    out_shape.append(jax.ShapeDtypeStruct((2, nb_ex, HEADS, DK, DK), F32))
    out_specs.append(pl.BlockSpec((2, 1, HEADS, DK, DK), lambda b, i: (0, b, 0, 0, 0)))
    res, handle = _host_call(body, name, (nb_ex, nb), in_specs, args, out_shape, out_specs,
                             [pltpu.VMEM((2, HEADS, DK, DK), F32)], sender=sender)
    return (*res, handle)


def _hgrn_bwd(p, gam, do, stash_f, stash_b, ds_end, rows_per_example, with_out, name, after=None, sender=None):
    rows = p.shape[0]
    nb_ex = rows // rows_per_example
    rb = min(TOKEN_TILE, rows_per_example)
    cpb = rb // CHUNK
    nb = rows_per_example // rb
    has_end = ds_end is not None

    def body(*refs):
        it = iter(refs)
        gam_ref = next(it)
        ins = []
        for _ in range(2):
            z_ref, v_ref = next(it), next(it)
            q_ref = next(it) if with_out else None
            do_ref = next(it) if with_out else None
            ins.append((z_ref, v_ref, q_ref, do_ref, next(it)))
        end_ref = next(it) if has_end else None
        outs = [next(it), next(it)]
        dlb_ref, ds0_ref = next(it), next(it)
        dst_ref = next(it)
        b_id, i = pl.program_id(0), pl.program_id(1)

        @pl.when(i == 0)
        def _():
            if has_end:
                dst_ref[...] = end_ref[:, 0]
            else:
                dst_ref[...] = jnp.zeros_like(dst_ref)

        @pl.when((i == 0) & (b_id == 0))
        def _():
            dlb_ref[...] = jnp.zeros_like(dlb_ref)

        for direction in range(2):
            z_ref, v_ref, q_ref, do_ref, stash = ins[direction]
            dgrp_ref = outs[direction]
            reverse = direction == 1
            tri = _tri(reverse, rb)
            tri_t = _tri(not reverse, rb)
            lb = _lower_bound(gam_ref, direction)
            heads = [slice(h * DK, (h + 1) * DK) for h in range(HEADS)]
            chunks = [slice(j * CHUNK, (j + 1) * CHUNK) for j in range(cpb)]
            grid_cat = lambda parts: jnp.concatenate([jnp.concatenate(row, axis=1) for row in parts], axis=0)
            cat = lambda parts: jnp.concatenate(parts, axis=1)
            z = z_ref[...].astype(F32)
            sg, g, k, b, em, em_rows, e2, mid = _gate_prep(z, lb, tri, reverse)
            e3 = e2 * em
            kd = k * e3
            kd_b = kd.astype(MXU_DTYPE)
            vb = v_ref[...].astype(MXU_DTYPE)
            if with_out:
                q = q_ref[...].astype(F32)
                dout = do_ref[...].astype(MXU_DTYPE)
                e1 = jnp.exp(b - mid)
                e4 = e1 * em
                qi, ki, qe = q * e1, k * e2, q * e4
                qi_b, ki_b, qe_b = qi.astype(MXU_DTYPE), ki.astype(MXU_DTYPE), qe.astype(MXU_DTYPE)
                dqi_p, dki_p, dv_p = [], [], []
                for hs in heads:
                    sc = jnp.where(tri, _dot(qi_b[:, hs], ki_b[:, hs], "nt"), 0.0)
                    dsc = jnp.where(tri, _dot(dout[:, hs], vb[:, hs], "nt"), 0.0)
                    dqi_p.append(_dot(dsc, ki_b[:, hs]))
                    dki_p.append(_dot(dsc, qi_b[:, hs], "tn"))
                    dv_p.append(_dot(sc, dout[:, hs], "tn"))
                dqi, dki, dv = cat(dqi_p), cat(dki_p), cat(dv_p)
                dqe = grid_cat([[_dot(dout[rs, hs], stash[j, h]) for h, hs in enumerate(heads)]
                                for j, rs in enumerate(chunks)])
                grow = [[_dot(dout[rs, hs], qe_b[rs, hs], "tn") for hs in heads] for rs in chunks]
            dkd_p = [[None] * HEADS for _ in range(cpb)]
            dvs_p = [[None] * HEADS for _ in range(cpb)]
            da_p = [[None] * HEADS for _ in range(cpb)]
            for j in (range(cpb) if reverse else range(cpb - 1, -1, -1)):
                rs = chunks[j]
                a = em_rows[j] * em_rows[j]
                for h, hs in enumerate(heads):
                    dst = dst_ref[direction, h]
                    dkd_p[j][h] = _dot(vb[rs, hs], dst)
                    dvs_p[j][h] = _dot(kd_b[rs, hs], dst, "nt")
                    da_p[j][h] = jnp.broadcast_to(
                        jnp.sum(dst * stash[j, h].astype(F32), axis=0, keepdims=True), (CHUNK, DK))
                    new_dst = dst * a[:, hs]
                    dst_ref[direction, h] = new_dst + grow[j][h] if with_out else new_dst
            dkd, dvs, da = grid_cat(dkd_p), grid_cat(dvs_p), grid_cat(da_p)
            t_kd = dkd * kd
            dk = dkd * e3
            db = -t_kd
            tot = t_kd
            if with_out:
                dgrp_ref[:, KW:2 * KW] = (dvs + dv).astype(dgrp_ref.dtype)
                dgrp_ref[:, 2 * KW:] = (dqi * e1 + dqe * e4).astype(dgrp_ref.dtype)
                dk = dk + dki * e2
                t_qi, t_ki, t_qe = dqi * qi, dki * ki, dqe * qe
                db = db + t_qi - t_ki + t_qe
                tot = tot + 0.5 * (t_ki - t_qi)
            else:
                dgrp_ref[:, KW:2 * KW] = dvs.astype(dgrp_ref.dtype)
            dbl = jnp.concatenate([jnp.broadcast_to(jnp.sum(tot[rs], axis=0, keepdims=True), (CHUNK, KW))
                                   for rs in chunks], axis=0) + da * (em * em)
            dg = _mask_dot(tri_t, db) + dbl
            df = dg * jnp.exp(-g) - dk
            dgrp_ref[:, 0:KW] = (df * (1.0 - lb) * sg * (1.0 - sg)).astype(dgrp_ref.dtype)
            dlb_ref[direction:direction + 1, :] += jnp.sum(df * (1.0 - sg), axis=0, keepdims=True)

        @pl.when(i == nb - 1)
        def _():
            ds0_ref[:, 0] = dst_ref[...]

    rows_of = (lambda b, i: b * nb + nb - 1 - i, lambda b, i: b * nb + i)
    in_specs, args = [_full((4, KW))], [gam]
    for direction in range(2):
        rf = rows_of[direction]
        col = lambda c, rf=rf: pl.BlockSpec((rb, KW), lambda b, i: (rf(b, i), c))
        in_specs += [col(direction), col(2)]
        args += [p, p]
        if with_out:
            in_specs += [col(3), col(0)]
            args += [p, do]
        in_specs.append(pl.BlockSpec((cpb, HEADS, DK, DK), lambda b, i, rf=rf: (rf(b, i), 0, 0, 0)))
        args.append((stash_f, stash_b)[direction])
    if has_end:
        in_specs.append(pl.BlockSpec((2, 1, HEADS, DK, DK), lambda b, i: (0, b, 0, 0, 0)))
        args.append(ds_end)
    out_shape, out_specs = [], []
    for direction in range(2):
        rf = rows_of[direction]
        width = (3 if with_out else 2) * KW
        out_shape.append(jax.ShapeDtypeStruct((rows, width), MXU_DTYPE))
        out_specs.append(pl.BlockSpec((rb, width), lambda b, i, rf=rf: (rf(b, i), 0)))
    out_shape += [jax.ShapeDtypeStruct((2, KW), F32), jax.ShapeDtypeStruct((2, nb_ex, HEADS, DK, DK), F32)]
    out_specs += [_full((2, KW)), pl.BlockSpec((2, 1, HEADS, DK, DK), lambda b, i: (0, b, 0, 0, 0))]
    res, handle = _host_call(body, name, (nb_ex, nb), in_specs, args, out_shape, out_specs,
                             [pltpu.VMEM((2, HEADS, DK, DK), F32)], after=after, sender=sender)
    return (*res, handle)


def _tail_forward(osum, og, u, v, ga, gb, gna, ln_g, ln_b, ws_ref, bs_ref, wpaT_ref, wpbT_ref):
    tm = osum.shape[0]
    gna4 = jnp.concatenate([gna] * HEADS, axis=1)
    r_parts = []
    for h in range(HEADS):
        oh = osum[:, h * DK:(h + 1) * DK]
        r_parts.append(jnp.broadcast_to(lax.rsqrt(jnp.mean(oh * oh, axis=-1, keepdims=True) + EPS), (tm, DK)))
    r = jnp.concatenate(r_parts, axis=1)
    on = osum * r
    sg_og = _sigmoid(og)
    silu_og = og * sg_og
    oan = on * gna4
    oa = oan * silu_og
    ug, tu = _gelu(u)
    vg, tv = _gelu(v)
    mu = jnp.mean(vg, axis=-1, keepdims=True)
    vc = vg - mu
    rstd = lax.rsqrt(jnp.mean(vc * vc, axis=-1, keepdims=True) + EPS)
    vhat = vc * rstd
    vln = vhat * ln_g + ln_b
    blocks = []
    for n in range(tm // SGU_BLOCK):
        rs = slice(n * SGU_BLOCK, (n + 1) * SGU_BLOCK)
        blocks.append(jnp.concatenate(
            [_dot(ws_ref[g], vln[rs, g * DK:(g + 1) * DK]) + bs_ref[g] for g in range(GROUPS)], axis=1))
    mixed = jnp.concatenate(blocks, axis=0) if len(blocks) > 1 else blocks[0]
    obm = ug * mixed
    pa = _dot(oa, wpaT_ref[...], "nt")
    pb = _dot(obm, wpbT_ref[...], "nt")
    sga, sgb = _sigmoid(ga), _sigmoid(gb)
    merged = sga * pa + sgb * pb
    return dict(r=r, on=on, sg_og=sg_og, silu_og=silu_og, oan=oan, oa=oa, ug=ug, tu=tu, tv=tv, rstd=rstd, vhat=vhat,
                vln=vln, mixed=mixed, obm=obm, pa=pa, pb=pb, sga=sga, sgb=sgb, merged=merged, gna4=gna4)


def _tail_in_specs(tm):
    tile = lambda c: pl.BlockSpec((tm, KW), lambda i: (i, c))
    return [tile(c) for c in range(4, 11)]


def _tail_weight_specs():
    return [_full((1, DK)), _full((1, KW)), _full((1, KW)), _full((GROUPS, SGU_BLOCK, SGU_BLOCK)),
            _full((GROUPS, SGU_BLOCK, 1)), _full((D, KW), single=True), _full((D, KW), single=True),
            _full((D, D), single=True)]


def _read_tail_inputs(of_ref, ob_ref, pcols):
    osum = of_ref[...] + ob_ref[...]
    og, u, v = (pcols[j][...].astype(F32) for j in range(3))
    ga = jnp.concatenate([pcols[3][...], pcols[4][...]], axis=1).astype(F32)
    gb = jnp.concatenate([pcols[5][...], pcols[6][...]], axis=1).astype(F32)
    return osum, og, u, v, ga, gb


def _tail_fwd(p, o_up, o_down, xt, modv, gna, ln_g, ln_b, w_s, b_s, w_paT, w_pbT, w_o, rows_per_example):
    rows = xt.shape[0]
    tm = min(TAIL_TILE, rows_per_example)
    per_b = rows_per_example // tm

    def body(of_ref, ob_ref, *rest):
        pcols = rest[:7]
        (x_ref, mod_ref, gna_ref, lng_ref, lnb_ref, ws_ref, bs_ref, wpaT_ref, wpbT_ref, wo_ref,
         x1_ref, mix_ref, merged_ref, oa_ref, obm_ref) = rest[7:]
        t = _tail_forward(*_read_tail_inputs(of_ref, ob_ref, pcols), gna_ref[...], lng_ref[...], lnb_ref[...],
                          ws_ref, bs_ref, wpaT_ref, wpbT_ref)
        mix = _dot(t["merged"], wo_ref[...])
        x1_ref[...] = x_ref[...] + mod_ref[0, 2:3, :] * mix
        mix_ref[...] = mix.astype(mix_ref.dtype)
        merged_ref[...] = t["merged"].astype(merged_ref.dtype)
        oa_ref[...] = t["oa"].astype(oa_ref.dtype)
        obm_ref[...] = t["obm"].astype(obm_ref.dtype)

    row = lambda w: pl.BlockSpec((tm, w), lambda i: (i, 0))
    in_specs = [row(KW), row(KW)] + _tail_in_specs(tm) + [row(D), pl.BlockSpec((1, N_MOD, D), lambda i: (i // per_b, 0, 0))]
    in_specs += _tail_weight_specs()
    return pl.pallas_call(
        body, name="tail_fwd", grid=(rows // tm,),
        out_shape=(jax.ShapeDtypeStruct((rows, D), F32), jax.ShapeDtypeStruct((rows, D), MXU_DTYPE),
                   jax.ShapeDtypeStruct((rows, D), MXU_DTYPE), jax.ShapeDtypeStruct((rows, KW), MXU_DTYPE),
                   jax.ShapeDtypeStruct((rows, KW), MXU_DTYPE)),
        in_specs=in_specs, out_specs=(row(D), row(D), row(D), row(KW), row(KW)),
        compiler_params=_params(("arbitrary",)),
    )(o_up, o_down, *([p] * 7), xt, modv, gna, ln_g, ln_b, w_s, b_s, w_paT, w_pbT, w_o)


def _tail_bwd(p, o_up, o_down, dx1, mix, modv, gna, ln_g, ln_b, w_s, b_s, w_paT, w_pbT, w_o, rows_per_example,
              after=None, sender=None):
    rows = dx1.shape[0]
    nb_ex = rows // rows_per_example
    tm = min(TAIL_TILE, rows_per_example)
    per_b = rows_per_example // tm

    def body(of_ref, ob_ref, *rest):
        pcols = rest[:7]
        (dx1_ref, mix_ref, mod_ref, gna_ref, lng_ref, lnb_ref, ws_ref, bs_ref, wpaT_ref, wpbT_ref, wo_ref,
         dpt_ref, do_ref, dmix_ref, dpa_ref, dpb_ref, dmod_ref, small_ref, dws_ref, dbs_ref) = rest[7:]
        i = pl.program_id(0)

        @pl.when(i == 0)
        def _():
            small_ref[...] = jnp.zeros_like(small_ref)
            dws_ref[...] = jnp.zeros_like(dws_ref)
            dbs_ref[...] = jnp.zeros_like(dbs_ref)

        @pl.when(i % per_b == 0)
        def _():
            dmod_ref[...] = jnp.zeros_like(dmod_ref)

        osum, og, u, v, ga, gb = _read_tail_inputs(of_ref, ob_ref, pcols)
        ln_g = lng_ref[...]
        t = _tail_forward(osum, og, u, v, ga, gb, gna_ref[...], ln_g, lnb_ref[...], ws_ref, bs_ref, wpaT_ref, wpbT_ref)
        dx1v = dx1_ref[...]
        dmod_ref[0, 2:3, :] += jnp.sum(dx1v * mix_ref[...].astype(F32), axis=0, keepdims=True)
        dmix = dx1v * mod_ref[0, 2:3, :]
        dmix_ref[...] = dmix.astype(dmix_ref.dtype)
        dmerged = _dot(dmix, wo_ref[...], "nt")
        sga, sgb = t["sga"], t["sgb"]
        dpa = dmerged * sga
        dpb = dmerged * sgb
        dpa_ref[...] = dpa.astype(dpa_ref.dtype)
        dpb_ref[...] = dpb.astype(dpb_ref.dtype)
        dga = dmerged * t["pa"] * sga * (1.0 - sga)
        dgb = dmerged * t["pb"] * sgb * (1.0 - sgb)
        doa = _dot(dpa, wpaT_ref[...])
        dobm = _dot(dpb, wpbT_ref[...])
        dug = dobm * t["mixed"]
        dmixed = dobm * t["ug"]
        du = dug * _gelu_grad(u, t["tu"])
        dvln_blocks = []
        for n in range(tm // SGU_BLOCK):
            rs = slice(n * SGU_BLOCK, (n + 1) * SGU_BLOCK)
            parts = []
            for g in range(GROUPS):
                gs = slice(g * DK, (g + 1) * DK)
                dm = dmixed[rs, gs]
                parts.append(_dot(ws_ref[g], dm, "tn"))
                dws_ref[g] += _dot(dm, t["vln"][rs, gs], "nt")
                dbs_ref[g] += jnp.sum(dm, axis=1, keepdims=True)
            dvln_blocks.append(jnp.concatenate(parts, axis=1))
        dvln = jnp.concatenate(dvln_blocks, axis=0) if len(dvln_blocks) > 1 else dvln_blocks[0]
        vhat = t["vhat"]
        small_ref[1:2, 0:KW] += jnp.sum(dvln * vhat, axis=0, keepdims=True)
        small_ref[2:3, 0:KW] += jnp.sum(dvln, axis=0, keepdims=True)
        dvhat = dvln * ln_g
        dvg = t["rstd"] * (dvhat - jnp.mean(dvhat, axis=-1, keepdims=True)
                           - vhat * jnp.mean(dvhat * vhat, axis=-1, keepdims=True))
        dv = dvg * _gelu_grad(v, t["tv"])
        sg_og = t["sg_og"]
        doan = doa * t["silu_og"]
        dog = doa * t["oan"] * (sg_og * (1.0 + og * (1.0 - sg_og)))
        prod = doan * t["on"]
        dgna = jnp.zeros((1, DK), F32)
        for h in range(HEADS):
            dgna = dgna + jnp.sum(prod[:, h * DK:(h + 1) * DK], axis=0, keepdims=True)
        small_ref[0:1, 0:DK] += dgna
        don = doan * t["gna4"]
        dot_parts = []
        for h in range(HEADS):
            hs = slice(h * DK, (h + 1) * DK)
            m = jnp.mean(don[:, hs] * t["on"][:, hs], axis=-1, keepdims=True)
            dot_parts.append(t["r"][:, hs] * (don[:, hs] - t["on"][:, hs] * m))
        do_ref[...] = jnp.concatenate(dot_parts, axis=1).astype(do_ref.dtype)
        for j, val in enumerate((dog, du, dv)):
            dpt_ref[:, j * KW:(j + 1) * KW] = val.astype(dpt_ref.dtype)
        dpt_ref[:, 3 * KW:3 * KW + D] = dga.astype(dpt_ref.dtype)
        dpt_ref[:, 3 * KW + D:] = dgb.astype(dpt_ref.dtype)

    row = lambda w: pl.BlockSpec((tm, w), lambda i: (i, 0))
    in_specs = [row(KW), row(KW)] + _tail_in_specs(tm) + [row(D), row(D), pl.BlockSpec((1, N_MOD, D), lambda i: (i // per_b, 0, 0))]
    in_specs += _tail_weight_specs()
    args = [o_up, o_down, *([p] * 7), dx1, mix, modv, gna, ln_g, ln_b, w_s, b_s, w_paT, w_pbT, w_o]
    cd = MXU_DTYPE
    res, handle = _host_call(
        body, "tail_bwd", (rows // tm,), in_specs, args,
        [jax.ShapeDtypeStruct((rows, TAIL_COLS), cd), jax.ShapeDtypeStruct((rows, KW), cd),
         jax.ShapeDtypeStruct((rows, D), cd), jax.ShapeDtypeStruct((rows, D), cd),
         jax.ShapeDtypeStruct((rows, D), cd), jax.ShapeDtypeStruct((nb_ex, 8, D), F32),
         jax.ShapeDtypeStruct((8, D), F32), jax.ShapeDtypeStruct((GROUPS, SGU_BLOCK, SGU_BLOCK), F32),
         jax.ShapeDtypeStruct((GROUPS, SGU_BLOCK, 1), F32)],
        [row(TAIL_COLS), row(KW), row(D), row(D), row(D),
         pl.BlockSpec((1, 8, D), lambda i: (i // per_b, 0, 0)), _full((8, D)),
         _full((GROUPS, SGU_BLOCK, SGU_BLOCK)), _full((GROUPS, SGU_BLOCK, 1))], [],
        after=after, sender=sender)
    return (*res, handle)


def _ffn(x1, target, modv, g_ffn, g_final, w_upT, w_down, rows_per_example):
    rows = x1.shape[0]
    nb_ex = rows // rows_per_example
    tm = min(TOKEN_TILE, rows_per_example)
    per_b = rows_per_example // tm
    n_ff = D_FF // FF_CHUNK

    def body(x1_ref, tgt_ref, mod_ref, gffn_ref, gfin_ref, wup_ref, wdn_ref,
             dx1_ref, h2_ref, dffn_ref, act_ref, dup_ref, dmod_ref, small_ref, up_scr):
        i = pl.program_id(0)

        @pl.when(i == 0)
        def _():
            small_ref[...] = jnp.zeros_like(small_ref)

        @pl.when(i % per_b == 0)
        def _():
            dmod_ref[...] = jnp.zeros_like(dmod_ref)

        x1v = x1_ref[...]
        g2 = gffn_ref[...]
        m3, m4, m5 = mod_ref[0, 3:4, :], mod_ref[0, 4:5, :], mod_ref[0, 5:6, :]
        r2 = lax.rsqrt(jnp.mean(x1v * x1v, axis=-1, keepdims=True) + EPS)
        xn2 = x1v * r2
        h2 = (xn2 * g2) * (1.0 + m4) + m3
        h2b = h2.astype(MXU_DTYPE)
        h2_ref[...] = h2b
        def up_pair(j):
            lo = j * FF_CHUNK
            return (_dot(h2b, wup_ref[lo:lo + FF_CHUNK, :], "nt"),
                    _dot(h2b, wup_ref[D_FF + lo:D_FF + lo + FF_CHUNK, :], "nt"))

        group_end = {min(e, n_ff): s for s, e in ((0, 4), (4, 8), (8, 12))}
        cur, ffn = up_pair(0), None
        for j in range(n_ff):
            nxt = up_pair(j + 1) if j + 1 < n_ff else None
            cs = slice(j * FF_CHUNK, (j + 1) * FF_CHUNK)
            a, bgate = cur
            up_scr[:, cs] = a
            up_scr[:, D_FF + j * FF_CHUNK:D_FF + (j + 1) * FF_CHUNK] = bgate
            act_ref[:, cs] = (a * _sigmoid(a) * bgate).astype(MXU_DTYPE)
            cur = nxt
            if j + 1 in group_end:
                gs = slice(group_end[j + 1] * FF_CHUNK, (j + 1) * FF_CHUNK)
                part = _dot(act_ref[:, gs], wdn_ref[gs, :])
                ffn = part if ffn is None else ffn + part
        x2 = x1v + m5 * ffn
        r3 = lax.rsqrt(jnp.mean(x2 * x2, axis=-1, keepdims=True) + EPS)
        xn3 = x2 * r3
        gf = gfin_ref[...]
        err = xn3 * gf - tgt_ref[...]
        loss = 0.5 * jnp.sum(jnp.mean(err * err, axis=-1, keepdims=True), axis=0, keepdims=True)
        small_ref[2:3, :] += jnp.broadcast_to(loss, (1, D))
        dy = err * (1.0 / D)
        small_ref[1:2, :] += jnp.sum(dy * xn3, axis=0, keepdims=True)
        dxn3 = dy * gf
        dx2 = r3 * (dxn3 - xn3 * jnp.mean(dxn3 * xn3, axis=-1, keepdims=True))
        dmod_ref[0, 5:6, :] += jnp.sum(dx2 * ffn, axis=0, keepdims=True)
        dffn = (dx2 * m5).astype(MXU_DTYPE)
        dffn_ref[...] = dffn
        dact_of = lambda j: _dot(dffn, wdn_ref[j * FF_CHUNK:(j + 1) * FF_CHUNK, :], "nt")
        cur, dh2 = dact_of(0), None
        for j in range(n_ff):
            nxt = dact_of(j + 1) if j + 1 < n_ff else None
            cs = slice(j * FF_CHUNK, (j + 1) * FF_CHUNK)
            a, bgate = up_scr[:, cs], up_scr[:, D_FF + j * FF_CHUNK:D_FF + (j + 1) * FF_CHUNK]
            s = _sigmoid(a)
            dup_ref[:, cs] = (cur * bgate * (s * (1.0 + a * (1.0 - s)))).astype(MXU_DTYPE)
            dup_ref[:, D_FF + j * FF_CHUNK:D_FF + (j + 1) * FF_CHUNK] = (cur * a * s).astype(MXU_DTYPE)
            cur = nxt
            if j + 1 in group_end:
                lo, hi = group_end[j + 1] * FF_CHUNK, (j + 1) * FF_CHUNK
                part = (_dot(dup_ref[:, lo:hi], wup_ref[lo:hi, :])
                        + _dot(dup_ref[:, D_FF + lo:D_FF + hi], wup_ref[D_FF + lo:D_FF + hi, :]))
                dh2 = part if dh2 is None else dh2 + part
        dmod_ref[0, 3:4, :] += jnp.sum(dh2, axis=0, keepdims=True)
        dmod_ref[0, 4:5, :] += jnp.sum(dh2 * xn2 * g2, axis=0, keepdims=True)
        small_ref[0:1, :] += jnp.sum(dh2 * (1.0 + m4) * xn2, axis=0, keepdims=True)
        dxn2 = dh2 * g2 * (1.0 + m4)
        dx1_ref[...] = dx2 + r2 * (dxn2 - xn2 * jnp.mean(dxn2 * xn2, axis=-1, keepdims=True))

    row = lambda w: pl.BlockSpec((tm, w), lambda i: (i, 0))
    cd = MXU_DTYPE
    return pl.pallas_call(
        body, name="ffn_fwd_bwd", grid=(rows // tm,),
        out_shape=(jax.ShapeDtypeStruct((rows, D), F32), jax.ShapeDtypeStruct((rows, D), cd),
                   jax.ShapeDtypeStruct((rows, D), cd), jax.ShapeDtypeStruct((rows, D_FF), cd),
                   jax.ShapeDtypeStruct((rows, 2 * D_FF), cd), jax.ShapeDtypeStruct((nb_ex, 8, D), F32),
                   jax.ShapeDtypeStruct((8, D), F32)),
        in_specs=[row(D), row(D), pl.BlockSpec((1, N_MOD, D), lambda i: (i // per_b, 0, 0)), _full((1, D)), _full((1, D)),
                  _full((2 * D_FF, D), single=True), _full((D_FF, D), single=True)],
        out_specs=(row(D), row(D), row(D), row(D_FF), row(2 * D_FF),
                   pl.BlockSpec((1, 8, D), lambda i: (i // per_b, 0, 0)), _full((8, D))),
        scratch_shapes=[pltpu.VMEM((tm, 2 * D_FF), F32)],
        compiler_params=_params(("arbitrary",)),
    )(x1, target, modv, g_ffn, g_final, w_upT, w_down)


def _scan_columns(up, down, n_groups):
    cols = [up[:, 0:KW].astype(F32), down[:, 0:KW].astype(F32)]
    for j in range(1, n_groups):
        cols.append(up[:, j * KW:(j + 1) * KW].astype(F32) + down[:, j * KW:(j + 1) * KW].astype(F32))
    return cols


def _inproj_bwd(d_up, d_down, dpt, xt, dx1, modv, g, w_inT, rows_per_example, name, sender=None):
    rows = xt.shape[0]
    latent = dx1 is not None
    n_cols = IN_COLS if latent else CTX_COLS
    n_groups = d_up.shape[1] // KW
    tm = min(TOKEN_TILE, rows_per_example)
    per_b = rows_per_example // tm
    n_mod_blocks = rows // rows_per_example if latent else 1

    def body(*refs):
        it = iter(refs)
        up_ref, down_ref = next(it), next(it)
        dpt_ref = next(it) if latent else None
        x_ref = next(it)
        dx1_ref = next(it) if latent else None
        mod_ref, g_ref, w_ref = next(it), next(it), next(it)
        gx_ref = next(it) if latent else None
        dp_out = None if latent else next(it)
        dmod_ref, small_ref = next(it), next(it)
        dp_ref = next(it) if latent else dp_out
        i = pl.program_id(0)

        @pl.when(i == 0)
        def _():
            small_ref[...] = jnp.zeros_like(small_ref)

        @pl.when((i % per_b == 0) if latent else (i == 0))
        def _():
            dmod_ref[...] = jnp.zeros_like(dmod_ref)

        for j, val in enumerate(_scan_columns(up_ref[...], down_ref[...], n_groups)):
            dp_ref[:, j * KW:(j + 1) * KW] = val.astype(MXU_DTYPE)
        if latent:
            dh = _dot(dp_ref[...], w_ref[0:4 * KW, :]) + _dot(dpt_ref[...], w_ref[4 * KW:, :])
        else:
            dh = _dot(dp_ref[...], w_ref[...])
        x = x_ref[...]
        gv = g_ref[...]
        m1 = mod_ref[0, 1:2, :]
        r = lax.rsqrt(jnp.mean(x * x, axis=-1, keepdims=True) + EPS)
        xn = x * r
        dmod_ref[0, 0:1, :] += jnp.sum(dh, axis=0, keepdims=True)
        dmod_ref[0, 1:2, :] += jnp.sum(dh * xn * gv, axis=0, keepdims=True)
        small_ref[0:1, :] += jnp.sum(dh * (1.0 + m1) * xn, axis=0, keepdims=True)
        if latent:
            dxn = dh * gv * (1.0 + m1)
            gx_ref[...] = dx1_ref[...] + r * (dxn - xn * jnp.mean(dxn * xn, axis=-1, keepdims=True))

    row = lambda w: pl.BlockSpec((tm, w), lambda i: (i, 0))
    mod_idx = (lambda i: (i // per_b, 0, 0)) if latent else (lambda i: (0, 0, 0))
    in_specs = [row(n_groups * KW)] * 2 + ([row(TAIL_COLS)] if latent else []) + [row(D)] + ([row(D)] if latent else [])
    in_specs += [pl.BlockSpec((1, N_MOD, D), mod_idx), _full((1, D)),
                 pl.BlockSpec((n_cols, D), lambda i: (0, 0), pipeline_mode=pl.Buffered(1))]
    args = [d_up, d_down] + ([dpt] if latent else []) + [xt] + ([dx1] if latent else []) + [modv, g, w_inT]
    first = jax.ShapeDtypeStruct((rows, D), F32) if latent else jax.ShapeDtypeStruct((rows, n_cols), MXU_DTYPE)
    out_shape = [first, jax.ShapeDtypeStruct((n_mod_blocks, 8, D), F32), jax.ShapeDtypeStruct((8, D), F32)]
    out_specs = [row(D) if latent else row(n_cols), pl.BlockSpec((1, 8, D), mod_idx), _full((8, D))]
    scratch = [pltpu.VMEM((tm, 4 * KW), MXU_DTYPE)] if latent else []
    res, handle = _host_call(body, name, (rows // tm,), in_specs, args, out_shape, out_specs, scratch, sender=sender)
    return (*res, handle)


def _grad_matmul(a, b, name, init=None, tn=512, sender=None):
    rows, n = a.shape
    k = b.shape[1]
    tn = min(tn, n)
    has_init = init is not None
    init_blocks = init.shape[0] // tn if has_init else 0

    def body(*refs):
        if has_init:
            a_ref, b_ref, init_ref, o_ref = refs
        else:
            a_ref, b_ref, o_ref = refs
        g = _dot(a_ref[...], b_ref[...], "tn")
        if has_init:
            g = g + jnp.where(pl.program_id(0) < init_blocks, init_ref[...].astype(F32), 0.0)
        o_ref[...] = g.astype(o_ref.dtype)

    in_specs = [pl.BlockSpec((rows, tn), lambda i: (0, i)), _full((rows, k), single=True)]
    args = [a, b]
    if has_init:
        in_specs.append(pl.BlockSpec((tn, k), lambda i: (jnp.minimum(i, init_blocks - 1), 0)))
        args.append(init)
    (out,), handle = _host_call(
        body, name, (n // tn,), in_specs, args, [jax.ShapeDtypeStruct((n, k), PAYLOAD_DTYPE)],
        [pl.BlockSpec((tn, k), lambda i: (i, 0))], [], sender=sender)
    return out, handle


def _grad_in(d_up, d_down, dpt, h, init, sender=None):
    rows = h.shape[0]
    tn = 256
    per_group = KW // tn
    n_scan = 4 * per_group
    init_blocks = init.shape[0] // tn

    def body(up_ref, down_ref, dpt_ref, h_ref, init_ref, o_ref):
        i = pl.program_id(0)
        both = (up_ref[...].astype(F32) + down_ref[...].astype(F32)).astype(MXU_DTYPE)
        a = jnp.where(i < per_group, up_ref[...],
                      jnp.where(i < 2 * per_group, down_ref[...], jnp.where(i < n_scan, both, dpt_ref[...])))
        g = _dot(a, h_ref[...], "tn") + jnp.where(i < init_blocks, init_ref[...].astype(F32), 0.0)
        o_ref[...] = g.astype(o_ref.dtype)

    last = 3 * per_group - 1
    col = lambda f: pl.BlockSpec((rows, tn), lambda i: (0, f(i)))
    in_specs = [col(lambda i: jnp.clip(jnp.where(i < per_group, i, i - per_group), 0, last)),
                col(lambda i: jnp.clip(i - per_group, 0, last)),
                col(lambda i: jnp.clip(i - n_scan, 0, TAIL_COLS // tn - 1)),
                _full((rows, D), single=True),
                pl.BlockSpec((tn, D), lambda i: (jnp.minimum(i, init_blocks - 1), 0))]
    (out,), handle = _host_call(
        body, "gw_in", (IN_COLS // tn,), in_specs, [d_up, d_down, dpt, h, init],
        [jax.ShapeDtypeStruct((IN_COLS, D), PAYLOAD_DTYPE)], [pl.BlockSpec((tn, D), lambda i: (i, 0))], [],
        sender=sender)
    return out, handle


def _row_tile(rows, limit=256):
    if rows <= limit:
        return rows
    for t in range(limit, 7, -8):
        if rows % t == 0:
            return t
    return rows


def _sum8(stack, name):
    _, rows, cols = stack.shape
    tr = _row_tile(rows)

    def body(s_ref, o_ref):
        acc = s_ref[0].astype(F32)
        for j in range(1, N_DEV):
            acc = acc + s_ref[j].astype(F32)
        o_ref[...] = acc

    return pl.pallas_call(
        body, name=name, grid=(rows // tr,), out_shape=jax.ShapeDtypeStruct((rows, cols), F32),
        in_specs=[pl.BlockSpec((N_DEV, tr, cols), lambda i: (0, i, 0))],
        out_specs=pl.BlockSpec((tr, cols), lambda i: (i, 0)),
        compiler_params=_params(("arbitrary",)),
    )(stack)


def _adamw_update(w, gv, m, v):
    nm = ADAM_B1 * m + (1.0 - ADAM_B1) * gv
    nv = ADAM_B2 * v + (1.0 - ADAM_B2) * (gv * gv)
    m_hat = nm / (1.0 - ADAM_B1 ** ADAM_STEP)
    v_hat = nv / (1.0 - ADAM_B2 ** ADAM_STEP)
    return -ADAM_LR * (m_hat / (jnp.sqrt(v_hat) + ADAM_EPS) + ADAM_WD * w), nm, nv


SMALL_PARAMS = (("g_mix", 0, D), ("g_ffn", 1, D), ("g_final", 2, D), ("g_norm_a", 3, DK), ("ln_v_g", 4, KW),
                ("ln_v_b", 5, KW), ("b_s", 6, GROUPS * SGU_BLOCK))


def _small_finish(early, late, dws, gam, nb_ex, params):
    names = [n for n, _, _ in SMALL_PARAMS] + ["b_mod", "w_s"]

    def body(*refs):
        s_ref, l_ref, dws_ref, gam_ref = refs[:4]
        p_refs = refs[4:4 + 3 * len(names)]
        tot_ref, dgam_ref = refs[4 + 3 * len(names):6 + 3 * len(names)]
        o_refs = refs[6 + 3 * len(names):]
        acc = s_ref[0] + l_ref[0]
        gws = dws_ref[0]
        for j in range(1, N_DEV):
            acc = acc + (s_ref[j] + l_ref[j])
            gws = gws + dws_ref[j]
        tot_ref[...] = acc
        bm = acc[8:8 + N_MOD, :]
        for e in range(nb_ex):
            bm = bm + acc[16 + e * N_MOD:16 + (e + 1) * N_MOD, :]
        lb = jnp.concatenate([_lower_bound(gam_ref, 0), _lower_bound(gam_ref, 1)], axis=1)
        dgam = acc[7:8, :] * lb * (1.0 - lb)
        dgam_ref[...] = jnp.concatenate([dgam, -dgam], axis=0)
        grads = [acc[row:row + 1, 0:width] for _, row, width in SMALL_PARAMS] + [bm, gws]
        for k, g in enumerate(grads):
            w_ref, m_ref, v_ref = p_refs[3 * k:3 * k + 3]
            o_refs[4 * k][...] = g
            o_refs[4 * k + 1][...], o_refs[4 * k + 2][...], o_refs[4 * k + 3][...] = _adamw_update(
                w_ref[...], g, m_ref[...], v_ref[...])

    p_args, p_specs, o_shapes, o_specs = [], [], [], []
    for n in names:
        for a in params[n]:
            p_args.append(a)
            p_specs.append(_full(a.shape))
        o_shapes += [jax.ShapeDtypeStruct(params[n][0].shape, F32)] * 4
        o_specs += [_full(params[n][0].shape)] * 4
    res = pl.pallas_call(
        body, name="small_finish", grid=(1,),
        out_shape=[jax.ShapeDtypeStruct((SMALL_ROWS, D), F32), jax.ShapeDtypeStruct((2, D), F32)] + o_shapes,
        in_specs=[_full(early.shape), _full(late.shape), _full(dws.shape), _full((4, KW))] + p_specs,
        out_specs=[_full((SMALL_ROWS, D)), _full((2, D))] + o_specs,
        compiler_params=_params(("arbitrary",)),
    )(early, late, dws, gam, *p_args)
    return res[0], res[1], {n: res[2 + 4 * k:6 + 4 * k] for k, n in enumerate(names)}


def _adamw_sum8(stack, w, m, v, name):
    _, rows, cols = stack.shape
    tr = _row_tile(rows)

    def body(s_ref, w_ref, m_ref, v_ref, g_ref, d_ref, nm_ref, nv_ref):
        gv = s_ref[0].astype(F32)
        for j in range(1, N_DEV):
            gv = gv + s_ref[j].astype(F32)
        g_ref[...] = gv
        d_ref[...], nm_ref[...], nv_ref[...] = _adamw_update(w_ref[...], gv, m_ref[...], v_ref[...])

    blk = pl.BlockSpec((tr, cols), lambda i: (i, 0))
    sd = jax.ShapeDtypeStruct((rows, cols), F32)
    return pl.pallas_call(
        body, name=name, grid=(rows // tr,), out_shape=(sd, sd, sd, sd),
        in_specs=[pl.BlockSpec((N_DEV, tr, cols), lambda i: (0, i, 0)), blk, blk, blk], out_specs=(blk, blk, blk, blk),
        compiler_params=_params(("arbitrary",)),
    )(stack, w, m, v)


def _adamw(w, g, m, v, name):
    shape = w.shape
    cols = shape[-1]
    rows = 1
    for s in shape[:-1]:
        rows *= s
    tr = _row_tile(rows)

    def body(w_ref, g_ref, m_ref, v_ref, d_ref, nm_ref, nv_ref):
        gv = g_ref[...]
        nm = ADAM_B1 * m_ref[...] + (1.0 - ADAM_B1) * gv
        nv = ADAM_B2 * v_ref[...] + (1.0 - ADAM_B2) * (gv * gv)
        m_hat = nm / (1.0 - ADAM_B1 ** ADAM_STEP)
        v_hat = nv / (1.0 - ADAM_B2 ** ADAM_STEP)
        d_ref[...] = -ADAM_LR * (m_hat / (jnp.sqrt(v_hat) + ADAM_EPS) + ADAM_WD * w_ref[...])
        nm_ref[...] = nm
        nv_ref[...] = nv

    blk = pl.BlockSpec((tr, cols), lambda i: (i, 0))
    sd = jax.ShapeDtypeStruct((rows, cols), F32)
    d, nm, nv = pl.pallas_call(
        body, name=name, grid=(rows // tr,), out_shape=(sd, sd, sd), in_specs=[blk] * 4, out_specs=(blk, blk, blk),
        compiler_params=_params(("arbitrary",)),
    )(w.reshape(rows, cols), g.reshape(rows, cols), m.reshape(rows, cols), v.reshape(rows, cols))
    return d.reshape(shape), nm.reshape(shape), nv.reshape(shape)


def _owner_blocks(a):
    return a.reshape(N_DEV, a.shape[0] // N_DEV, a.shape[1])


class _LocalWeights:
    def __init__(self, w_upT, w_down, w_o, w_paT, w_pbT):
        self.weights = (w_upT, w_down, w_o, w_paT, w_pbT)
        self.items = {}

    def sender(self, stage, items=None):
        self.items[stage] = items
        return None

    def sent(self, stage, handle):
        pass

    def mixer_weights(self, after):
        return self.weights[1:]

    def ffn_weights(self, after):
        return self.weights[0]


def _local_step(x, ctx, target, modv, mcv, gam, g_mix, g_ffn, gna, ln_g, ln_b, w_s, b_s, g_final, w_inT, comm):
    nb_ex, seq, _ = x.shape
    ctx_len = ctx.shape[1]
    xt = x.reshape(nb_ex * seq, D)
    ct = ctx.reshape(nb_ex * ctx_len, D)
    tgt = target.reshape(nb_ex * seq, D)
    bs3 = b_s.reshape(GROUPS, SGU_BLOCK, 1)

    pc, hc, _ = _inproj(ct, mcv, g_mix, w_inT, CTX_COLS, ctx_len, "inproj_ctx")
    p, h, handle = _inproj(xt, modv, g_mix, w_inT, IN_COLS, seq, "inproj_lat", sender=comm.sender("inproj"))
    comm.sent("inproj", handle)
    cst_f, cst_b, s_ctx, _ = _hgrn_fwd(pc, gam, None, ctx_len, False, "hgrn_fwd_ctx")
    o_up, o_down, st_f, st_b, _, handle = _hgrn_fwd(p, gam, s_ctx, seq, True, "hgrn_fwd_lat",
                                                    sender=comm.sender("scan"))
    comm.sent("scan", handle)
    w_down, w_o, w_paT, w_pbT = comm.mixer_weights(o_up)
    x1, mix, merged, oa, obm = _tail_fwd(p, o_up, o_down, xt, modv, gna, ln_g, ln_b, w_s, bs3, w_paT, w_pbT, w_o, seq)
    w_upT = comm.ffn_weights(x1)
    dx1, h2, dffn, act, dup, dmod_ffn, small_ffn = _ffn(x1, tgt, modv, g_ffn, g_final, w_upT, w_down, seq)
    gw_upT, _ = _grad_matmul(dup, h2, "gw_up")
    gw_down, _ = _grad_matmul(act, dffn, "gw_down", tn=256)
    scatter = lambda *grads: [(_owner_blocks(g), "scatter") for g in grads]
    dpt, do, dmix, dpa, dpb, dmod_tail, small_tail, dws, dbs, handle = _tail_bwd(
        p, o_up, o_down, dx1, mix, modv, gna, ln_g, ln_b, w_s, bs3, w_paT, w_pbT, w_o, seq,
        sender=comm.sender("tail_bwd", scatter(gw_upT)))
    comm.sent("tail_bwd", handle)
    gw_o, _ = _grad_matmul(merged, dmix, "gw_o")
    gw_paT, _ = _grad_matmul(dpa, oa, "gw_pa")
    gw_pbT, _ = _grad_matmul(dpb, obm, "gw_pb")
    def at_row(row, a):
        return jnp.pad(a, ((row, SMALL_ROWS - row - a.shape[0]), (0, D - a.shape[1])))

    small_early = (at_row(1, small_ffn[0:2])
                   + at_row(3, small_tail[0:3])
                   + at_row(6, dbs.reshape(1, GROUPS * SGU_BLOCK))
                   + at_row(14, small_ffn[2:3]))
    dws_rows = dws.reshape(GROUPS * SGU_BLOCK, SGU_BLOCK)
    d_up, d_down, dlb, ds0, handle = _hgrn_bwd(
        p, gam, do, st_f, st_b, None, seq, True, "hgrn_bwd_lat",
        sender=comm.sender("scan_bwd", scatter(gw_down, gw_o, gw_paT, gw_pbT)
                           + [(small_early, "gather"), (dws_rows, "gather")]))
    comm.sent("scan_bwd", handle)
    c_up, c_down, dlb_c, _, _ = _hgrn_bwd(pc, gam, None, cst_f, cst_b, ds0, ctx_len, False, "hgrn_bwd_ctx")
    dpc, dmc, small_c, _ = _inproj_bwd(c_up, c_down, None, ct, None, mcv, g_mix, w_inT, ctx_len, "inproj_bwd_ctx")
    gw_inT, _ = _grad_in(d_up, d_down, dpt, h, _grad_matmul(dpc, hc, "gw_in_ctx")[0])
    grad_x, dmod_in, small_in, handle = _inproj_bwd(d_up, d_down, dpt, xt, dx1, modv, g_mix, w_inT, seq,
                                                   "inproj_bwd_lat", sender=comm.sender("inproj_bwd", scatter(gw_inT)))
    comm.sent("inproj_bwd", handle)
    dmod = dmod_in + dmod_tail + dmod_ffn
    small_late = (at_row(0, small_in[0:1] + small_c[0:1])
                  + at_row(7, (dlb + dlb_c).reshape(1, 2 * KW))
                  + at_row(8, dmc[0, 0:N_MOD])
                  + at_row(16, dmod[:, 0:N_MOD].reshape(nb_ex * N_MOD, D)))
    comm.sender("last", [(small_late, "gather")])
    return grad_x.reshape(x.shape)


def kernel(x, c, ctx, c_ctx, w_mod, b_mod, g_mix, g_ffn, w_in, lb_gamma, g_norm_a, ln_v_g, ln_v_b, w_s, b_s, w_pa, w_pb, w_o, w_up, w_down, g_final, loss_target, m_c_ctx, m_w_mod, m_b_mod, m_g_mix, m_g_ffn, m_w_in, m_lb_gamma, m_g_norm_a, m_ln_v_g, m_ln_v_b, m_w_s, m_b_s, m_w_pa, m_w_pb, m_w_o, m_w_up, m_w_down, m_g_final, v_c_ctx, v_w_mod, v_b_mod, v_g_mix, v_g_ffn, v_w_in, v_lb_gamma, v_g_norm_a, v_ln_v_g, v_ln_v_b, v_w_s, v_b_s, v_w_pa, v_w_pb, v_w_o, v_w_up, v_w_down, v_g_final):
    nb_ex = x.shape[0]
    me = 4 * lax.axis_index("x") + 2 * lax.axis_index("y") + lax.axis_index("c")
    cd = MXU_DTYPE
    mod_cols = w_mod.shape[2]
    lb_cols = lb_gamma.shape[2]

    w_inT_l = w_in[0].T.astype(cd)
    w_upT_l = w_up[0].T.astype(cd)
    w_paT_l = w_pa[0].T.astype(cd)
    w_pbT_l = w_pb[0].T.astype(cd)
    cl = jnp.concatenate([c, jnp.pad(lb_gamma.reshape(1, 4 * lb_cols), ((0, 0), (0, D - 4 * lb_cols))),
                          jnp.zeros((8 - nb_ex - 1, D), F32)], axis=0)
    g_in, g_cl = _gather_two_level([w_inT_l, cl], "gather_w_in")
    w_inT = g_in.reshape(IN_COLS, D)
    c_all = g_cl[:, 0:nb_ex].reshape(N_DEV * nb_ex, D)
    gam = jnp.transpose(g_cl[:, nb_ex, 0:4 * lb_cols].reshape(N_DEV, 4, lb_cols), (1, 0, 2)).reshape(4, KW)

    n_c = N_DEV * nb_ex
    cvec = jnp.concatenate([c_all, c_ctx.reshape(1, D), jnp.zeros((7, D), F32)], axis=0)
    b_mod_l = lax.dynamic_slice(b_mod, (0, me * mod_cols), (1, mod_cols))
    mod_l, svec = _mod_fwd(cvec, w_mod[0], b_mod_l)
    (g_mod,) = _gather_two_level([mod_l], "gather_mod")
    mod_all = jnp.transpose(g_mod, (1, 0, 2)).reshape(n_c + 8, N_MOD * D)
    modv = lax.dynamic_slice(mod_all, (me * nb_ex, 0), (nb_ex, N_MOD * D)).reshape(nb_ex, N_MOD, D)
    mcv = mod_all[n_c].reshape(1, N_MOD, D)

    handles, leftover = {}, {}

    class Comm:
        def sender(self, stage, items=None):
            if stage == "inproj":
                return _Sender([(w_down[0].astype(cd), "gather"), (w_o[0].astype(cd), "gather"), (w_paT_l, "gather"),
                                (w_pbT_l, "gather")])
            if stage == "scan":
                return _Sender([(w_upT_l, "gather")])
            if stage == "last":
                leftover["items"] = items
                return None
            return _Sender(items)

        def sent(self, stage, handle):
            handles[stage] = handle

        def mixer_weights(self, after):
            g_down, g_o, g_pa, g_pb = _exchange_wait(handles["inproj"], after)
            return g_down.reshape(D_FF, D), g_o.reshape(D, D), g_pa.reshape(D, KW), g_pb.reshape(D, KW)

        def ffn_weights(self, after):
            (g_up,) = _exchange_wait(handles["scan"], after)
            return g_up.reshape(2 * D_FF, D)

    grad_x = _local_step(
        x, ctx, loss_target, modv, mcv, gam, g_mix, g_ffn, g_norm_a, ln_v_g, ln_v_b, w_s[0], b_s[0],
        g_final.reshape(1, D), w_inT, Comm())
    last, last_started = _exchange_start(leftover["items"], "gather_small_late", after=leftover["items"][0][0])

    (r_up,) = _exchange_wait(handles["tail_bwd"], last_started)
    r_down, r_o, r_pa, r_pb, r_small, r_dws = _exchange_wait(handles["scan_bwd"], r_up)
    raw_up = _adamw_sum8(r_up, w_up[0].T, m_w_up[0].T, v_w_up[0].T, "adamw_w_up")
    raw_down = _adamw_sum8(r_down, w_down[0], m_w_down[0], v_w_down[0], "adamw_w_down")
    raw_o = _adamw_sum8(r_o, w_o[0], m_w_o[0], v_w_o[0], "adamw_w_o")
    (r_in,) = _exchange_wait(handles["inproj_bwd"], raw_up[1])
    raw_in = _adamw_sum8(r_in, w_in[0].T, m_w_in[0].T, v_w_in[0].T, "adamw_w_in")
    (r_late,) = _exchange_wait(last, raw_in[1])
    done = {"w_in": [a.T[None] for a in raw_in], "w_up": [a.T[None] for a in raw_up],
            "w_down": [a[None] for a in raw_down], "w_o": [a[None] for a in raw_o]}
    grad_w_in, grad_w_up, grad_w_down, grad_w_o = (done[k][0] for k in ("w_in", "w_up", "w_down", "w_o"))
    grad_w_pa = _sum8(r_pa, "sum_w_pa").T[None]
    grad_w_pb = _sum8(r_pb, "sum_w_pb").T[None]
    as_2d = {"g_final": (1, D), "b_s": (1, GROUPS * SGU_BLOCK), "b_mod": (N_MOD, D), "w_s": (GROUPS * SGU_BLOCK, SGU_BLOCK)}
    small_params = {"g_mix": (g_mix, m_g_mix, v_g_mix), "g_ffn": (g_ffn, m_g_ffn, v_g_ffn),
                    "g_final": (g_final, m_g_final, v_g_final), "g_norm_a": (g_norm_a, m_g_norm_a, v_g_norm_a),
                    "ln_v_g": (ln_v_g, m_ln_v_g, v_ln_v_g), "ln_v_b": (ln_v_b, m_ln_v_b, v_ln_v_b),
                    "b_s": (b_s, m_b_s, v_b_s), "b_mod": (b_mod, m_b_mod, v_b_mod), "w_s": (w_s, m_w_s, v_w_s)}
    tot, dgam, small_done = _small_finish(
        r_small, r_late, r_dws, gam, nb_ex,
        {n: tuple(a.reshape(as_2d.get(n, a.shape)) for a in wmv) for n, wmv in small_params.items()})
    for n, outs in small_done.items():
        done[n] = [a.reshape(small_params[n][0].shape) for a in outs]
    loss = tot[14, 0]
    grad_g_mix, grad_g_ffn, grad_g_final, grad_g_norm_a, grad_ln_v_g, grad_ln_v_b, grad_b_s, grad_b_mod, grad_w_s = (
        done[n][0] for n in ("g_mix", "g_ffn", "g_final", "g_norm_a", "ln_v_g", "ln_v_b", "b_s", "b_mod", "w_s"))
    grad_lb_gamma = lax.dynamic_slice(dgam.reshape(2, 2, KW), (0, 0, me * lb_cols), (2, 2, lb_cols))

    dmod_all = r_late[:, 16:16 + nb_ex * N_MOD].reshape(n_c, N_MOD * D)
    dmod_l = jnp.concatenate([lax.dynamic_slice(dmod_all, (0, me * mod_cols), (n_c, mod_cols)),
                              lax.dynamic_slice(tot[8:8 + N_MOD].reshape(1, N_MOD * D), (0, me * mod_cols), (1, mod_cols)),
                              jnp.zeros((7, mod_cols), F32)], axis=0)
    gw_mod, gc = _mod_bwd(svec, cvec, dmod_l, w_mod[0])
    grad_w_mod = gw_mod[None]
    (r_gc,) = _exchange([(gc[n_c:n_c + 8], "gather")], "gather_c_ctx", after=r_late)
    grad_c_ctx = _sum8(r_gc, "sum_c_ctx")[0]

    names = ["c_ctx", "w_mod", "b_mod", "g_mix", "g_ffn", "w_in", "lb_gamma", "g_norm_a", "ln_v_g", "ln_v_b", "w_s",
             "b_s", "w_pa", "w_pb", "w_o", "w_up", "w_down", "g_final"]
    weights = [c_ctx, w_mod, b_mod, g_mix, g_ffn, w_in, lb_gamma, g_norm_a, ln_v_g, ln_v_b, w_s, b_s, w_pa, w_pb, w_o,
               w_up, w_down, g_final]
    grads = [grad_c_ctx, grad_w_mod, grad_b_mod, grad_g_mix, grad_g_ffn, grad_w_in, grad_lb_gamma, grad_g_norm_a,
             grad_ln_v_g, grad_ln_v_b, grad_w_s, grad_b_s, grad_w_pa, grad_w_pb, grad_w_o, grad_w_up, grad_w_down,
             grad_g_final]
    ms = [m_c_ctx, m_w_mod, m_b_mod, m_g_mix, m_g_ffn, m_w_in, m_lb_gamma, m_g_norm_a, m_ln_v_g, m_ln_v_b, m_w_s, m_b_s,
          m_w_pa, m_w_pb, m_w_o, m_w_up, m_w_down, m_g_final]
    vs = [v_c_ctx, v_w_mod, v_b_mod, v_g_mix, v_g_ffn, v_w_in, v_lb_gamma, v_g_norm_a, v_ln_v_g, v_ln_v_b, v_w_s, v_b_s,
          v_w_pa, v_w_pb, v_w_o, v_w_up, v_w_down, v_g_final]
    deltas, new_ms, new_vs = [], [], []
    for nm, w, g, m, v in zip(names, weights, grads, ms, vs):
        d, nm_, nv_ = done[nm][1:] if nm in done else _adamw(w, g.reshape(w.shape), m, v, "adamw_" + nm)
        deltas.append(d)
        new_ms.append(nm_)
        new_vs.append(nv_)
    grads = [g.reshape(w.shape) for g, w in zip(grads, weights)]
    return (loss, grad_x, *grads, *deltas, *new_ms, *new_vs)
```

```python
import functools

import jax
import jax.numpy as jnp
from jax import lax
from jax.experimental import pallas as pl
from jax.experimental.pallas import tpu as pltpu

F32 = jnp.float32
MXU_DTYPE = jnp.bfloat16
PAYLOAD_DTYPE = jnp.bfloat16

N_DEV = 8
D = 1024
HEADS = 4
DK = 128
KW = HEADS * DK
CHUNK = 64
SGU_BLOCK = 128
GROUPS = 4
D_FF = 2816
FF_CHUNK = 256
N_MOD = 6
IN_COLS = 5632
CTX_COLS = 1536
TAIL_COLS = IN_COLS - 4 * KW
EPS = 1e-6
ADAM_LR, ADAM_B1, ADAM_B2, ADAM_EPS, ADAM_WD, ADAM_STEP = 0.001, 0.9, 0.999, 1e-08, 0.01, 10

VMEM_LIMIT = 56 * 1024 * 1024
TOKEN_TILE = 256
PROJ_TILE = 512
TAIL_TILE = 512
SMALL_ROWS = 40


def _params(sem):
    return pltpu.CompilerParams(dimension_semantics=sem, vmem_limit_bytes=VMEM_LIMIT)


_DN = {"nn": (((1,), (0,)), ((), ())), "nt": (((1,), (1,)), ((), ())), "tn": (((0,), (0,)), ((), ()))}


def _dot(a, b, form="nn"):
    return lax.dot_general(a.astype(MXU_DTYPE), b.astype(MXU_DTYPE), _DN[form], preferred_element_type=F32)


def _mask_dot(mask, v):
    bf = jnp.bfloat16
    hi = v.astype(bf)
    r1 = v - hi.astype(F32)
    mid = r1.astype(bf)
    lo = (r1 - mid.astype(F32)).astype(bf)
    w = v.shape[1]
    s = lax.dot_general(mask.astype(bf), jnp.concatenate([hi, mid, lo], axis=1), _DN["nn"], preferred_element_type=F32)
    return (s[:, 2 * w:] + s[:, w:2 * w]) + s[:, :w]


def _full(shape, single=False):
    n = len(shape)
    if single:
        return pl.BlockSpec(shape, lambda *_: (0,) * n, pipeline_mode=pl.Buffered(1))
    return pl.BlockSpec(shape, lambda *_: (0,) * n)


def _ordered_behind(body, in_specs, args, after):
    if after is None:
        return body
    at = len(in_specs)
    in_specs.append(pl.BlockSpec(memory_space=pl.ANY))
    args.append(after)
    return lambda *refs: body(*refs[:at], *refs[at + 1:])


def _sigmoid(z):
    return 0.5 * jnp.tanh(0.5 * z) + 0.5


def _gelu(x):
    c = 0.7978845608028654
    t = jnp.tanh(c * (x + 0.044715 * x * x * x))
    return 0.5 * x * (1.0 + t), t


def _gelu_grad(x, t):
    c = 0.7978845608028654
    return 0.5 * (1.0 + t) + 0.5 * x * (1.0 - t * t) * c * (1.0 + 3 * 0.044715 * x * x)


def _exchange(items, name, after=None):
    n = len(items)
    out_shape = []
    for a, mode in items:
        blk = a.shape if mode == "gather" else a.shape[1:]
        out_shape.append(jax.ShapeDtypeStruct((N_DEV,) + tuple(blk), a.dtype))

    def body(*refs):
        srcs, dsts = refs[:n], refs[n:2 * n]
        send_sems, recv_sems, local_sems = refs[2 * n:]
        x, y, c = lax.axis_index("x"), lax.axis_index("y"), lax.axis_index("c")
        me = 4 * x + 2 * y + c

        def src_for(i, dev):
            return srcs[i] if items[i][1] == "gather" else srcs[i].at[dev]

        local = [pltpu.make_async_copy(src_for(i, me), dsts[i].at[me], local_sems.at[i]) for i in range(n)]
        for cp in local:
            cp.start()
        remote = []
        for k in range(1, N_DEV):
            px = jnp.bitwise_xor(x, (k >> 2) & 1)
            py = jnp.bitwise_xor(y, (k >> 1) & 1)
            pc = jnp.bitwise_xor(c, k & 1)
            peer = 4 * px + 2 * py + pc
            for i in range(n):
                cp = pltpu.make_async_remote_copy(
                    src_ref=src_for(i, peer), dst_ref=dsts[i].at[me],
                    send_sem=send_sems.at[i * (N_DEV - 1) + k - 1], recv_sem=recv_sems.at[i * (N_DEV - 1) + k - 1],
                    device_id=(px, py, pc), device_id_type=pl.DeviceIdType.MESH)
                cp.start()
                remote.append(cp)
        for cp in remote:
            cp.wait()
        for cp in local:
            cp.wait()

    any_spec = pl.BlockSpec(memory_space=pl.ANY)
    in_specs, args = [any_spec] * n, [a for a, _ in items]
    if after is not None:
        in_specs.append(any_spec)
        args.append(after)
        exchange = body
        body = lambda *refs: exchange(*refs[:n], *refs[n + 1:])
    return pl.pallas_call(
        body, name=name, out_shape=out_shape, in_specs=in_specs, out_specs=[any_spec] * n,
        scratch_shapes=[pltpu.SemaphoreType.DMA((n * (N_DEV - 1),)), pltpu.SemaphoreType.DMA((n * (N_DEV - 1),)),
                        pltpu.SemaphoreType.DMA((n,))],
    )(*args)


def _gather_two_level(arrays, name):
    n = len(arrays)
    pieces = []
    for i, a in enumerate(arrays):
        rows = _Sender.PIECE_ROWS if a.shape[0] % _Sender.PIECE_ROWS == 0 else a.shape[0]
        pieces += [(i, r0, rows) for r0 in range(0, a.shape[0], rows)]

    def body(*refs):
        srcs, dsts = refs[:n], refs[n:2 * n]
        send_sems, recv_sems, local_sems = refs[2 * n:]
        x, y, c = lax.axis_index("x"), lax.axis_index("y"), lax.axis_index("c")
        me, sibling = (x, y, c), (x, y, 1 - c)
        x_nbr, y_nbr, diag = (1 - x, y, c), (x, 1 - y, c), (1 - x, 1 - y, c)

        def slot(px, py, pc):
            return 4 * px + 2 * py + pc

        def copy(u, k, block, to, own=False):
            i, r0, rows = pieces[u]
            there = dsts[i].at[slot(*block)].at[pl.ds(r0, rows)]
            return pltpu.make_async_remote_copy(
                src_ref=srcs[i].at[pl.ds(r0, rows)] if own else there, dst_ref=there,
                send_sem=send_sems.at[u * 7 + k], recv_sem=recv_sems.at[u * 7 + k],
                device_id=to, device_id_type=pl.DeviceIdType.MESH)

        units = range(len(pieces))
        mine = [pltpu.make_async_copy(srcs[i], dsts[i].at[slot(*me)], local_sems.at[i]) for i in range(n)]
        for cp in mine:
            cp.start()
        for u in units:
            copy(u, 1, me, x_nbr, own=True).start()
            copy(u, 2, me, y_nbr, own=True).start()
        for u in units:
            copy(u, 0, me, sibling, own=True).start()

        def relay_then_pass(k_from, frm, to, k_other, other):
            for u in units:
                copy(u, k_from, frm, me).wait_recv()
                copy(u, 3, frm, to).start()
                copy(u, 3 + k_from, frm, sibling).start()
            for u in units:
                copy(u, k_other, other, me).wait_recv()
                copy(u, 3 + k_other, other, sibling).start()

        @pl.when(c == 1)
        def _():
            relay_then_pass(1, x_nbr, y_nbr, 2, y_nbr)

        @pl.when(c == 0)
        def _():
            relay_then_pass(2, y_nbr, x_nbr, 1, x_nbr)

        for u in units:
            copy(u, 3, diag, me).wait_recv()
            copy(u, 6, diag, sibling).start()
        for u in units:
            copy(u, 0, sibling, me).wait_recv()
            for k, chip in ((4, x_nbr), (5, y_nbr), (6, diag)):
                copy(u, k, (chip[0], chip[1], 1 - c), me).wait_recv()
        for u in units:
            for k in range(7):
                copy(u, k, me, me, own=True).wait_send()
        for cp in mine:
            cp.wait()

    any_spec = pl.BlockSpec(memory_space=pl.ANY)
    return pl.pallas_call(
        body, name=name, out_shape=[jax.ShapeDtypeStruct((N_DEV,) + a.shape, a.dtype) for a in arrays],
        in_specs=[any_spec] * n, out_specs=[any_spec] * n,
        scratch_shapes=[pltpu.SemaphoreType.DMA((len(pieces) * 7,)), pltpu.SemaphoreType.DMA((len(pieces) * 7,)),
                        pltpu.SemaphoreType.DMA((n,))],
    )(*arrays)


_HBM = pl.BlockSpec(memory_space=pltpu.HBM)
_SEM = pl.BlockSpec(memory_space=pltpu.SEMAPHORE)
_EFFECT = pltpu.SideEffectType.DATAFLOW_SIDE_EFFECTING


def _split_copies(items, srcs, lands, send_sems, recv_sems):
    x, y, c = lax.axis_index("x"), lax.axis_index("y"), lax.axis_index("c")
    me = 4 * x + 2 * y + c
    copies = []
    for k in range(1, N_DEV):
        px = jnp.bitwise_xor(x, (k >> 2) & 1)
        py = jnp.bitwise_xor(y, (k >> 1) & 1)
        pc = jnp.bitwise_xor(c, k & 1)
        peer = 4 * px + 2 * py + pc
        for i in range(len(items)):
            src = srcs[i] if items[i][1] == "gather" else srcs[i].at[peer]
            copies.append(pltpu.make_async_remote_copy(
                src_ref=src, dst_ref=lands[i].at[me],
                send_sem=send_sems.at[i * (N_DEV - 1) + k - 1], recv_sem=recv_sems.at[i * (N_DEV - 1) + k - 1],
                device_id=(px, py, pc), device_id_type=pl.DeviceIdType.MESH))
    return me, copies


def _exchange_start(items, name, after):
    n = len(items)
    n_sem = n * (N_DEV - 1)
    srcs, lands = [], []
    for a, mode in items:
        blk = a.shape if mode == "gather" else a.shape[1:]
        srcs.append(pltpu.with_memory_space_constraint(a, pltpu.HBM))
        lands.append(pltpu.with_memory_space_constraint(lax.empty((N_DEV,) + tuple(blk), a.dtype), pltpu.HBM))

    def body(*refs):
        src_refs, land_refs = refs[:n], refs[n:2 * n]
        send_sems, recv_sems = refs[2 * n + 1], refs[2 * n + 2]
        local_sems = refs[4 * n + 3]
        me, copies = _split_copies(items, src_refs, land_refs, send_sems, recv_sems)
        for i in range(n):
            own = src_refs[i] if items[i][1] == "gather" else src_refs[i].at[me]
            cp = pltpu.make_async_copy(own, land_refs[i].at[me], local_sems.at[i])
            cp.start()
            cp.wait()
        for cp in copies:
            cp.start()

    out_shape = [pltpu.SemaphoreType.DMA((n_sem,)), pltpu.SemaphoreType.DMA((n_sem,))]
    out_shape += [pltpu.HBM(a.shape, a.dtype) for a in srcs] + [pltpu.HBM(a.shape, a.dtype) for a in lands]
    outs = pl.pallas_call(
        body, name=name, out_shape=out_shape,
        in_specs=[_HBM] * (2 * n) + [pl.BlockSpec(memory_space=pl.ANY)],
        out_specs=[_SEM, _SEM] + [_HBM] * (2 * n),
        input_output_aliases={i: 2 + i for i in range(2 * n)},
        scratch_shapes=[pltpu.SemaphoreType.DMA((n,))],
        compiler_params=pltpu.CompilerParams(has_side_effects=_EFFECT),
    )(*srcs, *lands, after)
    handle = (items, name, outs[0], outs[1], outs[2:2 + n], outs[2 + n:2 + 2 * n])
    return handle, outs[2]


class _Sender:
    PIECE_ROWS = 352

    def __init__(self, items, chunks=None):
        self.items, self.n = items, len(items)
        self.chunks = chunks
        if chunks is None:
            block_rows = [a.shape[0] if mode == "gather" else a.shape[1] for a, mode in items]
            self.chunks = [r // self.PIECE_ROWS if r % self.PIECE_ROWS == 0 else 1 for r in block_rows]
        self.srcs, self.lands = [], []
        for a, mode in items:
            blk = a.shape if mode == "gather" else a.shape[1:]
            self.srcs.append(pltpu.with_memory_space_constraint(a, pltpu.HBM))
            self.lands.append(pltpu.with_memory_space_constraint(lax.empty((N_DEV,) + tuple(blk), a.dtype), pltpu.HBM))

    def issue(self, src_refs, land_refs, send_sems, recv_sems, local_sems, step, n_steps):
        x, y, c = lax.axis_index("x"), lax.axis_index("y"), lax.axis_index("c")
        me = 4 * x + 2 * y + c
        copies = []
        for ch in range(max(self.chunks)):
            for k in range(1, N_DEV):
                px = jnp.bitwise_xor(x, (k >> 2) & 1)
                py = jnp.bitwise_xor(y, (k >> 1) & 1)
                pc = jnp.bitwise_xor(c, k & 1)
                peer = 4 * px + 2 * py + pc
                for i, (_, mode) in enumerate(self.items):
                    if ch >= self.chunks[i]:
                        continue
                    n_rows = land_refs[i].shape[1] // self.chunks[i]
                    rows = pl.ds(ch * n_rows, n_rows)
                    src = src_refs[i].at[rows] if mode == "gather" else src_refs[i].at[peer].at[rows]
                    copies.append(pltpu.make_async_remote_copy(
                        src_ref=src, dst_ref=land_refs[i].at[me].at[rows],
                        send_sem=send_sems.at[i * (N_DEV - 1) + k - 1], recv_sem=recv_sems.at[i * (N_DEV - 1) + k - 1],
                        device_id=(px, py, pc), device_id_type=pl.DeviceIdType.MESH))
        own = [pltpu.make_async_copy(src_refs[i] if mode == "gather" else src_refs[i].at[me], land_refs[i].at[me],
                                     local_sems.at[i]) for i, (_, mode) in enumerate(self.items)]

        @pl.when(step == 0)
        def _():
            for cp in own:
                cp.start()

        for s in range(n_steps):
            group = [cp for j, cp in enumerate(copies) if (j * n_steps) // len(copies) == s]
            if group:
                @pl.when(step == s)
                def _(group=group):
                    for cp in group:
                        cp.start()

        @pl.when(step == n_steps - 1)
        def _():
            for cp in own:
                cp.wait()


def _host_call(body, name, grid, in_specs, args, out_shape, out_specs, scratch_shapes, after=None, sender=None):
    in_specs, args, out_shape, out_specs = list(in_specs), list(args), list(out_shape), list(out_specs)
    scratch_shapes = list(scratch_shapes)
    semantics = ("arbitrary",) * len(grid)
    body = _ordered_behind(body, in_specs, args, after)
    if sender is None:
        res = pl.pallas_call(body, name=name, grid=grid, in_specs=in_specs, out_specs=out_specs, out_shape=out_shape,
                             scratch_shapes=scratch_shapes, compiler_params=_params(semantics))(*args)
        return res, None
    n, n_in, n_out, n_scr = sender.n, len(in_specs), len(out_shape), len(scratch_shapes)
    n_sem = n * (N_DEV - 1)
    n_steps = 1
    for g in grid:
        n_steps *= g
    compute = body

    def body(*refs):
        ins, s_in = refs[:n_in], refs[n_in:n_in + 2 * n]
        o0 = n_in + 2 * n
        outs, s_out = refs[o0:o0 + n_out], refs[o0 + n_out:o0 + n_out + 2 + 2 * n]
        scr = refs[o0 + n_out + 2 + 2 * n:]
        compute(*ins, *outs, *scr[:n_scr])
        step = pl.program_id(0)
        for d in range(1, len(grid)):
            step = step * grid[d] + pl.program_id(d)
        sender.issue(s_in[:n], s_in[n:], s_out[0], s_out[1], scr[n_scr], step, n_steps)

    res = pl.pallas_call(
        body, name=name, grid=grid,
        in_specs=in_specs + [_HBM] * (2 * n), out_specs=out_specs + [_SEM, _SEM] + [_HBM] * (2 * n),
        out_shape=out_shape + [pltpu.SemaphoreType.DMA((n_sem,)), pltpu.SemaphoreType.DMA((n_sem,))]
        + [pltpu.HBM(a.shape, a.dtype) for a in sender.srcs] + [pltpu.HBM(a.shape, a.dtype) for a in sender.lands],
        input_output_aliases={n_in + j: n_out + 2 + j for j in range(2 * n)},
        scratch_shapes=scratch_shapes + [pltpu.SemaphoreType.DMA((n,))],
        compiler_params=pltpu.CompilerParams(dimension_semantics=semantics, vmem_limit_bytes=VMEM_LIMIT,
                                             has_side_effects=_EFFECT),
    )(*args, *sender.srcs, *sender.lands)
    handle = (sender.items, name, res[n_out], res[n_out + 1], res[n_out + 2:n_out + 2 + n],
              res[n_out + 2 + n:n_out + 2 + 2 * n])
    return res[:n_out], handle


def _exchange_wait(handle, after):
    items, name, send_sems, recv_sems, srcs, lands = handle
    n = len(items)

    def body(*refs):
        src_refs, land_refs = refs[:n], refs[n:2 * n]
        send_ref, recv_ref = refs[2 * n], refs[2 * n + 1]
        _, copies = _split_copies(items, src_refs, land_refs, send_ref, recv_ref)
        for cp in copies:
            cp.wait_send()
            cp.wait_recv()

    outs = pl.pallas_call(
        body, name=name + "_wait",
        out_shape=[pltpu.HBM(a.shape, a.dtype) for a in srcs] + [pltpu.HBM(a.shape, a.dtype) for a in lands],
        in_specs=[_HBM] * (2 * n) + [_SEM, _SEM, pl.BlockSpec(memory_space=pl.ANY)], out_specs=[_HBM] * (2 * n),
        input_output_aliases={i: i for i in range(2 * n)},
        compiler_params=pltpu.CompilerParams(has_side_effects=_EFFECT),
    )(*srcs, *lands, send_sems, recv_sems, after)
    return outs[n:]


def _mod_fwd(cvec, w_mod_l, b_mod_l):
    rows, cols = cvec.shape[0], w_mod_l.shape[1]

    def body(c_ref, w_ref, b_ref, o_ref, s_ref):
        cv = c_ref[...]
        s = cv * _sigmoid(cv)
        s_ref[...] = s
        o_ref[...] = _dot(s, w_ref[...]) + b_ref[...]

    return pl.pallas_call(
        body, name="mod_fwd",
        out_shape=(jax.ShapeDtypeStruct((rows, cols), F32), jax.ShapeDtypeStruct((rows, D), F32)),
        in_specs=[_full((rows, D)), _full((D, cols)), _full((1, cols))],
        out_specs=(_full((rows, cols)), _full((rows, D))), grid=(1,),
        compiler_params=_params(("arbitrary",)),
    )(cvec, w_mod_l, b_mod_l)


def _mod_bwd(svec, cvec, dmod_l, w_mod_l):
    rows, cols = dmod_l.shape

    def body(s_ref, c_ref, d_ref, w_ref, gw_ref, gc_ref):
        gw_ref[...] = _dot(s_ref[...], d_ref[...], "tn")
        cv = c_ref[...]
        sg = _sigmoid(cv)
        gc_ref[...] = _dot(d_ref[...], w_ref[...], "nt") * (sg * (1.0 + cv * (1.0 - sg)))

    return pl.pallas_call(
        body, name="mod_bwd",
        out_shape=(jax.ShapeDtypeStruct((D, cols), F32), jax.ShapeDtypeStruct((rows, D), F32)),
        in_specs=[_full((rows, D)), _full((rows, D)), _full((rows, cols)), _full((D, cols))],
        out_specs=(_full((D, cols)), _full((rows, D))), grid=(1,),
        compiler_params=_params(("arbitrary",)),
    )(svec, cvec, dmod_l, w_mod_l)


def _inproj(xt, modv, g, w_inT, n_cols, rows_per_example, name, after=None, sender=None):
    rows = xt.shape[0]
    tm = min(PROJ_TILE, rows_per_example)
    per_b = rows_per_example // tm
    shared_mod = modv.shape[0] == 1

    def body(x_ref, mod_ref, g_ref, w_ref, p_ref, h_ref):
        x = x_ref[...]
        r = lax.rsqrt(jnp.mean(x * x, axis=-1, keepdims=True) + EPS)
        h = (x * r * g_ref[...]) * (1.0 + mod_ref[0, 1:2, :]) + mod_ref[0, 0:1, :]
        hb = h.astype(MXU_DTYPE)
        h_ref[...] = hb
        for j in range(n_cols // KW):
            p_ref[:, j * KW:(j + 1) * KW] = _dot(hb, w_ref[j * KW:(j + 1) * KW, :], "nt").astype(p_ref.dtype)

    mod_idx = (lambda i: (0, 0, 0)) if shared_mod else (lambda i: (i // per_b, 0, 0))
    in_specs = [pl.BlockSpec((tm, D), lambda i: (i, 0)), pl.BlockSpec((1, N_MOD, D), mod_idx), _full((1, D)),
                pl.BlockSpec((n_cols, D), lambda i: (0, 0), pipeline_mode=pl.Buffered(1))]
    (p, h), handle = _host_call(
        body, name, (rows // tm,), in_specs, [xt, modv, g, w_inT],
        [jax.ShapeDtypeStruct((rows, n_cols), MXU_DTYPE), jax.ShapeDtypeStruct((rows, D), MXU_DTYPE)],
        [pl.BlockSpec((tm, n_cols), lambda i: (i, 0)), pl.BlockSpec((tm, D), lambda i: (i, 0))], [],
        after=after, sender=sender)
    return p, h, handle


def _tri(reverse, n):
    row = lax.broadcasted_iota(jnp.int32, (n, n), 0)
    col = lax.broadcasted_iota(jnp.int32, (n, n), 1)
    same = (row // CHUNK) == (col // CHUNK)
    return same & ((col >= row) if reverse else (col <= row))


def _per_chunk_rows(x, reverse):
    n = x.shape[0]
    rows = [x[j * CHUNK:j * CHUNK + 1] if reverse else x[(j + 1) * CHUNK - 1:(j + 1) * CHUNK] for j in range(n // CHUNK)]
    return jnp.concatenate([jnp.broadcast_to(r, (CHUNK, x.shape[1])) for r in rows], axis=0), rows


def _lower_bound(gam_ref, direction):
    return _sigmoid(gam_ref[direction:direction + 1, :] - gam_ref[2 + direction:3 + direction, :])


def _gate_prep(z, lb, tri, reverse):
    sg = _sigmoid(z)
    f = lb + (1.0 - lb) * sg
    g = jnp.log(f)
    b = _mask_dot(tri, g)
    bl, bl_rows = _per_chunk_rows(b, reverse)
    mid = 0.5 * bl
    return sg, g, 1.0 - f, b, jnp.exp(mid), [jnp.exp(0.5 * r) for r in bl_rows], jnp.exp(mid - b), mid


def _hgrn_fwd(p, gam, s0, rows_per_example, with_out, name, sender=None):
    rows = p.shape[0]
    nb_ex = rows // rows_per_example
    rb = min(TOKEN_TILE, rows_per_example)
    cpb = rb // CHUNK
    nb = rows_per_example // rb
    n_chunks = rows // CHUNK
    has_s0 = s0 is not None

    def body(*refs):
        it = iter(refs)
        gam_ref = next(it)
        zf_ref, vf_ref = next(it), next(it)
        qf_ref = next(it) if with_out else None
        zb_ref, vb_ref = next(it), next(it)
        qb_ref = next(it) if with_out else None
        s0_ref = next(it) if has_s0 else None
        if with_out:
            of_ref, ob_ref = next(it), next(it)
        stash_f, stash_b, fin_ref = next(it), next(it), next(it)
        st_ref = next(it)
        i = pl.program_id(1)

        @pl.when(i == 0)
        def _():
            if has_s0:
                st_ref[...] = s0_ref[:, 0]
            else:
                st_ref[...] = jnp.zeros_like(st_ref)

        for direction, (z_ref, v_ref, q_ref, stash) in enumerate(
                ((zf_ref, vf_ref, qf_ref, stash_f), (zb_ref, vb_ref, qb_ref, stash_b))):
            reverse = direction == 1
            tri = _tri(reverse, rb)
            lb = _lower_bound(gam_ref, direction)
            z = z_ref[...].astype(F32)
            v = v_ref[...].astype(F32)
            _, _, k, b, em, em_rows, e2, mid = _gate_prep(z, lb, tri, reverse)
            kd = (k * (e2 * em)).astype(MXU_DTYPE)
            vb = v.astype(MXU_DTYPE)
            if with_out:
                q = q_ref[...].astype(F32)
                qi = q * jnp.exp(b - mid)
                qe = (qi * em).astype(MXU_DTYPE)
                qi = qi.astype(MXU_DTYPE)
                ki = (k * e2).astype(MXU_DTYPE)
                intra = []
                for h in range(HEADS):
                    hs = slice(h * DK, (h + 1) * DK)
                    sc = jnp.where(tri, _dot(qi[:, hs], ki[:, hs], "nt"), 0.0)
                    intra.append(_dot(sc, vb[:, hs]))
            for j in (range(cpb - 1, -1, -1) if reverse else range(cpb)):
                rs = slice(j * CHUNK, (j + 1) * CHUNK)
                a = em_rows[j] * em_rows[j]
                for h in range(HEADS):
                    hs = slice(h * DK, (h + 1) * DK)
                    st = st_ref[direction, h]
                    stash[j, h] = st.astype(stash.dtype)
                    if with_out:
                        (ob_ref if reverse else of_ref)[rs, hs] = intra[h][rs] + _dot(qe[rs, hs], st, "nt")
                    st_ref[direction, h] = st * a[:, hs] + _dot(vb[rs, hs], kd[rs, hs], "tn")

        @pl.when(i == nb - 1)
        def _():
            fin_ref[:, 0] = st_ref[...]

    up = lambda b, i: b * nb + i
    down = lambda b, i: b * nb + nb - 1 - i
    col = lambda rowf, c: pl.BlockSpec((rb, KW), lambda b, i: (rowf(b, i), c))
    in_specs = [_full((4, KW)), col(up, 0), col(up, 2)] + ([col(up, 3)] if with_out else [])
    in_specs += [col(down, 1), col(down, 2)] + ([col(down, 3)] if with_out else [])
    args = [gam, p, p] + ([p] if with_out else []) + [p, p] + ([p] if with_out else [])
    if has_s0:
        in_specs.append(pl.BlockSpec((2, 1, HEADS, DK, DK), lambda b, i: (0, b, 0, 0, 0)))
        args.append(s0)
    out_shape, out_specs = [], []
    if with_out:
        out_shape += [jax.ShapeDtypeStruct((rows, KW), F32)] * 2
        out_specs += [pl.BlockSpec((rb, KW), lambda b, i: (up(b, i), 0)),
                      pl.BlockSpec((rb, KW), lambda b, i: (down(b, i), 0))]
    out_shape += [jax.ShapeDtypeStruct((n_chunks, HEADS, DK, DK), MXU_DTYPE)] * 2
    out_specs += [pl.BlockSpec((cpb, HEADS, DK, DK), lambda b, i: (up(b, i), 0, 0, 0)),
                  pl.BlockSpec((cpb, HEADS, DK, DK), lambda b, i: (down(b, i), 0, 0, 0))]
    out_shape.append(jax.ShapeDtypeStruct((2, nb_ex, HEADS, DK, DK), F32))
    out_specs.append(pl.BlockSpec((2, 1, HEADS, DK, DK), lambda b, i: (0, b, 0, 0, 0)))
    res, handle = _host_call(body, name, (nb_ex, nb), in_specs, args, out_shape, out_specs,
                             [pltpu.VMEM((2, HEADS, DK, DK), F32)], sender=sender)
    return (*res, handle)


def _hgrn_bwd(p, gam, do, stash_f, stash_b, ds_end, rows_per_example, with_out, name, after=None, sender=None):
    rows = p.shape[0]
    nb_ex = rows // rows_per_example
    rb = min(TOKEN_TILE, rows_per_example)
    cpb = rb // CHUNK
    nb = rows_per_example // rb
    has_end = ds_end is not None

    def body(*refs):
        it = iter(refs)
        gam_ref = next(it)
        ins = []
        for _ in range(2):
            z_ref, v_ref = next(it), next(it)
            q_ref = next(it) if with_out else None
            do_ref = next(it) if with_out else None
            ins.append((z_ref, v_ref, q_ref, do_ref, next(it)))
        end_ref = next(it) if has_end else None
        outs = [next(it), next(it)]
        dlb_ref, ds0_ref = next(it), next(it)
        dst_ref = next(it)
        b_id, i = pl.program_id(0), pl.program_id(1)

        @pl.when(i == 0)
        def _():
            if has_end:
                dst_ref[...] = end_ref[:, 0]
            else:
                dst_ref[...] = jnp.zeros_like(dst_ref)

        @pl.when((i == 0) & (b_id == 0))
        def _():
            dlb_ref[...] = jnp.zeros_like(dlb_ref)

        for direction in range(2):
            z_ref, v_ref, q_ref, do_ref, stash = ins[direction]
            dgrp_ref = outs[direction]
            reverse = direction == 1
            tri = _tri(reverse, rb)
            tri_t = _tri(not reverse, rb)
            lb = _lower_bound(gam_ref, direction)
            heads = [slice(h * DK, (h + 1) * DK) for h in range(HEADS)]
            chunks = [slice(j * CHUNK, (j + 1) * CHUNK) for j in range(cpb)]
            grid_cat = lambda parts: jnp.concatenate([jnp.concatenate(row, axis=1) for row in parts], axis=0)
            cat = lambda parts: jnp.concatenate(parts, axis=1)
            z = z_ref[...].astype(F32)
            sg, g, k, b, em, em_rows, e2, mid = _gate_prep(z, lb, tri, reverse)
            e3 = e2 * em
            kd = k * e3
            kd_b = kd.astype(MXU_DTYPE)
            vb = v_ref[...].astype(MXU_DTYPE)
            if with_out:
                q = q_ref[...].astype(F32)
                dout = do_ref[...].astype(MXU_DTYPE)
                e1 = jnp.exp(b - mid)
                e4 = e1 * em
                qi, ki, qe = q * e1, k * e2, q * e4
                qi_b, ki_b, qe_b = qi.astype(MXU_DTYPE), ki.astype(MXU_DTYPE), qe.astype(MXU_DTYPE)
                dqi_p, dki_p, dv_p = [], [], []
                for hs in heads:
                    sc = jnp.where(tri, _dot(qi_b[:, hs], ki_b[:, hs], "nt"), 0.0)
                    dsc = jnp.where(tri, _dot(dout[:, hs], vb[:, hs], "nt"), 0.0)
                    dqi_p.append(_dot(dsc, ki_b[:, hs]))
                    dki_p.append(_dot(dsc, qi_b[:, hs], "tn"))
                    dv_p.append(_dot(sc, dout[:, hs], "tn"))
                dqi, dki, dv = cat(dqi_p), cat(dki_p), cat(dv_p)
                dqe = grid_cat([[_dot(dout[rs, hs], stash[j, h]) for h, hs in enumerate(heads)]
                                for j, rs in enumerate(chunks)])
                grow = [[_dot(dout[rs, hs], qe_b[rs, hs], "tn") for hs in heads] for rs in chunks]
            dkd_p = [[None] * HEADS for _ in range(cpb)]
            dvs_p = [[None] * HEADS for _ in range(cpb)]
            da_p = [[None] * HEADS for _ in range(cpb)]
            for j in (range(cpb) if reverse else range(cpb - 1, -1, -1)):
                rs = chunks[j]
                a = em_rows[j] * em_rows[j]
                for h, hs in enumerate(heads):
                    dst = dst_ref[direction, h]
                    dkd_p[j][h] = _dot(vb[rs, hs], dst)
                    dvs_p[j][h] = _dot(kd_b[rs, hs], dst, "nt")
                    da_p[j][h] = jnp.broadcast_to(
                        jnp.sum(dst * stash[j, h].astype(F32), axis=0, keepdims=True), (CHUNK, DK))
                    new_dst = dst * a[:, hs]
                    dst_ref[direction, h] = new_dst + grow[j][h] if with_out else new_dst
            dkd, dvs, da = grid_cat(dkd_p), grid_cat(dvs_p), grid_cat(da_p)
            t_kd = dkd * kd
            dk = dkd * e3
            db = -t_kd
            tot = t_kd
            if with_out:
                dgrp_ref[:, KW:2 * KW] = (dvs + dv).astype(dgrp_ref.dtype)
                dgrp_ref[:, 2 * KW:] = (dqi * e1 + dqe * e4).astype(dgrp_ref.dtype)
                dk = dk + dki * e2
                t_qi, t_ki, t_qe = dqi * qi, dki * ki, dqe * qe
                db = db + t_qi - t_ki + t_qe
                tot = tot + 0.5 * (t_ki - t_qi)
            else:
                dgrp_ref[:, KW:2 * KW] = dvs.astype(dgrp_ref.dtype)
            dbl = jnp.concatenate([jnp.broadcast_to(jnp.sum(tot[rs], axis=0, keepdims=True), (CHUNK, KW))
                                   for rs in chunks], axis=0) + da * (em * em)
            dg = _mask_dot(tri_t, db) + dbl
            df = dg * jnp.exp(-g) - dk
            dgrp_ref[:, 0:KW] = (df * (1.0 - lb) * sg * (1.0 - sg)).astype(dgrp_ref.dtype)
            dlb_ref[direction:direction + 1, :] += jnp.sum(df * (1.0 - sg), axis=0, keepdims=True)

        @pl.when(i == nb - 1)
        def _():
            ds0_ref[:, 0] = dst_ref[...]

    rows_of = (lambda b, i: b * nb + nb - 1 - i, lambda b, i: b * nb + i)
    in_specs, args = [_full((4, KW))], [gam]
    for direction in range(2):
        rf = rows_of[direction]
        col = lambda c, rf=rf: pl.BlockSpec((rb, KW), lambda b, i: (rf(b, i), c))
        in_specs += [col(direction), col(2)]
        args += [p, p]
        if with_out:
            in_specs += [col(3), col(0)]
            args += [p, do]
        in_specs.append(pl.BlockSpec((cpb, HEADS, DK, DK), lambda b, i, rf=rf: (rf(b, i), 0, 0, 0)))
        args.append((stash_f, stash_b)[direction])
    if has_end:
        in_specs.append(pl.BlockSpec((2, 1, HEADS, DK, DK), lambda b, i: (0, b, 0, 0, 0)))
        args.append(ds_end)
    out_shape, out_specs = [], []
    for direction in range(2):
        rf = rows_of[direction]
        width = (3 if with_out else 2) * KW
        out_shape.append(jax.ShapeDtypeStruct((rows, width), MXU_DTYPE))
        out_specs.append(pl.BlockSpec((rb, width), lambda b, i, rf=rf: (rf(b, i), 0)))
    out_shape += [jax.ShapeDtypeStruct((2, KW), F32), jax.ShapeDtypeStruct((2, nb_ex, HEADS, DK, DK), F32)]
    out_specs += [_full((2, KW)), pl.BlockSpec((2, 1, HEADS, DK, DK), lambda b, i: (0, b, 0, 0, 0))]
    res, handle = _host_call(body, name, (nb_ex, nb), in_specs, args, out_shape, out_specs,
                             [pltpu.VMEM((2, HEADS, DK, DK), F32)], after=after, sender=sender)
    return (*res, handle)


def _tail_forward(osum, og, u, v, ga, gb, gna, ln_g, ln_b, ws_ref, bs_ref, wpaT_ref, wpbT_ref):
    tm = osum.shape[0]
    gna4 = jnp.concatenate([gna] * HEADS, axis=1)
    r_parts = []
    for h in range(HEADS):
        oh = osum[:, h * DK:(h + 1) * DK]
        r_parts.append(jnp.broadcast_to(lax.rsqrt(jnp.mean(oh * oh, axis=-1, keepdims=True) + EPS), (tm, DK)))
    r = jnp.concatenate(r_parts, axis=1)
    on = osum * r
    sg_og = _sigmoid(og)
    silu_og = og * sg_og
    oan = on * gna4
    oa = oan * silu_og
    ug, tu = _gelu(u)
    vg, tv = _gelu(v)
    mu = jnp.mean(vg, axis=-1, keepdims=True)
    vc = vg - mu
    rstd = lax.rsqrt(jnp.mean(vc * vc, axis=-1, keepdims=True) + EPS)
    vhat = vc * rstd
    vln = vhat * ln_g + ln_b
    blocks = []
    for n in range(tm // SGU_BLOCK):
        rs = slice(n * SGU_BLOCK, (n + 1) * SGU_BLOCK)
        blocks.append(jnp.concatenate(
            [_dot(ws_ref[g], vln[rs, g * DK:(g + 1) * DK]) + bs_ref[g] for g in range(GROUPS)], axis=1))
    mixed = jnp.concatenate(blocks, axis=0) if len(blocks) > 1 else blocks[0]
    obm = ug * mixed
    pa = _dot(oa, wpaT_ref[...], "nt")
    pb = _dot(obm, wpbT_ref[...], "nt")
    sga, sgb = _sigmoid(ga), _sigmoid(gb)
    merged = sga * pa + sgb * pb
    return dict(r=r, on=on, sg_og=sg_og, silu_og=silu_og, oan=oan, oa=oa, ug=ug, tu=tu, tv=tv, rstd=rstd, vhat=vhat,
                vln=vln, mixed=mixed, obm=obm, pa=pa, pb=pb, sga=sga, sgb=sgb, merged=merged, gna4=gna4)


def _tail_in_specs(tm):
    tile = lambda c: pl.BlockSpec((tm, KW), lambda i: (i, c))
    return [tile(c) for c in range(4, 11)]


def _tail_weight_specs():
    return [_full((1, DK)), _full((1, KW)), _full((1, KW)), _full((GROUPS, SGU_BLOCK, SGU_BLOCK)),
            _full((GROUPS, SGU_BLOCK, 1)), _full((D, KW), single=True), _full((D, KW), single=True),
            _full((D, D), single=True)]


def _read_tail_inputs(of_ref, ob_ref, pcols):
    osum = of_ref[...] + ob_ref[...]
    og, u, v = (pcols[j][...].astype(F32) for j in range(3))
    ga = jnp.concatenate([pcols[3][...], pcols[4][...]], axis=1).astype(F32)
    gb = jnp.concatenate([pcols[5][...], pcols[6][...]], axis=1).astype(F32)
    return osum, og, u, v, ga, gb


def _tail_fwd(p, o_up, o_down, xt, modv, gna, ln_g, ln_b, w_s, b_s, w_paT, w_pbT, w_o, rows_per_example):
    rows = xt.shape[0]
    tm = min(TAIL_TILE, rows_per_example)
    per_b = rows_per_example // tm

    def body(of_ref, ob_ref, *rest):
        pcols = rest[:7]
        (x_ref, mod_ref, gna_ref, lng_ref, lnb_ref, ws_ref, bs_ref, wpaT_ref, wpbT_ref, wo_ref,
         x1_ref, mix_ref, merged_ref, oa_ref, obm_ref) = rest[7:]
        t = _tail_forward(*_read_tail_inputs(of_ref, ob_ref, pcols), gna_ref[...], lng_ref[...], lnb_ref[...],
                          ws_ref, bs_ref, wpaT_ref, wpbT_ref)
        mix = _dot(t["merged"], wo_ref[...])
        x1_ref[...] = x_ref[...] + mod_ref[0, 2:3, :] * mix
        mix_ref[...] = mix.astype(mix_ref.dtype)
        merged_ref[...] = t["merged"].astype(merged_ref.dtype)
        oa_ref[...] = t["oa"].astype(oa_ref.dtype)
        obm_ref[...] = t["obm"].astype(obm_ref.dtype)

    row = lambda w: pl.BlockSpec((tm, w), lambda i: (i, 0))
    in_specs = [row(KW), row(KW)] + _tail_in_specs(tm) + [row(D), pl.BlockSpec((1, N_MOD, D), lambda i: (i // per_b, 0, 0))]
    in_specs += _tail_weight_specs()
    return pl.pallas_call(
        body, name="tail_fwd", grid=(rows // tm,),
        out_shape=(jax.ShapeDtypeStruct((rows, D), F32), jax.ShapeDtypeStruct((rows, D), MXU_DTYPE),
                   jax.ShapeDtypeStruct((rows, D), MXU_DTYPE), jax.ShapeDtypeStruct((rows, KW), MXU_DTYPE),
                   jax.ShapeDtypeStruct((rows, KW), MXU_DTYPE)),
        in_specs=in_specs, out_specs=(row(D), row(D), row(D), row(KW), row(KW)),
        compiler_params=_params(("arbitrary",)),
    )(o_up, o_down, *([p] * 7), xt, modv, gna, ln_g, ln_b, w_s, b_s, w_paT, w_pbT, w_o)


def _tail_bwd(p, o_up, o_down, dx1, mix, modv, gna, ln_g, ln_b, w_s, b_s, w_paT, w_pbT, w_o, rows_per_example,
              after=None, sender=None):
    rows = dx1.shape[0]
    nb_ex = rows // rows_per_example
    tm = min(TAIL_TILE, rows_per_example)
    per_b = rows_per_example // tm

    def body(of_ref, ob_ref, *rest):
        pcols = rest[:7]
        (dx1_ref, mix_ref, mod_ref, gna_ref, lng_ref, lnb_ref, ws_ref, bs_ref, wpaT_ref, wpbT_ref, wo_ref,
         dpt_ref, do_ref, dmix_ref, dpa_ref, dpb_ref, dmod_ref, small_ref, dws_ref, dbs_ref) = rest[7:]
        i = pl.program_id(0)

        @pl.when(i == 0)
        def _():
            small_ref[...] = jnp.zeros_like(small_ref)
            dws_ref[...] = jnp.zeros_like(dws_ref)
            dbs_ref[...] = jnp.zeros_like(dbs_ref)

        @pl.when(i % per_b == 0)
        def _():
            dmod_ref[...] = jnp.zeros_like(dmod_ref)

        osum, og, u, v, ga, gb = _read_tail_inputs(of_ref, ob_ref, pcols)
        ln_g = lng_ref[...]
        t = _tail_forward(osum, og, u, v, ga, gb, gna_ref[...], ln_g, lnb_ref[...], ws_ref, bs_ref, wpaT_ref, wpbT_ref)
        dx1v = dx1_ref[...]
        dmod_ref[0, 2:3, :] += jnp.sum(dx1v * mix_ref[...].astype(F32), axis=0, keepdims=True)
        dmix = dx1v * mod_ref[0, 2:3, :]
        dmix_ref[...] = dmix.astype(dmix_ref.dtype)
        dmerged = _dot(dmix, wo_ref[...], "nt")
        sga, sgb = t["sga"], t["sgb"]
        dpa = dmerged * sga
        dpb = dmerged * sgb
        dpa_ref[...] = dpa.astype(dpa_ref.dtype)
        dpb_ref[...] = dpb.astype(dpb_ref.dtype)
        dga = dmerged * t["pa"] * sga * (1.0 - sga)
        dgb = dmerged * t["pb"] * sgb * (1.0 - sgb)
        doa = _dot(dpa, wpaT_ref[...])
        dobm = _dot(dpb, wpbT_ref[...])
        dug = dobm * t["mixed"]
        dmixed = dobm * t["ug"]
        du = dug * _gelu_grad(u, t["tu"])
        dvln_blocks = []
        for n in range(tm // SGU_BLOCK):
            rs = slice(n * SGU_BLOCK, (n + 1) * SGU_BLOCK)
            parts = []
            for g in range(GROUPS):
                gs = slice(g * DK, (g + 1) * DK)
                dm = dmixed[rs, gs]
                parts.append(_dot(ws_ref[g], dm, "tn"))
                dws_ref[g] += _dot(dm, t["vln"][rs, gs], "nt")
                dbs_ref[g] += jnp.sum(dm, axis=1, keepdims=True)
            dvln_blocks.append(jnp.concatenate(parts, axis=1))
        dvln = jnp.concatenate(dvln_blocks, axis=0) if len(dvln_blocks) > 1 else dvln_blocks[0]
        vhat = t["vhat"]
        small_ref[1:2, 0:KW] += jnp.sum(dvln * vhat, axis=0, keepdims=True)
        small_ref[2:3, 0:KW] += jnp.sum(dvln, axis=0, keepdims=True)
        dvhat = dvln * ln_g
        dvg = t["rstd"] * (dvhat - jnp.mean(dvhat, axis=-1, keepdims=True)
                           - vhat * jnp.mean(dvhat * vhat, axis=-1, keepdims=True))
        dv = dvg * _gelu_grad(v, t["tv"])
        sg_og = t["sg_og"]
        doan = doa * t["silu_og"]
        dog = doa * t["oan"] * (sg_og * (1.0 + og * (1.0 - sg_og)))
        prod = doan * t["on"]
        dgna = jnp.zeros((1, DK), F32)
        for h in range(HEADS):
            dgna = dgna + jnp.sum(prod[:, h * DK:(h + 1) * DK], axis=0, keepdims=True)
        small_ref[0:1, 0:DK] += dgna
        don = doan * t["gna4"]
        dot_parts = []
        for h in range(HEADS):
            hs = slice(h * DK, (h + 1) * DK)
            m = jnp.mean(don[:, hs] * t["on"][:, hs], axis=-1, keepdims=True)
            dot_parts.append(t["r"][:, hs] * (don[:, hs] - t["on"][:, hs] * m))
        do_ref[...] = jnp.concatenate(dot_parts, axis=1).astype(do_ref.dtype)
        for j, val in enumerate((dog, du, dv)):
            dpt_ref[:, j * KW:(j + 1) * KW] = val.astype(dpt_ref.dtype)
        dpt_ref[:, 3 * KW:3 * KW + D] = dga.astype(dpt_ref.dtype)
        dpt_ref[:, 3 * KW + D:] = dgb.astype(dpt_ref.dtype)

    row = lambda w: pl.BlockSpec((tm, w), lambda i: (i, 0))
    in_specs = [row(KW), row(KW)] + _tail_in_specs(tm) + [row(D), row(D), pl.BlockSpec((1, N_MOD, D), lambda i: (i // per_b, 0, 0))]
    in_specs += _tail_weight_specs()
    args = [o_up, o_down, *([p] * 7), dx1, mix, modv, gna, ln_g, ln_b, w_s, b_s, w_paT, w_pbT, w_o]
    cd = MXU_DTYPE
    res, handle = _host_call(
        body, "tail_bwd", (rows // tm,), in_specs, args,
        [jax.ShapeDtypeStruct((rows, TAIL_COLS), cd), jax.ShapeDtypeStruct((rows, KW), cd),
         jax.ShapeDtypeStruct((rows, D), cd), jax.ShapeDtypeStruct((rows, D), cd),
         jax.ShapeDtypeStruct((rows, D), cd), jax.ShapeDtypeStruct((nb_ex, 8, D), F32),
         jax.ShapeDtypeStruct((8, D), F32), jax.ShapeDtypeStruct((GROUPS, SGU_BLOCK, SGU_BLOCK), F32),
         jax.ShapeDtypeStruct((GROUPS, SGU_BLOCK, 1), F32)],
        [row(TAIL_COLS), row(KW), row(D), row(D), row(D),
         pl.BlockSpec((1, 8, D), lambda i: (i // per_b, 0, 0)), _full((8, D)),
         _full((GROUPS, SGU_BLOCK, SGU_BLOCK)), _full((GROUPS, SGU_BLOCK, 1))], [],
        after=after, sender=sender)
    return (*res, handle)


def _ffn(x1, target, modv, g_ffn, g_final, w_upT, w_down, rows_per_example):
    rows = x1.shape[0]
    nb_ex = rows // rows_per_example
    tm = min(TOKEN_TILE, rows_per_example)
    per_b = rows_per_example // tm
    n_ff = D_FF // FF_CHUNK

    def body(x1_ref, tgt_ref, mod_ref, gffn_ref, gfin_ref, wup_ref, wdn_ref,
             dx1_ref, h2_ref, dffn_ref, act_ref, dup_ref, dmod_ref, small_ref, up_scr):
        i = pl.program_id(0)

        @pl.when(i == 0)
        def _():
            small_ref[...] = jnp.zeros_like(small_ref)

        @pl.when(i % per_b == 0)
        def _():
            dmod_ref[...] = jnp.zeros_like(dmod_ref)

        x1v = x1_ref[...]
        g2 = gffn_ref[...]
        m3, m4, m5 = mod_ref[0, 3:4, :], mod_ref[0, 4:5, :], mod_ref[0, 5:6, :]
        r2 = lax.rsqrt(jnp.mean(x1v * x1v, axis=-1, keepdims=True) + EPS)
        xn2 = x1v * r2
        h2 = (xn2 * g2) * (1.0 + m4) + m3
        h2b = h2.astype(MXU_DTYPE)
        h2_ref[...] = h2b
        def up_pair(j):
            lo = j * FF_CHUNK
            return (_dot(h2b, wup_ref[lo:lo + FF_CHUNK, :], "nt"),
                    _dot(h2b, wup_ref[D_FF + lo:D_FF + lo + FF_CHUNK, :], "nt"))

        group_end = {min(e, n_ff): s for s, e in ((0, 4), (4, 8), (8, 12))}
        cur, ffn = up_pair(0), None
        for j in range(n_ff):
            nxt = up_pair(j + 1) if j + 1 < n_ff else None
            cs = slice(j * FF_CHUNK, (j + 1) * FF_CHUNK)
            a, bgate = cur
            up_scr[:, cs] = a
            up_scr[:, D_FF + j * FF_CHUNK:D_FF + (j + 1) * FF_CHUNK] = bgate
            act_ref[:, cs] = (a * _sigmoid(a) * bgate).astype(MXU_DTYPE)
            cur = nxt
            if j + 1 in group_end:
                gs = slice(group_end[j + 1] * FF_CHUNK, (j + 1) * FF_CHUNK)
                part = _dot(act_ref[:, gs], wdn_ref[gs, :])
                ffn = part if ffn is None else ffn + part
        x2 = x1v + m5 * ffn
        r3 = lax.rsqrt(jnp.mean(x2 * x2, axis=-1, keepdims=True) + EPS)
        xn3 = x2 * r3
        gf = gfin_ref[...]
        err = xn3 * gf - tgt_ref[...]
        loss = 0.5 * jnp.sum(jnp.mean(err * err, axis=-1, keepdims=True), axis=0, keepdims=True)
        small_ref[2:3, :] += jnp.broadcast_to(loss, (1, D))
        dy = err * (1.0 / D)
        small_ref[1:2, :] += jnp.sum(dy * xn3, axis=0, keepdims=True)
        dxn3 = dy * gf
        dx2 = r3 * (dxn3 - xn3 * jnp.mean(dxn3 * xn3, axis=-1, keepdims=True))
        dmod_ref[0, 5:6, :] += jnp.sum(dx2 * ffn, axis=0, keepdims=True)
        dffn = (dx2 * m5).astype(MXU_DTYPE)
        dffn_ref[...] = dffn
        dact_of = lambda j: _dot(dffn, wdn_ref[j * FF_CHUNK:(j + 1) * FF_CHUNK, :], "nt")
        cur, dh2 = dact_of(0), None
        for j in range(n_ff):
            nxt = dact_of(j + 1) if j + 1 < n_ff else None
            cs = slice(j * FF_CHUNK, (j + 1) * FF_CHUNK)
            a, bgate = up_scr[:, cs], up_scr[:, D_FF + j * FF_CHUNK:D_FF + (j + 1) * FF_CHUNK]
            s = _sigmoid(a)
            dup_ref[:, cs] = (cur * bgate * (s * (1.0 + a * (1.0 - s)))).astype(MXU_DTYPE)
            dup_ref[:, D_FF + j * FF_CHUNK:D_FF + (j + 1) * FF_CHUNK] = (cur * a * s).astype(MXU_DTYPE)
            cur = nxt
            if j + 1 in group_end:
                lo, hi = group_end[j + 1] * FF_CHUNK, (j + 1) * FF_CHUNK
                part = (_dot(dup_ref[:, lo:hi], wup_ref[lo:hi, :])
                        + _dot(dup_ref[:, D_FF + lo:D_FF + hi], wup_ref[D_FF + lo:D_FF + hi, :]))
                dh2 = part if dh2 is None else dh2 + part
        dmod_ref[0, 3:4, :] += jnp.sum(dh2, axis=0, keepdims=True)
        dmod_ref[0, 4:5, :] += jnp.sum(dh2 * xn2 * g2, axis=0, keepdims=True)
        small_ref[0:1, :] += jnp.sum(dh2 * (1.0 + m4) * xn2, axis=0, keepdims=True)
        dxn2 = dh2 * g2 * (1.0 + m4)
        dx1_ref[...] = dx2 + r2 * (dxn2 - xn2 * jnp.mean(dxn2 * xn2, axis=-1, keepdims=True))

    row = lambda w: pl.BlockSpec((tm, w), lambda i: (i, 0))
    cd = MXU_DTYPE
    return pl.pallas_call(
        body, name="ffn_fwd_bwd", grid=(rows // tm,),
        out_shape=(jax.ShapeDtypeStruct((rows, D), F32), jax.ShapeDtypeStruct((rows, D), cd),
                   jax.ShapeDtypeStruct((rows, D), cd), jax.ShapeDtypeStruct((rows, D_FF), cd),
                   jax.ShapeDtypeStruct((rows, 2 * D_FF), cd), jax.ShapeDtypeStruct((nb_ex, 8, D), F32),
                   jax.ShapeDtypeStruct((8, D), F32)),
        in_specs=[row(D), row(D), pl.BlockSpec((1, N_MOD, D), lambda i: (i // per_b, 0, 0)), _full((1, D)), _full((1, D)),
                  _full((2 * D_FF, D), single=True), _full((D_FF, D), single=True)],
        out_specs=(row(D), row(D), row(D), row(D_FF), row(2 * D_FF),
                   pl.BlockSpec((1, 8, D), lambda i: (i // per_b, 0, 0)), _full((8, D))),
        scratch_shapes=[pltpu.VMEM((tm, 2 * D_FF), F32)],
        compiler_params=_params(("arbitrary",)),
    )(x1, target, modv, g_ffn, g_final, w_upT, w_down)


def _scan_columns(up, down, n_groups):
    cols = [up[:, 0:KW].astype(F32), down[:, 0:KW].astype(F32)]
    for j in range(1, n_groups):
        cols.append(up[:, j * KW:(j + 1) * KW].astype(F32) + down[:, j * KW:(j + 1) * KW].astype(F32))
    return cols


def _inproj_bwd(d_up, d_down, dpt, xt, dx1, modv, g, w_inT, rows_per_example, name, sender=None):
    rows = xt.shape[0]
    latent = dx1 is not None
    n_cols = IN_COLS if latent else CTX_COLS
    n_groups = d_up.shape[1] // KW
    tm = min(PROJ_TILE, rows_per_example)
    per_b = rows_per_example // tm
    n_mod_blocks = rows // rows_per_example if latent else 1

    def body(*refs):
        it = iter(refs)
        up_ref, down_ref = next(it), next(it)
        dpt_ref = next(it) if latent else None
        x_ref = next(it)
        dx1_ref = next(it) if latent else None
        mod_ref, g_ref, w_ref = next(it), next(it), next(it)
        gx_ref = next(it) if latent else None
        dp_out = None if latent else next(it)
        dmod_ref, small_ref = next(it), next(it)
        dp_ref = next(it) if latent else dp_out
        i = pl.program_id(0)

        @pl.when(i == 0)
        def _():
            small_ref[...] = jnp.zeros_like(small_ref)

        @pl.when((i % per_b == 0) if latent else (i == 0))
        def _():
            dmod_ref[...] = jnp.zeros_like(dmod_ref)

        for j, val in enumerate(_scan_columns(up_ref[...], down_ref[...], n_groups)):
            dp_ref[:, j * KW:(j + 1) * KW] = val.astype(MXU_DTYPE)
        if latent:
            dh = _dot(dp_ref[...], w_ref[0:4 * KW, :]) + _dot(dpt_ref[...], w_ref[4 * KW:, :])
        else:
            dh = _dot(dp_ref[...], w_ref[...])
        x = x_ref[...]
        gv = g_ref[...]
        m1 = mod_ref[0, 1:2, :]
        r = lax.rsqrt(jnp.mean(x * x, axis=-1, keepdims=True) + EPS)
        xn = x * r
        dmod_ref[0, 0:1, :] += jnp.sum(dh, axis=0, keepdims=True)
        dmod_ref[0, 1:2, :] += jnp.sum(dh * xn * gv, axis=0, keepdims=True)
        small_ref[0:1, :] += jnp.sum(dh * (1.0 + m1) * xn, axis=0, keepdims=True)
        if latent:
            dxn = dh * gv * (1.0 + m1)
            gx_ref[...] = dx1_ref[...] + r * (dxn - xn * jnp.mean(dxn * xn, axis=-1, keepdims=True))

    row = lambda w: pl.BlockSpec((tm, w), lambda i: (i, 0))
    mod_idx = (lambda i: (i // per_b, 0, 0)) if latent else (lambda i: (0, 0, 0))
    in_specs = [row(n_groups * KW)] * 2 + ([row(TAIL_COLS)] if latent else []) + [row(D)] + ([row(D)] if latent else [])
    in_specs += [pl.BlockSpec((1, N_MOD, D), mod_idx), _full((1, D)),
                 pl.BlockSpec((n_cols, D), lambda i: (0, 0), pipeline_mode=pl.Buffered(1))]
    args = [d_up, d_down] + ([dpt] if latent else []) + [xt] + ([dx1] if latent else []) + [modv, g, w_inT]
    first = jax.ShapeDtypeStruct((rows, D), F32) if latent else jax.ShapeDtypeStruct((rows, n_cols), MXU_DTYPE)
    out_shape = [first, jax.ShapeDtypeStruct((n_mod_blocks, 8, D), F32), jax.ShapeDtypeStruct((8, D), F32)]
    out_specs = [row(D) if latent else row(n_cols), pl.BlockSpec((1, 8, D), mod_idx), _full((8, D))]
    scratch = [pltpu.VMEM((tm, 4 * KW), MXU_DTYPE)] if latent else []
    res, handle = _host_call(body, name, (rows // tm,), in_specs, args, out_shape, out_specs, scratch, sender=sender)
    return (*res, handle)


def _grad_matmul(a, b, name, init=None, tn=512, sender=None):
    rows, n = a.shape
    k = b.shape[1]
    tn = min(tn, n)
    has_init = init is not None
    init_blocks = init.shape[0] // tn if has_init else 0

    def body(*refs):
        if has_init:
            a_ref, b_ref, init_ref, o_ref = refs
        else:
            a_ref, b_ref, o_ref = refs
        g = _dot(a_ref[...], b_ref[...], "tn")
        if has_init:
            g = g + jnp.where(pl.program_id(0) < init_blocks, init_ref[...].astype(F32), 0.0)
        o_ref[...] = g.astype(o_ref.dtype)

    in_specs = [pl.BlockSpec((rows, tn), lambda i: (0, i)), _full((rows, k), single=True)]
    args = [a, b]
    if has_init:
        in_specs.append(pl.BlockSpec((tn, k), lambda i: (jnp.minimum(i, init_blocks - 1), 0)))
        args.append(init)
    (out,), handle = _host_call(
        body, name, (n // tn,), in_specs, args, [jax.ShapeDtypeStruct((n, k), PAYLOAD_DTYPE)],
        [pl.BlockSpec((tn, k), lambda i: (i, 0))], [], sender=sender)
    return out, handle


def _grad_in(d_up, d_down, dpt, h, init, sender=None):
    rows = h.shape[0]
    tn = 256
    per_group = KW // tn
    n_scan = 4 * per_group
    init_blocks = init.shape[0] // tn

    def body(up_ref, down_ref, dpt_ref, h_ref, init_ref, o_ref):
        i = pl.program_id(0)
        both = (up_ref[...].astype(F32) + down_ref[...].astype(F32)).astype(MXU_DTYPE)
        a = jnp.where(i < per_group, up_ref[...],
                      jnp.where(i < 2 * per_group, down_ref[...], jnp.where(i < n_scan, both, dpt_ref[...])))
        g = _dot(a, h_ref[...], "tn") + jnp.where(i < init_blocks, init_ref[...].astype(F32), 0.0)
        o_ref[...] = g.astype(o_ref.dtype)

    last = 3 * per_group - 1
    col = lambda f: pl.BlockSpec((rows, tn), lambda i: (0, f(i)))
    in_specs = [col(lambda i: jnp.clip(jnp.where(i < per_group, i, i - per_group), 0, last)),
                col(lambda i: jnp.clip(i - per_group, 0, last)),
                col(lambda i: jnp.clip(i - n_scan, 0, TAIL_COLS // tn - 1)),
                _full((rows, D), single=True),
                pl.BlockSpec((tn, D), lambda i: (jnp.minimum(i, init_blocks - 1), 0))]
    (out,), handle = _host_call(
        body, "gw_in", (IN_COLS // tn,), in_specs, [d_up, d_down, dpt, h, init],
        [jax.ShapeDtypeStruct((IN_COLS, D), PAYLOAD_DTYPE)], [pl.BlockSpec((tn, D), lambda i: (i, 0))], [],
        sender=sender)
    return out, handle


def _row_tile(rows, limit=256):
    if rows <= limit:
        return rows
    for t in range(limit, 7, -8):
        if rows % t == 0:
            return t
    return rows


def _sum8(stack, name):
    _, rows, cols = stack.shape
    tr = _row_tile(rows)

    def body(s_ref, o_ref):
        acc = s_ref[0].astype(F32)
        for j in range(1, N_DEV):
            acc = acc + s_ref[j].astype(F32)
        o_ref[...] = acc

    return pl.pallas_call(
        body, name=name, grid=(rows // tr,), out_shape=jax.ShapeDtypeStruct((rows, cols), F32),
        in_specs=[pl.BlockSpec((N_DEV, tr, cols), lambda i: (0, i, 0))],
        out_specs=pl.BlockSpec((tr, cols), lambda i: (i, 0)),
        compiler_params=_params(("arbitrary",)),
    )(stack)


def _adamw_update(w, gv, m, v):
    nm = ADAM_B1 * m + (1.0 - ADAM_B1) * gv
    nv = ADAM_B2 * v + (1.0 - ADAM_B2) * (gv * gv)
    m_hat = nm / (1.0 - ADAM_B1 ** ADAM_STEP)
    v_hat = nv / (1.0 - ADAM_B2 ** ADAM_STEP)
    return -ADAM_LR * (m_hat / (jnp.sqrt(v_hat) + ADAM_EPS) + ADAM_WD * w), nm, nv


SMALL_PARAMS = (("g_mix", 0, D), ("g_ffn", 1, D), ("g_final", 2, D), ("g_norm_a", 3, DK), ("ln_v_g", 4, KW),
                ("ln_v_b", 5, KW), ("b_s", 6, GROUPS * SGU_BLOCK))


def _small_finish(early, late, dws, gam, nb_ex, params):
    names = [n for n, _, _ in SMALL_PARAMS] + ["b_mod", "w_s"]

    def body(*refs):
        s_ref, l_ref, dws_ref, gam_ref = refs[:4]
        p_refs = refs[4:4 + 3 * len(names)]
        tot_ref, dgam_ref = refs[4 + 3 * len(names):6 + 3 * len(names)]
        o_refs = refs[6 + 3 * len(names):]
        acc = s_ref[0] + l_ref[0]
        gws = dws_ref[0]
        for j in range(1, N_DEV):
            acc = acc + (s_ref[j] + l_ref[j])
            gws = gws + dws_ref[j]
        tot_ref[...] = acc
        bm = acc[8:8 + N_MOD, :]
        for e in range(nb_ex):
            bm = bm + acc[16 + e * N_MOD:16 + (e + 1) * N_MOD, :]
        lb = jnp.concatenate([_lower_bound(gam_ref, 0), _lower_bound(gam_ref, 1)], axis=1)
        dgam = acc[7:8, :] * lb * (1.0 - lb)
        dgam_ref[...] = jnp.concatenate([dgam, -dgam], axis=0)
        grads = [acc[row:row + 1, 0:width] for _, row, width in SMALL_PARAMS] + [bm, gws]
        for k, g in enumerate(grads):
            w_ref, m_ref, v_ref = p_refs[3 * k:3 * k + 3]
            o_refs[4 * k][...] = g
            o_refs[4 * k + 1][...], o_refs[4 * k + 2][...], o_refs[4 * k + 3][...] = _adamw_update(
                w_ref[...], g, m_ref[...], v_ref[...])

    p_args, p_specs, o_shapes, o_specs = [], [], [], []
    for n in names:
        for a in params[n]:
            p_args.append(a)
            p_specs.append(_full(a.shape))
        o_shapes += [jax.ShapeDtypeStruct(params[n][0].shape, F32)] * 4
        o_specs += [_full(params[n][0].shape)] * 4
    res = pl.pallas_call(
        body, name="small_finish", grid=(1,),
        out_shape=[jax.ShapeDtypeStruct((SMALL_ROWS, D), F32), jax.ShapeDtypeStruct((2, D), F32)] + o_shapes,
        in_specs=[_full(early.shape), _full(late.shape), _full(dws.shape), _full((4, KW))] + p_specs,
        out_specs=[_full((SMALL_ROWS, D)), _full((2, D))] + o_specs,
        compiler_params=_params(("arbitrary",)),
    )(early, late, dws, gam, *p_args)
    return res[0], res[1], {n: res[2 + 4 * k:6 + 4 * k] for k, n in enumerate(names)}


def _adamw_sum8(stack, w, m, v, name):
    _, rows, cols = stack.shape
    tr = _row_tile(rows)

    def body(s_ref, w_ref, m_ref, v_ref, g_ref, d_ref, nm_ref, nv_ref):
        gv = s_ref[0].astype(F32)
        for j in range(1, N_DEV):
            gv = gv + s_ref[j].astype(F32)
        g_ref[...] = gv
        d_ref[...], nm_ref[...], nv_ref[...] = _adamw_update(w_ref[...], gv, m_ref[...], v_ref[...])

    blk = pl.BlockSpec((tr, cols), lambda i: (i, 0))
    sd = jax.ShapeDtypeStruct((rows, cols), F32)
    return pl.pallas_call(
        body, name=name, grid=(rows // tr,), out_shape=(sd, sd, sd, sd),
        in_specs=[pl.BlockSpec((N_DEV, tr, cols), lambda i: (0, i, 0)), blk, blk, blk], out_specs=(blk, blk, blk, blk),
        compiler_params=_params(("arbitrary",)),
    )(stack, w, m, v)


def _adamw(w, g, m, v, name):
    shape = w.shape
    cols = shape[-1]
    rows = 1
    for s in shape[:-1]:
        rows *= s
    tr = _row_tile(rows)

    def body(w_ref, g_ref, m_ref, v_ref, d_ref, nm_ref, nv_ref):
        gv = g_ref[...]
        nm = ADAM_B1 * m_ref[...] + (1.0 - ADAM_B1) * gv
        nv = ADAM_B2 * v_ref[...] + (1.0 - ADAM_B2) * (gv * gv)
        m_hat = nm / (1.0 - ADAM_B1 ** ADAM_STEP)
        v_hat = nv / (1.0 - ADAM_B2 ** ADAM_STEP)
        d_ref[...] = -ADAM_LR * (m_hat / (jnp.sqrt(v_hat) + ADAM_EPS) + ADAM_WD * w_ref[...])
        nm_ref[...] = nm
        nv_ref[...] = nv

    blk = pl.BlockSpec((tr, cols), lambda i: (i, 0))
    sd = jax.ShapeDtypeStruct((rows, cols), F32)
    d, nm, nv = pl.pallas_call(
        body, name=name, grid=(rows // tr,), out_shape=(sd, sd, sd), in_specs=[blk] * 4, out_specs=(blk, blk, blk),
        compiler_params=_params(("arbitrary",)),
    )(w.reshape(rows, cols), g.reshape(rows, cols), m.reshape(rows, cols), v.reshape(rows, cols))
    return d.reshape(shape), nm.reshape(shape), nv.reshape(shape)


def _owner_blocks(a):
    return a.reshape(N_DEV, a.shape[0] // N_DEV, a.shape[1])


class _LocalWeights:
    def __init__(self, w_upT, w_down, w_o, w_paT, w_pbT):
        self.weights = (w_upT, w_down, w_o, w_paT, w_pbT)
        self.items = {}

    def sender(self, stage, items=None):
        self.items[stage] = items
        return None

    def sent(self, stage, handle):
        pass

    def mixer_weights(self, after):
        return self.weights[1:]

    def ffn_weights(self, after):
        return self.weights[0]


def _local_step(x, ctx, target, modv, mcv, gam, g_mix, g_ffn, gna, ln_g, ln_b, w_s, b_s, g_final, w_inT, comm):
    nb_ex, seq, _ = x.shape
    ctx_len = ctx.shape[1]
    xt = x.reshape(nb_ex * seq, D)
    ct = ctx.reshape(nb_ex * ctx_len, D)
    tgt = target.reshape(nb_ex * seq, D)
    bs3 = b_s.reshape(GROUPS, SGU_BLOCK, 1)

    pc, hc, _ = _inproj(ct, mcv, g_mix, w_inT, CTX_COLS, ctx_len, "inproj_ctx")
    p, h, handle = _inproj(xt, modv, g_mix, w_inT, IN_COLS, seq, "inproj_lat", sender=comm.sender("inproj"))
    comm.sent("inproj", handle)
    cst_f, cst_b, s_ctx, _ = _hgrn_fwd(pc, gam, None, ctx_len, False, "hgrn_fwd_ctx")
    o_up, o_down, st_f, st_b, _, handle = _hgrn_fwd(p, gam, s_ctx, seq, True, "hgrn_fwd_lat",
                                                    sender=comm.sender("scan"))
    comm.sent("scan", handle)
    w_down, w_o, w_paT, w_pbT = comm.mixer_weights(o_up)
    x1, mix, merged, oa, obm = _tail_fwd(p, o_up, o_down, xt, modv, gna, ln_g, ln_b, w_s, bs3, w_paT, w_pbT, w_o, seq)
    w_upT = comm.ffn_weights(x1)
    dx1, h2, dffn, act, dup, dmod_ffn, small_ffn = _ffn(x1, tgt, modv, g_ffn, g_final, w_upT, w_down, seq)
    gw_upT, _ = _grad_matmul(dup, h2, "gw_up")
    gw_down, _ = _grad_matmul(act, dffn, "gw_down", tn=256)
    scatter = lambda *grads: [(_owner_blocks(g), "scatter") for g in grads]
    dpt, do, dmix, dpa, dpb, dmod_tail, small_tail, dws, dbs, handle = _tail_bwd(
        p, o_up, o_down, dx1, mix, modv, gna, ln_g, ln_b, w_s, bs3, w_paT, w_pbT, w_o, seq,
        sender=comm.sender("tail_bwd", scatter(gw_upT)))
    comm.sent("tail_bwd", handle)
    gw_o, _ = _grad_matmul(merged, dmix, "gw_o")
    gw_paT, _ = _grad_matmul(dpa, oa, "gw_pa")
    gw_pbT, _ = _grad_matmul(dpb, obm, "gw_pb")
    def at_row(row, a):
        return jnp.pad(a, ((row, SMALL_ROWS - row - a.shape[0]), (0, D - a.shape[1])))

    small_early = (at_row(1, small_ffn[0:2])
                   + at_row(3, small_tail[0:3])
                   + at_row(6, dbs.reshape(1, GROUPS * SGU_BLOCK))
                   + at_row(14, small_ffn[2:3]))
    dws_rows = dws.reshape(GROUPS * SGU_BLOCK, SGU_BLOCK)
    d_up, d_down, dlb, ds0, handle = _hgrn_bwd(
        p, gam, do, st_f, st_b, None, seq, True, "hgrn_bwd_lat",
        sender=comm.sender("scan_bwd", scatter(gw_down, gw_o, gw_paT, gw_pbT)
                           + [(small_early, "gather"), (dws_rows, "gather")]))
    comm.sent("scan_bwd", handle)
    c_up, c_down, dlb_c, _, _ = _hgrn_bwd(pc, gam, None, cst_f, cst_b, ds0, ctx_len, False, "hgrn_bwd_ctx")
    dpc, dmc, small_c, _ = _inproj_bwd(c_up, c_down, None, ct, None, mcv, g_mix, w_inT, ctx_len, "inproj_bwd_ctx")
    gw_inT, _ = _grad_in(d_up, d_down, dpt, h, _grad_matmul(dpc, hc, "gw_in_ctx")[0])
    grad_x, dmod_in, small_in, handle = _inproj_bwd(d_up, d_down, dpt, xt, dx1, modv, g_mix, w_inT, seq,
                                                   "inproj_bwd_lat", sender=comm.sender("inproj_bwd", scatter(gw_inT)))
    comm.sent("inproj_bwd", handle)
    dmod = dmod_in + dmod_tail + dmod_ffn
    small_late = (at_row(0, small_in[0:1] + small_c[0:1])
                  + at_row(7, (dlb + dlb_c).reshape(1, 2 * KW))
                  + at_row(8, dmc[0, 0:N_MOD])
                  + at_row(16, dmod[:, 0:N_MOD].reshape(nb_ex * N_MOD, D)))
    comm.sender("last", [(small_late, "gather")])
    return grad_x.reshape(x.shape)


def kernel(x, c, ctx, c_ctx, w_mod, b_mod, g_mix, g_ffn, w_in, lb_gamma, g_norm_a, ln_v_g, ln_v_b, w_s, b_s, w_pa, w_pb, w_o, w_up, w_down, g_final, loss_target, m_c_ctx, m_w_mod, m_b_mod, m_g_mix, m_g_ffn, m_w_in, m_lb_gamma, m_g_norm_a, m_ln_v_g, m_ln_v_b, m_w_s, m_b_s, m_w_pa, m_w_pb, m_w_o, m_w_up, m_w_down, m_g_final, v_c_ctx, v_w_mod, v_b_mod, v_g_mix, v_g_ffn, v_w_in, v_lb_gamma, v_g_norm_a, v_ln_v_g, v_ln_v_b, v_w_s, v_b_s, v_w_pa, v_w_pb, v_w_o, v_w_up, v_w_down, v_g_final):
    nb_ex = x.shape[0]
    me = 4 * lax.axis_index("x") + 2 * lax.axis_index("y") + lax.axis_index("c")
    cd = MXU_DTYPE
    mod_cols = w_mod.shape[2]
    lb_cols = lb_gamma.shape[2]

    w_inT_l = w_in[0].T.astype(cd)
    w_upT_l = w_up[0].T.astype(cd)
    w_paT_l = w_pa[0].T.astype(cd)
    w_pbT_l = w_pb[0].T.astype(cd)
    cl = jnp.concatenate([c, jnp.pad(lb_gamma.reshape(1, 4 * lb_cols), ((0, 0), (0, D - 4 * lb_cols))),
                          jnp.zeros((8 - nb_ex - 1, D), F32)], axis=0)
    g_in, g_cl = _gather_two_level([w_inT_l, cl], "gather_w_in")
    w_inT = g_in.reshape(IN_COLS, D)
    c_all = g_cl[:, 0:nb_ex].reshape(N_DEV * nb_ex, D)
    gam = jnp.transpose(g_cl[:, nb_ex, 0:4 * lb_cols].reshape(N_DEV, 4, lb_cols), (1, 0, 2)).reshape(4, KW)

    n_c = N_DEV * nb_ex
    cvec = jnp.concatenate([c_all, c_ctx.reshape(1, D), jnp.zeros((7, D), F32)], axis=0)
    b_mod_l = lax.dynamic_slice(b_mod, (0, me * mod_cols), (1, mod_cols))
    mod_l, svec = _mod_fwd(cvec, w_mod[0], b_mod_l)
    (g_mod,) = _gather_two_level([mod_l], "gather_mod")
    mod_all = jnp.transpose(g_mod, (1, 0, 2)).reshape(n_c + 8, N_MOD * D)
    modv = lax.dynamic_slice(mod_all, (me * nb_ex, 0), (nb_ex, N_MOD * D)).reshape(nb_ex, N_MOD, D)
    mcv = mod_all[n_c].reshape(1, N_MOD, D)

    handles, leftover = {}, {}

    class Comm:
        def sender(self, stage, items=None):
            if stage == "inproj":
                return _Sender([(w_down[0].astype(cd), "gather"), (w_o[0].astype(cd), "gather"), (w_paT_l, "gather"),
                                (w_pbT_l, "gather")])
            if stage == "scan":
                return _Sender([(w_upT_l, "gather")])
            if stage == "last":
                leftover["items"] = items
                return None
            return _Sender(items)

        def sent(self, stage, handle):
            handles[stage] = handle

        def mixer_weights(self, after):
            g_down, g_o, g_pa, g_pb = _exchange_wait(handles["inproj"], after)
            return g_down.reshape(D_FF, D), g_o.reshape(D, D), g_pa.reshape(D, KW), g_pb.reshape(D, KW)

        def ffn_weights(self, after):
            (g_up,) = _exchange_wait(handles["scan"], after)
            return g_up.reshape(2 * D_FF, D)

    grad_x = _local_step(
        x, ctx, loss_target, modv, mcv, gam, g_mix, g_ffn, g_norm_a, ln_v_g, ln_v_b, w_s[0], b_s[0],
        g_final.reshape(1, D), w_inT, Comm())
    last, last_started = _exchange_start(leftover["items"], "gather_small_late", after=leftover["items"][0][0])

    (r_up,) = _exchange_wait(handles["tail_bwd"], last_started)
    r_down, r_o, r_pa, r_pb, r_small, r_dws = _exchange_wait(handles["scan_bwd"], r_up)
    raw_up = _adamw_sum8(r_up, w_up[0].T, m_w_up[0].T, v_w_up[0].T, "adamw_w_up")
    raw_down = _adamw_sum8(r_down, w_down[0], m_w_down[0], v_w_down[0], "adamw_w_down")
    raw_o = _adamw_sum8(r_o, w_o[0], m_w_o[0], v_w_o[0], "adamw_w_o")
    (r_in,) = _exchange_wait(handles["inproj_bwd"], raw_up[1])
    raw_in = _adamw_sum8(r_in, w_in[0].T, m_w_in[0].T, v_w_in[0].T, "adamw_w_in")
    (r_late,) = _exchange_wait(last, raw_in[1])
    done = {"w_in": [a.T[None] for a in raw_in], "w_up": [a.T[None] for a in raw_up],
            "w_down": [a[None] for a in raw_down], "w_o": [a[None] for a in raw_o]}
    grad_w_in, grad_w_up, grad_w_down, grad_w_o = (done[k][0] for k in ("w_in", "w_up", "w_down", "w_o"))
    grad_w_pa = _sum8(r_pa, "sum_w_pa").T[None]
    grad_w_pb = _sum8(r_pb, "sum_w_pb").T[None]
    as_2d = {"g_final": (1, D), "b_s": (1, GROUPS * SGU_BLOCK), "b_mod": (N_MOD, D), "w_s": (GROUPS * SGU_BLOCK, SGU_BLOCK)}
    small_params = {"g_mix": (g_mix, m_g_mix, v_g_mix), "g_ffn": (g_ffn, m_g_ffn, v_g_ffn),
                    "g_final": (g_final, m_g_final, v_g_final), "g_norm_a": (g_norm_a, m_g_norm_a, v_g_norm_a),
                    "ln_v_g": (ln_v_g, m_ln_v_g, v_ln_v_g), "ln_v_b": (ln_v_b, m_ln_v_b, v_ln_v_b),
                    "b_s": (b_s, m_b_s, v_b_s), "b_mod": (b_mod, m_b_mod, v_b_mod), "w_s": (w_s, m_w_s, v_w_s)}
    tot, dgam, small_done = _small_finish(
        r_small, r_late, r_dws, gam, nb_ex,
        {n: tuple(a.reshape(as_2d.get(n, a.shape)) for a in wmv) for n, wmv in small_params.items()})
    for n, outs in small_done.items():
        done[n] = [a.reshape(small_params[n][0].shape) for a in outs]
    loss = tot[14, 0]
    grad_g_mix, grad_g_ffn, grad_g_final, grad_g_norm_a, grad_ln_v_g, grad_ln_v_b, grad_b_s, grad_b_mod, grad_w_s = (
        done[n][0] for n in ("g_mix", "g_ffn", "g_final", "g_norm_a", "ln_v_g", "ln_v_b", "b_s", "b_mod", "w_s"))
    grad_lb_gamma = lax.dynamic_slice(dgam.reshape(2, 2, KW), (0, 0, me * lb_cols), (2, 2, lb_cols))

    dmod_all = r_late[:, 16:16 + nb_ex * N_MOD].reshape(n_c, N_MOD * D)
    dmod_l = jnp.concatenate([lax.dynamic_slice(dmod_all, (0, me * mod_cols), (n_c, mod_cols)),
                              lax.dynamic_slice(tot[8:8 + N_MOD].reshape(1, N_MOD * D), (0, me * mod_cols), (1, mod_cols)),
                              jnp.zeros((7, mod_cols), F32)], axis=0)
    gw_mod, gc = _mod_bwd(svec, cvec, dmod_l, w_mod[0])
    grad_w_mod = gw_mod[None]
    (r_gc,) = _exchange([(gc[n_c:n_c + 8], "gather")], "gather_c_ctx", after=r_late)
    grad_c_ctx = _sum8(r_gc, "sum_c_ctx")[0]

    names = ["c_ctx", "w_mod", "b_mod", "g_mix", "g_ffn", "w_in", "lb_gamma", "g_norm_a", "ln_v_g", "ln_v_b", "w_s",
             "b_s", "w_pa", "w_pb", "w_o", "w_up", "w_down", "g_final"]
    weights = [c_ctx, w_mod, b_mod, g_mix, g_ffn, w_in, lb_gamma, g_norm_a, ln_v_g, ln_v_b, w_s, b_s, w_pa, w_pb, w_o,
               w_up, w_down, g_final]
    grads = [grad_c_ctx, grad_w_mod, grad_b_mod, grad_g_mix, grad_g_ffn, grad_w_in, grad_lb_gamma, grad_g_norm_a,
             grad_ln_v_g, grad_ln_v_b, grad_w_s, grad_b_s, grad_w_pa, grad_w_pb, grad_w_o, grad_w_up, grad_w_down,
             grad_g_final]
    ms = [m_c_ctx, m_w_mod, m_b_mod, m_g_mix, m_g_ffn, m_w_in, m_lb_gamma, m_g_norm_a, m_ln_v_g, m_ln_v_b, m_w_s, m_b_s,
          m_w_pa, m_w_pb, m_w_o, m_w_up, m_w_down, m_g_final]
    vs = [v_c_ctx, v_w_mod, v_b_mod, v_g_mix, v_g_ffn, v_w_in, v_lb_gamma, v_g_norm_a, v_ln_v_g, v_ln_v_b, v_w_s, v_b_s,
          v_w_pa, v_w_pb, v_w_o, v_w_up, v_w_down, v_g_final]
    deltas, new_ms, new_vs = [], [], []
    for nm, w, g, m, v in zip(names, weights, grads, ms, vs):
        d, nm_, nv_ = done[nm][1:] if nm in done else _adamw(w, g.reshape(w.shape), m, v, "adamw_" + nm)
        deltas.append(d)
        new_ms.append(nm_)
        new_vs.append(nv_)
    grads = [g.reshape(w.shape) for g, w in zip(grads, weights)]
    return (loss, grad_x, *grads, *deltas, *new_ms, *new_vs)
```

```python
import functools

import jax
import jax.numpy as jnp
from jax import lax
from jax.experimental import pallas as pl
from jax.experimental.pallas import tpu as pltpu

F32 = jnp.float32
MXU_DTYPE = jnp.bfloat16
PAYLOAD_DTYPE = jnp.bfloat16

N_DEV = 8
D = 1024
HEADS = 4
DK = 128
KW = HEADS * DK
CHUNK = 64
SGU_BLOCK = 128
GROUPS = 4
D_FF = 2816
FF_CHUNK = 256
N_MOD = 6
IN_COLS = 5632
CTX_COLS = 1536
TAIL_COLS = IN_COLS - 4 * KW
EPS = 1e-6
ADAM_LR, ADAM_B1, ADAM_B2, ADAM_EPS, ADAM_WD, ADAM_STEP = 0.001, 0.9, 0.999, 1e-08, 0.01, 10

VMEM_LIMIT = 56 * 1024 * 1024
TOKEN_TILE = 256
PROJ_TILE = 512
TAIL_TILE = 512
SMALL_ROWS = 40


def _params(sem):
    return pltpu.CompilerParams(dimension_semantics=sem, vmem_limit_bytes=VMEM_LIMIT)


_DN = {"nn": (((1,), (0,)), ((), ())), "nt": (((1,), (1,)), ((), ())), "tn": (((0,), (0,)), ((), ()))}


def _dot(a, b, form="nn"):
    return lax.dot_general(a.astype(MXU_DTYPE), b.astype(MXU_DTYPE), _DN[form], preferred_element_type=F32)


def _mask_dot(mask, v):
    bf = jnp.bfloat16
    hi = v.astype(bf)
    r1 = v - hi.astype(F32)
    mid = r1.astype(bf)
    lo = (r1 - mid.astype(F32)).astype(bf)
    w = v.shape[1]
    s = lax.dot_general(mask.astype(bf), jnp.concatenate([hi, mid, lo], axis=1), _DN["nn"], preferred_element_type=F32)
    return (s[:, 2 * w:] + s[:, w:2 * w]) + s[:, :w]


def _full(shape, single=False):
    n = len(shape)
    if single:
        return pl.BlockSpec(shape, lambda *_: (0,) * n, pipeline_mode=pl.Buffered(1))
    return pl.BlockSpec(shape, lambda *_: (0,) * n)


def _ordered_behind(body, in_specs, args, after):
    if after is None:
        return body
    at = len(in_specs)
    in_specs.append(pl.BlockSpec(memory_space=pl.ANY))
    args.append(after)
    return lambda *refs: body(*refs[:at], *refs[at + 1:])


def _sigmoid(z):
    return 0.5 * jnp.tanh(0.5 * z) + 0.5


def _gelu(x):
    c = 0.7978845608028654
    t = jnp.tanh(c * (x + 0.044715 * x * x * x))
    return 0.5 * x * (1.0 + t), t


def _gelu_grad(x, t):
    c = 0.7978845608028654
    return 0.5 * (1.0 + t) + 0.5 * x * (1.0 - t * t) * c * (1.0 + 3 * 0.044715 * x * x)


def _exchange(items, name, after=None):
    n = len(items)
    out_shape = []
    for a, mode in items:
        blk = a.shape if mode == "gather" else a.shape[1:]
        out_shape.append(jax.ShapeDtypeStruct((N_DEV,) + tuple(blk), a.dtype))

    def body(*refs):
        srcs, dsts = refs[:n], refs[n:2 * n]
        send_sems, recv_sems, local_sems = refs[2 * n:]
        x, y, c = lax.axis_index("x"), lax.axis_index("y"), lax.axis_index("c")
        me = 4 * x + 2 * y + c

        def src_for(i, dev):
            return srcs[i] if items[i][1] == "gather" else srcs[i].at[dev]

        local = [pltpu.make_async_copy(src_for(i, me), dsts[i].at[me], local_sems.at[i]) for i in range(n)]
        for cp in local:
            cp.start()
        remote = []
        for k in range(1, N_DEV):
            px = jnp.bitwise_xor(x, (k >> 2) & 1)
            py = jnp.bitwise_xor(y, (k >> 1) & 1)
            pc = jnp.bitwise_xor(c, k & 1)
            peer = 4 * px + 2 * py + pc
            for i in range(n):
                cp = pltpu.make_async_remote_copy(
                    src_ref=src_for(i, peer), dst_ref=dsts[i].at[me],
                    send_sem=send_sems.at[i * (N_DEV - 1) + k - 1], recv_sem=recv_sems.at[i * (N_DEV - 1) + k - 1],
                    device_id=(px, py, pc), device_id_type=pl.DeviceIdType.MESH)
                cp.start()
                remote.append(cp)
        for cp in remote:
            cp.wait()
        for cp in local:
            cp.wait()

    any_spec = pl.BlockSpec(memory_space=pl.ANY)
    in_specs, args = [any_spec] * n, [a for a, _ in items]
    if after is not None:
        in_specs.append(any_spec)
        args.append(after)
        exchange = body
        body = lambda *refs: exchange(*refs[:n], *refs[n + 1:])
    return pl.pallas_call(
        body, name=name, out_shape=out_shape, in_specs=in_specs, out_specs=[any_spec] * n,
        scratch_shapes=[pltpu.SemaphoreType.DMA((n * (N_DEV - 1),)), pltpu.SemaphoreType.DMA((n * (N_DEV - 1),)),
                        pltpu.SemaphoreType.DMA((n,))],
    )(*args)


def _gather_two_level(arrays, name):
    n = len(arrays)
    pieces = []
    for i, a in enumerate(arrays):
        rows = _Sender.PIECE_ROWS if a.shape[0] % _Sender.PIECE_ROWS == 0 else a.shape[0]
        pieces += [(i, r0, rows) for r0 in range(0, a.shape[0], rows)]

    def body(*refs):
        srcs, dsts = refs[:n], refs[n:2 * n]
        send_sems, recv_sems, local_sems = refs[2 * n:]
        x, y, c = lax.axis_index("x"), lax.axis_index("y"), lax.axis_index("c")
        me, sibling = (x, y, c), (x, y, 1 - c)
        x_nbr, y_nbr, diag = (1 - x, y, c), (x, 1 - y, c), (1 - x, 1 - y, c)

        def slot(px, py, pc):
            return 4 * px + 2 * py + pc

        def copy(u, k, block, to, own=False):
            i, r0, rows = pieces[u]
            there = dsts[i].at[slot(*block)].at[pl.ds(r0, rows)]
            return pltpu.make_async_remote_copy(
                src_ref=srcs[i].at[pl.ds(r0, rows)] if own else there, dst_ref=there,
                send_sem=send_sems.at[u * 7 + k], recv_sem=recv_sems.at[u * 7 + k],
                device_id=to, device_id_type=pl.DeviceIdType.MESH)

        units = range(len(pieces))
        mine = [pltpu.make_async_copy(srcs[i], dsts[i].at[slot(*me)], local_sems.at[i]) for i in range(n)]
        for cp in mine:
            cp.start()
        for u in units:
            copy(u, 1, me, x_nbr, own=True).start()
            copy(u, 2, me, y_nbr, own=True).start()
        for u in units:
            copy(u, 0, me, sibling, own=True).start()

        def relay_then_pass(k_from, frm, to, k_other, other):
            for u in units:
                copy(u, k_from, frm, me).wait_recv()
                copy(u, 3, frm, to).start()
                copy(u, 3 + k_from, frm, sibling).start()
            for u in units:
                copy(u, k_other, other, me).wait_recv()
                copy(u, 3 + k_other, other, sibling).start()

        @pl.when(c == 1)
        def _():
            relay_then_pass(1, x_nbr, y_nbr, 2, y_nbr)

        @pl.when(c == 0)
        def _():
            relay_then_pass(2, y_nbr, x_nbr, 1, x_nbr)

        for u in units:
            copy(u, 3, diag, me).wait_recv()
            copy(u, 6, diag, sibling).start()
        for u in units:
            copy(u, 0, sibling, me).wait_recv()
            for k, chip in ((4, x_nbr), (5, y_nbr), (6, diag)):
                copy(u, k, (chip[0], chip[1], 1 - c), me).wait_recv()
        for u in units:
            for k in range(7):
                copy(u, k, me, me, own=True).wait_send()
        for cp in mine:
            cp.wait()

    any_spec = pl.BlockSpec(memory_space=pl.ANY)
    return pl.pallas_call(
        body, name=name, out_shape=[jax.ShapeDtypeStruct((N_DEV,) + a.shape, a.dtype) for a in arrays],
        in_specs=[any_spec] * n, out_specs=[any_spec] * n,
        scratch_shapes=[pltpu.SemaphoreType.DMA((len(pieces) * 7,)), pltpu.SemaphoreType.DMA((len(pieces) * 7,)),
                        pltpu.SemaphoreType.DMA((n,))],
    )(*arrays)


_HBM = pl.BlockSpec(memory_space=pltpu.HBM)
_SEM = pl.BlockSpec(memory_space=pltpu.SEMAPHORE)
_EFFECT = pltpu.SideEffectType.DATAFLOW_SIDE_EFFECTING


def _split_copies(items, srcs, lands, send_sems, recv_sems):
    x, y, c = lax.axis_index("x"), lax.axis_index("y"), lax.axis_index("c")
    me = 4 * x + 2 * y + c
    copies = []
    for k in range(1, N_DEV):
        px = jnp.bitwise_xor(x, (k >> 2) & 1)
        py = jnp.bitwise_xor(y, (k >> 1) & 1)
        pc = jnp.bitwise_xor(c, k & 1)
        peer = 4 * px + 2 * py + pc
        for i in range(len(items)):
            src = srcs[i] if items[i][1] == "gather" else srcs[i].at[peer]
            copies.append(pltpu.make_async_remote_copy(
                src_ref=src, dst_ref=lands[i].at[me],
                send_sem=send_sems.at[i * (N_DEV - 1) + k - 1], recv_sem=recv_sems.at[i * (N_DEV - 1) + k - 1],
                device_id=(px, py, pc), device_id_type=pl.DeviceIdType.MESH))
    return me, copies


def _exchange_start(items, name, after):
    n = len(items)
    n_sem = n * (N_DEV - 1)
    srcs, lands = [], []
    for a, mode in items:
        blk = a.shape if mode == "gather" else a.shape[1:]
        srcs.append(pltpu.with_memory_space_constraint(a, pltpu.HBM))
        lands.append(pltpu.with_memory_space_constraint(lax.empty((N_DEV,) + tuple(blk), a.dtype), pltpu.HBM))

    def body(*refs):
        src_refs, land_refs = refs[:n], refs[n:2 * n]
        send_sems, recv_sems = refs[2 * n + 1], refs[2 * n + 2]
        local_sems = refs[4 * n + 3]
        me, copies = _split_copies(items, src_refs, land_refs, send_sems, recv_sems)
        for i in range(n):
            own = src_refs[i] if items[i][1] == "gather" else src_refs[i].at[me]
            cp = pltpu.make_async_copy(own, land_refs[i].at[me], local_sems.at[i])
            cp.start()
            cp.wait()
        for cp in copies:
            cp.start()

    out_shape = [pltpu.SemaphoreType.DMA((n_sem,)), pltpu.SemaphoreType.DMA((n_sem,))]
    out_shape += [pltpu.HBM(a.shape, a.dtype) for a in srcs] + [pltpu.HBM(a.shape, a.dtype) for a in lands]
    outs = pl.pallas_call(
        body, name=name, out_shape=out_shape,
        in_specs=[_HBM] * (2 * n) + [pl.BlockSpec(memory_space=pl.ANY)],
        out_specs=[_SEM, _SEM] + [_HBM] * (2 * n),
        input_output_aliases={i: 2 + i for i in range(2 * n)},
        scratch_shapes=[pltpu.SemaphoreType.DMA((n,))],
        compiler_params=pltpu.CompilerParams(has_side_effects=_EFFECT),
    )(*srcs, *lands, after)
    handle = (items, name, outs[0], outs[1], outs[2:2 + n], outs[2 + n:2 + 2 * n])
    return handle, outs[2]


class _Sender:
    PIECE_ROWS = 352

    def __init__(self, items, chunks=None):
        self.items, self.n = items, len(items)
        self.chunks = chunks
        if chunks is None:
            block_rows = [a.shape[0] if mode == "gather" else a.shape[1] for a, mode in items]
            self.chunks = [r // self.PIECE_ROWS if r % self.PIECE_ROWS == 0 else 1 for r in block_rows]
        self.srcs, self.lands = [], []
        for a, mode in items:
            blk = a.shape if mode == "gather" else a.shape[1:]
            self.srcs.append(pltpu.with_memory_space_constraint(a, pltpu.HBM))
            self.lands.append(pltpu.with_memory_space_constraint(lax.empty((N_DEV,) + tuple(blk), a.dtype), pltpu.HBM))

    def issue(self, src_refs, land_refs, send_sems, recv_sems, local_sems, step, n_steps):
        x, y, c = lax.axis_index("x"), lax.axis_index("y"), lax.axis_index("c")
        me = 4 * x + 2 * y + c
        copies = []
        for ch in range(max(self.chunks)):
            for k in range(1, N_DEV):
                px = jnp.bitwise_xor(x, (k >> 2) & 1)
                py = jnp.bitwise_xor(y, (k >> 1) & 1)
                pc = jnp.bitwise_xor(c, k & 1)
                peer = 4 * px + 2 * py + pc
                for i, (_, mode) in enumerate(self.items):
                    if ch >= self.chunks[i]:
                        continue
                    n_rows = land_refs[i].shape[1] // self.chunks[i]
                    rows = pl.ds(ch * n_rows, n_rows)
                    src = src_refs[i].at[rows] if mode == "gather" else src_refs[i].at[peer].at[rows]
                    copies.append(pltpu.make_async_remote_copy(
                        src_ref=src, dst_ref=land_refs[i].at[me].at[rows],
                        send_sem=send_sems.at[i * (N_DEV - 1) + k - 1], recv_sem=recv_sems.at[i * (N_DEV - 1) + k - 1],
                        device_id=(px, py, pc), device_id_type=pl.DeviceIdType.MESH))
        own = [pltpu.make_async_copy(src_refs[i] if mode == "gather" else src_refs[i].at[me], land_refs[i].at[me],
                                     local_sems.at[i]) for i, (_, mode) in enumerate(self.items)]

        @pl.when(step == 0)
        def _():
            for cp in own:
                cp.start()

        for s in range(n_steps):
            group = [cp for j, cp in enumerate(copies) if (j * n_steps) // len(copies) == s]
            if group:
                @pl.when(step == s)
                def _(group=group):
                    for cp in group:
                        cp.start()

        @pl.when(step == n_steps - 1)
        def _():
            for cp in own:
                cp.wait()


def _host_call(body, name, grid, in_specs, args, out_shape, out_specs, scratch_shapes, after=None, sender=None):
    in_specs, args, out_shape, out_specs = list(in_specs), list(args), list(out_shape), list(out_specs)
    scratch_shapes = list(scratch_shapes)
    semantics = ("arbitrary",) * len(grid)
    body = _ordered_behind(body, in_specs, args, after)
    if sender is None:
        res = pl.pallas_call(body, name=name, grid=grid, in_specs=in_specs, out_specs=out_specs, out_shape=out_shape,
                             scratch_shapes=scratch_shapes, compiler_params=_params(semantics))(*args)
        return res, None
    n, n_in, n_out, n_scr = sender.n, len(in_specs), len(out_shape), len(scratch_shapes)
    n_sem = n * (N_DEV - 1)
    n_steps = 1
    for g in grid:
        n_steps *= g
    compute = body

    def body(*refs):
        ins, s_in = refs[:n_in], refs[n_in:n_in + 2 * n]
        o0 = n_in + 2 * n
        outs, s_out = refs[o0:o0 + n_out], refs[o0 + n_out:o0 + n_out + 2 + 2 * n]
        scr = refs[o0 + n_out + 2 + 2 * n:]
        compute(*ins, *outs, *scr[:n_scr])
        step = pl.program_id(0)
        for d in range(1, len(grid)):
            step = step * grid[d] + pl.program_id(d)
        sender.issue(s_in[:n], s_in[n:], s_out[0], s_out[1], scr[n_scr], step, n_steps)

    res = pl.pallas_call(
        body, name=name, grid=grid,
        in_specs=in_specs + [_HBM] * (2 * n), out_specs=out_specs + [_SEM, _SEM] + [_HBM] * (2 * n),
        out_shape=out_shape + [pltpu.SemaphoreType.DMA((n_sem,)), pltpu.SemaphoreType.DMA((n_sem,))]
        + [pltpu.HBM(a.shape, a.dtype) for a in sender.srcs] + [pltpu.HBM(a.shape, a.dtype) for a in sender.lands],
        input_output_aliases={n_in + j: n_out + 2 + j for j in range(2 * n)},
        scratch_shapes=scratch_shapes + [pltpu.SemaphoreType.DMA((n,))],
        compiler_params=pltpu.CompilerParams(dimension_semantics=semantics, vmem_limit_bytes=VMEM_LIMIT,
                                             has_side_effects=_EFFECT),
    )(*args, *sender.srcs, *sender.lands)
    handle = (sender.items, name, res[n_out], res[n_out + 1], res[n_out + 2:n_out + 2 + n],
              res[n_out + 2 + n:n_out + 2 + 2 * n])
    return res[:n_out], handle


def _exchange_wait(handle, after):
    items, name, send_sems, recv_sems, srcs, lands = handle
    n = len(items)

    def body(*refs):
        src_refs, land_refs = refs[:n], refs[n:2 * n]
        send_ref, recv_ref = refs[2 * n], refs[2 * n + 1]
        _, copies = _split_copies(items, src_refs, land_refs, send_ref, recv_ref)
        for cp in copies:
            cp.wait_send()
            cp.wait_recv()

    outs = pl.pallas_call(
        body, name=name + "_wait",
        out_shape=[pltpu.HBM(a.shape, a.dtype) for a in srcs] + [pltpu.HBM(a.shape, a.dtype) for a in lands],
        in_specs=[_HBM] * (2 * n) + [_SEM, _SEM, pl.BlockSpec(memory_space=pl.ANY)], out_specs=[_HBM] * (2 * n),
        input_output_aliases={i: i for i in range(2 * n)},
        compiler_params=pltpu.CompilerParams(has_side_effects=_EFFECT),
    )(*srcs, *lands, send_sems, recv_sems, after)
    return outs[n:]


def _mod_fwd(cvec, w_mod_l, b_mod_l):
    rows, cols = cvec.shape[0], w_mod_l.shape[1]

    def body(c_ref, w_ref, b_ref, o_ref, s_ref):
        cv = c_ref[...]
        s = cv * _sigmoid(cv)
        s_ref[...] = s
        o_ref[...] = _dot(s, w_ref[...]) + b_ref[...]

    return pl.pallas_call(
        body, name="mod_fwd",
        out_shape=(jax.ShapeDtypeStruct((rows, cols), F32), jax.ShapeDtypeStruct((rows, D), F32)),
        in_specs=[_full((rows, D)), _full((D, cols)), _full((1, cols))],
        out_specs=(_full((rows, cols)), _full((rows, D))), grid=(1,),
        compiler_params=_params(("arbitrary",)),
    )(cvec, w_mod_l, b_mod_l)


def _mod_bwd(svec, cvec, dmod_l, w_mod_l):
    rows, cols = dmod_l.shape

    def body(s_ref, c_ref, d_ref, w_ref, gw_ref, gc_ref):
        gw_ref[...] = _dot(s_ref[...], d_ref[...], "tn")
        cv = c_ref[...]
        sg = _sigmoid(cv)
        gc_ref[...] = _dot(d_ref[...], w_ref[...], "nt") * (sg * (1.0 + cv * (1.0 - sg)))

    return pl.pallas_call(
        body, name="mod_bwd",
        out_shape=(jax.ShapeDtypeStruct((D, cols), F32), jax.ShapeDtypeStruct((rows, D), F32)),
        in_specs=[_full((rows, D)), _full((rows, D)), _full((rows, cols)), _full((D, cols))],
        out_specs=(_full((D, cols)), _full((rows, D))), grid=(1,),
        compiler_params=_params(("arbitrary",)),
    )(svec, cvec, dmod_l, w_mod_l)


def _inproj(xt, modv, g, w_inT, n_cols, rows_per_example, name, after=None, sender=None):
    rows = xt.shape[0]
    tm = min(PROJ_TILE, rows_per_example)
    per_b = rows_per_example // tm
    shared_mod = modv.shape[0] == 1

    def body(x_ref, mod_ref, g_ref, w_ref, p_ref, h_ref):
        x = x_ref[...]
        r = lax.rsqrt(jnp.mean(x * x, axis=-1, keepdims=True) + EPS)
        h = (x * r * g_ref[...]) * (1.0 + mod_ref[0, 1:2, :]) + mod_ref[0, 0:1, :]
        hb = h.astype(MXU_DTYPE)
        h_ref[...] = hb
        for j in range(n_cols // KW):
            p_ref[:, j * KW:(j + 1) * KW] = _dot(hb, w_ref[j * KW:(j + 1) * KW, :], "nt").astype(p_ref.dtype)

    mod_idx = (lambda i: (0, 0, 0)) if shared_mod else (lambda i: (i // per_b, 0, 0))
    in_specs = [pl.BlockSpec((tm, D), lambda i: (i, 0)), pl.BlockSpec((1, N_MOD, D), mod_idx), _full((1, D)),
                pl.BlockSpec((n_cols, D), lambda i: (0, 0), pipeline_mode=pl.Buffered(1))]
    (p, h), handle = _host_call(
        body, name, (rows // tm,), in_specs, [xt, modv, g, w_inT],
        [jax.ShapeDtypeStruct((rows, n_cols), MXU_DTYPE), jax.ShapeDtypeStruct((rows, D), MXU_DTYPE)],
        [pl.BlockSpec((tm, n_cols), lambda i: (i, 0)), pl.BlockSpec((tm, D), lambda i: (i, 0))], [],
        after=after, sender=sender)
    return p, h, handle


def _tri(reverse, n):
    row = lax.broadcasted_iota(jnp.int32, (n, n), 0)
    col = lax.broadcasted_iota(jnp.int32, (n, n), 1)
    same = (row // CHUNK) == (col // CHUNK)
    return same & ((col >= row) if reverse else (col <= row))


def _per_chunk_rows(x, reverse):
    n = x.shape[0]
    rows = [x[j * CHUNK:j * CHUNK + 1] if reverse else x[(j + 1) * CHUNK - 1:(j + 1) * CHUNK] for j in range(n // CHUNK)]
    return jnp.concatenate([jnp.broadcast_to(r, (CHUNK, x.shape[1])) for r in rows], axis=0), rows


def _lower_bound(gam_ref, direction):
    return _sigmoid(gam_ref[direction:direction + 1, :] - gam_ref[2 + direction:3 + direction, :])


def _gate_prep(z, lb, tri, reverse):
    sg = _sigmoid(z)
    f = lb + (1.0 - lb) * sg
    g = jnp.log(f)
    b = _mask_dot(tri, g)
    bl, bl_rows = _per_chunk_rows(b, reverse)
    mid = 0.5 * bl
    return sg, g, 1.0 - f, b, jnp.exp(mid), [jnp.exp(0.5 * r) for r in bl_rows], jnp.exp(mid - b), mid


def _hgrn_fwd(p, gam, s0, rows_per_example, with_out, name, sender=None):
    rows = p.shape[0]
    nb_ex = rows // rows_per_example
    rb = min(TOKEN_TILE, rows_per_example)
    cpb = rb // CHUNK
    nb = rows_per_example // rb
    n_chunks = rows // CHUNK
    has_s0 = s0 is not None

    def body(*refs):
        it = iter(refs)
        gam_ref = next(it)
        zf_ref, vf_ref = next(it), next(it)
        qf_ref = next(it) if with_out else None
        zb_ref, vb_ref = next(it), next(it)
        qb_ref = next(it) if with_out else None
        s0_ref = next(it) if has_s0 else None
        if with_out:
            of_ref, ob_ref = next(it), next(it)
        stash_f, stash_b, fin_ref = next(it), next(it), next(it)
        st_ref = next(it)
        i = pl.program_id(1)

        @pl.when(i == 0)
        def _():
            if has_s0:
                st_ref[...] = s0_ref[:, 0]
            else:
                st_ref[...] = jnp.zeros_like(st_ref)

        for direction, (z_ref, v_ref, q_ref, stash) in enumerate(
                ((zf_ref, vf_ref, qf_ref, stash_f), (zb_ref, vb_ref, qb_ref, stash_b))):
            reverse = direction == 1
            tri = _tri(reverse, rb)
            lb = _lower_bound(gam_ref, direction)
            z = z_ref[...].astype(F32)
            v = v_ref[...].astype(F32)
            _, _, k, b, em, em_rows, e2, mid = _gate_prep(z, lb, tri, reverse)
            kd = (k * (e2 * em)).astype(MXU_DTYPE)
            vb = v.astype(MXU_DTYPE)
            if with_out:
                q = q_ref[...].astype(F32)
                qi = q * jnp.exp(b - mid)
                qe = (qi * em).astype(MXU_DTYPE)
                qi = qi.astype(MXU_DTYPE)
                ki = (k * e2).astype(MXU_DTYPE)
                intra = []
                for h in range(HEADS):
                    hs = slice(h * DK, (h + 1) * DK)
                    sc = jnp.where(tri, _dot(qi[:, hs], ki[:, hs], "nt"), 0.0)
                    intra.append(_dot(sc, vb[:, hs]))
            for j in (range(cpb - 1, -1, -1) if reverse else range(cpb)):
                rs = slice(j * CHUNK, (j + 1) * CHUNK)
                a = em_rows[j] * em_rows[j]
                for h in range(HEADS):
                    hs = slice(h * DK, (h + 1) * DK)
                    st = st_ref[direction, h]
                    stash[j, h] = st.astype(stash.dtype)
                    if with_out:
                        (ob_ref if reverse else of_ref)[rs, hs] = intra[h][rs] + _dot(qe[rs, hs], st, "nt")
                    st_ref[direction, h] = st * a[:, hs] + _dot(vb[rs, hs], kd[rs, hs], "tn")

        @pl.when(i == nb - 1)
        def _():
            fin_ref[:, 0] = st_ref[...]

    up = lambda b, i: b * nb + i
    down = lambda b, i: b * nb + nb - 1 - i
    col = lambda rowf, c: pl.BlockSpec((rb, KW), lambda b, i: (rowf(b, i), c))
    in_specs = [_full((4, KW)), col(up, 0), col(up, 2)] + ([col(up, 3)] if with_out else [])
    in_specs += [col(down, 1), col(down, 2)] + ([col(down, 3)] if with_out else [])
    args = [gam, p, p] + ([p] if with_out else []) + [p, p] + ([p] if with_out else [])
    if has_s0:
        in_specs.append(pl.BlockSpec((2, 1, HEADS, DK, DK), lambda b, i: (0, b, 0, 0, 0)))
        args.append(s0)
    out_shape, out_specs = [], []
    if with_out:
        out_shape += [jax.ShapeDtypeStruct((rows, KW), F32)] * 2
        out_specs += [pl.BlockSpec((rb, KW), lambda b, i: (up(b, i), 0)),
                      pl.BlockSpec((rb, KW), lambda b, i: (down(b, i), 0))]
    out_shape += [jax.ShapeDtypeStruct((n_chunks, HEADS, DK, DK), MXU_DTYPE)] * 2
    out_specs += [pl.BlockSpec((cpb, HEADS, DK, DK), lambda b, i: (up(b, i), 0, 0, 0)),
                  pl.BlockSpec((cpb, HEADS, DK, DK), lambda b, i: (down(b, i), 0, 0, 0))]
    out_shape.append(jax.ShapeDtypeStruct((2, nb_ex, HEADS, DK, DK), F32))
    out_specs.append(pl.BlockSpec((2, 1, HEADS, DK, DK), lambda b, i: (0, b, 0, 0, 0)))
    res, handle = _host_call(body, name, (nb_ex, nb), in_specs, args, out_shape, out_specs,
                             [pltpu.VMEM((2, HEADS, DK, DK), F32)], sender=sender)
    return (*res, handle)


def _hgrn_bwd(p, gam, do, stash_f, stash_b, ds_end, rows_per_example, with_out, name, after=None, sender=None):
    rows = p.shape[0]
    nb_ex = rows // rows_per_example
    rb = min(TOKEN_TILE, rows_per_example)
    cpb = rb // CHUNK
    nb = rows_per_example // rb
    has_end = ds_end is not None

    def body(*refs):
        it = iter(refs)
        gam_ref = next(it)
        ins = []
        for _ in range(2):
            z_ref, v_ref = next(it), next(it)
            q_ref = next(it) if with_out else None
            do_ref = next(it) if with_out else None
            ins.append((z_ref, v_ref, q_ref, do_ref, next(it)))
        end_ref = next(it) if has_end else None
        outs = [next(it), next(it)]
        dlb_ref, ds0_ref = next(it), next(it)
        dst_ref = next(it)
        b_id, i = pl.program_id(0), pl.program_id(1)

        @pl.when(i == 0)
        def _():
            if has_end:
                dst_ref[...] = end_ref[:, 0]
            else:
                dst_ref[...] = jnp.zeros_like(dst_ref)

        @pl.when((i == 0) & (b_id == 0))
        def _():
            dlb_ref[...] = jnp.zeros_like(dlb_ref)

        for direction in range(2):
            z_ref, v_ref, q_ref, do_ref, stash = ins[direction]
            dgrp_ref = outs[direction]
            reverse = direction == 1
            tri = _tri(reverse, rb)
            tri_t = _tri(not reverse, rb)
            lb = _lower_bound(gam_ref, direction)
            heads = [slice(h * DK, (h + 1) * DK) for h in range(HEADS)]
            chunks = [slice(j * CHUNK, (j + 1) * CHUNK) for j in range(cpb)]
            grid_cat = lambda parts: jnp.concatenate([jnp.concatenate(row, axis=1) for row in parts], axis=0)
            cat = lambda parts: jnp.concatenate(parts, axis=1)
            z = z_ref[...].astype(F32)
            sg, g, k, b, em, em_rows, e2, mid = _gate_prep(z, lb, tri, reverse)
            e3 = e2 * em
            kd = k * e3
            kd_b = kd.astype(MXU_DTYPE)
            vb = v_ref[...].astype(MXU_DTYPE)
            if with_out:
                q = q_ref[...].astype(F32)
                dout = do_ref[...].astype(MXU_DTYPE)
                e1 = jnp.exp(b - mid)
                e4 = e1 * em
                qi, ki, qe = q * e1, k * e2, q * e4
                qi_b, ki_b, qe_b = qi.astype(MXU_DTYPE), ki.astype(MXU_DTYPE), qe.astype(MXU_DTYPE)
                dqi_p, dki_p, dv_p = [], [], []
                for hs in heads:
                    sc = jnp.where(tri, _dot(qi_b[:, hs], ki_b[:, hs], "nt"), 0.0)
                    dsc = jnp.where(tri, _dot(dout[:, hs], vb[:, hs], "nt"), 0.0)
                    dqi_p.append(_dot(dsc, ki_b[:, hs]))
                    dki_p.append(_dot(dsc, qi_b[:, hs], "tn"))
                    dv_p.append(_dot(sc, dout[:, hs], "tn"))
                dqi, dki, dv = cat(dqi_p), cat(dki_p), cat(dv_p)
                dqe = grid_cat([[_dot(dout[rs, hs], stash[j, h]) for h, hs in enumerate(heads)]
                                for j, rs in enumerate(chunks)])
                grow = [[_dot(dout[rs, hs], qe_b[rs, hs], "tn") for hs in heads] for rs in chunks]
            dkd_p = [[None] * HEADS for _ in range(cpb)]
            dvs_p = [[None] * HEADS for _ in range(cpb)]
            da_p = [[None] * HEADS for _ in range(cpb)]
            for j in (range(cpb) if reverse else range(cpb - 1, -1, -1)):
                rs = chunks[j]
                a = em_rows[j] * em_rows[j]
                for h, hs in enumerate(heads):
                    dst = dst_ref[direction, h]
                    dkd_p[j][h] = _dot(vb[rs, hs], dst)
                    dvs_p[j][h] = _dot(kd_b[rs, hs], dst, "nt")
                    da_p[j][h] = jnp.broadcast_to(
                        jnp.sum(dst * stash[j, h].astype(F32), axis=0, keepdims=True), (CHUNK, DK))
                    new_dst = dst * a[:, hs]
                    dst_ref[direction, h] = new_dst + grow[j][h] if with_out else new_dst
            dkd, dvs, da = grid_cat(dkd_p), grid_cat(dvs_p), grid_cat(da_p)
            t_kd = dkd * kd
            dk = dkd * e3
            db = -t_kd
            tot = t_kd
            if with_out:
                dgrp_ref[:, KW:2 * KW] = (dvs + dv).astype(dgrp_ref.dtype)
                dgrp_ref[:, 2 * KW:] = (dqi * e1 + dqe * e4).astype(dgrp_ref.dtype)
                dk = dk + dki * e2
                t_qi, t_ki, t_qe = dqi * qi, dki * ki, dqe * qe
                db = db + t_qi - t_ki + t_qe
                tot = tot + 0.5 * (t_ki - t_qi)
            else:
                dgrp_ref[:, KW:2 * KW] = dvs.astype(dgrp_ref.dtype)
            dbl = jnp.concatenate([jnp.broadcast_to(jnp.sum(tot[rs], axis=0, keepdims=True), (CHUNK, KW))
                                   for rs in chunks], axis=0) + da * (em * em)
            dg = _mask_dot(tri_t, db) + dbl
            df = dg * jnp.exp(-g) - dk
            dgrp_ref[:, 0:KW] = (df * (1.0 - lb) * sg * (1.0 - sg)).astype(dgrp_ref.dtype)
            dlb_ref[direction:direction + 1, :] += jnp.sum(df * (1.0 - sg), axis=0, keepdims=True)

        @pl.when(i == nb - 1)
        def _():
            ds0_ref[:, 0] = dst_ref[...]

    rows_of = (lambda b, i: b * nb + nb - 1 - i, lambda b, i: b * nb + i)
    in_specs, args = [_full((4, KW))], [gam]
    for direction in range(2):
        rf = rows_of[direction]
        col = lambda c, rf=rf: pl.BlockSpec((rb, KW), lambda b, i: (rf(b, i), c))
        in_specs += [col(direction), col(2)]
        args += [p, p]
        if with_out:
            in_specs += [col(3), col(0)]
            args += [p, do]
        in_specs.append(pl.BlockSpec((cpb, HEADS, DK, DK), lambda b, i, rf=rf: (rf(b, i), 0, 0, 0)))
        args.append((stash_f, stash_b)[direction])
    if has_end:
        in_specs.append(pl.BlockSpec((2, 1, HEADS, DK, DK), lambda b, i: (0, b, 0, 0, 0)))
        args.append(ds_end)
    out_shape, out_specs = [], []
    for direction in range(2):
        rf = rows_of[direction]
        width = (3 if with_out else 2) * KW
        out_shape.append(jax.ShapeDtypeStruct((rows, width), MXU_DTYPE))
        out_specs.append(pl.BlockSpec((rb, width), lambda b, i, rf=rf: (rf(b, i), 0)))
    out_shape += [jax.ShapeDtypeStruct((2, KW), F32), jax.ShapeDtypeStruct((2, nb_ex, HEADS, DK, DK), F32)]
    out_specs += [_full((2, KW)), pl.BlockSpec((2, 1, HEADS, DK, DK), lambda b, i: (0, b, 0, 0, 0))]
    res, handle = _host_call(body, name, (nb_ex, nb), in_specs, args, out_shape, out_specs,
                             [pltpu.VMEM((2, HEADS, DK, DK), F32)], after=after, sender=sender)
    return (*res, handle)


def _tail_forward(osum, og, u, v, ga, gb, gna, ln_g, ln_b, ws_ref, bs_ref, wpaT_ref, wpbT_ref):
    tm = osum.shape[0]
    gna4 = jnp.concatenate([gna] * HEADS, axis=1)
    r_parts = []
    for h in range(HEADS):
        oh = osum[:, h * DK:(h + 1) * DK]
        r_parts.append(jnp.broadcast_to(lax.rsqrt(jnp.mean(oh * oh, axis=-1, keepdims=True) + EPS), (tm, DK)))
    r = jnp.concatenate(r_parts, axis=1)
    on = osum * r
    sg_og = _sigmoid(og)
    silu_og = og * sg_og
    oan = on * gna4
    oa = oan * silu_og
    ug, tu = _gelu(u)
    vg, tv = _gelu(v)
    mu = jnp.mean(vg, axis=-1, keepdims=True)
    vc = vg - mu
    rstd = lax.rsqrt(jnp.mean(vc * vc, axis=-1, keepdims=True) + EPS)
    vhat = vc * rstd
    vln = vhat * ln_g + ln_b
    blocks = []
    for n in range(tm // SGU_BLOCK):
        rs = slice(n * SGU_BLOCK, (n + 1) * SGU_BLOCK)
        blocks.append(jnp.concatenate(
            [_dot(ws_ref[g], vln[rs, g * DK:(g + 1) * DK]) + bs_ref[g] for g in range(GROUPS)], axis=1))
    mixed = jnp.concatenate(blocks, axis=0) if len(blocks) > 1 else blocks[0]
    obm = ug * mixed
    pa = _dot(oa, wpaT_ref[...], "nt")
    pb = _dot(obm, wpbT_ref[...], "nt")
    sga, sgb = _sigmoid(ga), _sigmoid(gb)
    merged = sga * pa + sgb * pb
    return dict(r=r, on=on, sg_og=sg_og, silu_og=silu_og, oan=oan, oa=oa, ug=ug, tu=tu, tv=tv, rstd=rstd, vhat=vhat,
                vln=vln, mixed=mixed, obm=obm, pa=pa, pb=pb, sga=sga, sgb=sgb, merged=merged, gna4=gna4)


def _tail_in_specs(tm):
    tile = lambda c: pl.BlockSpec((tm, KW), lambda i: (i, c))
    return [tile(c) for c in range(4, 11)]


def _tail_weight_specs():
    return [_full((1, DK)), _full((1, KW)), _full((1, KW)), _full((GROUPS, SGU_BLOCK, SGU_BLOCK)),
            _full((GROUPS, SGU_BLOCK, 1)), _full((D, KW), single=True), _full((D, KW), single=True),
            _full((D, D), single=True)]


def _read_tail_inputs(of_ref, ob_ref, pcols):
    osum = of_ref[...] + ob_ref[...]
    og, u, v = (pcols[j][...].astype(F32) for j in range(3))
    ga = jnp.concatenate([pcols[3][...], pcols[4][...]], axis=1).astype(F32)
    gb = jnp.concatenate([pcols[5][...], pcols[6][...]], axis=1).astype(F32)
    return osum, og, u, v, ga, gb


def _tail_fwd(p, o_up, o_down, xt, modv, gna, ln_g, ln_b, w_s, b_s, w_paT, w_pbT, w_o, rows_per_example):
    rows = xt.shape[0]
    tm = min(TAIL_TILE, rows_per_example)
    per_b = rows_per_example // tm

    def body(of_ref, ob_ref, *rest):
        pcols = rest[:7]
        (x_ref, mod_ref, gna_ref, lng_ref, lnb_ref, ws_ref, bs_ref, wpaT_ref, wpbT_ref, wo_ref,
         x1_ref, mix_ref, merged_ref, oa_ref, obm_ref) = rest[7:]
        t = _tail_forward(*_read_tail_inputs(of_ref, ob_ref, pcols), gna_ref[...], lng_ref[...], lnb_ref[...],
                          ws_ref, bs_ref, wpaT_ref, wpbT_ref)
        mix = _dot(t["merged"], wo_ref[...])
        x1_ref[...] = x_ref[...] + mod_ref[0, 2:3, :] * mix
        mix_ref[...] = mix.astype(mix_ref.dtype)
        merged_ref[...] = t["merged"].astype(merged_ref.dtype)
        oa_ref[...] = t["oa"].astype(oa_ref.dtype)
        obm_ref[...] = t["obm"].astype(obm_ref.dtype)

    row = lambda w: pl.BlockSpec((tm, w), lambda i: (i, 0))
    in_specs = [row(KW), row(KW)] + _tail_in_specs(tm) + [row(D), pl.BlockSpec((1, N_MOD, D), lambda i: (i // per_b, 0, 0))]
    in_specs += _tail_weight_specs()
    return pl.pallas_call(
        body, name="tail_fwd", grid=(rows // tm,),
        out_shape=(jax.ShapeDtypeStruct((rows, D), F32), jax.ShapeDtypeStruct((rows, D), MXU_DTYPE),
                   jax.ShapeDtypeStruct((rows, D), MXU_DTYPE), jax.ShapeDtypeStruct((rows, KW), MXU_DTYPE),
                   jax.ShapeDtypeStruct((rows, KW), MXU_DTYPE)),
        in_specs=in_specs, out_specs=(row(D), row(D), row(D), row(KW), row(KW)),
        compiler_params=_params(("arbitrary",)),
    )(o_up, o_down, *([p] * 7), xt, modv, gna, ln_g, ln_b, w_s, b_s, w_paT, w_pbT, w_o)


def _tail_bwd(p, o_up, o_down, dx1, mix, modv, gna, ln_g, ln_b, w_s, b_s, w_paT, w_pbT, w_o, rows_per_example,
              after=None, sender=None):
    rows = dx1.shape[0]
    nb_ex = rows // rows_per_example
    tm = min(TAIL_TILE, rows_per_example)
    per_b = rows_per_example // tm

    def body(of_ref, ob_ref, *rest):
        pcols = rest[:7]
        (dx1_ref, mix_ref, mod_ref, gna_ref, lng_ref, lnb_ref, ws_ref, bs_ref, wpaT_ref, wpbT_ref, wo_ref,
         dpt_ref, do_ref, dmix_ref, dpa_ref, dpb_ref, dmod_ref, small_ref, dws_ref, dbs_ref) = rest[7:]
        i = pl.program_id(0)

        @pl.when(i == 0)
        def _():
            small_ref[...] = jnp.zeros_like(small_ref)
            dws_ref[...] = jnp.zeros_like(dws_ref)
            dbs_ref[...] = jnp.zeros_like(dbs_ref)

        @pl.when(i % per_b == 0)
        def _():
            dmod_ref[...] = jnp.zeros_like(dmod_ref)

        osum, og, u, v, ga, gb = _read_tail_inputs(of_ref, ob_ref, pcols)
        ln_g = lng_ref[...]
        t = _tail_forward(osum, og, u, v, ga, gb, gna_ref[...], ln_g, lnb_ref[...], ws_ref, bs_ref, wpaT_ref, wpbT_ref)
        dx1v = dx1_ref[...]
        dmod_ref[0, 2:3, :] += jnp.sum(dx1v * mix_ref[...].astype(F32), axis=0, keepdims=True)
        dmix = dx1v * mod_ref[0, 2:3, :]
        dmix_ref[...] = dmix.astype(dmix_ref.dtype)
        dmerged = _dot(dmix, wo_ref[...], "nt")
        sga, sgb = t["sga"], t["sgb"]
        dpa = dmerged * sga
        dpb = dmerged * sgb
        dpa_ref[...] = dpa.astype(dpa_ref.dtype)
        dpb_ref[...] = dpb.astype(dpb_ref.dtype)
        dga = dmerged * t["pa"] * sga * (1.0 - sga)
        dgb = dmerged * t["pb"] * sgb * (1.0 - sgb)
        doa = _dot(dpa, wpaT_ref[...])
        dobm = _dot(dpb, wpbT_ref[...])
        dug = dobm * t["mixed"]
        dmixed = dobm * t["ug"]
        du = dug * _gelu_grad(u, t["tu"])
        dvln_blocks = []
        for n in range(tm // SGU_BLOCK):
            rs = slice(n * SGU_BLOCK, (n + 1) * SGU_BLOCK)
            parts = []
            for g in range(GROUPS):
                gs = slice(g * DK, (g + 1) * DK)
                dm = dmixed[rs, gs]
                parts.append(_dot(ws_ref[g], dm, "tn"))
                dws_ref[g] += _dot(dm, t["vln"][rs, gs], "nt")
                dbs_ref[g] += jnp.sum(dm, axis=1, keepdims=True)
            dvln_blocks.append(jnp.concatenate(parts, axis=1))
        dvln = jnp.concatenate(dvln_blocks, axis=0) if len(dvln_blocks) > 1 else dvln_blocks[0]
        vhat = t["vhat"]
        small_ref[1:2, 0:KW] += jnp.sum(dvln * vhat, axis=0, keepdims=True)
        small_ref[2:3, 0:KW] += jnp.sum(dvln, axis=0, keepdims=True)
        dvhat = dvln * ln_g
        dvg = t["rstd"] * (dvhat - jnp.mean(dvhat, axis=-1, keepdims=True)
                           - vhat * jnp.mean(dvhat * vhat, axis=-1, keepdims=True))
        dv = dvg * _gelu_grad(v, t["tv"])
        sg_og = t["sg_og"]
        doan = doa * t["silu_og"]
        dog = doa * t["oan"] * (sg_og * (1.0 + og * (1.0 - sg_og)))
        prod = doan * t["on"]
        dgna = jnp.zeros((1, DK), F32)
        for h in range(HEADS):
            dgna = dgna + jnp.sum(prod[:, h * DK:(h + 1) * DK], axis=0, keepdims=True)
        small_ref[0:1, 0:DK] += dgna
        don = doan * t["gna4"]
        dot_parts = []
        for h in range(HEADS):
            hs = slice(h * DK, (h + 1) * DK)
            m = jnp.mean(don[:, hs] * t["on"][:, hs], axis=-1, keepdims=True)
            dot_parts.append(t["r"][:, hs] * (don[:, hs] - t["on"][:, hs] * m))
        do_ref[...] = jnp.concatenate(dot_parts, axis=1).astype(do_ref.dtype)
        for j, val in enumerate((dog, du, dv)):
            dpt_ref[:, j * KW:(j + 1) * KW] = val.astype(dpt_ref.dtype)
        dpt_ref[:, 3 * KW:3 * KW + D] = dga.astype(dpt_ref.dtype)
        dpt_ref[:, 3 * KW + D:] = dgb.astype(dpt_ref.dtype)

    row = lambda w: pl.BlockSpec((tm, w), lambda i: (i, 0))
    in_specs = [row(KW), row(KW)] + _tail_in_specs(tm) + [row(D), row(D), pl.BlockSpec((1, N_MOD, D), lambda i: (i // per_b, 0, 0))]
    in_specs += _tail_weight_specs()
    args = [o_up, o_down, *([p] * 7), dx1, mix, modv, gna, ln_g, ln_b, w_s, b_s, w_paT, w_pbT, w_o]
    cd = MXU_DTYPE
    res, handle = _host_call(
        body, "tail_bwd", (rows // tm,), in_specs, args,
        [jax.ShapeDtypeStruct((rows, TAIL_COLS), cd), jax.ShapeDtypeStruct((rows, KW), cd),
         jax.ShapeDtypeStruct((rows, D), cd), jax.ShapeDtypeStruct((rows, D), cd),
         jax.ShapeDtypeStruct((rows, D), cd), jax.ShapeDtypeStruct((nb_ex, 8, D), F32),
         jax.ShapeDtypeStruct((8, D), F32), jax.ShapeDtypeStruct((GROUPS, SGU_BLOCK, SGU_BLOCK), F32),
         jax.ShapeDtypeStruct((GROUPS, SGU_BLOCK, 1), F32)],
        [row(TAIL_COLS), row(KW), row(D), row(D), row(D),
         pl.BlockSpec((1, 8, D), lambda i: (i // per_b, 0, 0)), _full((8, D)),
         _full((GROUPS, SGU_BLOCK, SGU_BLOCK)), _full((GROUPS, SGU_BLOCK, 1))], [],
        after=after, sender=sender)
    return (*res, handle)


def _ffn(x1, target, modv, g_ffn, g_final, w_upT, w_down, rows_per_example):
    rows = x1.shape[0]
    nb_ex = rows // rows_per_example
    tm = min(TOKEN_TILE, rows_per_example)
    per_b = rows_per_example // tm
    n_ff = D_FF // FF_CHUNK

    def body(x1_ref, tgt_ref, mod_ref, gffn_ref, gfin_ref, wup_ref, wdn_ref,
             dx1_ref, h2_ref, dffn_ref, act_ref, dup_ref, dmod_ref, small_ref, up_scr):
        i = pl.program_id(0)

        @pl.when(i == 0)
        def _():
            small_ref[...] = jnp.zeros_like(small_ref)

        @pl.when(i % per_b == 0)
        def _():
            dmod_ref[...] = jnp.zeros_like(dmod_ref)

        x1v = x1_ref[...]
        g2 = gffn_ref[...]
        m3, m4, m5 = mod_ref[0, 3:4, :], mod_ref[0, 4:5, :], mod_ref[0, 5:6, :]
        r2 = lax.rsqrt(jnp.mean(x1v * x1v, axis=-1, keepdims=True) + EPS)
        xn2 = x1v * r2
        h2 = (xn2 * g2) * (1.0 + m4) + m3
        h2b = h2.astype(MXU_DTYPE)
        h2_ref[...] = h2b
        def up_pair(j):
            lo = j * FF_CHUNK
            return (_dot(h2b, wup_ref[lo:lo + FF_CHUNK, :], "nt"),
                    _dot(h2b, wup_ref[D_FF + lo:D_FF + lo + FF_CHUNK, :], "nt"))

        group_end = {min(e, n_ff): s for s, e in ((0, 4), (4, 8), (8, 12))}
        cur, ffn = up_pair(0), None
        for j in range(n_ff):
            nxt = up_pair(j + 1) if j + 1 < n_ff else None
            cs = slice(j * FF_CHUNK, (j + 1) * FF_CHUNK)
            a, bgate = cur
            up_scr[:, cs] = a
            up_scr[:, D_FF + j * FF_CHUNK:D_FF + (j + 1) * FF_CHUNK] = bgate
            act_ref[:, cs] = (a * _sigmoid(a) * bgate).astype(MXU_DTYPE)
            cur = nxt
            if j + 1 in group_end:
                gs = slice(group_end[j + 1] * FF_CHUNK, (j + 1) * FF_CHUNK)
                part = _dot(act_ref[:, gs], wdn_ref[gs, :])
                ffn = part if ffn is None else ffn + part
        x2 = x1v + m5 * ffn
        r3 = lax.rsqrt(jnp.mean(x2 * x2, axis=-1, keepdims=True) + EPS)
        xn3 = x2 * r3
        gf = gfin_ref[...]
        err = xn3 * gf - tgt_ref[...]
        loss = 0.5 * jnp.sum(jnp.mean(err * err, axis=-1, keepdims=True), axis=0, keepdims=True)
        small_ref[2:3, :] += jnp.broadcast_to(loss, (1, D))
        dy = err * (1.0 / D)
        small_ref[1:2, :] += jnp.sum(dy * xn3, axis=0, keepdims=True)
        dxn3 = dy * gf
        dx2 = r3 * (dxn3 - xn3 * jnp.mean(dxn3 * xn3, axis=-1, keepdims=True))
        dmod_ref[0, 5:6, :] += jnp.sum(dx2 * ffn, axis=0, keepdims=True)
        dffn = (dx2 * m5).astype(MXU_DTYPE)
        dffn_ref[...] = dffn
        dact_of = lambda j: _dot(dffn, wdn_ref[j * FF_CHUNK:(j + 1) * FF_CHUNK, :], "nt")
        cur, dh2 = dact_of(0), None
        for j in range(n_ff):
            nxt = dact_of(j + 1) if j + 1 < n_ff else None
            cs = slice(j * FF_CHUNK, (j + 1) * FF_CHUNK)
            a, bgate = up_scr[:, cs], up_scr[:, D_FF + j * FF_CHUNK:D_FF + (j + 1) * FF_CHUNK]
            s = _sigmoid(a)
            dup_ref[:, cs] = (cur * bgate * (s * (1.0 + a * (1.0 - s)))).astype(MXU_DTYPE)
            dup_ref[:, D_FF + j * FF_CHUNK:D_FF + (j + 1) * FF_CHUNK] = (cur * a * s).astype(MXU_DTYPE)
            cur = nxt
            if j + 1 in group_end:
                lo, hi = group_end[j + 1] * FF_CHUNK, (j + 1) * FF_CHUNK
                part = (_dot(dup_ref[:, lo:hi], wup_ref[lo:hi, :])
                        + _dot(dup_ref[:, D_FF + lo:D_FF + hi], wup_ref[D_FF + lo:D_FF + hi, :]))
                dh2 = part if dh2 is None else dh2 + part
        dmod_ref[0, 3:4, :] += jnp.sum(dh2, axis=0, keepdims=True)
        dmod_ref[0, 4:5, :] += jnp.sum(dh2 * xn2 * g2, axis=0, keepdims=True)
        small_ref[0:1, :] += jnp.sum(dh2 * (1.0 + m4) * xn2, axis=0, keepdims=True)
        dxn2 = dh2 * g2 * (1.0 + m4)
        dx1_ref[...] = dx2 + r2 * (dxn2 - xn2 * jnp.mean(dxn2 * xn2, axis=-1, keepdims=True))

    row = lambda w: pl.BlockSpec((tm, w), lambda i: (i, 0))
    cd = MXU_DTYPE
    return pl.pallas_call(
        body, name="ffn_fwd_bwd", grid=(rows // tm,),
        out_shape=(jax.ShapeDtypeStruct((rows, D), F32), jax.ShapeDtypeStruct((rows, D), cd),
                   jax.ShapeDtypeStruct((rows, D), cd), jax.ShapeDtypeStruct((rows, D_FF), cd),
                   jax.ShapeDtypeStruct((rows, 2 * D_FF), cd), jax.ShapeDtypeStruct((nb_ex, 8, D), F32),
                   jax.ShapeDtypeStruct((8, D), F32)),
        in_specs=[row(D), row(D), pl.BlockSpec((1, N_MOD, D), lambda i: (i // per_b, 0, 0)), _full((1, D)), _full((1, D)),
                  _full((2 * D_FF, D), single=True), _full((D_FF, D), single=True)],
        out_specs=(row(D), row(D), row(D), row(D_FF), row(2 * D_FF),
                   pl.BlockSpec((1, 8, D), lambda i: (i // per_b, 0, 0)), _full((8, D))),
        scratch_shapes=[pltpu.VMEM((tm, 2 * D_FF), F32)],
        compiler_params=_params(("arbitrary",)),
    )(x1, target, modv, g_ffn, g_final, w_upT, w_down)


def _scan_columns(up, down, n_groups):
    cols = [up[:, 0:KW].astype(F32), down[:, 0:KW].astype(F32)]
    for j in range(1, n_groups):
        cols.append(up[:, j * KW:(j + 1) * KW].astype(F32) + down[:, j * KW:(j + 1) * KW].astype(F32))
    return cols


def _inproj_bwd(d_up, d_down, dpt, xt, dx1, modv, g, w_inT, rows_per_example, name, sender=None, examples=None,
                fill=None):
    rows = xt.shape[0]
    latent = dx1 is not None
    n_cols = IN_COLS if latent else CTX_COLS
    n_groups = d_up.shape[1] // KW
    tm = min(PROJ_TILE, rows_per_example)
    per_b = rows_per_example // tm
    n_mod_blocks = rows // rows_per_example if latent else 1
    first_ex, n_ex = examples if examples is not None else (0, rows // rows_per_example)
    t0 = first_ex * per_b

    def body(*refs):
        it = iter(refs)
        up_ref, down_ref = next(it), next(it)
        dpt_ref = next(it) if latent else None
        x_ref = next(it)
        dx1_ref = next(it) if latent else None
        mod_ref, g_ref, w_ref = next(it), next(it), next(it)
        gx_ref = next(it) if latent else None
        dp_out = None if latent else next(it)
        dmod_ref, small_ref = next(it), next(it)
        dp_ref = next(it) if latent else dp_out
        i = pl.program_id(0)

        @pl.when(i == 0)
        def _():
            small_ref[...] = jnp.zeros_like(small_ref)

        @pl.when((i % per_b == 0) if latent else (i == 0))
        def _():
            dmod_ref[...] = jnp.zeros_like(dmod_ref)

        for j, val in enumerate(_scan_columns(up_ref[...], down_ref[...], n_groups)):
            dp_ref[:, j * KW:(j + 1) * KW] = val.astype(MXU_DTYPE)
        if latent:
            dh = _dot(dp_ref[...], w_ref[0:4 * KW, :]) + _dot(dpt_ref[...], w_ref[4 * KW:, :])
        else:
            dh = _dot(dp_ref[...], w_ref[...])
        x = x_ref[...]
        gv = g_ref[...]
        m1 = mod_ref[0, 1:2, :]
        r = lax.rsqrt(jnp.mean(x * x, axis=-1, keepdims=True) + EPS)
        xn = x * r
        dmod_ref[0, 0:1, :] += jnp.sum(dh, axis=0, keepdims=True)
        dmod_ref[0, 1:2, :] += jnp.sum(dh * xn * gv, axis=0, keepdims=True)
        small_ref[0:1, :] += jnp.sum(dh * (1.0 + m1) * xn, axis=0, keepdims=True)
        if latent:
            dxn = dh * gv * (1.0 + m1)
            gx_ref[...] = dx1_ref[...] + r * (dxn - xn * jnp.mean(dxn * xn, axis=-1, keepdims=True))

    row = lambda w: pl.BlockSpec((tm, w), lambda i: (i + t0, 0))
    mod_idx = (lambda i: ((i + t0) // per_b, 0, 0)) if latent else (lambda i: (0, 0, 0))
    in_specs = [row(n_groups * KW)] * 2 + ([row(TAIL_COLS)] if latent else []) + [row(D)] + ([row(D)] if latent else [])
    in_specs += [pl.BlockSpec((1, N_MOD, D), mod_idx), _full((1, D)),
                 pl.BlockSpec((n_cols, D), lambda i: (0, 0), pipeline_mode=pl.Buffered(1))]
    args = [d_up, d_down] + ([dpt] if latent else []) + [xt] + ([dx1] if latent else []) + [modv, g, w_inT]
    first = jax.ShapeDtypeStruct((rows, D), F32) if latent else jax.ShapeDtypeStruct((rows, n_cols), MXU_DTYPE)
    out_shape = [first, jax.ShapeDtypeStruct((n_mod_blocks, 8, D), F32), jax.ShapeDtypeStruct((8, D), F32)]
    out_specs = [row(D) if latent else row(n_cols), pl.BlockSpec((1, 8, D), mod_idx), _full((8, D))]
    scratch = [pltpu.VMEM((tm, 4 * KW), MXU_DTYPE)] if latent else []
    grid = (n_ex * per_b,)
    if fill is None:
        res, handle = _host_call(body, name, grid, in_specs, args, out_shape, out_specs, scratch, sender=sender)
        return (*res, handle)
    n_in = len(in_specs)
    compute = body
    res = pl.pallas_call(
        lambda *refs: compute(*refs[:n_in], *refs[n_in + 2:]), name=name, grid=grid,
        in_specs=in_specs + [pl.BlockSpec(memory_space=pl.ANY)] * 2, out_specs=out_specs, out_shape=out_shape,
        input_output_aliases={n_in: 0, n_in + 1: 1}, scratch_shapes=scratch, compiler_params=_params(("arbitrary",)),
    )(*args, *fill)
    return (*res, None)


def _grad_matmul(a, b, name, init=None, tn=512, sender=None):
    rows, n = a.shape
    k = b.shape[1]
    tn = min(tn, n)
    has_init = init is not None
    init_blocks = init.shape[0] // tn if has_init else 0

    def body(*refs):
        if has_init:
            a_ref, b_ref, init_ref, o_ref = refs
        else:
            a_ref, b_ref, o_ref = refs
        g = _dot(a_ref[...], b_ref[...], "tn")
        if has_init:
            g = g + jnp.where(pl.program_id(0) < init_blocks, init_ref[...].astype(F32), 0.0)
        o_ref[...] = g.astype(o_ref.dtype)

    in_specs = [pl.BlockSpec((rows, tn), lambda i: (0, i)), _full((rows, k), single=True)]
    args = [a, b]
    if has_init:
        in_specs.append(pl.BlockSpec((tn, k), lambda i: (jnp.minimum(i, init_blocks - 1), 0)))
        args.append(init)
    (out,), handle = _host_call(
        body, name, (n // tn,), in_specs, args, [jax.ShapeDtypeStruct((n, k), PAYLOAD_DTYPE)],
        [pl.BlockSpec((tn, k), lambda i: (i, 0))], [], sender=sender)
    return out, handle


def _grad_in(d_up, d_down, dpt, h, init, sender=None):
    rows = h.shape[0]
    tn = 256
    per_group = KW // tn
    n_scan = 4 * per_group
    init_blocks = init.shape[0] // tn

    def body(up_ref, down_ref, dpt_ref, h_ref, init_ref, o_ref):
        i = pl.program_id(0)
        both = (up_ref[...].astype(F32) + down_ref[...].astype(F32)).astype(MXU_DTYPE)
        a = jnp.where(i < per_group, up_ref[...],
                      jnp.where(i < 2 * per_group, down_ref[...], jnp.where(i < n_scan, both, dpt_ref[...])))
        g = _dot(a, h_ref[...], "tn") + jnp.where(i < init_blocks, init_ref[...].astype(F32), 0.0)
        o_ref[...] = g.astype(o_ref.dtype)

    last = 3 * per_group - 1
    col = lambda f: pl.BlockSpec((rows, tn), lambda i: (0, f(i)))
    in_specs = [col(lambda i: jnp.clip(jnp.where(i < per_group, i, i - per_group), 0, last)),
                col(lambda i: jnp.clip(i - per_group, 0, last)),
                col(lambda i: jnp.clip(i - n_scan, 0, TAIL_COLS // tn - 1)),
                _full((rows, D), single=True),
                pl.BlockSpec((tn, D), lambda i: (jnp.minimum(i, init_blocks - 1), 0))]
    (out,), handle = _host_call(
        body, "gw_in", (IN_COLS // tn,), in_specs, [d_up, d_down, dpt, h, init],
        [jax.ShapeDtypeStruct((IN_COLS, D), PAYLOAD_DTYPE)], [pl.BlockSpec((tn, D), lambda i: (i, 0))], [],
        sender=sender)
    return out, handle


def _row_tile(rows, limit=256):
    if rows <= limit:
        return rows
    for t in range(limit, 7, -8):
        if rows % t == 0:
            return t
    return rows


def _sum8(stack, name):
    _, rows, cols = stack.shape
    tr = _row_tile(rows)

    def body(s_ref, o_ref):
        acc = s_ref[0].astype(F32)
        for j in range(1, N_DEV):
            acc = acc + s_ref[j].astype(F32)
        o_ref[...] = acc

    return pl.pallas_call(
        body, name=name, grid=(rows // tr,), out_shape=jax.ShapeDtypeStruct((rows, cols), F32),
        in_specs=[pl.BlockSpec((N_DEV, tr, cols), lambda i: (0, i, 0))],
        out_specs=pl.BlockSpec((tr, cols), lambda i: (i, 0)),
        compiler_params=_params(("arbitrary",)),
    )(stack)


def _adamw_update(w, gv, m, v):
    nm = ADAM_B1 * m + (1.0 - ADAM_B1) * gv
    nv = ADAM_B2 * v + (1.0 - ADAM_B2) * (gv * gv)
    m_hat = nm / (1.0 - ADAM_B1 ** ADAM_STEP)
    v_hat = nv / (1.0 - ADAM_B2 ** ADAM_STEP)
    return -ADAM_LR * (m_hat / (jnp.sqrt(v_hat) + ADAM_EPS) + ADAM_WD * w), nm, nv


SMALL_PARAMS = (("g_mix", 0, D), ("g_ffn", 1, D), ("g_final", 2, D), ("g_norm_a", 3, DK), ("ln_v_g", 4, KW),
                ("ln_v_b", 5, KW), ("b_s", 6, GROUPS * SGU_BLOCK))


def _small_finish(early, late, dws, gam, nb_ex, params):
    names = [n for n, _, _ in SMALL_PARAMS] + ["b_mod", "w_s"]

    def body(*refs):
        s_ref, l_ref, dws_ref, gam_ref = refs[:4]
        p_refs = refs[4:4 + 3 * len(names)]
        tot_ref, dgam_ref = refs[4 + 3 * len(names):6 + 3 * len(names)]
        o_refs = refs[6 + 3 * len(names):]
        acc = s_ref[0] + l_ref[0]
        gws = dws_ref[0]
        for j in range(1, N_DEV):
            acc = acc + (s_ref[j] + l_ref[j])
            gws = gws + dws_ref[j]
        tot_ref[...] = acc
        bm = acc[8:8 + N_MOD, :]
        for e in range(nb_ex):
            bm = bm + acc[16 + e * N_MOD:16 + (e + 1) * N_MOD, :]
        lb = jnp.concatenate([_lower_bound(gam_ref, 0), _lower_bound(gam_ref, 1)], axis=1)
        dgam = acc[7:8, :] * lb * (1.0 - lb)
        dgam_ref[...] = jnp.concatenate([dgam, -dgam], axis=0)
        grads = [acc[row:row + 1, 0:width] for _, row, width in SMALL_PARAMS] + [bm, gws]
        for k, g in enumerate(grads):
            w_ref, m_ref, v_ref = p_refs[3 * k:3 * k + 3]
            o_refs[4 * k][...] = g
            o_refs[4 * k + 1][...], o_refs[4 * k + 2][...], o_refs[4 * k + 3][...] = _adamw_update(
                w_ref[...], g, m_ref[...], v_ref[...])

    p_args, p_specs, o_shapes, o_specs = [], [], [], []
    for n in names:
        for a in params[n]:
            p_args.append(a)
            p_specs.append(_full(a.shape))
        o_shapes += [jax.ShapeDtypeStruct(params[n][0].shape, F32)] * 4
        o_specs += [_full(params[n][0].shape)] * 4
    res = pl.pallas_call(
        body, name="small_finish", grid=(1,),
        out_shape=[jax.ShapeDtypeStruct((SMALL_ROWS, D), F32), jax.ShapeDtypeStruct((2, D), F32)] + o_shapes,
        in_specs=[_full(early.shape), _full(late.shape), _full(dws.shape), _full((4, KW))] + p_specs,
        out_specs=[_full((SMALL_ROWS, D)), _full((2, D))] + o_specs,
        compiler_params=_params(("arbitrary",)),
    )(early, late, dws, gam, *p_args)
    return res[0], res[1], {n: res[2 + 4 * k:6 + 4 * k] for k, n in enumerate(names)}


def _adamw_sum8(stack, w, m, v, name):
    _, rows, cols = stack.shape
    tr = _row_tile(rows)

    def body(s_ref, w_ref, m_ref, v_ref, g_ref, d_ref, nm_ref, nv_ref):
        gv = s_ref[0].astype(F32)
        for j in range(1, N_DEV):
            gv = gv + s_ref[j].astype(F32)
        g_ref[...] = gv
        d_ref[...], nm_ref[...], nv_ref[...] = _adamw_update(w_ref[...], gv, m_ref[...], v_ref[...])

    blk = pl.BlockSpec((tr, cols), lambda i: (i, 0))
    sd = jax.ShapeDtypeStruct((rows, cols), F32)
    return pl.pallas_call(
        body, name=name, grid=(rows // tr,), out_shape=(sd, sd, sd, sd),
        in_specs=[pl.BlockSpec((N_DEV, tr, cols), lambda i: (0, i, 0)), blk, blk, blk], out_specs=(blk, blk, blk, blk),
        compiler_params=_params(("arbitrary",)),
    )(stack, w, m, v)


def _adamw(w, g, m, v, name):
    shape = w.shape
    cols = shape[-1]
    rows = 1
    for s in shape[:-1]:
        rows *= s
    tr = _row_tile(rows)

    def body(w_ref, g_ref, m_ref, v_ref, d_ref, nm_ref, nv_ref):
        gv = g_ref[...]
        nm = ADAM_B1 * m_ref[...] + (1.0 - ADAM_B1) * gv
        nv = ADAM_B2 * v_ref[...] + (1.0 - ADAM_B2) * (gv * gv)
        m_hat = nm / (1.0 - ADAM_B1 ** ADAM_STEP)
        v_hat = nv / (1.0 - ADAM_B2 ** ADAM_STEP)
        d_ref[...] = -ADAM_LR * (m_hat / (jnp.sqrt(v_hat) + ADAM_EPS) + ADAM_WD * w_ref[...])
        nm_ref[...] = nm
        nv_ref[...] = nv

    blk = pl.BlockSpec((tr, cols), lambda i: (i, 0))
    sd = jax.ShapeDtypeStruct((rows, cols), F32)
    d, nm, nv = pl.pallas_call(
        body, name=name, grid=(rows // tr,), out_shape=(sd, sd, sd), in_specs=[blk] * 4, out_specs=(blk, blk, blk),
        compiler_params=_params(("arbitrary",)),
    )(w.reshape(rows, cols), g.reshape(rows, cols), m.reshape(rows, cols), v.reshape(rows, cols))
    return d.reshape(shape), nm.reshape(shape), nv.reshape(shape)


def _owner_blocks(a):
    return a.reshape(N_DEV, a.shape[0] // N_DEV, a.shape[1])


class _LocalWeights:
    def __init__(self, w_upT, w_down, w_o, w_paT, w_pbT):
        self.weights = (w_upT, w_down, w_o, w_paT, w_pbT)
        self.items = {}

    def sender(self, stage, items=None):
        self.items[stage] = items
        return None

    def sent(self, stage, handle):
        pass

    def mixer_weights(self, after):
        return self.weights[1:]

    def ffn_weights(self, after):
        return self.weights[0]


def _local_step(x, ctx, target, modv, mcv, gam, g_mix, g_ffn, gna, ln_g, ln_b, w_s, b_s, g_final, w_inT, comm):
    nb_ex, seq, _ = x.shape
    ctx_len = ctx.shape[1]
    xt = x.reshape(nb_ex * seq, D)
    ct = ctx.reshape(nb_ex * ctx_len, D)
    tgt = target.reshape(nb_ex * seq, D)
    bs3 = b_s.reshape(GROUPS, SGU_BLOCK, 1)

    pc, hc, _ = _inproj(ct, mcv, g_mix, w_inT, CTX_COLS, ctx_len, "inproj_ctx")
    p, h, handle = _inproj(xt, modv, g_mix, w_inT, IN_COLS, seq, "inproj_lat", sender=comm.sender("inproj"))
    comm.sent("inproj", handle)
    cst_f, cst_b, s_ctx, _ = _hgrn_fwd(pc, gam, None, ctx_len, False, "hgrn_fwd_ctx")
    o_up, o_down, st_f, st_b, _, handle = _hgrn_fwd(p, gam, s_ctx, seq, True, "hgrn_fwd_lat",
                                                    sender=comm.sender("scan"))
    comm.sent("scan", handle)
    w_down, w_o, w_paT, w_pbT = comm.mixer_weights(o_up)
    x1, mix, merged, oa, obm = _tail_fwd(p, o_up, o_down, xt, modv, gna, ln_g, ln_b, w_s, bs3, w_paT, w_pbT, w_o, seq)
    w_upT = comm.ffn_weights(x1)
    dx1, h2, dffn, act, dup, dmod_ffn, small_ffn = _ffn(x1, tgt, modv, g_ffn, g_final, w_upT, w_down, seq)
    gw_upT, _ = _grad_matmul(dup, h2, "gw_up")
    gw_down, _ = _grad_matmul(act, dffn, "gw_down", tn=256)
    scatter = lambda *grads: [(_owner_blocks(g), "scatter") for g in grads]
    dpt, do, dmix, dpa, dpb, dmod_tail, small_tail, dws, dbs, handle = _tail_bwd(
        p, o_up, o_down, dx1, mix, modv, gna, ln_g, ln_b, w_s, bs3, w_paT, w_pbT, w_o, seq,
        sender=comm.sender("tail_bwd", scatter(gw_upT)))
    comm.sent("tail_bwd", handle)
    gw_o, _ = _grad_matmul(merged, dmix, "gw_o")
    gw_paT, _ = _grad_matmul(dpa, oa, "gw_pa")
    gw_pbT, _ = _grad_matmul(dpb, obm, "gw_pb")
    def at_row(row, a):
        return jnp.pad(a, ((row, SMALL_ROWS - row - a.shape[0]), (0, D - a.shape[1])))

    small_early = (at_row(1, small_ffn[0:2])
                   + at_row(3, small_tail[0:3])
                   + at_row(6, dbs.reshape(1, GROUPS * SGU_BLOCK))
                   + at_row(14, small_ffn[2:3]))
    dws_rows = dws.reshape(GROUPS * SGU_BLOCK, SGU_BLOCK)
    d_up, d_down, dlb, ds0, handle = _hgrn_bwd(
        p, gam, do, st_f, st_b, None, seq, True, "hgrn_bwd_lat",
        sender=comm.sender("scan_bwd", scatter(gw_down, gw_o, gw_paT, gw_pbT)
                           + [(small_early, "gather"), (dws_rows, "gather")]))
    comm.sent("scan_bwd", handle)
    c_up, c_down, dlb_c, _, _ = _hgrn_bwd(pc, gam, None, cst_f, cst_b, ds0, ctx_len, False, "hgrn_bwd_ctx")
    dpc, dmc, small_c, _ = _inproj_bwd(c_up, c_down, None, ct, None, mcv, g_mix, w_inT, ctx_len, "inproj_bwd_ctx")
    gw_inT, _ = _grad_in(d_up, d_down, dpt, h, _grad_matmul(dpc, hc, "gw_in_ctx")[0])
    n_first = max(nb_ex - 1, 1)
    grad_x, dmod_in, small_in, handle = _inproj_bwd(
        d_up, d_down, dpt, xt, dx1, modv, g_mix, w_inT, seq, "inproj_bwd_lat",
        sender=comm.sender("inproj_bwd", scatter(gw_inT)), examples=(0, n_first))
    comm.sent("inproj_bwd", handle)
    if nb_ex > n_first:
        grad_x, dmod_in, small_rest, _ = _inproj_bwd(
            d_up, d_down, dpt, xt, dx1, modv, g_mix, w_inT, seq, "inproj_bwd_rest",
            examples=(n_first, nb_ex - n_first), fill=(grad_x, dmod_in))
        small_in = small_in + small_rest
    dmod = dmod_in + dmod_tail + dmod_ffn
    small_late = (at_row(0, small_in[0:1] + small_c[0:1])
                  + at_row(7, (dlb + dlb_c).reshape(1, 2 * KW))
                  + at_row(8, dmc[0, 0:N_MOD])
                  + at_row(16, dmod[:, 0:N_MOD].reshape(nb_ex * N_MOD, D)))
    comm.sender("last", [(small_late, "gather")])
    return grad_x.reshape(x.shape)


def kernel(x, c, ctx, c_ctx, w_mod, b_mod, g_mix, g_ffn, w_in, lb_gamma, g_norm_a, ln_v_g, ln_v_b, w_s, b_s, w_pa, w_pb, w_o, w_up, w_down, g_final, loss_target, m_c_ctx, m_w_mod, m_b_mod, m_g_mix, m_g_ffn, m_w_in, m_lb_gamma, m_g_norm_a, m_ln_v_g, m_ln_v_b, m_w_s, m_b_s, m_w_pa, m_w_pb, m_w_o, m_w_up, m_w_down, m_g_final, v_c_ctx, v_w_mod, v_b_mod, v_g_mix, v_g_ffn, v_w_in, v_lb_gamma, v_g_norm_a, v_ln_v_g, v_ln_v_b, v_w_s, v_b_s, v_w_pa, v_w_pb, v_w_o, v_w_up, v_w_down, v_g_final):
    nb_ex = x.shape[0]
    me = 4 * lax.axis_index("x") + 2 * lax.axis_index("y") + lax.axis_index("c")
    cd = MXU_DTYPE
    mod_cols = w_mod.shape[2]
    lb_cols = lb_gamma.shape[2]

    w_inT_l = w_in[0].T.astype(cd)
    w_upT_l = w_up[0].T.astype(cd)
    w_paT_l = w_pa[0].T.astype(cd)
    w_pbT_l = w_pb[0].T.astype(cd)
    cl = jnp.concatenate([c, jnp.pad(lb_gamma.reshape(1, 4 * lb_cols), ((0, 0), (0, D - 4 * lb_cols))),
                          jnp.zeros((8 - nb_ex - 1, D), F32)], axis=0)
    g_in, g_cl = _gather_two_level([w_inT_l, cl], "gather_w_in")
    w_inT = g_in.reshape(IN_COLS, D)
    c_all = g_cl[:, 0:nb_ex].reshape(N_DEV * nb_ex, D)
    gam = jnp.transpose(g_cl[:, nb_ex, 0:4 * lb_cols].reshape(N_DEV, 4, lb_cols), (1, 0, 2)).reshape(4, KW)

    n_c = N_DEV * nb_ex
    cvec = jnp.concatenate([c_all, c_ctx.reshape(1, D), jnp.zeros((7, D), F32)], axis=0)
    b_mod_l = lax.dynamic_slice(b_mod, (0, me * mod_cols), (1, mod_cols))
    mod_l, svec = _mod_fwd(cvec, w_mod[0], b_mod_l)
    (g_mod,) = _gather_two_level([mod_l], "gather_mod")
    mod_all = jnp.transpose(g_mod, (1, 0, 2)).reshape(n_c + 8, N_MOD * D)
    modv = lax.dynamic_slice(mod_all, (me * nb_ex, 0), (nb_ex, N_MOD * D)).reshape(nb_ex, N_MOD, D)
    mcv = mod_all[n_c].reshape(1, N_MOD, D)

    handles, leftover = {}, {}

    class Comm:
        def sender(self, stage, items=None):
            if stage == "inproj":
                return _Sender([(w_down[0].astype(cd), "gather"), (w_o[0].astype(cd), "gather"), (w_paT_l, "gather"),
                                (w_pbT_l, "gather")])
            if stage == "scan":
                return _Sender([(w_upT_l, "gather")])
            if stage == "last":
                leftover["items"] = items
                return None
            return _Sender(items)

        def sent(self, stage, handle):
            handles[stage] = handle

        def mixer_weights(self, after):
            g_down, g_o, g_pa, g_pb = _exchange_wait(handles["inproj"], after)
            return g_down.reshape(D_FF, D), g_o.reshape(D, D), g_pa.reshape(D, KW), g_pb.reshape(D, KW)

        def ffn_weights(self, after):
            (g_up,) = _exchange_wait(handles["scan"], after)
            return g_up.reshape(2 * D_FF, D)

    grad_x = _local_step(
        x, ctx, loss_target, modv, mcv, gam, g_mix, g_ffn, g_norm_a, ln_v_g, ln_v_b, w_s[0], b_s[0],
        g_final.reshape(1, D), w_inT, Comm())
    last, last_started = _exchange_start(leftover["items"], "gather_small_late", after=leftover["items"][0][0])

    (r_up,) = _exchange_wait(handles["tail_bwd"], last_started)
    r_down, r_o, r_pa, r_pb, r_small, r_dws = _exchange_wait(handles["scan_bwd"], r_up)
    raw_up = _adamw_sum8(r_up, w_up[0].T, m_w_up[0].T, v_w_up[0].T, "adamw_w_up")
    raw_down = _adamw_sum8(r_down, w_down[0], m_w_down[0], v_w_down[0], "adamw_w_down")
    raw_o = _adamw_sum8(r_o, w_o[0], m_w_o[0], v_w_o[0], "adamw_w_o")
    (r_in,) = _exchange_wait(handles["inproj_bwd"], raw_up[1])
    raw_in = _adamw_sum8(r_in, w_in[0].T, m_w_in[0].T, v_w_in[0].T, "adamw_w_in")
    (r_late,) = _exchange_wait(last, raw_in[1])
    done = {"w_in": [a.T[None] for a in raw_in], "w_up": [a.T[None] for a in raw_up],
            "w_down": [a[None] for a in raw_down], "w_o": [a[None] for a in raw_o]}
    grad_w_in, grad_w_up, grad_w_down, grad_w_o = (done[k][0] for k in ("w_in", "w_up", "w_down", "w_o"))
    grad_w_pa = _sum8(r_pa, "sum_w_pa").T[None]
    grad_w_pb = _sum8(r_pb, "sum_w_pb").T[None]
    as_2d = {"g_final": (1, D), "b_s": (1, GROUPS * SGU_BLOCK), "b_mod": (N_MOD, D), "w_s": (GROUPS * SGU_BLOCK, SGU_BLOCK)}
    small_params = {"g_mix": (g_mix, m_g_mix, v_g_mix), "g_ffn": (g_ffn, m_g_ffn, v_g_ffn),
                    "g_final": (g_final, m_g_final, v_g_final), "g_norm_a": (g_norm_a, m_g_norm_a, v_g_norm_a),
                    "ln_v_g": (ln_v_g, m_ln_v_g, v_ln_v_g), "ln_v_b": (ln_v_b, m_ln_v_b, v_ln_v_b),
                    "b_s": (b_s, m_b_s, v_b_s), "b_mod": (b_mod, m_b_mod, v_b_mod), "w_s": (w_s, m_w_s, v_w_s)}
    tot, dgam, small_done = _small_finish(
        r_small, r_late, r_dws, gam, nb_ex,
        {n: tuple(a.reshape(as_2d.get(n, a.shape)) for a in wmv) for n, wmv in small_params.items()})
    for n, outs in small_done.items():
        done[n] = [a.reshape(small_params[n][0].shape) for a in outs]
    loss = tot[14, 0]
    grad_g_mix, grad_g_ffn, grad_g_final, grad_g_norm_a, grad_ln_v_g, grad_ln_v_b, grad_b_s, grad_b_mod, grad_w_s = (
        done[n][0] for n in ("g_mix", "g_ffn", "g_final", "g_norm_a", "ln_v_g", "ln_v_b", "b_s", "b_mod", "w_s"))
    grad_lb_gamma = lax.dynamic_slice(dgam.reshape(2, 2, KW), (0, 0, me * lb_cols), (2, 2, lb_cols))

    dmod_all = r_late[:, 16:16 + nb_ex * N_MOD].reshape(n_c, N_MOD * D)
    dmod_l = jnp.concatenate([lax.dynamic_slice(dmod_all, (0, me * mod_cols), (n_c, mod_cols)),
                              lax.dynamic_slice(tot[8:8 + N_MOD].reshape(1, N_MOD * D), (0, me * mod_cols), (1, mod_cols)),
                              jnp.zeros((7, mod_cols), F32)], axis=0)
    gw_mod, gc = _mod_bwd(svec, cvec, dmod_l, w_mod[0])
    grad_w_mod = gw_mod[None]
    (r_gc,) = _exchange([(gc[n_c:n_c + 8], "gather")], "gather_c_ctx", after=r_late)
    grad_c_ctx = _sum8(r_gc, "sum_c_ctx")[0]

    names = ["c_ctx", "w_mod", "b_mod", "g_mix", "g_ffn", "w_in", "lb_gamma", "g_norm_a", "ln_v_g", "ln_v_b", "w_s",
             "b_s", "w_pa", "w_pb", "w_o", "w_up", "w_down", "g_final"]
    weights = [c_ctx, w_mod, b_mod, g_mix, g_ffn, w_in, lb_gamma, g_norm_a, ln_v_g, ln_v_b, w_s, b_s, w_pa, w_pb, w_o,
               w_up, w_down, g_final]
    grads = [grad_c_ctx, grad_w_mod, grad_b_mod, grad_g_mix, grad_g_ffn, grad_w_in, grad_lb_gamma, grad_g_norm_a,
             grad_ln_v_g, grad_ln_v_b, grad_w_s, grad_b_s, grad_w_pa, grad_w_pb, grad_w_o, grad_w_up, grad_w_down,
             grad_g_final]
    ms = [m_c_ctx, m_w_mod, m_b_mod, m_g_mix, m_g_ffn, m_w_in, m_lb_gamma, m_g_norm_a, m_ln_v_g, m_ln_v_b, m_w_s, m_b_s,
          m_w_pa, m_w_pb, m_w_o, m_w_up, m_w_down, m_g_final]
    vs = [v_c_ctx, v_w_mod, v_b_mod, v_g_mix, v_g_ffn, v_w_in, v_lb_gamma, v_g_norm_a, v_ln_v_g, v_ln_v_b, v_w_s, v_b_s,
          v_w_pa, v_w_pb, v_w_o, v_w_up, v_w_down, v_g_final]
    deltas, new_ms, new_vs = [], [], []
    for nm, w, g, m, v in zip(names, weights, grads, ms, vs):
        d, nm_, nv_ = done[nm][1:] if nm in done else _adamw(w, g.reshape(w.shape), m, v, "adamw_" + nm)
        deltas.append(d)
        new_ms.append(nm_)
        new_vs.append(nv_)
    grads = [g.reshape(w.shape) for g, w in zip(grads, weights)]
    return (loss, grad_x, *grads, *deltas, *new_ms, *new_vs)
```

```python
import functools

import jax
import jax.numpy as jnp
from jax import lax
from jax.experimental import pallas as pl
from jax.experimental.pallas import tpu as pltpu

F32 = jnp.float32
MXU_DTYPE = jnp.bfloat16
PAYLOAD_DTYPE = jnp.bfloat16

N_DEV = 8
D = 1024
HEADS = 4
DK = 128
KW = HEADS * DK
CHUNK = 64
SGU_BLOCK = 128
GROUPS = 4
D_FF = 2816
FF_CHUNK = 256
N_MOD = 6
IN_COLS = 5632
CTX_COLS = 1536
TAIL_COLS = IN_COLS - 4 * KW
EPS = 1e-6
ADAM_LR, ADAM_B1, ADAM_B2, ADAM_EPS, ADAM_WD, ADAM_STEP = 0.001, 0.9, 0.999, 1e-08, 0.01, 10

VMEM_LIMIT = 56 * 1024 * 1024
TOKEN_TILE = 256
PROJ_TILE = 512
TAIL_TILE = 512
SMALL_ROWS = 40


def _params(sem):
    return pltpu.CompilerParams(dimension_semantics=sem, vmem_limit_bytes=VMEM_LIMIT)


_DN = {"nn": (((1,), (0,)), ((), ())), "nt": (((1,), (1,)), ((), ())), "tn": (((0,), (0,)), ((), ()))}


def _dot(a, b, form="nn"):
    return lax.dot_general(a.astype(MXU_DTYPE), b.astype(MXU_DTYPE), _DN[form], preferred_element_type=F32)


def _mask_dot(mask, v):
    bf = jnp.bfloat16
    hi = v.astype(bf)
    r1 = v - hi.astype(F32)
    mid = r1.astype(bf)
    lo = (r1 - mid.astype(F32)).astype(bf)
    w = v.shape[1]
    s = lax.dot_general(mask.astype(bf), jnp.concatenate([hi, mid, lo], axis=1), _DN["nn"], preferred_element_type=F32)
    return (s[:, 2 * w:] + s[:, w:2 * w]) + s[:, :w]


def _full(shape, single=False):
    n = len(shape)
    if single:
        return pl.BlockSpec(shape, lambda *_: (0,) * n, pipeline_mode=pl.Buffered(1))
    return pl.BlockSpec(shape, lambda *_: (0,) * n)


def _ordered_behind(body, in_specs, args, after):
    if after is None:
        return body
    at = len(in_specs)
    in_specs.append(pl.BlockSpec(memory_space=pl.ANY))
    args.append(after)
    return lambda *refs: body(*refs[:at], *refs[at + 1:])


def _sigmoid(z):
    return 0.5 * jnp.tanh(0.5 * z) + 0.5


def _gelu(x):
    c = 0.7978845608028654
    t = jnp.tanh(c * (x + 0.044715 * x * x * x))
    return 0.5 * x * (1.0 + t), t


def _gelu_grad(x, t):
    c = 0.7978845608028654
    return 0.5 * (1.0 + t) + 0.5 * x * (1.0 - t * t) * c * (1.0 + 3 * 0.044715 * x * x)


def _exchange(items, name, after=None):
    n = len(items)
    out_shape = []
    for a, mode in items:
        blk = a.shape if mode == "gather" else a.shape[1:]
        out_shape.append(jax.ShapeDtypeStruct((N_DEV,) + tuple(blk), a.dtype))

    def body(*refs):
        srcs, dsts = refs[:n], refs[n:2 * n]
        send_sems, recv_sems, local_sems = refs[2 * n:]
        x, y, c = lax.axis_index("x"), lax.axis_index("y"), lax.axis_index("c")
        me = 4 * x + 2 * y + c

        def src_for(i, dev):
            return srcs[i] if items[i][1] == "gather" else srcs[i].at[dev]

        local = [pltpu.make_async_copy(src_for(i, me), dsts[i].at[me], local_sems.at[i]) for i in range(n)]
        for cp in local:
            cp.start()
        remote = []
        for k in range(1, N_DEV):
            px = jnp.bitwise_xor(x, (k >> 2) & 1)
            py = jnp.bitwise_xor(y, (k >> 1) & 1)
            pc = jnp.bitwise_xor(c, k & 1)
            peer = 4 * px + 2 * py + pc
            for i in range(n):
                cp = pltpu.make_async_remote_copy(
                    src_ref=src_for(i, peer), dst_ref=dsts[i].at[me],
                    send_sem=send_sems.at[i * (N_DEV - 1) + k - 1], recv_sem=recv_sems.at[i * (N_DEV - 1) + k - 1],
                    device_id=(px, py, pc), device_id_type=pl.DeviceIdType.MESH)
                cp.start()
                remote.append(cp)
        for cp in remote:
            cp.wait()
        for cp in local:
            cp.wait()

    any_spec = pl.BlockSpec(memory_space=pl.ANY)
    in_specs, args = [any_spec] * n, [a for a, _ in items]
    if after is not None:
        in_specs.append(any_spec)
        args.append(after)
        exchange = body
        body = lambda *refs: exchange(*refs[:n], *refs[n + 1:])
    return pl.pallas_call(
        body, name=name, out_shape=out_shape, in_specs=in_specs, out_specs=[any_spec] * n,
        scratch_shapes=[pltpu.SemaphoreType.DMA((n * (N_DEV - 1),)), pltpu.SemaphoreType.DMA((n * (N_DEV - 1),)),
                        pltpu.SemaphoreType.DMA((n,))],
    )(*args)


def _gather_two_level(arrays, name):
    n = len(arrays)
    pieces = []
    for i, a in enumerate(arrays):
        rows = _Sender.PIECE_ROWS if a.shape[0] % _Sender.PIECE_ROWS == 0 else a.shape[0]
        pieces += [(i, r0, rows) for r0 in range(0, a.shape[0], rows)]

    def body(*refs):
        srcs, dsts = refs[:n], refs[n:2 * n]
        send_sems, recv_sems, local_sems = refs[2 * n:]
        x, y, c = lax.axis_index("x"), lax.axis_index("y"), lax.axis_index("c")
        me, sibling = (x, y, c), (x, y, 1 - c)
        x_nbr, y_nbr, diag = (1 - x, y, c), (x, 1 - y, c), (1 - x, 1 - y, c)

        def slot(px, py, pc):
            return 4 * px + 2 * py + pc

        def copy(u, k, block, to, own=False):
            i, r0, rows = pieces[u]
            there = dsts[i].at[slot(*block)].at[pl.ds(r0, rows)]
            return pltpu.make_async_remote_copy(
                src_ref=srcs[i].at[pl.ds(r0, rows)] if own else there, dst_ref=there,
                send_sem=send_sems.at[u * 7 + k], recv_sem=recv_sems.at[u * 7 + k],
                device_id=to, device_id_type=pl.DeviceIdType.MESH)

        units = range(len(pieces))
        mine = [pltpu.make_async_copy(srcs[i], dsts[i].at[slot(*me)], local_sems.at[i]) for i in range(n)]
        for cp in mine:
            cp.start()
        for u in units:
            copy(u, 1, me, x_nbr, own=True).start()
            copy(u, 2, me, y_nbr, own=True).start()
        for u in units:
            copy(u, 0, me, sibling, own=True).start()

        def relay_then_pass(k_from, frm, to, k_other, other):
            for u in units:
                copy(u, k_from, frm, me).wait_recv()
                copy(u, 3, frm, to).start()
                copy(u, 3 + k_from, frm, sibling).start()
            for u in units:
                copy(u, k_other, other, me).wait_recv()
                copy(u, 3 + k_other, other, sibling).start()

        @pl.when(c == 1)
        def _():
            relay_then_pass(1, x_nbr, y_nbr, 2, y_nbr)

        @pl.when(c == 0)
        def _():
            relay_then_pass(2, y_nbr, x_nbr, 1, x_nbr)

        for u in units:
            copy(u, 3, diag, me).wait_recv()
            copy(u, 6, diag, sibling).start()
        for u in units:
            copy(u, 0, sibling, me).wait_recv()
            for k, chip in ((4, x_nbr), (5, y_nbr), (6, diag)):
                copy(u, k, (chip[0], chip[1], 1 - c), me).wait_recv()
        for u in units:
            for k in range(7):
                copy(u, k, me, me, own=True).wait_send()
        for cp in mine:
            cp.wait()

    any_spec = pl.BlockSpec(memory_space=pl.ANY)
    return pl.pallas_call(
        body, name=name, out_shape=[jax.ShapeDtypeStruct((N_DEV,) + a.shape, a.dtype) for a in arrays],
        in_specs=[any_spec] * n, out_specs=[any_spec] * n,
        scratch_shapes=[pltpu.SemaphoreType.DMA((len(pieces) * 7,)), pltpu.SemaphoreType.DMA((len(pieces) * 7,)),
                        pltpu.SemaphoreType.DMA((n,))],
    )(*arrays)


_HBM = pl.BlockSpec(memory_space=pltpu.HBM)
_SEM = pl.BlockSpec(memory_space=pltpu.SEMAPHORE)
_EFFECT = pltpu.SideEffectType.DATAFLOW_SIDE_EFFECTING


def _split_copies(items, srcs, lands, send_sems, recv_sems):
    x, y, c = lax.axis_index("x"), lax.axis_index("y"), lax.axis_index("c")
    me = 4 * x + 2 * y + c
    copies = []
    for k in range(1, N_DEV):
        px = jnp.bitwise_xor(x, (k >> 2) & 1)
        py = jnp.bitwise_xor(y, (k >> 1) & 1)
        pc = jnp.bitwise_xor(c, k & 1)
        peer = 4 * px + 2 * py + pc
        for i in range(len(items)):
            src = srcs[i] if items[i][1] == "gather" else srcs[i].at[peer]
            copies.append(pltpu.make_async_remote_copy(
                src_ref=src, dst_ref=lands[i].at[me],
                send_sem=send_sems.at[i * (N_DEV - 1) + k - 1], recv_sem=recv_sems.at[i * (N_DEV - 1) + k - 1],
                device_id=(px, py, pc), device_id_type=pl.DeviceIdType.MESH))
    return me, copies


def _exchange_start(items, name, after):
    n = len(items)
    n_sem = n * (N_DEV - 1)
    srcs, lands = [], []
    for a, mode in items:
        blk = a.shape if mode == "gather" else a.shape[1:]
        srcs.append(pltpu.with_memory_space_constraint(a, pltpu.HBM))
        lands.append(pltpu.with_memory_space_constraint(lax.empty((N_DEV,) + tuple(blk), a.dtype), pltpu.HBM))

    def body(*refs):
        src_refs, land_refs = refs[:n], refs[n:2 * n]
        send_sems, recv_sems = refs[2 * n + 1], refs[2 * n + 2]
        local_sems = refs[4 * n + 3]
        me, copies = _split_copies(items, src_refs, land_refs, send_sems, recv_sems)
        for i in range(n):
            own = src_refs[i] if items[i][1] == "gather" else src_refs[i].at[me]
            cp = pltpu.make_async_copy(own, land_refs[i].at[me], local_sems.at[i])
            cp.start()
            cp.wait()
        for cp in copies:
            cp.start()

    out_shape = [pltpu.SemaphoreType.DMA((n_sem,)), pltpu.SemaphoreType.DMA((n_sem,))]
    out_shape += [pltpu.HBM(a.shape, a.dtype) for a in srcs] + [pltpu.HBM(a.shape, a.dtype) for a in lands]
    outs = pl.pallas_call(
        body, name=name, out_shape=out_shape,
        in_specs=[_HBM] * (2 * n) + [pl.BlockSpec(memory_space=pl.ANY)],
        out_specs=[_SEM, _SEM] + [_HBM] * (2 * n),
        input_output_aliases={i: 2 + i for i in range(2 * n)},
        scratch_shapes=[pltpu.SemaphoreType.DMA((n,))],
        compiler_params=pltpu.CompilerParams(has_side_effects=_EFFECT),
    )(*srcs, *lands, after)
    handle = (items, name, outs[0], outs[1], outs[2:2 + n], outs[2 + n:2 + 2 * n])
    return handle, outs[2]


class _Sender:
    PIECE_ROWS = 176

    def __init__(self, items, chunks=None):
        self.items, self.n = items, len(items)
        self.chunks = chunks
        if chunks is None:
            block_rows = [a.shape[0] if mode == "gather" else a.shape[1] for a, mode in items]
            self.chunks = [r // self.PIECE_ROWS if r % self.PIECE_ROWS == 0 else 1 for r in block_rows]
        self.srcs, self.lands = [], []
        for a, mode in items:
            blk = a.shape if mode == "gather" else a.shape[1:]
            self.srcs.append(pltpu.with_memory_space_constraint(a, pltpu.HBM))
            self.lands.append(pltpu.with_memory_space_constraint(lax.empty((N_DEV,) + tuple(blk), a.dtype), pltpu.HBM))

    def issue(self, src_refs, land_refs, send_sems, recv_sems, local_sems, step, n_steps):
        x, y, c = lax.axis_index("x"), lax.axis_index("y"), lax.axis_index("c")
        me = 4 * x + 2 * y + c
        copies = []
        for ch in range(max(self.chunks)):
            for k in range(1, N_DEV):
                px = jnp.bitwise_xor(x, (k >> 2) & 1)
                py = jnp.bitwise_xor(y, (k >> 1) & 1)
                pc = jnp.bitwise_xor(c, k & 1)
                peer = 4 * px + 2 * py + pc
                for i, (_, mode) in enumerate(self.items):
                    if ch >= self.chunks[i]:
                        continue
                    n_rows = land_refs[i].shape[1] // self.chunks[i]
                    rows = pl.ds(ch * n_rows, n_rows)
                    src = src_refs[i].at[rows] if mode == "gather" else src_refs[i].at[peer].at[rows]
                    copies.append(pltpu.make_async_remote_copy(
                        src_ref=src, dst_ref=land_refs[i].at[me].at[rows],
                        send_sem=send_sems.at[i * (N_DEV - 1) + k - 1], recv_sem=recv_sems.at[i * (N_DEV - 1) + k - 1],
                        device_id=(px, py, pc), device_id_type=pl.DeviceIdType.MESH))
        own = [pltpu.make_async_copy(src_refs[i] if mode == "gather" else src_refs[i].at[me], land_refs[i].at[me],
                                     local_sems.at[i]) for i, (_, mode) in enumerate(self.items)]

        @pl.when(step == 0)
        def _():
            for cp in own:
                cp.start()

        for s in range(n_steps):
            group = [cp for j, cp in enumerate(copies) if (j * n_steps) // len(copies) == s]
            if group:
                @pl.when(step == s)
                def _(group=group):
                    for cp in group:
                        cp.start()

        @pl.when(step == n_steps - 1)
        def _():
            for cp in own:
                cp.wait()


def _host_call(body, name, grid, in_specs, args, out_shape, out_specs, scratch_shapes, after=None, sender=None):
    in_specs, args, out_shape, out_specs = list(in_specs), list(args), list(out_shape), list(out_specs)
    scratch_shapes = list(scratch_shapes)
    semantics = ("arbitrary",) * len(grid)
    body = _ordered_behind(body, in_specs, args, after)
    if sender is None:
        res = pl.pallas_call(body, name=name, grid=grid, in_specs=in_specs, out_specs=out_specs, out_shape=out_shape,
                             scratch_shapes=scratch_shapes, compiler_params=_params(semantics))(*args)
        return res, None
    n, n_in, n_out, n_scr = sender.n, len(in_specs), len(out_shape), len(scratch_shapes)
    n_sem = n * (N_DEV - 1)
    n_steps = 1
    for g in grid:
        n_steps *= g
    compute = body

    def body(*refs):
        ins, s_in = refs[:n_in], refs[n_in:n_in + 2 * n]
        o0 = n_in + 2 * n
        outs, s_out = refs[o0:o0 + n_out], refs[o0 + n_out:o0 + n_out + 2 + 2 * n]
        scr = refs[o0 + n_out + 2 + 2 * n:]
        compute(*ins, *outs, *scr[:n_scr])
        step = pl.program_id(0)
        for d in range(1, len(grid)):
            step = step * grid[d] + pl.program_id(d)
        sender.issue(s_in[:n], s_in[n:], s_out[0], s_out[1], scr[n_scr], step, n_steps)

    res = pl.pallas_call(
        body, name=name, grid=grid,
        in_specs=in_specs + [_HBM] * (2 * n), out_specs=out_specs + [_SEM, _SEM] + [_HBM] * (2 * n),
        out_shape=out_shape + [pltpu.SemaphoreType.DMA((n_sem,)), pltpu.SemaphoreType.DMA((n_sem,))]
        + [pltpu.HBM(a.shape, a.dtype) for a in sender.srcs] + [pltpu.HBM(a.shape, a.dtype) for a in sender.lands],
        input_output_aliases={n_in + j: n_out + 2 + j for j in range(2 * n)},
        scratch_shapes=scratch_shapes + [pltpu.SemaphoreType.DMA((n,))],
        compiler_params=pltpu.CompilerParams(dimension_semantics=semantics, vmem_limit_bytes=VMEM_LIMIT,
                                             has_side_effects=_EFFECT),
    )(*args, *sender.srcs, *sender.lands)
    handle = (sender.items, name, res[n_out], res[n_out + 1], res[n_out + 2:n_out + 2 + n],
              res[n_out + 2 + n:n_out + 2 + 2 * n])
    return res[:n_out], handle


def _exchange_wait(handle, after):
    items, name, send_sems, recv_sems, srcs, lands = handle
    n = len(items)

    def body(*refs):
        src_refs, land_refs = refs[:n], refs[n:2 * n]
        send_ref, recv_ref = refs[2 * n], refs[2 * n + 1]
        _, copies = _split_copies(items, src_refs, land_refs, send_ref, recv_ref)
        for cp in copies:
            cp.wait_send()
            cp.wait_recv()

    outs = pl.pallas_call(
        body, name=name + "_wait",
        out_shape=[pltpu.HBM(a.shape, a.dtype) for a in srcs] + [pltpu.HBM(a.shape, a.dtype) for a in lands],
        in_specs=[_HBM] * (2 * n) + [_SEM, _SEM, pl.BlockSpec(memory_space=pl.ANY)], out_specs=[_HBM] * (2 * n),
        input_output_aliases={i: i for i in range(2 * n)},
        compiler_params=pltpu.CompilerParams(has_side_effects=_EFFECT),
    )(*srcs, *lands, send_sems, recv_sems, after)
    return outs[n:]


def _mod_fwd(cvec, w_mod_l, b_mod_l):
    rows, cols = cvec.shape[0], w_mod_l.shape[1]

    def body(c_ref, w_ref, b_ref, o_ref, s_ref):
        cv = c_ref[...]
        s = cv * _sigmoid(cv)
        s_ref[...] = s
        o_ref[...] = _dot(s, w_ref[...]) + b_ref[...]

    return pl.pallas_call(
        body, name="mod_fwd",
        out_shape=(jax.ShapeDtypeStruct((rows, cols), F32), jax.ShapeDtypeStruct((rows, D), F32)),
        in_specs=[_full((rows, D)), _full((D, cols)), _full((1, cols))],
        out_specs=(_full((rows, cols)), _full((rows, D))), grid=(1,),
        compiler_params=_params(("arbitrary",)),
    )(cvec, w_mod_l, b_mod_l)


def _mod_bwd(svec, cvec, dmod_l, w_mod_l):
    rows, cols = dmod_l.shape

    def body(s_ref, c_ref, d_ref, w_ref, gw_ref, gc_ref):
        gw_ref[...] = _dot(s_ref[...], d_ref[...], "tn")
        cv = c_ref[...]
        sg = _sigmoid(cv)
        gc_ref[...] = _dot(d_ref[...], w_ref[...], "nt") * (sg * (1.0 + cv * (1.0 - sg)))

    return pl.pallas_call(
        body, name="mod_bwd",
        out_shape=(jax.ShapeDtypeStruct((D, cols), F32), jax.ShapeDtypeStruct((rows, D), F32)),
        in_specs=[_full((rows, D)), _full((rows, D)), _full((rows, cols)), _full((D, cols))],
        out_specs=(_full((D, cols)), _full((rows, D))), grid=(1,),
        compiler_params=_params(("arbitrary",)),
    )(svec, cvec, dmod_l, w_mod_l)


def _inproj(xt, modv, g, w_inT, n_cols, rows_per_example, name, after=None, sender=None):
    rows = xt.shape[0]
    tm = min(PROJ_TILE, rows_per_example)
    per_b = rows_per_example // tm
    shared_mod = modv.shape[0] == 1

    def body(x_ref, mod_ref, g_ref, w_ref, p_ref, h_ref):
        x = x_ref[...]
        r = lax.rsqrt(jnp.mean(x * x, axis=-1, keepdims=True) + EPS)
        h = (x * r * g_ref[...]) * (1.0 + mod_ref[0, 1:2, :]) + mod_ref[0, 0:1, :]
        hb = h.astype(MXU_DTYPE)
        h_ref[...] = hb
        for j in range(n_cols // KW):
            p_ref[:, j * KW:(j + 1) * KW] = _dot(hb, w_ref[j * KW:(j + 1) * KW, :], "nt").astype(p_ref.dtype)

    mod_idx = (lambda i: (0, 0, 0)) if shared_mod else (lambda i: (i // per_b, 0, 0))
    in_specs = [pl.BlockSpec((tm, D), lambda i: (i, 0)), pl.BlockSpec((1, N_MOD, D), mod_idx), _full((1, D)),
                pl.BlockSpec((n_cols, D), lambda i: (0, 0), pipeline_mode=pl.Buffered(1))]
    (p, h), handle = _host_call(
        body, name, (rows // tm,), in_specs, [xt, modv, g, w_inT],
        [jax.ShapeDtypeStruct((rows, n_cols), MXU_DTYPE), jax.ShapeDtypeStruct((rows, D), MXU_DTYPE)],
        [pl.BlockSpec((tm, n_cols), lambda i: (i, 0)), pl.BlockSpec((tm, D), lambda i: (i, 0))], [],
        after=after, sender=sender)
    return p, h, handle


def _tri(reverse, n):
    row = lax.broadcasted_iota(jnp.int32, (n, n), 0)
    col = lax.broadcasted_iota(jnp.int32, (n, n), 1)
    same = (row // CHUNK) == (col // CHUNK)
    return same & ((col >= row) if reverse else (col <= row))


def _per_chunk_rows(x, reverse):
    n = x.shape[0]
    rows = [x[j * CHUNK:j * CHUNK + 1] if reverse else x[(j + 1) * CHUNK - 1:(j + 1) * CHUNK] for j in range(n // CHUNK)]
    return jnp.concatenate([jnp.broadcast_to(r, (CHUNK, x.shape[1])) for r in rows], axis=0), rows


def _lower_bound(gam_ref, direction):
    return _sigmoid(gam_ref[direction:direction + 1, :] - gam_ref[2 + direction:3 + direction, :])


def _gate_prep(z, lb, tri, reverse):
    sg = _sigmoid(z)
    f = lb + (1.0 - lb) * sg
    g = jnp.log(f)
    b = _mask_dot(tri, g)
    bl, bl_rows = _per_chunk_rows(b, reverse)
    mid = 0.5 * bl
    return sg, g, 1.0 - f, b, jnp.exp(mid), [jnp.exp(0.5 * r) for r in bl_rows], jnp.exp(mid - b), mid


def _hgrn_fwd(p, gam, s0, rows_per_example, with_out, name, sender=None):
    rows = p.shape[0]
    nb_ex = rows // rows_per_example
    rb = min(TOKEN_TILE, rows_per_example)
    cpb = rb // CHUNK
    nb = rows_per_example // rb
    n_chunks = rows // CHUNK
    has_s0 = s0 is not None

    def body(*refs):
        it = iter(refs)
        gam_ref = next(it)
        zf_ref, vf_ref = next(it), next(it)
        qf_ref = next(it) if with_out else None
        zb_ref, vb_ref = next(it), next(it)
        qb_ref = next(it) if with_out else None
        s0_ref = next(it) if has_s0 else None
        if with_out:
            of_ref, ob_ref = next(it), next(it)
        stash_f, stash_b, fin_ref = next(it), next(it), next(it)
        st_ref = next(it)
        i = pl.program_id(1)

        @pl.when(i == 0)
        def _():
            if has_s0:
                st_ref[...] = s0_ref[:, 0]
            else:
                st_ref[...] = jnp.zeros_like(st_ref)

        for direction, (z_ref, v_ref, q_ref, stash) in enumerate(
                ((zf_ref, vf_ref, qf_ref, stash_f), (zb_ref, vb_ref, qb_ref, stash_b))):
            reverse = direction == 1
            tri = _tri(reverse, rb)
            lb = _lower_bound(gam_ref, direction)
            z = z_ref[...].astype(F32)
            v = v_ref[...].astype(F32)
            _, _, k, b, em, em_rows, e2, mid = _gate_prep(z, lb, tri, reverse)
            kd = (k * (e2 * em)).astype(MXU_DTYPE)
            vb = v.astype(MXU_DTYPE)
            if with_out:
                q = q_ref[...].astype(F32)
                qi = q * jnp.exp(b - mid)
                qe = (qi * em).astype(MXU_DTYPE)
                qi = qi.astype(MXU_DTYPE)
                ki = (k * e2).astype(MXU_DTYPE)
                intra = []
                for h in range(HEADS):
                    hs = slice(h * DK, (h + 1) * DK)
                    sc = jnp.where(tri, _dot(qi[:, hs], ki[:, hs], "nt"), 0.0)
                    intra.append(_dot(sc, vb[:, hs]))
            for j in (range(cpb - 1, -1, -1) if reverse else range(cpb)):
                rs = slice(j * CHUNK, (j + 1) * CHUNK)
                a = em_rows[j] * em_rows[j]
                for h in range(HEADS):
                    hs = slice(h * DK, (h + 1) * DK)
                    st = st_ref[direction, h]
                    stash[j, h] = st.astype(stash.dtype)
                    if with_out:
                        (ob_ref if reverse else of_ref)[rs, hs] = intra[h][rs] + _dot(qe[rs, hs], st, "nt")
                    st_ref[direction, h] = st * a[:, hs] + _dot(vb[rs, hs], kd[rs, hs], "tn")

        @pl.when(i == nb - 1)
        def _():
            fin_ref[:, 0] = st_ref[...]

    up = lambda b, i: b * nb + i
    down = lambda b, i: b * nb + nb - 1 - i
    col = lambda rowf, c: pl.BlockSpec((rb, KW), lambda b, i: (rowf(b, i), c))
    in_specs = [_full((4, KW)), col(up, 0), col(up, 2)] + ([col(up, 3)] if with_out else [])
    in_specs += [col(down, 1), col(down, 2)] + ([col(down, 3)] if with_out else [])
    args = [gam, p, p] + ([p] if with_out else []) + [p, p] + ([p] if with_out else [])
    if has_s0:
        in_specs.append(pl.BlockSpec((2, 1, HEADS, DK, DK), lambda b, i: (0, b, 0, 0, 0)))
        args.append(s0)
    out_shape, out_specs = [], []
    if with_out:
        out_shape += [jax.ShapeDtypeStruct((rows, KW), F32)] * 2
        out_specs += [pl.BlockSpec((rb, KW), lambda b, i: (up(b, i), 0)),
                      pl.BlockSpec((rb, KW), lambda b, i: (down(b, i), 0))]
    out_shape += [jax.ShapeDtypeStruct((n_chunks, HEADS, DK, DK), MXU_DTYPE)] * 2
    out_specs += [pl.BlockSpec((cpb, HEADS, DK, DK), lambda b, i: (up(b, i), 0, 0, 0)),
                  pl.BlockSpec((cpb, HEADS, DK, DK), lambda b, i: (down(b, i), 0, 0, 0))]
    out_shape.append(jax.ShapeDtypeStruct((2, nb_ex, HEADS, DK, DK), F32))
    out_specs.append(pl.BlockSpec((2, 1, HEADS, DK, DK), lambda b, i: (0, b, 0, 0, 0)))
    res, handle = _host_call(body, name, (nb_ex, nb), in_specs, args, out_shape, out_specs,
                             [pltpu.VMEM((2, HEADS, DK, DK), F32)], sender=sender)
    return (*res, handle)


def _hgrn_bwd(p, gam, do, stash_f, stash_b, ds_end, rows_per_example, with_out, name, after=None, sender=None):
    rows = p.shape[0]
    nb_ex = rows // rows_per_example
    rb = min(TOKEN_TILE, rows_per_example)
    cpb = rb // CHUNK
    nb = rows_per_example // rb
    has_end = ds_end is not None

    def body(*refs):
        it = iter(refs)
        gam_ref = next(it)
        ins = []
        for _ in range(2):
            z_ref, v_ref = next(it), next(it)
            q_ref = next(it) if with_out else None
            do_ref = next(it) if with_out else None
            ins.append((z_ref, v_ref, q_ref, do_ref, next(it)))
        end_ref = next(it) if has_end else None
        outs = [next(it), next(it)]
        dlb_ref, ds0_ref = next(it), next(it)
        dst_ref = next(it)
        b_id, i = pl.program_id(0), pl.program_id(1)

        @pl.when(i == 0)
        def _():
            if has_end:
                dst_ref[...] = end_ref[:, 0]
            else:
                dst_ref[...] = jnp.zeros_like(dst_ref)

        @pl.when((i == 0) & (b_id == 0))
        def _():
            dlb_ref[...] = jnp.zeros_like(dlb_ref)

        for direction in range(2):
            z_ref, v_ref, q_ref, do_ref, stash = ins[direction]
            dgrp_ref = outs[direction]
            reverse = direction == 1
            tri = _tri(reverse, rb)
            tri_t = _tri(not reverse, rb)
            lb = _lower_bound(gam_ref, direction)
            heads = [slice(h * DK, (h + 1) * DK) for h in range(HEADS)]
            chunks = [slice(j * CHUNK, (j + 1) * CHUNK) for j in range(cpb)]
            grid_cat = lambda parts: jnp.concatenate([jnp.concatenate(row, axis=1) for row in parts], axis=0)
            cat = lambda parts: jnp.concatenate(parts, axis=1)
            z = z_ref[...].astype(F32)
            sg, g, k, b, em, em_rows, e2, mid = _gate_prep(z, lb, tri, reverse)
            e3 = e2 * em
            kd = k * e3
            kd_b = kd.astype(MXU_DTYPE)
            vb = v_ref[...].astype(MXU_DTYPE)
            if with_out:
                q = q_ref[...].astype(F32)
                dout = do_ref[...].astype(MXU_DTYPE)
                e1 = jnp.exp(b - mid)
                e4 = e1 * em
                qi, ki, qe = q * e1, k * e2, q * e4
                qi_b, ki_b, qe_b = qi.astype(MXU_DTYPE), ki.astype(MXU_DTYPE), qe.astype(MXU_DTYPE)
                dqi_p, dki_p, dv_p = [], [], []
                for hs in heads:
                    sc = jnp.where(tri, _dot(qi_b[:, hs], ki_b[:, hs], "nt"), 0.0)
                    dsc = jnp.where(tri, _dot(dout[:, hs], vb[:, hs], "nt"), 0.0)
                    dqi_p.append(_dot(dsc, ki_b[:, hs]))
                    dki_p.append(_dot(dsc, qi_b[:, hs], "tn"))
                    dv_p.append(_dot(sc, dout[:, hs], "tn"))
                dqi, dki, dv = cat(dqi_p), cat(dki_p), cat(dv_p)
                dqe = grid_cat([[_dot(dout[rs, hs], stash[j, h]) for h, hs in enumerate(heads)]
                                for j, rs in enumerate(chunks)])
                grow = [[_dot(dout[rs, hs], qe_b[rs, hs], "tn") for hs in heads] for rs in chunks]
            dkd_p = [[None] * HEADS for _ in range(cpb)]
            dvs_p = [[None] * HEADS for _ in range(cpb)]
            da_p = [[None] * HEADS for _ in range(cpb)]
            for j in (range(cpb) if reverse else range(cpb - 1, -1, -1)):
                rs = chunks[j]
                a = em_rows[j] * em_rows[j]
                for h, hs in enumerate(heads):
                    dst = dst_ref[direction, h]
                    dkd_p[j][h] = _dot(vb[rs, hs], dst)
                    dvs_p[j][h] = _dot(kd_b[rs, hs], dst, "nt")
                    da_p[j][h] = jnp.broadcast_to(
                        jnp.sum(dst * stash[j, h].astype(F32), axis=0, keepdims=True), (CHUNK, DK))
                    new_dst = dst * a[:, hs]
                    dst_ref[direction, h] = new_dst + grow[j][h] if with_out else new_dst
            dkd, dvs, da = grid_cat(dkd_p), grid_cat(dvs_p), grid_cat(da_p)
            t_kd = dkd * kd
            dk = dkd * e3
            db = -t_kd
            tot = t_kd
            if with_out:
                dgrp_ref[:, KW:2 * KW] = (dvs + dv).astype(dgrp_ref.dtype)
                dgrp_ref[:, 2 * KW:] = (dqi * e1 + dqe * e4).astype(dgrp_ref.dtype)
                dk = dk + dki * e2
                t_qi, t_ki, t_qe = dqi * qi, dki * ki, dqe * qe
                db = db + t_qi - t_ki + t_qe
                tot = tot + 0.5 * (t_ki - t_qi)
            else:
                dgrp_ref[:, KW:2 * KW] = dvs.astype(dgrp_ref.dtype)
            dbl = jnp.concatenate([jnp.broadcast_to(jnp.sum(tot[rs], axis=0, keepdims=True), (CHUNK, KW))
                                   for rs in chunks], axis=0) + da * (em * em)
            dg = _mask_dot(tri_t, db) + dbl
            df = dg * jnp.exp(-g) - dk
            dgrp_ref[:, 0:KW] = (df * (1.0 - lb) * sg * (1.0 - sg)).astype(dgrp_ref.dtype)
            dlb_ref[direction:direction + 1, :] += jnp.sum(df * (1.0 - sg), axis=0, keepdims=True)

        @pl.when(i == nb - 1)
        def _():
            ds0_ref[:, 0] = dst_ref[...]

    rows_of = (lambda b, i: b * nb + nb - 1 - i, lambda b, i: b * nb + i)
    in_specs, args = [_full((4, KW))], [gam]
    for direction in range(2):
        rf = rows_of[direction]
        col = lambda c, rf=rf: pl.BlockSpec((rb, KW), lambda b, i: (rf(b, i), c))
        in_specs += [col(direction), col(2)]
        args += [p, p]
        if with_out:
            in_specs += [col(3), col(0)]
            args += [p, do]
        in_specs.append(pl.BlockSpec((cpb, HEADS, DK, DK), lambda b, i, rf=rf: (rf(b, i), 0, 0, 0)))
        args.append((stash_f, stash_b)[direction])
    if has_end:
        in_specs.append(pl.BlockSpec((2, 1, HEADS, DK, DK), lambda b, i: (0, b, 0, 0, 0)))
        args.append(ds_end)
    out_shape, out_specs = [], []
    for direction in range(2):
        rf = rows_of[direction]
        width = (3 if with_out else 2) * KW
        out_shape.append(jax.ShapeDtypeStruct((rows, width), MXU_DTYPE))
        out_specs.append(pl.BlockSpec((rb, width), lambda b, i, rf=rf: (rf(b, i), 0)))
    out_shape += [jax.ShapeDtypeStruct((2, KW), F32), jax.ShapeDtypeStruct((2, nb_ex, HEADS, DK, DK), F32)]
    out_specs += [_full((2, KW)), pl.BlockSpec((2, 1, HEADS, DK, DK), lambda b, i: (0, b, 0, 0, 0))]
    res, handle = _host_call(body, name, (nb_ex, nb), in_specs, args, out_shape, out_specs,
                             [pltpu.VMEM((2, HEADS, DK, DK), F32)], after=after, sender=sender)
    return (*res, handle)


def _tail_forward(osum, og, u, v, ga, gb, gna, ln_g, ln_b, ws_ref, bs_ref, wpaT_ref, wpbT_ref):
    tm = osum.shape[0]
    gna4 = jnp.concatenate([gna] * HEADS, axis=1)
    r_parts = []
    for h in range(HEADS):
        oh = osum[:, h * DK:(h + 1) * DK]
        r_parts.append(jnp.broadcast_to(lax.rsqrt(jnp.mean(oh * oh, axis=-1, keepdims=True) + EPS), (tm, DK)))
    r = jnp.concatenate(r_parts, axis=1)
    on = osum * r
    sg_og = _sigmoid(og)
    silu_og = og * sg_og
    oan = on * gna4
    oa = oan * silu_og
    ug, tu = _gelu(u)
    vg, tv = _gelu(v)
    mu = jnp.mean(vg, axis=-1, keepdims=True)
    vc = vg - mu
    rstd = lax.rsqrt(jnp.mean(vc * vc, axis=-1, keepdims=True) + EPS)
    vhat = vc * rstd
    vln = vhat * ln_g + ln_b
    blocks = []
    for n in range(tm // SGU_BLOCK):
        rs = slice(n * SGU_BLOCK, (n + 1) * SGU_BLOCK)
        blocks.append(jnp.concatenate(
            [_dot(ws_ref[g], vln[rs, g * DK:(g + 1) * DK]) + bs_ref[g] for g in range(GROUPS)], axis=1))
    mixed = jnp.concatenate(blocks, axis=0) if len(blocks) > 1 else blocks[0]
    obm = ug * mixed
    pa = _dot(oa, wpaT_ref[...], "nt")
    pb = _dot(obm, wpbT_ref[...], "nt")
    sga, sgb = _sigmoid(ga), _sigmoid(gb)
    merged = sga * pa + sgb * pb
    return dict(r=r, on=on, sg_og=sg_og, silu_og=silu_og, oan=oan, oa=oa, ug=ug, tu=tu, tv=tv, rstd=rstd, vhat=vhat,
                vln=vln, mixed=mixed, obm=obm, pa=pa, pb=pb, sga=sga, sgb=sgb, merged=merged, gna4=gna4)


def _tail_in_specs(tm):
    tile = lambda c: pl.BlockSpec((tm, KW), lambda i: (i, c))
    return [tile(c) for c in range(4, 11)]


def _tail_weight_specs():
    return [_full((1, DK)), _full((1, KW)), _full((1, KW)), _full((GROUPS, SGU_BLOCK, SGU_BLOCK)),
            _full((GROUPS, SGU_BLOCK, 1)), _full((D, KW), single=True), _full((D, KW), single=True),
            _full((D, D), single=True)]


def _read_tail_inputs(of_ref, ob_ref, pcols):
    osum = of_ref[...] + ob_ref[...]
    og, u, v = (pcols[j][...].astype(F32) for j in range(3))
    ga = jnp.concatenate([pcols[3][...], pcols[4][...]], axis=1).astype(F32)
    gb = jnp.concatenate([pcols[5][...], pcols[6][...]], axis=1).astype(F32)
    return osum, og, u, v, ga, gb


def _tail_fwd(p, o_up, o_down, xt, modv, gna, ln_g, ln_b, w_s, b_s, w_paT, w_pbT, w_o, rows_per_example):
    rows = xt.shape[0]
    tm = min(TAIL_TILE, rows_per_example)
    per_b = rows_per_example // tm

    def body(of_ref, ob_ref, *rest):
        pcols = rest[:7]
        (x_ref, mod_ref, gna_ref, lng_ref, lnb_ref, ws_ref, bs_ref, wpaT_ref, wpbT_ref, wo_ref,
         x1_ref, mix_ref, merged_ref, oa_ref, obm_ref) = rest[7:]
        t = _tail_forward(*_read_tail_inputs(of_ref, ob_ref, pcols), gna_ref[...], lng_ref[...], lnb_ref[...],
                          ws_ref, bs_ref, wpaT_ref, wpbT_ref)
        mix = _dot(t["merged"], wo_ref[...])
        x1_ref[...] = x_ref[...] + mod_ref[0, 2:3, :] * mix
        mix_ref[...] = mix.astype(mix_ref.dtype)
        merged_ref[...] = t["merged"].astype(merged_ref.dtype)
        oa_ref[...] = t["oa"].astype(oa_ref.dtype)
        obm_ref[...] = t["obm"].astype(obm_ref.dtype)

    row = lambda w: pl.BlockSpec((tm, w), lambda i: (i, 0))
    in_specs = [row(KW), row(KW)] + _tail_in_specs(tm) + [row(D), pl.BlockSpec((1, N_MOD, D), lambda i: (i // per_b, 0, 0))]
    in_specs += _tail_weight_specs()
    return pl.pallas_call(
        body, name="tail_fwd", grid=(rows // tm,),
        out_shape=(jax.ShapeDtypeStruct((rows, D), F32), jax.ShapeDtypeStruct((rows, D), MXU_DTYPE),
                   jax.ShapeDtypeStruct((rows, D), MXU_DTYPE), jax.ShapeDtypeStruct((rows, KW), MXU_DTYPE),
                   jax.ShapeDtypeStruct((rows, KW), MXU_DTYPE)),
        in_specs=in_specs, out_specs=(row(D), row(D), row(D), row(KW), row(KW)),
        compiler_params=_params(("arbitrary",)),
    )(o_up, o_down, *([p] * 7), xt, modv, gna, ln_g, ln_b, w_s, b_s, w_paT, w_pbT, w_o)


def _tail_bwd(p, o_up, o_down, dx1, mix, modv, gna, ln_g, ln_b, w_s, b_s, w_paT, w_pbT, w_o, rows_per_example,
              after=None, sender=None):
    rows = dx1.shape[0]
    nb_ex = rows // rows_per_example
    tm = min(TAIL_TILE, rows_per_example)
    per_b = rows_per_example // tm

    def body(of_ref, ob_ref, *rest):
        pcols = rest[:7]
        (dx1_ref, mix_ref, mod_ref, gna_ref, lng_ref, lnb_ref, ws_ref, bs_ref, wpaT_ref, wpbT_ref, wo_ref,
         dpt_ref, do_ref, dmix_ref, dpa_ref, dpb_ref, dmod_ref, small_ref, dws_ref, dbs_ref) = rest[7:]
        i = pl.program_id(0)

        @pl.when(i == 0)
        def _():
            small_ref[...] = jnp.zeros_like(small_ref)
            dws_ref[...] = jnp.zeros_like(dws_ref)
            dbs_ref[...] = jnp.zeros_like(dbs_ref)

        @pl.when(i % per_b == 0)
        def _():
            dmod_ref[...] = jnp.zeros_like(dmod_ref)

        osum, og, u, v, ga, gb = _read_tail_inputs(of_ref, ob_ref, pcols)
        ln_g = lng_ref[...]
        t = _tail_forward(osum, og, u, v, ga, gb, gna_ref[...], ln_g, lnb_ref[...], ws_ref, bs_ref, wpaT_ref, wpbT_ref)
        dx1v = dx1_ref[...]
        dmod_ref[0, 2:3, :] += jnp.sum(dx1v * mix_ref[...].astype(F32), axis=0, keepdims=True)
        dmix = dx1v * mod_ref[0, 2:3, :]
        dmix_ref[...] = dmix.astype(dmix_ref.dtype)
        dmerged = _dot(dmix, wo_ref[...], "nt")
        sga, sgb = t["sga"], t["sgb"]
        dpa = dmerged * sga
        dpb = dmerged * sgb
        dpa_ref[...] = dpa.astype(dpa_ref.dtype)
        dpb_ref[...] = dpb.astype(dpb_ref.dtype)
        dga = dmerged * t["pa"] * sga * (1.0 - sga)
        dgb = dmerged * t["pb"] * sgb * (1.0 - sgb)
        doa = _dot(dpa, wpaT_ref[...])
        dobm = _dot(dpb, wpbT_ref[...])
        dug = dobm * t["mixed"]
        dmixed = dobm * t["ug"]
        du = dug * _gelu_grad(u, t["tu"])
        dvln_blocks = []
        for n in range(tm // SGU_BLOCK):
            rs = slice(n * SGU_BLOCK, (n + 1) * SGU_BLOCK)
            parts = []
            for g in range(GROUPS):
                gs = slice(g * DK, (g + 1) * DK)
                dm = dmixed[rs, gs]
                parts.append(_dot(ws_ref[g], dm, "tn"))
                dws_ref[g] += _dot(dm, t["vln"][rs, gs], "nt")
                dbs_ref[g] += jnp.sum(dm, axis=1, keepdims=True)
            dvln_blocks.append(jnp.concatenate(parts, axis=1))
        dvln = jnp.concatenate(dvln_blocks, axis=0) if len(dvln_blocks) > 1 else dvln_blocks[0]
        vhat = t["vhat"]
        small_ref[1:2, 0:KW] += jnp.sum(dvln * vhat, axis=0, keepdims=True)
        small_ref[2:3, 0:KW] += jnp.sum(dvln, axis=0, keepdims=True)
        dvhat = dvln * ln_g
        dvg = t["rstd"] * (dvhat - jnp.mean(dvhat, axis=-1, keepdims=True)
                           - vhat * jnp.mean(dvhat * vhat, axis=-1, keepdims=True))
        dv = dvg * _gelu_grad(v, t["tv"])
        sg_og = t["sg_og"]
        doan = doa * t["silu_og"]
        dog = doa * t["oan"] * (sg_og * (1.0 + og * (1.0 - sg_og)))
        prod = doan * t["on"]
        dgna = jnp.zeros((1, DK), F32)
        for h in range(HEADS):
            dgna = dgna + jnp.sum(prod[:, h * DK:(h + 1) * DK], axis=0, keepdims=True)
        small_ref[0:1, 0:DK] += dgna
        don = doan * t["gna4"]
        dot_parts = []
        for h in range(HEADS):
            hs = slice(h * DK, (h + 1) * DK)
            m = jnp.mean(don[:, hs] * t["on"][:, hs], axis=-1, keepdims=True)
            dot_parts.append(t["r"][:, hs] * (don[:, hs] - t["on"][:, hs] * m))
        do_ref[...] = jnp.concatenate(dot_parts, axis=1).astype(do_ref.dtype)
        for j, val in enumerate((dog, du, dv)):
            dpt_ref[:, j * KW:(j + 1) * KW] = val.astype(dpt_ref.dtype)
        dpt_ref[:, 3 * KW:3 * KW + D] = dga.astype(dpt_ref.dtype)
        dpt_ref[:, 3 * KW + D:] = dgb.astype(dpt_ref.dtype)

    row = lambda w: pl.BlockSpec((tm, w), lambda i: (i, 0))
    in_specs = [row(KW), row(KW)] + _tail_in_specs(tm) + [row(D), row(D), pl.BlockSpec((1, N_MOD, D), lambda i: (i // per_b, 0, 0))]
    in_specs += _tail_weight_specs()
    args = [o_up, o_down, *([p] * 7), dx1, mix, modv, gna, ln_g, ln_b, w_s, b_s, w_paT, w_pbT, w_o]
    cd = MXU_DTYPE
    res, handle = _host_call(
        body, "tail_bwd", (rows // tm,), in_specs, args,
        [jax.ShapeDtypeStruct((rows, TAIL_COLS), cd), jax.ShapeDtypeStruct((rows, KW), cd),
         jax.ShapeDtypeStruct((rows, D), cd), jax.ShapeDtypeStruct((rows, D), cd),
         jax.ShapeDtypeStruct((rows, D), cd), jax.ShapeDtypeStruct((nb_ex, 8, D), F32),
         jax.ShapeDtypeStruct((8, D), F32), jax.ShapeDtypeStruct((GROUPS, SGU_BLOCK, SGU_BLOCK), F32),
         jax.ShapeDtypeStruct((GROUPS, SGU_BLOCK, 1), F32)],
        [row(TAIL_COLS), row(KW), row(D), row(D), row(D),
         pl.BlockSpec((1, 8, D), lambda i: (i // per_b, 0, 0)), _full((8, D)),
         _full((GROUPS, SGU_BLOCK, SGU_BLOCK)), _full((GROUPS, SGU_BLOCK, 1))], [],
        after=after, sender=sender)
    return (*res, handle)


def _ffn(x1, target, modv, g_ffn, g_final, w_upT, w_down, rows_per_example):
    rows = x1.shape[0]
    nb_ex = rows // rows_per_example
    tm = min(TOKEN_TILE, rows_per_example)
    per_b = rows_per_example // tm
    n_ff = D_FF // FF_CHUNK

    def body(x1_ref, tgt_ref, mod_ref, gffn_ref, gfin_ref, wup_ref, wdn_ref,
             dx1_ref, h2_ref, dffn_ref, act_ref, dup_ref, dmod_ref, small_ref, up_scr):
        i = pl.program_id(0)

        @pl.when(i == 0)
        def _():
            small_ref[...] = jnp.zeros_like(small_ref)

        @pl.when(i % per_b == 0)
        def _():
            dmod_ref[...] = jnp.zeros_like(dmod_ref)

        x1v = x1_ref[...]
        g2 = gffn_ref[...]
        m3, m4, m5 = mod_ref[0, 3:4, :], mod_ref[0, 4:5, :], mod_ref[0, 5:6, :]
        r2 = lax.rsqrt(jnp.mean(x1v * x1v, axis=-1, keepdims=True) + EPS)
        xn2 = x1v * r2
        h2 = (xn2 * g2) * (1.0 + m4) + m3
        h2b = h2.astype(MXU_DTYPE)
        h2_ref[...] = h2b
        def up_pair(j):
            lo = j * FF_CHUNK
            return (_dot(h2b, wup_ref[lo:lo + FF_CHUNK, :], "nt"),
                    _dot(h2b, wup_ref[D_FF + lo:D_FF + lo + FF_CHUNK, :], "nt"))

        group_end = {min(e, n_ff): s for s, e in ((0, 4), (4, 8), (8, 12))}
        cur, ffn = up_pair(0), None
        for j in range(n_ff):
            nxt = up_pair(j + 1) if j + 1 < n_ff else None
            cs = slice(j * FF_CHUNK, (j + 1) * FF_CHUNK)
            a, bgate = cur
            up_scr[:, cs] = a
            up_scr[:, D_FF + j * FF_CHUNK:D_FF + (j + 1) * FF_CHUNK] = bgate
            act_ref[:, cs] = (a * _sigmoid(a) * bgate).astype(MXU_DTYPE)
            cur = nxt
            if j + 1 in group_end:
                gs = slice(group_end[j + 1] * FF_CHUNK, (j + 1) * FF_CHUNK)
                part = _dot(act_ref[:, gs], wdn_ref[gs, :])
                ffn = part if ffn is None else ffn + part
        x2 = x1v + m5 * ffn
        r3 = lax.rsqrt(jnp.mean(x2 * x2, axis=-1, keepdims=True) + EPS)
        xn3 = x2 * r3
        gf = gfin_ref[...]
        err = xn3 * gf - tgt_ref[...]
        loss = 0.5 * jnp.sum(jnp.mean(err * err, axis=-1, keepdims=True), axis=0, keepdims=True)
        small_ref[2:3, :] += jnp.broadcast_to(loss, (1, D))
        dy = err * (1.0 / D)
        small_ref[1:2, :] += jnp.sum(dy * xn3, axis=0, keepdims=True)
        dxn3 = dy * gf
        dx2 = r3 * (dxn3 - xn3 * jnp.mean(dxn3 * xn3, axis=-1, keepdims=True))
        dmod_ref[0, 5:6, :] += jnp.sum(dx2 * ffn, axis=0, keepdims=True)
        dffn = (dx2 * m5).astype(MXU_DTYPE)
        dffn_ref[...] = dffn
        dact_of = lambda j: _dot(dffn, wdn_ref[j * FF_CHUNK:(j + 1) * FF_CHUNK, :], "nt")
        cur, dh2 = dact_of(0), None
        for j in range(n_ff):
            nxt = dact_of(j + 1) if j + 1 < n_ff else None
            cs = slice(j * FF_CHUNK, (j + 1) * FF_CHUNK)
            a, bgate = up_scr[:, cs], up_scr[:, D_FF + j * FF_CHUNK:D_FF + (j + 1) * FF_CHUNK]
            s = _sigmoid(a)
            dup_ref[:, cs] = (cur * bgate * (s * (1.0 + a * (1.0 - s)))).astype(MXU_DTYPE)
            dup_ref[:, D_FF + j * FF_CHUNK:D_FF + (j + 1) * FF_CHUNK] = (cur * a * s).astype(MXU_DTYPE)
            cur = nxt
            if j + 1 in group_end:
                lo, hi = group_end[j + 1] * FF_CHUNK, (j + 1) * FF_CHUNK
                part = (_dot(dup_ref[:, lo:hi], wup_ref[lo:hi, :])
                        + _dot(dup_ref[:, D_FF + lo:D_FF + hi], wup_ref[D_FF + lo:D_FF + hi, :]))
                dh2 = part if dh2 is None else dh2 + part
        dmod_ref[0, 3:4, :] += jnp.sum(dh2, axis=0, keepdims=True)
        dmod_ref[0, 4:5, :] += jnp.sum(dh2 * xn2 * g2, axis=0, keepdims=True)
        small_ref[0:1, :] += jnp.sum(dh2 * (1.0 + m4) * xn2, axis=0, keepdims=True)
        dxn2 = dh2 * g2 * (1.0 + m4)
        dx1_ref[...] = dx2 + r2 * (dxn2 - xn2 * jnp.mean(dxn2 * xn2, axis=-1, keepdims=True))

    row = lambda w: pl.BlockSpec((tm, w), lambda i: (i, 0))
    cd = MXU_DTYPE
    return pl.pallas_call(
        body, name="ffn_fwd_bwd", grid=(rows // tm,),
        out_shape=(jax.ShapeDtypeStruct((rows, D), F32), jax.ShapeDtypeStruct((rows, D), cd),
                   jax.ShapeDtypeStruct((rows, D), cd), jax.ShapeDtypeStruct((rows, D_FF), cd),
                   jax.ShapeDtypeStruct((rows, 2 * D_FF), cd), jax.ShapeDtypeStruct((nb_ex, 8, D), F32),
                   jax.ShapeDtypeStruct((8, D), F32)),
        in_specs=[row(D), row(D), pl.BlockSpec((1, N_MOD, D), lambda i: (i // per_b, 0, 0)), _full((1, D)), _full((1, D)),
                  _full((2 * D_FF, D), single=True), _full((D_FF, D), single=True)],
        out_specs=(row(D), row(D), row(D), row(D_FF), row(2 * D_FF),
                   pl.BlockSpec((1, 8, D), lambda i: (i // per_b, 0, 0)), _full((8, D))),
        scratch_shapes=[pltpu.VMEM((tm, 2 * D_FF), F32)],
        compiler_params=_params(("arbitrary",)),
    )(x1, target, modv, g_ffn, g_final, w_upT, w_down)


def _scan_columns(up, down, n_groups):
    cols = [up[:, 0:KW].astype(F32), down[:, 0:KW].astype(F32)]
    for j in range(1, n_groups):
        cols.append(up[:, j * KW:(j + 1) * KW].astype(F32) + down[:, j * KW:(j + 1) * KW].astype(F32))
    return cols


def _inproj_bwd(d_up, d_down, dpt, xt, dx1, modv, g, w_inT, rows_per_example, name, sender=None):
    rows = xt.shape[0]
    latent = dx1 is not None
    n_cols = IN_COLS if latent else CTX_COLS
    n_groups = d_up.shape[1] // KW
    tm = min(PROJ_TILE, rows_per_example)
    per_b = rows_per_example // tm
    n_mod_blocks = rows // rows_per_example if latent else 1

    def body(*refs):
        it = iter(refs)
        up_ref, down_ref = next(it), next(it)
        dpt_ref = next(it) if latent else None
        x_ref = next(it)
        dx1_ref = next(it) if latent else None
        mod_ref, g_ref, w_ref = next(it), next(it), next(it)
        gx_ref = next(it) if latent else None
        dp_out = None if latent else next(it)
        dmod_ref, small_ref = next(it), next(it)
        dp_ref = next(it) if latent else dp_out
        i = pl.program_id(0)

        @pl.when(i == 0)
        def _():
            small_ref[...] = jnp.zeros_like(small_ref)

        @pl.when((i % per_b == 0) if latent else (i == 0))
        def _():
            dmod_ref[...] = jnp.zeros_like(dmod_ref)

        for j, val in enumerate(_scan_columns(up_ref[...], down_ref[...], n_groups)):
            dp_ref[:, j * KW:(j + 1) * KW] = val.astype(MXU_DTYPE)
        if latent:
            dh = _dot(dp_ref[...], w_ref[0:4 * KW, :]) + _dot(dpt_ref[...], w_ref[4 * KW:, :])
        else:
            dh = _dot(dp_ref[...], w_ref[...])
        x = x_ref[...]
        gv = g_ref[...]
        m1 = mod_ref[0, 1:2, :]
        r = lax.rsqrt(jnp.mean(x * x, axis=-1, keepdims=True) + EPS)
        xn = x * r
        dmod_ref[0, 0:1, :] += jnp.sum(dh, axis=0, keepdims=True)
        dmod_ref[0, 1:2, :] += jnp.sum(dh * xn * gv, axis=0, keepdims=True)
        small_ref[0:1, :] += jnp.sum(dh * (1.0 + m1) * xn, axis=0, keepdims=True)
        if latent:
            dxn = dh * gv * (1.0 + m1)
            gx_ref[...] = dx1_ref[...] + r * (dxn - xn * jnp.mean(dxn * xn, axis=-1, keepdims=True))

    row = lambda w: pl.BlockSpec((tm, w), lambda i: (i, 0))
    mod_idx = (lambda i: (i // per_b, 0, 0)) if latent else (lambda i: (0, 0, 0))
    in_specs = [row(n_groups * KW)] * 2 + ([row(TAIL_COLS)] if latent else []) + [row(D)] + ([row(D)] if latent else [])
    in_specs += [pl.BlockSpec((1, N_MOD, D), mod_idx), _full((1, D)),
                 pl.BlockSpec((n_cols, D), lambda i: (0, 0), pipeline_mode=pl.Buffered(1))]
    args = [d_up, d_down] + ([dpt] if latent else []) + [xt] + ([dx1] if latent else []) + [modv, g, w_inT]
    first = jax.ShapeDtypeStruct((rows, D), F32) if latent else jax.ShapeDtypeStruct((rows, n_cols), MXU_DTYPE)
    out_shape = [first, jax.ShapeDtypeStruct((n_mod_blocks, 8, D), F32), jax.ShapeDtypeStruct((8, D), F32)]
    out_specs = [row(D) if latent else row(n_cols), pl.BlockSpec((1, 8, D), mod_idx), _full((8, D))]
    scratch = [pltpu.VMEM((tm, 4 * KW), MXU_DTYPE)] if latent else []
    res, handle = _host_call(body, name, (rows // tm,), in_specs, args, out_shape, out_specs, scratch, sender=sender)
    return (*res, handle)


def _grad_matmul(a, b, name, init=None, tn=512, sender=None):
    rows, n = a.shape
    k = b.shape[1]
    tn = min(tn, n)
    has_init = init is not None
    init_blocks = init.shape[0] // tn if has_init else 0

    def body(*refs):
        if has_init:
            a_ref, b_ref, init_ref, o_ref = refs
        else:
            a_ref, b_ref, o_ref = refs
        g = _dot(a_ref[...], b_ref[...], "tn")
        if has_init:
            g = g + jnp.where(pl.program_id(0) < init_blocks, init_ref[...].astype(F32), 0.0)
        o_ref[...] = g.astype(o_ref.dtype)

    in_specs = [pl.BlockSpec((rows, tn), lambda i: (0, i)), _full((rows, k), single=True)]
    args = [a, b]
    if has_init:
        in_specs.append(pl.BlockSpec((tn, k), lambda i: (jnp.minimum(i, init_blocks - 1), 0)))
        args.append(init)
    (out,), handle = _host_call(
        body, name, (n // tn,), in_specs, args, [jax.ShapeDtypeStruct((n, k), PAYLOAD_DTYPE)],
        [pl.BlockSpec((tn, k), lambda i: (i, 0))], [], sender=sender)
    return out, handle


def _grad_in(d_up, d_down, dpt, h, init, sender=None):
    rows = h.shape[0]
    tn = 256
    per_group = KW // tn
    n_scan = 4 * per_group
    init_blocks = init.shape[0] // tn

    def body(up_ref, down_ref, dpt_ref, h_ref, init_ref, o_ref):
        i = pl.program_id(0)
        both = (up_ref[...].astype(F32) + down_ref[...].astype(F32)).astype(MXU_DTYPE)
        a = jnp.where(i < per_group, up_ref[...],
                      jnp.where(i < 2 * per_group, down_ref[...], jnp.where(i < n_scan, both, dpt_ref[...])))
        g = _dot(a, h_ref[...], "tn") + jnp.where(i < init_blocks, init_ref[...].astype(F32), 0.0)
        o_ref[...] = g.astype(o_ref.dtype)

    last = 3 * per_group - 1
    col = lambda f: pl.BlockSpec((rows, tn), lambda i: (0, f(i)))
    in_specs = [col(lambda i: jnp.clip(jnp.where(i < per_group, i, i - per_group), 0, last)),
                col(lambda i: jnp.clip(i - per_group, 0, last)),
                col(lambda i: jnp.clip(i - n_scan, 0, TAIL_COLS // tn - 1)),
                _full((rows, D), single=True),
                pl.BlockSpec((tn, D), lambda i: (jnp.minimum(i, init_blocks - 1), 0))]
    (out,), handle = _host_call(
        body, "gw_in", (IN_COLS // tn,), in_specs, [d_up, d_down, dpt, h, init],
        [jax.ShapeDtypeStruct((IN_COLS, D), PAYLOAD_DTYPE)], [pl.BlockSpec((tn, D), lambda i: (i, 0))], [],
        sender=sender)
    return out, handle


def _row_tile(rows, limit=256):
    if rows <= limit:
        return rows
    for t in range(limit, 7, -8):
        if rows % t == 0:
            return t
    return rows


def _sum8(stack, name):
    _, rows, cols = stack.shape
    tr = _row_tile(rows)

    def body(s_ref, o_ref):
        acc = s_ref[0].astype(F32)
        for j in range(1, N_DEV):
            acc = acc + s_ref[j].astype(F32)
        o_ref[...] = acc

    return pl.pallas_call(
        body, name=name, grid=(rows // tr,), out_shape=jax.ShapeDtypeStruct((rows, cols), F32),
        in_specs=[pl.BlockSpec((N_DEV, tr, cols), lambda i: (0, i, 0))],
        out_specs=pl.BlockSpec((tr, cols), lambda i: (i, 0)),
        compiler_params=_params(("arbitrary",)),
    )(stack)


def _adamw_update(w, gv, m, v):
    nm = ADAM_B1 * m + (1.0 - ADAM_B1) * gv
    nv = ADAM_B2 * v + (1.0 - ADAM_B2) * (gv * gv)
    m_hat = nm / (1.0 - ADAM_B1 ** ADAM_STEP)
    v_hat = nv / (1.0 - ADAM_B2 ** ADAM_STEP)
    return -ADAM_LR * (m_hat / (jnp.sqrt(v_hat) + ADAM_EPS) + ADAM_WD * w), nm, nv


SMALL_PARAMS = (("g_mix", 0, D), ("g_ffn", 1, D), ("g_final", 2, D), ("g_norm_a", 3, DK), ("ln_v_g", 4, KW),
                ("ln_v_b", 5, KW), ("b_s", 6, GROUPS * SGU_BLOCK))


def _small_finish(early, late, dws, gam, nb_ex, params):
    names = [n for n, _, _ in SMALL_PARAMS] + ["b_mod", "w_s"]

    def body(*refs):
        s_ref, l_ref, dws_ref, gam_ref = refs[:4]
        p_refs = refs[4:4 + 3 * len(names)]
        tot_ref, dgam_ref = refs[4 + 3 * len(names):6 + 3 * len(names)]
        o_refs = refs[6 + 3 * len(names):]
        acc = s_ref[0] + l_ref[0]
        gws = dws_ref[0]
        for j in range(1, N_DEV):
            acc = acc + (s_ref[j] + l_ref[j])
            gws = gws + dws_ref[j]
        tot_ref[...] = acc
        bm = acc[8:8 + N_MOD, :]
        for e in range(nb_ex):
            bm = bm + acc[16 + e * N_MOD:16 + (e + 1) * N_MOD, :]
        lb = jnp.concatenate([_lower_bound(gam_ref, 0), _lower_bound(gam_ref, 1)], axis=1)
        dgam = acc[7:8, :] * lb * (1.0 - lb)
        dgam_ref[...] = jnp.concatenate([dgam, -dgam], axis=0)
        grads = [acc[row:row + 1, 0:width] for _, row, width in SMALL_PARAMS] + [bm, gws]
        for k, g in enumerate(grads):
            w_ref, m_ref, v_ref = p_refs[3 * k:3 * k + 3]
            o_refs[4 * k][...] = g
            o_refs[4 * k + 1][...], o_refs[4 * k + 2][...], o_refs[4 * k + 3][...] = _adamw_update(
                w_ref[...], g, m_ref[...], v_ref[...])

    p_args, p_specs, o_shapes, o_specs = [], [], [], []
    for n in names:
        for a in params[n]:
            p_args.append(a)
            p_specs.append(_full(a.shape))
        o_shapes += [jax.ShapeDtypeStruct(params[n][0].shape, F32)] * 4
        o_specs += [_full(params[n][0].shape)] * 4
    res = pl.pallas_call(
        body, name="small_finish", grid=(1,),
        out_shape=[jax.ShapeDtypeStruct((SMALL_ROWS, D), F32), jax.ShapeDtypeStruct((2, D), F32)] + o_shapes,
        in_specs=[_full(early.shape), _full(late.shape), _full(dws.shape), _full((4, KW))] + p_specs,
        out_specs=[_full((SMALL_ROWS, D)), _full((2, D))] + o_specs,
        compiler_params=_params(("arbitrary",)),
    )(early, late, dws, gam, *p_args)
    return res[0], res[1], {n: res[2 + 4 * k:6 + 4 * k] for k, n in enumerate(names)}


def _adamw_sum8(stack, w, m, v, name):
    _, rows, cols = stack.shape
    tr = _row_tile(rows)

    def body(s_ref, w_ref, m_ref, v_ref, g_ref, d_ref, nm_ref, nv_ref):
        gv = s_ref[0].astype(F32)
        for j in range(1, N_DEV):
            gv = gv + s_ref[j].astype(F32)
        g_ref[...] = gv
        d_ref[...], nm_ref[...], nv_ref[...] = _adamw_update(w_ref[...], gv, m_ref[...], v_ref[...])

    blk = pl.BlockSpec((tr, cols), lambda i: (i, 0))
    sd = jax.ShapeDtypeStruct((rows, cols), F32)
    return pl.pallas_call(
        body, name=name, grid=(rows // tr,), out_shape=(sd, sd, sd, sd),
        in_specs=[pl.BlockSpec((N_DEV, tr, cols), lambda i: (0, i, 0)), blk, blk, blk], out_specs=(blk, blk, blk, blk),
        compiler_params=_params(("arbitrary",)),
    )(stack, w, m, v)


def _adamw(w, g, m, v, name):
    shape = w.shape
    cols = shape[-1]
    rows = 1
    for s in shape[:-1]:
        rows *= s
    tr = _row_tile(rows)

    def body(w_ref, g_ref, m_ref, v_ref, d_ref, nm_ref, nv_ref):
        gv = g_ref[...]
        nm = ADAM_B1 * m_ref[...] + (1.0 - ADAM_B1) * gv
        nv = ADAM_B2 * v_ref[...] + (1.0 - ADAM_B2) * (gv * gv)
        m_hat = nm / (1.0 - ADAM_B1 ** ADAM_STEP)
        v_hat = nv / (1.0 - ADAM_B2 ** ADAM_STEP)
        d_ref[...] = -ADAM_LR * (m_hat / (jnp.sqrt(v_hat) + ADAM_EPS) + ADAM_WD * w_ref[...])
        nm_ref[...] = nm
        nv_ref[...] = nv

    blk = pl.BlockSpec((tr, cols), lambda i: (i, 0))
    sd = jax.ShapeDtypeStruct((rows, cols), F32)
    d, nm, nv = pl.pallas_call(
        body, name=name, grid=(rows // tr,), out_shape=(sd, sd, sd), in_specs=[blk] * 4, out_specs=(blk, blk, blk),
        compiler_params=_params(("arbitrary",)),
    )(w.reshape(rows, cols), g.reshape(rows, cols), m.reshape(rows, cols), v.reshape(rows, cols))
    return d.reshape(shape), nm.reshape(shape), nv.reshape(shape)


def _owner_blocks(a):
    return a.reshape(N_DEV, a.shape[0] // N_DEV, a.shape[1])


class _LocalWeights:
    def __init__(self, w_upT, w_down, w_o, w_paT, w_pbT):
        self.weights = (w_upT, w_down, w_o, w_paT, w_pbT)
        self.items = {}

    def sender(self, stage, items=None):
        self.items[stage] = items
        return None

    def sent(self, stage, handle):
        pass

    def mixer_weights(self, after):
        return self.weights[1:]

    def ffn_weights(self, after):
        return self.weights[0]


def _local_step(x, ctx, target, modv, mcv, gam, g_mix, g_ffn, gna, ln_g, ln_b, w_s, b_s, g_final, w_inT, comm):
    nb_ex, seq, _ = x.shape
    ctx_len = ctx.shape[1]
    xt = x.reshape(nb_ex * seq, D)
    ct = ctx.reshape(nb_ex * ctx_len, D)
    tgt = target.reshape(nb_ex * seq, D)
    bs3 = b_s.reshape(GROUPS, SGU_BLOCK, 1)

    pc, hc, _ = _inproj(ct, mcv, g_mix, w_inT, CTX_COLS, ctx_len, "inproj_ctx")
    p, h, handle = _inproj(xt, modv, g_mix, w_inT, IN_COLS, seq, "inproj_lat", sender=comm.sender("inproj"))
    comm.sent("inproj", handle)
    cst_f, cst_b, s_ctx, _ = _hgrn_fwd(pc, gam, None, ctx_len, False, "hgrn_fwd_ctx")
    o_up, o_down, st_f, st_b, _, handle = _hgrn_fwd(p, gam, s_ctx, seq, True, "hgrn_fwd_lat",
                                                    sender=comm.sender("scan"))
    comm.sent("scan", handle)
    w_down, w_o, w_paT, w_pbT = comm.mixer_weights(o_up)
    x1, mix, merged, oa, obm = _tail_fwd(p, o_up, o_down, xt, modv, gna, ln_g, ln_b, w_s, bs3, w_paT, w_pbT, w_o, seq)
    w_upT = comm.ffn_weights(x1)
    dx1, h2, dffn, act, dup, dmod_ffn, small_ffn = _ffn(x1, tgt, modv, g_ffn, g_final, w_upT, w_down, seq)
    gw_upT, _ = _grad_matmul(dup, h2, "gw_up")
    gw_down, _ = _grad_matmul(act, dffn, "gw_down", tn=256)
    scatter = lambda *grads: [(_owner_blocks(g), "scatter") for g in grads]
    dpt, do, dmix, dpa, dpb, dmod_tail, small_tail, dws, dbs, handle = _tail_bwd(
        p, o_up, o_down, dx1, mix, modv, gna, ln_g, ln_b, w_s, bs3, w_paT, w_pbT, w_o, seq,
        sender=comm.sender("tail_bwd", scatter(gw_upT)))
    comm.sent("tail_bwd", handle)
    gw_o, _ = _grad_matmul(merged, dmix, "gw_o")
    gw_paT, _ = _grad_matmul(dpa, oa, "gw_pa")
    gw_pbT, _ = _grad_matmul(dpb, obm, "gw_pb")
    def at_row(row, a):
        return jnp.pad(a, ((row, SMALL_ROWS - row - a.shape[0]), (0, D - a.shape[1])))

    small_early = (at_row(1, small_ffn[0:2])
                   + at_row(3, small_tail[0:3])
                   + at_row(6, dbs.reshape(1, GROUPS * SGU_BLOCK))
                   + at_row(14, small_ffn[2:3]))
    dws_rows = dws.reshape(GROUPS * SGU_BLOCK, SGU_BLOCK)
    d_up, d_down, dlb, ds0, handle = _hgrn_bwd(
        p, gam, do, st_f, st_b, None, seq, True, "hgrn_bwd_lat",
        sender=comm.sender("scan_bwd", scatter(gw_down, gw_o, gw_paT, gw_pbT)
                           + [(small_early, "gather"), (dws_rows, "gather")]))
    comm.sent("scan_bwd", handle)
    c_up, c_down, dlb_c, _, _ = _hgrn_bwd(pc, gam, None, cst_f, cst_b, ds0, ctx_len, False, "hgrn_bwd_ctx")
    dpc, dmc, small_c, _ = _inproj_bwd(c_up, c_down, None, ct, None, mcv, g_mix, w_inT, ctx_len, "inproj_bwd_ctx")
    gw_inT, _ = _grad_in(d_up, d_down, dpt, h, _grad_matmul(dpc, hc, "gw_in_ctx")[0])
    grad_x, dmod_in, small_in, handle = _inproj_bwd(d_up, d_down, dpt, xt, dx1, modv, g_mix, w_inT, seq,
                                                   "inproj_bwd_lat", sender=comm.sender("inproj_bwd", scatter(gw_inT)))
    comm.sent("inproj_bwd", handle)
    dmod = dmod_in + dmod_tail + dmod_ffn
    small_late = (at_row(0, small_in[0:1] + small_c[0:1])
                  + at_row(7, (dlb + dlb_c).reshape(1, 2 * KW))
                  + at_row(8, dmc[0, 0:N_MOD])
                  + at_row(16, dmod[:, 0:N_MOD].reshape(nb_ex * N_MOD, D)))
    comm.sender("last", [(small_late, "gather")])
    return grad_x.reshape(x.shape)


def kernel(x, c, ctx, c_ctx, w_mod, b_mod, g_mix, g_ffn, w_in, lb_gamma, g_norm_a, ln_v_g, ln_v_b, w_s, b_s, w_pa, w_pb, w_o, w_up, w_down, g_final, loss_target, m_c_ctx, m_w_mod, m_b_mod, m_g_mix, m_g_ffn, m_w_in, m_lb_gamma, m_g_norm_a, m_ln_v_g, m_ln_v_b, m_w_s, m_b_s, m_w_pa, m_w_pb, m_w_o, m_w_up, m_w_down, m_g_final, v_c_ctx, v_w_mod, v_b_mod, v_g_mix, v_g_ffn, v_w_in, v_lb_gamma, v_g_norm_a, v_ln_v_g, v_ln_v_b, v_w_s, v_b_s, v_w_pa, v_w_pb, v_w_o, v_w_up, v_w_down, v_g_final):
    nb_ex = x.shape[0]
    me = 4 * lax.axis_index("x") + 2 * lax.axis_index("y") + lax.axis_index("c")
    cd = MXU_DTYPE
    mod_cols = w_mod.shape[2]
    lb_cols = lb_gamma.shape[2]

    w_inT_l = w_in[0].T.astype(cd)
    w_upT_l = w_up[0].T.astype(cd)
    w_paT_l = w_pa[0].T.astype(cd)
    w_pbT_l = w_pb[0].T.astype(cd)
    cl = jnp.concatenate([c, jnp.pad(lb_gamma.reshape(1, 4 * lb_cols), ((0, 0), (0, D - 4 * lb_cols))),
                          jnp.zeros((8 - nb_ex - 1, D), F32)], axis=0)
    g_in, g_cl = _gather_two_level([w_inT_l, cl], "gather_w_in")
    w_inT = g_in.reshape(IN_COLS, D)
    c_all = g_cl[:, 0:nb_ex].reshape(N_DEV * nb_ex, D)
    gam = jnp.transpose(g_cl[:, nb_ex, 0:4 * lb_cols].reshape(N_DEV, 4, lb_cols), (1, 0, 2)).reshape(4, KW)

    n_c = N_DEV * nb_ex
    cvec = jnp.concatenate([c_all, c_ctx.reshape(1, D), jnp.zeros((7, D), F32)], axis=0)
    b_mod_l = lax.dynamic_slice(b_mod, (0, me * mod_cols), (1, mod_cols))
    mod_l, svec = _mod_fwd(cvec, w_mod[0], b_mod_l)
    (g_mod,) = _gather_two_level([mod_l], "gather_mod")
    mod_all = jnp.transpose(g_mod, (1, 0, 2)).reshape(n_c + 8, N_MOD * D)
    modv = lax.dynamic_slice(mod_all, (me * nb_ex, 0), (nb_ex, N_MOD * D)).reshape(nb_ex, N_MOD, D)
    mcv = mod_all[n_c].reshape(1, N_MOD, D)

    handles, leftover = {}, {}

    class Comm:
        def sender(self, stage, items=None):
            if stage == "inproj":
                return _Sender([(w_down[0].astype(cd), "gather"), (w_o[0].astype(cd), "gather"), (w_paT_l, "gather"),
                                (w_pbT_l, "gather")])
            if stage == "scan":
                return _Sender([(w_upT_l, "gather")])
            if stage == "last":
                leftover["items"] = items
                return None
            return _Sender(items)

        def sent(self, stage, handle):
            handles[stage] = handle

        def mixer_weights(self, after):
            g_down, g_o, g_pa, g_pb = _exchange_wait(handles["inproj"], after)
            return g_down.reshape(D_FF, D), g_o.reshape(D, D), g_pa.reshape(D, KW), g_pb.reshape(D, KW)

        def ffn_weights(self, after):
            (g_up,) = _exchange_wait(handles["scan"], after)
            return g_up.reshape(2 * D_FF, D)

    grad_x = _local_step(
        x, ctx, loss_target, modv, mcv, gam, g_mix, g_ffn, g_norm_a, ln_v_g, ln_v_b, w_s[0], b_s[0],
        g_final.reshape(1, D), w_inT, Comm())
    last, last_started = _exchange_start(leftover["items"], "gather_small_late", after=leftover["items"][0][0])

    (r_up,) = _exchange_wait(handles["tail_bwd"], last_started)
    r_down, r_o, r_pa, r_pb, r_small, r_dws = _exchange_wait(handles["scan_bwd"], r_up)
    raw_up = _adamw_sum8(r_up, w_up[0].T, m_w_up[0].T, v_w_up[0].T, "adamw_w_up")
    raw_down = _adamw_sum8(r_down, w_down[0], m_w_down[0], v_w_down[0], "adamw_w_down")
    raw_o = _adamw_sum8(r_o, w_o[0], m_w_o[0], v_w_o[0], "adamw_w_o")
    (r_in,) = _exchange_wait(handles["inproj_bwd"], raw_up[1])
    raw_in = _adamw_sum8(r_in, w_in[0].T, m_w_in[0].T, v_w_in[0].T, "adamw_w_in")
    (r_late,) = _exchange_wait(last, raw_in[1])
    done = {"w_in": [a.T[None] for a in raw_in], "w_up": [a.T[None] for a in raw_up],
            "w_down": [a[None] for a in raw_down], "w_o": [a[None] for a in raw_o]}
    grad_w_in, grad_w_up, grad_w_down, grad_w_o = (done[k][0] for k in ("w_in", "w_up", "w_down", "w_o"))
    grad_w_pa = _sum8(r_pa, "sum_w_pa").T[None]
    grad_w_pb = _sum8(r_pb, "sum_w_pb").T[None]
    as_2d = {"g_final": (1, D), "b_s": (1, GROUPS * SGU_BLOCK), "b_mod": (N_MOD, D), "w_s": (GROUPS * SGU_BLOCK, SGU_BLOCK)}
    small_params = {"g_mix": (g_mix, m_g_mix, v_g_mix), "g_ffn": (g_ffn, m_g_ffn, v_g_ffn),
                    "g_final": (g_final, m_g_final, v_g_final), "g_norm_a": (g_norm_a, m_g_norm_a, v_g_norm_a),
                    "ln_v_g": (ln_v_g, m_ln_v_g, v_ln_v_g), "ln_v_b": (ln_v_b, m_ln_v_b, v_ln_v_b),
                    "b_s": (b_s, m_b_s, v_b_s), "b_mod": (b_mod, m_b_mod, v_b_mod), "w_s": (w_s, m_w_s, v_w_s)}
    tot, dgam, small_done = _small_finish(
        r_small, r_late, r_dws, gam, nb_ex,
        {n: tuple(a.reshape(as_2d.get(n, a.shape)) for a in wmv) for n, wmv in small_params.items()})
    for n, outs in small_done.items():
        done[n] = [a.reshape(small_params[n][0].shape) for a in outs]
    loss = tot[14, 0]
    grad_g_mix, grad_g_ffn, grad_g_final, grad_g_norm_a, grad_ln_v_g, grad_ln_v_b, grad_b_s, grad_b_mod, grad_w_s = (
        done[n][0] for n in ("g_mix", "g_ffn", "g_final", "g_norm_a", "ln_v_g", "ln_v_b", "b_s", "b_mod", "w_s"))
    grad_lb_gamma = lax.dynamic_slice(dgam.reshape(2, 2, KW), (0, 0, me * lb_cols), (2, 2, lb_cols))

    dmod_all = r_late[:, 16:16 + nb_ex * N_MOD].reshape(n_c, N_MOD * D)
    dmod_l = jnp.concatenate([lax.dynamic_slice(dmod_all, (0, me * mod_cols), (n_c, mod_cols)),
                              lax.dynamic_slice(tot[8:8 + N_MOD].reshape(1, N_MOD * D), (0, me * mod_cols), (1, mod_cols)),
                              jnp.zeros((7, mod_cols), F32)], axis=0)
    gw_mod, gc = _mod_bwd(svec, cvec, dmod_l, w_mod[0])
    grad_w_mod = gw_mod[None]
    (r_gc,) = _exchange([(gc[n_c:n_c + 8], "gather")], "gather_c_ctx", after=r_late)
    grad_c_ctx = _sum8(r_gc, "sum_c_ctx")[0]

    names = ["c_ctx", "w_mod", "b_mod", "g_mix", "g_ffn", "w_in", "lb_gamma", "g_norm_a", "ln_v_g", "ln_v_b", "w_s",
             "b_s", "w_pa", "w_pb", "w_o", "w_up", "w_down", "g_final"]
    weights = [c_ctx, w_mod, b_mod, g_mix, g_ffn, w_in, lb_gamma, g_norm_a, ln_v_g, ln_v_b, w_s, b_s, w_pa, w_pb, w_o,
               w_up, w_down, g_final]
    grads = [grad_c_ctx, grad_w_mod, grad_b_mod, grad_g_mix, grad_g_ffn, grad_w_in, grad_lb_gamma, grad_g_norm_a,
             grad_ln_v_g, grad_ln_v_b, grad_w_s, grad_b_s, grad_w_pa, grad_w_pb, grad_w_o, grad_w_up, grad_w_down,
             grad_g_final]
    ms = [m_c_ctx, m_w_mod, m_b_mod, m_g_mix, m_g_ffn, m_w_in, m_lb_gamma, m_g_norm_a, m_ln_v_g, m_ln_v_b, m_w_s, m_b_s,
          m_w_pa, m_w_pb, m_w_o, m_w_up, m_w_down, m_g_final]
    vs = [v_c_ctx, v_w_mod, v_b_mod, v_g_mix, v_g_ffn, v_w_in, v_lb_gamma, v_g_norm_a, v_ln_v_g, v_ln_v_b, v_w_s, v_b_s,
          v_w_pa, v_w_pb, v_w_o, v_w_up, v_w_down, v_g_final]
    deltas, new_ms, new_vs = [], [], []
    for nm, w, g, m, v in zip(names, weights, grads, ms, vs):
        d, nm_, nv_ = done[nm][1:] if nm in done else _adamw(w, g.reshape(w.shape), m, v, "adamw_" + nm)
        deltas.append(d)
        new_ms.append(nm_)
        new_vs.append(nv_)
    grads = [g.reshape(w.shape) for g, w in zip(grads, weights)]
    return (loss, grad_x, *grads, *deltas, *new_ms, *new_vs)
```

```python
import functools

import jax
import jax.numpy as jnp
from jax import lax
from jax.experimental import pallas as pl
from jax.experimental.pallas import tpu as pltpu

F32 = jnp.float32
MXU_DTYPE = jnp.bfloat16
PAYLOAD_DTYPE = jnp.bfloat16

N_DEV = 8
D = 1024
HEADS = 4
DK = 128
KW = HEADS * DK
CHUNK = 64
SGU_BLOCK = 128
GROUPS = 4
D_FF = 2816
FF_CHUNK = 256
N_MOD = 6
IN_COLS = 5632
CTX_COLS = 1536
TAIL_COLS = IN_COLS - 4 * KW
EPS = 1e-6
ADAM_LR, ADAM_B1, ADAM_B2, ADAM_EPS, ADAM_WD, ADAM_STEP = 0.001, 0.9, 0.999, 1e-08, 0.01, 10

VMEM_LIMIT = 56 * 1024 * 1024
TOKEN_TILE = 256
PROJ_TILE = 512
TAIL_TILE = 512
SMALL_ROWS = 40


def _params(sem):
    return pltpu.CompilerParams(dimension_semantics=sem, vmem_limit_bytes=VMEM_LIMIT)


_DN = {"nn": (((1,), (0,)), ((), ())), "nt": (((1,), (1,)), ((), ())), "tn": (((0,), (0,)), ((), ()))}


def _dot(a, b, form="nn"):
    return lax.dot_general(a.astype(MXU_DTYPE), b.astype(MXU_DTYPE), _DN[form], preferred_element_type=F32)


def _mask_dot(mask, v):
    bf = jnp.bfloat16
    hi = v.astype(bf)
    r1 = v - hi.astype(F32)
    mid = r1.astype(bf)
    lo = (r1 - mid.astype(F32)).astype(bf)
    w = v.shape[1]
    s = lax.dot_general(mask.astype(bf), jnp.concatenate([hi, mid, lo], axis=1), _DN["nn"], preferred_element_type=F32)
    return (s[:, 2 * w:] + s[:, w:2 * w]) + s[:, :w]


def _full(shape, single=False):
    n = len(shape)
    if single:
        return pl.BlockSpec(shape, lambda *_: (0,) * n, pipeline_mode=pl.Buffered(1))
    return pl.BlockSpec(shape, lambda *_: (0,) * n)


def _ordered_behind(body, in_specs, args, after):
    if after is None:
        return body
    at = len(in_specs)
    in_specs.append(pl.BlockSpec(memory_space=pl.ANY))
    args.append(after)
    return lambda *refs: body(*refs[:at], *refs[at + 1:])


def _sigmoid(z):
    return 0.5 * jnp.tanh(0.5 * z) + 0.5


def _gelu(x):
    c = 0.7978845608028654
    t = jnp.tanh(c * (x + 0.044715 * x * x * x))
    return 0.5 * x * (1.0 + t), t


def _gelu_grad(x, t):
    c = 0.7978845608028654
    return 0.5 * (1.0 + t) + 0.5 * x * (1.0 - t * t) * c * (1.0 + 3 * 0.044715 * x * x)


def _exchange(items, name, after=None):
    n = len(items)
    out_shape = []
    for a, mode in items:
        blk = a.shape if mode == "gather" else a.shape[1:]
        out_shape.append(jax.ShapeDtypeStruct((N_DEV,) + tuple(blk), a.dtype))

    def body(*refs):
        srcs, dsts = refs[:n], refs[n:2 * n]
        send_sems, recv_sems, local_sems = refs[2 * n:]
        x, y, c = lax.axis_index("x"), lax.axis_index("y"), lax.axis_index("c")
        me = 4 * x + 2 * y + c

        def src_for(i, dev):
            return srcs[i] if items[i][1] == "gather" else srcs[i].at[dev]

        local = [pltpu.make_async_copy(src_for(i, me), dsts[i].at[me], local_sems.at[i]) for i in range(n)]
        for cp in local:
            cp.start()
        remote = []
        for k in range(1, N_DEV):
            px = jnp.bitwise_xor(x, (k >> 2) & 1)
            py = jnp.bitwise_xor(y, (k >> 1) & 1)
            pc = jnp.bitwise_xor(c, k & 1)
            peer = 4 * px + 2 * py + pc
            for i in range(n):
                cp = pltpu.make_async_remote_copy(
                    src_ref=src_for(i, peer), dst_ref=dsts[i].at[me],
                    send_sem=send_sems.at[i * (N_DEV - 1) + k - 1], recv_sem=recv_sems.at[i * (N_DEV - 1) + k - 1],
                    device_id=(px, py, pc), device_id_type=pl.DeviceIdType.MESH)
                cp.start()
                remote.append(cp)
        for cp in remote:
            cp.wait()
        for cp in local:
            cp.wait()

    any_spec = pl.BlockSpec(memory_space=pl.ANY)
    in_specs, args = [any_spec] * n, [a for a, _ in items]
    if after is not None:
        in_specs.append(any_spec)
        args.append(after)
        exchange = body
        body = lambda *refs: exchange(*refs[:n], *refs[n + 1:])
    return pl.pallas_call(
        body, name=name, out_shape=out_shape, in_specs=in_specs, out_specs=[any_spec] * n,
        scratch_shapes=[pltpu.SemaphoreType.DMA((n * (N_DEV - 1),)), pltpu.SemaphoreType.DMA((n * (N_DEV - 1),)),
                        pltpu.SemaphoreType.DMA((n,))],
    )(*args)


def _gather_two_level(arrays, name):
    n = len(arrays)
    pieces = []
    for i, a in enumerate(arrays):
        rows = _Sender.PIECE_ROWS if a.shape[0] % _Sender.PIECE_ROWS == 0 else a.shape[0]
        pieces += [(i, r0, rows) for r0 in range(0, a.shape[0], rows)]

    def body(*refs):
        srcs, dsts = refs[:n], refs[n:2 * n]
        send_sems, recv_sems, local_sems = refs[2 * n:]
        x, y, c = lax.axis_index("x"), lax.axis_index("y"), lax.axis_index("c")
        me, sibling = (x, y, c), (x, y, 1 - c)
        x_nbr, y_nbr, diag = (1 - x, y, c), (x, 1 - y, c), (1 - x, 1 - y, c)

        def slot(px, py, pc):
            return 4 * px + 2 * py + pc

        def copy(u, k, block, to, own=False):
            i, r0, rows = pieces[u]
            there = dsts[i].at[slot(*block)].at[pl.ds(r0, rows)]
            return pltpu.make_async_remote_copy(
                src_ref=srcs[i].at[pl.ds(r0, rows)] if own else there, dst_ref=there,
                send_sem=send_sems.at[u * 7 + k], recv_sem=recv_sems.at[u * 7 + k],
                device_id=to, device_id_type=pl.DeviceIdType.MESH)

        units = range(len(pieces))
        mine = [pltpu.make_async_copy(srcs[i], dsts[i].at[slot(*me)], local_sems.at[i]) for i in range(n)]
        for cp in mine:
            cp.start()
        for u in units:
            copy(u, 1, me, x_nbr, own=True).start()
            copy(u, 2, me, y_nbr, own=True).start()
        for u in units:
            copy(u, 0, me, sibling, own=True).start()

        def relay_then_pass(k_from, frm, to, k_other, other):
            for u in units:
                copy(u, k_from, frm, me).wait_recv()
                copy(u, 3, frm, to).start()
                copy(u, 3 + k_from, frm, sibling).start()
            for u in units:
                copy(u, k_other, other, me).wait_recv()
                copy(u, 3 + k_other, other, sibling).start()

        @pl.when(c == 1)
        def _():
            relay_then_pass(1, x_nbr, y_nbr, 2, y_nbr)

        @pl.when(c == 0)
        def _():
            relay_then_pass(2, y_nbr, x_nbr, 1, x_nbr)

        for u in units:
            copy(u, 3, diag, me).wait_recv()
            copy(u, 6, diag, sibling).start()
        for u in units:
            copy(u, 0, sibling, me).wait_recv()
            for k, chip in ((4, x_nbr), (5, y_nbr), (6, diag)):
                copy(u, k, (chip[0], chip[1], 1 - c), me).wait_recv()
        for u in units:
            for k in range(7):
                copy(u, k, me, me, own=True).wait_send()
        for cp in mine:
            cp.wait()

    any_spec = pl.BlockSpec(memory_space=pl.ANY)
    return pl.pallas_call(
        body, name=name, out_shape=[jax.ShapeDtypeStruct((N_DEV,) + a.shape, a.dtype) for a in arrays],
        in_specs=[any_spec] * n, out_specs=[any_spec] * n,
        scratch_shapes=[pltpu.SemaphoreType.DMA((len(pieces) * 7,)), pltpu.SemaphoreType.DMA((len(pieces) * 7,)),
                        pltpu.SemaphoreType.DMA((n,))],
    )(*arrays)


_HBM = pl.BlockSpec(memory_space=pltpu.HBM)
_SEM = pl.BlockSpec(memory_space=pltpu.SEMAPHORE)
_EFFECT = pltpu.SideEffectType.DATAFLOW_SIDE_EFFECTING


def _split_copies(items, srcs, lands, send_sems, recv_sems):
    x, y, c = lax.axis_index("x"), lax.axis_index("y"), lax.axis_index("c")
    me = 4 * x + 2 * y + c
    copies = []
    for k in range(1, N_DEV):
        px = jnp.bitwise_xor(x, (k >> 2) & 1)
        py = jnp.bitwise_xor(y, (k >> 1) & 1)
        pc = jnp.bitwise_xor(c, k & 1)
        peer = 4 * px + 2 * py + pc
        for i in range(len(items)):
            src = srcs[i] if items[i][1] == "gather" else srcs[i].at[peer]
            copies.append(pltpu.make_async_remote_copy(
                src_ref=src, dst_ref=lands[i].at[me],
                send_sem=send_sems.at[i * (N_DEV - 1) + k - 1], recv_sem=recv_sems.at[i * (N_DEV - 1) + k - 1],
                device_id=(px, py, pc), device_id_type=pl.DeviceIdType.MESH))
    return me, copies


def _exchange_start(items, name, after):
    n = len(items)
    n_sem = n * (N_DEV - 1)
    srcs, lands = [], []
    for a, mode in items:
        blk = a.shape if mode == "gather" else a.shape[1:]
        srcs.append(pltpu.with_memory_space_constraint(a, pltpu.HBM))
        lands.append(pltpu.with_memory_space_constraint(lax.empty((N_DEV,) + tuple(blk), a.dtype), pltpu.HBM))

    def body(*refs):
        src_refs, land_refs = refs[:n], refs[n:2 * n]
        send_sems, recv_sems = refs[2 * n + 1], refs[2 * n + 2]
        local_sems = refs[4 * n + 3]
        me, copies = _split_copies(items, src_refs, land_refs, send_sems, recv_sems)
        for i in range(n):
            own = src_refs[i] if items[i][1] == "gather" else src_refs[i].at[me]
            cp = pltpu.make_async_copy(own, land_refs[i].at[me], local_sems.at[i])
            cp.start()
            cp.wait()
        for cp in copies:
            cp.start()

    out_shape = [pltpu.SemaphoreType.DMA((n_sem,)), pltpu.SemaphoreType.DMA((n_sem,))]
    out_shape += [pltpu.HBM(a.shape, a.dtype) for a in srcs] + [pltpu.HBM(a.shape, a.dtype) for a in lands]
    outs = pl.pallas_call(
        body, name=name, out_shape=out_shape,
        in_specs=[_HBM] * (2 * n) + [pl.BlockSpec(memory_space=pl.ANY)],
        out_specs=[_SEM, _SEM] + [_HBM] * (2 * n),
        input_output_aliases={i: 2 + i for i in range(2 * n)},
        scratch_shapes=[pltpu.SemaphoreType.DMA((n,))],
        compiler_params=pltpu.CompilerParams(has_side_effects=_EFFECT),
    )(*srcs, *lands, after)
    handle = (items, name, outs[0], outs[1], outs[2:2 + n], outs[2 + n:2 + 2 * n])
    return handle, outs[2]


class _Sender:
    PIECE_ROWS = 352

    def __init__(self, items, chunks=None):
        self.items, self.n = items, len(items)
        self.chunks = chunks
        if chunks is None:
            block_rows = [a.shape[0] if mode == "gather" else a.shape[1] for a, mode in items]
            self.chunks = [r // self.PIECE_ROWS if r % self.PIECE_ROWS == 0 else 1 for r in block_rows]
        self.srcs, self.lands = [], []
        for a, mode in items:
            blk = a.shape if mode == "gather" else a.shape[1:]
            self.srcs.append(pltpu.with_memory_space_constraint(a, pltpu.HBM))
            self.lands.append(pltpu.with_memory_space_constraint(lax.empty((N_DEV,) + tuple(blk), a.dtype), pltpu.HBM))

    def issue(self, src_refs, land_refs, send_sems, recv_sems, local_sems, step, n_steps):
        x, y, c = lax.axis_index("x"), lax.axis_index("y"), lax.axis_index("c")
        me = 4 * x + 2 * y + c
        copies = []
        for ch in range(max(self.chunks)):
            for k in range(1, N_DEV):
                px = jnp.bitwise_xor(x, (k >> 2) & 1)
                py = jnp.bitwise_xor(y, (k >> 1) & 1)
                pc = jnp.bitwise_xor(c, k & 1)
                peer = 4 * px + 2 * py + pc
                for i, (_, mode) in enumerate(self.items):
                    if ch >= self.chunks[i]:
                        continue
                    n_rows = land_refs[i].shape[1] // self.chunks[i]
                    rows = pl.ds(ch * n_rows, n_rows)
                    src = src_refs[i].at[rows] if mode == "gather" else src_refs[i].at[peer].at[rows]
                    copies.append(pltpu.make_async_remote_copy(
                        src_ref=src, dst_ref=land_refs[i].at[me].at[rows],
                        send_sem=send_sems.at[i * (N_DEV - 1) + k - 1], recv_sem=recv_sems.at[i * (N_DEV - 1) + k - 1],
                        device_id=(px, py, pc), device_id_type=pl.DeviceIdType.MESH))
        own = [pltpu.make_async_copy(src_refs[i] if mode == "gather" else src_refs[i].at[me], land_refs[i].at[me],
                                     local_sems.at[i]) for i, (_, mode) in enumerate(self.items)]

        @pl.when(step == 0)
        def _():
            for cp in own:
                cp.start()

        for s in range(n_steps):
            group = [cp for j, cp in enumerate(copies) if (j * n_steps) // len(copies) == s]
            if group:
                @pl.when(step == s)
                def _(group=group):
                    for cp in group:
                        cp.start()

        @pl.when(step == n_steps - 1)
        def _():
            for cp in own:
                cp.wait()


def _host_call(body, name, grid, in_specs, args, out_shape, out_specs, scratch_shapes, after=None, sender=None):
    in_specs, args, out_shape, out_specs = list(in_specs), list(args), list(out_shape), list(out_specs)
    scratch_shapes = list(scratch_shapes)
    semantics = ("arbitrary",) * len(grid)
    body = _ordered_behind(body, in_specs, args, after)
    if sender is None:
        res = pl.pallas_call(body, name=name, grid=grid, in_specs=in_specs, out_specs=out_specs, out_shape=out_shape,
                             scratch_shapes=scratch_shapes, compiler_params=_params(semantics))(*args)
        return res, None
    n, n_in, n_out, n_scr = sender.n, len(in_specs), len(out_shape), len(scratch_shapes)
    n_sem = n * (N_DEV - 1)
    n_steps = 1
    for g in grid:
        n_steps *= g
    compute = body

    def body(*refs):
        ins, s_in = refs[:n_in], refs[n_in:n_in + 2 * n]
        o0 = n_in + 2 * n
        outs, s_out = refs[o0:o0 + n_out], refs[o0 + n_out:o0 + n_out + 2 + 2 * n]
        scr = refs[o0 + n_out + 2 + 2 * n:]
        compute(*ins, *outs, *scr[:n_scr])
        step = pl.program_id(0)
        for d in range(1, len(grid)):
            step = step * grid[d] + pl.program_id(d)
        sender.issue(s_in[:n], s_in[n:], s_out[0], s_out[1], scr[n_scr], step, n_steps)

    res = pl.pallas_call(
        body, name=name, grid=grid,
        in_specs=in_specs + [_HBM] * (2 * n), out_specs=out_specs + [_SEM, _SEM] + [_HBM] * (2 * n),
        out_shape=out_shape + [pltpu.SemaphoreType.DMA((n_sem,)), pltpu.SemaphoreType.DMA((n_sem,))]
        + [pltpu.HBM(a.shape, a.dtype) for a in sender.srcs] + [pltpu.HBM(a.shape, a.dtype) for a in sender.lands],
        input_output_aliases={n_in + j: n_out + 2 + j for j in range(2 * n)},
        scratch_shapes=scratch_shapes + [pltpu.SemaphoreType.DMA((n,))],
        compiler_params=pltpu.CompilerParams(dimension_semantics=semantics, vmem_limit_bytes=VMEM_LIMIT,
                                             has_side_effects=_EFFECT),
    )(*args, *sender.srcs, *sender.lands)
    handle = (sender.items, name, res[n_out], res[n_out + 1], res[n_out + 2:n_out + 2 + n],
              res[n_out + 2 + n:n_out + 2 + 2 * n])
    return res[:n_out], handle


def _exchange_wait(handle, after):
    items, name, send_sems, recv_sems, srcs, lands = handle
    n = len(items)

    def body(*refs):
        src_refs, land_refs = refs[:n], refs[n:2 * n]
        send_ref, recv_ref = refs[2 * n], refs[2 * n + 1]
        _, copies = _split_copies(items, src_refs, land_refs, send_ref, recv_ref)
        for cp in copies:
            cp.wait_send()
            cp.wait_recv()

    outs = pl.pallas_call(
        body, name=name + "_wait",
        out_shape=[pltpu.HBM(a.shape, a.dtype) for a in srcs] + [pltpu.HBM(a.shape, a.dtype) for a in lands],
        in_specs=[_HBM] * (2 * n) + [_SEM, _SEM, pl.BlockSpec(memory_space=pl.ANY)], out_specs=[_HBM] * (2 * n),
        input_output_aliases={i: i for i in range(2 * n)},
        compiler_params=pltpu.CompilerParams(has_side_effects=_EFFECT),
    )(*srcs, *lands, send_sems, recv_sems, after)
    return outs[n:]


def _mod_fwd(cvec, w_mod_l, b_mod_l):
    rows, cols = cvec.shape[0], w_mod_l.shape[1]

    def body(c_ref, w_ref, b_ref, o_ref, s_ref):
        cv = c_ref[...]
        s = cv * _sigmoid(cv)
        s_ref[...] = s
        o_ref[...] = _dot(s, w_ref[...]) + b_ref[...]

    return pl.pallas_call(
        body, name="mod_fwd",
        out_shape=(jax.ShapeDtypeStruct((rows, cols), F32), jax.ShapeDtypeStruct((rows, D), F32)),
        in_specs=[_full((rows, D)), _full((D, cols)), _full((1, cols))],
        out_specs=(_full((rows, cols)), _full((rows, D))), grid=(1,),
        compiler_params=_params(("arbitrary",)),
    )(cvec, w_mod_l, b_mod_l)


def _mod_bwd(svec, cvec, dmod_l, w_mod_l):
    rows, cols = dmod_l.shape

    def body(s_ref, c_ref, d_ref, w_ref, gw_ref, gc_ref):
        gw_ref[...] = _dot(s_ref[...], d_ref[...], "tn")
        cv = c_ref[...]
        sg = _sigmoid(cv)
        gc_ref[...] = _dot(d_ref[...], w_ref[...], "nt") * (sg * (1.0 + cv * (1.0 - sg)))

    return pl.pallas_call(
        body, name="mod_bwd",
        out_shape=(jax.ShapeDtypeStruct((D, cols), F32), jax.ShapeDtypeStruct((rows, D), F32)),
        in_specs=[_full((rows, D)), _full((rows, D)), _full((rows, cols)), _full((D, cols))],
        out_specs=(_full((D, cols)), _full((rows, D))), grid=(1,),
        compiler_params=_params(("arbitrary",)),
    )(svec, cvec, dmod_l, w_mod_l)


def _inproj(xt, modv, g, w_inT, n_cols, rows_per_example, name, after=None, sender=None):
    rows = xt.shape[0]
    tm = min(PROJ_TILE, rows_per_example)
    per_b = rows_per_example // tm
    shared_mod = modv.shape[0] == 1

    def body(x_ref, mod_ref, g_ref, w_ref, p_ref, h_ref):
        x = x_ref[...]
        r = lax.rsqrt(jnp.mean(x * x, axis=-1, keepdims=True) + EPS)
        h = (x * r * g_ref[...]) * (1.0 + mod_ref[0, 1:2, :]) + mod_ref[0, 0:1, :]
        hb = h.astype(MXU_DTYPE)
        h_ref[...] = hb
        for j in range(n_cols // KW):
            p_ref[:, j * KW:(j + 1) * KW] = _dot(hb, w_ref[j * KW:(j + 1) * KW, :], "nt").astype(p_ref.dtype)

    mod_idx = (lambda i: (0, 0, 0)) if shared_mod else (lambda i: (i // per_b, 0, 0))
    in_specs = [pl.BlockSpec((tm, D), lambda i: (i, 0)), pl.BlockSpec((1, N_MOD, D), mod_idx), _full((1, D)),
                pl.BlockSpec((n_cols, D), lambda i: (0, 0), pipeline_mode=pl.Buffered(1))]
    (p, h), handle = _host_call(
        body, name, (rows // tm,), in_specs, [xt, modv, g, w_inT],
        [jax.ShapeDtypeStruct((rows, n_cols), MXU_DTYPE), jax.ShapeDtypeStruct((rows, D), MXU_DTYPE)],
        [pl.BlockSpec((tm, n_cols), lambda i: (i, 0)), pl.BlockSpec((tm, D), lambda i: (i, 0))], [],
        after=after, sender=sender)
    return p, h, handle


def _tri(reverse, n):
    row = lax.broadcasted_iota(jnp.int32, (n, n), 0)
    col = lax.broadcasted_iota(jnp.int32, (n, n), 1)
    same = (row // CHUNK) == (col // CHUNK)
    return same & ((col >= row) if reverse else (col <= row))


def _per_chunk_rows(x, reverse):
    n = x.shape[0]
    rows = [x[j * CHUNK:j * CHUNK + 1] if reverse else x[(j + 1) * CHUNK - 1:(j + 1) * CHUNK] for j in range(n // CHUNK)]
    return jnp.concatenate([jnp.broadcast_to(r, (CHUNK, x.shape[1])) for r in rows], axis=0), rows


def _lower_bound(gam_ref, direction):
    return _sigmoid(gam_ref[direction:direction + 1, :] - gam_ref[2 + direction:3 + direction, :])


def _gate_prep(z, lb, tri, reverse, b=None):
    sg = _sigmoid(z)
    f = lb + (1.0 - lb) * sg
    g = jnp.log(f)
    b = _mask_dot(tri, g) if b is None else b
    bl, bl_rows = _per_chunk_rows(b, reverse)
    mid = 0.5 * bl
    return sg, g, 1.0 - f, b, jnp.exp(mid), [jnp.exp(0.5 * r) for r in bl_rows], jnp.exp(mid - b), mid


def _hgrn_fwd(p, gam, s0, rows_per_example, with_out, name, sender=None):
    rows = p.shape[0]
    nb_ex = rows // rows_per_example
    rb = min(TOKEN_TILE, rows_per_example)
    cpb = rb // CHUNK
    nb = rows_per_example // rb
    n_chunks = rows // CHUNK
    has_s0 = s0 is not None

    def body(*refs):
        it = iter(refs)
        gam_ref = next(it)
        zf_ref, vf_ref = next(it), next(it)
        qf_ref = next(it) if with_out else None
        zb_ref, vb_ref = next(it), next(it)
        qb_ref = next(it) if with_out else None
        s0_ref = next(it) if has_s0 else None
        if with_out:
            of_ref, ob_ref = next(it), next(it)
        stash_f, stash_b, bsum_f, bsum_b, fin_ref = next(it), next(it), next(it), next(it), next(it)
        st_ref = next(it)
        i = pl.program_id(1)

        @pl.when(i == 0)
        def _():
            if has_s0:
                st_ref[...] = s0_ref[:, 0]
            else:
                st_ref[...] = jnp.zeros_like(st_ref)

        for direction, (z_ref, v_ref, q_ref, stash, bsum_ref) in enumerate(
                ((zf_ref, vf_ref, qf_ref, stash_f, bsum_f), (zb_ref, vb_ref, qb_ref, stash_b, bsum_b))):
            reverse = direction == 1
            tri = _tri(reverse, rb)
            lb = _lower_bound(gam_ref, direction)
            z = z_ref[...].astype(F32)
            v = v_ref[...].astype(F32)
            _, _, k, b, em, em_rows, e2, mid = _gate_prep(z, lb, tri, reverse)
            bsum_ref[...] = b
            kd = (k * (e2 * em)).astype(MXU_DTYPE)
            vb = v.astype(MXU_DTYPE)
            if with_out:
                q = q_ref[...].astype(F32)
                qi = q * jnp.exp(b - mid)
                qe = (qi * em).astype(MXU_DTYPE)
                qi = qi.astype(MXU_DTYPE)
                ki = (k * e2).astype(MXU_DTYPE)
                intra = []
                for h in range(HEADS):
                    hs = slice(h * DK, (h + 1) * DK)
                    sc = jnp.where(tri, _dot(qi[:, hs], ki[:, hs], "nt"), 0.0)
                    intra.append(_dot(sc, vb[:, hs]))
            for j in (range(cpb - 1, -1, -1) if reverse else range(cpb)):
                rs = slice(j * CHUNK, (j + 1) * CHUNK)
                a = em_rows[j] * em_rows[j]
                for h in range(HEADS):
                    hs = slice(h * DK, (h + 1) * DK)
                    st = st_ref[direction, h]
                    stash[j, h] = st.astype(stash.dtype)
                    if with_out:
                        (ob_ref if reverse else of_ref)[rs, hs] = intra[h][rs] + _dot(qe[rs, hs], st, "nt")
                    st_ref[direction, h] = st * a[:, hs] + _dot(vb[rs, hs], kd[rs, hs], "tn")

        @pl.when(i == nb - 1)
        def _():
            fin_ref[:, 0] = st_ref[...]

    up = lambda b, i: b * nb + i
    down = lambda b, i: b * nb + nb - 1 - i
    col = lambda rowf, c: pl.BlockSpec((rb, KW), lambda b, i: (rowf(b, i), c))
    in_specs = [_full((4, KW)), col(up, 0), col(up, 2)] + ([col(up, 3)] if with_out else [])
    in_specs += [col(down, 1), col(down, 2)] + ([col(down, 3)] if with_out else [])
    args = [gam, p, p] + ([p] if with_out else []) + [p, p] + ([p] if with_out else [])
    if has_s0:
        in_specs.append(pl.BlockSpec((2, 1, HEADS, DK, DK), lambda b, i: (0, b, 0, 0, 0)))
        args.append(s0)
    out_shape, out_specs = [], []
    if with_out:
        out_shape += [jax.ShapeDtypeStruct((rows, KW), F32)] * 2
        out_specs += [pl.BlockSpec((rb, KW), lambda b, i: (up(b, i), 0)),
                      pl.BlockSpec((rb, KW), lambda b, i: (down(b, i), 0))]
    out_shape += [jax.ShapeDtypeStruct((n_chunks, HEADS, DK, DK), MXU_DTYPE)] * 2
    out_specs += [pl.BlockSpec((cpb, HEADS, DK, DK), lambda b, i: (up(b, i), 0, 0, 0)),
                  pl.BlockSpec((cpb, HEADS, DK, DK), lambda b, i: (down(b, i), 0, 0, 0))]
    out_shape += [jax.ShapeDtypeStruct((rows, KW), F32)] * 2
    out_specs += [pl.BlockSpec((rb, KW), lambda b, i: (up(b, i), 0)),
                  pl.BlockSpec((rb, KW), lambda b, i: (down(b, i), 0))]
    out_shape.append(jax.ShapeDtypeStruct((2, nb_ex, HEADS, DK, DK), F32))
    out_specs.append(pl.BlockSpec((2, 1, HEADS, DK, DK), lambda b, i: (0, b, 0, 0, 0)))
    res, handle = _host_call(body, name, (nb_ex, nb), in_specs, args, out_shape, out_specs,
                             [pltpu.VMEM((2, HEADS, DK, DK), F32)], sender=sender)
    return (*res, handle)


def _hgrn_bwd(p, gam, do, stash_f, stash_b, bsum_f, bsum_b, ds_end, rows_per_example, with_out, name, after=None,
              sender=None):
    rows = p.shape[0]
    nb_ex = rows // rows_per_example
    rb = min(TOKEN_TILE, rows_per_example)
    cpb = rb // CHUNK
    nb = rows_per_example // rb
    has_end = ds_end is not None

    def body(*refs):
        it = iter(refs)
        gam_ref = next(it)
        ins = []
        for _ in range(2):
            z_ref, v_ref = next(it), next(it)
            q_ref = next(it) if with_out else None
            do_ref = next(it) if with_out else None
            ins.append((z_ref, v_ref, q_ref, do_ref, next(it), next(it)))
        end_ref = next(it) if has_end else None
        outs = [next(it), next(it)]
        dlb_ref, ds0_ref = next(it), next(it)
        dst_ref = next(it)
        b_id, i = pl.program_id(0), pl.program_id(1)

        @pl.when(i == 0)
        def _():
            if has_end:
                dst_ref[...] = end_ref[:, 0]
            else:
                dst_ref[...] = jnp.zeros_like(dst_ref)

        @pl.when((i == 0) & (b_id == 0))
        def _():
            dlb_ref[...] = jnp.zeros_like(dlb_ref)

        for direction in range(2):
            z_ref, v_ref, q_ref, do_ref, stash, b_ref = ins[direction]
            dgrp_ref = outs[direction]
            reverse = direction == 1
            tri = _tri(reverse, rb)
            tri_t = _tri(not reverse, rb)
            lb = _lower_bound(gam_ref, direction)
            heads = [slice(h * DK, (h + 1) * DK) for h in range(HEADS)]
            chunks = [slice(j * CHUNK, (j + 1) * CHUNK) for j in range(cpb)]
            grid_cat = lambda parts: jnp.concatenate([jnp.concatenate(row, axis=1) for row in parts], axis=0)
            cat = lambda parts: jnp.concatenate(parts, axis=1)
            z = z_ref[...].astype(F32)
            sg, g, k, b, em, em_rows, e2, mid = _gate_prep(z, lb, tri, reverse, b=b_ref[...])
            e3 = e2 * em
            kd = k * e3
            kd_b = kd.astype(MXU_DTYPE)
            vb = v_ref[...].astype(MXU_DTYPE)
            if with_out:
                q = q_ref[...].astype(F32)
                dout = do_ref[...].astype(MXU_DTYPE)
                e1 = jnp.exp(b - mid)
                e4 = e1 * em
                qi, ki, qe = q * e1, k * e2, q * e4
                qi_b, ki_b, qe_b = qi.astype(MXU_DTYPE), ki.astype(MXU_DTYPE), qe.astype(MXU_DTYPE)
                dqi_p, dki_p, dv_p = [], [], []
                for hs in heads:
                    sc = jnp.where(tri, _dot(qi_b[:, hs], ki_b[:, hs], "nt"), 0.0)
                    dsc = jnp.where(tri, _dot(dout[:, hs], vb[:, hs], "nt"), 0.0)
                    dqi_p.append(_dot(dsc, ki_b[:, hs]))
                    dki_p.append(_dot(dsc, qi_b[:, hs], "tn"))
                    dv_p.append(_dot(sc, dout[:, hs], "tn"))
                dqi, dki, dv = cat(dqi_p), cat(dki_p), cat(dv_p)
                dqe = grid_cat([[_dot(dout[rs, hs], stash[j, h]) for h, hs in enumerate(heads)]
                                for j, rs in enumerate(chunks)])
                grow = [[_dot(dout[rs, hs], qe_b[rs, hs], "tn") for hs in heads] for rs in chunks]
            dkd_p = [[None] * HEADS for _ in range(cpb)]
            dvs_p = [[None] * HEADS for _ in range(cpb)]
            da_p = [[None] * HEADS for _ in range(cpb)]
            for j in (range(cpb) if reverse else range(cpb - 1, -1, -1)):
                rs = chunks[j]
                a = em_rows[j] * em_rows[j]
                for h, hs in enumerate(heads):
                    dst = dst_ref[direction, h]
                    dkd_p[j][h] = _dot(vb[rs, hs], dst)
                    dvs_p[j][h] = _dot(kd_b[rs, hs], dst, "nt")
                    da_p[j][h] = jnp.broadcast_to(
                        jnp.sum(dst * stash[j, h].astype(F32), axis=0, keepdims=True), (CHUNK, DK))
                    new_dst = dst * a[:, hs]
                    dst_ref[direction, h] = new_dst + grow[j][h] if with_out else new_dst
            dkd, dvs, da = grid_cat(dkd_p), grid_cat(dvs_p), grid_cat(da_p)
            t_kd = dkd * kd
            dk = dkd * e3
            db = -t_kd
            tot = t_kd
            if with_out:
                dgrp_ref[:, KW:2 * KW] = (dvs + dv).astype(dgrp_ref.dtype)
                dgrp_ref[:, 2 * KW:] = (dqi * e1 + dqe * e4).astype(dgrp_ref.dtype)
                dk = dk + dki * e2
                t_qi, t_ki, t_qe = dqi * qi, dki * ki, dqe * qe
                db = db + t_qi - t_ki + t_qe
                tot = tot + 0.5 * (t_ki - t_qi)
            else:
                dgrp_ref[:, KW:2 * KW] = dvs.astype(dgrp_ref.dtype)
            dbl = jnp.concatenate([jnp.broadcast_to(jnp.sum(tot[rs], axis=0, keepdims=True), (CHUNK, KW))
                                   for rs in chunks], axis=0) + da * (em * em)
            dg = _mask_dot(tri_t, db) + dbl
            df = dg * jnp.exp(-g) - dk
            dgrp_ref[:, 0:KW] = (df * (1.0 - lb) * sg * (1.0 - sg)).astype(dgrp_ref.dtype)
            dlb_ref[direction:direction + 1, :] += jnp.sum(df * (1.0 - sg), axis=0, keepdims=True)

        @pl.when(i == nb - 1)
        def _():
            ds0_ref[:, 0] = dst_ref[...]

    rows_of = (lambda b, i: b * nb + nb - 1 - i, lambda b, i: b * nb + i)
    in_specs, args = [_full((4, KW))], [gam]
    for direction in range(2):
        rf = rows_of[direction]
        col = lambda c, rf=rf: pl.BlockSpec((rb, KW), lambda b, i: (rf(b, i), c))
        in_specs += [col(direction), col(2)]
        args += [p, p]
        if with_out:
            in_specs += [col(3), col(0)]
            args += [p, do]
        in_specs += [pl.BlockSpec((cpb, HEADS, DK, DK), lambda b, i, rf=rf: (rf(b, i), 0, 0, 0)), col(0)]
        args += [(stash_f, stash_b)[direction], (bsum_f, bsum_b)[direction]]
    if has_end:
        in_specs.append(pl.BlockSpec((2, 1, HEADS, DK, DK), lambda b, i: (0, b, 0, 0, 0)))
        args.append(ds_end)
    out_shape, out_specs = [], []
    for direction in range(2):
        rf = rows_of[direction]
        width = (3 if with_out else 2) * KW
        out_shape.append(jax.ShapeDtypeStruct((rows, width), MXU_DTYPE))
        out_specs.append(pl.BlockSpec((rb, width), lambda b, i, rf=rf: (rf(b, i), 0)))
    out_shape += [jax.ShapeDtypeStruct((2, KW), F32), jax.ShapeDtypeStruct((2, nb_ex, HEADS, DK, DK), F32)]
    out_specs += [_full((2, KW)), pl.BlockSpec((2, 1, HEADS, DK, DK), lambda b, i: (0, b, 0, 0, 0))]
    res, handle = _host_call(body, name, (nb_ex, nb), in_specs, args, out_shape, out_specs,
                             [pltpu.VMEM((2, HEADS, DK, DK), F32)], after=after, sender=sender)
    return (*res, handle)


def _tail_forward(osum, og, u, v, ga, gb, gna, ln_g, ln_b, ws_ref, bs_ref, wpaT_ref, wpbT_ref):
    tm = osum.shape[0]
    gna4 = jnp.concatenate([gna] * HEADS, axis=1)
    r_parts = []
    for h in range(HEADS):
        oh = osum[:, h * DK:(h + 1) * DK]
        r_parts.append(jnp.broadcast_to(lax.rsqrt(jnp.mean(oh * oh, axis=-1, keepdims=True) + EPS), (tm, DK)))
    r = jnp.concatenate(r_parts, axis=1)
    on = osum * r
    sg_og = _sigmoid(og)
    silu_og = og * sg_og
    oan = on * gna4
    oa = oan * silu_og
    ug, tu = _gelu(u)
    vg, tv = _gelu(v)
    mu = jnp.mean(vg, axis=-1, keepdims=True)
    vc = vg - mu
    rstd = lax.rsqrt(jnp.mean(vc * vc, axis=-1, keepdims=True) + EPS)
    vhat = vc * rstd
    vln = vhat * ln_g + ln_b
    blocks = []
    for n in range(tm // SGU_BLOCK):
        rs = slice(n * SGU_BLOCK, (n + 1) * SGU_BLOCK)
        blocks.append(jnp.concatenate(
            [_dot(ws_ref[g], vln[rs, g * DK:(g + 1) * DK]) + bs_ref[g] for g in range(GROUPS)], axis=1))
    mixed = jnp.concatenate(blocks, axis=0) if len(blocks) > 1 else blocks[0]
    obm = ug * mixed
    pa = _dot(oa, wpaT_ref[...], "nt")
    pb = _dot(obm, wpbT_ref[...], "nt")
    sga, sgb = _sigmoid(ga), _sigmoid(gb)
    merged = sga * pa + sgb * pb
    return dict(r=r, on=on, sg_og=sg_og, silu_og=silu_og, oan=oan, oa=oa, ug=ug, tu=tu, tv=tv, rstd=rstd, vhat=vhat,
                vln=vln, mixed=mixed, obm=obm, pa=pa, pb=pb, sga=sga, sgb=sgb, merged=merged, gna4=gna4)


def _tail_in_specs(tm):
    tile = lambda c: pl.BlockSpec((tm, KW), lambda i: (i, c))
    return [tile(c) for c in range(4, 11)]


def _tail_weight_specs():
    return [_full((1, DK)), _full((1, KW)), _full((1, KW)), _full((GROUPS, SGU_BLOCK, SGU_BLOCK)),
            _full((GROUPS, SGU_BLOCK, 1)), _full((D, KW), single=True), _full((D, KW), single=True),
            _full((D, D), single=True)]


def _read_tail_inputs(of_ref, ob_ref, pcols):
    osum = of_ref[...] + ob_ref[...]
    og, u, v = (pcols[j][...].astype(F32) for j in range(3))
    ga = jnp.concatenate([pcols[3][...], pcols[4][...]], axis=1).astype(F32)
    gb = jnp.concatenate([pcols[5][...], pcols[6][...]], axis=1).astype(F32)
    return osum, og, u, v, ga, gb


def _tail_fwd(p, o_up, o_down, xt, modv, gna, ln_g, ln_b, w_s, b_s, w_paT, w_pbT, w_o, rows_per_example):
    rows = xt.shape[0]
    tm = min(TAIL_TILE, rows_per_example)
    per_b = rows_per_example // tm

    def body(of_ref, ob_ref, *rest):
        pcols = rest[:7]
        (x_ref, mod_ref, gna_ref, lng_ref, lnb_ref, ws_ref, bs_ref, wpaT_ref, wpbT_ref, wo_ref,
         x1_ref, mix_ref, merged_ref, oa_ref, obm_ref) = rest[7:]
        t = _tail_forward(*_read_tail_inputs(of_ref, ob_ref, pcols), gna_ref[...], lng_ref[...], lnb_ref[...],
                          ws_ref, bs_ref, wpaT_ref, wpbT_ref)
        mix = _dot(t["merged"], wo_ref[...])
        x1_ref[...] = x_ref[...] + mod_ref[0, 2:3, :] * mix
        mix_ref[...] = mix.astype(mix_ref.dtype)
        merged_ref[...] = t["merged"].astype(merged_ref.dtype)
        oa_ref[...] = t["oa"].astype(oa_ref.dtype)
        obm_ref[...] = t["obm"].astype(obm_ref.dtype)

    row = lambda w: pl.BlockSpec((tm, w), lambda i: (i, 0))
    in_specs = [row(KW), row(KW)] + _tail_in_specs(tm) + [row(D), pl.BlockSpec((1, N_MOD, D), lambda i: (i // per_b, 0, 0))]
    in_specs += _tail_weight_specs()
    return pl.pallas_call(
        body, name="tail_fwd", grid=(rows // tm,),
        out_shape=(jax.ShapeDtypeStruct((rows, D), F32), jax.ShapeDtypeStruct((rows, D), MXU_DTYPE),
                   jax.ShapeDtypeStruct((rows, D), MXU_DTYPE), jax.ShapeDtypeStruct((rows, KW), MXU_DTYPE),
                   jax.ShapeDtypeStruct((rows, KW), MXU_DTYPE)),
        in_specs=in_specs, out_specs=(row(D), row(D), row(D), row(KW), row(KW)),
        compiler_params=_params(("arbitrary",)),
    )(o_up, o_down, *([p] * 7), xt, modv, gna, ln_g, ln_b, w_s, b_s, w_paT, w_pbT, w_o)


def _tail_bwd(p, o_up, o_down, dx1, mix, modv, gna, ln_g, ln_b, w_s, b_s, w_paT, w_pbT, w_o, rows_per_example,
              after=None, sender=None):
    rows = dx1.shape[0]
    nb_ex = rows // rows_per_example
    tm = min(TAIL_TILE, rows_per_example)
    per_b = rows_per_example // tm

    def body(of_ref, ob_ref, *rest):
        pcols = rest[:7]
        (dx1_ref, mix_ref, mod_ref, gna_ref, lng_ref, lnb_ref, ws_ref, bs_ref, wpaT_ref, wpbT_ref, wo_ref,
         dpt_ref, do_ref, dmix_ref, dpa_ref, dpb_ref, dmod_ref, small_ref, dws_ref, dbs_ref) = rest[7:]
        i = pl.program_id(0)

        @pl.when(i == 0)
        def _():
            small_ref[...] = jnp.zeros_like(small_ref)
            dws_ref[...] = jnp.zeros_like(dws_ref)
            dbs_ref[...] = jnp.zeros_like(dbs_ref)

        @pl.when(i % per_b == 0)
        def _():
            dmod_ref[...] = jnp.zeros_like(dmod_ref)

        osum, og, u, v, ga, gb = _read_tail_inputs(of_ref, ob_ref, pcols)
        ln_g = lng_ref[...]
        t = _tail_forward(osum, og, u, v, ga, gb, gna_ref[...], ln_g, lnb_ref[...], ws_ref, bs_ref, wpaT_ref, wpbT_ref)
        dx1v = dx1_ref[...]
        dmod_ref[0, 2:3, :] += jnp.sum(dx1v * mix_ref[...].astype(F32), axis=0, keepdims=True)
        dmix = dx1v * mod_ref[0, 2:3, :]
        dmix_ref[...] = dmix.astype(dmix_ref.dtype)
        dmerged = _dot(dmix, wo_ref[...], "nt")
        sga, sgb = t["sga"], t["sgb"]
        dpa = dmerged * sga
        dpb = dmerged * sgb
        dpa_ref[...] = dpa.astype(dpa_ref.dtype)
        dpb_ref[...] = dpb.astype(dpb_ref.dtype)
        dga = dmerged * t["pa"] * sga * (1.0 - sga)
        dgb = dmerged * t["pb"] * sgb * (1.0 - sgb)
        doa = _dot(dpa, wpaT_ref[...])
        dobm = _dot(dpb, wpbT_ref[...])
        dug = dobm * t["mixed"]
        dmixed = dobm * t["ug"]
        du = dug * _gelu_grad(u, t["tu"])
        dvln_blocks = []
        for n in range(tm // SGU_BLOCK):
            rs = slice(n * SGU_BLOCK, (n + 1) * SGU_BLOCK)
            parts = []
            for g in range(GROUPS):
                gs = slice(g * DK, (g + 1) * DK)
                dm = dmixed[rs, gs]
                parts.append(_dot(ws_ref[g], dm, "tn"))
                dws_ref[g] += _dot(dm, t["vln"][rs, gs], "nt")
                dbs_ref[g] += jnp.sum(dm, axis=1, keepdims=True)
            dvln_blocks.append(jnp.concatenate(parts, axis=1))
        dvln = jnp.concatenate(dvln_blocks, axis=0) if len(dvln_blocks) > 1 else dvln_blocks[0]
        vhat = t["vhat"]
        small_ref[1:2, 0:KW] += jnp.sum(dvln * vhat, axis=0, keepdims=True)
        small_ref[2:3, 0:KW] += jnp.sum(dvln, axis=0, keepdims=True)
        dvhat = dvln * ln_g
        dvg = t["rstd"] * (dvhat - jnp.mean(dvhat, axis=-1, keepdims=True)
                           - vhat * jnp.mean(dvhat * vhat, axis=-1, keepdims=True))
        dv = dvg * _gelu_grad(v, t["tv"])
        sg_og = t["sg_og"]
        doan = doa * t["silu_og"]
        dog = doa * t["oan"] * (sg_og * (1.0 + og * (1.0 - sg_og)))
        prod = doan * t["on"]
        dgna = jnp.zeros((1, DK), F32)
        for h in range(HEADS):
            dgna = dgna + jnp.sum(prod[:, h * DK:(h + 1) * DK], axis=0, keepdims=True)
        small_ref[0:1, 0:DK] += dgna
        don = doan * t["gna4"]
        dot_parts = []
        for h in range(HEADS):
            hs = slice(h * DK, (h + 1) * DK)
            m = jnp.mean(don[:, hs] * t["on"][:, hs], axis=-1, keepdims=True)
            dot_parts.append(t["r"][:, hs] * (don[:, hs] - t["on"][:, hs] * m))
        do_ref[...] = jnp.concatenate(dot_parts, axis=1).astype(do_ref.dtype)
        for j, val in enumerate((dog, du, dv)):
            dpt_ref[:, j * KW:(j + 1) * KW] = val.astype(dpt_ref.dtype)
        dpt_ref[:, 3 * KW:3 * KW + D] = dga.astype(dpt_ref.dtype)
        dpt_ref[:, 3 * KW + D:] = dgb.astype(dpt_ref.dtype)

    row = lambda w: pl.BlockSpec((tm, w), lambda i: (i, 0))
    in_specs = [row(KW), row(KW)] + _tail_in_specs(tm) + [row(D), row(D), pl.BlockSpec((1, N_MOD, D), lambda i: (i // per_b, 0, 0))]
    in_specs += _tail_weight_specs()
    args = [o_up, o_down, *([p] * 7), dx1, mix, modv, gna, ln_g, ln_b, w_s, b_s, w_paT, w_pbT, w_o]
    cd = MXU_DTYPE
    res, handle = _host_call(
        body, "tail_bwd", (rows // tm,), in_specs, args,
        [jax.ShapeDtypeStruct((rows, TAIL_COLS), cd), jax.ShapeDtypeStruct((rows, KW), cd),
         jax.ShapeDtypeStruct((rows, D), cd), jax.ShapeDtypeStruct((rows, D), cd),
         jax.ShapeDtypeStruct((rows, D), cd), jax.ShapeDtypeStruct((nb_ex, 8, D), F32),
         jax.ShapeDtypeStruct((8, D), F32), jax.ShapeDtypeStruct((GROUPS, SGU_BLOCK, SGU_BLOCK), F32),
         jax.ShapeDtypeStruct((GROUPS, SGU_BLOCK, 1), F32)],
        [row(TAIL_COLS), row(KW), row(D), row(D), row(D),
         pl.BlockSpec((1, 8, D), lambda i: (i // per_b, 0, 0)), _full((8, D)),
         _full((GROUPS, SGU_BLOCK, SGU_BLOCK)), _full((GROUPS, SGU_BLOCK, 1))], [],
        after=after, sender=sender)
    return (*res, handle)


def _ffn(x1, target, modv, g_ffn, g_final, w_upT, w_down, rows_per_example):
    rows = x1.shape[0]
    nb_ex = rows // rows_per_example
    tm = min(TOKEN_TILE, rows_per_example)
    per_b = rows_per_example // tm
    n_ff = D_FF // FF_CHUNK

    def body(x1_ref, tgt_ref, mod_ref, gffn_ref, gfin_ref, wup_ref, wdn_ref,
             dx1_ref, h2_ref, dffn_ref, act_ref, dup_ref, dmod_ref, small_ref, up_scr):
        i = pl.program_id(0)

        @pl.when(i == 0)
        def _():
            small_ref[...] = jnp.zeros_like(small_ref)

        @pl.when(i % per_b == 0)
        def _():
            dmod_ref[...] = jnp.zeros_like(dmod_ref)

        x1v = x1_ref[...]
        g2 = gffn_ref[...]
        m3, m4, m5 = mod_ref[0, 3:4, :], mod_ref[0, 4:5, :], mod_ref[0, 5:6, :]
        r2 = lax.rsqrt(jnp.mean(x1v * x1v, axis=-1, keepdims=True) + EPS)
        xn2 = x1v * r2
        h2 = (xn2 * g2) * (1.0 + m4) + m3
        h2b = h2.astype(MXU_DTYPE)
        h2_ref[...] = h2b
        def up_pair(j):
            lo = j * FF_CHUNK
            return (_dot(h2b, wup_ref[lo:lo + FF_CHUNK, :], "nt"),
                    _dot(h2b, wup_ref[D_FF + lo:D_FF + lo + FF_CHUNK, :], "nt"))

        group_end = {min(e, n_ff): s for s, e in ((0, 4), (4, 8), (8, 12))}
        cur, ffn = up_pair(0), None
        for j in range(n_ff):
            nxt = up_pair(j + 1) if j + 1 < n_ff else None
            cs = slice(j * FF_CHUNK, (j + 1) * FF_CHUNK)
            a, bgate = cur
            up_scr[:, cs] = a
            up_scr[:, D_FF + j * FF_CHUNK:D_FF + (j + 1) * FF_CHUNK] = bgate
            act_ref[:, cs] = (a * _sigmoid(a) * bgate).astype(MXU_DTYPE)
            cur = nxt
            if j + 1 in group_end:
                gs = slice(group_end[j + 1] * FF_CHUNK, (j + 1) * FF_CHUNK)
                part = _dot(act_ref[:, gs], wdn_ref[gs, :])
                ffn = part if ffn is None else ffn + part
        x2 = x1v + m5 * ffn
        r3 = lax.rsqrt(jnp.mean(x2 * x2, axis=-1, keepdims=True) + EPS)
        xn3 = x2 * r3
        gf = gfin_ref[...]
        err = xn3 * gf - tgt_ref[...]
        loss = 0.5 * jnp.sum(jnp.mean(err * err, axis=-1, keepdims=True), axis=0, keepdims=True)
        small_ref[2:3, :] += jnp.broadcast_to(loss, (1, D))
        dy = err * (1.0 / D)
        small_ref[1:2, :] += jnp.sum(dy * xn3, axis=0, keepdims=True)
        dxn3 = dy * gf
        dx2 = r3 * (dxn3 - xn3 * jnp.mean(dxn3 * xn3, axis=-1, keepdims=True))
        dmod_ref[0, 5:6, :] += jnp.sum(dx2 * ffn, axis=0, keepdims=True)
        dffn = (dx2 * m5).astype(MXU_DTYPE)
        dffn_ref[...] = dffn
        dact_of = lambda j: _dot(dffn, wdn_ref[j * FF_CHUNK:(j + 1) * FF_CHUNK, :], "nt")
        cur, dh2 = dact_of(0), None
        for j in range(n_ff):
            nxt = dact_of(j + 1) if j + 1 < n_ff else None
            cs = slice(j * FF_CHUNK, (j + 1) * FF_CHUNK)
            a, bgate = up_scr[:, cs], up_scr[:, D_FF + j * FF_CHUNK:D_FF + (j + 1) * FF_CHUNK]
            s = _sigmoid(a)
            dup_ref[:, cs] = (cur * bgate * (s * (1.0 + a * (1.0 - s)))).astype(MXU_DTYPE)
            dup_ref[:, D_FF + j * FF_CHUNK:D_FF + (j + 1) * FF_CHUNK] = (cur * a * s).astype(MXU_DTYPE)
            cur = nxt
            if j + 1 in group_end:
                lo, hi = group_end[j + 1] * FF_CHUNK, (j + 1) * FF_CHUNK
                part = (_dot(dup_ref[:, lo:hi], wup_ref[lo:hi, :])
                        + _dot(dup_ref[:, D_FF + lo:D_FF + hi], wup_ref[D_FF + lo:D_FF + hi, :]))
                dh2 = part if dh2 is None else dh2 + part
        dmod_ref[0, 3:4, :] += jnp.sum(dh2, axis=0, keepdims=True)
        dmod_ref[0, 4:5, :] += jnp.sum(dh2 * xn2 * g2, axis=0, keepdims=True)
        small_ref[0:1, :] += jnp.sum(dh2 * (1.0 + m4) * xn2, axis=0, keepdims=True)
        dxn2 = dh2 * g2 * (1.0 + m4)
        dx1_ref[...] = dx2 + r2 * (dxn2 - xn2 * jnp.mean(dxn2 * xn2, axis=-1, keepdims=True))

    row = lambda w: pl.BlockSpec((tm, w), lambda i: (i, 0))
    cd = MXU_DTYPE
    return pl.pallas_call(
        body, name="ffn_fwd_bwd", grid=(rows // tm,),
        out_shape=(jax.ShapeDtypeStruct((rows, D), F32), jax.ShapeDtypeStruct((rows, D), cd),
                   jax.ShapeDtypeStruct((rows, D), cd), jax.ShapeDtypeStruct((rows, D_FF), cd),
                   jax.ShapeDtypeStruct((rows, 2 * D_FF), cd), jax.ShapeDtypeStruct((nb_ex, 8, D), F32),
                   jax.ShapeDtypeStruct((8, D), F32)),
        in_specs=[row(D), row(D), pl.BlockSpec((1, N_MOD, D), lambda i: (i // per_b, 0, 0)), _full((1, D)), _full((1, D)),
                  _full((2 * D_FF, D), single=True), _full((D_FF, D), single=True)],
        out_specs=(row(D), row(D), row(D), row(D_FF), row(2 * D_FF),
                   pl.BlockSpec((1, 8, D), lambda i: (i // per_b, 0, 0)), _full((8, D))),
        scratch_shapes=[pltpu.VMEM((tm, 2 * D_FF), F32)],
        compiler_params=_params(("arbitrary",)),
    )(x1, target, modv, g_ffn, g_final, w_upT, w_down)


def _scan_columns(up, down, n_groups):
    cols = [up[:, 0:KW].astype(F32), down[:, 0:KW].astype(F32)]
    for j in range(1, n_groups):
        cols.append(up[:, j * KW:(j + 1) * KW].astype(F32) + down[:, j * KW:(j + 1) * KW].astype(F32))
    return cols


def _inproj_bwd(d_up, d_down, dpt, xt, dx1, modv, g, w_inT, rows_per_example, name, sender=None):
    rows = xt.shape[0]
    latent = dx1 is not None
    n_cols = IN_COLS if latent else CTX_COLS
    n_groups = d_up.shape[1] // KW
    tm = min(PROJ_TILE, rows_per_example)
    per_b = rows_per_example // tm
    n_mod_blocks = rows // rows_per_example if latent else 1

    def body(*refs):
        it = iter(refs)
        up_ref, down_ref = next(it), next(it)
        dpt_ref = next(it) if latent else None
        x_ref = next(it)
        dx1_ref = next(it) if latent else None
        mod_ref, g_ref, w_ref = next(it), next(it), next(it)
        gx_ref = next(it) if latent else None
        dp_out = None if latent else next(it)
        dmod_ref, small_ref = next(it), next(it)
        dp_ref = next(it) if latent else dp_out
        i = pl.program_id(0)

        @pl.when(i == 0)
        def _():
            small_ref[...] = jnp.zeros_like(small_ref)

        @pl.when((i % per_b == 0) if latent else (i == 0))
        def _():
            dmod_ref[...] = jnp.zeros_like(dmod_ref)

        for j, val in enumerate(_scan_columns(up_ref[...], down_ref[...], n_groups)):
            dp_ref[:, j * KW:(j + 1) * KW] = val.astype(MXU_DTYPE)
        if latent:
            dh = _dot(dp_ref[...], w_ref[0:4 * KW, :]) + _dot(dpt_ref[...], w_ref[4 * KW:, :])
        else:
            dh = _dot(dp_ref[...], w_ref[...])
        x = x_ref[...]
        gv = g_ref[...]
        m1 = mod_ref[0, 1:2, :]
        r = lax.rsqrt(jnp.mean(x * x, axis=-1, keepdims=True) + EPS)
        xn = x * r
        dmod_ref[0, 0:1, :] += jnp.sum(dh, axis=0, keepdims=True)
        dmod_ref[0, 1:2, :] += jnp.sum(dh * xn * gv, axis=0, keepdims=True)
        small_ref[0:1, :] += jnp.sum(dh * (1.0 + m1) * xn, axis=0, keepdims=True)
        if latent:
            dxn = dh * gv * (1.0 + m1)
            gx_ref[...] = dx1_ref[...] + r * (dxn - xn * jnp.mean(dxn * xn, axis=-1, keepdims=True))

    row = lambda w: pl.BlockSpec((tm, w), lambda i: (i, 0))
    mod_idx = (lambda i: (i // per_b, 0, 0)) if latent else (lambda i: (0, 0, 0))
    in_specs = [row(n_groups * KW)] * 2 + ([row(TAIL_COLS)] if latent else []) + [row(D)] + ([row(D)] if latent else [])
    in_specs += [pl.BlockSpec((1, N_MOD, D), mod_idx), _full((1, D)),
                 pl.BlockSpec((n_cols, D), lambda i: (0, 0), pipeline_mode=pl.Buffered(1))]
    args = [d_up, d_down] + ([dpt] if latent else []) + [xt] + ([dx1] if latent else []) + [modv, g, w_inT]
    first = jax.ShapeDtypeStruct((rows, D), F32) if latent else jax.ShapeDtypeStruct((rows, n_cols), MXU_DTYPE)
    out_shape = [first, jax.ShapeDtypeStruct((n_mod_blocks, 8, D), F32), jax.ShapeDtypeStruct((8, D), F32)]
    out_specs = [row(D) if latent else row(n_cols), pl.BlockSpec((1, 8, D), mod_idx), _full((8, D))]
    scratch = [pltpu.VMEM((tm, 4 * KW), MXU_DTYPE)] if latent else []
    res, handle = _host_call(body, name, (rows // tm,), in_specs, args, out_shape, out_specs, scratch, sender=sender)
    return (*res, handle)


def _grad_matmul(a, b, name, init=None, tn=512, sender=None):
    rows, n = a.shape
    k = b.shape[1]
    tn = min(tn, n)
    has_init = init is not None
    init_blocks = init.shape[0] // tn if has_init else 0

    def body(*refs):
        if has_init:
            a_ref, b_ref, init_ref, o_ref = refs
        else:
            a_ref, b_ref, o_ref = refs
        g = _dot(a_ref[...], b_ref[...], "tn")
        if has_init:
            g = g + jnp.where(pl.program_id(0) < init_blocks, init_ref[...].astype(F32), 0.0)
        o_ref[...] = g.astype(o_ref.dtype)

    in_specs = [pl.BlockSpec((rows, tn), lambda i: (0, i)), _full((rows, k), single=True)]
    args = [a, b]
    if has_init:
        in_specs.append(pl.BlockSpec((tn, k), lambda i: (jnp.minimum(i, init_blocks - 1), 0)))
        args.append(init)
    (out,), handle = _host_call(
        body, name, (n // tn,), in_specs, args, [jax.ShapeDtypeStruct((n, k), PAYLOAD_DTYPE)],
        [pl.BlockSpec((tn, k), lambda i: (i, 0))], [], sender=sender)
    return out, handle


def _grad_in(d_up, d_down, dpt, h, init, sender=None):
    rows = h.shape[0]
    tn = 256
    per_group = KW // tn
    n_scan = 4 * per_group
    init_blocks = init.shape[0] // tn

    def body(up_ref, down_ref, dpt_ref, h_ref, init_ref, o_ref):
        i = pl.program_id(0)
        both = (up_ref[...].astype(F32) + down_ref[...].astype(F32)).astype(MXU_DTYPE)
        a = jnp.where(i < per_group, up_ref[...],
                      jnp.where(i < 2 * per_group, down_ref[...], jnp.where(i < n_scan, both, dpt_ref[...])))
        g = _dot(a, h_ref[...], "tn") + jnp.where(i < init_blocks, init_ref[...].astype(F32), 0.0)
        o_ref[...] = g.astype(o_ref.dtype)

    last = 3 * per_group - 1
    col = lambda f: pl.BlockSpec((rows, tn), lambda i: (0, f(i)))
    in_specs = [col(lambda i: jnp.clip(jnp.where(i < per_group, i, i - per_group), 0, last)),
                col(lambda i: jnp.clip(i - per_group, 0, last)),
                col(lambda i: jnp.clip(i - n_scan, 0, TAIL_COLS // tn - 1)),
                _full((rows, D), single=True),
                pl.BlockSpec((tn, D), lambda i: (jnp.minimum(i, init_blocks - 1), 0))]
    (out,), handle = _host_call(
        body, "gw_in", (IN_COLS // tn,), in_specs, [d_up, d_down, dpt, h, init],
        [jax.ShapeDtypeStruct((IN_COLS, D), PAYLOAD_DTYPE)], [pl.BlockSpec((tn, D), lambda i: (i, 0))], [],
        sender=sender)
    return out, handle


def _row_tile(rows, limit=256):
    if rows <= limit:
        return rows
    for t in range(limit, 7, -8):
        if rows % t == 0:
            return t
    return rows


def _sum8(stack, name):
    _, rows, cols = stack.shape
    tr = _row_tile(rows)

    def body(s_ref, o_ref):
        acc = s_ref[0].astype(F32)
        for j in range(1, N_DEV):
            acc = acc + s_ref[j].astype(F32)
        o_ref[...] = acc

    return pl.pallas_call(
        body, name=name, grid=(rows // tr,), out_shape=jax.ShapeDtypeStruct((rows, cols), F32),
        in_specs=[pl.BlockSpec((N_DEV, tr, cols), lambda i: (0, i, 0))],
        out_specs=pl.BlockSpec((tr, cols), lambda i: (i, 0)),
        compiler_params=_params(("arbitrary",)),
    )(stack)


def _adamw_update(w, gv, m, v):
    nm = ADAM_B1 * m + (1.0 - ADAM_B1) * gv
    nv = ADAM_B2 * v + (1.0 - ADAM_B2) * (gv * gv)
    m_hat = nm / (1.0 - ADAM_B1 ** ADAM_STEP)
    v_hat = nv / (1.0 - ADAM_B2 ** ADAM_STEP)
    return -ADAM_LR * (m_hat / (jnp.sqrt(v_hat) + ADAM_EPS) + ADAM_WD * w), nm, nv


SMALL_PARAMS = (("g_mix", 0, D), ("g_ffn", 1, D), ("g_final", 2, D), ("g_norm_a", 3, DK), ("ln_v_g", 4, KW),
                ("ln_v_b", 5, KW), ("b_s", 6, GROUPS * SGU_BLOCK))


def _small_finish(early, late, dws, gam, nb_ex, params):
    names = [n for n, _, _ in SMALL_PARAMS] + ["b_mod", "w_s"]

    def body(*refs):
        s_ref, l_ref, dws_ref, gam_ref = refs[:4]
        p_refs = refs[4:4 + 3 * len(names)]
        tot_ref, dgam_ref = refs[4 + 3 * len(names):6 + 3 * len(names)]
        o_refs = refs[6 + 3 * len(names):]
        acc = s_ref[0] + l_ref[0]
        gws = dws_ref[0]
        for j in range(1, N_DEV):
            acc = acc + (s_ref[j] + l_ref[j])
            gws = gws + dws_ref[j]
        tot_ref[...] = acc
        bm = acc[8:8 + N_MOD, :]
        for e in range(nb_ex):
            bm = bm + acc[16 + e * N_MOD:16 + (e + 1) * N_MOD, :]
        lb = jnp.concatenate([_lower_bound(gam_ref, 0), _lower_bound(gam_ref, 1)], axis=1)
        dgam = acc[7:8, :] * lb * (1.0 - lb)
        dgam_ref[...] = jnp.concatenate([dgam, -dgam], axis=0)
        grads = [acc[row:row + 1, 0:width] for _, row, width in SMALL_PARAMS] + [bm, gws]
        for k, g in enumerate(grads):
            w_ref, m_ref, v_ref = p_refs[3 * k:3 * k + 3]
            o_refs[4 * k][...] = g
            o_refs[4 * k + 1][...], o_refs[4 * k + 2][...], o_refs[4 * k + 3][...] = _adamw_update(
                w_ref[...], g, m_ref[...], v_ref[...])

    p_args, p_specs, o_shapes, o_specs = [], [], [], []
    for n in names:
        for a in params[n]:
            p_args.append(a)
            p_specs.append(_full(a.shape))
        o_shapes += [jax.ShapeDtypeStruct(params[n][0].shape, F32)] * 4
        o_specs += [_full(params[n][0].shape)] * 4
    res = pl.pallas_call(
        body, name="small_finish", grid=(1,),
        out_shape=[jax.ShapeDtypeStruct((SMALL_ROWS, D), F32), jax.ShapeDtypeStruct((2, D), F32)] + o_shapes,
        in_specs=[_full(early.shape), _full(late.shape), _full(dws.shape), _full((4, KW))] + p_specs,
        out_specs=[_full((SMALL_ROWS, D)), _full((2, D))] + o_specs,
        compiler_params=_params(("arbitrary",)),
    )(early, late, dws, gam, *p_args)
    return res[0], res[1], {n: res[2 + 4 * k:6 + 4 * k] for k, n in enumerate(names)}


def _adamw_sum8(stack, w, m, v, name):
    _, rows, cols = stack.shape
    tr = _row_tile(rows)

    def body(s_ref, w_ref, m_ref, v_ref, g_ref, d_ref, nm_ref, nv_ref):
        gv = s_ref[0].astype(F32)
        for j in range(1, N_DEV):
            gv = gv + s_ref[j].astype(F32)
        g_ref[...] = gv
        d_ref[...], nm_ref[...], nv_ref[...] = _adamw_update(w_ref[...], gv, m_ref[...], v_ref[...])

    blk = pl.BlockSpec((tr, cols), lambda i: (i, 0))
    sd = jax.ShapeDtypeStruct((rows, cols), F32)
    return pl.pallas_call(
        body, name=name, grid=(rows // tr,), out_shape=(sd, sd, sd, sd),
        in_specs=[pl.BlockSpec((N_DEV, tr, cols), lambda i: (0, i, 0)), blk, blk, blk], out_specs=(blk, blk, blk, blk),
        compiler_params=_params(("arbitrary",)),
    )(stack, w, m, v)


def _adamw(w, g, m, v, name):
    shape = w.shape
    cols = shape[-1]
    rows = 1
    for s in shape[:-1]:
        rows *= s
    tr = _row_tile(rows)

    def body(w_ref, g_ref, m_ref, v_ref, d_ref, nm_ref, nv_ref):
        gv = g_ref[...]
        nm = ADAM_B1 * m_ref[...] + (1.0 - ADAM_B1) * gv
        nv = ADAM_B2 * v_ref[...] + (1.0 - ADAM_B2) * (gv * gv)
        m_hat = nm / (1.0 - ADAM_B1 ** ADAM_STEP)
        v_hat = nv / (1.0 - ADAM_B2 ** ADAM_STEP)
        d_ref[...] = -ADAM_LR * (m_hat / (jnp.sqrt(v_hat) + ADAM_EPS) + ADAM_WD * w_ref[...])
        nm_ref[...] = nm
        nv_ref[...] = nv

    blk = pl.BlockSpec((tr, cols), lambda i: (i, 0))
    sd = jax.ShapeDtypeStruct((rows, cols), F32)
    d, nm, nv = pl.pallas_call(
        body, name=name, grid=(rows // tr,), out_shape=(sd, sd, sd), in_specs=[blk] * 4, out_specs=(blk, blk, blk),
        compiler_params=_params(("arbitrary",)),
    )(w.reshape(rows, cols), g.reshape(rows, cols), m.reshape(rows, cols), v.reshape(rows, cols))
    return d.reshape(shape), nm.reshape(shape), nv.reshape(shape)


def _owner_blocks(a):
    return a.reshape(N_DEV, a.shape[0] // N_DEV, a.shape[1])


class _LocalWeights:
    def __init__(self, w_upT, w_down, w_o, w_paT, w_pbT):
        self.weights = (w_upT, w_down, w_o, w_paT, w_pbT)
        self.items = {}

    def sender(self, stage, items=None):
        self.items[stage] = items
        return None

    def sent(self, stage, handle):
        pass

    def mixer_weights(self, after):
        return self.weights[1:]

    def ffn_weights(self, after):
        return self.weights[0]


def _local_step(x, ctx, target, modv, mcv, gam, g_mix, g_ffn, gna, ln_g, ln_b, w_s, b_s, g_final, w_inT, comm):
    nb_ex, seq, _ = x.shape
    ctx_len = ctx.shape[1]
    xt = x.reshape(nb_ex * seq, D)
    ct = ctx.reshape(nb_ex * ctx_len, D)
    tgt = target.reshape(nb_ex * seq, D)
    bs3 = b_s.reshape(GROUPS, SGU_BLOCK, 1)

    pc, hc, _ = _inproj(ct, mcv, g_mix, w_inT, CTX_COLS, ctx_len, "inproj_ctx")
    p, h, handle = _inproj(xt, modv, g_mix, w_inT, IN_COLS, seq, "inproj_lat", sender=comm.sender("inproj"))
    comm.sent("inproj", handle)
    cst_f, cst_b, cb_f, cb_b, s_ctx, _ = _hgrn_fwd(pc, gam, None, ctx_len, False, "hgrn_fwd_ctx")
    o_up, o_down, st_f, st_b, b_f, b_b, _, handle = _hgrn_fwd(p, gam, s_ctx, seq, True, "hgrn_fwd_lat",
                                                              sender=comm.sender("scan"))
    comm.sent("scan", handle)
    w_down, w_o, w_paT, w_pbT = comm.mixer_weights(o_up)
    x1, mix, merged, oa, obm = _tail_fwd(p, o_up, o_down, xt, modv, gna, ln_g, ln_b, w_s, bs3, w_paT, w_pbT, w_o, seq)
    w_upT = comm.ffn_weights(x1)
    dx1, h2, dffn, act, dup, dmod_ffn, small_ffn = _ffn(x1, tgt, modv, g_ffn, g_final, w_upT, w_down, seq)
    gw_upT, _ = _grad_matmul(dup, h2, "gw_up")
    gw_down, _ = _grad_matmul(act, dffn, "gw_down", tn=256)
    scatter = lambda *grads: [(_owner_blocks(g), "scatter") for g in grads]
    dpt, do, dmix, dpa, dpb, dmod_tail, small_tail, dws, dbs, handle = _tail_bwd(
        p, o_up, o_down, dx1, mix, modv, gna, ln_g, ln_b, w_s, bs3, w_paT, w_pbT, w_o, seq,
        sender=comm.sender("tail_bwd", scatter(gw_upT)))
    comm.sent("tail_bwd", handle)
    gw_o, _ = _grad_matmul(merged, dmix, "gw_o")
    gw_paT, _ = _grad_matmul(dpa, oa, "gw_pa")
    gw_pbT, _ = _grad_matmul(dpb, obm, "gw_pb")
    def at_row(row, a):
        return jnp.pad(a, ((row, SMALL_ROWS - row - a.shape[0]), (0, D - a.shape[1])))

    small_early = (at_row(1, small_ffn[0:2])
                   + at_row(3, small_tail[0:3])
                   + at_row(6, dbs.reshape(1, GROUPS * SGU_BLOCK))
                   + at_row(14, small_ffn[2:3]))
    dws_rows = dws.reshape(GROUPS * SGU_BLOCK, SGU_BLOCK)
    d_up, d_down, dlb, ds0, handle = _hgrn_bwd(
        p, gam, do, st_f, st_b, b_f, b_b, None, seq, True, "hgrn_bwd_lat",
        sender=comm.sender("scan_bwd", scatter(gw_down, gw_o, gw_paT, gw_pbT)
                           + [(small_early, "gather"), (dws_rows, "gather")]))
    comm.sent("scan_bwd", handle)
    c_up, c_down, dlb_c, _, _ = _hgrn_bwd(pc, gam, None, cst_f, cst_b, cb_f, cb_b, ds0, ctx_len, False, "hgrn_bwd_ctx")
    dpc, dmc, small_c, _ = _inproj_bwd(c_up, c_down, None, ct, None, mcv, g_mix, w_inT, ctx_len, "inproj_bwd_ctx")
    gw_inT, _ = _grad_in(d_up, d_down, dpt, h, _grad_matmul(dpc, hc, "gw_in_ctx")[0])
    grad_x, dmod_in, small_in, handle = _inproj_bwd(d_up, d_down, dpt, xt, dx1, modv, g_mix, w_inT, seq,
                                                   "inproj_bwd_lat", sender=comm.sender("inproj_bwd", scatter(gw_inT)))
    comm.sent("inproj_bwd", handle)
    dmod = dmod_in + dmod_tail + dmod_ffn
    small_late = (at_row(0, small_in[0:1] + small_c[0:1])
                  + at_row(7, (dlb + dlb_c).reshape(1, 2 * KW))
                  + at_row(8, dmc[0, 0:N_MOD])
                  + at_row(16, dmod[:, 0:N_MOD].reshape(nb_ex * N_MOD, D)))
    comm.sender("last", [(small_late, "gather")])
    return grad_x.reshape(x.shape)


def kernel(x, c, ctx, c_ctx, w_mod, b_mod, g_mix, g_ffn, w_in, lb_gamma, g_norm_a, ln_v_g, ln_v_b, w_s, b_s, w_pa, w_pb, w_o, w_up, w_down, g_final, loss_target, m_c_ctx, m_w_mod, m_b_mod, m_g_mix, m_g_ffn, m_w_in, m_lb_gamma, m_g_norm_a, m_ln_v_g, m_ln_v_b, m_w_s, m_b_s, m_w_pa, m_w_pb, m_w_o, m_w_up, m_w_down, m_g_final, v_c_ctx, v_w_mod, v_b_mod, v_g_mix, v_g_ffn, v_w_in, v_lb_gamma, v_g_norm_a, v_ln_v_g, v_ln_v_b, v_w_s, v_b_s, v_w_pa, v_w_pb, v_w_o, v_w_up, v_w_down, v_g_final):
    nb_ex = x.shape[0]
    me = 4 * lax.axis_index("x") + 2 * lax.axis_index("y") + lax.axis_index("c")
    cd = MXU_DTYPE
    mod_cols = w_mod.shape[2]
    lb_cols = lb_gamma.shape[2]

    w_inT_l = w_in[0].T.astype(cd)
    w_upT_l = w_up[0].T.astype(cd)
    w_paT_l = w_pa[0].T.astype(cd)
    w_pbT_l = w_pb[0].T.astype(cd)
    cl = jnp.concatenate([c, jnp.pad(lb_gamma.reshape(1, 4 * lb_cols), ((0, 0), (0, D - 4 * lb_cols))),
                          jnp.zeros((8 - nb_ex - 1, D), F32)], axis=0)
    g_in, g_cl = _gather_two_level([w_inT_l, cl], "gather_w_in")
    w_inT = g_in.reshape(IN_COLS, D)
    c_all = g_cl[:, 0:nb_ex].reshape(N_DEV * nb_ex, D)
    gam = jnp.transpose(g_cl[:, nb_ex, 0:4 * lb_cols].reshape(N_DEV, 4, lb_cols), (1, 0, 2)).reshape(4, KW)

    n_c = N_DEV * nb_ex
    cvec = jnp.concatenate([c_all, c_ctx.reshape(1, D), jnp.zeros((7, D), F32)], axis=0)
    b_mod_l = lax.dynamic_slice(b_mod, (0, me * mod_cols), (1, mod_cols))
    mod_l, svec = _mod_fwd(cvec, w_mod[0], b_mod_l)
    (g_mod,) = _gather_two_level([mod_l], "gather_mod")
    mod_all = jnp.transpose(g_mod, (1, 0, 2)).reshape(n_c + 8, N_MOD * D)
    modv = lax.dynamic_slice(mod_all, (me * nb_ex, 0), (nb_ex, N_MOD * D)).reshape(nb_ex, N_MOD, D)
    mcv = mod_all[n_c].reshape(1, N_MOD, D)

    handles, leftover = {}, {}

    class Comm:
        def sender(self, stage, items=None):
            if stage == "inproj":
                return _Sender([(w_down[0].astype(cd), "gather"), (w_o[0].astype(cd), "gather"), (w_paT_l, "gather"),
                                (w_pbT_l, "gather")])
            if stage == "scan":
                return _Sender([(w_upT_l, "gather")])
            if stage == "last":
                leftover["items"] = items
                return None
            return _Sender(items)

        def sent(self, stage, handle):
            handles[stage] = handle

        def mixer_weights(self, after):
            g_down, g_o, g_pa, g_pb = _exchange_wait(handles["inproj"], after)
            return g_down.reshape(D_FF, D), g_o.reshape(D, D), g_pa.reshape(D, KW), g_pb.reshape(D, KW)

        def ffn_weights(self, after):
            (g_up,) = _exchange_wait(handles["scan"], after)
            return g_up.reshape(2 * D_FF, D)

    grad_x = _local_step(
        x, ctx, loss_target, modv, mcv, gam, g_mix, g_ffn, g_norm_a, ln_v_g, ln_v_b, w_s[0], b_s[0],
        g_final.reshape(1, D), w_inT, Comm())
    last, last_started = _exchange_start(leftover["items"], "gather_small_late", after=leftover["items"][0][0])

    (r_up,) = _exchange_wait(handles["tail_bwd"], last_started)
    r_down, r_o, r_pa, r_pb, r_small, r_dws = _exchange_wait(handles["scan_bwd"], r_up)
    raw_up = _adamw_sum8(r_up, w_up[0].T, m_w_up[0].T, v_w_up[0].T, "adamw_w_up")
    raw_down = _adamw_sum8(r_down, w_down[0], m_w_down[0], v_w_down[0], "adamw_w_down")
    raw_o = _adamw_sum8(r_o, w_o[0], m_w_o[0], v_w_o[0], "adamw_w_o")
    (r_in,) = _exchange_wait(handles["inproj_bwd"], raw_up[1])
    raw_in = _adamw_sum8(r_in, w_in[0].T, m_w_in[0].T, v_w_in[0].T, "adamw_w_in")
    (r_late,) = _exchange_wait(last, raw_in[1])
    done = {"w_in": [a.T[None] for a in raw_in], "w_up": [a.T[None] for a in raw_up],
            "w_down": [a[None] for a in raw_down], "w_o": [a[None] for a in raw_o]}
    grad_w_in, grad_w_up, grad_w_down, grad_w_o = (done[k][0] for k in ("w_in", "w_up", "w_down", "w_o"))
    grad_w_pa = _sum8(r_pa, "sum_w_pa").T[None]
    grad_w_pb = _sum8(r_pb, "sum_w_pb").T[None]
    as_2d = {"g_final": (1, D), "b_s": (1, GROUPS * SGU_BLOCK), "b_mod": (N_MOD, D), "w_s": (GROUPS * SGU_BLOCK, SGU_BLOCK)}
    small_params = {"g_mix": (g_mix, m_g_mix, v_g_mix), "g_ffn": (g_ffn, m_g_ffn, v_g_ffn),
                    "g_final": (g_final, m_g_final, v_g_final), "g_norm_a": (g_norm_a, m_g_norm_a, v_g_norm_a),
                    "ln_v_g": (ln_v_g, m_ln_v_g, v_ln_v_g), "ln_v_b": (ln_v_b, m_ln_v_b, v_ln_v_b),
                    "b_s": (b_s, m_b_s, v_b_s), "b_mod": (b_mod, m_b_mod, v_b_mod), "w_s": (w_s, m_w_s, v_w_s)}
    tot, dgam, small_done = _small_finish(
        r_small, r_late, r_dws, gam, nb_ex,
        {n: tuple(a.reshape(as_2d.get(n, a.shape)) for a in wmv) for n, wmv in small_params.items()})
    for n, outs in small_done.items():
        done[n] = [a.reshape(small_params[n][0].shape) for a in outs]
    loss = tot[14, 0]
    grad_g_mix, grad_g_ffn, grad_g_final, grad_g_norm_a, grad_ln_v_g, grad_ln_v_b, grad_b_s, grad_b_mod, grad_w_s = (
        done[n][0] for n in ("g_mix", "g_ffn", "g_final", "g_norm_a", "ln_v_g", "ln_v_b", "b_s", "b_mod", "w_s"))
    grad_lb_gamma = lax.dynamic_slice(dgam.reshape(2, 2, KW), (0, 0, me * lb_cols), (2, 2, lb_cols))

    dmod_all = r_late[:, 16:16 + nb_ex * N_MOD].reshape(n_c, N_MOD * D)
    dmod_l = jnp.concatenate([lax.dynamic_slice(dmod_all, (0, me * mod_cols), (n_c, mod_cols)),
                              lax.dynamic_slice(tot[8:8 + N_MOD].reshape(1, N_MOD * D), (0, me * mod_cols), (1, mod_cols)),
                              jnp.zeros((7, mod_cols), F32)], axis=0)
    gw_mod, gc = _mod_bwd(svec, cvec, dmod_l, w_mod[0])
    grad_w_mod = gw_mod[None]
    (r_gc,) = _exchange([(gc[n_c:n_c + 8], "gather")], "gather_c_ctx", after=r_late)
    grad_c_ctx = _sum8(r_gc, "sum_c_ctx")[0]

    names = ["c_ctx", "w_mod", "b_mod", "g_mix", "g_ffn", "w_in", "lb_gamma", "g_norm_a", "ln_v_g", "ln_v_b", "w_s",
             "b_s", "w_pa", "w_pb", "w_o", "w_up", "w_down", "g_final"]
    weights = [c_ctx, w_mod, b_mod, g_mix, g_ffn, w_in, lb_gamma, g_norm_a, ln_v_g, ln_v_b, w_s, b_s, w_pa, w_pb, w_o,
               w_up, w_down, g_final]
    grads = [grad_c_ctx, grad_w_mod, grad_b_mod, grad_g_mix, grad_g_ffn, grad_w_in, grad_lb_gamma, grad_g_norm_a,
             grad_ln_v_g, grad_ln_v_b, grad_w_s, grad_b_s, grad_w_pa, grad_w_pb, grad_w_o, grad_w_up, grad_w_down,
             grad_g_final]
    ms = [m_c_ctx, m_w_mod, m_b_mod, m_g_mix, m_g_ffn, m_w_in, m_lb_gamma, m_g_norm_a, m_ln_v_g, m_ln_v_b, m_w_s, m_b_s,
          m_w_pa, m_w_pb, m_w_o, m_w_up, m_w_down, m_g_final]
    vs = [v_c_ctx, v_w_mod, v_b_mod, v_g_mix, v_g_ffn, v_w_in, v_lb_gamma, v_g_norm_a, v_ln_v_g, v_ln_v_b, v_w_s, v_b_s,
          v_w_pa, v_w_pb, v_w_o, v_w_up, v_w_down, v_g_final]
    deltas, new_ms, new_vs = [], [], []
    for nm, w, g, m, v in zip(names, weights, grads, ms, vs):
        d, nm_, nv_ = done[nm][1:] if nm in done else _adamw(w, g.reshape(w.shape), m, v, "adamw_" + nm)
        deltas.append(d)
        new_ms.append(nm_)
        new_vs.append(nv_)
    grads = [g.reshape(w.shape) for g, w in zip(grads, weights)]
    return (loss, grad_x, *grads, *deltas, *new_ms, *new_vs)
```

```python
import functools

import jax
import jax.numpy as jnp
from jax import lax
from jax.experimental import pallas as pl
from jax.experimental.pallas import tpu as pltpu

F32 = jnp.float32
MXU_DTYPE = jnp.bfloat16
PAYLOAD_DTYPE = jnp.bfloat16

N_DEV = 8
D = 1024
HEADS = 4
DK = 128
KW = HEADS * DK
CHUNK = 64
SGU_BLOCK = 128
GROUPS = 4
D_FF = 2816
FF_CHUNK = 256
N_MOD = 6
IN_COLS = 5632
CTX_COLS = 1536
TAIL_COLS = IN_COLS - 4 * KW
EPS = 1e-6
ADAM_LR, ADAM_B1, ADAM_B2, ADAM_EPS, ADAM_WD, ADAM_STEP = 0.001, 0.9, 0.999, 1e-08, 0.01, 10

VMEM_LIMIT = 56 * 1024 * 1024
TOKEN_TILE = 256
PROJ_TILE = 512
TAIL_TILE = 512
SMALL_ROWS = 40


def _params(sem):
    return pltpu.CompilerParams(dimension_semantics=sem, vmem_limit_bytes=VMEM_LIMIT)


_DN = {"nn": (((1,), (0,)), ((), ())), "nt": (((1,), (1,)), ((), ())), "tn": (((0,), (0,)), ((), ()))}


def _dot(a, b, form="nn"):
    return lax.dot_general(a.astype(MXU_DTYPE), b.astype(MXU_DTYPE), _DN[form], preferred_element_type=F32)


def _mask_dot(mask, v):
    bf = jnp.bfloat16
    hi = v.astype(bf)
    mid = (v - hi.astype(F32)).astype(bf)
    w = v.shape[1]
    s = lax.dot_general(mask.astype(bf), jnp.concatenate([hi, mid], axis=1), _DN["nn"], preferred_element_type=F32)
    return s[:, w:] + s[:, :w]


def _full(shape, single=False):
    n = len(shape)
    if single:
        return pl.BlockSpec(shape, lambda *_: (0,) * n, pipeline_mode=pl.Buffered(1))
    return pl.BlockSpec(shape, lambda *_: (0,) * n)


def _ordered_behind(body, in_specs, args, after):
    if after is None:
        return body
    at = len(in_specs)
    in_specs.append(pl.BlockSpec(memory_space=pl.ANY))
    args.append(after)
    return lambda *refs: body(*refs[:at], *refs[at + 1:])


def _sigmoid(z):
    return 0.5 * jnp.tanh(0.5 * z) + 0.5


def _gelu(x):
    c = 0.7978845608028654
    t = jnp.tanh(c * (x + 0.044715 * x * x * x))
    return 0.5 * x * (1.0 + t), t


def _gelu_grad(x, t):
    c = 0.7978845608028654
    return 0.5 * (1.0 + t) + 0.5 * x * (1.0 - t * t) * c * (1.0 + 3 * 0.044715 * x * x)


def _exchange(items, name, after=None):
    n = len(items)
    out_shape = []
    for a, mode in items:
        blk = a.shape if mode == "gather" else a.shape[1:]
        out_shape.append(jax.ShapeDtypeStruct((N_DEV,) + tuple(blk), a.dtype))

    def body(*refs):
        srcs, dsts = refs[:n], refs[n:2 * n]
        send_sems, recv_sems, local_sems = refs[2 * n:]
        x, y, c = lax.axis_index("x"), lax.axis_index("y"), lax.axis_index("c")
        me = 4 * x + 2 * y + c

        def src_for(i, dev):
            return srcs[i] if items[i][1] == "gather" else srcs[i].at[dev]

        local = [pltpu.make_async_copy(src_for(i, me), dsts[i].at[me], local_sems.at[i]) for i in range(n)]
        for cp in local:
            cp.start()
        remote = []
        for k in range(1, N_DEV):
            px = jnp.bitwise_xor(x, (k >> 2) & 1)
            py = jnp.bitwise_xor(y, (k >> 1) & 1)
            pc = jnp.bitwise_xor(c, k & 1)
            peer = 4 * px + 2 * py + pc
            for i in range(n):
                cp = pltpu.make_async_remote_copy(
                    src_ref=src_for(i, peer), dst_ref=dsts[i].at[me],
                    send_sem=send_sems.at[i * (N_DEV - 1) + k - 1], recv_sem=recv_sems.at[i * (N_DEV - 1) + k - 1],
                    device_id=(px, py, pc), device_id_type=pl.DeviceIdType.MESH)
                cp.start()
                remote.append(cp)
        for cp in remote:
            cp.wait()
        for cp in local:
            cp.wait()

    any_spec = pl.BlockSpec(memory_space=pl.ANY)
    in_specs, args = [any_spec] * n, [a for a, _ in items]
    if after is not None:
        in_specs.append(any_spec)
        args.append(after)
        exchange = body
        body = lambda *refs: exchange(*refs[:n], *refs[n + 1:])
    return pl.pallas_call(
        body, name=name, out_shape=out_shape, in_specs=in_specs, out_specs=[any_spec] * n,
        scratch_shapes=[pltpu.SemaphoreType.DMA((n * (N_DEV - 1),)), pltpu.SemaphoreType.DMA((n * (N_DEV - 1),)),
                        pltpu.SemaphoreType.DMA((n,))],
    )(*args)


def _gather_two_level(arrays, name):
    n = len(arrays)
    pieces = []
    for i, a in enumerate(arrays):
        rows = _Sender.PIECE_ROWS if a.shape[0] % _Sender.PIECE_ROWS == 0 else a.shape[0]
        pieces += [(i, r0, rows) for r0 in range(0, a.shape[0], rows)]

    def body(*refs):
        srcs, dsts = refs[:n], refs[n:2 * n]
        send_sems, recv_sems, local_sems = refs[2 * n:]
        x, y, c = lax.axis_index("x"), lax.axis_index("y"), lax.axis_index("c")
        me, sibling = (x, y, c), (x, y, 1 - c)
        x_nbr, y_nbr, diag = (1 - x, y, c), (x, 1 - y, c), (1 - x, 1 - y, c)

        def slot(px, py, pc):
            return 4 * px + 2 * py + pc

        def copy(u, k, block, to, own=False):
            i, r0, rows = pieces[u]
            there = dsts[i].at[slot(*block)].at[pl.ds(r0, rows)]
            return pltpu.make_async_remote_copy(
                src_ref=srcs[i].at[pl.ds(r0, rows)] if own else there, dst_ref=there,
                send_sem=send_sems.at[u * 7 + k], recv_sem=recv_sems.at[u * 7 + k],
                device_id=to, device_id_type=pl.DeviceIdType.MESH)

        units = range(len(pieces))
        mine = [pltpu.make_async_copy(srcs[i], dsts[i].at[slot(*me)], local_sems.at[i]) for i in range(n)]
        for cp in mine:
            cp.start()
        for u in units:
            copy(u, 1, me, x_nbr, own=True).start()
            copy(u, 2, me, y_nbr, own=True).start()
        for u in units:
            copy(u, 0, me, sibling, own=True).start()

        def relay_then_pass(k_from, frm, to, k_other, other):
            for u in units:
                copy(u, k_from, frm, me).wait_recv()
                copy(u, 3, frm, to).start()
                copy(u, 3 + k_from, frm, sibling).start()
            for u in units:
                copy(u, k_other, other, me).wait_recv()
                copy(u, 3 + k_other, other, sibling).start()

        @pl.when(c == 1)
        def _():
            relay_then_pass(1, x_nbr, y_nbr, 2, y_nbr)

        @pl.when(c == 0)
        def _():
            relay_then_pass(2, y_nbr, x_nbr, 1, x_nbr)

        for u in units:
            copy(u, 3, diag, me).wait_recv()
            copy(u, 6, diag, sibling).start()
        for u in units:
            copy(u, 0, sibling, me).wait_recv()
            for k, chip in ((4, x_nbr), (5, y_nbr), (6, diag)):
                copy(u, k, (chip[0], chip[1], 1 - c), me).wait_recv()
        for u in units:
            for k in range(7):
                copy(u, k, me, me, own=True).wait_send()
        for cp in mine:
            cp.wait()

    any_spec = pl.BlockSpec(memory_space=pl.ANY)
    return pl.pallas_call(
        body, name=name, out_shape=[jax.ShapeDtypeStruct((N_DEV,) + a.shape, a.dtype) for a in arrays],
        in_specs=[any_spec] * n, out_specs=[any_spec] * n,
        scratch_shapes=[pltpu.SemaphoreType.DMA((len(pieces) * 7,)), pltpu.SemaphoreType.DMA((len(pieces) * 7,)),
                        pltpu.SemaphoreType.DMA((n,))],
    )(*arrays)


_HBM = pl.BlockSpec(memory_space=pltpu.HBM)
_SEM = pl.BlockSpec(memory_space=pltpu.SEMAPHORE)
_EFFECT = pltpu.SideEffectType.DATAFLOW_SIDE_EFFECTING


def _split_copies(items, srcs, lands, send_sems, recv_sems):
    x, y, c = lax.axis_index("x"), lax.axis_index("y"), lax.axis_index("c")
    me = 4 * x + 2 * y + c
    copies = []
    for k in range(1, N_DEV):
        px = jnp.bitwise_xor(x, (k >> 2) & 1)
        py = jnp.bitwise_xor(y, (k >> 1) & 1)
        pc = jnp.bitwise_xor(c, k & 1)
        peer = 4 * px + 2 * py + pc
        for i in range(len(items)):
            src = srcs[i] if items[i][1] == "gather" else srcs[i].at[peer]
            copies.append(pltpu.make_async_remote_copy(
                src_ref=src, dst_ref=lands[i].at[me],
                send_sem=send_sems.at[i * (N_DEV - 1) + k - 1], recv_sem=recv_sems.at[i * (N_DEV - 1) + k - 1],
                device_id=(px, py, pc), device_id_type=pl.DeviceIdType.MESH))
    return me, copies


def _exchange_start(items, name, after):
    n = len(items)
    n_sem = n * (N_DEV - 1)
    srcs, lands = [], []
    for a, mode in items:
        blk = a.shape if mode == "gather" else a.shape[1:]
        srcs.append(pltpu.with_memory_space_constraint(a, pltpu.HBM))
        lands.append(pltpu.with_memory_space_constraint(lax.empty((N_DEV,) + tuple(blk), a.dtype), pltpu.HBM))

    def body(*refs):
        src_refs, land_refs = refs[:n], refs[n:2 * n]
        send_sems, recv_sems = refs[2 * n + 1], refs[2 * n + 2]
        local_sems = refs[4 * n + 3]
        me, copies = _split_copies(items, src_refs, land_refs, send_sems, recv_sems)
        for i in range(n):
            own = src_refs[i] if items[i][1] == "gather" else src_refs[i].at[me]
            cp = pltpu.make_async_copy(own, land_refs[i].at[me], local_sems.at[i])
            cp.start()
            cp.wait()
        for cp in copies:
            cp.start()

    out_shape = [pltpu.SemaphoreType.DMA((n_sem,)), pltpu.SemaphoreType.DMA((n_sem,))]
    out_shape += [pltpu.HBM(a.shape, a.dtype) for a in srcs] + [pltpu.HBM(a.shape, a.dtype) for a in lands]
    outs = pl.pallas_call(
        body, name=name, out_shape=out_shape,
        in_specs=[_HBM] * (2 * n) + [pl.BlockSpec(memory_space=pl.ANY)],
        out_specs=[_SEM, _SEM] + [_HBM] * (2 * n),
        input_output_aliases={i: 2 + i for i in range(2 * n)},
        scratch_shapes=[pltpu.SemaphoreType.DMA((n,))],
        compiler_params=pltpu.CompilerParams(has_side_effects=_EFFECT),
    )(*srcs, *lands, after)
    handle = (items, name, outs[0], outs[1], outs[2:2 + n], outs[2 + n:2 + 2 * n])
    return handle, outs[2]


class _Sender:
    PIECE_ROWS = 352

    def __init__(self, items, chunks=None):
        self.items, self.n = items, len(items)
        self.chunks = chunks
        if chunks is None:
            block_rows = [a.shape[0] if mode == "gather" else a.shape[1] for a, mode in items]
            self.chunks = [r // self.PIECE_ROWS if r % self.PIECE_ROWS == 0 else 1 for r in block_rows]
        self.srcs, self.lands = [], []
        for a, mode in items:
            blk = a.shape if mode == "gather" else a.shape[1:]
            self.srcs.append(pltpu.with_memory_space_constraint(a, pltpu.HBM))
            self.lands.append(pltpu.with_memory_space_constraint(lax.empty((N_DEV,) + tuple(blk), a.dtype), pltpu.HBM))

    def issue(self, src_refs, land_refs, send_sems, recv_sems, local_sems, step, n_steps):
        x, y, c = lax.axis_index("x"), lax.axis_index("y"), lax.axis_index("c")
        me = 4 * x + 2 * y + c
        copies = []
        for ch in range(max(self.chunks)):
            for k in range(1, N_DEV):
                px = jnp.bitwise_xor(x, (k >> 2) & 1)
                py = jnp.bitwise_xor(y, (k >> 1) & 1)
                pc = jnp.bitwise_xor(c, k & 1)
                peer = 4 * px + 2 * py + pc
                for i, (_, mode) in enumerate(self.items):
                    if ch >= self.chunks[i]:
                        continue
                    n_rows = land_refs[i].shape[1] // self.chunks[i]
                    rows = pl.ds(ch * n_rows, n_rows)
                    src = src_refs[i].at[rows] if mode == "gather" else src_refs[i].at[peer].at[rows]
                    copies.append(pltpu.make_async_remote_copy(
                        src_ref=src, dst_ref=land_refs[i].at[me].at[rows],
                        send_sem=send_sems.at[i * (N_DEV - 1) + k - 1], recv_sem=recv_sems.at[i * (N_DEV - 1) + k - 1],
                        device_id=(px, py, pc), device_id_type=pl.DeviceIdType.MESH))
        own = [pltpu.make_async_copy(src_refs[i] if mode == "gather" else src_refs[i].at[me], land_refs[i].at[me],
                                     local_sems.at[i]) for i, (_, mode) in enumerate(self.items)]

        @pl.when(step == 0)
        def _():
            for cp in own:
                cp.start()

        for s in range(n_steps):
            group = [cp for j, cp in enumerate(copies) if (j * n_steps) // len(copies) == s]
            if group:
                @pl.when(step == s)
                def _(group=group):
                    for cp in group:
                        cp.start()

        @pl.when(step == n_steps - 1)
        def _():
            for cp in own:
                cp.wait()


def _host_call(body, name, grid, in_specs, args, out_shape, out_specs, scratch_shapes, after=None, sender=None):
    in_specs, args, out_shape, out_specs = list(in_specs), list(args), list(out_shape), list(out_specs)
    scratch_shapes = list(scratch_shapes)
    semantics = ("arbitrary",) * len(grid)
    body = _ordered_behind(body, in_specs, args, after)
    if sender is None:
        res = pl.pallas_call(body, name=name, grid=grid, in_specs=in_specs, out_specs=out_specs, out_shape=out_shape,
                             scratch_shapes=scratch_shapes, compiler_params=_params(semantics))(*args)
        return res, None
    n, n_in, n_out, n_scr = sender.n, len(in_specs), len(out_shape), len(scratch_shapes)
    n_sem = n * (N_DEV - 1)
    n_steps = 1
    for g in grid:
        n_steps *= g
    compute = body

    def body(*refs):
        ins, s_in = refs[:n_in], refs[n_in:n_in + 2 * n]
        o0 = n_in + 2 * n
        outs, s_out = refs[o0:o0 + n_out], refs[o0 + n_out:o0 + n_out + 2 + 2 * n]
        scr = refs[o0 + n_out + 2 + 2 * n:]
        compute(*ins, *outs, *scr[:n_scr])
        step = pl.program_id(0)
        for d in range(1, len(grid)):
            step = step * grid[d] + pl.program_id(d)
        sender.issue(s_in[:n], s_in[n:], s_out[0], s_out[1], scr[n_scr], step, n_steps)

    res = pl.pallas_call(
        body, name=name, grid=grid,
        in_specs=in_specs + [_HBM] * (2 * n), out_specs=out_specs + [_SEM, _SEM] + [_HBM] * (2 * n),
        out_shape=out_shape + [pltpu.SemaphoreType.DMA((n_sem,)), pltpu.SemaphoreType.DMA((n_sem,))]
        + [pltpu.HBM(a.shape, a.dtype) for a in sender.srcs] + [pltpu.HBM(a.shape, a.dtype) for a in sender.lands],
        input_output_aliases={n_in + j: n_out + 2 + j for j in range(2 * n)},
        scratch_shapes=scratch_shapes + [pltpu.SemaphoreType.DMA((n,))],
        compiler_params=pltpu.CompilerParams(dimension_semantics=semantics, vmem_limit_bytes=VMEM_LIMIT,
                                             has_side_effects=_EFFECT),
    )(*args, *sender.srcs, *sender.lands)
    handle = (sender.items, name, res[n_out], res[n_out + 1], res[n_out + 2:n_out + 2 + n],
              res[n_out + 2 + n:n_out + 2 + 2 * n])
    return res[:n_out], handle


def _exchange_wait(handle, after):
    items, name, send_sems, recv_sems, srcs, lands = handle
    n = len(items)

    def body(*refs):
        src_refs, land_refs = refs[:n], refs[n:2 * n]
        send_ref, recv_ref = refs[2 * n], refs[2 * n + 1]
        _, copies = _split_copies(items, src_refs, land_refs, send_ref, recv_ref)
        for cp in copies:
            cp.wait_send()
            cp.wait_recv()

    outs = pl.pallas_call(
        body, name=name + "_wait",
        out_shape=[pltpu.HBM(a.shape, a.dtype) for a in srcs] + [pltpu.HBM(a.shape, a.dtype) for a in lands],
        in_specs=[_HBM] * (2 * n) + [_SEM, _SEM, pl.BlockSpec(memory_space=pl.ANY)], out_specs=[_HBM] * (2 * n),
        input_output_aliases={i: i for i in range(2 * n)},
        compiler_params=pltpu.CompilerParams(has_side_effects=_EFFECT),
    )(*srcs, *lands, send_sems, recv_sems, after)
    return outs[n:]


def _mod_fwd(cvec, w_mod_l, b_mod_l):
    rows, cols = cvec.shape[0], w_mod_l.shape[1]

    def body(c_ref, w_ref, b_ref, o_ref, s_ref):
        cv = c_ref[...]
        s = cv * _sigmoid(cv)
        s_ref[...] = s
        o_ref[...] = _dot(s, w_ref[...]) + b_ref[...]

    return pl.pallas_call(
        body, name="mod_fwd",
        out_shape=(jax.ShapeDtypeStruct((rows, cols), F32), jax.ShapeDtypeStruct((rows, D), F32)),
        in_specs=[_full((rows, D)), _full((D, cols)), _full((1, cols))],
        out_specs=(_full((rows, cols)), _full((rows, D))), grid=(1,),
        compiler_params=_params(("arbitrary",)),
    )(cvec, w_mod_l, b_mod_l)


def _mod_bwd(svec, cvec, dmod_l, w_mod_l):
    rows, cols = dmod_l.shape

    def body(s_ref, c_ref, d_ref, w_ref, gw_ref, gc_ref):
        gw_ref[...] = _dot(s_ref[...], d_ref[...], "tn")
        cv = c_ref[...]
        sg = _sigmoid(cv)
        gc_ref[...] = _dot(d_ref[...], w_ref[...], "nt") * (sg * (1.0 + cv * (1.0 - sg)))

    return pl.pallas_call(
        body, name="mod_bwd",
        out_shape=(jax.ShapeDtypeStruct((D, cols), F32), jax.ShapeDtypeStruct((rows, D), F32)),
        in_specs=[_full((rows, D)), _full((rows, D)), _full((rows, cols)), _full((D, cols))],
        out_specs=(_full((D, cols)), _full((rows, D))), grid=(1,),
        compiler_params=_params(("arbitrary",)),
    )(svec, cvec, dmod_l, w_mod_l)


def _inproj(xt, modv, g, w_inT, n_cols, rows_per_example, name, after=None, sender=None):
    rows = xt.shape[0]
    tm = min(PROJ_TILE, rows_per_example)
    per_b = rows_per_example // tm
    shared_mod = modv.shape[0] == 1

    def body(x_ref, mod_ref, g_ref, w_ref, p_ref, h_ref):
        x = x_ref[...]
        r = lax.rsqrt(jnp.mean(x * x, axis=-1, keepdims=True) + EPS)
        h = (x * r * g_ref[...]) * (1.0 + mod_ref[0, 1:2, :]) + mod_ref[0, 0:1, :]
        hb = h.astype(MXU_DTYPE)
        h_ref[...] = hb
        for j in range(n_cols // KW):
            p_ref[:, j * KW:(j + 1) * KW] = _dot(hb, w_ref[j * KW:(j + 1) * KW, :], "nt").astype(p_ref.dtype)

    mod_idx = (lambda i: (0, 0, 0)) if shared_mod else (lambda i: (i // per_b, 0, 0))
    in_specs = [pl.BlockSpec((tm, D), lambda i: (i, 0)), pl.BlockSpec((1, N_MOD, D), mod_idx), _full((1, D)),
                pl.BlockSpec((n_cols, D), lambda i: (0, 0), pipeline_mode=pl.Buffered(1))]
    (p, h), handle = _host_call(
        body, name, (rows // tm,), in_specs, [xt, modv, g, w_inT],
        [jax.ShapeDtypeStruct((rows, n_cols), MXU_DTYPE), jax.ShapeDtypeStruct((rows, D), MXU_DTYPE)],
        [pl.BlockSpec((tm, n_cols), lambda i: (i, 0)), pl.BlockSpec((tm, D), lambda i: (i, 0))], [],
        after=after, sender=sender)
    return p, h, handle


def _tri(reverse, n):
    row = lax.broadcasted_iota(jnp.int32, (n, n), 0)
    col = lax.broadcasted_iota(jnp.int32, (n, n), 1)
    same = (row // CHUNK) == (col // CHUNK)
    return same & ((col >= row) if reverse else (col <= row))


def _per_chunk_rows(x, reverse):
    n = x.shape[0]
    rows = [x[j * CHUNK:j * CHUNK + 1] if reverse else x[(j + 1) * CHUNK - 1:(j + 1) * CHUNK] for j in range(n // CHUNK)]
    return jnp.concatenate([jnp.broadcast_to(r, (CHUNK, x.shape[1])) for r in rows], axis=0), rows


def _lower_bound(gam_ref, direction):
    return _sigmoid(gam_ref[direction:direction + 1, :] - gam_ref[2 + direction:3 + direction, :])


def _gate_prep(z, lb, tri, reverse, b=None):
    sg = _sigmoid(z)
    f = lb + (1.0 - lb) * sg
    g = jnp.log(f)
    b = _mask_dot(tri, g) if b is None else b
    bl, bl_rows = _per_chunk_rows(b, reverse)
    mid = 0.5 * bl
    return sg, g, 1.0 - f, b, jnp.exp(mid), [jnp.exp(0.5 * r) for r in bl_rows], jnp.exp(mid - b), mid


def _hgrn_fwd(p, gam, s0, rows_per_example, with_out, name, sender=None):
    rows = p.shape[0]
    nb_ex = rows // rows_per_example
    rb = min(TOKEN_TILE, rows_per_example)
    cpb = rb // CHUNK
    nb = rows_per_example // rb
    n_chunks = rows // CHUNK
    has_s0 = s0 is not None

    def body(*refs):
        it = iter(refs)
        gam_ref = next(it)
        zf_ref, vf_ref = next(it), next(it)
        qf_ref = next(it) if with_out else None
        zb_ref, vb_ref = next(it), next(it)
        qb_ref = next(it) if with_out else None
        s0_ref = next(it) if has_s0 else None
        if with_out:
            of_ref, ob_ref = next(it), next(it)
        stash_f, stash_b, bsum_f, bsum_b, fin_ref = next(it), next(it), next(it), next(it), next(it)
        st_ref = next(it)
        i = pl.program_id(1)

        @pl.when(i == 0)
        def _():
            if has_s0:
                st_ref[...] = s0_ref[:, 0]
            else:
                st_ref[...] = jnp.zeros_like(st_ref)

        for direction, (z_ref, v_ref, q_ref, stash, bsum_ref) in enumerate(
                ((zf_ref, vf_ref, qf_ref, stash_f, bsum_f), (zb_ref, vb_ref, qb_ref, stash_b, bsum_b))):
            reverse = direction == 1
            tri = _tri(reverse, rb)
            lb = _lower_bound(gam_ref, direction)
            z = z_ref[...].astype(F32)
            v = v_ref[...].astype(F32)
            _, _, k, b, em, em_rows, e2, mid = _gate_prep(z, lb, tri, reverse)
            bsum_ref[...] = b
            kd = (k * (e2 * em)).astype(MXU_DTYPE)
            vb = v.astype(MXU_DTYPE)
            if with_out:
                q = q_ref[...].astype(F32)
                qi = q * jnp.exp(b - mid)
                qe = (qi * em).astype(MXU_DTYPE)
                qi = qi.astype(MXU_DTYPE)
                ki = (k * e2).astype(MXU_DTYPE)
                intra = []
                for h in range(HEADS):
                    hs = slice(h * DK, (h + 1) * DK)
                    sc = jnp.where(tri, _dot(qi[:, hs], ki[:, hs], "nt"), 0.0)
                    intra.append(_dot(sc, vb[:, hs]))
            for j in (range(cpb - 1, -1, -1) if reverse else range(cpb)):
                rs = slice(j * CHUNK, (j + 1) * CHUNK)
                a = em_rows[j] * em_rows[j]
                for h in range(HEADS):
                    hs = slice(h * DK, (h + 1) * DK)
                    st = st_ref[direction, h]
                    stash[j, h] = st.astype(stash.dtype)
                    if with_out:
                        (ob_ref if reverse else of_ref)[rs, hs] = intra[h][rs] + _dot(qe[rs, hs], st, "nt")
                    st_ref[direction, h] = st * a[:, hs] + _dot(vb[rs, hs], kd[rs, hs], "tn")

        @pl.when(i == nb - 1)
        def _():
            fin_ref[:, 0] = st_ref[...]

    up = lambda b, i: b * nb + i
    down = lambda b, i: b * nb + nb - 1 - i
    col = lambda rowf, c: pl.BlockSpec((rb, KW), lambda b, i: (rowf(b, i), c))
    in_specs = [_full((4, KW)), col(up, 0), col(up, 2)] + ([col(up, 3)] if with_out else [])
    in_specs += [col(down, 1), col(down, 2)] + ([col(down, 3)] if with_out else [])
    args = [gam, p, p] + ([p] if with_out else []) + [p, p] + ([p] if with_out else [])
    if has_s0:
        in_specs.append(pl.BlockSpec((2, 1, HEADS, DK, DK), lambda b, i: (0, b, 0, 0, 0)))
        args.append(s0)
    out_shape, out_specs = [], []
    if with_out:
        out_shape += [jax.ShapeDtypeStruct((rows, KW), F32)] * 2
        out_specs += [pl.BlockSpec((rb, KW), lambda b, i: (up(b, i), 0)),
                      pl.BlockSpec((rb, KW), lambda b, i: (down(b, i), 0))]
    out_shape += [jax.ShapeDtypeStruct((n_chunks, HEADS, DK, DK), MXU_DTYPE)] * 2
    out_specs += [pl.BlockSpec((cpb, HEADS, DK, DK), lambda b, i: (up(b, i), 0, 0, 0)),
                  pl.BlockSpec((cpb, HEADS, DK, DK), lambda b, i: (down(b, i), 0, 0, 0))]
    out_shape += [jax.ShapeDtypeStruct((rows, KW), F32)] * 2
    out_specs += [pl.BlockSpec((rb, KW), lambda b, i: (up(b, i), 0)),
                  pl.BlockSpec((rb, KW), lambda b, i: (down(b, i), 0))]
    out_shape.append(jax.ShapeDtypeStruct((2, nb_ex, HEADS, DK, DK), F32))
    out_specs.append(pl.BlockSpec((2, 1, HEADS, DK, DK), lambda b, i: (0, b, 0, 0, 0)))
    res, handle = _host_call(body, name, (nb_ex, nb), in_specs, args, out_shape, out_specs,
                             [pltpu.VMEM((2, HEADS, DK, DK), F32)], sender=sender)
    return (*res, handle)


def _hgrn_bwd(p, gam, do, stash_f, stash_b, bsum_f, bsum_b, ds_end, rows_per_example, with_out, name, after=None,
              sender=None):
    rows = p.shape[0]
    nb_ex = rows // rows_per_example
    rb = min(TOKEN_TILE, rows_per_example)
    cpb = rb // CHUNK
    nb = rows_per_example // rb
    has_end = ds_end is not None

    def body(*refs):
        it = iter(refs)
        gam_ref = next(it)
        ins = []
        for _ in range(2):
            z_ref, v_ref = next(it), next(it)
            q_ref = next(it) if with_out else None
            do_ref = next(it) if with_out else None
            ins.append((z_ref, v_ref, q_ref, do_ref, next(it), next(it)))
        end_ref = next(it) if has_end else None
        outs = [next(it), next(it)]
        dlb_ref, ds0_ref = next(it), next(it)
        dst_ref = next(it)
        b_id, i = pl.program_id(0), pl.program_id(1)

        @pl.when(i == 0)
        def _():
            if has_end:
                dst_ref[...] = end_ref[:, 0]
            else:
                dst_ref[...] = jnp.zeros_like(dst_ref)

        @pl.when((i == 0) & (b_id == 0))
        def _():
            dlb_ref[...] = jnp.zeros_like(dlb_ref)

        for direction in range(2):
            z_ref, v_ref, q_ref, do_ref, stash, b_ref = ins[direction]
            dgrp_ref = outs[direction]
            reverse = direction == 1
            tri = _tri(reverse, rb)
            tri_t = _tri(not reverse, rb)
            lb = _lower_bound(gam_ref, direction)
            heads = [slice(h * DK, (h + 1) * DK) for h in range(HEADS)]
            chunks = [slice(j * CHUNK, (j + 1) * CHUNK) for j in range(cpb)]
            grid_cat = lambda parts: jnp.concatenate([jnp.concatenate(row, axis=1) for row in parts], axis=0)
            cat = lambda parts: jnp.concatenate(parts, axis=1)
            z = z_ref[...].astype(F32)
            sg, g, k, b, em, em_rows, e2, mid = _gate_prep(z, lb, tri, reverse, b=b_ref[...])
            e3 = e2 * em
            kd = k * e3
            kd_b = kd.astype(MXU_DTYPE)
            vb = v_ref[...].astype(MXU_DTYPE)
            if with_out:
                q = q_ref[...].astype(F32)
                dout = do_ref[...].astype(MXU_DTYPE)
                e1 = jnp.exp(b - mid)
                e4 = e1 * em
                qi, ki, qe = q * e1, k * e2, q * e4
                qi_b, ki_b, qe_b = qi.astype(MXU_DTYPE), ki.astype(MXU_DTYPE), qe.astype(MXU_DTYPE)
                dqi_p, dki_p, dv_p = [], [], []
                for hs in heads:
                    sc = jnp.where(tri, _dot(qi_b[:, hs], ki_b[:, hs], "nt"), 0.0)
                    dsc = jnp.where(tri, _dot(dout[:, hs], vb[:, hs], "nt"), 0.0)
                    dqi_p.append(_dot(dsc, ki_b[:, hs]))
                    dki_p.append(_dot(dsc, qi_b[:, hs], "tn"))
                    dv_p.append(_dot(sc, dout[:, hs], "tn"))
                dqi, dki, dv = cat(dqi_p), cat(dki_p), cat(dv_p)
                dqe = grid_cat([[_dot(dout[rs, hs], stash[j, h]) for h, hs in enumerate(heads)]
                                for j, rs in enumerate(chunks)])
                grow = [[_dot(dout[rs, hs], qe_b[rs, hs], "tn") for hs in heads] for rs in chunks]
            dkd_p = [[None] * HEADS for _ in range(cpb)]
            dvs_p = [[None] * HEADS for _ in range(cpb)]
            da_p = [[None] * HEADS for _ in range(cpb)]
            for j in (range(cpb) if reverse else range(cpb - 1, -1, -1)):
                rs = chunks[j]
                a = em_rows[j] * em_rows[j]
                for h, hs in enumerate(heads):
                    dst = dst_ref[direction, h]
                    dkd_p[j][h] = _dot(vb[rs, hs], dst)
                    dvs_p[j][h] = _dot(kd_b[rs, hs], dst, "nt")
                    da_p[j][h] = jnp.broadcast_to(
                        jnp.sum(dst * stash[j, h].astype(F32), axis=0, keepdims=True), (CHUNK, DK))
                    new_dst = dst * a[:, hs]
                    dst_ref[direction, h] = new_dst + grow[j][h] if with_out else new_dst
            dkd, dvs, da = grid_cat(dkd_p), grid_cat(dvs_p), grid_cat(da_p)
            t_kd = dkd * kd
            dk = dkd * e3
            db = -t_kd
            tot = t_kd
            if with_out:
                dgrp_ref[:, KW:2 * KW] = (dvs + dv).astype(dgrp_ref.dtype)
                dgrp_ref[:, 2 * KW:] = (dqi * e1 + dqe * e4).astype(dgrp_ref.dtype)
                dk = dk + dki * e2
                t_qi, t_ki, t_qe = dqi * qi, dki * ki, dqe * qe
                db = db + t_qi - t_ki + t_qe
                tot = tot + 0.5 * (t_ki - t_qi)
            else:
                dgrp_ref[:, KW:2 * KW] = dvs.astype(dgrp_ref.dtype)
            dbl = jnp.concatenate([jnp.broadcast_to(jnp.sum(tot[rs], axis=0, keepdims=True), (CHUNK, KW))
                                   for rs in chunks], axis=0) + da * (em * em)
            dg = _mask_dot(tri_t, db) + dbl
            df = dg * jnp.exp(-g) - dk
            dgrp_ref[:, 0:KW] = (df * (1.0 - lb) * sg * (1.0 - sg)).astype(dgrp_ref.dtype)
            dlb_ref[direction:direction + 1, :] += jnp.sum(df * (1.0 - sg), axis=0, keepdims=True)

        @pl.when(i == nb - 1)
        def _():
            ds0_ref[:, 0] = dst_ref[...]

    rows_of = (lambda b, i: b * nb + nb - 1 - i, lambda b, i: b * nb + i)
    in_specs, args = [_full((4, KW))], [gam]
    for direction in range(2):
        rf = rows_of[direction]
        col = lambda c, rf=rf: pl.BlockSpec((rb, KW), lambda b, i: (rf(b, i), c))
        in_specs += [col(direction), col(2)]
        args += [p, p]
        if with_out:
            in_specs += [col(3), col(0)]
            args += [p, do]
        in_specs += [pl.BlockSpec((cpb, HEADS, DK, DK), lambda b, i, rf=rf: (rf(b, i), 0, 0, 0)), col(0)]
        args += [(stash_f, stash_b)[direction], (bsum_f, bsum_b)[direction]]
    if has_end:
        in_specs.append(pl.BlockSpec((2, 1, HEADS, DK, DK), lambda b, i: (0, b, 0, 0, 0)))
        args.append(ds_end)
    out_shape, out_specs = [], []
    for direction in range(2):
        rf = rows_of[direction]
        width = (3 if with_out else 2) * KW
        out_shape.append(jax.ShapeDtypeStruct((rows, width), MXU_DTYPE))
        out_specs.append(pl.BlockSpec((rb, width), lambda b, i, rf=rf: (rf(b, i), 0)))
    out_shape += [jax.ShapeDtypeStruct((2, KW), F32), jax.ShapeDtypeStruct((2, nb_ex, HEADS, DK, DK), F32)]
    out_specs += [_full((2, KW)), pl.BlockSpec((2, 1, HEADS, DK, DK), lambda b, i: (0, b, 0, 0, 0))]
    res, handle = _host_call(body, name, (nb_ex, nb), in_specs, args, out_shape, out_specs,
                             [pltpu.VMEM((2, HEADS, DK, DK), F32)], after=after, sender=sender)
    return (*res, handle)


def _tail_forward(osum, og, u, v, ga, gb, gna, ln_g, ln_b, ws_ref, bs_ref, wpaT_ref, wpbT_ref):
    tm = osum.shape[0]
    gna4 = jnp.concatenate([gna] * HEADS, axis=1)
    r_parts = []
    for h in range(HEADS):
        oh = osum[:, h * DK:(h + 1) * DK]
        r_parts.append(jnp.broadcast_to(lax.rsqrt(jnp.mean(oh * oh, axis=-1, keepdims=True) + EPS), (tm, DK)))
    r = jnp.concatenate(r_parts, axis=1)
    on = osum * r
    sg_og = _sigmoid(og)
    silu_og = og * sg_og
    oan = on * gna4
    oa = oan * silu_og
    ug, tu = _gelu(u)
    vg, tv = _gelu(v)
    mu = jnp.mean(vg, axis=-1, keepdims=True)
    vc = vg - mu
    rstd = lax.rsqrt(jnp.mean(vc * vc, axis=-1, keepdims=True) + EPS)
    vhat = vc * rstd
    vln = vhat * ln_g + ln_b
    blocks = []
    for n in range(tm // SGU_BLOCK):
        rs = slice(n * SGU_BLOCK, (n + 1) * SGU_BLOCK)
        blocks.append(jnp.concatenate(
            [_dot(ws_ref[g], vln[rs, g * DK:(g + 1) * DK]) + bs_ref[g] for g in range(GROUPS)], axis=1))
    mixed = jnp.concatenate(blocks, axis=0) if len(blocks) > 1 else blocks[0]
    obm = ug * mixed
    pa = _dot(oa, wpaT_ref[...], "nt")
    pb = _dot(obm, wpbT_ref[...], "nt")
    sga, sgb = _sigmoid(ga), _sigmoid(gb)
    merged = sga * pa + sgb * pb
    return dict(r=r, on=on, sg_og=sg_og, silu_og=silu_og, oan=oan, oa=oa, ug=ug, tu=tu, tv=tv, rstd=rstd, vhat=vhat,
                vln=vln, mixed=mixed, obm=obm, pa=pa, pb=pb, sga=sga, sgb=sgb, merged=merged, gna4=gna4)


def _tail_in_specs(tm):
    tile = lambda c: pl.BlockSpec((tm, KW), lambda i: (i, c))
    return [tile(c) for c in range(4, 11)]


def _tail_weight_specs():
    return [_full((1, DK)), _full((1, KW)), _full((1, KW)), _full((GROUPS, SGU_BLOCK, SGU_BLOCK)),
            _full((GROUPS, SGU_BLOCK, 1)), _full((D, KW), single=True), _full((D, KW), single=True),
            _full((D, D), single=True)]


def _read_tail_inputs(of_ref, ob_ref, pcols):
    osum = of_ref[...] + ob_ref[...]
    og, u, v = (pcols[j][...].astype(F32) for j in range(3))
    ga = jnp.concatenate([pcols[3][...], pcols[4][...]], axis=1).astype(F32)
    gb = jnp.concatenate([pcols[5][...], pcols[6][...]], axis=1).astype(F32)
    return osum, og, u, v, ga, gb


def _tail_fwd(p, o_up, o_down, xt, modv, gna, ln_g, ln_b, w_s, b_s, w_paT, w_pbT, w_o, rows_per_example):
    rows = xt.shape[0]
    tm = min(TAIL_TILE, rows_per_example)
    per_b = rows_per_example // tm

    def body(of_ref, ob_ref, *rest):
        pcols = rest[:7]
        (x_ref, mod_ref, gna_ref, lng_ref, lnb_ref, ws_ref, bs_ref, wpaT_ref, wpbT_ref, wo_ref,
         x1_ref, mix_ref, merged_ref, oa_ref, obm_ref) = rest[7:]
        t = _tail_forward(*_read_tail_inputs(of_ref, ob_ref, pcols), gna_ref[...], lng_ref[...], lnb_ref[...],
                          ws_ref, bs_ref, wpaT_ref, wpbT_ref)
        mix = _dot(t["merged"], wo_ref[...])
        x1_ref[...] = x_ref[...] + mod_ref[0, 2:3, :] * mix
        mix_ref[...] = mix.astype(mix_ref.dtype)
        merged_ref[...] = t["merged"].astype(merged_ref.dtype)
        oa_ref[...] = t["oa"].astype(oa_ref.dtype)
        obm_ref[...] = t["obm"].astype(obm_ref.dtype)

    row = lambda w: pl.BlockSpec((tm, w), lambda i: (i, 0))
    in_specs = [row(KW), row(KW)] + _tail_in_specs(tm) + [row(D), pl.BlockSpec((1, N_MOD, D), lambda i: (i // per_b, 0, 0))]
    in_specs += _tail_weight_specs()
    return pl.pallas_call(
        body, name="tail_fwd", grid=(rows // tm,),
        out_shape=(jax.ShapeDtypeStruct((rows, D), F32), jax.ShapeDtypeStruct((rows, D), MXU_DTYPE),
                   jax.ShapeDtypeStruct((rows, D), MXU_DTYPE), jax.ShapeDtypeStruct((rows, KW), MXU_DTYPE),
                   jax.ShapeDtypeStruct((rows, KW), MXU_DTYPE)),
        in_specs=in_specs, out_specs=(row(D), row(D), row(D), row(KW), row(KW)),
        compiler_params=_params(("arbitrary",)),
    )(o_up, o_down, *([p] * 7), xt, modv, gna, ln_g, ln_b, w_s, b_s, w_paT, w_pbT, w_o)


def _tail_bwd(p, o_up, o_down, dx1, mix, modv, gna, ln_g, ln_b, w_s, b_s, w_paT, w_pbT, w_o, rows_per_example,
              after=None, sender=None):
    rows = dx1.shape[0]
    nb_ex = rows // rows_per_example
    tm = min(TAIL_TILE, rows_per_example)
    per_b = rows_per_example // tm

    def body(of_ref, ob_ref, *rest):
        pcols = rest[:7]
        (dx1_ref, mix_ref, mod_ref, gna_ref, lng_ref, lnb_ref, ws_ref, bs_ref, wpaT_ref, wpbT_ref, wo_ref,
         dpt_ref, do_ref, dmix_ref, dpa_ref, dpb_ref, dmod_ref, small_ref, dws_ref, dbs_ref) = rest[7:]
        i = pl.program_id(0)

        @pl.when(i == 0)
        def _():
            small_ref[...] = jnp.zeros_like(small_ref)
            dws_ref[...] = jnp.zeros_like(dws_ref)
            dbs_ref[...] = jnp.zeros_like(dbs_ref)

        @pl.when(i % per_b == 0)
        def _():
            dmod_ref[...] = jnp.zeros_like(dmod_ref)

        osum, og, u, v, ga, gb = _read_tail_inputs(of_ref, ob_ref, pcols)
        ln_g = lng_ref[...]
        t = _tail_forward(osum, og, u, v, ga, gb, gna_ref[...], ln_g, lnb_ref[...], ws_ref, bs_ref, wpaT_ref, wpbT_ref)
        dx1v = dx1_ref[...]
        dmod_ref[0, 2:3, :] += jnp.sum(dx1v * mix_ref[...].astype(F32), axis=0, keepdims=True)
        dmix = dx1v * mod_ref[0, 2:3, :]
        dmix_ref[...] = dmix.astype(dmix_ref.dtype)
        dmerged = _dot(dmix, wo_ref[...], "nt")
        sga, sgb = t["sga"], t["sgb"]
        dpa = dmerged * sga
        dpb = dmerged * sgb
        dpa_ref[...] = dpa.astype(dpa_ref.dtype)
        dpb_ref[...] = dpb.astype(dpb_ref.dtype)
        dga = dmerged * t["pa"] * sga * (1.0 - sga)
        dgb = dmerged * t["pb"] * sgb * (1.0 - sgb)
        doa = _dot(dpa, wpaT_ref[...])
        dobm = _dot(dpb, wpbT_ref[...])
        dug = dobm * t["mixed"]
        dmixed = dobm * t["ug"]
        du = dug * _gelu_grad(u, t["tu"])
        dvln_blocks = []
        for n in range(tm // SGU_BLOCK):
            rs = slice(n * SGU_BLOCK, (n + 1) * SGU_BLOCK)
            parts = []
            for g in range(GROUPS):
                gs = slice(g * DK, (g + 1) * DK)
                dm = dmixed[rs, gs]
                parts.append(_dot(ws_ref[g], dm, "tn"))
                dws_ref[g] += _dot(dm, t["vln"][rs, gs], "nt")
                dbs_ref[g] += jnp.sum(dm, axis=1, keepdims=True)
            dvln_blocks.append(jnp.concatenate(parts, axis=1))
        dvln = jnp.concatenate(dvln_blocks, axis=0) if len(dvln_blocks) > 1 else dvln_blocks[0]
        vhat = t["vhat"]
        small_ref[1:2, 0:KW] += jnp.sum(dvln * vhat, axis=0, keepdims=True)
        small_ref[2:3, 0:KW] += jnp.sum(dvln, axis=0, keepdims=True)
        dvhat = dvln * ln_g
        dvg = t["rstd"] * (dvhat - jnp.mean(dvhat, axis=-1, keepdims=True)
                           - vhat * jnp.mean(dvhat * vhat, axis=-1, keepdims=True))
        dv = dvg * _gelu_grad(v, t["tv"])
        sg_og = t["sg_og"]
        doan = doa * t["silu_og"]
        dog = doa * t["oan"] * (sg_og * (1.0 + og * (1.0 - sg_og)))
        prod = doan * t["on"]
        dgna = jnp.zeros((1, DK), F32)
        for h in range(HEADS):
            dgna = dgna + jnp.sum(prod[:, h * DK:(h + 1) * DK], axis=0, keepdims=True)
        small_ref[0:1, 0:DK] += dgna
        don = doan * t["gna4"]
        dot_parts = []
        for h in range(HEADS):
            hs = slice(h * DK, (h + 1) * DK)
            m = jnp.mean(don[:, hs] * t["on"][:, hs], axis=-1, keepdims=True)
            dot_parts.append(t["r"][:, hs] * (don[:, hs] - t["on"][:, hs] * m))
        do_ref[...] = jnp.concatenate(dot_parts, axis=1).astype(do_ref.dtype)
        for j, val in enumerate((dog, du, dv)):
            dpt_ref[:, j * KW:(j + 1) * KW] = val.astype(dpt_ref.dtype)
        dpt_ref[:, 3 * KW:3 * KW + D] = dga.astype(dpt_ref.dtype)
        dpt_ref[:, 3 * KW + D:] = dgb.astype(dpt_ref.dtype)

    row = lambda w: pl.BlockSpec((tm, w), lambda i: (i, 0))
    in_specs = [row(KW), row(KW)] + _tail_in_specs(tm) + [row(D), row(D), pl.BlockSpec((1, N_MOD, D), lambda i: (i // per_b, 0, 0))]
    in_specs += _tail_weight_specs()
    args = [o_up, o_down, *([p] * 7), dx1, mix, modv, gna, ln_g, ln_b, w_s, b_s, w_paT, w_pbT, w_o]
    cd = MXU_DTYPE
    res, handle = _host_call(
        body, "tail_bwd", (rows // tm,), in_specs, args,
        [jax.ShapeDtypeStruct((rows, TAIL_COLS), cd), jax.ShapeDtypeStruct((rows, KW), cd),
         jax.ShapeDtypeStruct((rows, D), cd), jax.ShapeDtypeStruct((rows, D), cd),
         jax.ShapeDtypeStruct((rows, D), cd), jax.ShapeDtypeStruct((nb_ex, 8, D), F32),
         jax.ShapeDtypeStruct((8, D), F32), jax.ShapeDtypeStruct((GROUPS, SGU_BLOCK, SGU_BLOCK), F32),
         jax.ShapeDtypeStruct((GROUPS, SGU_BLOCK, 1), F32)],
        [row(TAIL_COLS), row(KW), row(D), row(D), row(D),
         pl.BlockSpec((1, 8, D), lambda i: (i // per_b, 0, 0)), _full((8, D)),
         _full((GROUPS, SGU_BLOCK, SGU_BLOCK)), _full((GROUPS, SGU_BLOCK, 1))], [],
        after=after, sender=sender)
    return (*res, handle)


def _ffn(x1, target, modv, g_ffn, g_final, w_upT, w_down, rows_per_example):
    rows = x1.shape[0]
    nb_ex = rows // rows_per_example
    tm = min(TOKEN_TILE, rows_per_example)
    per_b = rows_per_example // tm
    n_ff = D_FF // FF_CHUNK

    def body(x1_ref, tgt_ref, mod_ref, gffn_ref, gfin_ref, wup_ref, wdn_ref,
             dx1_ref, h2_ref, dffn_ref, act_ref, dup_ref, dmod_ref, small_ref, up_scr):
        i = pl.program_id(0)

        @pl.when(i == 0)
        def _():
            small_ref[...] = jnp.zeros_like(small_ref)

        @pl.when(i % per_b == 0)
        def _():
            dmod_ref[...] = jnp.zeros_like(dmod_ref)

        x1v = x1_ref[...]
        g2 = gffn_ref[...]
        m3, m4, m5 = mod_ref[0, 3:4, :], mod_ref[0, 4:5, :], mod_ref[0, 5:6, :]
        r2 = lax.rsqrt(jnp.mean(x1v * x1v, axis=-1, keepdims=True) + EPS)
        xn2 = x1v * r2
        h2 = (xn2 * g2) * (1.0 + m4) + m3
        h2b = h2.astype(MXU_DTYPE)
        h2_ref[...] = h2b
        def up_pair(j):
            lo = j * FF_CHUNK
            return (_dot(h2b, wup_ref[lo:lo + FF_CHUNK, :], "nt"),
                    _dot(h2b, wup_ref[D_FF + lo:D_FF + lo + FF_CHUNK, :], "nt"))

        group_end = {min(e, n_ff): s for s, e in ((0, 4), (4, 8), (8, 12))}
        cur, ffn = up_pair(0), None
        for j in range(n_ff):
            nxt = up_pair(j + 1) if j + 1 < n_ff else None
            cs = slice(j * FF_CHUNK, (j + 1) * FF_CHUNK)
            a, bgate = cur
            up_scr[:, cs] = a
            up_scr[:, D_FF + j * FF_CHUNK:D_FF + (j + 1) * FF_CHUNK] = bgate
            act_ref[:, cs] = (a * _sigmoid(a) * bgate).astype(MXU_DTYPE)
            cur = nxt
            if j + 1 in group_end:
                gs = slice(group_end[j + 1] * FF_CHUNK, (j + 1) * FF_CHUNK)
                part = _dot(act_ref[:, gs], wdn_ref[gs, :])
                ffn = part if ffn is None else ffn + part
        x2 = x1v + m5 * ffn
        r3 = lax.rsqrt(jnp.mean(x2 * x2, axis=-1, keepdims=True) + EPS)
        xn3 = x2 * r3
        gf = gfin_ref[...]
        err = xn3 * gf - tgt_ref[...]
        loss = 0.5 * jnp.sum(jnp.mean(err * err, axis=-1, keepdims=True), axis=0, keepdims=True)
        small_ref[2:3, :] += jnp.broadcast_to(loss, (1, D))
        dy = err * (1.0 / D)
        small_ref[1:2, :] += jnp.sum(dy * xn3, axis=0, keepdims=True)
        dxn3 = dy * gf
        dx2 = r3 * (dxn3 - xn3 * jnp.mean(dxn3 * xn3, axis=-1, keepdims=True))
        dmod_ref[0, 5:6, :] += jnp.sum(dx2 * ffn, axis=0, keepdims=True)
        dffn = (dx2 * m5).astype(MXU_DTYPE)
        dffn_ref[...] = dffn
        dact_of = lambda j: _dot(dffn, wdn_ref[j * FF_CHUNK:(j + 1) * FF_CHUNK, :], "nt")
        cur, dh2 = dact_of(0), None
        for j in range(n_ff):
            nxt = dact_of(j + 1) if j + 1 < n_ff else None
            cs = slice(j * FF_CHUNK, (j + 1) * FF_CHUNK)
            a, bgate = up_scr[:, cs], up_scr[:, D_FF + j * FF_CHUNK:D_FF + (j + 1) * FF_CHUNK]
            s = _sigmoid(a)
            dup_ref[:, cs] = (cur * bgate * (s * (1.0 + a * (1.0 - s)))).astype(MXU_DTYPE)
            dup_ref[:, D_FF + j * FF_CHUNK:D_FF + (j + 1) * FF_CHUNK] = (cur * a * s).astype(MXU_DTYPE)
            cur = nxt
            if j + 1 in group_end:
                lo, hi = group_end[j + 1] * FF_CHUNK, (j + 1) * FF_CHUNK
                part = (_dot(dup_ref[:, lo:hi], wup_ref[lo:hi, :])
                        + _dot(dup_ref[:, D_FF + lo:D_FF + hi], wup_ref[D_FF + lo:D_FF + hi, :]))
                dh2 = part if dh2 is None else dh2 + part
        dmod_ref[0, 3:4, :] += jnp.sum(dh2, axis=0, keepdims=True)
        dmod_ref[0, 4:5, :] += jnp.sum(dh2 * xn2 * g2, axis=0, keepdims=True)
        small_ref[0:1, :] += jnp.sum(dh2 * (1.0 + m4) * xn2, axis=0, keepdims=True)
        dxn2 = dh2 * g2 * (1.0 + m4)
        dx1_ref[...] = dx2 + r2 * (dxn2 - xn2 * jnp.mean(dxn2 * xn2, axis=-1, keepdims=True))

    row = lambda w: pl.BlockSpec((tm, w), lambda i: (i, 0))
    cd = MXU_DTYPE
    return pl.pallas_call(
        body, name="ffn_fwd_bwd", grid=(rows // tm,),
        out_shape=(jax.ShapeDtypeStruct((rows, D), F32), jax.ShapeDtypeStruct((rows, D), cd),
                   jax.ShapeDtypeStruct((rows, D), cd), jax.ShapeDtypeStruct((rows, D_FF), cd),
                   jax.ShapeDtypeStruct((rows, 2 * D_FF), cd), jax.ShapeDtypeStruct((nb_ex, 8, D), F32),
                   jax.ShapeDtypeStruct((8, D), F32)),
        in_specs=[row(D), row(D), pl.BlockSpec((1, N_MOD, D), lambda i: (i // per_b, 0, 0)), _full((1, D)), _full((1, D)),
                  _full((2 * D_FF, D), single=True), _full((D_FF, D), single=True)],
        out_specs=(row(D), row(D), row(D), row(D_FF), row(2 * D_FF),
                   pl.BlockSpec((1, 8, D), lambda i: (i // per_b, 0, 0)), _full((8, D))),
        scratch_shapes=[pltpu.VMEM((tm, 2 * D_FF), F32)],
        compiler_params=_params(("arbitrary",)),
    )(x1, target, modv, g_ffn, g_final, w_upT, w_down)


def _scan_columns(up, down, n_groups):
    cols = [up[:, 0:KW].astype(F32), down[:, 0:KW].astype(F32)]
    for j in range(1, n_groups):
        cols.append(up[:, j * KW:(j + 1) * KW].astype(F32) + down[:, j * KW:(j + 1) * KW].astype(F32))
    return cols


def _inproj_bwd(d_up, d_down, dpt, xt, dx1, modv, g, w_inT, rows_per_example, name, sender=None):
    rows = xt.shape[0]
    latent = dx1 is not None
    n_cols = IN_COLS if latent else CTX_COLS
    n_groups = d_up.shape[1] // KW
    tm = min(PROJ_TILE, rows_per_example)
    per_b = rows_per_example // tm
    n_mod_blocks = rows // rows_per_example if latent else 1

    def body(*refs):
        it = iter(refs)
        up_ref, down_ref = next(it), next(it)
        dpt_ref = next(it) if latent else None
        x_ref = next(it)
        dx1_ref = next(it) if latent else None
        mod_ref, g_ref, w_ref = next(it), next(it), next(it)
        gx_ref = next(it) if latent else None
        dp_out = None if latent else next(it)
        dmod_ref, small_ref = next(it), next(it)
        dp_ref = next(it) if latent else dp_out
        i = pl.program_id(0)

        @pl.when(i == 0)
        def _():
            small_ref[...] = jnp.zeros_like(small_ref)

        @pl.when((i % per_b == 0) if latent else (i == 0))
        def _():
            dmod_ref[...] = jnp.zeros_like(dmod_ref)

        for j, val in enumerate(_scan_columns(up_ref[...], down_ref[...], n_groups)):
            dp_ref[:, j * KW:(j + 1) * KW] = val.astype(MXU_DTYPE)
        if latent:
            dh = _dot(dp_ref[...], w_ref[0:4 * KW, :]) + _dot(dpt_ref[...], w_ref[4 * KW:, :])
        else:
            dh = _dot(dp_ref[...], w_ref[...])
        x = x_ref[...]
        gv = g_ref[...]
        m1 = mod_ref[0, 1:2, :]
        r = lax.rsqrt(jnp.mean(x * x, axis=-1, keepdims=True) + EPS)
        xn = x * r
        dmod_ref[0, 0:1, :] += jnp.sum(dh, axis=0, keepdims=True)
        dmod_ref[0, 1:2, :] += jnp.sum(dh * xn * gv, axis=0, keepdims=True)
        small_ref[0:1, :] += jnp.sum(dh * (1.0 + m1) * xn, axis=0, keepdims=True)
        if latent:
            dxn = dh * gv * (1.0 + m1)
            gx_ref[...] = dx1_ref[...] + r * (dxn - xn * jnp.mean(dxn * xn, axis=-1, keepdims=True))

    row = lambda w: pl.BlockSpec((tm, w), lambda i: (i, 0))
    mod_idx = (lambda i: (i // per_b, 0, 0)) if latent else (lambda i: (0, 0, 0))
    in_specs = [row(n_groups * KW)] * 2 + ([row(TAIL_COLS)] if latent else []) + [row(D)] + ([row(D)] if latent else [])
    in_specs += [pl.BlockSpec((1, N_MOD, D), mod_idx), _full((1, D)),
                 pl.BlockSpec((n_cols, D), lambda i: (0, 0), pipeline_mode=pl.Buffered(1))]
    args = [d_up, d_down] + ([dpt] if latent else []) + [xt] + ([dx1] if latent else []) + [modv, g, w_inT]
    first = jax.ShapeDtypeStruct((rows, D), F32) if latent else jax.ShapeDtypeStruct((rows, n_cols), MXU_DTYPE)
    out_shape = [first, jax.ShapeDtypeStruct((n_mod_blocks, 8, D), F32), jax.ShapeDtypeStruct((8, D), F32)]
    out_specs = [row(D) if latent else row(n_cols), pl.BlockSpec((1, 8, D), mod_idx), _full((8, D))]
    scratch = [pltpu.VMEM((tm, 4 * KW), MXU_DTYPE)] if latent else []
    res, handle = _host_call(body, name, (rows // tm,), in_specs, args, out_shape, out_specs, scratch, sender=sender)
    return (*res, handle)


def _grad_matmul(a, b, name, init=None, tn=512, sender=None):
    rows, n = a.shape
    k = b.shape[1]
    tn = min(tn, n)
    has_init = init is not None
    init_blocks = init.shape[0] // tn if has_init else 0

    def body(*refs):
        if has_init:
            a_ref, b_ref, init_ref, o_ref = refs
        else:
            a_ref, b_ref, o_ref = refs
        g = _dot(a_ref[...], b_ref[...], "tn")
        if has_init:
            g = g + jnp.where(pl.program_id(0) < init_blocks, init_ref[...].astype(F32), 0.0)
        o_ref[...] = g.astype(o_ref.dtype)

    in_specs = [pl.BlockSpec((rows, tn), lambda i: (0, i)), _full((rows, k), single=True)]
    args = [a, b]
    if has_init:
        in_specs.append(pl.BlockSpec((tn, k), lambda i: (jnp.minimum(i, init_blocks - 1), 0)))
        args.append(init)
    (out,), handle = _host_call(
        body, name, (n // tn,), in_specs, args, [jax.ShapeDtypeStruct((n, k), PAYLOAD_DTYPE)],
        [pl.BlockSpec((tn, k), lambda i: (i, 0))], [], sender=sender)
    return out, handle


def _grad_in(d_up, d_down, dpt, h, init, sender=None):
    rows = h.shape[0]
    tn = 256
    per_group = KW // tn
    n_scan = 4 * per_group
    init_blocks = init.shape[0] // tn

    def body(up_ref, down_ref, dpt_ref, h_ref, init_ref, o_ref):
        i = pl.program_id(0)
        both = (up_ref[...].astype(F32) + down_ref[...].astype(F32)).astype(MXU_DTYPE)
        a = jnp.where(i < per_group, up_ref[...],
                      jnp.where(i < 2 * per_group, down_ref[...], jnp.where(i < n_scan, both, dpt_ref[...])))
        g = _dot(a, h_ref[...], "tn") + jnp.where(i < init_blocks, init_ref[...].astype(F32), 0.0)
        o_ref[...] = g.astype(o_ref.dtype)

    last = 3 * per_group - 1
    col = lambda f: pl.BlockSpec((rows, tn), lambda i: (0, f(i)))
    in_specs = [col(lambda i: jnp.clip(jnp.where(i < per_group, i, i - per_group), 0, last)),
                col(lambda i: jnp.clip(i - per_group, 0, last)),
                col(lambda i: jnp.clip(i - n_scan, 0, TAIL_COLS // tn - 1)),
                _full((rows, D), single=True),
                pl.BlockSpec((tn, D), lambda i: (jnp.minimum(i, init_blocks - 1), 0))]
    (out,), handle = _host_call(
        body, "gw_in", (IN_COLS // tn,), in_specs, [d_up, d_down, dpt, h, init],
        [jax.ShapeDtypeStruct((IN_COLS, D), PAYLOAD_DTYPE)], [pl.BlockSpec((tn, D), lambda i: (i, 0))], [],
        sender=sender)
    return out, handle


def _row_tile(rows, limit=256):
    if rows <= limit:
        return rows
    for t in range(limit, 7, -8):
        if rows % t == 0:
            return t
    return rows


def _sum8(stack, name):
    _, rows, cols = stack.shape
    tr = _row_tile(rows)

    def body(s_ref, o_ref):
        acc = s_ref[0].astype(F32)
        for j in range(1, N_DEV):
            acc = acc + s_ref[j].astype(F32)
        o_ref[...] = acc

    return pl.pallas_call(
        body, name=name, grid=(rows // tr,), out_shape=jax.ShapeDtypeStruct((rows, cols), F32),
        in_specs=[pl.BlockSpec((N_DEV, tr, cols), lambda i: (0, i, 0))],
        out_specs=pl.BlockSpec((tr, cols), lambda i: (i, 0)),
        compiler_params=_params(("arbitrary",)),
    )(stack)


def _adamw_update(w, gv, m, v):
    nm = ADAM_B1 * m + (1.0 - ADAM_B1) * gv
    nv = ADAM_B2 * v + (1.0 - ADAM_B2) * (gv * gv)
    m_hat = nm / (1.0 - ADAM_B1 ** ADAM_STEP)
    v_hat = nv / (1.0 - ADAM_B2 ** ADAM_STEP)
    return -ADAM_LR * (m_hat / (jnp.sqrt(v_hat) + ADAM_EPS) + ADAM_WD * w), nm, nv


SMALL_PARAMS = (("g_mix", 0, D), ("g_ffn", 1, D), ("g_final", 2, D), ("g_norm_a", 3, DK), ("ln_v_g", 4, KW),
                ("ln_v_b", 5, KW), ("b_s", 6, GROUPS * SGU_BLOCK))


def _small_finish(early, late, dws, gam, nb_ex, params):
    names = [n for n, _, _ in SMALL_PARAMS] + ["b_mod", "w_s"]

    def body(*refs):
        s_ref, l_ref, dws_ref, gam_ref = refs[:4]
        p_refs = refs[4:4 + 3 * len(names)]
        tot_ref, dgam_ref = refs[4 + 3 * len(names):6 + 3 * len(names)]
        o_refs = refs[6 + 3 * len(names):]
        acc = s_ref[0] + l_ref[0]
        gws = dws_ref[0]
        for j in range(1, N_DEV):
            acc = acc + (s_ref[j] + l_ref[j])
            gws = gws + dws_ref[j]
        tot_ref[...] = acc
        bm = acc[8:8 + N_MOD, :]
        for e in range(nb_ex):
            bm = bm + acc[16 + e * N_MOD:16 + (e + 1) * N_MOD, :]
        lb = jnp.concatenate([_lower_bound(gam_ref, 0), _lower_bound(gam_ref, 1)], axis=1)
        dgam = acc[7:8, :] * lb * (1.0 - lb)
        dgam_ref[...] = jnp.concatenate([dgam, -dgam], axis=0)
        grads = [acc[row:row + 1, 0:width] for _, row, width in SMALL_PARAMS] + [bm, gws]
        for k, g in enumerate(grads):
            w_ref, m_ref, v_ref = p_refs[3 * k:3 * k + 3]
            o_refs[4 * k][...] = g
            o_refs[4 * k + 1][...], o_refs[4 * k + 2][...], o_refs[4 * k + 3][...] = _adamw_update(
                w_ref[...], g, m_ref[...], v_ref[...])

    p_args, p_specs, o_shapes, o_specs = [], [], [], []
    for n in names:
        for a in params[n]:
            p_args.append(a)
            p_specs.append(_full(a.shape))
        o_shapes += [jax.ShapeDtypeStruct(params[n][0].shape, F32)] * 4
        o_specs += [_full(params[n][0].shape)] * 4
    res = pl.pallas_call(
        body, name="small_finish", grid=(1,),
        out_shape=[jax.ShapeDtypeStruct((SMALL_ROWS, D), F32), jax.ShapeDtypeStruct((2, D), F32)] + o_shapes,
        in_specs=[_full(early.shape), _full(late.shape), _full(dws.shape), _full((4, KW))] + p_specs,
        out_specs=[_full((SMALL_ROWS, D)), _full((2, D))] + o_specs,
        compiler_params=_params(("arbitrary",)),
    )(early, late, dws, gam, *p_args)
    return res[0], res[1], {n: res[2 + 4 * k:6 + 4 * k] for k, n in enumerate(names)}


def _adamw_sum8(stack, w, m, v, name):
    _, rows, cols = stack.shape
    tr = _row_tile(rows)

    def body(s_ref, w_ref, m_ref, v_ref, g_ref, d_ref, nm_ref, nv_ref):
        gv = s_ref[0].astype(F32)
        for j in range(1, N_DEV):
            gv = gv + s_ref[j].astype(F32)
        g_ref[...] = gv
        d_ref[...], nm_ref[...], nv_ref[...] = _adamw_update(w_ref[...], gv, m_ref[...], v_ref[...])

    blk = pl.BlockSpec((tr, cols), lambda i: (i, 0))
    sd = jax.ShapeDtypeStruct((rows, cols), F32)
    return pl.pallas_call(
        body, name=name, grid=(rows // tr,), out_shape=(sd, sd, sd, sd),
        in_specs=[pl.BlockSpec((N_DEV, tr, cols), lambda i: (0, i, 0)), blk, blk, blk], out_specs=(blk, blk, blk, blk),
        compiler_params=_params(("arbitrary",)),
    )(stack, w, m, v)


def _adamw(w, g, m, v, name):
    shape = w.shape
    cols = shape[-1]
    rows = 1
    for s in shape[:-1]:
        rows *= s
    tr = _row_tile(rows)

    def body(w_ref, g_ref, m_ref, v_ref, d_ref, nm_ref, nv_ref):
        gv = g_ref[...]
        nm = ADAM_B1 * m_ref[...] + (1.0 - ADAM_B1) * gv
        nv = ADAM_B2 * v_ref[...] + (1.0 - ADAM_B2) * (gv * gv)
        m_hat = nm / (1.0 - ADAM_B1 ** ADAM_STEP)
        v_hat = nv / (1.0 - ADAM_B2 ** ADAM_STEP)
        d_ref[...] = -ADAM_LR * (m_hat / (jnp.sqrt(v_hat) + ADAM_EPS) + ADAM_WD * w_ref[...])
        nm_ref[...] = nm
        nv_ref[...] = nv

    blk = pl.BlockSpec((tr, cols), lambda i: (i, 0))
    sd = jax.ShapeDtypeStruct((rows, cols), F32)
    d, nm, nv = pl.pallas_call(
        body, name=name, grid=(rows // tr,), out_shape=(sd, sd, sd), in_specs=[blk] * 4, out_specs=(blk, blk, blk),
        compiler_params=_params(("arbitrary",)),
    )(w.reshape(rows, cols), g.reshape(rows, cols), m.reshape(rows, cols), v.reshape(rows, cols))
    return d.reshape(shape), nm.reshape(shape), nv.reshape(shape)


def _owner_blocks(a):
    return a.reshape(N_DEV, a.shape[0] // N_DEV, a.shape[1])


class _LocalWeights:
    def __init__(self, w_upT, w_down, w_o, w_paT, w_pbT):
        self.weights = (w_upT, w_down, w_o, w_paT, w_pbT)
        self.items = {}

    def sender(self, stage, items=None):
        self.items[stage] = items
        return None

    def sent(self, stage, handle):
        pass

    def mixer_weights(self, after):
        return self.weights[1:]

    def ffn_weights(self, after):
        return self.weights[0]


def _local_step(x, ctx, target, modv, mcv, gam, g_mix, g_ffn, gna, ln_g, ln_b, w_s, b_s, g_final, w_inT, comm):
    nb_ex, seq, _ = x.shape
    ctx_len = ctx.shape[1]
    xt = x.reshape(nb_ex * seq, D)
    ct = ctx.reshape(nb_ex * ctx_len, D)
    tgt = target.reshape(nb_ex * seq, D)
    bs3 = b_s.reshape(GROUPS, SGU_BLOCK, 1)

    pc, hc, _ = _inproj(ct, mcv, g_mix, w_inT, CTX_COLS, ctx_len, "inproj_ctx")
    p, h, handle = _inproj(xt, modv, g_mix, w_inT, IN_COLS, seq, "inproj_lat", sender=comm.sender("inproj"))
    comm.sent("inproj", handle)
    cst_f, cst_b, cb_f, cb_b, s_ctx, _ = _hgrn_fwd(pc, gam, None, ctx_len, False, "hgrn_fwd_ctx")
    o_up, o_down, st_f, st_b, b_f, b_b, _, handle = _hgrn_fwd(p, gam, s_ctx, seq, True, "hgrn_fwd_lat",
                                                              sender=comm.sender("scan"))
    comm.sent("scan", handle)
    w_down, w_o, w_paT, w_pbT = comm.mixer_weights(o_up)
    x1, mix, merged, oa, obm = _tail_fwd(p, o_up, o_down, xt, modv, gna, ln_g, ln_b, w_s, bs3, w_paT, w_pbT, w_o, seq)
    w_upT = comm.ffn_weights(x1)
    dx1, h2, dffn, act, dup, dmod_ffn, small_ffn = _ffn(x1, tgt, modv, g_ffn, g_final, w_upT, w_down, seq)
    gw_upT, _ = _grad_matmul(dup, h2, "gw_up")
    gw_down, _ = _grad_matmul(act, dffn, "gw_down", tn=256)
    scatter = lambda *grads: [(_owner_blocks(g), "scatter") for g in grads]
    dpt, do, dmix, dpa, dpb, dmod_tail, small_tail, dws, dbs, handle = _tail_bwd(
        p, o_up, o_down, dx1, mix, modv, gna, ln_g, ln_b, w_s, bs3, w_paT, w_pbT, w_o, seq,
        sender=comm.sender("tail_bwd", scatter(gw_upT)))
    comm.sent("tail_bwd", handle)
    gw_o, _ = _grad_matmul(merged, dmix, "gw_o")
    gw_paT, _ = _grad_matmul(dpa, oa, "gw_pa")
    gw_pbT, _ = _grad_matmul(dpb, obm, "gw_pb")
    def at_row(row, a):
        return jnp.pad(a, ((row, SMALL_ROWS - row - a.shape[0]), (0, D - a.shape[1])))

    small_early = (at_row(1, small_ffn[0:2])
                   + at_row(3, small_tail[0:3])
                   + at_row(6, dbs.reshape(1, GROUPS * SGU_BLOCK))
                   + at_row(14, small_ffn[2:3]))
    dws_rows = dws.reshape(GROUPS * SGU_BLOCK, SGU_BLOCK)
    d_up, d_down, dlb, ds0, handle = _hgrn_bwd(
        p, gam, do, st_f, st_b, b_f, b_b, None, seq, True, "hgrn_bwd_lat",
        sender=comm.sender("scan_bwd", scatter(gw_down, gw_o, gw_paT, gw_pbT)
                           + [(small_early, "gather"), (dws_rows, "gather")]))
    comm.sent("scan_bwd", handle)
    c_up, c_down, dlb_c, _, _ = _hgrn_bwd(pc, gam, None, cst_f, cst_b, cb_f, cb_b, ds0, ctx_len, False, "hgrn_bwd_ctx")
    dpc, dmc, small_c, _ = _inproj_bwd(c_up, c_down, None, ct, None, mcv, g_mix, w_inT, ctx_len, "inproj_bwd_ctx")
    gw_inT, _ = _grad_in(d_up, d_down, dpt, h, _grad_matmul(dpc, hc, "gw_in_ctx")[0])
    grad_x, dmod_in, small_in, handle = _inproj_bwd(d_up, d_down, dpt, xt, dx1, modv, g_mix, w_inT, seq,
                                                   "inproj_bwd_lat", sender=comm.sender("inproj_bwd", scatter(gw_inT)))
    comm.sent("inproj_bwd", handle)
    dmod = dmod_in + dmod_tail + dmod_ffn
    small_late = (at_row(0, small_in[0:1] + small_c[0:1])
                  + at_row(7, (dlb + dlb_c).reshape(1, 2 * KW))
                  + at_row(8, dmc[0, 0:N_MOD])
                  + at_row(16, dmod[:, 0:N_MOD].reshape(nb_ex * N_MOD, D)))
    comm.sender("last", [(small_late, "gather")])
    return grad_x.reshape(x.shape)


def kernel(x, c, ctx, c_ctx, w_mod, b_mod, g_mix, g_ffn, w_in, lb_gamma, g_norm_a, ln_v_g, ln_v_b, w_s, b_s, w_pa, w_pb, w_o, w_up, w_down, g_final, loss_target, m_c_ctx, m_w_mod, m_b_mod, m_g_mix, m_g_ffn, m_w_in, m_lb_gamma, m_g_norm_a, m_ln_v_g, m_ln_v_b, m_w_s, m_b_s, m_w_pa, m_w_pb, m_w_o, m_w_up, m_w_down, m_g_final, v_c_ctx, v_w_mod, v_b_mod, v_g_mix, v_g_ffn, v_w_in, v_lb_gamma, v_g_norm_a, v_ln_v_g, v_ln_v_b, v_w_s, v_b_s, v_w_pa, v_w_pb, v_w_o, v_w_up, v_w_down, v_g_final):
    nb_ex = x.shape[0]
    me = 4 * lax.axis_index("x") + 2 * lax.axis_index("y") + lax.axis_index("c")
    cd = MXU_DTYPE
    mod_cols = w_mod.shape[2]
    lb_cols = lb_gamma.shape[2]

    w_inT_l = w_in[0].T.astype(cd)
    w_upT_l = w_up[0].T.astype(cd)
    w_paT_l = w_pa[0].T.astype(cd)
    w_pbT_l = w_pb[0].T.astype(cd)
    cl = jnp.concatenate([c, jnp.pad(lb_gamma.reshape(1, 4 * lb_cols), ((0, 0), (0, D - 4 * lb_cols))),
                          jnp.zeros((8 - nb_ex - 1, D), F32)], axis=0)
    g_in, g_cl = _gather_two_level([w_inT_l, cl], "gather_w_in")
    w_inT = g_in.reshape(IN_COLS, D)
    c_all = g_cl[:, 0:nb_ex].reshape(N_DEV * nb_ex, D)
    gam = jnp.transpose(g_cl[:, nb_ex, 0:4 * lb_cols].reshape(N_DEV, 4, lb_cols), (1, 0, 2)).reshape(4, KW)

    n_c = N_DEV * nb_ex
    cvec = jnp.concatenate([c_all, c_ctx.reshape(1, D), jnp.zeros((7, D), F32)], axis=0)
    b_mod_l = lax.dynamic_slice(b_mod, (0, me * mod_cols), (1, mod_cols))
    mod_l, svec = _mod_fwd(cvec, w_mod[0], b_mod_l)
    (g_mod,) = _gather_two_level([mod_l], "gather_mod")
    mod_all = jnp.transpose(g_mod, (1, 0, 2)).reshape(n_c + 8, N_MOD * D)
    modv = lax.dynamic_slice(mod_all, (me * nb_ex, 0), (nb_ex, N_MOD * D)).reshape(nb_ex, N_MOD, D)
    mcv = mod_all[n_c].reshape(1, N_MOD, D)

    handles, leftover = {}, {}

    class Comm:
        def sender(self, stage, items=None):
            if stage == "inproj":
                return _Sender([(w_down[0].astype(cd), "gather"), (w_o[0].astype(cd), "gather"), (w_paT_l, "gather"),
                                (w_pbT_l, "gather")])
            if stage == "scan":
                return _Sender([(w_upT_l, "gather")])
            if stage == "last":
                leftover["items"] = items
                return None
            return _Sender(items)

        def sent(self, stage, handle):
            handles[stage] = handle

        def mixer_weights(self, after):
            g_down, g_o, g_pa, g_pb = _exchange_wait(handles["inproj"], after)
            return g_down.reshape(D_FF, D), g_o.reshape(D, D), g_pa.reshape(D, KW), g_pb.reshape(D, KW)

        def ffn_weights(self, after):
            (g_up,) = _exchange_wait(handles["scan"], after)
            return g_up.reshape(2 * D_FF, D)

    grad_x = _local_step(
        x, ctx, loss_target, modv, mcv, gam, g_mix, g_ffn, g_norm_a, ln_v_g, ln_v_b, w_s[0], b_s[0],
        g_final.reshape(1, D), w_inT, Comm())
    last, last_started = _exchange_start(leftover["items"], "gather_small_late", after=leftover["items"][0][0])

    (r_up,) = _exchange_wait(handles["tail_bwd"], last_started)
    r_down, r_o, r_pa, r_pb, r_small, r_dws = _exchange_wait(handles["scan_bwd"], r_up)
    raw_up = _adamw_sum8(r_up, w_up[0].T, m_w_up[0].T, v_w_up[0].T, "adamw_w_up")
    raw_down = _adamw_sum8(r_down, w_down[0], m_w_down[0], v_w_down[0], "adamw_w_down")
    raw_o = _adamw_sum8(r_o, w_o[0], m_w_o[0], v_w_o[0], "adamw_w_o")
    (r_in,) = _exchange_wait(handles["inproj_bwd"], raw_up[1])
    raw_in = _adamw_sum8(r_in, w_in[0].T, m_w_in[0].T, v_w_in[0].T, "adamw_w_in")
    (r_late,) = _exchange_wait(last, raw_in[1])
    done = {"w_in": [a.T[None] for a in raw_in], "w_up": [a.T[None] for a in raw_up],
            "w_down": [a[None] for a in raw_down], "w_o": [a[None] for a in raw_o]}
    grad_w_in, grad_w_up, grad_w_down, grad_w_o = (done[k][0] for k in ("w_in", "w_up", "w_down", "w_o"))
    grad_w_pa = _sum8(r_pa, "sum_w_pa").T[None]
    grad_w_pb = _sum8(r_pb, "sum_w_pb").T[None]
    as_2d = {"g_final": (1, D), "b_s": (1, GROUPS * SGU_BLOCK), "b_mod": (N_MOD, D), "w_s": (GROUPS * SGU_BLOCK, SGU_BLOCK)}
    small_params = {"g_mix": (g_mix, m_g_mix, v_g_mix), "g_ffn": (g_ffn, m_g_ffn, v_g_ffn),
                    "g_final": (g_final, m_g_final, v_g_final), "g_norm_a": (g_norm_a, m_g_norm_a, v_g_norm_a),
                    "ln_v_g": (ln_v_g, m_ln_v_g, v_ln_v_g), "ln_v_b": (ln_v_b, m_ln_v_b, v_ln_v_b),
                    "b_s": (b_s, m_b_s, v_b_s), "b_mod": (b_mod, m_b_mod, v_b_mod), "w_s": (w_s, m_w_s, v_w_s)}
    tot, dgam, small_done = _small_finish(
        r_small, r_late, r_dws, gam, nb_ex,
        {n: tuple(a.reshape(as_2d.get(n, a.shape)) for a in wmv) for n, wmv in small_params.items()})
    for n, outs in small_done.items():
        done[n] = [a.reshape(small_params[n][0].shape) for a in outs]
    loss = tot[14, 0]
    grad_g_mix, grad_g_ffn, grad_g_final, grad_g_norm_a, grad_ln_v_g, grad_ln_v_b, grad_b_s, grad_b_mod, grad_w_s = (
        done[n][0] for n in ("g_mix", "g_ffn", "g_final", "g_norm_a", "ln_v_g", "ln_v_b", "b_s", "b_mod", "w_s"))
    grad_lb_gamma = lax.dynamic_slice(dgam.reshape(2, 2, KW), (0, 0, me * lb_cols), (2, 2, lb_cols))

    dmod_all = r_late[:, 16:16 + nb_ex * N_MOD].reshape(n_c, N_MOD * D)
    dmod_l = jnp.concatenate([lax.dynamic_slice(dmod_all, (0, me * mod_cols), (n_c, mod_cols)),
                              lax.dynamic_slice(tot[8:8 + N_MOD].reshape(1, N_MOD * D), (0, me * mod_cols), (1, mod_cols)),
                              jnp.zeros((7, mod_cols), F32)], axis=0)
    gw_mod, gc = _mod_bwd(svec, cvec, dmod_l, w_mod[0])
    grad_w_mod = gw_mod[None]
    (r_gc,) = _exchange([(gc[n_c:n_c + 8], "gather")], "gather_c_ctx", after=r_late)
    grad_c_ctx = _sum8(r_gc, "sum_c_ctx")[0]

    names = ["c_ctx", "w_mod", "b_mod", "g_mix", "g_ffn", "w_in", "lb_gamma", "g_norm_a", "ln_v_g", "ln_v_b", "w_s",
             "b_s", "w_pa", "w_pb", "w_o", "w_up", "w_down", "g_final"]
    weights = [c_ctx, w_mod, b_mod, g_mix, g_ffn, w_in, lb_gamma, g_norm_a, ln_v_g, ln_v_b, w_s, b_s, w_pa, w_pb, w_o,
               w_up, w_down, g_final]
    grads = [grad_c_ctx, grad_w_mod, grad_b_mod, grad_g_mix, grad_g_ffn, grad_w_in, grad_lb_gamma, grad_g_norm_a,
             grad_ln_v_g, grad_ln_v_b, grad_w_s, grad_b_s, grad_w_pa, grad_w_pb, grad_w_o, grad_w_up, grad_w_down,
             grad_g_final]
    ms = [m_c_ctx, m_w_mod, m_b_mod, m_g_mix, m_g_ffn, m_w_in, m_lb_gamma, m_g_norm_a, m_ln_v_g, m_ln_v_b, m_w_s, m_b_s,
          m_w_pa, m_w_pb, m_w_o, m_w_up, m_w_down, m_g_final]
    vs = [v_c_ctx, v_w_mod, v_b_mod, v_g_mix, v_g_ffn, v_w_in, v_lb_gamma, v_g_norm_a, v_ln_v_g, v_ln_v_b, v_w_s, v_b_s,
          v_w_pa, v_w_pb, v_w_o, v_w_up, v_w_down, v_g_final]
    deltas, new_ms, new_vs = [], [], []
    for nm, w, g, m, v in zip(names, weights, grads, ms, vs):
        d, nm_, nv_ = done[nm][1:] if nm in done else _adamw(w, g.reshape(w.shape), m, v, "adamw_" + nm)
        deltas.append(d)
        new_ms.append(nm_)
        new_vs.append(nv_)
    grads = [g.reshape(w.shape) for g, w in zip(grads, weights)]
    return (loss, grad_x, *grads, *deltas, *new_ms, *new_vs)
```

```python
import functools

import jax
import jax.numpy as jnp
from jax import lax
from jax.experimental import pallas as pl
from jax.experimental.pallas import tpu as pltpu

F32 = jnp.float32
MXU_DTYPE = jnp.bfloat16
PAYLOAD_DTYPE = jnp.bfloat16

N_DEV = 8
D = 1024
HEADS = 4
DK = 128
KW = HEADS * DK
CHUNK = 64
SGU_BLOCK = 128
GROUPS = 4
D_FF = 2816
FF_CHUNK = 256
N_MOD = 6
IN_COLS = 5632
CTX_COLS = 1536
TAIL_COLS = IN_COLS - 4 * KW
EPS = 1e-6
ADAM_LR, ADAM_B1, ADAM_B2, ADAM_EPS, ADAM_WD, ADAM_STEP = 0.001, 0.9, 0.999, 1e-08, 0.01, 10

VMEM_LIMIT = 56 * 1024 * 1024
TOKEN_TILE = 256
PROJ_TILE = 512
TAIL_TILE = 512
SMALL_ROWS = 40


def _params(sem):
    return pltpu.CompilerParams(dimension_semantics=sem, vmem_limit_bytes=VMEM_LIMIT)


_DN = {"nn": (((1,), (0,)), ((), ())), "nt": (((1,), (1,)), ((), ())), "tn": (((0,), (0,)), ((), ()))}


def _dot(a, b, form="nn"):
    return lax.dot_general(a.astype(MXU_DTYPE), b.astype(MXU_DTYPE), _DN[form], preferred_element_type=F32)


def _mask_dot(mask, v):
    bf = jnp.bfloat16
    hi = v.astype(bf)
    mid = (v - hi.astype(F32)).astype(bf)
    w = v.shape[1]
    s = lax.dot_general(mask.astype(bf), jnp.concatenate([hi, mid], axis=1), _DN["nn"], preferred_element_type=F32)
    return s[:, w:] + s[:, :w]


def _full(shape, single=False):
    n = len(shape)
    if single:
        return pl.BlockSpec(shape, lambda *_: (0,) * n, pipeline_mode=pl.Buffered(1))
    return pl.BlockSpec(shape, lambda *_: (0,) * n)


def _ordered_behind(body, in_specs, args, after):
    if after is None:
        return body
    at = len(in_specs)
    in_specs.append(pl.BlockSpec(memory_space=pl.ANY))
    args.append(after)
    return lambda *refs: body(*refs[:at], *refs[at + 1:])


def _sigmoid(z):
    return 0.5 * jnp.tanh(0.5 * z) + 0.5


def _gelu(x):
    c = 0.7978845608028654
    t = jnp.tanh(c * (x + 0.044715 * x * x * x))
    return 0.5 * x * (1.0 + t), t


def _gelu_grad(x, t):
    c = 0.7978845608028654
    return 0.5 * (1.0 + t) + 0.5 * x * (1.0 - t * t) * c * (1.0 + 3 * 0.044715 * x * x)


def _exchange(items, name, after=None):
    n = len(items)
    out_shape = []
    for a, mode in items:
        blk = a.shape if mode == "gather" else a.shape[1:]
        out_shape.append(jax.ShapeDtypeStruct((N_DEV,) + tuple(blk), a.dtype))

    def body(*refs):
        srcs, dsts = refs[:n], refs[n:2 * n]
        send_sems, recv_sems, local_sems = refs[2 * n:]
        x, y, c = lax.axis_index("x"), lax.axis_index("y"), lax.axis_index("c")
        me = 4 * x + 2 * y + c

        def src_for(i, dev):
            return srcs[i] if items[i][1] == "gather" else srcs[i].at[dev]

        local = [pltpu.make_async_copy(src_for(i, me), dsts[i].at[me], local_sems.at[i]) for i in range(n)]
        for cp in local:
            cp.start()
        remote = []
        for k in range(1, N_DEV):
            px = jnp.bitwise_xor(x, (k >> 2) & 1)
            py = jnp.bitwise_xor(y, (k >> 1) & 1)
            pc = jnp.bitwise_xor(c, k & 1)
            peer = 4 * px + 2 * py + pc
            for i in range(n):
                cp = pltpu.make_async_remote_copy(
                    src_ref=src_for(i, peer), dst_ref=dsts[i].at[me],
                    send_sem=send_sems.at[i * (N_DEV - 1) + k - 1], recv_sem=recv_sems.at[i * (N_DEV - 1) + k - 1],
                    device_id=(px, py, pc), device_id_type=pl.DeviceIdType.MESH)
                cp.start()
                remote.append(cp)
        for cp in remote:
            cp.wait()
        for cp in local:
            cp.wait()

    any_spec = pl.BlockSpec(memory_space=pl.ANY)
    in_specs, args = [any_spec] * n, [a for a, _ in items]
    if after is not None:
        in_specs.append(any_spec)
        args.append(after)
        exchange = body
        body = lambda *refs: exchange(*refs[:n], *refs[n + 1:])
    return pl.pallas_call(
        body, name=name, out_shape=out_shape, in_specs=in_specs, out_specs=[any_spec] * n,
        scratch_shapes=[pltpu.SemaphoreType.DMA((n * (N_DEV - 1),)), pltpu.SemaphoreType.DMA((n * (N_DEV - 1),)),
                        pltpu.SemaphoreType.DMA((n,))],
    )(*args)


def _gather_two_level(arrays, name):
    n = len(arrays)
    pieces = []
    for i, a in enumerate(arrays):
        rows = _Sender.PIECE_ROWS if a.shape[0] % _Sender.PIECE_ROWS == 0 else a.shape[0]
        pieces += [(i, r0, rows) for r0 in range(0, a.shape[0], rows)]

    def body(*refs):
        srcs, dsts = refs[:n], refs[n:2 * n]
        send_sems, recv_sems, local_sems = refs[2 * n:]
        x, y, c = lax.axis_index("x"), lax.axis_index("y"), lax.axis_index("c")
        me, sibling = (x, y, c), (x, y, 1 - c)
        x_nbr, y_nbr, diag = (1 - x, y, c), (x, 1 - y, c), (1 - x, 1 - y, c)

        def slot(px, py, pc):
            return 4 * px + 2 * py + pc

        def copy(u, k, block, to, own=False):
            i, r0, rows = pieces[u]
            there = dsts[i].at[slot(*block)].at[pl.ds(r0, rows)]
            return pltpu.make_async_remote_copy(
                src_ref=srcs[i].at[pl.ds(r0, rows)] if own else there, dst_ref=there,
                send_sem=send_sems.at[u * 7 + k], recv_sem=recv_sems.at[u * 7 + k],
                device_id=to, device_id_type=pl.DeviceIdType.MESH)

        units = range(len(pieces))
        mine = [pltpu.make_async_copy(srcs[i], dsts[i].at[slot(*me)], local_sems.at[i]) for i in range(n)]
        for cp in mine:
            cp.start()
        for u in units:
            copy(u, 1, me, x_nbr, own=True).start()
            copy(u, 2, me, y_nbr, own=True).start()
        for u in units:
            copy(u, 0, me, sibling, own=True).start()

        def relay_then_pass(k_from, frm, to, k_other, other):
            for u in units:
                copy(u, k_from, frm, me).wait_recv()
                copy(u, 3, frm, to).start()
                copy(u, 3 + k_from, frm, sibling).start()
            for u in units:
                copy(u, k_other, other, me).wait_recv()
                copy(u, 3 + k_other, other, sibling).start()

        @pl.when(c == 1)
        def _():
            relay_then_pass(1, x_nbr, y_nbr, 2, y_nbr)

        @pl.when(c == 0)
        def _():
            relay_then_pass(2, y_nbr, x_nbr, 1, x_nbr)

        for u in units:
            copy(u, 3, diag, me).wait_recv()
            copy(u, 6, diag, sibling).start()
        for u in units:
            copy(u, 0, sibling, me).wait_recv()
            for k, chip in ((4, x_nbr), (5, y_nbr), (6, diag)):
                copy(u, k, (chip[0], chip[1], 1 - c), me).wait_recv()
        for u in units:
            for k in range(7):
                copy(u, k, me, me, own=True).wait_send()
        for cp in mine:
            cp.wait()

    any_spec = pl.BlockSpec(memory_space=pl.ANY)
    return pl.pallas_call(
        body, name=name, out_shape=[jax.ShapeDtypeStruct((N_DEV,) + a.shape, a.dtype) for a in arrays],
        in_specs=[any_spec] * n, out_specs=[any_spec] * n,
        scratch_shapes=[pltpu.SemaphoreType.DMA((len(pieces) * 7,)), pltpu.SemaphoreType.DMA((len(pieces) * 7,)),
                        pltpu.SemaphoreType.DMA((n,))],
    )(*arrays)


_HBM = pl.BlockSpec(memory_space=pltpu.HBM)
_SEM = pl.BlockSpec(memory_space=pltpu.SEMAPHORE)
_EFFECT = pltpu.SideEffectType.DATAFLOW_SIDE_EFFECTING


def _split_copies(items, srcs, lands, send_sems, recv_sems):
    x, y, c = lax.axis_index("x"), lax.axis_index("y"), lax.axis_index("c")
    me = 4 * x + 2 * y + c
    copies = []
    for k in range(1, N_DEV):
        px = jnp.bitwise_xor(x, (k >> 2) & 1)
        py = jnp.bitwise_xor(y, (k >> 1) & 1)
        pc = jnp.bitwise_xor(c, k & 1)
        peer = 4 * px + 2 * py + pc
        for i in range(len(items)):
            src = srcs[i] if items[i][1] == "gather" else srcs[i].at[peer]
            copies.append(pltpu.make_async_remote_copy(
                src_ref=src, dst_ref=lands[i].at[me],
                send_sem=send_sems.at[i * (N_DEV - 1) + k - 1], recv_sem=recv_sems.at[i * (N_DEV - 1) + k - 1],
                device_id=(px, py, pc), device_id_type=pl.DeviceIdType.MESH))
    return me, copies


def _exchange_start(items, name, after):
    n = len(items)
    n_sem = n * (N_DEV - 1)
    srcs, lands = [], []
    for a, mode in items:
        blk = a.shape if mode == "gather" else a.shape[1:]
        srcs.append(pltpu.with_memory_space_constraint(a, pltpu.HBM))
        lands.append(pltpu.with_memory_space_constraint(lax.empty((N_DEV,) + tuple(blk), a.dtype), pltpu.HBM))

    def body(*refs):
        src_refs, land_refs = refs[:n], refs[n:2 * n]
        send_sems, recv_sems = refs[2 * n + 1], refs[2 * n + 2]
        local_sems = refs[4 * n + 3]
        me, copies = _split_copies(items, src_refs, land_refs, send_sems, recv_sems)
        for i in range(n):
            own = src_refs[i] if items[i][1] == "gather" else src_refs[i].at[me]
            cp = pltpu.make_async_copy(own, land_refs[i].at[me], local_sems.at[i])
            cp.start()
            cp.wait()
        for cp in copies:
            cp.start()

    out_shape = [pltpu.SemaphoreType.DMA((n_sem,)), pltpu.SemaphoreType.DMA((n_sem,))]
    out_shape += [pltpu.HBM(a.shape, a.dtype) for a in srcs] + [pltpu.HBM(a.shape, a.dtype) for a in lands]
    outs = pl.pallas_call(
        body, name=name, out_shape=out_shape,
        in_specs=[_HBM] * (2 * n) + [pl.BlockSpec(memory_space=pl.ANY)],
        out_specs=[_SEM, _SEM] + [_HBM] * (2 * n),
        input_output_aliases={i: 2 + i for i in range(2 * n)},
        scratch_shapes=[pltpu.SemaphoreType.DMA((n,))],
        compiler_params=pltpu.CompilerParams(has_side_effects=_EFFECT),
    )(*srcs, *lands, after)
    handle = (items, name, outs[0], outs[1], outs[2:2 + n], outs[2 + n:2 + 2 * n])
    return handle, outs[2]


class _Sender:
    PIECE_ROWS = 352

    def __init__(self, items, chunks=None):
        self.items, self.n = items, len(items)
        self.chunks = chunks
        if chunks is None:
            block_rows = [a.shape[0] if mode == "gather" else a.shape[1] for a, mode in items]
            self.chunks = [r // self.PIECE_ROWS if r % self.PIECE_ROWS == 0 else 1 for r in block_rows]
        self.srcs, self.lands = [], []
        for a, mode in items:
            blk = a.shape if mode == "gather" else a.shape[1:]
            self.srcs.append(pltpu.with_memory_space_constraint(a, pltpu.HBM))
            self.lands.append(pltpu.with_memory_space_constraint(lax.empty((N_DEV,) + tuple(blk), a.dtype), pltpu.HBM))

    def issue(self, src_refs, land_refs, send_sems, recv_sems, local_sems, step, n_steps):
        x, y, c = lax.axis_index("x"), lax.axis_index("y"), lax.axis_index("c")
        me = 4 * x + 2 * y + c
        copies = []
        for ch in range(max(self.chunks)):
            for k in range(1, N_DEV):
                px = jnp.bitwise_xor(x, (k >> 2) & 1)
                py = jnp.bitwise_xor(y, (k >> 1) & 1)
                pc = jnp.bitwise_xor(c, k & 1)
                peer = 4 * px + 2 * py + pc
                for i, (_, mode) in enumerate(self.items):
                    if ch >= self.chunks[i]:
                        continue
                    n_rows = land_refs[i].shape[1] // self.chunks[i]
                    rows = pl.ds(ch * n_rows, n_rows)
                    src = src_refs[i].at[rows] if mode == "gather" else src_refs[i].at[peer].at[rows]
                    copies.append(pltpu.make_async_remote_copy(
                        src_ref=src, dst_ref=land_refs[i].at[me].at[rows],
                        send_sem=send_sems.at[i * (N_DEV - 1) + k - 1], recv_sem=recv_sems.at[i * (N_DEV - 1) + k - 1],
                        device_id=(px, py, pc), device_id_type=pl.DeviceIdType.MESH))
        own = [pltpu.make_async_copy(src_refs[i] if mode == "gather" else src_refs[i].at[me], land_refs[i].at[me],
                                     local_sems.at[i]) for i, (_, mode) in enumerate(self.items)]

        @pl.when(step == 0)
        def _():
            for cp in own:
                cp.start()

        for s in range(n_steps):
            group = [cp for j, cp in enumerate(copies) if (j * n_steps) // len(copies) == s]
            if group:
                @pl.when(step == s)
                def _(group=group):
                    for cp in group:
                        cp.start()

        @pl.when(step == n_steps - 1)
        def _():
            for cp in own:
                cp.wait()


def _host_call(body, name, grid, in_specs, args, out_shape, out_specs, scratch_shapes, after=None, sender=None):
    in_specs, args, out_shape, out_specs = list(in_specs), list(args), list(out_shape), list(out_specs)
    scratch_shapes = list(scratch_shapes)
    semantics = ("arbitrary",) * len(grid)
    body = _ordered_behind(body, in_specs, args, after)
    if sender is None:
        res = pl.pallas_call(body, name=name, grid=grid, in_specs=in_specs, out_specs=out_specs, out_shape=out_shape,
                             scratch_shapes=scratch_shapes, compiler_params=_params(semantics))(*args)
        return res, None
    n, n_in, n_out, n_scr = sender.n, len(in_specs), len(out_shape), len(scratch_shapes)
    n_sem = n * (N_DEV - 1)
    n_steps = 1
    for g in grid:
        n_steps *= g
    compute = body

    def body(*refs):
        ins, s_in = refs[:n_in], refs[n_in:n_in + 2 * n]
        o0 = n_in + 2 * n
        outs, s_out = refs[o0:o0 + n_out], refs[o0 + n_out:o0 + n_out + 2 + 2 * n]
        scr = refs[o0 + n_out + 2 + 2 * n:]
        compute(*ins, *outs, *scr[:n_scr])
        step = pl.program_id(0)
        for d in range(1, len(grid)):
            step = step * grid[d] + pl.program_id(d)
        sender.issue(s_in[:n], s_in[n:], s_out[0], s_out[1], scr[n_scr], step, n_steps)

    res = pl.pallas_call(
        body, name=name, grid=grid,
        in_specs=in_specs + [_HBM] * (2 * n), out_specs=out_specs + [_SEM, _SEM] + [_HBM] * (2 * n),
        out_shape=out_shape + [pltpu.SemaphoreType.DMA((n_sem,)), pltpu.SemaphoreType.DMA((n_sem,))]
        + [pltpu.HBM(a.shape, a.dtype) for a in sender.srcs] + [pltpu.HBM(a.shape, a.dtype) for a in sender.lands],
        input_output_aliases={n_in + j: n_out + 2 + j for j in range(2 * n)},
        scratch_shapes=scratch_shapes + [pltpu.SemaphoreType.DMA((n,))],
        compiler_params=pltpu.CompilerParams(dimension_semantics=semantics, vmem_limit_bytes=VMEM_LIMIT,
                                             has_side_effects=_EFFECT),
    )(*args, *sender.srcs, *sender.lands)
    handle = (sender.items, name, res[n_out], res[n_out + 1], res[n_out + 2:n_out + 2 + n],
              res[n_out + 2 + n:n_out + 2 + 2 * n])
    return res[:n_out], handle


def _exchange_wait(handle, after):
    items, name, send_sems, recv_sems, srcs, lands = handle
    n = len(items)

    def body(*refs):
        src_refs, land_refs = refs[:n], refs[n:2 * n]
        send_ref, recv_ref = refs[2 * n], refs[2 * n + 1]
        _, copies = _split_copies(items, src_refs, land_refs, send_ref, recv_ref)
        for cp in copies:
            cp.wait_send()
            cp.wait_recv()

    outs = pl.pallas_call(
        body, name=name + "_wait",
        out_shape=[pltpu.HBM(a.shape, a.dtype) for a in srcs] + [pltpu.HBM(a.shape, a.dtype) for a in lands],
        in_specs=[_HBM] * (2 * n) + [_SEM, _SEM, pl.BlockSpec(memory_space=pl.ANY)], out_specs=[_HBM] * (2 * n),
        input_output_aliases={i: i for i in range(2 * n)},
        compiler_params=pltpu.CompilerParams(has_side_effects=_EFFECT),
    )(*srcs, *lands, send_sems, recv_sems, after)
    return outs[n:]


def _mod_fwd(cvec, w_mod_l, b_mod_l):
    rows, cols = cvec.shape[0], w_mod_l.shape[1]

    def body(c_ref, w_ref, b_ref, o_ref, s_ref):
        cv = c_ref[...]
        s = cv * _sigmoid(cv)
        s_ref[...] = s
        o_ref[...] = _dot(s, w_ref[...]) + b_ref[...]

    return pl.pallas_call(
        body, name="mod_fwd",
        out_shape=(jax.ShapeDtypeStruct((rows, cols), F32), jax.ShapeDtypeStruct((rows, D), F32)),
        in_specs=[_full((rows, D)), _full((D, cols)), _full((1, cols))],
        out_specs=(_full((rows, cols)), _full((rows, D))), grid=(1,),
        compiler_params=_params(("arbitrary",)),
    )(cvec, w_mod_l, b_mod_l)


def _mod_bwd(svec, cvec, dmod_l, w_mod_l):
    rows, cols = dmod_l.shape

    def body(s_ref, c_ref, d_ref, w_ref, gw_ref, gc_ref):
        gw_ref[...] = _dot(s_ref[...], d_ref[...], "tn")
        cv = c_ref[...]
        sg = _sigmoid(cv)
        gc_ref[...] = _dot(d_ref[...], w_ref[...], "nt") * (sg * (1.0 + cv * (1.0 - sg)))

    return pl.pallas_call(
        body, name="mod_bwd",
        out_shape=(jax.ShapeDtypeStruct((D, cols), F32), jax.ShapeDtypeStruct((rows, D), F32)),
        in_specs=[_full((rows, D)), _full((rows, D)), _full((rows, cols)), _full((D, cols))],
        out_specs=(_full((D, cols)), _full((rows, D))), grid=(1,),
        compiler_params=_params(("arbitrary",)),
    )(svec, cvec, dmod_l, w_mod_l)


def _inproj(xt, modv, g, w_inT, n_cols, rows_per_example, name, after=None, sender=None):
    rows = xt.shape[0]
    tm = min(PROJ_TILE, rows_per_example)
    per_b = rows_per_example // tm
    shared_mod = modv.shape[0] == 1

    def body(x_ref, mod_ref, g_ref, w_ref, p_ref, h_ref):
        x = x_ref[...]
        r = lax.rsqrt(jnp.mean(x * x, axis=-1, keepdims=True) + EPS)
        h = (x * r * g_ref[...]) * (1.0 + mod_ref[0, 1:2, :]) + mod_ref[0, 0:1, :]
        hb = h.astype(MXU_DTYPE)
        h_ref[...] = hb
        for j in range(n_cols // KW):
            p_ref[:, j * KW:(j + 1) * KW] = _dot(hb, w_ref[j * KW:(j + 1) * KW, :], "nt").astype(p_ref.dtype)

    mod_idx = (lambda i: (0, 0, 0)) if shared_mod else (lambda i: (i // per_b, 0, 0))
    in_specs = [pl.BlockSpec((tm, D), lambda i: (i, 0)), pl.BlockSpec((1, N_MOD, D), mod_idx), _full((1, D)),
                pl.BlockSpec((n_cols, D), lambda i: (0, 0), pipeline_mode=pl.Buffered(1))]
    (p, h), handle = _host_call(
        body, name, (rows // tm,), in_specs, [xt, modv, g, w_inT],
        [jax.ShapeDtypeStruct((rows, n_cols), MXU_DTYPE), jax.ShapeDtypeStruct((rows, D), MXU_DTYPE)],
        [pl.BlockSpec((tm, n_cols), lambda i: (i, 0)), pl.BlockSpec((tm, D), lambda i: (i, 0))], [],
        after=after, sender=sender)
    return p, h, handle


def _tri(reverse, n):
    row = lax.broadcasted_iota(jnp.int32, (n, n), 0)
    col = lax.broadcasted_iota(jnp.int32, (n, n), 1)
    same = (row // CHUNK) == (col // CHUNK)
    return same & ((col >= row) if reverse else (col <= row))


def _per_chunk_rows(x, reverse):
    n = x.shape[0]
    rows = [x[j * CHUNK:j * CHUNK + 1] if reverse else x[(j + 1) * CHUNK - 1:(j + 1) * CHUNK] for j in range(n // CHUNK)]
    return jnp.concatenate([jnp.broadcast_to(r, (CHUNK, x.shape[1])) for r in rows], axis=0), rows


def _lower_bound(gam_ref, direction):
    return _sigmoid(gam_ref[direction:direction + 1, :] - gam_ref[2 + direction:3 + direction, :])


def _gate_prep(z, lb, tri, reverse, b=None):
    sg = _sigmoid(z)
    f = lb + (1.0 - lb) * sg
    g = jnp.log(f)
    b = _mask_dot(tri, g) if b is None else b
    bl, bl_rows = _per_chunk_rows(b, reverse)
    mid = 0.5 * bl
    return sg, g, 1.0 - f, b, jnp.exp(mid), [jnp.exp(0.5 * r) for r in bl_rows], jnp.exp(mid - b), mid


def _hgrn_fwd(p, gam, s0, rows_per_example, with_out, name, sender=None):
    rows = p.shape[0]
    nb_ex = rows // rows_per_example
    rb = min(TOKEN_TILE, rows_per_example)
    cpb = rb // CHUNK
    nb = rows_per_example // rb
    n_chunks = rows // CHUNK
    has_s0 = s0 is not None

    def body(*refs):
        it = iter(refs)
        gam_ref = next(it)
        zf_ref, vf_ref = next(it), next(it)
        qf_ref = next(it) if with_out else None
        zb_ref, vb_ref = next(it), next(it)
        qb_ref = next(it) if with_out else None
        s0_ref = next(it) if has_s0 else None
        if with_out:
            of_ref, ob_ref = next(it), next(it)
        stash_f, stash_b, bsum_f, bsum_b, fin_ref = next(it), next(it), next(it), next(it), next(it)
        st_ref = next(it)
        i = pl.program_id(1)

        @pl.when(i == 0)
        def _():
            if has_s0:
                st_ref[...] = s0_ref[:, 0]
            else:
                st_ref[...] = jnp.zeros_like(st_ref)

        for direction, (z_ref, v_ref, q_ref, stash, bsum_ref) in enumerate(
                ((zf_ref, vf_ref, qf_ref, stash_f, bsum_f), (zb_ref, vb_ref, qb_ref, stash_b, bsum_b))):
            reverse = direction == 1
            tri = _tri(reverse, rb)
            lb = _lower_bound(gam_ref, direction)
            z = z_ref[...].astype(F32)
            v = v_ref[...].astype(F32)
            _, _, k, b, em, em_rows, e2, mid = _gate_prep(z, lb, tri, reverse)
            bsum_ref[...] = b
            kd = (k * (e2 * em)).astype(MXU_DTYPE)
            vb = v.astype(MXU_DTYPE)
            if with_out:
                q = q_ref[...].astype(F32)
                qi = q * jnp.exp(b - mid)
                qe = (qi * em).astype(MXU_DTYPE)
                qi = qi.astype(MXU_DTYPE)
                ki = (k * e2).astype(MXU_DTYPE)
                intra = []
                for h in range(HEADS):
                    hs = slice(h * DK, (h + 1) * DK)
                    sc = jnp.where(tri, _dot(qi[:, hs], ki[:, hs], "nt"), 0.0)
                    intra.append(_dot(sc, vb[:, hs]))
            for j in (range(cpb - 1, -1, -1) if reverse else range(cpb)):
                rs = slice(j * CHUNK, (j + 1) * CHUNK)
                a = em_rows[j] * em_rows[j]
                for h in range(HEADS):
                    hs = slice(h * DK, (h + 1) * DK)
                    st = st_ref[direction, h]
                    stash[j, h] = st.astype(stash.dtype)
                    if with_out:
                        (ob_ref if reverse else of_ref)[rs, hs] = intra[h][rs] + _dot(qe[rs, hs], st, "nt")
                    st_ref[direction, h] = st * a[:, hs] + _dot(vb[rs, hs], kd[rs, hs], "tn")

        @pl.when(i == nb - 1)
        def _():
            fin_ref[:, 0] = st_ref[...]

    up = lambda b, i: b * nb + i
    down = lambda b, i: b * nb + nb - 1 - i
    col = lambda rowf, c: pl.BlockSpec((rb, KW), lambda b, i: (rowf(b, i), c))
    in_specs = [_full((4, KW)), col(up, 0), col(up, 2)] + ([col(up, 3)] if with_out else [])
    in_specs += [col(down, 1), col(down, 2)] + ([col(down, 3)] if with_out else [])
    args = [gam, p, p] + ([p] if with_out else []) + [p, p] + ([p] if with_out else [])
    if has_s0:
        in_specs.append(pl.BlockSpec((2, 1, HEADS, DK, DK), lambda b, i: (0, b, 0, 0, 0)))
        args.append(s0)
    out_shape, out_specs = [], []
    if with_out:
        out_shape += [jax.ShapeDtypeStruct((rows, KW), F32)] * 2
        out_specs += [pl.BlockSpec((rb, KW), lambda b, i: (up(b, i), 0)),
                      pl.BlockSpec((rb, KW), lambda b, i: (down(b, i), 0))]
    out_shape += [jax.ShapeDtypeStruct((n_chunks, HEADS, DK, DK), MXU_DTYPE)] * 2
    out_specs += [pl.BlockSpec((cpb, HEADS, DK, DK), lambda b, i: (up(b, i), 0, 0, 0)),
                  pl.BlockSpec((cpb, HEADS, DK, DK), lambda b, i: (down(b, i), 0, 0, 0))]
    out_shape += [jax.ShapeDtypeStruct((rows, KW), F32)] * 2
    out_specs += [pl.BlockSpec((rb, KW), lambda b, i: (up(b, i), 0)),
                  pl.BlockSpec((rb, KW), lambda b, i: (down(b, i), 0))]
    out_shape.append(jax.ShapeDtypeStruct((2, nb_ex, HEADS, DK, DK), F32))
    out_specs.append(pl.BlockSpec((2, 1, HEADS, DK, DK), lambda b, i: (0, b, 0, 0, 0)))
    res, handle = _host_call(body, name, (nb_ex, nb), in_specs, args, out_shape, out_specs,
                             [pltpu.VMEM((2, HEADS, DK, DK), F32)], sender=sender)
    return (*res, handle)


def _hgrn_bwd(p, gam, do, stash_f, stash_b, bsum_f, bsum_b, ds_end, rows_per_example, with_out, name, after=None,
              sender=None):
    rows = p.shape[0]
    nb_ex = rows // rows_per_example
    rb = min(TOKEN_TILE, rows_per_example)
    cpb = rb // CHUNK
    nb = rows_per_example // rb
    has_end = ds_end is not None

    def body(*refs):
        it = iter(refs)
        gam_ref = next(it)
        ins = []
        for _ in range(2):
            z_ref, v_ref = next(it), next(it)
            q_ref = next(it) if with_out else None
            do_ref = next(it) if with_out else None
            ins.append((z_ref, v_ref, q_ref, do_ref, next(it), next(it)))
        end_ref = next(it) if has_end else None
        outs = [next(it), next(it)]
        dlb_ref, ds0_ref = next(it), next(it)
        dst_ref = next(it)
        b_id, i = pl.program_id(0), pl.program_id(1)

        @pl.when(i == 0)
        def _():
            if has_end:
                dst_ref[...] = end_ref[:, 0]
            else:
                dst_ref[...] = jnp.zeros_like(dst_ref)

        @pl.when((i == 0) & (b_id == 0))
        def _():
            dlb_ref[...] = jnp.zeros_like(dlb_ref)

        for direction in range(2):
            z_ref, v_ref, q_ref, do_ref, stash, b_ref = ins[direction]
            dgrp_ref = outs[direction]
            reverse = direction == 1
            tri = _tri(reverse, rb)
            tri_t = _tri(not reverse, rb)
            lb = _lower_bound(gam_ref, direction)
            heads = [slice(h * DK, (h + 1) * DK) for h in range(HEADS)]
            chunks = [slice(j * CHUNK, (j + 1) * CHUNK) for j in range(cpb)]
            grid_cat = lambda parts: jnp.concatenate([jnp.concatenate(row, axis=1) for row in parts], axis=0)
            cat = lambda parts: jnp.concatenate(parts, axis=1)
            z = z_ref[...].astype(F32)
            sg, g, k, b, em, em_rows, e2, mid = _gate_prep(z, lb, tri, reverse, b=b_ref[...])
            e3 = e2 * em
            kd = k * e3
            kd_b = kd.astype(MXU_DTYPE)
            vb = v_ref[...].astype(MXU_DTYPE)
            if with_out:
                q = q_ref[...].astype(F32)
                dout = do_ref[...].astype(MXU_DTYPE)
                e1 = jnp.exp(b - mid)
                e4 = e1 * em
                qi, ki, qe = q * e1, k * e2, q * e4
                qi_b, ki_b, qe_b = qi.astype(MXU_DTYPE), ki.astype(MXU_DTYPE), qe.astype(MXU_DTYPE)
                dqi_p, dki_p, dv_p = [], [], []
                for hs in heads:
                    sc = jnp.where(tri, _dot(qi_b[:, hs], ki_b[:, hs], "nt"), 0.0)
                    dsc = jnp.where(tri, _dot(dout[:, hs], vb[:, hs], "nt"), 0.0)
                    dqi_p.append(_dot(dsc, ki_b[:, hs]))
                    dki_p.append(_dot(dsc, qi_b[:, hs], "tn"))
                    dv_p.append(_dot(sc, dout[:, hs], "tn"))
                dqi, dki, dv = cat(dqi_p), cat(dki_p), cat(dv_p)
                dqe = grid_cat([[_dot(dout[rs, hs], stash[j, h]) for h, hs in enumerate(heads)]
                                for j, rs in enumerate(chunks)])
                grow = [[_dot(dout[rs, hs], qe_b[rs, hs], "tn") for hs in heads] for rs in chunks]
            dkd_p = [[None] * HEADS for _ in range(cpb)]
            dvs_p = [[None] * HEADS for _ in range(cpb)]
            da_p = [[None] * HEADS for _ in range(cpb)]
            for j in (range(cpb) if reverse else range(cpb - 1, -1, -1)):
                rs = chunks[j]
                a = em_rows[j] * em_rows[j]
                for h, hs in enumerate(heads):
                    dst = dst_ref[direction, h]
                    dkd_p[j][h] = _dot(vb[rs, hs], dst)
                    dvs_p[j][h] = _dot(kd_b[rs, hs], dst, "nt")
                    da_p[j][h] = jnp.broadcast_to(
                        jnp.sum(dst * stash[j, h].astype(F32), axis=0, keepdims=True), (CHUNK, DK))
                    new_dst = dst * a[:, hs]
                    dst_ref[direction, h] = new_dst + grow[j][h] if with_out else new_dst
            dkd, dvs, da = grid_cat(dkd_p), grid_cat(dvs_p), grid_cat(da_p)
            t_kd = dkd * kd
            dk = dkd * e3
            db = -t_kd
            tot = t_kd
            if with_out:
                dgrp_ref[:, KW:2 * KW] = (dvs + dv).astype(dgrp_ref.dtype)
                dgrp_ref[:, 2 * KW:] = (dqi * e1 + dqe * e4).astype(dgrp_ref.dtype)
                dk = dk + dki * e2
                t_qi, t_ki, t_qe = dqi * qi, dki * ki, dqe * qe
                db = db + t_qi - t_ki + t_qe
                tot = tot + 0.5 * (t_ki - t_qi)
            else:
                dgrp_ref[:, KW:2 * KW] = dvs.astype(dgrp_ref.dtype)
            dbl = jnp.concatenate([jnp.broadcast_to(jnp.sum(tot[rs], axis=0, keepdims=True), (CHUNK, KW))
                                   for rs in chunks], axis=0) + da * (em * em)
            dg = _mask_dot(tri_t, db) + dbl
            df = dg * jnp.exp(-g) - dk
            dgrp_ref[:, 0:KW] = (df * (1.0 - lb) * sg * (1.0 - sg)).astype(dgrp_ref.dtype)
            dlb_ref[direction:direction + 1, :] += jnp.sum(df * (1.0 - sg), axis=0, keepdims=True)

        @pl.when(i == nb - 1)
        def _():
            ds0_ref[:, 0] = dst_ref[...]

    rows_of = (lambda b, i: b * nb + nb - 1 - i, lambda b, i: b * nb + i)
    in_specs, args = [_full((4, KW))], [gam]
    for direction in range(2):
        rf = rows_of[direction]
        col = lambda c, rf=rf: pl.BlockSpec((rb, KW), lambda b, i: (rf(b, i), c))
        in_specs += [col(direction), col(2)]
        args += [p, p]
        if with_out:
            in_specs += [col(3), col(0)]
            args += [p, do]
        in_specs += [pl.BlockSpec((cpb, HEADS, DK, DK), lambda b, i, rf=rf: (rf(b, i), 0, 0, 0)), col(0)]
        args += [(stash_f, stash_b)[direction], (bsum_f, bsum_b)[direction]]
    if has_end:
        in_specs.append(pl.BlockSpec((2, 1, HEADS, DK, DK), lambda b, i: (0, b, 0, 0, 0)))
        args.append(ds_end)
    out_shape, out_specs = [], []
    for direction in range(2):
        rf = rows_of[direction]
        width = (3 if with_out else 2) * KW
        out_shape.append(jax.ShapeDtypeStruct((rows, width), MXU_DTYPE))
        out_specs.append(pl.BlockSpec((rb, width), lambda b, i, rf=rf: (rf(b, i), 0)))
    out_shape += [jax.ShapeDtypeStruct((2, KW), F32), jax.ShapeDtypeStruct((2, nb_ex, HEADS, DK, DK), F32)]
    out_specs += [_full((2, KW)), pl.BlockSpec((2, 1, HEADS, DK, DK), lambda b, i: (0, b, 0, 0, 0))]
    res, handle = _host_call(body, name, (nb_ex, nb), in_specs, args, out_shape, out_specs,
                             [pltpu.VMEM((2, HEADS, DK, DK), F32)], after=after, sender=sender)
    return (*res, handle)


def _tail_forward(osum, og, u, v, ga, gb, gna, ln_g, ln_b, ws_ref, bs_ref, wpaT_ref, wpbT_ref, proj=None):
    tm = osum.shape[0]
    gna4 = jnp.concatenate([gna] * HEADS, axis=1)
    r_parts = []
    for h in range(HEADS):
        oh = osum[:, h * DK:(h + 1) * DK]
        r_parts.append(jnp.broadcast_to(lax.rsqrt(jnp.mean(oh * oh, axis=-1, keepdims=True) + EPS), (tm, DK)))
    r = jnp.concatenate(r_parts, axis=1)
    on = osum * r
    sg_og = _sigmoid(og)
    silu_og = og * sg_og
    oan = on * gna4
    oa = oan * silu_og
    ug, tu = _gelu(u)
    vg, tv = _gelu(v)
    mu = jnp.mean(vg, axis=-1, keepdims=True)
    vc = vg - mu
    rstd = lax.rsqrt(jnp.mean(vc * vc, axis=-1, keepdims=True) + EPS)
    vhat = vc * rstd
    vln = vhat * ln_g + ln_b
    blocks = []
    for n in range(tm // SGU_BLOCK):
        rs = slice(n * SGU_BLOCK, (n + 1) * SGU_BLOCK)
        blocks.append(jnp.concatenate(
            [_dot(ws_ref[g], vln[rs, g * DK:(g + 1) * DK]) + bs_ref[g] for g in range(GROUPS)], axis=1))
    mixed = jnp.concatenate(blocks, axis=0) if len(blocks) > 1 else blocks[0]
    obm = ug * mixed
    if proj is None:
        pa = _dot(oa, wpaT_ref[...], "nt")
        pb = _dot(obm, wpbT_ref[...], "nt")
    else:
        pa, pb = proj
    sga, sgb = _sigmoid(ga), _sigmoid(gb)
    merged = sga * pa + sgb * pb
    return dict(r=r, on=on, sg_og=sg_og, silu_og=silu_og, oan=oan, oa=oa, ug=ug, tu=tu, tv=tv, rstd=rstd, vhat=vhat,
                vln=vln, mixed=mixed, obm=obm, pa=pa, pb=pb, sga=sga, sgb=sgb, merged=merged, gna4=gna4)


def _tail_in_specs(tm):
    tile = lambda c: pl.BlockSpec((tm, KW), lambda i: (i, c))
    return [tile(c) for c in range(4, 11)]


def _tail_weight_specs():
    return [_full((1, DK)), _full((1, KW)), _full((1, KW)), _full((GROUPS, SGU_BLOCK, SGU_BLOCK)),
            _full((GROUPS, SGU_BLOCK, 1)), _full((D, KW), single=True), _full((D, KW), single=True),
            _full((D, D), single=True)]


def _read_tail_inputs(of_ref, ob_ref, pcols):
    osum = of_ref[...] + ob_ref[...]
    og, u, v = (pcols[j][...].astype(F32) for j in range(3))
    ga = jnp.concatenate([pcols[3][...], pcols[4][...]], axis=1).astype(F32)
    gb = jnp.concatenate([pcols[5][...], pcols[6][...]], axis=1).astype(F32)
    return osum, og, u, v, ga, gb


def _tail_fwd(p, o_up, o_down, xt, modv, gna, ln_g, ln_b, w_s, b_s, w_paT, w_pbT, w_o, rows_per_example):
    rows = xt.shape[0]
    tm = min(TAIL_TILE, rows_per_example)
    per_b = rows_per_example // tm

    def body(of_ref, ob_ref, *rest):
        pcols = rest[:7]
        (x_ref, mod_ref, gna_ref, lng_ref, lnb_ref, ws_ref, bs_ref, wpaT_ref, wpbT_ref, wo_ref,
         x1_ref, mix_ref, merged_ref, oa_ref, obm_ref, pa_ref, pb_ref) = rest[7:]
        t = _tail_forward(*_read_tail_inputs(of_ref, ob_ref, pcols), gna_ref[...], lng_ref[...], lnb_ref[...],
                          ws_ref, bs_ref, wpaT_ref, wpbT_ref)
        mix = _dot(t["merged"], wo_ref[...])
        x1_ref[...] = x_ref[...] + mod_ref[0, 2:3, :] * mix
        mix_ref[...] = mix.astype(mix_ref.dtype)
        merged_ref[...] = t["merged"].astype(merged_ref.dtype)
        oa_ref[...] = t["oa"].astype(oa_ref.dtype)
        obm_ref[...] = t["obm"].astype(obm_ref.dtype)
        pa_ref[...] = t["pa"].astype(pa_ref.dtype)
        pb_ref[...] = t["pb"].astype(pb_ref.dtype)

    row = lambda w: pl.BlockSpec((tm, w), lambda i: (i, 0))
    in_specs = [row(KW), row(KW)] + _tail_in_specs(tm) + [row(D), pl.BlockSpec((1, N_MOD, D), lambda i: (i // per_b, 0, 0))]
    in_specs += _tail_weight_specs()
    return pl.pallas_call(
        body, name="tail_fwd", grid=(rows // tm,),
        out_shape=(jax.ShapeDtypeStruct((rows, D), F32), jax.ShapeDtypeStruct((rows, D), MXU_DTYPE),
                   jax.ShapeDtypeStruct((rows, D), MXU_DTYPE), jax.ShapeDtypeStruct((rows, KW), MXU_DTYPE),
                   jax.ShapeDtypeStruct((rows, KW), MXU_DTYPE), jax.ShapeDtypeStruct((rows, D), MXU_DTYPE),
                   jax.ShapeDtypeStruct((rows, D), MXU_DTYPE)),
        in_specs=in_specs, out_specs=(row(D), row(D), row(D), row(KW), row(KW), row(D), row(D)),
        compiler_params=_params(("arbitrary",)),
    )(o_up, o_down, *([p] * 7), xt, modv, gna, ln_g, ln_b, w_s, b_s, w_paT, w_pbT, w_o)


def _tail_bwd(p, o_up, o_down, dx1, mix, pa, pb, modv, gna, ln_g, ln_b, w_s, b_s, w_paT, w_pbT, w_o, rows_per_example,
              after=None, sender=None):
    rows = dx1.shape[0]
    nb_ex = rows // rows_per_example
    tm = min(TAIL_TILE, rows_per_example)
    per_b = rows_per_example // tm

    def body(of_ref, ob_ref, *rest):
        pcols = rest[:7]
        (dx1_ref, mix_ref, pa_ref, pb_ref, mod_ref, gna_ref, lng_ref, lnb_ref, ws_ref, bs_ref, wpaT_ref, wpbT_ref, wo_ref,
         dpt_ref, do_ref, dmix_ref, dpa_ref, dpb_ref, dmod_ref, small_ref, dws_ref, dbs_ref) = rest[7:]
        i = pl.program_id(0)

        @pl.when(i == 0)
        def _():
            small_ref[...] = jnp.zeros_like(small_ref)
            dws_ref[...] = jnp.zeros_like(dws_ref)
            dbs_ref[...] = jnp.zeros_like(dbs_ref)

        @pl.when(i % per_b == 0)
        def _():
            dmod_ref[...] = jnp.zeros_like(dmod_ref)

        osum, og, u, v, ga, gb = _read_tail_inputs(of_ref, ob_ref, pcols)
        ln_g = lng_ref[...]
        t = _tail_forward(osum, og, u, v, ga, gb, gna_ref[...], ln_g, lnb_ref[...], ws_ref, bs_ref, wpaT_ref, wpbT_ref,
                          proj=(pa_ref[...].astype(F32), pb_ref[...].astype(F32)))
        dx1v = dx1_ref[...]
        dmod_ref[0, 2:3, :] += jnp.sum(dx1v * mix_ref[...].astype(F32), axis=0, keepdims=True)
        dmix = dx1v * mod_ref[0, 2:3, :]
        dmix_ref[...] = dmix.astype(dmix_ref.dtype)
        dmerged = _dot(dmix, wo_ref[...], "nt")
        sga, sgb = t["sga"], t["sgb"]
        dpa = dmerged * sga
        dpb = dmerged * sgb
        dpa_ref[...] = dpa.astype(dpa_ref.dtype)
        dpb_ref[...] = dpb.astype(dpb_ref.dtype)
        dga = dmerged * t["pa"] * sga * (1.0 - sga)
        dgb = dmerged * t["pb"] * sgb * (1.0 - sgb)
        doa = _dot(dpa, wpaT_ref[...])
        dobm = _dot(dpb, wpbT_ref[...])
        dug = dobm * t["mixed"]
        dmixed = dobm * t["ug"]
        du = dug * _gelu_grad(u, t["tu"])
        dvln_blocks = []
        for n in range(tm // SGU_BLOCK):
            rs = slice(n * SGU_BLOCK, (n + 1) * SGU_BLOCK)
            parts = []
            for g in range(GROUPS):
                gs = slice(g * DK, (g + 1) * DK)
                dm = dmixed[rs, gs]
                parts.append(_dot(ws_ref[g], dm, "tn"))
                dws_ref[g] += _dot(dm, t["vln"][rs, gs], "nt")
                dbs_ref[g] += jnp.sum(dm, axis=1, keepdims=True)
            dvln_blocks.append(jnp.concatenate(parts, axis=1))
        dvln = jnp.concatenate(dvln_blocks, axis=0) if len(dvln_blocks) > 1 else dvln_blocks[0]
        vhat = t["vhat"]
        small_ref[1:2, 0:KW] += jnp.sum(dvln * vhat, axis=0, keepdims=True)
        small_ref[2:3, 0:KW] += jnp.sum(dvln, axis=0, keepdims=True)
        dvhat = dvln * ln_g
        dvg = t["rstd"] * (dvhat - jnp.mean(dvhat, axis=-1, keepdims=True)
                           - vhat * jnp.mean(dvhat * vhat, axis=-1, keepdims=True))
        dv = dvg * _gelu_grad(v, t["tv"])
        sg_og = t["sg_og"]
        doan = doa * t["silu_og"]
        dog = doa * t["oan"] * (sg_og * (1.0 + og * (1.0 - sg_og)))
        prod = doan * t["on"]
        dgna = jnp.zeros((1, DK), F32)
        for h in range(HEADS):
            dgna = dgna + jnp.sum(prod[:, h * DK:(h + 1) * DK], axis=0, keepdims=True)
        small_ref[0:1, 0:DK] += dgna
        don = doan * t["gna4"]
        dot_parts = []
        for h in range(HEADS):
            hs = slice(h * DK, (h + 1) * DK)
            m = jnp.mean(don[:, hs] * t["on"][:, hs], axis=-1, keepdims=True)
            dot_parts.append(t["r"][:, hs] * (don[:, hs] - t["on"][:, hs] * m))
        do_ref[...] = jnp.concatenate(dot_parts, axis=1).astype(do_ref.dtype)
        for j, val in enumerate((dog, du, dv)):
            dpt_ref[:, j * KW:(j + 1) * KW] = val.astype(dpt_ref.dtype)
        dpt_ref[:, 3 * KW:3 * KW + D] = dga.astype(dpt_ref.dtype)
        dpt_ref[:, 3 * KW + D:] = dgb.astype(dpt_ref.dtype)

    row = lambda w: pl.BlockSpec((tm, w), lambda i: (i, 0))
    in_specs = [row(KW), row(KW)] + _tail_in_specs(tm) + [row(D)] * 4 + [pl.BlockSpec((1, N_MOD, D), lambda i: (i // per_b, 0, 0))]
    in_specs += _tail_weight_specs()
    args = [o_up, o_down, *([p] * 7), dx1, mix, pa, pb, modv, gna, ln_g, ln_b, w_s, b_s, w_paT, w_pbT, w_o]
    cd = MXU_DTYPE
    res, handle = _host_call(
        body, "tail_bwd", (rows // tm,), in_specs, args,
        [jax.ShapeDtypeStruct((rows, TAIL_COLS), cd), jax.ShapeDtypeStruct((rows, KW), cd),
         jax.ShapeDtypeStruct((rows, D), cd), jax.ShapeDtypeStruct((rows, D), cd),
         jax.ShapeDtypeStruct((rows, D), cd), jax.ShapeDtypeStruct((nb_ex, 8, D), F32),
         jax.ShapeDtypeStruct((8, D), F32), jax.ShapeDtypeStruct((GROUPS, SGU_BLOCK, SGU_BLOCK), F32),
         jax.ShapeDtypeStruct((GROUPS, SGU_BLOCK, 1), F32)],
        [row(TAIL_COLS), row(KW), row(D), row(D), row(D),
         pl.BlockSpec((1, 8, D), lambda i: (i // per_b, 0, 0)), _full((8, D)),
         _full((GROUPS, SGU_BLOCK, SGU_BLOCK)), _full((GROUPS, SGU_BLOCK, 1))], [],
        after=after, sender=sender)
    return (*res, handle)


def _ffn(x1, target, modv, g_ffn, g_final, w_upT, w_down, rows_per_example):
    rows = x1.shape[0]
    nb_ex = rows // rows_per_example
    tm = min(TOKEN_TILE, rows_per_example)
    per_b = rows_per_example // tm
    n_ff = D_FF // FF_CHUNK

    def body(x1_ref, tgt_ref, mod_ref, gffn_ref, gfin_ref, wup_ref, wdn_ref,
             dx1_ref, h2_ref, dffn_ref, act_ref, dup_ref, dmod_ref, small_ref, up_scr):
        i = pl.program_id(0)

        @pl.when(i == 0)
        def _():
            small_ref[...] = jnp.zeros_like(small_ref)

        @pl.when(i % per_b == 0)
        def _():
            dmod_ref[...] = jnp.zeros_like(dmod_ref)

        x1v = x1_ref[...]
        g2 = gffn_ref[...]
        m3, m4, m5 = mod_ref[0, 3:4, :], mod_ref[0, 4:5, :], mod_ref[0, 5:6, :]
        r2 = lax.rsqrt(jnp.mean(x1v * x1v, axis=-1, keepdims=True) + EPS)
        xn2 = x1v * r2
        h2 = (xn2 * g2) * (1.0 + m4) + m3
        h2b = h2.astype(MXU_DTYPE)
        h2_ref[...] = h2b
        def up_pair(j):
            lo = j * FF_CHUNK
            return (_dot(h2b, wup_ref[lo:lo + FF_CHUNK, :], "nt"),
                    _dot(h2b, wup_ref[D_FF + lo:D_FF + lo + FF_CHUNK, :], "nt"))

        group_end = {min(e, n_ff): s for s, e in ((0, 4), (4, 8), (8, 12))}
        cur, ffn = up_pair(0), None
        for j in range(n_ff):
            nxt = up_pair(j + 1) if j + 1 < n_ff else None
            cs = slice(j * FF_CHUNK, (j + 1) * FF_CHUNK)
            a, bgate = cur
            up_scr[:, cs] = a
            up_scr[:, D_FF + j * FF_CHUNK:D_FF + (j + 1) * FF_CHUNK] = bgate
            act_ref[:, cs] = (a * _sigmoid(a) * bgate).astype(MXU_DTYPE)
            cur = nxt
            if j + 1 in group_end:
                gs = slice(group_end[j + 1] * FF_CHUNK, (j + 1) * FF_CHUNK)
                part = _dot(act_ref[:, gs], wdn_ref[gs, :])
                ffn = part if ffn is None else ffn + part
        x2 = x1v + m5 * ffn
        r3 = lax.rsqrt(jnp.mean(x2 * x2, axis=-1, keepdims=True) + EPS)
        xn3 = x2 * r3
        gf = gfin_ref[...]
        err = xn3 * gf - tgt_ref[...]
        loss = 0.5 * jnp.sum(jnp.mean(err * err, axis=-1, keepdims=True), axis=0, keepdims=True)
        small_ref[2:3, :] += jnp.broadcast_to(loss, (1, D))
        dy = err * (1.0 / D)
        small_ref[1:2, :] += jnp.sum(dy * xn3, axis=0, keepdims=True)
        dxn3 = dy * gf
        dx2 = r3 * (dxn3 - xn3 * jnp.mean(dxn3 * xn3, axis=-1, keepdims=True))
        dmod_ref[0, 5:6, :] += jnp.sum(dx2 * ffn, axis=0, keepdims=True)
        dffn = (dx2 * m5).astype(MXU_DTYPE)
        dffn_ref[...] = dffn
        dact_of = lambda j: _dot(dffn, wdn_ref[j * FF_CHUNK:(j + 1) * FF_CHUNK, :], "nt")
        cur, dh2 = dact_of(0), None
        for j in range(n_ff):
            nxt = dact_of(j + 1) if j + 1 < n_ff else None
            cs = slice(j * FF_CHUNK, (j + 1) * FF_CHUNK)
            a, bgate = up_scr[:, cs], up_scr[:, D_FF + j * FF_CHUNK:D_FF + (j + 1) * FF_CHUNK]
            s = _sigmoid(a)
            dup_ref[:, cs] = (cur * bgate * (s * (1.0 + a * (1.0 - s)))).astype(MXU_DTYPE)
            dup_ref[:, D_FF + j * FF_CHUNK:D_FF + (j + 1) * FF_CHUNK] = (cur * a * s).astype(MXU_DTYPE)
            cur = nxt
            if j + 1 in group_end:
                lo, hi = group_end[j + 1] * FF_CHUNK, (j + 1) * FF_CHUNK
                part = (_dot(dup_ref[:, lo:hi], wup_ref[lo:hi, :])
                        + _dot(dup_ref[:, D_FF + lo:D_FF + hi], wup_ref[D_FF + lo:D_FF + hi, :]))
                dh2 = part if dh2 is None else dh2 + part
        dmod_ref[0, 3:4, :] += jnp.sum(dh2, axis=0, keepdims=True)
        dmod_ref[0, 4:5, :] += jnp.sum(dh2 * xn2 * g2, axis=0, keepdims=True)
        small_ref[0:1, :] += jnp.sum(dh2 * (1.0 + m4) * xn2, axis=0, keepdims=True)
        dxn2 = dh2 * g2 * (1.0 + m4)
        dx1_ref[...] = dx2 + r2 * (dxn2 - xn2 * jnp.mean(dxn2 * xn2, axis=-1, keepdims=True))

    row = lambda w: pl.BlockSpec((tm, w), lambda i: (i, 0))
    cd = MXU_DTYPE
    return pl.pallas_call(
        body, name="ffn_fwd_bwd", grid=(rows // tm,),
        out_shape=(jax.ShapeDtypeStruct((rows, D), F32), jax.ShapeDtypeStruct((rows, D), cd),
                   jax.ShapeDtypeStruct((rows, D), cd), jax.ShapeDtypeStruct((rows, D_FF), cd),
                   jax.ShapeDtypeStruct((rows, 2 * D_FF), cd), jax.ShapeDtypeStruct((nb_ex, 8, D), F32),
                   jax.ShapeDtypeStruct((8, D), F32)),
        in_specs=[row(D), row(D), pl.BlockSpec((1, N_MOD, D), lambda i: (i // per_b, 0, 0)), _full((1, D)), _full((1, D)),
                  _full((2 * D_FF, D), single=True), _full((D_FF, D), single=True)],
        out_specs=(row(D), row(D), row(D), row(D_FF), row(2 * D_FF),
                   pl.BlockSpec((1, 8, D), lambda i: (i // per_b, 0, 0)), _full((8, D))),
        scratch_shapes=[pltpu.VMEM((tm, 2 * D_FF), F32)],
        compiler_params=_params(("arbitrary",)),
    )(x1, target, modv, g_ffn, g_final, w_upT, w_down)


def _scan_columns(up, down, n_groups):
    cols = [up[:, 0:KW].astype(F32), down[:, 0:KW].astype(F32)]
    for j in range(1, n_groups):
        cols.append(up[:, j * KW:(j + 1) * KW].astype(F32) + down[:, j * KW:(j + 1) * KW].astype(F32))
    return cols


def _inproj_bwd(d_up, d_down, dpt, xt, dx1, modv, g, w_inT, rows_per_example, name, sender=None):
    rows = xt.shape[0]
    latent = dx1 is not None
    n_cols = IN_COLS if latent else CTX_COLS
    n_groups = d_up.shape[1] // KW
    tm = min(PROJ_TILE, rows_per_example)
    per_b = rows_per_example // tm
    n_mod_blocks = rows // rows_per_example if latent else 1

    def body(*refs):
        it = iter(refs)
        up_ref, down_ref = next(it), next(it)
        dpt_ref = next(it) if latent else None
        x_ref = next(it)
        dx1_ref = next(it) if latent else None
        mod_ref, g_ref, w_ref = next(it), next(it), next(it)
        gx_ref = next(it) if latent else None
        dp_out = None if latent else next(it)
        dmod_ref, small_ref = next(it), next(it)
        dp_ref = next(it) if latent else dp_out
        i = pl.program_id(0)

        @pl.when(i == 0)
        def _():
            small_ref[...] = jnp.zeros_like(small_ref)

        @pl.when((i % per_b == 0) if latent else (i == 0))
        def _():
            dmod_ref[...] = jnp.zeros_like(dmod_ref)

        for j, val in enumerate(_scan_columns(up_ref[...], down_ref[...], n_groups)):
            dp_ref[:, j * KW:(j + 1) * KW] = val.astype(MXU_DTYPE)
        if latent:
            dh = _dot(dp_ref[...], w_ref[0:4 * KW, :]) + _dot(dpt_ref[...], w_ref[4 * KW:, :])
        else:
            dh = _dot(dp_ref[...], w_ref[...])
        x = x_ref[...]
        gv = g_ref[...]
        m1 = mod_ref[0, 1:2, :]
        r = lax.rsqrt(jnp.mean(x * x, axis=-1, keepdims=True) + EPS)
        xn = x * r
        dmod_ref[0, 0:1, :] += jnp.sum(dh, axis=0, keepdims=True)
        dmod_ref[0, 1:2, :] += jnp.sum(dh * xn * gv, axis=0, keepdims=True)
        small_ref[0:1, :] += jnp.sum(dh * (1.0 + m1) * xn, axis=0, keepdims=True)
        if latent:
            dxn = dh * gv * (1.0 + m1)
            gx_ref[...] = dx1_ref[...] + r * (dxn - xn * jnp.mean(dxn * xn, axis=-1, keepdims=True))

    row = lambda w: pl.BlockSpec((tm, w), lambda i: (i, 0))
    mod_idx = (lambda i: (i // per_b, 0, 0)) if latent else (lambda i: (0, 0, 0))
    in_specs = [row(n_groups * KW)] * 2 + ([row(TAIL_COLS)] if latent else []) + [row(D)] + ([row(D)] if latent else [])
    in_specs += [pl.BlockSpec((1, N_MOD, D), mod_idx), _full((1, D)),
                 pl.BlockSpec((n_cols, D), lambda i: (0, 0), pipeline_mode=pl.Buffered(1))]
    args = [d_up, d_down] + ([dpt] if latent else []) + [xt] + ([dx1] if latent else []) + [modv, g, w_inT]
    first = jax.ShapeDtypeStruct((rows, D), F32) if latent else jax.ShapeDtypeStruct((rows, n_cols), MXU_DTYPE)
    out_shape = [first, jax.ShapeDtypeStruct((n_mod_blocks, 8, D), F32), jax.ShapeDtypeStruct((8, D), F32)]
    out_specs = [row(D) if latent else row(n_cols), pl.BlockSpec((1, 8, D), mod_idx), _full((8, D))]
    scratch = [pltpu.VMEM((tm, 4 * KW), MXU_DTYPE)] if latent else []
    res, handle = _host_call(body, name, (rows // tm,), in_specs, args, out_shape, out_specs, scratch, sender=sender)
    return (*res, handle)


def _grad_matmul(a, b, name, init=None, tn=512, sender=None):
    rows, n = a.shape
    k = b.shape[1]
    tn = min(tn, n)
    has_init = init is not None
    init_blocks = init.shape[0] // tn if has_init else 0

    def body(*refs):
        if has_init:
            a_ref, b_ref, init_ref, o_ref = refs
        else:
            a_ref, b_ref, o_ref = refs
        g = _dot(a_ref[...], b_ref[...], "tn")
        if has_init:
            g = g + jnp.where(pl.program_id(0) < init_blocks, init_ref[...].astype(F32), 0.0)
        o_ref[...] = g.astype(o_ref.dtype)

    in_specs = [pl.BlockSpec((rows, tn), lambda i: (0, i)), _full((rows, k), single=True)]
    args = [a, b]
    if has_init:
        in_specs.append(pl.BlockSpec((tn, k), lambda i: (jnp.minimum(i, init_blocks - 1), 0)))
        args.append(init)
    (out,), handle = _host_call(
        body, name, (n // tn,), in_specs, args, [jax.ShapeDtypeStruct((n, k), PAYLOAD_DTYPE)],
        [pl.BlockSpec((tn, k), lambda i: (i, 0))], [], sender=sender)
    return out, handle


def _grad_in(d_up, d_down, dpt, h, init, sender=None):
    rows = h.shape[0]
    tn = 256
    per_group = KW // tn
    n_scan = 4 * per_group
    init_blocks = init.shape[0] // tn

    def body(up_ref, down_ref, dpt_ref, h_ref, init_ref, o_ref):
        i = pl.program_id(0)
        both = (up_ref[...].astype(F32) + down_ref[...].astype(F32)).astype(MXU_DTYPE)
        a = jnp.where(i < per_group, up_ref[...],
                      jnp.where(i < 2 * per_group, down_ref[...], jnp.where(i < n_scan, both, dpt_ref[...])))
        g = _dot(a, h_ref[...], "tn") + jnp.where(i < init_blocks, init_ref[...].astype(F32), 0.0)
        o_ref[...] = g.astype(o_ref.dtype)

    last = 3 * per_group - 1
    col = lambda f: pl.BlockSpec((rows, tn), lambda i: (0, f(i)))
    in_specs = [col(lambda i: jnp.clip(jnp.where(i < per_group, i, i - per_group), 0, last)),
                col(lambda i: jnp.clip(i - per_group, 0, last)),
                col(lambda i: jnp.clip(i - n_scan, 0, TAIL_COLS // tn - 1)),
                _full((rows, D), single=True),
                pl.BlockSpec((tn, D), lambda i: (jnp.minimum(i, init_blocks - 1), 0))]
    (out,), handle = _host_call(
        body, "gw_in", (IN_COLS // tn,), in_specs, [d_up, d_down, dpt, h, init],
        [jax.ShapeDtypeStruct((IN_COLS, D), PAYLOAD_DTYPE)], [pl.BlockSpec((tn, D), lambda i: (i, 0))], [],
        sender=sender)
    return out, handle


def _row_tile(rows, limit=256):
    if rows <= limit:
        return rows
    for t in range(limit, 7, -8):
        if rows % t == 0:
            return t
    return rows


def _sum8(stack, name):
    _, rows, cols = stack.shape
    tr = _row_tile(rows)

    def body(s_ref, o_ref):
        acc = s_ref[0].astype(F32)
        for j in range(1, N_DEV):
            acc = acc + s_ref[j].astype(F32)
        o_ref[...] = acc

    return pl.pallas_call(
        body, name=name, grid=(rows // tr,), out_shape=jax.ShapeDtypeStruct((rows, cols), F32),
        in_specs=[pl.BlockSpec((N_DEV, tr, cols), lambda i: (0, i, 0))],
        out_specs=pl.BlockSpec((tr, cols), lambda i: (i, 0)),
        compiler_params=_params(("arbitrary",)),
    )(stack)


def _adamw_update(w, gv, m, v):
    nm = ADAM_B1 * m + (1.0 - ADAM_B1) * gv
    nv = ADAM_B2 * v + (1.0 - ADAM_B2) * (gv * gv)
    m_hat = nm / (1.0 - ADAM_B1 ** ADAM_STEP)
    v_hat = nv / (1.0 - ADAM_B2 ** ADAM_STEP)
    return -ADAM_LR * (m_hat / (jnp.sqrt(v_hat) + ADAM_EPS) + ADAM_WD * w), nm, nv


SMALL_PARAMS = (("g_mix", 0, D), ("g_ffn", 1, D), ("g_final", 2, D), ("g_norm_a", 3, DK), ("ln_v_g", 4, KW),
                ("ln_v_b", 5, KW), ("b_s", 6, GROUPS * SGU_BLOCK))


def _small_finish(early, late, dws, gam, nb_ex, params):
    names = [n for n, _, _ in SMALL_PARAMS] + ["b_mod", "w_s"]

    def body(*refs):
        s_ref, l_ref, dws_ref, gam_ref = refs[:4]
        p_refs = refs[4:4 + 3 * len(names)]
        tot_ref, dgam_ref = refs[4 + 3 * len(names):6 + 3 * len(names)]
        o_refs = refs[6 + 3 * len(names):]
        acc = s_ref[0] + l_ref[0]
        gws = dws_ref[0]
        for j in range(1, N_DEV):
            acc = acc + (s_ref[j] + l_ref[j])
            gws = gws + dws_ref[j]
        tot_ref[...] = acc
        bm = acc[8:8 + N_MOD, :]
        for e in range(nb_ex):
            bm = bm + acc[16 + e * N_MOD:16 + (e + 1) * N_MOD, :]
        lb = jnp.concatenate([_lower_bound(gam_ref, 0), _lower_bound(gam_ref, 1)], axis=1)
        dgam = acc[7:8, :] * lb * (1.0 - lb)
        dgam_ref[...] = jnp.concatenate([dgam, -dgam], axis=0)
        grads = [acc[row:row + 1, 0:width] for _, row, width in SMALL_PARAMS] + [bm, gws]
        for k, g in enumerate(grads):
            w_ref, m_ref, v_ref = p_refs[3 * k:3 * k + 3]
            o_refs[4 * k][...] = g
            o_refs[4 * k + 1][...], o_refs[4 * k + 2][...], o_refs[4 * k + 3][...] = _adamw_update(
                w_ref[...], g, m_ref[...], v_ref[...])

    p_args, p_specs, o_shapes, o_specs = [], [], [], []
    for n in names:
        for a in params[n]:
            p_args.append(a)
            p_specs.append(_full(a.shape))
        o_shapes += [jax.ShapeDtypeStruct(params[n][0].shape, F32)] * 4
        o_specs += [_full(params[n][0].shape)] * 4
    res = pl.pallas_call(
        body, name="small_finish", grid=(1,),
        out_shape=[jax.ShapeDtypeStruct((SMALL_ROWS, D), F32), jax.ShapeDtypeStruct((2, D), F32)] + o_shapes,
        in_specs=[_full(early.shape), _full(late.shape), _full(dws.shape), _full((4, KW))] + p_specs,
        out_specs=[_full((SMALL_ROWS, D)), _full((2, D))] + o_specs,
        compiler_params=_params(("arbitrary",)),
    )(early, late, dws, gam, *p_args)
    return res[0], res[1], {n: res[2 + 4 * k:6 + 4 * k] for k, n in enumerate(names)}


def _adamw_sum8(stack, w, m, v, name):
    _, rows, cols = stack.shape
    tr = _row_tile(rows)

    def body(s_ref, w_ref, m_ref, v_ref, g_ref, d_ref, nm_ref, nv_ref):
        gv = s_ref[0].astype(F32)
        for j in range(1, N_DEV):
            gv = gv + s_ref[j].astype(F32)
        g_ref[...] = gv
        d_ref[...], nm_ref[...], nv_ref[...] = _adamw_update(w_ref[...], gv, m_ref[...], v_ref[...])

    blk = pl.BlockSpec((tr, cols), lambda i: (i, 0))
    sd = jax.ShapeDtypeStruct((rows, cols), F32)
    return pl.pallas_call(
        body, name=name, grid=(rows // tr,), out_shape=(sd, sd, sd, sd),
        in_specs=[pl.BlockSpec((N_DEV, tr, cols), lambda i: (0, i, 0)), blk, blk, blk], out_specs=(blk, blk, blk, blk),
        compiler_params=_params(("arbitrary",)),
    )(stack, w, m, v)


def _adamw(w, g, m, v, name):
    shape = w.shape
    cols = shape[-1]
    rows = 1
    for s in shape[:-1]:
        rows *= s
    tr = _row_tile(rows)

    def body(w_ref, g_ref, m_ref, v_ref, d_ref, nm_ref, nv_ref):
        gv = g_ref[...]
        nm = ADAM_B1 * m_ref[...] + (1.0 - ADAM_B1) * gv
        nv = ADAM_B2 * v_ref[...] + (1.0 - ADAM_B2) * (gv * gv)
        m_hat = nm / (1.0 - ADAM_B1 ** ADAM_STEP)
        v_hat = nv / (1.0 - ADAM_B2 ** ADAM_STEP)
        d_ref[...] = -ADAM_LR * (m_hat / (jnp.sqrt(v_hat) + ADAM_EPS) + ADAM_WD * w_ref[...])
        nm_ref[...] = nm
        nv_ref[...] = nv

    blk = pl.BlockSpec((tr, cols), lambda i: (i, 0))
    sd = jax.ShapeDtypeStruct((rows, cols), F32)
    d, nm, nv = pl.pallas_call(
        body, name=name, grid=(rows // tr,), out_shape=(sd, sd, sd), in_specs=[blk] * 4, out_specs=(blk, blk, blk),
        compiler_params=_params(("arbitrary",)),
    )(w.reshape(rows, cols), g.reshape(rows, cols), m.reshape(rows, cols), v.reshape(rows, cols))
    return d.reshape(shape), nm.reshape(shape), nv.reshape(shape)


def _owner_blocks(a):
    return a.reshape(N_DEV, a.shape[0] // N_DEV, a.shape[1])


class _LocalWeights:
    def __init__(self, w_upT, w_down, w_o, w_paT, w_pbT):
        self.weights = (w_upT, w_down, w_o, w_paT, w_pbT)
        self.items = {}

    def sender(self, stage, items=None):
        self.items[stage] = items
        return None

    def sent(self, stage, handle):
        pass

    def mixer_weights(self, after):
        return self.weights[1:]

    def ffn_weights(self, after):
        return self.weights[0]


def _local_step(x, ctx, target, modv, mcv, gam, g_mix, g_ffn, gna, ln_g, ln_b, w_s, b_s, g_final, w_inT, comm):
    nb_ex, seq, _ = x.shape
    ctx_len = ctx.shape[1]
    xt = x.reshape(nb_ex * seq, D)
    ct = ctx.reshape(nb_ex * ctx_len, D)
    tgt = target.reshape(nb_ex * seq, D)
    bs3 = b_s.reshape(GROUPS, SGU_BLOCK, 1)

    pc, hc, _ = _inproj(ct, mcv, g_mix, w_inT, CTX_COLS, ctx_len, "inproj_ctx")
    p, h, handle = _inproj(xt, modv, g_mix, w_inT, IN_COLS, seq, "inproj_lat", sender=comm.sender("inproj"))
    comm.sent("inproj", handle)
    cst_f, cst_b, cb_f, cb_b, s_ctx, _ = _hgrn_fwd(pc, gam, None, ctx_len, False, "hgrn_fwd_ctx")
    o_up, o_down, st_f, st_b, b_f, b_b, _, handle = _hgrn_fwd(p, gam, s_ctx, seq, True, "hgrn_fwd_lat",
                                                              sender=comm.sender("scan"))
    comm.sent("scan", handle)
    w_down, w_o, w_paT, w_pbT = comm.mixer_weights(o_up)
    x1, mix, merged, oa, obm, pa, pb = _tail_fwd(p, o_up, o_down, xt, modv, gna, ln_g, ln_b, w_s, bs3, w_paT, w_pbT,
                                                 w_o, seq)
    w_upT = comm.ffn_weights(x1)
    dx1, h2, dffn, act, dup, dmod_ffn, small_ffn = _ffn(x1, tgt, modv, g_ffn, g_final, w_upT, w_down, seq)
    gw_upT, _ = _grad_matmul(dup, h2, "gw_up")
    gw_down, _ = _grad_matmul(act, dffn, "gw_down", tn=256)
    scatter = lambda *grads: [(_owner_blocks(g), "scatter") for g in grads]
    dpt, do, dmix, dpa, dpb, dmod_tail, small_tail, dws, dbs, handle = _tail_bwd(
        p, o_up, o_down, dx1, mix, pa, pb, modv, gna, ln_g, ln_b, w_s, bs3, w_paT, w_pbT, w_o, seq,
        sender=comm.sender("tail_bwd", scatter(gw_upT)))
    comm.sent("tail_bwd", handle)
    gw_o, _ = _grad_matmul(merged, dmix, "gw_o")
    gw_paT, _ = _grad_matmul(dpa, oa, "gw_pa")
    gw_pbT, _ = _grad_matmul(dpb, obm, "gw_pb")
    def at_row(row, a):
        return jnp.pad(a, ((row, SMALL_ROWS - row - a.shape[0]), (0, D - a.shape[1])))

    small_early = (at_row(1, small_ffn[0:2])
                   + at_row(3, small_tail[0:3])
                   + at_row(6, dbs.reshape(1, GROUPS * SGU_BLOCK))
                   + at_row(14, small_ffn[2:3]))
    dws_rows = dws.reshape(GROUPS * SGU_BLOCK, SGU_BLOCK)
    d_up, d_down, dlb, ds0, handle = _hgrn_bwd(
        p, gam, do, st_f, st_b, b_f, b_b, None, seq, True, "hgrn_bwd_lat",
        sender=comm.sender("scan_bwd", scatter(gw_down, gw_o, gw_paT, gw_pbT)
                           + [(small_early, "gather"), (dws_rows, "gather")]))
    comm.sent("scan_bwd", handle)
    c_up, c_down, dlb_c, _, _ = _hgrn_bwd(pc, gam, None, cst_f, cst_b, cb_f, cb_b, ds0, ctx_len, False, "hgrn_bwd_ctx")
    dpc, dmc, small_c, _ = _inproj_bwd(c_up, c_down, None, ct, None, mcv, g_mix, w_inT, ctx_len, "inproj_bwd_ctx")
    gw_inT, _ = _grad_in(d_up, d_down, dpt, h, _grad_matmul(dpc, hc, "gw_in_ctx")[0])
    grad_x, dmod_in, small_in, handle = _inproj_bwd(d_up, d_down, dpt, xt, dx1, modv, g_mix, w_inT, seq,
                                                   "inproj_bwd_lat", sender=comm.sender("inproj_bwd", scatter(gw_inT)))
    comm.sent("inproj_bwd", handle)
    dmod = dmod_in + dmod_tail + dmod_ffn
    small_late = (at_row(0, small_in[0:1] + small_c[0:1])
                  + at_row(7, (dlb + dlb_c).reshape(1, 2 * KW))
                  + at_row(8, dmc[0, 0:N_MOD])
                  + at_row(16, dmod[:, 0:N_MOD].reshape(nb_ex * N_MOD, D)))
    comm.sender("last", [(small_late, "gather")])
    return grad_x.reshape(x.shape)


def kernel(x, c, ctx, c_ctx, w_mod, b_mod, g_mix, g_ffn, w_in, lb_gamma, g_norm_a, ln_v_g, ln_v_b, w_s, b_s, w_pa, w_pb, w_o, w_up, w_down, g_final, loss_target, m_c_ctx, m_w_mod, m_b_mod, m_g_mix, m_g_ffn, m_w_in, m_lb_gamma, m_g_norm_a, m_ln_v_g, m_ln_v_b, m_w_s, m_b_s, m_w_pa, m_w_pb, m_w_o, m_w_up, m_w_down, m_g_final, v_c_ctx, v_w_mod, v_b_mod, v_g_mix, v_g_ffn, v_w_in, v_lb_gamma, v_g_norm_a, v_ln_v_g, v_ln_v_b, v_w_s, v_b_s, v_w_pa, v_w_pb, v_w_o, v_w_up, v_w_down, v_g_final):
    nb_ex = x.shape[0]
    me = 4 * lax.axis_index("x") + 2 * lax.axis_index("y") + lax.axis_index("c")
    cd = MXU_DTYPE
    mod_cols = w_mod.shape[2]
    lb_cols = lb_gamma.shape[2]

    w_inT_l = w_in[0].T.astype(cd)
    w_upT_l = w_up[0].T.astype(cd)
    w_paT_l = w_pa[0].T.astype(cd)
    w_pbT_l = w_pb[0].T.astype(cd)
    cl = jnp.concatenate([c, jnp.pad(lb_gamma.reshape(1, 4 * lb_cols), ((0, 0), (0, D - 4 * lb_cols))),
                          jnp.zeros((8 - nb_ex - 1, D), F32)], axis=0)
    g_in, g_cl = _gather_two_level([w_inT_l, cl], "gather_w_in")
    w_inT = g_in.reshape(IN_COLS, D)
    c_all = g_cl[:, 0:nb_ex].reshape(N_DEV * nb_ex, D)
    gam = jnp.transpose(g_cl[:, nb_ex, 0:4 * lb_cols].reshape(N_DEV, 4, lb_cols), (1, 0, 2)).reshape(4, KW)

    n_c = N_DEV * nb_ex
    cvec = jnp.concatenate([c_all, c_ctx.reshape(1, D), jnp.zeros((7, D), F32)], axis=0)
    b_mod_l = lax.dynamic_slice(b_mod, (0, me * mod_cols), (1, mod_cols))
    mod_l, svec = _mod_fwd(cvec, w_mod[0], b_mod_l)
    (g_mod,) = _gather_two_level([mod_l], "gather_mod")
    mod_all = jnp.transpose(g_mod, (1, 0, 2)).reshape(n_c + 8, N_MOD * D)
    modv = lax.dynamic_slice(mod_all, (me * nb_ex, 0), (nb_ex, N_MOD * D)).reshape(nb_ex, N_MOD, D)
    mcv = mod_all[n_c].reshape(1, N_MOD, D)

    handles, leftover = {}, {}

    class Comm:
        def sender(self, stage, items=None):
            if stage == "inproj":
                return _Sender([(w_down[0].astype(cd), "gather"), (w_o[0].astype(cd), "gather"), (w_paT_l, "gather"),
                                (w_pbT_l, "gather")])
            if stage == "scan":
                return _Sender([(w_upT_l, "gather")])
            if stage == "last":
                leftover["items"] = items
                return None
            return _Sender(items)

        def sent(self, stage, handle):
            handles[stage] = handle

        def mixer_weights(self, after):
            g_down, g_o, g_pa, g_pb = _exchange_wait(handles["inproj"], after)
            return g_down.reshape(D_FF, D), g_o.reshape(D, D), g_pa.reshape(D, KW), g_pb.reshape(D, KW)

        def ffn_weights(self, after):
            (g_up,) = _exchange_wait(handles["scan"], after)
            return g_up.reshape(2 * D_FF, D)

    grad_x = _local_step(
        x, ctx, loss_target, modv, mcv, gam, g_mix, g_ffn, g_norm_a, ln_v_g, ln_v_b, w_s[0], b_s[0],
        g_final.reshape(1, D), w_inT, Comm())
    last, last_started = _exchange_start(leftover["items"], "gather_small_late", after=leftover["items"][0][0])

    (r_up,) = _exchange_wait(handles["tail_bwd"], last_started)
    r_down, r_o, r_pa, r_pb, r_small, r_dws = _exchange_wait(handles["scan_bwd"], r_up)
    raw_up = _adamw_sum8(r_up, w_up[0].T, m_w_up[0].T, v_w_up[0].T, "adamw_w_up")
    raw_down = _adamw_sum8(r_down, w_down[0], m_w_down[0], v_w_down[0], "adamw_w_down")
    raw_o = _adamw_sum8(r_o, w_o[0], m_w_o[0], v_w_o[0], "adamw_w_o")
    (r_in,) = _exchange_wait(handles["inproj_bwd"], raw_up[1])
    raw_in = _adamw_sum8(r_in, w_in[0].T, m_w_in[0].T, v_w_in[0].T, "adamw_w_in")
    (r_late,) = _exchange_wait(last, raw_in[1])
    done = {"w_in": [a.T[None] for a in raw_in], "w_up": [a.T[None] for a in raw_up],
            "w_down": [a[None] for a in raw_down], "w_o": [a[None] for a in raw_o]}
    grad_w_in, grad_w_up, grad_w_down, grad_w_o = (done[k][0] for k in ("w_in", "w_up", "w_down", "w_o"))
    grad_w_pa = _sum8(r_pa, "sum_w_pa").T[None]
    grad_w_pb = _sum8(r_pb, "sum_w_pb").T[None]
    as_2d = {"g_final": (1, D), "b_s": (1, GROUPS * SGU_BLOCK), "b_mod": (N_MOD, D), "w_s": (GROUPS * SGU_BLOCK, SGU_BLOCK)}
    small_params = {"g_mix": (g_mix, m_g_mix, v_g_mix), "g_ffn": (g_ffn, m_g_ffn, v_g_ffn),
                    "g_final": (g_final, m_g_final, v_g_final), "g_norm_a": (g_norm_a, m_g_norm_a, v_g_norm_a),
                    "ln_v_g": (ln_v_g, m_ln_v_g, v_ln_v_g), "ln_v_b": (ln_v_b, m_ln_v_b, v_ln_v_b),
                    "b_s": (b_s, m_b_s, v_b_s), "b_mod": (b_mod, m_b_mod, v_b_mod), "w_s": (w_s, m_w_s, v_w_s)}
    tot, dgam, small_done = _small_finish(
        r_small, r_late, r_dws, gam, nb_ex,
        {n: tuple(a.reshape(as_2d.get(n, a.shape)) for a in wmv) for n, wmv in small_params.items()})
    for n, outs in small_done.items():
        done[n] = [a.reshape(small_params[n][0].shape) for a in outs]
    loss = tot[14, 0]
    grad_g_mix, grad_g_ffn, grad_g_final, grad_g_norm_a, grad_ln_v_g, grad_ln_v_b, grad_b_s, grad_b_mod, grad_w_s = (
        done[n][0] for n in ("g_mix", "g_ffn", "g_final", "g_norm_a", "ln_v_g", "ln_v_b", "b_s", "b_mod", "w_s"))
    grad_lb_gamma = lax.dynamic_slice(dgam.reshape(2, 2, KW), (0, 0, me * lb_cols), (2, 2, lb_cols))

    dmod_all = r_late[:, 16:16 + nb_ex * N_MOD].reshape(n_c, N_MOD * D)
    dmod_l = jnp.concatenate([lax.dynamic_slice(dmod_all, (0, me * mod_cols), (n_c, mod_cols)),
                              lax.dynamic_slice(tot[8:8 + N_MOD].reshape(1, N_MOD * D), (0, me * mod_cols), (1, mod_cols)),
                              jnp.zeros((7, mod_cols), F32)], axis=0)
    gw_mod, gc = _mod_bwd(svec, cvec, dmod_l, w_mod[0])
    grad_w_mod = gw_mod[None]
    (r_gc,) = _exchange([(gc[n_c:n_c + 8], "gather")], "gather_c_ctx", after=r_late)
    grad_c_ctx = _sum8(r_gc, "sum_c_ctx")[0]

    names = ["c_ctx", "w_mod", "b_mod", "g_mix", "g_ffn", "w_in", "lb_gamma", "g_norm_a", "ln_v_g", "ln_v_b", "w_s",
             "b_s", "w_pa", "w_pb", "w_o", "w_up", "w_down", "g_final"]
    weights = [c_ctx, w_mod, b_mod, g_mix, g_ffn, w_in, lb_gamma, g_norm_a, ln_v_g, ln_v_b, w_s, b_s, w_pa, w_pb, w_o,
               w_up, w_down, g_final]
    grads = [grad_c_ctx, grad_w_mod, grad_b_mod, grad_g_mix, grad_g_ffn, grad_w_in, grad_lb_gamma, grad_g_norm_a,
             grad_ln_v_g, grad_ln_v_b, grad_w_s, grad_b_s, grad_w_pa, grad_w_pb, grad_w_o, grad_w_up, grad_w_down,
             grad_g_final]
    ms = [m_c_ctx, m_w_mod, m_b_mod, m_g_mix, m_g_ffn, m_w_in, m_lb_gamma, m_g_norm_a, m_ln_v_g, m_ln_v_b, m_w_s, m_b_s,
          m_w_pa, m_w_pb, m_w_o, m_w_up, m_w_down, m_g_final]
    vs = [v_c_ctx, v_w_mod, v_b_mod, v_g_mix, v_g_ffn, v_w_in, v_lb_gamma, v_g_norm_a, v_ln_v_g, v_ln_v_b, v_w_s, v_b_s,
          v_w_pa, v_w_pb, v_w_o, v_w_up, v_w_down, v_g_final]
    deltas, new_ms, new_vs = [], [], []
    for nm, w, g, m, v in zip(names, weights, grads, ms, vs):
        d, nm_, nv_ = done[nm][1:] if nm in done else _adamw(w, g.reshape(w.shape), m, v, "adamw_" + nm)
        deltas.append(d)
        new_ms.append(nm_)
        new_vs.append(nv_)
    grads = [g.reshape(w.shape) for g, w in zip(grads, weights)]
    return (loss, grad_x, *grads, *deltas, *new_ms, *new_vs)
```

```python
import functools

import jax
import jax.numpy as jnp
from jax import lax
from jax.experimental import pallas as pl
from jax.experimental.pallas import tpu as pltpu

F32 = jnp.float32
MXU_DTYPE = jnp.bfloat16
PAYLOAD_DTYPE = jnp.bfloat16

N_DEV = 8
D = 1024
HEADS = 4
DK = 128
KW = HEADS * DK
CHUNK = 64
SGU_BLOCK = 128
GROUPS = 4
D_FF = 2816
FF_CHUNK = 256
N_MOD = 6
IN_COLS = 5632
CTX_COLS = 1536
TAIL_COLS = IN_COLS - 4 * KW
EPS = 1e-6
ADAM_LR, ADAM_B1, ADAM_B2, ADAM_EPS, ADAM_WD, ADAM_STEP = 0.001, 0.9, 0.999, 1e-08, 0.01, 10

VMEM_LIMIT = 56 * 1024 * 1024
TOKEN_TILE = 256
PROJ_TILE = 512
TAIL_TILE = 512
SMALL_ROWS = 40


def _params(sem):
    return pltpu.CompilerParams(dimension_semantics=sem, vmem_limit_bytes=VMEM_LIMIT)


_DN = {"nn": (((1,), (0,)), ((), ())), "nt": (((1,), (1,)), ((), ())), "tn": (((0,), (0,)), ((), ()))}


def _dot(a, b, form="nn"):
    return lax.dot_general(a.astype(MXU_DTYPE), b.astype(MXU_DTYPE), _DN[form], preferred_element_type=F32)


def _mask_dot(mask, v):
    bf = jnp.bfloat16
    hi = v.astype(bf)
    mid = (v - hi.astype(F32)).astype(bf)
    w = v.shape[1]
    s = lax.dot_general(mask.astype(bf), jnp.concatenate([hi, mid], axis=1), _DN["nn"], preferred_element_type=F32)
    return s[:, w:] + s[:, :w]


def _full(shape, single=False):
    n = len(shape)
    if single:
        return pl.BlockSpec(shape, lambda *_: (0,) * n, pipeline_mode=pl.Buffered(1))
    return pl.BlockSpec(shape, lambda *_: (0,) * n)


def _ordered_behind(body, in_specs, args, after):
    if after is None:
        return body
    at = len(in_specs)
    in_specs.append(pl.BlockSpec(memory_space=pl.ANY))
    args.append(after)
    return lambda *refs: body(*refs[:at], *refs[at + 1:])


def _sigmoid(z):
    return 0.5 * jnp.tanh(0.5 * z) + 0.5


def _gelu(x):
    c = 0.7978845608028654
    t = jnp.tanh(c * (x + 0.044715 * x * x * x))
    return 0.5 * x * (1.0 + t), t


def _gelu_grad(x, t):
    c = 0.7978845608028654
    return 0.5 * (1.0 + t) + 0.5 * x * (1.0 - t * t) * c * (1.0 + 3 * 0.044715 * x * x)


def _exchange(items, name, after=None):
    n = len(items)
    out_shape = []
    for a, mode in items:
        blk = a.shape if mode == "gather" else a.shape[1:]
        out_shape.append(jax.ShapeDtypeStruct((N_DEV,) + tuple(blk), a.dtype))

    def body(*refs):
        srcs, dsts = refs[:n], refs[n:2 * n]
        send_sems, recv_sems, local_sems = refs[2 * n:]
        x, y, c = lax.axis_index("x"), lax.axis_index("y"), lax.axis_index("c")
        me = 4 * x + 2 * y + c

        def src_for(i, dev):
            return srcs[i] if items[i][1] == "gather" else srcs[i].at[dev]

        local = [pltpu.make_async_copy(src_for(i, me), dsts[i].at[me], local_sems.at[i]) for i in range(n)]
        for cp in local:
            cp.start()
        remote = []
        for k in range(1, N_DEV):
            px = jnp.bitwise_xor(x, (k >> 2) & 1)
            py = jnp.bitwise_xor(y, (k >> 1) & 1)
            pc = jnp.bitwise_xor(c, k & 1)
            peer = 4 * px + 2 * py + pc
            for i in range(n):
                cp = pltpu.make_async_remote_copy(
                    src_ref=src_for(i, peer), dst_ref=dsts[i].at[me],
                    send_sem=send_sems.at[i * (N_DEV - 1) + k - 1], recv_sem=recv_sems.at[i * (N_DEV - 1) + k - 1],
                    device_id=(px, py, pc), device_id_type=pl.DeviceIdType.MESH)
                cp.start()
                remote.append(cp)
        for cp in remote:
            cp.wait()
        for cp in local:
            cp.wait()

    any_spec = pl.BlockSpec(memory_space=pl.ANY)
    in_specs, args = [any_spec] * n, [a for a, _ in items]
    if after is not None:
        in_specs.append(any_spec)
        args.append(after)
        exchange = body
        body = lambda *refs: exchange(*refs[:n], *refs[n + 1:])
    return pl.pallas_call(
        body, name=name, out_shape=out_shape, in_specs=in_specs, out_specs=[any_spec] * n,
        scratch_shapes=[pltpu.SemaphoreType.DMA((n * (N_DEV - 1),)), pltpu.SemaphoreType.DMA((n * (N_DEV - 1),)),
                        pltpu.SemaphoreType.DMA((n,))],
    )(*args)


def _gather_two_level(arrays, name):
    n = len(arrays)
    pieces = []
    for i, a in enumerate(arrays):
        rows = _Sender.PIECE_ROWS if a.shape[0] % _Sender.PIECE_ROWS == 0 else a.shape[0]
        pieces += [(i, r0, rows) for r0 in range(0, a.shape[0], rows)]

    def body(*refs):
        srcs, dsts = refs[:n], refs[n:2 * n]
        send_sems, recv_sems, local_sems = refs[2 * n:]
        x, y, c = lax.axis_index("x"), lax.axis_index("y"), lax.axis_index("c")
        me, sibling = (x, y, c), (x, y, 1 - c)
        x_nbr, y_nbr, diag = (1 - x, y, c), (x, 1 - y, c), (1 - x, 1 - y, c)

        def slot(px, py, pc):
            return 4 * px + 2 * py + pc

        def copy(u, k, block, to, own=False):
            i, r0, rows = pieces[u]
            there = dsts[i].at[slot(*block)].at[pl.ds(r0, rows)]
            return pltpu.make_async_remote_copy(
                src_ref=srcs[i].at[pl.ds(r0, rows)] if own else there, dst_ref=there,
                send_sem=send_sems.at[u * 7 + k], recv_sem=recv_sems.at[u * 7 + k],
                device_id=to, device_id_type=pl.DeviceIdType.MESH)

        units = range(len(pieces))
        mine = [pltpu.make_async_copy(srcs[i], dsts[i].at[slot(*me)], local_sems.at[i]) for i in range(n)]
        for cp in mine:
            cp.start()
        for u in units:
            copy(u, 1, me, x_nbr, own=True).start()
            copy(u, 2, me, y_nbr, own=True).start()
        for u in units:
            copy(u, 0, me, sibling, own=True).start()

        def relay_then_pass(k_from, frm, to, k_other, other):
            for u in units:
                copy(u, k_from, frm, me).wait_recv()
                copy(u, 3, frm, to).start()
                copy(u, 3 + k_from, frm, sibling).start()
            for u in units:
                copy(u, k_other, other, me).wait_recv()
                copy(u, 3 + k_other, other, sibling).start()

        @pl.when(c == 1)
        def _():
            relay_then_pass(1, x_nbr, y_nbr, 2, y_nbr)

        @pl.when(c == 0)
        def _():
            relay_then_pass(2, y_nbr, x_nbr, 1, x_nbr)

        for u in units:
            copy(u, 3, diag, me).wait_recv()
            copy(u, 6, diag, sibling).start()
        for u in units:
            copy(u, 0, sibling, me).wait_recv()
            for k, chip in ((4, x_nbr), (5, y_nbr), (6, diag)):
                copy(u, k, (chip[0], chip[1], 1 - c), me).wait_recv()
        for u in units:
            for k in range(7):
                copy(u, k, me, me, own=True).wait_send()
        for cp in mine:
            cp.wait()

    any_spec = pl.BlockSpec(memory_space=pl.ANY)
    return pl.pallas_call(
        body, name=name, out_shape=[jax.ShapeDtypeStruct((N_DEV,) + a.shape, a.dtype) for a in arrays],
        in_specs=[any_spec] * n, out_specs=[any_spec] * n,
        scratch_shapes=[pltpu.SemaphoreType.DMA((len(pieces) * 7,)), pltpu.SemaphoreType.DMA((len(pieces) * 7,)),
                        pltpu.SemaphoreType.DMA((n,))],
    )(*arrays)


_HBM = pl.BlockSpec(memory_space=pltpu.HBM)
_SEM = pl.BlockSpec(memory_space=pltpu.SEMAPHORE)
_EFFECT = pltpu.SideEffectType.DATAFLOW_SIDE_EFFECTING


def _split_copies(items, srcs, lands, send_sems, recv_sems):
    x, y, c = lax.axis_index("x"), lax.axis_index("y"), lax.axis_index("c")
    me = 4 * x + 2 * y + c
    copies = []
    for k in range(1, N_DEV):
        px = jnp.bitwise_xor(x, (k >> 2) & 1)
        py = jnp.bitwise_xor(y, (k >> 1) & 1)
        pc = jnp.bitwise_xor(c, k & 1)
        peer = 4 * px + 2 * py + pc
        for i in range(len(items)):
            src = srcs[i] if items[i][1] == "gather" else srcs[i].at[peer]
            copies.append(pltpu.make_async_remote_copy(
                src_ref=src, dst_ref=lands[i].at[me],
                send_sem=send_sems.at[i * (N_DEV - 1) + k - 1], recv_sem=recv_sems.at[i * (N_DEV - 1) + k - 1],
                device_id=(px, py, pc), device_id_type=pl.DeviceIdType.MESH))
    return me, copies


def _exchange_start(items, name, after):
    n = len(items)
    n_sem = n * (N_DEV - 1)
    srcs, lands = [], []
    for a, mode in items:
        blk = a.shape if mode == "gather" else a.shape[1:]
        srcs.append(pltpu.with_memory_space_constraint(a, pltpu.HBM))
        lands.append(pltpu.with_memory_space_constraint(lax.empty((N_DEV,) + tuple(blk), a.dtype), pltpu.HBM))

    def body(*refs):
        src_refs, land_refs = refs[:n], refs[n:2 * n]
        send_sems, recv_sems = refs[2 * n + 1], refs[2 * n + 2]
        local_sems = refs[4 * n + 3]
        me, copies = _split_copies(items, src_refs, land_refs, send_sems, recv_sems)
        for i in range(n):
            own = src_refs[i] if items[i][1] == "gather" else src_refs[i].at[me]
            cp = pltpu.make_async_copy(own, land_refs[i].at[me], local_sems.at[i])
            cp.start()
            cp.wait()
        for cp in copies:
            cp.start()

    out_shape = [pltpu.SemaphoreType.DMA((n_sem,)), pltpu.SemaphoreType.DMA((n_sem,))]
    out_shape += [pltpu.HBM(a.shape, a.dtype) for a in srcs] + [pltpu.HBM(a.shape, a.dtype) for a in lands]
    outs = pl.pallas_call(
        body, name=name, out_shape=out_shape,
        in_specs=[_HBM] * (2 * n) + [pl.BlockSpec(memory_space=pl.ANY)],
        out_specs=[_SEM, _SEM] + [_HBM] * (2 * n),
        input_output_aliases={i: 2 + i for i in range(2 * n)},
        scratch_shapes=[pltpu.SemaphoreType.DMA((n,))],
        compiler_params=pltpu.CompilerParams(has_side_effects=_EFFECT),
    )(*srcs, *lands, after)
    handle = (items, name, outs[0], outs[1], outs[2:2 + n], outs[2 + n:2 + 2 * n])
    return handle, outs[2]


class _Sender:
    PIECE_ROWS = 352

    def __init__(self, items, chunks=None):
        self.items, self.n = items, len(items)
        self.chunks = chunks
        if chunks is None:
            block_rows = [a.shape[0] if mode == "gather" else a.shape[1] for a, mode in items]
            self.chunks = [r // self.PIECE_ROWS if r % self.PIECE_ROWS == 0 else 1 for r in block_rows]
        self.srcs, self.lands = [], []
        for a, mode in items:
            blk = a.shape if mode == "gather" else a.shape[1:]
            self.srcs.append(pltpu.with_memory_space_constraint(a, pltpu.HBM))
            self.lands.append(pltpu.with_memory_space_constraint(lax.empty((N_DEV,) + tuple(blk), a.dtype), pltpu.HBM))

    def issue(self, src_refs, land_refs, send_sems, recv_sems, local_sems, step, n_steps):
        x, y, c = lax.axis_index("x"), lax.axis_index("y"), lax.axis_index("c")
        me = 4 * x + 2 * y + c
        copies = []
        for ch in range(max(self.chunks)):
            for k in range(1, N_DEV):
                px = jnp.bitwise_xor(x, (k >> 2) & 1)
                py = jnp.bitwise_xor(y, (k >> 1) & 1)
                pc = jnp.bitwise_xor(c, k & 1)
                peer = 4 * px + 2 * py + pc
                for i, (_, mode) in enumerate(self.items):
                    if ch >= self.chunks[i]:
                        continue
                    n_rows = land_refs[i].shape[1] // self.chunks[i]
                    rows = pl.ds(ch * n_rows, n_rows)
                    src = src_refs[i].at[rows] if mode == "gather" else src_refs[i].at[peer].at[rows]
                    copies.append(pltpu.make_async_remote_copy(
                        src_ref=src, dst_ref=land_refs[i].at[me].at[rows],
                        send_sem=send_sems.at[i * (N_DEV - 1) + k - 1], recv_sem=recv_sems.at[i * (N_DEV - 1) + k - 1],
                        device_id=(px, py, pc), device_id_type=pl.DeviceIdType.MESH))
        own = [pltpu.make_async_copy(src_refs[i] if mode == "gather" else src_refs[i].at[me], land_refs[i].at[me],
                                     local_sems.at[i]) for i, (_, mode) in enumerate(self.items)]

        @pl.when(step == 0)
        def _():
            for cp in own:
                cp.start()

        for s in range(n_steps):
            group = [cp for j, cp in enumerate(copies) if (j * n_steps) // len(copies) == s]
            if group:
                @pl.when(step == s)
                def _(group=group):
                    for cp in group:
                        cp.start()

        @pl.when(step == n_steps - 1)
        def _():
            for cp in own:
                cp.wait()


def _host_call(body, name, grid, in_specs, args, out_shape, out_specs, scratch_shapes, after=None, sender=None):
    in_specs, args, out_shape, out_specs = list(in_specs), list(args), list(out_shape), list(out_specs)
    scratch_shapes = list(scratch_shapes)
    semantics = ("arbitrary",) * len(grid)
    body = _ordered_behind(body, in_specs, args, after)
    if sender is None:
        res = pl.pallas_call(body, name=name, grid=grid, in_specs=in_specs, out_specs=out_specs, out_shape=out_shape,
                             scratch_shapes=scratch_shapes, compiler_params=_params(semantics))(*args)
        return res, None
    n, n_in, n_out, n_scr = sender.n, len(in_specs), len(out_shape), len(scratch_shapes)
    n_sem = n * (N_DEV - 1)
    n_steps = 1
    for g in grid:
        n_steps *= g
    compute = body

    def body(*refs):
        ins, s_in = refs[:n_in], refs[n_in:n_in + 2 * n]
        o0 = n_in + 2 * n
        outs, s_out = refs[o0:o0 + n_out], refs[o0 + n_out:o0 + n_out + 2 + 2 * n]
        scr = refs[o0 + n_out + 2 + 2 * n:]
        compute(*ins, *outs, *scr[:n_scr])
        step = pl.program_id(0)
        for d in range(1, len(grid)):
            step = step * grid[d] + pl.program_id(d)
        sender.issue(s_in[:n], s_in[n:], s_out[0], s_out[1], scr[n_scr], step, n_steps)

    res = pl.pallas_call(
        body, name=name, grid=grid,
        in_specs=in_specs + [_HBM] * (2 * n), out_specs=out_specs + [_SEM, _SEM] + [_HBM] * (2 * n),
        out_shape=out_shape + [pltpu.SemaphoreType.DMA((n_sem,)), pltpu.SemaphoreType.DMA((n_sem,))]
        + [pltpu.HBM(a.shape, a.dtype) for a in sender.srcs] + [pltpu.HBM(a.shape, a.dtype) for a in sender.lands],
        input_output_aliases={n_in + j: n_out + 2 + j for j in range(2 * n)},
        scratch_shapes=scratch_shapes + [pltpu.SemaphoreType.DMA((n,))],
        compiler_params=pltpu.CompilerParams(dimension_semantics=semantics, vmem_limit_bytes=VMEM_LIMIT,
                                             has_side_effects=_EFFECT),
    )(*args, *sender.srcs, *sender.lands)
    handle = (sender.items, name, res[n_out], res[n_out + 1], res[n_out + 2:n_out + 2 + n],
              res[n_out + 2 + n:n_out + 2 + 2 * n])
    return res[:n_out], handle


def _exchange_wait(handle, after):
    items, name, send_sems, recv_sems, srcs, lands = handle
    n = len(items)

    def body(*refs):
        src_refs, land_refs = refs[:n], refs[n:2 * n]
        send_ref, recv_ref = refs[2 * n], refs[2 * n + 1]
        _, copies = _split_copies(items, src_refs, land_refs, send_ref, recv_ref)
        for cp in copies:
            cp.wait_send()
            cp.wait_recv()

    outs = pl.pallas_call(
        body, name=name + "_wait",
        out_shape=[pltpu.HBM(a.shape, a.dtype) for a in srcs] + [pltpu.HBM(a.shape, a.dtype) for a in lands],
        in_specs=[_HBM] * (2 * n) + [_SEM, _SEM, pl.BlockSpec(memory_space=pl.ANY)], out_specs=[_HBM] * (2 * n),
        input_output_aliases={i: i for i in range(2 * n)},
        compiler_params=pltpu.CompilerParams(has_side_effects=_EFFECT),
    )(*srcs, *lands, send_sems, recv_sems, after)
    return outs[n:]


def _mod_fwd(cvec, w_mod_l, b_mod_l):
    rows, cols = cvec.shape[0], w_mod_l.shape[1]

    def body(c_ref, w_ref, b_ref, o_ref, s_ref):
        cv = c_ref[...]
        s = cv * _sigmoid(cv)
        s_ref[...] = s
        o_ref[...] = _dot(s, w_ref[...]) + b_ref[...]

    return pl.pallas_call(
        body, name="mod_fwd",
        out_shape=(jax.ShapeDtypeStruct((rows, cols), F32), jax.ShapeDtypeStruct((rows, D), F32)),
        in_specs=[_full((rows, D)), _full((D, cols)), _full((1, cols))],
        out_specs=(_full((rows, cols)), _full((rows, D))), grid=(1,),
        compiler_params=_params(("arbitrary",)),
    )(cvec, w_mod_l, b_mod_l)


def _mod_bwd(svec, cvec, dmod_l, w_mod_l):
    rows, cols = dmod_l.shape

    def body(s_ref, c_ref, d_ref, w_ref, gw_ref, gc_ref):
        gw_ref[...] = _dot(s_ref[...], d_ref[...], "tn")
        cv = c_ref[...]
        sg = _sigmoid(cv)
        gc_ref[...] = _dot(d_ref[...], w_ref[...], "nt") * (sg * (1.0 + cv * (1.0 - sg)))

    return pl.pallas_call(
        body, name="mod_bwd",
        out_shape=(jax.ShapeDtypeStruct((D, cols), F32), jax.ShapeDtypeStruct((rows, D), F32)),
        in_specs=[_full((rows, D)), _full((rows, D)), _full((rows, cols)), _full((D, cols))],
        out_specs=(_full((D, cols)), _full((rows, D))), grid=(1,),
        compiler_params=_params(("arbitrary",)),
    )(svec, cvec, dmod_l, w_mod_l)


def _mod_bwd_ctx(c_row, d_l, w_mod_l):
    cols = d_l.shape[1]

    def body(c_ref, d_ref, w_ref, o_ref):
        cv = c_ref[...]
        sg = _sigmoid(cv)
        o_ref[...] = _dot(d_ref[...], w_ref[...], "nt") * (sg * (1.0 + cv * (1.0 - sg)))

    return pl.pallas_call(
        body, name="mod_bwd_ctx", out_shape=jax.ShapeDtypeStruct((8, D), F32),
        in_specs=[_full((1, D)), _full((8, cols)), _full((D, cols))], out_specs=_full((8, D)), grid=(1,),
        compiler_params=_params(("arbitrary",)),
    )(c_row, d_l, w_mod_l)


def _inproj(xt, modv, g, w_inT, n_cols, rows_per_example, name, after=None, sender=None):
    rows = xt.shape[0]
    tm = min(PROJ_TILE, rows_per_example)
    per_b = rows_per_example // tm
    shared_mod = modv.shape[0] == 1

    def body(x_ref, mod_ref, g_ref, w_ref, p_ref, h_ref):
        x = x_ref[...]
        r = lax.rsqrt(jnp.mean(x * x, axis=-1, keepdims=True) + EPS)
        h = (x * r * g_ref[...]) * (1.0 + mod_ref[0, 1:2, :]) + mod_ref[0, 0:1, :]
        hb = h.astype(MXU_DTYPE)
        h_ref[...] = hb
        for j in range(n_cols // KW):
            p_ref[:, j * KW:(j + 1) * KW] = _dot(hb, w_ref[j * KW:(j + 1) * KW, :], "nt").astype(p_ref.dtype)

    mod_idx = (lambda i: (0, 0, 0)) if shared_mod else (lambda i: (i // per_b, 0, 0))
    in_specs = [pl.BlockSpec((tm, D), lambda i: (i, 0)), pl.BlockSpec((1, N_MOD, D), mod_idx), _full((1, D)),
                pl.BlockSpec((n_cols, D), lambda i: (0, 0), pipeline_mode=pl.Buffered(1))]
    (p, h), handle = _host_call(
        body, name, (rows // tm,), in_specs, [xt, modv, g, w_inT],
        [jax.ShapeDtypeStruct((rows, n_cols), MXU_DTYPE), jax.ShapeDtypeStruct((rows, D), MXU_DTYPE)],
        [pl.BlockSpec((tm, n_cols), lambda i: (i, 0)), pl.BlockSpec((tm, D), lambda i: (i, 0))], [],
        after=after, sender=sender)
    return p, h, handle


def _tri(reverse, n):
    row = lax.broadcasted_iota(jnp.int32, (n, n), 0)
    col = lax.broadcasted_iota(jnp.int32, (n, n), 1)
    same = (row // CHUNK) == (col // CHUNK)
    return same & ((col >= row) if reverse else (col <= row))


def _per_chunk_rows(x, reverse):
    n = x.shape[0]
    rows = [x[j * CHUNK:j * CHUNK + 1] if reverse else x[(j + 1) * CHUNK - 1:(j + 1) * CHUNK] for j in range(n // CHUNK)]
    return jnp.concatenate([jnp.broadcast_to(r, (CHUNK, x.shape[1])) for r in rows], axis=0), rows


def _lower_bound(gam_ref, direction):
    return _sigmoid(gam_ref[direction:direction + 1, :] - gam_ref[2 + direction:3 + direction, :])


def _gate_prep(z, lb, tri, reverse, b=None):
    sg = _sigmoid(z)
    f = lb + (1.0 - lb) * sg
    g = jnp.log(f)
    b = _mask_dot(tri, g) if b is None else b
    bl, bl_rows = _per_chunk_rows(b, reverse)
    mid = 0.5 * bl
    return sg, g, 1.0 - f, b, jnp.exp(mid), [jnp.exp(0.5 * r) for r in bl_rows], jnp.exp(mid - b), mid


def _hgrn_fwd(p, gam, s0, rows_per_example, with_out, name, sender=None):
    rows = p.shape[0]
    nb_ex = rows // rows_per_example
    rb = min(TOKEN_TILE, rows_per_example)
    cpb = rb // CHUNK
    nb = rows_per_example // rb
    n_chunks = rows // CHUNK
    has_s0 = s0 is not None

    def body(*refs):
        it = iter(refs)
        gam_ref = next(it)
        zf_ref, vf_ref = next(it), next(it)
        qf_ref = next(it) if with_out else None
        zb_ref, vb_ref = next(it), next(it)
        qb_ref = next(it) if with_out else None
        s0_ref = next(it) if has_s0 else None
        if with_out:
            of_ref, ob_ref = next(it), next(it)
        stash_f, stash_b, bsum_f, bsum_b, fin_ref = next(it), next(it), next(it), next(it), next(it)
        st_ref = next(it)
        i = pl.program_id(1)

        @pl.when(i == 0)
        def _():
            if has_s0:
                st_ref[...] = s0_ref[:, 0]
            else:
                st_ref[...] = jnp.zeros_like(st_ref)

        for direction, (z_ref, v_ref, q_ref, stash, bsum_ref) in enumerate(
                ((zf_ref, vf_ref, qf_ref, stash_f, bsum_f), (zb_ref, vb_ref, qb_ref, stash_b, bsum_b))):
            reverse = direction == 1
            tri = _tri(reverse, rb)
            lb = _lower_bound(gam_ref, direction)
            z = z_ref[...].astype(F32)
            v = v_ref[...].astype(F32)
            _, _, k, b, em, em_rows, e2, mid = _gate_prep(z, lb, tri, reverse)
            bsum_ref[...] = b
            kd = (k * (e2 * em)).astype(MXU_DTYPE)
            vb = v.astype(MXU_DTYPE)
            if with_out:
                q = q_ref[...].astype(F32)
                qi = q * jnp.exp(b - mid)
                qe = (qi * em).astype(MXU_DTYPE)
                qi = qi.astype(MXU_DTYPE)
                ki = (k * e2).astype(MXU_DTYPE)
                intra = []
                for h in range(HEADS):
                    hs = slice(h * DK, (h + 1) * DK)
                    sc = jnp.where(tri, _dot(qi[:, hs], ki[:, hs], "nt"), 0.0)
                    intra.append(_dot(sc, vb[:, hs]))
            for j in (range(cpb - 1, -1, -1) if reverse else range(cpb)):
                rs = slice(j * CHUNK, (j + 1) * CHUNK)
                a = em_rows[j] * em_rows[j]
                for h in range(HEADS):
                    hs = slice(h * DK, (h + 1) * DK)
                    st = st_ref[direction, h]
                    stash[j, h] = st.astype(stash.dtype)
                    if with_out:
                        (ob_ref if reverse else of_ref)[rs, hs] = intra[h][rs] + _dot(qe[rs, hs], st, "nt")
                    st_ref[direction, h] = st * a[:, hs] + _dot(vb[rs, hs], kd[rs, hs], "tn")

        @pl.when(i == nb - 1)
        def _():
            fin_ref[:, 0] = st_ref[...]

    up = lambda b, i: b * nb + i
    down = lambda b, i: b * nb + nb - 1 - i
    col = lambda rowf, c: pl.BlockSpec((rb, KW), lambda b, i: (rowf(b, i), c))
    in_specs = [_full((4, KW)), col(up, 0), col(up, 2)] + ([col(up, 3)] if with_out else [])
    in_specs += [col(down, 1), col(down, 2)] + ([col(down, 3)] if with_out else [])
    args = [gam, p, p] + ([p] if with_out else []) + [p, p] + ([p] if with_out else [])
    if has_s0:
        in_specs.append(pl.BlockSpec((2, 1, HEADS, DK, DK), lambda b, i: (0, b, 0, 0, 0)))
        args.append(s0)
    out_shape, out_specs = [], []
    if with_out:
        out_shape += [jax.ShapeDtypeStruct((rows, KW), F32)] * 2
        out_specs += [pl.BlockSpec((rb, KW), lambda b, i: (up(b, i), 0)),
                      pl.BlockSpec((rb, KW), lambda b, i: (down(b, i), 0))]
    out_shape += [jax.ShapeDtypeStruct((n_chunks, HEADS, DK, DK), MXU_DTYPE)] * 2
    out_specs += [pl.BlockSpec((cpb, HEADS, DK, DK), lambda b, i: (up(b, i), 0, 0, 0)),
                  pl.BlockSpec((cpb, HEADS, DK, DK), lambda b, i: (down(b, i), 0, 0, 0))]
    out_shape += [jax.ShapeDtypeStruct((rows, KW), F32)] * 2
    out_specs += [pl.BlockSpec((rb, KW), lambda b, i: (up(b, i), 0)),
                  pl.BlockSpec((rb, KW), lambda b, i: (down(b, i), 0))]
    out_shape.append(jax.ShapeDtypeStruct((2, nb_ex, HEADS, DK, DK), F32))
    out_specs.append(pl.BlockSpec((2, 1, HEADS, DK, DK), lambda b, i: (0, b, 0, 0, 0)))
    res, handle = _host_call(body, name, (nb_ex, nb), in_specs, args, out_shape, out_specs,
                             [pltpu.VMEM((2, HEADS, DK, DK), F32)], sender=sender)
    return (*res, handle)


def _hgrn_bwd(p, gam, do, stash_f, stash_b, bsum_f, bsum_b, ds_end, rows_per_example, with_out, name, after=None,
              sender=None):
    rows = p.shape[0]
    nb_ex = rows // rows_per_example
    rb = min(TOKEN_TILE, rows_per_example)
    cpb = rb // CHUNK
    nb = rows_per_example // rb
    has_end = ds_end is not None

    def body(*refs):
        it = iter(refs)
        gam_ref = next(it)
        ins = []
        for _ in range(2):
            z_ref, v_ref = next(it), next(it)
            q_ref = next(it) if with_out else None
            do_ref = next(it) if with_out else None
            ins.append((z_ref, v_ref, q_ref, do_ref, next(it), next(it)))
        end_ref = next(it) if has_end else None
        outs = [next(it), next(it)]
        dlb_ref, ds0_ref = next(it), next(it)
        dst_ref = next(it)
        b_id, i = pl.program_id(0), pl.program_id(1)

        @pl.when(i == 0)
        def _():
            if has_end:
                dst_ref[...] = end_ref[:, 0]
            else:
                dst_ref[...] = jnp.zeros_like(dst_ref)

        @pl.when((i == 0) & (b_id == 0))
        def _():
            dlb_ref[...] = jnp.zeros_like(dlb_ref)

        for direction in range(2):
            z_ref, v_ref, q_ref, do_ref, stash, b_ref = ins[direction]
            dgrp_ref = outs[direction]
            reverse = direction == 1
            tri = _tri(reverse, rb)
            tri_t = _tri(not reverse, rb)
            lb = _lower_bound(gam_ref, direction)
            heads = [slice(h * DK, (h + 1) * DK) for h in range(HEADS)]
            chunks = [slice(j * CHUNK, (j + 1) * CHUNK) for j in range(cpb)]
            grid_cat = lambda parts: jnp.concatenate([jnp.concatenate(row, axis=1) for row in parts], axis=0)
            cat = lambda parts: jnp.concatenate(parts, axis=1)
            z = z_ref[...].astype(F32)
            sg, g, k, b, em, em_rows, e2, mid = _gate_prep(z, lb, tri, reverse, b=b_ref[...])
            e3 = e2 * em
            kd = k * e3
            kd_b = kd.astype(MXU_DTYPE)
            vb = v_ref[...].astype(MXU_DTYPE)
            if with_out:
                q = q_ref[...].astype(F32)
                dout = do_ref[...].astype(MXU_DTYPE)
                e1 = jnp.exp(b - mid)
                e4 = e1 * em
                qi, ki, qe = q * e1, k * e2, q * e4
                qi_b, ki_b, qe_b = qi.astype(MXU_DTYPE), ki.astype(MXU_DTYPE), qe.astype(MXU_DTYPE)
                dqi_p, dki_p, dv_p = [], [], []
                for hs in heads:
                    sc = jnp.where(tri, _dot(qi_b[:, hs], ki_b[:, hs], "nt"), 0.0)
                    dsc = jnp.where(tri, _dot(dout[:, hs], vb[:, hs], "nt"), 0.0)
                    dqi_p.append(_dot(dsc, ki_b[:, hs]))
                    dki_p.append(_dot(dsc, qi_b[:, hs], "tn"))
                    dv_p.append(_dot(sc, dout[:, hs], "tn"))
                dqi, dki, dv = cat(dqi_p), cat(dki_p), cat(dv_p)
                dqe = grid_cat([[_dot(dout[rs, hs], stash[j, h]) for h, hs in enumerate(heads)]
                                for j, rs in enumerate(chunks)])
                grow = [[_dot(dout[rs, hs], qe_b[rs, hs], "tn") for hs in heads] for rs in chunks]
            dkd_p = [[None] * HEADS for _ in range(cpb)]
            dvs_p = [[None] * HEADS for _ in range(cpb)]
            da_p = [[None] * HEADS for _ in range(cpb)]
            for j in (range(cpb) if reverse else range(cpb - 1, -1, -1)):
                rs = chunks[j]
                a = em_rows[j] * em_rows[j]
                for h, hs in enumerate(heads):
                    dst = dst_ref[direction, h]
                    dkd_p[j][h] = _dot(vb[rs, hs], dst)
                    dvs_p[j][h] = _dot(kd_b[rs, hs], dst, "nt")
                    da_p[j][h] = jnp.broadcast_to(
                        jnp.sum(dst * stash[j, h].astype(F32), axis=0, keepdims=True), (CHUNK, DK))
                    new_dst = dst * a[:, hs]
                    dst_ref[direction, h] = new_dst + grow[j][h] if with_out else new_dst
            dkd, dvs, da = grid_cat(dkd_p), grid_cat(dvs_p), grid_cat(da_p)
            t_kd = dkd * kd
            dk = dkd * e3
            db = -t_kd
            tot = t_kd
            if with_out:
                dgrp_ref[:, KW:2 * KW] = (dvs + dv).astype(dgrp_ref.dtype)
                dgrp_ref[:, 2 * KW:] = (dqi * e1 + dqe * e4).astype(dgrp_ref.dtype)
                dk = dk + dki * e2
                t_qi, t_ki, t_qe = dqi * qi, dki * ki, dqe * qe
                db = db + t_qi - t_ki + t_qe
                tot = tot + 0.5 * (t_ki - t_qi)
            else:
                dgrp_ref[:, KW:2 * KW] = dvs.astype(dgrp_ref.dtype)
            dbl = jnp.concatenate([jnp.broadcast_to(jnp.sum(tot[rs], axis=0, keepdims=True), (CHUNK, KW))
                                   for rs in chunks], axis=0) + da * (em * em)
            dg = _mask_dot(tri_t, db) + dbl
            df = dg * jnp.exp(-g) - dk
            dgrp_ref[:, 0:KW] = (df * (1.0 - lb) * sg * (1.0 - sg)).astype(dgrp_ref.dtype)
            dlb_ref[direction:direction + 1, :] += jnp.sum(df * (1.0 - sg), axis=0, keepdims=True)

        @pl.when(i == nb - 1)
        def _():
            ds0_ref[:, 0] = dst_ref[...]

    rows_of = (lambda b, i: b * nb + nb - 1 - i, lambda b, i: b * nb + i)
    in_specs, args = [_full((4, KW))], [gam]
    for direction in range(2):
        rf = rows_of[direction]
        col = lambda c, rf=rf: pl.BlockSpec((rb, KW), lambda b, i: (rf(b, i), c))
        in_specs += [col(direction), col(2)]
        args += [p, p]
        if with_out:
            in_specs += [col(3), col(0)]
            args += [p, do]
        in_specs += [pl.BlockSpec((cpb, HEADS, DK, DK), lambda b, i, rf=rf: (rf(b, i), 0, 0, 0)), col(0)]
        args += [(stash_f, stash_b)[direction], (bsum_f, bsum_b)[direction]]
    if has_end:
        in_specs.append(pl.BlockSpec((2, 1, HEADS, DK, DK), lambda b, i: (0, b, 0, 0, 0)))
        args.append(ds_end)
    out_shape, out_specs = [], []
    for direction in range(2):
        rf = rows_of[direction]
        width = (3 if with_out else 2) * KW
        out_shape.append(jax.ShapeDtypeStruct((rows, width), MXU_DTYPE))
        out_specs.append(pl.BlockSpec((rb, width), lambda b, i, rf=rf: (rf(b, i), 0)))
    out_shape += [jax.ShapeDtypeStruct((2, KW), F32), jax.ShapeDtypeStruct((2, nb_ex, HEADS, DK, DK), F32)]
    out_specs += [_full((2, KW)), pl.BlockSpec((2, 1, HEADS, DK, DK), lambda b, i: (0, b, 0, 0, 0))]
    res, handle = _host_call(body, name, (nb_ex, nb), in_specs, args, out_shape, out_specs,
                             [pltpu.VMEM((2, HEADS, DK, DK), F32)], after=after, sender=sender)
    return (*res, handle)


def _tail_forward(osum, og, u, v, ga, gb, gna, ln_g, ln_b, ws_ref, bs_ref, wpaT_ref, wpbT_ref, proj=None):
    tm = osum.shape[0]
    gna4 = jnp.concatenate([gna] * HEADS, axis=1)
    r_parts = []
    for h in range(HEADS):
        oh = osum[:, h * DK:(h + 1) * DK]
        r_parts.append(jnp.broadcast_to(lax.rsqrt(jnp.mean(oh * oh, axis=-1, keepdims=True) + EPS), (tm, DK)))
    r = jnp.concatenate(r_parts, axis=1)
    on = osum * r
    sg_og = _sigmoid(og)
    silu_og = og * sg_og
    oan = on * gna4
    oa = oan * silu_og
    ug, tu = _gelu(u)
    vg, tv = _gelu(v)
    mu = jnp.mean(vg, axis=-1, keepdims=True)
    vc = vg - mu
    rstd = lax.rsqrt(jnp.mean(vc * vc, axis=-1, keepdims=True) + EPS)
    vhat = vc * rstd
    vln = vhat * ln_g + ln_b
    blocks = []
    for n in range(tm // SGU_BLOCK):
        rs = slice(n * SGU_BLOCK, (n + 1) * SGU_BLOCK)
        blocks.append(jnp.concatenate(
            [_dot(ws_ref[g], vln[rs, g * DK:(g + 1) * DK]) + bs_ref[g] for g in range(GROUPS)], axis=1))
    mixed = jnp.concatenate(blocks, axis=0) if len(blocks) > 1 else blocks[0]
    obm = ug * mixed
    if proj is None:
        pa = _dot(oa, wpaT_ref[...], "nt")
        pb = _dot(obm, wpbT_ref[...], "nt")
    else:
        pa, pb = proj
    sga, sgb = _sigmoid(ga), _sigmoid(gb)
    merged = sga * pa + sgb * pb
    return dict(r=r, on=on, sg_og=sg_og, silu_og=silu_og, oan=oan, oa=oa, ug=ug, tu=tu, tv=tv, rstd=rstd, vhat=vhat,
                vln=vln, mixed=mixed, obm=obm, pa=pa, pb=pb, sga=sga, sgb=sgb, merged=merged, gna4=gna4)


def _tail_in_specs(tm):
    tile = lambda c: pl.BlockSpec((tm, KW), lambda i: (i, c))
    return [tile(c) for c in range(4, 11)]


def _tail_weight_specs():
    return [_full((1, DK)), _full((1, KW)), _full((1, KW)), _full((GROUPS, SGU_BLOCK, SGU_BLOCK)),
            _full((GROUPS, SGU_BLOCK, 1)), _full((D, KW), single=True), _full((D, KW), single=True),
            _full((D, D), single=True)]


def _read_tail_inputs(of_ref, ob_ref, pcols):
    osum = of_ref[...] + ob_ref[...]
    og, u, v = (pcols[j][...].astype(F32) for j in range(3))
    ga = jnp.concatenate([pcols[3][...], pcols[4][...]], axis=1).astype(F32)
    gb = jnp.concatenate([pcols[5][...], pcols[6][...]], axis=1).astype(F32)
    return osum, og, u, v, ga, gb


def _tail_fwd(p, o_up, o_down, xt, modv, gna, ln_g, ln_b, w_s, b_s, w_paT, w_pbT, w_o, rows_per_example):
    rows = xt.shape[0]
    tm = min(TAIL_TILE, rows_per_example)
    per_b = rows_per_example // tm

    def body(of_ref, ob_ref, *rest):
        pcols = rest[:7]
        (x_ref, mod_ref, gna_ref, lng_ref, lnb_ref, ws_ref, bs_ref, wpaT_ref, wpbT_ref, wo_ref,
         x1_ref, mix_ref, merged_ref, oa_ref, obm_ref, pa_ref, pb_ref) = rest[7:]
        t = _tail_forward(*_read_tail_inputs(of_ref, ob_ref, pcols), gna_ref[...], lng_ref[...], lnb_ref[...],
                          ws_ref, bs_ref, wpaT_ref, wpbT_ref)
        mix = _dot(t["merged"], wo_ref[...])
        x1_ref[...] = x_ref[...] + mod_ref[0, 2:3, :] * mix
        mix_ref[...] = mix.astype(mix_ref.dtype)
        merged_ref[...] = t["merged"].astype(merged_ref.dtype)
        oa_ref[...] = t["oa"].astype(oa_ref.dtype)
        obm_ref[...] = t["obm"].astype(obm_ref.dtype)
        pa_ref[...] = t["pa"].astype(pa_ref.dtype)
        pb_ref[...] = t["pb"].astype(pb_ref.dtype)

    row = lambda w: pl.BlockSpec((tm, w), lambda i: (i, 0))
    in_specs = [row(KW), row(KW)] + _tail_in_specs(tm) + [row(D), pl.BlockSpec((1, N_MOD, D), lambda i: (i // per_b, 0, 0))]
    in_specs += _tail_weight_specs()
    return pl.pallas_call(
        body, name="tail_fwd", grid=(rows // tm,),
        out_shape=(jax.ShapeDtypeStruct((rows, D), F32), jax.ShapeDtypeStruct((rows, D), MXU_DTYPE),
                   jax.ShapeDtypeStruct((rows, D), MXU_DTYPE), jax.ShapeDtypeStruct((rows, KW), MXU_DTYPE),
                   jax.ShapeDtypeStruct((rows, KW), MXU_DTYPE), jax.ShapeDtypeStruct((rows, D), MXU_DTYPE),
                   jax.ShapeDtypeStruct((rows, D), MXU_DTYPE)),
        in_specs=in_specs, out_specs=(row(D), row(D), row(D), row(KW), row(KW), row(D), row(D)),
        compiler_params=_params(("arbitrary",)),
    )(o_up, o_down, *([p] * 7), xt, modv, gna, ln_g, ln_b, w_s, b_s, w_paT, w_pbT, w_o)


def _tail_bwd(p, o_up, o_down, dx1, mix, pa, pb, modv, gna, ln_g, ln_b, w_s, b_s, w_paT, w_pbT, w_o, rows_per_example,
              after=None, sender=None):
    rows = dx1.shape[0]
    nb_ex = rows // rows_per_example
    tm = min(TAIL_TILE, rows_per_example)
    per_b = rows_per_example // tm

    def body(of_ref, ob_ref, *rest):
        pcols = rest[:7]
        (dx1_ref, mix_ref, pa_ref, pb_ref, mod_ref, gna_ref, lng_ref, lnb_ref, ws_ref, bs_ref, wpaT_ref, wpbT_ref, wo_ref,
         dpt_ref, do_ref, dmix_ref, dpa_ref, dpb_ref, dmod_ref, small_ref, dws_ref, dbs_ref) = rest[7:]
        i = pl.program_id(0)

        @pl.when(i == 0)
        def _():
            small_ref[...] = jnp.zeros_like(small_ref)
            dws_ref[...] = jnp.zeros_like(dws_ref)
            dbs_ref[...] = jnp.zeros_like(dbs_ref)

        @pl.when(i % per_b == 0)
        def _():
            dmod_ref[...] = jnp.zeros_like(dmod_ref)

        osum, og, u, v, ga, gb = _read_tail_inputs(of_ref, ob_ref, pcols)
        ln_g = lng_ref[...]
        t = _tail_forward(osum, og, u, v, ga, gb, gna_ref[...], ln_g, lnb_ref[...], ws_ref, bs_ref, wpaT_ref, wpbT_ref,
                          proj=(pa_ref[...].astype(F32), pb_ref[...].astype(F32)))
        dx1v = dx1_ref[...]
        dmod_ref[0, 2:3, :] += jnp.sum(dx1v * mix_ref[...].astype(F32), axis=0, keepdims=True)
        dmix = dx1v * mod_ref[0, 2:3, :]
        dmix_ref[...] = dmix.astype(dmix_ref.dtype)
        dmerged = _dot(dmix, wo_ref[...], "nt")
        sga, sgb = t["sga"], t["sgb"]
        dpa = dmerged * sga
        dpb = dmerged * sgb
        dpa_ref[...] = dpa.astype(dpa_ref.dtype)
        dpb_ref[...] = dpb.astype(dpb_ref.dtype)
        dga = dmerged * t["pa"] * sga * (1.0 - sga)
        dgb = dmerged * t["pb"] * sgb * (1.0 - sgb)
        doa = _dot(dpa, wpaT_ref[...])
        dobm = _dot(dpb, wpbT_ref[...])
        dug = dobm * t["mixed"]
        dmixed = dobm * t["ug"]
        du = dug * _gelu_grad(u, t["tu"])
        dvln_blocks = []
        for n in range(tm // SGU_BLOCK):
            rs = slice(n * SGU_BLOCK, (n + 1) * SGU_BLOCK)
            parts = []
            for g in range(GROUPS):
                gs = slice(g * DK, (g + 1) * DK)
                dm = dmixed[rs, gs]
                parts.append(_dot(ws_ref[g], dm, "tn"))
                dws_ref[g] += _dot(dm, t["vln"][rs, gs], "nt")
                dbs_ref[g] += jnp.sum(dm, axis=1, keepdims=True)
            dvln_blocks.append(jnp.concatenate(parts, axis=1))
        dvln = jnp.concatenate(dvln_blocks, axis=0) if len(dvln_blocks) > 1 else dvln_blocks[0]
        vhat = t["vhat"]
        small_ref[1:2, 0:KW] += jnp.sum(dvln * vhat, axis=0, keepdims=True)
        small_ref[2:3, 0:KW] += jnp.sum(dvln, axis=0, keepdims=True)
        dvhat = dvln * ln_g
        dvg = t["rstd"] * (dvhat - jnp.mean(dvhat, axis=-1, keepdims=True)
                           - vhat * jnp.mean(dvhat * vhat, axis=-1, keepdims=True))
        dv = dvg * _gelu_grad(v, t["tv"])
        sg_og = t["sg_og"]
        doan = doa * t["silu_og"]
        dog = doa * t["oan"] * (sg_og * (1.0 + og * (1.0 - sg_og)))
        prod = doan * t["on"]
        dgna = jnp.zeros((1, DK), F32)
        for h in range(HEADS):
            dgna = dgna + jnp.sum(prod[:, h * DK:(h + 1) * DK], axis=0, keepdims=True)
        small_ref[0:1, 0:DK] += dgna
        don = doan * t["gna4"]
        dot_parts = []
        for h in range(HEADS):
            hs = slice(h * DK, (h + 1) * DK)
            m = jnp.mean(don[:, hs] * t["on"][:, hs], axis=-1, keepdims=True)
            dot_parts.append(t["r"][:, hs] * (don[:, hs] - t["on"][:, hs] * m))
        do_ref[...] = jnp.concatenate(dot_parts, axis=1).astype(do_ref.dtype)
        for j, val in enumerate((dog, du, dv)):
            dpt_ref[:, j * KW:(j + 1) * KW] = val.astype(dpt_ref.dtype)
        dpt_ref[:, 3 * KW:3 * KW + D] = dga.astype(dpt_ref.dtype)
        dpt_ref[:, 3 * KW + D:] = dgb.astype(dpt_ref.dtype)

    row = lambda w: pl.BlockSpec((tm, w), lambda i: (i, 0))
    in_specs = [row(KW), row(KW)] + _tail_in_specs(tm) + [row(D)] * 4 + [pl.BlockSpec((1, N_MOD, D), lambda i: (i // per_b, 0, 0))]
    in_specs += _tail_weight_specs()
    args = [o_up, o_down, *([p] * 7), dx1, mix, pa, pb, modv, gna, ln_g, ln_b, w_s, b_s, w_paT, w_pbT, w_o]
    cd = MXU_DTYPE
    res, handle = _host_call(
        body, "tail_bwd", (rows // tm,), in_specs, args,
        [jax.ShapeDtypeStruct((rows, TAIL_COLS), cd), jax.ShapeDtypeStruct((rows, KW), cd),
         jax.ShapeDtypeStruct((rows, D), cd), jax.ShapeDtypeStruct((rows, D), cd),
         jax.ShapeDtypeStruct((rows, D), cd), jax.ShapeDtypeStruct((nb_ex, 8, D), F32),
         jax.ShapeDtypeStruct((8, D), F32), jax.ShapeDtypeStruct((GROUPS, SGU_BLOCK, SGU_BLOCK), F32),
         jax.ShapeDtypeStruct((GROUPS, SGU_BLOCK, 1), F32)],
        [row(TAIL_COLS), row(KW), row(D), row(D), row(D),
         pl.BlockSpec((1, 8, D), lambda i: (i // per_b, 0, 0)), _full((8, D)),
         _full((GROUPS, SGU_BLOCK, SGU_BLOCK)), _full((GROUPS, SGU_BLOCK, 1))], [],
        after=after, sender=sender)
    return (*res, handle)


def _ffn(x1, target, modv, g_ffn, g_final, w_upT, w_down, rows_per_example):
    rows = x1.shape[0]
    nb_ex = rows // rows_per_example
    tm = min(TOKEN_TILE, rows_per_example)
    per_b = rows_per_example // tm
    n_ff = D_FF // FF_CHUNK

    def body(x1_ref, tgt_ref, mod_ref, gffn_ref, gfin_ref, wup_ref, wdn_ref,
             dx1_ref, h2_ref, dffn_ref, act_ref, dup_ref, dmod_ref, small_ref, up_scr):
        i = pl.program_id(0)

        @pl.when(i == 0)
        def _():
            small_ref[...] = jnp.zeros_like(small_ref)

        @pl.when(i % per_b == 0)
        def _():
            dmod_ref[...] = jnp.zeros_like(dmod_ref)

        x1v = x1_ref[...]
        g2 = gffn_ref[...]
        m3, m4, m5 = mod_ref[0, 3:4, :], mod_ref[0, 4:5, :], mod_ref[0, 5:6, :]
        r2 = lax.rsqrt(jnp.mean(x1v * x1v, axis=-1, keepdims=True) + EPS)
        xn2 = x1v * r2
        h2 = (xn2 * g2) * (1.0 + m4) + m3
        h2b = h2.astype(MXU_DTYPE)
        h2_ref[...] = h2b
        def up_pair(j):
            lo = j * FF_CHUNK
            return (_dot(h2b, wup_ref[lo:lo + FF_CHUNK, :], "nt"),
                    _dot(h2b, wup_ref[D_FF + lo:D_FF + lo + FF_CHUNK, :], "nt"))

        group_end = {min(e, n_ff): s for s, e in ((0, 4), (4, 8), (8, 12))}
        cur, ffn = up_pair(0), None
        for j in range(n_ff):
            nxt = up_pair(j + 1) if j + 1 < n_ff else None
            cs = slice(j * FF_CHUNK, (j + 1) * FF_CHUNK)
            a, bgate = cur
            up_scr[:, cs] = a
            up_scr[:, D_FF + j * FF_CHUNK:D_FF + (j + 1) * FF_CHUNK] = bgate
            act_ref[:, cs] = (a * _sigmoid(a) * bgate).astype(MXU_DTYPE)
            cur = nxt
            if j + 1 in group_end:
                gs = slice(group_end[j + 1] * FF_CHUNK, (j + 1) * FF_CHUNK)
                part = _dot(act_ref[:, gs], wdn_ref[gs, :])
                ffn = part if ffn is None else ffn + part
        x2 = x1v + m5 * ffn
        r3 = lax.rsqrt(jnp.mean(x2 * x2, axis=-1, keepdims=True) + EPS)
        xn3 = x2 * r3
        gf = gfin_ref[...]
        err = xn3 * gf - tgt_ref[...]
        loss = 0.5 * jnp.sum(jnp.mean(err * err, axis=-1, keepdims=True), axis=0, keepdims=True)
        small_ref[2:3, :] += jnp.broadcast_to(loss, (1, D))
        dy = err * (1.0 / D)
        small_ref[1:2, :] += jnp.sum(dy * xn3, axis=0, keepdims=True)
        dxn3 = dy * gf
        dx2 = r3 * (dxn3 - xn3 * jnp.mean(dxn3 * xn3, axis=-1, keepdims=True))
        dmod_ref[0, 5:6, :] += jnp.sum(dx2 * ffn, axis=0, keepdims=True)
        dffn = (dx2 * m5).astype(MXU_DTYPE)
        dffn_ref[...] = dffn
        dact_of = lambda j: _dot(dffn, wdn_ref[j * FF_CHUNK:(j + 1) * FF_CHUNK, :], "nt")
        cur, dh2 = dact_of(0), None
        for j in range(n_ff):
            nxt = dact_of(j + 1) if j + 1 < n_ff else None
            cs = slice(j * FF_CHUNK, (j + 1) * FF_CHUNK)
            a, bgate = up_scr[:, cs], up_scr[:, D_FF + j * FF_CHUNK:D_FF + (j + 1) * FF_CHUNK]
            s = _sigmoid(a)
            dup_ref[:, cs] = (cur * bgate * (s * (1.0 + a * (1.0 - s)))).astype(MXU_DTYPE)
            dup_ref[:, D_FF + j * FF_CHUNK:D_FF + (j + 1) * FF_CHUNK] = (cur * a * s).astype(MXU_DTYPE)
            cur = nxt
            if j + 1 in group_end:
                lo, hi = group_end[j + 1] * FF_CHUNK, (j + 1) * FF_CHUNK
                part = (_dot(dup_ref[:, lo:hi], wup_ref[lo:hi, :])
                        + _dot(dup_ref[:, D_FF + lo:D_FF + hi], wup_ref[D_FF + lo:D_FF + hi, :]))
                dh2 = part if dh2 is None else dh2 + part
        dmod_ref[0, 3:4, :] += jnp.sum(dh2, axis=0, keepdims=True)
        dmod_ref[0, 4:5, :] += jnp.sum(dh2 * xn2 * g2, axis=0, keepdims=True)
        small_ref[0:1, :] += jnp.sum(dh2 * (1.0 + m4) * xn2, axis=0, keepdims=True)
        dxn2 = dh2 * g2 * (1.0 + m4)
        dx1_ref[...] = dx2 + r2 * (dxn2 - xn2 * jnp.mean(dxn2 * xn2, axis=-1, keepdims=True))

    row = lambda w: pl.BlockSpec((tm, w), lambda i: (i, 0))
    cd = MXU_DTYPE
    return pl.pallas_call(
        body, name="ffn_fwd_bwd", grid=(rows // tm,),
        out_shape=(jax.ShapeDtypeStruct((rows, D), F32), jax.ShapeDtypeStruct((rows, D), cd),
                   jax.ShapeDtypeStruct((rows, D), cd), jax.ShapeDtypeStruct((rows, D_FF), cd),
                   jax.ShapeDtypeStruct((rows, 2 * D_FF), cd), jax.ShapeDtypeStruct((nb_ex, 8, D), F32),
                   jax.ShapeDtypeStruct((8, D), F32)),
        in_specs=[row(D), row(D), pl.BlockSpec((1, N_MOD, D), lambda i: (i // per_b, 0, 0)), _full((1, D)), _full((1, D)),
                  _full((2 * D_FF, D), single=True), _full((D_FF, D), single=True)],
        out_specs=(row(D), row(D), row(D), row(D_FF), row(2 * D_FF),
                   pl.BlockSpec((1, 8, D), lambda i: (i // per_b, 0, 0)), _full((8, D))),
        scratch_shapes=[pltpu.VMEM((tm, 2 * D_FF), F32)],
        compiler_params=_params(("arbitrary",)),
    )(x1, target, modv, g_ffn, g_final, w_upT, w_down)


def _scan_columns(up, down, n_groups):
    cols = [up[:, 0:KW].astype(F32), down[:, 0:KW].astype(F32)]
    for j in range(1, n_groups):
        cols.append(up[:, j * KW:(j + 1) * KW].astype(F32) + down[:, j * KW:(j + 1) * KW].astype(F32))
    return cols


def _inproj_bwd(d_up, d_down, dpt, xt, dx1, modv, g, w_inT, rows_per_example, name, sender=None):
    rows = xt.shape[0]
    latent = dx1 is not None
    n_cols = IN_COLS if latent else CTX_COLS
    n_groups = d_up.shape[1] // KW
    tm = min(PROJ_TILE, rows_per_example)
    per_b = rows_per_example // tm
    n_mod_blocks = rows // rows_per_example if latent else 1

    def body(*refs):
        it = iter(refs)
        up_ref, down_ref = next(it), next(it)
        dpt_ref = next(it) if latent else None
        x_ref = next(it)
        dx1_ref = next(it) if latent else None
        mod_ref, g_ref, w_ref = next(it), next(it), next(it)
        gx_ref = next(it) if latent else None
        dp_out = None if latent else next(it)
        dmod_ref, small_ref = next(it), next(it)
        dp_ref = next(it) if latent else dp_out
        i = pl.program_id(0)

        @pl.when(i == 0)
        def _():
            small_ref[...] = jnp.zeros_like(small_ref)

        @pl.when((i % per_b == 0) if latent else (i == 0))
        def _():
            dmod_ref[...] = jnp.zeros_like(dmod_ref)

        for j, val in enumerate(_scan_columns(up_ref[...], down_ref[...], n_groups)):
            dp_ref[:, j * KW:(j + 1) * KW] = val.astype(MXU_DTYPE)
        if latent:
            dh = _dot(dp_ref[...], w_ref[0:4 * KW, :]) + _dot(dpt_ref[...], w_ref[4 * KW:, :])
        else:
            dh = _dot(dp_ref[...], w_ref[...])
        x = x_ref[...]
        gv = g_ref[...]
        m1 = mod_ref[0, 1:2, :]
        r = lax.rsqrt(jnp.mean(x * x, axis=-1, keepdims=True) + EPS)
        xn = x * r
        dmod_ref[0, 0:1, :] += jnp.sum(dh, axis=0, keepdims=True)
        dmod_ref[0, 1:2, :] += jnp.sum(dh * xn * gv, axis=0, keepdims=True)
        small_ref[0:1, :] += jnp.sum(dh * (1.0 + m1) * xn, axis=0, keepdims=True)
        if latent:
            dxn = dh * gv * (1.0 + m1)
            gx_ref[...] = dx1_ref[...] + r * (dxn - xn * jnp.mean(dxn * xn, axis=-1, keepdims=True))

    row = lambda w: pl.BlockSpec((tm, w), lambda i: (i, 0))
    mod_idx = (lambda i: (i // per_b, 0, 0)) if latent else (lambda i: (0, 0, 0))
    in_specs = [row(n_groups * KW)] * 2 + ([row(TAIL_COLS)] if latent else []) + [row(D)] + ([row(D)] if latent else [])
    in_specs += [pl.BlockSpec((1, N_MOD, D), mod_idx), _full((1, D)),
                 pl.BlockSpec((n_cols, D), lambda i: (0, 0), pipeline_mode=pl.Buffered(1))]
    args = [d_up, d_down] + ([dpt] if latent else []) + [xt] + ([dx1] if latent else []) + [modv, g, w_inT]
    first = jax.ShapeDtypeStruct((rows, D), F32) if latent else jax.ShapeDtypeStruct((rows, n_cols), MXU_DTYPE)
    out_shape = [first, jax.ShapeDtypeStruct((n_mod_blocks, 8, D), F32), jax.ShapeDtypeStruct((8, D), F32)]
    out_specs = [row(D) if latent else row(n_cols), pl.BlockSpec((1, 8, D), mod_idx), _full((8, D))]
    scratch = [pltpu.VMEM((tm, 4 * KW), MXU_DTYPE)] if latent else []
    res, handle = _host_call(body, name, (rows // tm,), in_specs, args, out_shape, out_specs, scratch, sender=sender)
    return (*res, handle)


def _grad_matmul(a, b, name, init=None, tn=512, sender=None):
    rows, n = a.shape
    k = b.shape[1]
    tn = min(tn, n)
    has_init = init is not None
    init_blocks = init.shape[0] // tn if has_init else 0

    def body(*refs):
        if has_init:
            a_ref, b_ref, init_ref, o_ref = refs
        else:
            a_ref, b_ref, o_ref = refs
        g = _dot(a_ref[...], b_ref[...], "tn")
        if has_init:
            g = g + jnp.where(pl.program_id(0) < init_blocks, init_ref[...].astype(F32), 0.0)
        o_ref[...] = g.astype(o_ref.dtype)

    in_specs = [pl.BlockSpec((rows, tn), lambda i: (0, i)), _full((rows, k), single=True)]
    args = [a, b]
    if has_init:
        in_specs.append(pl.BlockSpec((tn, k), lambda i: (jnp.minimum(i, init_blocks - 1), 0)))
        args.append(init)
    (out,), handle = _host_call(
        body, name, (n // tn,), in_specs, args, [jax.ShapeDtypeStruct((n, k), PAYLOAD_DTYPE)],
        [pl.BlockSpec((tn, k), lambda i: (i, 0))], [], sender=sender)
    return out, handle


def _grad_in(d_up, d_down, dpt, h, init, sender=None):
    rows = h.shape[0]
    tn = 256
    per_group = KW // tn
    n_scan = 4 * per_group
    init_blocks = init.shape[0] // tn

    def body(up_ref, down_ref, dpt_ref, h_ref, init_ref, o_ref):
        i = pl.program_id(0)
        both = (up_ref[...].astype(F32) + down_ref[...].astype(F32)).astype(MXU_DTYPE)
        a = jnp.where(i < per_group, up_ref[...],
                      jnp.where(i < 2 * per_group, down_ref[...], jnp.where(i < n_scan, both, dpt_ref[...])))
        g = _dot(a, h_ref[...], "tn") + jnp.where(i < init_blocks, init_ref[...].astype(F32), 0.0)
        o_ref[...] = g.astype(o_ref.dtype)

    last = 3 * per_group - 1
    col = lambda f: pl.BlockSpec((rows, tn), lambda i: (0, f(i)))
    in_specs = [col(lambda i: jnp.clip(jnp.where(i < per_group, i, i - per_group), 0, last)),
                col(lambda i: jnp.clip(i - per_group, 0, last)),
                col(lambda i: jnp.clip(i - n_scan, 0, TAIL_COLS // tn - 1)),
                _full((rows, D), single=True),
                pl.BlockSpec((tn, D), lambda i: (jnp.minimum(i, init_blocks - 1), 0))]
    (out,), handle = _host_call(
        body, "gw_in", (IN_COLS // tn,), in_specs, [d_up, d_down, dpt, h, init],
        [jax.ShapeDtypeStruct((IN_COLS, D), PAYLOAD_DTYPE)], [pl.BlockSpec((tn, D), lambda i: (i, 0))], [],
        sender=sender)
    return out, handle


def _row_tile(rows, limit=256):
    if rows <= limit:
        return rows
    for t in range(limit, 7, -8):
        if rows % t == 0:
            return t
    return rows


def _sum8(stack, name):
    _, rows, cols = stack.shape
    tr = _row_tile(rows)

    def body(s_ref, o_ref):
        acc = s_ref[0].astype(F32)
        for j in range(1, N_DEV):
            acc = acc + s_ref[j].astype(F32)
        o_ref[...] = acc

    return pl.pallas_call(
        body, name=name, grid=(rows // tr,), out_shape=jax.ShapeDtypeStruct((rows, cols), F32),
        in_specs=[pl.BlockSpec((N_DEV, tr, cols), lambda i: (0, i, 0))],
        out_specs=pl.BlockSpec((tr, cols), lambda i: (i, 0)),
        compiler_params=_params(("arbitrary",)),
    )(stack)


def _adamw_update(w, gv, m, v):
    nm = ADAM_B1 * m + (1.0 - ADAM_B1) * gv
    nv = ADAM_B2 * v + (1.0 - ADAM_B2) * (gv * gv)
    m_hat = nm / (1.0 - ADAM_B1 ** ADAM_STEP)
    v_hat = nv / (1.0 - ADAM_B2 ** ADAM_STEP)
    return -ADAM_LR * (m_hat / (jnp.sqrt(v_hat) + ADAM_EPS) + ADAM_WD * w), nm, nv


SMALL_PARAMS = (("g_mix", 0, D), ("g_ffn", 1, D), ("g_final", 2, D), ("g_norm_a", 3, DK), ("ln_v_g", 4, KW),
                ("ln_v_b", 5, KW), ("b_s", 6, GROUPS * SGU_BLOCK),
                ("c_ctx", 15, D))


def _small_finish(early, late, dws, gam, nb_ex, params):
    names = [n for n, _, _ in SMALL_PARAMS] + ["b_mod", "w_s"]

    def body(*refs):
        s_ref, l_ref, dws_ref, gam_ref = refs[:4]
        p_refs = refs[4:4 + 3 * len(names)]
        tot_ref, dgam_ref = refs[4 + 3 * len(names):6 + 3 * len(names)]
        o_refs = refs[6 + 3 * len(names):]
        acc = s_ref[0] + l_ref[0]
        gws = dws_ref[0]
        for j in range(1, N_DEV):
            acc = acc + (s_ref[j] + l_ref[j])
            gws = gws + dws_ref[j]
        tot_ref[...] = acc
        bm = acc[8:8 + N_MOD, :]
        for e in range(nb_ex):
            bm = bm + acc[16 + e * N_MOD:16 + (e + 1) * N_MOD, :]
        lb = jnp.concatenate([_lower_bound(gam_ref, 0), _lower_bound(gam_ref, 1)], axis=1)
        dgam = acc[7:8, :] * lb * (1.0 - lb)
        dgam_ref[...] = jnp.concatenate([dgam, -dgam], axis=0)
        grads = [acc[row:row + 1, 0:width] for _, row, width in SMALL_PARAMS] + [bm, gws]
        for k, g in enumerate(grads):
            w_ref, m_ref, v_ref = p_refs[3 * k:3 * k + 3]
            o_refs[4 * k][...] = g
            o_refs[4 * k + 1][...], o_refs[4 * k + 2][...], o_refs[4 * k + 3][...] = _adamw_update(
                w_ref[...], g, m_ref[...], v_ref[...])

    p_args, p_specs, o_shapes, o_specs = [], [], [], []
    for n in names:
        for a in params[n]:
            p_args.append(a)
            p_specs.append(_full(a.shape))
        o_shapes += [jax.ShapeDtypeStruct(params[n][0].shape, F32)] * 4
        o_specs += [_full(params[n][0].shape)] * 4
    res = pl.pallas_call(
        body, name="small_finish", grid=(1,),
        out_shape=[jax.ShapeDtypeStruct((SMALL_ROWS, D), F32), jax.ShapeDtypeStruct((2, D), F32)] + o_shapes,
        in_specs=[_full(early.shape), _full(late.shape), _full(dws.shape), _full((4, KW))] + p_specs,
        out_specs=[_full((SMALL_ROWS, D)), _full((2, D))] + o_specs,
        compiler_params=_params(("arbitrary",)),
    )(early, late, dws, gam, *p_args)
    return res[0], res[1], {n: res[2 + 4 * k:6 + 4 * k] for k, n in enumerate(names)}


def _adamw_sum8(stack, w, m, v, name):
    _, rows, cols = stack.shape
    tr = _row_tile(rows)

    def body(s_ref, w_ref, m_ref, v_ref, g_ref, d_ref, nm_ref, nv_ref):
        gv = s_ref[0].astype(F32)
        for j in range(1, N_DEV):
            gv = gv + s_ref[j].astype(F32)
        g_ref[...] = gv
        d_ref[...], nm_ref[...], nv_ref[...] = _adamw_update(w_ref[...], gv, m_ref[...], v_ref[...])

    blk = pl.BlockSpec((tr, cols), lambda i: (i, 0))
    sd = jax.ShapeDtypeStruct((rows, cols), F32)
    return pl.pallas_call(
        body, name=name, grid=(rows // tr,), out_shape=(sd, sd, sd, sd),
        in_specs=[pl.BlockSpec((N_DEV, tr, cols), lambda i: (0, i, 0)), blk, blk, blk], out_specs=(blk, blk, blk, blk),
        compiler_params=_params(("arbitrary",)),
    )(stack, w, m, v)


def _adamw(w, g, m, v, name):
    shape = w.shape
    cols = shape[-1]
    rows = 1
    for s in shape[:-1]:
        rows *= s
    tr = _row_tile(rows)

    def body(w_ref, g_ref, m_ref, v_ref, d_ref, nm_ref, nv_ref):
        gv = g_ref[...]
        nm = ADAM_B1 * m_ref[...] + (1.0 - ADAM_B1) * gv
        nv = ADAM_B2 * v_ref[...] + (1.0 - ADAM_B2) * (gv * gv)
        m_hat = nm / (1.0 - ADAM_B1 ** ADAM_STEP)
        v_hat = nv / (1.0 - ADAM_B2 ** ADAM_STEP)
        d_ref[...] = -ADAM_LR * (m_hat / (jnp.sqrt(v_hat) + ADAM_EPS) + ADAM_WD * w_ref[...])
        nm_ref[...] = nm
        nv_ref[...] = nv

    blk = pl.BlockSpec((tr, cols), lambda i: (i, 0))
    sd = jax.ShapeDtypeStruct((rows, cols), F32)
    d, nm, nv = pl.pallas_call(
        body, name=name, grid=(rows // tr,), out_shape=(sd, sd, sd), in_specs=[blk] * 4, out_specs=(blk, blk, blk),
        compiler_params=_params(("arbitrary",)),
    )(w.reshape(rows, cols), g.reshape(rows, cols), m.reshape(rows, cols), v.reshape(rows, cols))
    return d.reshape(shape), nm.reshape(shape), nv.reshape(shape)


def _owner_blocks(a):
    return a.reshape(N_DEV, a.shape[0] // N_DEV, a.shape[1])


class _LocalWeights:
    def __init__(self, w_upT, w_down, w_o, w_paT, w_pbT):
        self.weights = (w_upT, w_down, w_o, w_paT, w_pbT)
        self.items = {}

    def sender(self, stage, items=None):
        self.items[stage] = items
        return None

    def sent(self, stage, handle):
        pass

    def mixer_weights(self, after):
        return self.weights[1:]

    def ffn_weights(self, after):
        return self.weights[0]

    def c_ctx_part(self, after):
        return jnp.zeros((1, D), F32)


def _local_step(x, ctx, target, modv, mcv, gam, g_mix, g_ffn, gna, ln_g, ln_b, w_s, b_s, g_final, w_inT, comm):
    nb_ex, seq, _ = x.shape
    ctx_len = ctx.shape[1]
    xt = x.reshape(nb_ex * seq, D)
    ct = ctx.reshape(nb_ex * ctx_len, D)
    tgt = target.reshape(nb_ex * seq, D)
    bs3 = b_s.reshape(GROUPS, SGU_BLOCK, 1)

    pc, hc, _ = _inproj(ct, mcv, g_mix, w_inT, CTX_COLS, ctx_len, "inproj_ctx")
    p, h, handle = _inproj(xt, modv, g_mix, w_inT, IN_COLS, seq, "inproj_lat", sender=comm.sender("inproj"))
    comm.sent("inproj", handle)
    cst_f, cst_b, cb_f, cb_b, s_ctx, _ = _hgrn_fwd(pc, gam, None, ctx_len, False, "hgrn_fwd_ctx")
    o_up, o_down, st_f, st_b, b_f, b_b, _, handle = _hgrn_fwd(p, gam, s_ctx, seq, True, "hgrn_fwd_lat",
                                                              sender=comm.sender("scan"))
    comm.sent("scan", handle)
    w_down, w_o, w_paT, w_pbT = comm.mixer_weights(o_up)
    x1, mix, merged, oa, obm, pa, pb = _tail_fwd(p, o_up, o_down, xt, modv, gna, ln_g, ln_b, w_s, bs3, w_paT, w_pbT,
                                                 w_o, seq)
    w_upT = comm.ffn_weights(x1)
    dx1, h2, dffn, act, dup, dmod_ffn, small_ffn = _ffn(x1, tgt, modv, g_ffn, g_final, w_upT, w_down, seq)
    gw_upT, _ = _grad_matmul(dup, h2, "gw_up")
    gw_down, _ = _grad_matmul(act, dffn, "gw_down", tn=256)
    scatter = lambda *grads: [(_owner_blocks(g), "scatter") for g in grads]
    dpt, do, dmix, dpa, dpb, dmod_tail, small_tail, dws, dbs, handle = _tail_bwd(
        p, o_up, o_down, dx1, mix, pa, pb, modv, gna, ln_g, ln_b, w_s, bs3, w_paT, w_pbT, w_o, seq,
        sender=comm.sender("tail_bwd", scatter(gw_upT)))
    comm.sent("tail_bwd", handle)
    gw_o, _ = _grad_matmul(merged, dmix, "gw_o")
    gw_paT, _ = _grad_matmul(dpa, oa, "gw_pa")
    gw_pbT, _ = _grad_matmul(dpb, obm, "gw_pb")
    def at_row(row, a):
        return jnp.pad(a, ((row, SMALL_ROWS - row - a.shape[0]), (0, D - a.shape[1])))

    small_early = (at_row(1, small_ffn[0:2])
                   + at_row(3, small_tail[0:3])
                   + at_row(6, dbs.reshape(1, GROUPS * SGU_BLOCK))
                   + at_row(14, small_ffn[2:3]))
    dws_rows = dws.reshape(GROUPS * SGU_BLOCK, SGU_BLOCK)
    d_up, d_down, dlb, ds0, handle = _hgrn_bwd(
        p, gam, do, st_f, st_b, b_f, b_b, None, seq, True, "hgrn_bwd_lat",
        sender=comm.sender("scan_bwd", scatter(gw_down, gw_o, gw_paT, gw_pbT)
                           + [(small_early, "gather"), (dws_rows, "gather")]))
    comm.sent("scan_bwd", handle)
    c_up, c_down, dlb_c, _, _ = _hgrn_bwd(pc, gam, None, cst_f, cst_b, cb_f, cb_b, ds0, ctx_len, False, "hgrn_bwd_ctx")
    dpc, dmc, small_c, _ = _inproj_bwd(c_up, c_down, None, ct, None, mcv, g_mix, w_inT, ctx_len, "inproj_bwd_ctx")
    gw_in_c, handle = _grad_matmul(dpc, hc, "gw_in_ctx", sender=comm.sender("ctx_mod", [(dmc[0], "gather")]))
    comm.sent("ctx_mod", handle)
    gw_inT, _ = _grad_in(d_up, d_down, dpt, h, gw_in_c)
    grad_x, dmod_in, small_in, handle = _inproj_bwd(d_up, d_down, dpt, xt, dx1, modv, g_mix, w_inT, seq,
                                                   "inproj_bwd_lat", sender=comm.sender("inproj_bwd", scatter(gw_inT)))
    comm.sent("inproj_bwd", handle)
    dmod = dmod_in + dmod_tail + dmod_ffn
    small_late = (at_row(0, small_in[0:1] + small_c[0:1])
                  + at_row(7, (dlb + dlb_c).reshape(1, 2 * KW))
                  + at_row(8, dmc[0, 0:N_MOD])
                  + at_row(15, comm.c_ctx_part(gw_inT))
                  + at_row(16, dmod[:, 0:N_MOD].reshape(nb_ex * N_MOD, D)))
    comm.sender("last", [(small_late, "gather")])
    return grad_x.reshape(x.shape)


def kernel(x, c, ctx, c_ctx, w_mod, b_mod, g_mix, g_ffn, w_in, lb_gamma, g_norm_a, ln_v_g, ln_v_b, w_s, b_s, w_pa, w_pb, w_o, w_up, w_down, g_final, loss_target, m_c_ctx, m_w_mod, m_b_mod, m_g_mix, m_g_ffn, m_w_in, m_lb_gamma, m_g_norm_a, m_ln_v_g, m_ln_v_b, m_w_s, m_b_s, m_w_pa, m_w_pb, m_w_o, m_w_up, m_w_down, m_g_final, v_c_ctx, v_w_mod, v_b_mod, v_g_mix, v_g_ffn, v_w_in, v_lb_gamma, v_g_norm_a, v_ln_v_g, v_ln_v_b, v_w_s, v_b_s, v_w_pa, v_w_pb, v_w_o, v_w_up, v_w_down, v_g_final):
    nb_ex = x.shape[0]
    me = 4 * lax.axis_index("x") + 2 * lax.axis_index("y") + lax.axis_index("c")
    cd = MXU_DTYPE
    mod_cols = w_mod.shape[2]
    lb_cols = lb_gamma.shape[2]

    w_inT_l = w_in[0].T.astype(cd)
    w_upT_l = w_up[0].T.astype(cd)
    w_paT_l = w_pa[0].T.astype(cd)
    w_pbT_l = w_pb[0].T.astype(cd)
    cl = jnp.concatenate([c, jnp.pad(lb_gamma.reshape(1, 4 * lb_cols), ((0, 0), (0, D - 4 * lb_cols))),
                          jnp.zeros((8 - nb_ex - 1, D), F32)], axis=0)
    g_in, g_cl = _gather_two_level([w_inT_l, cl], "gather_w_in")
    w_inT = g_in.reshape(IN_COLS, D)
    c_all = g_cl[:, 0:nb_ex].reshape(N_DEV * nb_ex, D)
    gam = jnp.transpose(g_cl[:, nb_ex, 0:4 * lb_cols].reshape(N_DEV, 4, lb_cols), (1, 0, 2)).reshape(4, KW)

    n_c = N_DEV * nb_ex
    cvec = jnp.concatenate([c_all, c_ctx.reshape(1, D), jnp.zeros((7, D), F32)], axis=0)
    b_mod_l = lax.dynamic_slice(b_mod, (0, me * mod_cols), (1, mod_cols))
    mod_l, svec = _mod_fwd(cvec, w_mod[0], b_mod_l)
    (g_mod,) = _gather_two_level([mod_l], "gather_mod")
    mod_all = jnp.transpose(g_mod, (1, 0, 2)).reshape(n_c + 8, N_MOD * D)
    modv = lax.dynamic_slice(mod_all, (me * nb_ex, 0), (nb_ex, N_MOD * D)).reshape(nb_ex, N_MOD, D)
    mcv = mod_all[n_c].reshape(1, N_MOD, D)

    handles, leftover = {}, {}

    class Comm:
        def sender(self, stage, items=None):
            if stage == "inproj":
                return _Sender([(w_down[0].astype(cd), "gather"), (w_o[0].astype(cd), "gather"), (w_paT_l, "gather"),
                                (w_pbT_l, "gather")])
            if stage == "scan":
                return _Sender([(w_upT_l, "gather")])
            if stage == "last":
                leftover["items"] = items
                return None
            return _Sender(items)

        def sent(self, stage, handle):
            handles[stage] = handle

        def mixer_weights(self, after):
            g_down, g_o, g_pa, g_pb = _exchange_wait(handles["inproj"], after)
            return g_down.reshape(D_FF, D), g_o.reshape(D, D), g_pa.reshape(D, KW), g_pb.reshape(D, KW)

        def ffn_weights(self, after):
            (g_up,) = _exchange_wait(handles["scan"], after)
            return g_up.reshape(2 * D_FF, D)

        def c_ctx_part(self, after):
            (r_dmc,) = _exchange_wait(handles["ctx_mod"], after)
            dmc_tot = _sum8(r_dmc, "sum_ctx_mod")
            dmc_l = lax.dynamic_slice(dmc_tot[0:N_MOD].reshape(1, N_MOD * D), (0, me * mod_cols), (1, mod_cols))
            return _mod_bwd_ctx(c_ctx.reshape(1, D), jnp.pad(dmc_l, ((0, 7), (0, 0))), w_mod[0])[0:1]

    grad_x = _local_step(
        x, ctx, loss_target, modv, mcv, gam, g_mix, g_ffn, g_norm_a, ln_v_g, ln_v_b, w_s[0], b_s[0],
        g_final.reshape(1, D), w_inT, Comm())
    last, last_started = _exchange_start(leftover["items"], "gather_small_late", after=leftover["items"][0][0])

    (r_up,) = _exchange_wait(handles["tail_bwd"], last_started)
    r_down, r_o, r_pa, r_pb, r_small, r_dws = _exchange_wait(handles["scan_bwd"], r_up)
    raw_up = _adamw_sum8(r_up, w_up[0].T, m_w_up[0].T, v_w_up[0].T, "adamw_w_up")
    raw_down = _adamw_sum8(r_down, w_down[0], m_w_down[0], v_w_down[0], "adamw_w_down")
    raw_o = _adamw_sum8(r_o, w_o[0], m_w_o[0], v_w_o[0], "adamw_w_o")
    (r_in,) = _exchange_wait(handles["inproj_bwd"], raw_up[1])
    raw_in = _adamw_sum8(r_in, w_in[0].T, m_w_in[0].T, v_w_in[0].T, "adamw_w_in")
    (r_late,) = _exchange_wait(last, raw_in[1])
    done = {"w_in": [a.T[None] for a in raw_in], "w_up": [a.T[None] for a in raw_up],
            "w_down": [a[None] for a in raw_down], "w_o": [a[None] for a in raw_o]}
    grad_w_in, grad_w_up, grad_w_down, grad_w_o = (done[k][0] for k in ("w_in", "w_up", "w_down", "w_o"))
    grad_w_pa = _sum8(r_pa, "sum_w_pa").T[None]
    grad_w_pb = _sum8(r_pb, "sum_w_pb").T[None]
    as_2d = {"c_ctx": (1, D), "g_final": (1, D),"b_s": (1, GROUPS * SGU_BLOCK), "b_mod": (N_MOD, D), "w_s": (GROUPS * SGU_BLOCK, SGU_BLOCK)}
    small_params = {"g_mix": (g_mix, m_g_mix, v_g_mix), "g_ffn": (g_ffn, m_g_ffn, v_g_ffn),
                    "g_final": (g_final, m_g_final, v_g_final), "g_norm_a": (g_norm_a, m_g_norm_a, v_g_norm_a),
                    "ln_v_g": (ln_v_g, m_ln_v_g, v_ln_v_g), "ln_v_b": (ln_v_b, m_ln_v_b, v_ln_v_b),
                    "b_s": (b_s, m_b_s, v_b_s), "c_ctx": (c_ctx, m_c_ctx, v_c_ctx), "b_mod": (b_mod, m_b_mod, v_b_mod), "w_s": (w_s, m_w_s, v_w_s)}
    tot, dgam, small_done = _small_finish(
        r_small, r_late, r_dws, gam, nb_ex,
        {n: tuple(a.reshape(as_2d.get(n, a.shape)) for a in wmv) for n, wmv in small_params.items()})
    for n, outs in small_done.items():
        done[n] = [a.reshape(small_params[n][0].shape) for a in outs]
    loss = tot[14, 0]
    grad_g_mix, grad_g_ffn, grad_g_final, grad_g_norm_a, grad_ln_v_g, grad_ln_v_b, grad_b_s, grad_b_mod, grad_w_s = (
        done[n][0] for n in ("g_mix", "g_ffn", "g_final", "g_norm_a", "ln_v_g", "ln_v_b", "b_s", "b_mod", "w_s"))
    grad_lb_gamma = lax.dynamic_slice(dgam.reshape(2, 2, KW), (0, 0, me * lb_cols), (2, 2, lb_cols))

    dmod_all = r_late[:, 16:16 + nb_ex * N_MOD].reshape(n_c, N_MOD * D)
    dmod_l = jnp.concatenate([lax.dynamic_slice(dmod_all, (0, me * mod_cols), (n_c, mod_cols)),
                              lax.dynamic_slice(tot[8:8 + N_MOD].reshape(1, N_MOD * D), (0, me * mod_cols), (1, mod_cols)),
                              jnp.zeros((7, mod_cols), F32)], axis=0)
    gw_mod, _ = _mod_bwd(svec, cvec, dmod_l, w_mod[0])
    grad_w_mod = gw_mod[None]
    grad_c_ctx = done["c_ctx"][0]

    names = ["c_ctx", "w_mod", "b_mod", "g_mix", "g_ffn", "w_in", "lb_gamma", "g_norm_a", "ln_v_g", "ln_v_b", "w_s",
             "b_s", "w_pa", "w_pb", "w_o", "w_up", "w_down", "g_final"]
    weights = [c_ctx, w_mod, b_mod, g_mix, g_ffn, w_in, lb_gamma, g_norm_a, ln_v_g, ln_v_b, w_s, b_s, w_pa, w_pb, w_o,
               w_up, w_down, g_final]
    grads = [grad_c_ctx, grad_w_mod, grad_b_mod, grad_g_mix, grad_g_ffn, grad_w_in, grad_lb_gamma, grad_g_norm_a,
             grad_ln_v_g, grad_ln_v_b, grad_w_s, grad_b_s, grad_w_pa, grad_w_pb, grad_w_o, grad_w_up, grad_w_down,
             grad_g_final]
    ms = [m_c_ctx, m_w_mod, m_b_mod, m_g_mix, m_g_ffn, m_w_in, m_lb_gamma, m_g_norm_a, m_ln_v_g, m_ln_v_b, m_w_s, m_b_s,
          m_w_pa, m_w_pb, m_w_o, m_w_up, m_w_down, m_g_final]
    vs = [v_c_ctx, v_w_mod, v_b_mod, v_g_mix, v_g_ffn, v_w_in, v_lb_gamma, v_g_norm_a, v_ln_v_g, v_ln_v_b, v_w_s, v_b_s,
          v_w_pa, v_w_pb, v_w_o, v_w_up, v_w_down, v_g_final]
    deltas, new_ms, new_vs = [], [], []
    for nm, w, g, m, v in zip(names, weights, grads, ms, vs):
        d, nm_, nv_ = done[nm][1:] if nm in done else _adamw(w, g.reshape(w.shape), m, v, "adamw_" + nm)
        deltas.append(d)
        new_ms.append(nm_)
        new_vs.append(nv_)
    grads = [g.reshape(w.shape) for g, w in zip(grads, weights)]
    return (loss, grad_x, *grads, *deltas, *new_ms, *new_vs)
```

```python
import functools

import jax
import jax.numpy as jnp
from jax import lax
from jax.experimental import pallas as pl
from jax.experimental.pallas import tpu as pltpu

F32 = jnp.float32
MXU_DTYPE = jnp.bfloat16
PAYLOAD_DTYPE = jnp.bfloat16

N_DEV = 8
D = 1024
HEADS = 4
DK = 128
KW = HEADS * DK
CHUNK = 64
SGU_BLOCK = 128
GROUPS = 4
D_FF = 2816
FF_CHUNK = 256
N_MOD = 6
IN_COLS = 5632
CTX_COLS = 1536
TAIL_COLS = IN_COLS - 4 * KW
EPS = 1e-6
ADAM_LR, ADAM_B1, ADAM_B2, ADAM_EPS, ADAM_WD, ADAM_STEP = 0.001, 0.9, 0.999, 1e-08, 0.01, 10

VMEM_LIMIT = 56 * 1024 * 1024
TOKEN_TILE = 256
PROJ_TILE = 512
TAIL_TILE = 512
SMALL_ROWS = 40


def _params(sem):
    return pltpu.CompilerParams(dimension_semantics=sem, vmem_limit_bytes=VMEM_LIMIT)


_DN = {"nn": (((1,), (0,)), ((), ())), "nt": (((1,), (1,)), ((), ())), "tn": (((0,), (0,)), ((), ()))}


def _dot(a, b, form="nn"):
    return lax.dot_general(a.astype(MXU_DTYPE), b.astype(MXU_DTYPE), _DN[form], preferred_element_type=F32)


def _mask_dot(mask, v):
    bf = jnp.bfloat16
    hi = v.astype(bf)
    mid = (v - hi.astype(F32)).astype(bf)
    w = v.shape[1]
    s = lax.dot_general(mask.astype(bf), jnp.concatenate([hi, mid], axis=1), _DN["nn"], preferred_element_type=F32)
    return s[:, w:] + s[:, :w]


def _full(shape, single=False):
    n = len(shape)
    if single:
        return pl.BlockSpec(shape, lambda *_: (0,) * n, pipeline_mode=pl.Buffered(1))
    return pl.BlockSpec(shape, lambda *_: (0,) * n)


def _ordered_behind(body, in_specs, args, after):
    if after is None:
        return body
    at = len(in_specs)
    in_specs.append(pl.BlockSpec(memory_space=pl.ANY))
    args.append(after)
    return lambda *refs: body(*refs[:at], *refs[at + 1:])


def _sigmoid(z):
    return 0.5 * jnp.tanh(0.5 * z) + 0.5


def _gelu(x):
    c = 0.7978845608028654
    t = jnp.tanh(c * (x + 0.044715 * x * x * x))
    return 0.5 * x * (1.0 + t), t


def _gelu_grad(x, t):
    c = 0.7978845608028654
    return 0.5 * (1.0 + t) + 0.5 * x * (1.0 - t * t) * c * (1.0 + 3 * 0.044715 * x * x)


def _exchange(items, name, after=None):
    n = len(items)
    out_shape = []
    for a, mode in items:
        blk = a.shape if mode == "gather" else a.shape[1:]
        out_shape.append(jax.ShapeDtypeStruct((N_DEV,) + tuple(blk), a.dtype))

    def body(*refs):
        srcs, dsts = refs[:n], refs[n:2 * n]
        send_sems, recv_sems, local_sems = refs[2 * n:]
        x, y, c = lax.axis_index("x"), lax.axis_index("y"), lax.axis_index("c")
        me = 4 * x + 2 * y + c

        def src_for(i, dev):
            return srcs[i] if items[i][1] == "gather" else srcs[i].at[dev]

        local = [pltpu.make_async_copy(src_for(i, me), dsts[i].at[me], local_sems.at[i]) for i in range(n)]
        for cp in local:
            cp.start()
        remote = []
        for k in range(1, N_DEV):
            px = jnp.bitwise_xor(x, (k >> 2) & 1)
            py = jnp.bitwise_xor(y, (k >> 1) & 1)
            pc = jnp.bitwise_xor(c, k & 1)
            peer = 4 * px + 2 * py + pc
            for i in range(n):
                cp = pltpu.make_async_remote_copy(
                    src_ref=src_for(i, peer), dst_ref=dsts[i].at[me],
                    send_sem=send_sems.at[i * (N_DEV - 1) + k - 1], recv_sem=recv_sems.at[i * (N_DEV - 1) + k - 1],
                    device_id=(px, py, pc), device_id_type=pl.DeviceIdType.MESH)
                cp.start()
                remote.append(cp)
        for cp in remote:
            cp.wait()
        for cp in local:
            cp.wait()

    any_spec = pl.BlockSpec(memory_space=pl.ANY)
    in_specs, args = [any_spec] * n, [a for a, _ in items]
    if after is not None:
        in_specs.append(any_spec)
        args.append(after)
        exchange = body
        body = lambda *refs: exchange(*refs[:n], *refs[n + 1:])
    return pl.pallas_call(
        body, name=name, out_shape=out_shape, in_specs=in_specs, out_specs=[any_spec] * n,
        scratch_shapes=[pltpu.SemaphoreType.DMA((n * (N_DEV - 1),)), pltpu.SemaphoreType.DMA((n * (N_DEV - 1),)),
                        pltpu.SemaphoreType.DMA((n,))],
    )(*args)


def _gather_two_level(arrays, name):
    n = len(arrays)
    pieces = []
    for i, a in enumerate(arrays):
        rows = _Sender.PIECE_ROWS if a.shape[0] % _Sender.PIECE_ROWS == 0 else a.shape[0]
        pieces += [(i, r0, rows) for r0 in range(0, a.shape[0], rows)]

    def body(*refs):
        srcs, dsts = refs[:n], refs[n:2 * n]
        send_sems, recv_sems, local_sems = refs[2 * n:]
        x, y, c = lax.axis_index("x"), lax.axis_index("y"), lax.axis_index("c")
        me, sibling = (x, y, c), (x, y, 1 - c)
        x_nbr, y_nbr, diag = (1 - x, y, c), (x, 1 - y, c), (1 - x, 1 - y, c)

        def slot(px, py, pc):
            return 4 * px + 2 * py + pc

        def copy(u, k, block, to, own=False):
            i, r0, rows = pieces[u]
            there = dsts[i].at[slot(*block)].at[pl.ds(r0, rows)]
            return pltpu.make_async_remote_copy(
                src_ref=srcs[i].at[pl.ds(r0, rows)] if own else there, dst_ref=there,
                send_sem=send_sems.at[u * 7 + k], recv_sem=recv_sems.at[u * 7 + k],
                device_id=to, device_id_type=pl.DeviceIdType.MESH)

        units = range(len(pieces))
        mine = [pltpu.make_async_copy(srcs[i], dsts[i].at[slot(*me)], local_sems.at[i]) for i in range(n)]
        for cp in mine:
            cp.start()
        for u in units:
            copy(u, 1, me, x_nbr, own=True).start()
            copy(u, 2, me, y_nbr, own=True).start()
        for u in units:
            copy(u, 0, me, sibling, own=True).start()

        def relay_then_pass(k_from, frm, to, k_other, other):
            for u in units:
                copy(u, k_from, frm, me).wait_recv()
                copy(u, 3, frm, to).start()
                copy(u, 3 + k_from, frm, sibling).start()
            for u in units:
                copy(u, k_other, other, me).wait_recv()
                copy(u, 3 + k_other, other, sibling).start()

        @pl.when(c == 1)
        def _():
            relay_then_pass(1, x_nbr, y_nbr, 2, y_nbr)

        @pl.when(c == 0)
        def _():
            relay_then_pass(2, y_nbr, x_nbr, 1, x_nbr)

        for u in units:
            copy(u, 3, diag, me).wait_recv()
            copy(u, 6, diag, sibling).start()
        for u in units:
            copy(u, 0, sibling, me).wait_recv()
            for k, chip in ((4, x_nbr), (5, y_nbr), (6, diag)):
                copy(u, k, (chip[0], chip[1], 1 - c), me).wait_recv()
        for u in units:
            for k in range(7):
                copy(u, k, me, me, own=True).wait_send()
        for cp in mine:
            cp.wait()

    any_spec = pl.BlockSpec(memory_space=pl.ANY)
    return pl.pallas_call(
        body, name=name, out_shape=[jax.ShapeDtypeStruct((N_DEV,) + a.shape, a.dtype) for a in arrays],
        in_specs=[any_spec] * n, out_specs=[any_spec] * n,
        scratch_shapes=[pltpu.SemaphoreType.DMA((len(pieces) * 7,)), pltpu.SemaphoreType.DMA((len(pieces) * 7,)),
                        pltpu.SemaphoreType.DMA((n,))],
    )(*arrays)


_HBM = pl.BlockSpec(memory_space=pltpu.HBM)
_SEM = pl.BlockSpec(memory_space=pltpu.SEMAPHORE)
_EFFECT = pltpu.SideEffectType.DATAFLOW_SIDE_EFFECTING


def _split_copies(items, srcs, lands, send_sems, recv_sems):
    x, y, c = lax.axis_index("x"), lax.axis_index("y"), lax.axis_index("c")
    me = 4 * x + 2 * y + c
    copies = []
    for k in range(1, N_DEV):
        px = jnp.bitwise_xor(x, (k >> 2) & 1)
        py = jnp.bitwise_xor(y, (k >> 1) & 1)
        pc = jnp.bitwise_xor(c, k & 1)
        peer = 4 * px + 2 * py + pc
        for i in range(len(items)):
            src = srcs[i] if items[i][1] == "gather" else srcs[i].at[peer]
            copies.append(pltpu.make_async_remote_copy(
                src_ref=src, dst_ref=lands[i].at[me],
                send_sem=send_sems.at[i * (N_DEV - 1) + k - 1], recv_sem=recv_sems.at[i * (N_DEV - 1) + k - 1],
                device_id=(px, py, pc), device_id_type=pl.DeviceIdType.MESH))
    return me, copies


def _exchange_start(items, name, after):
    n = len(items)
    n_sem = n * (N_DEV - 1)
    srcs, lands = [], []
    for a, mode in items:
        blk = a.shape if mode == "gather" else a.shape[1:]
        srcs.append(pltpu.with_memory_space_constraint(a, pltpu.HBM))
        lands.append(pltpu.with_memory_space_constraint(lax.empty((N_DEV,) + tuple(blk), a.dtype), pltpu.HBM))

    def body(*refs):
        src_refs, land_refs = refs[:n], refs[n:2 * n]
        send_sems, recv_sems = refs[2 * n + 1], refs[2 * n + 2]
        local_sems = refs[4 * n + 3]
        me, copies = _split_copies(items, src_refs, land_refs, send_sems, recv_sems)
        for i in range(n):
            own = src_refs[i] if items[i][1] == "gather" else src_refs[i].at[me]
            cp = pltpu.make_async_copy(own, land_refs[i].at[me], local_sems.at[i])
            cp.start()
            cp.wait()
        for cp in copies:
            cp.start()

    out_shape = [pltpu.SemaphoreType.DMA((n_sem,)), pltpu.SemaphoreType.DMA((n_sem,))]
    out_shape += [pltpu.HBM(a.shape, a.dtype) for a in srcs] + [pltpu.HBM(a.shape, a.dtype) for a in lands]
    outs = pl.pallas_call(
        body, name=name, out_shape=out_shape,
        in_specs=[_HBM] * (2 * n) + [pl.BlockSpec(memory_space=pl.ANY)],
        out_specs=[_SEM, _SEM] + [_HBM] * (2 * n),
        input_output_aliases={i: 2 + i for i in range(2 * n)},
        scratch_shapes=[pltpu.SemaphoreType.DMA((n,))],
        compiler_params=pltpu.CompilerParams(has_side_effects=_EFFECT),
    )(*srcs, *lands, after)
    handle = (items, name, outs[0], outs[1], outs[2:2 + n], outs[2 + n:2 + 2 * n])
    return handle, outs[2]


class _Sender:
    PIECE_ROWS = 352

    def __init__(self, items, chunks=None):
        self.items, self.n = items, len(items)
        self.chunks = chunks
        if chunks is None:
            block_rows = [a.shape[0] if mode == "gather" else a.shape[1] for a, mode in items]
            self.chunks = [r // self.PIECE_ROWS if r % self.PIECE_ROWS == 0 else 1 for r in block_rows]
        self.srcs, self.lands = [], []
        for a, mode in items:
            blk = a.shape if mode == "gather" else a.shape[1:]
            self.srcs.append(pltpu.with_memory_space_constraint(a, pltpu.HBM))
            self.lands.append(pltpu.with_memory_space_constraint(lax.empty((N_DEV,) + tuple(blk), a.dtype), pltpu.HBM))

    def issue(self, src_refs, land_refs, send_sems, recv_sems, local_sems, step, n_steps):
        x, y, c = lax.axis_index("x"), lax.axis_index("y"), lax.axis_index("c")
        me = 4 * x + 2 * y + c
        copies = []
        for ch in range(max(self.chunks)):
            for k in range(1, N_DEV):
                px = jnp.bitwise_xor(x, (k >> 2) & 1)
                py = jnp.bitwise_xor(y, (k >> 1) & 1)
                pc = jnp.bitwise_xor(c, k & 1)
                peer = 4 * px + 2 * py + pc
                for i, (_, mode) in enumerate(self.items):
                    if ch >= self.chunks[i]:
                        continue
                    n_rows = land_refs[i].shape[1] // self.chunks[i]
                    rows = pl.ds(ch * n_rows, n_rows)
                    src = src_refs[i].at[rows] if mode == "gather" else src_refs[i].at[peer].at[rows]
                    copies.append(pltpu.make_async_remote_copy(
                        src_ref=src, dst_ref=land_refs[i].at[me].at[rows],
                        send_sem=send_sems.at[i * (N_DEV - 1) + k - 1], recv_sem=recv_sems.at[i * (N_DEV - 1) + k - 1],
                        device_id=(px, py, pc), device_id_type=pl.DeviceIdType.MESH))
        own = [pltpu.make_async_copy(src_refs[i] if mode == "gather" else src_refs[i].at[me], land_refs[i].at[me],
                                     local_sems.at[i]) for i, (_, mode) in enumerate(self.items)]

        @pl.when(step == 0)
        def _():
            for cp in own:
                cp.start()

        for s in range(n_steps):
            group = [cp for j, cp in enumerate(copies) if (j * n_steps) // len(copies) == s]
            if group:
                @pl.when(step == s)
                def _(group=group):
                    for cp in group:
                        cp.start()

        @pl.when(step == n_steps - 1)
        def _():
            for cp in own:
                cp.wait()


def _host_call(body, name, grid, in_specs, args, out_shape, out_specs, scratch_shapes, after=None, sender=None):
    in_specs, args, out_shape, out_specs = list(in_specs), list(args), list(out_shape), list(out_specs)
    scratch_shapes = list(scratch_shapes)
    semantics = ("arbitrary",) * len(grid)
    body = _ordered_behind(body, in_specs, args, after)
    if sender is None:
        res = pl.pallas_call(body, name=name, grid=grid, in_specs=in_specs, out_specs=out_specs, out_shape=out_shape,
                             scratch_shapes=scratch_shapes, compiler_params=_params(semantics))(*args)
        return res, None
    n, n_in, n_out, n_scr = sender.n, len(in_specs), len(out_shape), len(scratch_shapes)
    n_sem = n * (N_DEV - 1)
    n_steps = 1
    for g in grid:
        n_steps *= g
    compute = body

    def body(*refs):
        ins, s_in = refs[:n_in], refs[n_in:n_in + 2 * n]
        o0 = n_in + 2 * n
        outs, s_out = refs[o0:o0 + n_out], refs[o0 + n_out:o0 + n_out + 2 + 2 * n]
        scr = refs[o0 + n_out + 2 + 2 * n:]
        compute(*ins, *outs, *scr[:n_scr])
        step = pl.program_id(0)
        for d in range(1, len(grid)):
            step = step * grid[d] + pl.program_id(d)
        sender.issue(s_in[:n], s_in[n:], s_out[0], s_out[1], scr[n_scr], step, n_steps)

    res = pl.pallas_call(
        body, name=name, grid=grid,
        in_specs=in_specs + [_HBM] * (2 * n), out_specs=out_specs + [_SEM, _SEM] + [_HBM] * (2 * n),
        out_shape=out_shape + [pltpu.SemaphoreType.DMA((n_sem,)), pltpu.SemaphoreType.DMA((n_sem,))]
        + [pltpu.HBM(a.shape, a.dtype) for a in sender.srcs] + [pltpu.HBM(a.shape, a.dtype) for a in sender.lands],
        input_output_aliases={n_in + j: n_out + 2 + j for j in range(2 * n)},
        scratch_shapes=scratch_shapes + [pltpu.SemaphoreType.DMA((n,))],
        compiler_params=pltpu.CompilerParams(dimension_semantics=semantics, vmem_limit_bytes=VMEM_LIMIT,
                                             has_side_effects=_EFFECT),
    )(*args, *sender.srcs, *sender.lands)
    handle = (sender.items, name, res[n_out], res[n_out + 1], res[n_out + 2:n_out + 2 + n],
              res[n_out + 2 + n:n_out + 2 + 2 * n])
    return res[:n_out], handle


def _exchange_wait(handle, after):
    items, name, send_sems, recv_sems, srcs, lands = handle
    n = len(items)

    def body(*refs):
        src_refs, land_refs = refs[:n], refs[n:2 * n]
        send_ref, recv_ref = refs[2 * n], refs[2 * n + 1]
        _, copies = _split_copies(items, src_refs, land_refs, send_ref, recv_ref)
        for cp in copies:
            cp.wait_send()
            cp.wait_recv()

    outs = pl.pallas_call(
        body, name=name + "_wait",
        out_shape=[pltpu.HBM(a.shape, a.dtype) for a in srcs] + [pltpu.HBM(a.shape, a.dtype) for a in lands],
        in_specs=[_HBM] * (2 * n) + [_SEM, _SEM, pl.BlockSpec(memory_space=pl.ANY)], out_specs=[_HBM] * (2 * n),
        input_output_aliases={i: i for i in range(2 * n)},
        compiler_params=pltpu.CompilerParams(has_side_effects=_EFFECT),
    )(*srcs, *lands, send_sems, recv_sems, after)
    return outs[n:]


def _mod_fwd(cvec, w_mod_l, b_mod_l):
    rows, cols = cvec.shape[0], w_mod_l.shape[1]

    def body(c_ref, w_ref, b_ref, o_ref, s_ref):
        cv = c_ref[...]
        s = cv * _sigmoid(cv)
        s_ref[...] = s
        o_ref[...] = _dot(s, w_ref[...]) + b_ref[...]

    return pl.pallas_call(
        body, name="mod_fwd",
        out_shape=(jax.ShapeDtypeStruct((rows, cols), F32), jax.ShapeDtypeStruct((rows, D), F32)),
        in_specs=[_full((rows, D)), _full((D, cols)), _full((1, cols))],
        out_specs=(_full((rows, cols)), _full((rows, D))), grid=(1,),
        compiler_params=_params(("arbitrary",)),
    )(cvec, w_mod_l, b_mod_l)


def _mod_bwd(svec, dmod_l):
    rows, cols = dmod_l.shape

    def body(s_ref, d_ref, gw_ref):
        gw_ref[...] = _dot(s_ref[...], d_ref[...], "tn")

    return pl.pallas_call(
        body, name="mod_bwd", out_shape=jax.ShapeDtypeStruct((D, cols), F32),
        in_specs=[_full((rows, D)), _full((rows, cols))], out_specs=_full((D, cols)), grid=(1,),
        compiler_params=_params(("arbitrary",)),
    )(svec, dmod_l)


def _mod_bwd_ctx(c_row, d_l, w_mod_l):
    cols = d_l.shape[1]

    def body(c_ref, d_ref, w_ref, o_ref):
        cv = c_ref[...]
        sg = _sigmoid(cv)
        o_ref[...] = _dot(d_ref[...], w_ref[...], "nt") * (sg * (1.0 + cv * (1.0 - sg)))

    return pl.pallas_call(
        body, name="mod_bwd_ctx", out_shape=jax.ShapeDtypeStruct((8, D), F32),
        in_specs=[_full((1, D)), _full((8, cols)), _full((D, cols))], out_specs=_full((8, D)), grid=(1,),
        compiler_params=_params(("arbitrary",)),
    )(c_row, d_l, w_mod_l)


def _inproj(xt, modv, g, w_inT, n_cols, rows_per_example, name, after=None, sender=None):
    rows = xt.shape[0]
    tm = min(PROJ_TILE, rows_per_example)
    per_b = rows_per_example // tm
    shared_mod = modv.shape[0] == 1

    def body(x_ref, mod_ref, g_ref, w_ref, p_ref, h_ref):
        x = x_ref[...]
        r = lax.rsqrt(jnp.mean(x * x, axis=-1, keepdims=True) + EPS)
        h = (x * r * g_ref[...]) * (1.0 + mod_ref[0, 1:2, :]) + mod_ref[0, 0:1, :]
        hb = h.astype(MXU_DTYPE)
        h_ref[...] = hb
        for j in range(n_cols // KW):
            p_ref[:, j * KW:(j + 1) * KW] = _dot(hb, w_ref[j * KW:(j + 1) * KW, :], "nt").astype(p_ref.dtype)

    mod_idx = (lambda i: (0, 0, 0)) if shared_mod else (lambda i: (i // per_b, 0, 0))
    in_specs = [pl.BlockSpec((tm, D), lambda i: (i, 0)), pl.BlockSpec((1, N_MOD, D), mod_idx), _full((1, D)),
                pl.BlockSpec((n_cols, D), lambda i: (0, 0), pipeline_mode=pl.Buffered(1))]
    (p, h), handle = _host_call(
        body, name, (rows // tm,), in_specs, [xt, modv, g, w_inT],
        [jax.ShapeDtypeStruct((rows, n_cols), MXU_DTYPE), jax.ShapeDtypeStruct((rows, D), MXU_DTYPE)],
        [pl.BlockSpec((tm, n_cols), lambda i: (i, 0)), pl.BlockSpec((tm, D), lambda i: (i, 0))], [],
        after=after, sender=sender)
    return p, h, handle


def _tri(reverse, n):
    row = lax.broadcasted_iota(jnp.int32, (n, n), 0)
    col = lax.broadcasted_iota(jnp.int32, (n, n), 1)
    same = (row // CHUNK) == (col // CHUNK)
    return same & ((col >= row) if reverse else (col <= row))


def _per_chunk_rows(x, reverse):
    n = x.shape[0]
    rows = [x[j * CHUNK:j * CHUNK + 1] if reverse else x[(j + 1) * CHUNK - 1:(j + 1) * CHUNK] for j in range(n // CHUNK)]
    return jnp.concatenate([jnp.broadcast_to(r, (CHUNK, x.shape[1])) for r in rows], axis=0), rows


def _lower_bound(gam_ref, direction):
    return _sigmoid(gam_ref[direction:direction + 1, :] - gam_ref[2 + direction:3 + direction, :])


def _gate_prep(z, lb, tri, reverse, b=None):
    sg = _sigmoid(z)
    f = lb + (1.0 - lb) * sg
    g = jnp.log(f)
    b = _mask_dot(tri, g) if b is None else b
    bl, bl_rows = _per_chunk_rows(b, reverse)
    mid = 0.5 * bl
    return sg, g, 1.0 - f, b, jnp.exp(mid), [jnp.exp(0.5 * r) for r in bl_rows], jnp.exp(mid - b), mid


def _hgrn_fwd(p, gam, s0, rows_per_example, with_out, name, sender=None):
    rows = p.shape[0]
    nb_ex = rows // rows_per_example
    rb = min(TOKEN_TILE, rows_per_example)
    cpb = rb // CHUNK
    nb = rows_per_example // rb
    n_chunks = rows // CHUNK
    has_s0 = s0 is not None

    def body(*refs):
        it = iter(refs)
        gam_ref = next(it)
        zf_ref, vf_ref = next(it), next(it)
        qf_ref = next(it) if with_out else None
        zb_ref, vb_ref = next(it), next(it)
        qb_ref = next(it) if with_out else None
        s0_ref = next(it) if has_s0 else None
        if with_out:
            of_ref, ob_ref = next(it), next(it)
        stash_f, stash_b, bsum_f, bsum_b, fin_ref = next(it), next(it), next(it), next(it), next(it)
        st_ref = next(it)
        i = pl.program_id(1)

        @pl.when(i == 0)
        def _():
            if has_s0:
                st_ref[...] = s0_ref[:, 0]
            else:
                st_ref[...] = jnp.zeros_like(st_ref)

        for direction, (z_ref, v_ref, q_ref, stash, bsum_ref) in enumerate(
                ((zf_ref, vf_ref, qf_ref, stash_f, bsum_f), (zb_ref, vb_ref, qb_ref, stash_b, bsum_b))):
            reverse = direction == 1
            tri = _tri(reverse, rb)
            lb = _lower_bound(gam_ref, direction)
            z = z_ref[...].astype(F32)
            v = v_ref[...].astype(F32)
            _, _, k, b, em, em_rows, e2, mid = _gate_prep(z, lb, tri, reverse)
            bsum_ref[...] = b
            kd = (k * (e2 * em)).astype(MXU_DTYPE)
            vb = v.astype(MXU_DTYPE)
            if with_out:
                q = q_ref[...].astype(F32)
                qi = q * jnp.exp(b - mid)
                qe = (qi * em).astype(MXU_DTYPE)
                qi = qi.astype(MXU_DTYPE)
                ki = (k * e2).astype(MXU_DTYPE)
                intra = []
                for h in range(HEADS):
                    hs = slice(h * DK, (h + 1) * DK)
                    sc = jnp.where(tri, _dot(qi[:, hs], ki[:, hs], "nt"), 0.0)
                    intra.append(_dot(sc, vb[:, hs]))
            for j in (range(cpb - 1, -1, -1) if reverse else range(cpb)):
                rs = slice(j * CHUNK, (j + 1) * CHUNK)
                a = em_rows[j] * em_rows[j]
                for h in range(HEADS):
                    hs = slice(h * DK, (h + 1) * DK)
                    st = st_ref[direction, h]
                    stash[j, h] = st.astype(stash.dtype)
                    if with_out:
                        (ob_ref if reverse else of_ref)[rs, hs] = intra[h][rs] + _dot(qe[rs, hs], st, "nt")
                    st_ref[direction, h] = st * a[:, hs] + _dot(vb[rs, hs], kd[rs, hs], "tn")

        @pl.when(i == nb - 1)
        def _():
            fin_ref[:, 0] = st_ref[...]

    up = lambda b, i: b * nb + i
    down = lambda b, i: b * nb + nb - 1 - i
    col = lambda rowf, c: pl.BlockSpec((rb, KW), lambda b, i: (rowf(b, i), c))
    in_specs = [_full((4, KW)), col(up, 0), col(up, 2)] + ([col(up, 3)] if with_out else [])
    in_specs += [col(down, 1), col(down, 2)] + ([col(down, 3)] if with_out else [])
    args = [gam, p, p] + ([p] if with_out else []) + [p, p] + ([p] if with_out else [])
    if has_s0:
        in_specs.append(pl.BlockSpec((2, 1, HEADS, DK, DK), lambda b, i: (0, b, 0, 0, 0)))
        args.append(s0)
    out_shape, out_specs = [], []
    if with_out:
        out_shape += [jax.ShapeDtypeStruct((rows, KW), F32)] * 2
        out_specs += [pl.BlockSpec((rb, KW), lambda b, i: (up(b, i), 0)),
                      pl.BlockSpec((rb, KW), lambda b, i: (down(b, i), 0))]
    out_shape += [jax.ShapeDtypeStruct((n_chunks, HEADS, DK, DK), MXU_DTYPE)] * 2
    out_specs += [pl.BlockSpec((cpb, HEADS, DK, DK), lambda b, i: (up(b, i), 0, 0, 0)),
                  pl.BlockSpec((cpb, HEADS, DK, DK), lambda b, i: (down(b, i), 0, 0, 0))]
    out_shape += [jax.ShapeDtypeStruct((rows, KW), F32)] * 2
    out_specs += [pl.BlockSpec((rb, KW), lambda b, i: (up(b, i), 0)),
                  pl.BlockSpec((rb, KW), lambda b, i: (down(b, i), 0))]
    out_shape.append(jax.ShapeDtypeStruct((2, nb_ex, HEADS, DK, DK), F32))
    out_specs.append(pl.BlockSpec((2, 1, HEADS, DK, DK), lambda b, i: (0, b, 0, 0, 0)))
    res, handle = _host_call(body, name, (nb_ex, nb), in_specs, args, out_shape, out_specs,
                             [pltpu.VMEM((2, HEADS, DK, DK), F32)], sender=sender)
    return (*res, handle)


def _hgrn_bwd(p, gam, do, stash_f, stash_b, bsum_f, bsum_b, ds_end, rows_per_example, with_out, name, after=None,
              sender=None):
    rows = p.shape[0]
    nb_ex = rows // rows_per_example
    rb = min(TOKEN_TILE, rows_per_example)
    cpb = rb // CHUNK
    nb = rows_per_example // rb
    has_end = ds_end is not None

    def body(*refs):
        it = iter(refs)
        gam_ref = next(it)
        ins = []
        for _ in range(2):
            z_ref, v_ref = next(it), next(it)
            q_ref = next(it) if with_out else None
            do_ref = next(it) if with_out else None
            ins.append((z_ref, v_ref, q_ref, do_ref, next(it), next(it)))
        end_ref = next(it) if has_end else None
        outs = [next(it), next(it)]
        dlb_ref, ds0_ref = next(it), next(it)
        dst_ref = next(it)
        b_id, i = pl.program_id(0), pl.program_id(1)

        @pl.when(i == 0)
        def _():
            if has_end:
                dst_ref[...] = end_ref[:, 0]
            else:
                dst_ref[...] = jnp.zeros_like(dst_ref)

        @pl.when((i == 0) & (b_id == 0))
        def _():
            dlb_ref[...] = jnp.zeros_like(dlb_ref)

        for direction in range(2):
            z_ref, v_ref, q_ref, do_ref, stash, b_ref = ins[direction]
            dgrp_ref = outs[direction]
            reverse = direction == 1
            tri = _tri(reverse, rb)
            tri_t = _tri(not reverse, rb)
            lb = _lower_bound(gam_ref, direction)
            heads = [slice(h * DK, (h + 1) * DK) for h in range(HEADS)]
            chunks = [slice(j * CHUNK, (j + 1) * CHUNK) for j in range(cpb)]
            grid_cat = lambda parts: jnp.concatenate([jnp.concatenate(row, axis=1) for row in parts], axis=0)
            cat = lambda parts: jnp.concatenate(parts, axis=1)
            z = z_ref[...].astype(F32)
            sg, g, k, b, em, em_rows, e2, mid = _gate_prep(z, lb, tri, reverse, b=b_ref[...])
            e3 = e2 * em
            kd = k * e3
            kd_b = kd.astype(MXU_DTYPE)
            vb = v_ref[...].astype(MXU_DTYPE)
            if with_out:
                q = q_ref[...].astype(F32)
                dout = do_ref[...].astype(MXU_DTYPE)
                e1 = jnp.exp(b - mid)
                e4 = e1 * em
                qi, ki, qe = q * e1, k * e2, q * e4
                qi_b, ki_b, qe_b = qi.astype(MXU_DTYPE), ki.astype(MXU_DTYPE), qe.astype(MXU_DTYPE)
                dqi_p, dki_p, dv_p = [], [], []
                for hs in heads:
                    sc = jnp.where(tri, _dot(qi_b[:, hs], ki_b[:, hs], "nt"), 0.0)
                    dsc = jnp.where(tri, _dot(dout[:, hs], vb[:, hs], "nt"), 0.0)
                    dqi_p.append(_dot(dsc, ki_b[:, hs]))
                    dki_p.append(_dot(dsc, qi_b[:, hs], "tn"))
                    dv_p.append(_dot(sc, dout[:, hs], "tn"))
                dqi, dki, dv = cat(dqi_p), cat(dki_p), cat(dv_p)
                dqe = grid_cat([[_dot(dout[rs, hs], stash[j, h]) for h, hs in enumerate(heads)]
                                for j, rs in enumerate(chunks)])
                grow = [[_dot(dout[rs, hs], qe_b[rs, hs], "tn") for hs in heads] for rs in chunks]
            dkd_p = [[None] * HEADS for _ in range(cpb)]
            dvs_p = [[None] * HEADS for _ in range(cpb)]
            da_p = [[None] * HEADS for _ in range(cpb)]
            for j in (range(cpb) if reverse else range(cpb - 1, -1, -1)):
                rs = chunks[j]
                a = em_rows[j] * em_rows[j]
                for h, hs in enumerate(heads):
                    dst = dst_ref[direction, h]
                    dkd_p[j][h] = _dot(vb[rs, hs], dst)
                    dvs_p[j][h] = _dot(kd_b[rs, hs], dst, "nt")
                    da_p[j][h] = jnp.broadcast_to(
                        jnp.sum(dst * stash[j, h].astype(F32), axis=0, keepdims=True), (CHUNK, DK))
                    new_dst = dst * a[:, hs]
                    dst_ref[direction, h] = new_dst + grow[j][h] if with_out else new_dst
            dkd, dvs, da = grid_cat(dkd_p), grid_cat(dvs_p), grid_cat(da_p)
            t_kd = dkd * kd
            dk = dkd * e3
            db = -t_kd
            tot = t_kd
            if with_out:
                dgrp_ref[:, KW:2 * KW] = (dvs + dv).astype(dgrp_ref.dtype)
                dgrp_ref[:, 2 * KW:] = (dqi * e1 + dqe * e4).astype(dgrp_ref.dtype)
                dk = dk + dki * e2
                t_qi, t_ki, t_qe = dqi * qi, dki * ki, dqe * qe
                db = db + t_qi - t_ki + t_qe
                tot = tot + 0.5 * (t_ki - t_qi)
            else:
                dgrp_ref[:, KW:2 * KW] = dvs.astype(dgrp_ref.dtype)
            dbl = jnp.concatenate([jnp.broadcast_to(jnp.sum(tot[rs], axis=0, keepdims=True), (CHUNK, KW))
                                   for rs in chunks], axis=0) + da * (em * em)
            dg = _mask_dot(tri_t, db) + dbl
            df = dg * jnp.exp(-g) - dk
            dgrp_ref[:, 0:KW] = (df * (1.0 - lb) * sg * (1.0 - sg)).astype(dgrp_ref.dtype)
            dlb_ref[direction:direction + 1, :] += jnp.sum(df * (1.0 - sg), axis=0, keepdims=True)

        @pl.when(i == nb - 1)
        def _():
            ds0_ref[:, 0] = dst_ref[...]

    rows_of = (lambda b, i: b * nb + nb - 1 - i, lambda b, i: b * nb + i)
    in_specs, args = [_full((4, KW))], [gam]
    for direction in range(2):
        rf = rows_of[direction]
        col = lambda c, rf=rf: pl.BlockSpec((rb, KW), lambda b, i: (rf(b, i), c))
        in_specs += [col(direction), col(2)]
        args += [p, p]
        if with_out:
            in_specs += [col(3), col(0)]
            args += [p, do]
        in_specs += [pl.BlockSpec((cpb, HEADS, DK, DK), lambda b, i, rf=rf: (rf(b, i), 0, 0, 0)), col(0)]
        args += [(stash_f, stash_b)[direction], (bsum_f, bsum_b)[direction]]
    if has_end:
        in_specs.append(pl.BlockSpec((2, 1, HEADS, DK, DK), lambda b, i: (0, b, 0, 0, 0)))
        args.append(ds_end)
    out_shape, out_specs = [], []
    for direction in range(2):
        rf = rows_of[direction]
        width = (3 if with_out else 2) * KW
        out_shape.append(jax.ShapeDtypeStruct((rows, width), MXU_DTYPE))
        out_specs.append(pl.BlockSpec((rb, width), lambda b, i, rf=rf: (rf(b, i), 0)))
    out_shape += [jax.ShapeDtypeStruct((2, KW), F32), jax.ShapeDtypeStruct((2, nb_ex, HEADS, DK, DK), F32)]
    out_specs += [_full((2, KW)), pl.BlockSpec((2, 1, HEADS, DK, DK), lambda b, i: (0, b, 0, 0, 0))]
    res, handle = _host_call(body, name, (nb_ex, nb), in_specs, args, out_shape, out_specs,
                             [pltpu.VMEM((2, HEADS, DK, DK), F32)], after=after, sender=sender)
    return (*res, handle)


def _tail_forward(osum, og, u, v, ga, gb, gna, ln_g, ln_b, ws_ref, bs_ref, wpaT_ref, wpbT_ref, proj=None):
    tm = osum.shape[0]
    gna4 = jnp.concatenate([gna] * HEADS, axis=1)
    r_parts = []
    for h in range(HEADS):
        oh = osum[:, h * DK:(h + 1) * DK]
        r_parts.append(jnp.broadcast_to(lax.rsqrt(jnp.mean(oh * oh, axis=-1, keepdims=True) + EPS), (tm, DK)))
    r = jnp.concatenate(r_parts, axis=1)
    on = osum * r
    sg_og = _sigmoid(og)
    silu_og = og * sg_og
    oan = on * gna4
    oa = oan * silu_og
    ug, tu = _gelu(u)
    vg, tv = _gelu(v)
    mu = jnp.mean(vg, axis=-1, keepdims=True)
    vc = vg - mu
    rstd = lax.rsqrt(jnp.mean(vc * vc, axis=-1, keepdims=True) + EPS)
    vhat = vc * rstd
    vln = vhat * ln_g + ln_b
    blocks = []
    for n in range(tm // SGU_BLOCK):
        rs = slice(n * SGU_BLOCK, (n + 1) * SGU_BLOCK)
        blocks.append(jnp.concatenate(
            [_dot(ws_ref[g], vln[rs, g * DK:(g + 1) * DK]) + bs_ref[g] for g in range(GROUPS)], axis=1))
    mixed = jnp.concatenate(blocks, axis=0) if len(blocks) > 1 else blocks[0]
    obm = ug * mixed
    if proj is None:
        pa = _dot(oa, wpaT_ref[...], "nt")
        pb = _dot(obm, wpbT_ref[...], "nt")
    else:
        pa, pb = proj
    sga, sgb = _sigmoid(ga), _sigmoid(gb)
    merged = sga * pa + sgb * pb
    return dict(r=r, on=on, sg_og=sg_og, silu_og=silu_og, oan=oan, oa=oa, ug=ug, tu=tu, tv=tv, rstd=rstd, vhat=vhat,
                vln=vln, mixed=mixed, obm=obm, pa=pa, pb=pb, sga=sga, sgb=sgb, merged=merged, gna4=gna4)


def _tail_in_specs(tm):
    tile = lambda c: pl.BlockSpec((tm, KW), lambda i: (i, c))
    return [tile(c) for c in range(4, 11)]


def _tail_weight_specs():
    return [_full((1, DK)), _full((1, KW)), _full((1, KW)), _full((GROUPS, SGU_BLOCK, SGU_BLOCK)),
            _full((GROUPS, SGU_BLOCK, 1)), _full((D, KW), single=True), _full((D, KW), single=True),
            _full((D, D), single=True)]


def _read_tail_inputs(of_ref, ob_ref, pcols):
    osum = of_ref[...] + ob_ref[...]
    og, u, v = (pcols[j][...].astype(F32) for j in range(3))
    ga = jnp.concatenate([pcols[3][...], pcols[4][...]], axis=1).astype(F32)
    gb = jnp.concatenate([pcols[5][...], pcols[6][...]], axis=1).astype(F32)
    return osum, og, u, v, ga, gb


def _tail_fwd(p, o_up, o_down, xt, modv, gna, ln_g, ln_b, w_s, b_s, w_paT, w_pbT, w_o, rows_per_example):
    rows = xt.shape[0]
    tm = min(TAIL_TILE, rows_per_example)
    per_b = rows_per_example // tm

    def body(of_ref, ob_ref, *rest):
        pcols = rest[:7]
        (x_ref, mod_ref, gna_ref, lng_ref, lnb_ref, ws_ref, bs_ref, wpaT_ref, wpbT_ref, wo_ref,
         x1_ref, mix_ref, merged_ref, oa_ref, obm_ref, pa_ref, pb_ref) = rest[7:]
        t = _tail_forward(*_read_tail_inputs(of_ref, ob_ref, pcols), gna_ref[...], lng_ref[...], lnb_ref[...],
                          ws_ref, bs_ref, wpaT_ref, wpbT_ref)
        mix = _dot(t["merged"], wo_ref[...])
        x1_ref[...] = x_ref[...] + mod_ref[0, 2:3, :] * mix
        mix_ref[...] = mix.astype(mix_ref.dtype)
        merged_ref[...] = t["merged"].astype(merged_ref.dtype)
        oa_ref[...] = t["oa"].astype(oa_ref.dtype)
        obm_ref[...] = t["obm"].astype(obm_ref.dtype)
        pa_ref[...] = t["pa"].astype(pa_ref.dtype)
        pb_ref[...] = t["pb"].astype(pb_ref.dtype)

    row = lambda w: pl.BlockSpec((tm, w), lambda i: (i, 0))
    in_specs = [row(KW), row(KW)] + _tail_in_specs(tm) + [row(D), pl.BlockSpec((1, N_MOD, D), lambda i: (i // per_b, 0, 0))]
    in_specs += _tail_weight_specs()
    return pl.pallas_call(
        body, name="tail_fwd", grid=(rows // tm,),
        out_shape=(jax.ShapeDtypeStruct((rows, D), F32), jax.ShapeDtypeStruct((rows, D), MXU_DTYPE),
                   jax.ShapeDtypeStruct((rows, D), MXU_DTYPE), jax.ShapeDtypeStruct((rows, KW), MXU_DTYPE),
                   jax.ShapeDtypeStruct((rows, KW), MXU_DTYPE), jax.ShapeDtypeStruct((rows, D), MXU_DTYPE),
                   jax.ShapeDtypeStruct((rows, D), MXU_DTYPE)),
        in_specs=in_specs, out_specs=(row(D), row(D), row(D), row(KW), row(KW), row(D), row(D)),
        compiler_params=_params(("arbitrary",)),
    )(o_up, o_down, *([p] * 7), xt, modv, gna, ln_g, ln_b, w_s, b_s, w_paT, w_pbT, w_o)


def _tail_bwd(p, o_up, o_down, dx1, mix, pa, pb, modv, gna, ln_g, ln_b, w_s, b_s, w_paT, w_pbT, w_o, rows_per_example,
              after=None, sender=None):
    rows = dx1.shape[0]
    nb_ex = rows // rows_per_example
    tm = min(TAIL_TILE, rows_per_example)
    per_b = rows_per_example // tm

    def body(of_ref, ob_ref, *rest):
        pcols = rest[:7]
        (dx1_ref, mix_ref, pa_ref, pb_ref, mod_ref, gna_ref, lng_ref, lnb_ref, ws_ref, bs_ref, wpaT_ref, wpbT_ref, wo_ref,
         dpt_ref, do_ref, dmix_ref, dpa_ref, dpb_ref, dmod_ref, small_ref, dws_ref, dbs_ref) = rest[7:]
        i = pl.program_id(0)

        @pl.when(i == 0)
        def _():
            small_ref[...] = jnp.zeros_like(small_ref)
            dws_ref[...] = jnp.zeros_like(dws_ref)
            dbs_ref[...] = jnp.zeros_like(dbs_ref)

        @pl.when(i % per_b == 0)
        def _():
            dmod_ref[...] = jnp.zeros_like(dmod_ref)

        osum, og, u, v, ga, gb = _read_tail_inputs(of_ref, ob_ref, pcols)
        ln_g = lng_ref[...]
        t = _tail_forward(osum, og, u, v, ga, gb, gna_ref[...], ln_g, lnb_ref[...], ws_ref, bs_ref, wpaT_ref, wpbT_ref,
                          proj=(pa_ref[...].astype(F32), pb_ref[...].astype(F32)))
        dx1v = dx1_ref[...]
        dmod_ref[0, 2:3, :] += jnp.sum(dx1v * mix_ref[...].astype(F32), axis=0, keepdims=True)
        dmix = dx1v * mod_ref[0, 2:3, :]
        dmix_ref[...] = dmix.astype(dmix_ref.dtype)
        dmerged = _dot(dmix, wo_ref[...], "nt")
        sga, sgb = t["sga"], t["sgb"]
        dpa = dmerged * sga
        dpb = dmerged * sgb
        dpa_ref[...] = dpa.astype(dpa_ref.dtype)
        dpb_ref[...] = dpb.astype(dpb_ref.dtype)
        dga = dmerged * t["pa"] * sga * (1.0 - sga)
        dgb = dmerged * t["pb"] * sgb * (1.0 - sgb)
        doa = _dot(dpa, wpaT_ref[...])
        dobm = _dot(dpb, wpbT_ref[...])
        dug = dobm * t["mixed"]
        dmixed = dobm * t["ug"]
        du = dug * _gelu_grad(u, t["tu"])
        dvln_blocks = []
        for n in range(tm // SGU_BLOCK):
            rs = slice(n * SGU_BLOCK, (n + 1) * SGU_BLOCK)
            parts = []
            for g in range(GROUPS):
                gs = slice(g * DK, (g + 1) * DK)
                dm = dmixed[rs, gs]
                parts.append(_dot(ws_ref[g], dm, "tn"))
                dws_ref[g] += _dot(dm, t["vln"][rs, gs], "nt")
                dbs_ref[g] += jnp.sum(dm, axis=1, keepdims=True)
            dvln_blocks.append(jnp.concatenate(parts, axis=1))
        dvln = jnp.concatenate(dvln_blocks, axis=0) if len(dvln_blocks) > 1 else dvln_blocks[0]
        vhat = t["vhat"]
        small_ref[1:2, 0:KW] += jnp.sum(dvln * vhat, axis=0, keepdims=True)
        small_ref[2:3, 0:KW] += jnp.sum(dvln, axis=0, keepdims=True)
        dvhat = dvln * ln_g
        dvg = t["rstd"] * (dvhat - jnp.mean(dvhat, axis=-1, keepdims=True)
                           - vhat * jnp.mean(dvhat * vhat, axis=-1, keepdims=True))
        dv = dvg * _gelu_grad(v, t["tv"])
        sg_og = t["sg_og"]
        doan = doa * t["silu_og"]
        dog = doa * t["oan"] * (sg_og * (1.0 + og * (1.0 - sg_og)))
        prod = doan * t["on"]
        dgna = jnp.zeros((1, DK), F32)
        for h in range(HEADS):
            dgna = dgna + jnp.sum(prod[:, h * DK:(h + 1) * DK], axis=0, keepdims=True)
        small_ref[0:1, 0:DK] += dgna
        don = doan * t["gna4"]
        dot_parts = []
        for h in range(HEADS):
            hs = slice(h * DK, (h + 1) * DK)
            m = jnp.mean(don[:, hs] * t["on"][:, hs], axis=-1, keepdims=True)
            dot_parts.append(t["r"][:, hs] * (don[:, hs] - t["on"][:, hs] * m))
        do_ref[...] = jnp.concatenate(dot_parts, axis=1).astype(do_ref.dtype)
        for j, val in enumerate((dog, du, dv)):
            dpt_ref[:, j * KW:(j + 1) * KW] = val.astype(dpt_ref.dtype)
        dpt_ref[:, 3 * KW:3 * KW + D] = dga.astype(dpt_ref.dtype)
        dpt_ref[:, 3 * KW + D:] = dgb.astype(dpt_ref.dtype)

    row = lambda w: pl.BlockSpec((tm, w), lambda i: (i, 0))
    in_specs = [row(KW), row(KW)] + _tail_in_specs(tm) + [row(D)] * 4 + [pl.BlockSpec((1, N_MOD, D), lambda i: (i // per_b, 0, 0))]
    in_specs += _tail_weight_specs()
    args = [o_up, o_down, *([p] * 7), dx1, mix, pa, pb, modv, gna, ln_g, ln_b, w_s, b_s, w_paT, w_pbT, w_o]
    cd = MXU_DTYPE
    res, handle = _host_call(
        body, "tail_bwd", (rows // tm,), in_specs, args,
        [jax.ShapeDtypeStruct((rows, TAIL_COLS), cd), jax.ShapeDtypeStruct((rows, KW), cd),
         jax.ShapeDtypeStruct((rows, D), cd), jax.ShapeDtypeStruct((rows, D), cd),
         jax.ShapeDtypeStruct((rows, D), cd), jax.ShapeDtypeStruct((nb_ex, 8, D), F32),
         jax.ShapeDtypeStruct((8, D), F32), jax.ShapeDtypeStruct((GROUPS, SGU_BLOCK, SGU_BLOCK), F32),
         jax.ShapeDtypeStruct((GROUPS, SGU_BLOCK, 1), F32)],
        [row(TAIL_COLS), row(KW), row(D), row(D), row(D),
         pl.BlockSpec((1, 8, D), lambda i: (i // per_b, 0, 0)), _full((8, D)),
         _full((GROUPS, SGU_BLOCK, SGU_BLOCK)), _full((GROUPS, SGU_BLOCK, 1))], [],
        after=after, sender=sender)
    return (*res, handle)


def _ffn(x1, target, modv, g_ffn, g_final, w_upT, w_down, rows_per_example):
    rows = x1.shape[0]
    nb_ex = rows // rows_per_example
    tm = min(TOKEN_TILE, rows_per_example)
    per_b = rows_per_example // tm
    n_ff = D_FF // FF_CHUNK

    def body(x1_ref, tgt_ref, mod_ref, gffn_ref, gfin_ref, wup_ref, wdn_ref,
             dx1_ref, h2_ref, dffn_ref, act_ref, dup_ref, dmod_ref, small_ref, up_scr):
        i = pl.program_id(0)

        @pl.when(i == 0)
        def _():
            small_ref[...] = jnp.zeros_like(small_ref)

        @pl.when(i % per_b == 0)
        def _():
            dmod_ref[...] = jnp.zeros_like(dmod_ref)

        x1v = x1_ref[...]
        g2 = gffn_ref[...]
        m3, m4, m5 = mod_ref[0, 3:4, :], mod_ref[0, 4:5, :], mod_ref[0, 5:6, :]
        r2 = lax.rsqrt(jnp.mean(x1v * x1v, axis=-1, keepdims=True) + EPS)
        xn2 = x1v * r2
        h2 = (xn2 * g2) * (1.0 + m4) + m3
        h2b = h2.astype(MXU_DTYPE)
        h2_ref[...] = h2b
        def up_pair(j):
            lo = j * FF_CHUNK
            return (_dot(h2b, wup_ref[lo:lo + FF_CHUNK, :], "nt"),
                    _dot(h2b, wup_ref[D_FF + lo:D_FF + lo + FF_CHUNK, :], "nt"))

        group_end = {min(e, n_ff): s for s, e in ((0, 4), (4, 8), (8, 12))}
        cur, ffn = up_pair(0), None
        for j in range(n_ff):
            nxt = up_pair(j + 1) if j + 1 < n_ff else None
            cs = slice(j * FF_CHUNK, (j + 1) * FF_CHUNK)
            a, bgate = cur
            up_scr[:, cs] = a
            up_scr[:, D_FF + j * FF_CHUNK:D_FF + (j + 1) * FF_CHUNK] = bgate
            act_ref[:, cs] = (a * _sigmoid(a) * bgate).astype(MXU_DTYPE)
            cur = nxt
            if j + 1 in group_end:
                gs = slice(group_end[j + 1] * FF_CHUNK, (j + 1) * FF_CHUNK)
                part = _dot(act_ref[:, gs], wdn_ref[gs, :])
                ffn = part if ffn is None else ffn + part
        x2 = x1v + m5 * ffn
        r3 = lax.rsqrt(jnp.mean(x2 * x2, axis=-1, keepdims=True) + EPS)
        xn3 = x2 * r3
        gf = gfin_ref[...]
        err = xn3 * gf - tgt_ref[...]
        loss = 0.5 * jnp.sum(jnp.mean(err * err, axis=-1, keepdims=True), axis=0, keepdims=True)
        small_ref[2:3, :] += jnp.broadcast_to(loss, (1, D))
        dy = err * (1.0 / D)
        small_ref[1:2, :] += jnp.sum(dy * xn3, axis=0, keepdims=True)
        dxn3 = dy * gf
        dx2 = r3 * (dxn3 - xn3 * jnp.mean(dxn3 * xn3, axis=-1, keepdims=True))
        dmod_ref[0, 5:6, :] += jnp.sum(dx2 * ffn, axis=0, keepdims=True)
        dffn = (dx2 * m5).astype(MXU_DTYPE)
        dffn_ref[...] = dffn
        dact_of = lambda j: _dot(dffn, wdn_ref[j * FF_CHUNK:(j + 1) * FF_CHUNK, :], "nt")
        cur, dh2 = dact_of(0), None
        for j in range(n_ff):
            nxt = dact_of(j + 1) if j + 1 < n_ff else None
            cs = slice(j * FF_CHUNK, (j + 1) * FF_CHUNK)
            a, bgate = up_scr[:, cs], up_scr[:, D_FF + j * FF_CHUNK:D_FF + (j + 1) * FF_CHUNK]
            s = _sigmoid(a)
            dup_ref[:, cs] = (cur * bgate * (s * (1.0 + a * (1.0 - s)))).astype(MXU_DTYPE)
            dup_ref[:, D_FF + j * FF_CHUNK:D_FF + (j + 1) * FF_CHUNK] = (cur * a * s).astype(MXU_DTYPE)
            cur = nxt
            if j + 1 in group_end:
                lo, hi = group_end[j + 1] * FF_CHUNK, (j + 1) * FF_CHUNK
                part = (_dot(dup_ref[:, lo:hi], wup_ref[lo:hi, :])
                        + _dot(dup_ref[:, D_FF + lo:D_FF + hi], wup_ref[D_FF + lo:D_FF + hi, :]))
                dh2 = part if dh2 is None else dh2 + part
        dmod_ref[0, 3:4, :] += jnp.sum(dh2, axis=0, keepdims=True)
        dmod_ref[0, 4:5, :] += jnp.sum(dh2 * xn2 * g2, axis=0, keepdims=True)
        small_ref[0:1, :] += jnp.sum(dh2 * (1.0 + m4) * xn2, axis=0, keepdims=True)
        dxn2 = dh2 * g2 * (1.0 + m4)
        dx1_ref[...] = dx2 + r2 * (dxn2 - xn2 * jnp.mean(dxn2 * xn2, axis=-1, keepdims=True))

    row = lambda w: pl.BlockSpec((tm, w), lambda i: (i, 0))
    cd = MXU_DTYPE
    return pl.pallas_call(
        body, name="ffn_fwd_bwd", grid=(rows // tm,),
        out_shape=(jax.ShapeDtypeStruct((rows, D), F32), jax.ShapeDtypeStruct((rows, D), cd),
                   jax.ShapeDtypeStruct((rows, D), cd), jax.ShapeDtypeStruct((rows, D_FF), cd),
                   jax.ShapeDtypeStruct((rows, 2 * D_FF), cd), jax.ShapeDtypeStruct((nb_ex, 8, D), F32),
                   jax.ShapeDtypeStruct((8, D), F32)),
        in_specs=[row(D), row(D), pl.BlockSpec((1, N_MOD, D), lambda i: (i // per_b, 0, 0)), _full((1, D)), _full((1, D)),
                  _full((2 * D_FF, D), single=True), _full((D_FF, D), single=True)],
        out_specs=(row(D), row(D), row(D), row(D_FF), row(2 * D_FF),
                   pl.BlockSpec((1, 8, D), lambda i: (i // per_b, 0, 0)), _full((8, D))),
        scratch_shapes=[pltpu.VMEM((tm, 2 * D_FF), F32)],
        compiler_params=_params(("arbitrary",)),
    )(x1, target, modv, g_ffn, g_final, w_upT, w_down)


def _scan_columns(up, down, n_groups):
    cols = [up[:, 0:KW].astype(F32), down[:, 0:KW].astype(F32)]
    for j in range(1, n_groups):
        cols.append(up[:, j * KW:(j + 1) * KW].astype(F32) + down[:, j * KW:(j + 1) * KW].astype(F32))
    return cols


def _inproj_bwd(d_up, d_down, dpt, xt, dx1, modv, g, w_inT, rows_per_example, name, sender=None):
    rows = xt.shape[0]
    latent = dx1 is not None
    n_cols = IN_COLS if latent else CTX_COLS
    n_groups = d_up.shape[1] // KW
    tm = min(PROJ_TILE, rows_per_example)
    per_b = rows_per_example // tm
    n_mod_blocks = rows // rows_per_example if latent else 1

    def body(*refs):
        it = iter(refs)
        up_ref, down_ref = next(it), next(it)
        dpt_ref = next(it) if latent else None
        x_ref = next(it)
        dx1_ref = next(it) if latent else None
        mod_ref, g_ref, w_ref = next(it), next(it), next(it)
        gx_ref = next(it) if latent else None
        dp_out = None if latent else next(it)
        dmod_ref, small_ref = next(it), next(it)
        dp_ref = next(it) if latent else dp_out
        i = pl.program_id(0)

        @pl.when(i == 0)
        def _():
            small_ref[...] = jnp.zeros_like(small_ref)

        @pl.when((i % per_b == 0) if latent else (i == 0))
        def _():
            dmod_ref[...] = jnp.zeros_like(dmod_ref)

        for j, val in enumerate(_scan_columns(up_ref[...], down_ref[...], n_groups)):
            dp_ref[:, j * KW:(j + 1) * KW] = val.astype(MXU_DTYPE)
        if latent:
            dh = _dot(dp_ref[...], w_ref[0:4 * KW, :]) + _dot(dpt_ref[...], w_ref[4 * KW:, :])
        else:
            dh = _dot(dp_ref[...], w_ref[...])
        x = x_ref[...]
        gv = g_ref[...]
        m1 = mod_ref[0, 1:2, :]
        r = lax.rsqrt(jnp.mean(x * x, axis=-1, keepdims=True) + EPS)
        xn = x * r
        dmod_ref[0, 0:1, :] += jnp.sum(dh, axis=0, keepdims=True)
        dmod_ref[0, 1:2, :] += jnp.sum(dh * xn * gv, axis=0, keepdims=True)
        small_ref[0:1, :] += jnp.sum(dh * (1.0 + m1) * xn, axis=0, keepdims=True)
        if latent:
            dxn = dh * gv * (1.0 + m1)
            gx_ref[...] = dx1_ref[...] + r * (dxn - xn * jnp.mean(dxn * xn, axis=-1, keepdims=True))

    row = lambda w: pl.BlockSpec((tm, w), lambda i: (i, 0))
    mod_idx = (lambda i: (i // per_b, 0, 0)) if latent else (lambda i: (0, 0, 0))
    in_specs = [row(n_groups * KW)] * 2 + ([row(TAIL_COLS)] if latent else []) + [row(D)] + ([row(D)] if latent else [])
    in_specs += [pl.BlockSpec((1, N_MOD, D), mod_idx), _full((1, D)),
                 pl.BlockSpec((n_cols, D), lambda i: (0, 0), pipeline_mode=pl.Buffered(1))]
    args = [d_up, d_down] + ([dpt] if latent else []) + [xt] + ([dx1] if latent else []) + [modv, g, w_inT]
    first = jax.ShapeDtypeStruct((rows, D), F32) if latent else jax.ShapeDtypeStruct((rows, n_cols), MXU_DTYPE)
    out_shape = [first, jax.ShapeDtypeStruct((n_mod_blocks, 8, D), F32), jax.ShapeDtypeStruct((8, D), F32)]
    out_specs = [row(D) if latent else row(n_cols), pl.BlockSpec((1, 8, D), mod_idx), _full((8, D))]
    scratch = [pltpu.VMEM((tm, 4 * KW), MXU_DTYPE)] if latent else []
    res, handle = _host_call(body, name, (rows // tm,), in_specs, args, out_shape, out_specs, scratch, sender=sender)
    return (*res, handle)


def _grad_matmul(a, b, name, init=None, tn=512, sender=None):
    rows, n = a.shape
    k = b.shape[1]
    tn = min(tn, n)
    has_init = init is not None
    init_blocks = init.shape[0] // tn if has_init else 0

    def body(*refs):
        if has_init:
            a_ref, b_ref, init_ref, o_ref = refs
        else:
            a_ref, b_ref, o_ref = refs
        g = _dot(a_ref[...], b_ref[...], "tn")
        if has_init:
            g = g + jnp.where(pl.program_id(0) < init_blocks, init_ref[...].astype(F32), 0.0)
        o_ref[...] = g.astype(o_ref.dtype)

    in_specs = [pl.BlockSpec((rows, tn), lambda i: (0, i)), _full((rows, k), single=True)]
    args = [a, b]
    if has_init:
        in_specs.append(pl.BlockSpec((tn, k), lambda i: (jnp.minimum(i, init_blocks - 1), 0)))
        args.append(init)
    (out,), handle = _host_call(
        body, name, (n // tn,), in_specs, args, [jax.ShapeDtypeStruct((n, k), PAYLOAD_DTYPE)],
        [pl.BlockSpec((tn, k), lambda i: (i, 0))], [], sender=sender)
    return out, handle


def _grad_in(d_up, d_down, dpt, h, init, sender=None):
    rows = h.shape[0]
    tn = 256
    per_group = KW // tn
    n_scan = 4 * per_group
    init_blocks = init.shape[0] // tn

    def body(up_ref, down_ref, dpt_ref, h_ref, init_ref, o_ref):
        i = pl.program_id(0)
        both = (up_ref[...].astype(F32) + down_ref[...].astype(F32)).astype(MXU_DTYPE)
        a = jnp.where(i < per_group, up_ref[...],
                      jnp.where(i < 2 * per_group, down_ref[...], jnp.where(i < n_scan, both, dpt_ref[...])))
        g = _dot(a, h_ref[...], "tn") + jnp.where(i < init_blocks, init_ref[...].astype(F32), 0.0)
        o_ref[...] = g.astype(o_ref.dtype)

    last = 3 * per_group - 1
    col = lambda f: pl.BlockSpec((rows, tn), lambda i: (0, f(i)))
    in_specs = [col(lambda i: jnp.clip(jnp.where(i < per_group, i, i - per_group), 0, last)),
                col(lambda i: jnp.clip(i - per_group, 0, last)),
                col(lambda i: jnp.clip(i - n_scan, 0, TAIL_COLS // tn - 1)),
                _full((rows, D), single=True),
                pl.BlockSpec((tn, D), lambda i: (jnp.minimum(i, init_blocks - 1), 0))]
    (out,), handle = _host_call(
        body, "gw_in", (IN_COLS // tn,), in_specs, [d_up, d_down, dpt, h, init],
        [jax.ShapeDtypeStruct((IN_COLS, D), PAYLOAD_DTYPE)], [pl.BlockSpec((tn, D), lambda i: (i, 0))], [],
        sender=sender)
    return out, handle


def _row_tile(rows, limit=256):
    if rows <= limit:
        return rows
    for t in range(limit, 7, -8):
        if rows % t == 0:
            return t
    return rows


def _sum8(stack, name):
    _, rows, cols = stack.shape
    tr = _row_tile(rows)

    def body(s_ref, o_ref):
        acc = s_ref[0].astype(F32)
        for j in range(1, N_DEV):
            acc = acc + s_ref[j].astype(F32)
        o_ref[...] = acc

    return pl.pallas_call(
        body, name=name, grid=(rows // tr,), out_shape=jax.ShapeDtypeStruct((rows, cols), F32),
        in_specs=[pl.BlockSpec((N_DEV, tr, cols), lambda i: (0, i, 0))],
        out_specs=pl.BlockSpec((tr, cols), lambda i: (i, 0)),
        compiler_params=_params(("arbitrary",)),
    )(stack)


def _adamw_update(w, gv, m, v):
    nm = ADAM_B1 * m + (1.0 - ADAM_B1) * gv
    nv = ADAM_B2 * v + (1.0 - ADAM_B2) * (gv * gv)
    m_hat = nm / (1.0 - ADAM_B1 ** ADAM_STEP)
    v_hat = nv / (1.0 - ADAM_B2 ** ADAM_STEP)
    return -ADAM_LR * (m_hat / (jnp.sqrt(v_hat) + ADAM_EPS) + ADAM_WD * w), nm, nv


SMALL_PARAMS = (("g_mix", 0, D), ("g_ffn", 1, D), ("g_final", 2, D), ("g_norm_a", 3, DK), ("ln_v_g", 4, KW),
                ("ln_v_b", 5, KW), ("b_s", 6, GROUPS * SGU_BLOCK),
                ("c_ctx", 15, D))


def _small_finish(early, late, dws, gam, nb_ex, params):
    names = [n for n, _, _ in SMALL_PARAMS] + ["b_mod", "w_s"]

    def body(*refs):
        s_ref, l_ref, dws_ref, gam_ref = refs[:4]
        p_refs = refs[4:4 + 3 * len(names)]
        tot_ref, dgam_ref = refs[4 + 3 * len(names):6 + 3 * len(names)]
        o_refs = refs[6 + 3 * len(names):]
        acc = s_ref[0] + l_ref[0]
        gws = dws_ref[0]
        for j in range(1, N_DEV):
            acc = acc + (s_ref[j] + l_ref[j])
            gws = gws + dws_ref[j]
        tot_ref[...] = acc
        bm = acc[8:8 + N_MOD, :]
        for e in range(nb_ex):
            bm = bm + acc[16 + e * N_MOD:16 + (e + 1) * N_MOD, :]
        lb = jnp.concatenate([_lower_bound(gam_ref, 0), _lower_bound(gam_ref, 1)], axis=1)
        dgam = acc[7:8, :] * lb * (1.0 - lb)
        dgam_ref[...] = jnp.concatenate([dgam, -dgam], axis=0)
        bm = jnp.concatenate([bm[j:j + 1] for j in range(N_MOD)], axis=1)
        grads = [acc[row:row + 1, 0:width] for _, row, width in SMALL_PARAMS] + [bm, gws]
        for k, g in enumerate(grads):
            w_ref, m_ref, v_ref = p_refs[3 * k:3 * k + 3]
            o_refs[4 * k][...] = g
            o_refs[4 * k + 1][...], o_refs[4 * k + 2][...], o_refs[4 * k + 3][...] = _adamw_update(
                w_ref[...], g, m_ref[...], v_ref[...])

    p_args, p_specs, o_shapes, o_specs = [], [], [], []
    for n in names:
        for a in params[n]:
            p_args.append(a)
            p_specs.append(_full(a.shape))
        o_shapes += [jax.ShapeDtypeStruct(params[n][0].shape, F32)] * 4
        o_specs += [_full(params[n][0].shape)] * 4
    res = pl.pallas_call(
        body, name="small_finish", grid=(1,),
        out_shape=[jax.ShapeDtypeStruct((SMALL_ROWS, D), F32), jax.ShapeDtypeStruct((2, D), F32)] + o_shapes,
        in_specs=[_full(early.shape), _full(late.shape), _full(dws.shape), _full((4, KW))] + p_specs,
        out_specs=[_full((SMALL_ROWS, D)), _full((2, D))] + o_specs,
        compiler_params=_params(("arbitrary",)),
    )(early, late, dws, gam, *p_args)
    return res[0], res[1], {n: res[2 + 4 * k:6 + 4 * k] for k, n in enumerate(names)}


def _adamw_sum8(stack, w, m, v, name):
    _, rows, cols = stack.shape
    tr = _row_tile(rows)

    def body(s_ref, w_ref, m_ref, v_ref, g_ref, d_ref, nm_ref, nv_ref):
        gv = s_ref[0].astype(F32)
        for j in range(1, N_DEV):
            gv = gv + s_ref[j].astype(F32)
        g_ref[...] = gv
        d_ref[...], nm_ref[...], nv_ref[...] = _adamw_update(w_ref[...], gv, m_ref[...], v_ref[...])

    blk = pl.BlockSpec((tr, cols), lambda i: (i, 0))
    sd = jax.ShapeDtypeStruct((rows, cols), F32)
    return pl.pallas_call(
        body, name=name, grid=(rows // tr,), out_shape=(sd, sd, sd, sd),
        in_specs=[pl.BlockSpec((N_DEV, tr, cols), lambda i: (0, i, 0)), blk, blk, blk], out_specs=(blk, blk, blk, blk),
        compiler_params=_params(("arbitrary",)),
    )(stack, w, m, v)


def _adamw(w, g, m, v, name):
    shape = w.shape
    cols = shape[-1]
    rows = 1
    for s in shape[:-1]:
        rows *= s
    tr = _row_tile(rows)

    def body(w_ref, g_ref, m_ref, v_ref, d_ref, nm_ref, nv_ref):
        gv = g_ref[...]
        nm = ADAM_B1 * m_ref[...] + (1.0 - ADAM_B1) * gv
        nv = ADAM_B2 * v_ref[...] + (1.0 - ADAM_B2) * (gv * gv)
        m_hat = nm / (1.0 - ADAM_B1 ** ADAM_STEP)
        v_hat = nv / (1.0 - ADAM_B2 ** ADAM_STEP)
        d_ref[...] = -ADAM_LR * (m_hat / (jnp.sqrt(v_hat) + ADAM_EPS) + ADAM_WD * w_ref[...])
        nm_ref[...] = nm
        nv_ref[...] = nv

    blk = pl.BlockSpec((tr, cols), lambda i: (i, 0))
    sd = jax.ShapeDtypeStruct((rows, cols), F32)
    d, nm, nv = pl.pallas_call(
        body, name=name, grid=(rows // tr,), out_shape=(sd, sd, sd), in_specs=[blk] * 4, out_specs=(blk, blk, blk),
        compiler_params=_params(("arbitrary",)),
    )(w.reshape(rows, cols), g.reshape(rows, cols), m.reshape(rows, cols), v.reshape(rows, cols))
    return d.reshape(shape), nm.reshape(shape), nv.reshape(shape)


def _owner_blocks(a):
    return a.reshape(N_DEV, a.shape[0] // N_DEV, a.shape[1])


class _LocalWeights:
    def __init__(self, w_upT, w_down, w_o, w_paT, w_pbT):
        self.weights = (w_upT, w_down, w_o, w_paT, w_pbT)
        self.items = {}

    def sender(self, stage, items=None):
        self.items[stage] = items
        return None

    def sent(self, stage, handle):
        pass

    def mixer_weights(self, after):
        return self.weights[1:]

    def ffn_weights(self, after):
        return self.weights[0]

    def c_ctx_part(self, after):
        return jnp.zeros((1, D), F32)


def _local_step(x, ctx, target, modv, mcv, gam, g_mix, g_ffn, gna, ln_g, ln_b, w_s, b_s, g_final, w_inT, comm):
    nb_ex, seq, _ = x.shape
    ctx_len = ctx.shape[1]
    xt = x.reshape(nb_ex * seq, D)
    ct = ctx.reshape(nb_ex * ctx_len, D)
    tgt = target.reshape(nb_ex * seq, D)
    bs3 = b_s.reshape(GROUPS, SGU_BLOCK, 1)

    pc, hc, _ = _inproj(ct, mcv, g_mix, w_inT, CTX_COLS, ctx_len, "inproj_ctx")
    p, h, handle = _inproj(xt, modv, g_mix, w_inT, IN_COLS, seq, "inproj_lat", sender=comm.sender("inproj"))
    comm.sent("inproj", handle)
    cst_f, cst_b, cb_f, cb_b, s_ctx, _ = _hgrn_fwd(pc, gam, None, ctx_len, False, "hgrn_fwd_ctx")
    o_up, o_down, st_f, st_b, b_f, b_b, _, handle = _hgrn_fwd(p, gam, s_ctx, seq, True, "hgrn_fwd_lat",
                                                              sender=comm.sender("scan"))
    comm.sent("scan", handle)
    w_down, w_o, w_paT, w_pbT = comm.mixer_weights(o_up)
    x1, mix, merged, oa, obm, pa, pb = _tail_fwd(p, o_up, o_down, xt, modv, gna, ln_g, ln_b, w_s, bs3, w_paT, w_pbT,
                                                 w_o, seq)
    w_upT = comm.ffn_weights(x1)
    dx1, h2, dffn, act, dup, dmod_ffn, small_ffn = _ffn(x1, tgt, modv, g_ffn, g_final, w_upT, w_down, seq)
    gw_upT, _ = _grad_matmul(dup, h2, "gw_up")
    gw_down, _ = _grad_matmul(act, dffn, "gw_down", tn=256)
    scatter = lambda *grads: [(_owner_blocks(g), "scatter") for g in grads]
    dpt, do, dmix, dpa, dpb, dmod_tail, small_tail, dws, dbs, handle = _tail_bwd(
        p, o_up, o_down, dx1, mix, pa, pb, modv, gna, ln_g, ln_b, w_s, bs3, w_paT, w_pbT, w_o, seq,
        sender=comm.sender("tail_bwd", scatter(gw_upT)))
    comm.sent("tail_bwd", handle)
    gw_o, _ = _grad_matmul(merged, dmix, "gw_o")
    gw_paT, _ = _grad_matmul(dpa, oa, "gw_pa")
    gw_pbT, _ = _grad_matmul(dpb, obm, "gw_pb")
    def at_row(row, a):
        return jnp.pad(a, ((row, SMALL_ROWS - row - a.shape[0]), (0, D - a.shape[1])))

    small_early = (at_row(1, small_ffn[0:2])
                   + at_row(3, small_tail[0:3])
                   + at_row(6, dbs.reshape(1, GROUPS * SGU_BLOCK))
                   + at_row(14, small_ffn[2:3]))
    dws_rows = dws.reshape(GROUPS * SGU_BLOCK, SGU_BLOCK)
    d_up, d_down, dlb, ds0, handle = _hgrn_bwd(
        p, gam, do, st_f, st_b, b_f, b_b, None, seq, True, "hgrn_bwd_lat",
        sender=comm.sender("scan_bwd", scatter(gw_down, gw_o, gw_paT, gw_pbT)
                           + [(small_early, "gather"), (dws_rows, "gather")]))
    comm.sent("scan_bwd", handle)
    c_up, c_down, dlb_c, _, _ = _hgrn_bwd(pc, gam, None, cst_f, cst_b, cb_f, cb_b, ds0, ctx_len, False, "hgrn_bwd_ctx")
    dpc, dmc, small_c, _ = _inproj_bwd(c_up, c_down, None, ct, None, mcv, g_mix, w_inT, ctx_len, "inproj_bwd_ctx")
    gw_in_c, handle = _grad_matmul(dpc, hc, "gw_in_ctx", sender=comm.sender("ctx_mod", [(dmc[0], "gather")]))
    comm.sent("ctx_mod", handle)
    gw_inT, _ = _grad_in(d_up, d_down, dpt, h, gw_in_c)
    grad_x, dmod_in, small_in, handle = _inproj_bwd(d_up, d_down, dpt, xt, dx1, modv, g_mix, w_inT, seq,
                                                   "inproj_bwd_lat", sender=comm.sender("inproj_bwd", scatter(gw_inT)))
    comm.sent("inproj_bwd", handle)
    dmod = dmod_in + dmod_tail + dmod_ffn
    small_late = (at_row(0, small_in[0:1] + small_c[0:1])
                  + at_row(7, (dlb + dlb_c).reshape(1, 2 * KW))
                  + at_row(8, dmc[0, 0:N_MOD])
                  + at_row(15, comm.c_ctx_part(gw_inT))
                  + at_row(16, dmod[:, 0:N_MOD].reshape(nb_ex * N_MOD, D)))
    comm.sender("last", [(small_late, "gather")])
    return grad_x.reshape(x.shape)


def kernel(x, c, ctx, c_ctx, w_mod, b_mod, g_mix, g_ffn, w_in, lb_gamma, g_norm_a, ln_v_g, ln_v_b, w_s, b_s, w_pa, w_pb, w_o, w_up, w_down, g_final, loss_target, m_c_ctx, m_w_mod, m_b_mod, m_g_mix, m_g_ffn, m_w_in, m_lb_gamma, m_g_norm_a, m_ln_v_g, m_ln_v_b, m_w_s, m_b_s, m_w_pa, m_w_pb, m_w_o, m_w_up, m_w_down, m_g_final, v_c_ctx, v_w_mod, v_b_mod, v_g_mix, v_g_ffn, v_w_in, v_lb_gamma, v_g_norm_a, v_ln_v_g, v_ln_v_b, v_w_s, v_b_s, v_w_pa, v_w_pb, v_w_o, v_w_up, v_w_down, v_g_final):
    nb_ex = x.shape[0]
    me = 4 * lax.axis_index("x") + 2 * lax.axis_index("y") + lax.axis_index("c")
    cd = MXU_DTYPE
    mod_cols = w_mod.shape[2]
    lb_cols = lb_gamma.shape[2]

    w_inT_l = w_in[0].T.astype(cd)
    w_upT_l = w_up[0].T.astype(cd)
    w_paT_l = w_pa[0].T.astype(cd)
    w_pbT_l = w_pb[0].T.astype(cd)
    cl = jnp.concatenate([c, jnp.pad(lb_gamma.reshape(1, 4 * lb_cols), ((0, 0), (0, D - 4 * lb_cols))),
                          jnp.zeros((8 - nb_ex - 1, D), F32)], axis=0)
    g_in, g_cl = _gather_two_level([w_inT_l, cl], "gather_w_in")
    w_inT = g_in.reshape(IN_COLS, D)
    c_all = g_cl[:, 0:nb_ex].reshape(N_DEV * nb_ex, D)
    gam = jnp.transpose(g_cl[:, nb_ex, 0:4 * lb_cols].reshape(N_DEV, 4, lb_cols), (1, 0, 2)).reshape(4, KW)

    n_c = N_DEV * nb_ex
    cvec = jnp.concatenate([c_all, c_ctx.reshape(1, D), jnp.zeros((7, D), F32)], axis=0)
    b_mod_l = lax.dynamic_slice(b_mod, (0, me * mod_cols), (1, mod_cols))
    mod_l, svec = _mod_fwd(cvec, w_mod[0], b_mod_l)
    (g_mod,) = _gather_two_level([mod_l], "gather_mod")
    mod_all = jnp.transpose(g_mod, (1, 0, 2)).reshape(n_c + 8, N_MOD * D)
    modv = lax.dynamic_slice(mod_all, (me * nb_ex, 0), (nb_ex, N_MOD * D)).reshape(nb_ex, N_MOD, D)
    mcv = mod_all[n_c].reshape(1, N_MOD, D)

    handles, leftover = {}, {}

    class Comm:
        def sender(self, stage, items=None):
            if stage == "inproj":
                return _Sender([(w_down[0].astype(cd), "gather"), (w_o[0].astype(cd), "gather"), (w_paT_l, "gather"),
                                (w_pbT_l, "gather")])
            if stage == "scan":
                return _Sender([(w_upT_l, "gather")])
            if stage == "last":
                leftover["items"] = items
                return None
            return _Sender(items)

        def sent(self, stage, handle):
            handles[stage] = handle

        def mixer_weights(self, after):
            g_down, g_o, g_pa, g_pb = _exchange_wait(handles["inproj"], after)
            return g_down.reshape(D_FF, D), g_o.reshape(D, D), g_pa.reshape(D, KW), g_pb.reshape(D, KW)

        def ffn_weights(self, after):
            (g_up,) = _exchange_wait(handles["scan"], after)
            return g_up.reshape(2 * D_FF, D)

        def c_ctx_part(self, after):
            (r_dmc,) = _exchange_wait(handles["ctx_mod"], after)
            dmc_tot = _sum8(r_dmc, "sum_ctx_mod")
            dmc_l = lax.dynamic_slice(dmc_tot[0:N_MOD].reshape(1, N_MOD * D), (0, me * mod_cols), (1, mod_cols))
            return _mod_bwd_ctx(c_ctx.reshape(1, D), jnp.pad(dmc_l, ((0, 7), (0, 0))), w_mod[0])[0:1]

    grad_x = _local_step(
        x, ctx, loss_target, modv, mcv, gam, g_mix, g_ffn, g_norm_a, ln_v_g, ln_v_b, w_s[0], b_s[0],
        g_final.reshape(1, D), w_inT, Comm())
    last, last_started = _exchange_start(leftover["items"], "gather_small_late", after=leftover["items"][0][0])

    (r_up,) = _exchange_wait(handles["tail_bwd"], last_started)
    r_down, r_o, r_pa, r_pb, r_small, r_dws = _exchange_wait(handles["scan_bwd"], r_up)
    raw_up = _adamw_sum8(r_up, w_up[0].T, m_w_up[0].T, v_w_up[0].T, "adamw_w_up")
    raw_down = _adamw_sum8(r_down, w_down[0], m_w_down[0], v_w_down[0], "adamw_w_down")
    raw_o = _adamw_sum8(r_o, w_o[0], m_w_o[0], v_w_o[0], "adamw_w_o")
    (r_in,) = _exchange_wait(handles["inproj_bwd"], raw_o[1])
    raw_in = _adamw_sum8(r_in, w_in[0].T, m_w_in[0].T, v_w_in[0].T, "adamw_w_in")
    (r_late,) = _exchange_wait(last, raw_in[1])
    done = {"w_in": [a.T[None] for a in raw_in], "w_up": [a.T[None] for a in raw_up],
            "w_down": [a[None] for a in raw_down], "w_o": [a[None] for a in raw_o]}
    grad_w_in, grad_w_up, grad_w_down, grad_w_o = (done[k][0] for k in ("w_in", "w_up", "w_down", "w_o"))
    grad_w_pa = _sum8(r_pa, "sum_w_pa").T[None]
    grad_w_pb = _sum8(r_pb, "sum_w_pb").T[None]
    as_2d = {"c_ctx": (1, D), "g_final": (1, D), "b_s": (1, GROUPS * SGU_BLOCK), "w_s": (GROUPS * SGU_BLOCK, SGU_BLOCK)}
    small_params = {"g_mix": (g_mix, m_g_mix, v_g_mix), "g_ffn": (g_ffn, m_g_ffn, v_g_ffn),
                    "g_final": (g_final, m_g_final, v_g_final), "g_norm_a": (g_norm_a, m_g_norm_a, v_g_norm_a),
                    "ln_v_g": (ln_v_g, m_ln_v_g, v_ln_v_g), "ln_v_b": (ln_v_b, m_ln_v_b, v_ln_v_b),
                    "b_s": (b_s, m_b_s, v_b_s), "c_ctx": (c_ctx, m_c_ctx, v_c_ctx), "b_mod": (b_mod, m_b_mod, v_b_mod), "w_s": (w_s, m_w_s, v_w_s)}
    tot, dgam, small_done = _small_finish(
        r_small, r_late, r_dws, gam, nb_ex,
        {n: tuple(a.reshape(as_2d.get(n, a.shape)) for a in wmv) for n, wmv in small_params.items()})
    for n, outs in small_done.items():
        done[n] = [a.reshape(small_params[n][0].shape) for a in outs]
    loss = tot[14, 0]
    grad_g_mix, grad_g_ffn, grad_g_final, grad_g_norm_a, grad_ln_v_g, grad_ln_v_b, grad_b_s, grad_b_mod, grad_w_s = (
        done[n][0] for n in ("g_mix", "g_ffn", "g_final", "g_norm_a", "ln_v_g", "ln_v_b", "b_s", "b_mod", "w_s"))
    grad_lb_gamma = lax.dynamic_slice(dgam.reshape(2, 2, KW), (0, 0, me * lb_cols), (2, 2, lb_cols))

    dmod_all = r_late[:, 16:16 + nb_ex * N_MOD].reshape(n_c, N_MOD * D)
    dmod_l = jnp.concatenate([lax.dynamic_slice(dmod_all, (0, me * mod_cols), (n_c, mod_cols)),
                              lax.dynamic_slice(tot[8:8 + N_MOD].reshape(1, N_MOD * D), (0, me * mod_cols), (1, mod_cols)),
                              jnp.zeros((7, mod_cols), F32)], axis=0)
    grad_w_mod = _mod_bwd(svec, dmod_l)[None]
    grad_c_ctx = done["c_ctx"][0]

    names = ["c_ctx", "w_mod", "b_mod", "g_mix", "g_ffn", "w_in", "lb_gamma", "g_norm_a", "ln_v_g", "ln_v_b", "w_s",
             "b_s", "w_pa", "w_pb", "w_o", "w_up", "w_down", "g_final"]
    weights = [c_ctx, w_mod, b_mod, g_mix, g_ffn, w_in, lb_gamma, g_norm_a, ln_v_g, ln_v_b, w_s, b_s, w_pa, w_pb, w_o,
               w_up, w_down, g_final]
    grads = [grad_c_ctx, grad_w_mod, grad_b_mod, grad_g_mix, grad_g_ffn, grad_w_in, grad_lb_gamma, grad_g_norm_a,
             grad_ln_v_g, grad_ln_v_b, grad_w_s, grad_b_s, grad_w_pa, grad_w_pb, grad_w_o, grad_w_up, grad_w_down,
             grad_g_final]
    ms = [m_c_ctx, m_w_mod, m_b_mod, m_g_mix, m_g_ffn, m_w_in, m_lb_gamma, m_g_norm_a, m_ln_v_g, m_ln_v_b, m_w_s, m_b_s,
          m_w_pa, m_w_pb, m_w_o, m_w_up, m_w_down, m_g_final]
    vs = [v_c_ctx, v_w_mod, v_b_mod, v_g_mix, v_g_ffn, v_w_in, v_lb_gamma, v_g_norm_a, v_ln_v_g, v_ln_v_b, v_w_s, v_b_s,
          v_w_pa, v_w_pb, v_w_o, v_w_up, v_w_down, v_g_final]
    deltas, new_ms, new_vs = [], [], []
    for nm, w, g, m, v in zip(names, weights, grads, ms, vs):
        d, nm_, nv_ = done[nm][1:] if nm in done else _adamw(w, g.reshape(w.shape), m, v, "adamw_" + nm)
        deltas.append(d)
        new_ms.append(nm_)
        new_vs.append(nv_)
    grads = [g.reshape(w.shape) for g, w in zip(grads, weights)]
    return (loss, grad_x, *grads, *deltas, *new_ms, *new_vs)
```

```python
import functools

import jax
import jax.numpy as jnp
from jax import lax
from jax.experimental import pallas as pl
from jax.experimental.pallas import tpu as pltpu

F32 = jnp.float32
MXU_DTYPE = jnp.bfloat16
PAYLOAD_DTYPE = jnp.bfloat16

N_DEV = 8
D = 1024
HEADS = 4
DK = 128
KW = HEADS * DK
CHUNK = 64
SGU_BLOCK = 128
GROUPS = 4
D_FF = 2816
FF_CHUNK = 256
N_MOD = 6
IN_COLS = 5632
CTX_COLS = 1536
TAIL_COLS = IN_COLS - 4 * KW
EPS = 1e-6
ADAM_LR, ADAM_B1, ADAM_B2, ADAM_EPS, ADAM_WD, ADAM_STEP = 0.001, 0.9, 0.999, 1e-08, 0.01, 10

VMEM_LIMIT = 56 * 1024 * 1024
TOKEN_TILE = 256
PROJ_TILE = 512
TAIL_TILE = 512
SMALL_ROWS = 40


def _params(sem):
    return pltpu.CompilerParams(dimension_semantics=sem, vmem_limit_bytes=VMEM_LIMIT)


_DN = {"nn": (((1,), (0,)), ((), ())), "nt": (((1,), (1,)), ((), ())), "tn": (((0,), (0,)), ((), ()))}


def _dot(a, b, form="nn"):
    return lax.dot_general(a.astype(MXU_DTYPE), b.astype(MXU_DTYPE), _DN[form], preferred_element_type=F32)


def _mask_dot(mask, v):
    bf = jnp.bfloat16
    hi = v.astype(bf)
    mid = (v - hi.astype(F32)).astype(bf)
    w = v.shape[1]
    s = lax.dot_general(mask.astype(bf), jnp.concatenate([hi, mid], axis=1), _DN["nn"], preferred_element_type=F32)
    return s[:, w:] + s[:, :w]


def _full(shape, single=False):
    n = len(shape)
    if single:
        return pl.BlockSpec(shape, lambda *_: (0,) * n, pipeline_mode=pl.Buffered(1))
    return pl.BlockSpec(shape, lambda *_: (0,) * n)


def _ordered_behind(body, in_specs, args, after):
    if after is None:
        return body
    at = len(in_specs)
    in_specs.append(pl.BlockSpec(memory_space=pl.ANY))
    args.append(after)
    return lambda *refs: body(*refs[:at], *refs[at + 1:])


def _sigmoid(z):
    return 0.5 * jnp.tanh(0.5 * z) + 0.5


def _gelu(x):
    c = 0.7978845608028654
    t = jnp.tanh(c * (x + 0.044715 * x * x * x))
    return 0.5 * x * (1.0 + t), t


def _gelu_grad(x, t):
    c = 0.7978845608028654
    return 0.5 * (1.0 + t) + 0.5 * x * (1.0 - t * t) * c * (1.0 + 3 * 0.044715 * x * x)


def _exchange(items, name, after=None):
    n = len(items)
    out_shape = []
    for a, mode in items:
        blk = a.shape if mode == "gather" else a.shape[1:]
        out_shape.append(jax.ShapeDtypeStruct((N_DEV,) + tuple(blk), a.dtype))

    def body(*refs):
        srcs, dsts = refs[:n], refs[n:2 * n]
        send_sems, recv_sems, local_sems = refs[2 * n:]
        x, y, c = lax.axis_index("x"), lax.axis_index("y"), lax.axis_index("c")
        me = 4 * x + 2 * y + c

        def src_for(i, dev):
            return srcs[i] if items[i][1] == "gather" else srcs[i].at[dev]

        local = [pltpu.make_async_copy(src_for(i, me), dsts[i].at[me], local_sems.at[i]) for i in range(n)]
        for cp in local:
            cp.start()
        remote = []
        for k in range(1, N_DEV):
            px = jnp.bitwise_xor(x, (k >> 2) & 1)
            py = jnp.bitwise_xor(y, (k >> 1) & 1)
            pc = jnp.bitwise_xor(c, k & 1)
            peer = 4 * px + 2 * py + pc
            for i in range(n):
                cp = pltpu.make_async_remote_copy(
                    src_ref=src_for(i, peer), dst_ref=dsts[i].at[me],
                    send_sem=send_sems.at[i * (N_DEV - 1) + k - 1], recv_sem=recv_sems.at[i * (N_DEV - 1) + k - 1],
                    device_id=(px, py, pc), device_id_type=pl.DeviceIdType.MESH)
                cp.start()
                remote.append(cp)
        for cp in remote:
            cp.wait()
        for cp in local:
            cp.wait()

    any_spec = pl.BlockSpec(memory_space=pl.ANY)
    in_specs, args = [any_spec] * n, [a for a, _ in items]
    if after is not None:
        in_specs.append(any_spec)
        args.append(after)
        exchange = body
        body = lambda *refs: exchange(*refs[:n], *refs[n + 1:])
    return pl.pallas_call(
        body, name=name, out_shape=out_shape, in_specs=in_specs, out_specs=[any_spec] * n,
        scratch_shapes=[pltpu.SemaphoreType.DMA((n * (N_DEV - 1),)), pltpu.SemaphoreType.DMA((n * (N_DEV - 1),)),
                        pltpu.SemaphoreType.DMA((n,))],
    )(*args)


def _gather_two_level(arrays, name):
    n = len(arrays)
    pieces = []
    for i, a in enumerate(arrays):
        rows = _Sender.PIECE_ROWS if a.shape[0] % _Sender.PIECE_ROWS == 0 else a.shape[0]
        pieces += [(i, r0, rows) for r0 in range(0, a.shape[0], rows)]

    def body(*refs):
        srcs, dsts = refs[:n], refs[n:2 * n]
        send_sems, recv_sems, local_sems = refs[2 * n:]
        x, y, c = lax.axis_index("x"), lax.axis_index("y"), lax.axis_index("c")
        me, sibling = (x, y, c), (x, y, 1 - c)
        x_nbr, y_nbr, diag = (1 - x, y, c), (x, 1 - y, c), (1 - x, 1 - y, c)

        def slot(px, py, pc):
            return 4 * px + 2 * py + pc

        def copy(u, k, block, to, own=False):
            i, r0, rows = pieces[u]
            there = dsts[i].at[slot(*block)].at[pl.ds(r0, rows)]
            return pltpu.make_async_remote_copy(
                src_ref=srcs[i].at[pl.ds(r0, rows)] if own else there, dst_ref=there,
                send_sem=send_sems.at[u * 7 + k], recv_sem=recv_sems.at[u * 7 + k],
                device_id=to, device_id_type=pl.DeviceIdType.MESH)

        units = range(len(pieces))
        mine = [pltpu.make_async_copy(srcs[i], dsts[i].at[slot(*me)], local_sems.at[i]) for i in range(n)]
        for cp in mine:
            cp.start()
        for u in units:
            copy(u, 1, me, x_nbr, own=True).start()
            copy(u, 2, me, y_nbr, own=True).start()
        for u in units:
            copy(u, 0, me, sibling, own=True).start()

        def relay_then_pass(k_from, frm, to, k_other, other):
            for u in units:
                copy(u, k_from, frm, me).wait_recv()
                copy(u, 3, frm, to).start()
                copy(u, 3 + k_from, frm, sibling).start()
            for u in units:
                copy(u, k_other, other, me).wait_recv()
                copy(u, 3 + k_other, other, sibling).start()

        @pl.when(c == 1)
        def _():
            relay_then_pass(1, x_nbr, y_nbr, 2, y_nbr)

        @pl.when(c == 0)
        def _():
            relay_then_pass(2, y_nbr, x_nbr, 1, x_nbr)

        for u in units:
            copy(u, 3, diag, me).wait_recv()
            copy(u, 6, diag, sibling).start()
        for u in units:
            copy(u, 0, sibling, me).wait_recv()
            for k, chip in ((4, x_nbr), (5, y_nbr), (6, diag)):
                copy(u, k, (chip[0], chip[1], 1 - c), me).wait_recv()
        for u in units:
            for k in range(7):
                copy(u, k, me, me, own=True).wait_send()
        for cp in mine:
            cp.wait()

    any_spec = pl.BlockSpec(memory_space=pl.ANY)
    return pl.pallas_call(
        body, name=name, out_shape=[jax.ShapeDtypeStruct((N_DEV,) + a.shape, a.dtype) for a in arrays],
        in_specs=[any_spec] * n, out_specs=[any_spec] * n,
        scratch_shapes=[pltpu.SemaphoreType.DMA((len(pieces) * 7,)), pltpu.SemaphoreType.DMA((len(pieces) * 7,)),
                        pltpu.SemaphoreType.DMA((n,))],
    )(*arrays)


_HBM = pl.BlockSpec(memory_space=pltpu.HBM)
_SEM = pl.BlockSpec(memory_space=pltpu.SEMAPHORE)
_EFFECT = pltpu.SideEffectType.DATAFLOW_SIDE_EFFECTING


def _split_copies(items, srcs, lands, send_sems, recv_sems):
    x, y, c = lax.axis_index("x"), lax.axis_index("y"), lax.axis_index("c")
    me = 4 * x + 2 * y + c
    copies = []
    for k in range(1, N_DEV):
        px = jnp.bitwise_xor(x, (k >> 2) & 1)
        py = jnp.bitwise_xor(y, (k >> 1) & 1)
        pc = jnp.bitwise_xor(c, k & 1)
        peer = 4 * px + 2 * py + pc
        for i in range(len(items)):
            src = srcs[i] if items[i][1] == "gather" else srcs[i].at[peer]
            copies.append(pltpu.make_async_remote_copy(
                src_ref=src, dst_ref=lands[i].at[me],
                send_sem=send_sems.at[i * (N_DEV - 1) + k - 1], recv_sem=recv_sems.at[i * (N_DEV - 1) + k - 1],
                device_id=(px, py, pc), device_id_type=pl.DeviceIdType.MESH))
    return me, copies


def _exchange_start(items, name, after):
    n = len(items)
    n_sem = n * (N_DEV - 1)
    srcs, lands = [], []
    for a, mode in items:
        blk = a.shape if mode == "gather" else a.shape[1:]
        srcs.append(pltpu.with_memory_space_constraint(a, pltpu.HBM))
        lands.append(pltpu.with_memory_space_constraint(lax.empty((N_DEV,) + tuple(blk), a.dtype), pltpu.HBM))

    def body(*refs):
        src_refs, land_refs = refs[:n], refs[n:2 * n]
        send_sems, recv_sems = refs[2 * n + 1], refs[2 * n + 2]
        local_sems = refs[4 * n + 3]
        me, copies = _split_copies(items, src_refs, land_refs, send_sems, recv_sems)
        for i in range(n):
            own = src_refs[i] if items[i][1] == "gather" else src_refs[i].at[me]
            cp = pltpu.make_async_copy(own, land_refs[i].at[me], local_sems.at[i])
            cp.start()
            cp.wait()
        for cp in copies:
            cp.start()

    out_shape = [pltpu.SemaphoreType.DMA((n_sem,)), pltpu.SemaphoreType.DMA((n_sem,))]
    out_shape += [pltpu.HBM(a.shape, a.dtype) for a in srcs] + [pltpu.HBM(a.shape, a.dtype) for a in lands]
    outs = pl.pallas_call(
        body, name=name, out_shape=out_shape,
        in_specs=[_HBM] * (2 * n) + [pl.BlockSpec(memory_space=pl.ANY)],
        out_specs=[_SEM, _SEM] + [_HBM] * (2 * n),
        input_output_aliases={i: 2 + i for i in range(2 * n)},
        scratch_shapes=[pltpu.SemaphoreType.DMA((n,))],
        compiler_params=pltpu.CompilerParams(has_side_effects=_EFFECT),
    )(*srcs, *lands, after)
    handle = (items, name, outs[0], outs[1], outs[2:2 + n], outs[2 + n:2 + 2 * n])
    return handle, outs[2]


class _Sender:
    PIECE_ROWS = 352

    def __init__(self, items, chunks=None):
        self.items, self.n = items, len(items)
        self.chunks = chunks
        if chunks is None:
            block_rows = [a.shape[0] if mode == "gather" else a.shape[1] for a, mode in items]
            self.chunks = [r // self.PIECE_ROWS if r % self.PIECE_ROWS == 0 else 1 for r in block_rows]
        self.srcs, self.lands = [], []
        for a, mode in items:
            blk = a.shape if mode == "gather" else a.shape[1:]
            self.srcs.append(pltpu.with_memory_space_constraint(a, pltpu.HBM))
            self.lands.append(pltpu.with_memory_space_constraint(lax.empty((N_DEV,) + tuple(blk), a.dtype), pltpu.HBM))

    def issue(self, src_refs, land_refs, send_sems, recv_sems, local_sems, step, n_steps):
        x, y, c = lax.axis_index("x"), lax.axis_index("y"), lax.axis_index("c")
        me = 4 * x + 2 * y + c
        copies = []
        for ch in range(max(self.chunks)):
            for k in range(1, N_DEV):
                px = jnp.bitwise_xor(x, (k >> 2) & 1)
                py = jnp.bitwise_xor(y, (k >> 1) & 1)
                pc = jnp.bitwise_xor(c, k & 1)
                peer = 4 * px + 2 * py + pc
                for i, (_, mode) in enumerate(self.items):
                    if ch >= self.chunks[i]:
                        continue
                    n_rows = land_refs[i].shape[1] // self.chunks[i]
                    rows = pl.ds(ch * n_rows, n_rows)
                    src = src_refs[i].at[rows] if mode == "gather" else src_refs[i].at[peer].at[rows]
                    copies.append(pltpu.make_async_remote_copy(
                        src_ref=src, dst_ref=land_refs[i].at[me].at[rows],
                        send_sem=send_sems.at[i * (N_DEV - 1) + k - 1], recv_sem=recv_sems.at[i * (N_DEV - 1) + k - 1],
                        device_id=(px, py, pc), device_id_type=pl.DeviceIdType.MESH))
        own = [pltpu.make_async_copy(src_refs[i] if mode == "gather" else src_refs[i].at[me], land_refs[i].at[me],
                                     local_sems.at[i]) for i, (_, mode) in enumerate(self.items)]

        @pl.when(step == 0)
        def _():
            for cp in own:
                cp.start()

        for s in range(n_steps):
            group = [cp for j, cp in enumerate(copies) if (j * n_steps) // len(copies) == s]
            if group:
                @pl.when(step == s)
                def _(group=group):
                    for cp in group:
                        cp.start()

        @pl.when(step == n_steps - 1)
        def _():
            for cp in own:
                cp.wait()


def _host_call(body, name, grid, in_specs, args, out_shape, out_specs, scratch_shapes, after=None, sender=None):
    in_specs, args, out_shape, out_specs = list(in_specs), list(args), list(out_shape), list(out_specs)
    scratch_shapes = list(scratch_shapes)
    semantics = ("arbitrary",) * len(grid)
    body = _ordered_behind(body, in_specs, args, after)
    if sender is None:
        res = pl.pallas_call(body, name=name, grid=grid, in_specs=in_specs, out_specs=out_specs, out_shape=out_shape,
                             scratch_shapes=scratch_shapes, compiler_params=_params(semantics))(*args)
        return res, None
    n, n_in, n_out, n_scr = sender.n, len(in_specs), len(out_shape), len(scratch_shapes)
    n_sem = n * (N_DEV - 1)
    n_steps = 1
    for g in grid:
        n_steps *= g
    compute = body

    def body(*refs):
        ins, s_in = refs[:n_in], refs[n_in:n_in + 2 * n]
        o0 = n_in + 2 * n
        outs, s_out = refs[o0:o0 + n_out], refs[o0 + n_out:o0 + n_out + 2 + 2 * n]
        scr = refs[o0 + n_out + 2 + 2 * n:]
        compute(*ins, *outs, *scr[:n_scr])
        step = pl.program_id(0)
        for d in range(1, len(grid)):
            step = step * grid[d] + pl.program_id(d)
        sender.issue(s_in[:n], s_in[n:], s_out[0], s_out[1], scr[n_scr], step, n_steps)

    res = pl.pallas_call(
        body, name=name, grid=grid,
        in_specs=in_specs + [_HBM] * (2 * n), out_specs=out_specs + [_SEM, _SEM] + [_HBM] * (2 * n),
        out_shape=out_shape + [pltpu.SemaphoreType.DMA((n_sem,)), pltpu.SemaphoreType.DMA((n_sem,))]
        + [pltpu.HBM(a.shape, a.dtype) for a in sender.srcs] + [pltpu.HBM(a.shape, a.dtype) for a in sender.lands],
        input_output_aliases={n_in + j: n_out + 2 + j for j in range(2 * n)},
        scratch_shapes=scratch_shapes + [pltpu.SemaphoreType.DMA((n,))],
        compiler_params=pltpu.CompilerParams(dimension_semantics=semantics, vmem_limit_bytes=VMEM_LIMIT,
                                             has_side_effects=_EFFECT),
    )(*args, *sender.srcs, *sender.lands)
    handle = (sender.items, name, res[n_out], res[n_out + 1], res[n_out + 2:n_out + 2 + n],
              res[n_out + 2 + n:n_out + 2 + 2 * n])
    return res[:n_out], handle


def _exchange_wait(handle, after):
    items, name, send_sems, recv_sems, srcs, lands = handle
    n = len(items)

    def body(*refs):
        src_refs, land_refs = refs[:n], refs[n:2 * n]
        send_ref, recv_ref = refs[2 * n], refs[2 * n + 1]
        _, copies = _split_copies(items, src_refs, land_refs, send_ref, recv_ref)
        for cp in copies:
            cp.wait_send()
            cp.wait_recv()

    outs = pl.pallas_call(
        body, name=name + "_wait",
        out_shape=[pltpu.HBM(a.shape, a.dtype) for a in srcs] + [pltpu.HBM(a.shape, a.dtype) for a in lands],
        in_specs=[_HBM] * (2 * n) + [_SEM, _SEM, pl.BlockSpec(memory_space=pl.ANY)], out_specs=[_HBM] * (2 * n),
        input_output_aliases={i: i for i in range(2 * n)},
        compiler_params=pltpu.CompilerParams(has_side_effects=_EFFECT),
    )(*srcs, *lands, send_sems, recv_sems, after)
    return outs[n:]


def _mod_fwd(cvec, w_mod_l, b_mod_l):
    rows, cols = cvec.shape[0], w_mod_l.shape[1]

    def body(c_ref, w_ref, b_ref, o_ref, s_ref):
        cv = c_ref[...]
        s = cv * _sigmoid(cv)
        s_ref[...] = s
        o_ref[...] = _dot(s, w_ref[...]) + b_ref[...]

    return pl.pallas_call(
        body, name="mod_fwd",
        out_shape=(jax.ShapeDtypeStruct((rows, cols), F32), jax.ShapeDtypeStruct((rows, D), F32)),
        in_specs=[_full((rows, D)), _full((D, cols)), _full((1, cols))],
        out_specs=(_full((rows, cols)), _full((rows, D))), grid=(1,),
        compiler_params=_params(("arbitrary",)),
    )(cvec, w_mod_l, b_mod_l)


def _mod_bwd(svec, dmod_l):
    rows, cols = dmod_l.shape

    def body(s_ref, d_ref, gw_ref):
        gw_ref[...] = _dot(s_ref[...], d_ref[...], "tn")

    return pl.pallas_call(
        body, name="mod_bwd", out_shape=jax.ShapeDtypeStruct((D, cols), F32),
        in_specs=[_full((rows, D)), _full((rows, cols))], out_specs=_full((D, cols)), grid=(1,),
        compiler_params=_params(("arbitrary",)),
    )(svec, dmod_l)


def _mod_bwd_ctx(c_row, d_l, w_mod_l):
    cols = d_l.shape[1]

    def body(c_ref, d_ref, w_ref, o_ref):
        cv = c_ref[...]
        sg = _sigmoid(cv)
        o_ref[...] = _dot(d_ref[...], w_ref[...], "nt") * (sg * (1.0 + cv * (1.0 - sg)))

    return pl.pallas_call(
        body, name="mod_bwd_ctx", out_shape=jax.ShapeDtypeStruct((8, D), F32),
        in_specs=[_full((1, D)), _full((8, cols)), _full((D, cols))], out_specs=_full((8, D)), grid=(1,),
        compiler_params=_params(("arbitrary",)),
    )(c_row, d_l, w_mod_l)


def _inproj(xt, modv, g, w_inT, n_cols, rows_per_example, name, after=None, sender=None):
    rows = xt.shape[0]
    tm = min(PROJ_TILE, rows_per_example)
    per_b = rows_per_example // tm
    shared_mod = modv.shape[0] == 1

    def body(x_ref, mod_ref, g_ref, w_ref, p_ref, h_ref):
        x = x_ref[...]
        r = lax.rsqrt(jnp.mean(x * x, axis=-1, keepdims=True) + EPS)
        h = (x * r * g_ref[...]) * (1.0 + mod_ref[0, 1:2, :]) + mod_ref[0, 0:1, :]
        hb = h.astype(MXU_DTYPE)
        h_ref[...] = hb
        for j in range(n_cols // KW):
            p_ref[:, j * KW:(j + 1) * KW] = _dot(hb, w_ref[j * KW:(j + 1) * KW, :], "nt").astype(p_ref.dtype)

    mod_idx = (lambda i: (0, 0, 0)) if shared_mod else (lambda i: (i // per_b, 0, 0))
    in_specs = [pl.BlockSpec((tm, D), lambda i: (i, 0)), pl.BlockSpec((1, N_MOD, D), mod_idx), _full((1, D)),
                pl.BlockSpec((n_cols, D), lambda i: (0, 0), pipeline_mode=pl.Buffered(1))]
    (p, h), handle = _host_call(
        body, name, (rows // tm,), in_specs, [xt, modv, g, w_inT],
        [jax.ShapeDtypeStruct((rows, n_cols), MXU_DTYPE), jax.ShapeDtypeStruct((rows, D), MXU_DTYPE)],
        [pl.BlockSpec((tm, n_cols), lambda i: (i, 0)), pl.BlockSpec((tm, D), lambda i: (i, 0))], [],
        after=after, sender=sender)
    return p, h, handle


def _tri(reverse, n):
    row = lax.broadcasted_iota(jnp.int32, (n, n), 0)
    col = lax.broadcasted_iota(jnp.int32, (n, n), 1)
    same = (row // CHUNK) == (col // CHUNK)
    return same & ((col >= row) if reverse else (col <= row))


def _per_chunk_rows(x, reverse):
    n = x.shape[0]
    rows = [x[j * CHUNK:j * CHUNK + 1] if reverse else x[(j + 1) * CHUNK - 1:(j + 1) * CHUNK] for j in range(n // CHUNK)]
    return jnp.concatenate([jnp.broadcast_to(r, (CHUNK, x.shape[1])) for r in rows], axis=0), rows


def _lower_bound(gam_ref, direction):
    return _sigmoid(gam_ref[direction:direction + 1, :] - gam_ref[2 + direction:3 + direction, :])


def _gate_prep(z, lb, tri, reverse, b=None):
    sg = _sigmoid(z)
    f = lb + (1.0 - lb) * sg
    g = jnp.log(f)
    b = _mask_dot(tri, g) if b is None else b
    bl, bl_rows = _per_chunk_rows(b, reverse)
    mid = 0.5 * bl
    return sg, g, 1.0 - f, b, jnp.exp(mid), [jnp.exp(0.5 * r) for r in bl_rows], jnp.exp(mid - b), mid


def _hgrn_fwd(p, gam, s0, rows_per_example, with_out, name, sender=None):
    rows = p.shape[0]
    nb_ex = rows // rows_per_example
    rb = min(TOKEN_TILE, rows_per_example)
    cpb = rb // CHUNK
    nb = rows_per_example // rb
    n_chunks = rows // CHUNK
    has_s0 = s0 is not None

    def body(*refs):
        it = iter(refs)
        gam_ref = next(it)
        zf_ref, vf_ref = next(it), next(it)
        qf_ref = next(it) if with_out else None
        zb_ref, vb_ref = next(it), next(it)
        qb_ref = next(it) if with_out else None
        s0_ref = next(it) if has_s0 else None
        if with_out:
            of_ref, ob_ref = next(it), next(it)
        stash_f, stash_b, bsum_f, bsum_b, fin_ref = next(it), next(it), next(it), next(it), next(it)
        st_ref = next(it)
        i = pl.program_id(1)

        @pl.when(i == 0)
        def _():
            if has_s0:
                st_ref[...] = s0_ref[:, 0]
            else:
                st_ref[...] = jnp.zeros_like(st_ref)

        for direction, (z_ref, v_ref, q_ref, stash, bsum_ref) in enumerate(
                ((zf_ref, vf_ref, qf_ref, stash_f, bsum_f), (zb_ref, vb_ref, qb_ref, stash_b, bsum_b))):
            reverse = direction == 1
            tri = _tri(reverse, rb)
            lb = _lower_bound(gam_ref, direction)
            z = z_ref[...].astype(F32)
            v = v_ref[...].astype(F32)
            _, _, k, b, em, em_rows, e2, mid = _gate_prep(z, lb, tri, reverse)
            bsum_ref[...] = b
            kd = (k * (e2 * em)).astype(MXU_DTYPE)
            vb = v.astype(MXU_DTYPE)
            if with_out:
                q = q_ref[...].astype(F32)
                qi = q * jnp.exp(b - mid)
                qe = (qi * em).astype(MXU_DTYPE)
                qi = qi.astype(MXU_DTYPE)
                ki = (k * e2).astype(MXU_DTYPE)
                intra = []
                for h in range(HEADS):
                    hs = slice(h * DK, (h + 1) * DK)
                    sc = jnp.where(tri, _dot(qi[:, hs], ki[:, hs], "nt"), 0.0)
                    intra.append(_dot(sc, vb[:, hs]))
            for j in (range(cpb - 1, -1, -1) if reverse else range(cpb)):
                rs = slice(j * CHUNK, (j + 1) * CHUNK)
                a = em_rows[j] * em_rows[j]
                for h in range(HEADS):
                    hs = slice(h * DK, (h + 1) * DK)
                    st = st_ref[direction, h]
                    stash[j, h] = st.astype(stash.dtype)
                    if with_out:
                        (ob_ref if reverse else of_ref)[rs, hs] = intra[h][rs] + _dot(qe[rs, hs], st, "nt")
                    st_ref[direction, h] = st * a[:, hs] + _dot(vb[rs, hs], kd[rs, hs], "tn")

        @pl.when(i == nb - 1)
        def _():
            fin_ref[:, 0] = st_ref[...]

    up = lambda b, i: b * nb + i
    down = lambda b, i: b * nb + nb - 1 - i
    col = lambda rowf, c: pl.BlockSpec((rb, KW), lambda b, i: (rowf(b, i), c))
    in_specs = [_full((4, KW)), col(up, 0), col(up, 2)] + ([col(up, 3)] if with_out else [])
    in_specs += [col(down, 1), col(down, 2)] + ([col(down, 3)] if with_out else [])
    args = [gam, p, p] + ([p] if with_out else []) + [p, p] + ([p] if with_out else [])
    if has_s0:
        in_specs.append(pl.BlockSpec((2, 1, HEADS, DK, DK), lambda b, i: (0, b, 0, 0, 0)))
        args.append(s0)
    out_shape, out_specs = [], []
    if with_out:
        out_shape += [jax.ShapeDtypeStruct((rows, KW), F32)] * 2
        out_specs += [pl.BlockSpec((rb, KW), lambda b, i: (up(b, i), 0)),
                      pl.BlockSpec((rb, KW), lambda b, i: (down(b, i), 0))]
    out_shape += [jax.ShapeDtypeStruct((n_chunks, HEADS, DK, DK), MXU_DTYPE)] * 2
    out_specs += [pl.BlockSpec((cpb, HEADS, DK, DK), lambda b, i: (up(b, i), 0, 0, 0)),
                  pl.BlockSpec((cpb, HEADS, DK, DK), lambda b, i: (down(b, i), 0, 0, 0))]
    out_shape += [jax.ShapeDtypeStruct((rows, KW), F32)] * 2
    out_specs += [pl.BlockSpec((rb, KW), lambda b, i: (up(b, i), 0)),
                  pl.BlockSpec((rb, KW), lambda b, i: (down(b, i), 0))]
    out_shape.append(jax.ShapeDtypeStruct((2, nb_ex, HEADS, DK, DK), F32))
    out_specs.append(pl.BlockSpec((2, 1, HEADS, DK, DK), lambda b, i: (0, b, 0, 0, 0)))
    res, handle = _host_call(body, name, (nb_ex, nb), in_specs, args, out_shape, out_specs,
                             [pltpu.VMEM((2, HEADS, DK, DK), F32)], sender=sender)
    return (*res, handle)


def _hgrn_bwd(p, gam, do, stash_f, stash_b, bsum_f, bsum_b, ds_end, rows_per_example, with_out, name, after=None,
              sender=None):
    rows = p.shape[0]
    nb_ex = rows // rows_per_example
    rb = min(TOKEN_TILE, rows_per_example)
    cpb = rb // CHUNK
    nb = rows_per_example // rb
    has_end = ds_end is not None

    def body(*refs):
        it = iter(refs)
        gam_ref = next(it)
        ins = []
        for _ in range(2):
            z_ref, v_ref = next(it), next(it)
            q_ref = next(it) if with_out else None
            do_ref = next(it) if with_out else None
            ins.append((z_ref, v_ref, q_ref, do_ref, next(it), next(it)))
        end_ref = next(it) if has_end else None
        outs = [next(it), next(it)]
        dlb_ref, ds0_ref = next(it), next(it)
        dst_ref = next(it)
        b_id, i = pl.program_id(0), pl.program_id(1)

        @pl.when(i == 0)
        def _():
            if has_end:
                dst_ref[...] = end_ref[:, 0]
            else:
                dst_ref[...] = jnp.zeros_like(dst_ref)

        @pl.when((i == 0) & (b_id == 0))
        def _():
            dlb_ref[...] = jnp.zeros_like(dlb_ref)

        for direction in range(2):
            z_ref, v_ref, q_ref, do_ref, stash, b_ref = ins[direction]
            dgrp_ref = outs[direction]
            reverse = direction == 1
            tri = _tri(reverse, rb)
            tri_t = _tri(not reverse, rb)
            lb = _lower_bound(gam_ref, direction)
            heads = [slice(h * DK, (h + 1) * DK) for h in range(HEADS)]
            chunks = [slice(j * CHUNK, (j + 1) * CHUNK) for j in range(cpb)]
            grid_cat = lambda parts: jnp.concatenate([jnp.concatenate(row, axis=1) for row in parts], axis=0)
            cat = lambda parts: jnp.concatenate(parts, axis=1)
            z = z_ref[...].astype(F32)
            sg, g, k, b, em, em_rows, e2, mid = _gate_prep(z, lb, tri, reverse, b=b_ref[...])
            e3 = e2 * em
            kd = k * e3
            kd_b = kd.astype(MXU_DTYPE)
            vb = v_ref[...].astype(MXU_DTYPE)
            if with_out:
                q = q_ref[...].astype(F32)
                dout = do_ref[...].astype(MXU_DTYPE)
                e1 = jnp.exp(b - mid)
                e4 = e1 * em
                qi, ki, qe = q * e1, k * e2, q * e4
                qi_b, ki_b, qe_b = qi.astype(MXU_DTYPE), ki.astype(MXU_DTYPE), qe.astype(MXU_DTYPE)
                dqi_p, dki_p, dv_p = [], [], []
                for hs in heads:
                    sc = jnp.where(tri, _dot(qi_b[:, hs], ki_b[:, hs], "nt"), 0.0)
                    dsc = jnp.where(tri, _dot(dout[:, hs], vb[:, hs], "nt"), 0.0)
                    dqi_p.append(_dot(dsc, ki_b[:, hs]))
                    dki_p.append(_dot(dsc, qi_b[:, hs], "tn"))
                    dv_p.append(_dot(sc, dout[:, hs], "tn"))
                dqi, dki, dv = cat(dqi_p), cat(dki_p), cat(dv_p)
                dqe = grid_cat([[_dot(dout[rs, hs], stash[j, h]) for h, hs in enumerate(heads)]
                                for j, rs in enumerate(chunks)])
                grow = [[_dot(dout[rs, hs], qe_b[rs, hs], "tn") for hs in heads] for rs in chunks]
            dkd_p = [[None] * HEADS for _ in range(cpb)]
            dvs_p = [[None] * HEADS for _ in range(cpb)]
            da_p = [[None] * HEADS for _ in range(cpb)]
            for j in (range(cpb) if reverse else range(cpb - 1, -1, -1)):
                rs = chunks[j]
                a = em_rows[j] * em_rows[j]
                for h, hs in enumerate(heads):
                    dst = dst_ref[direction, h]
                    dkd_p[j][h] = _dot(vb[rs, hs], dst)
                    dvs_p[j][h] = _dot(kd_b[rs, hs], dst, "nt")
                    da_p[j][h] = jnp.broadcast_to(
                        jnp.sum(dst * stash[j, h].astype(F32), axis=0, keepdims=True), (CHUNK, DK))
                    new_dst = dst * a[:, hs]
                    dst_ref[direction, h] = new_dst + grow[j][h] if with_out else new_dst
            dkd, dvs, da = grid_cat(dkd_p), grid_cat(dvs_p), grid_cat(da_p)
            t_kd = dkd * kd
            dk = dkd * e3
            db = -t_kd
            tot = t_kd
            if with_out:
                dgrp_ref[:, KW:2 * KW] = (dvs + dv).astype(dgrp_ref.dtype)
                dgrp_ref[:, 2 * KW:] = (dqi * e1 + dqe * e4).astype(dgrp_ref.dtype)
                dk = dk + dki * e2
                t_qi, t_ki, t_qe = dqi * qi, dki * ki, dqe * qe
                db = db + t_qi - t_ki + t_qe
                tot = tot + 0.5 * (t_ki - t_qi)
            else:
                dgrp_ref[:, KW:2 * KW] = dvs.astype(dgrp_ref.dtype)
            dbl = jnp.concatenate([jnp.broadcast_to(jnp.sum(tot[rs], axis=0, keepdims=True), (CHUNK, KW))
                                   for rs in chunks], axis=0) + da * (em * em)
            dg = _mask_dot(tri_t, db) + dbl
            df = dg * jnp.exp(-g) - dk
            dgrp_ref[:, 0:KW] = (df * (1.0 - lb) * sg * (1.0 - sg)).astype(dgrp_ref.dtype)
            dlb_ref[direction:direction + 1, :] += jnp.sum(df * (1.0 - sg), axis=0, keepdims=True)

        @pl.when(i == nb - 1)
        def _():
            ds0_ref[:, 0] = dst_ref[...]

    rows_of = (lambda b, i: b * nb + nb - 1 - i, lambda b, i: b * nb + i)
    in_specs, args = [_full((4, KW))], [gam]
    for direction in range(2):
        rf = rows_of[direction]
        col = lambda c, rf=rf: pl.BlockSpec((rb, KW), lambda b, i: (rf(b, i), c))
        in_specs += [col(direction), col(2)]
        args += [p, p]
        if with_out:
            in_specs += [col(3), col(0)]
            args += [p, do]
        in_specs += [pl.BlockSpec((cpb, HEADS, DK, DK), lambda b, i, rf=rf: (rf(b, i), 0, 0, 0)), col(0)]
        args += [(stash_f, stash_b)[direction], (bsum_f, bsum_b)[direction]]
    if has_end:
        in_specs.append(pl.BlockSpec((2, 1, HEADS, DK, DK), lambda b, i: (0, b, 0, 0, 0)))
        args.append(ds_end)
    out_shape, out_specs = [], []
    for direction in range(2):
        rf = rows_of[direction]
        width = (3 if with_out else 2) * KW
        out_shape.append(jax.ShapeDtypeStruct((rows, width), MXU_DTYPE))
        out_specs.append(pl.BlockSpec((rb, width), lambda b, i, rf=rf: (rf(b, i), 0)))
    out_shape += [jax.ShapeDtypeStruct((2, KW), F32), jax.ShapeDtypeStruct((2, nb_ex, HEADS, DK, DK), F32)]
    out_specs += [_full((2, KW)), pl.BlockSpec((2, 1, HEADS, DK, DK), lambda b, i: (0, b, 0, 0, 0))]
    res, handle = _host_call(body, name, (nb_ex, nb), in_specs, args, out_shape, out_specs,
                             [pltpu.VMEM((2, HEADS, DK, DK), F32)], after=after, sender=sender)
    return (*res, handle)


def _tail_forward(osum, og, u, v, ga, gb, gna, ln_g, ln_b, ws_ref, bs_ref, wpaT_ref, wpbT_ref, proj=None):
    tm = osum.shape[0]
    gna4 = jnp.concatenate([gna] * HEADS, axis=1)
    r_parts = []
    for h in range(HEADS):
        oh = osum[:, h * DK:(h + 1) * DK]
        r_parts.append(jnp.broadcast_to(lax.rsqrt(jnp.mean(oh * oh, axis=-1, keepdims=True) + EPS), (tm, DK)))
    r = jnp.concatenate(r_parts, axis=1)
    on = osum * r
    sg_og = _sigmoid(og)
    silu_og = og * sg_og
    oan = on * gna4
    oa = oan * silu_og
    ug, tu = _gelu(u)
    vg, tv = _gelu(v)
    mu = jnp.mean(vg, axis=-1, keepdims=True)
    vc = vg - mu
    rstd = lax.rsqrt(jnp.mean(vc * vc, axis=-1, keepdims=True) + EPS)
    vhat = vc * rstd
    vln = vhat * ln_g + ln_b
    blocks = []
    for n in range(tm // SGU_BLOCK):
        rs = slice(n * SGU_BLOCK, (n + 1) * SGU_BLOCK)
        blocks.append(jnp.concatenate(
            [_dot(ws_ref[g], vln[rs, g * DK:(g + 1) * DK]) + bs_ref[g] for g in range(GROUPS)], axis=1))
    mixed = jnp.concatenate(blocks, axis=0) if len(blocks) > 1 else blocks[0]
    obm = ug * mixed
    if proj is None:
        pa = _dot(oa, wpaT_ref[...], "nt")
        pb = _dot(obm, wpbT_ref[...], "nt")
    else:
        pa, pb = proj
    sga, sgb = _sigmoid(ga), _sigmoid(gb)
    merged = sga * pa + sgb * pb
    return dict(r=r, on=on, sg_og=sg_og, silu_og=silu_og, oan=oan, oa=oa, ug=ug, tu=tu, tv=tv, rstd=rstd, vhat=vhat,
                vln=vln, mixed=mixed, obm=obm, pa=pa, pb=pb, sga=sga, sgb=sgb, merged=merged, gna4=gna4)


def _tail_in_specs(tm):
    tile = lambda c: pl.BlockSpec((tm, KW), lambda i: (i, c))
    return [tile(c) for c in range(4, 11)]


def _tail_weight_specs():
    return [_full((1, DK)), _full((1, KW)), _full((1, KW)), _full((GROUPS, SGU_BLOCK, SGU_BLOCK)),
            _full((GROUPS, SGU_BLOCK, 1)), _full((D, KW), single=True), _full((D, KW), single=True),
            _full((D, D), single=True)]


def _read_tail_inputs(of_ref, ob_ref, pcols):
    osum = of_ref[...] + ob_ref[...]
    og, u, v = (pcols[j][...].astype(F32) for j in range(3))
    ga = jnp.concatenate([pcols[3][...], pcols[4][...]], axis=1).astype(F32)
    gb = jnp.concatenate([pcols[5][...], pcols[6][...]], axis=1).astype(F32)
    return osum, og, u, v, ga, gb


def _tail_fwd(p, o_up, o_down, xt, modv, gna, ln_g, ln_b, w_s, b_s, w_paT, w_pbT, w_o, rows_per_example):
    rows = xt.shape[0]
    tm = min(TAIL_TILE, rows_per_example)
    per_b = rows_per_example // tm

    def body(of_ref, ob_ref, *rest):
        pcols = rest[:7]
        (x_ref, mod_ref, gna_ref, lng_ref, lnb_ref, ws_ref, bs_ref, wpaT_ref, wpbT_ref, wo_ref,
         x1_ref, mix_ref, merged_ref, oa_ref, obm_ref, pa_ref, pb_ref) = rest[7:]
        t = _tail_forward(*_read_tail_inputs(of_ref, ob_ref, pcols), gna_ref[...], lng_ref[...], lnb_ref[...],
                          ws_ref, bs_ref, wpaT_ref, wpbT_ref)
        mix = _dot(t["merged"], wo_ref[...])
        x1_ref[...] = x_ref[...] + mod_ref[0, 2:3, :] * mix
        mix_ref[...] = mix.astype(mix_ref.dtype)
        merged_ref[...] = t["merged"].astype(merged_ref.dtype)
        oa_ref[...] = t["oa"].astype(oa_ref.dtype)
        obm_ref[...] = t["obm"].astype(obm_ref.dtype)
        pa_ref[...] = t["pa"].astype(pa_ref.dtype)
        pb_ref[...] = t["pb"].astype(pb_ref.dtype)

    row = lambda w: pl.BlockSpec((tm, w), lambda i: (i, 0))
    in_specs = [row(KW), row(KW)] + _tail_in_specs(tm) + [row(D), pl.BlockSpec((1, N_MOD, D), lambda i: (i // per_b, 0, 0))]
    in_specs += _tail_weight_specs()
    return pl.pallas_call(
        body, name="tail_fwd", grid=(rows // tm,),
        out_shape=(jax.ShapeDtypeStruct((rows, D), F32), jax.ShapeDtypeStruct((rows, D), MXU_DTYPE),
                   jax.ShapeDtypeStruct((rows, D), MXU_DTYPE), jax.ShapeDtypeStruct((rows, KW), MXU_DTYPE),
                   jax.ShapeDtypeStruct((rows, KW), MXU_DTYPE), jax.ShapeDtypeStruct((rows, D), MXU_DTYPE),
                   jax.ShapeDtypeStruct((rows, D), MXU_DTYPE)),
        in_specs=in_specs, out_specs=(row(D), row(D), row(D), row(KW), row(KW), row(D), row(D)),
        compiler_params=_params(("arbitrary",)),
    )(o_up, o_down, *([p] * 7), xt, modv, gna, ln_g, ln_b, w_s, b_s, w_paT, w_pbT, w_o)


def _tail_bwd(p, o_up, o_down, dx1, mix, pa, pb, modv, gna, ln_g, ln_b, w_s, b_s, w_paT, w_pbT, w_o, rows_per_example,
              after=None, sender=None):
    rows = dx1.shape[0]
    nb_ex = rows // rows_per_example
    tm = min(TAIL_TILE, rows_per_example)
    per_b = rows_per_example // tm

    def body(of_ref, ob_ref, *rest):
        pcols = rest[:7]
        (dx1_ref, mix_ref, pa_ref, pb_ref, mod_ref, gna_ref, lng_ref, lnb_ref, ws_ref, bs_ref, wpaT_ref, wpbT_ref, wo_ref,
         dpt_ref, do_ref, dmix_ref, dpa_ref, dpb_ref, dmod_ref, small_ref, dws_ref, dbs_ref) = rest[7:]
        i = pl.program_id(0)

        @pl.when(i == 0)
        def _():
            small_ref[...] = jnp.zeros_like(small_ref)
            dws_ref[...] = jnp.zeros_like(dws_ref)
            dbs_ref[...] = jnp.zeros_like(dbs_ref)

        @pl.when(i % per_b == 0)
        def _():
            dmod_ref[...] = jnp.zeros_like(dmod_ref)

        osum, og, u, v, ga, gb = _read_tail_inputs(of_ref, ob_ref, pcols)
        ln_g = lng_ref[...]
        t = _tail_forward(osum, og, u, v, ga, gb, gna_ref[...], ln_g, lnb_ref[...], ws_ref, bs_ref, wpaT_ref, wpbT_ref,
                          proj=(pa_ref[...].astype(F32), pb_ref[...].astype(F32)))
        dx1v = dx1_ref[...]
        dmod_ref[0, 2:3, :] += jnp.sum(dx1v * mix_ref[...].astype(F32), axis=0, keepdims=True)
        dmix = dx1v * mod_ref[0, 2:3, :]
        dmix_ref[...] = dmix.astype(dmix_ref.dtype)
        dmerged = _dot(dmix, wo_ref[...], "nt")
        sga, sgb = t["sga"], t["sgb"]
        dpa = dmerged * sga
        dpb = dmerged * sgb
        dpa_ref[...] = dpa.astype(dpa_ref.dtype)
        dpb_ref[...] = dpb.astype(dpb_ref.dtype)
        dga = dmerged * t["pa"] * sga * (1.0 - sga)
        dgb = dmerged * t["pb"] * sgb * (1.0 - sgb)
        doa = _dot(dpa, wpaT_ref[...])
        dobm = _dot(dpb, wpbT_ref[...])
        dug = dobm * t["mixed"]
        dmixed = dobm * t["ug"]
        du = dug * _gelu_grad(u, t["tu"])
        dvln_blocks = []
        for n in range(tm // SGU_BLOCK):
            rs = slice(n * SGU_BLOCK, (n + 1) * SGU_BLOCK)
            parts = []
            for g in range(GROUPS):
                gs = slice(g * DK, (g + 1) * DK)
                dm = dmixed[rs, gs]
                parts.append(_dot(ws_ref[g], dm, "tn"))
                dws_ref[g] += _dot(dm, t["vln"][rs, gs], "nt")
                dbs_ref[g] += jnp.sum(dm, axis=1, keepdims=True)
            dvln_blocks.append(jnp.concatenate(parts, axis=1))
        dvln = jnp.concatenate(dvln_blocks, axis=0) if len(dvln_blocks) > 1 else dvln_blocks[0]
        vhat = t["vhat"]
        small_ref[1:2, 0:KW] += jnp.sum(dvln * vhat, axis=0, keepdims=True)
        small_ref[2:3, 0:KW] += jnp.sum(dvln, axis=0, keepdims=True)
        dvhat = dvln * ln_g
        dvg = t["rstd"] * (dvhat - jnp.mean(dvhat, axis=-1, keepdims=True)
                           - vhat * jnp.mean(dvhat * vhat, axis=-1, keepdims=True))
        dv = dvg * _gelu_grad(v, t["tv"])
        sg_og = t["sg_og"]
        doan = doa * t["silu_og"]
        dog = doa * t["oan"] * (sg_og * (1.0 + og * (1.0 - sg_og)))
        prod = doan * t["on"]
        dgna = jnp.zeros((1, DK), F32)
        for h in range(HEADS):
            dgna = dgna + jnp.sum(prod[:, h * DK:(h + 1) * DK], axis=0, keepdims=True)
        small_ref[0:1, 0:DK] += dgna
        don = doan * t["gna4"]
        dot_parts = []
        for h in range(HEADS):
            hs = slice(h * DK, (h + 1) * DK)
            m = jnp.mean(don[:, hs] * t["on"][:, hs], axis=-1, keepdims=True)
            dot_parts.append(t["r"][:, hs] * (don[:, hs] - t["on"][:, hs] * m))
        do_ref[...] = jnp.concatenate(dot_parts, axis=1).astype(do_ref.dtype)
        for j, val in enumerate((dog, du, dv)):
            dpt_ref[:, j * KW:(j + 1) * KW] = val.astype(dpt_ref.dtype)
        dpt_ref[:, 3 * KW:3 * KW + D] = dga.astype(dpt_ref.dtype)
        dpt_ref[:, 3 * KW + D:] = dgb.astype(dpt_ref.dtype)

    row = lambda w: pl.BlockSpec((tm, w), lambda i: (i, 0))
    in_specs = [row(KW), row(KW)] + _tail_in_specs(tm) + [row(D)] * 4 + [pl.BlockSpec((1, N_MOD, D), lambda i: (i // per_b, 0, 0))]
    in_specs += _tail_weight_specs()
    args = [o_up, o_down, *([p] * 7), dx1, mix, pa, pb, modv, gna, ln_g, ln_b, w_s, b_s, w_paT, w_pbT, w_o]
    cd = MXU_DTYPE
    res, handle = _host_call(
        body, "tail_bwd", (rows // tm,), in_specs, args,
        [jax.ShapeDtypeStruct((rows, TAIL_COLS), cd), jax.ShapeDtypeStruct((rows, KW), cd),
         jax.ShapeDtypeStruct((rows, D), cd), jax.ShapeDtypeStruct((rows, D), cd),
         jax.ShapeDtypeStruct((rows, D), cd), jax.ShapeDtypeStruct((nb_ex, 8, D), F32),
         jax.ShapeDtypeStruct((8, D), F32), jax.ShapeDtypeStruct((GROUPS, SGU_BLOCK, SGU_BLOCK), F32),
         jax.ShapeDtypeStruct((GROUPS, SGU_BLOCK, 1), F32)],
        [row(TAIL_COLS), row(KW), row(D), row(D), row(D),
         pl.BlockSpec((1, 8, D), lambda i: (i // per_b, 0, 0)), _full((8, D)),
         _full((GROUPS, SGU_BLOCK, SGU_BLOCK)), _full((GROUPS, SGU_BLOCK, 1))], [],
        after=after, sender=sender)
    return (*res, handle)


def _ffn(x1, target, modv, g_ffn, g_final, w_upT, w_down, rows_per_example):
    rows = x1.shape[0]
    nb_ex = rows // rows_per_example
    tm = min(TOKEN_TILE, rows_per_example)
    per_b = rows_per_example // tm
    n_ff = D_FF // FF_CHUNK

    def body(x1_ref, tgt_ref, mod_ref, gffn_ref, gfin_ref, wup_ref, wdn_ref,
             dx1_ref, h2_ref, dffn_ref, act_ref, dup_ref, dmod_ref, small_ref, up_scr):
        i = pl.program_id(0)

        @pl.when(i == 0)
        def _():
            small_ref[...] = jnp.zeros_like(small_ref)

        @pl.when(i % per_b == 0)
        def _():
            dmod_ref[...] = jnp.zeros_like(dmod_ref)

        x1v = x1_ref[...]
        g2 = gffn_ref[...]
        m3, m4, m5 = mod_ref[0, 3:4, :], mod_ref[0, 4:5, :], mod_ref[0, 5:6, :]
        r2 = lax.rsqrt(jnp.mean(x1v * x1v, axis=-1, keepdims=True) + EPS)
        xn2 = x1v * r2
        h2 = (xn2 * g2) * (1.0 + m4) + m3
        h2b = h2.astype(MXU_DTYPE)
        h2_ref[...] = h2b
        def up_pair(j):
            lo = j * FF_CHUNK
            return (_dot(h2b, wup_ref[lo:lo + FF_CHUNK, :], "nt"),
                    _dot(h2b, wup_ref[D_FF + lo:D_FF + lo + FF_CHUNK, :], "nt"))

        group_end = {min(e, n_ff): s for s, e in ((0, 4), (4, 8), (8, 12))}
        cur, ffn = up_pair(0), None
        for j in range(n_ff):
            nxt = up_pair(j + 1) if j + 1 < n_ff else None
            cs = slice(j * FF_CHUNK, (j + 1) * FF_CHUNK)
            a, bgate = cur
            up_scr[:, cs] = a
            up_scr[:, D_FF + j * FF_CHUNK:D_FF + (j + 1) * FF_CHUNK] = bgate
            act_ref[:, cs] = (a * _sigmoid(a) * bgate).astype(MXU_DTYPE)
            cur = nxt
            if j + 1 in group_end:
                gs = slice(group_end[j + 1] * FF_CHUNK, (j + 1) * FF_CHUNK)
                part = _dot(act_ref[:, gs], wdn_ref[gs, :])
                ffn = part if ffn is None else ffn + part
        x2 = x1v + m5 * ffn
        r3 = lax.rsqrt(jnp.mean(x2 * x2, axis=-1, keepdims=True) + EPS)
        xn3 = x2 * r3
        gf = gfin_ref[...]
        err = xn3 * gf - tgt_ref[...]
        loss = 0.5 * jnp.sum(jnp.mean(err * err, axis=-1, keepdims=True), axis=0, keepdims=True)
        small_ref[2:3, :] += jnp.broadcast_to(loss, (1, D))
        dy = err * (1.0 / D)
        small_ref[1:2, :] += jnp.sum(dy * xn3, axis=0, keepdims=True)
        dxn3 = dy * gf
        dx2 = r3 * (dxn3 - xn3 * jnp.mean(dxn3 * xn3, axis=-1, keepdims=True))
        dmod_ref[0, 5:6, :] += jnp.sum(dx2 * ffn, axis=0, keepdims=True)
        dffn = (dx2 * m5).astype(MXU_DTYPE)
        dffn_ref[...] = dffn
        dact_of = lambda j: _dot(dffn, wdn_ref[j * FF_CHUNK:(j + 1) * FF_CHUNK, :], "nt")
        cur, dh2 = dact_of(0), None
        for j in range(n_ff):
            nxt = dact_of(j + 1) if j + 1 < n_ff else None
            cs = slice(j * FF_CHUNK, (j + 1) * FF_CHUNK)
            a, bgate = up_scr[:, cs], up_scr[:, D_FF + j * FF_CHUNK:D_FF + (j + 1) * FF_CHUNK]
            s = _sigmoid(a)
            dup_ref[:, cs] = (cur * bgate * (s * (1.0 + a * (1.0 - s)))).astype(MXU_DTYPE)
            dup_ref[:, D_FF + j * FF_CHUNK:D_FF + (j + 1) * FF_CHUNK] = (cur * a * s).astype(MXU_DTYPE)
            cur = nxt
            if j + 1 in group_end:
                lo, hi = group_end[j + 1] * FF_CHUNK, (j + 1) * FF_CHUNK
                part = (_dot(dup_ref[:, lo:hi], wup_ref[lo:hi, :])
                        + _dot(dup_ref[:, D_FF + lo:D_FF + hi], wup_ref[D_FF + lo:D_FF + hi, :]))
                dh2 = part if dh2 is None else dh2 + part
        dmod_ref[0, 3:4, :] += jnp.sum(dh2, axis=0, keepdims=True)
        dmod_ref[0, 4:5, :] += jnp.sum(dh2 * xn2 * g2, axis=0, keepdims=True)
        small_ref[0:1, :] += jnp.sum(dh2 * (1.0 + m4) * xn2, axis=0, keepdims=True)
        dxn2 = dh2 * g2 * (1.0 + m4)
        dx1_ref[...] = dx2 + r2 * (dxn2 - xn2 * jnp.mean(dxn2 * xn2, axis=-1, keepdims=True))

    row = lambda w: pl.BlockSpec((tm, w), lambda i: (i, 0))
    cd = MXU_DTYPE
    return pl.pallas_call(
        body, name="ffn_fwd_bwd", grid=(rows // tm,),
        out_shape=(jax.ShapeDtypeStruct((rows, D), F32), jax.ShapeDtypeStruct((rows, D), cd),
                   jax.ShapeDtypeStruct((rows, D), cd), jax.ShapeDtypeStruct((rows, D_FF), cd),
                   jax.ShapeDtypeStruct((rows, 2 * D_FF), cd), jax.ShapeDtypeStruct((nb_ex, 8, D), F32),
                   jax.ShapeDtypeStruct((8, D), F32)),
        in_specs=[row(D), row(D), pl.BlockSpec((1, N_MOD, D), lambda i: (i // per_b, 0, 0)), _full((1, D)), _full((1, D)),
                  _full((2 * D_FF, D), single=True), _full((D_FF, D), single=True)],
        out_specs=(row(D), row(D), row(D), row(D_FF), row(2 * D_FF),
                   pl.BlockSpec((1, 8, D), lambda i: (i // per_b, 0, 0)), _full((8, D))),
        scratch_shapes=[pltpu.VMEM((tm, 2 * D_FF), F32)],
        compiler_params=_params(("arbitrary",)),
    )(x1, target, modv, g_ffn, g_final, w_upT, w_down)


def _scan_columns(up, down, n_groups):
    cols = [up[:, 0:KW].astype(F32), down[:, 0:KW].astype(F32)]
    for j in range(1, n_groups):
        cols.append(up[:, j * KW:(j + 1) * KW].astype(F32) + down[:, j * KW:(j + 1) * KW].astype(F32))
    return cols


def _inproj_bwd(d_up, d_down, dpt, xt, dx1, modv, g, w_inT, rows_per_example, name, sender=None):
    rows = xt.shape[0]
    latent = dx1 is not None
    n_cols = IN_COLS if latent else CTX_COLS
    n_groups = d_up.shape[1] // KW
    tm = min(PROJ_TILE, rows_per_example)
    per_b = rows_per_example // tm
    n_mod_blocks = rows // rows_per_example if latent else 1

    def body(*refs):
        it = iter(refs)
        up_ref, down_ref = next(it), next(it)
        dpt_ref = next(it) if latent else None
        x_ref = next(it)
        dx1_ref = next(it) if latent else None
        mod_ref, g_ref, w_ref = next(it), next(it), next(it)
        gx_ref = next(it) if latent else None
        dp_out = None if latent else next(it)
        dmod_ref, small_ref = next(it), next(it)
        dp_ref = next(it) if latent else dp_out
        i = pl.program_id(0)

        @pl.when(i == 0)
        def _():
            small_ref[...] = jnp.zeros_like(small_ref)

        @pl.when((i % per_b == 0) if latent else (i == 0))
        def _():
            dmod_ref[...] = jnp.zeros_like(dmod_ref)

        for j, val in enumerate(_scan_columns(up_ref[...], down_ref[...], n_groups)):
            dp_ref[:, j * KW:(j + 1) * KW] = val.astype(MXU_DTYPE)
        if latent:
            dh = _dot(dp_ref[...], w_ref[0:4 * KW, :]) + _dot(dpt_ref[...], w_ref[4 * KW:, :])
        else:
            dh = _dot(dp_ref[...], w_ref[...])
        x = x_ref[...]
        gv = g_ref[...]
        m1 = mod_ref[0, 1:2, :]
        r = lax.rsqrt(jnp.mean(x * x, axis=-1, keepdims=True) + EPS)
        xn = x * r
        dmod_ref[0, 0:1, :] += jnp.sum(dh, axis=0, keepdims=True)
        dmod_ref[0, 1:2, :] += jnp.sum(dh * xn * gv, axis=0, keepdims=True)
        small_ref[0:1, :] += jnp.sum(dh * (1.0 + m1) * xn, axis=0, keepdims=True)
        if latent:
            dxn = dh * gv * (1.0 + m1)
            gx_ref[...] = dx1_ref[...] + r * (dxn - xn * jnp.mean(dxn * xn, axis=-1, keepdims=True))

    row = lambda w: pl.BlockSpec((tm, w), lambda i: (i, 0))
    mod_idx = (lambda i: (i // per_b, 0, 0)) if latent else (lambda i: (0, 0, 0))
    in_specs = [row(n_groups * KW)] * 2 + ([row(TAIL_COLS)] if latent else []) + [row(D)] + ([row(D)] if latent else [])
    in_specs += [pl.BlockSpec((1, N_MOD, D), mod_idx), _full((1, D)),
                 pl.BlockSpec((n_cols, D), lambda i: (0, 0), pipeline_mode=pl.Buffered(1))]
    args = [d_up, d_down] + ([dpt] if latent else []) + [xt] + ([dx1] if latent else []) + [modv, g, w_inT]
    first = jax.ShapeDtypeStruct((rows, D), F32) if latent else jax.ShapeDtypeStruct((rows, n_cols), MXU_DTYPE)
    out_shape = [first, jax.ShapeDtypeStruct((n_mod_blocks, 8, D), F32), jax.ShapeDtypeStruct((8, D), F32)]
    out_specs = [row(D) if latent else row(n_cols), pl.BlockSpec((1, 8, D), mod_idx), _full((8, D))]
    scratch = [pltpu.VMEM((tm, 4 * KW), MXU_DTYPE)] if latent else []
    res, handle = _host_call(body, name, (rows // tm,), in_specs, args, out_shape, out_specs, scratch, sender=sender)
    return (*res, handle)


def _grad_matmul(a, b, name, init=None, tn=512, sender=None):
    rows, n = a.shape
    k = b.shape[1]
    tn = min(tn, n)
    has_init = init is not None
    init_blocks = init.shape[0] // tn if has_init else 0

    def body(*refs):
        if has_init:
            a_ref, b_ref, init_ref, o_ref = refs
        else:
            a_ref, b_ref, o_ref = refs
        g = _dot(a_ref[...], b_ref[...], "tn")
        if has_init:
            g = g + jnp.where(pl.program_id(0) < init_blocks, init_ref[...].astype(F32), 0.0)
        o_ref[...] = g.astype(o_ref.dtype)

    in_specs = [pl.BlockSpec((rows, tn), lambda i: (0, i)), _full((rows, k), single=True)]
    args = [a, b]
    if has_init:
        in_specs.append(pl.BlockSpec((tn, k), lambda i: (jnp.minimum(i, init_blocks - 1), 0)))
        args.append(init)
    (out,), handle = _host_call(
        body, name, (n // tn,), in_specs, args, [jax.ShapeDtypeStruct((n, k), PAYLOAD_DTYPE)],
        [pl.BlockSpec((tn, k), lambda i: (i, 0))], [], sender=sender)
    return out, handle


def _grad_in(d_up, d_down, dpt, h, init, sender=None):
    rows = h.shape[0]
    tn = 256
    per_group = KW // tn
    n_scan = 4 * per_group
    init_blocks = init.shape[0] // tn

    def body(up_ref, down_ref, dpt_ref, h_ref, init_ref, o_ref):
        i = pl.program_id(0)
        both = (up_ref[...].astype(F32) + down_ref[...].astype(F32)).astype(MXU_DTYPE)
        a = jnp.where(i < per_group, up_ref[...],
                      jnp.where(i < 2 * per_group, down_ref[...], jnp.where(i < n_scan, both, dpt_ref[...])))
        g = _dot(a, h_ref[...], "tn") + jnp.where(i < init_blocks, init_ref[...].astype(F32), 0.0)
        o_ref[...] = g.astype(o_ref.dtype)

    last = 3 * per_group - 1
    col = lambda f: pl.BlockSpec((rows, tn), lambda i: (0, f(i)))
    in_specs = [col(lambda i: jnp.clip(jnp.where(i < per_group, i, i - per_group), 0, last)),
                col(lambda i: jnp.clip(i - per_group, 0, last)),
                col(lambda i: jnp.clip(i - n_scan, 0, TAIL_COLS // tn - 1)),
                _full((rows, D), single=True),
                pl.BlockSpec((tn, D), lambda i: (jnp.minimum(i, init_blocks - 1), 0))]
    (out,), handle = _host_call(
        body, "gw_in", (IN_COLS // tn,), in_specs, [d_up, d_down, dpt, h, init],
        [jax.ShapeDtypeStruct((IN_COLS, D), PAYLOAD_DTYPE)], [pl.BlockSpec((tn, D), lambda i: (i, 0))], [],
        sender=sender)
    return out, handle


def _row_tile(rows, limit=256):
    if rows <= limit:
        return rows
    for t in range(limit, 7, -8):
        if rows % t == 0:
            return t
    return rows


def _sum8(stack, name):
    _, rows, cols = stack.shape
    tr = _row_tile(rows)

    def body(s_ref, o_ref):
        acc = s_ref[0].astype(F32)
        for j in range(1, N_DEV):
            acc = acc + s_ref[j].astype(F32)
        o_ref[...] = acc

    return pl.pallas_call(
        body, name=name, grid=(rows // tr,), out_shape=jax.ShapeDtypeStruct((rows, cols), F32),
        in_specs=[pl.BlockSpec((N_DEV, tr, cols), lambda i: (0, i, 0))],
        out_specs=pl.BlockSpec((tr, cols), lambda i: (i, 0)),
        compiler_params=_params(("arbitrary",)),
    )(stack)


def _adamw_update(w, gv, m, v):
    nm = ADAM_B1 * m + (1.0 - ADAM_B1) * gv
    nv = ADAM_B2 * v + (1.0 - ADAM_B2) * (gv * gv)
    m_hat = nm / (1.0 - ADAM_B1 ** ADAM_STEP)
    v_hat = nv / (1.0 - ADAM_B2 ** ADAM_STEP)
    return -ADAM_LR * (m_hat / (jnp.sqrt(v_hat) + ADAM_EPS) + ADAM_WD * w), nm, nv


SMALL_PARAMS = (("g_mix", 0, D), ("g_ffn", 1, D), ("g_final", 2, D), ("g_norm_a", 3, DK), ("ln_v_g", 4, KW),
                ("ln_v_b", 5, KW), ("b_s", 6, GROUPS * SGU_BLOCK),
                ("c_ctx", 15, D))


def _small_finish(early, late, dws, gam, nb_ex, params):
    names = [n for n, _, _ in SMALL_PARAMS] + ["b_mod", "w_s"]

    def body(*refs):
        s_ref, l_ref, dws_ref, gam_ref = refs[:4]
        p_refs = refs[4:4 + 3 * len(names)]
        tot_ref, dgam_ref = refs[4 + 3 * len(names):6 + 3 * len(names)]
        o_refs = refs[6 + 3 * len(names):]
        acc = s_ref[0] + l_ref[0]
        gws = dws_ref[0]
        for j in range(1, N_DEV):
            acc = acc + (s_ref[j] + l_ref[j])
            gws = gws + dws_ref[j]
        tot_ref[...] = acc
        bm = acc[8:8 + N_MOD, :]
        for e in range(nb_ex):
            bm = bm + acc[16 + e * N_MOD:16 + (e + 1) * N_MOD, :]
        lb = jnp.concatenate([_lower_bound(gam_ref, 0), _lower_bound(gam_ref, 1)], axis=1)
        dgam = acc[7:8, :] * lb * (1.0 - lb)
        dgam_ref[...] = jnp.concatenate([dgam, -dgam], axis=0)
        bm = jnp.concatenate([bm[j:j + 1] for j in range(N_MOD)], axis=1)
        grads = [acc[row:row + 1, 0:width] for _, row, width in SMALL_PARAMS] + [bm, gws]
        for k, g in enumerate(grads):
            w_ref, m_ref, v_ref = p_refs[3 * k:3 * k + 3]
            o_refs[4 * k][...] = g
            o_refs[4 * k + 1][...], o_refs[4 * k + 2][...], o_refs[4 * k + 3][...] = _adamw_update(
                w_ref[...], g, m_ref[...], v_ref[...])

    p_args, p_specs, o_shapes, o_specs = [], [], [], []
    for n in names:
        for a in params[n]:
            p_args.append(a)
            p_specs.append(_full(a.shape))
        o_shapes += [jax.ShapeDtypeStruct(params[n][0].shape, F32)] * 4
        o_specs += [_full(params[n][0].shape)] * 4
    res = pl.pallas_call(
        body, name="small_finish", grid=(1,),
        out_shape=[jax.ShapeDtypeStruct((SMALL_ROWS, D), F32), jax.ShapeDtypeStruct((2, D), F32)] + o_shapes,
        in_specs=[_full(early.shape), _full(late.shape), _full(dws.shape), _full((4, KW))] + p_specs,
        out_specs=[_full((SMALL_ROWS, D)), _full((2, D))] + o_specs,
        compiler_params=_params(("arbitrary",)),
    )(early, late, dws, gam, *p_args)
    return res[0], res[1], {n: res[2 + 4 * k:6 + 4 * k] for k, n in enumerate(names)}


def _adamw_sum8(stack, w, m, v, name):
    _, rows, cols = stack.shape
    tr = _row_tile(rows)

    def body(s_ref, w_ref, m_ref, v_ref, g_ref, d_ref, nm_ref, nv_ref):
        gv = s_ref[0].astype(F32)
        for j in range(1, N_DEV):
            gv = gv + s_ref[j].astype(F32)
        g_ref[...] = gv
        d_ref[...], nm_ref[...], nv_ref[...] = _adamw_update(w_ref[...], gv, m_ref[...], v_ref[...])

    blk = pl.BlockSpec((tr, cols), lambda i: (i, 0))
    sd = jax.ShapeDtypeStruct((rows, cols), F32)
    return pl.pallas_call(
        body, name=name, grid=(rows // tr,), out_shape=(sd, sd, sd, sd),
        in_specs=[pl.BlockSpec((N_DEV, tr, cols), lambda i: (0, i, 0)), blk, blk, blk], out_specs=(blk, blk, blk, blk),
        compiler_params=_params(("arbitrary",)),
    )(stack, w, m, v)


def _adamw(w, g, m, v, name):
    shape = w.shape
    cols = shape[-1]
    rows = 1
    for s in shape[:-1]:
        rows *= s
    tr = _row_tile(rows)

    def body(w_ref, g_ref, m_ref, v_ref, d_ref, nm_ref, nv_ref):
        gv = g_ref[...]
        nm = ADAM_B1 * m_ref[...] + (1.0 - ADAM_B1) * gv
        nv = ADAM_B2 * v_ref[...] + (1.0 - ADAM_B2) * (gv * gv)
        m_hat = nm / (1.0 - ADAM_B1 ** ADAM_STEP)
        v_hat = nv / (1.0 - ADAM_B2 ** ADAM_STEP)
        d_ref[...] = -ADAM_LR * (m_hat / (jnp.sqrt(v_hat) + ADAM_EPS) + ADAM_WD * w_ref[...])
        nm_ref[...] = nm
        nv_ref[...] = nv

    blk = pl.BlockSpec((tr, cols), lambda i: (i, 0))
    sd = jax.ShapeDtypeStruct((rows, cols), F32)
    d, nm, nv = pl.pallas_call(
        body, name=name, grid=(rows // tr,), out_shape=(sd, sd, sd), in_specs=[blk] * 4, out_specs=(blk, blk, blk),
        compiler_params=_params(("arbitrary",)),
    )(w.reshape(rows, cols), g.reshape(rows, cols), m.reshape(rows, cols), v.reshape(rows, cols))
    return d.reshape(shape), nm.reshape(shape), nv.reshape(shape)


def _owner_blocks(a):
    return a.reshape(N_DEV, a.shape[0] // N_DEV, a.shape[1])


class _LocalWeights:
    def __init__(self, w_upT, w_down, w_o, w_paT, w_pbT):
        self.weights = (w_upT, w_down, w_o, w_paT, w_pbT)
        self.items = {}

    def sender(self, stage, items=None):
        self.items[stage] = items
        return None

    def sent(self, stage, handle):
        pass

    def mixer_weights(self, after):
        return self.weights[1:]

    def ffn_weights(self, after):
        return self.weights[0]

    def c_ctx_part(self, after):
        return jnp.zeros((1, D), F32)


def _local_step(x, ctx, target, modv, mcv, gam, g_mix, g_ffn, gna, ln_g, ln_b, w_s, b_s, g_final, w_inT, comm):
    nb_ex, seq, _ = x.shape
    ctx_len = ctx.shape[1]
    xt = x.reshape(nb_ex * seq, D)
    ct = ctx.reshape(nb_ex * ctx_len, D)
    tgt = target.reshape(nb_ex * seq, D)
    bs3 = b_s.reshape(GROUPS, SGU_BLOCK, 1)

    pc, hc, _ = _inproj(ct, mcv, g_mix, w_inT, CTX_COLS, ctx_len, "inproj_ctx")
    p, h, handle = _inproj(xt, modv, g_mix, w_inT, IN_COLS, seq, "inproj_lat", sender=comm.sender("inproj"))
    comm.sent("inproj", handle)
    cst_f, cst_b, cb_f, cb_b, s_ctx, _ = _hgrn_fwd(pc, gam, None, ctx_len, False, "hgrn_fwd_ctx")
    o_up, o_down, st_f, st_b, b_f, b_b, _, handle = _hgrn_fwd(p, gam, s_ctx, seq, True, "hgrn_fwd_lat",
                                                              sender=comm.sender("scan"))
    comm.sent("scan", handle)
    w_down, w_o, w_paT, w_pbT = comm.mixer_weights(o_up)
    x1, mix, merged, oa, obm, pa, pb = _tail_fwd(p, o_up, o_down, xt, modv, gna, ln_g, ln_b, w_s, bs3, w_paT, w_pbT,
                                                 w_o, seq)
    w_upT = comm.ffn_weights(x1)
    dx1, h2, dffn, act, dup, dmod_ffn, small_ffn = _ffn(x1, tgt, modv, g_ffn, g_final, w_upT, w_down, seq)
    gw_upT, _ = _grad_matmul(dup, h2, "gw_up")
    gw_down, _ = _grad_matmul(act, dffn, "gw_down", tn=256)
    scatter = lambda *grads: [(_owner_blocks(g), "scatter") for g in grads]
    dpt, do, dmix, dpa, dpb, dmod_tail, small_tail, dws, dbs, handle = _tail_bwd(
        p, o_up, o_down, dx1, mix, pa, pb, modv, gna, ln_g, ln_b, w_s, bs3, w_paT, w_pbT, w_o, seq,
        sender=comm.sender("tail_bwd", scatter(gw_upT)))
    comm.sent("tail_bwd", handle)
    gw_o, _ = _grad_matmul(merged, dmix, "gw_o")
    gw_paT, _ = _grad_matmul(dpa, oa, "gw_pa")
    gw_pbT, _ = _grad_matmul(dpb, obm, "gw_pb")
    def at_row(row, a):
        return jnp.pad(a, ((row, SMALL_ROWS - row - a.shape[0]), (0, D - a.shape[1])))

    small_early = (at_row(1, small_ffn[0:2])
                   + at_row(3, small_tail[0:3])
                   + at_row(6, dbs.reshape(1, GROUPS * SGU_BLOCK))
                   + at_row(14, small_ffn[2:3]))
    dws_rows = dws.reshape(GROUPS * SGU_BLOCK, SGU_BLOCK)
    d_up, d_down, dlb, ds0, handle = _hgrn_bwd(
        p, gam, do, st_f, st_b, b_f, b_b, None, seq, True, "hgrn_bwd_lat",
        sender=comm.sender("scan_bwd", scatter(gw_down, gw_o, gw_paT, gw_pbT)
                           + [(small_early, "gather"), (dws_rows, "gather")]))
    comm.sent("scan_bwd", handle)
    c_up, c_down, dlb_c, _, _ = _hgrn_bwd(pc, gam, None, cst_f, cst_b, cb_f, cb_b, ds0, ctx_len, False, "hgrn_bwd_ctx")
    dpc, dmc, small_c, _ = _inproj_bwd(c_up, c_down, None, ct, None, mcv, g_mix, w_inT, ctx_len, "inproj_bwd_ctx")
    gw_in_c, handle = _grad_matmul(dpc, hc, "gw_in_ctx", sender=comm.sender("ctx_mod", [(dmc[0], "gather")]))
    comm.sent("ctx_mod", handle)
    gw_inT, _ = _grad_in(d_up, d_down, dpt, h, gw_in_c)
    grad_x, dmod_in, small_in, handle = _inproj_bwd(d_up, d_down, dpt, xt, dx1, modv, g_mix, w_inT, seq,
                                                   "inproj_bwd_lat", sender=comm.sender("inproj_bwd", scatter(gw_inT)))
    comm.sent("inproj_bwd", handle)
    dmod = dmod_in + dmod_tail + dmod_ffn
    small_late = (at_row(0, small_in[0:1] + small_c[0:1])
                  + at_row(7, (dlb + dlb_c).reshape(1, 2 * KW))
                  + at_row(8, dmc[0, 0:N_MOD])
                  + at_row(15, comm.c_ctx_part(gw_inT))
                  + at_row(16, dmod[:, 0:N_MOD].reshape(nb_ex * N_MOD, D)))
    comm.sender("last", [(small_late, "gather")])
    return grad_x.reshape(x.shape)


def kernel(x, c, ctx, c_ctx, w_mod, b_mod, g_mix, g_ffn, w_in, lb_gamma, g_norm_a, ln_v_g, ln_v_b, w_s, b_s, w_pa, w_pb, w_o, w_up, w_down, g_final, loss_target, m_c_ctx, m_w_mod, m_b_mod, m_g_mix, m_g_ffn, m_w_in, m_lb_gamma, m_g_norm_a, m_ln_v_g, m_ln_v_b, m_w_s, m_b_s, m_w_pa, m_w_pb, m_w_o, m_w_up, m_w_down, m_g_final, v_c_ctx, v_w_mod, v_b_mod, v_g_mix, v_g_ffn, v_w_in, v_lb_gamma, v_g_norm_a, v_ln_v_g, v_ln_v_b, v_w_s, v_b_s, v_w_pa, v_w_pb, v_w_o, v_w_up, v_w_down, v_g_final):
    nb_ex = x.shape[0]
    me = 4 * lax.axis_index("x") + 2 * lax.axis_index("y") + lax.axis_index("c")
    cd = MXU_DTYPE
    mod_cols = w_mod.shape[2]
    lb_cols = lb_gamma.shape[2]

    w_inT_l = w_in[0].T.astype(cd)
    w_upT_l = w_up[0].T.astype(cd)
    w_paT_l = w_pa[0].T.astype(cd)
    w_pbT_l = w_pb[0].T.astype(cd)
    cl = jnp.concatenate([c, jnp.pad(lb_gamma.reshape(1, 4 * lb_cols), ((0, 0), (0, D - 4 * lb_cols))),
                          jnp.zeros((8 - nb_ex - 1, D), F32)], axis=0)
    g_in, g_cl = _gather_two_level([w_inT_l, cl], "gather_w_in")
    w_inT = g_in.reshape(IN_COLS, D)
    c_all = g_cl[:, 0:nb_ex].reshape(N_DEV * nb_ex, D)
    gam = jnp.transpose(g_cl[:, nb_ex, 0:4 * lb_cols].reshape(N_DEV, 4, lb_cols), (1, 0, 2)).reshape(4, KW)

    n_c = N_DEV * nb_ex
    cvec = jnp.concatenate([c_all, c_ctx.reshape(1, D), jnp.zeros((7, D), F32)], axis=0)
    b_mod_l = lax.dynamic_slice(b_mod, (0, me * mod_cols), (1, mod_cols))
    mod_l, svec = _mod_fwd(cvec, w_mod[0], b_mod_l)
    (g_mod,) = _gather_two_level([mod_l], "gather_mod")
    mod_all = jnp.transpose(g_mod, (1, 0, 2)).reshape(n_c + 8, N_MOD * D)
    modv = lax.dynamic_slice(mod_all, (me * nb_ex, 0), (nb_ex, N_MOD * D)).reshape(nb_ex, N_MOD, D)
    mcv = mod_all[n_c].reshape(1, N_MOD, D)

    handles, leftover = {}, {}

    class Comm:
        def sender(self, stage, items=None):
            if stage == "inproj":
                return _Sender([(w_down[0].astype(cd), "gather"), (w_o[0].astype(cd), "gather"), (w_paT_l, "gather"),
                                (w_pbT_l, "gather")])
            if stage == "scan":
                return _Sender([(w_upT_l, "gather")])
            if stage == "last":
                leftover["items"] = items
                return None
            return _Sender(items)

        def sent(self, stage, handle):
            handles[stage] = handle

        def mixer_weights(self, after):
            g_down, g_o, g_pa, g_pb = _exchange_wait(handles["inproj"], after)
            return g_down.reshape(D_FF, D), g_o.reshape(D, D), g_pa.reshape(D, KW), g_pb.reshape(D, KW)

        def ffn_weights(self, after):
            (g_up,) = _exchange_wait(handles["scan"], after)
            return g_up.reshape(2 * D_FF, D)

        def c_ctx_part(self, after):
            (r_dmc,) = _exchange_wait(handles["ctx_mod"], after)
            dmc_tot = _sum8(r_dmc, "sum_ctx_mod")
            dmc_l = lax.dynamic_slice(dmc_tot[0:N_MOD].reshape(1, N_MOD * D), (0, me * mod_cols), (1, mod_cols))
            return _mod_bwd_ctx(c_ctx.reshape(1, D), jnp.pad(dmc_l, ((0, 7), (0, 0))), w_mod[0])[0:1]

    grad_x = _local_step(
        x, ctx, loss_target, modv, mcv, gam, g_mix, g_ffn, g_norm_a, ln_v_g, ln_v_b, w_s[0], b_s[0],
        g_final.reshape(1, D), w_inT, Comm())
    last, last_started = _exchange_start(leftover["items"], "gather_small_late", after=leftover["items"][0][0])

    (r_up,) = _exchange_wait(handles["tail_bwd"], last_started)
    r_down, r_o, r_pa, r_pb, r_small, r_dws = _exchange_wait(handles["scan_bwd"], r_up)
    raw_up = _adamw_sum8(r_up, w_up[0].T, m_w_up[0].T, v_w_up[0].T, "adamw_w_up")
    raw_down = _adamw_sum8(r_down, w_down[0], m_w_down[0], v_w_down[0], "adamw_w_down")
    raw_o = _adamw_sum8(r_o, w_o[0], m_w_o[0], v_w_o[0], "adamw_w_o")
    updated = raw_up[1][0:8, 0:128] + raw_down[1][0:8, 0:128] + raw_o[1][0:8, 0:128]
    (r_in,) = _exchange_wait(handles["inproj_bwd"], updated)
    raw_in = _adamw_sum8(r_in, w_in[0].T, m_w_in[0].T, v_w_in[0].T, "adamw_w_in")
    (r_late,) = _exchange_wait(last, raw_in[1])
    done = {"w_in": [a.T[None] for a in raw_in], "w_up": [a.T[None] for a in raw_up],
            "w_down": [a[None] for a in raw_down], "w_o": [a[None] for a in raw_o]}
    grad_w_in, grad_w_up, grad_w_down, grad_w_o = (done[k][0] for k in ("w_in", "w_up", "w_down", "w_o"))
    grad_w_pa = _sum8(r_pa, "sum_w_pa").T[None]
    grad_w_pb = _sum8(r_pb, "sum_w_pb").T[None]
    as_2d = {"c_ctx": (1, D), "g_final": (1, D), "b_s": (1, GROUPS * SGU_BLOCK), "w_s": (GROUPS * SGU_BLOCK, SGU_BLOCK)}
    small_params = {"g_mix": (g_mix, m_g_mix, v_g_mix), "g_ffn": (g_ffn, m_g_ffn, v_g_ffn),
                    "g_final": (g_final, m_g_final, v_g_final), "g_norm_a": (g_norm_a, m_g_norm_a, v_g_norm_a),
                    "ln_v_g": (ln_v_g, m_ln_v_g, v_ln_v_g), "ln_v_b": (ln_v_b, m_ln_v_b, v_ln_v_b),
                    "b_s": (b_s, m_b_s, v_b_s), "c_ctx": (c_ctx, m_c_ctx, v_c_ctx), "b_mod": (b_mod, m_b_mod, v_b_mod), "w_s": (w_s, m_w_s, v_w_s)}
    tot, dgam, small_done = _small_finish(
        r_small, r_late, r_dws, gam, nb_ex,
        {n: tuple(a.reshape(as_2d.get(n, a.shape)) for a in wmv) for n, wmv in small_params.items()})
    for n, outs in small_done.items():
        done[n] = [a.reshape(small_params[n][0].shape) for a in outs]
    loss = tot[14, 0]
    grad_g_mix, grad_g_ffn, grad_g_final, grad_g_norm_a, grad_ln_v_g, grad_ln_v_b, grad_b_s, grad_b_mod, grad_w_s = (
        done[n][0] for n in ("g_mix", "g_ffn", "g_final", "g_norm_a", "ln_v_g", "ln_v_b", "b_s", "b_mod", "w_s"))
    grad_lb_gamma = lax.dynamic_slice(dgam.reshape(2, 2, KW), (0, 0, me * lb_cols), (2, 2, lb_cols))

    dmod_all = r_late[:, 16:16 + nb_ex * N_MOD].reshape(n_c, N_MOD * D)
    dmod_l = jnp.concatenate([lax.dynamic_slice(dmod_all, (0, me * mod_cols), (n_c, mod_cols)),
                              lax.dynamic_slice(tot[8:8 + N_MOD].reshape(1, N_MOD * D), (0, me * mod_cols), (1, mod_cols)),
                              jnp.zeros((7, mod_cols), F32)], axis=0)
    grad_w_mod = _mod_bwd(svec, dmod_l)[None]
    grad_c_ctx = done["c_ctx"][0]

    names = ["c_ctx", "w_mod", "b_mod", "g_mix", "g_ffn", "w_in", "lb_gamma", "g_norm_a", "ln_v_g", "ln_v_b", "w_s",
             "b_s", "w_pa", "w_pb", "w_o", "w_up", "w_down", "g_final"]
    weights = [c_ctx, w_mod, b_mod, g_mix, g_ffn, w_in, lb_gamma, g_norm_a, ln_v_g, ln_v_b, w_s, b_s, w_pa, w_pb, w_o,
               w_up, w_down, g_final]
    grads = [grad_c_ctx, grad_w_mod, grad_b_mod, grad_g_mix, grad_g_ffn, grad_w_in, grad_lb_gamma, grad_g_norm_a,
             grad_ln_v_g, grad_ln_v_b, grad_w_s, grad_b_s, grad_w_pa, grad_w_pb, grad_w_o, grad_w_up, grad_w_down,
             grad_g_final]
    ms = [m_c_ctx, m_w_mod, m_b_mod, m_g_mix, m_g_ffn, m_w_in, m_lb_gamma, m_g_norm_a, m_ln_v_g, m_ln_v_b, m_w_s, m_b_s,
          m_w_pa, m_w_pb, m_w_o, m_w_up, m_w_down, m_g_final]
    vs = [v_c_ctx, v_w_mod, v_b_mod, v_g_mix, v_g_ffn, v_w_in, v_lb_gamma, v_g_norm_a, v_ln_v_g, v_ln_v_b, v_w_s, v_b_s,
          v_w_pa, v_w_pb, v_w_o, v_w_up, v_w_down, v_g_final]
    deltas, new_ms, new_vs = [], [], []
    for nm, w, g, m, v in zip(names, weights, grads, ms, vs):
        d, nm_, nv_ = done[nm][1:] if nm in done else _adamw(w, g.reshape(w.shape), m, v, "adamw_" + nm)
        deltas.append(d)
        new_ms.append(nm_)
        new_vs.append(nv_)
    grads = [g.reshape(w.shape) for g, w in zip(grads, weights)]
    return (loss, grad_x, *grads, *deltas, *new_ms, *new_vs)
```

```python
import functools

import jax
import jax.numpy as jnp
from jax import lax
from jax.experimental import pallas as pl
from jax.experimental.pallas import tpu as pltpu

F32 = jnp.float32
MXU_DTYPE = jnp.bfloat16
PAYLOAD_DTYPE = jnp.bfloat16

N_DEV = 8
D = 1024
HEADS = 4
DK = 128
KW = HEADS * DK
CHUNK = 64
SGU_BLOCK = 128
GROUPS = 4
D_FF = 2816
FF_CHUNK = 256
N_MOD = 6
IN_COLS = 5632
CTX_COLS = 1536
TAIL_COLS = IN_COLS - 4 * KW
EPS = 1e-6
ADAM_LR, ADAM_B1, ADAM_B2, ADAM_EPS, ADAM_WD, ADAM_STEP = 0.001, 0.9, 0.999, 1e-08, 0.01, 10

VMEM_LIMIT = 56 * 1024 * 1024
TOKEN_TILE = 256
PROJ_TILE = 512
TAIL_TILE = 512
SMALL_ROWS = 40


def _params(sem):
    return pltpu.CompilerParams(dimension_semantics=sem, vmem_limit_bytes=VMEM_LIMIT)


_DN = {"nn": (((1,), (0,)), ((), ())), "nt": (((1,), (1,)), ((), ())), "tn": (((0,), (0,)), ((), ()))}


def _dot(a, b, form="nn"):
    return lax.dot_general(a.astype(MXU_DTYPE), b.astype(MXU_DTYPE), _DN[form], preferred_element_type=F32)


def _mask_dot(mask, v):
    bf = jnp.bfloat16
    hi = v.astype(bf)
    mid = (v - hi.astype(F32)).astype(bf)
    w = v.shape[1]
    s = lax.dot_general(mask.astype(bf), jnp.concatenate([hi, mid], axis=1), _DN["nn"], preferred_element_type=F32)
    return s[:, w:] + s[:, :w]


def _full(shape, single=False):
    n = len(shape)
    if single:
        return pl.BlockSpec(shape, lambda *_: (0,) * n, pipeline_mode=pl.Buffered(1))
    return pl.BlockSpec(shape, lambda *_: (0,) * n)


def _ordered_behind(body, in_specs, args, after):
    if after is None:
        return body
    at = len(in_specs)
    in_specs.append(pl.BlockSpec(memory_space=pl.ANY))
    args.append(after)
    return lambda *refs: body(*refs[:at], *refs[at + 1:])


def _sigmoid(z):
    return 0.5 * jnp.tanh(0.5 * z) + 0.5


def _gelu(x):
    c = 0.7978845608028654
    t = jnp.tanh(c * (x + 0.044715 * x * x * x))
    return 0.5 * x * (1.0 + t), t


def _gelu_grad(x, t):
    c = 0.7978845608028654
    return 0.5 * (1.0 + t) + 0.5 * x * (1.0 - t * t) * c * (1.0 + 3 * 0.044715 * x * x)


def _exchange(items, name, after=None):
    n = len(items)
    out_shape = []
    for a, mode in items:
        blk = a.shape if mode == "gather" else a.shape[1:]
        out_shape.append(jax.ShapeDtypeStruct((N_DEV,) + tuple(blk), a.dtype))

    def body(*refs):
        srcs, dsts = refs[:n], refs[n:2 * n]
        send_sems, recv_sems, local_sems = refs[2 * n:]
        x, y, c = lax.axis_index("x"), lax.axis_index("y"), lax.axis_index("c")
        me = 4 * x + 2 * y + c

        def src_for(i, dev):
            return srcs[i] if items[i][1] == "gather" else srcs[i].at[dev]

        local = [pltpu.make_async_copy(src_for(i, me), dsts[i].at[me], local_sems.at[i]) for i in range(n)]
        for cp in local:
            cp.start()
        remote = []
        for k in range(1, N_DEV):
            px = jnp.bitwise_xor(x, (k >> 2) & 1)
            py = jnp.bitwise_xor(y, (k >> 1) & 1)
            pc = jnp.bitwise_xor(c, k & 1)
            peer = 4 * px + 2 * py + pc
            for i in range(n):
                cp = pltpu.make_async_remote_copy(
                    src_ref=src_for(i, peer), dst_ref=dsts[i].at[me],
                    send_sem=send_sems.at[i * (N_DEV - 1) + k - 1], recv_sem=recv_sems.at[i * (N_DEV - 1) + k - 1],
                    device_id=(px, py, pc), device_id_type=pl.DeviceIdType.MESH)
                cp.start()
                remote.append(cp)
        for cp in remote:
            cp.wait()
        for cp in local:
            cp.wait()

    any_spec = pl.BlockSpec(memory_space=pl.ANY)
    in_specs, args = [any_spec] * n, [a for a, _ in items]
    if after is not None:
        in_specs.append(any_spec)
        args.append(after)
        exchange = body
        body = lambda *refs: exchange(*refs[:n], *refs[n + 1:])
    return pl.pallas_call(
        body, name=name, out_shape=out_shape, in_specs=in_specs, out_specs=[any_spec] * n,
        scratch_shapes=[pltpu.SemaphoreType.DMA((n * (N_DEV - 1),)), pltpu.SemaphoreType.DMA((n * (N_DEV - 1),)),
                        pltpu.SemaphoreType.DMA((n,))],
    )(*args)


def _gather_two_level(arrays, name):
    n = len(arrays)
    pieces = []
    for i, a in enumerate(arrays):
        rows = _Sender.PIECE_ROWS if a.shape[0] % _Sender.PIECE_ROWS == 0 else a.shape[0]
        pieces += [(i, r0, rows) for r0 in range(0, a.shape[0], rows)]

    def body(*refs):
        srcs, dsts = refs[:n], refs[n:2 * n]
        send_sems, recv_sems, local_sems = refs[2 * n:]
        x, y, c = lax.axis_index("x"), lax.axis_index("y"), lax.axis_index("c")
        me, sibling = (x, y, c), (x, y, 1 - c)
        x_nbr, y_nbr, diag = (1 - x, y, c), (x, 1 - y, c), (1 - x, 1 - y, c)

        def slot(px, py, pc):
            return 4 * px + 2 * py + pc

        def copy(u, k, block, to, own=False):
            i, r0, rows = pieces[u]
            there = dsts[i].at[slot(*block)].at[pl.ds(r0, rows)]
            return pltpu.make_async_remote_copy(
                src_ref=srcs[i].at[pl.ds(r0, rows)] if own else there, dst_ref=there,
                send_sem=send_sems.at[u * 7 + k], recv_sem=recv_sems.at[u * 7 + k],
                device_id=to, device_id_type=pl.DeviceIdType.MESH)

        units = range(len(pieces))
        mine = [pltpu.make_async_copy(srcs[i], dsts[i].at[slot(*me)], local_sems.at[i]) for i in range(n)]
        for cp in mine:
            cp.start()
        for u in units:
            copy(u, 1, me, x_nbr, own=True).start()
            copy(u, 2, me, y_nbr, own=True).start()
        for u in units:
            copy(u, 0, me, sibling, own=True).start()

        def relay_then_pass(k_from, frm, to, k_other, other):
            for u in units:
                copy(u, k_from, frm, me).wait_recv()
                copy(u, 3, frm, to).start()
                copy(u, 3 + k_from, frm, sibling).start()
            for u in units:
                copy(u, k_other, other, me).wait_recv()
                copy(u, 3 + k_other, other, sibling).start()

        @pl.when(c == 1)
        def _():
            relay_then_pass(1, x_nbr, y_nbr, 2, y_nbr)

        @pl.when(c == 0)
        def _():
            relay_then_pass(2, y_nbr, x_nbr, 1, x_nbr)

        for u in units:
            copy(u, 3, diag, me).wait_recv()
            copy(u, 6, diag, sibling).start()
        for u in units:
            copy(u, 0, sibling, me).wait_recv()
            for k, chip in ((4, x_nbr), (5, y_nbr), (6, diag)):
                copy(u, k, (chip[0], chip[1], 1 - c), me).wait_recv()
        for u in units:
            for k in range(7):
                copy(u, k, me, me, own=True).wait_send()
        for cp in mine:
            cp.wait()

    any_spec = pl.BlockSpec(memory_space=pl.ANY)
    return pl.pallas_call(
        body, name=name, out_shape=[jax.ShapeDtypeStruct((N_DEV,) + a.shape, a.dtype) for a in arrays],
        in_specs=[any_spec] * n, out_specs=[any_spec] * n,
        scratch_shapes=[pltpu.SemaphoreType.DMA((len(pieces) * 7,)), pltpu.SemaphoreType.DMA((len(pieces) * 7,)),
                        pltpu.SemaphoreType.DMA((n,))],
    )(*arrays)


_HBM = pl.BlockSpec(memory_space=pltpu.HBM)
_SEM = pl.BlockSpec(memory_space=pltpu.SEMAPHORE)
_EFFECT = pltpu.SideEffectType.DATAFLOW_SIDE_EFFECTING


def _split_copies(items, srcs, lands, send_sems, recv_sems):
    x, y, c = lax.axis_index("x"), lax.axis_index("y"), lax.axis_index("c")
    me = 4 * x + 2 * y + c
    copies = []
    for k in range(1, N_DEV):
        px = jnp.bitwise_xor(x, (k >> 2) & 1)
        py = jnp.bitwise_xor(y, (k >> 1) & 1)
        pc = jnp.bitwise_xor(c, k & 1)
        peer = 4 * px + 2 * py + pc
        for i in range(len(items)):
            src = srcs[i] if items[i][1] == "gather" else srcs[i].at[peer]
            copies.append(pltpu.make_async_remote_copy(
                src_ref=src, dst_ref=lands[i].at[me],
                send_sem=send_sems.at[i * (N_DEV - 1) + k - 1], recv_sem=recv_sems.at[i * (N_DEV - 1) + k - 1],
                device_id=(px, py, pc), device_id_type=pl.DeviceIdType.MESH))
    return me, copies


def _exchange_start(items, name, after):
    n = len(items)
    n_sem = n * (N_DEV - 1)
    srcs, lands = [], []
    for a, mode in items:
        blk = a.shape if mode == "gather" else a.shape[1:]
        srcs.append(pltpu.with_memory_space_constraint(a, pltpu.HBM))
        lands.append(pltpu.with_memory_space_constraint(lax.empty((N_DEV,) + tuple(blk), a.dtype), pltpu.HBM))

    def body(*refs):
        src_refs, land_refs = refs[:n], refs[n:2 * n]
        send_sems, recv_sems = refs[2 * n + 1], refs[2 * n + 2]
        local_sems = refs[4 * n + 3]
        me, copies = _split_copies(items, src_refs, land_refs, send_sems, recv_sems)
        for i in range(n):
            own = src_refs[i] if items[i][1] == "gather" else src_refs[i].at[me]
            cp = pltpu.make_async_copy(own, land_refs[i].at[me], local_sems.at[i])
            cp.start()
            cp.wait()
        for cp in copies:
            cp.start()

    out_shape = [pltpu.SemaphoreType.DMA((n_sem,)), pltpu.SemaphoreType.DMA((n_sem,))]
    out_shape += [pltpu.HBM(a.shape, a.dtype) for a in srcs] + [pltpu.HBM(a.shape, a.dtype) for a in lands]
    outs = pl.pallas_call(
        body, name=name, out_shape=out_shape,
        in_specs=[_HBM] * (2 * n) + [pl.BlockSpec(memory_space=pl.ANY)],
        out_specs=[_SEM, _SEM] + [_HBM] * (2 * n),
        input_output_aliases={i: 2 + i for i in range(2 * n)},
        scratch_shapes=[pltpu.SemaphoreType.DMA((n,))],
        compiler_params=pltpu.CompilerParams(has_side_effects=_EFFECT),
    )(*srcs, *lands, after)
    handle = (items, name, outs[0], outs[1], outs[2:2 + n], outs[2 + n:2 + 2 * n])
    return handle, outs[2]


class _Sender:
    PIECE_ROWS = 352

    def __init__(self, items, chunks=None):
        self.items, self.n = items, len(items)
        self.chunks = chunks
        if chunks is None:
            block_rows = [a.shape[0] if mode == "gather" else a.shape[1] for a, mode in items]
            self.chunks = [r // self.PIECE_ROWS if r % self.PIECE_ROWS == 0 else 1 for r in block_rows]
        self.srcs, self.lands = [], []
        for a, mode in items:
            blk = a.shape if mode == "gather" else a.shape[1:]
            self.srcs.append(pltpu.with_memory_space_constraint(a, pltpu.HBM))
            self.lands.append(pltpu.with_memory_space_constraint(lax.empty((N_DEV,) + tuple(blk), a.dtype), pltpu.HBM))

    def issue(self, src_refs, land_refs, send_sems, recv_sems, local_sems, step, n_steps):
        x, y, c = lax.axis_index("x"), lax.axis_index("y"), lax.axis_index("c")
        me = 4 * x + 2 * y + c
        copies = []
        for ch in range(max(self.chunks)):
            for k in range(1, N_DEV):
                px = jnp.bitwise_xor(x, (k >> 2) & 1)
                py = jnp.bitwise_xor(y, (k >> 1) & 1)
                pc = jnp.bitwise_xor(c, k & 1)
                peer = 4 * px + 2 * py + pc
                for i, (_, mode) in enumerate(self.items):
                    if ch >= self.chunks[i]:
                        continue
                    n_rows = land_refs[i].shape[1] // self.chunks[i]
                    rows = pl.ds(ch * n_rows, n_rows)
                    src = src_refs[i].at[rows] if mode == "gather" else src_refs[i].at[peer].at[rows]
                    copies.append(pltpu.make_async_remote_copy(
                        src_ref=src, dst_ref=land_refs[i].at[me].at[rows],
                        send_sem=send_sems.at[i * (N_DEV - 1) + k - 1], recv_sem=recv_sems.at[i * (N_DEV - 1) + k - 1],
                        device_id=(px, py, pc), device_id_type=pl.DeviceIdType.MESH))
        own = [pltpu.make_async_copy(src_refs[i] if mode == "gather" else src_refs[i].at[me], land_refs[i].at[me],
                                     local_sems.at[i]) for i, (_, mode) in enumerate(self.items)]

        @pl.when(step == 0)
        def _():
            for cp in own:
                cp.start()

        for s in range(n_steps):
            group = [cp for j, cp in enumerate(copies) if (j * n_steps) // len(copies) == s]
            if group:
                @pl.when(step == s)
                def _(group=group):
                    for cp in group:
                        cp.start()

        @pl.when(step == n_steps - 1)
        def _():
            for cp in own:
                cp.wait()


def _host_call(body, name, grid, in_specs, args, out_shape, out_specs, scratch_shapes, after=None, sender=None):
    in_specs, args, out_shape, out_specs = list(in_specs), list(args), list(out_shape), list(out_specs)
    scratch_shapes = list(scratch_shapes)
    semantics = ("arbitrary",) * len(grid)
    body = _ordered_behind(body, in_specs, args, after)
    if sender is None:
        res = pl.pallas_call(body, name=name, grid=grid, in_specs=in_specs, out_specs=out_specs, out_shape=out_shape,
                             scratch_shapes=scratch_shapes, compiler_params=_params(semantics))(*args)
        return res, None
    n, n_in, n_out, n_scr = sender.n, len(in_specs), len(out_shape), len(scratch_shapes)
    n_sem = n * (N_DEV - 1)
    n_steps = 1
    for g in grid:
        n_steps *= g
    compute = body

    def body(*refs):
        ins, s_in = refs[:n_in], refs[n_in:n_in + 2 * n]
        o0 = n_in + 2 * n
        outs, s_out = refs[o0:o0 + n_out], refs[o0 + n_out:o0 + n_out + 2 + 2 * n]
        scr = refs[o0 + n_out + 2 + 2 * n:]
        compute(*ins, *outs, *scr[:n_scr])
        step = pl.program_id(0)
        for d in range(1, len(grid)):
            step = step * grid[d] + pl.program_id(d)
        sender.issue(s_in[:n], s_in[n:], s_out[0], s_out[1], scr[n_scr], step, n_steps)

    res = pl.pallas_call(
        body, name=name, grid=grid,
        in_specs=in_specs + [_HBM] * (2 * n), out_specs=out_specs + [_SEM, _SEM] + [_HBM] * (2 * n),
        out_shape=out_shape + [pltpu.SemaphoreType.DMA((n_sem,)), pltpu.SemaphoreType.DMA((n_sem,))]
        + [pltpu.HBM(a.shape, a.dtype) for a in sender.srcs] + [pltpu.HBM(a.shape, a.dtype) for a in sender.lands],
        input_output_aliases={n_in + j: n_out + 2 + j for j in range(2 * n)},
        scratch_shapes=scratch_shapes + [pltpu.SemaphoreType.DMA((n,))],
        compiler_params=pltpu.CompilerParams(dimension_semantics=semantics, vmem_limit_bytes=VMEM_LIMIT,
                                             has_side_effects=_EFFECT),
    )(*args, *sender.srcs, *sender.lands)
    handle = (sender.items, name, res[n_out], res[n_out + 1], res[n_out + 2:n_out + 2 + n],
              res[n_out + 2 + n:n_out + 2 + 2 * n])
    return res[:n_out], handle


def _exchange_wait(handle, after):
    items, name, send_sems, recv_sems, srcs, lands = handle
    n = len(items)

    def body(*refs):
        src_refs, land_refs = refs[:n], refs[n:2 * n]
        send_ref, recv_ref = refs[2 * n], refs[2 * n + 1]
        _, copies = _split_copies(items, src_refs, land_refs, send_ref, recv_ref)
        for cp in copies:
            cp.wait_send()
            cp.wait_recv()

    outs = pl.pallas_call(
        body, name=name + "_wait",
        out_shape=[pltpu.HBM(a.shape, a.dtype) for a in srcs] + [pltpu.HBM(a.shape, a.dtype) for a in lands],
        in_specs=[_HBM] * (2 * n) + [_SEM, _SEM, pl.BlockSpec(memory_space=pl.ANY)], out_specs=[_HBM] * (2 * n),
        input_output_aliases={i: i for i in range(2 * n)},
        compiler_params=pltpu.CompilerParams(has_side_effects=_EFFECT),
    )(*srcs, *lands, send_sems, recv_sems, after)
    return outs[n:]


def _mod_fwd(cvec, w_mod_l, b_mod_l):
    rows, cols = cvec.shape[0], w_mod_l.shape[1]

    def body(c_ref, w_ref, b_ref, o_ref, s_ref):
        cv = c_ref[...]
        s = cv * _sigmoid(cv)
        s_ref[...] = s
        o_ref[...] = _dot(s, w_ref[...]) + b_ref[...]

    return pl.pallas_call(
        body, name="mod_fwd",
        out_shape=(jax.ShapeDtypeStruct((rows, cols), F32), jax.ShapeDtypeStruct((rows, D), F32)),
        in_specs=[_full((rows, D)), _full((D, cols)), _full((1, cols))],
        out_specs=(_full((rows, cols)), _full((rows, D))), grid=(1,),
        compiler_params=_params(("arbitrary",)),
    )(cvec, w_mod_l, b_mod_l)


def _mod_bwd(svec, dmod_l):
    rows, cols = dmod_l.shape

    def body(s_ref, d_ref, gw_ref):
        gw_ref[...] = _dot(s_ref[...], d_ref[...], "tn")

    return pl.pallas_call(
        body, name="mod_bwd", out_shape=jax.ShapeDtypeStruct((D, cols), F32),
        in_specs=[_full((rows, D)), _full((rows, cols))], out_specs=_full((D, cols)), grid=(1,),
        compiler_params=_params(("arbitrary",)),
    )(svec, dmod_l)


def _mod_bwd_ctx(c_row, d_l, w_mod_l):
    cols = d_l.shape[1]

    def body(c_ref, d_ref, w_ref, o_ref):
        cv = c_ref[...]
        sg = _sigmoid(cv)
        o_ref[...] = _dot(d_ref[...], w_ref[...], "nt") * (sg * (1.0 + cv * (1.0 - sg)))

    return pl.pallas_call(
        body, name="mod_bwd_ctx", out_shape=jax.ShapeDtypeStruct((8, D), F32),
        in_specs=[_full((1, D)), _full((8, cols)), _full((D, cols))], out_specs=_full((8, D)), grid=(1,),
        compiler_params=_params(("arbitrary",)),
    )(c_row, d_l, w_mod_l)


def _inproj(xt, modv, g, w_inT, n_cols, rows_per_example, name, after=None, sender=None):
    rows = xt.shape[0]
    tm = min(PROJ_TILE, rows_per_example)
    per_b = rows_per_example // tm
    shared_mod = modv.shape[0] == 1

    def body(x_ref, mod_ref, g_ref, w_ref, p_ref, h_ref):
        x = x_ref[...]
        r = lax.rsqrt(jnp.mean(x * x, axis=-1, keepdims=True) + EPS)
        h = (x * r * g_ref[...]) * (1.0 + mod_ref[0, 1:2, :]) + mod_ref[0, 0:1, :]
        hb = h.astype(MXU_DTYPE)
        h_ref[...] = hb
        for j in range(n_cols // KW):
            p_ref[:, j * KW:(j + 1) * KW] = _dot(hb, w_ref[j * KW:(j + 1) * KW, :], "nt").astype(p_ref.dtype)

    mod_idx = (lambda i: (0, 0, 0)) if shared_mod else (lambda i: (i // per_b, 0, 0))
    in_specs = [pl.BlockSpec((tm, D), lambda i: (i, 0)), pl.BlockSpec((1, N_MOD, D), mod_idx), _full((1, D)),
                pl.BlockSpec((n_cols, D), lambda i: (0, 0), pipeline_mode=pl.Buffered(1))]
    (p, h), handle = _host_call(
        body, name, (rows // tm,), in_specs, [xt, modv, g, w_inT],
        [jax.ShapeDtypeStruct((rows, n_cols), MXU_DTYPE), jax.ShapeDtypeStruct((rows, D), MXU_DTYPE)],
        [pl.BlockSpec((tm, n_cols), lambda i: (i, 0)), pl.BlockSpec((tm, D), lambda i: (i, 0))], [],
        after=after, sender=sender)
    return p, h, handle


def _tri(reverse, n):
    row = lax.broadcasted_iota(jnp.int32, (n, n), 0)
    col = lax.broadcasted_iota(jnp.int32, (n, n), 1)
    same = (row // CHUNK) == (col // CHUNK)
    return same & ((col >= row) if reverse else (col <= row))


def _per_chunk_rows(x, reverse):
    n = x.shape[0]
    rows = [x[j * CHUNK:j * CHUNK + 1] if reverse else x[(j + 1) * CHUNK - 1:(j + 1) * CHUNK] for j in range(n // CHUNK)]
    return jnp.concatenate([jnp.broadcast_to(r, (CHUNK, x.shape[1])) for r in rows], axis=0), rows


def _lower_bound(gam_ref, direction):
    return _sigmoid(gam_ref[direction:direction + 1, :] - gam_ref[2 + direction:3 + direction, :])


def _gate_prep(z, lb, tri, reverse, b=None):
    sg = _sigmoid(z)
    f = lb + (1.0 - lb) * sg
    g = jnp.log(f)
    b = _mask_dot(tri, g) if b is None else b
    bl, bl_rows = _per_chunk_rows(b, reverse)
    mid = 0.5 * bl
    return sg, g, 1.0 - f, b, jnp.exp(mid), [jnp.exp(0.5 * r) for r in bl_rows], jnp.exp(mid - b), mid


def _hgrn_fwd(p, gam, s0, rows_per_example, with_out, name, sender=None):
    rows = p.shape[0]
    nb_ex = rows // rows_per_example
    rb = min(TOKEN_TILE, rows_per_example)
    cpb = rb // CHUNK
    nb = rows_per_example // rb
    n_chunks = rows // CHUNK
    has_s0 = s0 is not None

    def body(*refs):
        it = iter(refs)
        gam_ref = next(it)
        zf_ref, vf_ref = next(it), next(it)
        qf_ref = next(it) if with_out else None
        zb_ref, vb_ref = next(it), next(it)
        qb_ref = next(it) if with_out else None
        s0_ref = next(it) if has_s0 else None
        if with_out:
            of_ref, ob_ref = next(it), next(it)
        stash_f, stash_b, bsum_f, bsum_b, fin_ref = next(it), next(it), next(it), next(it), next(it)
        st_ref = next(it)
        i = pl.program_id(1)

        @pl.when(i == 0)
        def _():
            if has_s0:
                st_ref[...] = s0_ref[:, 0]
            else:
                st_ref[...] = jnp.zeros_like(st_ref)

        for direction, (z_ref, v_ref, q_ref, stash, bsum_ref) in enumerate(
                ((zf_ref, vf_ref, qf_ref, stash_f, bsum_f), (zb_ref, vb_ref, qb_ref, stash_b, bsum_b))):
            reverse = direction == 1
            tri = _tri(reverse, rb)
            lb = _lower_bound(gam_ref, direction)
            z = z_ref[...].astype(F32)
            v = v_ref[...].astype(F32)
            _, _, k, b, em, em_rows, e2, mid = _gate_prep(z, lb, tri, reverse)
            bsum_ref[...] = b
            kd = (k * (e2 * em)).astype(MXU_DTYPE)
            vb = v.astype(MXU_DTYPE)
            if with_out:
                q = q_ref[...].astype(F32)
                qi = q * jnp.exp(b - mid)
                qe = (qi * em).astype(MXU_DTYPE)
                qi = qi.astype(MXU_DTYPE)
                ki = (k * e2).astype(MXU_DTYPE)
                intra = []
                for h in range(HEADS):
                    hs = slice(h * DK, (h + 1) * DK)
                    sc = jnp.where(tri, _dot(qi[:, hs], ki[:, hs], "nt"), 0.0)
                    intra.append(_dot(sc, vb[:, hs]))
            for j in (range(cpb - 1, -1, -1) if reverse else range(cpb)):
                rs = slice(j * CHUNK, (j + 1) * CHUNK)
                a = em_rows[j] * em_rows[j]
                for h in range(HEADS):
                    hs = slice(h * DK, (h + 1) * DK)
                    st = st_ref[direction, h]
                    stash[j, h] = st.astype(stash.dtype)
                    if with_out:
                        (ob_ref if reverse else of_ref)[rs, hs] = intra[h][rs] + _dot(qe[rs, hs], st, "nt")
                    st_ref[direction, h] = st * a[:, hs] + _dot(vb[rs, hs], kd[rs, hs], "tn")

        @pl.when(i == nb - 1)
        def _():
            fin_ref[:, 0] = st_ref[...]

    up = lambda b, i: b * nb + i
    down = lambda b, i: b * nb + nb - 1 - i
    col = lambda rowf, c: pl.BlockSpec((rb, KW), lambda b, i: (rowf(b, i), c))
    in_specs = [_full((4, KW)), col(up, 0), col(up, 2)] + ([col(up, 3)] if with_out else [])
    in_specs += [col(down, 1), col(down, 2)] + ([col(down, 3)] if with_out else [])
    args = [gam, p, p] + ([p] if with_out else []) + [p, p] + ([p] if with_out else [])
    if has_s0:
        in_specs.append(pl.BlockSpec((2, 1, HEADS, DK, DK), lambda b, i: (0, b, 0, 0, 0)))
        args.append(s0)
    out_shape, out_specs = [], []
    if with_out:
        out_shape += [jax.ShapeDtypeStruct((rows, KW), F32)] * 2
        out_specs += [pl.BlockSpec((rb, KW), lambda b, i: (up(b, i), 0)),
                      pl.BlockSpec((rb, KW), lambda b, i: (down(b, i), 0))]
    out_shape += [jax.ShapeDtypeStruct((n_chunks, HEADS, DK, DK), MXU_DTYPE)] * 2
    out_specs += [pl.BlockSpec((cpb, HEADS, DK, DK), lambda b, i: (up(b, i), 0, 0, 0)),
                  pl.BlockSpec((cpb, HEADS, DK, DK), lambda b, i: (down(b, i), 0, 0, 0))]
    out_shape += [jax.ShapeDtypeStruct((rows, KW), F32)] * 2
    out_specs += [pl.BlockSpec((rb, KW), lambda b, i: (up(b, i), 0)),
                  pl.BlockSpec((rb, KW), lambda b, i: (down(b, i), 0))]
    out_shape.append(jax.ShapeDtypeStruct((2, nb_ex, HEADS, DK, DK), F32))
    out_specs.append(pl.BlockSpec((2, 1, HEADS, DK, DK), lambda b, i: (0, b, 0, 0, 0)))
    res, handle = _host_call(body, name, (nb_ex, nb), in_specs, args, out_shape, out_specs,
                             [pltpu.VMEM((2, HEADS, DK, DK), F32)], sender=sender)
    return (*res, handle)


def _hgrn_bwd(p, gam, do, stash_f, stash_b, bsum_f, bsum_b, ds_end, rows_per_example, with_out, name, after=None,
              sender=None):
    rows = p.shape[0]
    nb_ex = rows // rows_per_example
    rb = min(TOKEN_TILE, rows_per_example)
    cpb = rb // CHUNK
    nb = rows_per_example // rb
    has_end = ds_end is not None

    def body(*refs):
        it = iter(refs)
        gam_ref = next(it)
        ins = []
        for _ in range(2):
            z_ref, v_ref = next(it), next(it)
            q_ref = next(it) if with_out else None
            do_ref = next(it) if with_out else None
            ins.append((z_ref, v_ref, q_ref, do_ref, next(it), next(it)))
        end_ref = next(it) if has_end else None
        outs = [next(it), next(it)]
        dlb_ref, ds0_ref = next(it), next(it)
        dst_ref = next(it)
        b_id, i = pl.program_id(0), pl.program_id(1)

        @pl.when(i == 0)
        def _():
            if has_end:
                dst_ref[...] = end_ref[:, 0]
            else:
                dst_ref[...] = jnp.zeros_like(dst_ref)

        @pl.when((i == 0) & (b_id == 0))
        def _():
            dlb_ref[...] = jnp.zeros_like(dlb_ref)

        for direction in range(2):
            z_ref, v_ref, q_ref, do_ref, stash, b_ref = ins[direction]
            dgrp_ref = outs[direction]
            reverse = direction == 1
            tri = _tri(reverse, rb)
            tri_t = _tri(not reverse, rb)
            lb = _lower_bound(gam_ref, direction)
            heads = [slice(h * DK, (h + 1) * DK) for h in range(HEADS)]
            chunks = [slice(j * CHUNK, (j + 1) * CHUNK) for j in range(cpb)]
            grid_cat = lambda parts: jnp.concatenate([jnp.concatenate(row, axis=1) for row in parts], axis=0)
            cat = lambda parts: jnp.concatenate(parts, axis=1)
            z = z_ref[...].astype(F32)
            sg, g, k, b, em, em_rows, e2, mid = _gate_prep(z, lb, tri, reverse, b=b_ref[...])
            e3 = e2 * em
            kd = k * e3
            kd_b = kd.astype(MXU_DTYPE)
            vb = v_ref[...].astype(MXU_DTYPE)
            if with_out:
                q = q_ref[...].astype(F32)
                dout = do_ref[...].astype(MXU_DTYPE)
                e1 = jnp.exp(b - mid)
                e4 = e1 * em
                qi, ki, qe = q * e1, k * e2, q * e4
                qi_b, ki_b, qe_b = qi.astype(MXU_DTYPE), ki.astype(MXU_DTYPE), qe.astype(MXU_DTYPE)
                dqi_p, dki_p, dv_p = [], [], []
                for hs in heads:
                    sc = jnp.where(tri, _dot(qi_b[:, hs], ki_b[:, hs], "nt"), 0.0)
                    dsc = jnp.where(tri, _dot(dout[:, hs], vb[:, hs], "nt"), 0.0)
                    dqi_p.append(_dot(dsc, ki_b[:, hs]))
                    dki_p.append(_dot(dsc, qi_b[:, hs], "tn"))
                    dv_p.append(_dot(sc, dout[:, hs], "tn"))
                dqi, dki, dv = cat(dqi_p), cat(dki_p), cat(dv_p)
                dqe = grid_cat([[_dot(dout[rs, hs], stash[j, h]) for h, hs in enumerate(heads)]
                                for j, rs in enumerate(chunks)])
                grow = [[_dot(dout[rs, hs], qe_b[rs, hs], "tn") for hs in heads] for rs in chunks]
            dkd_p = [[None] * HEADS for _ in range(cpb)]
            dvs_p = [[None] * HEADS for _ in range(cpb)]
            da_p = [[None] * HEADS for _ in range(cpb)]
            for j in (range(cpb) if reverse else range(cpb - 1, -1, -1)):
                rs = chunks[j]
                a = em_rows[j] * em_rows[j]
                for h, hs in enumerate(heads):
                    dst = dst_ref[direction, h]
                    dkd_p[j][h] = _dot(vb[rs, hs], dst)
                    dvs_p[j][h] = _dot(kd_b[rs, hs], dst, "nt")
                    da_p[j][h] = jnp.broadcast_to(
                        jnp.sum(dst * stash[j, h].astype(F32), axis=0, keepdims=True), (CHUNK, DK))
                    new_dst = dst * a[:, hs]
                    dst_ref[direction, h] = new_dst + grow[j][h] if with_out else new_dst
            dkd, dvs, da = grid_cat(dkd_p), grid_cat(dvs_p), grid_cat(da_p)
            t_kd = dkd * kd
            dk = dkd * e3
            db = -t_kd
            tot = t_kd
            if with_out:
                dgrp_ref[:, KW:2 * KW] = (dvs + dv).astype(dgrp_ref.dtype)
                dgrp_ref[:, 2 * KW:] = (dqi * e1 + dqe * e4).astype(dgrp_ref.dtype)
                dk = dk + dki * e2
                t_qi, t_ki, t_qe = dqi * qi, dki * ki, dqe * qe
                db = db + t_qi - t_ki + t_qe
                tot = tot + 0.5 * (t_ki - t_qi)
            else:
                dgrp_ref[:, KW:2 * KW] = dvs.astype(dgrp_ref.dtype)
            dbl = jnp.concatenate([jnp.broadcast_to(jnp.sum(tot[rs], axis=0, keepdims=True), (CHUNK, KW))
                                   for rs in chunks], axis=0) + da * (em * em)
            dg = _mask_dot(tri_t, db) + dbl
            df = dg * jnp.exp(-g) - dk
            dgrp_ref[:, 0:KW] = (df * (1.0 - lb) * sg * (1.0 - sg)).astype(dgrp_ref.dtype)
            dlb_ref[direction:direction + 1, :] += jnp.sum(df * (1.0 - sg), axis=0, keepdims=True)

        @pl.when(i == nb - 1)
        def _():
            ds0_ref[:, 0] = dst_ref[...]

    rows_of = (lambda b, i: b * nb + nb - 1 - i, lambda b, i: b * nb + i)
    in_specs, args = [_full((4, KW))], [gam]
    for direction in range(2):
        rf = rows_of[direction]
        col = lambda c, rf=rf: pl.BlockSpec((rb, KW), lambda b, i: (rf(b, i), c))
        in_specs += [col(direction), col(2)]
        args += [p, p]
        if with_out:
            in_specs += [col(3), col(0)]
            args += [p, do]
        in_specs += [pl.BlockSpec((cpb, HEADS, DK, DK), lambda b, i, rf=rf: (rf(b, i), 0, 0, 0)), col(0)]
        args += [(stash_f, stash_b)[direction], (bsum_f, bsum_b)[direction]]
    if has_end:
        in_specs.append(pl.BlockSpec((2, 1, HEADS, DK, DK), lambda b, i: (0, b, 0, 0, 0)))
        args.append(ds_end)
    out_shape, out_specs = [], []
    for direction in range(2):
        rf = rows_of[direction]
        width = (3 if with_out else 2) * KW
        out_shape.append(jax.ShapeDtypeStruct((rows, width), MXU_DTYPE))
        out_specs.append(pl.BlockSpec((rb, width), lambda b, i, rf=rf: (rf(b, i), 0)))
    out_shape += [jax.ShapeDtypeStruct((2, KW), F32), jax.ShapeDtypeStruct((2, nb_ex, HEADS, DK, DK), F32)]
    out_specs += [_full((2, KW)), pl.BlockSpec((2, 1, HEADS, DK, DK), lambda b, i: (0, b, 0, 0, 0))]
    res, handle = _host_call(body, name, (nb_ex, nb), in_specs, args, out_shape, out_specs,
                             [pltpu.VMEM((2, HEADS, DK, DK), F32)], after=after, sender=sender)
    return (*res, handle)


def _tail_forward(osum, og, u, v, ga, gb, gna, ln_g, ln_b, ws_ref, bs_ref, wpaT_ref, wpbT_ref, proj=None):
    tm = osum.shape[0]
    gna4 = jnp.concatenate([gna] * HEADS, axis=1)
    r_parts = []
    for h in range(HEADS):
        oh = osum[:, h * DK:(h + 1) * DK]
        r_parts.append(jnp.broadcast_to(lax.rsqrt(jnp.mean(oh * oh, axis=-1, keepdims=True) + EPS), (tm, DK)))
    r = jnp.concatenate(r_parts, axis=1)
    on = osum * r
    sg_og = _sigmoid(og)
    silu_og = og * sg_og
    oan = on * gna4
    oa = oan * silu_og
    ug, tu = _gelu(u)
    vg, tv = _gelu(v)
    mu = jnp.mean(vg, axis=-1, keepdims=True)
    vc = vg - mu
    rstd = lax.rsqrt(jnp.mean(vc * vc, axis=-1, keepdims=True) + EPS)
    vhat = vc * rstd
    vln = vhat * ln_g + ln_b
    blocks = []
    for n in range(tm // SGU_BLOCK):
        rs = slice(n * SGU_BLOCK, (n + 1) * SGU_BLOCK)
        blocks.append(jnp.concatenate(
            [_dot(ws_ref[g], vln[rs, g * DK:(g + 1) * DK]) + bs_ref[g] for g in range(GROUPS)], axis=1))
    mixed = jnp.concatenate(blocks, axis=0) if len(blocks) > 1 else blocks[0]
    obm = ug * mixed
    if proj is None:
        pa = _dot(oa, wpaT_ref[...], "nt")
        pb = _dot(obm, wpbT_ref[...], "nt")
    else:
        pa, pb = proj
    sga, sgb = _sigmoid(ga), _sigmoid(gb)
    merged = sga * pa + sgb * pb
    return dict(r=r, on=on, sg_og=sg_og, silu_og=silu_og, oan=oan, oa=oa, ug=ug, tu=tu, tv=tv, rstd=rstd, vhat=vhat,
                vln=vln, mixed=mixed, obm=obm, pa=pa, pb=pb, sga=sga, sgb=sgb, merged=merged, gna4=gna4)


def _tail_in_specs(tm):
    tile = lambda c: pl.BlockSpec((tm, KW), lambda i: (i, c))
    return [tile(c) for c in range(4, 11)]


def _tail_weight_specs():
    return [_full((1, DK)), _full((1, KW)), _full((1, KW)), _full((GROUPS, SGU_BLOCK, SGU_BLOCK)),
            _full((GROUPS, SGU_BLOCK, 1)), _full((D, KW), single=True), _full((D, KW), single=True),
            _full((D, D), single=True)]


def _read_tail_inputs(of_ref, ob_ref, pcols):
    osum = of_ref[...] + ob_ref[...]
    og, u, v = (pcols[j][...].astype(F32) for j in range(3))
    ga = jnp.concatenate([pcols[3][...], pcols[4][...]], axis=1).astype(F32)
    gb = jnp.concatenate([pcols[5][...], pcols[6][...]], axis=1).astype(F32)
    return osum, og, u, v, ga, gb


def _tail_fwd(p, o_up, o_down, xt, modv, gna, ln_g, ln_b, w_s, b_s, w_paT, w_pbT, w_o, rows_per_example):
    rows = xt.shape[0]
    tm = min(TAIL_TILE, rows_per_example)
    per_b = rows_per_example // tm

    def body(of_ref, ob_ref, *rest):
        pcols = rest[:7]
        (x_ref, mod_ref, gna_ref, lng_ref, lnb_ref, ws_ref, bs_ref, wpaT_ref, wpbT_ref, wo_ref,
         x1_ref, mix_ref, merged_ref, oa_ref, obm_ref, pa_ref, pb_ref) = rest[7:]
        t = _tail_forward(*_read_tail_inputs(of_ref, ob_ref, pcols), gna_ref[...], lng_ref[...], lnb_ref[...],
                          ws_ref, bs_ref, wpaT_ref, wpbT_ref)
        mix = _dot(t["merged"], wo_ref[...])
        x1_ref[...] = x_ref[...] + mod_ref[0, 2:3, :] * mix
        mix_ref[...] = mix.astype(mix_ref.dtype)
        merged_ref[...] = t["merged"].astype(merged_ref.dtype)
        oa_ref[...] = t["oa"].astype(oa_ref.dtype)
        obm_ref[...] = t["obm"].astype(obm_ref.dtype)
        pa_ref[...] = t["pa"].astype(pa_ref.dtype)
        pb_ref[...] = t["pb"].astype(pb_ref.dtype)

    row = lambda w: pl.BlockSpec((tm, w), lambda i: (i, 0))
    in_specs = [row(KW), row(KW)] + _tail_in_specs(tm) + [row(D), pl.BlockSpec((1, N_MOD, D), lambda i: (i // per_b, 0, 0))]
    in_specs += _tail_weight_specs()
    return pl.pallas_call(
        body, name="tail_fwd", grid=(rows // tm,),
        out_shape=(jax.ShapeDtypeStruct((rows, D), F32), jax.ShapeDtypeStruct((rows, D), MXU_DTYPE),
                   jax.ShapeDtypeStruct((rows, D), MXU_DTYPE), jax.ShapeDtypeStruct((rows, KW), MXU_DTYPE),
                   jax.ShapeDtypeStruct((rows, KW), MXU_DTYPE), jax.ShapeDtypeStruct((rows, D), MXU_DTYPE),
                   jax.ShapeDtypeStruct((rows, D), MXU_DTYPE)),
        in_specs=in_specs, out_specs=(row(D), row(D), row(D), row(KW), row(KW), row(D), row(D)),
        compiler_params=_params(("arbitrary",)),
    )(o_up, o_down, *([p] * 7), xt, modv, gna, ln_g, ln_b, w_s, b_s, w_paT, w_pbT, w_o)


def _tail_bwd(p, o_up, o_down, dx1, mix, pa, pb, modv, gna, ln_g, ln_b, w_s, b_s, w_paT, w_pbT, w_o, rows_per_example,
              after=None, sender=None):
    rows = dx1.shape[0]
    nb_ex = rows // rows_per_example
    tm = min(TAIL_TILE, rows_per_example)
    per_b = rows_per_example // tm

    def body(of_ref, ob_ref, *rest):
        pcols = rest[:7]
        (dx1_ref, mix_ref, pa_ref, pb_ref, mod_ref, gna_ref, lng_ref, lnb_ref, ws_ref, bs_ref, wpaT_ref, wpbT_ref, wo_ref,
         dpt_ref, do_ref, dmix_ref, dpa_ref, dpb_ref, dmod_ref, small_ref, dws_ref, dbs_ref) = rest[7:]
        i = pl.program_id(0)

        @pl.when(i == 0)
        def _():
            small_ref[...] = jnp.zeros_like(small_ref)
            dws_ref[...] = jnp.zeros_like(dws_ref)
            dbs_ref[...] = jnp.zeros_like(dbs_ref)

        @pl.when(i % per_b == 0)
        def _():
            dmod_ref[...] = jnp.zeros_like(dmod_ref)

        osum, og, u, v, ga, gb = _read_tail_inputs(of_ref, ob_ref, pcols)
        ln_g = lng_ref[...]
        t = _tail_forward(osum, og, u, v, ga, gb, gna_ref[...], ln_g, lnb_ref[...], ws_ref, bs_ref, wpaT_ref, wpbT_ref,
                          proj=(pa_ref[...].astype(F32), pb_ref[...].astype(F32)))
        dx1v = dx1_ref[...]
        dmod_ref[0, 2:3, :] += jnp.sum(dx1v * mix_ref[...].astype(F32), axis=0, keepdims=True)
        dmix = dx1v * mod_ref[0, 2:3, :]
        dmix_ref[...] = dmix.astype(dmix_ref.dtype)
        dmerged = _dot(dmix, wo_ref[...], "nt")
        sga, sgb = t["sga"], t["sgb"]
        dpa = dmerged * sga
        dpb = dmerged * sgb
        dpa_ref[...] = dpa.astype(dpa_ref.dtype)
        dpb_ref[...] = dpb.astype(dpb_ref.dtype)
        dga = dmerged * t["pa"] * sga * (1.0 - sga)
        dgb = dmerged * t["pb"] * sgb * (1.0 - sgb)
        doa = _dot(dpa, wpaT_ref[...])
        dobm = _dot(dpb, wpbT_ref[...])
        dug = dobm * t["mixed"]
        dmixed = dobm * t["ug"]
        du = dug * _gelu_grad(u, t["tu"])
        dvln_blocks = []
        for n in range(tm // SGU_BLOCK):
            rs = slice(n * SGU_BLOCK, (n + 1) * SGU_BLOCK)
            parts = []
            for g in range(GROUPS):
                gs = slice(g * DK, (g + 1) * DK)
                dm = dmixed[rs, gs]
                parts.append(_dot(ws_ref[g], dm, "tn"))
                dws_ref[g] += _dot(dm, t["vln"][rs, gs], "nt")
                dbs_ref[g] += jnp.sum(dm, axis=1, keepdims=True)
            dvln_blocks.append(jnp.concatenate(parts, axis=1))
        dvln = jnp.concatenate(dvln_blocks, axis=0) if len(dvln_blocks) > 1 else dvln_blocks[0]
        vhat = t["vhat"]
        small_ref[1:2, 0:KW] += jnp.sum(dvln * vhat, axis=0, keepdims=True)
        small_ref[2:3, 0:KW] += jnp.sum(dvln, axis=0, keepdims=True)
        dvhat = dvln * ln_g
        dvg = t["rstd"] * (dvhat - jnp.mean(dvhat, axis=-1, keepdims=True)
                           - vhat * jnp.mean(dvhat * vhat, axis=-1, keepdims=True))
        dv = dvg * _gelu_grad(v, t["tv"])
        sg_og = t["sg_og"]
        doan = doa * t["silu_og"]
        dog = doa * t["oan"] * (sg_og * (1.0 + og * (1.0 - sg_og)))
        prod = doan * t["on"]
        dgna = jnp.zeros((1, DK), F32)
        for h in range(HEADS):
            dgna = dgna + jnp.sum(prod[:, h * DK:(h + 1) * DK], axis=0, keepdims=True)
        small_ref[0:1, 0:DK] += dgna
        don = doan * t["gna4"]
        dot_parts = []
        for h in range(HEADS):
            hs = slice(h * DK, (h + 1) * DK)
            m = jnp.mean(don[:, hs] * t["on"][:, hs], axis=-1, keepdims=True)
            dot_parts.append(t["r"][:, hs] * (don[:, hs] - t["on"][:, hs] * m))
        do_ref[...] = jnp.concatenate(dot_parts, axis=1).astype(do_ref.dtype)
        for j, val in enumerate((dog, du, dv)):
            dpt_ref[:, j * KW:(j + 1) * KW] = val.astype(dpt_ref.dtype)
        dpt_ref[:, 3 * KW:3 * KW + D] = dga.astype(dpt_ref.dtype)
        dpt_ref[:, 3 * KW + D:] = dgb.astype(dpt_ref.dtype)

    row = lambda w: pl.BlockSpec((tm, w), lambda i: (i, 0))
    in_specs = [row(KW), row(KW)] + _tail_in_specs(tm) + [row(D)] * 4 + [pl.BlockSpec((1, N_MOD, D), lambda i: (i // per_b, 0, 0))]
    in_specs += _tail_weight_specs()
    args = [o_up, o_down, *([p] * 7), dx1, mix, pa, pb, modv, gna, ln_g, ln_b, w_s, b_s, w_paT, w_pbT, w_o]
    cd = MXU_DTYPE
    res, handle = _host_call(
        body, "tail_bwd", (rows // tm,), in_specs, args,
        [jax.ShapeDtypeStruct((rows, TAIL_COLS), cd), jax.ShapeDtypeStruct((rows, KW), cd),
         jax.ShapeDtypeStruct((rows, D), cd), jax.ShapeDtypeStruct((rows, D), cd),
         jax.ShapeDtypeStruct((rows, D), cd), jax.ShapeDtypeStruct((nb_ex, 8, D), F32),
         jax.ShapeDtypeStruct((8, D), F32), jax.ShapeDtypeStruct((GROUPS, SGU_BLOCK, SGU_BLOCK), F32),
         jax.ShapeDtypeStruct((GROUPS, SGU_BLOCK, 1), F32)],
        [row(TAIL_COLS), row(KW), row(D), row(D), row(D),
         pl.BlockSpec((1, 8, D), lambda i: (i // per_b, 0, 0)), _full((8, D)),
         _full((GROUPS, SGU_BLOCK, SGU_BLOCK)), _full((GROUPS, SGU_BLOCK, 1))], [],
        after=after, sender=sender)
    return (*res, handle)


def _ffn(x1, target, modv, g_ffn, g_final, w_upT, w_down, rows_per_example):
    rows = x1.shape[0]
    nb_ex = rows // rows_per_example
    tm = min(TOKEN_TILE, rows_per_example)
    per_b = rows_per_example // tm
    n_ff = D_FF // FF_CHUNK

    def body(x1_ref, tgt_ref, mod_ref, gffn_ref, gfin_ref, wup_ref, wdn_ref,
             dx1_ref, h2_ref, dffn_ref, act_ref, dup_ref, dmod_ref, small_ref, up_scr):
        i = pl.program_id(0)

        @pl.when(i == 0)
        def _():
            small_ref[...] = jnp.zeros_like(small_ref)

        @pl.when(i % per_b == 0)
        def _():
            dmod_ref[...] = jnp.zeros_like(dmod_ref)

        x1v = x1_ref[...]
        g2 = gffn_ref[...]
        m3, m4, m5 = mod_ref[0, 3:4, :], mod_ref[0, 4:5, :], mod_ref[0, 5:6, :]
        r2 = lax.rsqrt(jnp.mean(x1v * x1v, axis=-1, keepdims=True) + EPS)
        xn2 = x1v * r2
        h2 = (xn2 * g2) * (1.0 + m4) + m3
        h2b = h2.astype(MXU_DTYPE)
        h2_ref[...] = h2b
        def up_pair(j):
            lo = j * FF_CHUNK
            return (_dot(h2b, wup_ref[lo:lo + FF_CHUNK, :], "nt"),
                    _dot(h2b, wup_ref[D_FF + lo:D_FF + lo + FF_CHUNK, :], "nt"))

        group_end = {min(e, n_ff): s for s, e in ((0, 4), (4, 8), (8, 12))}
        cur, ffn = up_pair(0), None
        for j in range(n_ff):
            nxt = up_pair(j + 1) if j + 1 < n_ff else None
            cs = slice(j * FF_CHUNK, (j + 1) * FF_CHUNK)
            a, bgate = cur
            up_scr[:, cs] = a
            up_scr[:, D_FF + j * FF_CHUNK:D_FF + (j + 1) * FF_CHUNK] = bgate
            act_ref[:, cs] = (a * _sigmoid(a) * bgate).astype(MXU_DTYPE)
            cur = nxt
            if j + 1 in group_end:
                gs = slice(group_end[j + 1] * FF_CHUNK, (j + 1) * FF_CHUNK)
                part = _dot(act_ref[:, gs], wdn_ref[gs, :])
                ffn = part if ffn is None else ffn + part
        x2 = x1v + m5 * ffn
        r3 = lax.rsqrt(jnp.mean(x2 * x2, axis=-1, keepdims=True) + EPS)
        xn3 = x2 * r3
        gf = gfin_ref[...]
        err = xn3 * gf - tgt_ref[...]
        loss = 0.5 * jnp.sum(jnp.mean(err * err, axis=-1, keepdims=True), axis=0, keepdims=True)
        small_ref[2:3, :] += jnp.broadcast_to(loss, (1, D))
        dy = err * (1.0 / D)
        small_ref[1:2, :] += jnp.sum(dy * xn3, axis=0, keepdims=True)
        dxn3 = dy * gf
        dx2 = r3 * (dxn3 - xn3 * jnp.mean(dxn3 * xn3, axis=-1, keepdims=True))
        dmod_ref[0, 5:6, :] += jnp.sum(dx2 * ffn, axis=0, keepdims=True)
        dffn = (dx2 * m5).astype(MXU_DTYPE)
        dffn_ref[...] = dffn
        dact_of = lambda j: _dot(dffn, wdn_ref[j * FF_CHUNK:(j + 1) * FF_CHUNK, :], "nt")
        cur, dh2 = dact_of(0), None
        for j in range(n_ff):
            nxt = dact_of(j + 1) if j + 1 < n_ff else None
            cs = slice(j * FF_CHUNK, (j + 1) * FF_CHUNK)
            a, bgate = up_scr[:, cs], up_scr[:, D_FF + j * FF_CHUNK:D_FF + (j + 1) * FF_CHUNK]
            s = _sigmoid(a)
            dup_ref[:, cs] = (cur * bgate * (s * (1.0 + a * (1.0 - s)))).astype(MXU_DTYPE)
            dup_ref[:, D_FF + j * FF_CHUNK:D_FF + (j + 1) * FF_CHUNK] = (cur * a * s).astype(MXU_DTYPE)
            cur = nxt
            if j + 1 in group_end:
                lo, hi = group_end[j + 1] * FF_CHUNK, (j + 1) * FF_CHUNK
                part = (_dot(dup_ref[:, lo:hi], wup_ref[lo:hi, :])
                        + _dot(dup_ref[:, D_FF + lo:D_FF + hi], wup_ref[D_FF + lo:D_FF + hi, :]))
                dh2 = part if dh2 is None else dh2 + part
        dmod_ref[0, 3:4, :] += jnp.sum(dh2, axis=0, keepdims=True)
        dmod_ref[0, 4:5, :] += jnp.sum(dh2 * xn2 * g2, axis=0, keepdims=True)
        small_ref[0:1, :] += jnp.sum(dh2 * (1.0 + m4) * xn2, axis=0, keepdims=True)
        dxn2 = dh2 * g2 * (1.0 + m4)
        dx1_ref[...] = dx2 + r2 * (dxn2 - xn2 * jnp.mean(dxn2 * xn2, axis=-1, keepdims=True))

    row = lambda w: pl.BlockSpec((tm, w), lambda i: (i, 0))
    cd = MXU_DTYPE
    return pl.pallas_call(
        body, name="ffn_fwd_bwd", grid=(rows // tm,),
        out_shape=(jax.ShapeDtypeStruct((rows, D), F32), jax.ShapeDtypeStruct((rows, D), cd),
                   jax.ShapeDtypeStruct((rows, D), cd), jax.ShapeDtypeStruct((rows, D_FF), cd),
                   jax.ShapeDtypeStruct((rows, 2 * D_FF), cd), jax.ShapeDtypeStruct((nb_ex, 8, D), F32),
                   jax.ShapeDtypeStruct((8, D), F32)),
        in_specs=[row(D), row(D), pl.BlockSpec((1, N_MOD, D), lambda i: (i // per_b, 0, 0)), _full((1, D)), _full((1, D)),
                  _full((2 * D_FF, D), single=True), _full((D_FF, D), single=True)],
        out_specs=(row(D), row(D), row(D), row(D_FF), row(2 * D_FF),
                   pl.BlockSpec((1, 8, D), lambda i: (i // per_b, 0, 0)), _full((8, D))),
        scratch_shapes=[pltpu.VMEM((tm, 2 * D_FF), F32)],
        compiler_params=_params(("arbitrary",)),
    )(x1, target, modv, g_ffn, g_final, w_upT, w_down)


def _scan_columns(up, down, n_groups):
    cols = [up[:, 0:KW].astype(F32), down[:, 0:KW].astype(F32)]
    for j in range(1, n_groups):
        cols.append(up[:, j * KW:(j + 1) * KW].astype(F32) + down[:, j * KW:(j + 1) * KW].astype(F32))
    return cols


def _inproj_bwd(d_up, d_down, dpt, xt, dx1, modv, g, w_inT, rows_per_example, name, sender=None):
    rows = xt.shape[0]
    latent = dx1 is not None
    n_cols = IN_COLS if latent else CTX_COLS
    n_groups = d_up.shape[1] // KW
    tm = min(PROJ_TILE, rows_per_example)
    per_b = rows_per_example // tm
    n_mod_blocks = rows // rows_per_example if latent else 1

    def body(*refs):
        it = iter(refs)
        up_ref, down_ref = next(it), next(it)
        dpt_ref = next(it) if latent else None
        x_ref = next(it)
        dx1_ref = next(it) if latent else None
        mod_ref, g_ref, w_ref = next(it), next(it), next(it)
        gx_ref = next(it) if latent else None
        dp_out = None if latent else next(it)
        dmod_ref, small_ref = next(it), next(it)
        dp_ref = next(it) if latent else dp_out
        i = pl.program_id(0)

        @pl.when(i == 0)
        def _():
            small_ref[...] = jnp.zeros_like(small_ref)

        @pl.when((i % per_b == 0) if latent else (i == 0))
        def _():
            dmod_ref[...] = jnp.zeros_like(dmod_ref)

        for j, val in enumerate(_scan_columns(up_ref[...], down_ref[...], n_groups)):
            dp_ref[:, j * KW:(j + 1) * KW] = val.astype(MXU_DTYPE)
        if latent:
            dh = _dot(dp_ref[...], w_ref[0:4 * KW, :]) + _dot(dpt_ref[...], w_ref[4 * KW:, :])
        else:
            dh = _dot(dp_ref[...], w_ref[...])
        x = x_ref[...]
        gv = g_ref[...]
        m1 = mod_ref[0, 1:2, :]
        r = lax.rsqrt(jnp.mean(x * x, axis=-1, keepdims=True) + EPS)
        xn = x * r
        dmod_ref[0, 0:1, :] += jnp.sum(dh, axis=0, keepdims=True)
        dmod_ref[0, 1:2, :] += jnp.sum(dh * xn * gv, axis=0, keepdims=True)
        small_ref[0:1, :] += jnp.sum(dh * (1.0 + m1) * xn, axis=0, keepdims=True)
        if latent:
            dxn = dh * gv * (1.0 + m1)
            gx_ref[...] = dx1_ref[...] + r * (dxn - xn * jnp.mean(dxn * xn, axis=-1, keepdims=True))

    row = lambda w: pl.BlockSpec((tm, w), lambda i: (i, 0))
    mod_idx = (lambda i: (i // per_b, 0, 0)) if latent else (lambda i: (0, 0, 0))
    in_specs = [row(n_groups * KW)] * 2 + ([row(TAIL_COLS)] if latent else []) + [row(D)] + ([row(D)] if latent else [])
    in_specs += [pl.BlockSpec((1, N_MOD, D), mod_idx), _full((1, D)),
                 pl.BlockSpec((n_cols, D), lambda i: (0, 0), pipeline_mode=pl.Buffered(1))]
    args = [d_up, d_down] + ([dpt] if latent else []) + [xt] + ([dx1] if latent else []) + [modv, g, w_inT]
    first = jax.ShapeDtypeStruct((rows, D), F32) if latent else jax.ShapeDtypeStruct((rows, n_cols), MXU_DTYPE)
    out_shape = [first, jax.ShapeDtypeStruct((n_mod_blocks, 8, D), F32), jax.ShapeDtypeStruct((8, D), F32)]
    out_specs = [row(D) if latent else row(n_cols), pl.BlockSpec((1, 8, D), mod_idx), _full((8, D))]
    scratch = [pltpu.VMEM((tm, 4 * KW), MXU_DTYPE)] if latent else []
    res, handle = _host_call(body, name, (rows // tm,), in_specs, args, out_shape, out_specs, scratch, sender=sender)
    return (*res, handle)


def _grad_matmul(a, b, name, init=None, tn=512, sender=None):
    rows, n = a.shape
    k = b.shape[1]
    tn = min(tn, n)
    has_init = init is not None
    init_blocks = init.shape[0] // tn if has_init else 0

    def body(*refs):
        if has_init:
            a_ref, b_ref, init_ref, o_ref = refs
        else:
            a_ref, b_ref, o_ref = refs
        g = _dot(a_ref[...], b_ref[...], "tn")
        if has_init:
            g = g + jnp.where(pl.program_id(0) < init_blocks, init_ref[...].astype(F32), 0.0)
        o_ref[...] = g.astype(o_ref.dtype)

    in_specs = [pl.BlockSpec((rows, tn), lambda i: (0, i)), _full((rows, k), single=True)]
    args = [a, b]
    if has_init:
        in_specs.append(pl.BlockSpec((tn, k), lambda i: (jnp.minimum(i, init_blocks - 1), 0)))
        args.append(init)
    (out,), handle = _host_call(
        body, name, (n // tn,), in_specs, args, [jax.ShapeDtypeStruct((n, k), PAYLOAD_DTYPE)],
        [pl.BlockSpec((tn, k), lambda i: (i, 0))], [], sender=sender)
    return out, handle


def _grad_in(d_up, d_down, dpt, h, init, sender=None):
    rows = h.shape[0]
    tn = 256
    per_group = KW // tn
    n_scan = 4 * per_group
    init_blocks = init.shape[0] // tn

    def body(up_ref, down_ref, dpt_ref, h_ref, init_ref, o_ref):
        i = pl.program_id(0)
        both = (up_ref[...].astype(F32) + down_ref[...].astype(F32)).astype(MXU_DTYPE)
        a = jnp.where(i < per_group, up_ref[...],
                      jnp.where(i < 2 * per_group, down_ref[...], jnp.where(i < n_scan, both, dpt_ref[...])))
        g = _dot(a, h_ref[...], "tn") + jnp.where(i < init_blocks, init_ref[...].astype(F32), 0.0)
        o_ref[...] = g.astype(o_ref.dtype)

    last = 3 * per_group - 1
    col = lambda f: pl.BlockSpec((rows, tn), lambda i: (0, f(i)))
    in_specs = [col(lambda i: jnp.clip(jnp.where(i < per_group, i, i - per_group), 0, last)),
                col(lambda i: jnp.clip(i - per_group, 0, last)),
                col(lambda i: jnp.clip(i - n_scan, 0, TAIL_COLS // tn - 1)),
                _full((rows, D), single=True),
                pl.BlockSpec((tn, D), lambda i: (jnp.minimum(i, init_blocks - 1), 0))]
    (out,), handle = _host_call(
        body, "gw_in", (IN_COLS // tn,), in_specs, [d_up, d_down, dpt, h, init],
        [jax.ShapeDtypeStruct((IN_COLS, D), PAYLOAD_DTYPE)], [pl.BlockSpec((tn, D), lambda i: (i, 0))], [],
        sender=sender)
    return out, handle


def _row_tile(rows, limit=256):
    if rows <= limit:
        return rows
    for t in range(limit, 7, -8):
        if rows % t == 0:
            return t
    return rows


def _sum8(stack, name):
    _, rows, cols = stack.shape
    tr = _row_tile(rows)

    def body(s_ref, o_ref):
        acc = s_ref[0].astype(F32)
        for j in range(1, N_DEV):
            acc = acc + s_ref[j].astype(F32)
        o_ref[...] = acc

    return pl.pallas_call(
        body, name=name, grid=(rows // tr,), out_shape=jax.ShapeDtypeStruct((rows, cols), F32),
        in_specs=[pl.BlockSpec((N_DEV, tr, cols), lambda i: (0, i, 0))],
        out_specs=pl.BlockSpec((tr, cols), lambda i: (i, 0)),
        compiler_params=_params(("arbitrary",)),
    )(stack)


def _adamw_update(w, gv, m, v):
    nm = ADAM_B1 * m + (1.0 - ADAM_B1) * gv
    nv = ADAM_B2 * v + (1.0 - ADAM_B2) * (gv * gv)
    m_hat = nm / (1.0 - ADAM_B1 ** ADAM_STEP)
    v_hat = nv / (1.0 - ADAM_B2 ** ADAM_STEP)
    return -ADAM_LR * (m_hat / (jnp.sqrt(v_hat) + ADAM_EPS) + ADAM_WD * w), nm, nv


SMALL_PARAMS = (("g_mix", 0, D), ("g_ffn", 1, D), ("g_final", 2, D), ("g_norm_a", 3, DK), ("ln_v_g", 4, KW),
                ("ln_v_b", 5, KW), ("b_s", 6, GROUPS * SGU_BLOCK),
                ("c_ctx", 15, D))


def _small_finish(early, late, dws, gam, nb_ex, params):
    names = [n for n, _, _ in SMALL_PARAMS] + ["b_mod", "w_s"]

    def body(*refs):
        s_ref, l_ref, dws_ref, gam_ref = refs[:4]
        p_refs = refs[4:4 + 3 * len(names)]
        tot_ref, dgam_ref = refs[4 + 3 * len(names):6 + 3 * len(names)]
        o_refs = refs[6 + 3 * len(names):]
        acc = s_ref[0] + l_ref[0]
        gws = dws_ref[0]
        for j in range(1, N_DEV):
            acc = acc + (s_ref[j] + l_ref[j])
            gws = gws + dws_ref[j]
        tot_ref[...] = acc
        bm = acc[8:8 + N_MOD, :]
        for e in range(nb_ex):
            bm = bm + acc[16 + e * N_MOD:16 + (e + 1) * N_MOD, :]
        lb = jnp.concatenate([_lower_bound(gam_ref, 0), _lower_bound(gam_ref, 1)], axis=1)
        dgam = acc[7:8, :] * lb * (1.0 - lb)
        dgam_ref[...] = jnp.concatenate([dgam, -dgam], axis=0)
        bm = jnp.concatenate([bm[j:j + 1] for j in range(N_MOD)], axis=1)
        grads = [acc[row:row + 1, 0:width] for _, row, width in SMALL_PARAMS] + [bm, gws]
        for k, g in enumerate(grads):
            w_ref, m_ref, v_ref = p_refs[3 * k:3 * k + 3]
            o_refs[4 * k][...] = g
            o_refs[4 * k + 1][...], o_refs[4 * k + 2][...], o_refs[4 * k + 3][...] = _adamw_update(
                w_ref[...], g, m_ref[...], v_ref[...])

    p_args, p_specs, o_shapes, o_specs = [], [], [], []
    for n in names:
        for a in params[n]:
            p_args.append(a)
            p_specs.append(_full(a.shape))
        o_shapes += [jax.ShapeDtypeStruct(params[n][0].shape, F32)] * 4
        o_specs += [_full(params[n][0].shape)] * 4
    res = pl.pallas_call(
        body, name="small_finish", grid=(1,),
        out_shape=[jax.ShapeDtypeStruct((SMALL_ROWS, D), F32), jax.ShapeDtypeStruct((2, D), F32)] + o_shapes,
        in_specs=[_full(early.shape), _full(late.shape), _full(dws.shape), _full((4, KW))] + p_specs,
        out_specs=[_full((SMALL_ROWS, D)), _full((2, D))] + o_specs,
        compiler_params=_params(("arbitrary",)),
    )(early, late, dws, gam, *p_args)
    return res[0], res[1], {n: res[2 + 4 * k:6 + 4 * k] for k, n in enumerate(names)}


def _adamw_sum8(stack, w, m, v, name):
    _, rows, cols = stack.shape
    tr = _row_tile(rows)

    def body(s_ref, w_ref, m_ref, v_ref, g_ref, d_ref, nm_ref, nv_ref):
        gv = s_ref[0].astype(F32)
        for j in range(1, N_DEV):
            gv = gv + s_ref[j].astype(F32)
        g_ref[...] = gv
        d_ref[...], nm_ref[...], nv_ref[...] = _adamw_update(w_ref[...], gv, m_ref[...], v_ref[...])

    blk = pl.BlockSpec((tr, cols), lambda i: (i, 0))
    sd = jax.ShapeDtypeStruct((rows, cols), F32)
    return pl.pallas_call(
        body, name=name, grid=(rows // tr,), out_shape=(sd, sd, sd, sd),
        in_specs=[pl.BlockSpec((N_DEV, tr, cols), lambda i: (0, i, 0)), blk, blk, blk], out_specs=(blk, blk, blk, blk),
        compiler_params=_params(("arbitrary",)),
    )(stack, w, m, v)


def _adamw_sum8_t(stack, w, m, v, name):
    _, cols, rows = stack.shape

    def body(s_ref, w_ref, m_ref, v_ref, g_ref, d_ref, nm_ref, nv_ref):
        gv = s_ref[0].astype(F32)
        for j in range(1, N_DEV):
            gv = gv + s_ref[j].astype(F32)
        gv = gv.T
        g_ref[...] = gv
        d_ref[...], nm_ref[...], nv_ref[...] = _adamw_update(w_ref[...], gv, m_ref[...], v_ref[...])

    blk = _full((rows, cols))
    sd = jax.ShapeDtypeStruct((rows, cols), F32)
    return pl.pallas_call(
        body, name=name, grid=(1,), out_shape=(sd, sd, sd, sd),
        in_specs=[_full(stack.shape), blk, blk, blk], out_specs=(blk, blk, blk, blk),
        compiler_params=_params(("arbitrary",)),
    )(stack, w, m, v)


def _adamw(w, g, m, v, name):
    shape = w.shape
    cols = shape[-1]
    rows = 1
    for s in shape[:-1]:
        rows *= s
    tr = _row_tile(rows)

    def body(w_ref, g_ref, m_ref, v_ref, d_ref, nm_ref, nv_ref):
        gv = g_ref[...]
        nm = ADAM_B1 * m_ref[...] + (1.0 - ADAM_B1) * gv
        nv = ADAM_B2 * v_ref[...] + (1.0 - ADAM_B2) * (gv * gv)
        m_hat = nm / (1.0 - ADAM_B1 ** ADAM_STEP)
        v_hat = nv / (1.0 - ADAM_B2 ** ADAM_STEP)
        d_ref[...] = -ADAM_LR * (m_hat / (jnp.sqrt(v_hat) + ADAM_EPS) + ADAM_WD * w_ref[...])
        nm_ref[...] = nm
        nv_ref[...] = nv

    blk = pl.BlockSpec((tr, cols), lambda i: (i, 0))
    sd = jax.ShapeDtypeStruct((rows, cols), F32)
    d, nm, nv = pl.pallas_call(
        body, name=name, grid=(rows // tr,), out_shape=(sd, sd, sd), in_specs=[blk] * 4, out_specs=(blk, blk, blk),
        compiler_params=_params(("arbitrary",)),
    )(w.reshape(rows, cols), g.reshape(rows, cols), m.reshape(rows, cols), v.reshape(rows, cols))
    return d.reshape(shape), nm.reshape(shape), nv.reshape(shape)


def _owner_blocks(a):
    return a.reshape(N_DEV, a.shape[0] // N_DEV, a.shape[1])


class _LocalWeights:
    def __init__(self, w_upT, w_down, w_o, w_paT, w_pbT):
        self.weights = (w_upT, w_down, w_o, w_paT, w_pbT)
        self.items = {}

    def sender(self, stage, items=None):
        self.items[stage] = items
        return None

    def sent(self, stage, handle):
        pass

    def mixer_weights(self, after):
        return self.weights[1:]

    def ffn_weights(self, after):
        return self.weights[0]

    def c_ctx_part(self, after):
        return jnp.zeros((1, D), F32)


def _local_step(x, ctx, target, modv, mcv, gam, g_mix, g_ffn, gna, ln_g, ln_b, w_s, b_s, g_final, w_inT, comm):
    nb_ex, seq, _ = x.shape
    ctx_len = ctx.shape[1]
    xt = x.reshape(nb_ex * seq, D)
    ct = ctx.reshape(nb_ex * ctx_len, D)
    tgt = target.reshape(nb_ex * seq, D)
    bs3 = b_s.reshape(GROUPS, SGU_BLOCK, 1)

    pc, hc, _ = _inproj(ct, mcv, g_mix, w_inT, CTX_COLS, ctx_len, "inproj_ctx")
    p, h, handle = _inproj(xt, modv, g_mix, w_inT, IN_COLS, seq, "inproj_lat", sender=comm.sender("inproj"))
    comm.sent("inproj", handle)
    cst_f, cst_b, cb_f, cb_b, s_ctx, _ = _hgrn_fwd(pc, gam, None, ctx_len, False, "hgrn_fwd_ctx")
    o_up, o_down, st_f, st_b, b_f, b_b, _, handle = _hgrn_fwd(p, gam, s_ctx, seq, True, "hgrn_fwd_lat",
                                                              sender=comm.sender("scan"))
    comm.sent("scan", handle)
    w_down, w_o, w_paT, w_pbT = comm.mixer_weights(o_up)
    x1, mix, merged, oa, obm, pa, pb = _tail_fwd(p, o_up, o_down, xt, modv, gna, ln_g, ln_b, w_s, bs3, w_paT, w_pbT,
                                                 w_o, seq)
    w_upT = comm.ffn_weights(x1)
    dx1, h2, dffn, act, dup, dmod_ffn, small_ffn = _ffn(x1, tgt, modv, g_ffn, g_final, w_upT, w_down, seq)
    gw_upT, _ = _grad_matmul(dup, h2, "gw_up")
    gw_down, _ = _grad_matmul(act, dffn, "gw_down", tn=256)
    scatter = lambda *grads: [(_owner_blocks(g), "scatter") for g in grads]
    dpt, do, dmix, dpa, dpb, dmod_tail, small_tail, dws, dbs, handle = _tail_bwd(
        p, o_up, o_down, dx1, mix, pa, pb, modv, gna, ln_g, ln_b, w_s, bs3, w_paT, w_pbT, w_o, seq,
        sender=comm.sender("tail_bwd", scatter(gw_upT)))
    comm.sent("tail_bwd", handle)
    gw_o, _ = _grad_matmul(merged, dmix, "gw_o")
    gw_paT, _ = _grad_matmul(dpa, oa, "gw_pa")
    gw_pbT, _ = _grad_matmul(dpb, obm, "gw_pb")
    def at_row(row, a):
        return jnp.pad(a, ((row, SMALL_ROWS - row - a.shape[0]), (0, D - a.shape[1])))

    small_early = (at_row(1, small_ffn[0:2])
                   + at_row(3, small_tail[0:3])
                   + at_row(6, dbs.reshape(1, GROUPS * SGU_BLOCK))
                   + at_row(14, small_ffn[2:3]))
    dws_rows = dws.reshape(GROUPS * SGU_BLOCK, SGU_BLOCK)
    d_up, d_down, dlb, ds0, handle = _hgrn_bwd(
        p, gam, do, st_f, st_b, b_f, b_b, None, seq, True, "hgrn_bwd_lat",
        sender=comm.sender("scan_bwd", scatter(gw_down, gw_o, gw_paT, gw_pbT)
                           + [(small_early, "gather"), (dws_rows, "gather")]))
    comm.sent("scan_bwd", handle)
    c_up, c_down, dlb_c, _, _ = _hgrn_bwd(pc, gam, None, cst_f, cst_b, cb_f, cb_b, ds0, ctx_len, False, "hgrn_bwd_ctx")
    dpc, dmc, small_c, _ = _inproj_bwd(c_up, c_down, None, ct, None, mcv, g_mix, w_inT, ctx_len, "inproj_bwd_ctx")
    gw_in_c, handle = _grad_matmul(dpc, hc, "gw_in_ctx", sender=comm.sender("ctx_mod", [(dmc[0], "gather")]))
    comm.sent("ctx_mod", handle)
    gw_inT, _ = _grad_in(d_up, d_down, dpt, h, gw_in_c)
    grad_x, dmod_in, small_in, handle = _inproj_bwd(d_up, d_down, dpt, xt, dx1, modv, g_mix, w_inT, seq,
                                                   "inproj_bwd_lat", sender=comm.sender("inproj_bwd", scatter(gw_inT)))
    comm.sent("inproj_bwd", handle)
    dmod = dmod_in + dmod_tail + dmod_ffn
    small_late = (at_row(0, small_in[0:1] + small_c[0:1])
                  + at_row(7, (dlb + dlb_c).reshape(1, 2 * KW))
                  + at_row(8, dmc[0, 0:N_MOD])
                  + at_row(15, comm.c_ctx_part(gw_inT))
                  + at_row(16, dmod[:, 0:N_MOD].reshape(nb_ex * N_MOD, D)))
    comm.sender("last", [(small_late, "gather")])
    return grad_x.reshape(x.shape)


def kernel(x, c, ctx, c_ctx, w_mod, b_mod, g_mix, g_ffn, w_in, lb_gamma, g_norm_a, ln_v_g, ln_v_b, w_s, b_s, w_pa, w_pb, w_o, w_up, w_down, g_final, loss_target, m_c_ctx, m_w_mod, m_b_mod, m_g_mix, m_g_ffn, m_w_in, m_lb_gamma, m_g_norm_a, m_ln_v_g, m_ln_v_b, m_w_s, m_b_s, m_w_pa, m_w_pb, m_w_o, m_w_up, m_w_down, m_g_final, v_c_ctx, v_w_mod, v_b_mod, v_g_mix, v_g_ffn, v_w_in, v_lb_gamma, v_g_norm_a, v_ln_v_g, v_ln_v_b, v_w_s, v_b_s, v_w_pa, v_w_pb, v_w_o, v_w_up, v_w_down, v_g_final):
    nb_ex = x.shape[0]
    me = 4 * lax.axis_index("x") + 2 * lax.axis_index("y") + lax.axis_index("c")
    cd = MXU_DTYPE
    mod_cols = w_mod.shape[2]
    lb_cols = lb_gamma.shape[2]

    w_inT_l = w_in[0].T.astype(cd)
    w_upT_l = w_up[0].T.astype(cd)
    w_paT_l = w_pa[0].T.astype(cd)
    w_pbT_l = w_pb[0].T.astype(cd)
    cl = jnp.concatenate([c, jnp.pad(lb_gamma.reshape(1, 4 * lb_cols), ((0, 0), (0, D - 4 * lb_cols))),
                          jnp.zeros((8 - nb_ex - 1, D), F32)], axis=0)
    g_in, g_cl = _gather_two_level([w_inT_l, cl], "gather_w_in")
    w_inT = g_in.reshape(IN_COLS, D)
    c_all = g_cl[:, 0:nb_ex].reshape(N_DEV * nb_ex, D)
    gam = jnp.transpose(g_cl[:, nb_ex, 0:4 * lb_cols].reshape(N_DEV, 4, lb_cols), (1, 0, 2)).reshape(4, KW)

    n_c = N_DEV * nb_ex
    cvec = jnp.concatenate([c_all, c_ctx.reshape(1, D), jnp.zeros((7, D), F32)], axis=0)
    b_mod_l = lax.dynamic_slice(b_mod, (0, me * mod_cols), (1, mod_cols))
    mod_l, svec = _mod_fwd(cvec, w_mod[0], b_mod_l)
    (g_mod,) = _gather_two_level([mod_l], "gather_mod")
    mod_all = jnp.transpose(g_mod, (1, 0, 2)).reshape(n_c + 8, N_MOD * D)
    modv = lax.dynamic_slice(mod_all, (me * nb_ex, 0), (nb_ex, N_MOD * D)).reshape(nb_ex, N_MOD, D)
    mcv = mod_all[n_c].reshape(1, N_MOD, D)

    handles, leftover = {}, {}

    class Comm:
        def sender(self, stage, items=None):
            if stage == "inproj":
                return _Sender([(w_down[0].astype(cd), "gather"), (w_o[0].astype(cd), "gather"), (w_paT_l, "gather"),
                                (w_pbT_l, "gather")])
            if stage == "scan":
                return _Sender([(w_upT_l, "gather")])
            if stage == "last":
                leftover["items"] = items
                return None
            return _Sender(items)

        def sent(self, stage, handle):
            handles[stage] = handle

        def mixer_weights(self, after):
            g_down, g_o, g_pa, g_pb = _exchange_wait(handles["inproj"], after)
            return g_down.reshape(D_FF, D), g_o.reshape(D, D), g_pa.reshape(D, KW), g_pb.reshape(D, KW)

        def ffn_weights(self, after):
            (g_up,) = _exchange_wait(handles["scan"], after)
            return g_up.reshape(2 * D_FF, D)

        def c_ctx_part(self, after):
            (r_dmc,) = _exchange_wait(handles["ctx_mod"], after)
            dmc_tot = _sum8(r_dmc, "sum_ctx_mod")
            dmc_l = lax.dynamic_slice(dmc_tot[0:N_MOD].reshape(1, N_MOD * D), (0, me * mod_cols), (1, mod_cols))
            return _mod_bwd_ctx(c_ctx.reshape(1, D), jnp.pad(dmc_l, ((0, 7), (0, 0))), w_mod[0])[0:1]

    grad_x = _local_step(
        x, ctx, loss_target, modv, mcv, gam, g_mix, g_ffn, g_norm_a, ln_v_g, ln_v_b, w_s[0], b_s[0],
        g_final.reshape(1, D), w_inT, Comm())
    last, last_started = _exchange_start(leftover["items"], "gather_small_late", after=leftover["items"][0][0])

    (r_up,) = _exchange_wait(handles["tail_bwd"], last_started)
    r_down, r_o, r_pa, r_pb, r_small, r_dws = _exchange_wait(handles["scan_bwd"], r_up)
    raw_up = _adamw_sum8(r_up, w_up[0].T, m_w_up[0].T, v_w_up[0].T, "adamw_w_up")
    raw_down = _adamw_sum8(r_down, w_down[0], m_w_down[0], v_w_down[0], "adamw_w_down")
    raw_o = _adamw_sum8(r_o, w_o[0], m_w_o[0], v_w_o[0], "adamw_w_o")
    updated = raw_up[1][0:8, 0:128] + raw_down[1][0:8, 0:128] + raw_o[1][0:8, 0:128]
    (r_in,) = _exchange_wait(handles["inproj_bwd"], updated)
    raw_in = _adamw_sum8(r_in, w_in[0].T, m_w_in[0].T, v_w_in[0].T, "adamw_w_in")
    (r_late,) = _exchange_wait(last, raw_in[1])
    done = {"w_in": [a.T[None] for a in raw_in], "w_up": [a.T[None] for a in raw_up],
            "w_down": [a[None] for a in raw_down], "w_o": [a[None] for a in raw_o]}
    grad_w_in, grad_w_up, grad_w_down, grad_w_o = (done[k][0] for k in ("w_in", "w_up", "w_down", "w_o"))
    done["w_pa"] = [a[None] for a in _adamw_sum8_t(r_pa, w_pa[0], m_w_pa[0], v_w_pa[0], "adamw_w_pa")]
    done["w_pb"] = [a[None] for a in _adamw_sum8_t(r_pb, w_pb[0], m_w_pb[0], v_w_pb[0], "adamw_w_pb")]
    grad_w_pa, grad_w_pb = done["w_pa"][0], done["w_pb"][0]
    as_2d = {"c_ctx": (1, D), "g_final": (1, D), "b_s": (1, GROUPS * SGU_BLOCK), "w_s": (GROUPS * SGU_BLOCK, SGU_BLOCK)}
    small_params = {"g_mix": (g_mix, m_g_mix, v_g_mix), "g_ffn": (g_ffn, m_g_ffn, v_g_ffn),
                    "g_final": (g_final, m_g_final, v_g_final), "g_norm_a": (g_norm_a, m_g_norm_a, v_g_norm_a),
                    "ln_v_g": (ln_v_g, m_ln_v_g, v_ln_v_g), "ln_v_b": (ln_v_b, m_ln_v_b, v_ln_v_b),
                    "b_s": (b_s, m_b_s, v_b_s), "c_ctx": (c_ctx, m_c_ctx, v_c_ctx), "b_mod": (b_mod, m_b_mod, v_b_mod), "w_s": (w_s, m_w_s, v_w_s)}
    tot, dgam, small_done = _small_finish(
        r_small, r_late, r_dws, gam, nb_ex,
        {n: tuple(a.reshape(as_2d.get(n, a.shape)) for a in wmv) for n, wmv in small_params.items()})
    for n, outs in small_done.items():
        done[n] = [a.reshape(small_params[n][0].shape) for a in outs]
    loss = tot[14, 0]
    grad_g_mix, grad_g_ffn, grad_g_final, grad_g_norm_a, grad_ln_v_g, grad_ln_v_b, grad_b_s, grad_b_mod, grad_w_s = (
        done[n][0] for n in ("g_mix", "g_ffn", "g_final", "g_norm_a", "ln_v_g", "ln_v_b", "b_s", "b_mod", "w_s"))
    grad_lb_gamma = lax.dynamic_slice(dgam.reshape(2, 2, KW), (0, 0, me * lb_cols), (2, 2, lb_cols))

    dmod_all = r_late[:, 16:16 + nb_ex * N_MOD].reshape(n_c, N_MOD * D)
    dmod_l = jnp.concatenate([lax.dynamic_slice(dmod_all, (0, me * mod_cols), (n_c, mod_cols)),
                              lax.dynamic_slice(tot[8:8 + N_MOD].reshape(1, N_MOD * D), (0, me * mod_cols), (1, mod_cols)),
                              jnp.zeros((7, mod_cols), F32)], axis=0)
    grad_w_mod = _mod_bwd(svec, dmod_l)[None]
    grad_c_ctx = done["c_ctx"][0]

    names = ["c_ctx", "w_mod", "b_mod", "g_mix", "g_ffn", "w_in", "lb_gamma", "g_norm_a", "ln_v_g", "ln_v_b", "w_s",
             "b_s", "w_pa", "w_pb", "w_o", "w_up", "w_down", "g_final"]
    weights = [c_ctx, w_mod, b_mod, g_mix, g_ffn, w_in, lb_gamma, g_norm_a, ln_v_g, ln_v_b, w_s, b_s, w_pa, w_pb, w_o,
               w_up, w_down, g_final]
    grads = [grad_c_ctx, grad_w_mod, grad_b_mod, grad_g_mix, grad_g_ffn, grad_w_in, grad_lb_gamma, grad_g_norm_a,
             grad_ln_v_g, grad_ln_v_b, grad_w_s, grad_b_s, grad_w_pa, grad_w_pb, grad_w_o, grad_w_up, grad_w_down,
             grad_g_final]
    ms = [m_c_ctx, m_w_mod, m_b_mod, m_g_mix, m_g_ffn, m_w_in, m_lb_gamma, m_g_norm_a, m_ln_v_g, m_ln_v_b, m_w_s, m_b_s,
          m_w_pa, m_w_pb, m_w_o, m_w_up, m_w_down, m_g_final]
    vs = [v_c_ctx, v_w_mod, v_b_mod, v_g_mix, v_g_ffn, v_w_in, v_lb_gamma, v_g_norm_a, v_ln_v_g, v_ln_v_b, v_w_s, v_b_s,
          v_w_pa, v_w_pb, v_w_o, v_w_up, v_w_down, v_g_final]
    deltas, new_ms, new_vs = [], [], []
    for nm, w, g, m, v in zip(names, weights, grads, ms, vs):
        d, nm_, nv_ = done[nm][1:] if nm in done else _adamw(w, g.reshape(w.shape), m, v, "adamw_" + nm)
        deltas.append(d)
        new_ms.append(nm_)
        new_vs.append(nv_)
    grads = [g.reshape(w.shape) for g, w in zip(grads, weights)]
    return (loss, grad_x, *grads, *deltas, *new_ms, *new_vs)
```
